```python
import jax, jax.numpy as jnp
from jax import lax
import numpy as np

D_MODEL = 1024
BATCH = 8
SEQ = 4096
DEPTH = 1

MIX_WIDTH = D_MODEL
CONV_WIDTH = MIX_WIDTH // 2
CONV_GROUPS = 8
CONV_KERNEL = 31
GM_WIDTH = MIX_WIDTH - CONV_WIDTH
GM_HEADS = 8
GM_HEAD_DIM = GM_WIDTH // GM_HEADS
CHUNK = 128
IN_COLS = 2 * CONV_WIDTH + 2 * GM_WIDTH
MEM_LEN = 256
XA_HEADS = 4
XA_HEAD_DIM = D_MODEL // XA_HEADS
FFN_HIDDEN = ((8 * D_MODEL // 3 + 255) // 256) * 256
RMS_EPS = 1e-6
LN_EPS = 1e-5

kernel_name = "hybrid_conv_gmlp_xattn_block"


def rmsnorm(x, g):
    xf = x.astype(jnp.float32)
    y = xf * lax.rsqrt(jnp.mean(xf * xf, axis=-1, keepdims=True) + RMS_EPS)
    return (y * g.astype(jnp.float32)).astype(x.dtype)


def layernorm(x, g, b):
    xf = x.astype(jnp.float32)
    mu = jnp.mean(xf, axis=-1, keepdims=True)
    var = jnp.mean(jnp.square(xf - mu), axis=-1, keepdims=True)
    y = (xf - mu) * lax.rsqrt(var + LN_EPS)
    return (y * g.astype(jnp.float32) + b.astype(jnp.float32)).astype(x.dtype)


def causal_depthwise_conv(a, w, b):
    k, c = w.shape
    a_pad = jnp.pad(a, ((0, 0), (k - 1, 0), (0, 0)))
    y = lax.conv_general_dilated(
        a_pad, w[:, None, :], window_strides=(1,), padding='VALID',
        dimension_numbers=('NWC', 'WIO', 'NWC'), feature_group_count=c)
    return y + b


def conformer_conv_group(za, zg, conv_w, conv_b, ln_g, ln_b):
    a = za * jax.nn.sigmoid(zg)
    a = causal_depthwise_conv(a, conv_w, conv_b)
    a = layernorm(a, ln_g, ln_b)
    return jax.nn.silu(a)


def gmlp_group(zu, zv, ln_g, ln_b, w_s, b_s):
    u = jax.nn.gelu(zu)
    v = layernorm(jax.nn.gelu(zv), ln_g, ln_b)
    bsz, s, _ = v.shape
    vh = v.reshape(bsz, s // CHUNK, CHUNK, GM_HEADS, GM_HEAD_DIM)
    mask = jnp.tril(jnp.ones((CHUNK, CHUNK), dtype=bool))
    ws = jnp.where(mask[None], w_s, jnp.zeros_like(w_s))
    mixed = jnp.einsum('hts,bnshd->bnthd', ws, vh)
    mixed = mixed + b_s.T[None, None, :, :, None]
    return u * mixed.reshape(bsz, s, GM_WIDTH)


def cross_attention(hn, mn, wq, wkv, wo):
    bsz, s, _ = hn.shape
    q = (hn @ wq).reshape(bsz, s, XA_HEADS, XA_HEAD_DIM)
    kv = mn @ wkv
    k, v = jnp.split(kv, 2, axis=-1)
    k = k.reshape(bsz, MEM_LEN, XA_HEADS, XA_HEAD_DIM)
    v = v.reshape(bsz, MEM_LEN, XA_HEADS, XA_HEAD_DIM)
    scale = XA_HEAD_DIM ** -0.5
    scores = jnp.einsum('bshd,bmhd->bhsm', q, k).astype(jnp.float32) * scale
    p = jax.nn.softmax(scores, axis=-1).astype(v.dtype)
    o = jnp.einsum('bhsm,bmhd->bshd', p, v).reshape(bsz, s, D_MODEL)
    return o @ wo


def swiglu(hn, w_gate_up, w_down):
    gu = hn @ w_gate_up
    g, u = jnp.split(gu, 2, axis=-1)
    return (jax.nn.silu(g) * u) @ w_down


def _fwd_setup_inputs(seed: int = 0) -> dict:
    key = jax.random.key(seed)
    ks = jax.random.split(key, 24)
    f32 = jnp.float32

    def nrm(k, shape, scale):
        return jax.random.normal(k, shape, f32) * scale

    def gain(k, n):
        return jnp.ones((n,), f32) + 0.05 * jax.random.normal(k, (n,), f32)

    return {
        "x": nrm(ks[0], (BATCH, SEQ, D_MODEL), 1.0),
        "mem": nrm(ks[1], (BATCH, MEM_LEN, D_MODEL), 1.0),
        "norm_mix_g": gain(ks[2], D_MODEL),
        "w_in": nrm(ks[3], (D_MODEL, IN_COLS), D_MODEL ** -0.5),
        "b_in": nrm(ks[4], (IN_COLS,), 0.02),
        "conv_w": nrm(ks[5], (CONV_KERNEL, CONV_WIDTH), CONV_KERNEL ** -0.5),
        "conv_b": nrm(ks[6], (CONV_WIDTH,), 0.02),
        "conv_ln_g": gain(ks[7], CONV_WIDTH),
        "conv_ln_b": nrm(ks[8], (CONV_WIDTH,), 0.02),
        "gm_ln_g": gain(ks[9], GM_WIDTH),
        "gm_ln_b": nrm(ks[10], (GM_WIDTH,), 0.02),
        "gm_w_s": nrm(ks[11], (GM_HEADS, CHUNK, CHUNK), CHUNK ** -0.5),
        "gm_b_s": jnp.ones((GM_HEADS, CHUNK), f32) + 0.1 * jax.random.normal(ks[12], (GM_HEADS, CHUNK), f32),
        "w_out": nrm(ks[13], (MIX_WIDTH, D_MODEL), MIX_WIDTH ** -0.5),
        "norm_xa_g": gain(ks[14], D_MODEL),
        "mem_norm_g": gain(ks[15], D_MODEL),
        "xa_wq": nrm(ks[16], (D_MODEL, D_MODEL), D_MODEL ** -0.5),
        "xa_wkv": nrm(ks[17], (D_MODEL, 2 * D_MODEL), D_MODEL ** -0.5),
        "xa_wo": nrm(ks[18], (D_MODEL, D_MODEL), D_MODEL ** -0.5),
        "norm_ffn_g": gain(ks[19], D_MODEL),
        "ffn_w_gate_up": nrm(ks[20], (D_MODEL, 2 * FFN_HIDDEN), D_MODEL ** -0.5),
        "ffn_w_down": nrm(ks[21], (FFN_HIDDEN, D_MODEL), FFN_HIDDEN ** -0.5),
        "final_norm_g": gain(ks[22], D_MODEL),
    }


def _fwd_reference(x, mem, norm_mix_g, w_in, b_in, conv_w, conv_b, conv_ln_g, conv_ln_b,
              gm_ln_g, gm_ln_b, gm_w_s, gm_b_s, w_out, norm_xa_g, mem_norm_g,
              xa_wq, xa_wkv, xa_wo, norm_ffn_g, ffn_w_gate_up, ffn_w_down,
              final_norm_g):
    h = x
    mn = rmsnorm(mem, mem_norm_g)
    for _ in range(DEPTH):
        hn = rmsnorm(h, norm_mix_g)
        z = hn @ w_in + b_in
        za, zg, zu, zv = jnp.split(
            z, [CONV_WIDTH, 2 * CONV_WIDTH, 2 * CONV_WIDTH + GM_WIDTH], axis=-1)
        conv_out = conformer_conv_group(za, zg, conv_w, conv_b, conv_ln_g, conv_ln_b)
        gm_out = gmlp_group(zu, zv, gm_ln_g, gm_ln_b, gm_w_s, gm_b_s)
        h = h + jnp.concatenate([conv_out, gm_out], axis=-1) @ w_out
        h = h + cross_attention(rmsnorm(h, norm_xa_g), mn, xa_wq, xa_wkv, xa_wo)
        h = h + swiglu(rmsnorm(h, norm_ffn_g), ffn_w_gate_up, ffn_w_down)
    return rmsnorm(h, final_norm_g)


import jax as _jax
import jax.numpy as _jnp

TWIN_FORMAT = 'train_step'
FWD_PARAMS = ['x', 'mem', 'norm_mix_g', 'w_in', 'b_in', 'conv_w', 'conv_b', 'conv_ln_g', 'conv_ln_b', 'gm_ln_g', 'gm_ln_b', 'gm_w_s', 'gm_b_s', 'w_out', 'norm_xa_g', 'mem_norm_g', 'xa_wq', 'xa_wkv', 'xa_wo', 'norm_ffn_g', 'ffn_w_gate_up', 'ffn_w_down', 'final_norm_g']
TWIN_WEIGHTS = ['norm_mix_g', 'w_in', 'b_in', 'conv_w', 'conv_b', 'conv_ln_g', 'conv_ln_b', 'gm_ln_g', 'gm_ln_b', 'gm_w_s', 'gm_b_s', 'w_out', 'norm_xa_g', 'mem_norm_g', 'xa_wq', 'xa_wkv', 'xa_wo', 'norm_ffn_g', 'ffn_w_gate_up', 'ffn_w_down', 'final_norm_g']
TWIN_DIFF_INPUT = 'x'
TWIN_INPUTS = ['x', 'mem', 'norm_mix_g', 'w_in', 'b_in', 'conv_w', 'conv_b', 'conv_ln_g', 'conv_ln_b', 'gm_ln_g', 'gm_ln_b', 'gm_w_s', 'gm_b_s', 'w_out', 'norm_xa_g', 'mem_norm_g', 'xa_wq', 'xa_wkv', 'xa_wo', 'norm_ffn_g', 'ffn_w_gate_up', 'ffn_w_down', 'final_norm_g', 'loss_target', 'm_norm_mix_g', 'm_w_in', 'm_b_in', 'm_conv_w', 'm_conv_b', 'm_conv_ln_g', 'm_conv_ln_b', 'm_gm_ln_g', 'm_gm_ln_b', 'm_gm_w_s', 'm_gm_b_s', 'm_w_out', 'm_norm_xa_g', 'm_mem_norm_g', 'm_xa_wq', 'm_xa_wkv', 'm_xa_wo', 'm_norm_ffn_g', 'm_ffn_w_gate_up', 'm_ffn_w_down', 'm_final_norm_g', 'v_norm_mix_g', 'v_w_in', 'v_b_in', 'v_conv_w', 'v_conv_b', 'v_conv_ln_g', 'v_conv_ln_b', 'v_gm_ln_g', 'v_gm_ln_b', 'v_gm_w_s', 'v_gm_b_s', 'v_w_out', 'v_norm_xa_g', 'v_mem_norm_g', 'v_xa_wq', 'v_xa_wkv', 'v_xa_wo', 'v_norm_ffn_g', 'v_ffn_w_gate_up', 'v_ffn_w_down', 'v_final_norm_g']
TWIN_OUTPUTS = ['loss', 'grad_x', 'grad_norm_mix_g', 'grad_w_in', 'grad_b_in', 'grad_conv_w', 'grad_conv_b', 'grad_conv_ln_g', 'grad_conv_ln_b', 'grad_gm_ln_g', 'grad_gm_ln_b', 'grad_gm_w_s', 'grad_gm_b_s', 'grad_w_out', 'grad_norm_xa_g', 'grad_mem_norm_g', 'grad_xa_wq', 'grad_xa_wkv', 'grad_xa_wo', 'grad_norm_ffn_g', 'grad_ffn_w_gate_up', 'grad_ffn_w_down', 'grad_final_norm_g', 'delta_norm_mix_g', 'delta_w_in', 'delta_b_in', 'delta_conv_w', 'delta_conv_b', 'delta_conv_ln_g', 'delta_conv_ln_b', 'delta_gm_ln_g', 'delta_gm_ln_b', 'delta_gm_w_s', 'delta_gm_b_s', 'delta_w_out', 'delta_norm_xa_g', 'delta_mem_norm_g', 'delta_xa_wq', 'delta_xa_wkv', 'delta_xa_wo', 'delta_norm_ffn_g', 'delta_ffn_w_gate_up', 'delta_ffn_w_down', 'delta_final_norm_g', 'new_m_norm_mix_g', 'new_m_w_in', 'new_m_b_in', 'new_m_conv_w', 'new_m_conv_b', 'new_m_conv_ln_g', 'new_m_conv_ln_b', 'new_m_gm_ln_g', 'new_m_gm_ln_b', 'new_m_gm_w_s', 'new_m_gm_b_s', 'new_m_w_out', 'new_m_norm_xa_g', 'new_m_mem_norm_g', 'new_m_xa_wq', 'new_m_xa_wkv', 'new_m_xa_wo', 'new_m_norm_ffn_g', 'new_m_ffn_w_gate_up', 'new_m_ffn_w_down', 'new_m_final_norm_g', 'new_v_norm_mix_g', 'new_v_w_in', 'new_v_b_in', 'new_v_conv_w', 'new_v_conv_b', 'new_v_conv_ln_g', 'new_v_conv_ln_b', 'new_v_gm_ln_g', 'new_v_gm_ln_b', 'new_v_gm_w_s', 'new_v_gm_b_s', 'new_v_w_out', 'new_v_norm_xa_g', 'new_v_mem_norm_g', 'new_v_xa_wq', 'new_v_xa_wkv', 'new_v_xa_wo', 'new_v_norm_ffn_g', 'new_v_ffn_w_gate_up', 'new_v_ffn_w_down', 'new_v_final_norm_g']
TWIN_LEAF_KINDS = {'loss': 'loss', 'grad_x': 'grad_x', 'grad_norm_mix_g': 'grad_w', 'grad_w_in': 'grad_w', 'grad_b_in': 'grad_w', 'grad_conv_w': 'grad_w', 'grad_conv_b': 'grad_w', 'grad_conv_ln_g': 'grad_w', 'grad_conv_ln_b': 'grad_w', 'grad_gm_ln_g': 'grad_w', 'grad_gm_ln_b': 'grad_w', 'grad_gm_w_s': 'grad_w', 'grad_gm_b_s': 'grad_w', 'grad_w_out': 'grad_w', 'grad_norm_xa_g': 'grad_w', 'grad_mem_norm_g': 'grad_w', 'grad_xa_wq': 'grad_w', 'grad_xa_wkv': 'grad_w', 'grad_xa_wo': 'grad_w', 'grad_norm_ffn_g': 'grad_w', 'grad_ffn_w_gate_up': 'grad_w', 'grad_ffn_w_down': 'grad_w', 'grad_final_norm_g': 'grad_w', 'delta_norm_mix_g': 'delta_w', 'delta_w_in': 'delta_w', 'delta_b_in': 'delta_w', 'delta_conv_w': 'delta_w', 'delta_conv_b': 'delta_w', 'delta_conv_ln_g': 'delta_w', 'delta_conv_ln_b': 'delta_w', 'delta_gm_ln_g': 'delta_w', 'delta_gm_ln_b': 'delta_w', 'delta_gm_w_s': 'delta_w', 'delta_gm_b_s': 'delta_w', 'delta_w_out': 'delta_w', 'delta_norm_xa_g': 'delta_w', 'delta_mem_norm_g': 'delta_w', 'delta_xa_wq': 'delta_w', 'delta_xa_wkv': 'delta_w', 'delta_xa_wo': 'delta_w', 'delta_norm_ffn_g': 'delta_w', 'delta_ffn_w_gate_up': 'delta_w', 'delta_ffn_w_down': 'delta_w', 'delta_final_norm_g': 'delta_w', 'new_m_norm_mix_g': 'new_m', 'new_m_w_in': 'new_m', 'new_m_b_in': 'new_m', 'new_m_conv_w': 'new_m', 'new_m_conv_b': 'new_m', 'new_m_conv_ln_g': 'new_m', 'new_m_conv_ln_b': 'new_m', 'new_m_gm_ln_g': 'new_m', 'new_m_gm_ln_b': 'new_m', 'new_m_gm_w_s': 'new_m', 'new_m_gm_b_s': 'new_m', 'new_m_w_out': 'new_m', 'new_m_norm_xa_g': 'new_m', 'new_m_mem_norm_g': 'new_m', 'new_m_xa_wq': 'new_m', 'new_m_xa_wkv': 'new_m', 'new_m_xa_wo': 'new_m', 'new_m_norm_ffn_g': 'new_m', 'new_m_ffn_w_gate_up': 'new_m', 'new_m_ffn_w_down': 'new_m', 'new_m_final_norm_g': 'new_m', 'new_v_norm_mix_g': 'new_v', 'new_v_w_in': 'new_v', 'new_v_b_in': 'new_v', 'new_v_conv_w': 'new_v', 'new_v_conv_b': 'new_v', 'new_v_conv_ln_g': 'new_v', 'new_v_conv_ln_b': 'new_v', 'new_v_gm_ln_g': 'new_v', 'new_v_gm_ln_b': 'new_v', 'new_v_gm_w_s': 'new_v', 'new_v_gm_b_s': 'new_v', 'new_v_w_out': 'new_v', 'new_v_norm_xa_g': 'new_v', 'new_v_mem_norm_g': 'new_v', 'new_v_xa_wq': 'new_v', 'new_v_xa_wkv': 'new_v', 'new_v_xa_wo': 'new_v', 'new_v_norm_ffn_g': 'new_v', 'new_v_ffn_w_gate_up': 'new_v', 'new_v_ffn_w_down': 'new_v', 'new_v_final_norm_g': 'new_v'}


def _forward(args):
    return _fwd_reference(*[args[k] for k in FWD_PARAMS])


def _output_shape():
    def fwd():
        inp = _fwd_setup_inputs(0)
        return _fwd_reference(*[inp[k] for k in FWD_PARAMS])
    out = _jax.eval_shape(fwd)
    return out.shape, out.dtype

N_MICROBATCH = 1
ADAM_LR = 0.001
ADAM_B1 = 0.9
ADAM_B2 = 0.999
ADAM_EPS = 1e-08
ADAM_WD = 0.01
ADAM_STEP = 10
PER_EXAMPLE_BATCH_AXIS = {'x': 0, 'mem': 0, 'loss_target': 0}
SHARED_INPUTS = []
_WEIGHT_DTYPES = {'norm_mix_g': _jnp.float32, 'w_in': _jnp.float32, 'b_in': _jnp.float32, 'conv_w': _jnp.float32, 'conv_b': _jnp.float32, 'conv_ln_g': _jnp.float32, 'conv_ln_b': _jnp.float32, 'gm_ln_g': _jnp.float32, 'gm_ln_b': _jnp.float32, 'gm_w_s': _jnp.float32, 'gm_b_s': _jnp.float32, 'w_out': _jnp.float32, 'norm_xa_g': _jnp.float32, 'mem_norm_g': _jnp.float32, 'xa_wq': _jnp.float32, 'xa_wkv': _jnp.float32, 'xa_wo': _jnp.float32, 'norm_ffn_g': _jnp.float32, 'ffn_w_gate_up': _jnp.float32, 'ffn_w_down': _jnp.float32, 'final_norm_g': _jnp.float32}
MOMENT_SCALE = {'norm_mix_g': 1.362728e-01, 'w_in': 9.527770e-02, 'b_in': 1.574710e-01, 'conv_w': 9.987099e-02, 'conv_b': 3.523564e-01, 'conv_ln_g': 1.662047e-01, 'conv_ln_b': 2.060715e-01, 'gm_ln_g': 7.475249e-02, 'gm_ln_b': 6.941585e-02, 'gm_w_s': 5.152605e-02, 'gm_b_s': 7.604696e-02, 'w_out': 1.406925e-01, 'norm_xa_g': 1.687193e-02, 'mem_norm_g': 2.623547e-02, 'xa_wq': 1.693419e-02, 'xa_wkv': 1.773161e-02, 'xa_wo': 1.853058e-02, 'norm_ffn_g': 1.149655e-01, 'ffn_w_gate_up': 4.858282e-02, 'ffn_w_down': 8.042471e-02, 'final_norm_g': 3.209769e+01}


def _to_microbatches(a, axis):
    t = _jnp.moveaxis(a, axis, 0)
    t = t.reshape((N_MICROBATCH, t.shape[0] // N_MICROBATCH) + t.shape[1:])
    return _jnp.moveaxis(t, 1, axis + 1)


def setup_inputs(seed: int = 0) -> dict:
    inp = _fwd_setup_inputs(seed)
    key = _jax.random.fold_in(_jax.random.key(seed), 7919)
    shape, _ = _output_shape()
    out = dict(inp)
    out["loss_target"] = _jax.random.normal(_jax.random.fold_in(key, 0), shape, _jnp.float32)
    for i, name in enumerate(TWIN_WEIGHTS):
        w = inp[name].astype(_jnp.float32)
        if MOMENT_SCALE is None:
            s = _jnp.sqrt(_jnp.mean(_jnp.square(w)) + 1e-30)
        else:
            s = MOMENT_SCALE[name]
        km, kv = _jax.random.split(_jax.random.fold_in(key, i + 1))
        out[name] = w
        out["m_" + name] = s * _jax.random.normal(km, w.shape, _jnp.float32)
        out["v_" + name] = (s * s) * _jax.random.uniform(kv, w.shape, _jnp.float32, 0.5, 1.5)
    if N_MICROBATCH > 1:
        for name, axis in PER_EXAMPLE_BATCH_AXIS.items():
            out[name] = _to_microbatches(out[name], axis)
    return {'x': out['x'], 'mem': out['mem'], 'norm_mix_g': out['norm_mix_g'], 'w_in': out['w_in'], 'b_in': out['b_in'], 'conv_w': out['conv_w'], 'conv_b': out['conv_b'], 'conv_ln_g': out['conv_ln_g'], 'conv_ln_b': out['conv_ln_b'], 'gm_ln_g': out['gm_ln_g'], 'gm_ln_b': out['gm_ln_b'], 'gm_w_s': out['gm_w_s'], 'gm_b_s': out['gm_b_s'], 'w_out': out['w_out'], 'norm_xa_g': out['norm_xa_g'], 'mem_norm_g': out['mem_norm_g'], 'xa_wq': out['xa_wq'], 'xa_wkv': out['xa_wkv'], 'xa_wo': out['xa_wo'], 'norm_ffn_g': out['norm_ffn_g'], 'ffn_w_gate_up': out['ffn_w_gate_up'], 'ffn_w_down': out['ffn_w_down'], 'final_norm_g': out['final_norm_g'], 'loss_target': out['loss_target'], 'm_norm_mix_g': out['m_norm_mix_g'], 'm_w_in': out['m_w_in'], 'm_b_in': out['m_b_in'], 'm_conv_w': out['m_conv_w'], 'm_conv_b': out['m_conv_b'], 'm_conv_ln_g': out['m_conv_ln_g'], 'm_conv_ln_b': out['m_conv_ln_b'], 'm_gm_ln_g': out['m_gm_ln_g'], 'm_gm_ln_b': out['m_gm_ln_b'], 'm_gm_w_s': out['m_gm_w_s'], 'm_gm_b_s': out['m_gm_b_s'], 'm_w_out': out['m_w_out'], 'm_norm_xa_g': out['m_norm_xa_g'], 'm_mem_norm_g': out['m_mem_norm_g'], 'm_xa_wq': out['m_xa_wq'], 'm_xa_wkv': out['m_xa_wkv'], 'm_xa_wo': out['m_xa_wo'], 'm_norm_ffn_g': out['m_norm_ffn_g'], 'm_ffn_w_gate_up': out['m_ffn_w_gate_up'], 'm_ffn_w_down': out['m_ffn_w_down'], 'm_final_norm_g': out['m_final_norm_g'], 'v_norm_mix_g': out['v_norm_mix_g'], 'v_w_in': out['v_w_in'], 'v_b_in': out['v_b_in'], 'v_conv_w': out['v_conv_w'], 'v_conv_b': out['v_conv_b'], 'v_conv_ln_g': out['v_conv_ln_g'], 'v_conv_ln_b': out['v_conv_ln_b'], 'v_gm_ln_g': out['v_gm_ln_g'], 'v_gm_ln_b': out['v_gm_ln_b'], 'v_gm_w_s': out['v_gm_w_s'], 'v_gm_b_s': out['v_gm_b_s'], 'v_w_out': out['v_w_out'], 'v_norm_xa_g': out['v_norm_xa_g'], 'v_mem_norm_g': out['v_mem_norm_g'], 'v_xa_wq': out['v_xa_wq'], 'v_xa_wkv': out['v_xa_wkv'], 'v_xa_wo': out['v_xa_wo'], 'v_norm_ffn_g': out['v_norm_ffn_g'], 'v_ffn_w_gate_up': out['v_ffn_w_gate_up'], 'v_ffn_w_down': out['v_ffn_w_down'], 'v_final_norm_g': out['v_final_norm_g']}


def _loss(weights, diff, rest, loss_target):
    with _jax.named_scope("forward"):
        args = {**rest, TWIN_DIFF_INPUT: diff, **{k: w.astype(_WEIGHT_DTYPES[k]) for k, w in weights.items()}}
        y = _forward(args)
    with _jax.named_scope("loss_head"):
        err = _jnp.square(y.astype(_jnp.float32) - loss_target)
        return 0.5 * _jnp.sum(_jnp.mean(err, axis=-1)) if err.ndim else 0.5 * err


def _adamw(w, g, m, v):
    m = ADAM_B1 * m + (1.0 - ADAM_B1) * g
    v = ADAM_B2 * v + (1.0 - ADAM_B2) * _jnp.square(g)
    m_hat = m / (1.0 - ADAM_B1 ** ADAM_STEP)
    v_hat = v / (1.0 - ADAM_B2 ** ADAM_STEP)
    delta = -ADAM_LR * (m_hat / (_jnp.sqrt(v_hat) + ADAM_EPS) + ADAM_WD * w)
    return delta, m, v


def reference(x, mem, norm_mix_g, w_in, b_in, conv_w, conv_b, conv_ln_g, conv_ln_b, gm_ln_g, gm_ln_b, gm_w_s, gm_b_s, w_out, norm_xa_g, mem_norm_g, xa_wq, xa_wkv, xa_wo, norm_ffn_g, ffn_w_gate_up, ffn_w_down, final_norm_g, loss_target, m_norm_mix_g, m_w_in, m_b_in, m_conv_w, m_conv_b, m_conv_ln_g, m_conv_ln_b, m_gm_ln_g, m_gm_ln_b, m_gm_w_s, m_gm_b_s, m_w_out, m_norm_xa_g, m_mem_norm_g, m_xa_wq, m_xa_wkv, m_xa_wo, m_norm_ffn_g, m_ffn_w_gate_up, m_ffn_w_down, m_final_norm_g, v_norm_mix_g, v_w_in, v_b_in, v_conv_w, v_conv_b, v_conv_ln_g, v_conv_ln_b, v_gm_ln_g, v_gm_ln_b, v_gm_w_s, v_gm_b_s, v_w_out, v_norm_xa_g, v_mem_norm_g, v_xa_wq, v_xa_wkv, v_xa_wo, v_norm_ffn_g, v_ffn_w_gate_up, v_ffn_w_down, v_final_norm_g):
    given = dict(x=x, mem=mem, norm_mix_g=norm_mix_g, w_in=w_in, b_in=b_in, conv_w=conv_w, conv_b=conv_b, conv_ln_g=conv_ln_g, conv_ln_b=conv_ln_b, gm_ln_g=gm_ln_g, gm_ln_b=gm_ln_b, gm_w_s=gm_w_s, gm_b_s=gm_b_s, w_out=w_out, norm_xa_g=norm_xa_g, mem_norm_g=mem_norm_g, xa_wq=xa_wq, xa_wkv=xa_wkv, xa_wo=xa_wo, norm_ffn_g=norm_ffn_g, ffn_w_gate_up=ffn_w_gate_up, ffn_w_down=ffn_w_down, final_norm_g=final_norm_g, loss_target=loss_target, m_norm_mix_g=m_norm_mix_g, m_w_in=m_w_in, m_b_in=m_b_in, m_conv_w=m_conv_w, m_conv_b=m_conv_b, m_conv_ln_g=m_conv_ln_g, m_conv_ln_b=m_conv_ln_b, m_gm_ln_g=m_gm_ln_g, m_gm_ln_b=m_gm_ln_b, m_gm_w_s=m_gm_w_s, m_gm_b_s=m_gm_b_s, m_w_out=m_w_out, m_norm_xa_g=m_norm_xa_g, m_mem_norm_g=m_mem_norm_g, m_xa_wq=m_xa_wq, m_xa_wkv=m_xa_wkv, m_xa_wo=m_xa_wo, m_norm_ffn_g=m_norm_ffn_g, m_ffn_w_gate_up=m_ffn_w_gate_up, m_ffn_w_down=m_ffn_w_down, m_final_norm_g=m_final_norm_g, v_norm_mix_g=v_norm_mix_g, v_w_in=v_w_in, v_b_in=v_b_in, v_conv_w=v_conv_w, v_conv_b=v_conv_b, v_conv_ln_g=v_conv_ln_g, v_conv_ln_b=v_conv_ln_b, v_gm_ln_g=v_gm_ln_g, v_gm_ln_b=v_gm_ln_b, v_gm_w_s=v_gm_w_s, v_gm_b_s=v_gm_b_s, v_w_out=v_w_out, v_norm_xa_g=v_norm_xa_g, v_mem_norm_g=v_mem_norm_g, v_xa_wq=v_xa_wq, v_xa_wkv=v_xa_wkv, v_xa_wo=v_xa_wo, v_norm_ffn_g=v_norm_ffn_g, v_ffn_w_gate_up=v_ffn_w_gate_up, v_ffn_w_down=v_ffn_w_down, v_final_norm_g=v_final_norm_g)
    weights = {n: given[n] for n in TWIN_WEIGHTS}
    shared = {n: given[n] for n in SHARED_INPUTS}
    per_example = {n: given[n] for n in ['x', 'mem']}
    grad_fn = _jax.value_and_grad(_loss, argnums=(0, 1))

    def one_microbatch(ex, loss_target):
        ex = dict(ex)
        diff = ex.pop(TWIN_DIFF_INPUT)
        return grad_fn(weights, diff, {**shared, **ex}, loss_target)

    if N_MICROBATCH == 1:
        loss, (grad_w, grad_x) = one_microbatch(per_example, given["loss_target"])
    else:
        def body(carry, xs):
            loss_sum, grad_sum = carry
            l_k, (gw_k, gx_k) = one_microbatch(xs[0], xs[1])
            with _jax.named_scope("update"):
                return (loss_sum + l_k, _jax.tree.map(_jnp.add, grad_sum, gw_k)), gx_k

        init = (_jnp.zeros((), _jnp.float32), _jax.tree.map(_jnp.zeros_like, weights))
        (loss, grad_w), grad_x = _jax.lax.scan(body, init, (per_example, given["loss_target"]))
    with _jax.named_scope("update"):
        delta_w, new_m, new_v = {}, {}, {}
        for n in TWIN_WEIGHTS:
            delta_w[n], new_m[n], new_v[n] = _adamw(weights[n], grad_w[n], given["m_" + n], given["v_" + n])
    return (loss, grad_x, *[grad_w[n] for n in TWIN_WEIGHTS], *[delta_w[n] for n in TWIN_WEIGHTS],
            *[new_m[n] for n in TWIN_WEIGHTS], *[new_v[n] for n in TWIN_WEIGHTS])
```

```python
import functools

import jax
import jax.numpy as jnp
from jax import lax
from jax.experimental import pallas as pl
from jax.experimental.pallas import tpu as pltpu

F32 = jnp.float32
BF16 = jnp.bfloat16

D_MODEL = 1024
CONV_WIDTH = 512
GM_WIDTH = 512
CONV_KERNEL = 31
CONV_HALO = 32
CHUNK = 128
GM_HEADS = 8
GM_HEAD_DIM = 64
XA_HEADS = 4
XA_HEAD_DIM = 256
FFN_HIDDEN = 2816
FFN_HALF = FFN_HIDDEN // 2
RMS_EPS = 1e-6
LN_EPS = 1e-5
N_CHIPS = 4
LANES = 128

ADAM_LR = 0.001
ADAM_B1 = 0.9
ADAM_B2 = 0.999
ADAM_EPS = 1e-08
ADAM_WD = 0.01
ADAM_STEP = 10

VMEM_LIMIT_BYTES = 56 * 1024 * 1024
MESH = pl.DeviceIdType.MESH
ANY = pl.BlockSpec(memory_space=pl.ANY)

_NT = (((1,), (1,)), ((), ()))
_TN = (((0,), (0,)), ((), ()))
_GELU_C = 0.7978845608028654
_GELU_A = 0.044715


def _dot(a, b):
    return jnp.dot(a, b, preferred_element_type=F32)


def _dot_nt(a, b):
    return lax.dot_general(a, b, _NT, preferred_element_type=F32)


def _dot_tn(a, b):
    return lax.dot_general(a, b, _TN, preferred_element_type=F32)


def _mean(v):
    return jnp.mean(v, axis=-1, keepdims=True)


def _rowsum(v):
    return jnp.sum(v, axis=0, keepdims=True)


def _sigmoid(v):
    return 1.0 / (1.0 + jnp.exp(-v))


def _gelu_parts(v):
    v2 = v * v
    t = jnp.tanh(_GELU_C * (v + _GELU_A * v * v2))
    g = 0.5 * v * (1.0 + t)
    dg = 0.5 * (1.0 + t) + 0.5 * v * (1.0 - t * t) * (_GELU_C * (1.0 + 3.0 * _GELU_A * v2))
    return g, dg


def _rms_stats(v):
    return lax.rsqrt(_mean(v * v) + RMS_EPS)


def _rms_bwd(dy, v, r, g):
    n = v * r
    dn = dy * g
    dv = r * (dn - n * _mean(dn * n))
    return dv, _rowsum(dy * n)


def _ln_stats(v):
    mu = _mean(v)
    xc = v - mu
    rs = lax.rsqrt(_mean(xc * xc) + LN_EPS)
    return xc * rs, rs


def _ln_bwd(dy, xh, rs, g):
    dxh = dy * g
    dv = rs * (dxh - _mean(dxh) - xh * _mean(dxh * xh))
    return dv, _rowsum(dy * xh), _rowsum(dy)


def _params(sem):
    return pltpu.CompilerParams(dimension_semantics=sem, vmem_limit_bytes=VMEM_LIMIT_BYTES)


def _row_tile(s):
    return 512 if s % 512 == 0 and s >= 2048 else 128


def _mesh_pos():
    return lax.axis_index("x"), lax.axis_index("y"), lax.axis_index("c")


def _cast_bf16(w, name):
    r, c = w.shape
    tr = r if r <= 512 else r // 2

    def body(w_ref, o_ref):
        o_ref[...] = w_ref[...].astype(BF16)

    return pl.pallas_call(
        body, name=name, grid=(r // tr,),
        in_specs=[pl.BlockSpec((tr, c), lambda i: (i, 0))],
        out_specs=pl.BlockSpec((tr, c), lambda i: (i, 0)),
        out_shape=jax.ShapeDtypeStruct((r, c), BF16),
        compiler_params=_params(("parallel",)),
    )(w)


def _adam_rows(r, c):
    for tr in (1024, 704, 640, 512, 352, 320, 256, 128, 64, 32, 16, 8):
        if r % tr == 0 and tr * c * 4 <= (3 << 19):
            return tr
    return r


def _adamw(w, g, m, v, name):
    r, c = w.shape
    tr = _adam_rows(r, c)

    def body(w_ref, g_ref, m_ref, v_ref, d_ref, nm_ref, nv_ref):
        gv = g_ref[...]
        nm = ADAM_B1 * m_ref[...] + (1.0 - ADAM_B1) * gv
        nv = ADAM_B2 * v_ref[...] + (1.0 - ADAM_B2) * (gv * gv)
        m_hat = nm / (1.0 - ADAM_B1 ** ADAM_STEP)
        v_hat = nv / (1.0 - ADAM_B2 ** ADAM_STEP)
        d_ref[...] = -ADAM_LR * (m_hat / (jnp.sqrt(v_hat) + ADAM_EPS) + ADAM_WD * w_ref[...])
        nm_ref[...] = nm
        nv_ref[...] = nv

    spec = pl.BlockSpec((tr, c), lambda i: (i, 0))
    shp = jax.ShapeDtypeStruct((r, c), F32)
    return pl.pallas_call(
        body, name=name, grid=(r // tr,),
        in_specs=[spec] * 4, out_specs=[spec] * 3, out_shape=[shp] * 3,
        compiler_params=_params(("parallel",)),
    )(w, g, m, v)


def _other_chips(x, y):
    return [(1 - x, y), (x, 1 - y), (1 - x, 1 - y)]


def _gather_shards(shards):
    n = len(shards)

    def body(*refs):
        ins, outs = refs[:n], refs[n:2 * n]
        ici_send, ici_recv, fwd_send, fwd_recv, loc_sem = refs[2 * n:]
        x, y, c = _mesh_pos()
        me = 2 * x + y
        sib = (x, y, 1 - c)
        chips = _other_chips(x, y)

        locs = [pltpu.make_async_copy(ins[a], outs[a].at[me], loc_sem.at[a]) for a in range(n)]
        for cp in locs:
            cp.start()

        def ici(a, k, chip_idx, to):
            return pltpu.make_async_remote_copy(
                src_ref=ins[a].at[c], dst_ref=outs[a].at[chip_idx, c],
                send_sem=ici_send.at[a, k], recv_sem=ici_recv.at[a, k],
                device_id=to, device_id_type=MESH)

        def fwd(a, k, chip_idx, half):
            return pltpu.make_async_remote_copy(
                src_ref=outs[a].at[chip_idx, half], dst_ref=outs[a].at[chip_idx, half],
                send_sem=fwd_send.at[a, k], recv_sem=fwd_recv.at[a, k],
                device_id=sib, device_id_type=MESH)

        sends = [ici(a, k, me, (*chips[k], c)) for a in range(n) for k in range(3)]
        for cp in sends:
            cp.start()
        passed = []
        for a in range(n):
            for k in range(3):
                ck = 2 * chips[k][0] + chips[k][1]
                ici(a, k, ck, (*chips[k], c)).wait_recv()
                cp = fwd(a, k, ck, c)
                cp.start()
                passed.append(cp)
        for a in range(n):
            for k in range(3):
                ck = 2 * chips[k][0] + chips[k][1]
                fwd(a, k, ck, 1 - c).wait_recv()
        for cp in sends + passed:
            cp.wait_send()
        for cp in locs:
            cp.wait()

    out_shape = [jax.ShapeDtypeStruct((N_CHIPS,) + s.shape, s.dtype) for s in shards]
    return pl.pallas_call(
        body, name="gather_weights",
        in_specs=[ANY] * n, out_specs=[ANY] * n, out_shape=out_shape,
        scratch_shapes=[pltpu.SemaphoreType.DMA((n, 3))] * 4 + [pltpu.SemaphoreType.DMA((n,))],
    )(*shards)


def _swap_halves(grads):
    n = len(grads)

    def body(*refs):
        ins, outs = refs[:n], refs[n:2 * n]
        send_sem, recv_sem = refs[2 * n:]
        x, y, c = _mesh_pos()
        cps = [pltpu.make_async_remote_copy(
            src_ref=ins[a].at[:, pl.ds(1 - c, 1)], dst_ref=outs[a],
            send_sem=send_sem.at[a], recv_sem=recv_sem.at[a],
            device_id=(x, y, 1 - c), device_id_type=MESH) for a in range(n)]
        for cp in cps:
            cp.start()
        for cp in cps:
            cp.wait()

    out_shape = [jax.ShapeDtypeStruct((g.shape[0], 1) + g.shape[2:], g.dtype) for g in grads]
    return pl.pallas_call(
        body, name="swap_halves",
        in_specs=[ANY] * n, out_specs=[ANY] * n, out_shape=out_shape,
        scratch_shapes=[pltpu.SemaphoreType.DMA((n,))] * 2,
    )(*grads)


def _add_half(g, got, c_arr, name):
    j, _, h, c = g.shape

    def body(c_ref, g_ref, r_ref, o_ref):
        o_ref[0] = (g_ref[0, 0] + r_ref[0, 0]).astype(BF16)

    return pl.pallas_call(
        body, name=name,
        grid_spec=pltpu.PrefetchScalarGridSpec(
            num_scalar_prefetch=1, grid=(j,),
            in_specs=[pl.BlockSpec((1, 1, h, c), lambda i, cr: (i, cr[0], 0, 0)),
                      pl.BlockSpec((1, 1, h, c), lambda i, cr: (i, 0, 0, 0))],
            out_specs=pl.BlockSpec((1, h, c), lambda i, cr: (i, 0, 0))),
        out_shape=jax.ShapeDtypeStruct((j, h, c), BF16),
        compiler_params=_params(("parallel",)),
    )(c_arr, g, got)


def _exchange_chip_sums(sums):
    n = len(sums)

    def body(*refs):
        ins, outs = refs[:n], refs[n:2 * n]
        send_sem, recv_sem, loc_sem = refs[2 * n:]
        x, y, c = _mesh_pos()
        me = 2 * x + y
        chips = _other_chips(x, y)

        def block(a, chip_idx):
            return ins[a].at[chip_idx] if ins[a].shape[0] == N_CHIPS else ins[a].at[0]

        locs = [pltpu.make_async_copy(block(a, me), outs[a].at[me], loc_sem.at[a]) for a in range(n)]
        for cp in locs:
            cp.start()

        def ici(a, k, src_chip, dst_slot):
            return pltpu.make_async_remote_copy(
                src_ref=block(a, src_chip), dst_ref=outs[a].at[dst_slot],
                send_sem=send_sem.at[a, k], recv_sem=recv_sem.at[a, k],
                device_id=(*chips[k], c), device_id_type=MESH)

        sends = []
        for a in range(n):
            for k in range(3):
                ck = 2 * chips[k][0] + chips[k][1]
                sends.append(ici(a, k, ck, me))
        for cp in sends:
            cp.start()
        for a in range(n):
            for k in range(3):
                ck = 2 * chips[k][0] + chips[k][1]
                ici(a, k, ck, ck).wait_recv()
        for cp in sends:
            cp.wait_send()
        for cp in locs:
            cp.wait()

    out_shape = [jax.ShapeDtypeStruct((N_CHIPS,) + s.shape[1:], s.dtype) for s in sums]
    return pl.pallas_call(
        body, name="exchange_chip_sums",
        in_specs=[ANY] * n, out_specs=[ANY] * n, out_shape=out_shape,
        scratch_shapes=[pltpu.SemaphoreType.DMA((n, 3))] * 2 + [pltpu.SemaphoreType.DMA((n,))],
    )(*sums)


def _sum_chips(parts, name):
    _, h, c = parts.shape

    def body(p_ref, o_ref):
        o_ref[...] = ((p_ref[0].astype(F32) + p_ref[1].astype(F32)) + p_ref[2].astype(F32)) + p_ref[3].astype(F32)

    return pl.pallas_call(
        body, name=name,
        in_specs=[pl.BlockSpec((N_CHIPS, h, c), lambda: (0, 0, 0))],
        out_specs=pl.BlockSpec((h, c), lambda: (0, 0)),
        out_shape=jax.ShapeDtypeStruct((h, c), F32),
        compiler_params=pltpu.CompilerParams(vmem_limit_bytes=VMEM_LIMIT_BYTES),
    )(parts)


def _join_halves(totals):
    n = len(totals)

    def body(*refs):
        ins, outs = refs[:n], refs[n:2 * n]
        send_sem, recv_sem, loc_sem = refs[2 * n:]
        x, y, c = _mesh_pos()
        locs = [pltpu.make_async_copy(ins[a], outs[a].at[c], loc_sem.at[a]) for a in range(n)]
        for cp in locs:
            cp.start()
        sends = [pltpu.make_async_remote_copy(
            src_ref=ins[a], dst_ref=outs[a].at[c],
            send_sem=send_sem.at[a], recv_sem=recv_sem.at[a],
            device_id=(x, y, 1 - c), device_id_type=MESH) for a in range(n)]
        for cp in sends:
            cp.start()
        for a in range(n):
            pltpu.make_async_remote_copy(
                src_ref=ins[a], dst_ref=outs[a].at[1 - c],
                send_sem=send_sem.at[a], recv_sem=recv_sem.at[a],
                device_id=(x, y, 1 - c), device_id_type=MESH).wait_recv()
        for cp in sends:
            cp.wait_send()
        for cp in locs:
            cp.wait()

    out_shape = [jax.ShapeDtypeStruct((2,) + t.shape, t.dtype) for t in totals]
    return pl.pallas_call(
        body, name="join_halves",
        in_specs=[ANY] * n, out_specs=[ANY] * n, out_shape=out_shape,
        scratch_shapes=[pltpu.SemaphoreType.DMA((n,))] * 3,
    )(*totals)


def _mix_in(x, g, w_in, b_in, ts):
    s = x.shape[0]

    def body(x_ref, g_ref, w_ref, b_ref, z_ref, hn_ref):
        xv = x_ref[...]
        hn = (xv * _rms_stats(xv) * g_ref[...]).astype(BF16)
        hn_ref[...] = hn
        for j in range(4):
            cols = slice(j * 512, (j + 1) * 512)
            z_ref[:, cols] = _dot(hn, w_ref[j]) + b_ref[:, cols]

    return pl.pallas_call(
        body, name="mix_in", grid=(s // ts,),
        in_specs=[pl.BlockSpec((ts, D_MODEL), lambda i: (i, 0)),
                  pl.BlockSpec((1, D_MODEL), lambda i: (0, 0)),
                  pl.BlockSpec((4, D_MODEL, 512), lambda i: (0, 0, 0)),
                  pl.BlockSpec((1, 2048), lambda i: (0, 0))],
        out_specs=[pl.BlockSpec((ts, 2048), lambda i: (i, 0)),
                   pl.BlockSpec((ts, D_MODEL), lambda i: (i, 0))],
        out_shape=[jax.ShapeDtypeStruct((s, 2048), F32), jax.ShapeDtypeStruct((s, D_MODEL), BF16)],
        compiler_params=_params(("parallel",)),
    )(x, g, w_in, b_in)


def _lane_is_low_head():
    lane = lax.broadcasted_iota(jnp.int32, (1, GM_WIDTH), 1)
    return (lane & GM_HEAD_DIM) == 0


def _gm_mix(v_lo, v_hi, wpair_ref, bias_ref, mixed_ref, t):
    for n in range(t // CHUNK):
        rows = slice(n * CHUNK, (n + 1) * CHUNK)
        for j in range(GM_HEADS // 2):
            cols = slice(j * LANES, (j + 1) * LANES)
            rhs = jnp.concatenate([v_lo[rows, cols], v_hi[rows, cols]], axis=0)
            mixed_ref[rows, cols] = _dot(wpair_ref[j], rhs) + bias_ref[:, cols]


def _seqmix_fwd(z, cw, cb, lng, lnb, gg, gb, wpair, bias, t):
    s = z.shape[0]

    def body(z_ref, cw_ref, cb_ref, lng_ref, lnb_ref, gg_ref, gb_ref, wpair_ref, bias_ref,
             mix_ref, c1_ref, abuf, mixed_ref):
        i = pl.program_id(0)

        @pl.when(i == 0)
        def _():
            abuf[0:CONV_HALO, :] = jnp.zeros((CONV_HALO, CONV_WIDTH), F32)

        @pl.when(i > 0)
        def _():
            abuf[0:CONV_HALO, :] = abuf[t:t + CONV_HALO, :]

        abuf[CONV_HALO:, :] = z_ref[:, 0:512] * _sigmoid(z_ref[:, 512:1024])
        acc = jnp.zeros((t, CONV_WIDTH), F32)
        for k in range(CONV_KERNEL):
            acc = acc + cw_ref[k:k + 1, :] * abuf[pl.ds(CONV_HALO - (CONV_KERNEL - 1) + k, t), :]
        c1 = acc + cb_ref[...]
        c1_ref[...] = c1
        xh, _ = _ln_stats(c1)
        ln = xh * lng_ref[...] + lnb_ref[...]
        mix_ref[:, 0:512] = (ln * _sigmoid(ln)).astype(BF16)

        u, _ = _gelu_parts(z_ref[:, 1024:1536])
        gv, _ = _gelu_parts(z_ref[:, 1536:2048])
        vxh, _ = _ln_stats(gv)
        v = vxh * gg_ref[...] + gb_ref[...]
        low = _lane_is_low_head()
        v_lo = jnp.where(low, v, 0.0).astype(BF16)
        v_hi = jnp.where(low, 0.0, v).astype(BF16)
        _gm_mix(v_lo, v_hi, wpair_ref, bias_ref, mixed_ref, t)
        mix_ref[:, 512:1024] = (u * mixed_ref[...]).astype(BF16)

    vec = lambda n: pl.BlockSpec((1, n), lambda i: (0, 0))
    return pl.pallas_call(
        body, name="seqmix_fwd", grid=(s // t,),
        in_specs=[pl.BlockSpec((t, 2048), lambda i: (i, 0)),
                  pl.BlockSpec((CONV_HALO, CONV_WIDTH), lambda i: (0, 0)),
                  vec(512), vec(512), vec(512), vec(512), vec(512),
                  pl.BlockSpec((4, CHUNK, 2 * CHUNK), lambda i: (0, 0, 0)),
                  pl.BlockSpec((CHUNK, GM_WIDTH), lambda i: (0, 0))],
        out_specs=[pl.BlockSpec((t, D_MODEL), lambda i: (i, 0)),
                   pl.BlockSpec((t, CONV_WIDTH), lambda i: (i, 0))],
        out_shape=[jax.ShapeDtypeStruct((s, D_MODEL), BF16), jax.ShapeDtypeStruct((s, CONV_WIDTH), F32)],
        scratch_shapes=[pltpu.VMEM((t + CONV_HALO, CONV_WIDTH), F32), pltpu.VMEM((t, GM_WIDTH), F32)],
        compiler_params=_params(("arbitrary",)),
    )(z, cw, cb, lng, lnb, gg, gb, wpair, bias)


def _out_proj_q(x, mix, w_out, g, wq, ts):
    s = x.shape[0]

    def body(x_ref, mix_ref, wo_ref, g_ref, wq_ref, h1_ref, hn_ref, q_ref):
        h1 = x_ref[...] + _dot(mix_ref[...], wo_ref[...])
        h1_ref[...] = h1
        hn = (h1 * _rms_stats(h1) * g_ref[...]).astype(BF16)
        hn_ref[...] = hn
        q_ref[...] = _dot(hn, wq_ref[...]).astype(BF16)

    row = lambda dt: pl.BlockSpec((ts, D_MODEL), lambda i: (i, 0))
    full = pl.BlockSpec((D_MODEL, D_MODEL), lambda i: (0, 0))
    return pl.pallas_call(
        body, name="out_proj_q", grid=(s // ts,),
        in_specs=[row(F32), row(BF16), full, pl.BlockSpec((1, D_MODEL), lambda i: (0, 0)), full],
        out_specs=[row(F32), row(BF16), row(BF16)],
        out_shape=[jax.ShapeDtypeStruct((s, D_MODEL), F32), jax.ShapeDtypeStruct((s, D_MODEL), BF16),
                   jax.ShapeDtypeStruct((s, D_MODEL), BF16)],
        compiler_params=_params(("parallel",)),
    )(x, mix, w_out, g, wq)


def _mem_kv(mem, g, wkv):
    m = mem.shape[0]

    def body(mem_ref, g_ref, w_ref, mn_ref, kv_ref):
        mv = mem_ref[...]
        mn = (mv * _rms_stats(mv) * g_ref[...]).astype(BF16)
        mn_ref[...] = mn
        for j in range(4):
            kv_ref[:, j * 512:(j + 1) * 512] = _dot(mn, w_ref[j]).astype(BF16)

    return pl.pallas_call(
        body, name="mem_kv",
        out_shape=[jax.ShapeDtypeStruct((m, D_MODEL), BF16), jax.ShapeDtypeStruct((m, 2 * D_MODEL), BF16)],
        compiler_params=pltpu.CompilerParams(vmem_limit_bytes=VMEM_LIMIT_BYTES),
    )(mem, g, wkv)


def _softmax_rows(sc):
    e = jnp.exp(sc - jnp.max(sc, axis=-1, keepdims=True))
    return e / jnp.sum(e, axis=-1, keepdims=True)


def _attn_fwd(q, kv, h1, wo, g, ts):
    s, m = q.shape[0], kv.shape[0]
    scale = XA_HEAD_DIM ** -0.5

    def body(q_ref, kv_ref, h1_ref, wo_ref, g_ref, o_ref, h2_ref, hn_ref):
        for h in range(XA_HEADS):
            cols = slice(h * XA_HEAD_DIM, (h + 1) * XA_HEAD_DIM)
            vcols = slice(D_MODEL + h * XA_HEAD_DIM, D_MODEL + (h + 1) * XA_HEAD_DIM)
            p = _softmax_rows(_dot_nt(q_ref[:, cols], kv_ref[:, cols]) * scale)
            o_ref[:, cols] = _dot(p.astype(BF16), kv_ref[:, vcols]).astype(BF16)
        h2 = h1_ref[...] + _dot(o_ref[...], wo_ref[...])
        h2_ref[...] = h2
        hn_ref[...] = (h2 * _rms_stats(h2) * g_ref[...]).astype(BF16)

    row = pl.BlockSpec((ts, D_MODEL), lambda i: (i, 0))
    return pl.pallas_call(
        body, name="attn_fwd", grid=(s // ts,),
        in_specs=[row, pl.BlockSpec((m, 2 * D_MODEL), lambda i: (0, 0)), row,
                  pl.BlockSpec((D_MODEL, D_MODEL), lambda i: (0, 0)),
                  pl.BlockSpec((1, D_MODEL), lambda i: (0, 0))],
        out_specs=[row, row, row],
        out_shape=[jax.ShapeDtypeStruct((s, D_MODEL), BF16), jax.ShapeDtypeStruct((s, D_MODEL), F32),
                   jax.ShapeDtypeStruct((s, D_MODEL), BF16)],
        compiler_params=_params(("parallel",)),
    )(q, kv, h1, wo, g)


def _ffn_up(hn, wgu, ts):
    s = hn.shape[0]

    def body(hn_ref, w_ref, gu_ref, act_ref):
        hv = hn_ref[...]
        gate = _dot(hv, w_ref[0, 0])
        up = _dot(hv, w_ref[1, 0])
        gu_ref[0] = gate
        gu_ref[1] = up
        act_ref[...] = (gate * _sigmoid(gate) * up).astype(BF16)

    return pl.pallas_call(
        body, name="ffn_up", grid=(2, s // ts),
        in_specs=[pl.BlockSpec((ts, D_MODEL), lambda j, i: (i, 0)),
                  pl.BlockSpec((2, 1, D_MODEL, FFN_HALF), lambda j, i: (0, j, 0, 0))],
        out_specs=[pl.BlockSpec((2, ts, FFN_HALF), lambda j, i: (0, i, j)),
                   pl.BlockSpec((ts, FFN_HALF), lambda j, i: (i, j))],
        out_shape=[jax.ShapeDtypeStruct((2, s, FFN_HIDDEN), F32), jax.ShapeDtypeStruct((s, FFN_HIDDEN), BF16)],
        compiler_params=_params(("parallel", "parallel")),
    )(hn, wgu)


def _ffn_down_loss(act, wd, h2, g, target, ts):
    s = act.shape[0]

    def body(act_ref, wd_ref, h2_ref, g_ref, t_ref, dh_ref, sq_ref, dg_ref):
        @pl.when(pl.program_id(0) == 0)
        def _():
            sq_ref[...] = jnp.zeros_like(sq_ref)
            dg_ref[...] = jnp.zeros_like(dg_ref)

        h3 = h2_ref[...] + _dot(act_ref[...], wd_ref[...])
        r = _rms_stats(h3)
        gv = g_ref[...]
        diff = h3 * r * gv - t_ref[...]
        sq_ref[...] += _rowsum(diff * diff)
        dh, dg = _rms_bwd(diff / D_MODEL, h3, r, gv)
        dh_ref[...] = dh
        dg_ref[...] += dg

    row = pl.BlockSpec((ts, D_MODEL), lambda i: (i, 0))
    vec = pl.BlockSpec((1, D_MODEL), lambda i: (0, 0))
    return pl.pallas_call(
        body, name="ffn_down_loss", grid=(s // ts,),
        in_specs=[pl.BlockSpec((ts, FFN_HIDDEN), lambda i: (i, 0)),
                  pl.BlockSpec((FFN_HIDDEN, D_MODEL), lambda i: (0, 0)), row, vec, row],
        out_specs=[row, vec, vec],
        out_shape=[jax.ShapeDtypeStruct((s, D_MODEL), F32), jax.ShapeDtypeStruct((1, D_MODEL), F32),
                   jax.ShapeDtypeStruct((1, D_MODEL), F32)],
        compiler_params=_params(("arbitrary",)),
    )(act, wd, h2, g, target)


def _grad_w(a, b, tk, tn, name):
    s, k = a.shape
    gb, _, n = b.shape
    nblk = n // tn
    tsr = 512 if s % 512 == 0 else s

    def body(a_ref, b_ref, o_ref):
        @pl.when(pl.program_id(2) == 0)
        def _():
            o_ref[...] = jnp.zeros_like(o_ref)

        o_ref[0] += _dot_tn(a_ref[...].astype(BF16), b_ref[0].astype(BF16))

    return pl.pallas_call(
        body, name=name, grid=(gb * nblk, k // tk, s // tsr),
        in_specs=[pl.BlockSpec((tsr, tk), lambda ni, ki, si: (si, ki)),
                  pl.BlockSpec((1, tsr, tn), lambda ni, ki, si: (ni // nblk, si, ni % nblk))],
        out_specs=pl.BlockSpec((1, tk, tn), lambda ni, ki, si: (ni, ki, 0)),
        out_shape=jax.ShapeDtypeStruct((gb * nblk, k, tn), F32),
        compiler_params=_params(("parallel", "parallel", "arbitrary")),
    )(a, b)


def _ffn_bwd_act(dh3, wd, gu, ts):
    s = dh3.shape[0]

    def body(dh_ref, wd_ref, gu_ref, dgu_ref):
        dact = _dot_nt(dh_ref[...].astype(BF16), wd_ref[0])
        gate, up = gu_ref[0], gu_ref[1]
        sg = _sigmoid(gate)
        dgu_ref[0] = (dact * up * (sg * (1.0 + gate * (1.0 - sg)))).astype(BF16)
        dgu_ref[1] = (dact * (gate * sg)).astype(BF16)

    return pl.pallas_call(
        body, name="ffn_bwd_act", grid=(2, s // ts),
        in_specs=[pl.BlockSpec((ts, D_MODEL), lambda j, i: (i, 0)),
                  pl.BlockSpec((1, FFN_HALF, D_MODEL), lambda j, i: (j, 0, 0)),
                  pl.BlockSpec((2, ts, FFN_HALF), lambda j, i: (0, i, j))],
        out_specs=pl.BlockSpec((2, ts, FFN_HALF), lambda j, i: (0, i, j)),
        out_shape=jax.ShapeDtypeStruct((2, s, FFN_HIDDEN), BF16),
        compiler_params=_params(("parallel", "parallel")),
    )(dh3, wd, gu)


def _ffn_bwd_in(dgu, wgu, dh3, h2, g, ts):
    s = dh3.shape[0]

    def body(dgu_ref, w_ref, dh3_ref, h2_ref, g_ref, dh2_ref, dg_ref):
        @pl.when(pl.program_id(0) == 0)
        def _():
            dg_ref[...] = jnp.zeros_like(dg_ref)

        dhn = jnp.zeros((ts, D_MODEL), F32)
        for p in range(2):
            for j in range(2):
                dhn = dhn + _dot_nt(dgu_ref[p, :, j * FFN_HALF:(j + 1) * FFN_HALF], w_ref[2 * p + j])
        h2 = h2_ref[...]
        dv, dg = _rms_bwd(dhn, h2, _rms_stats(h2), g_ref[...])
        dh2_ref[...] = dh3_ref[...] + dv
        dg_ref[...] += dg

    row = pl.BlockSpec((ts, D_MODEL), lambda i: (i, 0))
    vec = pl.BlockSpec((1, D_MODEL), lambda i: (0, 0))
    return pl.pallas_call(
        body, name="ffn_bwd_in", grid=(s // ts,),
        in_specs=[pl.BlockSpec((2, ts, FFN_HIDDEN), lambda i: (0, i, 0)),
                  pl.BlockSpec((4, D_MODEL, FFN_HALF), lambda i: (0, 0, 0)), row, row, vec],
        out_specs=[row, vec],
        out_shape=[jax.ShapeDtypeStruct((s, D_MODEL), F32), jax.ShapeDtypeStruct((1, D_MODEL), F32)],
        compiler_params=_params(("arbitrary",)),
    )(dgu, wgu, dh3, h2, g)


def _attn_bwd(dh2, wo, q, kv, wq, h1, g, ts):
    s, m = q.shape[0], kv.shape[0]
    scale = XA_HEAD_DIM ** -0.5

    def body(dh2_ref, wo_ref, q_ref, kv_ref, wq_ref, h1_ref, g_ref, dh1_ref, dq_ref, dkv_ref, dg_ref):
        @pl.when(pl.program_id(0) == 0)
        def _():
            dkv_ref[...] = jnp.zeros_like(dkv_ref)
            dg_ref[...] = jnp.zeros_like(dg_ref)

        do = _dot_nt(dh2_ref[...].astype(BF16), wo_ref[...]).astype(BF16)
        for h in range(XA_HEADS):
            cols = slice(h * XA_HEAD_DIM, (h + 1) * XA_HEAD_DIM)
            vcols = slice(D_MODEL + h * XA_HEAD_DIM, D_MODEL + (h + 1) * XA_HEAD_DIM)
            qh, kh, vh, doh = q_ref[:, cols], kv_ref[:, cols], kv_ref[:, vcols], do[:, cols]
            p = _softmax_rows(_dot_nt(qh, kh) * scale)
            dp = _dot_nt(doh, vh)
            ds = (p * (dp - jnp.sum(dp * p, axis=-1, keepdims=True)) * scale).astype(BF16)
            dq_ref[:, cols] = _dot(ds, kh).astype(BF16)
            dkv_ref[:, cols] += _dot_tn(ds, qh)
            dkv_ref[:, vcols] += _dot_tn(p.astype(BF16), doh)
        dhn = _dot_nt(dq_ref[...], wq_ref[...])
        h1 = h1_ref[...]
        dv, dg = _rms_bwd(dhn, h1, _rms_stats(h1), g_ref[...])
        dh1_ref[...] = dh2_ref[...] + dv
        dg_ref[...] += dg

    row = pl.BlockSpec((ts, D_MODEL), lambda i: (i, 0))
    full = pl.BlockSpec((D_MODEL, D_MODEL), lambda i: (0, 0))
    kvs = pl.BlockSpec((m, 2 * D_MODEL), lambda i: (0, 0))
    vec = pl.BlockSpec((1, D_MODEL), lambda i: (0, 0))
    return pl.pallas_call(
        body, name="attn_bwd", grid=(s // ts,),
        in_specs=[row, full, row, kvs, full, row, vec],
        out_specs=[row, row, kvs, vec],
        out_shape=[jax.ShapeDtypeStruct((s, D_MODEL), F32), jax.ShapeDtypeStruct((s, D_MODEL), BF16),
                   jax.ShapeDtypeStruct((m, 2 * D_MODEL), F32), jax.ShapeDtypeStruct((1, D_MODEL), F32)],
        compiler_params=_params(("arbitrary",)),
    )(dh2, wo, q, kv, wq, h1, g)


def _mem_kv_bwd(dkv, mn, wkv, mem, g):
    m = mem.shape[0]

    def body(dkv_ref, mn_ref, w_ref, mem_ref, g_ref, dw_ref, dg_ref):
        dmn = jnp.zeros((m, D_MODEL), F32)
        mn = mn_ref[...]
        for j in range(4):
            dj = dkv_ref[:, j * 512:(j + 1) * 512].astype(BF16)
            dw_ref[j] = _dot_tn(mn, dj)
            dmn = dmn + _dot_nt(dj, w_ref[j])
        mv = mem_ref[...]
        dg_ref[...] = _rowsum(dmn * (mv * _rms_stats(mv)))

    return pl.pallas_call(
        body, name="mem_kv_bwd",
        out_shape=[jax.ShapeDtypeStruct((4, D_MODEL, 512), F32), jax.ShapeDtypeStruct((1, D_MODEL), F32)],
        compiler_params=pltpu.CompilerParams(vmem_limit_bytes=VMEM_LIMIT_BYTES),
    )(dkv, mn, wkv, mem, g)


def _seqmix_bwd(dh1, x, z, c1, w_out, w_in, g_mix, cw, lng, lnb, gg, gb, wpair, wpair_t, bias, t):
    s = x.shape[0]
    nt = s // t
    halo_blocks = t // CONV_HALO

    def body(dh1_ref, x_ref, z_ref, zh_ref, c1_ref, wo_ref, wi_ref, gm_ref, cw_ref, lng_ref, lnb_ref,
             gg_ref, gb_ref, wpair_ref, wpt_ref, bias_ref,
             gx_ref, dz_ref, dcw_ref, dcb_ref, dlng_ref, dlnb_ref, dgg_ref, dgb_ref, dws_ref, dbs_ref,
             dbin_ref, dgm_ref, abuf, dbuf, mixed_ref, dv_ref):
        i = pl.program_id(0)
        tile = nt - 1 - i
        accs = (dcw_ref, dcb_ref, dlng_ref, dlnb_ref, dgg_ref, dgb_ref, dws_ref, dbs_ref, dbin_ref, dgm_ref)

        @pl.when(i == 0)
        def _():
            for r in accs:
                r[...] = jnp.zeros_like(r)
            dbuf[t:t + CONV_HALO, :] = jnp.zeros((CONV_HALO, CONV_WIDTH), F32)

        @pl.when(i > 0)
        def _():
            dbuf[t:t + CONV_HALO, :] = dbuf[0:CONV_HALO, :]

        dmix = _dot_nt(dh1_ref[...].astype(BF16), wo_ref[...])

        xh, rs = _ln_stats(c1_ref[...])
        lng = lng_ref[...]
        ln = xh * lng + lnb_ref[...]
        sl = _sigmoid(ln)
        dln = dmix[:, 0:512] * (sl * (1.0 + ln * (1.0 - sl)))
        dc1, dg_ln, db_ln = _ln_bwd(dln, xh, rs, lng)
        dlng_ref[...] += dg_ln
        dlnb_ref[...] += db_ln
        dcb_ref[...] += _rowsum(dc1)
        dbuf[0:t, :] = dc1

        zh = zh_ref[...]
        a_halo = zh[:, 0:512] * _sigmoid(zh[:, 512:1024])
        abuf[0:CONV_HALO, :] = jnp.where(tile > 0, a_halo, 0.0)
        za = z_ref[:, 0:512]
        sg = _sigmoid(z_ref[:, 512:1024])
        abuf[CONV_HALO:, :] = za * sg

        da = jnp.zeros((t, CONV_WIDTH), F32)
        for k in range(CONV_KERNEL):
            da = da + cw_ref[k:k + 1, :] * dbuf[pl.ds(CONV_KERNEL - 1 - k, t), :]
            dcw_ref[k:k + 1, :] += _rowsum(dc1 * abuf[pl.ds(CONV_HALO - (CONV_KERNEL - 1) + k, t), :])
        dza = da * sg
        dzg = da * za * (sg * (1.0 - sg))
        dz_ref[:, 0:512] = dza.astype(BF16)
        dz_ref[:, 512:1024] = dzg.astype(BF16)
        dbin_ref[:, 0:512] += _rowsum(dza)
        dbin_ref[:, 512:1024] += _rowsum(dzg)

        dgm = dmix[:, 512:1024]
        u, du_dz = _gelu_parts(z_ref[:, 1024:1536])
        gv, dgv_dz = _gelu_parts(z_ref[:, 1536:2048])
        vxh, vrs = _ln_stats(gv)
        ggv = gg_ref[...]
        v = vxh * ggv + gb_ref[...]
        low = _lane_is_low_head()
        v_lo = jnp.where(low, v, 0.0).astype(BF16)
        v_hi = jnp.where(low, 0.0, v).astype(BF16)
        _gm_mix(v_lo, v_hi, wpair_ref, bias_ref, mixed_ref, t)
        dzu = dgm * mixed_ref[...] * du_dz
        dm = dgm * u
        dm_lo = jnp.where(low, dm, 0.0).astype(BF16)
        dm_hi = jnp.where(low, 0.0, dm).astype(BF16)
        vb = v.astype(BF16)
        tril = (lax.broadcasted_iota(jnp.int32, (CHUNK, CHUNK), 1)
                <= lax.broadcasted_iota(jnp.int32, (CHUNK, CHUNK), 0))
        for n in range(t // CHUNK):
            rows = slice(n * CHUNK, (n + 1) * CHUNK)
            dbs_ref[...] += dm[rows, :]
            for j in range(GM_HEADS // 2):
                cols = slice(j * LANES, (j + 1) * LANES)
                stack = jnp.concatenate([dm_lo[rows, cols], dm_hi[rows, cols]], axis=0)
                dws = _dot_nt(stack, vb[rows, cols])
                dws_ref[2 * j] += jnp.where(tril, dws[0:CHUNK], 0.0)
                dws_ref[2 * j + 1] += jnp.where(tril, dws[CHUNK:2 * CHUNK], 0.0)
                dv_ref[rows, cols] = _dot(wpt_ref[j], stack)
        dgv, dg_gm, db_gm = _ln_bwd(dv_ref[...], vxh, vrs, ggv)
        dgg_ref[...] += dg_gm
        dgb_ref[...] += db_gm
        dzv = dgv * dgv_dz
        dz_ref[:, 1024:1536] = dzu.astype(BF16)
        dz_ref[:, 1536:2048] = dzv.astype(BF16)
        dbin_ref[:, 1024:1536] += _rowsum(dzu)
        dbin_ref[:, 1536:2048] += _rowsum(dzv)

        dhn = jnp.zeros((t, D_MODEL), F32)
        for j in range(4):
            dhn = dhn + _dot_nt(dz_ref[:, j * 512:(j + 1) * 512], wi_ref[j])
        xv = x_ref[...]
        dv, dg = _rms_bwd(dhn, xv, _rms_stats(xv), gm_ref[...])
        gx_ref[...] = dh1_ref[...] + dv
        dgm_ref[...] += dg

    rev = lambda w: pl.BlockSpec((t, w), lambda i: (nt - 1 - i, 0))
    const = lambda *shape: pl.BlockSpec(shape, lambda i: (0,) * len(shape))
    halo = pl.BlockSpec((CONV_HALO, D_MODEL), lambda i: (jnp.maximum((nt - 1 - i) * halo_blocks - 1, 0), 0))
    f32 = lambda *shape: jax.ShapeDtypeStruct(shape, F32)
    return pl.pallas_call(
        body, name="seqmix_bwd", grid=(nt,),
        in_specs=[rev(D_MODEL), rev(D_MODEL), rev(2048), halo, rev(CONV_WIDTH),
                  const(D_MODEL, D_MODEL), const(4, D_MODEL, 512), const(1, D_MODEL),
                  const(CONV_HALO, CONV_WIDTH), const(1, 512), const(1, 512), const(1, 512), const(1, 512),
                  const(4, CHUNK, 2 * CHUNK), const(4, CHUNK, 2 * CHUNK), const(CHUNK, GM_WIDTH)],
        out_specs=[rev(D_MODEL), rev(2048),
                   const(CONV_HALO, CONV_WIDTH), const(1, 512), const(1, 512), const(1, 512), const(1, 512),
                   const(1, 512), const(GM_HEADS, CHUNK, CHUNK), const(CHUNK, GM_WIDTH), const(1, 2048),
                   const(1, D_MODEL)],
        out_shape=[f32(s, D_MODEL), jax.ShapeDtypeStruct((s, 2048), BF16),
                   f32(CONV_HALO, CONV_WIDTH), f32(1, 512), f32(1, 512), f32(1, 512), f32(1, 512),
                   f32(1, 512), f32(GM_HEADS, CHUNK, CHUNK), f32(CHUNK, GM_WIDTH), f32(1, 2048),
                   f32(1, D_MODEL)],
        scratch_shapes=[pltpu.VMEM((t + CONV_HALO, CONV_WIDTH), F32), pltpu.VMEM((t + CONV_HALO, CONV_WIDTH), F32),
                        pltpu.VMEM((t, GM_WIDTH), F32), pltpu.VMEM((t, GM_WIDTH), F32)],
        compiler_params=_params(("arbitrary",)),
    )(dh1, x, z, z, c1, w_out, w_in, g_mix, cw, lng, lnb, gg, gb, wpair, wpair_t, bias)


def _head_bias_grad(dbs):
    def body(d_ref, o_ref):
        dv = d_ref[...]
        lane = lax.broadcasted_iota(jnp.int32, (CHUNK, LANES), 1)
        acc = jnp.zeros((CHUNK, LANES), F32)
        for h in range(GM_HEADS):
            sh = jnp.sum(dv[:, h * GM_HEAD_DIM:(h + 1) * GM_HEAD_DIM], axis=-1, keepdims=True)
            acc = acc + jnp.where(lane == h, sh, 0.0)
        o_ref[...] = acc

    return pl.pallas_call(body, name="head_bias_grad",
                          out_shape=jax.ShapeDtypeStruct((CHUNK, LANES), F32))(dbs)


def _pack(arrays, rows):
    flat = jnp.concatenate([a.reshape(-1) for a in arrays])
    flat = jnp.pad(flat, (0, rows * LANES - flat.shape[0]))
    return flat.reshape(rows, LANES)


def _unpack(buf, shapes):
    flat = buf.reshape(-1)
    out, off = [], 0
    for shp in shapes:
        size = 1
        for d in shp:
            size *= d
        out.append(flat[off:off + size].reshape(shp))
        off += size
    return out


def _rows_for(shapes, multiple):
    total = 0
    for shp in shapes:
        size = 1
        for d in shp:
            size *= d
        total += size
    rows = -(-total // LANES)
    return -(-rows // multiple) * multiple


def kernel(x, mem, norm_mix_g, w_in, b_in, conv_w, conv_b, conv_ln_g, conv_ln_b, gm_ln_g, gm_ln_b, gm_w_s, gm_b_s, w_out, norm_xa_g, mem_norm_g, xa_wq, xa_wkv, xa_wo, norm_ffn_g, ffn_w_gate_up, ffn_w_down, final_norm_g, loss_target, m_norm_mix_g, m_w_in, m_b_in, m_conv_w, m_conv_b, m_conv_ln_g, m_conv_ln_b, m_gm_ln_g, m_gm_ln_b, m_gm_w_s, m_gm_b_s, m_w_out, m_norm_xa_g, m_mem_norm_g, m_xa_wq, m_xa_wkv, m_xa_wo, m_norm_ffn_g, m_ffn_w_gate_up, m_ffn_w_down, m_final_norm_g, v_norm_mix_g, v_w_in, v_b_in, v_conv_w, v_conv_b, v_conv_ln_g, v_conv_ln_b, v_gm_ln_g, v_gm_ln_b, v_gm_w_s, v_gm_b_s, v_w_out, v_norm_xa_g, v_mem_norm_g, v_xa_wq, v_xa_wkv, v_xa_wo, v_norm_ffn_g, v_ffn_w_gate_up, v_ffn_w_down, v_final_norm_g):
    s = x.shape[1]
    ts = _row_tile(s)
    tb = max(CHUNK, ts // 2)
    cx, cy, cc = _mesh_pos()
    chip = 2 * cx + cy
    c_arr = jnp.reshape(cc, (1,)).astype(jnp.int32)
    row = lambda a: a.reshape(1, -1)
    x2, mem2, tgt2 = x[0], mem[0], loss_target[0]

    big = dict(w_in=w_in, xa_wkv=xa_wkv, w_out=w_out, xa_wq=xa_wq, xa_wo=xa_wo,
               ffn_w_gate_up=ffn_w_gate_up, ffn_w_down=ffn_w_down)
    big_names = list(big)
    halves = lambda a: a.reshape(2, a.shape[0] // 2, a.shape[1])
    shards = [halves(_cast_bf16(big[nm], "cast_" + nm)) for nm in big_names]
    conv_w_pad = jnp.pad(conv_w, ((0, CONV_HALO - CONV_KERNEL), (0, 0)))
    gathered = _gather_shards(shards + [halves(conv_w_pad)])
    gw = {nm: gathered[i].reshape(N_CHIPS, big[nm].shape[0], big[nm].shape[1]) for i, nm in enumerate(big_names)}
    w_in_g = gw["w_in"]
    wkv_g = gw["xa_wkv"]
    w_out_g = gw["w_out"].reshape(D_MODEL, D_MODEL)
    wq_g = gw["xa_wq"].reshape(D_MODEL, D_MODEL)
    wo_g = gw["xa_wo"].reshape(D_MODEL, D_MODEL)
    wgu_g = gw["ffn_w_gate_up"]
    wd_g = gw["ffn_w_down"].reshape(FFN_HIDDEN, D_MODEL)
    cw_g = jnp.concatenate([gathered[-1][k].reshape(CONV_HALO, LANES) for k in range(N_CHIPS)], axis=1)

    tril = jnp.tril(jnp.ones((CHUNK, CHUNK), dtype=bool))
    ws = jnp.where(tril[None], gm_w_s, 0.0)
    wpair = jnp.concatenate([ws[0::2], ws[1::2]], axis=2).astype(BF16)
    ws_t = jnp.swapaxes(ws, 1, 2)
    wpair_t = jnp.concatenate([ws_t[0::2], ws_t[1::2]], axis=2).astype(BF16)
    bias = jnp.repeat(gm_b_s.T, GM_HEAD_DIM, axis=1)

    z, hn1 = _mix_in(x2, row(norm_mix_g), w_in_g, row(b_in), ts)
    mix, c1 = _seqmix_fwd(z, cw_g, row(conv_b), row(conv_ln_g), row(conv_ln_b), row(gm_ln_g), row(gm_ln_b),
                          wpair, bias, ts)
    h1, hn2, q = _out_proj_q(x2, mix, w_out_g, row(norm_xa_g), wq_g, ts)
    mn, kv = _mem_kv(mem2, row(mem_norm_g), wkv_g)
    o, h2, hn3 = _attn_fwd(q, kv, h1, wo_g, row(norm_ffn_g), ts)
    gu, act = _ffn_up(hn3, wgu_g.reshape(2, 2, D_MODEL, FFN_HALF), ts)
    dh3, sq, d_final_g = _ffn_down_loss(act, wd_g, h2, row(final_norm_g), tgt2, ts)
    loss = lax.psum(0.5 * jnp.sum(sq) / D_MODEL, ("x", "y", "c"))

    dgu = _ffn_bwd_act(dh3, wd_g.reshape(2, FFN_HALF, D_MODEL), gu, ts)
    dh2, d_ffn_g = _ffn_bwd_in(dgu, wgu_g, dh3, h2, row(norm_ffn_g), tb)
    dh1, dq, dkv, d_xa_g = _attn_bwd(dh2, wo_g, q, kv, wq_g, h1, row(norm_xa_g), ts)
    d_wkv, d_mem_g = _mem_kv_bwd(dkv, mn, wkv_g, mem2, row(mem_norm_g))
    (gx, dz, d_cw, d_cb, d_lng, d_lnb, d_gg, d_gb, d_ws, d_bs_sum, d_bin, d_mix_g) = _seqmix_bwd(
        dh1, x2, z, c1, w_out_g, w_in_g, row(norm_mix_g), cw_g, row(conv_ln_g), row(conv_ln_b),
        row(gm_ln_g), row(gm_ln_b), wpair, wpair_t, bias, tb)
    d_bs = _head_bias_grad(d_bs_sum)[:, :GM_HEADS].T

    as3 = lambda a: a.reshape((1,) + a.shape)
    d_big = {
        "w_in": _grad_w(hn1, as3(dz), 512, 512, "grad_w_in"),
        "xa_wkv": d_wkv,
        "w_out": _grad_w(mix, as3(dh1), 512, D_MODEL, "grad_w_out"),
        "xa_wq": _grad_w(hn2, as3(dq), 512, D_MODEL, "grad_xa_wq"),
        "xa_wo": _grad_w(o, as3(dh2), 512, D_MODEL, "grad_xa_wo"),
        "ffn_w_gate_up": _grad_w(hn3, dgu, 512, FFN_HALF, "grad_ffn_w_gate_up"),
        "ffn_w_down": _grad_w(act, as3(dh3), FFN_HALF, D_MODEL, "grad_ffn_w_down"),
    }

    small_names = ["norm_mix_g", "b_in", "conv_w", "conv_b", "conv_ln_g", "conv_ln_b", "gm_ln_g", "gm_ln_b",
                   "gm_w_s", "gm_b_s", "norm_xa_g", "mem_norm_g", "norm_ffn_g", "final_norm_g"]
    small_grads = dict(norm_mix_g=d_mix_g, b_in=d_bin, conv_w=d_cw[:CONV_KERNEL], conv_b=d_cb, conv_ln_g=d_lng,
                       conv_ln_b=d_lnb, gm_ln_g=d_gg, gm_ln_b=d_gb, gm_w_s=d_ws, gm_b_s=d_bs, norm_xa_g=d_xa_g,
                       mem_norm_g=d_mem_g, norm_ffn_g=d_ffn_g, final_norm_g=d_final_g)
    full_shapes = dict(norm_mix_g=(D_MODEL,), b_in=(2048,), conv_w=(CONV_KERNEL, CONV_WIDTH), conv_b=(512,),
                       conv_ln_g=(512,), conv_ln_b=(512,), gm_ln_g=(512,), gm_ln_b=(512,),
                       gm_w_s=(GM_HEADS, CHUNK, CHUNK), gm_b_s=(GM_HEADS, CHUNK), norm_xa_g=(D_MODEL,),
                       mem_norm_g=(D_MODEL,), norm_ffn_g=(D_MODEL,), final_norm_g=(D_MODEL,))
    pack_rows = _rows_for([full_shapes[nm] for nm in small_names], 32)
    small_pack = _pack([small_grads[nm] for nm in small_names], pack_rows)

    def split(g, nm):
        r, c = big[nm].shape
        return g.reshape(N_CHIPS, 2, r // 2, c)

    to_reduce = [split(d_big[nm], nm) for nm in big_names] + [small_pack.reshape(1, 2, pack_rows // 2, LANES)]
    red_names = big_names + ["small"]
    got = _swap_halves(to_reduce)
    chip_sums = [_add_half(g, r, c_arr, "chip_sum_" + nm) for g, r, nm in zip(to_reduce, got, red_names)]
    parts = _exchange_chip_sums(chip_sums)
    totals = [_sum_chips(p, "total_" + nm) for p, nm in zip(parts, red_names)]
    joined = _join_halves(totals)
    grads = {nm: joined[i].reshape(big[nm].shape) for i, nm in enumerate(big_names)}
    small_red = _unpack(joined[-1].reshape(pack_rows, LANES), [full_shapes[nm] for nm in small_names])
    for nm, g in zip(small_names, small_red):
        grads[nm] = g
    grads["conv_w"] = lax.dynamic_slice(grads["conv_w"], (0, chip * LANES), (CONV_KERNEL, LANES))

    weights = dict(norm_mix_g=norm_mix_g, w_in=w_in, b_in=b_in, conv_w=conv_w, conv_b=conv_b, conv_ln_g=conv_ln_g,
                   conv_ln_b=conv_ln_b, gm_ln_g=gm_ln_g, gm_ln_b=gm_ln_b, gm_w_s=gm_w_s, gm_b_s=gm_b_s, w_out=w_out,
                   norm_xa_g=norm_xa_g, mem_norm_g=mem_norm_g, xa_wq=xa_wq, xa_wkv=xa_wkv, xa_wo=xa_wo,
                   norm_ffn_g=norm_ffn_g, ffn_w_gate_up=ffn_w_gate_up, ffn_w_down=ffn_w_down,
                   final_norm_g=final_norm_g)
    m_in = dict(norm_mix_g=m_norm_mix_g, w_in=m_w_in, b_in=m_b_in, conv_w=m_conv_w, conv_b=m_conv_b,
                conv_ln_g=m_conv_ln_g, conv_ln_b=m_conv_ln_b, gm_ln_g=m_gm_ln_g, gm_ln_b=m_gm_ln_b, gm_w_s=m_gm_w_s,
                gm_b_s=m_gm_b_s, w_out=m_w_out, norm_xa_g=m_norm_xa_g, mem_norm_g=m_mem_norm_g, xa_wq=m_xa_wq,
                xa_wkv=m_xa_wkv, xa_wo=m_xa_wo, norm_ffn_g=m_norm_ffn_g, ffn_w_gate_up=m_ffn_w_gate_up,
                ffn_w_down=m_ffn_w_down, final_norm_g=m_final_norm_g)
    v_in = dict(norm_mix_g=v_norm_mix_g, w_in=v_w_in, b_in=v_b_in, conv_w=v_conv_w, conv_b=v_conv_b,
                conv_ln_g=v_conv_ln_g, conv_ln_b=v_conv_ln_b, gm_ln_g=v_gm_ln_g, gm_ln_b=v_gm_ln_b, gm_w_s=v_gm_w_s,
                gm_b_s=v_gm_b_s, w_out=v_w_out, norm_xa_g=v_norm_xa_g, mem_norm_g=v_mem_norm_g, xa_wq=v_xa_wq,
                xa_wkv=v_xa_wkv, xa_wo=v_xa_wo, norm_ffn_g=v_norm_ffn_g, ffn_w_gate_up=v_ffn_w_gate_up,
                ffn_w_down=v_ffn_w_down, final_norm_g=v_final_norm_g)
    delta, new_m, new_v = {}, {}, {}
    for nm in big_names:
        delta[nm], new_m[nm], new_v[nm] = _adamw(weights[nm], grads[nm], m_in[nm], v_in[nm], "adamw_" + nm)
    local_shapes = [weights[nm].shape for nm in small_names]
    adam_rows = _rows_for(local_shapes, 8)
    packed = [_pack([src[nm] for nm in small_names], adam_rows) for src in (weights, grads, m_in, v_in)]
    outs = _adamw(*packed, "adamw_small")
    for dst, buf in zip((delta, new_m, new_v), outs):
        for nm, a in zip(small_names, _unpack(buf, local_shapes)):
            dst[nm] = a

    order = ["norm_mix_g", "w_in", "b_in", "conv_w", "conv_b", "conv_ln_g", "conv_ln_b", "gm_ln_g", "gm_ln_b",
             "gm_w_s", "gm_b_s", "w_out", "norm_xa_g", "mem_norm_g", "xa_wq", "xa_wkv", "xa_wo", "norm_ffn_g",
             "ffn_w_gate_up", "ffn_w_down", "final_norm_g"]
    fit = lambda a, nm: a.reshape(weights[nm].shape)
    return (loss, gx.reshape(x.shape),
            *[fit(grads[nm], nm) for nm in order], *[fit(delta[nm], nm) for nm in order],
            *[fit(new_m[nm], nm) for nm in order], *[fit(new_v[nm], nm) for nm in order])
```

```python
import functools

import jax
import jax.numpy as jnp
from jax import lax
from jax.experimental import pallas as pl
from jax.experimental.pallas import tpu as pltpu

F32 = jnp.float32
BF16 = jnp.bfloat16

D_MODEL = 1024
CONV_WIDTH = 512
GM_WIDTH = 512
CONV_KERNEL = 31
CONV_HALO = 32
CHUNK = 128
GM_HEADS = 8
GM_HEAD_DIM = 64
XA_HEADS = 4
XA_HEAD_DIM = 256
FFN_HIDDEN = 2816
FFN_HALF = FFN_HIDDEN // 2
RMS_EPS = 1e-6
LN_EPS = 1e-5
N_CHIPS = 4
LANES = 128

ADAM_LR = 0.001
ADAM_B1 = 0.9
ADAM_B2 = 0.999
ADAM_EPS = 1e-08
ADAM_WD = 0.01
ADAM_STEP = 10

VMEM_LIMIT_BYTES = 56 * 1024 * 1024
MESH = pl.DeviceIdType.MESH
ANY = pl.BlockSpec(memory_space=pl.ANY)

_NT = (((1,), (1,)), ((), ()))
_TN = (((0,), (0,)), ((), ()))
_GELU_C = 0.7978845608028654
_GELU_A = 0.044715


def _dot(a, b):
    return jnp.dot(a, b, preferred_element_type=F32)


def _dot_nt(a, b):
    return lax.dot_general(a, b, _NT, preferred_element_type=F32)


def _dot_tn(a, b):
    return lax.dot_general(a, b, _TN, preferred_element_type=F32)


def _mean(v):
    return jnp.mean(v, axis=-1, keepdims=True)


def _rowsum(v):
    return jnp.sum(v, axis=0, keepdims=True)


def _sigmoid(v):
    return 1.0 / (1.0 + jnp.exp(-v))


def _gelu_parts(v):
    v2 = v * v
    t = jnp.tanh(_GELU_C * (v + _GELU_A * v * v2))
    g = 0.5 * v * (1.0 + t)
    dg = 0.5 * (1.0 + t) + 0.5 * v * (1.0 - t * t) * (_GELU_C * (1.0 + 3.0 * _GELU_A * v2))
    return g, dg


def _rms_stats(v):
    return lax.rsqrt(_mean(v * v) + RMS_EPS)


def _rms_bwd(dy, v, r, g):
    n = v * r
    dn = dy * g
    dv = r * (dn - n * _mean(dn * n))
    return dv, _rowsum(dy * n)


def _ln_stats(v):
    mu = _mean(v)
    xc = v - mu
    rs = lax.rsqrt(_mean(xc * xc) + LN_EPS)
    return xc * rs, rs


def _ln_bwd(dy, xh, rs, g):
    dxh = dy * g
    dv = rs * (dxh - _mean(dxh) - xh * _mean(dxh * xh))
    return dv, _rowsum(dy * xh), _rowsum(dy)


def _params(sem):
    return pltpu.CompilerParams(dimension_semantics=sem, vmem_limit_bytes=VMEM_LIMIT_BYTES)


def _row_tile(s):
    return 512 if s % 512 == 0 and s >= 2048 else 128


def _mesh_pos():
    return lax.axis_index("x"), lax.axis_index("y"), lax.axis_index("c")


def _cast_into_slot(w, pos, dtype, name):
    _, h, c = w.shape

    def body(pos_ref, w_ref, o_ref):
        o_ref[0] = w_ref[...].astype(dtype)

    return pl.pallas_call(
        body, name=name,
        grid_spec=pltpu.PrefetchScalarGridSpec(
            num_scalar_prefetch=1, grid=(2,),
            in_specs=[pl.BlockSpec((1, h, c), lambda i, p: (i, 0, 0))],
            out_specs=pl.BlockSpec((1, 1, h, c), lambda i, p: (p[0], i, 0, 0))),
        out_shape=jax.ShapeDtypeStruct((N_CHIPS, 2, h, c), dtype),
        compiler_params=_params(("parallel",)),
    )(pos, w)


def _adam_rows(r, c):
    for tr in (r, 1024, 704, 640, 512, 352, 320, 256, 128, 64, 32, 16, 8):
        if r % tr == 0 and tr * c * 4 <= (3 << 19):
            return tr
    return r


def _adamw(w, g, m, v, name):
    r, c = w.shape
    tr = _adam_rows(r, c)

    def body(w_ref, g_ref, m_ref, v_ref, d_ref, nm_ref, nv_ref):
        gv = g_ref[...]
        nm = ADAM_B1 * m_ref[...] + (1.0 - ADAM_B1) * gv
        nv = ADAM_B2 * v_ref[...] + (1.0 - ADAM_B2) * (gv * gv)
        m_hat = nm / (1.0 - ADAM_B1 ** ADAM_STEP)
        v_hat = nv / (1.0 - ADAM_B2 ** ADAM_STEP)
        d_ref[...] = -ADAM_LR * (m_hat / (jnp.sqrt(v_hat) + ADAM_EPS) + ADAM_WD * w_ref[...])
        nm_ref[...] = nm
        nv_ref[...] = nv

    spec = pl.BlockSpec((tr, c), lambda i: (i, 0))
    shp = jax.ShapeDtypeStruct((r, c), F32)
    return pl.pallas_call(
        body, name=name, grid=(r // tr,),
        in_specs=[spec] * 4, out_specs=[spec] * 3, out_shape=[shp] * 3,
        compiler_params=_params(("parallel",)),
    )(w, g, m, v)


def _other_chips(x, y):
    return [(1 - x, y), (x, 1 - y), (1 - x, 1 - y)]


def _gather_shards(bufs):
    n = len(bufs)

    def body(*refs):
        outs = refs[n:2 * n]
        ici_send, ici_recv, fwd_send, fwd_recv = refs[2 * n:]
        x, y, c = _mesh_pos()
        me = 2 * x + y
        sib = (x, y, 1 - c)
        chips = _other_chips(x, y)

        def ici(a, k, chip_idx, to):
            return pltpu.make_async_remote_copy(
                src_ref=outs[a].at[chip_idx, c], dst_ref=outs[a].at[chip_idx, c],
                send_sem=ici_send.at[a, k], recv_sem=ici_recv.at[a, k],
                device_id=to, device_id_type=MESH)

        def fwd(a, k, chip_idx, half):
            return pltpu.make_async_remote_copy(
                src_ref=outs[a].at[chip_idx, half], dst_ref=outs[a].at[chip_idx, half],
                send_sem=fwd_send.at[a, k], recv_sem=fwd_recv.at[a, k],
                device_id=sib, device_id_type=MESH)

        sends = [ici(a, k, me, (*chips[k], c)) for a in range(n) for k in range(3)]
        for cp in sends:
            cp.start()
        passed = []
        for a in range(n):
            for k in range(3):
                ck = 2 * chips[k][0] + chips[k][1]
                ici(a, k, ck, (*chips[k], c)).wait_recv()
                cp = fwd(a, k, ck, c)
                cp.start()
                passed.append(cp)
        for a in range(n):
            for k in range(3):
                ck = 2 * chips[k][0] + chips[k][1]
                fwd(a, k, ck, 1 - c).wait_recv()
        for cp in sends + passed:
            cp.wait_send()

    out_shape = [jax.ShapeDtypeStruct(b.shape, b.dtype) for b in bufs]
    return pl.pallas_call(
        body, name="gather_weights",
        in_specs=[ANY] * n, out_specs=[ANY] * n, out_shape=out_shape,
        input_output_aliases={a: a for a in range(n)},
        scratch_shapes=[pltpu.SemaphoreType.DMA((n, 3))] * 4,
    )(*bufs)


def _swap_halves(grads):
    n = len(grads)

    def body(*refs):
        ins, outs = refs[:n], refs[n:2 * n]
        send_sem, recv_sem = refs[2 * n:]
        x, y, c = _mesh_pos()
        cps = [pltpu.make_async_remote_copy(
            src_ref=ins[a].at[:, pl.ds(1 - c, 1)], dst_ref=outs[a],
            send_sem=send_sem.at[a], recv_sem=recv_sem.at[a],
            device_id=(x, y, 1 - c), device_id_type=MESH) for a in range(n)]
        for cp in cps:
            cp.start()
        for cp in cps:
            cp.wait()

    out_shape = [jax.ShapeDtypeStruct((g.shape[0], 1) + g.shape[2:], g.dtype) for g in grads]
    return pl.pallas_call(
        body, name="swap_halves",
        in_specs=[ANY] * n, out_specs=[ANY] * n, out_shape=out_shape,
        scratch_shapes=[pltpu.SemaphoreType.DMA((n,))] * 2,
    )(*grads)


def _add_half(g, got, pos, name):
    j, _, h, c = g.shape

    def body(pos_ref, g_ref, r_ref, o_ref, p_ref):
        val = (g_ref[0, 0] + r_ref[0, 0]).astype(BF16)
        o_ref[0] = val
        if j == 1:
            p_ref[0] = val
        else:
            @pl.when(pl.program_id(0) == pos_ref[0])
            def _():
                p_ref[0] = val

    return pl.pallas_call(
        body, name=name,
        grid_spec=pltpu.PrefetchScalarGridSpec(
            num_scalar_prefetch=1, grid=(j,),
            in_specs=[pl.BlockSpec((1, 1, h, c), lambda i, p: (i, p[1], 0, 0)),
                      pl.BlockSpec((1, 1, h, c), lambda i, p: (i, 0, 0, 0))],
            out_specs=[pl.BlockSpec((1, h, c), lambda i, p: (i, 0, 0)),
                       pl.BlockSpec((1, h, c), lambda i, p: (p[0], 0, 0))]),
        out_shape=[jax.ShapeDtypeStruct((j, h, c), BF16), jax.ShapeDtypeStruct((N_CHIPS, h, c), BF16)],
        compiler_params=_params(("arbitrary",)),
    )(pos, g, got)


def _exchange_chip_sums(sums, parts):
    n = len(sums)

    def body(*refs):
        ins, outs = refs[:n], refs[2 * n:3 * n]
        send_sem, recv_sem = refs[3 * n:]
        x, y, c = _mesh_pos()
        me = 2 * x + y
        chips = _other_chips(x, y)

        def block(a, chip_idx):
            return ins[a].at[chip_idx] if ins[a].shape[0] == N_CHIPS else ins[a].at[0]

        def ici(a, k, src_chip, dst_slot):
            return pltpu.make_async_remote_copy(
                src_ref=block(a, src_chip), dst_ref=outs[a].at[dst_slot],
                send_sem=send_sem.at[a, k], recv_sem=recv_sem.at[a, k],
                device_id=(*chips[k], c), device_id_type=MESH)

        sends = []
        for a in range(n):
            for k in range(3):
                ck = 2 * chips[k][0] + chips[k][1]
                sends.append(ici(a, k, ck, me))
        for cp in sends:
            cp.start()
        for a in range(n):
            for k in range(3):
                ck = 2 * chips[k][0] + chips[k][1]
                ici(a, k, ck, ck).wait_recv()
        for cp in sends:
            cp.wait_send()

    out_shape = [jax.ShapeDtypeStruct(p.shape, p.dtype) for p in parts]
    return pl.pallas_call(
        body, name="exchange_chip_sums",
        in_specs=[ANY] * (2 * n), out_specs=[ANY] * n, out_shape=out_shape,
        input_output_aliases={n + a: a for a in range(n)},
        scratch_shapes=[pltpu.SemaphoreType.DMA((n, 3))] * 2,
    )(*sums, *parts)


def _sum_chips(parts, pos, name):
    _, h, c = parts.shape

    def body(pos_ref, p_ref, o_ref):
        o_ref[0] = ((p_ref[0].astype(F32) + p_ref[1].astype(F32)) + p_ref[2].astype(F32)) + p_ref[3].astype(F32)

    return pl.pallas_call(
        body, name=name,
        grid_spec=pltpu.PrefetchScalarGridSpec(
            num_scalar_prefetch=1, grid=(1,),
            in_specs=[pl.BlockSpec((N_CHIPS, h, c), lambda i, p: (0, 0, 0))],
            out_specs=pl.BlockSpec((1, h, c), lambda i, p: (p[1], 0, 0))),
        out_shape=jax.ShapeDtypeStruct((2, h, c), F32),
        compiler_params=_params(("arbitrary",)),
    )(pos, parts)


def _join_halves(fulls):
    n = len(fulls)

    def body(*refs):
        outs = refs[n:2 * n]
        send_sem, recv_sem = refs[2 * n:]
        x, y, c = _mesh_pos()

        def half(a, which):
            return pltpu.make_async_remote_copy(
                src_ref=outs[a].at[which], dst_ref=outs[a].at[which],
                send_sem=send_sem.at[a], recv_sem=recv_sem.at[a],
                device_id=(x, y, 1 - c), device_id_type=MESH)

        sends = [half(a, c) for a in range(n)]
        for cp in sends:
            cp.start()
        for a in range(n):
            half(a, 1 - c).wait_recv()
        for cp in sends:
            cp.wait_send()

    out_shape = [jax.ShapeDtypeStruct(f.shape, f.dtype) for f in fulls]
    return pl.pallas_call(
        body, name="join_halves",
        in_specs=[ANY] * n, out_specs=[ANY] * n, out_shape=out_shape,
        input_output_aliases={a: a for a in range(n)},
        scratch_shapes=[pltpu.SemaphoreType.DMA((n,))] * 2,
    )(*fulls)


def _mix_in(x, g, w_in, b_in, ts):
    s = x.shape[0]

    def body(x_ref, g_ref, w_ref, b_ref, z_ref, hn_ref):
        xv = x_ref[...]
        hn = (xv * _rms_stats(xv) * g_ref[...]).astype(BF16)
        hn_ref[...] = hn
        for j in range(4):
            cols = slice(j * 512, (j + 1) * 512)
            z_ref[:, cols] = _dot(hn, w_ref[j]) + b_ref[:, cols]

    return pl.pallas_call(
        body, name="mix_in", grid=(s // ts,),
        in_specs=[pl.BlockSpec((ts, D_MODEL), lambda i: (i, 0)),
                  pl.BlockSpec((1, D_MODEL), lambda i: (0, 0)),
                  pl.BlockSpec((4, D_MODEL, 512), lambda i: (0, 0, 0)),
                  pl.BlockSpec((1, 2048), lambda i: (0, 0))],
        out_specs=[pl.BlockSpec((ts, 2048), lambda i: (i, 0)),
                   pl.BlockSpec((ts, D_MODEL), lambda i: (i, 0))],
        out_shape=[jax.ShapeDtypeStruct((s, 2048), F32), jax.ShapeDtypeStruct((s, D_MODEL), BF16)],
        compiler_params=_params(("parallel",)),
    )(x, g, w_in, b_in)


def _lane_is_low_head():
    lane = lax.broadcasted_iota(jnp.int32, (1, GM_WIDTH), 1)
    return (lane & GM_HEAD_DIM) == 0


def _gm_mix(v_lo, v_hi, wpair_ref, bias_ref, mixed_ref, t):
    for n in range(t // CHUNK):
        rows = slice(n * CHUNK, (n + 1) * CHUNK)
        for j in range(GM_HEADS // 2):
            cols = slice(j * LANES, (j + 1) * LANES)
            rhs = jnp.concatenate([v_lo[rows, cols], v_hi[rows, cols]], axis=0)
            mixed_ref[rows, cols] = _dot(wpair_ref[j], rhs) + bias_ref[:, cols]


def _seqmix_fwd(z, cw, cb, lng, lnb, gg, gb, wpair, bias, t):
    s = z.shape[0]

    def body(z_ref, cw_ref, cb_ref, lng_ref, lnb_ref, gg_ref, gb_ref, wpair_ref, bias_ref,
             mix_ref, c1_ref, abuf, mixed_ref):
        i = pl.program_id(0)

        @pl.when(i == 0)
        def _():
            abuf[0:CONV_HALO, :] = jnp.zeros((CONV_HALO, CONV_WIDTH), F32)

        @pl.when(i > 0)
        def _():
            abuf[0:CONV_HALO, :] = abuf[t:t + CONV_HALO, :]

        abuf[CONV_HALO:, :] = z_ref[:, 0:512] * _sigmoid(z_ref[:, 512:1024])
        acc = jnp.zeros((t, CONV_WIDTH), F32)
        for k in range(CONV_KERNEL):
            acc = acc + cw_ref[k:k + 1, :] * abuf[pl.ds(CONV_HALO - (CONV_KERNEL - 1) + k, t), :]
        c1 = acc + cb_ref[...]
        c1_ref[...] = c1
        xh, _ = _ln_stats(c1)
        ln = xh * lng_ref[...] + lnb_ref[...]
        mix_ref[:, 0:512] = (ln * _sigmoid(ln)).astype(BF16)

        u, _ = _gelu_parts(z_ref[:, 1024:1536])
        gv, _ = _gelu_parts(z_ref[:, 1536:2048])
        vxh, _ = _ln_stats(gv)
        v = vxh * gg_ref[...] + gb_ref[...]
        low = _lane_is_low_head()
        v_lo = jnp.where(low, v, 0.0).astype(BF16)
        v_hi = jnp.where(low, 0.0, v).astype(BF16)
        _gm_mix(v_lo, v_hi, wpair_ref, bias_ref, mixed_ref, t)
        mix_ref[:, 512:1024] = (u * mixed_ref[...]).astype(BF16)

    vec = lambda n: pl.BlockSpec((1, n), lambda i: (0, 0))
    return pl.pallas_call(
        body, name="seqmix_fwd", grid=(s // t,),
        in_specs=[pl.BlockSpec((t, 2048), lambda i: (i, 0)),
                  pl.BlockSpec((CONV_HALO, CONV_WIDTH), lambda i: (0, 0)),
                  vec(512), vec(512), vec(512), vec(512), vec(512),
                  pl.BlockSpec((4, CHUNK, 2 * CHUNK), lambda i: (0, 0, 0)),
                  pl.BlockSpec((CHUNK, GM_WIDTH), lambda i: (0, 0))],
        out_specs=[pl.BlockSpec((t, D_MODEL), lambda i: (i, 0)),
                   pl.BlockSpec((t, CONV_WIDTH), lambda i: (i, 0))],
        out_shape=[jax.ShapeDtypeStruct((s, D_MODEL), BF16), jax.ShapeDtypeStruct((s, CONV_WIDTH), F32)],
        scratch_shapes=[pltpu.VMEM((t + CONV_HALO, CONV_WIDTH), F32), pltpu.VMEM((t, GM_WIDTH), F32)],
        compiler_params=_params(("arbitrary",)),
    )(z, cw, cb, lng, lnb, gg, gb, wpair, bias)


def _out_proj_q(x, mix, w_out, g, wq, ts):
    s = x.shape[0]

    def body(x_ref, mix_ref, wo_ref, g_ref, wq_ref, h1_ref, hn_ref, q_ref):
        h1 = x_ref[...] + _dot(mix_ref[...], wo_ref[...])
        h1_ref[...] = h1
        hn = (h1 * _rms_stats(h1) * g_ref[...]).astype(BF16)
        hn_ref[...] = hn
        q_ref[...] = _dot(hn, wq_ref[...]).astype(BF16)

    row = lambda dt: pl.BlockSpec((ts, D_MODEL), lambda i: (i, 0))
    full = pl.BlockSpec((D_MODEL, D_MODEL), lambda i: (0, 0))
    return pl.pallas_call(
        body, name="out_proj_q", grid=(s // ts,),
        in_specs=[row(F32), row(BF16), full, pl.BlockSpec((1, D_MODEL), lambda i: (0, 0)), full],
        out_specs=[row(F32), row(BF16), row(BF16)],
        out_shape=[jax.ShapeDtypeStruct((s, D_MODEL), F32), jax.ShapeDtypeStruct((s, D_MODEL), BF16),
                   jax.ShapeDtypeStruct((s, D_MODEL), BF16)],
        compiler_params=_params(("parallel",)),
    )(x, mix, w_out, g, wq)


def _mem_kv(mem, g, wkv):
    m = mem.shape[0]

    def body(mem_ref, g_ref, w_ref, mn_ref, kv_ref):
        mv = mem_ref[...]
        mn = (mv * _rms_stats(mv) * g_ref[...]).astype(BF16)
        mn_ref[...] = mn
        for j in range(4):
            kv_ref[:, j * 512:(j + 1) * 512] = _dot(mn, w_ref[j]).astype(BF16)

    return pl.pallas_call(
        body, name="mem_kv",
        out_shape=[jax.ShapeDtypeStruct((m, D_MODEL), BF16), jax.ShapeDtypeStruct((m, 2 * D_MODEL), BF16)],
        compiler_params=pltpu.CompilerParams(vmem_limit_bytes=VMEM_LIMIT_BYTES),
    )(mem, g, wkv)


def _softmax_rows(sc):
    e = jnp.exp(sc - jnp.max(sc, axis=-1, keepdims=True))
    return e / jnp.sum(e, axis=-1, keepdims=True)


def _attn_fwd(q, kv, h1, wo, g, ts):
    s, m = q.shape[0], kv.shape[0]
    scale = XA_HEAD_DIM ** -0.5

    def body(q_ref, kv_ref, h1_ref, wo_ref, g_ref, o_ref, h2_ref, hn_ref):
        for h in range(XA_HEADS):
            cols = slice(h * XA_HEAD_DIM, (h + 1) * XA_HEAD_DIM)
            vcols = slice(D_MODEL + h * XA_HEAD_DIM, D_MODEL + (h + 1) * XA_HEAD_DIM)
            p = _softmax_rows(_dot_nt(q_ref[:, cols], kv_ref[:, cols]) * scale)
            o_ref[:, cols] = _dot(p.astype(BF16), kv_ref[:, vcols]).astype(BF16)
        h2 = h1_ref[...] + _dot(o_ref[...], wo_ref[...])
        h2_ref[...] = h2
        hn_ref[...] = (h2 * _rms_stats(h2) * g_ref[...]).astype(BF16)

    row = pl.BlockSpec((ts, D_MODEL), lambda i: (i, 0))
    return pl.pallas_call(
        body, name="attn_fwd", grid=(s // ts,),
        in_specs=[row, pl.BlockSpec((m, 2 * D_MODEL), lambda i: (0, 0)), row,
                  pl.BlockSpec((D_MODEL, D_MODEL), lambda i: (0, 0)),
                  pl.BlockSpec((1, D_MODEL), lambda i: (0, 0))],
        out_specs=[row, row, row],
        out_shape=[jax.ShapeDtypeStruct((s, D_MODEL), BF16), jax.ShapeDtypeStruct((s, D_MODEL), F32),
                   jax.ShapeDtypeStruct((s, D_MODEL), BF16)],
        compiler_params=_params(("parallel",)),
    )(q, kv, h1, wo, g)


def _ffn_up(hn, wgu, ts):
    s = hn.shape[0]

    def body(hn_ref, w_ref, gu_ref, act_ref):
        hv = hn_ref[...]
        gate = _dot(hv, w_ref[0, 0])
        up = _dot(hv, w_ref[1, 0])
        gu_ref[0] = gate
        gu_ref[1] = up
        act_ref[...] = (gate * _sigmoid(gate) * up).astype(BF16)

    return pl.pallas_call(
        body, name="ffn_up", grid=(2, s // ts),
        in_specs=[pl.BlockSpec((ts, D_MODEL), lambda j, i: (i, 0)),
                  pl.BlockSpec((2, 1, D_MODEL, FFN_HALF), lambda j, i: (0, j, 0, 0))],
        out_specs=[pl.BlockSpec((2, ts, FFN_HALF), lambda j, i: (0, i, j)),
                   pl.BlockSpec((ts, FFN_HALF), lambda j, i: (i, j))],
        out_shape=[jax.ShapeDtypeStruct((2, s, FFN_HIDDEN), F32), jax.ShapeDtypeStruct((s, FFN_HIDDEN), BF16)],
        compiler_params=_params(("parallel", "parallel")),
    )(hn, wgu)


def _ffn_down_loss(act, wd, h2, g, target, ts):
    s = act.shape[0]

    def body(act_ref, wd_ref, h2_ref, g_ref, t_ref, dh_ref, sq_ref, dg_ref):
        @pl.when(pl.program_id(0) == 0)
        def _():
            sq_ref[...] = jnp.zeros_like(sq_ref)
            dg_ref[...] = jnp.zeros_like(dg_ref)

        h3 = h2_ref[...] + _dot(act_ref[...], wd_ref[...])
        r = _rms_stats(h3)
        gv = g_ref[...]
        diff = h3 * r * gv - t_ref[...]
        sq_ref[...] += _rowsum(diff * diff)
        dh, dg = _rms_bwd(diff / D_MODEL, h3, r, gv)
        dh_ref[...] = dh
        dg_ref[...] += dg

    row = pl.BlockSpec((ts, D_MODEL), lambda i: (i, 0))
    vec = pl.BlockSpec((1, D_MODEL), lambda i: (0, 0))
    return pl.pallas_call(
        body, name="ffn_down_loss", grid=(s // ts,),
        in_specs=[pl.BlockSpec((ts, FFN_HIDDEN), lambda i: (i, 0)),
                  pl.BlockSpec((FFN_HIDDEN, D_MODEL), lambda i: (0, 0)), row, vec, row],
        out_specs=[row, vec, vec],
        out_shape=[jax.ShapeDtypeStruct((s, D_MODEL), F32), jax.ShapeDtypeStruct((1, D_MODEL), F32),
                   jax.ShapeDtypeStruct((1, D_MODEL), F32)],
        compiler_params=_params(("arbitrary",)),
    )(act, wd, h2, g, target)


def _grad_w(a, b, tk, tn, name):
    s, k = a.shape
    gb, _, n = b.shape
    nblk = n // tn
    tsr = 512 if s % 512 == 0 else s

    def body(a_ref, b_ref, o_ref):
        @pl.when(pl.program_id(2) == 0)
        def _():
            o_ref[...] = jnp.zeros_like(o_ref)

        o_ref[0] += _dot_tn(a_ref[...].astype(BF16), b_ref[0].astype(BF16))

    return pl.pallas_call(
        body, name=name, grid=(gb * nblk, k // tk, s // tsr),
        in_specs=[pl.BlockSpec((tsr, tk), lambda ni, ki, si: (si, ki)),
                  pl.BlockSpec((1, tsr, tn), lambda ni, ki, si: (ni // nblk, si, ni % nblk))],
        out_specs=pl.BlockSpec((1, tk, tn), lambda ni, ki, si: (ni, ki, 0)),
        out_shape=jax.ShapeDtypeStruct((gb * nblk, k, tn), F32),
        compiler_params=_params(("parallel", "parallel", "arbitrary")),
    )(a, b)


def _ffn_bwd_act(dh3, wd, gu, ts):
    s = dh3.shape[0]

    def body(dh_ref, wd_ref, gu_ref, dgu_ref):
        dact = _dot_nt(dh_ref[...].astype(BF16), wd_ref[0])
        gate, up = gu_ref[0], gu_ref[1]
        sg = _sigmoid(gate)
        dgu_ref[0] = (dact * up * (sg * (1.0 + gate * (1.0 - sg)))).astype(BF16)
        dgu_ref[1] = (dact * (gate * sg)).astype(BF16)

    return pl.pallas_call(
        body, name="ffn_bwd_act", grid=(2, s // ts),
        in_specs=[pl.BlockSpec((ts, D_MODEL), lambda j, i: (i, 0)),
                  pl.BlockSpec((1, FFN_HALF, D_MODEL), lambda j, i: (j, 0, 0)),
                  pl.BlockSpec((2, ts, FFN_HALF), lambda j, i: (0, i, j))],
        out_specs=pl.BlockSpec((2, ts, FFN_HALF), lambda j, i: (0, i, j)),
        out_shape=jax.ShapeDtypeStruct((2, s, FFN_HIDDEN), BF16),
        compiler_params=_params(("parallel", "parallel")),
    )(dh3, wd, gu)


def _ffn_bwd_in(dgu, wgu, dh3, h2, g, ts):
    s = dh3.shape[0]

    def body(dgu_ref, w_ref, dh3_ref, h2_ref, g_ref, dh2_ref, dg_ref):
        @pl.when(pl.program_id(0) == 0)
        def _():
            dg_ref[...] = jnp.zeros_like(dg_ref)

        dhn = jnp.zeros((ts, D_MODEL), F32)
        for p in range(2):
            for j in range(2):
                dhn = dhn + _dot_nt(dgu_ref[p, :, j * FFN_HALF:(j + 1) * FFN_HALF], w_ref[2 * p + j])
        h2 = h2_ref[...]
        dv, dg = _rms_bwd(dhn, h2, _rms_stats(h2), g_ref[...])
        dh2_ref[...] = dh3_ref[...] + dv
        dg_ref[...] += dg

    row = pl.BlockSpec((ts, D_MODEL), lambda i: (i, 0))
    vec = pl.BlockSpec((1, D_MODEL), lambda i: (0, 0))
    return pl.pallas_call(
        body, name="ffn_bwd_in", grid=(s // ts,),
        in_specs=[pl.BlockSpec((2, ts, FFN_HIDDEN), lambda i: (0, i, 0)),
                  pl.BlockSpec((4, D_MODEL, FFN_HALF), lambda i: (0, 0, 0)), row, row, vec],
        out_specs=[row, vec],
        out_shape=[jax.ShapeDtypeStruct((s, D_MODEL), F32), jax.ShapeDtypeStruct((1, D_MODEL), F32)],
        compiler_params=_params(("arbitrary",)),
    )(dgu, wgu, dh3, h2, g)


def _attn_bwd(dh2, wo, q, kv, wq, h1, g, ts):
    s, m = q.shape[0], kv.shape[0]
    scale = XA_HEAD_DIM ** -0.5

    def body(dh2_ref, wo_ref, q_ref, kv_ref, wq_ref, h1_ref, g_ref, dh1_ref, dq_ref, dkv_ref, dg_ref):
        @pl.when(pl.program_id(0) == 0)
        def _():
            dkv_ref[...] = jnp.zeros_like(dkv_ref)
            dg_ref[...] = jnp.zeros_like(dg_ref)

        do = _dot_nt(dh2_ref[...].astype(BF16), wo_ref[...]).astype(BF16)
        for h in range(XA_HEADS):
            cols = slice(h * XA_HEAD_DIM, (h + 1) * XA_HEAD_DIM)
            vcols = slice(D_MODEL + h * XA_HEAD_DIM, D_MODEL + (h + 1) * XA_HEAD_DIM)
            qh, kh, vh, doh = q_ref[:, cols], kv_ref[:, cols], kv_ref[:, vcols], do[:, cols]
            p = _softmax_rows(_dot_nt(qh, kh) * scale)
            dp = _dot_nt(doh, vh)
            ds = (p * (dp - jnp.sum(dp * p, axis=-1, keepdims=True)) * scale).astype(BF16)
            dq_ref[:, cols] = _dot(ds, kh).astype(BF16)
            dkv_ref[:, cols] += _dot_tn(ds, qh)
            dkv_ref[:, vcols] += _dot_tn(p.astype(BF16), doh)
        dhn = _dot_nt(dq_ref[...], wq_ref[...])
        h1 = h1_ref[...]
        dv, dg = _rms_bwd(dhn, h1, _rms_stats(h1), g_ref[...])
        dh1_ref[...] = dh2_ref[...] + dv
        dg_ref[...] += dg

    row = pl.BlockSpec((ts, D_MODEL), lambda i: (i, 0))
    full = pl.BlockSpec((D_MODEL, D_MODEL), lambda i: (0, 0))
    kvs = pl.BlockSpec((m, 2 * D_MODEL), lambda i: (0, 0))
    vec = pl.BlockSpec((1, D_MODEL), lambda i: (0, 0))
    return pl.pallas_call(
        body, name="attn_bwd", grid=(s // ts,),
        in_specs=[row, full, row, kvs, full, row, vec],
        out_specs=[row, row, kvs, vec],
        out_shape=[jax.ShapeDtypeStruct((s, D_MODEL), F32), jax.ShapeDtypeStruct((s, D_MODEL), BF16),
                   jax.ShapeDtypeStruct((m, 2 * D_MODEL), F32), jax.ShapeDtypeStruct((1, D_MODEL), F32)],
        compiler_params=_params(("arbitrary",)),
    )(dh2, wo, q, kv, wq, h1, g)


def _mem_kv_bwd(dkv, mn, wkv, mem, g):
    m = mem.shape[0]

    def body(dkv_ref, mn_ref, w_ref, mem_ref, g_ref, dw_ref, dg_ref):
        dmn = jnp.zeros((m, D_MODEL), F32)
        mn = mn_ref[...]
        for j in range(4):
            dj = dkv_ref[:, j * 512:(j + 1) * 512].astype(BF16)
            dw_ref[j] = _dot_tn(mn, dj)
            dmn = dmn + _dot_nt(dj, w_ref[j])
        mv = mem_ref[...]
        dg_ref[...] = _rowsum(dmn * (mv * _rms_stats(mv)))

    return pl.pallas_call(
        body, name="mem_kv_bwd",
        out_shape=[jax.ShapeDtypeStruct((4, D_MODEL, 512), F32), jax.ShapeDtypeStruct((1, D_MODEL), F32)],
        compiler_params=pltpu.CompilerParams(vmem_limit_bytes=VMEM_LIMIT_BYTES),
    )(dkv, mn, wkv, mem, g)


def _seqmix_bwd(dh1, x, z, c1, w_out, w_in, g_mix, cw, lng, lnb, gg, gb, wpair, wpair_t, bias, t):
    s = x.shape[0]
    nt = s // t
    halo_blocks = t // CONV_HALO

    def body(dh1_ref, x_ref, z_ref, zh_ref, c1_ref, wo_ref, wi_ref, gm_ref, cw_ref, lng_ref, lnb_ref,
             gg_ref, gb_ref, wpair_ref, wpt_ref, bias_ref,
             gx_ref, dz_ref, dcw_ref, dcb_ref, dlng_ref, dlnb_ref, dgg_ref, dgb_ref, dws_ref, dbs_ref,
             dbin_ref, dgm_ref, abuf, dbuf, mixed_ref, dv_ref):
        i = pl.program_id(0)
        tile = nt - 1 - i
        accs = (dcw_ref, dcb_ref, dlng_ref, dlnb_ref, dgg_ref, dgb_ref, dws_ref, dbs_ref, dbin_ref, dgm_ref)

        @pl.when(i == 0)
        def _():
            for r in accs:
                r[...] = jnp.zeros_like(r)
            dbuf[t:t + CONV_HALO, :] = jnp.zeros((CONV_HALO, CONV_WIDTH), F32)

        @pl.when(i > 0)
        def _():
            dbuf[t:t + CONV_HALO, :] = dbuf[0:CONV_HALO, :]

        dmix = _dot_nt(dh1_ref[...].astype(BF16), wo_ref[...])

        xh, rs = _ln_stats(c1_ref[...])
        lng = lng_ref[...]
        ln = xh * lng + lnb_ref[...]
        sl = _sigmoid(ln)
        dln = dmix[:, 0:512] * (sl * (1.0 + ln * (1.0 - sl)))
        dc1, dg_ln, db_ln = _ln_bwd(dln, xh, rs, lng)
        dlng_ref[...] += dg_ln
        dlnb_ref[...] += db_ln
        dcb_ref[...] += _rowsum(dc1)
        dbuf[0:t, :] = dc1

        zh = zh_ref[...]
        a_halo = zh[:, 0:512] * _sigmoid(zh[:, 512:1024])
        abuf[0:CONV_HALO, :] = jnp.where(tile > 0, a_halo, 0.0)
        za = z_ref[:, 0:512]
        sg = _sigmoid(z_ref[:, 512:1024])
        abuf[CONV_HALO:, :] = za * sg

        da = jnp.zeros((t, CONV_WIDTH), F32)
        for k in range(CONV_KERNEL):
            da = da + cw_ref[k:k + 1, :] * dbuf[pl.ds(CONV_KERNEL - 1 - k, t), :]
            dcw_ref[k:k + 1, :] += _rowsum(dc1 * abuf[pl.ds(CONV_HALO - (CONV_KERNEL - 1) + k, t), :])
        dza = da * sg
        dzg = da * za * (sg * (1.0 - sg))
        dz_ref[:, 0:512] = dza.astype(BF16)
        dz_ref[:, 512:1024] = dzg.astype(BF16)
        dbin_ref[:, 0:512] += _rowsum(dza)
        dbin_ref[:, 512:1024] += _rowsum(dzg)

        dgm = dmix[:, 512:1024]
        u, du_dz = _gelu_parts(z_ref[:, 1024:1536])
        gv, dgv_dz = _gelu_parts(z_ref[:, 1536:2048])
        vxh, vrs = _ln_stats(gv)
        ggv = gg_ref[...]
        v = vxh * ggv + gb_ref[...]
        low = _lane_is_low_head()
        v_lo = jnp.where(low, v, 0.0).astype(BF16)
        v_hi = jnp.where(low, 0.0, v).astype(BF16)
        _gm_mix(v_lo, v_hi, wpair_ref, bias_ref, mixed_ref, t)
        dzu = dgm * mixed_ref[...] * du_dz
        dm = dgm * u
        dm_lo = jnp.where(low, dm, 0.0).astype(BF16)
        dm_hi = jnp.where(low, 0.0, dm).astype(BF16)
        vb = v.astype(BF16)
        tril = (lax.broadcasted_iota(jnp.int32, (CHUNK, CHUNK), 1)
                <= lax.broadcasted_iota(jnp.int32, (CHUNK, CHUNK), 0))
        for n in range(t // CHUNK):
            rows = slice(n * CHUNK, (n + 1) * CHUNK)
            dbs_ref[...] += dm[rows, :]
            for j in range(GM_HEADS // 2):
                cols = slice(j * LANES, (j + 1) * LANES)
                stack = jnp.concatenate([dm_lo[rows, cols], dm_hi[rows, cols]], axis=0)
                dws = _dot_nt(stack, vb[rows, cols])
                dws_ref[2 * j] += jnp.where(tril, dws[0:CHUNK], 0.0)
                dws_ref[2 * j + 1] += jnp.where(tril, dws[CHUNK:2 * CHUNK], 0.0)
                dv_ref[rows, cols] = _dot(wpt_ref[j], stack)
        dgv, dg_gm, db_gm = _ln_bwd(dv_ref[...], vxh, vrs, ggv)
        dgg_ref[...] += dg_gm
        dgb_ref[...] += db_gm
        dzv = dgv * dgv_dz
        dz_ref[:, 1024:1536] = dzu.astype(BF16)
        dz_ref[:, 1536:2048] = dzv.astype(BF16)
        dbin_ref[:, 1024:1536] += _rowsum(dzu)
        dbin_ref[:, 1536:2048] += _rowsum(dzv)

        dhn = jnp.zeros((t, D_MODEL), F32)
        for j in range(4):
            dhn = dhn + _dot_nt(dz_ref[:, j * 512:(j + 1) * 512], wi_ref[j])
        xv = x_ref[...]
        dv, dg = _rms_bwd(dhn, xv, _rms_stats(xv), gm_ref[...])
        gx_ref[...] = dh1_ref[...] + dv
        dgm_ref[...] += dg

    rev = lambda w: pl.BlockSpec((t, w), lambda i: (nt - 1 - i, 0))
    const = lambda *shape: pl.BlockSpec(shape, lambda i: (0,) * len(shape))
    halo = pl.BlockSpec((CONV_HALO, D_MODEL), lambda i: (jnp.maximum((nt - 1 - i) * halo_blocks - 1, 0), 0))
    f32 = lambda *shape: jax.ShapeDtypeStruct(shape, F32)
    return pl.pallas_call(
        body, name="seqmix_bwd", grid=(nt,),
        in_specs=[rev(D_MODEL), rev(D_MODEL), rev(2048), halo, rev(CONV_WIDTH),
                  const(D_MODEL, D_MODEL), const(4, D_MODEL, 512), const(1, D_MODEL),
                  const(CONV_HALO, CONV_WIDTH), const(1, 512), const(1, 512), const(1, 512), const(1, 512),
                  const(4, CHUNK, 2 * CHUNK), const(4, CHUNK, 2 * CHUNK), const(CHUNK, GM_WIDTH)],
        out_specs=[rev(D_MODEL), rev(2048),
                   const(CONV_HALO, CONV_WIDTH), const(1, 512), const(1, 512), const(1, 512), const(1, 512),
                   const(1, 512), const(GM_HEADS, CHUNK, CHUNK), const(CHUNK, GM_WIDTH), const(1, 2048),
                   const(1, D_MODEL)],
        out_shape=[f32(s, D_MODEL), jax.ShapeDtypeStruct((s, 2048), BF16),
                   f32(CONV_HALO, CONV_WIDTH), f32(1, 512), f32(1, 512), f32(1, 512), f32(1, 512),
                   f32(1, 512), f32(GM_HEADS, CHUNK, CHUNK), f32(CHUNK, GM_WIDTH), f32(1, 2048),
                   f32(1, D_MODEL)],
        scratch_shapes=[pltpu.VMEM((t + CONV_HALO, CONV_WIDTH), F32), pltpu.VMEM((t + CONV_HALO, CONV_WIDTH), F32),
                        pltpu.VMEM((t, GM_WIDTH), F32), pltpu.VMEM((t, GM_WIDTH), F32)],
        compiler_params=_params(("arbitrary",)),
    )(dh1, x, z, z, c1, w_out, w_in, g_mix, cw, lng, lnb, gg, gb, wpair, wpair_t, bias)


def _head_bias_grad(dbs):
    def body(d_ref, o_ref):
        dv = d_ref[...]
        lane = lax.broadcasted_iota(jnp.int32, (CHUNK, LANES), 1)
        acc = jnp.zeros((CHUNK, LANES), F32)
        for h in range(GM_HEADS):
            sh = jnp.sum(dv[:, h * GM_HEAD_DIM:(h + 1) * GM_HEAD_DIM], axis=-1, keepdims=True)
            acc = acc + jnp.where(lane == h, sh, 0.0)
        o_ref[...] = acc

    return pl.pallas_call(body, name="head_bias_grad",
                          out_shape=jax.ShapeDtypeStruct((CHUNK, LANES), F32))(dbs)


def _pack(arrays, rows):
    flat = jnp.concatenate([a.reshape(-1) for a in arrays])
    flat = jnp.pad(flat, (0, rows * LANES - flat.shape[0]))
    return flat.reshape(rows, LANES)


def _unpack(buf, shapes):
    flat = buf.reshape(-1)
    out, off = [], 0
    for shp in shapes:
        size = 1
        for d in shp:
            size *= d
        out.append(flat[off:off + size].reshape(shp))
        off += size
    return out


def _rows_for(shapes, multiple):
    total = 0
    for shp in shapes:
        size = 1
        for d in shp:
            size *= d
        total += size
    rows = -(-total // LANES)
    return -(-rows // multiple) * multiple


def kernel(x, mem, norm_mix_g, w_in, b_in, conv_w, conv_b, conv_ln_g, conv_ln_b, gm_ln_g, gm_ln_b, gm_w_s, gm_b_s, w_out, norm_xa_g, mem_norm_g, xa_wq, xa_wkv, xa_wo, norm_ffn_g, ffn_w_gate_up, ffn_w_down, final_norm_g, loss_target, m_norm_mix_g, m_w_in, m_b_in, m_conv_w, m_conv_b, m_conv_ln_g, m_conv_ln_b, m_gm_ln_g, m_gm_ln_b, m_gm_w_s, m_gm_b_s, m_w_out, m_norm_xa_g, m_mem_norm_g, m_xa_wq, m_xa_wkv, m_xa_wo, m_norm_ffn_g, m_ffn_w_gate_up, m_ffn_w_down, m_final_norm_g, v_norm_mix_g, v_w_in, v_b_in, v_conv_w, v_conv_b, v_conv_ln_g, v_conv_ln_b, v_gm_ln_g, v_gm_ln_b, v_gm_w_s, v_gm_b_s, v_w_out, v_norm_xa_g, v_mem_norm_g, v_xa_wq, v_xa_wkv, v_xa_wo, v_norm_ffn_g, v_ffn_w_gate_up, v_ffn_w_down, v_final_norm_g):
    s = x.shape[1]
    ts = _row_tile(s)
    tb = max(CHUNK, ts // 2)
    cx, cy, cc = _mesh_pos()
    chip = 2 * cx + cy
    pos = jnp.stack([chip, cc]).astype(jnp.int32)
    row = lambda a: a.reshape(1, -1)
    x2, mem2, tgt2 = x[0], mem[0], loss_target[0]

    big = dict(w_in=w_in, xa_wkv=xa_wkv, w_out=w_out, xa_wq=xa_wq, xa_wo=xa_wo,
               ffn_w_gate_up=ffn_w_gate_up, ffn_w_down=ffn_w_down)
    big_names = list(big)
    halves = lambda a: a.reshape(2, a.shape[0] // 2, a.shape[1])
    bufs = [_cast_into_slot(halves(big[nm]), pos, BF16, "cast_" + nm) for nm in big_names]
    conv_w_pad = jnp.pad(conv_w, ((0, CONV_HALO - CONV_KERNEL), (0, 0)))
    gathered = _gather_shards(bufs + [_cast_into_slot(halves(conv_w_pad), pos, F32, "slot_conv_w")])
    gw = {nm: gathered[i].reshape(N_CHIPS, big[nm].shape[0], big[nm].shape[1]) for i, nm in enumerate(big_names)}
    w_in_g = gw["w_in"]
    wkv_g = gw["xa_wkv"]
    w_out_g = gw["w_out"].reshape(D_MODEL, D_MODEL)
    wq_g = gw["xa_wq"].reshape(D_MODEL, D_MODEL)
    wo_g = gw["xa_wo"].reshape(D_MODEL, D_MODEL)
    wgu_g = gw["ffn_w_gate_up"]
    wd_g = gw["ffn_w_down"].reshape(FFN_HIDDEN, D_MODEL)
    cw_g = jnp.concatenate([gathered[-1][k].reshape(CONV_HALO, LANES) for k in range(N_CHIPS)], axis=1)

    tril = jnp.tril(jnp.ones((CHUNK, CHUNK), dtype=bool))
    ws = jnp.where(tril[None], gm_w_s, 0.0)
    wpair = jnp.concatenate([ws[0::2], ws[1::2]], axis=2).astype(BF16)
    ws_t = jnp.swapaxes(ws, 1, 2)
    wpair_t = jnp.concatenate([ws_t[0::2], ws_t[1::2]], axis=2).astype(BF16)
    bias = jnp.repeat(gm_b_s.T, GM_HEAD_DIM, axis=1)

    z, hn1 = _mix_in(x2, row(norm_mix_g), w_in_g, row(b_in), ts)
    mix, c1 = _seqmix_fwd(z, cw_g, row(conv_b), row(conv_ln_g), row(conv_ln_b), row(gm_ln_g), row(gm_ln_b),
                          wpair, bias, ts)
    h1, hn2, q = _out_proj_q(x2, mix, w_out_g, row(norm_xa_g), wq_g, ts)
    mn, kv = _mem_kv(mem2, row(mem_norm_g), wkv_g)
    o, h2, hn3 = _attn_fwd(q, kv, h1, wo_g, row(norm_ffn_g), ts)
    gu, act = _ffn_up(hn3, wgu_g.reshape(2, 2, D_MODEL, FFN_HALF), ts)
    dh3, sq, d_final_g = _ffn_down_loss(act, wd_g, h2, row(final_norm_g), tgt2, ts)
    loss = lax.psum(0.5 * jnp.sum(sq) / D_MODEL, ("x", "y", "c"))

    dgu = _ffn_bwd_act(dh3, wd_g.reshape(2, FFN_HALF, D_MODEL), gu, ts)
    dh2, d_ffn_g = _ffn_bwd_in(dgu, wgu_g, dh3, h2, row(norm_ffn_g), tb)
    dh1, dq, dkv, d_xa_g = _attn_bwd(dh2, wo_g, q, kv, wq_g, h1, row(norm_xa_g), ts)
    d_wkv, d_mem_g = _mem_kv_bwd(dkv, mn, wkv_g, mem2, row(mem_norm_g))
    (gx, dz, d_cw, d_cb, d_lng, d_lnb, d_gg, d_gb, d_ws, d_bs_sum, d_bin, d_mix_g) = _seqmix_bwd(
        dh1, x2, z, c1, w_out_g, w_in_g, row(norm_mix_g), cw_g, row(conv_ln_g), row(conv_ln_b),
        row(gm_ln_g), row(gm_ln_b), wpair, wpair_t, bias, tb)
    d_bs = _head_bias_grad(d_bs_sum)[:, :GM_HEADS].T

    as3 = lambda a: a.reshape((1,) + a.shape)
    d_big = {
        "w_in": _grad_w(hn1, as3(dz), 512, 512, "grad_w_in"),
        "xa_wkv": d_wkv,
        "w_out": _grad_w(mix, as3(dh1), 512, D_MODEL, "grad_w_out"),
        "xa_wq": _grad_w(hn2, as3(dq), 512, D_MODEL, "grad_xa_wq"),
        "xa_wo": _grad_w(o, as3(dh2), 512, D_MODEL, "grad_xa_wo"),
        "ffn_w_gate_up": _grad_w(hn3, dgu, 512, FFN_HALF, "grad_ffn_w_gate_up"),
        "ffn_w_down": _grad_w(act, as3(dh3), FFN_HALF, D_MODEL, "grad_ffn_w_down"),
    }

    small_names = ["norm_mix_g", "b_in", "conv_w", "conv_b", "conv_ln_g", "conv_ln_b", "gm_ln_g", "gm_ln_b",
                   "gm_w_s", "gm_b_s", "norm_xa_g", "mem_norm_g", "norm_ffn_g", "final_norm_g"]
    small_grads = dict(norm_mix_g=d_mix_g, b_in=d_bin, conv_w=d_cw[:CONV_KERNEL], conv_b=d_cb, conv_ln_g=d_lng,
                       conv_ln_b=d_lnb, gm_ln_g=d_gg, gm_ln_b=d_gb, gm_w_s=d_ws, gm_b_s=d_bs, norm_xa_g=d_xa_g,
                       mem_norm_g=d_mem_g, norm_ffn_g=d_ffn_g, final_norm_g=d_final_g)
    full_shapes = dict(norm_mix_g=(D_MODEL,), b_in=(2048,), conv_w=(CONV_KERNEL, CONV_WIDTH), conv_b=(512,),
                       conv_ln_g=(512,), conv_ln_b=(512,), gm_ln_g=(512,), gm_ln_b=(512,),
                       gm_w_s=(GM_HEADS, CHUNK, CHUNK), gm_b_s=(GM_HEADS, CHUNK), norm_xa_g=(D_MODEL,),
                       mem_norm_g=(D_MODEL,), norm_ffn_g=(D_MODEL,), final_norm_g=(D_MODEL,))
    pack_rows = _rows_for([full_shapes[nm] for nm in small_names], 32)
    small_pack = _pack([small_grads[nm] for nm in small_names], pack_rows)

    def split(g, nm):
        r, c = big[nm].shape
        return g.reshape(N_CHIPS, 2, r // 2, c)

    to_reduce = [split(d_big[nm], nm) for nm in big_names] + [small_pack.reshape(1, 2, pack_rows // 2, LANES)]
    red_names = big_names + ["small"]
    got = _swap_halves(to_reduce)
    chip_sums = [_add_half(g, r, pos, "chip_sum_" + nm) for g, r, nm in zip(to_reduce, got, red_names)]
    parts = _exchange_chip_sums([cs[0] for cs in chip_sums], [cs[1] for cs in chip_sums])
    joined = _join_halves([_sum_chips(p, pos, "total_" + nm) for p, nm in zip(parts, red_names)])
    grads = {nm: joined[i].reshape(big[nm].shape) for i, nm in enumerate(big_names)}
    small_red = _unpack(joined[-1].reshape(pack_rows, LANES), [full_shapes[nm] for nm in small_names])
    for nm, g in zip(small_names, small_red):
        grads[nm] = g
    grads["conv_w"] = lax.dynamic_slice(grads["conv_w"], (0, chip * LANES), (CONV_KERNEL, LANES))

    weights = dict(norm_mix_g=norm_mix_g, w_in=w_in, b_in=b_in, conv_w=conv_w, conv_b=conv_b, conv_ln_g=conv_ln_g,
                   conv_ln_b=conv_ln_b, gm_ln_g=gm_ln_g, gm_ln_b=gm_ln_b, gm_w_s=gm_w_s, gm_b_s=gm_b_s, w_out=w_out,
                   norm_xa_g=norm_xa_g, mem_norm_g=mem_norm_g, xa_wq=xa_wq, xa_wkv=xa_wkv, xa_wo=xa_wo,
                   norm_ffn_g=norm_ffn_g, ffn_w_gate_up=ffn_w_gate_up, ffn_w_down=ffn_w_down,
                   final_norm_g=final_norm_g)
    m_in = dict(norm_mix_g=m_norm_mix_g, w_in=m_w_in, b_in=m_b_in, conv_w=m_conv_w, conv_b=m_conv_b,
                conv_ln_g=m_conv_ln_g, conv_ln_b=m_conv_ln_b, gm_ln_g=m_gm_ln_g, gm_ln_b=m_gm_ln_b, gm_w_s=m_gm_w_s,
                gm_b_s=m_gm_b_s, w_out=m_w_out, norm_xa_g=m_norm_xa_g, mem_norm_g=m_mem_norm_g, xa_wq=m_xa_wq,
                xa_wkv=m_xa_wkv, xa_wo=m_xa_wo, norm_ffn_g=m_norm_ffn_g, ffn_w_gate_up=m_ffn_w_gate_up,
                ffn_w_down=m_ffn_w_down, final_norm_g=m_final_norm_g)
    v_in = dict(norm_mix_g=v_norm_mix_g, w_in=v_w_in, b_in=v_b_in, conv_w=v_conv_w, conv_b=v_conv_b,
                conv_ln_g=v_conv_ln_g, conv_ln_b=v_conv_ln_b, gm_ln_g=v_gm_ln_g, gm_ln_b=v_gm_ln_b, gm_w_s=v_gm_w_s,
                gm_b_s=v_gm_b_s, w_out=v_w_out, norm_xa_g=v_norm_xa_g, mem_norm_g=v_mem_norm_g, xa_wq=v_xa_wq,
                xa_wkv=v_xa_wkv, xa_wo=v_xa_wo, norm_ffn_g=v_norm_ffn_g, ffn_w_gate_up=v_ffn_w_gate_up,
                ffn_w_down=v_ffn_w_down, final_norm_g=v_final_norm_g)
    delta, new_m, new_v = {}, {}, {}
    for nm in big_names:
        delta[nm], new_m[nm], new_v[nm] = _adamw(weights[nm], grads[nm], m_in[nm], v_in[nm], "adamw_" + nm)
    local_shapes = [weights[nm].shape for nm in small_names]
    adam_rows = _rows_for(local_shapes, LANES)
    packed = [_pack([src[nm] for nm in small_names], adam_rows) for src in (weights, grads, m_in, v_in)]
    outs = _adamw(*packed, "adamw_small")
    for dst, buf in zip((delta, new_m, new_v), outs):
        for nm, a in zip(small_names, _unpack(buf, local_shapes)):
            dst[nm] = a

    order = ["norm_mix_g", "w_in", "b_in", "conv_w", "conv_b", "conv_ln_g", "conv_ln_b", "gm_ln_g", "gm_ln_b",
             "gm_w_s", "gm_b_s", "w_out", "norm_xa_g", "mem_norm_g", "xa_wq", "xa_wkv", "xa_wo", "norm_ffn_g",
             "ffn_w_gate_up", "ffn_w_down", "final_norm_g"]
    fit = lambda a, nm: a.reshape(weights[nm].shape)
    return (loss, gx.reshape(x.shape),
            *[fit(grads[nm], nm) for nm in order], *[fit(delta[nm], nm) for nm in order],
            *[fit(new_m[nm], nm) for nm in order], *[fit(new_v[nm], nm) for nm in order])
```

```python
import functools

import jax
import jax.numpy as jnp
from jax import lax
from jax.experimental import pallas as pl
from jax.experimental.pallas import tpu as pltpu

F32 = jnp.float32
BF16 = jnp.bfloat16

D_MODEL = 1024
CONV_WIDTH = 512
GM_WIDTH = 512
CONV_KERNEL = 31
CONV_HALO = 32
CHUNK = 128
GM_HEADS = 8
GM_HEAD_DIM = 64
XA_HEADS = 4
XA_HEAD_DIM = 256
FFN_HIDDEN = 2816
FFN_HALF = FFN_HIDDEN // 2
RMS_EPS = 1e-6
LN_EPS = 1e-5
N_CHIPS = 4
LANES = 128

ADAM_LR = 0.001
ADAM_B1 = 0.9
ADAM_B2 = 0.999
ADAM_EPS = 1e-08
ADAM_WD = 0.01
ADAM_STEP = 10

VMEM_LIMIT_BYTES = 56 * 1024 * 1024
MESH = pl.DeviceIdType.MESH
ANY = pl.BlockSpec(memory_space=pl.ANY)
HBM_SPEC = pl.BlockSpec(memory_space=pltpu.HBM)
SEM_SPEC = pl.BlockSpec(memory_space=pltpu.SEMAPHORE)

_NT = (((1,), (1,)), ((), ()))
_TN = (((0,), (0,)), ((), ()))
_GELU_C = 0.7978845608028654
_GELU_A = 0.044715


def _dot(a, b):
    return jnp.dot(a, b, preferred_element_type=F32)


def _dot_nt(a, b):
    return lax.dot_general(a, b, _NT, preferred_element_type=F32)


def _dot_tn(a, b):
    return lax.dot_general(a, b, _TN, preferred_element_type=F32)


def _mean(v):
    return jnp.mean(v, axis=-1, keepdims=True)


def _rowsum(v):
    return jnp.sum(v, axis=0, keepdims=True)


def _sigmoid(v):
    return 1.0 / (1.0 + jnp.exp(-v))


def _gelu_parts(v):
    v2 = v * v
    t = jnp.tanh(_GELU_C * (v + _GELU_A * v * v2))
    g = 0.5 * v * (1.0 + t)
    dg = 0.5 * (1.0 + t) + 0.5 * v * (1.0 - t * t) * (_GELU_C * (1.0 + 3.0 * _GELU_A * v2))
    return g, dg


def _rms_stats(v):
    return lax.rsqrt(_mean(v * v) + RMS_EPS)


def _rms_bwd(dy, v, r, g):
    n = v * r
    dn = dy * g
    dv = r * (dn - n * _mean(dn * n))
    return dv, _rowsum(dy * n)


def _ln_stats(v):
    mu = _mean(v)
    xc = v - mu
    rs = lax.rsqrt(_mean(xc * xc) + LN_EPS)
    return xc * rs, rs


def _ln_bwd(dy, xh, rs, g):
    dxh = dy * g
    dv = rs * (dxh - _mean(dxh) - xh * _mean(dxh * xh))
    return dv, _rowsum(dy * xh), _rowsum(dy)


def _params(sem):
    return pltpu.CompilerParams(dimension_semantics=sem, vmem_limit_bytes=VMEM_LIMIT_BYTES)


def _row_tile(s):
    return 512 if s % 512 == 0 and s >= 2048 else 128


def _mesh_pos():
    return lax.axis_index("x"), lax.axis_index("y"), lax.axis_index("c")


def _cast_into_slot(w, pos, dtype, name):
    _, h, c = w.shape

    def body(pos_ref, w_ref, o_ref):
        o_ref[0] = w_ref[...].astype(dtype)

    return pl.pallas_call(
        body, name=name,
        grid_spec=pltpu.PrefetchScalarGridSpec(
            num_scalar_prefetch=1, grid=(2,),
            in_specs=[pl.BlockSpec((1, h, c), lambda i, p: (i, 0, 0))],
            out_specs=pl.BlockSpec((1, 1, h, c), lambda i, p: (p[0], i, 0, 0))),
        out_shape=jax.ShapeDtypeStruct((N_CHIPS, 2, h, c), dtype),
        compiler_params=_params(("parallel",)),
    )(pos, w)


def _adam_rows(r, c):
    for tr in (r, 1024, 704, 640, 512, 352, 320, 256, 128, 64, 32, 16, 8):
        if r % tr == 0 and tr * c * 4 <= (3 << 19):
            return tr
    return r


def _adamw(w, g, m, v, name):
    r, c = w.shape
    tr = _adam_rows(r, c)

    def body(w_ref, g_ref, m_ref, v_ref, d_ref, nm_ref, nv_ref):
        gv = g_ref[...]
        nm = ADAM_B1 * m_ref[...] + (1.0 - ADAM_B1) * gv
        nv = ADAM_B2 * v_ref[...] + (1.0 - ADAM_B2) * (gv * gv)
        m_hat = nm / (1.0 - ADAM_B1 ** ADAM_STEP)
        v_hat = nv / (1.0 - ADAM_B2 ** ADAM_STEP)
        d_ref[...] = -ADAM_LR * (m_hat / (jnp.sqrt(v_hat) + ADAM_EPS) + ADAM_WD * w_ref[...])
        nm_ref[...] = nm
        nv_ref[...] = nv

    spec = pl.BlockSpec((tr, c), lambda i: (i, 0))
    shp = jax.ShapeDtypeStruct((r, c), F32)
    return pl.pallas_call(
        body, name=name, grid=(r // tr,),
        in_specs=[spec] * 4, out_specs=[spec] * 3, out_shape=[shp] * 3,
        compiler_params=_params(("parallel",)),
    )(w, g, m, v)


def _other_chips(x, y):
    return [(1 - x, y), (x, 1 - y), (1 - x, 1 - y)]


def _gather_shards(bufs):
    n = len(bufs)

    def body(*refs):
        outs = refs[n:2 * n]
        ici_send, ici_recv, fwd_send, fwd_recv = refs[2 * n:]
        x, y, c = _mesh_pos()
        me = 2 * x + y
        sib = (x, y, 1 - c)
        chips = _other_chips(x, y)

        def ici(a, k, chip_idx, to):
            return pltpu.make_async_remote_copy(
                src_ref=outs[a].at[chip_idx, c], dst_ref=outs[a].at[chip_idx, c],
                send_sem=ici_send.at[a, k], recv_sem=ici_recv.at[a, k],
                device_id=to, device_id_type=MESH)

        def fwd(a, k, chip_idx, half):
            return pltpu.make_async_remote_copy(
                src_ref=outs[a].at[chip_idx, half], dst_ref=outs[a].at[chip_idx, half],
                send_sem=fwd_send.at[a, k], recv_sem=fwd_recv.at[a, k],
                device_id=sib, device_id_type=MESH)

        sends = [ici(a, k, me, (*chips[k], c)) for a in range(n) for k in range(3)]
        for cp in sends:
            cp.start()
        passed = []
        for a in range(n):
            for k in range(3):
                ck = 2 * chips[k][0] + chips[k][1]
                ici(a, k, ck, (*chips[k], c)).wait_recv()
                cp = fwd(a, k, ck, c)
                cp.start()
                passed.append(cp)
        for a in range(n):
            for k in range(3):
                ck = 2 * chips[k][0] + chips[k][1]
                fwd(a, k, ck, 1 - c).wait_recv()
        for cp in sends + passed:
            cp.wait_send()

    out_shape = [jax.ShapeDtypeStruct(b.shape, b.dtype) for b in bufs]
    return pl.pallas_call(
        body, name="gather_weights",
        in_specs=[ANY] * n, out_specs=[ANY] * n, out_shape=out_shape,
        input_output_aliases={a: a for a in range(n)},
        scratch_shapes=[pltpu.SemaphoreType.DMA((n, 3))] * 4,
    )(*bufs)


def _swap_halves(grads, name):
    n = len(grads)

    def body(*refs):
        ins, outs = refs[:n], refs[n:2 * n]
        send_sem, recv_sem = refs[2 * n:]
        x, y, c = _mesh_pos()
        cps = [pltpu.make_async_remote_copy(
            src_ref=ins[a].at[:, pl.ds(1 - c, 1)], dst_ref=outs[a],
            send_sem=send_sem.at[a], recv_sem=recv_sem.at[a],
            device_id=(x, y, 1 - c), device_id_type=MESH) for a in range(n)]
        for cp in cps:
            cp.start()
        for cp in cps:
            cp.wait()

    out_shape = [jax.ShapeDtypeStruct((g.shape[0], 1) + g.shape[2:], g.dtype) for g in grads]
    return pl.pallas_call(
        body, name=name,
        in_specs=[ANY] * n, out_specs=[ANY] * n, out_shape=out_shape,
        scratch_shapes=[pltpu.SemaphoreType.DMA((n,))] * 2,
    )(*grads)


def _add_half(g, got, pos, name):
    j, _, h, c = g.shape

    def body(pos_ref, g_ref, r_ref, o_ref, p_ref):
        val = (g_ref[0, 0] + r_ref[0, 0]).astype(BF16)
        o_ref[0] = val
        if j == 1:
            p_ref[0] = val
        else:
            @pl.when(pl.program_id(0) == pos_ref[0])
            def _():
                p_ref[0] = val

    return pl.pallas_call(
        body, name=name,
        grid_spec=pltpu.PrefetchScalarGridSpec(
            num_scalar_prefetch=1, grid=(j,),
            in_specs=[pl.BlockSpec((1, 1, h, c), lambda i, p: (i, p[1], 0, 0)),
                      pl.BlockSpec((1, 1, h, c), lambda i, p: (i, 0, 0, 0))],
            out_specs=[pl.BlockSpec((1, h, c), lambda i, p: (i, 0, 0)),
                       pl.BlockSpec((1, h, c), lambda i, p: (p[0], 0, 0))]),
        out_shape=[jax.ShapeDtypeStruct((j, h, c), BF16), jax.ShapeDtypeStruct((N_CHIPS, h, c), BF16)],
        compiler_params=_params(("arbitrary",)),
    )(pos, g, got)


def _exchange_chip_sums(sums, parts):
    n = len(sums)

    def body(*refs):
        ins, outs = refs[:n], refs[2 * n:3 * n]
        send_sem, recv_sem = refs[3 * n:]
        sends, arrivals = _exchange_descriptors(ins, outs, lambda a, k: send_sem.at[a, k],
                                                lambda a, k: recv_sem.at[a, k])
        for cp in sends:
            cp().start()
        for cp in arrivals:
            cp().wait_recv()
        for cp in sends:
            cp().wait_send()

    out_shape = [jax.ShapeDtypeStruct(p.shape, p.dtype) for p in parts]
    return pl.pallas_call(
        body, name="exchange_chip_sums",
        in_specs=[ANY] * (2 * n), out_specs=[ANY] * n, out_shape=out_shape,
        input_output_aliases={n + a: a for a in range(n)},
        scratch_shapes=[pltpu.SemaphoreType.DMA((n, 3))] * 2,
    )(*sums, *parts)


def _exchange_descriptors(sums, parts, send_of, recv_of):
    x, y, c = _mesh_pos()
    me = 2 * x + y
    chips = _other_chips(x, y)
    sends, arrivals = [], []
    for a in range(len(sums)):
        for k in range(3):
            ck = 2 * chips[k][0] + chips[k][1]
            mine = sums[a].at[ck] if sums[a].shape[0] == N_CHIPS else sums[a].at[0]

            def copy(dst_slot, a=a, k=k, mine=mine):
                return pltpu.make_async_remote_copy(
                    src_ref=mine, dst_ref=parts[a].at[dst_slot],
                    send_sem=send_of(a, k), recv_sem=recv_of(a, k),
                    device_id=(*chips[k], c), device_id_type=MESH)

            sends.append(functools.partial(copy, me))
            arrivals.append(functools.partial(copy, ck))
    return sends, arrivals


def _exchange_start(sums, parts, name):
    n = len(sums)
    ns = 3 * n

    def body(*refs):
        sems = refs[2 * n:2 * n + 2 * ns]
        sums_thru = refs[2 * n + 2 * ns:3 * n + 2 * ns]
        parts_thru = refs[3 * n + 2 * ns:4 * n + 2 * ns]
        token = refs[4 * n + 2 * ns]
        sends, _ = _exchange_descriptors(sums_thru, parts_thru, lambda a, k: sems[3 * a + k],
                                         lambda a, k: sems[ns + 3 * a + k])
        for cp in sends:
            cp().start()
        token[...] = jnp.zeros_like(token)

    hbm = lambda a: pltpu.HBM(a.shape, a.dtype)
    held = [pltpu.with_memory_space_constraint(a, pltpu.HBM) for a in (*sums, *parts)]
    out = pl.pallas_call(
        body, name=name,
        out_shape=(*[pltpu.SemaphoreType.DMA(())] * (2 * ns), *[hbm(a) for a in held],
                   jax.ShapeDtypeStruct((8, LANES), F32)),
        in_specs=[HBM_SPEC] * (2 * n),
        out_specs=(*[SEM_SPEC] * (2 * ns), *[HBM_SPEC] * (2 * n), pl.BlockSpec(memory_space=pltpu.VMEM)),
        input_output_aliases={i: 2 * ns + i for i in range(2 * n)},
        compiler_params=pltpu.CompilerParams(has_side_effects=pltpu.SideEffectType.DATAFLOW_SIDE_EFFECTING),
    )(*held)
    return (list(out[:2 * ns]), list(out[2 * ns:2 * ns + n]), list(out[2 * ns + n:2 * ns + 2 * n]),
            out[2 * ns + 2 * n])


def _exchange_wait(sems, sums, parts, after, name):
    n = len(sums)
    ns = 3 * n

    def body(*refs):
        sums_ref, parts_ref = refs[:n], refs[n:2 * n]
        sem_ref = refs[2 * n:2 * n + 2 * ns]
        sends, arrivals = _exchange_descriptors(sums_ref, parts_ref, lambda a, k: sem_ref[3 * a + k],
                                                lambda a, k: sem_ref[ns + 3 * a + k])
        for cp in sends:
            cp().wait_send()
        for cp in arrivals:
            cp().wait_recv()

    hbm = lambda a: pltpu.HBM(a.shape, a.dtype)
    out = pl.pallas_call(
        body, name=name,
        out_shape=tuple(hbm(a) for a in (*sums, *parts)),
        in_specs=[HBM_SPEC] * (2 * n) + [SEM_SPEC] * (2 * ns) + [ANY],
        out_specs=tuple([HBM_SPEC] * (2 * n)),
        input_output_aliases={i: i for i in range(2 * n)},
        compiler_params=pltpu.CompilerParams(has_side_effects=pltpu.SideEffectType.DATAFLOW_SIDE_EFFECTING),
    )(*sums, *parts, *sems, after)
    return list(out[n:])


def _after(value, token):
    return lax.optimization_barrier((value, token))[0]


def _sum_chips(parts, pos, name):
    _, h, c = parts.shape

    def body(pos_ref, p_ref, o_ref):
        o_ref[0] = ((p_ref[0].astype(F32) + p_ref[1].astype(F32)) + p_ref[2].astype(F32)) + p_ref[3].astype(F32)

    return pl.pallas_call(
        body, name=name,
        grid_spec=pltpu.PrefetchScalarGridSpec(
            num_scalar_prefetch=1, grid=(1,),
            in_specs=[pl.BlockSpec((N_CHIPS, h, c), lambda i, p: (0, 0, 0))],
            out_specs=pl.BlockSpec((1, h, c), lambda i, p: (p[1], 0, 0))),
        out_shape=jax.ShapeDtypeStruct((2, h, c), F32),
        compiler_params=_params(("arbitrary",)),
    )(pos, parts)


def _join_halves(fulls):
    n = len(fulls)

    def body(*refs):
        outs = refs[n:2 * n]
        send_sem, recv_sem = refs[2 * n:]
        x, y, c = _mesh_pos()

        def half(a, which):
            return pltpu.make_async_remote_copy(
                src_ref=outs[a].at[which], dst_ref=outs[a].at[which],
                send_sem=send_sem.at[a], recv_sem=recv_sem.at[a],
                device_id=(x, y, 1 - c), device_id_type=MESH)

        sends = [half(a, c) for a in range(n)]
        for cp in sends:
            cp.start()
        for a in range(n):
            half(a, 1 - c).wait_recv()
        for cp in sends:
            cp.wait_send()

    out_shape = [jax.ShapeDtypeStruct(f.shape, f.dtype) for f in fulls]
    return pl.pallas_call(
        body, name="join_halves",
        in_specs=[ANY] * n, out_specs=[ANY] * n, out_shape=out_shape,
        input_output_aliases={a: a for a in range(n)},
        scratch_shapes=[pltpu.SemaphoreType.DMA((n,))] * 2,
    )(*fulls)


def _mix_in(x, g, w_in, b_in, ts):
    s = x.shape[0]

    def body(x_ref, g_ref, w_ref, b_ref, z_ref, hn_ref):
        xv = x_ref[...]
        hn = (xv * _rms_stats(xv) * g_ref[...]).astype(BF16)
        hn_ref[...] = hn
        for j in range(4):
            cols = slice(j * 512, (j + 1) * 512)
            z_ref[:, cols] = _dot(hn, w_ref[j]) + b_ref[:, cols]

    return pl.pallas_call(
        body, name="mix_in", grid=(s // ts,),
        in_specs=[pl.BlockSpec((ts, D_MODEL), lambda i: (i, 0)),
                  pl.BlockSpec((1, D_MODEL), lambda i: (0, 0)),
                  pl.BlockSpec((4, D_MODEL, 512), lambda i: (0, 0, 0)),
                  pl.BlockSpec((1, 2048), lambda i: (0, 0))],
        out_specs=[pl.BlockSpec((ts, 2048), lambda i: (i, 0)),
                   pl.BlockSpec((ts, D_MODEL), lambda i: (i, 0))],
        out_shape=[jax.ShapeDtypeStruct((s, 2048), F32), jax.ShapeDtypeStruct((s, D_MODEL), BF16)],
        compiler_params=_params(("parallel",)),
    )(x, g, w_in, b_in)


def _lane_is_low_head():
    lane = lax.broadcasted_iota(jnp.int32, (1, GM_WIDTH), 1)
    return (lane & GM_HEAD_DIM) == 0


def _gm_mix(v_lo, v_hi, wpair_ref, bias_ref, mixed_ref, t):
    for n in range(t // CHUNK):
        rows = slice(n * CHUNK, (n + 1) * CHUNK)
        for j in range(GM_HEADS // 2):
            cols = slice(j * LANES, (j + 1) * LANES)
            rhs = jnp.concatenate([v_lo[rows, cols], v_hi[rows, cols]], axis=0)
            mixed_ref[rows, cols] = _dot(wpair_ref[j], rhs) + bias_ref[:, cols]


def _seqmix_fwd(z, cw, cb, lng, lnb, gg, gb, wpair, bias, t):
    s = z.shape[0]

    def body(z_ref, cw_ref, cb_ref, lng_ref, lnb_ref, gg_ref, gb_ref, wpair_ref, bias_ref,
             mix_ref, c1_ref, abuf, mixed_ref):
        i = pl.program_id(0)

        @pl.when(i == 0)
        def _():
            abuf[0:CONV_HALO, :] = jnp.zeros((CONV_HALO, CONV_WIDTH), F32)

        @pl.when(i > 0)
        def _():
            abuf[0:CONV_HALO, :] = abuf[t:t + CONV_HALO, :]

        abuf[CONV_HALO:, :] = z_ref[:, 0:512] * _sigmoid(z_ref[:, 512:1024])
        acc = jnp.zeros((t, CONV_WIDTH), F32)
        for k in range(CONV_KERNEL):
            acc = acc + cw_ref[k:k + 1, :] * abuf[pl.ds(CONV_HALO - (CONV_KERNEL - 1) + k, t), :]
        c1 = acc + cb_ref[...]
        c1_ref[...] = c1
        xh, _ = _ln_stats(c1)
        ln = xh * lng_ref[...] + lnb_ref[...]
        mix_ref[:, 0:512] = (ln * _sigmoid(ln)).astype(BF16)

        u, _ = _gelu_parts(z_ref[:, 1024:1536])
        gv, _ = _gelu_parts(z_ref[:, 1536:2048])
        vxh, _ = _ln_stats(gv)
        v = vxh * gg_ref[...] + gb_ref[...]
        low = _lane_is_low_head()
        v_lo = jnp.where(low, v, 0.0).astype(BF16)
        v_hi = jnp.where(low, 0.0, v).astype(BF16)
        _gm_mix(v_lo, v_hi, wpair_ref, bias_ref, mixed_ref, t)
        mix_ref[:, 512:1024] = (u * mixed_ref[...]).astype(BF16)

    vec = lambda n: pl.BlockSpec((1, n), lambda i: (0, 0))
    return pl.pallas_call(
        body, name="seqmix_fwd", grid=(s // t,),
        in_specs=[pl.BlockSpec((t, 2048), lambda i: (i, 0)),
                  pl.BlockSpec((CONV_HALO, CONV_WIDTH), lambda i: (0, 0)),
                  vec(512), vec(512), vec(512), vec(512), vec(512),
                  pl.BlockSpec((4, CHUNK, 2 * CHUNK), lambda i: (0, 0, 0)),
                  pl.BlockSpec((CHUNK, GM_WIDTH), lambda i: (0, 0))],
        out_specs=[pl.BlockSpec((t, D_MODEL), lambda i: (i, 0)),
                   pl.BlockSpec((t, CONV_WIDTH), lambda i: (i, 0))],
        out_shape=[jax.ShapeDtypeStruct((s, D_MODEL), BF16), jax.ShapeDtypeStruct((s, CONV_WIDTH), F32)],
        scratch_shapes=[pltpu.VMEM((t + CONV_HALO, CONV_WIDTH), F32), pltpu.VMEM((t, GM_WIDTH), F32)],
        compiler_params=_params(("arbitrary",)),
    )(z, cw, cb, lng, lnb, gg, gb, wpair, bias)


def _out_proj_q(x, mix, w_out, g, wq, ts):
    s = x.shape[0]

    def body(x_ref, mix_ref, wo_ref, g_ref, wq_ref, h1_ref, hn_ref, q_ref):
        h1 = x_ref[...] + _dot(mix_ref[...], wo_ref[...])
        h1_ref[...] = h1
        hn = (h1 * _rms_stats(h1) * g_ref[...]).astype(BF16)
        hn_ref[...] = hn
        q_ref[...] = _dot(hn, wq_ref[...]).astype(BF16)

    row = lambda dt: pl.BlockSpec((ts, D_MODEL), lambda i: (i, 0))
    full = pl.BlockSpec((D_MODEL, D_MODEL), lambda i: (0, 0))
    return pl.pallas_call(
        body, name="out_proj_q", grid=(s // ts,),
        in_specs=[row(F32), row(BF16), full, pl.BlockSpec((1, D_MODEL), lambda i: (0, 0)), full],
        out_specs=[row(F32), row(BF16), row(BF16)],
        out_shape=[jax.ShapeDtypeStruct((s, D_MODEL), F32), jax.ShapeDtypeStruct((s, D_MODEL), BF16),
                   jax.ShapeDtypeStruct((s, D_MODEL), BF16)],
        compiler_params=_params(("parallel",)),
    )(x, mix, w_out, g, wq)


def _mem_kv(mem, g, wkv):
    m = mem.shape[0]

    def body(mem_ref, g_ref, w_ref, mn_ref, kv_ref):
        mv = mem_ref[...]
        mn = (mv * _rms_stats(mv) * g_ref[...]).astype(BF16)
        mn_ref[...] = mn
        for j in range(4):
            kv_ref[:, j * 512:(j + 1) * 512] = _dot(mn, w_ref[j]).astype(BF16)

    return pl.pallas_call(
        body, name="mem_kv",
        out_shape=[jax.ShapeDtypeStruct((m, D_MODEL), BF16), jax.ShapeDtypeStruct((m, 2 * D_MODEL), BF16)],
        compiler_params=pltpu.CompilerParams(vmem_limit_bytes=VMEM_LIMIT_BYTES),
    )(mem, g, wkv)


def _softmax_rows(sc):
    e = jnp.exp(sc - jnp.max(sc, axis=-1, keepdims=True))
    return e / jnp.sum(e, axis=-1, keepdims=True)


def _attn_fwd(q, kv, h1, wo, g, ts):
    s, m = q.shape[0], kv.shape[0]
    scale = XA_HEAD_DIM ** -0.5

    def body(q_ref, kv_ref, h1_ref, wo_ref, g_ref, o_ref, h2_ref, hn_ref):
        for h in range(XA_HEADS):
            cols = slice(h * XA_HEAD_DIM, (h + 1) * XA_HEAD_DIM)
            vcols = slice(D_MODEL + h * XA_HEAD_DIM, D_MODEL + (h + 1) * XA_HEAD_DIM)
            p = _softmax_rows(_dot_nt(q_ref[:, cols], kv_ref[:, cols]) * scale)
            o_ref[:, cols] = _dot(p.astype(BF16), kv_ref[:, vcols]).astype(BF16)
        h2 = h1_ref[...] + _dot(o_ref[...], wo_ref[...])
        h2_ref[...] = h2
        hn_ref[...] = (h2 * _rms_stats(h2) * g_ref[...]).astype(BF16)

    row = pl.BlockSpec((ts, D_MODEL), lambda i: (i, 0))
    return pl.pallas_call(
        body, name="attn_fwd", grid=(s // ts,),
        in_specs=[row, pl.BlockSpec((m, 2 * D_MODEL), lambda i: (0, 0)), row,
                  pl.BlockSpec((D_MODEL, D_MODEL), lambda i: (0, 0)),
                  pl.BlockSpec((1, D_MODEL), lambda i: (0, 0))],
        out_specs=[row, row, row],
        out_shape=[jax.ShapeDtypeStruct((s, D_MODEL), BF16), jax.ShapeDtypeStruct((s, D_MODEL), F32),
                   jax.ShapeDtypeStruct((s, D_MODEL), BF16)],
        compiler_params=_params(("parallel",)),
    )(q, kv, h1, wo, g)


def _ffn_up(hn, wgu, ts):
    s = hn.shape[0]

    def body(hn_ref, w_ref, gu_ref, act_ref):
        hv = hn_ref[...]
        gate = _dot(hv, w_ref[0, 0])
        up = _dot(hv, w_ref[1, 0])
        gu_ref[0] = gate
        gu_ref[1] = up
        act_ref[...] = (gate * _sigmoid(gate) * up).astype(BF16)

    return pl.pallas_call(
        body, name="ffn_up", grid=(2, s // ts),
        in_specs=[pl.BlockSpec((ts, D_MODEL), lambda j, i: (i, 0)),
                  pl.BlockSpec((2, 1, D_MODEL, FFN_HALF), lambda j, i: (0, j, 0, 0))],
        out_specs=[pl.BlockSpec((2, ts, FFN_HALF), lambda j, i: (0, i, j)),
                   pl.BlockSpec((ts, FFN_HALF), lambda j, i: (i, j))],
        out_shape=[jax.ShapeDtypeStruct((2, s, FFN_HIDDEN), F32), jax.ShapeDtypeStruct((s, FFN_HIDDEN), BF16)],
        compiler_params=_params(("parallel", "parallel")),
    )(hn, wgu)


def _ffn_down_loss(act, wd, h2, g, target, ts):
    s = act.shape[0]

    def body(act_ref, wd_ref, h2_ref, g_ref, t_ref, dh_ref, sq_ref, dg_ref):
        @pl.when(pl.program_id(0) == 0)
        def _():
            sq_ref[...] = jnp.zeros_like(sq_ref)
            dg_ref[...] = jnp.zeros_like(dg_ref)

        h3 = h2_ref[...] + _dot(act_ref[...], wd_ref[...])
        r = _rms_stats(h3)
        gv = g_ref[...]
        diff = h3 * r * gv - t_ref[...]
        sq_ref[...] += _rowsum(diff * diff)
        dh, dg = _rms_bwd(diff / D_MODEL, h3, r, gv)
        dh_ref[...] = dh
        dg_ref[...] += dg

    row = pl.BlockSpec((ts, D_MODEL), lambda i: (i, 0))
    vec = pl.BlockSpec((1, D_MODEL), lambda i: (0, 0))
    return pl.pallas_call(
        body, name="ffn_down_loss", grid=(s // ts,),
        in_specs=[pl.BlockSpec((ts, FFN_HIDDEN), lambda i: (i, 0)),
                  pl.BlockSpec((FFN_HIDDEN, D_MODEL), lambda i: (0, 0)), row, vec, row],
        out_specs=[row, vec, vec],
        out_shape=[jax.ShapeDtypeStruct((s, D_MODEL), F32), jax.ShapeDtypeStruct((1, D_MODEL), F32),
                   jax.ShapeDtypeStruct((1, D_MODEL), F32)],
        compiler_params=_params(("arbitrary",)),
    )(act, wd, h2, g, target)


def _grad_w(a, b, tk, tn, name):
    s, k = a.shape
    gb, _, n = b.shape
    nblk = n // tn
    tsr = 512 if s % 512 == 0 else s

    def body(a_ref, b_ref, o_ref):
        @pl.when(pl.program_id(2) == 0)
        def _():
            o_ref[...] = jnp.zeros_like(o_ref)

        o_ref[0] += _dot_tn(a_ref[...].astype(BF16), b_ref[0].astype(BF16))

    return pl.pallas_call(
        body, name=name, grid=(gb * nblk, k // tk, s // tsr),
        in_specs=[pl.BlockSpec((tsr, tk), lambda ni, ki, si: (si, ki)),
                  pl.BlockSpec((1, tsr, tn), lambda ni, ki, si: (ni // nblk, si, ni % nblk))],
        out_specs=pl.BlockSpec((1, tk, tn), lambda ni, ki, si: (ni, ki, 0)),
        out_shape=jax.ShapeDtypeStruct((gb * nblk, k, tn), F32),
        compiler_params=_params(("parallel", "parallel", "arbitrary")),
    )(a, b)


def _ffn_bwd_act(dh3, wd, gu, ts):
    s = dh3.shape[0]

    def body(dh_ref, wd_ref, gu_ref, dgu_ref):
        dact = _dot_nt(dh_ref[...].astype(BF16), wd_ref[0])
        gate, up = gu_ref[0], gu_ref[1]
        sg = _sigmoid(gate)
        dgu_ref[0] = (dact * up * (sg * (1.0 + gate * (1.0 - sg)))).astype(BF16)
        dgu_ref[1] = (dact * (gate * sg)).astype(BF16)

    return pl.pallas_call(
        body, name="ffn_bwd_act", grid=(2, s // ts),
        in_specs=[pl.BlockSpec((ts, D_MODEL), lambda j, i: (i, 0)),
                  pl.BlockSpec((1, FFN_HALF, D_MODEL), lambda j, i: (j, 0, 0)),
                  pl.BlockSpec((2, ts, FFN_HALF), lambda j, i: (0, i, j))],
        out_specs=pl.BlockSpec((2, ts, FFN_HALF), lambda j, i: (0, i, j)),
        out_shape=jax.ShapeDtypeStruct((2, s, FFN_HIDDEN), BF16),
        compiler_params=_params(("parallel", "parallel")),
    )(dh3, wd, gu)


def _ffn_bwd_in(dgu, wgu, dh3, h2, g, ts):
    s = dh3.shape[0]

    def body(dgu_ref, w_ref, dh3_ref, h2_ref, g_ref, dh2_ref, dg_ref):
        @pl.when(pl.program_id(0) == 0)
        def _():
            dg_ref[...] = jnp.zeros_like(dg_ref)

        dhn = jnp.zeros((ts, D_MODEL), F32)
        for p in range(2):
            for j in range(2):
                dhn = dhn + _dot_nt(dgu_ref[p, :, j * FFN_HALF:(j + 1) * FFN_HALF], w_ref[2 * p + j])
        h2 = h2_ref[...]
        dv, dg = _rms_bwd(dhn, h2, _rms_stats(h2), g_ref[...])
        dh2_ref[...] = dh3_ref[...] + dv
        dg_ref[...] += dg

    row = pl.BlockSpec((ts, D_MODEL), lambda i: (i, 0))
    vec = pl.BlockSpec((1, D_MODEL), lambda i: (0, 0))
    return pl.pallas_call(
        body, name="ffn_bwd_in", grid=(s // ts,),
        in_specs=[pl.BlockSpec((2, ts, FFN_HIDDEN), lambda i: (0, i, 0)),
                  pl.BlockSpec((4, D_MODEL, FFN_HALF), lambda i: (0, 0, 0)), row, row, vec],
        out_specs=[row, vec],
        out_shape=[jax.ShapeDtypeStruct((s, D_MODEL), F32), jax.ShapeDtypeStruct((1, D_MODEL), F32)],
        compiler_params=_params(("arbitrary",)),
    )(dgu, wgu, dh3, h2, g)


def _attn_bwd(dh2, wo, q, kv, wq, h1, g, ts):
    s, m = q.shape[0], kv.shape[0]
    scale = XA_HEAD_DIM ** -0.5

    def body(dh2_ref, wo_ref, q_ref, kv_ref, wq_ref, h1_ref, g_ref, dh1_ref, dq_ref, dkv_ref, dg_ref):
        @pl.when(pl.program_id(0) == 0)
        def _():
            dkv_ref[...] = jnp.zeros_like(dkv_ref)
            dg_ref[...] = jnp.zeros_like(dg_ref)

        do = _dot_nt(dh2_ref[...].astype(BF16), wo_ref[...]).astype(BF16)
        for h in range(XA_HEADS):
            cols = slice(h * XA_HEAD_DIM, (h + 1) * XA_HEAD_DIM)
            vcols = slice(D_MODEL + h * XA_HEAD_DIM, D_MODEL + (h + 1) * XA_HEAD_DIM)
            qh, kh, vh, doh = q_ref[:, cols], kv_ref[:, cols], kv_ref[:, vcols], do[:, cols]
            p = _softmax_rows(_dot_nt(qh, kh) * scale)
            dp = _dot_nt(doh, vh)
            ds = (p * (dp - jnp.sum(dp * p, axis=-1, keepdims=True)) * scale).astype(BF16)
            dq_ref[:, cols] = _dot(ds, kh).astype(BF16)
            dkv_ref[:, cols] += _dot_tn(ds, qh)
            dkv_ref[:, vcols] += _dot_tn(p.astype(BF16), doh)
        dhn = _dot_nt(dq_ref[...], wq_ref[...])
        h1 = h1_ref[...]
        dv, dg = _rms_bwd(dhn, h1, _rms_stats(h1), g_ref[...])
        dh1_ref[...] = dh2_ref[...] + dv
        dg_ref[...] += dg

    row = pl.BlockSpec((ts, D_MODEL), lambda i: (i, 0))
    full = pl.BlockSpec((D_MODEL, D_MODEL), lambda i: (0, 0))
    kvs = pl.BlockSpec((m, 2 * D_MODEL), lambda i: (0, 0))
    vec = pl.BlockSpec((1, D_MODEL), lambda i: (0, 0))
    return pl.pallas_call(
        body, name="attn_bwd", grid=(s // ts,),
        in_specs=[row, full, row, kvs, full, row, vec],
        out_specs=[row, row, kvs, vec],
        out_shape=[jax.ShapeDtypeStruct((s, D_MODEL), F32), jax.ShapeDtypeStruct((s, D_MODEL), BF16),
                   jax.ShapeDtypeStruct((m, 2 * D_MODEL), F32), jax.ShapeDtypeStruct((1, D_MODEL), F32)],
        compiler_params=_params(("arbitrary",)),
    )(dh2, wo, q, kv, wq, h1, g)


def _mem_kv_bwd(dkv, mn, wkv, mem, g):
    m = mem.shape[0]

    def body(dkv_ref, mn_ref, w_ref, mem_ref, g_ref, dw_ref, dg_ref):
        dmn = jnp.zeros((m, D_MODEL), F32)
        mn = mn_ref[...]
        for j in range(4):
            dj = dkv_ref[:, j * 512:(j + 1) * 512].astype(BF16)
            dw_ref[j] = _dot_tn(mn, dj)
            dmn = dmn + _dot_nt(dj, w_ref[j])
        mv = mem_ref[...]
        dg_ref[...] = _rowsum(dmn * (mv * _rms_stats(mv)))

    return pl.pallas_call(
        body, name="mem_kv_bwd",
        out_shape=[jax.ShapeDtypeStruct((4, D_MODEL, 512), F32), jax.ShapeDtypeStruct((1, D_MODEL), F32)],
        compiler_params=pltpu.CompilerParams(vmem_limit_bytes=VMEM_LIMIT_BYTES),
    )(dkv, mn, wkv, mem, g)


def _seqmix_bwd(dh1, x, z, c1, w_out, w_in, g_mix, cw, lng, lnb, gg, gb, wpair, wpair_t, bias, t):
    s = x.shape[0]
    nt = s // t
    halo_blocks = t // CONV_HALO

    def body(dh1_ref, x_ref, z_ref, zh_ref, c1_ref, wo_ref, wi_ref, gm_ref, cw_ref, lng_ref, lnb_ref,
             gg_ref, gb_ref, wpair_ref, wpt_ref, bias_ref,
             gx_ref, dz_ref, dcw_ref, dcb_ref, dlng_ref, dlnb_ref, dgg_ref, dgb_ref, dws_ref, dbs_ref,
             dbin_ref, dgm_ref, abuf, dbuf, mixed_ref, dv_ref):
        i = pl.program_id(0)
        tile = nt - 1 - i
        accs = (dcw_ref, dcb_ref, dlng_ref, dlnb_ref, dgg_ref, dgb_ref, dws_ref, dbs_ref, dbin_ref, dgm_ref)

        @pl.when(i == 0)
        def _():
            for r in accs:
                r[...] = jnp.zeros_like(r)
            dbuf[t:t + CONV_HALO, :] = jnp.zeros((CONV_HALO, CONV_WIDTH), F32)

        @pl.when(i > 0)
        def _():
            dbuf[t:t + CONV_HALO, :] = dbuf[0:CONV_HALO, :]

        dmix = _dot_nt(dh1_ref[...].astype(BF16), wo_ref[...])

        xh, rs = _ln_stats(c1_ref[...])
        lng = lng_ref[...]
        ln = xh * lng + lnb_ref[...]
        sl = _sigmoid(ln)
        dln = dmix[:, 0:512] * (sl * (1.0 + ln * (1.0 - sl)))
        dc1, dg_ln, db_ln = _ln_bwd(dln, xh, rs, lng)
        dlng_ref[...] += dg_ln
        dlnb_ref[...] += db_ln
        dcb_ref[...] += _rowsum(dc1)
        dbuf[0:t, :] = dc1

        zh = zh_ref[...]
        a_halo = zh[:, 0:512] * _sigmoid(zh[:, 512:1024])
        abuf[0:CONV_HALO, :] = jnp.where(tile > 0, a_halo, 0.0)
        za = z_ref[:, 0:512]
        sg = _sigmoid(z_ref[:, 512:1024])
        abuf[CONV_HALO:, :] = za * sg

        da = jnp.zeros((t, CONV_WIDTH), F32)
        for k in range(CONV_KERNEL):
            da = da + cw_ref[k:k + 1, :] * dbuf[pl.ds(CONV_KERNEL - 1 - k, t), :]
            dcw_ref[k:k + 1, :] += _rowsum(dc1 * abuf[pl.ds(CONV_HALO - (CONV_KERNEL - 1) + k, t), :])
        dza = da * sg
        dzg = da * za * (sg * (1.0 - sg))
        dz_ref[:, 0:512] = dza.astype(BF16)
        dz_ref[:, 512:1024] = dzg.astype(BF16)
        dbin_ref[:, 0:512] += _rowsum(dza)
        dbin_ref[:, 512:1024] += _rowsum(dzg)

        dgm = dmix[:, 512:1024]
        u, du_dz = _gelu_parts(z_ref[:, 1024:1536])
        gv, dgv_dz = _gelu_parts(z_ref[:, 1536:2048])
        vxh, vrs = _ln_stats(gv)
        ggv = gg_ref[...]
        v = vxh * ggv + gb_ref[...]
        low = _lane_is_low_head()
        v_lo = jnp.where(low, v, 0.0).astype(BF16)
        v_hi = jnp.where(low, 0.0, v).astype(BF16)
        _gm_mix(v_lo, v_hi, wpair_ref, bias_ref, mixed_ref, t)
        dzu = dgm * mixed_ref[...] * du_dz
        dm = dgm * u
        dm_lo = jnp.where(low, dm, 0.0).astype(BF16)
        dm_hi = jnp.where(low, 0.0, dm).astype(BF16)
        vb = v.astype(BF16)
        tril = (lax.broadcasted_iota(jnp.int32, (CHUNK, CHUNK), 1)
                <= lax.broadcasted_iota(jnp.int32, (CHUNK, CHUNK), 0))
        for n in range(t // CHUNK):
            rows = slice(n * CHUNK, (n + 1) * CHUNK)
            dbs_ref[...] += dm[rows, :]
            for j in range(GM_HEADS // 2):
                cols = slice(j * LANES, (j + 1) * LANES)
                stack = jnp.concatenate([dm_lo[rows, cols], dm_hi[rows, cols]], axis=0)
                dws = _dot_nt(stack, vb[rows, cols])
                dws_ref[2 * j] += jnp.where(tril, dws[0:CHUNK], 0.0)
                dws_ref[2 * j + 1] += jnp.where(tril, dws[CHUNK:2 * CHUNK], 0.0)
                dv_ref[rows, cols] = _dot(wpt_ref[j], stack)
        dgv, dg_gm, db_gm = _ln_bwd(dv_ref[...], vxh, vrs, ggv)
        dgg_ref[...] += dg_gm
        dgb_ref[...] += db_gm
        dzv = dgv * dgv_dz
        dz_ref[:, 1024:1536] = dzu.astype(BF16)
        dz_ref[:, 1536:2048] = dzv.astype(BF16)
        dbin_ref[:, 1024:1536] += _rowsum(dzu)
        dbin_ref[:, 1536:2048] += _rowsum(dzv)

        dhn = jnp.zeros((t, D_MODEL), F32)
        for j in range(4):
            dhn = dhn + _dot_nt(dz_ref[:, j * 512:(j + 1) * 512], wi_ref[j])
        xv = x_ref[...]
        dv, dg = _rms_bwd(dhn, xv, _rms_stats(xv), gm_ref[...])
        gx_ref[...] = dh1_ref[...] + dv
        dgm_ref[...] += dg

    rev = lambda w: pl.BlockSpec((t, w), lambda i: (nt - 1 - i, 0))
    const = lambda *shape: pl.BlockSpec(shape, lambda i: (0,) * len(shape))
    halo = pl.BlockSpec((CONV_HALO, D_MODEL), lambda i: (jnp.maximum((nt - 1 - i) * halo_blocks - 1, 0), 0))
    f32 = lambda *shape: jax.ShapeDtypeStruct(shape, F32)
    return pl.pallas_call(
        body, name="seqmix_bwd", grid=(nt,),
        in_specs=[rev(D_MODEL), rev(D_MODEL), rev(2048), halo, rev(CONV_WIDTH),
                  const(D_MODEL, D_MODEL), const(4, D_MODEL, 512), const(1, D_MODEL),
                  const(CONV_HALO, CONV_WIDTH), const(1, 512), const(1, 512), const(1, 512), const(1, 512),
                  const(4, CHUNK, 2 * CHUNK), const(4, CHUNK, 2 * CHUNK), const(CHUNK, GM_WIDTH)],
        out_specs=[rev(D_MODEL), rev(2048),
                   const(CONV_HALO, CONV_WIDTH), const(1, 512), const(1, 512), const(1, 512), const(1, 512),
                   const(1, 512), const(GM_HEADS, CHUNK, CHUNK), const(CHUNK, GM_WIDTH), const(1, 2048),
                   const(1, D_MODEL)],
        out_shape=[f32(s, D_MODEL), jax.ShapeDtypeStruct((s, 2048), BF16),
                   f32(CONV_HALO, CONV_WIDTH), f32(1, 512), f32(1, 512), f32(1, 512), f32(1, 512),
                   f32(1, 512), f32(GM_HEADS, CHUNK, CHUNK), f32(CHUNK, GM_WIDTH), f32(1, 2048),
                   f32(1, D_MODEL)],
        scratch_shapes=[pltpu.VMEM((t + CONV_HALO, CONV_WIDTH), F32), pltpu.VMEM((t + CONV_HALO, CONV_WIDTH), F32),
                        pltpu.VMEM((t, GM_WIDTH), F32), pltpu.VMEM((t, GM_WIDTH), F32)],
        compiler_params=_params(("arbitrary",)),
    )(dh1, x, z, z, c1, w_out, w_in, g_mix, cw, lng, lnb, gg, gb, wpair, wpair_t, bias)


def _head_bias_grad(dbs):
    def body(d_ref, o_ref):
        dv = d_ref[...]
        lane = lax.broadcasted_iota(jnp.int32, (CHUNK, LANES), 1)
        acc = jnp.zeros((CHUNK, LANES), F32)
        for h in range(GM_HEADS):
            sh = jnp.sum(dv[:, h * GM_HEAD_DIM:(h + 1) * GM_HEAD_DIM], axis=-1, keepdims=True)
            acc = acc + jnp.where(lane == h, sh, 0.0)
        o_ref[...] = acc

    return pl.pallas_call(body, name="head_bias_grad",
                          out_shape=jax.ShapeDtypeStruct((CHUNK, LANES), F32))(dbs)


def _pack(arrays, rows):
    flat = jnp.concatenate([a.reshape(-1) for a in arrays])
    flat = jnp.pad(flat, (0, rows * LANES - flat.shape[0]))
    return flat.reshape(rows, LANES)


def _unpack(buf, shapes):
    flat = buf.reshape(-1)
    out, off = [], 0
    for shp in shapes:
        size = 1
        for d in shp:
            size *= d
        out.append(flat[off:off + size].reshape(shp))
        off += size
    return out


def _rows_for(shapes, multiple):
    total = 0
    for shp in shapes:
        size = 1
        for d in shp:
            size *= d
        total += size
    rows = -(-total // LANES)
    return -(-rows // multiple) * multiple


def kernel(x, mem, norm_mix_g, w_in, b_in, conv_w, conv_b, conv_ln_g, conv_ln_b, gm_ln_g, gm_ln_b, gm_w_s, gm_b_s, w_out, norm_xa_g, mem_norm_g, xa_wq, xa_wkv, xa_wo, norm_ffn_g, ffn_w_gate_up, ffn_w_down, final_norm_g, loss_target, m_norm_mix_g, m_w_in, m_b_in, m_conv_w, m_conv_b, m_conv_ln_g, m_conv_ln_b, m_gm_ln_g, m_gm_ln_b, m_gm_w_s, m_gm_b_s, m_w_out, m_norm_xa_g, m_mem_norm_g, m_xa_wq, m_xa_wkv, m_xa_wo, m_norm_ffn_g, m_ffn_w_gate_up, m_ffn_w_down, m_final_norm_g, v_norm_mix_g, v_w_in, v_b_in, v_conv_w, v_conv_b, v_conv_ln_g, v_conv_ln_b, v_gm_ln_g, v_gm_ln_b, v_gm_w_s, v_gm_b_s, v_w_out, v_norm_xa_g, v_mem_norm_g, v_xa_wq, v_xa_wkv, v_xa_wo, v_norm_ffn_g, v_ffn_w_gate_up, v_ffn_w_down, v_final_norm_g):
    s = x.shape[1]
    ts = _row_tile(s)
    tb = max(CHUNK, ts // 2)
    cx, cy, cc = _mesh_pos()
    chip = 2 * cx + cy
    pos = jnp.stack([chip, cc]).astype(jnp.int32)
    row = lambda a: a.reshape(1, -1)
    x2, mem2, tgt2 = x[0], mem[0], loss_target[0]

    big = dict(w_in=w_in, xa_wkv=xa_wkv, w_out=w_out, xa_wq=xa_wq, xa_wo=xa_wo,
               ffn_w_gate_up=ffn_w_gate_up, ffn_w_down=ffn_w_down)
    big_names = list(big)
    halves = lambda a: a.reshape(2, a.shape[0] // 2, a.shape[1])
    bufs = [_cast_into_slot(halves(big[nm]), pos, BF16, "cast_" + nm) for nm in big_names]
    conv_w_pad = jnp.pad(conv_w, ((0, CONV_HALO - CONV_KERNEL), (0, 0)))
    gathered = _gather_shards(bufs + [_cast_into_slot(halves(conv_w_pad), pos, F32, "slot_conv_w")])
    gw = {nm: gathered[i].reshape(N_CHIPS, big[nm].shape[0], big[nm].shape[1]) for i, nm in enumerate(big_names)}
    w_in_g = gw["w_in"]
    wkv_g = gw["xa_wkv"]
    w_out_g = gw["w_out"].reshape(D_MODEL, D_MODEL)
    wq_g = gw["xa_wq"].reshape(D_MODEL, D_MODEL)
    wo_g = gw["xa_wo"].reshape(D_MODEL, D_MODEL)
    wgu_g = gw["ffn_w_gate_up"]
    wd_g = gw["ffn_w_down"].reshape(FFN_HIDDEN, D_MODEL)
    cw_g = jnp.concatenate([gathered[-1][k].reshape(CONV_HALO, LANES) for k in range(N_CHIPS)], axis=1)

    tril = jnp.tril(jnp.ones((CHUNK, CHUNK), dtype=bool))
    ws = jnp.where(tril[None], gm_w_s, 0.0)
    wpair = jnp.concatenate([ws[0::2], ws[1::2]], axis=2).astype(BF16)
    ws_t = jnp.swapaxes(ws, 1, 2)
    wpair_t = jnp.concatenate([ws_t[0::2], ws_t[1::2]], axis=2).astype(BF16)
    bias = jnp.repeat(gm_b_s.T, GM_HEAD_DIM, axis=1)

    z, hn1 = _mix_in(x2, row(norm_mix_g), w_in_g, row(b_in), ts)
    mix, c1 = _seqmix_fwd(z, cw_g, row(conv_b), row(conv_ln_g), row(conv_ln_b), row(gm_ln_g), row(gm_ln_b),
                          wpair, bias, ts)
    h1, hn2, q = _out_proj_q(x2, mix, w_out_g, row(norm_xa_g), wq_g, ts)
    mn, kv = _mem_kv(mem2, row(mem_norm_g), wkv_g)
    o, h2, hn3 = _attn_fwd(q, kv, h1, wo_g, row(norm_ffn_g), ts)
    gu, act = _ffn_up(hn3, wgu_g.reshape(2, 2, D_MODEL, FFN_HALF), ts)
    dh3, sq, d_final_g = _ffn_down_loss(act, wd_g, h2, row(final_norm_g), tgt2, ts)
    loss = lax.psum(0.5 * jnp.sum(sq) / D_MODEL, ("x", "y", "c"))

    def split(g, nm):
        r, c = big[nm].shape
        return g.reshape(N_CHIPS, 2, r // 2, c)

    def chip_sums_of(group, arrays):
        got = _swap_halves(arrays, "swap_halves_" + group[0])
        both = [_add_half(g, r, pos, "chip_sum_" + nm) for g, r, nm in zip(arrays, got, group)]
        return [b[0] for b in both], [b[1] for b in both]

    def start_exchange(group, arrays):
        sums, parts = chip_sums_of(group, arrays)
        return _exchange_start(sums, parts, "exchange_start_" + group[0])

    def finish_exchange(group, started, after):
        sems, sums, parts, _ = started
        parts = _exchange_wait(sems, sums, parts, after, "exchange_wait_" + group[0])
        return [_sum_chips(p, pos, "total_" + nm) for p, nm in zip(parts, group)]

    as3 = lambda a: a.reshape((1,) + a.shape)
    halves_of = {}

    dgu = _ffn_bwd_act(dh3, wd_g.reshape(2, FFN_HALF, D_MODEL), gu, ts)
    g_down = _grad_w(act, as3(dh3), FFN_HALF, D_MODEL, "grad_ffn_w_down")
    group_a = ["ffn_w_down"]
    started_a = start_exchange(group_a, [split(g_down, "ffn_w_down")])

    dgu = _after(dgu, started_a[3])
    dh2, d_ffn_g = _ffn_bwd_in(dgu, wgu_g, dh3, h2, row(norm_ffn_g), tb)
    g_gu = _grad_w(hn3, dgu, 512, FFN_HALF, "grad_ffn_w_gate_up")
    halves_of.update(zip(group_a, finish_exchange(group_a, started_a, g_gu)))
    group_b = ["ffn_w_gate_up"]
    started_b = start_exchange(group_b, [split(g_gu, "ffn_w_gate_up")])

    dh2 = _after(dh2, started_b[3])
    dh1, dq, dkv, d_xa_g = _attn_bwd(dh2, wo_g, q, kv, wq_g, h1, row(norm_xa_g), ts)
    g_wkv, d_mem_g = _mem_kv_bwd(dkv, mn, wkv_g, mem2, row(mem_norm_g))
    g_wo = _grad_w(o, as3(dh2), 512, D_MODEL, "grad_xa_wo")
    g_wq = _grad_w(hn2, as3(dq), 512, D_MODEL, "grad_xa_wq")
    halves_of.update(zip(group_b, finish_exchange(group_b, started_b, g_wq)))
    group_c = ["xa_wo", "xa_wq", "xa_wkv"]
    started_c = start_exchange(group_c, [split(g, nm) for g, nm in zip((g_wo, g_wq, g_wkv), group_c)])

    dh1 = _after(dh1, started_c[3])
    (gx, dz, d_cw, d_cb, d_lng, d_lnb, d_gg, d_gb, d_ws, d_bs_sum, d_bin, d_mix_g) = _seqmix_bwd(
        dh1, x2, z, c1, w_out_g, w_in_g, row(norm_mix_g), cw_g, row(conv_ln_g), row(conv_ln_b),
        row(gm_ln_g), row(gm_ln_b), wpair, wpair_t, bias, tb)
    d_bs = _head_bias_grad(d_bs_sum)[:, :GM_HEADS].T
    g_wout = _grad_w(mix, as3(dh1), 512, D_MODEL, "grad_w_out")
    g_win = _grad_w(hn1, as3(dz), 512, 512, "grad_w_in")
    halves_of.update(zip(group_c, finish_exchange(group_c, started_c, g_win)))

    small_names = ["norm_mix_g", "b_in", "conv_w", "conv_b", "conv_ln_g", "conv_ln_b", "gm_ln_g", "gm_ln_b",
                   "gm_w_s", "gm_b_s", "norm_xa_g", "mem_norm_g", "norm_ffn_g", "final_norm_g"]
    small_grads = dict(norm_mix_g=d_mix_g, b_in=d_bin, conv_w=d_cw[:CONV_KERNEL], conv_b=d_cb, conv_ln_g=d_lng,
                       conv_ln_b=d_lnb, gm_ln_g=d_gg, gm_ln_b=d_gb, gm_w_s=d_ws, gm_b_s=d_bs, norm_xa_g=d_xa_g,
                       mem_norm_g=d_mem_g, norm_ffn_g=d_ffn_g, final_norm_g=d_final_g)
    full_shapes = dict(norm_mix_g=(D_MODEL,), b_in=(2048,), conv_w=(CONV_KERNEL, CONV_WIDTH), conv_b=(512,),
                       conv_ln_g=(512,), conv_ln_b=(512,), gm_ln_g=(512,), gm_ln_b=(512,),
                       gm_w_s=(GM_HEADS, CHUNK, CHUNK), gm_b_s=(GM_HEADS, CHUNK), norm_xa_g=(D_MODEL,),
                       mem_norm_g=(D_MODEL,), norm_ffn_g=(D_MODEL,), final_norm_g=(D_MODEL,))
    pack_rows = _rows_for([full_shapes[nm] for nm in small_names], 32)
    small_pack = _pack([small_grads[nm] for nm in small_names], pack_rows)

    group_d = ["w_out", "w_in", "small"]
    sums_d, parts_d = chip_sums_of(group_d, [split(g_wout, "w_out"), split(g_win, "w_in"),
                                             small_pack.reshape(1, 2, pack_rows // 2, LANES)])
    parts_d = _exchange_chip_sums(sums_d, parts_d)
    halves_of.update(zip(group_d, [_sum_chips(p, pos, "total_" + nm) for p, nm in zip(parts_d, group_d)]))

    red_names = big_names + ["small"]
    joined = _join_halves([halves_of[nm] for nm in red_names])
    grads = {nm: joined[i].reshape(big[nm].shape) for i, nm in enumerate(big_names)}
    small_red = _unpack(joined[-1].reshape(pack_rows, LANES), [full_shapes[nm] for nm in small_names])
    for nm, g in zip(small_names, small_red):
        grads[nm] = g
    grads["conv_w"] = lax.dynamic_slice(grads["conv_w"], (0, chip * LANES), (CONV_KERNEL, LANES))

    weights = dict(norm_mix_g=norm_mix_g, w_in=w_in, b_in=b_in, conv_w=conv_w, conv_b=conv_b, conv_ln_g=conv_ln_g,
                   conv_ln_b=conv_ln_b, gm_ln_g=gm_ln_g, gm_ln_b=gm_ln_b, gm_w_s=gm_w_s, gm_b_s=gm_b_s, w_out=w_out,
                   norm_xa_g=norm_xa_g, mem_norm_g=mem_norm_g, xa_wq=xa_wq, xa_wkv=xa_wkv, xa_wo=xa_wo,
                   norm_ffn_g=norm_ffn_g, ffn_w_gate_up=ffn_w_gate_up, ffn_w_down=ffn_w_down,
                   final_norm_g=final_norm_g)
    m_in = dict(norm_mix_g=m_norm_mix_g, w_in=m_w_in, b_in=m_b_in, conv_w=m_conv_w, conv_b=m_conv_b,
                conv_ln_g=m_conv_ln_g, conv_ln_b=m_conv_ln_b, gm_ln_g=m_gm_ln_g, gm_ln_b=m_gm_ln_b, gm_w_s=m_gm_w_s,
                gm_b_s=m_gm_b_s, w_out=m_w_out, norm_xa_g=m_norm_xa_g, mem_norm_g=m_mem_norm_g, xa_wq=m_xa_wq,
                xa_wkv=m_xa_wkv, xa_wo=m_xa_wo, norm_ffn_g=m_norm_ffn_g, ffn_w_gate_up=m_ffn_w_gate_up,
                ffn_w_down=m_ffn_w_down, final_norm_g=m_final_norm_g)
    v_in = dict(norm_mix_g=v_norm_mix_g, w_in=v_w_in, b_in=v_b_in, conv_w=v_conv_w, conv_b=v_conv_b,
                conv_ln_g=v_conv_ln_g, conv_ln_b=v_conv_ln_b, gm_ln_g=v_gm_ln_g, gm_ln_b=v_gm_ln_b, gm_w_s=v_gm_w_s,
                gm_b_s=v_gm_b_s, w_out=v_w_out, norm_xa_g=v_norm_xa_g, mem_norm_g=v_mem_norm_g, xa_wq=v_xa_wq,
                xa_wkv=v_xa_wkv, xa_wo=v_xa_wo, norm_ffn_g=v_norm_ffn_g, ffn_w_gate_up=v_ffn_w_gate_up,
                ffn_w_down=v_ffn_w_down, final_norm_g=v_final_norm_g)
    delta, new_m, new_v = {}, {}, {}
    for nm in big_names:
        delta[nm], new_m[nm], new_v[nm] = _adamw(weights[nm], grads[nm], m_in[nm], v_in[nm], "adamw_" + nm)
    local_shapes = [weights[nm].shape for nm in small_names]
    adam_rows = _rows_for(local_shapes, LANES)
    packed = [_pack([src[nm] for nm in small_names], adam_rows) for src in (weights, grads, m_in, v_in)]
    outs = _adamw(*packed, "adamw_small")
    for dst, buf in zip((delta, new_m, new_v), outs):
        for nm, a in zip(small_names, _unpack(buf, local_shapes)):
            dst[nm] = a

    order = ["norm_mix_g", "w_in", "b_in", "conv_w", "conv_b", "conv_ln_g", "conv_ln_b", "gm_ln_g", "gm_ln_b",
             "gm_w_s", "gm_b_s", "w_out", "norm_xa_g", "mem_norm_g", "xa_wq", "xa_wkv", "xa_wo", "norm_ffn_g",
             "ffn_w_gate_up", "ffn_w_down", "final_norm_g"]
    fit = lambda a, nm: a.reshape(weights[nm].shape)
    return (loss, gx.reshape(x.shape),
            *[fit(grads[nm], nm) for nm in order], *[fit(delta[nm], nm) for nm in order],
            *[fit(new_m[nm], nm) for nm in order], *[fit(new_v[nm], nm) for nm in order])
```

```python
import functools

import jax
import jax.numpy as jnp
from jax import lax
from jax.experimental import pallas as pl
from jax.experimental.pallas import tpu as pltpu

F32 = jnp.float32
BF16 = jnp.bfloat16

D_MODEL = 1024
CONV_WIDTH = 512
GM_WIDTH = 512
CONV_KERNEL = 31
CONV_HALO = 32
CHUNK = 128
GM_HEADS = 8
GM_HEAD_DIM = 64
XA_HEADS = 4
XA_HEAD_DIM = 256
FFN_HIDDEN = 2816
FFN_HALF = FFN_HIDDEN // 2
RMS_EPS = 1e-6
LN_EPS = 1e-5
N_CHIPS = 4
LANES = 128

ADAM_LR = 0.001
ADAM_B1 = 0.9
ADAM_B2 = 0.999
ADAM_EPS = 1e-08
ADAM_WD = 0.01
ADAM_STEP = 10

VMEM_LIMIT_BYTES = 56 * 1024 * 1024
MESH = pl.DeviceIdType.MESH
ANY = pl.BlockSpec(memory_space=pl.ANY)
HBM_SPEC = pl.BlockSpec(memory_space=pltpu.HBM)
SEM_SPEC = pl.BlockSpec(memory_space=pltpu.SEMAPHORE)

_NT = (((1,), (1,)), ((), ()))
_TN = (((0,), (0,)), ((), ()))
_GELU_C = 0.7978845608028654
_GELU_A = 0.044715


def _dot(a, b):
    return jnp.dot(a, b, preferred_element_type=F32)


def _dot_nt(a, b):
    return lax.dot_general(a, b, _NT, preferred_element_type=F32)


def _dot_tn(a, b):
    return lax.dot_general(a, b, _TN, preferred_element_type=F32)


def _mean(v):
    return jnp.mean(v, axis=-1, keepdims=True)


def _rowsum(v):
    return jnp.sum(v, axis=0, keepdims=True)


def _sigmoid(v):
    return 1.0 / (1.0 + jnp.exp(-v))


def _gelu_parts(v):
    v2 = v * v
    t = jnp.tanh(_GELU_C * (v + _GELU_A * v * v2))
    g = 0.5 * v * (1.0 + t)
    dg = 0.5 * (1.0 + t) + 0.5 * v * (1.0 - t * t) * (_GELU_C * (1.0 + 3.0 * _GELU_A * v2))
    return g, dg


def _rms_stats(v):
    return lax.rsqrt(_mean(v * v) + RMS_EPS)


def _rms_bwd(dy, v, r, g):
    n = v * r
    dn = dy * g
    dv = r * (dn - n * _mean(dn * n))
    return dv, _rowsum(dy * n)


def _ln_stats(v):
    mu = _mean(v)
    xc = v - mu
    rs = lax.rsqrt(_mean(xc * xc) + LN_EPS)
    return xc * rs, rs


def _ln_bwd(dy, xh, rs, g):
    dxh = dy * g
    dv = rs * (dxh - _mean(dxh) - xh * _mean(dxh * xh))
    return dv, _rowsum(dy * xh), _rowsum(dy)


def _params(sem):
    return pltpu.CompilerParams(dimension_semantics=sem, vmem_limit_bytes=VMEM_LIMIT_BYTES)


def _row_tile(s):
    return 512 if s % 512 == 0 and s >= 2048 else 128


def _mesh_pos():
    return lax.axis_index("x"), lax.axis_index("y"), lax.axis_index("c")


def _cast_into_slot(w, pos, dtype, name):
    _, h, c = w.shape

    def body(pos_ref, w_ref, o_ref):
        o_ref[0] = w_ref[...].astype(dtype)

    return pl.pallas_call(
        body, name=name,
        grid_spec=pltpu.PrefetchScalarGridSpec(
            num_scalar_prefetch=1, grid=(2,),
            in_specs=[pl.BlockSpec((1, h, c), lambda i, p: (i, 0, 0))],
            out_specs=pl.BlockSpec((1, 1, h, c), lambda i, p: (p[0], i, 0, 0))),
        out_shape=jax.ShapeDtypeStruct((N_CHIPS, 2, h, c), dtype),
        compiler_params=_params(("parallel",)),
    )(pos, w)


def _adam_rows(r, c):
    for tr in (r, 1024, 704, 640, 512, 352, 320, 256, 128, 64, 32, 16, 8):
        if r % tr == 0 and tr * c * 4 <= (3 << 19):
            return tr
    return r


def _adamw(w, g, m, v, name):
    r, c = w.shape
    tr = _adam_rows(r, c)

    def body(w_ref, g_ref, m_ref, v_ref, d_ref, nm_ref, nv_ref):
        gv = g_ref[...]
        nm = ADAM_B1 * m_ref[...] + (1.0 - ADAM_B1) * gv
        nv = ADAM_B2 * v_ref[...] + (1.0 - ADAM_B2) * (gv * gv)
        m_hat = nm / (1.0 - ADAM_B1 ** ADAM_STEP)
        v_hat = nv / (1.0 - ADAM_B2 ** ADAM_STEP)
        d_ref[...] = -ADAM_LR * (m_hat / (jnp.sqrt(v_hat) + ADAM_EPS) + ADAM_WD * w_ref[...])
        nm_ref[...] = nm
        nv_ref[...] = nv

    spec = pl.BlockSpec((tr, c), lambda i: (i, 0))
    shp = jax.ShapeDtypeStruct((r, c), F32)
    return pl.pallas_call(
        body, name=name, grid=(r // tr,),
        in_specs=[spec] * 4, out_specs=[spec] * 3, out_shape=[shp] * 3,
        compiler_params=_params(("parallel",)),
    )(w, g, m, v)


def _other_chips(x, y):
    return [(1 - x, y), (x, 1 - y), (1 - x, 1 - y)]


def _gather_shards(bufs):
    n = len(bufs)

    def body(*refs):
        outs = refs[n:2 * n]
        ici_send, ici_recv, fwd_send, fwd_recv = refs[2 * n:]
        x, y, c = _mesh_pos()
        me = 2 * x + y
        sib = (x, y, 1 - c)
        chips = _other_chips(x, y)

        def ici(a, k, chip_idx, to):
            return pltpu.make_async_remote_copy(
                src_ref=outs[a].at[chip_idx, c], dst_ref=outs[a].at[chip_idx, c],
                send_sem=ici_send.at[a, k], recv_sem=ici_recv.at[a, k],
                device_id=to, device_id_type=MESH)

        def fwd(a, k, chip_idx, half):
            return pltpu.make_async_remote_copy(
                src_ref=outs[a].at[chip_idx, half], dst_ref=outs[a].at[chip_idx, half],
                send_sem=fwd_send.at[a, k], recv_sem=fwd_recv.at[a, k],
                device_id=sib, device_id_type=MESH)

        sends = [ici(a, k, me, (*chips[k], c)) for a in range(n) for k in range(3)]
        for cp in sends:
            cp.start()
        passed = []
        for a in range(n):
            for k in range(3):
                ck = 2 * chips[k][0] + chips[k][1]
                ici(a, k, ck, (*chips[k], c)).wait_recv()
                cp = fwd(a, k, ck, c)
                cp.start()
                passed.append(cp)
        for a in range(n):
            for k in range(3):
                ck = 2 * chips[k][0] + chips[k][1]
                fwd(a, k, ck, 1 - c).wait_recv()
        for cp in sends + passed:
            cp.wait_send()

    out_shape = [jax.ShapeDtypeStruct(b.shape, b.dtype) for b in bufs]
    return pl.pallas_call(
        body, name="gather_weights",
        in_specs=[ANY] * n, out_specs=[ANY] * n, out_shape=out_shape,
        input_output_aliases={a: a for a in range(n)},
        scratch_shapes=[pltpu.SemaphoreType.DMA((n, 3))] * 4,
    )(*bufs)


def _gather_descriptors(bufs, send_of, recv_of):
    x, y, c = _mesh_pos()
    me = 2 * x + y
    chips = _other_chips(x, y)
    sends, arrivals = [], []
    for a in range(len(bufs)):
        for k in range(3):
            ck = 2 * chips[k][0] + chips[k][1]

            def copy(slot, a=a, k=k):
                return pltpu.make_async_remote_copy(
                    src_ref=bufs[a].at[slot, c], dst_ref=bufs[a].at[slot, c],
                    send_sem=send_of(a, k), recv_sem=recv_of(a, k),
                    device_id=(*chips[k], c), device_id_type=MESH)

            sends.append(functools.partial(copy, me))
            arrivals.append(functools.partial(copy, ck))
    return sends, arrivals


def _gather_start(bufs, name):
    n = len(bufs)
    ns = 3 * n

    def body(*refs):
        sems = refs[n:n + 2 * ns]
        thru = refs[n + 2 * ns:2 * n + 2 * ns]
        token = refs[2 * n + 2 * ns]
        sends, _ = _gather_descriptors(thru, lambda a, k: sems[3 * a + k], lambda a, k: sems[ns + 3 * a + k])
        for cp in sends:
            cp().start()
        token[...] = jnp.zeros_like(token)

    held = [pltpu.with_memory_space_constraint(b, pltpu.HBM) for b in bufs]
    out = pl.pallas_call(
        body, name=name,
        out_shape=(*[pltpu.SemaphoreType.DMA(())] * (2 * ns), *[pltpu.HBM(b.shape, b.dtype) for b in held],
                   jax.ShapeDtypeStruct((8, LANES), F32)),
        in_specs=[HBM_SPEC] * n,
        out_specs=(*[SEM_SPEC] * (2 * ns), *[HBM_SPEC] * n, pl.BlockSpec(memory_space=pltpu.VMEM)),
        input_output_aliases={i: 2 * ns + i for i in range(n)},
        compiler_params=pltpu.CompilerParams(has_side_effects=pltpu.SideEffectType.DATAFLOW_SIDE_EFFECTING),
    )(*held)
    return list(out[:ns]), list(out[ns:2 * ns]), list(out[2 * ns:2 * ns + n]), out[2 * ns + n]


def _gather_wait(send_sems, recv_sems, bufs, after, name):
    n = len(bufs)
    ns = 3 * n

    def body(*refs):
        buf_ref = refs[:n]
        sem_ref = refs[n:n + 2 * ns]
        sends, arrivals = _gather_descriptors(buf_ref, lambda a, k: sem_ref[3 * a + k],
                                              lambda a, k: sem_ref[ns + 3 * a + k])
        for cp in sends:
            cp().wait_send()
        for cp in arrivals:
            cp().wait_recv()

    out = pl.pallas_call(
        body, name=name,
        out_shape=tuple(pltpu.HBM(b.shape, b.dtype) for b in bufs),
        in_specs=[HBM_SPEC] * n + [SEM_SPEC] * (2 * ns) + [ANY],
        out_specs=tuple([HBM_SPEC] * n),
        input_output_aliases={i: i for i in range(n)},
        compiler_params=pltpu.CompilerParams(has_side_effects=pltpu.SideEffectType.DATAFLOW_SIDE_EFFECTING),
    )(*bufs, *send_sems, *recv_sems, after)
    return list(out)


def _pass_to_sibling(bufs, name):
    n = len(bufs)

    def body(*refs):
        outs = refs[n:2 * n]
        send_sem, recv_sem = refs[2 * n:]
        x, y, c = _mesh_pos()
        chips = _other_chips(x, y)

        def half(a, k, which):
            ck = 2 * chips[k][0] + chips[k][1]
            return pltpu.make_async_remote_copy(
                src_ref=outs[a].at[ck, which], dst_ref=outs[a].at[ck, which],
                send_sem=send_sem.at[a, k], recv_sem=recv_sem.at[a, k],
                device_id=(x, y, 1 - c), device_id_type=MESH)

        sends = [half(a, k, c) for a in range(n) for k in range(3)]
        for cp in sends:
            cp.start()
        for a in range(n):
            for k in range(3):
                half(a, k, 1 - c).wait_recv()
        for cp in sends:
            cp.wait_send()

    return pl.pallas_call(
        body, name=name,
        in_specs=[ANY] * n, out_specs=[ANY] * n,
        out_shape=[jax.ShapeDtypeStruct(b.shape, b.dtype) for b in bufs],
        input_output_aliases={a: a for a in range(n)},
        scratch_shapes=[pltpu.SemaphoreType.DMA((n, 3))] * 2,
    )(*bufs)


def _swap_halves(grads, name):
    n = len(grads)

    def body(*refs):
        ins, outs = refs[:n], refs[n:2 * n]
        send_sem, recv_sem = refs[2 * n:]
        x, y, c = _mesh_pos()
        cps = [pltpu.make_async_remote_copy(
            src_ref=ins[a].at[:, pl.ds(1 - c, 1)], dst_ref=outs[a],
            send_sem=send_sem.at[a], recv_sem=recv_sem.at[a],
            device_id=(x, y, 1 - c), device_id_type=MESH) for a in range(n)]
        for cp in cps:
            cp.start()
        for cp in cps:
            cp.wait()

    out_shape = [jax.ShapeDtypeStruct((g.shape[0], 1) + g.shape[2:], g.dtype) for g in grads]
    return pl.pallas_call(
        body, name=name,
        in_specs=[ANY] * n, out_specs=[ANY] * n, out_shape=out_shape,
        scratch_shapes=[pltpu.SemaphoreType.DMA((n,))] * 2,
    )(*grads)


def _add_half(g, got, pos, name):
    j, _, h, c = g.shape

    def body(pos_ref, g_ref, r_ref, o_ref, p_ref):
        val = (g_ref[0, 0] + r_ref[0, 0]).astype(BF16)
        o_ref[0] = val
        if j == 1:
            p_ref[0] = val
        else:
            @pl.when(pl.program_id(0) == pos_ref[0])
            def _():
                p_ref[0] = val

    return pl.pallas_call(
        body, name=name,
        grid_spec=pltpu.PrefetchScalarGridSpec(
            num_scalar_prefetch=1, grid=(j,),
            in_specs=[pl.BlockSpec((1, 1, h, c), lambda i, p: (i, p[1], 0, 0)),
                      pl.BlockSpec((1, 1, h, c), lambda i, p: (i, 0, 0, 0))],
            out_specs=[pl.BlockSpec((1, h, c), lambda i, p: (i, 0, 0)),
                       pl.BlockSpec((1, h, c), lambda i, p: (p[0], 0, 0))]),
        out_shape=[jax.ShapeDtypeStruct((j, h, c), BF16), jax.ShapeDtypeStruct((N_CHIPS, h, c), BF16)],
        compiler_params=_params(("arbitrary",)),
    )(pos, g, got)


def _exchange_chip_sums(sums, parts):
    n = len(sums)

    def body(*refs):
        ins, outs = refs[:n], refs[2 * n:3 * n]
        send_sem, recv_sem = refs[3 * n:]
        sends, arrivals = _exchange_descriptors(ins, outs, lambda a, k: send_sem.at[a, k],
                                                lambda a, k: recv_sem.at[a, k])
        for cp in sends:
            cp().start()
        for cp in arrivals:
            cp().wait_recv()
        for cp in sends:
            cp().wait_send()

    out_shape = [jax.ShapeDtypeStruct(p.shape, p.dtype) for p in parts]
    return pl.pallas_call(
        body, name="exchange_chip_sums",
        in_specs=[ANY] * (2 * n), out_specs=[ANY] * n, out_shape=out_shape,
        input_output_aliases={n + a: a for a in range(n)},
        scratch_shapes=[pltpu.SemaphoreType.DMA((n, 3))] * 2,
    )(*sums, *parts)


def _exchange_descriptors(sums, parts, send_of, recv_of):
    x, y, c = _mesh_pos()
    me = 2 * x + y
    chips = _other_chips(x, y)
    sends, arrivals = [], []
    for a in range(len(sums)):
        for k in range(3):
            ck = 2 * chips[k][0] + chips[k][1]
            mine = sums[a].at[ck] if sums[a].shape[0] == N_CHIPS else sums[a].at[0]

            def copy(dst_slot, a=a, k=k, mine=mine):
                return pltpu.make_async_remote_copy(
                    src_ref=mine, dst_ref=parts[a].at[dst_slot],
                    send_sem=send_of(a, k), recv_sem=recv_of(a, k),
                    device_id=(*chips[k], c), device_id_type=MESH)

            sends.append(functools.partial(copy, me))
            arrivals.append(functools.partial(copy, ck))
    return sends, arrivals


def _exchange_start(sums, parts, name):
    n = len(sums)
    ns = 3 * n

    def body(*refs):
        sems = refs[2 * n:2 * n + 2 * ns]
        sums_thru = refs[2 * n + 2 * ns:3 * n + 2 * ns]
        parts_thru = refs[3 * n + 2 * ns:4 * n + 2 * ns]
        token = refs[4 * n + 2 * ns]
        sends, _ = _exchange_descriptors(sums_thru, parts_thru, lambda a, k: sems[3 * a + k],
                                         lambda a, k: sems[ns + 3 * a + k])
        for cp in sends:
            cp().start()
        token[...] = jnp.zeros_like(token)

    hbm = lambda a: pltpu.HBM(a.shape, a.dtype)
    held = [pltpu.with_memory_space_constraint(a, pltpu.HBM) for a in (*sums, *parts)]
    out = pl.pallas_call(
        body, name=name,
        out_shape=(*[pltpu.SemaphoreType.DMA(())] * (2 * ns), *[hbm(a) for a in held],
                   jax.ShapeDtypeStruct((8, LANES), F32)),
        in_specs=[HBM_SPEC] * (2 * n),
        out_specs=(*[SEM_SPEC] * (2 * ns), *[HBM_SPEC] * (2 * n), pl.BlockSpec(memory_space=pltpu.VMEM)),
        input_output_aliases={i: 2 * ns + i for i in range(2 * n)},
        compiler_params=pltpu.CompilerParams(has_side_effects=pltpu.SideEffectType.DATAFLOW_SIDE_EFFECTING),
    )(*held)
    return (list(out[:2 * ns]), list(out[2 * ns:2 * ns + n]), list(out[2 * ns + n:2 * ns + 2 * n]),
            out[2 * ns + 2 * n])


def _exchange_wait(sems, sums, parts, after, name):
    n = len(sums)
    ns = 3 * n

    def body(*refs):
        sums_ref, parts_ref = refs[:n], refs[n:2 * n]
        sem_ref = refs[2 * n:2 * n + 2 * ns]
        sends, arrivals = _exchange_descriptors(sums_ref, parts_ref, lambda a, k: sem_ref[3 * a + k],
                                                lambda a, k: sem_ref[ns + 3 * a + k])
        for cp in sends:
            cp().wait_send()
        for cp in arrivals:
            cp().wait_recv()

    hbm = lambda a: pltpu.HBM(a.shape, a.dtype)
    out = pl.pallas_call(
        body, name=name,
        out_shape=tuple(hbm(a) for a in (*sums, *parts)),
        in_specs=[HBM_SPEC] * (2 * n) + [SEM_SPEC] * (2 * ns) + [ANY],
        out_specs=tuple([HBM_SPEC] * (2 * n)),
        input_output_aliases={i: i for i in range(2 * n)},
        compiler_params=pltpu.CompilerParams(has_side_effects=pltpu.SideEffectType.DATAFLOW_SIDE_EFFECTING),
    )(*sums, *parts, *sems, after)
    return list(out[n:])


def _after(value, token):
    return lax.optimization_barrier((value, token))[0]


def _sum_chips(parts, pos, name):
    _, h, c = parts.shape

    def body(pos_ref, p_ref, o_ref):
        o_ref[0] = ((p_ref[0].astype(F32) + p_ref[1].astype(F32)) + p_ref[2].astype(F32)) + p_ref[3].astype(F32)

    return pl.pallas_call(
        body, name=name,
        grid_spec=pltpu.PrefetchScalarGridSpec(
            num_scalar_prefetch=1, grid=(1,),
            in_specs=[pl.BlockSpec((N_CHIPS, h, c), lambda i, p: (0, 0, 0))],
            out_specs=pl.BlockSpec((1, h, c), lambda i, p: (p[1], 0, 0))),
        out_shape=jax.ShapeDtypeStruct((2, h, c), F32),
        compiler_params=_params(("arbitrary",)),
    )(pos, parts)


def _join_halves(fulls):
    n = len(fulls)

    def body(*refs):
        outs = refs[n:2 * n]
        send_sem, recv_sem = refs[2 * n:]
        x, y, c = _mesh_pos()

        def half(a, which):
            return pltpu.make_async_remote_copy(
                src_ref=outs[a].at[which], dst_ref=outs[a].at[which],
                send_sem=send_sem.at[a], recv_sem=recv_sem.at[a],
                device_id=(x, y, 1 - c), device_id_type=MESH)

        sends = [half(a, c) for a in range(n)]
        for cp in sends:
            cp.start()
        for a in range(n):
            half(a, 1 - c).wait_recv()
        for cp in sends:
            cp.wait_send()

    out_shape = [jax.ShapeDtypeStruct(f.shape, f.dtype) for f in fulls]
    return pl.pallas_call(
        body, name="join_halves",
        in_specs=[ANY] * n, out_specs=[ANY] * n, out_shape=out_shape,
        input_output_aliases={a: a for a in range(n)},
        scratch_shapes=[pltpu.SemaphoreType.DMA((n,))] * 2,
    )(*fulls)


def _mix_in(x, g, w_in, b_in, ts):
    s = x.shape[0]

    def body(x_ref, g_ref, w_ref, b_ref, z_ref, hn_ref):
        xv = x_ref[...]
        hn = (xv * _rms_stats(xv) * g_ref[...]).astype(BF16)
        hn_ref[...] = hn
        for j in range(4):
            cols = slice(j * 512, (j + 1) * 512)
            z_ref[:, cols] = _dot(hn, w_ref[j]) + b_ref[:, cols]

    return pl.pallas_call(
        body, name="mix_in", grid=(s // ts,),
        in_specs=[pl.BlockSpec((ts, D_MODEL), lambda i: (i, 0)),
                  pl.BlockSpec((1, D_MODEL), lambda i: (0, 0)),
                  pl.BlockSpec((4, D_MODEL, 512), lambda i: (0, 0, 0)),
                  pl.BlockSpec((1, 2048), lambda i: (0, 0))],
        out_specs=[pl.BlockSpec((ts, 2048), lambda i: (i, 0)),
                   pl.BlockSpec((ts, D_MODEL), lambda i: (i, 0))],
        out_shape=[jax.ShapeDtypeStruct((s, 2048), F32), jax.ShapeDtypeStruct((s, D_MODEL), BF16)],
        compiler_params=_params(("parallel",)),
    )(x, g, w_in, b_in)


def _lane_is_low_head():
    lane = lax.broadcasted_iota(jnp.int32, (1, GM_WIDTH), 1)
    return (lane & GM_HEAD_DIM) == 0


def _gm_mix(v_lo, v_hi, wpair_ref, bias_ref, mixed_ref, t):
    for n in range(t // CHUNK):
        rows = slice(n * CHUNK, (n + 1) * CHUNK)
        for j in range(GM_HEADS // 2):
            cols = slice(j * LANES, (j + 1) * LANES)
            rhs = jnp.concatenate([v_lo[rows, cols], v_hi[rows, cols]], axis=0)
            mixed_ref[rows, cols] = _dot(wpair_ref[j], rhs) + bias_ref[:, cols]


def _seqmix_fwd(z, cw, cb, lng, lnb, gg, gb, wpair, bias, t):
    s = z.shape[0]

    def body(z_ref, cw_ref, cb_ref, lng_ref, lnb_ref, gg_ref, gb_ref, wpair_ref, bias_ref,
             mix_ref, c1_ref, abuf, mixed_ref):
        i = pl.program_id(0)

        @pl.when(i == 0)
        def _():
            abuf[0:CONV_HALO, :] = jnp.zeros((CONV_HALO, CONV_WIDTH), F32)

        @pl.when(i > 0)
        def _():
            abuf[0:CONV_HALO, :] = abuf[t:t + CONV_HALO, :]

        abuf[CONV_HALO:, :] = z_ref[:, 0:512] * _sigmoid(z_ref[:, 512:1024])
        acc = jnp.zeros((t, CONV_WIDTH), F32)
        for k in range(CONV_KERNEL):
            acc = acc + cw_ref[k:k + 1, :] * abuf[pl.ds(CONV_HALO - (CONV_KERNEL - 1) + k, t), :]
        c1 = acc + cb_ref[...]
        c1_ref[...] = c1
        xh, _ = _ln_stats(c1)
        ln = xh * lng_ref[...] + lnb_ref[...]
        mix_ref[:, 0:512] = (ln * _sigmoid(ln)).astype(BF16)

        u, _ = _gelu_parts(z_ref[:, 1024:1536])
        gv, _ = _gelu_parts(z_ref[:, 1536:2048])
        vxh, _ = _ln_stats(gv)
        v = vxh * gg_ref[...] + gb_ref[...]
        low = _lane_is_low_head()
        v_lo = jnp.where(low, v, 0.0).astype(BF16)
        v_hi = jnp.where(low, 0.0, v).astype(BF16)
        _gm_mix(v_lo, v_hi, wpair_ref, bias_ref, mixed_ref, t)
        mix_ref[:, 512:1024] = (u * mixed_ref[...]).astype(BF16)

    vec = lambda n: pl.BlockSpec((1, n), lambda i: (0, 0))
    return pl.pallas_call(
        body, name="seqmix_fwd", grid=(s // t,),
        in_specs=[pl.BlockSpec((t, 2048), lambda i: (i, 0)),
                  pl.BlockSpec((CONV_HALO, CONV_WIDTH), lambda i: (0, 0)),
                  vec(512), vec(512), vec(512), vec(512), vec(512),
                  pl.BlockSpec((4, CHUNK, 2 * CHUNK), lambda i: (0, 0, 0)),
                  pl.BlockSpec((CHUNK, GM_WIDTH), lambda i: (0, 0))],
        out_specs=[pl.BlockSpec((t, D_MODEL), lambda i: (i, 0)),
                   pl.BlockSpec((t, CONV_WIDTH), lambda i: (i, 0))],
        out_shape=[jax.ShapeDtypeStruct((s, D_MODEL), BF16), jax.ShapeDtypeStruct((s, CONV_WIDTH), F32)],
        scratch_shapes=[pltpu.VMEM((t + CONV_HALO, CONV_WIDTH), F32), pltpu.VMEM((t, GM_WIDTH), F32)],
        compiler_params=_params(("arbitrary",)),
    )(z, cw, cb, lng, lnb, gg, gb, wpair, bias)


def _out_proj_q(x, mix, w_out, g, wq, ts):
    s = x.shape[0]

    def body(x_ref, mix_ref, wo_ref, g_ref, wq_ref, h1_ref, hn_ref, q_ref):
        h1 = x_ref[...] + _dot(mix_ref[...], wo_ref[...])
        h1_ref[...] = h1
        hn = (h1 * _rms_stats(h1) * g_ref[...]).astype(BF16)
        hn_ref[...] = hn
        q_ref[...] = _dot(hn, wq_ref[...]).astype(BF16)

    row = lambda dt: pl.BlockSpec((ts, D_MODEL), lambda i: (i, 0))
    full = pl.BlockSpec((D_MODEL, D_MODEL), lambda i: (0, 0))
    return pl.pallas_call(
        body, name="out_proj_q", grid=(s // ts,),
        in_specs=[row(F32), row(BF16), full, pl.BlockSpec((1, D_MODEL), lambda i: (0, 0)), full],
        out_specs=[row(F32), row(BF16), row(BF16)],
        out_shape=[jax.ShapeDtypeStruct((s, D_MODEL), F32), jax.ShapeDtypeStruct((s, D_MODEL), BF16),
                   jax.ShapeDtypeStruct((s, D_MODEL), BF16)],
        compiler_params=_params(("parallel",)),
    )(x, mix, w_out, g, wq)


def _mem_kv(mem, g, wkv):
    m = mem.shape[0]

    def body(mem_ref, g_ref, w_ref, mn_ref, kv_ref):
        mv = mem_ref[...]
        mn = (mv * _rms_stats(mv) * g_ref[...]).astype(BF16)
        mn_ref[...] = mn
        for j in range(4):
            kv_ref[:, j * 512:(j + 1) * 512] = _dot(mn, w_ref[j]).astype(BF16)

    return pl.pallas_call(
        body, name="mem_kv",
        out_shape=[jax.ShapeDtypeStruct((m, D_MODEL), BF16), jax.ShapeDtypeStruct((m, 2 * D_MODEL), BF16)],
        compiler_params=pltpu.CompilerParams(vmem_limit_bytes=VMEM_LIMIT_BYTES),
    )(mem, g, wkv)


def _softmax_rows(sc):
    e = jnp.exp(sc - jnp.max(sc, axis=-1, keepdims=True))
    return e / jnp.sum(e, axis=-1, keepdims=True)


def _attn_fwd(q, kv, h1, wo, g, ts):
    s, m = q.shape[0], kv.shape[0]
    scale = XA_HEAD_DIM ** -0.5

    def body(q_ref, kv_ref, h1_ref, wo_ref, g_ref, o_ref, h2_ref, hn_ref):
        for h in range(XA_HEADS):
            cols = slice(h * XA_HEAD_DIM, (h + 1) * XA_HEAD_DIM)
            vcols = slice(D_MODEL + h * XA_HEAD_DIM, D_MODEL + (h + 1) * XA_HEAD_DIM)
            p = _softmax_rows(_dot_nt(q_ref[:, cols], kv_ref[:, cols]) * scale)
            o_ref[:, cols] = _dot(p.astype(BF16), kv_ref[:, vcols]).astype(BF16)
        h2 = h1_ref[...] + _dot(o_ref[...], wo_ref[...])
        h2_ref[...] = h2
        hn_ref[...] = (h2 * _rms_stats(h2) * g_ref[...]).astype(BF16)

    row = pl.BlockSpec((ts, D_MODEL), lambda i: (i, 0))
    return pl.pallas_call(
        body, name="attn_fwd", grid=(s // ts,),
        in_specs=[row, pl.BlockSpec((m, 2 * D_MODEL), lambda i: (0, 0)), row,
                  pl.BlockSpec((D_MODEL, D_MODEL), lambda i: (0, 0)),
                  pl.BlockSpec((1, D_MODEL), lambda i: (0, 0))],
        out_specs=[row, row, row],
        out_shape=[jax.ShapeDtypeStruct((s, D_MODEL), BF16), jax.ShapeDtypeStruct((s, D_MODEL), F32),
                   jax.ShapeDtypeStruct((s, D_MODEL), BF16)],
        compiler_params=_params(("parallel",)),
    )(q, kv, h1, wo, g)


def _ffn_up(hn, wgu, ts):
    s = hn.shape[0]

    def body(hn_ref, w_ref, gu_ref, act_ref):
        hv = hn_ref[...]
        gate = _dot(hv, w_ref[0, 0])
        up = _dot(hv, w_ref[1, 0])
        gu_ref[0] = gate
        gu_ref[1] = up
        act_ref[...] = (gate * _sigmoid(gate) * up).astype(BF16)

    return pl.pallas_call(
        body, name="ffn_up", grid=(2, s // ts),
        in_specs=[pl.BlockSpec((ts, D_MODEL), lambda j, i: (i, 0)),
                  pl.BlockSpec((2, 1, D_MODEL, FFN_HALF), lambda j, i: (0, j, 0, 0))],
        out_specs=[pl.BlockSpec((2, ts, FFN_HALF), lambda j, i: (0, i, j)),
                   pl.BlockSpec((ts, FFN_HALF), lambda j, i: (i, j))],
        out_shape=[jax.ShapeDtypeStruct((2, s, FFN_HIDDEN), F32), jax.ShapeDtypeStruct((s, FFN_HIDDEN), BF16)],
        compiler_params=_params(("parallel", "parallel")),
    )(hn, wgu)


def _ffn_down_loss(act, wd, h2, g, target, ts):
    s = act.shape[0]

    def body(act_ref, wd_ref, h2_ref, g_ref, t_ref, dh_ref, sq_ref, dg_ref):
        @pl.when(pl.program_id(0) == 0)
        def _():
            sq_ref[...] = jnp.zeros_like(sq_ref)
            dg_ref[...] = jnp.zeros_like(dg_ref)

        h3 = h2_ref[...] + _dot(act_ref[...], wd_ref[...])
        r = _rms_stats(h3)
        gv = g_ref[...]
        diff = h3 * r * gv - t_ref[...]
        sq_ref[...] += _rowsum(diff * diff)
        dh, dg = _rms_bwd(diff / D_MODEL, h3, r, gv)
        dh_ref[...] = dh
        dg_ref[...] += dg

    row = pl.BlockSpec((ts, D_MODEL), lambda i: (i, 0))
    vec = pl.BlockSpec((1, D_MODEL), lambda i: (0, 0))
    return pl.pallas_call(
        body, name="ffn_down_loss", grid=(s // ts,),
        in_specs=[pl.BlockSpec((ts, FFN_HIDDEN), lambda i: (i, 0)),
                  pl.BlockSpec((FFN_HIDDEN, D_MODEL), lambda i: (0, 0)), row, vec, row],
        out_specs=[row, vec, vec],
        out_shape=[jax.ShapeDtypeStruct((s, D_MODEL), F32), jax.ShapeDtypeStruct((1, D_MODEL), F32),
                   jax.ShapeDtypeStruct((1, D_MODEL), F32)],
        compiler_params=_params(("arbitrary",)),
    )(act, wd, h2, g, target)


def _grad_w(a, b, tk, tn, name):
    s, k = a.shape
    gb, _, n = b.shape
    nblk = n // tn
    tsr = 512 if s % 512 == 0 else s

    def body(a_ref, b_ref, o_ref):
        @pl.when(pl.program_id(2) == 0)
        def _():
            o_ref[...] = jnp.zeros_like(o_ref)

        o_ref[0] += _dot_tn(a_ref[...].astype(BF16), b_ref[0].astype(BF16))

    return pl.pallas_call(
        body, name=name, grid=(gb * nblk, k // tk, s // tsr),
        in_specs=[pl.BlockSpec((tsr, tk), lambda ni, ki, si: (si, ki)),
                  pl.BlockSpec((1, tsr, tn), lambda ni, ki, si: (ni // nblk, si, ni % nblk))],
        out_specs=pl.BlockSpec((1, tk, tn), lambda ni, ki, si: (ni, ki, 0)),
        out_shape=jax.ShapeDtypeStruct((gb * nblk, k, tn), F32),
        compiler_params=_params(("parallel", "parallel", "arbitrary")),
    )(a, b)


def _ffn_bwd_act(dh3, wd, gu, ts):
    s = dh3.shape[0]

    def body(dh_ref, wd_ref, gu_ref, dgu_ref):
        dact = _dot_nt(dh_ref[...].astype(BF16), wd_ref[0])
        gate, up = gu_ref[0], gu_ref[1]
        sg = _sigmoid(gate)
        dgu_ref[0] = (dact * up * (sg * (1.0 + gate * (1.0 - sg)))).astype(BF16)
        dgu_ref[1] = (dact * (gate * sg)).astype(BF16)

    return pl.pallas_call(
        body, name="ffn_bwd_act", grid=(2, s // ts),
        in_specs=[pl.BlockSpec((ts, D_MODEL), lambda j, i: (i, 0)),
                  pl.BlockSpec((1, FFN_HALF, D_MODEL), lambda j, i: (j, 0, 0)),
                  pl.BlockSpec((2, ts, FFN_HALF), lambda j, i: (0, i, j))],
        out_specs=pl.BlockSpec((2, ts, FFN_HALF), lambda j, i: (0, i, j)),
        out_shape=jax.ShapeDtypeStruct((2, s, FFN_HIDDEN), BF16),
        compiler_params=_params(("parallel", "parallel")),
    )(dh3, wd, gu)


def _ffn_bwd_in(dgu, wgu, dh3, h2, g, ts):
    s = dh3.shape[0]

    def body(dgu_ref, w_ref, dh3_ref, h2_ref, g_ref, dh2_ref, dg_ref):
        @pl.when(pl.program_id(0) == 0)
        def _():
            dg_ref[...] = jnp.zeros_like(dg_ref)

        dhn = jnp.zeros((ts, D_MODEL), F32)
        for p in range(2):
            for j in range(2):
                dhn = dhn + _dot_nt(dgu_ref[p, :, j * FFN_HALF:(j + 1) * FFN_HALF], w_ref[2 * p + j])
        h2 = h2_ref[...]
        dv, dg = _rms_bwd(dhn, h2, _rms_stats(h2), g_ref[...])
        dh2_ref[...] = dh3_ref[...] + dv
        dg_ref[...] += dg

    row = pl.BlockSpec((ts, D_MODEL), lambda i: (i, 0))
    vec = pl.BlockSpec((1, D_MODEL), lambda i: (0, 0))
    return pl.pallas_call(
        body, name="ffn_bwd_in", grid=(s // ts,),
        in_specs=[pl.BlockSpec((2, ts, FFN_HIDDEN), lambda i: (0, i, 0)),
                  pl.BlockSpec((4, D_MODEL, FFN_HALF), lambda i: (0, 0, 0)), row, row, vec],
        out_specs=[row, vec],
        out_shape=[jax.ShapeDtypeStruct((s, D_MODEL), F32), jax.ShapeDtypeStruct((1, D_MODEL), F32)],
        compiler_params=_params(("arbitrary",)),
    )(dgu, wgu, dh3, h2, g)


def _attn_bwd(dh2, wo, q, kv, wq, h1, g, ts):
    s, m = q.shape[0], kv.shape[0]
    scale = XA_HEAD_DIM ** -0.5

    def body(dh2_ref, wo_ref, q_ref, kv_ref, wq_ref, h1_ref, g_ref, dh1_ref, dq_ref, dkv_ref, dg_ref):
        @pl.when(pl.program_id(0) == 0)
        def _():
            dkv_ref[...] = jnp.zeros_like(dkv_ref)
            dg_ref[...] = jnp.zeros_like(dg_ref)

        do = _dot_nt(dh2_ref[...].astype(BF16), wo_ref[...]).astype(BF16)
        for h in range(XA_HEADS):
            cols = slice(h * XA_HEAD_DIM, (h + 1) * XA_HEAD_DIM)
            vcols = slice(D_MODEL + h * XA_HEAD_DIM, D_MODEL + (h + 1) * XA_HEAD_DIM)
            qh, kh, vh, doh = q_ref[:, cols], kv_ref[:, cols], kv_ref[:, vcols], do[:, cols]
            p = _softmax_rows(_dot_nt(qh, kh) * scale)
            dp = _dot_nt(doh, vh)
            ds = (p * (dp - jnp.sum(dp * p, axis=-1, keepdims=True)) * scale).astype(BF16)
            dq_ref[:, cols] = _dot(ds, kh).astype(BF16)
            dkv_ref[:, cols] += _dot_tn(ds, qh)
            dkv_ref[:, vcols] += _dot_tn(p.astype(BF16), doh)
        dhn = _dot_nt(dq_ref[...], wq_ref[...])
        h1 = h1_ref[...]
        dv, dg = _rms_bwd(dhn, h1, _rms_stats(h1), g_ref[...])
        dh1_ref[...] = dh2_ref[...] + dv
        dg_ref[...] += dg

    row = pl.BlockSpec((ts, D_MODEL), lambda i: (i, 0))
    full = pl.BlockSpec((D_MODEL, D_MODEL), lambda i: (0, 0))
    kvs = pl.BlockSpec((m, 2 * D_MODEL), lambda i: (0, 0))
    vec = pl.BlockSpec((1, D_MODEL), lambda i: (0, 0))
    return pl.pallas_call(
        body, name="attn_bwd", grid=(s // ts,),
        in_specs=[row, full, row, kvs, full, row, vec],
        out_specs=[row, row, kvs, vec],
        out_shape=[jax.ShapeDtypeStruct((s, D_MODEL), F32), jax.ShapeDtypeStruct((s, D_MODEL), BF16),
                   jax.ShapeDtypeStruct((m, 2 * D_MODEL), F32), jax.ShapeDtypeStruct((1, D_MODEL), F32)],
        compiler_params=_params(("arbitrary",)),
    )(dh2, wo, q, kv, wq, h1, g)


def _mem_kv_bwd(dkv, mn, wkv, mem, g):
    m = mem.shape[0]

    def body(dkv_ref, mn_ref, w_ref, mem_ref, g_ref, dw_ref, dg_ref):
        dmn = jnp.zeros((m, D_MODEL), F32)
        mn = mn_ref[...]
        for j in range(4):
            dj = dkv_ref[:, j * 512:(j + 1) * 512].astype(BF16)
            dw_ref[j] = _dot_tn(mn, dj)
            dmn = dmn + _dot_nt(dj, w_ref[j])
        mv = mem_ref[...]
        dg_ref[...] = _rowsum(dmn * (mv * _rms_stats(mv)))

    return pl.pallas_call(
        body, name="mem_kv_bwd",
        out_shape=[jax.ShapeDtypeStruct((4, D_MODEL, 512), F32), jax.ShapeDtypeStruct((1, D_MODEL), F32)],
        compiler_params=pltpu.CompilerParams(vmem_limit_bytes=VMEM_LIMIT_BYTES),
    )(dkv, mn, wkv, mem, g)


def _seqmix_bwd(dh1, x, z, c1, w_out, w_in, g_mix, cw, lng, lnb, gg, gb, wpair, wpair_t, bias, t):
    s = x.shape[0]
    nt = s // t
    halo_blocks = t // CONV_HALO

    def body(dh1_ref, x_ref, z_ref, zh_ref, c1_ref, wo_ref, wi_ref, gm_ref, cw_ref, lng_ref, lnb_ref,
             gg_ref, gb_ref, wpair_ref, wpt_ref, bias_ref,
             gx_ref, dz_ref, dcw_ref, dcb_ref, dlng_ref, dlnb_ref, dgg_ref, dgb_ref, dws_ref, dbs_ref,
             dbin_ref, dgm_ref, abuf, dbuf, mixed_ref, dv_ref):
        i = pl.program_id(0)
        tile = nt - 1 - i
        accs = (dcw_ref, dcb_ref, dlng_ref, dlnb_ref, dgg_ref, dgb_ref, dws_ref, dbs_ref, dbin_ref, dgm_ref)

        @pl.when(i == 0)
        def _():
            for r in accs:
                r[...] = jnp.zeros_like(r)
            dbuf[t:t + CONV_HALO, :] = jnp.zeros((CONV_HALO, CONV_WIDTH), F32)

        @pl.when(i > 0)
        def _():
            dbuf[t:t + CONV_HALO, :] = dbuf[0:CONV_HALO, :]

        dmix = _dot_nt(dh1_ref[...].astype(BF16), wo_ref[...])

        xh, rs = _ln_stats(c1_ref[...])
        lng = lng_ref[...]
        ln = xh * lng + lnb_ref[...]
        sl = _sigmoid(ln)
        dln = dmix[:, 0:512] * (sl * (1.0 + ln * (1.0 - sl)))
        dc1, dg_ln, db_ln = _ln_bwd(dln, xh, rs, lng)
        dlng_ref[...] += dg_ln
        dlnb_ref[...] += db_ln
        dcb_ref[...] += _rowsum(dc1)
        dbuf[0:t, :] = dc1

        zh = zh_ref[...]
        a_halo = zh[:, 0:512] * _sigmoid(zh[:, 512:1024])
        abuf[0:CONV_HALO, :] = jnp.where(tile > 0, a_halo, 0.0)
        za = z_ref[:, 0:512]
        sg = _sigmoid(z_ref[:, 512:1024])
        abuf[CONV_HALO:, :] = za * sg

        da = jnp.zeros((t, CONV_WIDTH), F32)
        for k in range(CONV_KERNEL):
            da = da + cw_ref[k:k + 1, :] * dbuf[pl.ds(CONV_KERNEL - 1 - k, t), :]
            dcw_ref[k:k + 1, :] += _rowsum(dc1 * abuf[pl.ds(CONV_HALO - (CONV_KERNEL - 1) + k, t), :])
        dza = da * sg
        dzg = da * za * (sg * (1.0 - sg))
        dz_ref[:, 0:512] = dza.astype(BF16)
        dz_ref[:, 512:1024] = dzg.astype(BF16)
        dbin_ref[:, 0:512] += _rowsum(dza)
        dbin_ref[:, 512:1024] += _rowsum(dzg)

        dgm = dmix[:, 512:1024]
        u, du_dz = _gelu_parts(z_ref[:, 1024:1536])
        gv, dgv_dz = _gelu_parts(z_ref[:, 1536:2048])
        vxh, vrs = _ln_stats(gv)
        ggv = gg_ref[...]
        v = vxh * ggv + gb_ref[...]
        low = _lane_is_low_head()
        v_lo = jnp.where(low, v, 0.0).astype(BF16)
        v_hi = jnp.where(low, 0.0, v).astype(BF16)
        _gm_mix(v_lo, v_hi, wpair_ref, bias_ref, mixed_ref, t)
        dzu = dgm * mixed_ref[...] * du_dz
        dm = dgm * u
        dm_lo = jnp.where(low, dm, 0.0).astype(BF16)
        dm_hi = jnp.where(low, 0.0, dm).astype(BF16)
        vb = v.astype(BF16)
        tril = (lax.broadcasted_iota(jnp.int32, (CHUNK, CHUNK), 1)
                <= lax.broadcasted_iota(jnp.int32, (CHUNK, CHUNK), 0))
        for n in range(t // CHUNK):
            rows = slice(n * CHUNK, (n + 1) * CHUNK)
            dbs_ref[...] += dm[rows, :]
            for j in range(GM_HEADS // 2):
                cols = slice(j * LANES, (j + 1) * LANES)
                stack = jnp.concatenate([dm_lo[rows, cols], dm_hi[rows, cols]], axis=0)
                dws = _dot_nt(stack, vb[rows, cols])
                dws_ref[2 * j] += jnp.where(tril, dws[0:CHUNK], 0.0)
                dws_ref[2 * j + 1] += jnp.where(tril, dws[CHUNK:2 * CHUNK], 0.0)
                dv_ref[rows, cols] = _dot(wpt_ref[j], stack)
        dgv, dg_gm, db_gm = _ln_bwd(dv_ref[...], vxh, vrs, ggv)
        dgg_ref[...] += dg_gm
        dgb_ref[...] += db_gm
        dzv = dgv * dgv_dz
        dz_ref[:, 1024:1536] = dzu.astype(BF16)
        dz_ref[:, 1536:2048] = dzv.astype(BF16)
        dbin_ref[:, 1024:1536] += _rowsum(dzu)
        dbin_ref[:, 1536:2048] += _rowsum(dzv)

        dhn = jnp.zeros((t, D_MODEL), F32)
        for j in range(4):
            dhn = dhn + _dot_nt(dz_ref[:, j * 512:(j + 1) * 512], wi_ref[j])
        xv = x_ref[...]
        dv, dg = _rms_bwd(dhn, xv, _rms_stats(xv), gm_ref[...])
        gx_ref[...] = dh1_ref[...] + dv
        dgm_ref[...] += dg

    rev = lambda w: pl.BlockSpec((t, w), lambda i: (nt - 1 - i, 0))
    const = lambda *shape: pl.BlockSpec(shape, lambda i: (0,) * len(shape))
    halo = pl.BlockSpec((CONV_HALO, D_MODEL), lambda i: (jnp.maximum((nt - 1 - i) * halo_blocks - 1, 0), 0))
    f32 = lambda *shape: jax.ShapeDtypeStruct(shape, F32)
    return pl.pallas_call(
        body, name="seqmix_bwd", grid=(nt,),
        in_specs=[rev(D_MODEL), rev(D_MODEL), rev(2048), halo, rev(CONV_WIDTH),
                  const(D_MODEL, D_MODEL), const(4, D_MODEL, 512), const(1, D_MODEL),
                  const(CONV_HALO, CONV_WIDTH), const(1, 512), const(1, 512), const(1, 512), const(1, 512),
                  const(4, CHUNK, 2 * CHUNK), const(4, CHUNK, 2 * CHUNK), const(CHUNK, GM_WIDTH)],
        out_specs=[rev(D_MODEL), rev(2048),
                   const(CONV_HALO, CONV_WIDTH), const(1, 512), const(1, 512), const(1, 512), const(1, 512),
                   const(1, 512), const(GM_HEADS, CHUNK, CHUNK), const(CHUNK, GM_WIDTH), const(1, 2048),
                   const(1, D_MODEL)],
        out_shape=[f32(s, D_MODEL), jax.ShapeDtypeStruct((s, 2048), BF16),
                   f32(CONV_HALO, CONV_WIDTH), f32(1, 512), f32(1, 512), f32(1, 512), f32(1, 512),
                   f32(1, 512), f32(GM_HEADS, CHUNK, CHUNK), f32(CHUNK, GM_WIDTH), f32(1, 2048),
                   f32(1, D_MODEL)],
        scratch_shapes=[pltpu.VMEM((t + CONV_HALO, CONV_WIDTH), F32), pltpu.VMEM((t + CONV_HALO, CONV_WIDTH), F32),
                        pltpu.VMEM((t, GM_WIDTH), F32), pltpu.VMEM((t, GM_WIDTH), F32)],
        compiler_params=_params(("arbitrary",)),
    )(dh1, x, z, z, c1, w_out, w_in, g_mix, cw, lng, lnb, gg, gb, wpair, wpair_t, bias)


def _head_bias_grad(dbs):
    def body(d_ref, o_ref):
        dv = d_ref[...]
        lane = lax.broadcasted_iota(jnp.int32, (CHUNK, LANES), 1)
        acc = jnp.zeros((CHUNK, LANES), F32)
        for h in range(GM_HEADS):
            sh = jnp.sum(dv[:, h * GM_HEAD_DIM:(h + 1) * GM_HEAD_DIM], axis=-1, keepdims=True)
            acc = acc + jnp.where(lane == h, sh, 0.0)
        o_ref[...] = acc

    return pl.pallas_call(body, name="head_bias_grad",
                          out_shape=jax.ShapeDtypeStruct((CHUNK, LANES), F32))(dbs)


def _pack(arrays, rows):
    flat = jnp.concatenate([a.reshape(-1) for a in arrays])
    flat = jnp.pad(flat, (0, rows * LANES - flat.shape[0]))
    return flat.reshape(rows, LANES)


def _unpack(buf, shapes):
    flat = buf.reshape(-1)
    out, off = [], 0
    for shp in shapes:
        size = 1
        for d in shp:
            size *= d
        out.append(flat[off:off + size].reshape(shp))
        off += size
    return out


def _rows_for(shapes, multiple):
    total = 0
    for shp in shapes:
        size = 1
        for d in shp:
            size *= d
        total += size
    rows = -(-total // LANES)
    return -(-rows // multiple) * multiple


def kernel(x, mem, norm_mix_g, w_in, b_in, conv_w, conv_b, conv_ln_g, conv_ln_b, gm_ln_g, gm_ln_b, gm_w_s, gm_b_s, w_out, norm_xa_g, mem_norm_g, xa_wq, xa_wkv, xa_wo, norm_ffn_g, ffn_w_gate_up, ffn_w_down, final_norm_g, loss_target, m_norm_mix_g, m_w_in, m_b_in, m_conv_w, m_conv_b, m_conv_ln_g, m_conv_ln_b, m_gm_ln_g, m_gm_ln_b, m_gm_w_s, m_gm_b_s, m_w_out, m_norm_xa_g, m_mem_norm_g, m_xa_wq, m_xa_wkv, m_xa_wo, m_norm_ffn_g, m_ffn_w_gate_up, m_ffn_w_down, m_final_norm_g, v_norm_mix_g, v_w_in, v_b_in, v_conv_w, v_conv_b, v_conv_ln_g, v_conv_ln_b, v_gm_ln_g, v_gm_ln_b, v_gm_w_s, v_gm_b_s, v_w_out, v_norm_xa_g, v_mem_norm_g, v_xa_wq, v_xa_wkv, v_xa_wo, v_norm_ffn_g, v_ffn_w_gate_up, v_ffn_w_down, v_final_norm_g):
    s = x.shape[1]
    ts = _row_tile(s)
    tb = max(CHUNK, ts // 2)
    cx, cy, cc = _mesh_pos()
    chip = 2 * cx + cy
    pos = jnp.stack([chip, cc]).astype(jnp.int32)
    row = lambda a: a.reshape(1, -1)
    x2, mem2, tgt2 = x[0], mem[0], loss_target[0]

    big = dict(w_in=w_in, xa_wkv=xa_wkv, w_out=w_out, xa_wq=xa_wq, xa_wo=xa_wo,
               ffn_w_gate_up=ffn_w_gate_up, ffn_w_down=ffn_w_down)
    big_names = list(big)
    halves = lambda a: a.reshape(2, a.shape[0] // 2, a.shape[1])
    cast = {nm: _cast_into_slot(halves(big[nm]), pos, BF16, "cast_" + nm) for nm in big_names}
    conv_w_pad = jnp.pad(conv_w, ((0, CONV_HALO - CONV_KERNEL), (0, 0)))
    first = _gather_shards([cast["w_in"], _cast_into_slot(halves(conv_w_pad), pos, F32, "slot_conv_w")])
    attn_names = ["w_out", "xa_wq", "xa_wkv", "xa_wo"]
    ffn_names = ["ffn_w_gate_up", "ffn_w_down"]
    send_sems, recv_sems, later, gather_token = _gather_start(
        [cast[nm] for nm in attn_names + ffn_names], "gather_start")
    n_attn = 3 * len(attn_names)
    shaped = lambda buf, nm: buf.reshape(N_CHIPS, big[nm].shape[0], big[nm].shape[1])
    w_in_g = _after(shaped(first[0], "w_in"), gather_token)
    cw_g = jnp.concatenate([first[1][k].reshape(CONV_HALO, LANES) for k in range(N_CHIPS)], axis=1)

    tril = jnp.tril(jnp.ones((CHUNK, CHUNK), dtype=bool))
    ws = jnp.where(tril[None], gm_w_s, 0.0)
    wpair = jnp.concatenate([ws[0::2], ws[1::2]], axis=2).astype(BF16)
    ws_t = jnp.swapaxes(ws, 1, 2)
    wpair_t = jnp.concatenate([ws_t[0::2], ws_t[1::2]], axis=2).astype(BF16)
    bias = jnp.repeat(gm_b_s.T, GM_HEAD_DIM, axis=1)

    z, hn1 = _mix_in(x2, row(norm_mix_g), w_in_g, row(b_in), ts)
    mix, c1 = _seqmix_fwd(z, cw_g, row(conv_b), row(conv_ln_g), row(conv_ln_b), row(gm_ln_g), row(gm_ln_b),
                          wpair, bias, ts)
    landed = _gather_wait(send_sems[:n_attn], recv_sems[:n_attn], later[:len(attn_names)], mix, "gather_wait_attn")
    gw = {nm: shaped(b, nm) for nm, b in zip(attn_names, _pass_to_sibling(landed, "pass_attn"))}
    w_out_g = gw["w_out"].reshape(D_MODEL, D_MODEL)
    wq_g = gw["xa_wq"].reshape(D_MODEL, D_MODEL)
    wkv_g = gw["xa_wkv"]
    wo_g = gw["xa_wo"].reshape(D_MODEL, D_MODEL)
    h1, hn2, q = _out_proj_q(x2, mix, w_out_g, row(norm_xa_g), wq_g, ts)
    mn, kv = _mem_kv(mem2, row(mem_norm_g), wkv_g)
    o, h2, hn3 = _attn_fwd(q, kv, h1, wo_g, row(norm_ffn_g), ts)
    landed = _gather_wait(send_sems[n_attn:], recv_sems[n_attn:], later[len(attn_names):], hn3, "gather_wait_ffn")
    gw = {nm: shaped(b, nm) for nm, b in zip(ffn_names, _pass_to_sibling(landed, "pass_ffn"))}
    wgu_g = gw["ffn_w_gate_up"]
    wd_g = gw["ffn_w_down"].reshape(FFN_HIDDEN, D_MODEL)
    gu, act = _ffn_up(hn3, wgu_g.reshape(2, 2, D_MODEL, FFN_HALF), ts)
    dh3, sq, d_final_g = _ffn_down_loss(act, wd_g, h2, row(final_norm_g), tgt2, ts)
    loss = lax.psum(0.5 * jnp.sum(sq) / D_MODEL, ("x", "y", "c"))

    def split(g, nm):
        r, c = big[nm].shape
        return g.reshape(N_CHIPS, 2, r // 2, c)

    def chip_sums_of(group, arrays):
        got = _swap_halves(arrays, "swap_halves_" + group[0])
        both = [_add_half(g, r, pos, "chip_sum_" + nm) for g, r, nm in zip(arrays, got, group)]
        return [b[0] for b in both], [b[1] for b in both]

    def start_exchange(group, arrays):
        sums, parts = chip_sums_of(group, arrays)
        return _exchange_start(sums, parts, "exchange_start_" + group[0])

    def finish_exchange(group, started, after):
        sems, sums, parts, _ = started
        parts = _exchange_wait(sems, sums, parts, after, "exchange_wait_" + group[0])
        return [_sum_chips(p, pos, "total_" + nm) for p, nm in zip(parts, group)]

    as3 = lambda a: a.reshape((1,) + a.shape)
    halves_of = {}

    dgu = _ffn_bwd_act(dh3, wd_g.reshape(2, FFN_HALF, D_MODEL), gu, ts)
    g_down = _grad_w(act, as3(dh3), FFN_HALF, D_MODEL, "grad_ffn_w_down")
    group_a = ["ffn_w_down"]
    started_a = start_exchange(group_a, [split(g_down, "ffn_w_down")])

    dgu = _after(dgu, started_a[3])
    dh2, d_ffn_g = _ffn_bwd_in(dgu, wgu_g, dh3, h2, row(norm_ffn_g), tb)
    g_gu = _grad_w(hn3, dgu, 512, FFN_HALF, "grad_ffn_w_gate_up")
    halves_of.update(zip(group_a, finish_exchange(group_a, started_a, g_gu)))
    group_b = ["ffn_w_gate_up"]
    started_b = start_exchange(group_b, [split(g_gu, "ffn_w_gate_up")])

    dh2 = _after(dh2, started_b[3])
    dh1, dq, dkv, d_xa_g = _attn_bwd(dh2, wo_g, q, kv, wq_g, h1, row(norm_xa_g), ts)
    g_wkv, d_mem_g = _mem_kv_bwd(dkv, mn, wkv_g, mem2, row(mem_norm_g))
    g_wo = _grad_w(o, as3(dh2), 512, D_MODEL, "grad_xa_wo")
    g_wq = _grad_w(hn2, as3(dq), 512, D_MODEL, "grad_xa_wq")
    g_wout = _grad_w(mix, as3(dh1), 512, D_MODEL, "grad_w_out")
    halves_of.update(zip(group_b, finish_exchange(group_b, started_b, g_wout)))
    group_c = ["xa_wo", "xa_wq", "xa_wkv", "w_out"]
    started_c = start_exchange(group_c, [split(g, nm) for g, nm in zip((g_wo, g_wq, g_wkv, g_wout), group_c)])

    dh1 = _after(dh1, started_c[3])
    (gx, dz, d_cw, d_cb, d_lng, d_lnb, d_gg, d_gb, d_ws, d_bs_sum, d_bin, d_mix_g) = _seqmix_bwd(
        dh1, x2, z, c1, w_out_g, w_in_g, row(norm_mix_g), cw_g, row(conv_ln_g), row(conv_ln_b),
        row(gm_ln_g), row(gm_ln_b), wpair, wpair_t, bias, tb)
    d_bs = _head_bias_grad(d_bs_sum)[:, :GM_HEADS].T
    g_win = _grad_w(hn1, as3(dz), 512, 512, "grad_w_in")
    halves_of.update(zip(group_c, finish_exchange(group_c, started_c, g_win)))

    small_names = ["norm_mix_g", "b_in", "conv_w", "conv_b", "conv_ln_g", "conv_ln_b", "gm_ln_g", "gm_ln_b",
                   "gm_w_s", "gm_b_s", "norm_xa_g", "mem_norm_g", "norm_ffn_g", "final_norm_g"]
    small_grads = dict(norm_mix_g=d_mix_g, b_in=d_bin, conv_w=d_cw[:CONV_KERNEL], conv_b=d_cb, conv_ln_g=d_lng,
                       conv_ln_b=d_lnb, gm_ln_g=d_gg, gm_ln_b=d_gb, gm_w_s=d_ws, gm_b_s=d_bs, norm_xa_g=d_xa_g,
                       mem_norm_g=d_mem_g, norm_ffn_g=d_ffn_g, final_norm_g=d_final_g)
    full_shapes = dict(norm_mix_g=(D_MODEL,), b_in=(2048,), conv_w=(CONV_KERNEL, CONV_WIDTH), conv_b=(512,),
                       conv_ln_g=(512,), conv_ln_b=(512,), gm_ln_g=(512,), gm_ln_b=(512,),
                       gm_w_s=(GM_HEADS, CHUNK, CHUNK), gm_b_s=(GM_HEADS, CHUNK), norm_xa_g=(D_MODEL,),
                       mem_norm_g=(D_MODEL,), norm_ffn_g=(D_MODEL,), final_norm_g=(D_MODEL,))
    pack_rows = _rows_for([full_shapes[nm] for nm in small_names], 32)
    small_pack = _pack([small_grads[nm] for nm in small_names], pack_rows)

    group_d = ["w_in", "small"]
    sums_d, parts_d = chip_sums_of(group_d, [split(g_win, "w_in"),
                                             small_pack.reshape(1, 2, pack_rows // 2, LANES)])
    parts_d = _exchange_chip_sums(sums_d, parts_d)
    halves_of.update(zip(group_d, [_sum_chips(p, pos, "total_" + nm) for p, nm in zip(parts_d, group_d)]))

    red_names = big_names + ["small"]
    joined = _join_halves([halves_of[nm] for nm in red_names])
    grads = {nm: joined[i].reshape(big[nm].shape) for i, nm in enumerate(big_names)}
    small_red = _unpack(joined[-1].reshape(pack_rows, LANES), [full_shapes[nm] for nm in small_names])
    for nm, g in zip(small_names, small_red):
        grads[nm] = g
    grads["conv_w"] = lax.dynamic_slice(grads["conv_w"], (0, chip * LANES), (CONV_KERNEL, LANES))

    weights = dict(norm_mix_g=norm_mix_g, w_in=w_in, b_in=b_in, conv_w=conv_w, conv_b=conv_b, conv_ln_g=conv_ln_g,
                   conv_ln_b=conv_ln_b, gm_ln_g=gm_ln_g, gm_ln_b=gm_ln_b, gm_w_s=gm_w_s, gm_b_s=gm_b_s, w_out=w_out,
                   norm_xa_g=norm_xa_g, mem_norm_g=mem_norm_g, xa_wq=xa_wq, xa_wkv=xa_wkv, xa_wo=xa_wo,
                   norm_ffn_g=norm_ffn_g, ffn_w_gate_up=ffn_w_gate_up, ffn_w_down=ffn_w_down,
                   final_norm_g=final_norm_g)
    m_in = dict(norm_mix_g=m_norm_mix_g, w_in=m_w_in, b_in=m_b_in, conv_w=m_conv_w, conv_b=m_conv_b,
                conv_ln_g=m_conv_ln_g, conv_ln_b=m_conv_ln_b, gm_ln_g=m_gm_ln_g, gm_ln_b=m_gm_ln_b, gm_w_s=m_gm_w_s,
                gm_b_s=m_gm_b_s, w_out=m_w_out, norm_xa_g=m_norm_xa_g, mem_norm_g=m_mem_norm_g, xa_wq=m_xa_wq,
                xa_wkv=m_xa_wkv, xa_wo=m_xa_wo, norm_ffn_g=m_norm_ffn_g, ffn_w_gate_up=m_ffn_w_gate_up,
                ffn_w_down=m_ffn_w_down, final_norm_g=m_final_norm_g)
    v_in = dict(norm_mix_g=v_norm_mix_g, w_in=v_w_in, b_in=v_b_in, conv_w=v_conv_w, conv_b=v_conv_b,
                conv_ln_g=v_conv_ln_g, conv_ln_b=v_conv_ln_b, gm_ln_g=v_gm_ln_g, gm_ln_b=v_gm_ln_b, gm_w_s=v_gm_w_s,
                gm_b_s=v_gm_b_s, w_out=v_w_out, norm_xa_g=v_norm_xa_g, mem_norm_g=v_mem_norm_g, xa_wq=v_xa_wq,
                xa_wkv=v_xa_wkv, xa_wo=v_xa_wo, norm_ffn_g=v_norm_ffn_g, ffn_w_gate_up=v_ffn_w_gate_up,
                ffn_w_down=v_ffn_w_down, final_norm_g=v_final_norm_g)
    delta, new_m, new_v = {}, {}, {}
    for nm in big_names:
        delta[nm], new_m[nm], new_v[nm] = _adamw(weights[nm], grads[nm], m_in[nm], v_in[nm], "adamw_" + nm)
    local_shapes = [weights[nm].shape for nm in small_names]
    adam_rows = _rows_for(local_shapes, LANES)
    packed = [_pack([src[nm] for nm in small_names], adam_rows) for src in (weights, grads, m_in, v_in)]
    outs = _adamw(*packed, "adamw_small")
    for dst, buf in zip((delta, new_m, new_v), outs):
        for nm, a in zip(small_names, _unpack(buf, local_shapes)):
            dst[nm] = a

    order = ["norm_mix_g", "w_in", "b_in", "conv_w", "conv_b", "conv_ln_g", "conv_ln_b", "gm_ln_g", "gm_ln_b",
             "gm_w_s", "gm_b_s", "w_out", "norm_xa_g", "mem_norm_g", "xa_wq", "xa_wkv", "xa_wo", "norm_ffn_g",
             "ffn_w_gate_up", "ffn_w_down", "final_norm_g"]
    fit = lambda a, nm: a.reshape(weights[nm].shape)
    return (loss, gx.reshape(x.shape),
            *[fit(grads[nm], nm) for nm in order], *[fit(delta[nm], nm) for nm in order],
            *[fit(new_m[nm], nm) for nm in order], *[fit(new_v[nm], nm) for nm in order])
```

```python
import functools

import jax
import jax.numpy as jnp
from jax import lax
from jax.experimental import pallas as pl
from jax.experimental.pallas import tpu as pltpu

F32 = jnp.float32
BF16 = jnp.bfloat16

D_MODEL = 1024
CONV_WIDTH = 512
GM_WIDTH = 512
CONV_KERNEL = 31
CONV_HALO = 32
CHUNK = 128
GM_HEADS = 8
GM_HEAD_DIM = 64
XA_HEADS = 4
XA_HEAD_DIM = 256
FFN_HIDDEN = 2816
FFN_HALF = FFN_HIDDEN // 2
RMS_EPS = 1e-6
LN_EPS = 1e-5
N_CHIPS = 4
LANES = 128

ADAM_LR = 0.001
ADAM_B1 = 0.9
ADAM_B2 = 0.999
ADAM_EPS = 1e-08
ADAM_WD = 0.01
ADAM_STEP = 10

VMEM_LIMIT_BYTES = 56 * 1024 * 1024
MESH = pl.DeviceIdType.MESH
ANY = pl.BlockSpec(memory_space=pl.ANY)
HBM_SPEC = pl.BlockSpec(memory_space=pltpu.HBM)
SEM_SPEC = pl.BlockSpec(memory_space=pltpu.SEMAPHORE)

_NT = (((1,), (1,)), ((), ()))
_TN = (((0,), (0,)), ((), ()))
_GELU_C = 0.7978845608028654
_GELU_A = 0.044715


def _dot(a, b):
    return jnp.dot(a, b, preferred_element_type=F32)


def _dot_nt(a, b):
    return lax.dot_general(a, b, _NT, preferred_element_type=F32)


def _dot_tn(a, b):
    return lax.dot_general(a, b, _TN, preferred_element_type=F32)


def _mean(v):
    return jnp.mean(v, axis=-1, keepdims=True)


def _rowsum(v):
    return jnp.sum(v, axis=0, keepdims=True)


def _sigmoid(v):
    return 1.0 / (1.0 + jnp.exp(-v))


def _gelu_parts(v):
    v2 = v * v
    t = jnp.tanh(_GELU_C * (v + _GELU_A * v * v2))
    g = 0.5 * v * (1.0 + t)
    dg = 0.5 * (1.0 + t) + 0.5 * v * (1.0 - t * t) * (_GELU_C * (1.0 + 3.0 * _GELU_A * v2))
    return g, dg


def _rms_stats(v):
    return lax.rsqrt(_mean(v * v) + RMS_EPS)


def _rms_bwd(dy, v, r, g):
    n = v * r
    dn = dy * g
    dv = r * (dn - n * _mean(dn * n))
    return dv, _rowsum(dy * n)


def _ln_stats(v):
    mu = _mean(v)
    xc = v - mu
    rs = lax.rsqrt(_mean(xc * xc) + LN_EPS)
    return xc * rs, rs


def _ln_bwd(dy, xh, rs, g):
    dxh = dy * g
    dv = rs * (dxh - _mean(dxh) - xh * _mean(dxh * xh))
    return dv, _rowsum(dy * xh), _rowsum(dy)


def _params(sem):
    return pltpu.CompilerParams(dimension_semantics=sem, vmem_limit_bytes=VMEM_LIMIT_BYTES)


def _row_tile(s):
    return 512 if s % 512 == 0 and s >= 2048 else 128


def _mesh_pos():
    return lax.axis_index("x"), lax.axis_index("y"), lax.axis_index("c")


def _cast_into_slot(w, pos, dtype, name):
    _, h, c = w.shape

    def body(pos_ref, w_ref, o_ref):
        o_ref[0] = w_ref[...].astype(dtype)

    return pl.pallas_call(
        body, name=name,
        grid_spec=pltpu.PrefetchScalarGridSpec(
            num_scalar_prefetch=1, grid=(2,),
            in_specs=[pl.BlockSpec((1, h, c), lambda i, p: (i, 0, 0))],
            out_specs=pl.BlockSpec((1, 1, h, c), lambda i, p: (p[0], i, 0, 0))),
        out_shape=jax.ShapeDtypeStruct((N_CHIPS, 2, h, c), dtype),
        compiler_params=_params(("parallel",)),
    )(pos, w)


def _adam_rows(r, c):
    for tr in (r, 1024, 704, 640, 512, 352, 320, 256, 128, 64, 32, 16, 8):
        if r % tr == 0 and tr * c * 4 <= (3 << 19):
            return tr
    return r


def _adamw(w, g, m, v, name):
    r, c = w.shape
    tr = _adam_rows(r, c)

    def body(w_ref, g_ref, m_ref, v_ref, d_ref, nm_ref, nv_ref):
        gv = g_ref[...]
        nm = ADAM_B1 * m_ref[...] + (1.0 - ADAM_B1) * gv
        nv = ADAM_B2 * v_ref[...] + (1.0 - ADAM_B2) * (gv * gv)
        m_hat = nm / (1.0 - ADAM_B1 ** ADAM_STEP)
        v_hat = nv / (1.0 - ADAM_B2 ** ADAM_STEP)
        d_ref[...] = -ADAM_LR * (m_hat / (jnp.sqrt(v_hat) + ADAM_EPS) + ADAM_WD * w_ref[...])
        nm_ref[...] = nm
        nv_ref[...] = nv

    spec = pl.BlockSpec((tr, c), lambda i: (i, 0))
    shp = jax.ShapeDtypeStruct((r, c), F32)
    return pl.pallas_call(
        body, name=name, grid=(r // tr,),
        in_specs=[spec] * 4, out_specs=[spec] * 3, out_shape=[shp] * 3,
        compiler_params=_params(("parallel",)),
    )(w, g, m, v)


def _as_tuple(after):
    return tuple(after) if isinstance(after, (tuple, list)) else (after,)


def _other_chips(x, y):
    return [(1 - x, y), (x, 1 - y), (1 - x, 1 - y)]


def _gather_shards(bufs):
    n = len(bufs)

    def body(*refs):
        outs = refs[n:2 * n]
        ici_send, ici_recv, fwd_send, fwd_recv = refs[2 * n:]
        x, y, c = _mesh_pos()
        me = 2 * x + y
        sib = (x, y, 1 - c)
        chips = _other_chips(x, y)

        def ici(a, k, chip_idx, to):
            return pltpu.make_async_remote_copy(
                src_ref=outs[a].at[chip_idx, c], dst_ref=outs[a].at[chip_idx, c],
                send_sem=ici_send.at[a, k], recv_sem=ici_recv.at[a, k],
                device_id=to, device_id_type=MESH)

        def fwd(a, k, chip_idx, half):
            return pltpu.make_async_remote_copy(
                src_ref=outs[a].at[chip_idx, half], dst_ref=outs[a].at[chip_idx, half],
                send_sem=fwd_send.at[a, k], recv_sem=fwd_recv.at[a, k],
                device_id=sib, device_id_type=MESH)

        sends = [ici(a, k, me, (*chips[k], c)) for a in range(n) for k in range(3)]
        for cp in sends:
            cp.start()
        passed = []
        for a in range(n):
            for k in range(3):
                ck = 2 * chips[k][0] + chips[k][1]
                ici(a, k, ck, (*chips[k], c)).wait_recv()
                cp = fwd(a, k, ck, c)
                cp.start()
                passed.append(cp)
        for a in range(n):
            for k in range(3):
                ck = 2 * chips[k][0] + chips[k][1]
                fwd(a, k, ck, 1 - c).wait_recv()
        for cp in sends + passed:
            cp.wait_send()

    out_shape = [jax.ShapeDtypeStruct(b.shape, b.dtype) for b in bufs]
    return pl.pallas_call(
        body, name="gather_weights",
        in_specs=[ANY] * n, out_specs=[ANY] * n, out_shape=out_shape,
        input_output_aliases={a: a for a in range(n)},
        scratch_shapes=[pltpu.SemaphoreType.DMA((n, 3))] * 4,
    )(*bufs)


def _gather_descriptors(bufs, send_of, recv_of):
    x, y, c = _mesh_pos()
    me = 2 * x + y
    chips = _other_chips(x, y)
    sends, arrivals = [], []
    for a in range(len(bufs)):
        for k in range(3):
            ck = 2 * chips[k][0] + chips[k][1]

            def copy(slot, a=a, k=k):
                return pltpu.make_async_remote_copy(
                    src_ref=bufs[a].at[slot, c], dst_ref=bufs[a].at[slot, c],
                    send_sem=send_of(a, k), recv_sem=recv_of(a, k),
                    device_id=(*chips[k], c), device_id_type=MESH)

            sends.append(functools.partial(copy, me))
            arrivals.append(functools.partial(copy, ck))
    return sends, arrivals


def _gather_start(bufs, name):
    n = len(bufs)
    ns = 3 * n

    def body(*refs):
        sems = refs[n:n + 2 * ns]
        thru = refs[n + 2 * ns:2 * n + 2 * ns]
        token = refs[2 * n + 2 * ns]
        sends, _ = _gather_descriptors(thru, lambda a, k: sems[3 * a + k], lambda a, k: sems[ns + 3 * a + k])
        for cp in sends:
            cp().start()
        token[...] = jnp.zeros_like(token)

    held = [pltpu.with_memory_space_constraint(b, pltpu.HBM) for b in bufs]
    out = pl.pallas_call(
        body, name=name,
        out_shape=(*[pltpu.SemaphoreType.DMA(())] * (2 * ns), *[pltpu.HBM(b.shape, b.dtype) for b in held],
                   jax.ShapeDtypeStruct((8, LANES), F32)),
        in_specs=[HBM_SPEC] * n,
        out_specs=(*[SEM_SPEC] * (2 * ns), *[HBM_SPEC] * n, pl.BlockSpec(memory_space=pltpu.VMEM)),
        input_output_aliases={i: 2 * ns + i for i in range(n)},
        compiler_params=pltpu.CompilerParams(has_side_effects=pltpu.SideEffectType.DATAFLOW_SIDE_EFFECTING),
    )(*held)
    return list(out[:ns]), list(out[ns:2 * ns]), list(out[2 * ns:2 * ns + n]), out[2 * ns + n]


def _gather_wait(send_sems, recv_sems, bufs, after, name):
    n = len(bufs)
    ns = 3 * n

    def body(*refs):
        buf_ref = refs[:n]
        sem_ref = refs[n:n + 2 * ns]
        sends, arrivals = _gather_descriptors(buf_ref, lambda a, k: sem_ref[3 * a + k],
                                              lambda a, k: sem_ref[ns + 3 * a + k])
        for cp in sends:
            cp().wait_send()
        for cp in arrivals:
            cp().wait_recv()

    out = pl.pallas_call(
        body, name=name,
        out_shape=tuple(pltpu.HBM(b.shape, b.dtype) for b in bufs),
        in_specs=[HBM_SPEC] * n + [SEM_SPEC] * (2 * ns) + [ANY],
        out_specs=tuple([HBM_SPEC] * n),
        input_output_aliases={i: i for i in range(n)},
        compiler_params=pltpu.CompilerParams(has_side_effects=pltpu.SideEffectType.DATAFLOW_SIDE_EFFECTING),
    )(*bufs, *send_sems, *recv_sems, after)
    return list(out)


def _pass_to_sibling(bufs, name):
    n = len(bufs)

    def body(*refs):
        outs = refs[n:2 * n]
        send_sem, recv_sem = refs[2 * n:]
        x, y, c = _mesh_pos()
        chips = _other_chips(x, y)

        def half(a, k, which):
            ck = 2 * chips[k][0] + chips[k][1]
            return pltpu.make_async_remote_copy(
                src_ref=outs[a].at[ck, which], dst_ref=outs[a].at[ck, which],
                send_sem=send_sem.at[a, k], recv_sem=recv_sem.at[a, k],
                device_id=(x, y, 1 - c), device_id_type=MESH)

        sends = [half(a, k, c) for a in range(n) for k in range(3)]
        for cp in sends:
            cp.start()
        for a in range(n):
            for k in range(3):
                half(a, k, 1 - c).wait_recv()
        for cp in sends:
            cp.wait_send()

    return pl.pallas_call(
        body, name=name,
        in_specs=[ANY] * n, out_specs=[ANY] * n,
        out_shape=[jax.ShapeDtypeStruct(b.shape, b.dtype) for b in bufs],
        input_output_aliases={a: a for a in range(n)},
        scratch_shapes=[pltpu.SemaphoreType.DMA((n, 3))] * 2,
    )(*bufs)


def _swap_halves(grads, name):
    n = len(grads)

    def body(*refs):
        ins, outs = refs[:n], refs[n:2 * n]
        send_sem, recv_sem = refs[2 * n:]
        x, y, c = _mesh_pos()
        cps = [pltpu.make_async_remote_copy(
            src_ref=ins[a].at[:, pl.ds(1 - c, 1)], dst_ref=outs[a],
            send_sem=send_sem.at[a], recv_sem=recv_sem.at[a],
            device_id=(x, y, 1 - c), device_id_type=MESH) for a in range(n)]
        for cp in cps:
            cp.start()
        for cp in cps:
            cp.wait()

    out_shape = [jax.ShapeDtypeStruct((g.shape[0], 1) + g.shape[2:], g.dtype) for g in grads]
    return pl.pallas_call(
        body, name=name,
        in_specs=[ANY] * n, out_specs=[ANY] * n, out_shape=out_shape,
        scratch_shapes=[pltpu.SemaphoreType.DMA((n,))] * 2,
    )(*grads)


def _swap_descriptors(grads, lands, send_of, recv_of):
    x, y, c = _mesh_pos()
    return [functools.partial(
        pltpu.make_async_remote_copy,
        src_ref=grads[a].at[:, pl.ds(1 - c, 1)], dst_ref=lands[a],
        send_sem=send_of(a), recv_sem=recv_of(a),
        device_id=(x, y, 1 - c), device_id_type=MESH) for a in range(len(grads))]


def _swap_start(grads, name):
    n = len(grads)

    def body(*refs):
        sems = refs[2 * n:4 * n]
        g_thru, l_thru = refs[4 * n:5 * n], refs[5 * n:6 * n]
        token = refs[6 * n]
        for cp in _swap_descriptors(g_thru, l_thru, lambda a: sems[a], lambda a: sems[n + a]):
            cp().start()
        token[...] = jnp.zeros_like(token)

    lands = [lax.empty((g.shape[0], 1) + g.shape[2:], g.dtype) for g in grads]
    held = [pltpu.with_memory_space_constraint(a, pltpu.HBM) for a in (*grads, *lands)]
    out = pl.pallas_call(
        body, name=name,
        out_shape=(*[pltpu.SemaphoreType.DMA(())] * (2 * n), *[pltpu.HBM(a.shape, a.dtype) for a in held],
                   jax.ShapeDtypeStruct((8, LANES), F32)),
        in_specs=[HBM_SPEC] * (2 * n),
        out_specs=(*[SEM_SPEC] * (2 * n), *[HBM_SPEC] * (2 * n), pl.BlockSpec(memory_space=pltpu.VMEM)),
        input_output_aliases={i: 2 * n + i for i in range(2 * n)},
        compiler_params=pltpu.CompilerParams(has_side_effects=pltpu.SideEffectType.DATAFLOW_SIDE_EFFECTING),
    )(*held)
    return list(out[:2 * n]), list(out[2 * n:3 * n]), list(out[3 * n:4 * n]), out[4 * n]


def _swap_wait(sems, grads, lands, after, name):
    n = len(grads)

    def body(*refs):
        g_ref, l_ref = refs[:n], refs[n:2 * n]
        sem_ref = refs[2 * n:4 * n]
        for cp in _swap_descriptors(g_ref, l_ref, lambda a: sem_ref[a], lambda a: sem_ref[n + a]):
            cp().wait()

    out = pl.pallas_call(
        body, name=name,
        out_shape=tuple(pltpu.HBM(a.shape, a.dtype) for a in (*grads, *lands)),
        in_specs=[HBM_SPEC] * (2 * n) + [SEM_SPEC] * (2 * n) + [ANY],
        out_specs=tuple([HBM_SPEC] * (2 * n)),
        input_output_aliases={i: i for i in range(2 * n)},
        compiler_params=pltpu.CompilerParams(has_side_effects=pltpu.SideEffectType.DATAFLOW_SIDE_EFFECTING),
    )(*grads, *lands, *sems, after)
    return list(out[:n]), list(out[n:])


def _add_half(g, got, pos, name):
    j, _, h, c = g.shape

    def body(pos_ref, g_ref, r_ref, o_ref, p_ref):
        val = (g_ref[0, 0] + r_ref[0, 0]).astype(BF16)
        o_ref[0] = val
        if j == 1:
            p_ref[0] = val
        else:
            @pl.when(pl.program_id(0) == pos_ref[0])
            def _():
                p_ref[0] = val

    return pl.pallas_call(
        body, name=name,
        grid_spec=pltpu.PrefetchScalarGridSpec(
            num_scalar_prefetch=1, grid=(j,),
            in_specs=[pl.BlockSpec((1, 1, h, c), lambda i, p: (i, p[1], 0, 0)),
                      pl.BlockSpec((1, 1, h, c), lambda i, p: (i, 0, 0, 0))],
            out_specs=[pl.BlockSpec((1, h, c), lambda i, p: (i, 0, 0)),
                       pl.BlockSpec((1, h, c), lambda i, p: (p[0], 0, 0))]),
        out_shape=[jax.ShapeDtypeStruct((j, h, c), BF16), jax.ShapeDtypeStruct((N_CHIPS, h, c), BF16)],
        compiler_params=_params(("arbitrary",)),
    )(pos, g, got)


def _exchange_chip_sums(sums, parts):
    n = len(sums)

    def body(*refs):
        ins, outs = refs[:n], refs[2 * n:3 * n]
        send_sem, recv_sem = refs[3 * n:]
        sends, arrivals = _exchange_descriptors(ins, outs, lambda a, k: send_sem.at[a, k],
                                                lambda a, k: recv_sem.at[a, k])
        for cp in sends:
            cp().start()
        for cp in arrivals:
            cp().wait_recv()
        for cp in sends:
            cp().wait_send()

    out_shape = [jax.ShapeDtypeStruct(p.shape, p.dtype) for p in parts]
    return pl.pallas_call(
        body, name="exchange_chip_sums",
        in_specs=[ANY] * (2 * n), out_specs=[ANY] * n, out_shape=out_shape,
        input_output_aliases={n + a: a for a in range(n)},
        scratch_shapes=[pltpu.SemaphoreType.DMA((n, 3))] * 2,
    )(*sums, *parts)


def _exchange_descriptors(sums, parts, send_of, recv_of):
    x, y, c = _mesh_pos()
    me = 2 * x + y
    chips = _other_chips(x, y)
    sends, arrivals = [], []
    for a in range(len(sums)):
        for k in range(3):
            ck = 2 * chips[k][0] + chips[k][1]
            mine = sums[a].at[ck] if sums[a].shape[0] == N_CHIPS else sums[a].at[0]

            def copy(dst_slot, a=a, k=k, mine=mine):
                return pltpu.make_async_remote_copy(
                    src_ref=mine, dst_ref=parts[a].at[dst_slot],
                    send_sem=send_of(a, k), recv_sem=recv_of(a, k),
                    device_id=(*chips[k], c), device_id_type=MESH)

            sends.append(functools.partial(copy, me))
            arrivals.append(functools.partial(copy, ck))
    return sends, arrivals


def _exchange_start(sums, parts, name):
    n = len(sums)
    ns = 3 * n

    def body(*refs):
        sems = refs[2 * n:2 * n + 2 * ns]
        sums_thru = refs[2 * n + 2 * ns:3 * n + 2 * ns]
        parts_thru = refs[3 * n + 2 * ns:4 * n + 2 * ns]
        token = refs[4 * n + 2 * ns]
        sends, _ = _exchange_descriptors(sums_thru, parts_thru, lambda a, k: sems[3 * a + k],
                                         lambda a, k: sems[ns + 3 * a + k])
        for cp in sends:
            cp().start()
        token[...] = jnp.zeros_like(token)

    hbm = lambda a: pltpu.HBM(a.shape, a.dtype)
    held = [pltpu.with_memory_space_constraint(a, pltpu.HBM) for a in (*sums, *parts)]
    out = pl.pallas_call(
        body, name=name,
        out_shape=(*[pltpu.SemaphoreType.DMA(())] * (2 * ns), *[hbm(a) for a in held],
                   jax.ShapeDtypeStruct((8, LANES), F32)),
        in_specs=[HBM_SPEC] * (2 * n),
        out_specs=(*[SEM_SPEC] * (2 * ns), *[HBM_SPEC] * (2 * n), pl.BlockSpec(memory_space=pltpu.VMEM)),
        input_output_aliases={i: 2 * ns + i for i in range(2 * n)},
        compiler_params=pltpu.CompilerParams(has_side_effects=pltpu.SideEffectType.DATAFLOW_SIDE_EFFECTING),
    )(*held)
    return (list(out[:2 * ns]), list(out[2 * ns:2 * ns + n]), list(out[2 * ns + n:2 * ns + 2 * n]),
            out[2 * ns + 2 * n])


def _exchange_wait(sems, sums, parts, after, name):
    n = len(sums)
    ns = 3 * n

    def body(*refs):
        sums_ref, parts_ref = refs[:n], refs[n:2 * n]
        sem_ref = refs[2 * n:2 * n + 2 * ns]
        sends, arrivals = _exchange_descriptors(sums_ref, parts_ref, lambda a, k: sem_ref[3 * a + k],
                                                lambda a, k: sem_ref[ns + 3 * a + k])
        for cp in sends:
            cp().wait_send()
        for cp in arrivals:
            cp().wait_recv()

    hbm = lambda a: pltpu.HBM(a.shape, a.dtype)
    out = pl.pallas_call(
        body, name=name,
        out_shape=tuple(hbm(a) for a in (*sums, *parts)),
        in_specs=[HBM_SPEC] * (2 * n) + [SEM_SPEC] * (2 * ns) + [ANY] * len(_as_tuple(after)),
        out_specs=tuple([HBM_SPEC] * (2 * n)),
        input_output_aliases={i: i for i in range(2 * n)},
        compiler_params=pltpu.CompilerParams(has_side_effects=pltpu.SideEffectType.DATAFLOW_SIDE_EFFECTING),
    )(*sums, *parts, *sems, *_as_tuple(after))
    return list(out[n:])


def _after(value, token):
    return lax.optimization_barrier((value, token))[0]


def _sum_chips(parts, pos, name):
    _, h, c = parts.shape

    def body(pos_ref, p_ref, o_ref):
        o_ref[0] = ((p_ref[0].astype(F32) + p_ref[1].astype(F32)) + p_ref[2].astype(F32)) + p_ref[3].astype(F32)

    return pl.pallas_call(
        body, name=name,
        grid_spec=pltpu.PrefetchScalarGridSpec(
            num_scalar_prefetch=1, grid=(1,),
            in_specs=[pl.BlockSpec((N_CHIPS, h, c), lambda i, p: (0, 0, 0))],
            out_specs=pl.BlockSpec((1, h, c), lambda i, p: (p[1], 0, 0))),
        out_shape=jax.ShapeDtypeStruct((2, h, c), F32),
        compiler_params=_params(("arbitrary",)),
    )(pos, parts)


def _join_halves(fulls):
    n = len(fulls)

    def body(*refs):
        outs = refs[n:2 * n]
        send_sem, recv_sem = refs[2 * n:]
        x, y, c = _mesh_pos()

        def half(a, which):
            return pltpu.make_async_remote_copy(
                src_ref=outs[a].at[which], dst_ref=outs[a].at[which],
                send_sem=send_sem.at[a], recv_sem=recv_sem.at[a],
                device_id=(x, y, 1 - c), device_id_type=MESH)

        sends = [half(a, c) for a in range(n)]
        for cp in sends:
            cp.start()
        for a in range(n):
            half(a, 1 - c).wait_recv()
        for cp in sends:
            cp.wait_send()

    out_shape = [jax.ShapeDtypeStruct(f.shape, f.dtype) for f in fulls]
    return pl.pallas_call(
        body, name="join_halves",
        in_specs=[ANY] * n, out_specs=[ANY] * n, out_shape=out_shape,
        input_output_aliases={a: a for a in range(n)},
        scratch_shapes=[pltpu.SemaphoreType.DMA((n,))] * 2,
    )(*fulls)


def _mix_in(x, g, w_in, b_in, ts):
    s = x.shape[0]

    def body(x_ref, g_ref, w_ref, b_ref, z_ref, hn_ref):
        xv = x_ref[...]
        hn = (xv * _rms_stats(xv) * g_ref[...]).astype(BF16)
        hn_ref[...] = hn
        for j in range(4):
            cols = slice(j * 512, (j + 1) * 512)
            z_ref[:, cols] = _dot(hn, w_ref[j]) + b_ref[:, cols]

    return pl.pallas_call(
        body, name="mix_in", grid=(s // ts,),
        in_specs=[pl.BlockSpec((ts, D_MODEL), lambda i: (i, 0)),
                  pl.BlockSpec((1, D_MODEL), lambda i: (0, 0)),
                  pl.BlockSpec((4, D_MODEL, 512), lambda i: (0, 0, 0)),
                  pl.BlockSpec((1, 2048), lambda i: (0, 0))],
        out_specs=[pl.BlockSpec((ts, 2048), lambda i: (i, 0)),
                   pl.BlockSpec((ts, D_MODEL), lambda i: (i, 0))],
        out_shape=[jax.ShapeDtypeStruct((s, 2048), F32), jax.ShapeDtypeStruct((s, D_MODEL), BF16)],
        compiler_params=_params(("parallel",)),
    )(x, g, w_in, b_in)


def _lane_is_low_head():
    lane = lax.broadcasted_iota(jnp.int32, (1, GM_WIDTH), 1)
    return (lane & GM_HEAD_DIM) == 0


def _gm_mix(v_lo, v_hi, wpair_ref, bias_ref, mixed_ref, t):
    for n in range(t // CHUNK):
        rows = slice(n * CHUNK, (n + 1) * CHUNK)
        for j in range(GM_HEADS // 2):
            cols = slice(j * LANES, (j + 1) * LANES)
            rhs = jnp.concatenate([v_lo[rows, cols], v_hi[rows, cols]], axis=0)
            mixed_ref[rows, cols] = _dot(wpair_ref[j], rhs) + bias_ref[:, cols]


def _seqmix_fwd(z, cw, cb, lng, lnb, gg, gb, wpair, bias, t):
    s = z.shape[0]

    def body(z_ref, cw_ref, cb_ref, lng_ref, lnb_ref, gg_ref, gb_ref, wpair_ref, bias_ref,
             mix_ref, c1_ref, abuf, mixed_ref):
        i = pl.program_id(0)

        @pl.when(i == 0)
        def _():
            abuf[0:CONV_HALO, :] = jnp.zeros((CONV_HALO, CONV_WIDTH), F32)

        @pl.when(i > 0)
        def _():
            abuf[0:CONV_HALO, :] = abuf[t:t + CONV_HALO, :]

        abuf[CONV_HALO:, :] = z_ref[:, 0:512] * _sigmoid(z_ref[:, 512:1024])
        acc = jnp.zeros((t, CONV_WIDTH), F32)
        for k in range(CONV_KERNEL):
            acc = acc + cw_ref[k:k + 1, :] * abuf[pl.ds(CONV_HALO - (CONV_KERNEL - 1) + k, t), :]
        c1 = acc + cb_ref[...]
        c1_ref[...] = c1
        xh, _ = _ln_stats(c1)
        ln = xh * lng_ref[...] + lnb_ref[...]
        mix_ref[:, 0:512] = (ln * _sigmoid(ln)).astype(BF16)

        u, _ = _gelu_parts(z_ref[:, 1024:1536])
        gv, _ = _gelu_parts(z_ref[:, 1536:2048])
        vxh, _ = _ln_stats(gv)
        v = vxh * gg_ref[...] + gb_ref[...]
        low = _lane_is_low_head()
        v_lo = jnp.where(low, v, 0.0).astype(BF16)
        v_hi = jnp.where(low, 0.0, v).astype(BF16)
        _gm_mix(v_lo, v_hi, wpair_ref, bias_ref, mixed_ref, t)
        mix_ref[:, 512:1024] = (u * mixed_ref[...]).astype(BF16)

    vec = lambda n: pl.BlockSpec((1, n), lambda i: (0, 0))
    return pl.pallas_call(
        body, name="seqmix_fwd", grid=(s // t,),
        in_specs=[pl.BlockSpec((t, 2048), lambda i: (i, 0)),
                  pl.BlockSpec((CONV_HALO, CONV_WIDTH), lambda i: (0, 0)),
                  vec(512), vec(512), vec(512), vec(512), vec(512),
                  pl.BlockSpec((4, CHUNK, 2 * CHUNK), lambda i: (0, 0, 0)),
                  pl.BlockSpec((CHUNK, GM_WIDTH), lambda i: (0, 0))],
        out_specs=[pl.BlockSpec((t, D_MODEL), lambda i: (i, 0)),
                   pl.BlockSpec((t, CONV_WIDTH), lambda i: (i, 0))],
        out_shape=[jax.ShapeDtypeStruct((s, D_MODEL), BF16), jax.ShapeDtypeStruct((s, CONV_WIDTH), F32)],
        scratch_shapes=[pltpu.VMEM((t + CONV_HALO, CONV_WIDTH), F32), pltpu.VMEM((t, GM_WIDTH), F32)],
        compiler_params=_params(("arbitrary",)),
    )(z, cw, cb, lng, lnb, gg, gb, wpair, bias)


def _out_proj_q(x, mix, w_out, g, wq, ts):
    s = x.shape[0]

    def body(x_ref, mix_ref, wo_ref, g_ref, wq_ref, h1_ref, hn_ref, q_ref):
        h1 = x_ref[...] + _dot(mix_ref[...], wo_ref[...])
        h1_ref[...] = h1
        hn = (h1 * _rms_stats(h1) * g_ref[...]).astype(BF16)
        hn_ref[...] = hn
        q_ref[...] = _dot(hn, wq_ref[...]).astype(BF16)

    row = lambda dt: pl.BlockSpec((ts, D_MODEL), lambda i: (i, 0))
    full = pl.BlockSpec((D_MODEL, D_MODEL), lambda i: (0, 0))
    return pl.pallas_call(
        body, name="out_proj_q", grid=(s // ts,),
        in_specs=[row(F32), row(BF16), full, pl.BlockSpec((1, D_MODEL), lambda i: (0, 0)), full],
        out_specs=[row(F32), row(BF16), row(BF16)],
        out_shape=[jax.ShapeDtypeStruct((s, D_MODEL), F32), jax.ShapeDtypeStruct((s, D_MODEL), BF16),
                   jax.ShapeDtypeStruct((s, D_MODEL), BF16)],
        compiler_params=_params(("parallel",)),
    )(x, mix, w_out, g, wq)


def _mem_kv(mem, g, wkv):
    m = mem.shape[0]

    def body(mem_ref, g_ref, w_ref, mn_ref, kv_ref):
        mv = mem_ref[...]
        mn = (mv * _rms_stats(mv) * g_ref[...]).astype(BF16)
        mn_ref[...] = mn
        for j in range(4):
            kv_ref[:, j * 512:(j + 1) * 512] = _dot(mn, w_ref[j]).astype(BF16)

    return pl.pallas_call(
        body, name="mem_kv",
        out_shape=[jax.ShapeDtypeStruct((m, D_MODEL), BF16), jax.ShapeDtypeStruct((m, 2 * D_MODEL), BF16)],
        compiler_params=pltpu.CompilerParams(vmem_limit_bytes=VMEM_LIMIT_BYTES),
    )(mem, g, wkv)


def _softmax_rows(sc):
    e = jnp.exp(sc - jnp.max(sc, axis=-1, keepdims=True))
    return e / jnp.sum(e, axis=-1, keepdims=True)


def _attn_fwd(q, kv, h1, wo, g, ts):
    s, m = q.shape[0], kv.shape[0]
    scale = XA_HEAD_DIM ** -0.5

    def body(q_ref, kv_ref, h1_ref, wo_ref, g_ref, o_ref, h2_ref, hn_ref):
        for h in range(XA_HEADS):
            cols = slice(h * XA_HEAD_DIM, (h + 1) * XA_HEAD_DIM)
            vcols = slice(D_MODEL + h * XA_HEAD_DIM, D_MODEL + (h + 1) * XA_HEAD_DIM)
            p = _softmax_rows(_dot_nt(q_ref[:, cols], kv_ref[:, cols]) * scale)
            o_ref[:, cols] = _dot(p.astype(BF16), kv_ref[:, vcols]).astype(BF16)
        h2 = h1_ref[...] + _dot(o_ref[...], wo_ref[...])
        h2_ref[...] = h2
        hn_ref[...] = (h2 * _rms_stats(h2) * g_ref[...]).astype(BF16)

    row = pl.BlockSpec((ts, D_MODEL), lambda i: (i, 0))
    return pl.pallas_call(
        body, name="attn_fwd", grid=(s // ts,),
        in_specs=[row, pl.BlockSpec((m, 2 * D_MODEL), lambda i: (0, 0)), row,
                  pl.BlockSpec((D_MODEL, D_MODEL), lambda i: (0, 0)),
                  pl.BlockSpec((1, D_MODEL), lambda i: (0, 0))],
        out_specs=[row, row, row],
        out_shape=[jax.ShapeDtypeStruct((s, D_MODEL), BF16), jax.ShapeDtypeStruct((s, D_MODEL), F32),
                   jax.ShapeDtypeStruct((s, D_MODEL), BF16)],
        compiler_params=_params(("parallel",)),
    )(q, kv, h1, wo, g)


def _ffn_up(hn, wgu, ts):
    s = hn.shape[0]

    def body(hn_ref, w_ref, gu_ref, act_ref):
        hv = hn_ref[...]
        gate = _dot(hv, w_ref[0, 0])
        up = _dot(hv, w_ref[1, 0])
        gu_ref[0] = gate
        gu_ref[1] = up
        act_ref[...] = (gate * _sigmoid(gate) * up).astype(BF16)

    return pl.pallas_call(
        body, name="ffn_up", grid=(2, s // ts),
        in_specs=[pl.BlockSpec((ts, D_MODEL), lambda j, i: (i, 0)),
                  pl.BlockSpec((2, 1, D_MODEL, FFN_HALF), lambda j, i: (0, j, 0, 0))],
        out_specs=[pl.BlockSpec((2, ts, FFN_HALF), lambda j, i: (0, i, j)),
                   pl.BlockSpec((ts, FFN_HALF), lambda j, i: (i, j))],
        out_shape=[jax.ShapeDtypeStruct((2, s, FFN_HIDDEN), F32), jax.ShapeDtypeStruct((s, FFN_HIDDEN), BF16)],
        compiler_params=_params(("parallel", "parallel")),
    )(hn, wgu)


def _ffn_down_loss(act, wd, h2, g, target, ts):
    s = act.shape[0]

    def body(act_ref, wd_ref, h2_ref, g_ref, t_ref, dh_ref, sq_ref, dg_ref):
        @pl.when(pl.program_id(0) == 0)
        def _():
            sq_ref[...] = jnp.zeros_like(sq_ref)
            dg_ref[...] = jnp.zeros_like(dg_ref)

        h3 = h2_ref[...] + _dot(act_ref[...], wd_ref[...])
        r = _rms_stats(h3)
        gv = g_ref[...]
        diff = h3 * r * gv - t_ref[...]
        sq_ref[...] += _rowsum(diff * diff)
        dh, dg = _rms_bwd(diff / D_MODEL, h3, r, gv)
        dh_ref[...] = dh
        dg_ref[...] += dg

    row = pl.BlockSpec((ts, D_MODEL), lambda i: (i, 0))
    vec = pl.BlockSpec((1, D_MODEL), lambda i: (0, 0))
    return pl.pallas_call(
        body, name="ffn_down_loss", grid=(s // ts,),
        in_specs=[pl.BlockSpec((ts, FFN_HIDDEN), lambda i: (i, 0)),
                  pl.BlockSpec((FFN_HIDDEN, D_MODEL), lambda i: (0, 0)), row, vec, row],
        out_specs=[row, vec, vec],
        out_shape=[jax.ShapeDtypeStruct((s, D_MODEL), F32), jax.ShapeDtypeStruct((1, D_MODEL), F32),
                   jax.ShapeDtypeStruct((1, D_MODEL), F32)],
        compiler_params=_params(("arbitrary",)),
    )(act, wd, h2, g, target)


def _grad_w(a, b, tk, tn, name):
    s, k = a.shape
    gb, _, n = b.shape
    nblk = n // tn
    tsr = 512 if s % 512 == 0 else s

    def body(a_ref, b_ref, o_ref):
        @pl.when(pl.program_id(2) == 0)
        def _():
            o_ref[...] = jnp.zeros_like(o_ref)

        o_ref[0] += _dot_tn(a_ref[...].astype(BF16), b_ref[0].astype(BF16))

    return pl.pallas_call(
        body, name=name, grid=(gb * nblk, k // tk, s // tsr),
        in_specs=[pl.BlockSpec((tsr, tk), lambda ni, ki, si: (si, ki)),
                  pl.BlockSpec((1, tsr, tn), lambda ni, ki, si: (ni // nblk, si, ni % nblk))],
        out_specs=pl.BlockSpec((1, tk, tn), lambda ni, ki, si: (ni, ki, 0)),
        out_shape=jax.ShapeDtypeStruct((gb * nblk, k, tn), F32),
        compiler_params=_params(("parallel", "parallel", "arbitrary")),
    )(a, b)


def _ffn_bwd_act(dh3, wd, gu, ts):
    s = dh3.shape[0]

    def body(dh_ref, wd_ref, gu_ref, dgu_ref):
        dact = _dot_nt(dh_ref[...].astype(BF16), wd_ref[0])
        gate, up = gu_ref[0], gu_ref[1]
        sg = _sigmoid(gate)
        dgu_ref[0] = (dact * up * (sg * (1.0 + gate * (1.0 - sg)))).astype(BF16)
        dgu_ref[1] = (dact * (gate * sg)).astype(BF16)

    return pl.pallas_call(
        body, name="ffn_bwd_act", grid=(2, s // ts),
        in_specs=[pl.BlockSpec((ts, D_MODEL), lambda j, i: (i, 0)),
                  pl.BlockSpec((1, FFN_HALF, D_MODEL), lambda j, i: (j, 0, 0)),
                  pl.BlockSpec((2, ts, FFN_HALF), lambda j, i: (0, i, j))],
        out_specs=pl.BlockSpec((2, ts, FFN_HALF), lambda j, i: (0, i, j)),
        out_shape=jax.ShapeDtypeStruct((2, s, FFN_HIDDEN), BF16),
        compiler_params=_params(("parallel", "parallel")),
    )(dh3, wd, gu)


def _ffn_bwd_in(dgu, wgu, dh3, h2, g, ts):
    s = dh3.shape[0]

    def body(dgu_ref, w_ref, dh3_ref, h2_ref, g_ref, dh2_ref, dg_ref):
        @pl.when(pl.program_id(0) == 0)
        def _():
            dg_ref[...] = jnp.zeros_like(dg_ref)

        dhn = jnp.zeros((ts, D_MODEL), F32)
        for p in range(2):
            for j in range(2):
                dhn = dhn + _dot_nt(dgu_ref[p, :, j * FFN_HALF:(j + 1) * FFN_HALF], w_ref[2 * p + j])
        h2 = h2_ref[...]
        dv, dg = _rms_bwd(dhn, h2, _rms_stats(h2), g_ref[...])
        dh2_ref[...] = dh3_ref[...] + dv
        dg_ref[...] += dg

    row = pl.BlockSpec((ts, D_MODEL), lambda i: (i, 0))
    vec = pl.BlockSpec((1, D_MODEL), lambda i: (0, 0))
    return pl.pallas_call(
        body, name="ffn_bwd_in", grid=(s // ts,),
        in_specs=[pl.BlockSpec((2, ts, FFN_HIDDEN), lambda i: (0, i, 0)),
                  pl.BlockSpec((4, D_MODEL, FFN_HALF), lambda i: (0, 0, 0)), row, row, vec],
        out_specs=[row, vec],
        out_shape=[jax.ShapeDtypeStruct((s, D_MODEL), F32), jax.ShapeDtypeStruct((1, D_MODEL), F32)],
        compiler_params=_params(("arbitrary",)),
    )(dgu, wgu, dh3, h2, g)


def _attn_bwd(dh2, wo, q, kv, wq, h1, g, ts):
    s, m = q.shape[0], kv.shape[0]
    scale = XA_HEAD_DIM ** -0.5

    def body(dh2_ref, wo_ref, q_ref, kv_ref, wq_ref, h1_ref, g_ref, dh1_ref, dq_ref, dkv_ref, dg_ref):
        @pl.when(pl.program_id(0) == 0)
        def _():
            dkv_ref[...] = jnp.zeros_like(dkv_ref)
            dg_ref[...] = jnp.zeros_like(dg_ref)

        do = _dot_nt(dh2_ref[...].astype(BF16), wo_ref[...]).astype(BF16)
        for h in range(XA_HEADS):
            cols = slice(h * XA_HEAD_DIM, (h + 1) * XA_HEAD_DIM)
            vcols = slice(D_MODEL + h * XA_HEAD_DIM, D_MODEL + (h + 1) * XA_HEAD_DIM)
            qh, kh, vh, doh = q_ref[:, cols], kv_ref[:, cols], kv_ref[:, vcols], do[:, cols]
            p = _softmax_rows(_dot_nt(qh, kh) * scale)
            dp = _dot_nt(doh, vh)
            ds = (p * (dp - jnp.sum(dp * p, axis=-1, keepdims=True)) * scale).astype(BF16)
            dq_ref[:, cols] = _dot(ds, kh).astype(BF16)
            dkv_ref[:, cols] += _dot_tn(ds, qh)
            dkv_ref[:, vcols] += _dot_tn(p.astype(BF16), doh)
        dhn = _dot_nt(dq_ref[...], wq_ref[...])
        h1 = h1_ref[...]
        dv, dg = _rms_bwd(dhn, h1, _rms_stats(h1), g_ref[...])
        dh1_ref[...] = dh2_ref[...] + dv
        dg_ref[...] += dg

    row = pl.BlockSpec((ts, D_MODEL), lambda i: (i, 0))
    full = pl.BlockSpec((D_MODEL, D_MODEL), lambda i: (0, 0))
    kvs = pl.BlockSpec((m, 2 * D_MODEL), lambda i: (0, 0))
    vec = pl.BlockSpec((1, D_MODEL), lambda i: (0, 0))
    return pl.pallas_call(
        body, name="attn_bwd", grid=(s // ts,),
        in_specs=[row, full, row, kvs, full, row, vec],
        out_specs=[row, row, kvs, vec],
        out_shape=[jax.ShapeDtypeStruct((s, D_MODEL), F32), jax.ShapeDtypeStruct((s, D_MODEL), BF16),
                   jax.ShapeDtypeStruct((m, 2 * D_MODEL), F32), jax.ShapeDtypeStruct((1, D_MODEL), F32)],
        compiler_params=_params(("arbitrary",)),
    )(dh2, wo, q, kv, wq, h1, g)


def _mem_kv_bwd(dkv, mn, wkv, mem, g):
    m = mem.shape[0]

    def body(dkv_ref, mn_ref, w_ref, mem_ref, g_ref, dw_ref, dg_ref):
        dmn = jnp.zeros((m, D_MODEL), F32)
        mn = mn_ref[...]
        for j in range(4):
            dj = dkv_ref[:, j * 512:(j + 1) * 512].astype(BF16)
            dw_ref[j] = _dot_tn(mn, dj)
            dmn = dmn + _dot_nt(dj, w_ref[j])
        mv = mem_ref[...]
        dg_ref[...] = _rowsum(dmn * (mv * _rms_stats(mv)))

    return pl.pallas_call(
        body, name="mem_kv_bwd",
        out_shape=[jax.ShapeDtypeStruct((4, D_MODEL, 512), F32), jax.ShapeDtypeStruct((1, D_MODEL), F32)],
        compiler_params=pltpu.CompilerParams(vmem_limit_bytes=VMEM_LIMIT_BYTES),
    )(dkv, mn, wkv, mem, g)


def _seqmix_bwd(dh1, x, z, c1, w_out, w_in, g_mix, cw, lng, lnb, gg, gb, wpair, wpair_t, bias, t):
    s = x.shape[0]
    nt = s // t
    halo_blocks = t // CONV_HALO

    def body(dh1_ref, x_ref, z_ref, zh_ref, c1_ref, wo_ref, wi_ref, gm_ref, cw_ref, lng_ref, lnb_ref,
             gg_ref, gb_ref, wpair_ref, wpt_ref, bias_ref,
             gx_ref, dz_ref, dcw_ref, dcb_ref, dlng_ref, dlnb_ref, dgg_ref, dgb_ref, dws_ref, dbs_ref,
             dbin_ref, dgm_ref, abuf, dbuf, mixed_ref, dv_ref):
        i = pl.program_id(0)
        tile = nt - 1 - i
        accs = (dcw_ref, dcb_ref, dlng_ref, dlnb_ref, dgg_ref, dgb_ref, dws_ref, dbs_ref, dbin_ref, dgm_ref)

        @pl.when(i == 0)
        def _():
            for r in accs:
                r[...] = jnp.zeros_like(r)
            dbuf[t:t + CONV_HALO, :] = jnp.zeros((CONV_HALO, CONV_WIDTH), F32)

        @pl.when(i > 0)
        def _():
            dbuf[t:t + CONV_HALO, :] = dbuf[0:CONV_HALO, :]

        dmix = _dot_nt(dh1_ref[...].astype(BF16), wo_ref[...])

        xh, rs = _ln_stats(c1_ref[...])
        lng = lng_ref[...]
        ln = xh * lng + lnb_ref[...]
        sl = _sigmoid(ln)
        dln = dmix[:, 0:512] * (sl * (1.0 + ln * (1.0 - sl)))
        dc1, dg_ln, db_ln = _ln_bwd(dln, xh, rs, lng)
        dlng_ref[...] += dg_ln
        dlnb_ref[...] += db_ln
        dcb_ref[...] += _rowsum(dc1)
        dbuf[0:t, :] = dc1

        zh = zh_ref[...]
        a_halo = zh[:, 0:512] * _sigmoid(zh[:, 512:1024])
        abuf[0:CONV_HALO, :] = jnp.where(tile > 0, a_halo, 0.0)
        za = z_ref[:, 0:512]
        sg = _sigmoid(z_ref[:, 512:1024])
        abuf[CONV_HALO:, :] = za * sg

        da = jnp.zeros((t, CONV_WIDTH), F32)
        for k in range(CONV_KERNEL):
            da = da + cw_ref[k:k + 1, :] * dbuf[pl.ds(CONV_KERNEL - 1 - k, t), :]
            dcw_ref[k:k + 1, :] += _rowsum(dc1 * abuf[pl.ds(CONV_HALO - (CONV_KERNEL - 1) + k, t), :])
        dza = da * sg
        dzg = da * za * (sg * (1.0 - sg))
        dz_ref[:, 0:512] = dza.astype(BF16)
        dz_ref[:, 512:1024] = dzg.astype(BF16)
        dbin_ref[:, 0:512] += _rowsum(dza)
        dbin_ref[:, 512:1024] += _rowsum(dzg)

        dgm = dmix[:, 512:1024]
        u, du_dz = _gelu_parts(z_ref[:, 1024:1536])
        gv, dgv_dz = _gelu_parts(z_ref[:, 1536:2048])
        vxh, vrs = _ln_stats(gv)
        ggv = gg_ref[...]
        v = vxh * ggv + gb_ref[...]
        low = _lane_is_low_head()
        v_lo = jnp.where(low, v, 0.0).astype(BF16)
        v_hi = jnp.where(low, 0.0, v).astype(BF16)
        _gm_mix(v_lo, v_hi, wpair_ref, bias_ref, mixed_ref, t)
        dzu = dgm * mixed_ref[...] * du_dz
        dm = dgm * u
        dm_lo = jnp.where(low, dm, 0.0).astype(BF16)
        dm_hi = jnp.where(low, 0.0, dm).astype(BF16)
        vb = v.astype(BF16)
        tril = (lax.broadcasted_iota(jnp.int32, (CHUNK, CHUNK), 1)
                <= lax.broadcasted_iota(jnp.int32, (CHUNK, CHUNK), 0))
        for n in range(t // CHUNK):
            rows = slice(n * CHUNK, (n + 1) * CHUNK)
            dbs_ref[...] += dm[rows, :]
            for j in range(GM_HEADS // 2):
                cols = slice(j * LANES, (j + 1) * LANES)
                stack = jnp.concatenate([dm_lo[rows, cols], dm_hi[rows, cols]], axis=0)
                dws = _dot_nt(stack, vb[rows, cols])
                dws_ref[2 * j] += jnp.where(tril, dws[0:CHUNK], 0.0)
                dws_ref[2 * j + 1] += jnp.where(tril, dws[CHUNK:2 * CHUNK], 0.0)
                dv_ref[rows, cols] = _dot(wpt_ref[j], stack)
        dgv, dg_gm, db_gm = _ln_bwd(dv_ref[...], vxh, vrs, ggv)
        dgg_ref[...] += dg_gm
        dgb_ref[...] += db_gm
        dzv = dgv * dgv_dz
        dz_ref[:, 1024:1536] = dzu.astype(BF16)
        dz_ref[:, 1536:2048] = dzv.astype(BF16)
        dbin_ref[:, 1024:1536] += _rowsum(dzu)
        dbin_ref[:, 1536:2048] += _rowsum(dzv)

        dhn = jnp.zeros((t, D_MODEL), F32)
        for j in range(4):
            dhn = dhn + _dot_nt(dz_ref[:, j * 512:(j + 1) * 512], wi_ref[j])
        xv = x_ref[...]
        dv, dg = _rms_bwd(dhn, xv, _rms_stats(xv), gm_ref[...])
        gx_ref[...] = dh1_ref[...] + dv
        dgm_ref[...] += dg

    rev = lambda w: pl.BlockSpec((t, w), lambda i: (nt - 1 - i, 0))
    const = lambda *shape: pl.BlockSpec(shape, lambda i: (0,) * len(shape))
    halo = pl.BlockSpec((CONV_HALO, D_MODEL), lambda i: (jnp.maximum((nt - 1 - i) * halo_blocks - 1, 0), 0))
    f32 = lambda *shape: jax.ShapeDtypeStruct(shape, F32)
    return pl.pallas_call(
        body, name="seqmix_bwd", grid=(nt,),
        in_specs=[rev(D_MODEL), rev(D_MODEL), rev(2048), halo, rev(CONV_WIDTH),
                  const(D_MODEL, D_MODEL), const(4, D_MODEL, 512), const(1, D_MODEL),
                  const(CONV_HALO, CONV_WIDTH), const(1, 512), const(1, 512), const(1, 512), const(1, 512),
                  const(4, CHUNK, 2 * CHUNK), const(4, CHUNK, 2 * CHUNK), const(CHUNK, GM_WIDTH)],
        out_specs=[rev(D_MODEL), rev(2048),
                   const(CONV_HALO, CONV_WIDTH), const(1, 512), const(1, 512), const(1, 512), const(1, 512),
                   const(1, 512), const(GM_HEADS, CHUNK, CHUNK), const(CHUNK, GM_WIDTH), const(1, 2048),
                   const(1, D_MODEL)],
        out_shape=[f32(s, D_MODEL), jax.ShapeDtypeStruct((s, 2048), BF16),
                   f32(CONV_HALO, CONV_WIDTH), f32(1, 512), f32(1, 512), f32(1, 512), f32(1, 512),
                   f32(1, 512), f32(GM_HEADS, CHUNK, CHUNK), f32(CHUNK, GM_WIDTH), f32(1, 2048),
                   f32(1, D_MODEL)],
        scratch_shapes=[pltpu.VMEM((t + CONV_HALO, CONV_WIDTH), F32), pltpu.VMEM((t + CONV_HALO, CONV_WIDTH), F32),
                        pltpu.VMEM((t, GM_WIDTH), F32), pltpu.VMEM((t, GM_WIDTH), F32)],
        compiler_params=_params(("arbitrary",)),
    )(dh1, x, z, z, c1, w_out, w_in, g_mix, cw, lng, lnb, gg, gb, wpair, wpair_t, bias)


def _head_bias_grad(dbs):
    def body(d_ref, o_ref):
        dv = d_ref[...]
        lane = lax.broadcasted_iota(jnp.int32, (CHUNK, LANES), 1)
        acc = jnp.zeros((CHUNK, LANES), F32)
        for h in range(GM_HEADS):
            sh = jnp.sum(dv[:, h * GM_HEAD_DIM:(h + 1) * GM_HEAD_DIM], axis=-1, keepdims=True)
            acc = acc + jnp.where(lane == h, sh, 0.0)
        o_ref[...] = acc

    return pl.pallas_call(body, name="head_bias_grad",
                          out_shape=jax.ShapeDtypeStruct((CHUNK, LANES), F32))(dbs)


def _pack(arrays, rows):
    flat = jnp.concatenate([a.reshape(-1) for a in arrays])
    flat = jnp.pad(flat, (0, rows * LANES - flat.shape[0]))
    return flat.reshape(rows, LANES)


def _unpack(buf, shapes):
    flat = buf.reshape(-1)
    out, off = [], 0
    for shp in shapes:
        size = 1
        for d in shp:
            size *= d
        out.append(flat[off:off + size].reshape(shp))
        off += size
    return out


def _rows_for(shapes, multiple):
    total = 0
    for shp in shapes:
        size = 1
        for d in shp:
            size *= d
        total += size
    rows = -(-total // LANES)
    return -(-rows // multiple) * multiple


def kernel(x, mem, norm_mix_g, w_in, b_in, conv_w, conv_b, conv_ln_g, conv_ln_b, gm_ln_g, gm_ln_b, gm_w_s, gm_b_s, w_out, norm_xa_g, mem_norm_g, xa_wq, xa_wkv, xa_wo, norm_ffn_g, ffn_w_gate_up, ffn_w_down, final_norm_g, loss_target, m_norm_mix_g, m_w_in, m_b_in, m_conv_w, m_conv_b, m_conv_ln_g, m_conv_ln_b, m_gm_ln_g, m_gm_ln_b, m_gm_w_s, m_gm_b_s, m_w_out, m_norm_xa_g, m_mem_norm_g, m_xa_wq, m_xa_wkv, m_xa_wo, m_norm_ffn_g, m_ffn_w_gate_up, m_ffn_w_down, m_final_norm_g, v_norm_mix_g, v_w_in, v_b_in, v_conv_w, v_conv_b, v_conv_ln_g, v_conv_ln_b, v_gm_ln_g, v_gm_ln_b, v_gm_w_s, v_gm_b_s, v_w_out, v_norm_xa_g, v_mem_norm_g, v_xa_wq, v_xa_wkv, v_xa_wo, v_norm_ffn_g, v_ffn_w_gate_up, v_ffn_w_down, v_final_norm_g):
    s = x.shape[1]
    ts = _row_tile(s)
    tb = max(CHUNK, ts // 2)
    cx, cy, cc = _mesh_pos()
    chip = 2 * cx + cy
    pos = jnp.stack([chip, cc]).astype(jnp.int32)
    row = lambda a: a.reshape(1, -1)
    x2, mem2, tgt2 = x[0], mem[0], loss_target[0]

    big = dict(w_in=w_in, xa_wkv=xa_wkv, w_out=w_out, xa_wq=xa_wq, xa_wo=xa_wo,
               ffn_w_gate_up=ffn_w_gate_up, ffn_w_down=ffn_w_down)
    big_names = list(big)
    halves = lambda a: a.reshape(2, a.shape[0] // 2, a.shape[1])
    cast = {nm: _cast_into_slot(halves(big[nm]), pos, BF16, "cast_" + nm) for nm in big_names}
    conv_w_pad = jnp.pad(conv_w, ((0, CONV_HALO - CONV_KERNEL), (0, 0)))
    first = _gather_shards([cast["w_in"], _cast_into_slot(halves(conv_w_pad), pos, F32, "slot_conv_w")])
    shaped = lambda buf, nm: buf.reshape(N_CHIPS, big[nm].shape[0], big[nm].shape[1])

    def start_gather(names, not_before):
        bufs = [cast[nm] for nm in names]
        if not_before is not None:
            bufs = _after(bufs, not_before)
        return _gather_start(bufs, "gather_start_" + names[0])

    def finish_gather(names, started, after):
        send_sems, recv_sems, bufs, _ = started
        landed = _gather_wait(send_sems, recv_sems, bufs, after, "gather_wait_" + names[0])
        return {nm: shaped(b, nm) for nm, b in zip(names, _pass_to_sibling(landed, "pass_" + names[0]))}

    attn_names = ["w_out", "xa_wq", "xa_wkv", "xa_wo"]
    gather_attn = start_gather(attn_names, None)
    w_in_g = _after(shaped(first[0], "w_in"), gather_attn[3])
    cw_g = jnp.concatenate([first[1][k].reshape(CONV_HALO, LANES) for k in range(N_CHIPS)], axis=1)

    tril = jnp.tril(jnp.ones((CHUNK, CHUNK), dtype=bool))
    ws = jnp.where(tril[None], gm_w_s, 0.0)
    wpair = jnp.concatenate([ws[0::2], ws[1::2]], axis=2).astype(BF16)
    ws_t = jnp.swapaxes(ws, 1, 2)
    wpair_t = jnp.concatenate([ws_t[0::2], ws_t[1::2]], axis=2).astype(BF16)
    bias = jnp.repeat(gm_b_s.T, GM_HEAD_DIM, axis=1)

    z, hn1 = _mix_in(x2, row(norm_mix_g), w_in_g, row(b_in), ts)
    mix, c1 = _seqmix_fwd(z, cw_g, row(conv_b), row(conv_ln_g), row(conv_ln_b), row(gm_ln_g), row(gm_ln_b),
                          wpair, bias, ts)
    gw = finish_gather(attn_names, gather_attn, mix)
    w_out_g = gw["w_out"].reshape(D_MODEL, D_MODEL)
    wq_g = gw["xa_wq"].reshape(D_MODEL, D_MODEL)
    wkv_g = gw["xa_wkv"]
    wo_g = gw["xa_wo"].reshape(D_MODEL, D_MODEL)
    gather_gu = start_gather(["ffn_w_gate_up"], w_out_g)
    h1, hn2, q = _out_proj_q(x2, _after(mix, gather_gu[3]), w_out_g, row(norm_xa_g), wq_g, ts)
    mn, kv = _mem_kv(mem2, row(mem_norm_g), wkv_g)
    o, h2, hn3 = _attn_fwd(q, kv, h1, wo_g, row(norm_ffn_g), ts)
    wgu_g = finish_gather(["ffn_w_gate_up"], gather_gu, hn3)["ffn_w_gate_up"]
    gather_down = start_gather(["ffn_w_down"], wgu_g)
    gu, act = _ffn_up(_after(hn3, gather_down[3]), wgu_g.reshape(2, 2, D_MODEL, FFN_HALF), ts)
    wd_g = finish_gather(["ffn_w_down"], gather_down, act)["ffn_w_down"].reshape(FFN_HIDDEN, D_MODEL)
    dh3, sq, d_final_g = _ffn_down_loss(act, wd_g, h2, row(final_norm_g), tgt2, ts)
    loss = lax.psum(0.5 * jnp.sum(sq) / D_MODEL, ("x", "y", "c"))

    def split(g, nm):
        r, c = big[nm].shape
        return g.reshape(N_CHIPS, 2, r // 2, c)

    def chip_sums_of(group, arrays):
        got = _swap_halves(arrays, "swap_halves_" + group[0])
        both = [_add_half(g, r, pos, "chip_sum_" + nm) for g, r, nm in zip(arrays, got, group)]
        return [b[0] for b in both], [b[1] for b in both]

    def start_swap(group, grads):
        return _swap_start([split(g, nm) for g, nm in zip(grads, group)], "swap_start_" + group[0])

    def start_exchange(group, swapping, after):
        sems, arrays, lands, _ = swapping
        arrays, got = _swap_wait(sems, arrays, lands, after, "swap_wait_" + group[0])
        both = [_add_half(g, r, pos, "chip_sum_" + nm) for g, r, nm in zip(arrays, got, group)]
        return _exchange_start([b[0] for b in both], [b[1] for b in both], "exchange_start_" + group[0])

    def finish_exchange(group, started, after):
        sems, sums, parts, _ = started
        parts = _exchange_wait(sems, sums, parts, after, "exchange_wait_" + group[0])
        return [_sum_chips(p, pos, "total_" + nm) for p, nm in zip(parts, group)]

    as3 = lambda a: a.reshape((1,) + a.shape)
    halves_of = {}

    dgu = _ffn_bwd_act(dh3, wd_g.reshape(2, FFN_HALF, D_MODEL), gu, ts)
    g_down = _grad_w(act, as3(dh3), FFN_HALF, D_MODEL, "grad_ffn_w_down")
    group_a = ["ffn_w_down"]
    swap_a = start_swap(group_a, [g_down])
    dh2, d_ffn_g = _ffn_bwd_in(_after(dgu, swap_a[3]), wgu_g, dh3, h2, row(norm_ffn_g), tb)
    exch_a = start_exchange(group_a, swap_a, dh2)
    g_gu = _grad_w(_after(hn3, exch_a[3]), dgu, 512, FFN_HALF, "grad_ffn_w_gate_up")
    halves_of.update(zip(group_a, finish_exchange(group_a, exch_a, g_gu)))

    group_b = ["ffn_w_gate_up"]
    swap_b = start_swap(group_b, [g_gu])
    dh1, dq, dkv, d_xa_g = _attn_bwd(_after(dh2, swap_b[3]), wo_g, q, kv, wq_g, h1, row(norm_xa_g), ts)
    exch_b = start_exchange(group_b, swap_b, dh1)
    dkv_t, o_t, dq_t, mix_t = _after((dkv, o, dq, mix), exch_b[3])
    g_wkv, d_mem_g = _mem_kv_bwd(dkv_t, mn, wkv_g, mem2, row(mem_norm_g))
    g_wo = _grad_w(o_t, as3(dh2), 512, D_MODEL, "grad_xa_wo")
    g_wq = _grad_w(hn2, as3(dq_t), 512, D_MODEL, "grad_xa_wq")
    g_wout = _grad_w(mix_t, as3(dh1), 512, D_MODEL, "grad_w_out")
    halves_of.update(zip(group_b, finish_exchange(group_b, exch_b, (g_wkv, g_wo, g_wq, g_wout))))

    group_c = ["xa_wo", "xa_wq", "xa_wkv", "w_out"]
    swap_c = start_swap(group_c, [g_wo, g_wq, g_wkv, g_wout])
    (gx, dz, d_cw, d_cb, d_lng, d_lnb, d_gg, d_gb, d_ws, d_bs_sum, d_bin, d_mix_g) = _seqmix_bwd(
        _after(dh1, swap_c[3]), x2, z, c1, w_out_g, w_in_g, row(norm_mix_g), cw_g, row(conv_ln_g), row(conv_ln_b),
        row(gm_ln_g), row(gm_ln_b), wpair, wpair_t, bias, tb)
    d_bs = _head_bias_grad(d_bs_sum)[:, :GM_HEADS].T
    exch_c = start_exchange(group_c, swap_c, dz)
    g_win = _grad_w(_after(hn1, exch_c[3]), as3(dz), 512, 512, "grad_w_in")
    halves_of.update(zip(group_c, finish_exchange(group_c, exch_c, g_win)))

    small_names = ["norm_mix_g", "b_in", "conv_w", "conv_b", "conv_ln_g", "conv_ln_b", "gm_ln_g", "gm_ln_b",
                   "gm_w_s", "gm_b_s", "norm_xa_g", "mem_norm_g", "norm_ffn_g", "final_norm_g"]
    small_grads = dict(norm_mix_g=d_mix_g, b_in=d_bin, conv_w=d_cw[:CONV_KERNEL], conv_b=d_cb, conv_ln_g=d_lng,
                       conv_ln_b=d_lnb, gm_ln_g=d_gg, gm_ln_b=d_gb, gm_w_s=d_ws, gm_b_s=d_bs, norm_xa_g=d_xa_g,
                       mem_norm_g=d_mem_g, norm_ffn_g=d_ffn_g, final_norm_g=d_final_g)
    full_shapes = dict(norm_mix_g=(D_MODEL,), b_in=(2048,), conv_w=(CONV_KERNEL, CONV_WIDTH), conv_b=(512,),
                       conv_ln_g=(512,), conv_ln_b=(512,), gm_ln_g=(512,), gm_ln_b=(512,),
                       gm_w_s=(GM_HEADS, CHUNK, CHUNK), gm_b_s=(GM_HEADS, CHUNK), norm_xa_g=(D_MODEL,),
                       mem_norm_g=(D_MODEL,), norm_ffn_g=(D_MODEL,), final_norm_g=(D_MODEL,))
    pack_rows = _rows_for([full_shapes[nm] for nm in small_names], 32)
    small_pack = _pack([small_grads[nm] for nm in small_names], pack_rows)

    group_d = ["w_in", "small"]
    sums_d, parts_d = chip_sums_of(group_d, [split(g_win, "w_in"),
                                             small_pack.reshape(1, 2, pack_rows // 2, LANES)])
    parts_d = _exchange_chip_sums(sums_d, parts_d)
    halves_of.update(zip(group_d, [_sum_chips(p, pos, "total_" + nm) for p, nm in zip(parts_d, group_d)]))

    red_names = big_names + ["small"]
    joined = _join_halves([halves_of[nm] for nm in red_names])
    grads = {nm: joined[i].reshape(big[nm].shape) for i, nm in enumerate(big_names)}
    small_red = _unpack(joined[-1].reshape(pack_rows, LANES), [full_shapes[nm] for nm in small_names])
    for nm, g in zip(small_names, small_red):
        grads[nm] = g
    grads["conv_w"] = lax.dynamic_slice(grads["conv_w"], (0, chip * LANES), (CONV_KERNEL, LANES))

    weights = dict(norm_mix_g=norm_mix_g, w_in=w_in, b_in=b_in, conv_w=conv_w, conv_b=conv_b, conv_ln_g=conv_ln_g,
                   conv_ln_b=conv_ln_b, gm_ln_g=gm_ln_g, gm_ln_b=gm_ln_b, gm_w_s=gm_w_s, gm_b_s=gm_b_s, w_out=w_out,
                   norm_xa_g=norm_xa_g, mem_norm_g=mem_norm_g, xa_wq=xa_wq, xa_wkv=xa_wkv, xa_wo=xa_wo,
                   norm_ffn_g=norm_ffn_g, ffn_w_gate_up=ffn_w_gate_up, ffn_w_down=ffn_w_down,
                   final_norm_g=final_norm_g)
    m_in = dict(norm_mix_g=m_norm_mix_g, w_in=m_w_in, b_in=m_b_in, conv_w=m_conv_w, conv_b=m_conv_b,
                conv_ln_g=m_conv_ln_g, conv_ln_b=m_conv_ln_b, gm_ln_g=m_gm_ln_g, gm_ln_b=m_gm_ln_b, gm_w_s=m_gm_w_s,
                gm_b_s=m_gm_b_s, w_out=m_w_out, norm_xa_g=m_norm_xa_g, mem_norm_g=m_mem_norm_g, xa_wq=m_xa_wq,
                xa_wkv=m_xa_wkv, xa_wo=m_xa_wo, norm_ffn_g=m_norm_ffn_g, ffn_w_gate_up=m_ffn_w_gate_up,
                ffn_w_down=m_ffn_w_down, final_norm_g=m_final_norm_g)
    v_in = dict(norm_mix_g=v_norm_mix_g, w_in=v_w_in, b_in=v_b_in, conv_w=v_conv_w, conv_b=v_conv_b,
                conv_ln_g=v_conv_ln_g, conv_ln_b=v_conv_ln_b, gm_ln_g=v_gm_ln_g, gm_ln_b=v_gm_ln_b, gm_w_s=v_gm_w_s,
                gm_b_s=v_gm_b_s, w_out=v_w_out, norm_xa_g=v_norm_xa_g, mem_norm_g=v_mem_norm_g, xa_wq=v_xa_wq,
                xa_wkv=v_xa_wkv, xa_wo=v_xa_wo, norm_ffn_g=v_norm_ffn_g, ffn_w_gate_up=v_ffn_w_gate_up,
                ffn_w_down=v_ffn_w_down, final_norm_g=v_final_norm_g)
    delta, new_m, new_v = {}, {}, {}
    for nm in big_names:
        delta[nm], new_m[nm], new_v[nm] = _adamw(weights[nm], grads[nm], m_in[nm], v_in[nm], "adamw_" + nm)
    local_shapes = [weights[nm].shape for nm in small_names]
    adam_rows = _rows_for(local_shapes, LANES)
    packed = [_pack([src[nm] for nm in small_names], adam_rows) for src in (weights, grads, m_in, v_in)]
    outs = _adamw(*packed, "adamw_small")
    for dst, buf in zip((delta, new_m, new_v), outs):
        for nm, a in zip(small_names, _unpack(buf, local_shapes)):
            dst[nm] = a

    order = ["norm_mix_g", "w_in", "b_in", "conv_w", "conv_b", "conv_ln_g", "conv_ln_b", "gm_ln_g", "gm_ln_b",
             "gm_w_s", "gm_b_s", "w_out", "norm_xa_g", "mem_norm_g", "xa_wq", "xa_wkv", "xa_wo", "norm_ffn_g",
             "ffn_w_gate_up", "ffn_w_down", "final_norm_g"]
    fit = lambda a, nm: a.reshape(weights[nm].shape)
    return (loss, gx.reshape(x.shape),
            *[fit(grads[nm], nm) for nm in order], *[fit(delta[nm], nm) for nm in order],
            *[fit(new_m[nm], nm) for nm in order], *[fit(new_v[nm], nm) for nm in order])
```

```python
import functools

import jax
import jax.numpy as jnp
from jax import lax
from jax.experimental import pallas as pl
from jax.experimental.pallas import tpu as pltpu

F32 = jnp.float32
BF16 = jnp.bfloat16

D_MODEL = 1024
CONV_WIDTH = 512
GM_WIDTH = 512
CONV_KERNEL = 31
CONV_HALO = 32
CHUNK = 128
GM_HEADS = 8
GM_HEAD_DIM = 64
XA_HEADS = 4
XA_HEAD_DIM = 256
FFN_HIDDEN = 2816
FFN_HALF = FFN_HIDDEN // 2
RMS_EPS = 1e-6
LN_EPS = 1e-5
N_CHIPS = 4
LANES = 128

ADAM_LR = 0.001
ADAM_B1 = 0.9
ADAM_B2 = 0.999
ADAM_EPS = 1e-08
ADAM_WD = 0.01
ADAM_STEP = 10

VMEM_LIMIT_BYTES = 56 * 1024 * 1024
MESH = pl.DeviceIdType.MESH
ANY = pl.BlockSpec(memory_space=pl.ANY)
HBM_SPEC = pl.BlockSpec(memory_space=pltpu.HBM)
SEM_SPEC = pl.BlockSpec(memory_space=pltpu.SEMAPHORE)

_NT = (((1,), (1,)), ((), ()))
_TN = (((0,), (0,)), ((), ()))
_GELU_C = 0.7978845608028654
_GELU_A = 0.044715


def _dot(a, b):
    return jnp.dot(a, b, preferred_element_type=F32)


def _dot_nt(a, b):
    return lax.dot_general(a, b, _NT, preferred_element_type=F32)


def _dot_tn(a, b):
    return lax.dot_general(a, b, _TN, preferred_element_type=F32)


def _mean(v):
    return jnp.mean(v, axis=-1, keepdims=True)


def _rowsum(v):
    return jnp.sum(v, axis=0, keepdims=True)


def _sigmoid(v):
    return 1.0 / (1.0 + jnp.exp(-v))


def _gelu_parts(v):
    v2 = v * v
    t = jnp.tanh(_GELU_C * (v + _GELU_A * v * v2))
    g = 0.5 * v * (1.0 + t)
    dg = 0.5 * (1.0 + t) + 0.5 * v * (1.0 - t * t) * (_GELU_C * (1.0 + 3.0 * _GELU_A * v2))
    return g, dg


def _rms_stats(v):
    return lax.rsqrt(_mean(v * v) + RMS_EPS)


def _rms_bwd(dy, v, r, g):
    n = v * r
    dn = dy * g
    dv = r * (dn - n * _mean(dn * n))
    return dv, _rowsum(dy * n)


def _ln_stats(v):
    mu = _mean(v)
    xc = v - mu
    rs = lax.rsqrt(_mean(xc * xc) + LN_EPS)
    return xc * rs, rs


def _ln_bwd(dy, xh, rs, g):
    dxh = dy * g
    dv = rs * (dxh - _mean(dxh) - xh * _mean(dxh * xh))
    return dv, _rowsum(dy * xh), _rowsum(dy)


def _params(sem):
    return pltpu.CompilerParams(dimension_semantics=sem, vmem_limit_bytes=VMEM_LIMIT_BYTES)


def _row_tile(s):
    return 512 if s % 512 == 0 and s >= 2048 else 128


def _mesh_pos():
    return lax.axis_index("x"), lax.axis_index("y"), lax.axis_index("c")


def _cast_into_slot(w, pos, dtype, name):
    _, h, c = w.shape

    def body(pos_ref, w_ref, o_ref):
        o_ref[0] = w_ref[...].astype(dtype)

    return pl.pallas_call(
        body, name=name,
        grid_spec=pltpu.PrefetchScalarGridSpec(
            num_scalar_prefetch=1, grid=(2,),
            in_specs=[pl.BlockSpec((1, h, c), lambda i, p: (i, 0, 0))],
            out_specs=pl.BlockSpec((1, 1, h, c), lambda i, p: (p[0], i, 0, 0))),
        out_shape=jax.ShapeDtypeStruct((N_CHIPS, 2, h, c), dtype),
        compiler_params=_params(("parallel",)),
    )(pos, w)


def _adam_rows(r, c):
    for tr in (r, 1024, 704, 640, 512, 352, 320, 256, 128, 64, 32, 16, 8):
        if r % tr == 0 and tr * c * 4 <= (3 << 19):
            return tr
    return r


def _adamw(w, g, m, v, name):
    r, c = w.shape
    tr = _adam_rows(r, c)

    def body(w_ref, g_ref, m_ref, v_ref, d_ref, nm_ref, nv_ref):
        gv = g_ref[...]
        nm = ADAM_B1 * m_ref[...] + (1.0 - ADAM_B1) * gv
        nv = ADAM_B2 * v_ref[...] + (1.0 - ADAM_B2) * (gv * gv)
        m_hat = nm / (1.0 - ADAM_B1 ** ADAM_STEP)
        v_hat = nv / (1.0 - ADAM_B2 ** ADAM_STEP)
        d_ref[...] = -ADAM_LR * (m_hat / (jnp.sqrt(v_hat) + ADAM_EPS) + ADAM_WD * w_ref[...])
        nm_ref[...] = nm
        nv_ref[...] = nv

    spec = pl.BlockSpec((tr, c), lambda i: (i, 0))
    shp = jax.ShapeDtypeStruct((r, c), F32)
    return pl.pallas_call(
        body, name=name, grid=(r // tr,),
        in_specs=[spec] * 4, out_specs=[spec] * 3, out_shape=[shp] * 3,
        compiler_params=_params(("parallel",)),
    )(w, g, m, v)


def _as_tuple(after):
    return tuple(after) if isinstance(after, (tuple, list)) else (after,)


def _tied_call(body, after, *, in_specs, **kwargs):
    after = _as_tuple(after)
    n_in, n_after = len(in_specs), len(after)

    def tied(*refs):
        body(*refs[:n_in], *refs[n_in + n_after:])

    call = pl.pallas_call(tied, in_specs=list(in_specs) + [ANY] * n_after, **kwargs)
    return lambda *operands: call(*operands, *after)


def _other_chips(x, y):
    return [(1 - x, y), (x, 1 - y), (1 - x, 1 - y)]


def _gather_shards(bufs):
    n = len(bufs)

    def body(*refs):
        outs = refs[n:2 * n]
        ici_send, ici_recv, fwd_send, fwd_recv = refs[2 * n:]
        x, y, c = _mesh_pos()
        me = 2 * x + y
        sib = (x, y, 1 - c)
        chips = _other_chips(x, y)

        def ici(a, k, chip_idx, to):
            return pltpu.make_async_remote_copy(
                src_ref=outs[a].at[chip_idx, c], dst_ref=outs[a].at[chip_idx, c],
                send_sem=ici_send.at[a, k], recv_sem=ici_recv.at[a, k],
                device_id=to, device_id_type=MESH)

        def fwd(a, k, chip_idx, half):
            return pltpu.make_async_remote_copy(
                src_ref=outs[a].at[chip_idx, half], dst_ref=outs[a].at[chip_idx, half],
                send_sem=fwd_send.at[a, k], recv_sem=fwd_recv.at[a, k],
                device_id=sib, device_id_type=MESH)

        sends = [ici(a, k, me, (*chips[k], c)) for a in range(n) for k in range(3)]
        for cp in sends:
            cp.start()
        passed = []
        for a in range(n):
            for k in range(3):
                ck = 2 * chips[k][0] + chips[k][1]
                ici(a, k, ck, (*chips[k], c)).wait_recv()
                cp = fwd(a, k, ck, c)
                cp.start()
                passed.append(cp)
        for a in range(n):
            for k in range(3):
                ck = 2 * chips[k][0] + chips[k][1]
                fwd(a, k, ck, 1 - c).wait_recv()
        for cp in sends + passed:
            cp.wait_send()

    out_shape = [jax.ShapeDtypeStruct(b.shape, b.dtype) for b in bufs]
    return pl.pallas_call(
        body, name="gather_weights",
        in_specs=[ANY] * n, out_specs=[ANY] * n, out_shape=out_shape,
        input_output_aliases={a: a for a in range(n)},
        scratch_shapes=[pltpu.SemaphoreType.DMA((n, 3))] * 4,
    )(*bufs)


def _gather_descriptors(bufs, send_of, recv_of):
    x, y, c = _mesh_pos()
    me = 2 * x + y
    chips = _other_chips(x, y)
    sends, arrivals = [], []
    for a in range(len(bufs)):
        for k in range(3):
            ck = 2 * chips[k][0] + chips[k][1]

            def copy(slot, a=a, k=k):
                return pltpu.make_async_remote_copy(
                    src_ref=bufs[a].at[slot, c], dst_ref=bufs[a].at[slot, c],
                    send_sem=send_of(a, k), recv_sem=recv_of(a, k),
                    device_id=(*chips[k], c), device_id_type=MESH)

            sends.append(functools.partial(copy, me))
            arrivals.append(functools.partial(copy, ck))
    return sends, arrivals


def _gather_start(bufs, name, after=()):
    n = len(bufs)
    ns = 3 * n

    def body(*refs):
        sems = refs[n:n + 2 * ns]
        thru = refs[n + 2 * ns:2 * n + 2 * ns]
        token = refs[2 * n + 2 * ns]
        sends, _ = _gather_descriptors(thru, lambda a, k: sems[3 * a + k], lambda a, k: sems[ns + 3 * a + k])
        for cp in sends:
            cp().start()
        token[...] = jnp.zeros_like(token)

    held = [pltpu.with_memory_space_constraint(b, pltpu.HBM) for b in bufs]
    out = _tied_call(
        body, after, name=name,
        out_shape=(*[pltpu.SemaphoreType.DMA(())] * (2 * ns), *[pltpu.HBM(b.shape, b.dtype) for b in held],
                   jax.ShapeDtypeStruct((8, LANES), F32)),
        in_specs=[HBM_SPEC] * n,
        out_specs=(*[SEM_SPEC] * (2 * ns), *[HBM_SPEC] * n, pl.BlockSpec(memory_space=pltpu.VMEM)),
        input_output_aliases={i: 2 * ns + i for i in range(n)},
        compiler_params=pltpu.CompilerParams(has_side_effects=pltpu.SideEffectType.DATAFLOW_SIDE_EFFECTING),
    )(*held)
    return list(out[:ns]), list(out[ns:2 * ns]), list(out[2 * ns:2 * ns + n]), out[2 * ns + n]


def _gather_wait(send_sems, recv_sems, bufs, after, name):
    n = len(bufs)
    ns = 3 * n

    def body(*refs):
        buf_ref = refs[:n]
        sem_ref = refs[n:n + 2 * ns]
        sends, arrivals = _gather_descriptors(buf_ref, lambda a, k: sem_ref[3 * a + k],
                                              lambda a, k: sem_ref[ns + 3 * a + k])
        for cp in sends:
            cp().wait_send()
        for cp in arrivals:
            cp().wait_recv()

    out = pl.pallas_call(
        body, name=name,
        out_shape=tuple(pltpu.HBM(b.shape, b.dtype) for b in bufs),
        in_specs=[HBM_SPEC] * n + [SEM_SPEC] * (2 * ns) + [ANY],
        out_specs=tuple([HBM_SPEC] * n),
        input_output_aliases={i: i for i in range(n)},
        compiler_params=pltpu.CompilerParams(has_side_effects=pltpu.SideEffectType.DATAFLOW_SIDE_EFFECTING),
    )(*bufs, *send_sems, *recv_sems, after)
    return list(out)


def _pass_to_sibling(bufs, name):
    n = len(bufs)

    def body(*refs):
        outs = refs[n:2 * n]
        send_sem, recv_sem = refs[2 * n:]
        x, y, c = _mesh_pos()
        chips = _other_chips(x, y)

        def half(a, k, which):
            ck = 2 * chips[k][0] + chips[k][1]
            return pltpu.make_async_remote_copy(
                src_ref=outs[a].at[ck, which], dst_ref=outs[a].at[ck, which],
                send_sem=send_sem.at[a, k], recv_sem=recv_sem.at[a, k],
                device_id=(x, y, 1 - c), device_id_type=MESH)

        sends = [half(a, k, c) for a in range(n) for k in range(3)]
        for cp in sends:
            cp.start()
        for a in range(n):
            for k in range(3):
                half(a, k, 1 - c).wait_recv()
        for cp in sends:
            cp.wait_send()

    return pl.pallas_call(
        body, name=name,
        in_specs=[ANY] * n, out_specs=[ANY] * n,
        out_shape=[jax.ShapeDtypeStruct(b.shape, b.dtype) for b in bufs],
        input_output_aliases={a: a for a in range(n)},
        scratch_shapes=[pltpu.SemaphoreType.DMA((n, 3))] * 2,
    )(*bufs)


def _swap_halves(grads, name):
    n = len(grads)

    def body(*refs):
        ins, outs = refs[:n], refs[n:2 * n]
        send_sem, recv_sem = refs[2 * n:]
        x, y, c = _mesh_pos()
        cps = [pltpu.make_async_remote_copy(
            src_ref=ins[a].at[:, pl.ds(1 - c, 1)], dst_ref=outs[a],
            send_sem=send_sem.at[a], recv_sem=recv_sem.at[a],
            device_id=(x, y, 1 - c), device_id_type=MESH) for a in range(n)]
        for cp in cps:
            cp.start()
        for cp in cps:
            cp.wait()

    out_shape = [jax.ShapeDtypeStruct((g.shape[0], 1) + g.shape[2:], g.dtype) for g in grads]
    return pl.pallas_call(
        body, name=name,
        in_specs=[ANY] * n, out_specs=[ANY] * n, out_shape=out_shape,
        scratch_shapes=[pltpu.SemaphoreType.DMA((n,))] * 2,
    )(*grads)


def _swap_descriptors(grads, lands, send_of, recv_of):
    x, y, c = _mesh_pos()
    return [functools.partial(
        pltpu.make_async_remote_copy,
        src_ref=grads[a].at[:, pl.ds(1 - c, 1)], dst_ref=lands[a],
        send_sem=send_of(a), recv_sem=recv_of(a),
        device_id=(x, y, 1 - c), device_id_type=MESH) for a in range(len(grads))]


def _swap_start(grads, name):
    n = len(grads)

    def body(*refs):
        sems = refs[2 * n:4 * n]
        g_thru, l_thru = refs[4 * n:5 * n], refs[5 * n:6 * n]
        token = refs[6 * n]
        for cp in _swap_descriptors(g_thru, l_thru, lambda a: sems[a], lambda a: sems[n + a]):
            cp().start()
        token[...] = jnp.zeros_like(token)

    lands = [lax.empty((g.shape[0], 1) + g.shape[2:], g.dtype) for g in grads]
    held = [pltpu.with_memory_space_constraint(a, pltpu.HBM) for a in (*grads, *lands)]
    out = pl.pallas_call(
        body, name=name,
        out_shape=(*[pltpu.SemaphoreType.DMA(())] * (2 * n), *[pltpu.HBM(a.shape, a.dtype) for a in held],
                   jax.ShapeDtypeStruct((8, LANES), F32)),
        in_specs=[HBM_SPEC] * (2 * n),
        out_specs=(*[SEM_SPEC] * (2 * n), *[HBM_SPEC] * (2 * n), pl.BlockSpec(memory_space=pltpu.VMEM)),
        input_output_aliases={i: 2 * n + i for i in range(2 * n)},
        compiler_params=pltpu.CompilerParams(has_side_effects=pltpu.SideEffectType.DATAFLOW_SIDE_EFFECTING),
    )(*held)
    return list(out[:2 * n]), list(out[2 * n:3 * n]), list(out[3 * n:4 * n]), out[4 * n]


def _swap_wait(sems, grads, lands, after, name):
    n = len(grads)

    def body(*refs):
        g_ref, l_ref = refs[:n], refs[n:2 * n]
        sem_ref = refs[2 * n:4 * n]
        for cp in _swap_descriptors(g_ref, l_ref, lambda a: sem_ref[a], lambda a: sem_ref[n + a]):
            cp().wait()

    out = pl.pallas_call(
        body, name=name,
        out_shape=tuple(pltpu.HBM(a.shape, a.dtype) for a in (*grads, *lands)),
        in_specs=[HBM_SPEC] * (2 * n) + [SEM_SPEC] * (2 * n) + [ANY],
        out_specs=tuple([HBM_SPEC] * (2 * n)),
        input_output_aliases={i: i for i in range(2 * n)},
        compiler_params=pltpu.CompilerParams(has_side_effects=pltpu.SideEffectType.DATAFLOW_SIDE_EFFECTING),
    )(*grads, *lands, *sems, after)
    return list(out[:n]), list(out[n:])


def _add_half(g, got, pos, name):
    j, _, h, c = g.shape

    def body(pos_ref, g_ref, r_ref, o_ref, p_ref):
        val = (g_ref[0, 0] + r_ref[0, 0]).astype(BF16)
        o_ref[0] = val
        if j == 1:
            p_ref[0] = val
        else:
            @pl.when(pl.program_id(0) == pos_ref[0])
            def _():
                p_ref[0] = val

    return pl.pallas_call(
        body, name=name,
        grid_spec=pltpu.PrefetchScalarGridSpec(
            num_scalar_prefetch=1, grid=(j,),
            in_specs=[pl.BlockSpec((1, 1, h, c), lambda i, p: (i, p[1], 0, 0)),
                      pl.BlockSpec((1, 1, h, c), lambda i, p: (i, 0, 0, 0))],
            out_specs=[pl.BlockSpec((1, h, c), lambda i, p: (i, 0, 0)),
                       pl.BlockSpec((1, h, c), lambda i, p: (p[0], 0, 0))]),
        out_shape=[jax.ShapeDtypeStruct((j, h, c), BF16), jax.ShapeDtypeStruct((N_CHIPS, h, c), BF16)],
        compiler_params=_params(("arbitrary",)),
    )(pos, g, got)


def _exchange_chip_sums(sums, parts):
    n = len(sums)

    def body(*refs):
        ins, outs = refs[:n], refs[2 * n:3 * n]
        send_sem, recv_sem = refs[3 * n:]
        sends, arrivals = _exchange_descriptors(ins, outs, lambda a, k: send_sem.at[a, k],
                                                lambda a, k: recv_sem.at[a, k])
        for cp in sends:
            cp().start()
        for cp in arrivals:
            cp().wait_recv()
        for cp in sends:
            cp().wait_send()

    out_shape = [jax.ShapeDtypeStruct(p.shape, p.dtype) for p in parts]
    return pl.pallas_call(
        body, name="exchange_chip_sums",
        in_specs=[ANY] * (2 * n), out_specs=[ANY] * n, out_shape=out_shape,
        input_output_aliases={n + a: a for a in range(n)},
        scratch_shapes=[pltpu.SemaphoreType.DMA((n, 3))] * 2,
    )(*sums, *parts)


def _exchange_descriptors(sums, parts, send_of, recv_of):
    x, y, c = _mesh_pos()
    me = 2 * x + y
    chips = _other_chips(x, y)
    sends, arrivals = [], []
    for a in range(len(sums)):
        for k in range(3):
            ck = 2 * chips[k][0] + chips[k][1]
            mine = sums[a].at[ck] if sums[a].shape[0] == N_CHIPS else sums[a].at[0]

            def copy(dst_slot, a=a, k=k, mine=mine):
                return pltpu.make_async_remote_copy(
                    src_ref=mine, dst_ref=parts[a].at[dst_slot],
                    send_sem=send_of(a, k), recv_sem=recv_of(a, k),
                    device_id=(*chips[k], c), device_id_type=MESH)

            sends.append(functools.partial(copy, me))
            arrivals.append(functools.partial(copy, ck))
    return sends, arrivals


def _exchange_start(sums, parts, name):
    n = len(sums)
    ns = 3 * n

    def body(*refs):
        sems = refs[2 * n:2 * n + 2 * ns]
        sums_thru = refs[2 * n + 2 * ns:3 * n + 2 * ns]
        parts_thru = refs[3 * n + 2 * ns:4 * n + 2 * ns]
        token = refs[4 * n + 2 * ns]
        sends, _ = _exchange_descriptors(sums_thru, parts_thru, lambda a, k: sems[3 * a + k],
                                         lambda a, k: sems[ns + 3 * a + k])
        for cp in sends:
            cp().start()
        token[...] = jnp.zeros_like(token)

    hbm = lambda a: pltpu.HBM(a.shape, a.dtype)
    held = [pltpu.with_memory_space_constraint(a, pltpu.HBM) for a in (*sums, *parts)]
    out = pl.pallas_call(
        body, name=name,
        out_shape=(*[pltpu.SemaphoreType.DMA(())] * (2 * ns), *[hbm(a) for a in held],
                   jax.ShapeDtypeStruct((8, LANES), F32)),
        in_specs=[HBM_SPEC] * (2 * n),
        out_specs=(*[SEM_SPEC] * (2 * ns), *[HBM_SPEC] * (2 * n), pl.BlockSpec(memory_space=pltpu.VMEM)),
        input_output_aliases={i: 2 * ns + i for i in range(2 * n)},
        compiler_params=pltpu.CompilerParams(has_side_effects=pltpu.SideEffectType.DATAFLOW_SIDE_EFFECTING),
    )(*held)
    return (list(out[:2 * ns]), list(out[2 * ns:2 * ns + n]), list(out[2 * ns + n:2 * ns + 2 * n]),
            out[2 * ns + 2 * n])


def _exchange_wait(sems, sums, parts, after, name):
    n = len(sums)
    ns = 3 * n

    def body(*refs):
        sums_ref, parts_ref = refs[:n], refs[n:2 * n]
        sem_ref = refs[2 * n:2 * n + 2 * ns]
        sends, arrivals = _exchange_descriptors(sums_ref, parts_ref, lambda a, k: sem_ref[3 * a + k],
                                                lambda a, k: sem_ref[ns + 3 * a + k])
        for cp in sends:
            cp().wait_send()
        for cp in arrivals:
            cp().wait_recv()

    hbm = lambda a: pltpu.HBM(a.shape, a.dtype)
    out = pl.pallas_call(
        body, name=name,
        out_shape=tuple(hbm(a) for a in (*sums, *parts)),
        in_specs=[HBM_SPEC] * (2 * n) + [SEM_SPEC] * (2 * ns) + [ANY] * len(_as_tuple(after)),
        out_specs=tuple([HBM_SPEC] * (2 * n)),
        input_output_aliases={i: i for i in range(2 * n)},
        compiler_params=pltpu.CompilerParams(has_side_effects=pltpu.SideEffectType.DATAFLOW_SIDE_EFFECTING),
    )(*sums, *parts, *sems, *_as_tuple(after))
    return list(out[n:])


def _sum_chips(parts, pos, name):
    _, h, c = parts.shape

    def body(pos_ref, p_ref, o_ref):
        o_ref[0] = ((p_ref[0].astype(F32) + p_ref[1].astype(F32)) + p_ref[2].astype(F32)) + p_ref[3].astype(F32)

    return pl.pallas_call(
        body, name=name,
        grid_spec=pltpu.PrefetchScalarGridSpec(
            num_scalar_prefetch=1, grid=(1,),
            in_specs=[pl.BlockSpec((N_CHIPS, h, c), lambda i, p: (0, 0, 0))],
            out_specs=pl.BlockSpec((1, h, c), lambda i, p: (p[1], 0, 0))),
        out_shape=jax.ShapeDtypeStruct((2, h, c), F32),
        compiler_params=_params(("arbitrary",)),
    )(pos, parts)


def _join_halves(fulls):
    n = len(fulls)

    def body(*refs):
        outs = refs[n:2 * n]
        send_sem, recv_sem = refs[2 * n:]
        x, y, c = _mesh_pos()

        def half(a, which):
            return pltpu.make_async_remote_copy(
                src_ref=outs[a].at[which], dst_ref=outs[a].at[which],
                send_sem=send_sem.at[a], recv_sem=recv_sem.at[a],
                device_id=(x, y, 1 - c), device_id_type=MESH)

        sends = [half(a, c) for a in range(n)]
        for cp in sends:
            cp.start()
        for a in range(n):
            half(a, 1 - c).wait_recv()
        for cp in sends:
            cp.wait_send()

    out_shape = [jax.ShapeDtypeStruct(f.shape, f.dtype) for f in fulls]
    return pl.pallas_call(
        body, name="join_halves",
        in_specs=[ANY] * n, out_specs=[ANY] * n, out_shape=out_shape,
        input_output_aliases={a: a for a in range(n)},
        scratch_shapes=[pltpu.SemaphoreType.DMA((n,))] * 2,
    )(*fulls)


def _mix_in(x, g, w_in, b_in, ts, after=()):
    s = x.shape[0]

    def body(x_ref, g_ref, w_ref, b_ref, z_ref, hn_ref):
        xv = x_ref[...]
        hn = (xv * _rms_stats(xv) * g_ref[...]).astype(BF16)
        hn_ref[...] = hn
        for j in range(4):
            cols = slice(j * 512, (j + 1) * 512)
            z_ref[:, cols] = _dot(hn, w_ref[j]) + b_ref[:, cols]

    return _tied_call(
        body, after, name="mix_in", grid=(s // ts,),
        in_specs=[pl.BlockSpec((ts, D_MODEL), lambda i: (i, 0)),
                  pl.BlockSpec((1, D_MODEL), lambda i: (0, 0)),
                  pl.BlockSpec((4, D_MODEL, 512), lambda i: (0, 0, 0)),
                  pl.BlockSpec((1, 2048), lambda i: (0, 0))],
        out_specs=[pl.BlockSpec((ts, 2048), lambda i: (i, 0)),
                   pl.BlockSpec((ts, D_MODEL), lambda i: (i, 0))],
        out_shape=[jax.ShapeDtypeStruct((s, 2048), F32), jax.ShapeDtypeStruct((s, D_MODEL), BF16)],
        compiler_params=_params(("parallel",)),
    )(x, g, w_in, b_in)


def _lane_is_low_head():
    lane = lax.broadcasted_iota(jnp.int32, (1, GM_WIDTH), 1)
    return (lane & GM_HEAD_DIM) == 0


def _gm_mix(v_lo, v_hi, wpair_ref, bias_ref, mixed_ref, t):
    for n in range(t // CHUNK):
        rows = slice(n * CHUNK, (n + 1) * CHUNK)
        for j in range(GM_HEADS // 2):
            cols = slice(j * LANES, (j + 1) * LANES)
            rhs = jnp.concatenate([v_lo[rows, cols], v_hi[rows, cols]], axis=0)
            mixed_ref[rows, cols] = _dot(wpair_ref[j], rhs) + bias_ref[:, cols]


def _seqmix_fwd(z, cw, cb, lng, lnb, gg, gb, wpair, bias, t):
    s = z.shape[0]

    def body(z_ref, cw_ref, cb_ref, lng_ref, lnb_ref, gg_ref, gb_ref, wpair_ref, bias_ref,
             mix_ref, c1_ref, abuf, mixed_ref):
        i = pl.program_id(0)

        @pl.when(i == 0)
        def _():
            abuf[0:CONV_HALO, :] = jnp.zeros((CONV_HALO, CONV_WIDTH), F32)

        @pl.when(i > 0)
        def _():
            abuf[0:CONV_HALO, :] = abuf[t:t + CONV_HALO, :]

        abuf[CONV_HALO:, :] = z_ref[:, 0:512] * _sigmoid(z_ref[:, 512:1024])
        acc = jnp.zeros((t, CONV_WIDTH), F32)
        for k in range(CONV_KERNEL):
            acc = acc + cw_ref[k:k + 1, :] * abuf[pl.ds(CONV_HALO - (CONV_KERNEL - 1) + k, t), :]
        c1 = acc + cb_ref[...]
        c1_ref[...] = c1
        xh, _ = _ln_stats(c1)
        ln = xh * lng_ref[...] + lnb_ref[...]
        mix_ref[:, 0:512] = (ln * _sigmoid(ln)).astype(BF16)

        u, _ = _gelu_parts(z_ref[:, 1024:1536])
        gv, _ = _gelu_parts(z_ref[:, 1536:2048])
        vxh, _ = _ln_stats(gv)
        v = vxh * gg_ref[...] + gb_ref[...]
        low = _lane_is_low_head()
        v_lo = jnp.where(low, v, 0.0).astype(BF16)
        v_hi = jnp.where(low, 0.0, v).astype(BF16)
        _gm_mix(v_lo, v_hi, wpair_ref, bias_ref, mixed_ref, t)
        mix_ref[:, 512:1024] = (u * mixed_ref[...]).astype(BF16)

    vec = lambda n: pl.BlockSpec((1, n), lambda i: (0, 0))
    return pl.pallas_call(
        body, name="seqmix_fwd", grid=(s // t,),
        in_specs=[pl.BlockSpec((t, 2048), lambda i: (i, 0)),
                  pl.BlockSpec((CONV_HALO, CONV_WIDTH), lambda i: (0, 0)),
                  vec(512), vec(512), vec(512), vec(512), vec(512),
                  pl.BlockSpec((4, CHUNK, 2 * CHUNK), lambda i: (0, 0, 0)),
                  pl.BlockSpec((CHUNK, GM_WIDTH), lambda i: (0, 0))],
        out_specs=[pl.BlockSpec((t, D_MODEL), lambda i: (i, 0)),
                   pl.BlockSpec((t, CONV_WIDTH), lambda i: (i, 0))],
        out_shape=[jax.ShapeDtypeStruct((s, D_MODEL), BF16), jax.ShapeDtypeStruct((s, CONV_WIDTH), F32)],
        scratch_shapes=[pltpu.VMEM((t + CONV_HALO, CONV_WIDTH), F32), pltpu.VMEM((t, GM_WIDTH), F32)],
        compiler_params=_params(("arbitrary",)),
    )(z, cw, cb, lng, lnb, gg, gb, wpair, bias)


def _out_proj_q(x, mix, w_out, g, wq, ts, after=()):
    s = x.shape[0]

    def body(x_ref, mix_ref, wo_ref, g_ref, wq_ref, h1_ref, hn_ref, q_ref):
        h1 = x_ref[...] + _dot(mix_ref[...], wo_ref[...])
        h1_ref[...] = h1
        hn = (h1 * _rms_stats(h1) * g_ref[...]).astype(BF16)
        hn_ref[...] = hn
        q_ref[...] = _dot(hn, wq_ref[...]).astype(BF16)

    row = lambda dt: pl.BlockSpec((ts, D_MODEL), lambda i: (i, 0))
    full = pl.BlockSpec((D_MODEL, D_MODEL), lambda i: (0, 0))
    return _tied_call(
        body, after, name="out_proj_q", grid=(s // ts,),
        in_specs=[row(F32), row(BF16), full, pl.BlockSpec((1, D_MODEL), lambda i: (0, 0)), full],
        out_specs=[row(F32), row(BF16), row(BF16)],
        out_shape=[jax.ShapeDtypeStruct((s, D_MODEL), F32), jax.ShapeDtypeStruct((s, D_MODEL), BF16),
                   jax.ShapeDtypeStruct((s, D_MODEL), BF16)],
        compiler_params=_params(("parallel",)),
    )(x, mix, w_out, g, wq)


def _mem_kv(mem, g, wkv):
    m = mem.shape[0]

    def body(mem_ref, g_ref, w_ref, mn_ref, kv_ref):
        mv = mem_ref[...]
        mn = (mv * _rms_stats(mv) * g_ref[...]).astype(BF16)
        mn_ref[...] = mn
        for j in range(4):
            kv_ref[:, j * 512:(j + 1) * 512] = _dot(mn, w_ref[j]).astype(BF16)

    return pl.pallas_call(
        body, name="mem_kv",
        out_shape=[jax.ShapeDtypeStruct((m, D_MODEL), BF16), jax.ShapeDtypeStruct((m, 2 * D_MODEL), BF16)],
        compiler_params=pltpu.CompilerParams(vmem_limit_bytes=VMEM_LIMIT_BYTES),
    )(mem, g, wkv)


def _softmax_rows(sc):
    e = jnp.exp(sc - jnp.max(sc, axis=-1, keepdims=True))
    return e / jnp.sum(e, axis=-1, keepdims=True)


def _attn_fwd(q, kv, h1, wo, g, ts):
    s, m = q.shape[0], kv.shape[0]
    scale = XA_HEAD_DIM ** -0.5

    def body(q_ref, kv_ref, h1_ref, wo_ref, g_ref, o_ref, h2_ref, hn_ref):
        for h in range(XA_HEADS):
            cols = slice(h * XA_HEAD_DIM, (h + 1) * XA_HEAD_DIM)
            vcols = slice(D_MODEL + h * XA_HEAD_DIM, D_MODEL + (h + 1) * XA_HEAD_DIM)
            p = _softmax_rows(_dot_nt(q_ref[:, cols], kv_ref[:, cols]) * scale)
            o_ref[:, cols] = _dot(p.astype(BF16), kv_ref[:, vcols]).astype(BF16)
        h2 = h1_ref[...] + _dot(o_ref[...], wo_ref[...])
        h2_ref[...] = h2
        hn_ref[...] = (h2 * _rms_stats(h2) * g_ref[...]).astype(BF16)

    row = pl.BlockSpec((ts, D_MODEL), lambda i: (i, 0))
    return pl.pallas_call(
        body, name="attn_fwd", grid=(s // ts,),
        in_specs=[row, pl.BlockSpec((m, 2 * D_MODEL), lambda i: (0, 0)), row,
                  pl.BlockSpec((D_MODEL, D_MODEL), lambda i: (0, 0)),
                  pl.BlockSpec((1, D_MODEL), lambda i: (0, 0))],
        out_specs=[row, row, row],
        out_shape=[jax.ShapeDtypeStruct((s, D_MODEL), BF16), jax.ShapeDtypeStruct((s, D_MODEL), F32),
                   jax.ShapeDtypeStruct((s, D_MODEL), BF16)],
        compiler_params=_params(("parallel",)),
    )(q, kv, h1, wo, g)


def _ffn_up(hn, wgu, ts, after=()):
    s = hn.shape[0]

    def body(hn_ref, w_ref, gu_ref, act_ref):
        hv = hn_ref[...]
        gate = _dot(hv, w_ref[0, 0])
        up = _dot(hv, w_ref[1, 0])
        gu_ref[0] = gate
        gu_ref[1] = up
        act_ref[...] = (gate * _sigmoid(gate) * up).astype(BF16)

    return _tied_call(
        body, after, name="ffn_up", grid=(2, s // ts),
        in_specs=[pl.BlockSpec((ts, D_MODEL), lambda j, i: (i, 0)),
                  pl.BlockSpec((2, 1, D_MODEL, FFN_HALF), lambda j, i: (0, j, 0, 0))],
        out_specs=[pl.BlockSpec((2, ts, FFN_HALF), lambda j, i: (0, i, j)),
                   pl.BlockSpec((ts, FFN_HALF), lambda j, i: (i, j))],
        out_shape=[jax.ShapeDtypeStruct((2, s, FFN_HIDDEN), F32), jax.ShapeDtypeStruct((s, FFN_HIDDEN), BF16)],
        compiler_params=_params(("parallel", "parallel")),
    )(hn, wgu)


def _ffn_down_loss(act, wd, h2, g, target, ts):
    s = act.shape[0]

    def body(act_ref, wd_ref, h2_ref, g_ref, t_ref, dh_ref, sq_ref, dg_ref):
        @pl.when(pl.program_id(0) == 0)
        def _():
            sq_ref[...] = jnp.zeros_like(sq_ref)
            dg_ref[...] = jnp.zeros_like(dg_ref)

        h3 = h2_ref[...] + _dot(act_ref[...], wd_ref[...])
        r = _rms_stats(h3)
        gv = g_ref[...]
        diff = h3 * r * gv - t_ref[...]
        sq_ref[...] += _rowsum(diff * diff)
        dh, dg = _rms_bwd(diff / D_MODEL, h3, r, gv)
        dh_ref[...] = dh
        dg_ref[...] += dg

    row = pl.BlockSpec((ts, D_MODEL), lambda i: (i, 0))
    vec = pl.BlockSpec((1, D_MODEL), lambda i: (0, 0))
    return pl.pallas_call(
        body, name="ffn_down_loss", grid=(s // ts,),
        in_specs=[pl.BlockSpec((ts, FFN_HIDDEN), lambda i: (i, 0)),
                  pl.BlockSpec((FFN_HIDDEN, D_MODEL), lambda i: (0, 0)), row, vec, row],
        out_specs=[row, vec, vec],
        out_shape=[jax.ShapeDtypeStruct((s, D_MODEL), F32), jax.ShapeDtypeStruct((1, D_MODEL), F32),
                   jax.ShapeDtypeStruct((1, D_MODEL), F32)],
        compiler_params=_params(("arbitrary",)),
    )(act, wd, h2, g, target)


def _grad_w(a, b, tk, tn, name, after=()):
    s, k = a.shape
    gb, _, n = b.shape
    nblk = n // tn
    tsr = 512 if s % 512 == 0 else s

    def body(a_ref, b_ref, o_ref):
        @pl.when(pl.program_id(2) == 0)
        def _():
            o_ref[...] = jnp.zeros_like(o_ref)

        o_ref[0] += _dot_tn(a_ref[...].astype(BF16), b_ref[0].astype(BF16))

    return _tied_call(
        body, after, name=name, grid=(gb * nblk, k // tk, s // tsr),
        in_specs=[pl.BlockSpec((tsr, tk), lambda ni, ki, si: (si, ki)),
                  pl.BlockSpec((1, tsr, tn), lambda ni, ki, si: (ni // nblk, si, ni % nblk))],
        out_specs=pl.BlockSpec((1, tk, tn), lambda ni, ki, si: (ni, ki, 0)),
        out_shape=jax.ShapeDtypeStruct((gb * nblk, k, tn), F32),
        compiler_params=_params(("parallel", "parallel", "arbitrary")),
    )(a, b)


def _ffn_bwd_act(dh3, wd, gu, ts):
    s = dh3.shape[0]

    def body(dh_ref, wd_ref, gu_ref, dgu_ref):
        dact = _dot_nt(dh_ref[...].astype(BF16), wd_ref[0])
        gate, up = gu_ref[0], gu_ref[1]
        sg = _sigmoid(gate)
        dgu_ref[0] = (dact * up * (sg * (1.0 + gate * (1.0 - sg)))).astype(BF16)
        dgu_ref[1] = (dact * (gate * sg)).astype(BF16)

    return pl.pallas_call(
        body, name="ffn_bwd_act", grid=(2, s // ts),
        in_specs=[pl.BlockSpec((ts, D_MODEL), lambda j, i: (i, 0)),
                  pl.BlockSpec((1, FFN_HALF, D_MODEL), lambda j, i: (j, 0, 0)),
                  pl.BlockSpec((2, ts, FFN_HALF), lambda j, i: (0, i, j))],
        out_specs=pl.BlockSpec((2, ts, FFN_HALF), lambda j, i: (0, i, j)),
        out_shape=jax.ShapeDtypeStruct((2, s, FFN_HIDDEN), BF16),
        compiler_params=_params(("parallel", "parallel")),
    )(dh3, wd, gu)


def _ffn_bwd_in(dgu, wgu, dh3, h2, g, ts, after=()):
    s = dh3.shape[0]

    def body(dgu_ref, w_ref, dh3_ref, h2_ref, g_ref, dh2_ref, dg_ref):
        @pl.when(pl.program_id(0) == 0)
        def _():
            dg_ref[...] = jnp.zeros_like(dg_ref)

        dhn = jnp.zeros((ts, D_MODEL), F32)
        for p in range(2):
            for j in range(2):
                dhn = dhn + _dot_nt(dgu_ref[p, :, j * FFN_HALF:(j + 1) * FFN_HALF], w_ref[2 * p + j])
        h2 = h2_ref[...]
        dv, dg = _rms_bwd(dhn, h2, _rms_stats(h2), g_ref[...])
        dh2_ref[...] = dh3_ref[...] + dv
        dg_ref[...] += dg

    row = pl.BlockSpec((ts, D_MODEL), lambda i: (i, 0))
    vec = pl.BlockSpec((1, D_MODEL), lambda i: (0, 0))
    return _tied_call(
        body, after, name="ffn_bwd_in", grid=(s // ts,),
        in_specs=[pl.BlockSpec((2, ts, FFN_HIDDEN), lambda i: (0, i, 0)),
                  pl.BlockSpec((4, D_MODEL, FFN_HALF), lambda i: (0, 0, 0)), row, row, vec],
        out_specs=[row, vec],
        out_shape=[jax.ShapeDtypeStruct((s, D_MODEL), F32), jax.ShapeDtypeStruct((1, D_MODEL), F32)],
        compiler_params=_params(("arbitrary",)),
    )(dgu, wgu, dh3, h2, g)


def _attn_bwd(dh2, wo, q, kv, wq, h1, g, ts, after=()):
    s, m = q.shape[0], kv.shape[0]
    scale = XA_HEAD_DIM ** -0.5

    def body(dh2_ref, wo_ref, q_ref, kv_ref, wq_ref, h1_ref, g_ref, dh1_ref, dq_ref, dkv_ref, dg_ref):
        @pl.when(pl.program_id(0) == 0)
        def _():
            dkv_ref[...] = jnp.zeros_like(dkv_ref)
            dg_ref[...] = jnp.zeros_like(dg_ref)

        do = _dot_nt(dh2_ref[...].astype(BF16), wo_ref[...]).astype(BF16)
        for h in range(XA_HEADS):
            cols = slice(h * XA_HEAD_DIM, (h + 1) * XA_HEAD_DIM)
            vcols = slice(D_MODEL + h * XA_HEAD_DIM, D_MODEL + (h + 1) * XA_HEAD_DIM)
            qh, kh, vh, doh = q_ref[:, cols], kv_ref[:, cols], kv_ref[:, vcols], do[:, cols]
            p = _softmax_rows(_dot_nt(qh, kh) * scale)
            dp = _dot_nt(doh, vh)
            ds = (p * (dp - jnp.sum(dp * p, axis=-1, keepdims=True)) * scale).astype(BF16)
            dq_ref[:, cols] = _dot(ds, kh).astype(BF16)
            dkv_ref[:, cols] += _dot_tn(ds, qh)
            dkv_ref[:, vcols] += _dot_tn(p.astype(BF16), doh)
        dhn = _dot_nt(dq_ref[...], wq_ref[...])
        h1 = h1_ref[...]
        dv, dg = _rms_bwd(dhn, h1, _rms_stats(h1), g_ref[...])
        dh1_ref[...] = dh2_ref[...] + dv
        dg_ref[...] += dg

    row = pl.BlockSpec((ts, D_MODEL), lambda i: (i, 0))
    full = pl.BlockSpec((D_MODEL, D_MODEL), lambda i: (0, 0))
    kvs = pl.BlockSpec((m, 2 * D_MODEL), lambda i: (0, 0))
    vec = pl.BlockSpec((1, D_MODEL), lambda i: (0, 0))
    return _tied_call(
        body, after, name="attn_bwd", grid=(s // ts,),
        in_specs=[row, full, row, kvs, full, row, vec],
        out_specs=[row, row, kvs, vec],
        out_shape=[jax.ShapeDtypeStruct((s, D_MODEL), F32), jax.ShapeDtypeStruct((s, D_MODEL), BF16),
                   jax.ShapeDtypeStruct((m, 2 * D_MODEL), F32), jax.ShapeDtypeStruct((1, D_MODEL), F32)],
        compiler_params=_params(("arbitrary",)),
    )(dh2, wo, q, kv, wq, h1, g)


def _mem_kv_bwd(dkv, mn, wkv, mem, g, after=()):
    m = mem.shape[0]

    def body(dkv_ref, mn_ref, w_ref, mem_ref, g_ref, dw_ref, dg_ref):
        dmn = jnp.zeros((m, D_MODEL), F32)
        mn = mn_ref[...]
        for j in range(4):
            dj = dkv_ref[:, j * 512:(j + 1) * 512].astype(BF16)
            dw_ref[j] = _dot_tn(mn, dj)
            dmn = dmn + _dot_nt(dj, w_ref[j])
        mv = mem_ref[...]
        dg_ref[...] = _rowsum(dmn * (mv * _rms_stats(mv)))

    return _tied_call(
        body, after, name="mem_kv_bwd", in_specs=[pl.BlockSpec(memory_space=pltpu.VMEM)] * 5,
        out_shape=[jax.ShapeDtypeStruct((4, D_MODEL, 512), F32), jax.ShapeDtypeStruct((1, D_MODEL), F32)],
        compiler_params=pltpu.CompilerParams(vmem_limit_bytes=VMEM_LIMIT_BYTES),
    )(dkv, mn, wkv, mem, g)


def _seqmix_bwd(dh1, x, z, c1, w_out, w_in, g_mix, cw, lng, lnb, gg, gb, wpair, wpair_t, bias, t, after=()):
    s = x.shape[0]
    nt = s // t
    halo_blocks = t // CONV_HALO

    def body(dh1_ref, x_ref, z_ref, zh_ref, c1_ref, wo_ref, wi_ref, gm_ref, cw_ref, lng_ref, lnb_ref,
             gg_ref, gb_ref, wpair_ref, wpt_ref, bias_ref,
             gx_ref, dz_ref, dcw_ref, dcb_ref, dlng_ref, dlnb_ref, dgg_ref, dgb_ref, dws_ref, dbs_ref,
             dbin_ref, dgm_ref, abuf, dbuf, mixed_ref, dv_ref):
        i = pl.program_id(0)
        tile = nt - 1 - i
        accs = (dcw_ref, dcb_ref, dlng_ref, dlnb_ref, dgg_ref, dgb_ref, dws_ref, dbs_ref, dbin_ref, dgm_ref)

        @pl.when(i == 0)
        def _():
            for r in accs:
                r[...] = jnp.zeros_like(r)
            dbuf[t:t + CONV_HALO, :] = jnp.zeros((CONV_HALO, CONV_WIDTH), F32)

        @pl.when(i > 0)
        def _():
            dbuf[t:t + CONV_HALO, :] = dbuf[0:CONV_HALO, :]

        dmix = _dot_nt(dh1_ref[...].astype(BF16), wo_ref[...])

        xh, rs = _ln_stats(c1_ref[...])
        lng = lng_ref[...]
        ln = xh * lng + lnb_ref[...]
        sl = _sigmoid(ln)
        dln = dmix[:, 0:512] * (sl * (1.0 + ln * (1.0 - sl)))
        dc1, dg_ln, db_ln = _ln_bwd(dln, xh, rs, lng)
        dlng_ref[...] += dg_ln
        dlnb_ref[...] += db_ln
        dcb_ref[...] += _rowsum(dc1)
        dbuf[0:t, :] = dc1

        zh = zh_ref[...]
        a_halo = zh[:, 0:512] * _sigmoid(zh[:, 512:1024])
        abuf[0:CONV_HALO, :] = jnp.where(tile > 0, a_halo, 0.0)
        za = z_ref[:, 0:512]
        sg = _sigmoid(z_ref[:, 512:1024])
        abuf[CONV_HALO:, :] = za * sg

        da = jnp.zeros((t, CONV_WIDTH), F32)
        for k in range(CONV_KERNEL):
            da = da + cw_ref[k:k + 1, :] * dbuf[pl.ds(CONV_KERNEL - 1 - k, t), :]
            dcw_ref[k:k + 1, :] += _rowsum(dc1 * abuf[pl.ds(CONV_HALO - (CONV_KERNEL - 1) + k, t), :])
        dza = da * sg
        dzg = da * za * (sg * (1.0 - sg))
        dz_ref[:, 0:512] = dza.astype(BF16)
        dz_ref[:, 512:1024] = dzg.astype(BF16)
        dbin_ref[:, 0:512] += _rowsum(dza)
        dbin_ref[:, 512:1024] += _rowsum(dzg)

        dgm = dmix[:, 512:1024]
        u, du_dz = _gelu_parts(z_ref[:, 1024:1536])
        gv, dgv_dz = _gelu_parts(z_ref[:, 1536:2048])
        vxh, vrs = _ln_stats(gv)
        ggv = gg_ref[...]
        v = vxh * ggv + gb_ref[...]
        low = _lane_is_low_head()
        v_lo = jnp.where(low, v, 0.0).astype(BF16)
        v_hi = jnp.where(low, 0.0, v).astype(BF16)
        _gm_mix(v_lo, v_hi, wpair_ref, bias_ref, mixed_ref, t)
        dzu = dgm * mixed_ref[...] * du_dz
        dm = dgm * u
        dm_lo = jnp.where(low, dm, 0.0).astype(BF16)
        dm_hi = jnp.where(low, 0.0, dm).astype(BF16)
        vb = v.astype(BF16)
        tril = (lax.broadcasted_iota(jnp.int32, (CHUNK, CHUNK), 1)
                <= lax.broadcasted_iota(jnp.int32, (CHUNK, CHUNK), 0))
        for n in range(t // CHUNK):
            rows = slice(n * CHUNK, (n + 1) * CHUNK)
            dbs_ref[...] += dm[rows, :]
            for j in range(GM_HEADS // 2):
                cols = slice(j * LANES, (j + 1) * LANES)
                stack = jnp.concatenate([dm_lo[rows, cols], dm_hi[rows, cols]], axis=0)
                dws = _dot_nt(stack, vb[rows, cols])
                dws_ref[2 * j] += jnp.where(tril, dws[0:CHUNK], 0.0)
                dws_ref[2 * j + 1] += jnp.where(tril, dws[CHUNK:2 * CHUNK], 0.0)
                dv_ref[rows, cols] = _dot(wpt_ref[j], stack)
        dgv, dg_gm, db_gm = _ln_bwd(dv_ref[...], vxh, vrs, ggv)
        dgg_ref[...] += dg_gm
        dgb_ref[...] += db_gm
        dzv = dgv * dgv_dz
        dz_ref[:, 1024:1536] = dzu.astype(BF16)
        dz_ref[:, 1536:2048] = dzv.astype(BF16)
        dbin_ref[:, 1024:1536] += _rowsum(dzu)
        dbin_ref[:, 1536:2048] += _rowsum(dzv)

        dhn = jnp.zeros((t, D_MODEL), F32)
        for j in range(4):
            dhn = dhn + _dot_nt(dz_ref[:, j * 512:(j + 1) * 512], wi_ref[j])
        xv = x_ref[...]
        dv, dg = _rms_bwd(dhn, xv, _rms_stats(xv), gm_ref[...])
        gx_ref[...] = dh1_ref[...] + dv
        dgm_ref[...] += dg

    rev = lambda w: pl.BlockSpec((t, w), lambda i: (nt - 1 - i, 0))
    const = lambda *shape: pl.BlockSpec(shape, lambda i: (0,) * len(shape))
    halo = pl.BlockSpec((CONV_HALO, D_MODEL), lambda i: (jnp.maximum((nt - 1 - i) * halo_blocks - 1, 0), 0))
    f32 = lambda *shape: jax.ShapeDtypeStruct(shape, F32)
    return _tied_call(
        body, after, name="seqmix_bwd", grid=(nt,),
        in_specs=[rev(D_MODEL), rev(D_MODEL), rev(2048), halo, rev(CONV_WIDTH),
                  const(D_MODEL, D_MODEL), const(4, D_MODEL, 512), const(1, D_MODEL),
                  const(CONV_HALO, CONV_WIDTH), const(1, 512), const(1, 512), const(1, 512), const(1, 512),
                  const(4, CHUNK, 2 * CHUNK), const(4, CHUNK, 2 * CHUNK), const(CHUNK, GM_WIDTH)],
        out_specs=[rev(D_MODEL), rev(2048),
                   const(CONV_HALO, CONV_WIDTH), const(1, 512), const(1, 512), const(1, 512), const(1, 512),
                   const(1, 512), const(GM_HEADS, CHUNK, CHUNK), const(CHUNK, GM_WIDTH), const(1, 2048),
                   const(1, D_MODEL)],
        out_shape=[f32(s, D_MODEL), jax.ShapeDtypeStruct((s, 2048), BF16),
                   f32(CONV_HALO, CONV_WIDTH), f32(1, 512), f32(1, 512), f32(1, 512), f32(1, 512),
                   f32(1, 512), f32(GM_HEADS, CHUNK, CHUNK), f32(CHUNK, GM_WIDTH), f32(1, 2048),
                   f32(1, D_MODEL)],
        scratch_shapes=[pltpu.VMEM((t + CONV_HALO, CONV_WIDTH), F32), pltpu.VMEM((t + CONV_HALO, CONV_WIDTH), F32),
                        pltpu.VMEM((t, GM_WIDTH), F32), pltpu.VMEM((t, GM_WIDTH), F32)],
        compiler_params=_params(("arbitrary",)),
    )(dh1, x, z, z, c1, w_out, w_in, g_mix, cw, lng, lnb, gg, gb, wpair, wpair_t, bias)


def _head_bias_grad(dbs):
    def body(d_ref, o_ref):
        dv = d_ref[...]
        lane = lax.broadcasted_iota(jnp.int32, (CHUNK, LANES), 1)
        acc = jnp.zeros((CHUNK, LANES), F32)
        for h in range(GM_HEADS):
            sh = jnp.sum(dv[:, h * GM_HEAD_DIM:(h + 1) * GM_HEAD_DIM], axis=-1, keepdims=True)
            acc = acc + jnp.where(lane == h, sh, 0.0)
        o_ref[...] = acc

    return pl.pallas_call(body, name="head_bias_grad",
                          out_shape=jax.ShapeDtypeStruct((CHUNK, LANES), F32))(dbs)


def _pack(arrays, rows):
    flat = jnp.concatenate([a.reshape(-1) for a in arrays])
    flat = jnp.pad(flat, (0, rows * LANES - flat.shape[0]))
    return flat.reshape(rows, LANES)


def _unpack(buf, shapes):
    flat = buf.reshape(-1)
    out, off = [], 0
    for shp in shapes:
        size = 1
        for d in shp:
            size *= d
        out.append(flat[off:off + size].reshape(shp))
        off += size
    return out


def _rows_for(shapes, multiple):
    total = 0
    for shp in shapes:
        size = 1
        for d in shp:
            size *= d
        total += size
    rows = -(-total // LANES)
    return -(-rows // multiple) * multiple


def kernel(x, mem, norm_mix_g, w_in, b_in, conv_w, conv_b, conv_ln_g, conv_ln_b, gm_ln_g, gm_ln_b, gm_w_s, gm_b_s, w_out, norm_xa_g, mem_norm_g, xa_wq, xa_wkv, xa_wo, norm_ffn_g, ffn_w_gate_up, ffn_w_down, final_norm_g, loss_target, m_norm_mix_g, m_w_in, m_b_in, m_conv_w, m_conv_b, m_conv_ln_g, m_conv_ln_b, m_gm_ln_g, m_gm_ln_b, m_gm_w_s, m_gm_b_s, m_w_out, m_norm_xa_g, m_mem_norm_g, m_xa_wq, m_xa_wkv, m_xa_wo, m_norm_ffn_g, m_ffn_w_gate_up, m_ffn_w_down, m_final_norm_g, v_norm_mix_g, v_w_in, v_b_in, v_conv_w, v_conv_b, v_conv_ln_g, v_conv_ln_b, v_gm_ln_g, v_gm_ln_b, v_gm_w_s, v_gm_b_s, v_w_out, v_norm_xa_g, v_mem_norm_g, v_xa_wq, v_xa_wkv, v_xa_wo, v_norm_ffn_g, v_ffn_w_gate_up, v_ffn_w_down, v_final_norm_g):
    s = x.shape[1]
    ts = _row_tile(s)
    tb = max(CHUNK, ts // 2)
    cx, cy, cc = _mesh_pos()
    chip = 2 * cx + cy
    pos = jnp.stack([chip, cc]).astype(jnp.int32)
    row = lambda a: a.reshape(1, -1)
    x2, mem2, tgt2 = x[0], mem[0], loss_target[0]

    big = dict(w_in=w_in, xa_wkv=xa_wkv, w_out=w_out, xa_wq=xa_wq, xa_wo=xa_wo,
               ffn_w_gate_up=ffn_w_gate_up, ffn_w_down=ffn_w_down)
    big_names = list(big)
    halves = lambda a: a.reshape(2, a.shape[0] // 2, a.shape[1])
    cast = {nm: _cast_into_slot(halves(big[nm]), pos, BF16, "cast_" + nm) for nm in big_names}
    conv_w_pad = jnp.pad(conv_w, ((0, CONV_HALO - CONV_KERNEL), (0, 0)))
    first = _gather_shards([cast["w_in"], _cast_into_slot(halves(conv_w_pad), pos, F32, "slot_conv_w")])
    shaped = lambda buf, nm: buf.reshape(N_CHIPS, big[nm].shape[0], big[nm].shape[1])

    def start_gather(names, after):
        return _gather_start([cast[nm] for nm in names], "gather_start_" + names[0], after)

    def finish_gather(names, started, after):
        send_sems, recv_sems, bufs, _ = started
        landed = _gather_wait(send_sems, recv_sems, bufs, after, "gather_wait_" + names[0])
        return {nm: shaped(b, nm) for nm, b in zip(names, _pass_to_sibling(landed, "pass_" + names[0]))}

    attn_names = ["w_out", "xa_wq", "xa_wkv", "xa_wo"]
    gather_attn = start_gather(attn_names, ())
    w_in_g = shaped(first[0], "w_in")
    cw_g = jnp.concatenate([first[1][k].reshape(CONV_HALO, LANES) for k in range(N_CHIPS)], axis=1)

    tril = jnp.tril(jnp.ones((CHUNK, CHUNK), dtype=bool))
    ws = jnp.where(tril[None], gm_w_s, 0.0)
    wpair = jnp.concatenate([ws[0::2], ws[1::2]], axis=2).astype(BF16)
    ws_t = jnp.swapaxes(ws, 1, 2)
    wpair_t = jnp.concatenate([ws_t[0::2], ws_t[1::2]], axis=2).astype(BF16)
    bias = jnp.repeat(gm_b_s.T, GM_HEAD_DIM, axis=1)

    z, hn1 = _mix_in(x2, row(norm_mix_g), w_in_g, row(b_in), ts, after=gather_attn[3])
    mix, c1 = _seqmix_fwd(z, cw_g, row(conv_b), row(conv_ln_g), row(conv_ln_b), row(gm_ln_g), row(gm_ln_b),
                          wpair, bias, ts)
    gw = finish_gather(attn_names, gather_attn, mix)
    w_out_g = gw["w_out"].reshape(D_MODEL, D_MODEL)
    wq_g = gw["xa_wq"].reshape(D_MODEL, D_MODEL)
    wkv_g = gw["xa_wkv"]
    wo_g = gw["xa_wo"].reshape(D_MODEL, D_MODEL)
    gather_gu = start_gather(["ffn_w_gate_up"], w_out_g)
    h1, hn2, q = _out_proj_q(x2, mix, w_out_g, row(norm_xa_g), wq_g, ts, after=gather_gu[3])
    mn, kv = _mem_kv(mem2, row(mem_norm_g), wkv_g)
    o, h2, hn3 = _attn_fwd(q, kv, h1, wo_g, row(norm_ffn_g), ts)
    wgu_g = finish_gather(["ffn_w_gate_up"], gather_gu, hn3)["ffn_w_gate_up"]
    gather_down = start_gather(["ffn_w_down"], wgu_g)
    gu, act = _ffn_up(hn3, wgu_g.reshape(2, 2, D_MODEL, FFN_HALF), ts, after=gather_down[3])
    wd_g = finish_gather(["ffn_w_down"], gather_down, act)["ffn_w_down"].reshape(FFN_HIDDEN, D_MODEL)
    dh3, sq, d_final_g = _ffn_down_loss(act, wd_g, h2, row(final_norm_g), tgt2, ts)
    loss = lax.psum(0.5 * jnp.sum(sq) / D_MODEL, ("x", "y", "c"))

    def split(g, nm):
        r, c = big[nm].shape
        return g.reshape(N_CHIPS, 2, r // 2, c)

    def chip_sums_of(group, arrays):
        got = _swap_halves(arrays, "swap_halves_" + group[0])
        both = [_add_half(g, r, pos, "chip_sum_" + nm) for g, r, nm in zip(arrays, got, group)]
        return [b[0] for b in both], [b[1] for b in both]

    def start_swap(group, grads):
        return _swap_start([split(g, nm) for g, nm in zip(grads, group)], "swap_start_" + group[0])

    def start_exchange(group, swapping, after):
        sems, arrays, lands, _ = swapping
        arrays, got = _swap_wait(sems, arrays, lands, after, "swap_wait_" + group[0])
        both = [_add_half(g, r, pos, "chip_sum_" + nm) for g, r, nm in zip(arrays, got, group)]
        return _exchange_start([b[0] for b in both], [b[1] for b in both], "exchange_start_" + group[0])

    def finish_exchange(group, started, after):
        sems, sums, parts, _ = started
        parts = _exchange_wait(sems, sums, parts, after, "exchange_wait_" + group[0])
        return [_sum_chips(p, pos, "total_" + nm) for p, nm in zip(parts, group)]

    as3 = lambda a: a.reshape((1,) + a.shape)
    halves_of = {}

    dgu = _ffn_bwd_act(dh3, wd_g.reshape(2, FFN_HALF, D_MODEL), gu, ts)
    g_down = _grad_w(act, as3(dh3), FFN_HALF, D_MODEL, "grad_ffn_w_down")
    group_a = ["ffn_w_down"]
    swap_a = start_swap(group_a, [g_down])
    dh2, d_ffn_g = _ffn_bwd_in(dgu, wgu_g, dh3, h2, row(norm_ffn_g), tb, after=swap_a[3])
    exch_a = start_exchange(group_a, swap_a, dh2)
    g_gu = _grad_w(hn3, dgu, 512, FFN_HALF, "grad_ffn_w_gate_up", after=exch_a[3])
    halves_of.update(zip(group_a, finish_exchange(group_a, exch_a, g_gu)))

    group_b = ["ffn_w_gate_up"]
    swap_b = start_swap(group_b, [g_gu])
    dh1, dq, dkv, d_xa_g = _attn_bwd(dh2, wo_g, q, kv, wq_g, h1, row(norm_xa_g), ts, after=swap_b[3])
    exch_b = start_exchange(group_b, swap_b, dh1)
    g_wkv, d_mem_g = _mem_kv_bwd(dkv, mn, wkv_g, mem2, row(mem_norm_g), after=exch_b[3])
    g_wo = _grad_w(o, as3(dh2), 512, D_MODEL, "grad_xa_wo", after=exch_b[3])
    g_wq = _grad_w(hn2, as3(dq), 512, D_MODEL, "grad_xa_wq", after=exch_b[3])
    g_wout = _grad_w(mix, as3(dh1), 512, D_MODEL, "grad_w_out", after=exch_b[3])
    halves_of.update(zip(group_b, finish_exchange(group_b, exch_b, (g_wkv, g_wo, g_wq, g_wout))))

    group_c = ["xa_wo", "xa_wq", "xa_wkv", "w_out"]
    swap_c = start_swap(group_c, [g_wo, g_wq, g_wkv, g_wout])
    (gx, dz, d_cw, d_cb, d_lng, d_lnb, d_gg, d_gb, d_ws, d_bs_sum, d_bin, d_mix_g) = _seqmix_bwd(
        dh1, x2, z, c1, w_out_g, w_in_g, row(norm_mix_g), cw_g, row(conv_ln_g), row(conv_ln_b),
        row(gm_ln_g), row(gm_ln_b), wpair, wpair_t, bias, tb, after=swap_c[3])
    d_bs = _head_bias_grad(d_bs_sum)[:, :GM_HEADS].T
    exch_c = start_exchange(group_c, swap_c, dz)
    g_win = _grad_w(hn1, as3(dz), 512, 512, "grad_w_in", after=exch_c[3])
    halves_of.update(zip(group_c, finish_exchange(group_c, exch_c, g_win)))

    small_names = ["norm_mix_g", "b_in", "conv_w", "conv_b", "conv_ln_g", "conv_ln_b", "gm_ln_g", "gm_ln_b",
                   "gm_w_s", "gm_b_s", "norm_xa_g", "mem_norm_g", "norm_ffn_g", "final_norm_g"]
    small_grads = dict(norm_mix_g=d_mix_g, b_in=d_bin, conv_w=d_cw[:CONV_KERNEL], conv_b=d_cb, conv_ln_g=d_lng,
                       conv_ln_b=d_lnb, gm_ln_g=d_gg, gm_ln_b=d_gb, gm_w_s=d_ws, gm_b_s=d_bs, norm_xa_g=d_xa_g,
                       mem_norm_g=d_mem_g, norm_ffn_g=d_ffn_g, final_norm_g=d_final_g)
    full_shapes = dict(norm_mix_g=(D_MODEL,), b_in=(2048,), conv_w=(CONV_KERNEL, CONV_WIDTH), conv_b=(512,),
                       conv_ln_g=(512,), conv_ln_b=(512,), gm_ln_g=(512,), gm_ln_b=(512,),
                       gm_w_s=(GM_HEADS, CHUNK, CHUNK), gm_b_s=(GM_HEADS, CHUNK), norm_xa_g=(D_MODEL,),
                       mem_norm_g=(D_MODEL,), norm_ffn_g=(D_MODEL,), final_norm_g=(D_MODEL,))
    pack_rows = _rows_for([full_shapes[nm] for nm in small_names], 32)
    small_pack = _pack([small_grads[nm] for nm in small_names], pack_rows)

    group_d = ["w_in", "small"]
    sums_d, parts_d = chip_sums_of(group_d, [split(g_win, "w_in"),
                                             small_pack.reshape(1, 2, pack_rows // 2, LANES)])
    parts_d = _exchange_chip_sums(sums_d, parts_d)
    halves_of.update(zip(group_d, [_sum_chips(p, pos, "total_" + nm) for p, nm in zip(parts_d, group_d)]))

    red_names = big_names + ["small"]
    joined = _join_halves([halves_of[nm] for nm in red_names])
    grads = {nm: joined[i].reshape(big[nm].shape) for i, nm in enumerate(big_names)}
    small_red = _unpack(joined[-1].reshape(pack_rows, LANES), [full_shapes[nm] for nm in small_names])
    for nm, g in zip(small_names, small_red):
        grads[nm] = g
    grads["conv_w"] = lax.dynamic_slice(grads["conv_w"], (0, chip * LANES), (CONV_KERNEL, LANES))

    weights = dict(norm_mix_g=norm_mix_g, w_in=w_in, b_in=b_in, conv_w=conv_w, conv_b=conv_b, conv_ln_g=conv_ln_g,
                   conv_ln_b=conv_ln_b, gm_ln_g=gm_ln_g, gm_ln_b=gm_ln_b, gm_w_s=gm_w_s, gm_b_s=gm_b_s, w_out=w_out,
                   norm_xa_g=norm_xa_g, mem_norm_g=mem_norm_g, xa_wq=xa_wq, xa_wkv=xa_wkv, xa_wo=xa_wo,
                   norm_ffn_g=norm_ffn_g, ffn_w_gate_up=ffn_w_gate_up, ffn_w_down=ffn_w_down,
                   final_norm_g=final_norm_g)
    m_in = dict(norm_mix_g=m_norm_mix_g, w_in=m_w_in, b_in=m_b_in, conv_w=m_conv_w, conv_b=m_conv_b,
                conv_ln_g=m_conv_ln_g, conv_ln_b=m_conv_ln_b, gm_ln_g=m_gm_ln_g, gm_ln_b=m_gm_ln_b, gm_w_s=m_gm_w_s,
                gm_b_s=m_gm_b_s, w_out=m_w_out, norm_xa_g=m_norm_xa_g, mem_norm_g=m_mem_norm_g, xa_wq=m_xa_wq,
                xa_wkv=m_xa_wkv, xa_wo=m_xa_wo, norm_ffn_g=m_norm_ffn_g, ffn_w_gate_up=m_ffn_w_gate_up,
                ffn_w_down=m_ffn_w_down, final_norm_g=m_final_norm_g)
    v_in = dict(norm_mix_g=v_norm_mix_g, w_in=v_w_in, b_in=v_b_in, conv_w=v_conv_w, conv_b=v_conv_b,
                conv_ln_g=v_conv_ln_g, conv_ln_b=v_conv_ln_b, gm_ln_g=v_gm_ln_g, gm_ln_b=v_gm_ln_b, gm_w_s=v_gm_w_s,
                gm_b_s=v_gm_b_s, w_out=v_w_out, norm_xa_g=v_norm_xa_g, mem_norm_g=v_mem_norm_g, xa_wq=v_xa_wq,
                xa_wkv=v_xa_wkv, xa_wo=v_xa_wo, norm_ffn_g=v_norm_ffn_g, ffn_w_gate_up=v_ffn_w_gate_up,
                ffn_w_down=v_ffn_w_down, final_norm_g=v_final_norm_g)
    delta, new_m, new_v = {}, {}, {}
    for nm in big_names:
        delta[nm], new_m[nm], new_v[nm] = _adamw(weights[nm], grads[nm], m_in[nm], v_in[nm], "adamw_" + nm)
    local_shapes = [weights[nm].shape for nm in small_names]
    adam_rows = _rows_for(local_shapes, LANES)
    packed = [_pack([src[nm] for nm in small_names], adam_rows) for src in (weights, grads, m_in, v_in)]
    outs = _adamw(*packed, "adamw_small")
    for dst, buf in zip((delta, new_m, new_v), outs):
        for nm, a in zip(small_names, _unpack(buf, local_shapes)):
            dst[nm] = a

    order = ["norm_mix_g", "w_in", "b_in", "conv_w", "conv_b", "conv_ln_g", "conv_ln_b", "gm_ln_g", "gm_ln_b",
             "gm_w_s", "gm_b_s", "w_out", "norm_xa_g", "mem_norm_g", "xa_wq", "xa_wkv", "xa_wo", "norm_ffn_g",
             "ffn_w_gate_up", "ffn_w_down", "final_norm_g"]
    fit = lambda a, nm: a.reshape(weights[nm].shape)
    return (loss, gx.reshape(x.shape),
            *[fit(grads[nm], nm) for nm in order], *[fit(delta[nm], nm) for nm in order],
            *[fit(new_m[nm], nm) for nm in order], *[fit(new_v[nm], nm) for nm in order])
```

```python
import functools

import jax
import jax.numpy as jnp
from jax import lax
from jax.experimental import pallas as pl
from jax.experimental.pallas import tpu as pltpu

F32 = jnp.float32
BF16 = jnp.bfloat16

D_MODEL = 1024
CONV_WIDTH = 512
GM_WIDTH = 512
CONV_KERNEL = 31
CONV_HALO = 32
GRAD_ROWS = 2048
CHUNK = 128
GM_HEADS = 8
GM_HEAD_DIM = 64
XA_HEADS = 4
XA_HEAD_DIM = 256
FFN_HIDDEN = 2816
FFN_HALF = FFN_HIDDEN // 2
RMS_EPS = 1e-6
LN_EPS = 1e-5
N_CHIPS = 4
LANES = 128

ADAM_LR = 0.001
ADAM_B1 = 0.9
ADAM_B2 = 0.999
ADAM_EPS = 1e-08
ADAM_WD = 0.01
ADAM_STEP = 10

VMEM_LIMIT_BYTES = 56 * 1024 * 1024
MESH = pl.DeviceIdType.MESH
ANY = pl.BlockSpec(memory_space=pl.ANY)
HBM_SPEC = pl.BlockSpec(memory_space=pltpu.HBM)
SEM_SPEC = pl.BlockSpec(memory_space=pltpu.SEMAPHORE)

_NT = (((1,), (1,)), ((), ()))
_TN = (((0,), (0,)), ((), ()))
_GELU_C = 0.7978845608028654
_GELU_A = 0.044715


def _dot(a, b):
    return jnp.dot(a, b, preferred_element_type=F32)


def _dot_nt(a, b):
    return lax.dot_general(a, b, _NT, preferred_element_type=F32)


def _dot_tn(a, b):
    return lax.dot_general(a, b, _TN, preferred_element_type=F32)


def _mean(v):
    return jnp.mean(v, axis=-1, keepdims=True)


def _rowsum(v):
    return jnp.sum(v, axis=0, keepdims=True)


def _sigmoid(v):
    return 1.0 / (1.0 + jnp.exp(-v))


def _gelu_parts(v):
    v2 = v * v
    t = jnp.tanh(_GELU_C * (v + _GELU_A * v * v2))
    g = 0.5 * v * (1.0 + t)
    dg = 0.5 * (1.0 + t) + 0.5 * v * (1.0 - t * t) * (_GELU_C * (1.0 + 3.0 * _GELU_A * v2))
    return g, dg


def _rms_stats(v):
    return lax.rsqrt(_mean(v * v) + RMS_EPS)


def _rms_bwd(dy, v, r, g):
    n = v * r
    dn = dy * g
    dv = r * (dn - n * _mean(dn * n))
    return dv, _rowsum(dy * n)


def _ln_stats(v):
    mu = _mean(v)
    xc = v - mu
    rs = lax.rsqrt(_mean(xc * xc) + LN_EPS)
    return xc * rs, rs


def _ln_bwd(dy, xh, rs, g):
    dxh = dy * g
    dv = rs * (dxh - _mean(dxh) - xh * _mean(dxh * xh))
    return dv, _rowsum(dy * xh), _rowsum(dy)


def _params(sem):
    return pltpu.CompilerParams(dimension_semantics=sem, vmem_limit_bytes=VMEM_LIMIT_BYTES)


def _row_tile(s):
    return 512 if s % 512 == 0 and s >= 2048 else 128


def _mesh_pos():
    return lax.axis_index("x"), lax.axis_index("y"), lax.axis_index("c")


def _cast_into_slot(w, pos, dtype, name):
    _, h, c = w.shape

    def body(pos_ref, w_ref, o_ref):
        o_ref[0] = w_ref[...].astype(dtype)

    return pl.pallas_call(
        body, name=name,
        grid_spec=pltpu.PrefetchScalarGridSpec(
            num_scalar_prefetch=1, grid=(2,),
            in_specs=[pl.BlockSpec((1, h, c), lambda i, p: (i, 0, 0))],
            out_specs=pl.BlockSpec((1, 1, h, c), lambda i, p: (p[0], i, 0, 0))),
        out_shape=jax.ShapeDtypeStruct((N_CHIPS, 2, h, c), dtype),
        compiler_params=_params(("parallel",)),
    )(pos, w)


def _adam_rows(r, c):
    for tr in (r, 1024, 704, 640, 512, 352, 320, 256, 128, 64, 32, 16, 8):
        if r % tr == 0 and tr * c * 4 <= (3 << 19):
            return tr
    return r


def _adamw(w, g, m, v, name):
    r, c = w.shape
    tr = _adam_rows(r, c)

    def body(w_ref, g_ref, m_ref, v_ref, d_ref, nm_ref, nv_ref):
        gv = g_ref[...]
        nm = ADAM_B1 * m_ref[...] + (1.0 - ADAM_B1) * gv
        nv = ADAM_B2 * v_ref[...] + (1.0 - ADAM_B2) * (gv * gv)
        m_hat = nm / (1.0 - ADAM_B1 ** ADAM_STEP)
        v_hat = nv / (1.0 - ADAM_B2 ** ADAM_STEP)
        d_ref[...] = -ADAM_LR * (m_hat / (jnp.sqrt(v_hat) + ADAM_EPS) + ADAM_WD * w_ref[...])
        nm_ref[...] = nm
        nv_ref[...] = nv

    spec = pl.BlockSpec((tr, c), lambda i: (i, 0))
    shp = jax.ShapeDtypeStruct((r, c), F32)
    return pl.pallas_call(
        body, name=name, grid=(r // tr,),
        in_specs=[spec] * 4, out_specs=[spec] * 3, out_shape=[shp] * 3,
        compiler_params=_params(("parallel",)),
    )(w, g, m, v)


def _as_tuple(after):
    return tuple(after) if isinstance(after, (tuple, list)) else (after,)


def _tied_call(body, after, *, in_specs, **kwargs):
    after = _as_tuple(after)
    n_in, n_after = len(in_specs), len(after)

    def tied(*refs):
        body(*refs[:n_in], *refs[n_in + n_after:])

    call = pl.pallas_call(tied, in_specs=list(in_specs) + [ANY] * n_after, **kwargs)
    return lambda *operands: call(*operands, *after)


def _other_chips(x, y):
    return [(1 - x, y), (x, 1 - y), (1 - x, 1 - y)]


def _gather_shards(bufs):
    n = len(bufs)

    def body(*refs):
        outs = refs[n:2 * n]
        ici_send, ici_recv, fwd_send, fwd_recv = refs[2 * n:]
        x, y, c = _mesh_pos()
        me = 2 * x + y
        sib = (x, y, 1 - c)
        chips = _other_chips(x, y)

        def ici(a, k, chip_idx, to):
            return pltpu.make_async_remote_copy(
                src_ref=outs[a].at[chip_idx, c], dst_ref=outs[a].at[chip_idx, c],
                send_sem=ici_send.at[a, k], recv_sem=ici_recv.at[a, k],
                device_id=to, device_id_type=MESH)

        def fwd(a, k, chip_idx, half):
            return pltpu.make_async_remote_copy(
                src_ref=outs[a].at[chip_idx, half], dst_ref=outs[a].at[chip_idx, half],
                send_sem=fwd_send.at[a, k], recv_sem=fwd_recv.at[a, k],
                device_id=sib, device_id_type=MESH)

        sends = [ici(a, k, me, (*chips[k], c)) for a in range(n) for k in range(3)]
        for cp in sends:
            cp.start()
        passed = []
        for a in range(n):
            for k in range(3):
                ck = 2 * chips[k][0] + chips[k][1]
                ici(a, k, ck, (*chips[k], c)).wait_recv()
                cp = fwd(a, k, ck, c)
                cp.start()
                passed.append(cp)
        for a in range(n):
            for k in range(3):
                ck = 2 * chips[k][0] + chips[k][1]
                fwd(a, k, ck, 1 - c).wait_recv()
        for cp in sends + passed:
            cp.wait_send()

    out_shape = [jax.ShapeDtypeStruct(b.shape, b.dtype) for b in bufs]
    return pl.pallas_call(
        body, name="gather_weights",
        in_specs=[ANY] * n, out_specs=[ANY] * n, out_shape=out_shape,
        input_output_aliases={a: a for a in range(n)},
        scratch_shapes=[pltpu.SemaphoreType.DMA((n, 3))] * 4,
    )(*bufs)


def _gather_descriptors(bufs, send_of, recv_of):
    x, y, c = _mesh_pos()
    me = 2 * x + y
    chips = _other_chips(x, y)
    sends, arrivals = [], []
    for a in range(len(bufs)):
        for k in range(3):
            ck = 2 * chips[k][0] + chips[k][1]

            def copy(slot, a=a, k=k):
                return pltpu.make_async_remote_copy(
                    src_ref=bufs[a].at[slot, c], dst_ref=bufs[a].at[slot, c],
                    send_sem=send_of(a, k), recv_sem=recv_of(a, k),
                    device_id=(*chips[k], c), device_id_type=MESH)

            sends.append(functools.partial(copy, me))
            arrivals.append(functools.partial(copy, ck))
    return sends, arrivals


def _gather_start(bufs, name, after=()):
    n = len(bufs)
    ns = 3 * n

    def body(*refs):
        sems = refs[n:n + 2 * ns]
        thru = refs[n + 2 * ns:2 * n + 2 * ns]
        token = refs[2 * n + 2 * ns]
        sends, _ = _gather_descriptors(thru, lambda a, k: sems[3 * a + k], lambda a, k: sems[ns + 3 * a + k])
        for cp in sends:
            cp().start()
        token[...] = jnp.zeros_like(token)

    held = [pltpu.with_memory_space_constraint(b, pltpu.HBM) for b in bufs]
    out = _tied_call(
        body, after, name=name,
        out_shape=(*[pltpu.SemaphoreType.DMA(())] * (2 * ns), *[pltpu.HBM(b.shape, b.dtype) for b in held],
                   jax.ShapeDtypeStruct((8, LANES), F32)),
        in_specs=[HBM_SPEC] * n,
        out_specs=(*[SEM_SPEC] * (2 * ns), *[HBM_SPEC] * n, pl.BlockSpec(memory_space=pltpu.VMEM)),
        input_output_aliases={i: 2 * ns + i for i in range(n)},
        compiler_params=pltpu.CompilerParams(has_side_effects=pltpu.SideEffectType.DATAFLOW_SIDE_EFFECTING),
    )(*held)
    return list(out[:ns]), list(out[ns:2 * ns]), list(out[2 * ns:2 * ns + n]), out[2 * ns + n]


def _gather_wait(send_sems, recv_sems, bufs, after, name):
    n = len(bufs)
    ns = 3 * n

    def body(*refs):
        buf_ref = refs[:n]
        sem_ref = refs[n:n + 2 * ns]
        sends, arrivals = _gather_descriptors(buf_ref, lambda a, k: sem_ref[3 * a + k],
                                              lambda a, k: sem_ref[ns + 3 * a + k])
        for cp in sends:
            cp().wait_send()
        for cp in arrivals:
            cp().wait_recv()

    out = pl.pallas_call(
        body, name=name,
        out_shape=tuple(pltpu.HBM(b.shape, b.dtype) for b in bufs),
        in_specs=[HBM_SPEC] * n + [SEM_SPEC] * (2 * ns) + [ANY],
        out_specs=tuple([HBM_SPEC] * n),
        input_output_aliases={i: i for i in range(n)},
        compiler_params=pltpu.CompilerParams(has_side_effects=pltpu.SideEffectType.DATAFLOW_SIDE_EFFECTING),
    )(*bufs, *send_sems, *recv_sems, after)
    return list(out)


def _pass_to_sibling(bufs, name):
    n = len(bufs)

    def body(*refs):
        outs = refs[n:2 * n]
        send_sem, recv_sem = refs[2 * n:]
        x, y, c = _mesh_pos()
        chips = _other_chips(x, y)

        def half(a, k, which):
            ck = 2 * chips[k][0] + chips[k][1]
            return pltpu.make_async_remote_copy(
                src_ref=outs[a].at[ck, which], dst_ref=outs[a].at[ck, which],
                send_sem=send_sem.at[a, k], recv_sem=recv_sem.at[a, k],
                device_id=(x, y, 1 - c), device_id_type=MESH)

        sends = [half(a, k, c) for a in range(n) for k in range(3)]
        for cp in sends:
            cp.start()
        for a in range(n):
            for k in range(3):
                half(a, k, 1 - c).wait_recv()
        for cp in sends:
            cp.wait_send()

    return pl.pallas_call(
        body, name=name,
        in_specs=[ANY] * n, out_specs=[ANY] * n,
        out_shape=[jax.ShapeDtypeStruct(b.shape, b.dtype) for b in bufs],
        input_output_aliases={a: a for a in range(n)},
        scratch_shapes=[pltpu.SemaphoreType.DMA((n, 3))] * 2,
    )(*bufs)


def _swap_halves(grads, name):
    n = len(grads)

    def body(*refs):
        ins, outs = refs[:n], refs[n:2 * n]
        send_sem, recv_sem = refs[2 * n:]
        x, y, c = _mesh_pos()
        cps = [pltpu.make_async_remote_copy(
            src_ref=ins[a].at[:, pl.ds(1 - c, 1)], dst_ref=outs[a],
            send_sem=send_sem.at[a], recv_sem=recv_sem.at[a],
            device_id=(x, y, 1 - c), device_id_type=MESH) for a in range(n)]
        for cp in cps:
            cp.start()
        for cp in cps:
            cp.wait()

    out_shape = [jax.ShapeDtypeStruct((g.shape[0], 1) + g.shape[2:], g.dtype) for g in grads]
    return pl.pallas_call(
        body, name=name,
        in_specs=[ANY] * n, out_specs=[ANY] * n, out_shape=out_shape,
        scratch_shapes=[pltpu.SemaphoreType.DMA((n,))] * 2,
    )(*grads)


def _swap_descriptors(grads, lands, send_of, recv_of):
    x, y, c = _mesh_pos()
    return [functools.partial(
        pltpu.make_async_remote_copy,
        src_ref=grads[a].at[:, pl.ds(1 - c, 1)], dst_ref=lands[a],
        send_sem=send_of(a), recv_sem=recv_of(a),
        device_id=(x, y, 1 - c), device_id_type=MESH) for a in range(len(grads))]


def _swap_start(grads, name):
    n = len(grads)

    def body(*refs):
        sems = refs[2 * n:4 * n]
        g_thru, l_thru = refs[4 * n:5 * n], refs[5 * n:6 * n]
        token = refs[6 * n]
        for cp in _swap_descriptors(g_thru, l_thru, lambda a: sems[a], lambda a: sems[n + a]):
            cp().start()
        token[...] = jnp.zeros_like(token)

    lands = [lax.empty((g.shape[0], 1) + g.shape[2:], g.dtype) for g in grads]
    held = [pltpu.with_memory_space_constraint(a, pltpu.HBM) for a in (*grads, *lands)]
    out = pl.pallas_call(
        body, name=name,
        out_shape=(*[pltpu.SemaphoreType.DMA(())] * (2 * n), *[pltpu.HBM(a.shape, a.dtype) for a in held],
                   jax.ShapeDtypeStruct((8, LANES), F32)),
        in_specs=[HBM_SPEC] * (2 * n),
        out_specs=(*[SEM_SPEC] * (2 * n), *[HBM_SPEC] * (2 * n), pl.BlockSpec(memory_space=pltpu.VMEM)),
        input_output_aliases={i: 2 * n + i for i in range(2 * n)},
        compiler_params=pltpu.CompilerParams(has_side_effects=pltpu.SideEffectType.DATAFLOW_SIDE_EFFECTING),
    )(*held)
    return list(out[:2 * n]), list(out[2 * n:3 * n]), list(out[3 * n:4 * n]), out[4 * n]


def _swap_wait(sems, grads, lands, after, name):
    n = len(grads)

    def body(*refs):
        g_ref, l_ref = refs[:n], refs[n:2 * n]
        sem_ref = refs[2 * n:4 * n]
        for cp in _swap_descriptors(g_ref, l_ref, lambda a: sem_ref[a], lambda a: sem_ref[n + a]):
            cp().wait()

    out = pl.pallas_call(
        body, name=name,
        out_shape=tuple(pltpu.HBM(a.shape, a.dtype) for a in (*grads, *lands)),
        in_specs=[HBM_SPEC] * (2 * n) + [SEM_SPEC] * (2 * n) + [ANY],
        out_specs=tuple([HBM_SPEC] * (2 * n)),
        input_output_aliases={i: i for i in range(2 * n)},
        compiler_params=pltpu.CompilerParams(has_side_effects=pltpu.SideEffectType.DATAFLOW_SIDE_EFFECTING),
    )(*grads, *lands, *sems, after)
    return list(out[:n]), list(out[n:])


def _add_half(g, got, pos, name):
    j, _, h, c = g.shape

    def body(pos_ref, g_ref, r_ref, o_ref, p_ref):
        val = (g_ref[0, 0] + r_ref[0, 0]).astype(BF16)
        o_ref[0] = val
        if j == 1:
            p_ref[0] = val
        else:
            @pl.when(pl.program_id(0) == pos_ref[0])
            def _():
                p_ref[0] = val

    return pl.pallas_call(
        body, name=name,
        grid_spec=pltpu.PrefetchScalarGridSpec(
            num_scalar_prefetch=1, grid=(j,),
            in_specs=[pl.BlockSpec((1, 1, h, c), lambda i, p: (i, p[1], 0, 0)),
                      pl.BlockSpec((1, 1, h, c), lambda i, p: (i, 0, 0, 0))],
            out_specs=[pl.BlockSpec((1, h, c), lambda i, p: (i, 0, 0)),
                       pl.BlockSpec((1, h, c), lambda i, p: (p[0], 0, 0))]),
        out_shape=[jax.ShapeDtypeStruct((j, h, c), BF16), jax.ShapeDtypeStruct((N_CHIPS, h, c), BF16)],
        compiler_params=_params(("arbitrary",)),
    )(pos, g, got)


def _exchange_chip_sums(sums, parts):
    n = len(sums)

    def body(*refs):
        ins, outs = refs[:n], refs[2 * n:3 * n]
        send_sem, recv_sem = refs[3 * n:]
        sends, arrivals = _exchange_descriptors(ins, outs, lambda a, k: send_sem.at[a, k],
                                                lambda a, k: recv_sem.at[a, k])
        for cp in sends:
            cp().start()
        for cp in arrivals:
            cp().wait_recv()
        for cp in sends:
            cp().wait_send()

    out_shape = [jax.ShapeDtypeStruct(p.shape, p.dtype) for p in parts]
    return pl.pallas_call(
        body, name="exchange_chip_sums",
        in_specs=[ANY] * (2 * n), out_specs=[ANY] * n, out_shape=out_shape,
        input_output_aliases={n + a: a for a in range(n)},
        scratch_shapes=[pltpu.SemaphoreType.DMA((n, 3))] * 2,
    )(*sums, *parts)


def _exchange_descriptors(sums, parts, send_of, recv_of):
    x, y, c = _mesh_pos()
    me = 2 * x + y
    chips = _other_chips(x, y)
    sends, arrivals = [], []
    for a in range(len(sums)):
        for k in range(3):
            ck = 2 * chips[k][0] + chips[k][1]
            mine = sums[a].at[ck] if sums[a].shape[0] == N_CHIPS else sums[a].at[0]

            def copy(dst_slot, a=a, k=k, mine=mine):
                return pltpu.make_async_remote_copy(
                    src_ref=mine, dst_ref=parts[a].at[dst_slot],
                    send_sem=send_of(a, k), recv_sem=recv_of(a, k),
                    device_id=(*chips[k], c), device_id_type=MESH)

            sends.append(functools.partial(copy, me))
            arrivals.append(functools.partial(copy, ck))
    return sends, arrivals


def _exchange_start(sums, parts, name):
    n = len(sums)
    ns = 3 * n

    def body(*refs):
        sems = refs[2 * n:2 * n + 2 * ns]
        sums_thru = refs[2 * n + 2 * ns:3 * n + 2 * ns]
        parts_thru = refs[3 * n + 2 * ns:4 * n + 2 * ns]
        token = refs[4 * n + 2 * ns]
        sends, _ = _exchange_descriptors(sums_thru, parts_thru, lambda a, k: sems[3 * a + k],
                                         lambda a, k: sems[ns + 3 * a + k])
        for cp in sends:
            cp().start()
        token[...] = jnp.zeros_like(token)

    hbm = lambda a: pltpu.HBM(a.shape, a.dtype)
    held = [pltpu.with_memory_space_constraint(a, pltpu.HBM) for a in (*sums, *parts)]
    out = pl.pallas_call(
        body, name=name,
        out_shape=(*[pltpu.SemaphoreType.DMA(())] * (2 * ns), *[hbm(a) for a in held],
                   jax.ShapeDtypeStruct((8, LANES), F32)),
        in_specs=[HBM_SPEC] * (2 * n),
        out_specs=(*[SEM_SPEC] * (2 * ns), *[HBM_SPEC] * (2 * n), pl.BlockSpec(memory_space=pltpu.VMEM)),
        input_output_aliases={i: 2 * ns + i for i in range(2 * n)},
        compiler_params=pltpu.CompilerParams(has_side_effects=pltpu.SideEffectType.DATAFLOW_SIDE_EFFECTING),
    )(*held)
    return (list(out[:2 * ns]), list(out[2 * ns:2 * ns + n]), list(out[2 * ns + n:2 * ns + 2 * n]),
            out[2 * ns + 2 * n])


def _exchange_wait(sems, sums, parts, after, name):
    n = len(sums)
    ns = 3 * n

    def body(*refs):
        sums_ref, parts_ref = refs[:n], refs[n:2 * n]
        sem_ref = refs[2 * n:2 * n + 2 * ns]
        sends, arrivals = _exchange_descriptors(sums_ref, parts_ref, lambda a, k: sem_ref[3 * a + k],
                                                lambda a, k: sem_ref[ns + 3 * a + k])
        for cp in sends:
            cp().wait_send()
        for cp in arrivals:
            cp().wait_recv()

    hbm = lambda a: pltpu.HBM(a.shape, a.dtype)
    out = pl.pallas_call(
        body, name=name,
        out_shape=tuple(hbm(a) for a in (*sums, *parts)),
        in_specs=[HBM_SPEC] * (2 * n) + [SEM_SPEC] * (2 * ns) + [ANY] * len(_as_tuple(after)),
        out_specs=tuple([HBM_SPEC] * (2 * n)),
        input_output_aliases={i: i for i in range(2 * n)},
        compiler_params=pltpu.CompilerParams(has_side_effects=pltpu.SideEffectType.DATAFLOW_SIDE_EFFECTING),
    )(*sums, *parts, *sems, *_as_tuple(after))
    return list(out[n:])


def _sum_chips(parts, pos, name):
    _, h, c = parts.shape

    def body(pos_ref, p_ref, o_ref):
        o_ref[0] = ((p_ref[0].astype(F32) + p_ref[1].astype(F32)) + p_ref[2].astype(F32)) + p_ref[3].astype(F32)

    return pl.pallas_call(
        body, name=name,
        grid_spec=pltpu.PrefetchScalarGridSpec(
            num_scalar_prefetch=1, grid=(1,),
            in_specs=[pl.BlockSpec((N_CHIPS, h, c), lambda i, p: (0, 0, 0))],
            out_specs=pl.BlockSpec((1, h, c), lambda i, p: (p[1], 0, 0))),
        out_shape=jax.ShapeDtypeStruct((2, h, c), F32),
        compiler_params=_params(("arbitrary",)),
    )(pos, parts)


def _join_halves(fulls):
    n = len(fulls)

    def body(*refs):
        outs = refs[n:2 * n]
        send_sem, recv_sem = refs[2 * n:]
        x, y, c = _mesh_pos()

        def half(a, which):
            return pltpu.make_async_remote_copy(
                src_ref=outs[a].at[which], dst_ref=outs[a].at[which],
                send_sem=send_sem.at[a], recv_sem=recv_sem.at[a],
                device_id=(x, y, 1 - c), device_id_type=MESH)

        sends = [half(a, c) for a in range(n)]
        for cp in sends:
            cp.start()
        for a in range(n):
            half(a, 1 - c).wait_recv()
        for cp in sends:
            cp.wait_send()

    out_shape = [jax.ShapeDtypeStruct(f.shape, f.dtype) for f in fulls]
    return pl.pallas_call(
        body, name="join_halves",
        in_specs=[ANY] * n, out_specs=[ANY] * n, out_shape=out_shape,
        input_output_aliases={a: a for a in range(n)},
        scratch_shapes=[pltpu.SemaphoreType.DMA((n,))] * 2,
    )(*fulls)


def _mix_in(x, g, w_in, b_in, ts, after=()):
    s = x.shape[0]

    def body(x_ref, g_ref, w_ref, b_ref, z_ref, hn_ref):
        xv = x_ref[...]
        hn = (xv * _rms_stats(xv) * g_ref[...]).astype(BF16)
        hn_ref[...] = hn
        for j in range(4):
            cols = slice(j * 512, (j + 1) * 512)
            z_ref[:, cols] = _dot(hn, w_ref[j]) + b_ref[:, cols]

    return _tied_call(
        body, after, name="mix_in", grid=(s // ts,),
        in_specs=[pl.BlockSpec((ts, D_MODEL), lambda i: (i, 0)),
                  pl.BlockSpec((1, D_MODEL), lambda i: (0, 0)),
                  pl.BlockSpec((4, D_MODEL, 512), lambda i: (0, 0, 0)),
                  pl.BlockSpec((1, 2048), lambda i: (0, 0))],
        out_specs=[pl.BlockSpec((ts, 2048), lambda i: (i, 0)),
                   pl.BlockSpec((ts, D_MODEL), lambda i: (i, 0))],
        out_shape=[jax.ShapeDtypeStruct((s, 2048), F32), jax.ShapeDtypeStruct((s, D_MODEL), BF16)],
        compiler_params=_params(("parallel",)),
    )(x, g, w_in, b_in)


def _lane_is_low_head():
    lane = lax.broadcasted_iota(jnp.int32, (1, GM_WIDTH), 1)
    return (lane & GM_HEAD_DIM) == 0


def _gm_mix(v_lo, v_hi, wpair_ref, bias_ref, mixed_ref, t):
    for n in range(t // CHUNK):
        rows = slice(n * CHUNK, (n + 1) * CHUNK)
        for j in range(GM_HEADS // 2):
            cols = slice(j * LANES, (j + 1) * LANES)
            rhs = jnp.concatenate([v_lo[rows, cols], v_hi[rows, cols]], axis=0)
            mixed_ref[rows, cols] = _dot(wpair_ref[j], rhs) + bias_ref[:, cols]


def _seqmix_fwd(z, cw, cb, lng, lnb, gg, gb, wpair, bias, t):
    s = z.shape[0]

    def body(z_ref, cw_ref, cb_ref, lng_ref, lnb_ref, gg_ref, gb_ref, wpair_ref, bias_ref,
             mix_ref, c1_ref, abuf, mixed_ref):
        i = pl.program_id(0)

        @pl.when(i == 0)
        def _():
            abuf[0:CONV_HALO, :] = jnp.zeros((CONV_HALO, CONV_WIDTH), F32)

        @pl.when(i > 0)
        def _():
            abuf[0:CONV_HALO, :] = abuf[t:t + CONV_HALO, :]

        abuf[CONV_HALO:, :] = z_ref[:, 0:512] * _sigmoid(z_ref[:, 512:1024])
        acc = jnp.zeros((t, CONV_WIDTH), F32)
        for k in range(CONV_KERNEL):
            acc = acc + cw_ref[k:k + 1, :] * abuf[pl.ds(CONV_HALO - (CONV_KERNEL - 1) + k, t), :]
        c1 = acc + cb_ref[...]
        c1_ref[...] = c1
        xh, _ = _ln_stats(c1)
        ln = xh * lng_ref[...] + lnb_ref[...]
        mix_ref[:, 0:512] = (ln * _sigmoid(ln)).astype(BF16)

        u, _ = _gelu_parts(z_ref[:, 1024:1536])
        gv, _ = _gelu_parts(z_ref[:, 1536:2048])
        vxh, _ = _ln_stats(gv)
        v = vxh * gg_ref[...] + gb_ref[...]
        low = _lane_is_low_head()
        v_lo = jnp.where(low, v, 0.0).astype(BF16)
        v_hi = jnp.where(low, 0.0, v).astype(BF16)
        _gm_mix(v_lo, v_hi, wpair_ref, bias_ref, mixed_ref, t)
        mix_ref[:, 512:1024] = (u * mixed_ref[...]).astype(BF16)

    vec = lambda n: pl.BlockSpec((1, n), lambda i: (0, 0))
    return pl.pallas_call(
        body, name="seqmix_fwd", grid=(s // t,),
        in_specs=[pl.BlockSpec((t, 2048), lambda i: (i, 0)),
                  pl.BlockSpec((CONV_HALO, CONV_WIDTH), lambda i: (0, 0)),
                  vec(512), vec(512), vec(512), vec(512), vec(512),
                  pl.BlockSpec((4, CHUNK, 2 * CHUNK), lambda i: (0, 0, 0)),
                  pl.BlockSpec((CHUNK, GM_WIDTH), lambda i: (0, 0))],
        out_specs=[pl.BlockSpec((t, D_MODEL), lambda i: (i, 0)),
                   pl.BlockSpec((t, CONV_WIDTH), lambda i: (i, 0))],
        out_shape=[jax.ShapeDtypeStruct((s, D_MODEL), BF16), jax.ShapeDtypeStruct((s, CONV_WIDTH), F32)],
        scratch_shapes=[pltpu.VMEM((t + CONV_HALO, CONV_WIDTH), F32), pltpu.VMEM((t, GM_WIDTH), F32)],
        compiler_params=_params(("arbitrary",)),
    )(z, cw, cb, lng, lnb, gg, gb, wpair, bias)


def _out_proj_q(x, mix, w_out, g, wq, ts, after=()):
    s = x.shape[0]

    def body(x_ref, mix_ref, wo_ref, g_ref, wq_ref, h1_ref, hn_ref, q_ref):
        h1 = x_ref[...] + _dot(mix_ref[...], wo_ref[...])
        h1_ref[...] = h1
        hn = (h1 * _rms_stats(h1) * g_ref[...]).astype(BF16)
        hn_ref[...] = hn
        q_ref[...] = _dot(hn, wq_ref[...]).astype(BF16)

    row = lambda dt: pl.BlockSpec((ts, D_MODEL), lambda i: (i, 0))
    full = pl.BlockSpec((D_MODEL, D_MODEL), lambda i: (0, 0))
    return _tied_call(
        body, after, name="out_proj_q", grid=(s // ts,),
        in_specs=[row(F32), row(BF16), full, pl.BlockSpec((1, D_MODEL), lambda i: (0, 0)), full],
        out_specs=[row(F32), row(BF16), row(BF16)],
        out_shape=[jax.ShapeDtypeStruct((s, D_MODEL), F32), jax.ShapeDtypeStruct((s, D_MODEL), BF16),
                   jax.ShapeDtypeStruct((s, D_MODEL), BF16)],
        compiler_params=_params(("parallel",)),
    )(x, mix, w_out, g, wq)


def _mem_kv(mem, g, wkv):
    m = mem.shape[0]

    def body(mem_ref, g_ref, w_ref, mn_ref, kv_ref):
        mv = mem_ref[...]
        mn = (mv * _rms_stats(mv) * g_ref[...]).astype(BF16)
        mn_ref[...] = mn
        for j in range(4):
            kv_ref[:, j * 512:(j + 1) * 512] = _dot(mn, w_ref[j]).astype(BF16)

    return pl.pallas_call(
        body, name="mem_kv",
        out_shape=[jax.ShapeDtypeStruct((m, D_MODEL), BF16), jax.ShapeDtypeStruct((m, 2 * D_MODEL), BF16)],
        compiler_params=pltpu.CompilerParams(vmem_limit_bytes=VMEM_LIMIT_BYTES),
    )(mem, g, wkv)


def _softmax_rows(sc):
    e = jnp.exp(sc - jnp.max(sc, axis=-1, keepdims=True))
    return e / jnp.sum(e, axis=-1, keepdims=True)


def _attn_fwd(q, kv, h1, wo, g, ts):
    s, m = q.shape[0], kv.shape[0]
    scale = XA_HEAD_DIM ** -0.5

    def body(q_ref, kv_ref, h1_ref, wo_ref, g_ref, o_ref, h2_ref, hn_ref):
        for h in range(XA_HEADS):
            cols = slice(h * XA_HEAD_DIM, (h + 1) * XA_HEAD_DIM)
            vcols = slice(D_MODEL + h * XA_HEAD_DIM, D_MODEL + (h + 1) * XA_HEAD_DIM)
            p = _softmax_rows(_dot_nt(q_ref[:, cols], kv_ref[:, cols]) * scale)
            o_ref[:, cols] = _dot(p.astype(BF16), kv_ref[:, vcols]).astype(BF16)
        h2 = h1_ref[...] + _dot(o_ref[...], wo_ref[...])
        h2_ref[...] = h2
        hn_ref[...] = (h2 * _rms_stats(h2) * g_ref[...]).astype(BF16)

    row = pl.BlockSpec((ts, D_MODEL), lambda i: (i, 0))
    return pl.pallas_call(
        body, name="attn_fwd", grid=(s // ts,),
        in_specs=[row, pl.BlockSpec((m, 2 * D_MODEL), lambda i: (0, 0)), row,
                  pl.BlockSpec((D_MODEL, D_MODEL), lambda i: (0, 0)),
                  pl.BlockSpec((1, D_MODEL), lambda i: (0, 0))],
        out_specs=[row, row, row],
        out_shape=[jax.ShapeDtypeStruct((s, D_MODEL), BF16), jax.ShapeDtypeStruct((s, D_MODEL), F32),
                   jax.ShapeDtypeStruct((s, D_MODEL), BF16)],
        compiler_params=_params(("parallel",)),
    )(q, kv, h1, wo, g)


def _ffn_up(hn, wgu, ts, after=()):
    s = hn.shape[0]

    def body(hn_ref, w_ref, gu_ref, act_ref):
        hv = hn_ref[...]
        gate = _dot(hv, w_ref[0, 0])
        up = _dot(hv, w_ref[1, 0])
        gu_ref[0] = gate
        gu_ref[1] = up
        act_ref[...] = (gate * _sigmoid(gate) * up).astype(BF16)

    return _tied_call(
        body, after, name="ffn_up", grid=(2, s // ts),
        in_specs=[pl.BlockSpec((ts, D_MODEL), lambda j, i: (i, 0)),
                  pl.BlockSpec((2, 1, D_MODEL, FFN_HALF), lambda j, i: (0, j, 0, 0))],
        out_specs=[pl.BlockSpec((2, ts, FFN_HALF), lambda j, i: (0, i, j)),
                   pl.BlockSpec((ts, FFN_HALF), lambda j, i: (i, j))],
        out_shape=[jax.ShapeDtypeStruct((2, s, FFN_HIDDEN), F32), jax.ShapeDtypeStruct((s, FFN_HIDDEN), BF16)],
        compiler_params=_params(("parallel", "parallel")),
    )(hn, wgu)


def _ffn_down_loss(act, wd, h2, g, target, ts):
    s = act.shape[0]

    def body(act_ref, wd_ref, h2_ref, g_ref, t_ref, dh_ref, dhb_ref, sq_ref, dg_ref):
        @pl.when(pl.program_id(0) == 0)
        def _():
            sq_ref[...] = jnp.zeros_like(sq_ref)
            dg_ref[...] = jnp.zeros_like(dg_ref)

        h3 = h2_ref[...] + _dot(act_ref[...], wd_ref[...])
        r = _rms_stats(h3)
        gv = g_ref[...]
        diff = h3 * r * gv - t_ref[...]
        sq_ref[...] += _rowsum(diff * diff)
        dh, dg = _rms_bwd(diff / D_MODEL, h3, r, gv)
        dh_ref[...] = dh
        dhb_ref[...] = dh.astype(BF16)
        dg_ref[...] += dg

    row = pl.BlockSpec((ts, D_MODEL), lambda i: (i, 0))
    vec = pl.BlockSpec((1, D_MODEL), lambda i: (0, 0))
    return pl.pallas_call(
        body, name="ffn_down_loss", grid=(s // ts,),
        in_specs=[pl.BlockSpec((ts, FFN_HIDDEN), lambda i: (i, 0)),
                  pl.BlockSpec((FFN_HIDDEN, D_MODEL), lambda i: (0, 0)), row, vec, row],
        out_specs=[row, row, vec, vec],
        out_shape=[jax.ShapeDtypeStruct((s, D_MODEL), F32), jax.ShapeDtypeStruct((s, D_MODEL), BF16),
                   jax.ShapeDtypeStruct((1, D_MODEL), F32), jax.ShapeDtypeStruct((1, D_MODEL), F32)],
        compiler_params=_params(("arbitrary",)),
    )(act, wd, h2, g, target)


def _grad_w(a, b, tk, tn, name, after=()):
    s, k = a.shape
    gb, _, n = b.shape
    nblk = n // tn
    tsr = GRAD_ROWS if s % GRAD_ROWS == 0 else s

    def body(a_ref, b_ref, o_ref):
        part = _dot_tn(a_ref[...], b_ref[0])

        @pl.when(pl.program_id(2) == 0)
        def _():
            o_ref[0] = part

        @pl.when(pl.program_id(2) > 0)
        def _():
            o_ref[0] += part

    return _tied_call(
        body, after, name=name, grid=(gb * nblk, k // tk, s // tsr),
        in_specs=[pl.BlockSpec((tsr, tk), lambda ni, ki, si: (si, ki)),
                  pl.BlockSpec((1, tsr, tn), lambda ni, ki, si: (ni // nblk, si, ni % nblk))],
        out_specs=pl.BlockSpec((1, tk, tn), lambda ni, ki, si: (ni, ki, 0)),
        out_shape=jax.ShapeDtypeStruct((gb * nblk, k, tn), F32),
        compiler_params=_params(("parallel", "parallel", "arbitrary")),
    )(a, b)


def _ffn_bwd_act(dh3, wd, gu, ts):
    s = dh3.shape[0]

    def body(dh_ref, wd_ref, gu_ref, dgu_ref):
        dact = _dot_nt(dh_ref[...], wd_ref[0])
        gate, up = gu_ref[0], gu_ref[1]
        sg = _sigmoid(gate)
        dgu_ref[0] = (dact * up * (sg * (1.0 + gate * (1.0 - sg)))).astype(BF16)
        dgu_ref[1] = (dact * (gate * sg)).astype(BF16)

    return pl.pallas_call(
        body, name="ffn_bwd_act", grid=(2, s // ts),
        in_specs=[pl.BlockSpec((ts, D_MODEL), lambda j, i: (i, 0)),
                  pl.BlockSpec((1, FFN_HALF, D_MODEL), lambda j, i: (j, 0, 0)),
                  pl.BlockSpec((2, ts, FFN_HALF), lambda j, i: (0, i, j))],
        out_specs=pl.BlockSpec((2, ts, FFN_HALF), lambda j, i: (0, i, j)),
        out_shape=jax.ShapeDtypeStruct((2, s, FFN_HIDDEN), BF16),
        compiler_params=_params(("parallel", "parallel")),
    )(dh3, wd, gu)


def _ffn_bwd_in(dgu, wgu, dh3, h2, g, ts, after=()):
    s = dh3.shape[0]

    def body(dgu_ref, w_ref, dh3_ref, h2_ref, g_ref, dh2_ref, dh2b_ref, dg_ref):
        @pl.when(pl.program_id(0) == 0)
        def _():
            dg_ref[...] = jnp.zeros_like(dg_ref)

        dhn = jnp.zeros((ts, D_MODEL), F32)
        for p in range(2):
            for j in range(2):
                dhn = dhn + _dot_nt(dgu_ref[p, :, j * FFN_HALF:(j + 1) * FFN_HALF], w_ref[2 * p + j])
        h2 = h2_ref[...]
        dv, dg = _rms_bwd(dhn, h2, _rms_stats(h2), g_ref[...])
        dh2 = dh3_ref[...] + dv
        dh2_ref[...] = dh2
        dh2b_ref[...] = dh2.astype(BF16)
        dg_ref[...] += dg

    row = pl.BlockSpec((ts, D_MODEL), lambda i: (i, 0))
    vec = pl.BlockSpec((1, D_MODEL), lambda i: (0, 0))
    return _tied_call(
        body, after, name="ffn_bwd_in", grid=(s // ts,),
        in_specs=[pl.BlockSpec((2, ts, FFN_HIDDEN), lambda i: (0, i, 0)),
                  pl.BlockSpec((4, D_MODEL, FFN_HALF), lambda i: (0, 0, 0)), row, row, vec],
        out_specs=[row, row, vec],
        out_shape=[jax.ShapeDtypeStruct((s, D_MODEL), F32), jax.ShapeDtypeStruct((s, D_MODEL), BF16),
                   jax.ShapeDtypeStruct((1, D_MODEL), F32)],
        compiler_params=_params(("arbitrary",)),
    )(dgu, wgu, dh3, h2, g)


def _attn_bwd(dh2, wo, q, kv, wq, h1, g, ts, after=()):
    s, m = q.shape[0], kv.shape[0]
    scale = XA_HEAD_DIM ** -0.5

    def body(dh2_ref, wo_ref, q_ref, kv_ref, wq_ref, h1_ref, g_ref, dh1_ref, dh1b_ref, dq_ref, dkv_ref, dg_ref):
        @pl.when(pl.program_id(0) == 0)
        def _():
            dkv_ref[...] = jnp.zeros_like(dkv_ref)
            dg_ref[...] = jnp.zeros_like(dg_ref)

        do = _dot_nt(dh2_ref[...].astype(BF16), wo_ref[...]).astype(BF16)
        for h in range(XA_HEADS):
            cols = slice(h * XA_HEAD_DIM, (h + 1) * XA_HEAD_DIM)
            vcols = slice(D_MODEL + h * XA_HEAD_DIM, D_MODEL + (h + 1) * XA_HEAD_DIM)
            qh, kh, vh, doh = q_ref[:, cols], kv_ref[:, cols], kv_ref[:, vcols], do[:, cols]
            p = _softmax_rows(_dot_nt(qh, kh) * scale)
            dp = _dot_nt(doh, vh)
            ds = (p * (dp - jnp.sum(dp * p, axis=-1, keepdims=True)) * scale).astype(BF16)
            dq_ref[:, cols] = _dot(ds, kh).astype(BF16)
            dkv_ref[:, cols] += _dot_tn(ds, qh)
            dkv_ref[:, vcols] += _dot_tn(p.astype(BF16), doh)
        dhn = _dot_nt(dq_ref[...], wq_ref[...])
        h1 = h1_ref[...]
        dv, dg = _rms_bwd(dhn, h1, _rms_stats(h1), g_ref[...])
        dh1 = dh2_ref[...] + dv
        dh1_ref[...] = dh1
        dh1b_ref[...] = dh1.astype(BF16)
        dg_ref[...] += dg

    row = pl.BlockSpec((ts, D_MODEL), lambda i: (i, 0))
    full = pl.BlockSpec((D_MODEL, D_MODEL), lambda i: (0, 0))
    kvs = pl.BlockSpec((m, 2 * D_MODEL), lambda i: (0, 0))
    vec = pl.BlockSpec((1, D_MODEL), lambda i: (0, 0))
    return _tied_call(
        body, after, name="attn_bwd", grid=(s // ts,),
        in_specs=[row, full, row, kvs, full, row, vec],
        out_specs=[row, row, row, kvs, vec],
        out_shape=[jax.ShapeDtypeStruct((s, D_MODEL), F32), jax.ShapeDtypeStruct((s, D_MODEL), BF16),
                   jax.ShapeDtypeStruct((s, D_MODEL), BF16),
                   jax.ShapeDtypeStruct((m, 2 * D_MODEL), F32), jax.ShapeDtypeStruct((1, D_MODEL), F32)],
        compiler_params=_params(("arbitrary",)),
    )(dh2, wo, q, kv, wq, h1, g)


def _mem_kv_bwd(dkv, mn, wkv, mem, g, after=()):
    m = mem.shape[0]

    def body(dkv_ref, mn_ref, w_ref, mem_ref, g_ref, dw_ref, dg_ref):
        dmn = jnp.zeros((m, D_MODEL), F32)
        mn = mn_ref[...]
        for j in range(4):
            dj = dkv_ref[:, j * 512:(j + 1) * 512].astype(BF16)
            dw_ref[j] = _dot_tn(mn, dj)
            dmn = dmn + _dot_nt(dj, w_ref[j])
        mv = mem_ref[...]
        dg_ref[...] = _rowsum(dmn * (mv * _rms_stats(mv)))

    return _tied_call(
        body, after, name="mem_kv_bwd", in_specs=[pl.BlockSpec(memory_space=pltpu.VMEM)] * 5,
        out_shape=[jax.ShapeDtypeStruct((4, D_MODEL, 512), F32), jax.ShapeDtypeStruct((1, D_MODEL), F32)],
        compiler_params=pltpu.CompilerParams(vmem_limit_bytes=VMEM_LIMIT_BYTES),
    )(dkv, mn, wkv, mem, g)


def _seqmix_bwd(dh1, x, z, c1, w_out, w_in, g_mix, cw, lng, lnb, gg, gb, wpair, wpair_t, bias, t, after=()):
    s = x.shape[0]
    nt = s // t
    halo_blocks = t // CONV_HALO

    def body(dh1_ref, x_ref, z_ref, zh_ref, c1_ref, wo_ref, wi_ref, gm_ref, cw_ref, lng_ref, lnb_ref,
             gg_ref, gb_ref, wpair_ref, wpt_ref, bias_ref,
             gx_ref, dz_ref, dcw_ref, dcb_ref, dlng_ref, dlnb_ref, dgg_ref, dgb_ref, dws_ref, dbs_ref,
             dbin_ref, dgm_ref, abuf, dbuf, mixed_ref, dv_ref):
        i = pl.program_id(0)
        tile = nt - 1 - i
        accs = (dcw_ref, dcb_ref, dlng_ref, dlnb_ref, dgg_ref, dgb_ref, dws_ref, dbs_ref, dbin_ref, dgm_ref)

        @pl.when(i == 0)
        def _():
            for r in accs:
                r[...] = jnp.zeros_like(r)
            dbuf[t:t + CONV_HALO, :] = jnp.zeros((CONV_HALO, CONV_WIDTH), F32)

        @pl.when(i > 0)
        def _():
            dbuf[t:t + CONV_HALO, :] = dbuf[0:CONV_HALO, :]

        dmix = _dot_nt(dh1_ref[...].astype(BF16), wo_ref[...])

        xh, rs = _ln_stats(c1_ref[...])
        lng = lng_ref[...]
        ln = xh * lng + lnb_ref[...]
        sl = _sigmoid(ln)
        dln = dmix[:, 0:512] * (sl * (1.0 + ln * (1.0 - sl)))
        dc1, dg_ln, db_ln = _ln_bwd(dln, xh, rs, lng)
        dlng_ref[...] += dg_ln
        dlnb_ref[...] += db_ln
        dcb_ref[...] += _rowsum(dc1)
        dbuf[0:t, :] = dc1

        zh = zh_ref[...]
        a_halo = zh[:, 0:512] * _sigmoid(zh[:, 512:1024])
        abuf[0:CONV_HALO, :] = jnp.where(tile > 0, a_halo, 0.0)
        za = z_ref[:, 0:512]
        sg = _sigmoid(z_ref[:, 512:1024])
        abuf[CONV_HALO:, :] = za * sg

        da = jnp.zeros((t, CONV_WIDTH), F32)
        for k in range(CONV_KERNEL):
            da = da + cw_ref[k:k + 1, :] * dbuf[pl.ds(CONV_KERNEL - 1 - k, t), :]
            dcw_ref[k:k + 1, :] += _rowsum(dc1 * abuf[pl.ds(CONV_HALO - (CONV_KERNEL - 1) + k, t), :])
        dza = da * sg
        dzg = da * za * (sg * (1.0 - sg))
        dz_ref[:, 0:512] = dza.astype(BF16)
        dz_ref[:, 512:1024] = dzg.astype(BF16)
        dbin_ref[:, 0:512] += _rowsum(dza)
        dbin_ref[:, 512:1024] += _rowsum(dzg)

        dgm = dmix[:, 512:1024]
        u, du_dz = _gelu_parts(z_ref[:, 1024:1536])
        gv, dgv_dz = _gelu_parts(z_ref[:, 1536:2048])
        vxh, vrs = _ln_stats(gv)
        ggv = gg_ref[...]
        v = vxh * ggv + gb_ref[...]
        low = _lane_is_low_head()
        v_lo = jnp.where(low, v, 0.0).astype(BF16)
        v_hi = jnp.where(low, 0.0, v).astype(BF16)
        _gm_mix(v_lo, v_hi, wpair_ref, bias_ref, mixed_ref, t)
        dzu = dgm * mixed_ref[...] * du_dz
        dm = dgm * u
        dm_lo = jnp.where(low, dm, 0.0).astype(BF16)
        dm_hi = jnp.where(low, 0.0, dm).astype(BF16)
        vb = v.astype(BF16)
        tril = (lax.broadcasted_iota(jnp.int32, (CHUNK, CHUNK), 1)
                <= lax.broadcasted_iota(jnp.int32, (CHUNK, CHUNK), 0))
        for n in range(t // CHUNK):
            rows = slice(n * CHUNK, (n + 1) * CHUNK)
            dbs_ref[...] += dm[rows, :]
            for j in range(GM_HEADS // 2):
                cols = slice(j * LANES, (j + 1) * LANES)
                stack = jnp.concatenate([dm_lo[rows, cols], dm_hi[rows, cols]], axis=0)
                dws = _dot_nt(stack, vb[rows, cols])
                dws_ref[2 * j] += jnp.where(tril, dws[0:CHUNK], 0.0)
                dws_ref[2 * j + 1] += jnp.where(tril, dws[CHUNK:2 * CHUNK], 0.0)
                dv_ref[rows, cols] = _dot(wpt_ref[j], stack)
        dgv, dg_gm, db_gm = _ln_bwd(dv_ref[...], vxh, vrs, ggv)
        dgg_ref[...] += dg_gm
        dgb_ref[...] += db_gm
        dzv = dgv * dgv_dz
        dz_ref[:, 1024:1536] = dzu.astype(BF16)
        dz_ref[:, 1536:2048] = dzv.astype(BF16)
        dbin_ref[:, 1024:1536] += _rowsum(dzu)
        dbin_ref[:, 1536:2048] += _rowsum(dzv)

        dhn = jnp.zeros((t, D_MODEL), F32)
        for j in range(4):
            dhn = dhn + _dot_nt(dz_ref[:, j * 512:(j + 1) * 512], wi_ref[j])
        xv = x_ref[...]
        dv, dg = _rms_bwd(dhn, xv, _rms_stats(xv), gm_ref[...])
        gx_ref[...] = dh1_ref[...] + dv
        dgm_ref[...] += dg

    rev = lambda w: pl.BlockSpec((t, w), lambda i: (nt - 1 - i, 0))
    const = lambda *shape: pl.BlockSpec(shape, lambda i: (0,) * len(shape))
    halo = pl.BlockSpec((CONV_HALO, D_MODEL), lambda i: (jnp.maximum((nt - 1 - i) * halo_blocks - 1, 0), 0))
    f32 = lambda *shape: jax.ShapeDtypeStruct(shape, F32)
    return _tied_call(
        body, after, name="seqmix_bwd", grid=(nt,),
        in_specs=[rev(D_MODEL), rev(D_MODEL), rev(2048), halo, rev(CONV_WIDTH),
                  const(D_MODEL, D_MODEL), const(4, D_MODEL, 512), const(1, D_MODEL),
                  const(CONV_HALO, CONV_WIDTH), const(1, 512), const(1, 512), const(1, 512), const(1, 512),
                  const(4, CHUNK, 2 * CHUNK), const(4, CHUNK, 2 * CHUNK), const(CHUNK, GM_WIDTH)],
        out_specs=[rev(D_MODEL), rev(2048),
                   const(CONV_HALO, CONV_WIDTH), const(1, 512), const(1, 512), const(1, 512), const(1, 512),
                   const(1, 512), const(GM_HEADS, CHUNK, CHUNK), const(CHUNK, GM_WIDTH), const(1, 2048),
                   const(1, D_MODEL)],
        out_shape=[f32(s, D_MODEL), jax.ShapeDtypeStruct((s, 2048), BF16),
                   f32(CONV_HALO, CONV_WIDTH), f32(1, 512), f32(1, 512), f32(1, 512), f32(1, 512),
                   f32(1, 512), f32(GM_HEADS, CHUNK, CHUNK), f32(CHUNK, GM_WIDTH), f32(1, 2048),
                   f32(1, D_MODEL)],
        scratch_shapes=[pltpu.VMEM((t + CONV_HALO, CONV_WIDTH), F32), pltpu.VMEM((t + CONV_HALO, CONV_WIDTH), F32),
                        pltpu.VMEM((t, GM_WIDTH), F32), pltpu.VMEM((t, GM_WIDTH), F32)],
        compiler_params=_params(("arbitrary",)),
    )(dh1, x, z, z, c1, w_out, w_in, g_mix, cw, lng, lnb, gg, gb, wpair, wpair_t, bias)


def _head_bias_grad(dbs):
    def body(d_ref, o_ref):
        dv = d_ref[...]
        lane = lax.broadcasted_iota(jnp.int32, (CHUNK, LANES), 1)
        acc = jnp.zeros((CHUNK, LANES), F32)
        for h in range(GM_HEADS):
            sh = jnp.sum(dv[:, h * GM_HEAD_DIM:(h + 1) * GM_HEAD_DIM], axis=-1, keepdims=True)
            acc = acc + jnp.where(lane == h, sh, 0.0)
        o_ref[...] = acc

    return pl.pallas_call(body, name="head_bias_grad",
                          out_shape=jax.ShapeDtypeStruct((CHUNK, LANES), F32))(dbs)


def _pack(arrays, rows):
    flat = jnp.concatenate([a.reshape(-1) for a in arrays])
    flat = jnp.pad(flat, (0, rows * LANES - flat.shape[0]))
    return flat.reshape(rows, LANES)


def _unpack(buf, shapes):
    flat = buf.reshape(-1)
    out, off = [], 0
    for shp in shapes:
        size = 1
        for d in shp:
            size *= d
        out.append(flat[off:off + size].reshape(shp))
        off += size
    return out


def _rows_for(shapes, multiple):
    total = 0
    for shp in shapes:
        size = 1
        for d in shp:
            size *= d
        total += size
    rows = -(-total // LANES)
    return -(-rows // multiple) * multiple


def kernel(x, mem, norm_mix_g, w_in, b_in, conv_w, conv_b, conv_ln_g, conv_ln_b, gm_ln_g, gm_ln_b, gm_w_s, gm_b_s, w_out, norm_xa_g, mem_norm_g, xa_wq, xa_wkv, xa_wo, norm_ffn_g, ffn_w_gate_up, ffn_w_down, final_norm_g, loss_target, m_norm_mix_g, m_w_in, m_b_in, m_conv_w, m_conv_b, m_conv_ln_g, m_conv_ln_b, m_gm_ln_g, m_gm_ln_b, m_gm_w_s, m_gm_b_s, m_w_out, m_norm_xa_g, m_mem_norm_g, m_xa_wq, m_xa_wkv, m_xa_wo, m_norm_ffn_g, m_ffn_w_gate_up, m_ffn_w_down, m_final_norm_g, v_norm_mix_g, v_w_in, v_b_in, v_conv_w, v_conv_b, v_conv_ln_g, v_conv_ln_b, v_gm_ln_g, v_gm_ln_b, v_gm_w_s, v_gm_b_s, v_w_out, v_norm_xa_g, v_mem_norm_g, v_xa_wq, v_xa_wkv, v_xa_wo, v_norm_ffn_g, v_ffn_w_gate_up, v_ffn_w_down, v_final_norm_g):
    s = x.shape[1]
    ts = _row_tile(s)
    tb = max(CHUNK, ts // 2)
    cx, cy, cc = _mesh_pos()
    chip = 2 * cx + cy
    pos = jnp.stack([chip, cc]).astype(jnp.int32)
    row = lambda a: a.reshape(1, -1)
    x2, mem2, tgt2 = x[0], mem[0], loss_target[0]

    big = dict(w_in=w_in, xa_wkv=xa_wkv, w_out=w_out, xa_wq=xa_wq, xa_wo=xa_wo,
               ffn_w_gate_up=ffn_w_gate_up, ffn_w_down=ffn_w_down)
    big_names = list(big)
    halves = lambda a: a.reshape(2, a.shape[0] // 2, a.shape[1])
    cast = {nm: _cast_into_slot(halves(big[nm]), pos, BF16, "cast_" + nm) for nm in big_names}
    conv_w_pad = jnp.pad(conv_w, ((0, CONV_HALO - CONV_KERNEL), (0, 0)))
    first = _gather_shards([cast["w_in"], _cast_into_slot(halves(conv_w_pad), pos, F32, "slot_conv_w")])
    shaped = lambda buf, nm: buf.reshape(N_CHIPS, big[nm].shape[0], big[nm].shape[1])

    def start_gather(names, after):
        return _gather_start([cast[nm] for nm in names], "gather_start_" + names[0], after)

    def finish_gather(names, started, after):
        send_sems, recv_sems, bufs, _ = started
        landed = _gather_wait(send_sems, recv_sems, bufs, after, "gather_wait_" + names[0])
        return {nm: shaped(b, nm) for nm, b in zip(names, _pass_to_sibling(landed, "pass_" + names[0]))}

    attn_names = ["w_out", "xa_wq", "xa_wkv", "xa_wo"]
    gather_attn = start_gather(attn_names, ())
    w_in_g = shaped(first[0], "w_in")
    cw_g = jnp.concatenate([first[1][k].reshape(CONV_HALO, LANES) for k in range(N_CHIPS)], axis=1)

    tril = jnp.tril(jnp.ones((CHUNK, CHUNK), dtype=bool))
    ws = jnp.where(tril[None], gm_w_s, 0.0)
    wpair = jnp.concatenate([ws[0::2], ws[1::2]], axis=2).astype(BF16)
    ws_t = jnp.swapaxes(ws, 1, 2)
    wpair_t = jnp.concatenate([ws_t[0::2], ws_t[1::2]], axis=2).astype(BF16)
    bias = jnp.repeat(gm_b_s.T, GM_HEAD_DIM, axis=1)

    z, hn1 = _mix_in(x2, row(norm_mix_g), w_in_g, row(b_in), ts, after=gather_attn[3])
    mix, c1 = _seqmix_fwd(z, cw_g, row(conv_b), row(conv_ln_g), row(conv_ln_b), row(gm_ln_g), row(gm_ln_b),
                          wpair, bias, ts)
    gw = finish_gather(attn_names, gather_attn, mix)
    w_out_g = gw["w_out"].reshape(D_MODEL, D_MODEL)
    wq_g = gw["xa_wq"].reshape(D_MODEL, D_MODEL)
    wkv_g = gw["xa_wkv"]
    wo_g = gw["xa_wo"].reshape(D_MODEL, D_MODEL)
    gather_gu = start_gather(["ffn_w_gate_up"], w_out_g)
    h1, hn2, q = _out_proj_q(x2, mix, w_out_g, row(norm_xa_g), wq_g, ts, after=gather_gu[3])
    mn, kv = _mem_kv(mem2, row(mem_norm_g), wkv_g)
    o, h2, hn3 = _attn_fwd(q, kv, h1, wo_g, row(norm_ffn_g), ts)
    wgu_g = finish_gather(["ffn_w_gate_up"], gather_gu, hn3)["ffn_w_gate_up"]
    gather_down = start_gather(["ffn_w_down"], wgu_g)
    gu, act = _ffn_up(hn3, wgu_g.reshape(2, 2, D_MODEL, FFN_HALF), ts, after=gather_down[3])
    wd_g = finish_gather(["ffn_w_down"], gather_down, act)["ffn_w_down"].reshape(FFN_HIDDEN, D_MODEL)
    dh3, dh3_b, sq, d_final_g = _ffn_down_loss(act, wd_g, h2, row(final_norm_g), tgt2, ts)
    loss = lax.psum(0.5 * jnp.sum(sq) / D_MODEL, ("x", "y", "c"))

    def split(g, nm):
        r, c = big[nm].shape
        return g.reshape(N_CHIPS, 2, r // 2, c)

    def chip_sums_of(group, arrays):
        got = _swap_halves(arrays, "swap_halves_" + group[0])
        both = [_add_half(g, r, pos, "chip_sum_" + nm) for g, r, nm in zip(arrays, got, group)]
        return [b[0] for b in both], [b[1] for b in both]

    def start_swap(group, grads):
        return _swap_start([split(g, nm) for g, nm in zip(grads, group)], "swap_start_" + group[0])

    def start_exchange(group, swapping, after):
        sems, arrays, lands, _ = swapping
        arrays, got = _swap_wait(sems, arrays, lands, after, "swap_wait_" + group[0])
        both = [_add_half(g, r, pos, "chip_sum_" + nm) for g, r, nm in zip(arrays, got, group)]
        return _exchange_start([b[0] for b in both], [b[1] for b in both], "exchange_start_" + group[0])

    def finish_exchange(group, started, after):
        sems, sums, parts, _ = started
        parts = _exchange_wait(sems, sums, parts, after, "exchange_wait_" + group[0])
        return [_sum_chips(p, pos, "total_" + nm) for p, nm in zip(parts, group)]

    as3 = lambda a: a.reshape((1,) + a.shape)
    halves_of = {}

    dgu = _ffn_bwd_act(dh3_b, wd_g.reshape(2, FFN_HALF, D_MODEL), gu, ts)
    g_down = _grad_w(act, as3(dh3_b), FFN_HALF, D_MODEL, "grad_ffn_w_down")
    group_a = ["ffn_w_down"]
    swap_a = start_swap(group_a, [g_down])
    dh2, dh2_b, d_ffn_g = _ffn_bwd_in(dgu, wgu_g, dh3, h2, row(norm_ffn_g), tb, after=swap_a[3])
    exch_a = start_exchange(group_a, swap_a, dh2)
    g_gu = _grad_w(hn3, dgu, D_MODEL, FFN_HALF, "grad_ffn_w_gate_up", after=exch_a[3])
    halves_of.update(zip(group_a, finish_exchange(group_a, exch_a, g_gu)))

    group_b = ["ffn_w_gate_up"]
    swap_b = start_swap(group_b, [g_gu])
    dh1, dh1_b, dq, dkv, d_xa_g = _attn_bwd(dh2, wo_g, q, kv, wq_g, h1, row(norm_xa_g), ts, after=swap_b[3])
    exch_b = start_exchange(group_b, swap_b, dh1)
    g_wkv, d_mem_g = _mem_kv_bwd(dkv, mn, wkv_g, mem2, row(mem_norm_g), after=exch_b[3])
    g_wo = _grad_w(o, as3(dh2_b), D_MODEL, D_MODEL, "grad_xa_wo", after=exch_b[3])
    g_wq = _grad_w(hn2, as3(dq), D_MODEL, D_MODEL, "grad_xa_wq", after=exch_b[3])
    g_wout = _grad_w(mix, as3(dh1_b), D_MODEL, D_MODEL, "grad_w_out", after=exch_b[3])
    halves_of.update(zip(group_b, finish_exchange(group_b, exch_b, (g_wkv, g_wo, g_wq, g_wout))))

    group_c = ["xa_wo", "xa_wq", "xa_wkv", "w_out"]
    swap_c = start_swap(group_c, [g_wo, g_wq, g_wkv, g_wout])
    (gx, dz, d_cw, d_cb, d_lng, d_lnb, d_gg, d_gb, d_ws, d_bs_sum, d_bin, d_mix_g) = _seqmix_bwd(
        dh1, x2, z, c1, w_out_g, w_in_g, row(norm_mix_g), cw_g, row(conv_ln_g), row(conv_ln_b),
        row(gm_ln_g), row(gm_ln_b), wpair, wpair_t, bias, tb, after=swap_c[3])
    d_bs = _head_bias_grad(d_bs_sum)[:, :GM_HEADS].T
    exch_c = start_exchange(group_c, swap_c, dz)
    g_win = _grad_w(hn1, as3(dz), D_MODEL, 512, "grad_w_in", after=exch_c[3])
    halves_of.update(zip(group_c, finish_exchange(group_c, exch_c, g_win)))

    small_names = ["norm_mix_g", "b_in", "conv_w", "conv_b", "conv_ln_g", "conv_ln_b", "gm_ln_g", "gm_ln_b",
                   "gm_w_s", "gm_b_s", "norm_xa_g", "mem_norm_g", "norm_ffn_g", "final_norm_g"]
    small_grads = dict(norm_mix_g=d_mix_g, b_in=d_bin, conv_w=d_cw[:CONV_KERNEL], conv_b=d_cb, conv_ln_g=d_lng,
                       conv_ln_b=d_lnb, gm_ln_g=d_gg, gm_ln_b=d_gb, gm_w_s=d_ws, gm_b_s=d_bs, norm_xa_g=d_xa_g,
                       mem_norm_g=d_mem_g, norm_ffn_g=d_ffn_g, final_norm_g=d_final_g)
    full_shapes = dict(norm_mix_g=(D_MODEL,), b_in=(2048,), conv_w=(CONV_KERNEL, CONV_WIDTH), conv_b=(512,),
                       conv_ln_g=(512,), conv_ln_b=(512,), gm_ln_g=(512,), gm_ln_b=(512,),
                       gm_w_s=(GM_HEADS, CHUNK, CHUNK), gm_b_s=(GM_HEADS, CHUNK), norm_xa_g=(D_MODEL,),
                       mem_norm_g=(D_MODEL,), norm_ffn_g=(D_MODEL,), final_norm_g=(D_MODEL,))
    pack_rows = _rows_for([full_shapes[nm] for nm in small_names], 32)
    small_pack = _pack([small_grads[nm] for nm in small_names], pack_rows)

    group_d = ["w_in", "small"]
    sums_d, parts_d = chip_sums_of(group_d, [split(g_win, "w_in"),
                                             small_pack.reshape(1, 2, pack_rows // 2, LANES)])
    parts_d = _exchange_chip_sums(sums_d, parts_d)
    halves_of.update(zip(group_d, [_sum_chips(p, pos, "total_" + nm) for p, nm in zip(parts_d, group_d)]))

    red_names = big_names + ["small"]
    joined = _join_halves([halves_of[nm] for nm in red_names])
    grads = {nm: joined[i].reshape(big[nm].shape) for i, nm in enumerate(big_names)}
    small_red = _unpack(joined[-1].reshape(pack_rows, LANES), [full_shapes[nm] for nm in small_names])
    for nm, g in zip(small_names, small_red):
        grads[nm] = g
    grads["conv_w"] = lax.dynamic_slice(grads["conv_w"], (0, chip * LANES), (CONV_KERNEL, LANES))

    weights = dict(norm_mix_g=norm_mix_g, w_in=w_in, b_in=b_in, conv_w=conv_w, conv_b=conv_b, conv_ln_g=conv_ln_g,
                   conv_ln_b=conv_ln_b, gm_ln_g=gm_ln_g, gm_ln_b=gm_ln_b, gm_w_s=gm_w_s, gm_b_s=gm_b_s, w_out=w_out,
                   norm_xa_g=norm_xa_g, mem_norm_g=mem_norm_g, xa_wq=xa_wq, xa_wkv=xa_wkv, xa_wo=xa_wo,
                   norm_ffn_g=norm_ffn_g, ffn_w_gate_up=ffn_w_gate_up, ffn_w_down=ffn_w_down,
                   final_norm_g=final_norm_g)
    m_in = dict(norm_mix_g=m_norm_mix_g, w_in=m_w_in, b_in=m_b_in, conv_w=m_conv_w, conv_b=m_conv_b,
                conv_ln_g=m_conv_ln_g, conv_ln_b=m_conv_ln_b, gm_ln_g=m_gm_ln_g, gm_ln_b=m_gm_ln_b, gm_w_s=m_gm_w_s,
                gm_b_s=m_gm_b_s, w_out=m_w_out, norm_xa_g=m_norm_xa_g, mem_norm_g=m_mem_norm_g, xa_wq=m_xa_wq,
                xa_wkv=m_xa_wkv, xa_wo=m_xa_wo, norm_ffn_g=m_norm_ffn_g, ffn_w_gate_up=m_ffn_w_gate_up,
                ffn_w_down=m_ffn_w_down, final_norm_g=m_final_norm_g)
    v_in = dict(norm_mix_g=v_norm_mix_g, w_in=v_w_in, b_in=v_b_in, conv_w=v_conv_w, conv_b=v_conv_b,
                conv_ln_g=v_conv_ln_g, conv_ln_b=v_conv_ln_b, gm_ln_g=v_gm_ln_g, gm_ln_b=v_gm_ln_b, gm_w_s=v_gm_w_s,
                gm_b_s=v_gm_b_s, w_out=v_w_out, norm_xa_g=v_norm_xa_g, mem_norm_g=v_mem_norm_g, xa_wq=v_xa_wq,
                xa_wkv=v_xa_wkv, xa_wo=v_xa_wo, norm_ffn_g=v_norm_ffn_g, ffn_w_gate_up=v_ffn_w_gate_up,
                ffn_w_down=v_ffn_w_down, final_norm_g=v_final_norm_g)
    delta, new_m, new_v = {}, {}, {}
    for nm in big_names:
        delta[nm], new_m[nm], new_v[nm] = _adamw(weights[nm], grads[nm], m_in[nm], v_in[nm], "adamw_" + nm)
    local_shapes = [weights[nm].shape for nm in small_names]
    adam_rows = _rows_for(local_shapes, LANES)
    packed = [_pack([src[nm] for nm in small_names], adam_rows) for src in (weights, grads, m_in, v_in)]
    outs = _adamw(*packed, "adamw_small")
    for dst, buf in zip((delta, new_m, new_v), outs):
        for nm, a in zip(small_names, _unpack(buf, local_shapes)):
            dst[nm] = a

    order = ["norm_mix_g", "w_in", "b_in", "conv_w", "conv_b", "conv_ln_g", "conv_ln_b", "gm_ln_g", "gm_ln_b",
             "gm_w_s", "gm_b_s", "w_out", "norm_xa_g", "mem_norm_g", "xa_wq", "xa_wkv", "xa_wo", "norm_ffn_g",
             "ffn_w_gate_up", "ffn_w_down", "final_norm_g"]
    fit = lambda a, nm: a.reshape(weights[nm].shape)
    return (loss, gx.reshape(x.shape),
            *[fit(grads[nm], nm) for nm in order], *[fit(delta[nm], nm) for nm in order],
            *[fit(new_m[nm], nm) for nm in order], *[fit(new_v[nm], nm) for nm in order])
```

```python
import functools

import jax
import jax.numpy as jnp
from jax import lax
from jax.experimental import pallas as pl
from jax.experimental.pallas import tpu as pltpu

F32 = jnp.float32
BF16 = jnp.bfloat16

D_MODEL = 1024
CONV_WIDTH = 512
GM_WIDTH = 512
CONV_KERNEL = 31
CONV_HALO = 32
GRAD_ROWS = 2048
CHUNK = 128
GM_HEADS = 8
GM_HEAD_DIM = 64
XA_HEADS = 4
XA_HEAD_DIM = 256
FFN_HIDDEN = 2816
FFN_HALF = FFN_HIDDEN // 2
RMS_EPS = 1e-6
LN_EPS = 1e-5
N_CHIPS = 4
LANES = 128
SUBLANES = 8

ADAM_LR = 0.001
ADAM_B1 = 0.9
ADAM_B2 = 0.999
ADAM_EPS = 1e-08
ADAM_WD = 0.01
ADAM_STEP = 10

VMEM_LIMIT_BYTES = 56 * 1024 * 1024
MESH = pl.DeviceIdType.MESH
ANY = pl.BlockSpec(memory_space=pl.ANY)
HBM_SPEC = pl.BlockSpec(memory_space=pltpu.HBM)
SEM_SPEC = pl.BlockSpec(memory_space=pltpu.SEMAPHORE)

_NT = (((1,), (1,)), ((), ()))
_TN = (((0,), (0,)), ((), ()))
_GELU_C = 0.7978845608028654
_GELU_A = 0.044715


def _dot(a, b):
    return jnp.dot(a, b, preferred_element_type=F32)


def _dot_nt(a, b):
    return lax.dot_general(a, b, _NT, preferred_element_type=F32)


def _dot_tn(a, b):
    return lax.dot_general(a, b, _TN, preferred_element_type=F32)


def _mean(v):
    return jnp.mean(v, axis=-1, keepdims=True)


def _rowsum(v):
    return jnp.sum(v, axis=0, keepdims=True)


def _sigmoid(v):
    return 1.0 / (1.0 + jnp.exp(-v))


def _gelu_parts(v):
    v2 = v * v
    t = jnp.tanh(_GELU_C * (v + _GELU_A * v * v2))
    g = 0.5 * v * (1.0 + t)
    dg = 0.5 * (1.0 + t) + 0.5 * v * (1.0 - t * t) * (_GELU_C * (1.0 + 3.0 * _GELU_A * v2))
    return g, dg


def _rms_stats(v):
    return lax.rsqrt(_mean(v * v) + RMS_EPS)


def _rms_bwd(dy, v, r, g):
    n = v * r
    dn = dy * g
    dv = r * (dn - n * _mean(dn * n))
    return dv, _rowsum(dy * n)


def _ln_stats(v):
    mu = _mean(v)
    xc = v - mu
    rs = lax.rsqrt(_mean(xc * xc) + LN_EPS)
    return xc * rs, rs


def _ln_bwd(dy, xh, rs, g):
    dxh = dy * g
    dv = rs * (dxh - _mean(dxh) - xh * _mean(dxh * xh))
    return dv, _rowsum(dy * xh), _rowsum(dy)


def _params(sem):
    return pltpu.CompilerParams(dimension_semantics=sem, vmem_limit_bytes=VMEM_LIMIT_BYTES)


def _row_tile(s):
    return 512 if s % 512 == 0 and s >= 2048 else 128


def _mesh_pos():
    return lax.axis_index("x"), lax.axis_index("y"), lax.axis_index("c")


def _cast_into_slot(w, pos, dtype, name):
    _, h, c = w.shape

    def body(pos_ref, w_ref, o_ref):
        o_ref[0] = w_ref[...].astype(dtype)

    return pl.pallas_call(
        body, name=name,
        grid_spec=pltpu.PrefetchScalarGridSpec(
            num_scalar_prefetch=1, grid=(2,),
            in_specs=[pl.BlockSpec((1, h, c), lambda i, p: (i, 0, 0))],
            out_specs=pl.BlockSpec((1, 1, h, c), lambda i, p: (p[0], i, 0, 0))),
        out_shape=jax.ShapeDtypeStruct((N_CHIPS, 2, h, c), dtype),
        compiler_params=_params(("parallel",)),
    )(pos, w)


def _adam_rows(r, c):
    for tr in (r, 1024, 704, 640, 512, 352, 320, 256, 128, 64, 32, 16, 8):
        if r % tr == 0 and tr * c * 4 <= (3 << 19):
            return tr
    return r


def _adam_update(w, g, m, v):
    nm = ADAM_B1 * m + (1.0 - ADAM_B1) * g
    nv = ADAM_B2 * v + (1.0 - ADAM_B2) * (g * g)
    m_hat = nm / (1.0 - ADAM_B1 ** ADAM_STEP)
    v_hat = nv / (1.0 - ADAM_B2 ** ADAM_STEP)
    return -ADAM_LR * (m_hat / (jnp.sqrt(v_hat) + ADAM_EPS) + ADAM_WD * w), nm, nv


def _adamw(w, g, m, v, name, after=()):
    r, c = w.shape
    tr = _adam_rows(r, c)

    def body(w_ref, g_ref, m_ref, v_ref, d_ref, nm_ref, nv_ref):
        d_ref[...], nm_ref[...], nv_ref[...] = _adam_update(w_ref[...], g_ref[...], m_ref[...], v_ref[...])

    spec = pl.BlockSpec((tr, c), lambda i: (i, 0))
    shp = jax.ShapeDtypeStruct((r, c), F32)
    return _tied_call(
        body, after, name=name, grid=(r // tr,),
        in_specs=[spec] * 4, out_specs=[spec] * 3, out_shape=[shp] * 3,
        compiler_params=_params(("parallel",)),
    )(w, g, m, v)


def _adamw_small(gpack, pos, params, offsets, conv_at):
    n = len(params)

    def body(pos_ref, g_ref, *refs):
        ins, outs = refs[:3 * n], refs[3 * n:]
        for k in range(n):
            rows = params[k][0].shape[0]
            start = offsets[k]
            if k == conv_at:
                start = pl.multiple_of(start + pos_ref[0] * CONV_HALO, SUBLANES)
            g = g_ref[pl.ds(start, rows), :]
            outs[4 * k][...] = g
            outs[4 * k + 1][...], outs[4 * k + 2][...], outs[4 * k + 3][...] = _adam_update(
                ins[3 * k][...], g, ins[3 * k + 1][...], ins[3 * k + 2][...])

    flat = [a for p in params for a in p]
    vmem = pl.BlockSpec(memory_space=pltpu.VMEM)
    return pl.pallas_call(
        body, name="adamw_small",
        in_specs=[pl.BlockSpec(memory_space=pltpu.SMEM), vmem] + [vmem] * len(flat),
        out_specs=[vmem] * (4 * n),
        out_shape=[jax.ShapeDtypeStruct(p[0].shape, F32) for p in params for _ in range(4)],
    )(pos, gpack, *flat)


def _as_tuple(after):
    return tuple(after) if isinstance(after, (tuple, list)) else (after,)


def _tied_call(body, after, *, in_specs, **kwargs):
    after = _as_tuple(after)
    n_in, n_after = len(in_specs), len(after)

    def tied(*refs):
        body(*refs[:n_in], *refs[n_in + n_after:])

    call = pl.pallas_call(tied, in_specs=list(in_specs) + [ANY] * n_after, **kwargs)
    return lambda *operands: call(*operands, *after)


def _other_chips(x, y):
    return [(1 - x, y), (x, 1 - y), (1 - x, 1 - y)]


def _gather_shards(bufs):
    n = len(bufs)

    def body(*refs):
        outs = refs[n:2 * n]
        ici_send, ici_recv, fwd_send, fwd_recv = refs[2 * n:]
        x, y, c = _mesh_pos()
        me = 2 * x + y
        sib = (x, y, 1 - c)
        chips = _other_chips(x, y)

        def ici(a, k, chip_idx, to):
            return pltpu.make_async_remote_copy(
                src_ref=outs[a].at[chip_idx, c], dst_ref=outs[a].at[chip_idx, c],
                send_sem=ici_send.at[a, k], recv_sem=ici_recv.at[a, k],
                device_id=to, device_id_type=MESH)

        def fwd(a, k, chip_idx, half):
            return pltpu.make_async_remote_copy(
                src_ref=outs[a].at[chip_idx, half], dst_ref=outs[a].at[chip_idx, half],
                send_sem=fwd_send.at[a, k], recv_sem=fwd_recv.at[a, k],
                device_id=sib, device_id_type=MESH)

        sends = [ici(a, k, me, (*chips[k], c)) for a in range(n) for k in range(3)]
        for cp in sends:
            cp.start()
        passed = []
        for a in range(n):
            for k in range(3):
                ck = 2 * chips[k][0] + chips[k][1]
                ici(a, k, ck, (*chips[k], c)).wait_recv()
                cp = fwd(a, k, ck, c)
                cp.start()
                passed.append(cp)
        for a in range(n):
            for k in range(3):
                ck = 2 * chips[k][0] + chips[k][1]
                fwd(a, k, ck, 1 - c).wait_recv()
        for cp in sends + passed:
            cp.wait_send()

    out_shape = [jax.ShapeDtypeStruct(b.shape, b.dtype) for b in bufs]
    return pl.pallas_call(
        body, name="gather_weights",
        in_specs=[ANY] * n, out_specs=[ANY] * n, out_shape=out_shape,
        input_output_aliases={a: a for a in range(n)},
        scratch_shapes=[pltpu.SemaphoreType.DMA((n, 3))] * 4,
    )(*bufs)


def _gather_descriptors(bufs, send_of, recv_of):
    x, y, c = _mesh_pos()
    me = 2 * x + y
    chips = _other_chips(x, y)
    sends, arrivals = [], []
    for a in range(len(bufs)):
        for k in range(3):
            ck = 2 * chips[k][0] + chips[k][1]

            def copy(slot, a=a, k=k):
                return pltpu.make_async_remote_copy(
                    src_ref=bufs[a].at[slot, c], dst_ref=bufs[a].at[slot, c],
                    send_sem=send_of(a, k), recv_sem=recv_of(a, k),
                    device_id=(*chips[k], c), device_id_type=MESH)

            sends.append(functools.partial(copy, me))
            arrivals.append(functools.partial(copy, ck))
    return sends, arrivals


def _gather_start(bufs, name, after=()):
    n = len(bufs)
    ns = 3 * n

    def body(*refs):
        sems = refs[n:n + 2 * ns]
        thru = refs[n + 2 * ns:2 * n + 2 * ns]
        token = refs[2 * n + 2 * ns]
        sends, _ = _gather_descriptors(thru, lambda a, k: sems[3 * a + k], lambda a, k: sems[ns + 3 * a + k])
        for cp in sends:
            cp().start()
        token[...] = jnp.zeros_like(token)

    held = [pltpu.with_memory_space_constraint(b, pltpu.HBM) for b in bufs]
    out = _tied_call(
        body, after, name=name,
        out_shape=(*[pltpu.SemaphoreType.DMA(())] * (2 * ns), *[pltpu.HBM(b.shape, b.dtype) for b in held],
                   jax.ShapeDtypeStruct((8, LANES), F32)),
        in_specs=[HBM_SPEC] * n,
        out_specs=(*[SEM_SPEC] * (2 * ns), *[HBM_SPEC] * n, pl.BlockSpec(memory_space=pltpu.VMEM)),
        input_output_aliases={i: 2 * ns + i for i in range(n)},
        compiler_params=pltpu.CompilerParams(has_side_effects=pltpu.SideEffectType.DATAFLOW_SIDE_EFFECTING),
    )(*held)
    return list(out[:ns]), list(out[ns:2 * ns]), list(out[2 * ns:2 * ns + n]), out[2 * ns + n]


def _gather_wait(send_sems, recv_sems, bufs, after, name):
    n = len(bufs)
    ns = 3 * n

    def body(*refs):
        buf_ref = refs[:n]
        sem_ref = refs[n:n + 2 * ns]
        sends, arrivals = _gather_descriptors(buf_ref, lambda a, k: sem_ref[3 * a + k],
                                              lambda a, k: sem_ref[ns + 3 * a + k])
        for cp in sends:
            cp().wait_send()
        for cp in arrivals:
            cp().wait_recv()

    out = pl.pallas_call(
        body, name=name,
        out_shape=tuple(pltpu.HBM(b.shape, b.dtype) for b in bufs),
        in_specs=[HBM_SPEC] * n + [SEM_SPEC] * (2 * ns) + [ANY],
        out_specs=tuple([HBM_SPEC] * n),
        input_output_aliases={i: i for i in range(n)},
        compiler_params=pltpu.CompilerParams(has_side_effects=pltpu.SideEffectType.DATAFLOW_SIDE_EFFECTING),
    )(*bufs, *send_sems, *recv_sems, after)
    return list(out)


def _pass_to_sibling(bufs, name):
    n = len(bufs)

    def body(*refs):
        outs = refs[n:2 * n]
        send_sem, recv_sem = refs[2 * n:]
        x, y, c = _mesh_pos()
        chips = _other_chips(x, y)

        def half(a, k, which):
            ck = 2 * chips[k][0] + chips[k][1]
            return pltpu.make_async_remote_copy(
                src_ref=outs[a].at[ck, which], dst_ref=outs[a].at[ck, which],
                send_sem=send_sem.at[a, k], recv_sem=recv_sem.at[a, k],
                device_id=(x, y, 1 - c), device_id_type=MESH)

        sends = [half(a, k, c) for a in range(n) for k in range(3)]
        for cp in sends:
            cp.start()
        for a in range(n):
            for k in range(3):
                half(a, k, 1 - c).wait_recv()
        for cp in sends:
            cp.wait_send()

    return pl.pallas_call(
        body, name=name,
        in_specs=[ANY] * n, out_specs=[ANY] * n,
        out_shape=[jax.ShapeDtypeStruct(b.shape, b.dtype) for b in bufs],
        input_output_aliases={a: a for a in range(n)},
        scratch_shapes=[pltpu.SemaphoreType.DMA((n, 3))] * 2,
    )(*bufs)


def _swap_halves(grads, name):
    n = len(grads)

    def body(*refs):
        ins, outs = refs[:n], refs[n:2 * n]
        send_sem, recv_sem = refs[2 * n:]
        x, y, c = _mesh_pos()
        cps = [pltpu.make_async_remote_copy(
            src_ref=ins[a].at[:, pl.ds(1 - c, 1)], dst_ref=outs[a],
            send_sem=send_sem.at[a], recv_sem=recv_sem.at[a],
            device_id=(x, y, 1 - c), device_id_type=MESH) for a in range(n)]
        for cp in cps:
            cp.start()
        for cp in cps:
            cp.wait()

    out_shape = [jax.ShapeDtypeStruct((g.shape[0], 1) + g.shape[2:], g.dtype) for g in grads]
    return pl.pallas_call(
        body, name=name,
        in_specs=[ANY] * n, out_specs=[ANY] * n, out_shape=out_shape,
        scratch_shapes=[pltpu.SemaphoreType.DMA((n,))] * 2,
    )(*grads)


def _swap_descriptors(grads, lands, send_of, recv_of):
    x, y, c = _mesh_pos()
    return [functools.partial(
        pltpu.make_async_remote_copy,
        src_ref=grads[a].at[:, pl.ds(1 - c, 1)], dst_ref=lands[a],
        send_sem=send_of(a), recv_sem=recv_of(a),
        device_id=(x, y, 1 - c), device_id_type=MESH) for a in range(len(grads))]


def _swap_start(grads, name):
    n = len(grads)

    def body(*refs):
        sems = refs[2 * n:4 * n]
        g_thru, l_thru = refs[4 * n:5 * n], refs[5 * n:6 * n]
        token = refs[6 * n]
        for cp in _swap_descriptors(g_thru, l_thru, lambda a: sems[a], lambda a: sems[n + a]):
            cp().start()
        token[...] = jnp.zeros_like(token)

    lands = [lax.empty((g.shape[0], 1) + g.shape[2:], g.dtype) for g in grads]
    held = [pltpu.with_memory_space_constraint(a, pltpu.HBM) for a in (*grads, *lands)]
    out = pl.pallas_call(
        body, name=name,
        out_shape=(*[pltpu.SemaphoreType.DMA(())] * (2 * n), *[pltpu.HBM(a.shape, a.dtype) for a in held],
                   jax.ShapeDtypeStruct((8, LANES), F32)),
        in_specs=[HBM_SPEC] * (2 * n),
        out_specs=(*[SEM_SPEC] * (2 * n), *[HBM_SPEC] * (2 * n), pl.BlockSpec(memory_space=pltpu.VMEM)),
        input_output_aliases={i: 2 * n + i for i in range(2 * n)},
        compiler_params=pltpu.CompilerParams(has_side_effects=pltpu.SideEffectType.DATAFLOW_SIDE_EFFECTING),
    )(*held)
    return list(out[:2 * n]), list(out[2 * n:3 * n]), list(out[3 * n:4 * n]), out[4 * n]


def _swap_wait(sems, grads, lands, after, name):
    n = len(grads)

    def body(*refs):
        g_ref, l_ref = refs[:n], refs[n:2 * n]
        sem_ref = refs[2 * n:4 * n]
        for cp in _swap_descriptors(g_ref, l_ref, lambda a: sem_ref[a], lambda a: sem_ref[n + a]):
            cp().wait()

    out = pl.pallas_call(
        body, name=name,
        out_shape=tuple(pltpu.HBM(a.shape, a.dtype) for a in (*grads, *lands)),
        in_specs=[HBM_SPEC] * (2 * n) + [SEM_SPEC] * (2 * n) + [ANY],
        out_specs=tuple([HBM_SPEC] * (2 * n)),
        input_output_aliases={i: i for i in range(2 * n)},
        compiler_params=pltpu.CompilerParams(has_side_effects=pltpu.SideEffectType.DATAFLOW_SIDE_EFFECTING),
    )(*grads, *lands, *sems, after)
    return list(out[:n]), list(out[n:])


def _add_half(g, got, pos, name):
    j, _, h, c = g.shape

    def body(pos_ref, g_ref, r_ref, o_ref, p_ref):
        val = (g_ref[0, 0] + r_ref[0, 0]).astype(BF16)
        o_ref[0] = val
        if j == 1:
            p_ref[0] = val
        else:
            @pl.when(pl.program_id(0) == pos_ref[0])
            def _():
                p_ref[0] = val

    return pl.pallas_call(
        body, name=name,
        grid_spec=pltpu.PrefetchScalarGridSpec(
            num_scalar_prefetch=1, grid=(j,),
            in_specs=[pl.BlockSpec((1, 1, h, c), lambda i, p: (i, p[1], 0, 0)),
                      pl.BlockSpec((1, 1, h, c), lambda i, p: (i, 0, 0, 0))],
            out_specs=[pl.BlockSpec((1, h, c), lambda i, p: (i, 0, 0)),
                       pl.BlockSpec((1, h, c), lambda i, p: (p[0], 0, 0))]),
        out_shape=[jax.ShapeDtypeStruct((j, h, c), BF16), jax.ShapeDtypeStruct((N_CHIPS, h, c), BF16)],
        compiler_params=_params(("arbitrary",)),
    )(pos, g, got)


def _exchange_chip_sums(sums, parts):
    n = len(sums)

    def body(*refs):
        ins, outs = refs[:n], refs[2 * n:3 * n]
        send_sem, recv_sem = refs[3 * n:]
        sends, arrivals = _exchange_descriptors(ins, outs, lambda a, k: send_sem.at[a, k],
                                                lambda a, k: recv_sem.at[a, k])
        for cp in sends:
            cp().start()
        for cp in arrivals:
            cp().wait_recv()
        for cp in sends:
            cp().wait_send()

    out_shape = [jax.ShapeDtypeStruct(p.shape, p.dtype) for p in parts]
    return pl.pallas_call(
        body, name="exchange_chip_sums",
        in_specs=[ANY] * (2 * n), out_specs=[ANY] * n, out_shape=out_shape,
        input_output_aliases={n + a: a for a in range(n)},
        scratch_shapes=[pltpu.SemaphoreType.DMA((n, 3))] * 2,
    )(*sums, *parts)


def _exchange_descriptors(sums, parts, send_of, recv_of):
    x, y, c = _mesh_pos()
    me = 2 * x + y
    chips = _other_chips(x, y)
    sends, arrivals = [], []
    for a in range(len(sums)):
        for k in range(3):
            ck = 2 * chips[k][0] + chips[k][1]
            mine = sums[a].at[ck] if sums[a].shape[0] == N_CHIPS else sums[a].at[0]

            def copy(dst_slot, a=a, k=k, mine=mine):
                return pltpu.make_async_remote_copy(
                    src_ref=mine, dst_ref=parts[a].at[dst_slot],
                    send_sem=send_of(a, k), recv_sem=recv_of(a, k),
                    device_id=(*chips[k], c), device_id_type=MESH)

            sends.append(functools.partial(copy, me))
            arrivals.append(functools.partial(copy, ck))
    return sends, arrivals


def _exchange_start(sums, parts, name):
    n = len(sums)
    ns = 3 * n

    def body(*refs):
        sems = refs[2 * n:2 * n + 2 * ns]
        sums_thru = refs[2 * n + 2 * ns:3 * n + 2 * ns]
        parts_thru = refs[3 * n + 2 * ns:4 * n + 2 * ns]
        token = refs[4 * n + 2 * ns]
        sends, _ = _exchange_descriptors(sums_thru, parts_thru, lambda a, k: sems[3 * a + k],
                                         lambda a, k: sems[ns + 3 * a + k])
        for cp in sends:
            cp().start()
        token[...] = jnp.zeros_like(token)

    hbm = lambda a: pltpu.HBM(a.shape, a.dtype)
    held = [pltpu.with_memory_space_constraint(a, pltpu.HBM) for a in (*sums, *parts)]
    out = pl.pallas_call(
        body, name=name,
        out_shape=(*[pltpu.SemaphoreType.DMA(())] * (2 * ns), *[hbm(a) for a in held],
                   jax.ShapeDtypeStruct((8, LANES), F32)),
        in_specs=[HBM_SPEC] * (2 * n),
        out_specs=(*[SEM_SPEC] * (2 * ns), *[HBM_SPEC] * (2 * n), pl.BlockSpec(memory_space=pltpu.VMEM)),
        input_output_aliases={i: 2 * ns + i for i in range(2 * n)},
        compiler_params=pltpu.CompilerParams(has_side_effects=pltpu.SideEffectType.DATAFLOW_SIDE_EFFECTING),
    )(*held)
    return (list(out[:2 * ns]), list(out[2 * ns:2 * ns + n]), list(out[2 * ns + n:2 * ns + 2 * n]),
            out[2 * ns + 2 * n])


def _exchange_wait(sems, sums, parts, after, name):
    n = len(sums)
    ns = 3 * n

    def body(*refs):
        sums_ref, parts_ref = refs[:n], refs[n:2 * n]
        sem_ref = refs[2 * n:2 * n + 2 * ns]
        sends, arrivals = _exchange_descriptors(sums_ref, parts_ref, lambda a, k: sem_ref[3 * a + k],
                                                lambda a, k: sem_ref[ns + 3 * a + k])
        for cp in sends:
            cp().wait_send()
        for cp in arrivals:
            cp().wait_recv()

    hbm = lambda a: pltpu.HBM(a.shape, a.dtype)
    out = pl.pallas_call(
        body, name=name,
        out_shape=tuple(hbm(a) for a in (*sums, *parts)),
        in_specs=[HBM_SPEC] * (2 * n) + [SEM_SPEC] * (2 * ns) + [ANY] * len(_as_tuple(after)),
        out_specs=tuple([HBM_SPEC] * (2 * n)),
        input_output_aliases={i: i for i in range(2 * n)},
        compiler_params=pltpu.CompilerParams(has_side_effects=pltpu.SideEffectType.DATAFLOW_SIDE_EFFECTING),
    )(*sums, *parts, *sems, *_as_tuple(after))
    return list(out[n:])


def _sum_chips(parts, pos, name):
    _, h, c = parts.shape

    def body(pos_ref, p_ref, o_ref):
        o_ref[0] = ((p_ref[0].astype(F32) + p_ref[1].astype(F32)) + p_ref[2].astype(F32)) + p_ref[3].astype(F32)

    return pl.pallas_call(
        body, name=name,
        grid_spec=pltpu.PrefetchScalarGridSpec(
            num_scalar_prefetch=1, grid=(1,),
            in_specs=[pl.BlockSpec((N_CHIPS, h, c), lambda i, p: (0, 0, 0))],
            out_specs=pl.BlockSpec((1, h, c), lambda i, p: (p[1], 0, 0))),
        out_shape=jax.ShapeDtypeStruct((2, h, c), F32),
        compiler_params=_params(("arbitrary",)),
    )(pos, parts)


def _join_halves(fulls, name, after=()):
    n = len(fulls)

    def body(*refs):
        outs = refs[n:2 * n]
        send_sem, recv_sem = refs[2 * n:]
        x, y, c = _mesh_pos()

        def half(a, which):
            return pltpu.make_async_remote_copy(
                src_ref=outs[a].at[which], dst_ref=outs[a].at[which],
                send_sem=send_sem.at[a], recv_sem=recv_sem.at[a],
                device_id=(x, y, 1 - c), device_id_type=MESH)

        sends = [half(a, c) for a in range(n)]
        for cp in sends:
            cp.start()
        for a in range(n):
            half(a, 1 - c).wait_recv()
        for cp in sends:
            cp.wait_send()

    out_shape = [jax.ShapeDtypeStruct(f.shape, f.dtype) for f in fulls]
    return _tied_call(
        body, after, name=name,
        in_specs=[ANY] * n, out_specs=[ANY] * n, out_shape=out_shape,
        input_output_aliases={a: a for a in range(n)},
        scratch_shapes=[pltpu.SemaphoreType.DMA((n,))] * 2,
    )(*fulls)


def _mix_in(x, g, w_in, b_in, ts, after=()):
    s = x.shape[0]

    def body(x_ref, g_ref, w_ref, b_ref, z_ref, hn_ref):
        xv = x_ref[...]
        hn = (xv * _rms_stats(xv) * g_ref[...]).astype(BF16)
        hn_ref[...] = hn
        for j in range(4):
            cols = slice(j * 512, (j + 1) * 512)
            z_ref[:, cols] = _dot(hn, w_ref[j]) + b_ref[:, cols]

    return _tied_call(
        body, after, name="mix_in", grid=(s // ts,),
        in_specs=[pl.BlockSpec((ts, D_MODEL), lambda i: (i, 0)),
                  pl.BlockSpec((1, D_MODEL), lambda i: (0, 0)),
                  pl.BlockSpec((4, D_MODEL, 512), lambda i: (0, 0, 0)),
                  pl.BlockSpec((1, 2048), lambda i: (0, 0))],
        out_specs=[pl.BlockSpec((ts, 2048), lambda i: (i, 0)),
                   pl.BlockSpec((ts, D_MODEL), lambda i: (i, 0))],
        out_shape=[jax.ShapeDtypeStruct((s, 2048), F32), jax.ShapeDtypeStruct((s, D_MODEL), BF16)],
        compiler_params=_params(("parallel",)),
    )(x, g, w_in, b_in)


def _lane_is_low_head():
    lane = lax.broadcasted_iota(jnp.int32, (1, GM_WIDTH), 1)
    return (lane & GM_HEAD_DIM) == 0


def _gm_mix(v_lo, v_hi, wpair_ref, bias_ref, mixed_ref, t):
    for n in range(t // CHUNK):
        rows = slice(n * CHUNK, (n + 1) * CHUNK)
        for j in range(GM_HEADS // 2):
            cols = slice(j * LANES, (j + 1) * LANES)
            rhs = jnp.concatenate([v_lo[rows, cols], v_hi[rows, cols]], axis=0)
            mixed_ref[rows, cols] = _dot(wpair_ref[j], rhs) + bias_ref[:, cols]


def _seqmix_fwd(z, cw, cb, lng, lnb, gg, gb, wpair, bias, t):
    s = z.shape[0]

    def body(z_ref, cw_ref, cb_ref, lng_ref, lnb_ref, gg_ref, gb_ref, wpair_ref, bias_ref,
             mix_ref, c1_ref, abuf, mixed_ref):
        i = pl.program_id(0)

        @pl.when(i == 0)
        def _():
            abuf[0:CONV_HALO, :] = jnp.zeros((CONV_HALO, CONV_WIDTH), F32)

        @pl.when(i > 0)
        def _():
            abuf[0:CONV_HALO, :] = abuf[t:t + CONV_HALO, :]

        abuf[CONV_HALO:, :] = z_ref[:, 0:512] * _sigmoid(z_ref[:, 512:1024])
        acc = jnp.zeros((t, CONV_WIDTH), F32)
        for k in range(CONV_KERNEL):
            acc = acc + cw_ref[k:k + 1, :] * abuf[pl.ds(CONV_HALO - (CONV_KERNEL - 1) + k, t), :]
        c1 = acc + cb_ref[...]
        c1_ref[...] = c1
        xh, _ = _ln_stats(c1)
        ln = xh * lng_ref[...] + lnb_ref[...]
        mix_ref[:, 0:512] = (ln * _sigmoid(ln)).astype(BF16)

        u, _ = _gelu_parts(z_ref[:, 1024:1536])
        gv, _ = _gelu_parts(z_ref[:, 1536:2048])
        vxh, _ = _ln_stats(gv)
        v = vxh * gg_ref[...] + gb_ref[...]
        low = _lane_is_low_head()
        v_lo = jnp.where(low, v, 0.0).astype(BF16)
        v_hi = jnp.where(low, 0.0, v).astype(BF16)
        _gm_mix(v_lo, v_hi, wpair_ref, bias_ref, mixed_ref, t)
        mix_ref[:, 512:1024] = (u * mixed_ref[...]).astype(BF16)

    vec = lambda n: pl.BlockSpec((1, n), lambda i: (0, 0))
    return pl.pallas_call(
        body, name="seqmix_fwd", grid=(s // t,),
        in_specs=[pl.BlockSpec((t, 2048), lambda i: (i, 0)),
                  pl.BlockSpec((CONV_HALO, CONV_WIDTH), lambda i: (0, 0)),
                  vec(512), vec(512), vec(512), vec(512), vec(512),
                  pl.BlockSpec((4, CHUNK, 2 * CHUNK), lambda i: (0, 0, 0)),
                  pl.BlockSpec((CHUNK, GM_WIDTH), lambda i: (0, 0))],
        out_specs=[pl.BlockSpec((t, D_MODEL), lambda i: (i, 0)),
                   pl.BlockSpec((t, CONV_WIDTH), lambda i: (i, 0))],
        out_shape=[jax.ShapeDtypeStruct((s, D_MODEL), BF16), jax.ShapeDtypeStruct((s, CONV_WIDTH), F32)],
        scratch_shapes=[pltpu.VMEM((t + CONV_HALO, CONV_WIDTH), F32), pltpu.VMEM((t, GM_WIDTH), F32)],
        compiler_params=_params(("arbitrary",)),
    )(z, cw, cb, lng, lnb, gg, gb, wpair, bias)


def _out_proj_q(x, mix, w_out, g, wq, ts, after=()):
    s = x.shape[0]

    def body(x_ref, mix_ref, wo_ref, g_ref, wq_ref, h1_ref, hn_ref, q_ref):
        h1 = x_ref[...] + _dot(mix_ref[...], wo_ref[...])
        h1_ref[...] = h1
        hn = (h1 * _rms_stats(h1) * g_ref[...]).astype(BF16)
        hn_ref[...] = hn
        q_ref[...] = _dot(hn, wq_ref[...]).astype(BF16)

    row = lambda dt: pl.BlockSpec((ts, D_MODEL), lambda i: (i, 0))
    full = pl.BlockSpec((D_MODEL, D_MODEL), lambda i: (0, 0))
    return _tied_call(
        body, after, name="out_proj_q", grid=(s // ts,),
        in_specs=[row(F32), row(BF16), full, pl.BlockSpec((1, D_MODEL), lambda i: (0, 0)), full],
        out_specs=[row(F32), row(BF16), row(BF16)],
        out_shape=[jax.ShapeDtypeStruct((s, D_MODEL), F32), jax.ShapeDtypeStruct((s, D_MODEL), BF16),
                   jax.ShapeDtypeStruct((s, D_MODEL), BF16)],
        compiler_params=_params(("parallel",)),
    )(x, mix, w_out, g, wq)


def _mem_kv(mem, g, wkv):
    m = mem.shape[0]

    def body(mem_ref, g_ref, w_ref, mn_ref, kv_ref):
        mv = mem_ref[...]
        mn = (mv * _rms_stats(mv) * g_ref[...]).astype(BF16)
        mn_ref[...] = mn
        for j in range(4):
            kv_ref[:, j * 512:(j + 1) * 512] = _dot(mn, w_ref[j]).astype(BF16)

    return pl.pallas_call(
        body, name="mem_kv",
        out_shape=[jax.ShapeDtypeStruct((m, D_MODEL), BF16), jax.ShapeDtypeStruct((m, 2 * D_MODEL), BF16)],
        compiler_params=pltpu.CompilerParams(vmem_limit_bytes=VMEM_LIMIT_BYTES),
    )(mem, g, wkv)


def _softmax_rows(sc):
    e = jnp.exp(sc - jnp.max(sc, axis=-1, keepdims=True))
    return e / jnp.sum(e, axis=-1, keepdims=True)


def _attn_fwd(q, kv, h1, wo, g, ts):
    s, m = q.shape[0], kv.shape[0]
    scale = XA_HEAD_DIM ** -0.5

    def body(q_ref, kv_ref, h1_ref, wo_ref, g_ref, o_ref, h2_ref, hn_ref):
        for h in range(XA_HEADS):
            cols = slice(h * XA_HEAD_DIM, (h + 1) * XA_HEAD_DIM)
            vcols = slice(D_MODEL + h * XA_HEAD_DIM, D_MODEL + (h + 1) * XA_HEAD_DIM)
            p = _softmax_rows(_dot_nt(q_ref[:, cols], kv_ref[:, cols]) * scale)
            o_ref[:, cols] = _dot(p.astype(BF16), kv_ref[:, vcols]).astype(BF16)
        h2 = h1_ref[...] + _dot(o_ref[...], wo_ref[...])
        h2_ref[...] = h2
        hn_ref[...] = (h2 * _rms_stats(h2) * g_ref[...]).astype(BF16)

    row = pl.BlockSpec((ts, D_MODEL), lambda i: (i, 0))
    return pl.pallas_call(
        body, name="attn_fwd", grid=(s // ts,),
        in_specs=[row, pl.BlockSpec((m, 2 * D_MODEL), lambda i: (0, 0)), row,
                  pl.BlockSpec((D_MODEL, D_MODEL), lambda i: (0, 0)),
                  pl.BlockSpec((1, D_MODEL), lambda i: (0, 0))],
        out_specs=[row, row, row],
        out_shape=[jax.ShapeDtypeStruct((s, D_MODEL), BF16), jax.ShapeDtypeStruct((s, D_MODEL), F32),
                   jax.ShapeDtypeStruct((s, D_MODEL), BF16)],
        compiler_params=_params(("parallel",)),
    )(q, kv, h1, wo, g)


def _ffn_up(hn, wgu, ts, after=()):
    s = hn.shape[0]

    def body(hn_ref, w_ref, gu_ref, act_ref):
        hv = hn_ref[...]
        gate = _dot(hv, w_ref[0, 0])
        up = _dot(hv, w_ref[1, 0])
        gu_ref[0] = gate
        gu_ref[1] = up
        act_ref[...] = (gate * _sigmoid(gate) * up).astype(BF16)

    return _tied_call(
        body, after, name="ffn_up", grid=(2, s // ts),
        in_specs=[pl.BlockSpec((ts, D_MODEL), lambda j, i: (i, 0)),
                  pl.BlockSpec((2, 1, D_MODEL, FFN_HALF), lambda j, i: (0, j, 0, 0))],
        out_specs=[pl.BlockSpec((2, ts, FFN_HALF), lambda j, i: (0, i, j)),
                   pl.BlockSpec((ts, FFN_HALF), lambda j, i: (i, j))],
        out_shape=[jax.ShapeDtypeStruct((2, s, FFN_HIDDEN), F32), jax.ShapeDtypeStruct((s, FFN_HIDDEN), BF16)],
        compiler_params=_params(("parallel", "parallel")),
    )(hn, wgu)


def _ffn_down_loss(act, wd, h2, g, target, ts):
    s = act.shape[0]

    def body(act_ref, wd_ref, h2_ref, g_ref, t_ref, dh_ref, dhb_ref, sq_ref, dg_ref):
        @pl.when(pl.program_id(0) == 0)
        def _():
            sq_ref[...] = jnp.zeros_like(sq_ref)
            dg_ref[...] = jnp.zeros_like(dg_ref)

        h3 = h2_ref[...] + _dot(act_ref[...], wd_ref[...])
        r = _rms_stats(h3)
        gv = g_ref[...]
        diff = h3 * r * gv - t_ref[...]
        sq_ref[...] += _rowsum(diff * diff)
        dh, dg = _rms_bwd(diff / D_MODEL, h3, r, gv)
        dh_ref[...] = dh
        dhb_ref[...] = dh.astype(BF16)
        dg_ref[...] += dg

    row = pl.BlockSpec((ts, D_MODEL), lambda i: (i, 0))
    vec = pl.BlockSpec((1, D_MODEL), lambda i: (0, 0))
    return pl.pallas_call(
        body, name="ffn_down_loss", grid=(s // ts,),
        in_specs=[pl.BlockSpec((ts, FFN_HIDDEN), lambda i: (i, 0)),
                  pl.BlockSpec((FFN_HIDDEN, D_MODEL), lambda i: (0, 0)), row, vec, row],
        out_specs=[row, row, vec, vec],
        out_shape=[jax.ShapeDtypeStruct((s, D_MODEL), F32), jax.ShapeDtypeStruct((s, D_MODEL), BF16),
                   jax.ShapeDtypeStruct((1, D_MODEL), F32), jax.ShapeDtypeStruct((1, D_MODEL), F32)],
        compiler_params=_params(("arbitrary",)),
    )(act, wd, h2, g, target)


def _grad_w(a, b, tk, tn, name, after=()):
    s, k = a.shape
    gb, _, n = b.shape
    nblk = n // tn
    tsr = GRAD_ROWS if s % GRAD_ROWS == 0 else s

    def body(a_ref, b_ref, o_ref):
        part = _dot_tn(a_ref[...], b_ref[0])

        @pl.when(pl.program_id(2) == 0)
        def _():
            o_ref[0] = part

        @pl.when(pl.program_id(2) > 0)
        def _():
            o_ref[0] += part

    return _tied_call(
        body, after, name=name, grid=(gb * nblk, k // tk, s // tsr),
        in_specs=[pl.BlockSpec((tsr, tk), lambda ni, ki, si: (si, ki)),
                  pl.BlockSpec((1, tsr, tn), lambda ni, ki, si: (ni // nblk, si, ni % nblk))],
        out_specs=pl.BlockSpec((1, tk, tn), lambda ni, ki, si: (ni, ki, 0)),
        out_shape=jax.ShapeDtypeStruct((gb * nblk, k, tn), F32),
        compiler_params=_params(("parallel", "parallel", "arbitrary")),
    )(a, b)


def _ffn_bwd_act(dh3, wd, gu, ts):
    s = dh3.shape[0]

    def body(dh_ref, wd_ref, gu_ref, dgu_ref):
        dact = _dot_nt(dh_ref[...], wd_ref[0])
        gate, up = gu_ref[0], gu_ref[1]
        sg = _sigmoid(gate)
        dgu_ref[0] = (dact * up * (sg * (1.0 + gate * (1.0 - sg)))).astype(BF16)
        dgu_ref[1] = (dact * (gate * sg)).astype(BF16)

    return pl.pallas_call(
        body, name="ffn_bwd_act", grid=(2, s // ts),
        in_specs=[pl.BlockSpec((ts, D_MODEL), lambda j, i: (i, 0)),
                  pl.BlockSpec((1, FFN_HALF, D_MODEL), lambda j, i: (j, 0, 0)),
                  pl.BlockSpec((2, ts, FFN_HALF), lambda j, i: (0, i, j))],
        out_specs=pl.BlockSpec((2, ts, FFN_HALF), lambda j, i: (0, i, j)),
        out_shape=jax.ShapeDtypeStruct((2, s, FFN_HIDDEN), BF16),
        compiler_params=_params(("parallel", "parallel")),
    )(dh3, wd, gu)


def _ffn_bwd_in(dgu, wgu, dh3, h2, g, ts, after=()):
    s = dh3.shape[0]

    def body(dgu_ref, w_ref, dh3_ref, h2_ref, g_ref, dh2_ref, dh2b_ref, dg_ref):
        @pl.when(pl.program_id(0) == 0)
        def _():
            dg_ref[...] = jnp.zeros_like(dg_ref)

        dhn = jnp.zeros((ts, D_MODEL), F32)
        for p in range(2):
            for j in range(2):
                dhn = dhn + _dot_nt(dgu_ref[p, :, j * FFN_HALF:(j + 1) * FFN_HALF], w_ref[2 * p + j])
        h2 = h2_ref[...]
        dv, dg = _rms_bwd(dhn, h2, _rms_stats(h2), g_ref[...])
        dh2 = dh3_ref[...] + dv
        dh2_ref[...] = dh2
        dh2b_ref[...] = dh2.astype(BF16)
        dg_ref[...] += dg

    row = pl.BlockSpec((ts, D_MODEL), lambda i: (i, 0))
    vec = pl.BlockSpec((1, D_MODEL), lambda i: (0, 0))
    return _tied_call(
        body, after, name="ffn_bwd_in", grid=(s // ts,),
        in_specs=[pl.BlockSpec((2, ts, FFN_HIDDEN), lambda i: (0, i, 0)),
                  pl.BlockSpec((4, D_MODEL, FFN_HALF), lambda i: (0, 0, 0)), row, row, vec],
        out_specs=[row, row, vec],
        out_shape=[jax.ShapeDtypeStruct((s, D_MODEL), F32), jax.ShapeDtypeStruct((s, D_MODEL), BF16),
                   jax.ShapeDtypeStruct((1, D_MODEL), F32)],
        compiler_params=_params(("arbitrary",)),
    )(dgu, wgu, dh3, h2, g)


def _attn_bwd(dh2, wo, q, kv, wq, h1, g, ts, after=()):
    s, m = q.shape[0], kv.shape[0]
    scale = XA_HEAD_DIM ** -0.5

    def body(dh2_ref, wo_ref, q_ref, kv_ref, wq_ref, h1_ref, g_ref, dh1_ref, dh1b_ref, dq_ref, dkv_ref, dg_ref):
        @pl.when(pl.program_id(0) == 0)
        def _():
            dkv_ref[...] = jnp.zeros_like(dkv_ref)
            dg_ref[...] = jnp.zeros_like(dg_ref)

        do = _dot_nt(dh2_ref[...].astype(BF16), wo_ref[...]).astype(BF16)
        for h in range(XA_HEADS):
            cols = slice(h * XA_HEAD_DIM, (h + 1) * XA_HEAD_DIM)
            vcols = slice(D_MODEL + h * XA_HEAD_DIM, D_MODEL + (h + 1) * XA_HEAD_DIM)
            qh, kh, vh, doh = q_ref[:, cols], kv_ref[:, cols], kv_ref[:, vcols], do[:, cols]
            p = _softmax_rows(_dot_nt(qh, kh) * scale)
            dp = _dot_nt(doh, vh)
            ds = (p * (dp - jnp.sum(dp * p, axis=-1, keepdims=True)) * scale).astype(BF16)
            dq_ref[:, cols] = _dot(ds, kh).astype(BF16)
            dkv_ref[:, cols] += _dot_tn(ds, qh)
            dkv_ref[:, vcols] += _dot_tn(p.astype(BF16), doh)
        dhn = _dot_nt(dq_ref[...], wq_ref[...])
        h1 = h1_ref[...]
        dv, dg = _rms_bwd(dhn, h1, _rms_stats(h1), g_ref[...])
        dh1 = dh2_ref[...] + dv
        dh1_ref[...] = dh1
        dh1b_ref[...] = dh1.astype(BF16)
        dg_ref[...] += dg

    row = pl.BlockSpec((ts, D_MODEL), lambda i: (i, 0))
    full = pl.BlockSpec((D_MODEL, D_MODEL), lambda i: (0, 0))
    kvs = pl.BlockSpec((m, 2 * D_MODEL), lambda i: (0, 0))
    vec = pl.BlockSpec((1, D_MODEL), lambda i: (0, 0))
    return _tied_call(
        body, after, name="attn_bwd", grid=(s // ts,),
        in_specs=[row, full, row, kvs, full, row, vec],
        out_specs=[row, row, row, kvs, vec],
        out_shape=[jax.ShapeDtypeStruct((s, D_MODEL), F32), jax.ShapeDtypeStruct((s, D_MODEL), BF16),
                   jax.ShapeDtypeStruct((s, D_MODEL), BF16),
                   jax.ShapeDtypeStruct((m, 2 * D_MODEL), F32), jax.ShapeDtypeStruct((1, D_MODEL), F32)],
        compiler_params=_params(("arbitrary",)),
    )(dh2, wo, q, kv, wq, h1, g)


def _mem_kv_bwd(dkv, mn, wkv, mem, g, after=()):
    m = mem.shape[0]

    def body(dkv_ref, mn_ref, w_ref, mem_ref, g_ref, dw_ref, dg_ref):
        dmn = jnp.zeros((m, D_MODEL), F32)
        mn = mn_ref[...]
        for j in range(4):
            dj = dkv_ref[:, j * 512:(j + 1) * 512].astype(BF16)
            dw_ref[j] = _dot_tn(mn, dj)
            dmn = dmn + _dot_nt(dj, w_ref[j])
        mv = mem_ref[...]
        dg_ref[...] = _rowsum(dmn * (mv * _rms_stats(mv)))

    return _tied_call(
        body, after, name="mem_kv_bwd", in_specs=[pl.BlockSpec(memory_space=pltpu.VMEM)] * 5,
        out_shape=[jax.ShapeDtypeStruct((4, D_MODEL, 512), F32), jax.ShapeDtypeStruct((1, D_MODEL), F32)],
        compiler_params=pltpu.CompilerParams(vmem_limit_bytes=VMEM_LIMIT_BYTES),
    )(dkv, mn, wkv, mem, g)


def _seqmix_bwd(dh1, x, z, c1, w_out, w_in, g_mix, cw, lng, lnb, gg, gb, wpair, wpair_t, bias, t, after=()):
    s = x.shape[0]
    nt = s // t
    halo_blocks = t // CONV_HALO

    def body(dh1_ref, x_ref, z_ref, zh_ref, c1_ref, wo_ref, wi_ref, gm_ref, cw_ref, lng_ref, lnb_ref,
             gg_ref, gb_ref, wpair_ref, wpt_ref, bias_ref,
             gx_ref, dz_ref, dcw_ref, dcb_ref, dlng_ref, dlnb_ref, dgg_ref, dgb_ref, dws_ref, dbs_ref,
             dbin_ref, dgm_ref, abuf, dbuf, mixed_ref, dv_ref):
        i = pl.program_id(0)
        tile = nt - 1 - i
        accs = (dcw_ref, dcb_ref, dlng_ref, dlnb_ref, dgg_ref, dgb_ref, dws_ref, dbs_ref, dbin_ref, dgm_ref)

        @pl.when(i == 0)
        def _():
            for r in accs:
                r[...] = jnp.zeros_like(r)
            dbuf[t:t + CONV_HALO, :] = jnp.zeros((CONV_HALO, CONV_WIDTH), F32)

        @pl.when(i > 0)
        def _():
            dbuf[t:t + CONV_HALO, :] = dbuf[0:CONV_HALO, :]

        dmix = _dot_nt(dh1_ref[...].astype(BF16), wo_ref[...])

        xh, rs = _ln_stats(c1_ref[...])
        lng = lng_ref[...]
        ln = xh * lng + lnb_ref[...]
        sl = _sigmoid(ln)
        dln = dmix[:, 0:512] * (sl * (1.0 + ln * (1.0 - sl)))
        dc1, dg_ln, db_ln = _ln_bwd(dln, xh, rs, lng)
        dlng_ref[...] += dg_ln
        dlnb_ref[...] += db_ln
        dcb_ref[...] += _rowsum(dc1)
        dbuf[0:t, :] = dc1

        zh = zh_ref[...]
        a_halo = zh[:, 0:512] * _sigmoid(zh[:, 512:1024])
        abuf[0:CONV_HALO, :] = jnp.where(tile > 0, a_halo, 0.0)
        za = z_ref[:, 0:512]
        sg = _sigmoid(z_ref[:, 512:1024])
        abuf[CONV_HALO:, :] = za * sg

        da = jnp.zeros((t, CONV_WIDTH), F32)
        for k in range(CONV_KERNEL):
            da = da + cw_ref[k:k + 1, :] * dbuf[pl.ds(CONV_KERNEL - 1 - k, t), :]
            dcw_ref[k:k + 1, :] += _rowsum(dc1 * abuf[pl.ds(CONV_HALO - (CONV_KERNEL - 1) + k, t), :])
        dza = da * sg
        dzg = da * za * (sg * (1.0 - sg))
        dz_ref[:, 0:512] = dza.astype(BF16)
        dz_ref[:, 512:1024] = dzg.astype(BF16)
        dbin_ref[:, 0:512] += _rowsum(dza)
        dbin_ref[:, 512:1024] += _rowsum(dzg)

        dgm = dmix[:, 512:1024]
        u, du_dz = _gelu_parts(z_ref[:, 1024:1536])
        gv, dgv_dz = _gelu_parts(z_ref[:, 1536:2048])
        vxh, vrs = _ln_stats(gv)
        ggv = gg_ref[...]
        v = vxh * ggv + gb_ref[...]
        low = _lane_is_low_head()
        v_lo = jnp.where(low, v, 0.0).astype(BF16)
        v_hi = jnp.where(low, 0.0, v).astype(BF16)
        _gm_mix(v_lo, v_hi, wpair_ref, bias_ref, mixed_ref, t)
        dzu = dgm * mixed_ref[...] * du_dz
        dm = dgm * u
        dm_lo = jnp.where(low, dm, 0.0).astype(BF16)
        dm_hi = jnp.where(low, 0.0, dm).astype(BF16)
        vb = v.astype(BF16)
        tril = (lax.broadcasted_iota(jnp.int32, (CHUNK, CHUNK), 1)
                <= lax.broadcasted_iota(jnp.int32, (CHUNK, CHUNK), 0))
        for n in range(t // CHUNK):
            rows = slice(n * CHUNK, (n + 1) * CHUNK)
            dbs_ref[...] += dm[rows, :]
            for j in range(GM_HEADS // 2):
                cols = slice(j * LANES, (j + 1) * LANES)
                stack = jnp.concatenate([dm_lo[rows, cols], dm_hi[rows, cols]], axis=0)
                dws = _dot_nt(stack, vb[rows, cols])
                dws_ref[2 * j] += jnp.where(tril, dws[0:CHUNK], 0.0)
                dws_ref[2 * j + 1] += jnp.where(tril, dws[CHUNK:2 * CHUNK], 0.0)
                dv_ref[rows, cols] = _dot(wpt_ref[j], stack)
        dgv, dg_gm, db_gm = _ln_bwd(dv_ref[...], vxh, vrs, ggv)
        dgg_ref[...] += dg_gm
        dgb_ref[...] += db_gm
        dzv = dgv * dgv_dz
        dz_ref[:, 1024:1536] = dzu.astype(BF16)
        dz_ref[:, 1536:2048] = dzv.astype(BF16)
        dbin_ref[:, 1024:1536] += _rowsum(dzu)
        dbin_ref[:, 1536:2048] += _rowsum(dzv)

        dhn = jnp.zeros((t, D_MODEL), F32)
        for j in range(4):
            dhn = dhn + _dot_nt(dz_ref[:, j * 512:(j + 1) * 512], wi_ref[j])
        xv = x_ref[...]
        dv, dg = _rms_bwd(dhn, xv, _rms_stats(xv), gm_ref[...])
        gx_ref[...] = dh1_ref[...] + dv
        dgm_ref[...] += dg

    rev = lambda w: pl.BlockSpec((t, w), lambda i: (nt - 1 - i, 0))
    const = lambda *shape: pl.BlockSpec(shape, lambda i: (0,) * len(shape))
    halo = pl.BlockSpec((CONV_HALO, D_MODEL), lambda i: (jnp.maximum((nt - 1 - i) * halo_blocks - 1, 0), 0))
    f32 = lambda *shape: jax.ShapeDtypeStruct(shape, F32)
    return _tied_call(
        body, after, name="seqmix_bwd", grid=(nt,),
        in_specs=[rev(D_MODEL), rev(D_MODEL), rev(2048), halo, rev(CONV_WIDTH),
                  const(D_MODEL, D_MODEL), const(4, D_MODEL, 512), const(1, D_MODEL),
                  const(CONV_HALO, CONV_WIDTH), const(1, 512), const(1, 512), const(1, 512), const(1, 512),
                  const(4, CHUNK, 2 * CHUNK), const(4, CHUNK, 2 * CHUNK), const(CHUNK, GM_WIDTH)],
        out_specs=[rev(D_MODEL), rev(2048),
                   const(CONV_HALO, CONV_WIDTH), const(1, 512), const(1, 512), const(1, 512), const(1, 512),
                   const(1, 512), const(GM_HEADS, CHUNK, CHUNK), const(CHUNK, GM_WIDTH), const(1, 2048),
                   const(1, D_MODEL)],
        out_shape=[f32(s, D_MODEL), jax.ShapeDtypeStruct((s, 2048), BF16),
                   f32(CONV_HALO, CONV_WIDTH), f32(1, 512), f32(1, 512), f32(1, 512), f32(1, 512),
                   f32(1, 512), f32(GM_HEADS, CHUNK, CHUNK), f32(CHUNK, GM_WIDTH), f32(1, 2048),
                   f32(1, D_MODEL)],
        scratch_shapes=[pltpu.VMEM((t + CONV_HALO, CONV_WIDTH), F32), pltpu.VMEM((t + CONV_HALO, CONV_WIDTH), F32),
                        pltpu.VMEM((t, GM_WIDTH), F32), pltpu.VMEM((t, GM_WIDTH), F32)],
        compiler_params=_params(("arbitrary",)),
    )(dh1, x, z, z, c1, w_out, w_in, g_mix, cw, lng, lnb, gg, gb, wpair, wpair_t, bias)


def _head_bias_grad(dbs):
    def body(d_ref, o_ref):
        dv = d_ref[...]
        lane = lax.broadcasted_iota(jnp.int32, (CHUNK, LANES), 1)
        acc = jnp.zeros((CHUNK, LANES), F32)
        for h in range(GM_HEADS):
            sh = jnp.sum(dv[:, h * GM_HEAD_DIM:(h + 1) * GM_HEAD_DIM], axis=-1, keepdims=True)
            acc = acc + jnp.where(lane == h, sh, 0.0)
        o_ref[...] = acc

    return pl.pallas_call(body, name="head_bias_grad",
                          out_shape=jax.ShapeDtypeStruct((CHUNK, LANES), F32))(dbs)


def kernel(x, mem, norm_mix_g, w_in, b_in, conv_w, conv_b, conv_ln_g, conv_ln_b, gm_ln_g, gm_ln_b, gm_w_s, gm_b_s, w_out, norm_xa_g, mem_norm_g, xa_wq, xa_wkv, xa_wo, norm_ffn_g, ffn_w_gate_up, ffn_w_down, final_norm_g, loss_target, m_norm_mix_g, m_w_in, m_b_in, m_conv_w, m_conv_b, m_conv_ln_g, m_conv_ln_b, m_gm_ln_g, m_gm_ln_b, m_gm_w_s, m_gm_b_s, m_w_out, m_norm_xa_g, m_mem_norm_g, m_xa_wq, m_xa_wkv, m_xa_wo, m_norm_ffn_g, m_ffn_w_gate_up, m_ffn_w_down, m_final_norm_g, v_norm_mix_g, v_w_in, v_b_in, v_conv_w, v_conv_b, v_conv_ln_g, v_conv_ln_b, v_gm_ln_g, v_gm_ln_b, v_gm_w_s, v_gm_b_s, v_w_out, v_norm_xa_g, v_mem_norm_g, v_xa_wq, v_xa_wkv, v_xa_wo, v_norm_ffn_g, v_ffn_w_gate_up, v_ffn_w_down, v_final_norm_g):
    weights = dict(norm_mix_g=norm_mix_g, w_in=w_in, b_in=b_in, conv_w=conv_w, conv_b=conv_b, conv_ln_g=conv_ln_g,
                   conv_ln_b=conv_ln_b, gm_ln_g=gm_ln_g, gm_ln_b=gm_ln_b, gm_w_s=gm_w_s, gm_b_s=gm_b_s, w_out=w_out,
                   norm_xa_g=norm_xa_g, mem_norm_g=mem_norm_g, xa_wq=xa_wq, xa_wkv=xa_wkv, xa_wo=xa_wo,
                   norm_ffn_g=norm_ffn_g, ffn_w_gate_up=ffn_w_gate_up, ffn_w_down=ffn_w_down,
                   final_norm_g=final_norm_g)
    m_in = dict(norm_mix_g=m_norm_mix_g, w_in=m_w_in, b_in=m_b_in, conv_w=m_conv_w, conv_b=m_conv_b,
                conv_ln_g=m_conv_ln_g, conv_ln_b=m_conv_ln_b, gm_ln_g=m_gm_ln_g, gm_ln_b=m_gm_ln_b, gm_w_s=m_gm_w_s,
                gm_b_s=m_gm_b_s, w_out=m_w_out, norm_xa_g=m_norm_xa_g, mem_norm_g=m_mem_norm_g, xa_wq=m_xa_wq,
                xa_wkv=m_xa_wkv, xa_wo=m_xa_wo, norm_ffn_g=m_norm_ffn_g, ffn_w_gate_up=m_ffn_w_gate_up,
                ffn_w_down=m_ffn_w_down, final_norm_g=m_final_norm_g)
    v_in = dict(norm_mix_g=v_norm_mix_g, w_in=v_w_in, b_in=v_b_in, conv_w=v_conv_w, conv_b=v_conv_b,
                conv_ln_g=v_conv_ln_g, conv_ln_b=v_conv_ln_b, gm_ln_g=v_gm_ln_g, gm_ln_b=v_gm_ln_b, gm_w_s=v_gm_w_s,
                gm_b_s=v_gm_b_s, w_out=v_w_out, norm_xa_g=v_norm_xa_g, mem_norm_g=v_mem_norm_g, xa_wq=v_xa_wq,
                xa_wkv=v_xa_wkv, xa_wo=v_xa_wo, norm_ffn_g=v_norm_ffn_g, ffn_w_gate_up=v_ffn_w_gate_up,
                ffn_w_down=v_ffn_w_down, final_norm_g=v_final_norm_g)
    grads, delta, new_m, new_v = {}, {}, {}, {}

    s = x.shape[1]
    ts = _row_tile(s)
    tb = max(CHUNK, ts // 2)
    cx, cy, cc = _mesh_pos()
    chip = 2 * cx + cy
    pos = jnp.stack([chip, cc]).astype(jnp.int32)
    row = lambda a: a.reshape(1, -1)
    x2, mem2, tgt2 = x[0], mem[0], loss_target[0]

    big = dict(w_in=w_in, xa_wkv=xa_wkv, w_out=w_out, xa_wq=xa_wq, xa_wo=xa_wo,
               ffn_w_gate_up=ffn_w_gate_up, ffn_w_down=ffn_w_down)
    big_names = list(big)
    halves = lambda a: a.reshape(2, a.shape[0] // 2, a.shape[1])
    cast = {nm: _cast_into_slot(halves(big[nm]), pos, BF16, "cast_" + nm) for nm in big_names}
    conv_w_pad = jnp.pad(conv_w, ((0, CONV_HALO - CONV_KERNEL), (0, 0)))
    first = _gather_shards([cast["w_in"], _cast_into_slot(halves(conv_w_pad), pos, F32, "slot_conv_w")])
    shaped = lambda buf, nm: buf.reshape(N_CHIPS, big[nm].shape[0], big[nm].shape[1])

    def start_gather(names, after):
        return _gather_start([cast[nm] for nm in names], "gather_start_" + names[0], after)

    def finish_gather(names, started, after):
        send_sems, recv_sems, bufs, _ = started
        landed = _gather_wait(send_sems, recv_sems, bufs, after, "gather_wait_" + names[0])
        return {nm: shaped(b, nm) for nm, b in zip(names, _pass_to_sibling(landed, "pass_" + names[0]))}

    attn_names = ["w_out", "xa_wq", "xa_wkv", "xa_wo"]
    gather_attn = start_gather(attn_names, ())
    w_in_g = shaped(first[0], "w_in")
    cw_g = jnp.concatenate([first[1][k].reshape(CONV_HALO, LANES) for k in range(N_CHIPS)], axis=1)

    tril = jnp.tril(jnp.ones((CHUNK, CHUNK), dtype=bool))
    ws = jnp.where(tril[None], gm_w_s, 0.0)
    wpair = jnp.concatenate([ws[0::2], ws[1::2]], axis=2).astype(BF16)
    ws_t = jnp.swapaxes(ws, 1, 2)
    wpair_t = jnp.concatenate([ws_t[0::2], ws_t[1::2]], axis=2).astype(BF16)
    bias = jnp.repeat(gm_b_s.T, GM_HEAD_DIM, axis=1)

    z, hn1 = _mix_in(x2, row(norm_mix_g), w_in_g, row(b_in), ts, after=gather_attn[3])
    mix, c1 = _seqmix_fwd(z, cw_g, row(conv_b), row(conv_ln_g), row(conv_ln_b), row(gm_ln_g), row(gm_ln_b),
                          wpair, bias, ts)
    gw = finish_gather(attn_names, gather_attn, mix)
    w_out_g = gw["w_out"].reshape(D_MODEL, D_MODEL)
    wq_g = gw["xa_wq"].reshape(D_MODEL, D_MODEL)
    wkv_g = gw["xa_wkv"]
    wo_g = gw["xa_wo"].reshape(D_MODEL, D_MODEL)
    gather_gu = start_gather(["ffn_w_gate_up"], w_out_g)
    h1, hn2, q = _out_proj_q(x2, mix, w_out_g, row(norm_xa_g), wq_g, ts, after=gather_gu[3])
    mn, kv = _mem_kv(mem2, row(mem_norm_g), wkv_g)
    o, h2, hn3 = _attn_fwd(q, kv, h1, wo_g, row(norm_ffn_g), ts)
    wgu_g = finish_gather(["ffn_w_gate_up"], gather_gu, hn3)["ffn_w_gate_up"]
    gather_down = start_gather(["ffn_w_down"], wgu_g)
    gu, act = _ffn_up(hn3, wgu_g.reshape(2, 2, D_MODEL, FFN_HALF), ts, after=gather_down[3])
    wd_g = finish_gather(["ffn_w_down"], gather_down, act)["ffn_w_down"].reshape(FFN_HIDDEN, D_MODEL)
    dh3, dh3_b, sq, d_final_g = _ffn_down_loss(act, wd_g, h2, row(final_norm_g), tgt2, ts)
    loss = lax.psum(0.5 * jnp.sum(sq) / D_MODEL, ("x", "y", "c"))

    def split(g, nm):
        r, c = big[nm].shape
        return g.reshape(N_CHIPS, 2, r // 2, c)

    def chip_sums_of(group, arrays):
        got = _swap_halves(arrays, "swap_halves_" + group[0])
        both = [_add_half(g, r, pos, "chip_sum_" + nm) for g, r, nm in zip(arrays, got, group)]
        return [b[0] for b in both], [b[1] for b in both]

    def start_swap(group, grads):
        return _swap_start([split(g, nm) for g, nm in zip(grads, group)], "swap_start_" + group[0])

    def start_exchange(group, swapping, after):
        sems, arrays, lands, _ = swapping
        arrays, got = _swap_wait(sems, arrays, lands, after, "swap_wait_" + group[0])
        both = [_add_half(g, r, pos, "chip_sum_" + nm) for g, r, nm in zip(arrays, got, group)]
        return _exchange_start([b[0] for b in both], [b[1] for b in both], "exchange_start_" + group[0])

    def finish_exchange(group, started, after):
        sems, sums, parts, _ = started
        parts = _exchange_wait(sems, sums, parts, after, "exchange_wait_" + group[0])
        return [_sum_chips(p, pos, "total_" + nm) for p, nm in zip(parts, group)]

    def join_and_update(group, after):
        joined = _join_halves([halves_of[nm] for nm in group], "join_halves_" + group[0], after)
        for nm, j in zip(group, joined):
            grads[nm] = j.reshape(big[nm].shape)
            delta[nm], new_m[nm], new_v[nm] = _adamw(weights[nm], grads[nm], m_in[nm], v_in[nm], "adamw_" + nm)
        return [new_v[nm] for nm in group]

    as3 = lambda a: a.reshape((1,) + a.shape)
    halves_of = {}

    dgu = _ffn_bwd_act(dh3_b, wd_g.reshape(2, FFN_HALF, D_MODEL), gu, ts)
    g_down = _grad_w(act, as3(dh3_b), FFN_HALF, D_MODEL, "grad_ffn_w_down")
    group_a = ["ffn_w_down"]
    swap_a = start_swap(group_a, [g_down])
    dh2, dh2_b, d_ffn_g = _ffn_bwd_in(dgu, wgu_g, dh3, h2, row(norm_ffn_g), tb, after=swap_a[3])
    exch_a = start_exchange(group_a, swap_a, dh2)
    g_gu = _grad_w(hn3, dgu, D_MODEL, FFN_HALF, "grad_ffn_w_gate_up", after=exch_a[3])
    halves_of.update(zip(group_a, finish_exchange(group_a, exch_a, g_gu)))

    group_b = ["ffn_w_gate_up"]
    swap_b = start_swap(group_b, [g_gu])
    dh1, dh1_b, dq, dkv, d_xa_g = _attn_bwd(dh2, wo_g, q, kv, wq_g, h1, row(norm_xa_g), ts, after=swap_b[3])
    exch_b = start_exchange(group_b, swap_b, dh1)
    g_wkv, d_mem_g = _mem_kv_bwd(dkv, mn, wkv_g, mem2, row(mem_norm_g), after=exch_b[3])
    g_wo = _grad_w(o, as3(dh2_b), D_MODEL, D_MODEL, "grad_xa_wo", after=exch_b[3])
    g_wq = _grad_w(hn2, as3(dq), D_MODEL, D_MODEL, "grad_xa_wq", after=exch_b[3])
    g_wout = _grad_w(mix, as3(dh1_b), D_MODEL, D_MODEL, "grad_w_out", after=exch_b[3])
    halves_of.update(zip(group_b, finish_exchange(group_b, exch_b, (g_wkv, g_wo, g_wq, g_wout))))

    group_c = ["xa_wo", "xa_wq", "xa_wkv", "w_out"]
    swap_c = start_swap(group_c, [g_wo, g_wq, g_wkv, g_wout])
    (gx, dz, d_cw, d_cb, d_lng, d_lnb, d_gg, d_gb, d_ws, d_bs_sum, d_bin, d_mix_g) = _seqmix_bwd(
        dh1, x2, z, c1, w_out_g, w_in_g, row(norm_mix_g), cw_g, row(conv_ln_g), row(conv_ln_b),
        row(gm_ln_g), row(gm_ln_b), wpair, wpair_t, bias, tb, after=swap_c[3])
    d_bs = _head_bias_grad(d_bs_sum)[:, :GM_HEADS].T
    exch_c = start_exchange(group_c, swap_c, dz)
    g_win = _grad_w(hn1, as3(dz), D_MODEL, 512, "grad_w_in", after=exch_c[3])
    done_ab = join_and_update(group_a + group_b, g_win)
    halves_of.update(zip(group_c, finish_exchange(group_c, exch_c, (g_win, *done_ab))))

    small_names = ["norm_mix_g", "b_in", "conv_w", "conv_b", "conv_ln_g", "conv_ln_b", "gm_ln_g", "gm_ln_b",
                   "gm_w_s", "gm_b_s", "norm_xa_g", "mem_norm_g", "norm_ffn_g", "final_norm_g"]
    d_cw_by_chip = jnp.swapaxes(d_cw.reshape(CONV_HALO, N_CHIPS, LANES), 0, 1).reshape(-1, LANES)
    small_grads = dict(norm_mix_g=d_mix_g, b_in=d_bin, conv_w=d_cw_by_chip, conv_b=d_cb, conv_ln_g=d_lng,
                       conv_ln_b=d_lnb, gm_ln_g=d_gg, gm_ln_b=d_gb, gm_w_s=d_ws, gm_b_s=d_bs, norm_xa_g=d_xa_g,
                       mem_norm_g=d_mem_g, norm_ffn_g=d_ffn_g, final_norm_g=d_final_g)

    def rows_form(a):
        a = a.reshape(-1, LANES)
        return jnp.pad(a, ((0, -a.shape[0] % SUBLANES), (0, 0)))

    pieces = [rows_form(small_grads[nm]) for nm in small_names]
    offsets, total = [], 0
    for p in pieces:
        offsets.append(total)
        total += p.shape[0]
    pack_rows = -(-total // 32) * 32
    small_pack = jnp.pad(jnp.concatenate(pieces, axis=0), ((0, pack_rows - total), (0, 0)))

    group_d = ["w_in", "small"]
    sums_d, parts_d = chip_sums_of(group_d, [split(g_win, "w_in"),
                                             small_pack.reshape(1, 2, pack_rows // 2, LANES)])
    exch_d = _exchange_start(sums_d, parts_d, "exchange_start_w_in")
    done_c = join_and_update(group_c, exch_d[3])
    halves_of.update(zip(group_d, finish_exchange(group_d, exch_d, done_c)))
    joined_d = _join_halves([halves_of[nm] for nm in group_d], "join_halves_w_in")
    grads["w_in"] = joined_d[0].reshape(w_in.shape)
    delta["w_in"], new_m["w_in"], new_v["w_in"] = _adamw(w_in, grads["w_in"], m_w_in, v_w_in, "adamw_w_in")

    local_rows = lambda a, nm: a if nm == "conv_w" else a.reshape(-1, LANES)
    params = [tuple(local_rows(src[nm], nm) for src in (weights, m_in, v_in)) for nm in small_names]
    outs = _adamw_small(joined_d[1].reshape(pack_rows, LANES), pos, params, offsets, small_names.index("conv_w"))
    for k, nm in enumerate(small_names):
        for dst, a in zip((grads, delta, new_m, new_v), outs[4 * k:4 * k + 4]):
            dst[nm] = a

    order = ["norm_mix_g", "w_in", "b_in", "conv_w", "conv_b", "conv_ln_g", "conv_ln_b", "gm_ln_g", "gm_ln_b",
             "gm_w_s", "gm_b_s", "w_out", "norm_xa_g", "mem_norm_g", "xa_wq", "xa_wkv", "xa_wo", "norm_ffn_g",
             "ffn_w_gate_up", "ffn_w_down", "final_norm_g"]
    fit = lambda a, nm: a.reshape(weights[nm].shape)
    return (loss, gx.reshape(x.shape),
            *[fit(grads[nm], nm) for nm in order], *[fit(delta[nm], nm) for nm in order],
            *[fit(new_m[nm], nm) for nm in order], *[fit(new_v[nm], nm) for nm in order])
```

```python
import functools

import jax
import jax.numpy as jnp
from jax import lax
from jax.experimental import pallas as pl
from jax.experimental.pallas import tpu as pltpu

F32 = jnp.float32
BF16 = jnp.bfloat16

D_MODEL = 1024
CONV_WIDTH = 512
GM_WIDTH = 512
CONV_KERNEL = 31
CONV_HALO = 32
GRAD_ROWS = 2048
CHUNK = 128
GM_HEADS = 8
GM_HEAD_DIM = 64
XA_HEADS = 4
XA_HEAD_DIM = 256
FFN_HIDDEN = 2816
FFN_HALF = FFN_HIDDEN // 2
RMS_EPS = 1e-6
LN_EPS = 1e-5
N_CHIPS = 4
LANES = 128
SUBLANES = 8

ADAM_LR = 0.001
ADAM_B1 = 0.9
ADAM_B2 = 0.999
ADAM_EPS = 1e-08
ADAM_WD = 0.01
ADAM_STEP = 10

VMEM_LIMIT_BYTES = 56 * 1024 * 1024
MESH = pl.DeviceIdType.MESH
ANY = pl.BlockSpec(memory_space=pl.ANY)
HBM_SPEC = pl.BlockSpec(memory_space=pltpu.HBM)
SEM_SPEC = pl.BlockSpec(memory_space=pltpu.SEMAPHORE)

_NT = (((1,), (1,)), ((), ()))
_TN = (((0,), (0,)), ((), ()))
_GELU_C = 0.7978845608028654
_GELU_A = 0.044715


def _dot(a, b):
    return jnp.dot(a, b, preferred_element_type=F32)


def _dot_nt(a, b):
    return lax.dot_general(a, b, _NT, preferred_element_type=F32)


def _dot_tn(a, b):
    return lax.dot_general(a, b, _TN, preferred_element_type=F32)


def _mean(v):
    return jnp.mean(v, axis=-1, keepdims=True)


def _rowsum(v):
    return jnp.sum(v, axis=0, keepdims=True)


def _sigmoid(v):
    return 1.0 / (1.0 + jnp.exp(-v))


def _gelu_parts(v):
    v2 = v * v
    t = jnp.tanh(_GELU_C * (v + _GELU_A * v * v2))
    g = 0.5 * v * (1.0 + t)
    dg = 0.5 * (1.0 + t) + 0.5 * v * (1.0 - t * t) * (_GELU_C * (1.0 + 3.0 * _GELU_A * v2))
    return g, dg


def _rms_stats(v):
    return lax.rsqrt(_mean(v * v) + RMS_EPS)


def _rms_bwd(dy, v, r, g):
    n = v * r
    dn = dy * g
    dv = r * (dn - n * _mean(dn * n))
    return dv, _rowsum(dy * n)


def _ln_stats(v):
    mu = _mean(v)
    xc = v - mu
    rs = lax.rsqrt(_mean(xc * xc) + LN_EPS)
    return xc * rs, rs


def _ln_bwd(dy, xh, rs, g):
    dxh = dy * g
    dv = rs * (dxh - _mean(dxh) - xh * _mean(dxh * xh))
    return dv, _rowsum(dy * xh), _rowsum(dy)


def _params(sem):
    return pltpu.CompilerParams(dimension_semantics=sem, vmem_limit_bytes=VMEM_LIMIT_BYTES)


def _row_tile(s):
    return 512 if s % 512 == 0 and s >= 2048 else 128


def _mesh_pos():
    return lax.axis_index("x"), lax.axis_index("y"), lax.axis_index("c")


def _cast_into_slot(w, pos, dtype, name):
    _, h, c = w.shape

    def body(pos_ref, w_ref, o_ref):
        o_ref[0] = w_ref[...].astype(dtype)

    return pl.pallas_call(
        body, name=name,
        grid_spec=pltpu.PrefetchScalarGridSpec(
            num_scalar_prefetch=1, grid=(2,),
            in_specs=[pl.BlockSpec((1, h, c), lambda i, p: (i, 0, 0))],
            out_specs=pl.BlockSpec((1, 1, h, c), lambda i, p: (p[0], i, 0, 0))),
        out_shape=jax.ShapeDtypeStruct((N_CHIPS, 2, h, c), dtype),
        compiler_params=_params(("parallel",)),
    )(pos, w)


def _adam_rows(r, c):
    for tr in (r, 1024, 704, 640, 512, 352, 320, 256, 128, 64, 32, 16, 8):
        if r % tr == 0 and tr * c * 4 <= (3 << 19):
            return tr
    return r


def _adam_update(w, g, m, v):
    nm = ADAM_B1 * m + (1.0 - ADAM_B1) * g
    nv = ADAM_B2 * v + (1.0 - ADAM_B2) * (g * g)
    m_hat = nm / (1.0 - ADAM_B1 ** ADAM_STEP)
    v_hat = nv / (1.0 - ADAM_B2 ** ADAM_STEP)
    return -ADAM_LR * (m_hat / (jnp.sqrt(v_hat) + ADAM_EPS) + ADAM_WD * w), nm, nv


def _adamw(w, g, m, v, name, after=()):
    r, c = w.shape
    tr = _adam_rows(r, c)

    def body(w_ref, g_ref, m_ref, v_ref, d_ref, nm_ref, nv_ref):
        d_ref[...], nm_ref[...], nv_ref[...] = _adam_update(w_ref[...], g_ref[...], m_ref[...], v_ref[...])

    spec = pl.BlockSpec((tr, c), lambda i: (i, 0))
    shp = jax.ShapeDtypeStruct((r, c), F32)
    return _tied_call(
        body, after, name=name, grid=(r // tr,),
        in_specs=[spec] * 4, out_specs=[spec] * 3, out_shape=[shp] * 3,
        compiler_params=_params(("parallel",)),
    )(w, g, m, v)


def _adamw_small(gpack, pos, params, offsets, conv_at):
    n = len(params)

    def body(pos_ref, g_ref, *refs):
        ins, outs = refs[:3 * n], refs[3 * n:]
        for k in range(n):
            rows = params[k][0].shape[0]
            start = offsets[k]
            if k == conv_at:
                start = pl.multiple_of(start + pos_ref[0] * CONV_HALO, SUBLANES)
            g = g_ref[pl.ds(start, rows), :]
            outs[4 * k][...] = g
            outs[4 * k + 1][...], outs[4 * k + 2][...], outs[4 * k + 3][...] = _adam_update(
                ins[3 * k][...], g, ins[3 * k + 1][...], ins[3 * k + 2][...])

    flat = [a for p in params for a in p]
    vmem = pl.BlockSpec(memory_space=pltpu.VMEM)
    return pl.pallas_call(
        body, name="adamw_small",
        in_specs=[pl.BlockSpec(memory_space=pltpu.SMEM), vmem] + [vmem] * len(flat),
        out_specs=[vmem] * (4 * n),
        out_shape=[jax.ShapeDtypeStruct(p[0].shape, F32) for p in params for _ in range(4)],
    )(pos, gpack, *flat)


def _as_tuple(after):
    return tuple(after) if isinstance(after, (tuple, list)) else (after,)


def _tied_call(body, after, *, in_specs, **kwargs):
    after = _as_tuple(after)
    n_in, n_after = len(in_specs), len(after)

    def tied(*refs):
        body(*refs[:n_in], *refs[n_in + n_after:])

    call = pl.pallas_call(tied, in_specs=list(in_specs) + [ANY] * n_after, **kwargs)
    return lambda *operands: call(*operands, *after)


def _other_chips(x, y):
    return [(1 - x, y), (x, 1 - y), (1 - x, 1 - y)]


def _gather_descriptors(bufs, send_of, recv_of):
    x, y, c = _mesh_pos()
    me = 2 * x + y
    chips = _other_chips(x, y)
    sends, arrivals = [], []
    for a in range(len(bufs)):
        for k in range(3):
            ck = 2 * chips[k][0] + chips[k][1]

            def copy(slot, a=a, k=k):
                return pltpu.make_async_remote_copy(
                    src_ref=bufs[a].at[slot, c], dst_ref=bufs[a].at[slot, c],
                    send_sem=send_of(a, k), recv_sem=recv_of(a, k),
                    device_id=(*chips[k], c), device_id_type=MESH)

            sends.append(functools.partial(copy, me))
            arrivals.append(functools.partial(copy, ck))
    return sends, arrivals


def _gather_start(bufs, name, after=()):
    n = len(bufs)
    ns = 3 * n

    def body(*refs):
        sems = refs[n:n + 2 * ns]
        thru = refs[n + 2 * ns:2 * n + 2 * ns]
        token = refs[2 * n + 2 * ns]
        sends, _ = _gather_descriptors(thru, lambda a, k: sems[3 * a + k], lambda a, k: sems[ns + 3 * a + k])
        for cp in sends:
            cp().start()
        token[...] = jnp.zeros_like(token)

    held = [pltpu.with_memory_space_constraint(b, pltpu.HBM) for b in bufs]
    out = _tied_call(
        body, after, name=name,
        out_shape=(*[pltpu.SemaphoreType.DMA(())] * (2 * ns), *[pltpu.HBM(b.shape, b.dtype) for b in held],
                   jax.ShapeDtypeStruct((8, LANES), F32)),
        in_specs=[HBM_SPEC] * n,
        out_specs=(*[SEM_SPEC] * (2 * ns), *[HBM_SPEC] * n, pl.BlockSpec(memory_space=pltpu.VMEM)),
        input_output_aliases={i: 2 * ns + i for i in range(n)},
        compiler_params=pltpu.CompilerParams(has_side_effects=pltpu.SideEffectType.DATAFLOW_SIDE_EFFECTING),
    )(*held)
    return list(out[:ns]), list(out[ns:2 * ns]), list(out[2 * ns:2 * ns + n]), out[2 * ns + n]


def _gather_wait(send_sems, recv_sems, bufs, after, name):
    n = len(bufs)
    ns = 3 * n

    def body(*refs):
        buf_ref = refs[:n]
        sem_ref = refs[n:n + 2 * ns]
        sends, arrivals = _gather_descriptors(buf_ref, lambda a, k: sem_ref[3 * a + k],
                                              lambda a, k: sem_ref[ns + 3 * a + k])
        for cp in sends:
            cp().wait_send()
        for cp in arrivals:
            cp().wait_recv()

    out = pl.pallas_call(
        body, name=name,
        out_shape=tuple(pltpu.HBM(b.shape, b.dtype) for b in bufs),
        in_specs=[HBM_SPEC] * n + [SEM_SPEC] * (2 * ns) + [ANY] * len(_as_tuple(after)),
        out_specs=tuple([HBM_SPEC] * n),
        input_output_aliases={i: i for i in range(n)},
        compiler_params=pltpu.CompilerParams(has_side_effects=pltpu.SideEffectType.DATAFLOW_SIDE_EFFECTING),
    )(*bufs, *send_sems, *recv_sems, *_as_tuple(after))
    return list(out)


def _pass_to_sibling(bufs, name):
    n = len(bufs)

    def body(*refs):
        outs = refs[n:2 * n]
        send_sem, recv_sem = refs[2 * n:]
        x, y, c = _mesh_pos()
        chips = _other_chips(x, y)

        def half(a, k, which):
            ck = 2 * chips[k][0] + chips[k][1]
            return pltpu.make_async_remote_copy(
                src_ref=outs[a].at[ck, which], dst_ref=outs[a].at[ck, which],
                send_sem=send_sem.at[a, k], recv_sem=recv_sem.at[a, k],
                device_id=(x, y, 1 - c), device_id_type=MESH)

        sends = [half(a, k, c) for a in range(n) for k in range(3)]
        for cp in sends:
            cp.start()
        for a in range(n):
            for k in range(3):
                half(a, k, 1 - c).wait_recv()
        for cp in sends:
            cp.wait_send()

    return pl.pallas_call(
        body, name=name,
        in_specs=[ANY] * n, out_specs=[ANY] * n,
        out_shape=[jax.ShapeDtypeStruct(b.shape, b.dtype) for b in bufs],
        input_output_aliases={a: a for a in range(n)},
        scratch_shapes=[pltpu.SemaphoreType.DMA((n, 3))] * 2,
    )(*bufs)


def _swap_halves(grads, name):
    n = len(grads)

    def body(*refs):
        ins, outs = refs[:n], refs[n:2 * n]
        send_sem, recv_sem = refs[2 * n:]
        x, y, c = _mesh_pos()
        cps = [pltpu.make_async_remote_copy(
            src_ref=ins[a].at[:, pl.ds(1 - c, 1)], dst_ref=outs[a],
            send_sem=send_sem.at[a], recv_sem=recv_sem.at[a],
            device_id=(x, y, 1 - c), device_id_type=MESH) for a in range(n)]
        for cp in cps:
            cp.start()
        for cp in cps:
            cp.wait()

    out_shape = [jax.ShapeDtypeStruct((g.shape[0], 1) + g.shape[2:], g.dtype) for g in grads]
    return pl.pallas_call(
        body, name=name,
        in_specs=[ANY] * n, out_specs=[ANY] * n, out_shape=out_shape,
        scratch_shapes=[pltpu.SemaphoreType.DMA((n,))] * 2,
    )(*grads)


def _swap_descriptors(grads, lands, send_of, recv_of):
    x, y, c = _mesh_pos()
    return [functools.partial(
        pltpu.make_async_remote_copy,
        src_ref=grads[a].at[:, pl.ds(1 - c, 1)], dst_ref=lands[a],
        send_sem=send_of(a), recv_sem=recv_of(a),
        device_id=(x, y, 1 - c), device_id_type=MESH) for a in range(len(grads))]


def _swap_start(grads, name):
    n = len(grads)

    def body(*refs):
        sems = refs[2 * n:4 * n]
        g_thru, l_thru = refs[4 * n:5 * n], refs[5 * n:6 * n]
        token = refs[6 * n]
        for cp in _swap_descriptors(g_thru, l_thru, lambda a: sems[a], lambda a: sems[n + a]):
            cp().start()
        token[...] = jnp.zeros_like(token)

    lands = [lax.empty((g.shape[0], 1) + g.shape[2:], g.dtype) for g in grads]
    held = [pltpu.with_memory_space_constraint(a, pltpu.HBM) for a in (*grads, *lands)]
    out = pl.pallas_call(
        body, name=name,
        out_shape=(*[pltpu.SemaphoreType.DMA(())] * (2 * n), *[pltpu.HBM(a.shape, a.dtype) for a in held],
                   jax.ShapeDtypeStruct((8, LANES), F32)),
        in_specs=[HBM_SPEC] * (2 * n),
        out_specs=(*[SEM_SPEC] * (2 * n), *[HBM_SPEC] * (2 * n), pl.BlockSpec(memory_space=pltpu.VMEM)),
        input_output_aliases={i: 2 * n + i for i in range(2 * n)},
        compiler_params=pltpu.CompilerParams(has_side_effects=pltpu.SideEffectType.DATAFLOW_SIDE_EFFECTING),
    )(*held)
    return list(out[:2 * n]), list(out[2 * n:3 * n]), list(out[3 * n:4 * n]), out[4 * n]


def _swap_wait(sems, grads, lands, after, name):
    n = len(grads)

    def body(*refs):
        g_ref, l_ref = refs[:n], refs[n:2 * n]
        sem_ref = refs[2 * n:4 * n]
        for cp in _swap_descriptors(g_ref, l_ref, lambda a: sem_ref[a], lambda a: sem_ref[n + a]):
            cp().wait()

    out = pl.pallas_call(
        body, name=name,
        out_shape=tuple(pltpu.HBM(a.shape, a.dtype) for a in (*grads, *lands)),
        in_specs=[HBM_SPEC] * (2 * n) + [SEM_SPEC] * (2 * n) + [ANY] * len(_as_tuple(after)),
        out_specs=tuple([HBM_SPEC] * (2 * n)),
        input_output_aliases={i: i for i in range(2 * n)},
        compiler_params=pltpu.CompilerParams(has_side_effects=pltpu.SideEffectType.DATAFLOW_SIDE_EFFECTING),
    )(*grads, *lands, *sems, *_as_tuple(after))
    return list(out[:n]), list(out[n:])


def _add_half(g, got, pos, name, dtype=BF16):
    j, _, h, c = g.shape

    def body(pos_ref, g_ref, r_ref, o_ref, p_ref):
        val = (g_ref[0, 0] + r_ref[0, 0]).astype(dtype)
        o_ref[0] = val
        if j == 1:
            p_ref[0] = val
        else:
            @pl.when(pl.program_id(0) == pos_ref[0])
            def _():
                p_ref[0] = val

    return pl.pallas_call(
        body, name=name,
        grid_spec=pltpu.PrefetchScalarGridSpec(
            num_scalar_prefetch=1, grid=(j,),
            in_specs=[pl.BlockSpec((1, 1, h, c), lambda i, p: (i, p[1], 0, 0)),
                      pl.BlockSpec((1, 1, h, c), lambda i, p: (i, 0, 0, 0))],
            out_specs=[pl.BlockSpec((1, h, c), lambda i, p: (i, 0, 0)),
                       pl.BlockSpec((1, h, c), lambda i, p: (p[0], 0, 0))]),
        out_shape=[jax.ShapeDtypeStruct((j, h, c), dtype), jax.ShapeDtypeStruct((N_CHIPS, h, c), dtype)],
        compiler_params=_params(("arbitrary",)),
    )(pos, g, got)


def _exchange_descriptors(sums, parts, send_of, recv_of):
    x, y, c = _mesh_pos()
    me = 2 * x + y
    chips = _other_chips(x, y)
    sends, arrivals = [], []
    for a in range(len(sums)):
        for k in range(3):
            ck = 2 * chips[k][0] + chips[k][1]
            mine = sums[a].at[ck] if sums[a].shape[0] == N_CHIPS else sums[a].at[0]

            def copy(dst_slot, a=a, k=k, mine=mine):
                return pltpu.make_async_remote_copy(
                    src_ref=mine, dst_ref=parts[a].at[dst_slot],
                    send_sem=send_of(a, k), recv_sem=recv_of(a, k),
                    device_id=(*chips[k], c), device_id_type=MESH)

            sends.append(functools.partial(copy, me))
            arrivals.append(functools.partial(copy, ck))
    return sends, arrivals


def _exchange_start(sums, parts, name):
    n = len(sums)
    ns = 3 * n

    def body(*refs):
        sems = refs[2 * n:2 * n + 2 * ns]
        sums_thru = refs[2 * n + 2 * ns:3 * n + 2 * ns]
        parts_thru = refs[3 * n + 2 * ns:4 * n + 2 * ns]
        token = refs[4 * n + 2 * ns]
        sends, _ = _exchange_descriptors(sums_thru, parts_thru, lambda a, k: sems[3 * a + k],
                                         lambda a, k: sems[ns + 3 * a + k])
        for cp in sends:
            cp().start()
        token[...] = jnp.zeros_like(token)

    hbm = lambda a: pltpu.HBM(a.shape, a.dtype)
    held = [pltpu.with_memory_space_constraint(a, pltpu.HBM) for a in (*sums, *parts)]
    out = pl.pallas_call(
        body, name=name,
        out_shape=(*[pltpu.SemaphoreType.DMA(())] * (2 * ns), *[hbm(a) for a in held],
                   jax.ShapeDtypeStruct((8, LANES), F32)),
        in_specs=[HBM_SPEC] * (2 * n),
        out_specs=(*[SEM_SPEC] * (2 * ns), *[HBM_SPEC] * (2 * n), pl.BlockSpec(memory_space=pltpu.VMEM)),
        input_output_aliases={i: 2 * ns + i for i in range(2 * n)},
        compiler_params=pltpu.CompilerParams(has_side_effects=pltpu.SideEffectType.DATAFLOW_SIDE_EFFECTING),
    )(*held)
    return (list(out[:2 * ns]), list(out[2 * ns:2 * ns + n]), list(out[2 * ns + n:2 * ns + 2 * n]),
            out[2 * ns + 2 * n])


def _exchange_wait(sems, sums, parts, after, name):
    n = len(sums)
    ns = 3 * n

    def body(*refs):
        sums_ref, parts_ref = refs[:n], refs[n:2 * n]
        sem_ref = refs[2 * n:2 * n + 2 * ns]
        sends, arrivals = _exchange_descriptors(sums_ref, parts_ref, lambda a, k: sem_ref[3 * a + k],
                                                lambda a, k: sem_ref[ns + 3 * a + k])
        for cp in sends:
            cp().wait_send()
        for cp in arrivals:
            cp().wait_recv()

    hbm = lambda a: pltpu.HBM(a.shape, a.dtype)
    out = pl.pallas_call(
        body, name=name,
        out_shape=tuple(hbm(a) for a in (*sums, *parts)),
        in_specs=[HBM_SPEC] * (2 * n) + [SEM_SPEC] * (2 * ns) + [ANY] * len(_as_tuple(after)),
        out_specs=tuple([HBM_SPEC] * (2 * n)),
        input_output_aliases={i: i for i in range(2 * n)},
        compiler_params=pltpu.CompilerParams(has_side_effects=pltpu.SideEffectType.DATAFLOW_SIDE_EFFECTING),
    )(*sums, *parts, *sems, *_as_tuple(after))
    return list(out[n:])


def _sum_chips(parts, pos, name):
    _, h, c = parts.shape

    def body(pos_ref, p_ref, o_ref):
        o_ref[0] = ((p_ref[0].astype(F32) + p_ref[1].astype(F32)) + p_ref[2].astype(F32)) + p_ref[3].astype(F32)

    return pl.pallas_call(
        body, name=name,
        grid_spec=pltpu.PrefetchScalarGridSpec(
            num_scalar_prefetch=1, grid=(1,),
            in_specs=[pl.BlockSpec((N_CHIPS, h, c), lambda i, p: (0, 0, 0))],
            out_specs=pl.BlockSpec((1, h, c), lambda i, p: (p[1], 0, 0))),
        out_shape=jax.ShapeDtypeStruct((2, h, c), F32),
        compiler_params=_params(("arbitrary",)),
    )(pos, parts)


def _join_halves(fulls, name, after=()):
    n = len(fulls)

    def body(*refs):
        outs = refs[n:2 * n]
        send_sem, recv_sem = refs[2 * n:]
        x, y, c = _mesh_pos()

        def half(a, which):
            return pltpu.make_async_remote_copy(
                src_ref=outs[a].at[which], dst_ref=outs[a].at[which],
                send_sem=send_sem.at[a], recv_sem=recv_sem.at[a],
                device_id=(x, y, 1 - c), device_id_type=MESH)

        sends = [half(a, c) for a in range(n)]
        for cp in sends:
            cp.start()
        for a in range(n):
            half(a, 1 - c).wait_recv()
        for cp in sends:
            cp.wait_send()

    out_shape = [jax.ShapeDtypeStruct(f.shape, f.dtype) for f in fulls]
    return _tied_call(
        body, after, name=name,
        in_specs=[ANY] * n, out_specs=[ANY] * n, out_shape=out_shape,
        input_output_aliases={a: a for a in range(n)},
        scratch_shapes=[pltpu.SemaphoreType.DMA((n,))] * 2,
    )(*fulls)


def _mix_in(x, g, w_in, b_in, ts, after=()):
    s = x.shape[0]

    def body(x_ref, g_ref, w_ref, b_ref, z_ref, hn_ref):
        xv = x_ref[...]
        hn = (xv * _rms_stats(xv) * g_ref[...]).astype(BF16)
        hn_ref[...] = hn
        for j in range(4):
            cols = slice(j * 512, (j + 1) * 512)
            z_ref[:, cols] = _dot(hn, w_ref[j]) + b_ref[:, cols]

    return _tied_call(
        body, after, name="mix_in", grid=(s // ts,),
        in_specs=[pl.BlockSpec((ts, D_MODEL), lambda i: (i, 0)),
                  pl.BlockSpec((1, D_MODEL), lambda i: (0, 0)),
                  pl.BlockSpec((4, D_MODEL, 512), lambda i: (0, 0, 0)),
                  pl.BlockSpec((1, 2048), lambda i: (0, 0))],
        out_specs=[pl.BlockSpec((ts, 2048), lambda i: (i, 0)),
                   pl.BlockSpec((ts, D_MODEL), lambda i: (i, 0))],
        out_shape=[jax.ShapeDtypeStruct((s, 2048), F32), jax.ShapeDtypeStruct((s, D_MODEL), BF16)],
        compiler_params=_params(("parallel",)),
    )(x, g, w_in, b_in)


def _shift_rows(buf, shifted, t):
    rows = t + CONV_HALO - SUBLANES
    for r in range(1, SUBLANES):
        shifted[r - 1, 0:rows, :] = buf[pl.ds(r, rows), :]


def _window(buf, shifted, offset, t):
    r = offset % SUBLANES
    if r == 0:
        return buf[pl.ds(offset, t), :]
    return shifted[r - 1, pl.ds(offset - r, t), :]


def _lane_is_low_head():
    lane = lax.broadcasted_iota(jnp.int32, (1, GM_WIDTH), 1)
    return (lane & GM_HEAD_DIM) == 0


def _gm_mix(v_lo, v_hi, wpair_ref, bias_ref, mixed_ref, t):
    for n in range(t // CHUNK):
        rows = slice(n * CHUNK, (n + 1) * CHUNK)
        for j in range(GM_HEADS // 2):
            cols = slice(j * LANES, (j + 1) * LANES)
            rhs = jnp.concatenate([v_lo[rows, cols], v_hi[rows, cols]], axis=0)
            mixed_ref[rows, cols] = _dot(wpair_ref[j], rhs) + bias_ref[:, cols]


def _seqmix_fwd(z, cw, cb, lng, lnb, gg, gb, wpair, bias, t):
    s = z.shape[0]

    def body(z_ref, cw_ref, cb_ref, lng_ref, lnb_ref, gg_ref, gb_ref, wpair_ref, bias_ref,
             mix_ref, c1_ref, abuf, ash, mixed_ref):
        i = pl.program_id(0)

        @pl.when(i == 0)
        def _():
            abuf[0:CONV_HALO, :] = jnp.zeros((CONV_HALO, CONV_WIDTH), F32)

        @pl.when(i > 0)
        def _():
            abuf[0:CONV_HALO, :] = abuf[t:t + CONV_HALO, :]

        abuf[CONV_HALO:, :] = z_ref[:, 0:512] * _sigmoid(z_ref[:, 512:1024])
        _shift_rows(abuf, ash, t)
        acc = jnp.zeros((t, CONV_WIDTH), F32)
        for k in range(CONV_KERNEL):
            acc = acc + cw_ref[k:k + 1, :] * _window(abuf, ash, CONV_HALO - (CONV_KERNEL - 1) + k, t)
        c1 = acc + cb_ref[...]
        c1_ref[...] = c1
        xh, _ = _ln_stats(c1)
        ln = xh * lng_ref[...] + lnb_ref[...]
        mix_ref[:, 0:512] = (ln * _sigmoid(ln)).astype(BF16)

        u, _ = _gelu_parts(z_ref[:, 1024:1536])
        gv, _ = _gelu_parts(z_ref[:, 1536:2048])
        vxh, _ = _ln_stats(gv)
        v = vxh * gg_ref[...] + gb_ref[...]
        low = _lane_is_low_head()
        v_lo = jnp.where(low, v, 0.0).astype(BF16)
        v_hi = jnp.where(low, 0.0, v).astype(BF16)
        _gm_mix(v_lo, v_hi, wpair_ref, bias_ref, mixed_ref, t)
        mix_ref[:, 512:1024] = (u * mixed_ref[...]).astype(BF16)

    vec = lambda n: pl.BlockSpec((1, n), lambda i: (0, 0))
    return pl.pallas_call(
        body, name="seqmix_fwd", grid=(s // t,),
        in_specs=[pl.BlockSpec((t, 2048), lambda i: (i, 0)),
                  pl.BlockSpec((CONV_HALO, CONV_WIDTH), lambda i: (0, 0)),
                  vec(512), vec(512), vec(512), vec(512), vec(512),
                  pl.BlockSpec((4, CHUNK, 2 * CHUNK), lambda i: (0, 0, 0)),
                  pl.BlockSpec((CHUNK, GM_WIDTH), lambda i: (0, 0))],
        out_specs=[pl.BlockSpec((t, D_MODEL), lambda i: (i, 0)),
                   pl.BlockSpec((t, CONV_WIDTH), lambda i: (i, 0))],
        out_shape=[jax.ShapeDtypeStruct((s, D_MODEL), BF16), jax.ShapeDtypeStruct((s, CONV_WIDTH), F32)],
        scratch_shapes=[pltpu.VMEM((t + CONV_HALO, CONV_WIDTH), F32),
                        pltpu.VMEM((SUBLANES - 1, t + CONV_HALO - SUBLANES, CONV_WIDTH), F32),
                        pltpu.VMEM((t, GM_WIDTH), F32)],
        compiler_params=_params(("arbitrary",)),
    )(z, cw, cb, lng, lnb, gg, gb, wpair, bias)


def _out_proj_q(x, mix, w_out, g, wq, ts, after=()):
    s = x.shape[0]

    def body(x_ref, mix_ref, wo_ref, g_ref, wq_ref, h1_ref, hn_ref, q_ref):
        h1 = x_ref[...] + _dot(mix_ref[...], wo_ref[...])
        h1_ref[...] = h1
        hn = (h1 * _rms_stats(h1) * g_ref[...]).astype(BF16)
        hn_ref[...] = hn
        q_ref[...] = _dot(hn, wq_ref[...]).astype(BF16)

    row = lambda dt: pl.BlockSpec((ts, D_MODEL), lambda i: (i, 0))
    full = pl.BlockSpec((D_MODEL, D_MODEL), lambda i: (0, 0))
    return _tied_call(
        body, after, name="out_proj_q", grid=(s // ts,),
        in_specs=[row(F32), row(BF16), full, pl.BlockSpec((1, D_MODEL), lambda i: (0, 0)), full],
        out_specs=[row(F32), row(BF16), row(BF16)],
        out_shape=[jax.ShapeDtypeStruct((s, D_MODEL), F32), jax.ShapeDtypeStruct((s, D_MODEL), BF16),
                   jax.ShapeDtypeStruct((s, D_MODEL), BF16)],
        compiler_params=_params(("parallel",)),
    )(x, mix, w_out, g, wq)


def _mem_kv(mem, g, wkv):
    m = mem.shape[0]

    def body(mem_ref, g_ref, w_ref, mn_ref, kv_ref):
        mv = mem_ref[...]
        mn = (mv * _rms_stats(mv) * g_ref[...]).astype(BF16)
        mn_ref[...] = mn
        for j in range(4):
            kv_ref[:, j * 512:(j + 1) * 512] = _dot(mn, w_ref[j]).astype(BF16)

    return pl.pallas_call(
        body, name="mem_kv",
        out_shape=[jax.ShapeDtypeStruct((m, D_MODEL), BF16), jax.ShapeDtypeStruct((m, 2 * D_MODEL), BF16)],
        compiler_params=pltpu.CompilerParams(vmem_limit_bytes=VMEM_LIMIT_BYTES),
    )(mem, g, wkv)


def _softmax_rows(sc):
    e = jnp.exp(sc - jnp.max(sc, axis=-1, keepdims=True))
    return e / jnp.sum(e, axis=-1, keepdims=True)


def _attn_fwd(q, kv, h1, wo, g, ts):
    s, m = q.shape[0], kv.shape[0]
    scale = XA_HEAD_DIM ** -0.5

    def body(q_ref, kv_ref, h1_ref, wo_ref, g_ref, o_ref, h2_ref, hn_ref):
        for h in range(XA_HEADS):
            cols = slice(h * XA_HEAD_DIM, (h + 1) * XA_HEAD_DIM)
            vcols = slice(D_MODEL + h * XA_HEAD_DIM, D_MODEL + (h + 1) * XA_HEAD_DIM)
            p = _softmax_rows(_dot_nt(q_ref[:, cols], kv_ref[:, cols]) * scale)
            o_ref[:, cols] = _dot(p.astype(BF16), kv_ref[:, vcols]).astype(BF16)
        h2 = h1_ref[...] + _dot(o_ref[...], wo_ref[...])
        h2_ref[...] = h2
        hn_ref[...] = (h2 * _rms_stats(h2) * g_ref[...]).astype(BF16)

    row = pl.BlockSpec((ts, D_MODEL), lambda i: (i, 0))
    return pl.pallas_call(
        body, name="attn_fwd", grid=(s // ts,),
        in_specs=[row, pl.BlockSpec((m, 2 * D_MODEL), lambda i: (0, 0)), row,
                  pl.BlockSpec((D_MODEL, D_MODEL), lambda i: (0, 0)),
                  pl.BlockSpec((1, D_MODEL), lambda i: (0, 0))],
        out_specs=[row, row, row],
        out_shape=[jax.ShapeDtypeStruct((s, D_MODEL), BF16), jax.ShapeDtypeStruct((s, D_MODEL), F32),
                   jax.ShapeDtypeStruct((s, D_MODEL), BF16)],
        compiler_params=_params(("parallel",)),
    )(q, kv, h1, wo, g)


def _ffn_up(hn, wgu, ts, after=()):
    s = hn.shape[0]

    def body(hn_ref, w_ref, gu_ref, act_ref):
        hv = hn_ref[...]
        gate = _dot(hv, w_ref[0, 0])
        up = _dot(hv, w_ref[1, 0])
        gu_ref[0] = gate.astype(BF16)
        gu_ref[1] = up.astype(BF16)
        act_ref[...] = (gate * _sigmoid(gate) * up).astype(BF16)

    return _tied_call(
        body, after, name="ffn_up", grid=(2, s // ts),
        in_specs=[pl.BlockSpec((ts, D_MODEL), lambda j, i: (i, 0)),
                  pl.BlockSpec((2, 1, D_MODEL, FFN_HALF), lambda j, i: (0, j, 0, 0))],
        out_specs=[pl.BlockSpec((2, ts, FFN_HALF), lambda j, i: (0, i, j)),
                   pl.BlockSpec((ts, FFN_HALF), lambda j, i: (i, j))],
        out_shape=[jax.ShapeDtypeStruct((2, s, FFN_HIDDEN), BF16), jax.ShapeDtypeStruct((s, FFN_HIDDEN), BF16)],
        compiler_params=_params(("parallel", "parallel")),
    )(hn, wgu)


def _ffn_down_loss(act, wd, h2, g, target, ts):
    s = act.shape[0]

    def body(act_ref, wd_ref, h2_ref, g_ref, t_ref, dh_ref, dhb_ref, sq_ref, dg_ref):
        @pl.when(pl.program_id(0) == 0)
        def _():
            sq_ref[...] = jnp.zeros_like(sq_ref)
            dg_ref[...] = jnp.zeros_like(dg_ref)

        h3 = h2_ref[...] + _dot(act_ref[...], wd_ref[...])
        r = _rms_stats(h3)
        gv = g_ref[...]
        diff = h3 * r * gv - t_ref[...]
        sq_ref[...] += _rowsum(diff * diff)
        dh, dg = _rms_bwd(diff / D_MODEL, h3, r, gv)
        dh_ref[...] = dh
        dhb_ref[...] = dh.astype(BF16)
        dg_ref[...] += dg

    row = pl.BlockSpec((ts, D_MODEL), lambda i: (i, 0))
    vec = pl.BlockSpec((1, D_MODEL), lambda i: (0, 0))
    return pl.pallas_call(
        body, name="ffn_down_loss", grid=(s // ts,),
        in_specs=[pl.BlockSpec((ts, FFN_HIDDEN), lambda i: (i, 0)),
                  pl.BlockSpec((FFN_HIDDEN, D_MODEL), lambda i: (0, 0)), row, vec, row],
        out_specs=[row, row, vec, vec],
        out_shape=[jax.ShapeDtypeStruct((s, D_MODEL), F32), jax.ShapeDtypeStruct((s, D_MODEL), BF16),
                   jax.ShapeDtypeStruct((1, D_MODEL), F32), jax.ShapeDtypeStruct((1, D_MODEL), F32)],
        compiler_params=_params(("arbitrary",)),
    )(act, wd, h2, g, target)


def _grad_w(a, b, tk, tn, name, after=()):
    s, k = a.shape
    gb, _, n = b.shape
    nblk = n // tn
    tsr = GRAD_ROWS if s % GRAD_ROWS == 0 else s

    def body(a_ref, b_ref, o_ref):
        part = _dot_tn(a_ref[...], b_ref[0])

        @pl.when(pl.program_id(2) == 0)
        def _():
            o_ref[0] = part

        @pl.when(pl.program_id(2) > 0)
        def _():
            o_ref[0] += part

    return _tied_call(
        body, after, name=name, grid=(gb * nblk, k // tk, s // tsr),
        in_specs=[pl.BlockSpec((tsr, tk), lambda ni, ki, si: (si, ki)),
                  pl.BlockSpec((1, tsr, tn), lambda ni, ki, si: (ni // nblk, si, ni % nblk))],
        out_specs=pl.BlockSpec((1, tk, tn), lambda ni, ki, si: (ni, ki, 0)),
        out_shape=jax.ShapeDtypeStruct((gb * nblk, k, tn), F32),
        compiler_params=_params(("parallel", "parallel", "arbitrary")),
    )(a, b)


def _ffn_bwd_act(dh3, wd, gu, ts):
    s = dh3.shape[0]

    def body(dh_ref, wd_ref, gu_ref, dgu_ref):
        dact = _dot_nt(dh_ref[...], wd_ref[0])
        gate, up = gu_ref[0].astype(F32), gu_ref[1].astype(F32)
        sg = _sigmoid(gate)
        dgu_ref[0] = (dact * up * (sg * (1.0 + gate * (1.0 - sg)))).astype(BF16)
        dgu_ref[1] = (dact * (gate * sg)).astype(BF16)

    return pl.pallas_call(
        body, name="ffn_bwd_act", grid=(2, s // ts),
        in_specs=[pl.BlockSpec((ts, D_MODEL), lambda j, i: (i, 0)),
                  pl.BlockSpec((1, FFN_HALF, D_MODEL), lambda j, i: (j, 0, 0)),
                  pl.BlockSpec((2, ts, FFN_HALF), lambda j, i: (0, i, j))],
        out_specs=pl.BlockSpec((2, ts, FFN_HALF), lambda j, i: (0, i, j)),
        out_shape=jax.ShapeDtypeStruct((2, s, FFN_HIDDEN), BF16),
        compiler_params=_params(("parallel", "parallel")),
    )(dh3, wd, gu)


def _ffn_bwd_in(dgu, wgu, dh3, h2, g, ts, after=()):
    s = dh3.shape[0]

    def body(dgu_ref, w_ref, dh3_ref, h2_ref, g_ref, dh2_ref, dh2b_ref, dg_ref):
        @pl.when(pl.program_id(0) == 0)
        def _():
            dg_ref[...] = jnp.zeros_like(dg_ref)

        dhn = jnp.zeros((ts, D_MODEL), F32)
        for p in range(2):
            for j in range(2):
                dhn = dhn + _dot_nt(dgu_ref[p, :, j * FFN_HALF:(j + 1) * FFN_HALF], w_ref[2 * p + j])
        h2 = h2_ref[...]
        dv, dg = _rms_bwd(dhn, h2, _rms_stats(h2), g_ref[...])
        dh2 = dh3_ref[...] + dv
        dh2_ref[...] = dh2
        dh2b_ref[...] = dh2.astype(BF16)
        dg_ref[...] += dg

    row = pl.BlockSpec((ts, D_MODEL), lambda i: (i, 0))
    vec = pl.BlockSpec((1, D_MODEL), lambda i: (0, 0))
    return _tied_call(
        body, after, name="ffn_bwd_in", grid=(s // ts,),
        in_specs=[pl.BlockSpec((2, ts, FFN_HIDDEN), lambda i: (0, i, 0)),
                  pl.BlockSpec((4, D_MODEL, FFN_HALF), lambda i: (0, 0, 0)), row, row, vec],
        out_specs=[row, row, vec],
        out_shape=[jax.ShapeDtypeStruct((s, D_MODEL), F32), jax.ShapeDtypeStruct((s, D_MODEL), BF16),
                   jax.ShapeDtypeStruct((1, D_MODEL), F32)],
        compiler_params=_params(("arbitrary",)),
    )(dgu, wgu, dh3, h2, g)


def _attn_bwd(dh2, wo, q, kv, wq, h1, g, ts, after=()):
    s, m = q.shape[0], kv.shape[0]
    scale = XA_HEAD_DIM ** -0.5

    def body(dh2_ref, wo_ref, q_ref, kv_ref, wq_ref, h1_ref, g_ref, dh1_ref, dh1b_ref, dq_ref, dkv_ref, dg_ref):
        @pl.when(pl.program_id(0) == 0)
        def _():
            dkv_ref[...] = jnp.zeros_like(dkv_ref)
            dg_ref[...] = jnp.zeros_like(dg_ref)

        do = _dot_nt(dh2_ref[...].astype(BF16), wo_ref[...]).astype(BF16)
        for h in range(XA_HEADS):
            cols = slice(h * XA_HEAD_DIM, (h + 1) * XA_HEAD_DIM)
            vcols = slice(D_MODEL + h * XA_HEAD_DIM, D_MODEL + (h + 1) * XA_HEAD_DIM)
            qh, kh, vh, doh = q_ref[:, cols], kv_ref[:, cols], kv_ref[:, vcols], do[:, cols]
            p = _softmax_rows(_dot_nt(qh, kh) * scale)
            dp = _dot_nt(doh, vh)
            ds = (p * (dp - jnp.sum(dp * p, axis=-1, keepdims=True)) * scale).astype(BF16)
            dq_ref[:, cols] = _dot(ds, kh).astype(BF16)
            dkv_ref[:, cols] += _dot_tn(ds, qh)
            dkv_ref[:, vcols] += _dot_tn(p.astype(BF16), doh)
        dhn = _dot_nt(dq_ref[...], wq_ref[...])
        h1 = h1_ref[...]
        dv, dg = _rms_bwd(dhn, h1, _rms_stats(h1), g_ref[...])
        dh1 = dh2_ref[...] + dv
        dh1_ref[...] = dh1
        dh1b_ref[...] = dh1.astype(BF16)
        dg_ref[...] += dg

    row = pl.BlockSpec((ts, D_MODEL), lambda i: (i, 0))
    full = pl.BlockSpec((D_MODEL, D_MODEL), lambda i: (0, 0))
    kvs = pl.BlockSpec((m, 2 * D_MODEL), lambda i: (0, 0))
    vec = pl.BlockSpec((1, D_MODEL), lambda i: (0, 0))
    return _tied_call(
        body, after, name="attn_bwd", grid=(s // ts,),
        in_specs=[row, full, row, kvs, full, row, vec],
        out_specs=[row, row, row, kvs, vec],
        out_shape=[jax.ShapeDtypeStruct((s, D_MODEL), F32), jax.ShapeDtypeStruct((s, D_MODEL), BF16),
                   jax.ShapeDtypeStruct((s, D_MODEL), BF16),
                   jax.ShapeDtypeStruct((m, 2 * D_MODEL), F32), jax.ShapeDtypeStruct((1, D_MODEL), F32)],
        compiler_params=_params(("arbitrary",)),
    )(dh2, wo, q, kv, wq, h1, g)


def _mem_kv_bwd(dkv, mn, wkv, mem, g, after=()):
    m = mem.shape[0]

    def body(dkv_ref, mn_ref, w_ref, mem_ref, g_ref, dw_ref, dg_ref):
        dmn = jnp.zeros((m, D_MODEL), F32)
        mn = mn_ref[...]
        for j in range(4):
            dj = dkv_ref[:, j * 512:(j + 1) * 512].astype(BF16)
            dw_ref[j] = _dot_tn(mn, dj)
            dmn = dmn + _dot_nt(dj, w_ref[j])
        mv = mem_ref[...]
        dg_ref[...] = _rowsum(dmn * (mv * _rms_stats(mv)))

    return _tied_call(
        body, after, name="mem_kv_bwd", in_specs=[pl.BlockSpec(memory_space=pltpu.VMEM)] * 5,
        out_shape=[jax.ShapeDtypeStruct((4, D_MODEL, 512), F32), jax.ShapeDtypeStruct((1, D_MODEL), F32)],
        compiler_params=pltpu.CompilerParams(vmem_limit_bytes=VMEM_LIMIT_BYTES),
    )(dkv, mn, wkv, mem, g)


def _seqmix_bwd(dh1, x, z, c1, w_out, w_in, g_mix, cw, lng, lnb, gg, gb, wpair, wpair_t, bias, t, after=()):
    s = x.shape[0]
    nt = s // t
    halo_blocks = t // CONV_HALO

    def body(dh1_ref, x_ref, z_ref, zh_ref, c1_ref, wo_ref, wi_ref, gm_ref, cw_ref, lng_ref, lnb_ref,
             gg_ref, gb_ref, wpair_ref, wpt_ref, bias_ref,
             gx_ref, dz_ref, dcw_ref, dcb_ref, dlng_ref, dlnb_ref, dgg_ref, dgb_ref, dws_ref, dbs_ref,
             dbin_ref, dgm_ref, abuf, dbuf, ash, dsh, mixed_ref, dv_ref):
        i = pl.program_id(0)
        tile = nt - 1 - i
        accs = (dcw_ref, dcb_ref, dlng_ref, dlnb_ref, dgg_ref, dgb_ref, dws_ref, dbs_ref, dbin_ref, dgm_ref)

        @pl.when(i == 0)
        def _():
            for r in accs:
                r[...] = jnp.zeros_like(r)
            dbuf[t:t + CONV_HALO, :] = jnp.zeros((CONV_HALO, CONV_WIDTH), F32)

        @pl.when(i > 0)
        def _():
            dbuf[t:t + CONV_HALO, :] = dbuf[0:CONV_HALO, :]

        dmix = _dot_nt(dh1_ref[...].astype(BF16), wo_ref[...])

        xh, rs = _ln_stats(c1_ref[...])
        lng = lng_ref[...]
        ln = xh * lng + lnb_ref[...]
        sl = _sigmoid(ln)
        dln = dmix[:, 0:512] * (sl * (1.0 + ln * (1.0 - sl)))
        dc1, dg_ln, db_ln = _ln_bwd(dln, xh, rs, lng)
        dlng_ref[...] += dg_ln
        dlnb_ref[...] += db_ln
        dcb_ref[...] += _rowsum(dc1)
        dbuf[0:t, :] = dc1

        zh = zh_ref[...]
        a_halo = zh[:, 0:512] * _sigmoid(zh[:, 512:1024])
        abuf[0:CONV_HALO, :] = jnp.where(tile > 0, a_halo, 0.0)
        za = z_ref[:, 0:512]
        sg = _sigmoid(z_ref[:, 512:1024])
        abuf[CONV_HALO:, :] = za * sg
        _shift_rows(abuf, ash, t)
        _shift_rows(dbuf, dsh, t)

        da = jnp.zeros((t, CONV_WIDTH), F32)
        for k in range(CONV_KERNEL):
            da = da + cw_ref[k:k + 1, :] * _window(dbuf, dsh, CONV_KERNEL - 1 - k, t)
            dcw_ref[k:k + 1, :] += _rowsum(dc1 * _window(abuf, ash, CONV_HALO - (CONV_KERNEL - 1) + k, t))
        dza = da * sg
        dzg = da * za * (sg * (1.0 - sg))
        dz_ref[:, 0:512] = dza.astype(BF16)
        dz_ref[:, 512:1024] = dzg.astype(BF16)
        dbin_ref[:, 0:512] += _rowsum(dza)
        dbin_ref[:, 512:1024] += _rowsum(dzg)

        dgm = dmix[:, 512:1024]
        u, du_dz = _gelu_parts(z_ref[:, 1024:1536])
        gv, dgv_dz = _gelu_parts(z_ref[:, 1536:2048])
        vxh, vrs = _ln_stats(gv)
        ggv = gg_ref[...]
        v = vxh * ggv + gb_ref[...]
        low = _lane_is_low_head()
        v_lo = jnp.where(low, v, 0.0).astype(BF16)
        v_hi = jnp.where(low, 0.0, v).astype(BF16)
        _gm_mix(v_lo, v_hi, wpair_ref, bias_ref, mixed_ref, t)
        dzu = dgm * mixed_ref[...] * du_dz
        dm = dgm * u
        dm_lo = jnp.where(low, dm, 0.0).astype(BF16)
        dm_hi = jnp.where(low, 0.0, dm).astype(BF16)
        vb = v.astype(BF16)
        tril = (lax.broadcasted_iota(jnp.int32, (CHUNK, CHUNK), 1)
                <= lax.broadcasted_iota(jnp.int32, (CHUNK, CHUNK), 0))
        for n in range(t // CHUNK):
            rows = slice(n * CHUNK, (n + 1) * CHUNK)
            dbs_ref[...] += dm[rows, :]
            for j in range(GM_HEADS // 2):
                cols = slice(j * LANES, (j + 1) * LANES)
                stack = jnp.concatenate([dm_lo[rows, cols], dm_hi[rows, cols]], axis=0)
                dws = _dot_nt(stack, vb[rows, cols])
                dws_ref[2 * j] += jnp.where(tril, dws[0:CHUNK], 0.0)
                dws_ref[2 * j + 1] += jnp.where(tril, dws[CHUNK:2 * CHUNK], 0.0)
                dv_ref[rows, cols] = _dot(wpt_ref[j], stack)
        dgv, dg_gm, db_gm = _ln_bwd(dv_ref[...], vxh, vrs, ggv)
        dgg_ref[...] += dg_gm
        dgb_ref[...] += db_gm
        dzv = dgv * dgv_dz
        dz_ref[:, 1024:1536] = dzu.astype(BF16)
        dz_ref[:, 1536:2048] = dzv.astype(BF16)
        dbin_ref[:, 1024:1536] += _rowsum(dzu)
        dbin_ref[:, 1536:2048] += _rowsum(dzv)

        dhn = jnp.zeros((t, D_MODEL), F32)
        for j in range(4):
            dhn = dhn + _dot_nt(dz_ref[:, j * 512:(j + 1) * 512], wi_ref[j])
        xv = x_ref[...]
        dv, dg = _rms_bwd(dhn, xv, _rms_stats(xv), gm_ref[...])
        gx_ref[...] = dh1_ref[...] + dv
        dgm_ref[...] += dg

    rev = lambda w: pl.BlockSpec((t, w), lambda i: (nt - 1 - i, 0))
    const = lambda *shape: pl.BlockSpec(shape, lambda i: (0,) * len(shape))
    halo = pl.BlockSpec((CONV_HALO, D_MODEL), lambda i: (jnp.maximum((nt - 1 - i) * halo_blocks - 1, 0), 0))
    f32 = lambda *shape: jax.ShapeDtypeStruct(shape, F32)
    return _tied_call(
        body, after, name="seqmix_bwd", grid=(nt,),
        in_specs=[rev(D_MODEL), rev(D_MODEL), rev(2048), halo, rev(CONV_WIDTH),
                  const(D_MODEL, D_MODEL), const(4, D_MODEL, 512), const(1, D_MODEL),
                  const(CONV_HALO, CONV_WIDTH), const(1, 512), const(1, 512), const(1, 512), const(1, 512),
                  const(4, CHUNK, 2 * CHUNK), const(4, CHUNK, 2 * CHUNK), const(CHUNK, GM_WIDTH)],
        out_specs=[rev(D_MODEL), rev(2048),
                   const(CONV_HALO, CONV_WIDTH), const(1, 512), const(1, 512), const(1, 512), const(1, 512),
                   const(1, 512), const(GM_HEADS, CHUNK, CHUNK), const(CHUNK, GM_WIDTH), const(1, 2048),
                   const(1, D_MODEL)],
        out_shape=[f32(s, D_MODEL), jax.ShapeDtypeStruct((s, 2048), BF16),
                   f32(CONV_HALO, CONV_WIDTH), f32(1, 512), f32(1, 512), f32(1, 512), f32(1, 512),
                   f32(1, 512), f32(GM_HEADS, CHUNK, CHUNK), f32(CHUNK, GM_WIDTH), f32(1, 2048),
                   f32(1, D_MODEL)],
        scratch_shapes=[pltpu.VMEM((t + CONV_HALO, CONV_WIDTH), F32), pltpu.VMEM((t + CONV_HALO, CONV_WIDTH), F32),
                        pltpu.VMEM((SUBLANES - 1, t + CONV_HALO - SUBLANES, CONV_WIDTH), F32),
                        pltpu.VMEM((SUBLANES - 1, t + CONV_HALO - SUBLANES, CONV_WIDTH), F32),
                        pltpu.VMEM((t, GM_WIDTH), F32), pltpu.VMEM((t, GM_WIDTH), F32)],
        compiler_params=_params(("arbitrary",)),
    )(dh1, x, z, z, c1, w_out, w_in, g_mix, cw, lng, lnb, gg, gb, wpair, wpair_t, bias)


def _head_bias_grad(dbs):
    def body(d_ref, o_ref):
        dv = d_ref[...]
        lane = lax.broadcasted_iota(jnp.int32, (CHUNK, LANES), 1)
        acc = jnp.zeros((CHUNK, LANES), F32)
        for h in range(GM_HEADS):
            sh = jnp.sum(dv[:, h * GM_HEAD_DIM:(h + 1) * GM_HEAD_DIM], axis=-1, keepdims=True)
            acc = acc + jnp.where(lane == h, sh, 0.0)
        o_ref[...] = acc

    return pl.pallas_call(body, name="head_bias_grad",
                          out_shape=jax.ShapeDtypeStruct((CHUNK, LANES), F32))(dbs)


def kernel(x, mem, norm_mix_g, w_in, b_in, conv_w, conv_b, conv_ln_g, conv_ln_b, gm_ln_g, gm_ln_b, gm_w_s, gm_b_s, w_out, norm_xa_g, mem_norm_g, xa_wq, xa_wkv, xa_wo, norm_ffn_g, ffn_w_gate_up, ffn_w_down, final_norm_g, loss_target, m_norm_mix_g, m_w_in, m_b_in, m_conv_w, m_conv_b, m_conv_ln_g, m_conv_ln_b, m_gm_ln_g, m_gm_ln_b, m_gm_w_s, m_gm_b_s, m_w_out, m_norm_xa_g, m_mem_norm_g, m_xa_wq, m_xa_wkv, m_xa_wo, m_norm_ffn_g, m_ffn_w_gate_up, m_ffn_w_down, m_final_norm_g, v_norm_mix_g, v_w_in, v_b_in, v_conv_w, v_conv_b, v_conv_ln_g, v_conv_ln_b, v_gm_ln_g, v_gm_ln_b, v_gm_w_s, v_gm_b_s, v_w_out, v_norm_xa_g, v_mem_norm_g, v_xa_wq, v_xa_wkv, v_xa_wo, v_norm_ffn_g, v_ffn_w_gate_up, v_ffn_w_down, v_final_norm_g):
    weights = dict(norm_mix_g=norm_mix_g, w_in=w_in, b_in=b_in, conv_w=conv_w, conv_b=conv_b, conv_ln_g=conv_ln_g,
                   conv_ln_b=conv_ln_b, gm_ln_g=gm_ln_g, gm_ln_b=gm_ln_b, gm_w_s=gm_w_s, gm_b_s=gm_b_s, w_out=w_out,
                   norm_xa_g=norm_xa_g, mem_norm_g=mem_norm_g, xa_wq=xa_wq, xa_wkv=xa_wkv, xa_wo=xa_wo,
                   norm_ffn_g=norm_ffn_g, ffn_w_gate_up=ffn_w_gate_up, ffn_w_down=ffn_w_down,
                   final_norm_g=final_norm_g)
    m_in = dict(norm_mix_g=m_norm_mix_g, w_in=m_w_in, b_in=m_b_in, conv_w=m_conv_w, conv_b=m_conv_b,
                conv_ln_g=m_conv_ln_g, conv_ln_b=m_conv_ln_b, gm_ln_g=m_gm_ln_g, gm_ln_b=m_gm_ln_b, gm_w_s=m_gm_w_s,
                gm_b_s=m_gm_b_s, w_out=m_w_out, norm_xa_g=m_norm_xa_g, mem_norm_g=m_mem_norm_g, xa_wq=m_xa_wq,
                xa_wkv=m_xa_wkv, xa_wo=m_xa_wo, norm_ffn_g=m_norm_ffn_g, ffn_w_gate_up=m_ffn_w_gate_up,
                ffn_w_down=m_ffn_w_down, final_norm_g=m_final_norm_g)
    v_in = dict(norm_mix_g=v_norm_mix_g, w_in=v_w_in, b_in=v_b_in, conv_w=v_conv_w, conv_b=v_conv_b,
                conv_ln_g=v_conv_ln_g, conv_ln_b=v_conv_ln_b, gm_ln_g=v_gm_ln_g, gm_ln_b=v_gm_ln_b, gm_w_s=v_gm_w_s,
                gm_b_s=v_gm_b_s, w_out=v_w_out, norm_xa_g=v_norm_xa_g, mem_norm_g=v_mem_norm_g, xa_wq=v_xa_wq,
                xa_wkv=v_xa_wkv, xa_wo=v_xa_wo, norm_ffn_g=v_norm_ffn_g, ffn_w_gate_up=v_ffn_w_gate_up,
                ffn_w_down=v_ffn_w_down, final_norm_g=v_final_norm_g)
    grads, delta, new_m, new_v = {}, {}, {}, {}

    s = x.shape[1]
    ts = _row_tile(s)
    tb = max(CHUNK, ts // 2)
    cx, cy, cc = _mesh_pos()
    chip = 2 * cx + cy
    pos = jnp.stack([chip, cc]).astype(jnp.int32)
    row = lambda a: a.reshape(1, -1)
    x2, mem2, tgt2 = x[0], mem[0], loss_target[0]

    big = dict(w_in=w_in, xa_wkv=xa_wkv, w_out=w_out, xa_wq=xa_wq, xa_wo=xa_wo,
               ffn_w_gate_up=ffn_w_gate_up, ffn_w_down=ffn_w_down)
    big_names = list(big)
    halves = lambda a: a.reshape(2, a.shape[0] // 2, a.shape[1])
    cast = {nm: _cast_into_slot(halves(big[nm]), pos, BF16, "cast_" + nm) for nm in big_names}
    conv_w_pad = jnp.pad(conv_w, ((0, CONV_HALO - CONV_KERNEL), (0, 0)))
    cast["conv_w"] = _cast_into_slot(halves(conv_w_pad), pos, F32, "slot_conv_w")

    def start_gather(names, after):
        return _gather_start([cast[nm] for nm in names], "gather_start_" + names[0], after)

    def finish_gather(names, started, after):
        send_sems, recv_sems, bufs, _ = started
        landed = _gather_wait(send_sems, recv_sems, bufs, after, "gather_wait_" + names[0])
        return dict(zip(names, (b.reshape(N_CHIPS, -1, b.shape[-1])
                                for b in _pass_to_sibling(landed, "pass_" + names[0]))))

    first_names = ["w_in", "conv_w"]
    attn_names = ["w_out", "xa_wq", "xa_wkv", "xa_wo"]
    gather_first = start_gather(first_names, ())
    gw = finish_gather(first_names, gather_first, [cast[nm] for nm in big_names if nm != "w_in"])
    w_in_g = gw["w_in"]
    cw_g = jnp.concatenate([gw["conv_w"][k] for k in range(N_CHIPS)], axis=1)
    gather_attn = start_gather(attn_names, w_in_g)

    tril = jnp.tril(jnp.ones((CHUNK, CHUNK), dtype=bool))
    ws = jnp.where(tril[None], gm_w_s, 0.0)
    wpair = jnp.concatenate([ws[0::2], ws[1::2]], axis=2).astype(BF16)
    ws_t = jnp.swapaxes(ws, 1, 2)
    wpair_t = jnp.concatenate([ws_t[0::2], ws_t[1::2]], axis=2).astype(BF16)
    bias = jnp.repeat(gm_b_s.T, GM_HEAD_DIM, axis=1)

    z, hn1 = _mix_in(x2, row(norm_mix_g), w_in_g, row(b_in), ts, after=gather_attn[3])
    mix, c1 = _seqmix_fwd(z, cw_g, row(conv_b), row(conv_ln_g), row(conv_ln_b), row(gm_ln_g), row(gm_ln_b),
                          wpair, bias, ts)
    gw = finish_gather(attn_names, gather_attn, mix)
    w_out_g = gw["w_out"].reshape(D_MODEL, D_MODEL)
    wq_g = gw["xa_wq"].reshape(D_MODEL, D_MODEL)
    wkv_g = gw["xa_wkv"]
    wo_g = gw["xa_wo"].reshape(D_MODEL, D_MODEL)
    gather_gu = start_gather(["ffn_w_gate_up"], w_out_g)
    h1, hn2, q = _out_proj_q(x2, mix, w_out_g, row(norm_xa_g), wq_g, ts, after=gather_gu[3])
    mn, kv = _mem_kv(mem2, row(mem_norm_g), wkv_g)
    o, h2, hn3 = _attn_fwd(q, kv, h1, wo_g, row(norm_ffn_g), ts)
    wgu_g = finish_gather(["ffn_w_gate_up"], gather_gu, hn3)["ffn_w_gate_up"]
    gather_down = start_gather(["ffn_w_down"], wgu_g)
    gu, act = _ffn_up(hn3, wgu_g.reshape(2, 2, D_MODEL, FFN_HALF), ts, after=gather_down[3])
    wd_g = finish_gather(["ffn_w_down"], gather_down, act)["ffn_w_down"].reshape(FFN_HIDDEN, D_MODEL)
    dh3, dh3_b, sq, d_final_g = _ffn_down_loss(act, wd_g, h2, row(final_norm_g), tgt2, ts)
    loss_here = jnp.broadcast_to(0.5 * jnp.sum(sq) / D_MODEL, (1, 2, SUBLANES, LANES))

    def split(g, nm):
        r, c = big[nm].shape
        return g.reshape(N_CHIPS, 2, r // 2, c)

    def chip_sums_of(group, arrays):
        got = _swap_halves(arrays, "swap_halves_" + group[0])
        both = [_add_half(g, r, pos, "chip_sum_" + nm, F32 if nm == "loss" else BF16)
                for g, r, nm in zip(arrays, got, group)]
        return [b[0] for b in both], [b[1] for b in both]

    def start_swap(group, grads):
        return _swap_start([split(g, nm) for g, nm in zip(grads, group)], "swap_start_" + group[0])

    def start_exchange(group, swapping, after):
        sems, arrays, lands, _ = swapping
        arrays, got = _swap_wait(sems, arrays, lands, after, "swap_wait_" + group[0])
        both = [_add_half(g, r, pos, "chip_sum_" + nm) for g, r, nm in zip(arrays, got, group)]
        return _exchange_start([b[0] for b in both], [b[1] for b in both], "exchange_start_" + group[0])

    def finish_exchange(group, started, after):
        sems, sums, parts, _ = started
        parts = _exchange_wait(sems, sums, parts, after, "exchange_wait_" + group[0])
        return [_sum_chips(p, pos, "total_" + nm) for p, nm in zip(parts, group)]

    def join_and_update(group, after):
        joined = _join_halves([halves_of[nm] for nm in group], "join_halves_" + group[0], after)
        for nm, j in zip(group, joined):
            grads[nm] = j.reshape(big[nm].shape)
            delta[nm], new_m[nm], new_v[nm] = _adamw(weights[nm], grads[nm], m_in[nm], v_in[nm], "adamw_" + nm)
        return [new_v[nm] for nm in group]

    as3 = lambda a: a.reshape((1,) + a.shape)
    halves_of = {}

    dgu = _ffn_bwd_act(dh3_b, wd_g.reshape(2, FFN_HALF, D_MODEL), gu, ts)
    g_down = _grad_w(act, as3(dh3_b), FFN_HALF, D_MODEL, "grad_ffn_w_down")
    group_a = ["ffn_w_down"]
    swap_a = start_swap(group_a, [g_down])
    dh2, dh2_b, d_ffn_g = _ffn_bwd_in(dgu, wgu_g, dh3, h2, row(norm_ffn_g), tb, after=swap_a[3])
    exch_a = start_exchange(group_a, swap_a, dh2)
    g_gu = _grad_w(hn3, dgu, D_MODEL, FFN_HALF, "grad_ffn_w_gate_up", after=exch_a[3])
    halves_of.update(zip(group_a, finish_exchange(group_a, exch_a, g_gu)))

    group_b = ["ffn_w_gate_up"]
    swap_b = start_swap(group_b, [g_gu])
    dh1, dh1_b, dq, dkv, d_xa_g = _attn_bwd(dh2, wo_g, q, kv, wq_g, h1, row(norm_xa_g), ts, after=swap_b[3])
    exch_b = start_exchange(group_b, swap_b, dh1)
    g_wkv, d_mem_g = _mem_kv_bwd(dkv, mn, wkv_g, mem2, row(mem_norm_g), after=exch_b[3])
    g_wo = _grad_w(o, as3(dh2_b), D_MODEL, D_MODEL, "grad_xa_wo", after=exch_b[3])
    g_wq = _grad_w(hn2, as3(dq), D_MODEL, D_MODEL, "grad_xa_wq", after=exch_b[3])
    g_wout = _grad_w(mix, as3(dh1_b), D_MODEL, D_MODEL, "grad_w_out", after=exch_b[3])
    halves_of.update(zip(group_b, finish_exchange(group_b, exch_b, (g_wkv, g_wo, g_wq, g_wout))))

    group_c = ["xa_wo", "xa_wq", "xa_wkv", "w_out"]
    swap_c = start_swap(group_c, [g_wo, g_wq, g_wkv, g_wout])
    (gx, dz, d_cw, d_cb, d_lng, d_lnb, d_gg, d_gb, d_ws, d_bs_sum, d_bin, d_mix_g) = _seqmix_bwd(
        dh1, x2, z, c1, w_out_g, w_in_g, row(norm_mix_g), cw_g, row(conv_ln_g), row(conv_ln_b),
        row(gm_ln_g), row(gm_ln_b), wpair, wpair_t, bias, tb, after=swap_c[3])
    d_bs = _head_bias_grad(d_bs_sum)[:, :GM_HEADS].T
    exch_c = start_exchange(group_c, swap_c, dz)
    g_win = _grad_w(hn1, as3(dz), D_MODEL, 512, "grad_w_in", after=exch_c[3])
    done_ab = join_and_update(group_a + group_b, g_win)
    halves_of.update(zip(group_c, finish_exchange(group_c, exch_c, (g_win, *done_ab))))

    small_names = ["norm_mix_g", "b_in", "conv_w", "conv_b", "conv_ln_g", "conv_ln_b", "gm_ln_g", "gm_ln_b",
                   "gm_w_s", "gm_b_s", "norm_xa_g", "mem_norm_g", "norm_ffn_g", "final_norm_g"]
    d_cw_by_chip = jnp.swapaxes(d_cw.reshape(CONV_HALO, N_CHIPS, LANES), 0, 1).reshape(-1, LANES)
    small_grads = dict(norm_mix_g=d_mix_g, b_in=d_bin, conv_w=d_cw_by_chip, conv_b=d_cb, conv_ln_g=d_lng,
                       conv_ln_b=d_lnb, gm_ln_g=d_gg, gm_ln_b=d_gb, gm_w_s=d_ws, gm_b_s=d_bs, norm_xa_g=d_xa_g,
                       mem_norm_g=d_mem_g, norm_ffn_g=d_ffn_g, final_norm_g=d_final_g)

    def rows_form(a):
        a = a.reshape(-1, LANES)
        return jnp.pad(a, ((0, -a.shape[0] % SUBLANES), (0, 0)))

    pieces = [rows_form(small_grads[nm]) for nm in small_names]
    offsets, total = [], 0
    for p in pieces:
        offsets.append(total)
        total += p.shape[0]
    pack_rows = -(-total // 32) * 32
    small_pack = jnp.pad(jnp.concatenate(pieces, axis=0), ((0, pack_rows - total), (0, 0)))

    group_d = ["w_in", "small", "loss"]
    sums_d, parts_d = chip_sums_of(group_d, [split(g_win, "w_in"),
                                             small_pack.reshape(1, 2, pack_rows // 2, LANES), loss_here])
    exch_d = _exchange_start(sums_d, parts_d, "exchange_start_w_in")
    done_c = join_and_update(group_c, exch_d[3])
    halves_of.update(zip(group_d, finish_exchange(group_d, exch_d, done_c)))
    joined_d = _join_halves([halves_of[nm] for nm in group_d], "join_halves_w_in")
    grads["w_in"] = joined_d[0].reshape(w_in.shape)
    loss = joined_d[2][0, 0, 0]
    delta["w_in"], new_m["w_in"], new_v["w_in"] = _adamw(w_in, grads["w_in"], m_w_in, v_w_in, "adamw_w_in")

    local_rows = lambda a, nm: a if nm == "conv_w" else a.reshape(-1, LANES)
    params = [tuple(local_rows(src[nm], nm) for src in (weights, m_in, v_in)) for nm in small_names]
    outs = _adamw_small(joined_d[1].reshape(pack_rows, LANES), pos, params, offsets, small_names.index("conv_w"))
    for k, nm in enumerate(small_names):
        for dst, a in zip((grads, delta, new_m, new_v), outs[4 * k:4 * k + 4]):
            dst[nm] = a

    order = ["norm_mix_g", "w_in", "b_in", "conv_w", "conv_b", "conv_ln_g", "conv_ln_b", "gm_ln_g", "gm_ln_b",
             "gm_w_s", "gm_b_s", "w_out", "norm_xa_g", "mem_norm_g", "xa_wq", "xa_wkv", "xa_wo", "norm_ffn_g",
             "ffn_w_gate_up", "ffn_w_down", "final_norm_g"]
    fit = lambda a, nm: a.reshape(weights[nm].shape)
    return (loss, gx.reshape(x.shape),
            *[fit(grads[nm], nm) for nm in order], *[fit(delta[nm], nm) for nm in order],
            *[fit(new_m[nm], nm) for nm in order], *[fit(new_v[nm], nm) for nm in order])
```

```python
import functools

import jax
import jax.numpy as jnp
from jax import lax
from jax.experimental import pallas as pl
from jax.experimental.pallas import tpu as pltpu

F32 = jnp.float32
BF16 = jnp.bfloat16

D_MODEL = 1024
CONV_WIDTH = 512
GM_WIDTH = 512
CONV_KERNEL = 31
CONV_HALO = 32
GRAD_ROWS = 2048
CHUNK = 128
GM_HEADS = 8
GM_HEAD_DIM = 64
XA_HEADS = 4
XA_HEAD_DIM = 256
FFN_HIDDEN = 2816
FFN_HALF = FFN_HIDDEN // 2
RMS_EPS = 1e-6
LN_EPS = 1e-5
N_CHIPS = 4
LANES = 128
SUBLANES = 8

ADAM_LR = 0.001
ADAM_B1 = 0.9
ADAM_B2 = 0.999
ADAM_EPS = 1e-08
ADAM_WD = 0.01
ADAM_STEP = 10

VMEM_LIMIT_BYTES = 56 * 1024 * 1024
MESH = pl.DeviceIdType.MESH
ANY = pl.BlockSpec(memory_space=pl.ANY)
HBM_SPEC = pl.BlockSpec(memory_space=pltpu.HBM)
SEM_SPEC = pl.BlockSpec(memory_space=pltpu.SEMAPHORE)

_NT = (((1,), (1,)), ((), ()))
_TN = (((0,), (0,)), ((), ()))
_GELU_C = 0.7978845608028654
_GELU_A = 0.044715


def _dot(a, b):
    return jnp.dot(a, b, preferred_element_type=F32)


def _dot_nt(a, b):
    return lax.dot_general(a, b, _NT, preferred_element_type=F32)


def _dot_tn(a, b):
    return lax.dot_general(a, b, _TN, preferred_element_type=F32)


def _mean(v):
    return jnp.mean(v, axis=-1, keepdims=True)


def _rowsum(v):
    return jnp.sum(v, axis=0, keepdims=True)


def _sigmoid(v):
    return 1.0 / (1.0 + jnp.exp(-v))


def _gelu_parts(v):
    v2 = v * v
    t = jnp.tanh(_GELU_C * (v + _GELU_A * v * v2))
    g = 0.5 * v * (1.0 + t)
    dg = 0.5 * (1.0 + t) + 0.5 * v * (1.0 - t * t) * (_GELU_C * (1.0 + 3.0 * _GELU_A * v2))
    return g, dg


def _rms_stats(v):
    return lax.rsqrt(_mean(v * v) + RMS_EPS)


def _rms_bwd(dy, v, r, g):
    n = v * r
    dn = dy * g
    dv = r * (dn - n * _mean(dn * n))
    return dv, _rowsum(dy * n)


def _ln_stats(v):
    mu = _mean(v)
    xc = v - mu
    rs = lax.rsqrt(_mean(xc * xc) + LN_EPS)
    return xc * rs, rs


def _ln_bwd(dy, xh, rs, g):
    dxh = dy * g
    dv = rs * (dxh - _mean(dxh) - xh * _mean(dxh * xh))
    return dv, _rowsum(dy * xh), _rowsum(dy)


def _params(sem):
    return pltpu.CompilerParams(dimension_semantics=sem, vmem_limit_bytes=VMEM_LIMIT_BYTES)


def _row_tile(s):
    return 512 if s % 512 == 0 and s >= 2048 else 128


def _mesh_pos():
    return lax.axis_index("x"), lax.axis_index("y"), lax.axis_index("c")


def _cast_into_slot(w, pos, dtype, name):
    _, h, c = w.shape

    def body(pos_ref, w_ref, o_ref):
        o_ref[0] = w_ref[...].astype(dtype)

    return pl.pallas_call(
        body, name=name,
        grid_spec=pltpu.PrefetchScalarGridSpec(
            num_scalar_prefetch=1, grid=(2,),
            in_specs=[pl.BlockSpec((1, h, c), lambda i, p: (i, 0, 0))],
            out_specs=pl.BlockSpec((1, 1, h, c), lambda i, p: (p[0], i, 0, 0))),
        out_shape=jax.ShapeDtypeStruct((N_CHIPS, 2, h, c), dtype),
        compiler_params=_params(("parallel",)),
    )(pos, w)


def _adam_rows(r, c):
    for tr in (r, 1024, 704, 640, 512, 352, 320, 256, 128, 64, 32, 16, 8):
        if r % tr == 0 and tr * c * 4 <= (3 << 19):
            return tr
    return r


def _adam_update(w, g, m, v):
    nm = ADAM_B1 * m + (1.0 - ADAM_B1) * g
    nv = ADAM_B2 * v + (1.0 - ADAM_B2) * (g * g)
    m_hat = nm / (1.0 - ADAM_B1 ** ADAM_STEP)
    v_hat = nv / (1.0 - ADAM_B2 ** ADAM_STEP)
    return -ADAM_LR * (m_hat / (jnp.sqrt(v_hat) + ADAM_EPS) + ADAM_WD * w), nm, nv


def _adamw(w, g, m, v, name, after=()):
    r, c = w.shape
    tr = _adam_rows(r, c)

    def body(w_ref, g_ref, m_ref, v_ref, d_ref, nm_ref, nv_ref):
        d_ref[...], nm_ref[...], nv_ref[...] = _adam_update(w_ref[...], g_ref[...], m_ref[...], v_ref[...])

    spec = pl.BlockSpec((tr, c), lambda i: (i, 0))
    shp = jax.ShapeDtypeStruct((r, c), F32)
    return _tied_call(
        body, after, name=name, grid=(r // tr,),
        in_specs=[spec] * 4, out_specs=[spec] * 3, out_shape=[shp] * 3,
        compiler_params=_params(("parallel",)),
    )(w, g, m, v)


def _adamw_small(gpack, pos, params, offsets, conv_at):
    n = len(params)

    def body(pos_ref, g_ref, *refs):
        ins, outs = refs[:3 * n], refs[3 * n:]
        for k in range(n):
            rows = params[k][0].shape[0]
            start = offsets[k]
            if k == conv_at:
                start = pl.multiple_of(start + pos_ref[0] * CONV_HALO, SUBLANES)
            g = g_ref[pl.ds(start, rows), :]
            outs[4 * k][...] = g
            outs[4 * k + 1][...], outs[4 * k + 2][...], outs[4 * k + 3][...] = _adam_update(
                ins[3 * k][...], g, ins[3 * k + 1][...], ins[3 * k + 2][...])

    flat = [a for p in params for a in p]
    vmem = pl.BlockSpec(memory_space=pltpu.VMEM)
    return pl.pallas_call(
        body, name="adamw_small",
        in_specs=[pl.BlockSpec(memory_space=pltpu.SMEM), vmem] + [vmem] * len(flat),
        out_specs=[vmem] * (4 * n),
        out_shape=[jax.ShapeDtypeStruct(p[0].shape, F32) for p in params for _ in range(4)],
    )(pos, gpack, *flat)


def _as_tuple(after):
    return tuple(after) if isinstance(after, (tuple, list)) else (after,)


def _tied_call(body, after, *, in_specs, **kwargs):
    after = _as_tuple(after)
    n_in, n_after = len(in_specs), len(after)

    def tied(*refs):
        body(*refs[:n_in], *refs[n_in + n_after:])

    call = pl.pallas_call(tied, in_specs=list(in_specs) + [ANY] * n_after, **kwargs)
    return lambda *operands: call(*operands, *after)


def _other_chips(x, y):
    return [(1 - x, y), (x, 1 - y), (1 - x, 1 - y)]


def _gather_descriptors(bufs, send_of, recv_of):
    x, y, c = _mesh_pos()
    me = 2 * x + y
    chips = _other_chips(x, y)
    sends, arrivals = [], []
    for a in range(len(bufs)):
        for k in range(3):
            ck = 2 * chips[k][0] + chips[k][1]

            def copy(slot, a=a, k=k):
                return pltpu.make_async_remote_copy(
                    src_ref=bufs[a].at[slot, c], dst_ref=bufs[a].at[slot, c],
                    send_sem=send_of(a, k), recv_sem=recv_of(a, k),
                    device_id=(*chips[k], c), device_id_type=MESH)

            sends.append(functools.partial(copy, me))
            arrivals.append(functools.partial(copy, ck))
    return sends, arrivals


def _gather_start(bufs, name, after=()):
    n = len(bufs)
    ns = 3 * n

    def body(*refs):
        sems = refs[n:n + 2 * ns]
        thru = refs[n + 2 * ns:2 * n + 2 * ns]
        token = refs[2 * n + 2 * ns]
        sends, _ = _gather_descriptors(thru, lambda a, k: sems[3 * a + k], lambda a, k: sems[ns + 3 * a + k])
        for cp in sends:
            cp().start()
        token[...] = jnp.zeros_like(token)

    held = [pltpu.with_memory_space_constraint(b, pltpu.HBM) for b in bufs]
    out = _tied_call(
        body, after, name=name,
        out_shape=(*[pltpu.SemaphoreType.DMA(())] * (2 * ns), *[pltpu.HBM(b.shape, b.dtype) for b in held],
                   jax.ShapeDtypeStruct((8, LANES), F32)),
        in_specs=[HBM_SPEC] * n,
        out_specs=(*[SEM_SPEC] * (2 * ns), *[HBM_SPEC] * n, pl.BlockSpec(memory_space=pltpu.VMEM)),
        input_output_aliases={i: 2 * ns + i for i in range(n)},
        compiler_params=pltpu.CompilerParams(has_side_effects=pltpu.SideEffectType.DATAFLOW_SIDE_EFFECTING),
    )(*held)
    return list(out[:ns]), list(out[ns:2 * ns]), list(out[2 * ns:2 * ns + n]), out[2 * ns + n]


def _gather_wait(send_sems, recv_sems, bufs, after, name):
    n = len(bufs)
    ns = 3 * n

    def body(*refs):
        buf_ref = refs[:n]
        sem_ref = refs[n:n + 2 * ns]
        sends, arrivals = _gather_descriptors(buf_ref, lambda a, k: sem_ref[3 * a + k],
                                              lambda a, k: sem_ref[ns + 3 * a + k])
        for cp in sends:
            cp().wait_send()
        for cp in arrivals:
            cp().wait_recv()

    out = pl.pallas_call(
        body, name=name,
        out_shape=tuple(pltpu.HBM(b.shape, b.dtype) for b in bufs),
        in_specs=[HBM_SPEC] * n + [SEM_SPEC] * (2 * ns) + [ANY] * len(_as_tuple(after)),
        out_specs=tuple([HBM_SPEC] * n),
        input_output_aliases={i: i for i in range(n)},
        compiler_params=pltpu.CompilerParams(has_side_effects=pltpu.SideEffectType.DATAFLOW_SIDE_EFFECTING),
    )(*bufs, *send_sems, *recv_sems, *_as_tuple(after))
    return list(out)


def _pass_to_sibling(bufs, name):
    n = len(bufs)

    def body(*refs):
        outs = refs[n:2 * n]
        send_sem, recv_sem = refs[2 * n:]
        x, y, c = _mesh_pos()
        chips = _other_chips(x, y)

        def half(a, k, which):
            ck = 2 * chips[k][0] + chips[k][1]
            return pltpu.make_async_remote_copy(
                src_ref=outs[a].at[ck, which], dst_ref=outs[a].at[ck, which],
                send_sem=send_sem.at[a, k], recv_sem=recv_sem.at[a, k],
                device_id=(x, y, 1 - c), device_id_type=MESH)

        sends = [half(a, k, c) for a in range(n) for k in range(3)]
        for cp in sends:
            cp.start()
        for a in range(n):
            for k in range(3):
                half(a, k, 1 - c).wait_recv()
        for cp in sends:
            cp.wait_send()

    return pl.pallas_call(
        body, name=name,
        in_specs=[ANY] * n, out_specs=[ANY] * n,
        out_shape=[jax.ShapeDtypeStruct(b.shape, b.dtype) for b in bufs],
        input_output_aliases={a: a for a in range(n)},
        scratch_shapes=[pltpu.SemaphoreType.DMA((n, 3))] * 2,
    )(*bufs)


def _swap_halves(grads, name):
    n = len(grads)

    def body(*refs):
        ins, outs = refs[:n], refs[n:2 * n]
        send_sem, recv_sem = refs[2 * n:]
        x, y, c = _mesh_pos()
        cps = [pltpu.make_async_remote_copy(
            src_ref=ins[a].at[:, pl.ds(1 - c, 1)], dst_ref=outs[a],
            send_sem=send_sem.at[a], recv_sem=recv_sem.at[a],
            device_id=(x, y, 1 - c), device_id_type=MESH) for a in range(n)]
        for cp in cps:
            cp.start()
        for cp in cps:
            cp.wait()

    out_shape = [jax.ShapeDtypeStruct((g.shape[0], 1) + g.shape[2:], g.dtype) for g in grads]
    return pl.pallas_call(
        body, name=name,
        in_specs=[ANY] * n, out_specs=[ANY] * n, out_shape=out_shape,
        scratch_shapes=[pltpu.SemaphoreType.DMA((n,))] * 2,
    )(*grads)


def _swap_descriptors(grads, lands, send_of, recv_of):
    x, y, c = _mesh_pos()
    return [functools.partial(
        pltpu.make_async_remote_copy,
        src_ref=grads[a].at[:, pl.ds(1 - c, 1)], dst_ref=lands[a],
        send_sem=send_of(a), recv_sem=recv_of(a),
        device_id=(x, y, 1 - c), device_id_type=MESH) for a in range(len(grads))]


def _swap_start(grads, name):
    n = len(grads)

    def body(*refs):
        sems = refs[2 * n:4 * n]
        g_thru, l_thru = refs[4 * n:5 * n], refs[5 * n:6 * n]
        token = refs[6 * n]
        for cp in _swap_descriptors(g_thru, l_thru, lambda a: sems[a], lambda a: sems[n + a]):
            cp().start()
        token[...] = jnp.zeros_like(token)

    lands = [lax.empty((g.shape[0], 1) + g.shape[2:], g.dtype) for g in grads]
    held = [pltpu.with_memory_space_constraint(a, pltpu.HBM) for a in (*grads, *lands)]
    out = pl.pallas_call(
        body, name=name,
        out_shape=(*[pltpu.SemaphoreType.DMA(())] * (2 * n), *[pltpu.HBM(a.shape, a.dtype) for a in held],
                   jax.ShapeDtypeStruct((8, LANES), F32)),
        in_specs=[HBM_SPEC] * (2 * n),
        out_specs=(*[SEM_SPEC] * (2 * n), *[HBM_SPEC] * (2 * n), pl.BlockSpec(memory_space=pltpu.VMEM)),
        input_output_aliases={i: 2 * n + i for i in range(2 * n)},
        compiler_params=pltpu.CompilerParams(has_side_effects=pltpu.SideEffectType.DATAFLOW_SIDE_EFFECTING),
    )(*held)
    return list(out[:2 * n]), list(out[2 * n:3 * n]), list(out[3 * n:4 * n]), out[4 * n]


def _swap_wait(sems, grads, lands, after, name):
    n = len(grads)

    def body(*refs):
        g_ref, l_ref = refs[:n], refs[n:2 * n]
        sem_ref = refs[2 * n:4 * n]
        for cp in _swap_descriptors(g_ref, l_ref, lambda a: sem_ref[a], lambda a: sem_ref[n + a]):
            cp().wait()

    out = pl.pallas_call(
        body, name=name,
        out_shape=tuple(pltpu.HBM(a.shape, a.dtype) for a in (*grads, *lands)),
        in_specs=[HBM_SPEC] * (2 * n) + [SEM_SPEC] * (2 * n) + [ANY] * len(_as_tuple(after)),
        out_specs=tuple([HBM_SPEC] * (2 * n)),
        input_output_aliases={i: i for i in range(2 * n)},
        compiler_params=pltpu.CompilerParams(has_side_effects=pltpu.SideEffectType.DATAFLOW_SIDE_EFFECTING),
    )(*grads, *lands, *sems, *_as_tuple(after))
    return list(out[:n]), list(out[n:])


def _add_half(g, got, pos, name, dtype=BF16):
    j, _, h, c = g.shape

    def body(pos_ref, g_ref, r_ref, o_ref, p_ref):
        val = (g_ref[0, 0] + r_ref[0, 0]).astype(dtype)
        o_ref[0] = val
        if j == 1:
            p_ref[0] = val
        else:
            @pl.when(pl.program_id(0) == pos_ref[0])
            def _():
                p_ref[0] = val

    return pl.pallas_call(
        body, name=name,
        grid_spec=pltpu.PrefetchScalarGridSpec(
            num_scalar_prefetch=1, grid=(j,),
            in_specs=[pl.BlockSpec((1, 1, h, c), lambda i, p: (i, p[1], 0, 0)),
                      pl.BlockSpec((1, 1, h, c), lambda i, p: (i, 0, 0, 0))],
            out_specs=[pl.BlockSpec((1, h, c), lambda i, p: (i, 0, 0)),
                       pl.BlockSpec((1, h, c), lambda i, p: (p[0], 0, 0))]),
        out_shape=[jax.ShapeDtypeStruct((j, h, c), dtype), jax.ShapeDtypeStruct((N_CHIPS, h, c), dtype)],
        compiler_params=_params(("arbitrary",)),
    )(pos, g, got)


def _exchange_descriptors(sums, parts, send_of, recv_of):
    x, y, c = _mesh_pos()
    me = 2 * x + y
    chips = _other_chips(x, y)
    sends, arrivals = [], []
    for a in range(len(sums)):
        for k in range(3):
            ck = 2 * chips[k][0] + chips[k][1]
            mine = sums[a].at[ck] if sums[a].shape[0] == N_CHIPS else sums[a].at[0]

            def copy(dst_slot, a=a, k=k, mine=mine):
                return pltpu.make_async_remote_copy(
                    src_ref=mine, dst_ref=parts[a].at[dst_slot],
                    send_sem=send_of(a, k), recv_sem=recv_of(a, k),
                    device_id=(*chips[k], c), device_id_type=MESH)

            sends.append(functools.partial(copy, me))
            arrivals.append(functools.partial(copy, ck))
    return sends, arrivals


def _exchange_start(sums, parts, name):
    n = len(sums)
    ns = 3 * n

    def body(*refs):
        sems = refs[2 * n:2 * n + 2 * ns]
        sums_thru = refs[2 * n + 2 * ns:3 * n + 2 * ns]
        parts_thru = refs[3 * n + 2 * ns:4 * n + 2 * ns]
        token = refs[4 * n + 2 * ns]
        sends, _ = _exchange_descriptors(sums_thru, parts_thru, lambda a, k: sems[3 * a + k],
                                         lambda a, k: sems[ns + 3 * a + k])
        for cp in sends:
            cp().start()
        token[...] = jnp.zeros_like(token)

    hbm = lambda a: pltpu.HBM(a.shape, a.dtype)
    held = [pltpu.with_memory_space_constraint(a, pltpu.HBM) for a in (*sums, *parts)]
    out = pl.pallas_call(
        body, name=name,
        out_shape=(*[pltpu.SemaphoreType.DMA(())] * (2 * ns), *[hbm(a) for a in held],
                   jax.ShapeDtypeStruct((8, LANES), F32)),
        in_specs=[HBM_SPEC] * (2 * n),
        out_specs=(*[SEM_SPEC] * (2 * ns), *[HBM_SPEC] * (2 * n), pl.BlockSpec(memory_space=pltpu.VMEM)),
        input_output_aliases={i: 2 * ns + i for i in range(2 * n)},
        compiler_params=pltpu.CompilerParams(has_side_effects=pltpu.SideEffectType.DATAFLOW_SIDE_EFFECTING),
    )(*held)
    return (list(out[:2 * ns]), list(out[2 * ns:2 * ns + n]), list(out[2 * ns + n:2 * ns + 2 * n]),
            out[2 * ns + 2 * n])


def _exchange_wait(sems, sums, parts, after, name):
    n = len(sums)
    ns = 3 * n

    def body(*refs):
        sums_ref, parts_ref = refs[:n], refs[n:2 * n]
        sem_ref = refs[2 * n:2 * n + 2 * ns]
        sends, arrivals = _exchange_descriptors(sums_ref, parts_ref, lambda a, k: sem_ref[3 * a + k],
                                                lambda a, k: sem_ref[ns + 3 * a + k])
        for cp in sends:
            cp().wait_send()
        for cp in arrivals:
            cp().wait_recv()

    hbm = lambda a: pltpu.HBM(a.shape, a.dtype)
    out = pl.pallas_call(
        body, name=name,
        out_shape=tuple(hbm(a) for a in (*sums, *parts)),
        in_specs=[HBM_SPEC] * (2 * n) + [SEM_SPEC] * (2 * ns) + [ANY] * len(_as_tuple(after)),
        out_specs=tuple([HBM_SPEC] * (2 * n)),
        input_output_aliases={i: i for i in range(2 * n)},
        compiler_params=pltpu.CompilerParams(has_side_effects=pltpu.SideEffectType.DATAFLOW_SIDE_EFFECTING),
    )(*sums, *parts, *sems, *_as_tuple(after))
    return list(out[n:])


def _sum_chips(parts, pos, name):
    _, h, c = parts.shape

    def body(pos_ref, p_ref, o_ref):
        o_ref[0] = ((p_ref[0].astype(F32) + p_ref[1].astype(F32)) + p_ref[2].astype(F32)) + p_ref[3].astype(F32)

    return pl.pallas_call(
        body, name=name,
        grid_spec=pltpu.PrefetchScalarGridSpec(
            num_scalar_prefetch=1, grid=(1,),
            in_specs=[pl.BlockSpec((N_CHIPS, h, c), lambda i, p: (0, 0, 0))],
            out_specs=pl.BlockSpec((1, h, c), lambda i, p: (p[1], 0, 0))),
        out_shape=jax.ShapeDtypeStruct((2, h, c), F32),
        compiler_params=_params(("arbitrary",)),
    )(pos, parts)


def _join_halves(fulls, name, after=()):
    n = len(fulls)

    def body(*refs):
        outs = refs[n:2 * n]
        send_sem, recv_sem = refs[2 * n:]
        x, y, c = _mesh_pos()

        def half(a, which):
            return pltpu.make_async_remote_copy(
                src_ref=outs[a].at[which], dst_ref=outs[a].at[which],
                send_sem=send_sem.at[a], recv_sem=recv_sem.at[a],
                device_id=(x, y, 1 - c), device_id_type=MESH)

        sends = [half(a, c) for a in range(n)]
        for cp in sends:
            cp.start()
        for a in range(n):
            half(a, 1 - c).wait_recv()
        for cp in sends:
            cp.wait_send()

    out_shape = [jax.ShapeDtypeStruct(f.shape, f.dtype) for f in fulls]
    return _tied_call(
        body, after, name=name,
        in_specs=[ANY] * n, out_specs=[ANY] * n, out_shape=out_shape,
        input_output_aliases={a: a for a in range(n)},
        scratch_shapes=[pltpu.SemaphoreType.DMA((n,))] * 2,
    )(*fulls)


def _mix_in(x, g, w_in, b_in, ts, after=()):
    s = x.shape[0]

    def body(x_ref, g_ref, w_ref, b_ref, z_ref, hn_ref):
        xv = x_ref[...]
        hn = (xv * _rms_stats(xv) * g_ref[...]).astype(BF16)
        hn_ref[...] = hn
        for j in range(4):
            cols = slice(j * 512, (j + 1) * 512)
            z_ref[:, cols] = _dot(hn, w_ref[j]) + b_ref[:, cols]

    return _tied_call(
        body, after, name="mix_in", grid=(s // ts,),
        in_specs=[pl.BlockSpec((ts, D_MODEL), lambda i: (i, 0)),
                  pl.BlockSpec((1, D_MODEL), lambda i: (0, 0)),
                  pl.BlockSpec((4, D_MODEL, 512), lambda i: (0, 0, 0)),
                  pl.BlockSpec((1, 2048), lambda i: (0, 0))],
        out_specs=[pl.BlockSpec((ts, 2048), lambda i: (i, 0)),
                   pl.BlockSpec((ts, D_MODEL), lambda i: (i, 0))],
        out_shape=[jax.ShapeDtypeStruct((s, 2048), F32), jax.ShapeDtypeStruct((s, D_MODEL), BF16)],
        compiler_params=_params(("parallel",)),
    )(x, g, w_in, b_in)


def _shift_rows(buf, shifted, t):
    rows = t + CONV_HALO - SUBLANES
    for r in range(1, SUBLANES):
        shifted[r - 1, 0:rows, :] = buf[pl.ds(r, rows), :]


def _window(buf, shifted, offset, t):
    r = offset % SUBLANES
    if r == 0:
        return buf[pl.ds(offset, t), :]
    return shifted[r - 1, pl.ds(offset - r, t), :]


def _lane_is_low_head():
    lane = lax.broadcasted_iota(jnp.int32, (1, GM_WIDTH), 1)
    return (lane & GM_HEAD_DIM) == 0


def _gm_mix(v_lo, v_hi, wpair_ref, bias_ref, mixed_ref, t):
    for n in range(t // CHUNK):
        rows = slice(n * CHUNK, (n + 1) * CHUNK)
        for j in range(GM_HEADS // 2):
            cols = slice(j * LANES, (j + 1) * LANES)
            rhs = jnp.concatenate([v_lo[rows, cols], v_hi[rows, cols]], axis=0)
            mixed_ref[rows, cols] = _dot(wpair_ref[j], rhs) + bias_ref[:, cols]


def _seqmix_fwd(z, cw, cb, lng, lnb, gg, gb, wpair, bias, t):
    s = z.shape[0]

    def body(z_ref, cw_ref, cb_ref, lng_ref, lnb_ref, gg_ref, gb_ref, wpair_ref, bias_ref,
             mix_ref, c1_ref, abuf, ash, mixed_ref):
        i = pl.program_id(0)

        @pl.when(i == 0)
        def _():
            abuf[0:CONV_HALO, :] = jnp.zeros((CONV_HALO, CONV_WIDTH), F32)

        @pl.when(i > 0)
        def _():
            abuf[0:CONV_HALO, :] = abuf[t:t + CONV_HALO, :]

        abuf[CONV_HALO:, :] = z_ref[:, 0:512] * _sigmoid(z_ref[:, 512:1024])
        _shift_rows(abuf, ash, t)
        acc = jnp.zeros((t, CONV_WIDTH), F32)
        for k in range(CONV_KERNEL):
            acc = acc + cw_ref[k:k + 1, :] * _window(abuf, ash, CONV_HALO - (CONV_KERNEL - 1) + k, t)
        c1 = acc + cb_ref[...]
        c1_ref[...] = c1
        xh, _ = _ln_stats(c1)
        ln = xh * lng_ref[...] + lnb_ref[...]
        mix_ref[:, 0:512] = (ln * _sigmoid(ln)).astype(BF16)

        u, _ = _gelu_parts(z_ref[:, 1024:1536])
        gv, _ = _gelu_parts(z_ref[:, 1536:2048])
        vxh, _ = _ln_stats(gv)
        v = vxh * gg_ref[...] + gb_ref[...]
        low = _lane_is_low_head()
        v_lo = jnp.where(low, v, 0.0).astype(BF16)
        v_hi = jnp.where(low, 0.0, v).astype(BF16)
        _gm_mix(v_lo, v_hi, wpair_ref, bias_ref, mixed_ref, t)
        mix_ref[:, 512:1024] = (u * mixed_ref[...]).astype(BF16)

    vec = lambda n: pl.BlockSpec((1, n), lambda i: (0, 0))
    return pl.pallas_call(
        body, name="seqmix_fwd", grid=(s // t,),
        in_specs=[pl.BlockSpec((t, 2048), lambda i: (i, 0)),
                  pl.BlockSpec((CONV_HALO, CONV_WIDTH), lambda i: (0, 0)),
                  vec(512), vec(512), vec(512), vec(512), vec(512),
                  pl.BlockSpec((4, CHUNK, 2 * CHUNK), lambda i: (0, 0, 0)),
                  pl.BlockSpec((CHUNK, GM_WIDTH), lambda i: (0, 0))],
        out_specs=[pl.BlockSpec((t, D_MODEL), lambda i: (i, 0)),
                   pl.BlockSpec((t, CONV_WIDTH), lambda i: (i, 0))],
        out_shape=[jax.ShapeDtypeStruct((s, D_MODEL), BF16), jax.ShapeDtypeStruct((s, CONV_WIDTH), F32)],
        scratch_shapes=[pltpu.VMEM((t + CONV_HALO, CONV_WIDTH), F32),
                        pltpu.VMEM((SUBLANES - 1, t + CONV_HALO - SUBLANES, CONV_WIDTH), F32),
                        pltpu.VMEM((t, GM_WIDTH), F32)],
        compiler_params=_params(("arbitrary",)),
    )(z, cw, cb, lng, lnb, gg, gb, wpair, bias)


def _out_proj_q(x, mix, w_out, g, wq, ts, after=()):
    s = x.shape[0]

    def body(x_ref, mix_ref, wo_ref, g_ref, wq_ref, h1_ref, hn_ref, q_ref):
        h1 = x_ref[...] + _dot(mix_ref[...], wo_ref[...])
        h1_ref[...] = h1
        hn = (h1 * _rms_stats(h1) * g_ref[...]).astype(BF16)
        hn_ref[...] = hn
        q_ref[...] = _dot(hn, wq_ref[...]).astype(BF16)

    row = lambda dt: pl.BlockSpec((ts, D_MODEL), lambda i: (i, 0))
    full = pl.BlockSpec((D_MODEL, D_MODEL), lambda i: (0, 0))
    return _tied_call(
        body, after, name="out_proj_q", grid=(s // ts,),
        in_specs=[row(F32), row(BF16), full, pl.BlockSpec((1, D_MODEL), lambda i: (0, 0)), full],
        out_specs=[row(F32), row(BF16), row(BF16)],
        out_shape=[jax.ShapeDtypeStruct((s, D_MODEL), F32), jax.ShapeDtypeStruct((s, D_MODEL), BF16),
                   jax.ShapeDtypeStruct((s, D_MODEL), BF16)],
        compiler_params=_params(("parallel",)),
    )(x, mix, w_out, g, wq)


def _mem_kv(mem, g, wkv):
    m = mem.shape[0]

    def body(mem_ref, g_ref, w_ref, mn_ref, kv_ref):
        mv = mem_ref[...]
        mn = (mv * _rms_stats(mv) * g_ref[...]).astype(BF16)
        mn_ref[...] = mn
        for j in range(4):
            kv_ref[:, j * 512:(j + 1) * 512] = _dot(mn, w_ref[j]).astype(BF16)

    return pl.pallas_call(
        body, name="mem_kv",
        out_shape=[jax.ShapeDtypeStruct((m, D_MODEL), BF16), jax.ShapeDtypeStruct((m, 2 * D_MODEL), BF16)],
        compiler_params=pltpu.CompilerParams(vmem_limit_bytes=VMEM_LIMIT_BYTES),
    )(mem, g, wkv)


def _softmax_rows(sc):
    e = jnp.exp(sc - jnp.max(sc, axis=-1, keepdims=True))
    return e / jnp.sum(e, axis=-1, keepdims=True)


def _attn_fwd(q, kv, h1, wo, g, ts):
    s, m = q.shape[0], kv.shape[0]
    scale = XA_HEAD_DIM ** -0.5

    def body(q_ref, kv_ref, h1_ref, wo_ref, g_ref, o_ref, h2_ref, hn_ref):
        for h in range(XA_HEADS):
            cols = slice(h * XA_HEAD_DIM, (h + 1) * XA_HEAD_DIM)
            vcols = slice(D_MODEL + h * XA_HEAD_DIM, D_MODEL + (h + 1) * XA_HEAD_DIM)
            p = _softmax_rows(_dot_nt(q_ref[:, cols], kv_ref[:, cols]) * scale)
            o_ref[:, cols] = _dot(p.astype(BF16), kv_ref[:, vcols]).astype(BF16)
        h2 = h1_ref[...] + _dot(o_ref[...], wo_ref[...])
        h2_ref[...] = h2
        hn_ref[...] = (h2 * _rms_stats(h2) * g_ref[...]).astype(BF16)

    row = pl.BlockSpec((ts, D_MODEL), lambda i: (i, 0))
    return pl.pallas_call(
        body, name="attn_fwd", grid=(s // ts,),
        in_specs=[row, pl.BlockSpec((m, 2 * D_MODEL), lambda i: (0, 0)), row,
                  pl.BlockSpec((D_MODEL, D_MODEL), lambda i: (0, 0)),
                  pl.BlockSpec((1, D_MODEL), lambda i: (0, 0))],
        out_specs=[row, row, row],
        out_shape=[jax.ShapeDtypeStruct((s, D_MODEL), BF16), jax.ShapeDtypeStruct((s, D_MODEL), F32),
                   jax.ShapeDtypeStruct((s, D_MODEL), BF16)],
        compiler_params=_params(("parallel",)),
    )(q, kv, h1, wo, g)


_FFN_CHUNKS_FWD = (slice(0, 6 * LANES), slice(6 * LANES, FFN_HALF))
_FFN_CHUNKS_BWD = tuple(slice(lo, min(lo + 3 * LANES, FFN_HALF)) for lo in range(0, FFN_HALF, 3 * LANES))


def _ffn_up(hn, wgu, ts, after=()):
    s = hn.shape[0]

    def body(hn_ref, w_ref, gu_ref, act_ref):
        hv = hn_ref[...]
        for cols in _FFN_CHUNKS_FWD:
            gate = _dot(hv, w_ref[0, 0, :, cols])
            up = _dot(hv, w_ref[1, 0, :, cols])
            gu_ref[0, :, cols] = gate.astype(BF16)
            gu_ref[1, :, cols] = up.astype(BF16)
            act_ref[:, cols] = (gate * _sigmoid(gate) * up).astype(BF16)

    return _tied_call(
        body, after, name="ffn_up", grid=(2, s // ts),
        in_specs=[pl.BlockSpec((ts, D_MODEL), lambda j, i: (i, 0)),
                  pl.BlockSpec((2, 1, D_MODEL, FFN_HALF), lambda j, i: (0, j, 0, 0))],
        out_specs=[pl.BlockSpec((2, ts, FFN_HALF), lambda j, i: (0, i, j)),
                   pl.BlockSpec((ts, FFN_HALF), lambda j, i: (i, j))],
        out_shape=[jax.ShapeDtypeStruct((2, s, FFN_HIDDEN), BF16), jax.ShapeDtypeStruct((s, FFN_HIDDEN), BF16)],
        compiler_params=_params(("parallel", "parallel")),
    )(hn, wgu)


def _ffn_down_loss(act, wd, h2, g, target, ts):
    s = act.shape[0]

    def body(act_ref, wd_ref, h2_ref, g_ref, t_ref, dh_ref, dhb_ref, sq_ref, dg_ref):
        @pl.when(pl.program_id(0) == 0)
        def _():
            sq_ref[...] = jnp.zeros_like(sq_ref)
            dg_ref[...] = jnp.zeros_like(dg_ref)

        h3 = h2_ref[...] + _dot(act_ref[...], wd_ref[...])
        r = _rms_stats(h3)
        gv = g_ref[...]
        diff = h3 * r * gv - t_ref[...]
        sq_ref[...] += _rowsum(diff * diff)
        dh, dg = _rms_bwd(diff / D_MODEL, h3, r, gv)
        dh_ref[...] = dh
        dhb_ref[...] = dh.astype(BF16)
        dg_ref[...] += dg

    row = pl.BlockSpec((ts, D_MODEL), lambda i: (i, 0))
    vec = pl.BlockSpec((1, D_MODEL), lambda i: (0, 0))
    return pl.pallas_call(
        body, name="ffn_down_loss", grid=(s // ts,),
        in_specs=[pl.BlockSpec((ts, FFN_HIDDEN), lambda i: (i, 0)),
                  pl.BlockSpec((FFN_HIDDEN, D_MODEL), lambda i: (0, 0)), row, vec, row],
        out_specs=[row, row, vec, vec],
        out_shape=[jax.ShapeDtypeStruct((s, D_MODEL), F32), jax.ShapeDtypeStruct((s, D_MODEL), BF16),
                   jax.ShapeDtypeStruct((1, D_MODEL), F32), jax.ShapeDtypeStruct((1, D_MODEL), F32)],
        compiler_params=_params(("arbitrary",)),
    )(act, wd, h2, g, target)


def _grad_w(a, b, tk, tn, name, after=()):
    s, k = a.shape
    gb, _, n = b.shape
    nblk = n // tn
    tsr = GRAD_ROWS if s % GRAD_ROWS == 0 else s

    def body(a_ref, b_ref, o_ref):
        part = _dot_tn(a_ref[...], b_ref[0])

        @pl.when(pl.program_id(2) == 0)
        def _():
            o_ref[0] = part

        @pl.when(pl.program_id(2) > 0)
        def _():
            o_ref[0] += part

    return _tied_call(
        body, after, name=name, grid=(gb * nblk, k // tk, s // tsr),
        in_specs=[pl.BlockSpec((tsr, tk), lambda ni, ki, si: (si, ki)),
                  pl.BlockSpec((1, tsr, tn), lambda ni, ki, si: (ni // nblk, si, ni % nblk))],
        out_specs=pl.BlockSpec((1, tk, tn), lambda ni, ki, si: (ni, ki, 0)),
        out_shape=jax.ShapeDtypeStruct((gb * nblk, k, tn), F32),
        compiler_params=_params(("parallel", "parallel", "arbitrary")),
    )(a, b)


def _ffn_bwd_act(dh3, wd, gu, ts):
    s = dh3.shape[0]

    def body(dh_ref, wd_ref, gu_ref, dgu_ref):
        dh = dh_ref[...]
        for cols in _FFN_CHUNKS_BWD:
            dact = _dot_nt(dh, wd_ref[0, cols, :])
            gate, up = gu_ref[0, :, cols].astype(F32), gu_ref[1, :, cols].astype(F32)
            sg = _sigmoid(gate)
            dgu_ref[0, :, cols] = (dact * up * (sg * (1.0 + gate * (1.0 - sg)))).astype(BF16)
            dgu_ref[1, :, cols] = (dact * (gate * sg)).astype(BF16)

    return pl.pallas_call(
        body, name="ffn_bwd_act", grid=(2, s // ts),
        in_specs=[pl.BlockSpec((ts, D_MODEL), lambda j, i: (i, 0)),
                  pl.BlockSpec((1, FFN_HALF, D_MODEL), lambda j, i: (j, 0, 0)),
                  pl.BlockSpec((2, ts, FFN_HALF), lambda j, i: (0, i, j))],
        out_specs=pl.BlockSpec((2, ts, FFN_HALF), lambda j, i: (0, i, j)),
        out_shape=jax.ShapeDtypeStruct((2, s, FFN_HIDDEN), BF16),
        compiler_params=_params(("parallel", "parallel")),
    )(dh3, wd, gu)


def _ffn_bwd_in(dgu, wgu, dh3, h2, g, ts, after=()):
    s = dh3.shape[0]

    def body(dgu_ref, w_ref, dh3_ref, h2_ref, g_ref, dh2_ref, dh2b_ref, dg_ref):
        @pl.when(pl.program_id(0) == 0)
        def _():
            dg_ref[...] = jnp.zeros_like(dg_ref)

        dhn = jnp.zeros((ts, D_MODEL), F32)
        for p in range(2):
            for j in range(2):
                dhn = dhn + _dot_nt(dgu_ref[p, :, j * FFN_HALF:(j + 1) * FFN_HALF], w_ref[2 * p + j])
        h2 = h2_ref[...]
        dv, dg = _rms_bwd(dhn, h2, _rms_stats(h2), g_ref[...])
        dh2 = dh3_ref[...] + dv
        dh2_ref[...] = dh2
        dh2b_ref[...] = dh2.astype(BF16)
        dg_ref[...] += dg

    row = pl.BlockSpec((ts, D_MODEL), lambda i: (i, 0))
    vec = pl.BlockSpec((1, D_MODEL), lambda i: (0, 0))
    return _tied_call(
        body, after, name="ffn_bwd_in", grid=(s // ts,),
        in_specs=[pl.BlockSpec((2, ts, FFN_HIDDEN), lambda i: (0, i, 0)),
                  pl.BlockSpec((4, D_MODEL, FFN_HALF), lambda i: (0, 0, 0)), row, row, vec],
        out_specs=[row, row, vec],
        out_shape=[jax.ShapeDtypeStruct((s, D_MODEL), F32), jax.ShapeDtypeStruct((s, D_MODEL), BF16),
                   jax.ShapeDtypeStruct((1, D_MODEL), F32)],
        compiler_params=_params(("arbitrary",)),
    )(dgu, wgu, dh3, h2, g)


def _attn_bwd(dh2, wo, q, kv, wq, h1, g, ts, after=()):
    s, m = q.shape[0], kv.shape[0]
    scale = XA_HEAD_DIM ** -0.5

    def body(dh2_ref, wo_ref, q_ref, kv_ref, wq_ref, h1_ref, g_ref, dh1_ref, dh1b_ref, dq_ref, dkv_ref, dg_ref):
        @pl.when(pl.program_id(0) == 0)
        def _():
            dkv_ref[...] = jnp.zeros_like(dkv_ref)
            dg_ref[...] = jnp.zeros_like(dg_ref)

        do = _dot_nt(dh2_ref[...].astype(BF16), wo_ref[...]).astype(BF16)
        for h in range(XA_HEADS):
            cols = slice(h * XA_HEAD_DIM, (h + 1) * XA_HEAD_DIM)
            vcols = slice(D_MODEL + h * XA_HEAD_DIM, D_MODEL + (h + 1) * XA_HEAD_DIM)
            qh, kh, vh, doh = q_ref[:, cols], kv_ref[:, cols], kv_ref[:, vcols], do[:, cols]
            p = _softmax_rows(_dot_nt(qh, kh) * scale)
            dp = _dot_nt(doh, vh)
            ds = (p * (dp - jnp.sum(dp * p, axis=-1, keepdims=True)) * scale).astype(BF16)
            dq_ref[:, cols] = _dot(ds, kh).astype(BF16)
            dkv_ref[:, cols] += _dot_tn(ds, qh)
            dkv_ref[:, vcols] += _dot_tn(p.astype(BF16), doh)
        dhn = _dot_nt(dq_ref[...], wq_ref[...])
        h1 = h1_ref[...]
        dv, dg = _rms_bwd(dhn, h1, _rms_stats(h1), g_ref[...])
        dh1 = dh2_ref[...] + dv
        dh1_ref[...] = dh1
        dh1b_ref[...] = dh1.astype(BF16)
        dg_ref[...] += dg

    row = pl.BlockSpec((ts, D_MODEL), lambda i: (i, 0))
    full = pl.BlockSpec((D_MODEL, D_MODEL), lambda i: (0, 0))
    kvs = pl.BlockSpec((m, 2 * D_MODEL), lambda i: (0, 0))
    vec = pl.BlockSpec((1, D_MODEL), lambda i: (0, 0))
    return _tied_call(
        body, after, name="attn_bwd", grid=(s // ts,),
        in_specs=[row, full, row, kvs, full, row, vec],
        out_specs=[row, row, row, kvs, vec],
        out_shape=[jax.ShapeDtypeStruct((s, D_MODEL), F32), jax.ShapeDtypeStruct((s, D_MODEL), BF16),
                   jax.ShapeDtypeStruct((s, D_MODEL), BF16),
                   jax.ShapeDtypeStruct((m, 2 * D_MODEL), F32), jax.ShapeDtypeStruct((1, D_MODEL), F32)],
        compiler_params=_params(("arbitrary",)),
    )(dh2, wo, q, kv, wq, h1, g)


def _mem_kv_bwd(dkv, mn, wkv, mem, g, after=()):
    m = mem.shape[0]

    def body(dkv_ref, mn_ref, w_ref, mem_ref, g_ref, dw_ref, dg_ref):
        dmn = jnp.zeros((m, D_MODEL), F32)
        mn = mn_ref[...]
        for j in range(4):
            dj = dkv_ref[:, j * 512:(j + 1) * 512].astype(BF16)
            dw_ref[j] = _dot_tn(mn, dj)
            dmn = dmn + _dot_nt(dj, w_ref[j])
        mv = mem_ref[...]
        dg_ref[...] = _rowsum(dmn * (mv * _rms_stats(mv)))

    return _tied_call(
        body, after, name="mem_kv_bwd", in_specs=[pl.BlockSpec(memory_space=pltpu.VMEM)] * 5,
        out_shape=[jax.ShapeDtypeStruct((4, D_MODEL, 512), F32), jax.ShapeDtypeStruct((1, D_MODEL), F32)],
        compiler_params=pltpu.CompilerParams(vmem_limit_bytes=VMEM_LIMIT_BYTES),
    )(dkv, mn, wkv, mem, g)


def _seqmix_bwd(dh1, x, z, c1, w_out, w_in, g_mix, cw, lng, lnb, gg, gb, wpair, wpair_t, bias, t, after=()):
    s = x.shape[0]
    nt = s // t
    halo_blocks = t // CONV_HALO

    def body(dh1_ref, x_ref, z_ref, zh_ref, c1_ref, wo_ref, wi_ref, gm_ref, cw_ref, lng_ref, lnb_ref,
             gg_ref, gb_ref, wpair_ref, wpt_ref, bias_ref,
             gx_ref, dz_ref, dcw_ref, dcb_ref, dlng_ref, dlnb_ref, dgg_ref, dgb_ref, dws_ref, dbs_ref,
             dbin_ref, dgm_ref, abuf, dbuf, ash, dsh, mixed_ref, dv_ref):
        i = pl.program_id(0)
        tile = nt - 1 - i
        accs = (dcw_ref, dcb_ref, dlng_ref, dlnb_ref, dgg_ref, dgb_ref, dws_ref, dbs_ref, dbin_ref, dgm_ref)

        @pl.when(i == 0)
        def _():
            for r in accs:
                r[...] = jnp.zeros_like(r)
            dbuf[t:t + CONV_HALO, :] = jnp.zeros((CONV_HALO, CONV_WIDTH), F32)

        @pl.when(i > 0)
        def _():
            dbuf[t:t + CONV_HALO, :] = dbuf[0:CONV_HALO, :]

        dmix = _dot_nt(dh1_ref[...].astype(BF16), wo_ref[...])

        xh, rs = _ln_stats(c1_ref[...])
        lng = lng_ref[...]
        ln = xh * lng + lnb_ref[...]
        sl = _sigmoid(ln)
        dln = dmix[:, 0:512] * (sl * (1.0 + ln * (1.0 - sl)))
        dc1, dg_ln, db_ln = _ln_bwd(dln, xh, rs, lng)
        dlng_ref[...] += dg_ln
        dlnb_ref[...] += db_ln
        dcb_ref[...] += _rowsum(dc1)
        dbuf[0:t, :] = dc1

        zh = zh_ref[...]
        a_halo = zh[:, 0:512] * _sigmoid(zh[:, 512:1024])
        abuf[0:CONV_HALO, :] = jnp.where(tile > 0, a_halo, 0.0)
        za = z_ref[:, 0:512]
        sg = _sigmoid(z_ref[:, 512:1024])
        abuf[CONV_HALO:, :] = za * sg
        _shift_rows(abuf, ash, t)
        _shift_rows(dbuf, dsh, t)

        da = jnp.zeros((t, CONV_WIDTH), F32)
        for k in range(CONV_KERNEL):
            da = da + cw_ref[k:k + 1, :] * _window(dbuf, dsh, CONV_KERNEL - 1 - k, t)
            dcw_ref[k:k + 1, :] += _rowsum(dc1 * _window(abuf, ash, CONV_HALO - (CONV_KERNEL - 1) + k, t))
        dza = da * sg
        dzg = da * za * (sg * (1.0 - sg))
        dz_ref[:, 0:512] = dza.astype(BF16)
        dz_ref[:, 512:1024] = dzg.astype(BF16)
        dbin_ref[:, 0:512] += _rowsum(dza)
        dbin_ref[:, 512:1024] += _rowsum(dzg)

        dgm = dmix[:, 512:1024]
        u, du_dz = _gelu_parts(z_ref[:, 1024:1536])
        gv, dgv_dz = _gelu_parts(z_ref[:, 1536:2048])
        vxh, vrs = _ln_stats(gv)
        ggv = gg_ref[...]
        v = vxh * ggv + gb_ref[...]
        low = _lane_is_low_head()
        v_lo = jnp.where(low, v, 0.0).astype(BF16)
        v_hi = jnp.where(low, 0.0, v).astype(BF16)
        _gm_mix(v_lo, v_hi, wpair_ref, bias_ref, mixed_ref, t)
        dzu = dgm * mixed_ref[...] * du_dz
        dm = dgm * u
        dm_lo = jnp.where(low, dm, 0.0).astype(BF16)
        dm_hi = jnp.where(low, 0.0, dm).astype(BF16)
        vb = v.astype(BF16)
        tril = (lax.broadcasted_iota(jnp.int32, (CHUNK, CHUNK), 1)
                <= lax.broadcasted_iota(jnp.int32, (CHUNK, CHUNK), 0))
        for n in range(t // CHUNK):
            rows = slice(n * CHUNK, (n + 1) * CHUNK)
            dbs_ref[...] += dm[rows, :]
            for j in range(GM_HEADS // 2):
                cols = slice(j * LANES, (j + 1) * LANES)
                stack = jnp.concatenate([dm_lo[rows, cols], dm_hi[rows, cols]], axis=0)
                dws = _dot_nt(stack, vb[rows, cols])
                dws_ref[2 * j] += jnp.where(tril, dws[0:CHUNK], 0.0)
                dws_ref[2 * j + 1] += jnp.where(tril, dws[CHUNK:2 * CHUNK], 0.0)
                dv_ref[rows, cols] = _dot(wpt_ref[j], stack)
        dgv, dg_gm, db_gm = _ln_bwd(dv_ref[...], vxh, vrs, ggv)
        dgg_ref[...] += dg_gm
        dgb_ref[...] += db_gm
        dzv = dgv * dgv_dz
        dz_ref[:, 1024:1536] = dzu.astype(BF16)
        dz_ref[:, 1536:2048] = dzv.astype(BF16)
        dbin_ref[:, 1024:1536] += _rowsum(dzu)
        dbin_ref[:, 1536:2048] += _rowsum(dzv)

        dhn = jnp.zeros((t, D_MODEL), F32)
        for j in range(4):
            dhn = dhn + _dot_nt(dz_ref[:, j * 512:(j + 1) * 512], wi_ref[j])
        xv = x_ref[...]
        dv, dg = _rms_bwd(dhn, xv, _rms_stats(xv), gm_ref[...])
        gx_ref[...] = dh1_ref[...] + dv
        dgm_ref[...] += dg

    rev = lambda w: pl.BlockSpec((t, w), lambda i: (nt - 1 - i, 0))
    const = lambda *shape: pl.BlockSpec(shape, lambda i: (0,) * len(shape))
    halo = pl.BlockSpec((CONV_HALO, D_MODEL), lambda i: (jnp.maximum((nt - 1 - i) * halo_blocks - 1, 0), 0))
    f32 = lambda *shape: jax.ShapeDtypeStruct(shape, F32)
    return _tied_call(
        body, after, name="seqmix_bwd", grid=(nt,),
        in_specs=[rev(D_MODEL), rev(D_MODEL), rev(2048), halo, rev(CONV_WIDTH),
                  const(D_MODEL, D_MODEL), const(4, D_MODEL, 512), const(1, D_MODEL),
                  const(CONV_HALO, CONV_WIDTH), const(1, 512), const(1, 512), const(1, 512), const(1, 512),
                  const(4, CHUNK, 2 * CHUNK), const(4, CHUNK, 2 * CHUNK), const(CHUNK, GM_WIDTH)],
        out_specs=[rev(D_MODEL), rev(2048),
                   const(CONV_HALO, CONV_WIDTH), const(1, 512), const(1, 512), const(1, 512), const(1, 512),
                   const(1, 512), const(GM_HEADS, CHUNK, CHUNK), const(CHUNK, GM_WIDTH), const(1, 2048),
                   const(1, D_MODEL)],
        out_shape=[f32(s, D_MODEL), jax.ShapeDtypeStruct((s, 2048), BF16),
                   f32(CONV_HALO, CONV_WIDTH), f32(1, 512), f32(1, 512), f32(1, 512), f32(1, 512),
                   f32(1, 512), f32(GM_HEADS, CHUNK, CHUNK), f32(CHUNK, GM_WIDTH), f32(1, 2048),
                   f32(1, D_MODEL)],
        scratch_shapes=[pltpu.VMEM((t + CONV_HALO, CONV_WIDTH), F32), pltpu.VMEM((t + CONV_HALO, CONV_WIDTH), F32),
                        pltpu.VMEM((SUBLANES - 1, t + CONV_HALO - SUBLANES, CONV_WIDTH), F32),
                        pltpu.VMEM((SUBLANES - 1, t + CONV_HALO - SUBLANES, CONV_WIDTH), F32),
                        pltpu.VMEM((t, GM_WIDTH), F32), pltpu.VMEM((t, GM_WIDTH), F32)],
        compiler_params=_params(("arbitrary",)),
    )(dh1, x, z, z, c1, w_out, w_in, g_mix, cw, lng, lnb, gg, gb, wpair, wpair_t, bias)


def _head_bias_grad(dbs):
    def body(d_ref, o_ref):
        dv = d_ref[...]
        lane = lax.broadcasted_iota(jnp.int32, (CHUNK, LANES), 1)
        acc = jnp.zeros((CHUNK, LANES), F32)
        for h in range(GM_HEADS):
            sh = jnp.sum(dv[:, h * GM_HEAD_DIM:(h + 1) * GM_HEAD_DIM], axis=-1, keepdims=True)
            acc = acc + jnp.where(lane == h, sh, 0.0)
        o_ref[...] = acc

    return pl.pallas_call(body, name="head_bias_grad",
                          out_shape=jax.ShapeDtypeStruct((CHUNK, LANES), F32))(dbs)


def kernel(x, mem, norm_mix_g, w_in, b_in, conv_w, conv_b, conv_ln_g, conv_ln_b, gm_ln_g, gm_ln_b, gm_w_s, gm_b_s, w_out, norm_xa_g, mem_norm_g, xa_wq, xa_wkv, xa_wo, norm_ffn_g, ffn_w_gate_up, ffn_w_down, final_norm_g, loss_target, m_norm_mix_g, m_w_in, m_b_in, m_conv_w, m_conv_b, m_conv_ln_g, m_conv_ln_b, m_gm_ln_g, m_gm_ln_b, m_gm_w_s, m_gm_b_s, m_w_out, m_norm_xa_g, m_mem_norm_g, m_xa_wq, m_xa_wkv, m_xa_wo, m_norm_ffn_g, m_ffn_w_gate_up, m_ffn_w_down, m_final_norm_g, v_norm_mix_g, v_w_in, v_b_in, v_conv_w, v_conv_b, v_conv_ln_g, v_conv_ln_b, v_gm_ln_g, v_gm_ln_b, v_gm_w_s, v_gm_b_s, v_w_out, v_norm_xa_g, v_mem_norm_g, v_xa_wq, v_xa_wkv, v_xa_wo, v_norm_ffn_g, v_ffn_w_gate_up, v_ffn_w_down, v_final_norm_g):
    weights = dict(norm_mix_g=norm_mix_g, w_in=w_in, b_in=b_in, conv_w=conv_w, conv_b=conv_b, conv_ln_g=conv_ln_g,
                   conv_ln_b=conv_ln_b, gm_ln_g=gm_ln_g, gm_ln_b=gm_ln_b, gm_w_s=gm_w_s, gm_b_s=gm_b_s, w_out=w_out,
                   norm_xa_g=norm_xa_g, mem_norm_g=mem_norm_g, xa_wq=xa_wq, xa_wkv=xa_wkv, xa_wo=xa_wo,
                   norm_ffn_g=norm_ffn_g, ffn_w_gate_up=ffn_w_gate_up, ffn_w_down=ffn_w_down,
                   final_norm_g=final_norm_g)
    m_in = dict(norm_mix_g=m_norm_mix_g, w_in=m_w_in, b_in=m_b_in, conv_w=m_conv_w, conv_b=m_conv_b,
                conv_ln_g=m_conv_ln_g, conv_ln_b=m_conv_ln_b, gm_ln_g=m_gm_ln_g, gm_ln_b=m_gm_ln_b, gm_w_s=m_gm_w_s,
                gm_b_s=m_gm_b_s, w_out=m_w_out, norm_xa_g=m_norm_xa_g, mem_norm_g=m_mem_norm_g, xa_wq=m_xa_wq,
                xa_wkv=m_xa_wkv, xa_wo=m_xa_wo, norm_ffn_g=m_norm_ffn_g, ffn_w_gate_up=m_ffn_w_gate_up,
                ffn_w_down=m_ffn_w_down, final_norm_g=m_final_norm_g)
    v_in = dict(norm_mix_g=v_norm_mix_g, w_in=v_w_in, b_in=v_b_in, conv_w=v_conv_w, conv_b=v_conv_b,
                conv_ln_g=v_conv_ln_g, conv_ln_b=v_conv_ln_b, gm_ln_g=v_gm_ln_g, gm_ln_b=v_gm_ln_b, gm_w_s=v_gm_w_s,
                gm_b_s=v_gm_b_s, w_out=v_w_out, norm_xa_g=v_norm_xa_g, mem_norm_g=v_mem_norm_g, xa_wq=v_xa_wq,
                xa_wkv=v_xa_wkv, xa_wo=v_xa_wo, norm_ffn_g=v_norm_ffn_g, ffn_w_gate_up=v_ffn_w_gate_up,
                ffn_w_down=v_ffn_w_down, final_norm_g=v_final_norm_g)
    grads, delta, new_m, new_v = {}, {}, {}, {}

    s = x.shape[1]
    ts = _row_tile(s)
    tb = max(CHUNK, ts // 2)
    tw = 2 * ts if s % (2 * ts) == 0 and ts >= 512 else ts
    cx, cy, cc = _mesh_pos()
    chip = 2 * cx + cy
    pos = jnp.stack([chip, cc]).astype(jnp.int32)
    row = lambda a: a.reshape(1, -1)
    x2, mem2, tgt2 = x[0], mem[0], loss_target[0]

    big = dict(w_in=w_in, xa_wkv=xa_wkv, w_out=w_out, xa_wq=xa_wq, xa_wo=xa_wo,
               ffn_w_gate_up=ffn_w_gate_up, ffn_w_down=ffn_w_down)
    big_names = list(big)
    halves = lambda a: a.reshape(2, a.shape[0] // 2, a.shape[1])
    cast = {nm: _cast_into_slot(halves(big[nm]), pos, BF16, "cast_" + nm) for nm in big_names}
    conv_w_pad = jnp.pad(conv_w, ((0, CONV_HALO - CONV_KERNEL), (0, 0)))
    cast["conv_w"] = _cast_into_slot(halves(conv_w_pad), pos, F32, "slot_conv_w")

    def start_gather(names, after):
        return _gather_start([cast[nm] for nm in names], "gather_start_" + names[0], after)

    def finish_gather(names, started, after):
        send_sems, recv_sems, bufs, _ = started
        landed = _gather_wait(send_sems, recv_sems, bufs, after, "gather_wait_" + names[0])
        return dict(zip(names, (b.reshape(N_CHIPS, -1, b.shape[-1])
                                for b in _pass_to_sibling(landed, "pass_" + names[0]))))

    first_names = ["w_in", "conv_w"]
    attn_names = ["w_out", "xa_wq", "xa_wkv", "xa_wo"]
    gather_first = start_gather(first_names, ())
    gw = finish_gather(first_names, gather_first, [cast[nm] for nm in big_names if nm != "w_in"])
    w_in_g = gw["w_in"]
    cw_g = jnp.concatenate([gw["conv_w"][k] for k in range(N_CHIPS)], axis=1)
    gather_attn = start_gather(attn_names, w_in_g)

    tril = jnp.tril(jnp.ones((CHUNK, CHUNK), dtype=bool))
    ws = jnp.where(tril[None], gm_w_s, 0.0)
    wpair = jnp.concatenate([ws[0::2], ws[1::2]], axis=2).astype(BF16)
    ws_t = jnp.swapaxes(ws, 1, 2)
    wpair_t = jnp.concatenate([ws_t[0::2], ws_t[1::2]], axis=2).astype(BF16)
    bias = jnp.repeat(gm_b_s.T, GM_HEAD_DIM, axis=1)

    z, hn1 = _mix_in(x2, row(norm_mix_g), w_in_g, row(b_in), tw, after=gather_attn[3])
    mix, c1 = _seqmix_fwd(z, cw_g, row(conv_b), row(conv_ln_g), row(conv_ln_b), row(gm_ln_g), row(gm_ln_b),
                          wpair, bias, ts)
    gw = finish_gather(attn_names, gather_attn, mix)
    w_out_g = gw["w_out"].reshape(D_MODEL, D_MODEL)
    wq_g = gw["xa_wq"].reshape(D_MODEL, D_MODEL)
    wkv_g = gw["xa_wkv"]
    wo_g = gw["xa_wo"].reshape(D_MODEL, D_MODEL)
    gather_gu = start_gather(["ffn_w_gate_up"], w_out_g)
    h1, hn2, q = _out_proj_q(x2, mix, w_out_g, row(norm_xa_g), wq_g, tw, after=gather_gu[3])
    mn, kv = _mem_kv(mem2, row(mem_norm_g), wkv_g)
    o, h2, hn3 = _attn_fwd(q, kv, h1, wo_g, row(norm_ffn_g), ts)
    wgu_g = finish_gather(["ffn_w_gate_up"], gather_gu, hn3)["ffn_w_gate_up"]
    gather_down = start_gather(["ffn_w_down"], wgu_g)
    gu, act = _ffn_up(hn3, wgu_g.reshape(2, 2, D_MODEL, FFN_HALF), tw, after=gather_down[3])
    wd_g = finish_gather(["ffn_w_down"], gather_down, act)["ffn_w_down"].reshape(FFN_HIDDEN, D_MODEL)
    dh3, dh3_b, sq, d_final_g = _ffn_down_loss(act, wd_g, h2, row(final_norm_g), tgt2, ts)
    loss_here = jnp.broadcast_to(0.5 * jnp.sum(sq) / D_MODEL, (1, 2, SUBLANES, LANES))

    def split(g, nm):
        r, c = big[nm].shape
        return g.reshape(N_CHIPS, 2, r // 2, c)

    def chip_sums_of(group, arrays):
        got = _swap_halves(arrays, "swap_halves_" + group[0])
        both = [_add_half(g, r, pos, "chip_sum_" + nm, F32 if nm == "loss" else BF16)
                for g, r, nm in zip(arrays, got, group)]
        return [b[0] for b in both], [b[1] for b in both]

    def start_swap(group, grads):
        return _swap_start([split(g, nm) for g, nm in zip(grads, group)], "swap_start_" + group[0])

    def start_exchange(group, swapping, after):
        sems, arrays, lands, _ = swapping
        arrays, got = _swap_wait(sems, arrays, lands, after, "swap_wait_" + group[0])
        both = [_add_half(g, r, pos, "chip_sum_" + nm) for g, r, nm in zip(arrays, got, group)]
        return _exchange_start([b[0] for b in both], [b[1] for b in both], "exchange_start_" + group[0])

    def finish_exchange(group, started, after):
        sems, sums, parts, _ = started
        parts = _exchange_wait(sems, sums, parts, after, "exchange_wait_" + group[0])
        return [_sum_chips(p, pos, "total_" + nm) for p, nm in zip(parts, group)]

    def join_and_update(group, after):
        joined = _join_halves([halves_of[nm] for nm in group], "join_halves_" + group[0], after)
        for nm, j in zip(group, joined):
            grads[nm] = j.reshape(big[nm].shape)
            delta[nm], new_m[nm], new_v[nm] = _adamw(weights[nm], grads[nm], m_in[nm], v_in[nm], "adamw_" + nm)
        return [new_v[nm] for nm in group]

    as3 = lambda a: a.reshape((1,) + a.shape)
    halves_of = {}

    dgu = _ffn_bwd_act(dh3_b, wd_g.reshape(2, FFN_HALF, D_MODEL), gu, tw)
    g_down = _grad_w(act, as3(dh3_b), FFN_HALF, D_MODEL, "grad_ffn_w_down")
    group_a = ["ffn_w_down"]
    swap_a = start_swap(group_a, [g_down])
    dh2, dh2_b, d_ffn_g = _ffn_bwd_in(dgu, wgu_g, dh3, h2, row(norm_ffn_g), ts, after=swap_a[3])
    exch_a = start_exchange(group_a, swap_a, dh2)
    g_gu = _grad_w(hn3, dgu, D_MODEL, FFN_HALF, "grad_ffn_w_gate_up", after=exch_a[3])
    halves_of.update(zip(group_a, finish_exchange(group_a, exch_a, g_gu)))

    group_b = ["ffn_w_gate_up"]
    swap_b = start_swap(group_b, [g_gu])
    dh1, dh1_b, dq, dkv, d_xa_g = _attn_bwd(dh2, wo_g, q, kv, wq_g, h1, row(norm_xa_g), ts, after=swap_b[3])
    exch_b = start_exchange(group_b, swap_b, dh1)
    g_wkv, d_mem_g = _mem_kv_bwd(dkv, mn, wkv_g, mem2, row(mem_norm_g), after=exch_b[3])
    g_wo = _grad_w(o, as3(dh2_b), D_MODEL, D_MODEL, "grad_xa_wo", after=exch_b[3])
    g_wq = _grad_w(hn2, as3(dq), D_MODEL, D_MODEL, "grad_xa_wq", after=exch_b[3])
    g_wout = _grad_w(mix, as3(dh1_b), D_MODEL, D_MODEL, "grad_w_out", after=exch_b[3])
    halves_of.update(zip(group_b, finish_exchange(group_b, exch_b, (g_wkv, g_wo, g_wq, g_wout))))

    group_c = ["xa_wo", "xa_wq", "xa_wkv", "w_out"]
    swap_c = start_swap(group_c, [g_wo, g_wq, g_wkv, g_wout])
    (gx, dz, d_cw, d_cb, d_lng, d_lnb, d_gg, d_gb, d_ws, d_bs_sum, d_bin, d_mix_g) = _seqmix_bwd(
        dh1, x2, z, c1, w_out_g, w_in_g, row(norm_mix_g), cw_g, row(conv_ln_g), row(conv_ln_b),
        row(gm_ln_g), row(gm_ln_b), wpair, wpair_t, bias, tb, after=swap_c[3])
    d_bs = _head_bias_grad(d_bs_sum)[:, :GM_HEADS].T
    exch_c = start_exchange(group_c, swap_c, dz)
    g_win = _grad_w(hn1, as3(dz), D_MODEL, 512, "grad_w_in", after=exch_c[3])
    done_ab = join_and_update(group_a + group_b, g_win)
    halves_of.update(zip(group_c, finish_exchange(group_c, exch_c, (g_win, *done_ab))))

    small_names = ["norm_mix_g", "b_in", "conv_w", "conv_b", "conv_ln_g", "conv_ln_b", "gm_ln_g", "gm_ln_b",
                   "gm_w_s", "gm_b_s", "norm_xa_g", "mem_norm_g", "norm_ffn_g", "final_norm_g"]
    d_cw_by_chip = jnp.swapaxes(d_cw.reshape(CONV_HALO, N_CHIPS, LANES), 0, 1).reshape(-1, LANES)
    small_grads = dict(norm_mix_g=d_mix_g, b_in=d_bin, conv_w=d_cw_by_chip, conv_b=d_cb, conv_ln_g=d_lng,
                       conv_ln_b=d_lnb, gm_ln_g=d_gg, gm_ln_b=d_gb, gm_w_s=d_ws, gm_b_s=d_bs, norm_xa_g=d_xa_g,
                       mem_norm_g=d_mem_g, norm_ffn_g=d_ffn_g, final_norm_g=d_final_g)

    def rows_form(a):
        a = a.reshape(-1, LANES)
        return jnp.pad(a, ((0, -a.shape[0] % SUBLANES), (0, 0)))

    pieces = [rows_form(small_grads[nm]) for nm in small_names]
    offsets, total = [], 0
    for p in pieces:
        offsets.append(total)
        total += p.shape[0]
    pack_rows = -(-total // 32) * 32
    small_pack = jnp.pad(jnp.concatenate(pieces, axis=0), ((0, pack_rows - total), (0, 0)))

    group_d = ["w_in", "small", "loss"]
    sums_d, parts_d = chip_sums_of(group_d, [split(g_win, "w_in"),
                                             small_pack.reshape(1, 2, pack_rows // 2, LANES), loss_here])
    exch_d = _exchange_start(sums_d, parts_d, "exchange_start_w_in")
    done_c = join_and_update(group_c, exch_d[3])
    halves_of.update(zip(group_d, finish_exchange(group_d, exch_d, done_c)))
    joined_d = _join_halves([halves_of[nm] for nm in group_d], "join_halves_w_in")
    grads["w_in"] = joined_d[0].reshape(w_in.shape)
    loss = joined_d[2][0, 0, 0]
    delta["w_in"], new_m["w_in"], new_v["w_in"] = _adamw(w_in, grads["w_in"], m_w_in, v_w_in, "adamw_w_in")

    local_rows = lambda a, nm: a if nm == "conv_w" else a.reshape(-1, LANES)
    params = [tuple(local_rows(src[nm], nm) for src in (weights, m_in, v_in)) for nm in small_names]
    outs = _adamw_small(joined_d[1].reshape(pack_rows, LANES), pos, params, offsets, small_names.index("conv_w"))
    for k, nm in enumerate(small_names):
        for dst, a in zip((grads, delta, new_m, new_v), outs[4 * k:4 * k + 4]):
            dst[nm] = a

    order = ["norm_mix_g", "w_in", "b_in", "conv_w", "conv_b", "conv_ln_g", "conv_ln_b", "gm_ln_g", "gm_ln_b",
             "gm_w_s", "gm_b_s", "w_out", "norm_xa_g", "mem_norm_g", "xa_wq", "xa_wkv", "xa_wo", "norm_ffn_g",
             "ffn_w_gate_up", "ffn_w_down", "final_norm_g"]
    fit = lambda a, nm: a.reshape(weights[nm].shape)
    return (loss, gx.reshape(x.shape),
            *[fit(grads[nm], nm) for nm in order], *[fit(delta[nm], nm) for nm in order],
            *[fit(new_m[nm], nm) for nm in order], *[fit(new_v[nm], nm) for nm in order])
```

```python
import functools

import jax
import jax.numpy as jnp
from jax import lax
from jax.experimental import pallas as pl
from jax.experimental.pallas import tpu as pltpu

F32 = jnp.float32
BF16 = jnp.bfloat16

D_MODEL = 1024
CONV_WIDTH = 512
GM_WIDTH = 512
CONV_KERNEL = 31
CONV_HALO = 32
GRAD_ROWS = 2048
CHUNK = 128
GM_HEADS = 8
GM_HEAD_DIM = 64
XA_HEADS = 4
XA_HEAD_DIM = 256
FFN_HIDDEN = 2816
FFN_HALF = FFN_HIDDEN // 2
RMS_EPS = 1e-6
LN_EPS = 1e-5
N_CHIPS = 4
LANES = 128
SUBLANES = 8

ADAM_LR = 0.001
ADAM_B1 = 0.9
ADAM_B2 = 0.999
ADAM_EPS = 1e-08
ADAM_WD = 0.01
ADAM_STEP = 10

VMEM_LIMIT_BYTES = 56 * 1024 * 1024
MESH = pl.DeviceIdType.MESH
ANY = pl.BlockSpec(memory_space=pl.ANY)
HBM_SPEC = pl.BlockSpec(memory_space=pltpu.HBM)
SEM_SPEC = pl.BlockSpec(memory_space=pltpu.SEMAPHORE)

_NT = (((1,), (1,)), ((), ()))
_TN = (((0,), (0,)), ((), ()))
_GELU_C = 0.7978845608028654
_GELU_A = 0.044715


def _dot(a, b):
    return jnp.dot(a, b, preferred_element_type=F32)


def _dot_nt(a, b):
    return lax.dot_general(a, b, _NT, preferred_element_type=F32)


def _dot_tn(a, b):
    return lax.dot_general(a, b, _TN, preferred_element_type=F32)


def _mean(v):
    return jnp.mean(v, axis=-1, keepdims=True)


def _rowsum(v):
    return jnp.sum(v, axis=0, keepdims=True)


def _sigmoid(v):
    return 1.0 / (1.0 + jnp.exp(-v))


def _gelu_parts(v):
    v2 = v * v
    t = jnp.tanh(_GELU_C * (v + _GELU_A * v * v2))
    g = 0.5 * v * (1.0 + t)
    dg = 0.5 * (1.0 + t) + 0.5 * v * (1.0 - t * t) * (_GELU_C * (1.0 + 3.0 * _GELU_A * v2))
    return g, dg


def _rms_stats(v):
    return lax.rsqrt(_mean(v * v) + RMS_EPS)


def _rms_bwd(dy, v, r, g):
    n = v * r
    dn = dy * g
    dv = r * (dn - n * _mean(dn * n))
    return dv, _rowsum(dy * n)


def _ln_stats(v):
    mu = _mean(v)
    xc = v - mu
    rs = lax.rsqrt(_mean(xc * xc) + LN_EPS)
    return xc * rs, rs


def _ln_bwd(dy, xh, rs, g):
    dxh = dy * g
    dv = rs * (dxh - _mean(dxh) - xh * _mean(dxh * xh))
    return dv, _rowsum(dy * xh), _rowsum(dy)


def _params(sem):
    return pltpu.CompilerParams(dimension_semantics=sem, vmem_limit_bytes=VMEM_LIMIT_BYTES)


def _row_tile(s):
    return 512 if s % 512 == 0 and s >= 2048 else 128


def _mesh_pos():
    return lax.axis_index("x"), lax.axis_index("y"), lax.axis_index("c")


def _cast_into_slots(ws, pos, dtypes, name):
    n = len(ws)

    def body(pos_ref, *refs):
        for a in range(n):
            refs[n + a][0] = refs[a][...].astype(dtypes[a])

    return pl.pallas_call(
        body, name=name,
        grid_spec=pltpu.PrefetchScalarGridSpec(
            num_scalar_prefetch=1, grid=(2,),
            in_specs=[pl.BlockSpec((1,) + w.shape[1:], lambda i, p: (i, 0, 0)) for w in ws],
            out_specs=[pl.BlockSpec((1, 1) + w.shape[1:], lambda i, p: (p[0], i, 0, 0)) for w in ws]),
        out_shape=[jax.ShapeDtypeStruct((N_CHIPS,) + w.shape, dt) for w, dt in zip(ws, dtypes)],
        compiler_params=_params(("parallel",)),
    )(pos, *ws)


def _adam_update(w, g, m, v):
    nm = ADAM_B1 * m + (1.0 - ADAM_B1) * g
    nv = ADAM_B2 * v + (1.0 - ADAM_B2) * (g * g)
    m_hat = nm / (1.0 - ADAM_B1 ** ADAM_STEP)
    v_hat = nv / (1.0 - ADAM_B2 ** ADAM_STEP)
    return -ADAM_LR * (m_hat / (jnp.sqrt(v_hat) + ADAM_EPS) + ADAM_WD * w), nm, nv


ADAM_STEPS = 4


def _adamw(quads, name, after=()):
    n = len(quads)

    def body(*refs):
        ins, outs = refs[:4 * n], refs[4 * n:]
        for a in range(n):
            w, g, m, v = (r[...] for r in ins[4 * a:4 * a + 4])
            outs[3 * a][...], outs[3 * a + 1][...], outs[3 * a + 2][...] = _adam_update(w, g, m, v)

    specs = [pl.BlockSpec((q[0].shape[0] // ADAM_STEPS, q[0].shape[1]), lambda i: (i, 0)) for q in quads]
    out = _tied_call(
        body, after, name=name, grid=(ADAM_STEPS,),
        in_specs=[sp for sp in specs for _ in range(4)], out_specs=[sp for sp in specs for _ in range(3)],
        out_shape=[jax.ShapeDtypeStruct(q[0].shape, F32) for q in quads for _ in range(3)],
        compiler_params=_params(("parallel",)),
    )(*[a for q in quads for a in q])
    return [tuple(out[3 * a:3 * a + 3]) for a in range(n)]


def _adamw_small(gpack, pos, params, offsets, conv_at):
    n = len(params)

    def body(pos_ref, g_ref, *refs):
        ins, outs = refs[:3 * n], refs[3 * n:]
        for k in range(n):
            rows = params[k][0].shape[0]
            start = offsets[k]
            if k == conv_at:
                start = pl.multiple_of(start + pos_ref[0] * CONV_HALO, SUBLANES)
            g = g_ref[pl.ds(start, rows), :]
            outs[4 * k][...] = g
            outs[4 * k + 1][...], outs[4 * k + 2][...], outs[4 * k + 3][...] = _adam_update(
                ins[3 * k][...], g, ins[3 * k + 1][...], ins[3 * k + 2][...])

    flat = [a for p in params for a in p]
    vmem = pl.BlockSpec(memory_space=pltpu.VMEM)
    return pl.pallas_call(
        body, name="adamw_small",
        in_specs=[pl.BlockSpec(memory_space=pltpu.SMEM), vmem] + [vmem] * len(flat),
        out_specs=[vmem] * (4 * n),
        out_shape=[jax.ShapeDtypeStruct(p[0].shape, F32) for p in params for _ in range(4)],
    )(pos, gpack, *flat)


def _as_tuple(after):
    return tuple(after) if isinstance(after, (tuple, list)) else (after,)


def _tied_call(body, after, *, in_specs, **kwargs):
    after = _as_tuple(after)
    n_in, n_after = len(in_specs), len(after)

    def tied(*refs):
        body(*refs[:n_in], *refs[n_in + n_after:])

    call = pl.pallas_call(tied, in_specs=list(in_specs) + [ANY] * n_after, **kwargs)
    return lambda *operands: call(*operands, *after)


def _other_chips(x, y):
    return [(1 - x, y), (x, 1 - y), (1 - x, 1 - y)]


def _gather_descriptors(bufs, send_of, recv_of):
    x, y, c = _mesh_pos()
    me = 2 * x + y
    chips = _other_chips(x, y)
    sends, arrivals = [], []
    for a in range(len(bufs)):
        for k in range(3):
            ck = 2 * chips[k][0] + chips[k][1]

            def copy(slot, a=a, k=k):
                return pltpu.make_async_remote_copy(
                    src_ref=bufs[a].at[slot, c], dst_ref=bufs[a].at[slot, c],
                    send_sem=send_of(a, k), recv_sem=recv_of(a, k),
                    device_id=(*chips[k], c), device_id_type=MESH)

            sends.append(functools.partial(copy, me))
            arrivals.append(functools.partial(copy, ck))
    return sends, arrivals


def _gather_start(bufs, name, after=()):
    n = len(bufs)
    ns = 3 * n

    def body(*refs):
        sems = refs[n:n + 2 * ns]
        thru = refs[n + 2 * ns:2 * n + 2 * ns]
        token = refs[2 * n + 2 * ns]
        sends, _ = _gather_descriptors(thru, lambda a, k: sems[3 * a + k], lambda a, k: sems[ns + 3 * a + k])
        for cp in sends:
            cp().start()
        token[...] = jnp.zeros_like(token)

    held = [pltpu.with_memory_space_constraint(b, pltpu.HBM) for b in bufs]
    out = _tied_call(
        body, after, name=name,
        out_shape=(*[pltpu.SemaphoreType.DMA(())] * (2 * ns), *[pltpu.HBM(b.shape, b.dtype) for b in held],
                   jax.ShapeDtypeStruct((8, LANES), F32)),
        in_specs=[HBM_SPEC] * n,
        out_specs=(*[SEM_SPEC] * (2 * ns), *[HBM_SPEC] * n, pl.BlockSpec(memory_space=pltpu.VMEM)),
        input_output_aliases={i: 2 * ns + i for i in range(n)},
        compiler_params=pltpu.CompilerParams(has_side_effects=pltpu.SideEffectType.DATAFLOW_SIDE_EFFECTING),
    )(*held)
    return list(out[:ns]), list(out[ns:2 * ns]), list(out[2 * ns:2 * ns + n]), out[2 * ns + n]


def _gather_wait(send_sems, recv_sems, bufs, after, name):
    n = len(bufs)
    ns = 3 * n

    def body(*refs):
        buf_ref = refs[:n]
        sem_ref = refs[n:n + 2 * ns]
        sends, arrivals = _gather_descriptors(buf_ref, lambda a, k: sem_ref[3 * a + k],
                                              lambda a, k: sem_ref[ns + 3 * a + k])
        for cp in sends:
            cp().wait_send()
        for cp in arrivals:
            cp().wait_recv()

    out = pl.pallas_call(
        body, name=name,
        out_shape=tuple(pltpu.HBM(b.shape, b.dtype) for b in bufs),
        in_specs=[HBM_SPEC] * n + [SEM_SPEC] * (2 * ns) + [ANY] * len(_as_tuple(after)),
        out_specs=tuple([HBM_SPEC] * n),
        input_output_aliases={i: i for i in range(n)},
        compiler_params=pltpu.CompilerParams(has_side_effects=pltpu.SideEffectType.DATAFLOW_SIDE_EFFECTING),
    )(*bufs, *send_sems, *recv_sems, *_as_tuple(after))
    return list(out)


def _pass_to_sibling(bufs, name):
    n = len(bufs)

    def body(*refs):
        outs = refs[n:2 * n]
        send_sem, recv_sem = refs[2 * n:]
        x, y, c = _mesh_pos()
        chips = _other_chips(x, y)

        def half(a, k, which):
            ck = 2 * chips[k][0] + chips[k][1]
            return pltpu.make_async_remote_copy(
                src_ref=outs[a].at[ck, which], dst_ref=outs[a].at[ck, which],
                send_sem=send_sem.at[a, k], recv_sem=recv_sem.at[a, k],
                device_id=(x, y, 1 - c), device_id_type=MESH)

        sends = [half(a, k, c) for a in range(n) for k in range(3)]
        for cp in sends:
            cp.start()
        for a in range(n):
            for k in range(3):
                half(a, k, 1 - c).wait_recv()
        for cp in sends:
            cp.wait_send()

    return pl.pallas_call(
        body, name=name,
        in_specs=[ANY] * n, out_specs=[ANY] * n,
        out_shape=[jax.ShapeDtypeStruct(b.shape, b.dtype) for b in bufs],
        input_output_aliases={a: a for a in range(n)},
        scratch_shapes=[pltpu.SemaphoreType.DMA((n, 3))] * 2,
    )(*bufs)


def _swap_halves(grads, name):
    n = len(grads)

    def body(*refs):
        ins, outs = refs[:n], refs[n:2 * n]
        send_sem, recv_sem = refs[2 * n:]
        x, y, c = _mesh_pos()
        cps = [pltpu.make_async_remote_copy(
            src_ref=ins[a].at[:, pl.ds(1 - c, 1)], dst_ref=outs[a],
            send_sem=send_sem.at[a], recv_sem=recv_sem.at[a],
            device_id=(x, y, 1 - c), device_id_type=MESH) for a in range(n)]
        for cp in cps:
            cp.start()
        for cp in cps:
            cp.wait()

    out_shape = [jax.ShapeDtypeStruct((g.shape[0], 1) + g.shape[2:], g.dtype) for g in grads]
    return pl.pallas_call(
        body, name=name,
        in_specs=[ANY] * n, out_specs=[ANY] * n, out_shape=out_shape,
        scratch_shapes=[pltpu.SemaphoreType.DMA((n,))] * 2,
    )(*grads)


def _swap_descriptors(grads, lands, send_of, recv_of):
    x, y, c = _mesh_pos()
    return [functools.partial(
        pltpu.make_async_remote_copy,
        src_ref=grads[a].at[:, pl.ds(1 - c, 1)], dst_ref=lands[a],
        send_sem=send_of(a), recv_sem=recv_of(a),
        device_id=(x, y, 1 - c), device_id_type=MESH) for a in range(len(grads))]


def _swap_start(grads, name):
    n = len(grads)

    def body(*refs):
        sems = refs[2 * n:4 * n]
        g_thru, l_thru = refs[4 * n:5 * n], refs[5 * n:6 * n]
        token = refs[6 * n]
        for cp in _swap_descriptors(g_thru, l_thru, lambda a: sems[a], lambda a: sems[n + a]):
            cp().start()
        token[...] = jnp.zeros_like(token)

    lands = [lax.empty((g.shape[0], 1) + g.shape[2:], g.dtype) for g in grads]
    held = [pltpu.with_memory_space_constraint(a, pltpu.HBM) for a in (*grads, *lands)]
    out = pl.pallas_call(
        body, name=name,
        out_shape=(*[pltpu.SemaphoreType.DMA(())] * (2 * n), *[pltpu.HBM(a.shape, a.dtype) for a in held],
                   jax.ShapeDtypeStruct((8, LANES), F32)),
        in_specs=[HBM_SPEC] * (2 * n),
        out_specs=(*[SEM_SPEC] * (2 * n), *[HBM_SPEC] * (2 * n), pl.BlockSpec(memory_space=pltpu.VMEM)),
        input_output_aliases={i: 2 * n + i for i in range(2 * n)},
        compiler_params=pltpu.CompilerParams(has_side_effects=pltpu.SideEffectType.DATAFLOW_SIDE_EFFECTING),
    )(*held)
    return list(out[:2 * n]), list(out[2 * n:3 * n]), list(out[3 * n:4 * n]), out[4 * n]


def _swap_wait(sems, grads, lands, after, name):
    n = len(grads)

    def body(*refs):
        g_ref, l_ref = refs[:n], refs[n:2 * n]
        sem_ref = refs[2 * n:4 * n]
        for cp in _swap_descriptors(g_ref, l_ref, lambda a: sem_ref[a], lambda a: sem_ref[n + a]):
            cp().wait()

    out = pl.pallas_call(
        body, name=name,
        out_shape=tuple(pltpu.HBM(a.shape, a.dtype) for a in (*grads, *lands)),
        in_specs=[HBM_SPEC] * (2 * n) + [SEM_SPEC] * (2 * n) + [ANY] * len(_as_tuple(after)),
        out_specs=tuple([HBM_SPEC] * (2 * n)),
        input_output_aliases={i: i for i in range(2 * n)},
        compiler_params=pltpu.CompilerParams(has_side_effects=pltpu.SideEffectType.DATAFLOW_SIDE_EFFECTING),
    )(*grads, *lands, *sems, *_as_tuple(after))
    return list(out[:n]), list(out[n:])


def _add_halves(gs, gots, pos, name, dtypes):
    n = len(gs)
    j = gs[0].shape[0]

    def body(pos_ref, *refs):
        g_refs, r_refs = refs[:n], refs[n:2 * n]
        o_refs, p_refs = refs[2 * n:3 * n], refs[3 * n:]
        vals = [(g_refs[a][0, 0] + r_refs[a][0, 0]).astype(dtypes[a]) for a in range(n)]
        for a in range(n):
            o_refs[a][0] = vals[a]
        if j == 1:
            for a in range(n):
                p_refs[a][0] = vals[a]
        else:
            @pl.when(pl.program_id(0) == pos_ref[0])
            def _():
                for a in range(n):
                    p_refs[a][0] = vals[a]

    blk = lambda g: (1,) + g.shape[2:]
    out = pl.pallas_call(
        body, name=name,
        grid_spec=pltpu.PrefetchScalarGridSpec(
            num_scalar_prefetch=1, grid=(j,),
            in_specs=[pl.BlockSpec((1,) + blk(g), lambda i, p: (i, p[1], 0, 0)) for g in gs]
            + [pl.BlockSpec((1,) + blk(g), lambda i, p: (i, 0, 0, 0)) for g in gs],
            out_specs=[pl.BlockSpec(blk(g), lambda i, p: (i, 0, 0)) for g in gs]
            + [pl.BlockSpec(blk(g), lambda i, p: (p[0], 0, 0)) for g in gs]),
        out_shape=[jax.ShapeDtypeStruct((j,) + g.shape[2:], dt) for g, dt in zip(gs, dtypes)]
        + [jax.ShapeDtypeStruct((N_CHIPS,) + g.shape[2:], dt) for g, dt in zip(gs, dtypes)],
        compiler_params=_params(("arbitrary",)),
    )(pos, *gs, *gots)
    return list(out[:n]), list(out[n:])


def _exchange_descriptors(sums, parts, send_of, recv_of):
    x, y, c = _mesh_pos()
    me = 2 * x + y
    chips = _other_chips(x, y)
    sends, arrivals = [], []
    for a in range(len(sums)):
        for k in range(3):
            ck = 2 * chips[k][0] + chips[k][1]
            mine = sums[a].at[ck] if sums[a].shape[0] == N_CHIPS else sums[a].at[0]

            def copy(dst_slot, a=a, k=k, mine=mine):
                return pltpu.make_async_remote_copy(
                    src_ref=mine, dst_ref=parts[a].at[dst_slot],
                    send_sem=send_of(a, k), recv_sem=recv_of(a, k),
                    device_id=(*chips[k], c), device_id_type=MESH)

            sends.append(functools.partial(copy, me))
            arrivals.append(functools.partial(copy, ck))
    return sends, arrivals


def _exchange_start(sums, parts, name):
    n = len(sums)
    ns = 3 * n

    def body(*refs):
        sems = refs[2 * n:2 * n + 2 * ns]
        sums_thru = refs[2 * n + 2 * ns:3 * n + 2 * ns]
        parts_thru = refs[3 * n + 2 * ns:4 * n + 2 * ns]
        token = refs[4 * n + 2 * ns]
        sends, _ = _exchange_descriptors(sums_thru, parts_thru, lambda a, k: sems[3 * a + k],
                                         lambda a, k: sems[ns + 3 * a + k])
        for cp in sends:
            cp().start()
        token[...] = jnp.zeros_like(token)

    hbm = lambda a: pltpu.HBM(a.shape, a.dtype)
    held = [pltpu.with_memory_space_constraint(a, pltpu.HBM) for a in (*sums, *parts)]
    out = pl.pallas_call(
        body, name=name,
        out_shape=(*[pltpu.SemaphoreType.DMA(())] * (2 * ns), *[hbm(a) for a in held],
                   jax.ShapeDtypeStruct((8, LANES), F32)),
        in_specs=[HBM_SPEC] * (2 * n),
        out_specs=(*[SEM_SPEC] * (2 * ns), *[HBM_SPEC] * (2 * n), pl.BlockSpec(memory_space=pltpu.VMEM)),
        input_output_aliases={i: 2 * ns + i for i in range(2 * n)},
        compiler_params=pltpu.CompilerParams(has_side_effects=pltpu.SideEffectType.DATAFLOW_SIDE_EFFECTING),
    )(*held)
    return (list(out[:2 * ns]), list(out[2 * ns:2 * ns + n]), list(out[2 * ns + n:2 * ns + 2 * n]),
            out[2 * ns + 2 * n])


def _exchange_wait(sems, sums, parts, after, name):
    n = len(sums)
    ns = 3 * n

    def body(*refs):
        sums_ref, parts_ref = refs[:n], refs[n:2 * n]
        sem_ref = refs[2 * n:2 * n + 2 * ns]
        sends, arrivals = _exchange_descriptors(sums_ref, parts_ref, lambda a, k: sem_ref[3 * a + k],
                                                lambda a, k: sem_ref[ns + 3 * a + k])
        for cp in sends:
            cp().wait_send()
        for cp in arrivals:
            cp().wait_recv()

    hbm = lambda a: pltpu.HBM(a.shape, a.dtype)
    out = pl.pallas_call(
        body, name=name,
        out_shape=tuple(hbm(a) for a in (*sums, *parts)),
        in_specs=[HBM_SPEC] * (2 * n) + [SEM_SPEC] * (2 * ns) + [ANY] * len(_as_tuple(after)),
        out_specs=tuple([HBM_SPEC] * (2 * n)),
        input_output_aliases={i: i for i in range(2 * n)},
        compiler_params=pltpu.CompilerParams(has_side_effects=pltpu.SideEffectType.DATAFLOW_SIDE_EFFECTING),
    )(*sums, *parts, *sems, *_as_tuple(after))
    return list(out[n:])


def _sum_chips(parts, pos, name):
    n = len(parts)

    def body(pos_ref, *refs):
        for a in range(n):
            p_ref = refs[a]
            refs[n + a][0] = (((p_ref[0].astype(F32) + p_ref[1].astype(F32)) + p_ref[2].astype(F32))
                              + p_ref[3].astype(F32))

    out = pl.pallas_call(
        body, name=name,
        grid_spec=pltpu.PrefetchScalarGridSpec(
            num_scalar_prefetch=1, grid=(1,),
            in_specs=[pl.BlockSpec(p.shape, lambda i, q: (0, 0, 0)) for p in parts],
            out_specs=[pl.BlockSpec((1,) + p.shape[1:], lambda i, q: (q[1], 0, 0)) for p in parts]),
        out_shape=[jax.ShapeDtypeStruct((2,) + p.shape[1:], F32) for p in parts],
        compiler_params=_params(("arbitrary",)),
    )(pos, *parts)
    return list(out)


def _join_halves(fulls, name, after=()):
    n = len(fulls)

    def body(*refs):
        outs = refs[n:2 * n]
        send_sem, recv_sem = refs[2 * n:]
        x, y, c = _mesh_pos()

        def half(a, which):
            return pltpu.make_async_remote_copy(
                src_ref=outs[a].at[which], dst_ref=outs[a].at[which],
                send_sem=send_sem.at[a], recv_sem=recv_sem.at[a],
                device_id=(x, y, 1 - c), device_id_type=MESH)

        sends = [half(a, c) for a in range(n)]
        for cp in sends:
            cp.start()
        for a in range(n):
            half(a, 1 - c).wait_recv()
        for cp in sends:
            cp.wait_send()

    out_shape = [jax.ShapeDtypeStruct(f.shape, f.dtype) for f in fulls]
    return _tied_call(
        body, after, name=name,
        in_specs=[ANY] * n, out_specs=[ANY] * n, out_shape=out_shape,
        input_output_aliases={a: a for a in range(n)},
        scratch_shapes=[pltpu.SemaphoreType.DMA((n,))] * 2,
    )(*fulls)


def _mix_in(x, g, w_in, b_in, ts, after=()):
    s = x.shape[0]

    def body(x_ref, g_ref, w_ref, b_ref, z_ref, hn_ref):
        xv = x_ref[...]
        hn = (xv * _rms_stats(xv) * g_ref[...]).astype(BF16)
        hn_ref[...] = hn
        for j in range(4):
            cols = slice(j * 512, (j + 1) * 512)
            z_ref[:, cols] = _dot(hn, w_ref[j]) + b_ref[:, cols]

    return _tied_call(
        body, after, name="mix_in", grid=(s // ts,),
        in_specs=[pl.BlockSpec((ts, D_MODEL), lambda i: (i, 0)),
                  pl.BlockSpec((1, D_MODEL), lambda i: (0, 0)),
                  pl.BlockSpec((4, D_MODEL, 512), lambda i: (0, 0, 0)),
                  pl.BlockSpec((1, 2048), lambda i: (0, 0))],
        out_specs=[pl.BlockSpec((ts, 2048), lambda i: (i, 0)),
                   pl.BlockSpec((ts, D_MODEL), lambda i: (i, 0))],
        out_shape=[jax.ShapeDtypeStruct((s, 2048), F32), jax.ShapeDtypeStruct((s, D_MODEL), BF16)],
        compiler_params=_params(("parallel",)),
    )(x, g, w_in, b_in)


def _shift_rows(buf, shifted, t):
    rows = t + CONV_HALO - SUBLANES
    for r in range(1, SUBLANES):
        shifted[r - 1, 0:rows, :] = buf[pl.ds(r, rows), :]


def _window(buf, shifted, offset, t):
    r = offset % SUBLANES
    if r == 0:
        return buf[pl.ds(offset, t), :]
    return shifted[r - 1, pl.ds(offset - r, t), :]


def _lane_is_low_head():
    lane = lax.broadcasted_iota(jnp.int32, (1, GM_WIDTH), 1)
    return (lane & GM_HEAD_DIM) == 0


def _gm_mix(v_lo, v_hi, wpair_ref, bias_ref, mixed_ref, t):
    for n in range(t // CHUNK):
        rows = slice(n * CHUNK, (n + 1) * CHUNK)
        for j in range(GM_HEADS // 2):
            cols = slice(j * LANES, (j + 1) * LANES)
            rhs = jnp.concatenate([v_lo[rows, cols], v_hi[rows, cols]], axis=0)
            mixed_ref[rows, cols] = _dot(wpair_ref[j], rhs) + bias_ref[:, cols]


def _seqmix_fwd(z, cw, cb, lng, lnb, gg, gb, wpair, bias, t):
    s = z.shape[0]

    def body(z_ref, cw_ref, cb_ref, lng_ref, lnb_ref, gg_ref, gb_ref, wpair_ref, bias_ref,
             mix_ref, c1_ref, abuf, ash, mixed_ref):
        i = pl.program_id(0)

        @pl.when(i == 0)
        def _():
            abuf[0:CONV_HALO, :] = jnp.zeros((CONV_HALO, CONV_WIDTH), F32)

        @pl.when(i > 0)
        def _():
            abuf[0:CONV_HALO, :] = abuf[t:t + CONV_HALO, :]

        abuf[CONV_HALO:, :] = z_ref[:, 0:512] * _sigmoid(z_ref[:, 512:1024])
        _shift_rows(abuf, ash, t)
        acc = jnp.zeros((t, CONV_WIDTH), F32)
        for k in range(CONV_KERNEL):
            acc = acc + cw_ref[k:k + 1, :] * _window(abuf, ash, CONV_HALO - (CONV_KERNEL - 1) + k, t)
        c1 = acc + cb_ref[...]
        c1_ref[...] = c1
        xh, _ = _ln_stats(c1)
        ln = xh * lng_ref[...] + lnb_ref[...]
        mix_ref[:, 0:512] = (ln * _sigmoid(ln)).astype(BF16)

        u, _ = _gelu_parts(z_ref[:, 1024:1536])
        gv, _ = _gelu_parts(z_ref[:, 1536:2048])
        vxh, _ = _ln_stats(gv)
        v = vxh * gg_ref[...] + gb_ref[...]
        low = _lane_is_low_head()
        v_lo = jnp.where(low, v, 0.0).astype(BF16)
        v_hi = jnp.where(low, 0.0, v).astype(BF16)
        _gm_mix(v_lo, v_hi, wpair_ref, bias_ref, mixed_ref, t)
        mix_ref[:, 512:1024] = (u * mixed_ref[...]).astype(BF16)

    vec = lambda n: pl.BlockSpec((1, n), lambda i: (0, 0))
    return pl.pallas_call(
        body, name="seqmix_fwd", grid=(s // t,),
        in_specs=[pl.BlockSpec((t, 2048), lambda i: (i, 0)),
                  pl.BlockSpec((CONV_HALO, CONV_WIDTH), lambda i: (0, 0)),
                  vec(512), vec(512), vec(512), vec(512), vec(512),
                  pl.BlockSpec((4, CHUNK, 2 * CHUNK), lambda i: (0, 0, 0)),
                  pl.BlockSpec((CHUNK, GM_WIDTH), lambda i: (0, 0))],
        out_specs=[pl.BlockSpec((t, D_MODEL), lambda i: (i, 0)),
                   pl.BlockSpec((t, CONV_WIDTH), lambda i: (i, 0))],
        out_shape=[jax.ShapeDtypeStruct((s, D_MODEL), BF16), jax.ShapeDtypeStruct((s, CONV_WIDTH), F32)],
        scratch_shapes=[pltpu.VMEM((t + CONV_HALO, CONV_WIDTH), F32),
                        pltpu.VMEM((SUBLANES - 1, t + CONV_HALO - SUBLANES, CONV_WIDTH), F32),
                        pltpu.VMEM((t, GM_WIDTH), F32)],
        compiler_params=_params(("arbitrary",)),
    )(z, cw, cb, lng, lnb, gg, gb, wpair, bias)


def _out_proj_q(x, mix, w_out, g, wq, ts, after=()):
    s = x.shape[0]

    def body(x_ref, mix_ref, wo_ref, g_ref, wq_ref, h1_ref, hn_ref, q_ref):
        h1 = x_ref[...] + _dot(mix_ref[...], wo_ref[...])
        h1_ref[...] = h1
        hn = (h1 * _rms_stats(h1) * g_ref[...]).astype(BF16)
        hn_ref[...] = hn
        q_ref[...] = _dot(hn, wq_ref[...]).astype(BF16)

    row = lambda dt: pl.BlockSpec((ts, D_MODEL), lambda i: (i, 0))
    full = pl.BlockSpec((D_MODEL, D_MODEL), lambda i: (0, 0))
    return _tied_call(
        body, after, name="out_proj_q", grid=(s // ts,),
        in_specs=[row(F32), row(BF16), full, pl.BlockSpec((1, D_MODEL), lambda i: (0, 0)), full],
        out_specs=[row(F32), row(BF16), row(BF16)],
        out_shape=[jax.ShapeDtypeStruct((s, D_MODEL), F32), jax.ShapeDtypeStruct((s, D_MODEL), BF16),
                   jax.ShapeDtypeStruct((s, D_MODEL), BF16)],
        compiler_params=_params(("parallel",)),
    )(x, mix, w_out, g, wq)


def _mem_kv(mem, g, wkv):
    m = mem.shape[0]

    def body(mem_ref, g_ref, w_ref, mn_ref, kv_ref):
        mv = mem_ref[...]
        mn = (mv * _rms_stats(mv) * g_ref[...]).astype(BF16)
        mn_ref[...] = mn
        for j in range(4):
            kv_ref[:, j * 512:(j + 1) * 512] = _dot(mn, w_ref[j]).astype(BF16)

    return pl.pallas_call(
        body, name="mem_kv",
        out_shape=[jax.ShapeDtypeStruct((m, D_MODEL), BF16), jax.ShapeDtypeStruct((m, 2 * D_MODEL), BF16)],
        compiler_params=pltpu.CompilerParams(vmem_limit_bytes=VMEM_LIMIT_BYTES),
    )(mem, g, wkv)


def _softmax_rows(sc):
    e = jnp.exp(sc - jnp.max(sc, axis=-1, keepdims=True))
    return e / jnp.sum(e, axis=-1, keepdims=True)


def _attn_fwd(q, kv, h1, wo, g, ts):
    s, m = q.shape[0], kv.shape[0]
    scale = XA_HEAD_DIM ** -0.5

    def body(q_ref, kv_ref, h1_ref, wo_ref, g_ref, o_ref, h2_ref, hn_ref):
        for h in range(XA_HEADS):
            cols = slice(h * XA_HEAD_DIM, (h + 1) * XA_HEAD_DIM)
            vcols = slice(D_MODEL + h * XA_HEAD_DIM, D_MODEL + (h + 1) * XA_HEAD_DIM)
            p = _softmax_rows(_dot_nt(q_ref[:, cols], kv_ref[:, cols]) * scale)
            o_ref[:, cols] = _dot(p.astype(BF16), kv_ref[:, vcols]).astype(BF16)
        h2 = h1_ref[...] + _dot(o_ref[...], wo_ref[...])
        h2_ref[...] = h2
        hn_ref[...] = (h2 * _rms_stats(h2) * g_ref[...]).astype(BF16)

    row = pl.BlockSpec((ts, D_MODEL), lambda i: (i, 0))
    return pl.pallas_call(
        body, name="attn_fwd", grid=(s // ts,),
        in_specs=[row, pl.BlockSpec((m, 2 * D_MODEL), lambda i: (0, 0)), row,
                  pl.BlockSpec((D_MODEL, D_MODEL), lambda i: (0, 0)),
                  pl.BlockSpec((1, D_MODEL), lambda i: (0, 0))],
        out_specs=[row, row, row],
        out_shape=[jax.ShapeDtypeStruct((s, D_MODEL), BF16), jax.ShapeDtypeStruct((s, D_MODEL), F32),
                   jax.ShapeDtypeStruct((s, D_MODEL), BF16)],
        compiler_params=_params(("parallel",)),
    )(q, kv, h1, wo, g)


_FFN_CHUNKS_FWD = (slice(0, 6 * LANES), slice(6 * LANES, FFN_HALF))
_FFN_CHUNKS_BWD = tuple(slice(lo, min(lo + 3 * LANES, FFN_HALF)) for lo in range(0, FFN_HALF, 3 * LANES))


def _ffn_up(hn, wgu, ts, after=()):
    s = hn.shape[0]

    def body(hn_ref, w_ref, gu_ref, act_ref):
        hv = hn_ref[...]
        for cols in _FFN_CHUNKS_FWD:
            gate = _dot(hv, w_ref[0, 0, :, cols])
            up = _dot(hv, w_ref[1, 0, :, cols])
            gu_ref[0, :, cols] = gate.astype(BF16)
            gu_ref[1, :, cols] = up.astype(BF16)
            act_ref[:, cols] = (gate * _sigmoid(gate) * up).astype(BF16)

    return _tied_call(
        body, after, name="ffn_up", grid=(2, s // ts),
        in_specs=[pl.BlockSpec((ts, D_MODEL), lambda j, i: (i, 0)),
                  pl.BlockSpec((2, 1, D_MODEL, FFN_HALF), lambda j, i: (0, j, 0, 0))],
        out_specs=[pl.BlockSpec((2, ts, FFN_HALF), lambda j, i: (0, i, j)),
                   pl.BlockSpec((ts, FFN_HALF), lambda j, i: (i, j))],
        out_shape=[jax.ShapeDtypeStruct((2, s, FFN_HIDDEN), BF16), jax.ShapeDtypeStruct((s, FFN_HIDDEN), BF16)],
        compiler_params=_params(("parallel", "parallel")),
    )(hn, wgu)


def _ffn_down_loss(act, wd, h2, g, target, ts):
    s = act.shape[0]

    def body(act_ref, wd_ref, h2_ref, g_ref, t_ref, dh_ref, dhb_ref, sq_ref, dg_ref):
        @pl.when(pl.program_id(0) == 0)
        def _():
            sq_ref[...] = jnp.zeros_like(sq_ref)
            dg_ref[...] = jnp.zeros_like(dg_ref)

        h3 = h2_ref[...] + _dot(act_ref[...], wd_ref[...])
        r = _rms_stats(h3)
        gv = g_ref[...]
        diff = h3 * r * gv - t_ref[...]
        sq_ref[...] += _rowsum(diff * diff)
        dh, dg = _rms_bwd(diff / D_MODEL, h3, r, gv)
        dh_ref[...] = dh
        dhb_ref[...] = dh.astype(BF16)
        dg_ref[...] += dg

    row = pl.BlockSpec((ts, D_MODEL), lambda i: (i, 0))
    vec = pl.BlockSpec((1, D_MODEL), lambda i: (0, 0))
    return pl.pallas_call(
        body, name="ffn_down_loss", grid=(s // ts,),
        in_specs=[pl.BlockSpec((ts, FFN_HIDDEN), lambda i: (i, 0)),
                  pl.BlockSpec((FFN_HIDDEN, D_MODEL), lambda i: (0, 0)), row, vec, row],
        out_specs=[row, row, vec, vec],
        out_shape=[jax.ShapeDtypeStruct((s, D_MODEL), F32), jax.ShapeDtypeStruct((s, D_MODEL), BF16),
                   jax.ShapeDtypeStruct((1, D_MODEL), F32), jax.ShapeDtypeStruct((1, D_MODEL), F32)],
        compiler_params=_params(("arbitrary",)),
    )(act, wd, h2, g, target)


def _grad_w(a, b, tk, tn, name, after=()):
    s, k = a.shape
    gb, _, n = b.shape
    nblk = n // tn
    tsr = GRAD_ROWS if s % GRAD_ROWS == 0 else s

    def body(a_ref, b_ref, o_ref):
        part = _dot_tn(a_ref[...], b_ref[0])

        @pl.when(pl.program_id(2) == 0)
        def _():
            o_ref[0] = part

        @pl.when(pl.program_id(2) > 0)
        def _():
            o_ref[0] += part

    return _tied_call(
        body, after, name=name, grid=(gb * nblk, k // tk, s // tsr),
        in_specs=[pl.BlockSpec((tsr, tk), lambda ni, ki, si: (si, ki)),
                  pl.BlockSpec((1, tsr, tn), lambda ni, ki, si: (ni // nblk, si, ni % nblk))],
        out_specs=pl.BlockSpec((1, tk, tn), lambda ni, ki, si: (ni, ki, 0)),
        out_shape=jax.ShapeDtypeStruct((gb * nblk, k, tn), F32),
        compiler_params=_params(("parallel", "parallel", "arbitrary")),
    )(a, b)


def _ffn_bwd_act(dh3, wd, gu, ts):
    s = dh3.shape[0]

    def body(dh_ref, wd_ref, gu_ref, dgu_ref):
        dh = dh_ref[...]
        for cols in _FFN_CHUNKS_BWD:
            dact = _dot_nt(dh, wd_ref[0, cols, :])
            gate, up = gu_ref[0, :, cols].astype(F32), gu_ref[1, :, cols].astype(F32)
            sg = _sigmoid(gate)
            dgu_ref[0, :, cols] = (dact * up * (sg * (1.0 + gate * (1.0 - sg)))).astype(BF16)
            dgu_ref[1, :, cols] = (dact * (gate * sg)).astype(BF16)

    return pl.pallas_call(
        body, name="ffn_bwd_act", grid=(2, s // ts),
        in_specs=[pl.BlockSpec((ts, D_MODEL), lambda j, i: (i, 0)),
                  pl.BlockSpec((1, FFN_HALF, D_MODEL), lambda j, i: (j, 0, 0)),
                  pl.BlockSpec((2, ts, FFN_HALF), lambda j, i: (0, i, j))],
        out_specs=pl.BlockSpec((2, ts, FFN_HALF), lambda j, i: (0, i, j)),
        out_shape=jax.ShapeDtypeStruct((2, s, FFN_HIDDEN), BF16),
        compiler_params=_params(("parallel", "parallel")),
    )(dh3, wd, gu)


def _ffn_bwd_in(dgu, wgu, dh3, h2, g, ts, after=()):
    s = dh3.shape[0]

    def body(dgu_ref, w_ref, dh3_ref, h2_ref, g_ref, dh2_ref, dh2b_ref, dg_ref):
        @pl.when(pl.program_id(0) == 0)
        def _():
            dg_ref[...] = jnp.zeros_like(dg_ref)

        dhn = jnp.zeros((ts, D_MODEL), F32)
        for p in range(2):
            for j in range(2):
                dhn = dhn + _dot_nt(dgu_ref[p, :, j * FFN_HALF:(j + 1) * FFN_HALF], w_ref[2 * p + j])
        h2 = h2_ref[...]
        dv, dg = _rms_bwd(dhn, h2, _rms_stats(h2), g_ref[...])
        dh2 = dh3_ref[...] + dv
        dh2_ref[...] = dh2
        dh2b_ref[...] = dh2.astype(BF16)
        dg_ref[...] += dg

    row = pl.BlockSpec((ts, D_MODEL), lambda i: (i, 0))
    vec = pl.BlockSpec((1, D_MODEL), lambda i: (0, 0))
    return _tied_call(
        body, after, name="ffn_bwd_in", grid=(s // ts,),
        in_specs=[pl.BlockSpec((2, ts, FFN_HIDDEN), lambda i: (0, i, 0)),
                  pl.BlockSpec((4, D_MODEL, FFN_HALF), lambda i: (0, 0, 0)), row, row, vec],
        out_specs=[row, row, vec],
        out_shape=[jax.ShapeDtypeStruct((s, D_MODEL), F32), jax.ShapeDtypeStruct((s, D_MODEL), BF16),
                   jax.ShapeDtypeStruct((1, D_MODEL), F32)],
        compiler_params=_params(("arbitrary",)),
    )(dgu, wgu, dh3, h2, g)


def _attn_bwd(dh2, wo, q, kv, wq, h1, g, ts, after=()):
    s, m = q.shape[0], kv.shape[0]
    scale = XA_HEAD_DIM ** -0.5

    def body(dh2_ref, wo_ref, q_ref, kv_ref, wq_ref, h1_ref, g_ref, dh1_ref, dh1b_ref, dq_ref, dkv_ref, dg_ref):
        @pl.when(pl.program_id(0) == 0)
        def _():
            dkv_ref[...] = jnp.zeros_like(dkv_ref)
            dg_ref[...] = jnp.zeros_like(dg_ref)

        do = _dot_nt(dh2_ref[...].astype(BF16), wo_ref[...]).astype(BF16)
        for h in range(XA_HEADS):
            cols = slice(h * XA_HEAD_DIM, (h + 1) * XA_HEAD_DIM)
            vcols = slice(D_MODEL + h * XA_HEAD_DIM, D_MODEL + (h + 1) * XA_HEAD_DIM)
            qh, kh, vh, doh = q_ref[:, cols], kv_ref[:, cols], kv_ref[:, vcols], do[:, cols]
            p = _softmax_rows(_dot_nt(qh, kh) * scale)
            dp = _dot_nt(doh, vh)
            ds = (p * (dp - jnp.sum(dp * p, axis=-1, keepdims=True)) * scale).astype(BF16)
            dq_ref[:, cols] = _dot(ds, kh).astype(BF16)
            dkv_ref[:, cols] += _dot_tn(ds, qh)
            dkv_ref[:, vcols] += _dot_tn(p.astype(BF16), doh)
        dhn = _dot_nt(dq_ref[...], wq_ref[...])
        h1 = h1_ref[...]
        dv, dg = _rms_bwd(dhn, h1, _rms_stats(h1), g_ref[...])
        dh1 = dh2_ref[...] + dv
        dh1_ref[...] = dh1
        dh1b_ref[...] = dh1.astype(BF16)
        dg_ref[...] += dg

    row = pl.BlockSpec((ts, D_MODEL), lambda i: (i, 0))
    full = pl.BlockSpec((D_MODEL, D_MODEL), lambda i: (0, 0))
    kvs = pl.BlockSpec((m, 2 * D_MODEL), lambda i: (0, 0))
    vec = pl.BlockSpec((1, D_MODEL), lambda i: (0, 0))
    return _tied_call(
        body, after, name="attn_bwd", grid=(s // ts,),
        in_specs=[row, full, row, kvs, full, row, vec],
        out_specs=[row, row, row, kvs, vec],
        out_shape=[jax.ShapeDtypeStruct((s, D_MODEL), F32), jax.ShapeDtypeStruct((s, D_MODEL), BF16),
                   jax.ShapeDtypeStruct((s, D_MODEL), BF16),
                   jax.ShapeDtypeStruct((m, 2 * D_MODEL), F32), jax.ShapeDtypeStruct((1, D_MODEL), F32)],
        compiler_params=_params(("arbitrary",)),
    )(dh2, wo, q, kv, wq, h1, g)


def _mem_kv_bwd(dkv, mn, wkv, mem, g, after=()):
    m = mem.shape[0]

    def body(dkv_ref, mn_ref, w_ref, mem_ref, g_ref, dw_ref, dg_ref):
        dmn = jnp.zeros((m, D_MODEL), F32)
        mn = mn_ref[...]
        for j in range(4):
            dj = dkv_ref[:, j * 512:(j + 1) * 512].astype(BF16)
            dw_ref[j] = _dot_tn(mn, dj)
            dmn = dmn + _dot_nt(dj, w_ref[j])
        mv = mem_ref[...]
        dg_ref[...] = _rowsum(dmn * (mv * _rms_stats(mv)))

    return _tied_call(
        body, after, name="mem_kv_bwd", in_specs=[pl.BlockSpec(memory_space=pltpu.VMEM)] * 5,
        out_shape=[jax.ShapeDtypeStruct((4, D_MODEL, 512), F32), jax.ShapeDtypeStruct((1, D_MODEL), F32)],
        compiler_params=pltpu.CompilerParams(vmem_limit_bytes=VMEM_LIMIT_BYTES),
    )(dkv, mn, wkv, mem, g)


def _seqmix_bwd(dh1, x, z, c1, w_out, w_in, g_mix, cw, lng, lnb, gg, gb, wpair, wpair_t, bias, t, after=()):
    s = x.shape[0]
    nt = s // t
    halo_blocks = t // CONV_HALO

    def body(dh1_ref, x_ref, z_ref, zh_ref, c1_ref, wo_ref, wi_ref, gm_ref, cw_ref, lng_ref, lnb_ref,
             gg_ref, gb_ref, wpair_ref, wpt_ref, bias_ref,
             gx_ref, dz_ref, dcw_ref, dcb_ref, dlng_ref, dlnb_ref, dgg_ref, dgb_ref, dws_ref, dbs_ref,
             dbin_ref, dgm_ref, abuf, dbuf, ash, dsh, mixed_ref, dv_ref):
        i = pl.program_id(0)
        tile = nt - 1 - i
        accs = (dcw_ref, dcb_ref, dlng_ref, dlnb_ref, dgg_ref, dgb_ref, dws_ref, dbs_ref, dbin_ref, dgm_ref)

        @pl.when(i == 0)
        def _():
            for r in accs:
                r[...] = jnp.zeros_like(r)
            dbuf[t:t + CONV_HALO, :] = jnp.zeros((CONV_HALO, CONV_WIDTH), F32)

        @pl.when(i > 0)
        def _():
            dbuf[t:t + CONV_HALO, :] = dbuf[0:CONV_HALO, :]

        dmix = _dot_nt(dh1_ref[...].astype(BF16), wo_ref[...])

        xh, rs = _ln_stats(c1_ref[...])
        lng = lng_ref[...]
        ln = xh * lng + lnb_ref[...]
        sl = _sigmoid(ln)
        dln = dmix[:, 0:512] * (sl * (1.0 + ln * (1.0 - sl)))
        dc1, dg_ln, db_ln = _ln_bwd(dln, xh, rs, lng)
        dlng_ref[...] += dg_ln
        dlnb_ref[...] += db_ln
        dcb_ref[...] += _rowsum(dc1)
        dbuf[0:t, :] = dc1

        zh = zh_ref[...]
        a_halo = zh[:, 0:512] * _sigmoid(zh[:, 512:1024])
        abuf[0:CONV_HALO, :] = jnp.where(tile > 0, a_halo, 0.0)
        za = z_ref[:, 0:512]
        sg = _sigmoid(z_ref[:, 512:1024])
        abuf[CONV_HALO:, :] = za * sg
        _shift_rows(abuf, ash, t)
        _shift_rows(dbuf, dsh, t)

        da = jnp.zeros((t, CONV_WIDTH), F32)
        for k in range(CONV_KERNEL):
            da = da + cw_ref[k:k + 1, :] * _window(dbuf, dsh, CONV_KERNEL - 1 - k, t)
            dcw_ref[k:k + 1, :] += _rowsum(dc1 * _window(abuf, ash, CONV_HALO - (CONV_KERNEL - 1) + k, t))
        dza = da * sg
        dzg = da * za * (sg * (1.0 - sg))
        dz_ref[:, 0:512] = dza.astype(BF16)
        dz_ref[:, 512:1024] = dzg.astype(BF16)
        dbin_ref[:, 0:512] += _rowsum(dza)
        dbin_ref[:, 512:1024] += _rowsum(dzg)

        dgm = dmix[:, 512:1024]
        u, du_dz = _gelu_parts(z_ref[:, 1024:1536])
        gv, dgv_dz = _gelu_parts(z_ref[:, 1536:2048])
        vxh, vrs = _ln_stats(gv)
        ggv = gg_ref[...]
        v = vxh * ggv + gb_ref[...]
        low = _lane_is_low_head()
        v_lo = jnp.where(low, v, 0.0).astype(BF16)
        v_hi = jnp.where(low, 0.0, v).astype(BF16)
        _gm_mix(v_lo, v_hi, wpair_ref, bias_ref, mixed_ref, t)
        dzu = dgm * mixed_ref[...] * du_dz
        dm = dgm * u
        dm_lo = jnp.where(low, dm, 0.0).astype(BF16)
        dm_hi = jnp.where(low, 0.0, dm).astype(BF16)
        vb = v.astype(BF16)
        tril = (lax.broadcasted_iota(jnp.int32, (CHUNK, CHUNK), 1)
                <= lax.broadcasted_iota(jnp.int32, (CHUNK, CHUNK), 0))
        for n in range(t // CHUNK):
            rows = slice(n * CHUNK, (n + 1) * CHUNK)
            dbs_ref[...] += dm[rows, :]
            for j in range(GM_HEADS // 2):
                cols = slice(j * LANES, (j + 1) * LANES)
                stack = jnp.concatenate([dm_lo[rows, cols], dm_hi[rows, cols]], axis=0)
                dws = _dot_nt(stack, vb[rows, cols])
                dws_ref[2 * j] += jnp.where(tril, dws[0:CHUNK], 0.0)
                dws_ref[2 * j + 1] += jnp.where(tril, dws[CHUNK:2 * CHUNK], 0.0)
                dv_ref[rows, cols] = _dot(wpt_ref[j], stack)
        dgv, dg_gm, db_gm = _ln_bwd(dv_ref[...], vxh, vrs, ggv)
        dgg_ref[...] += dg_gm
        dgb_ref[...] += db_gm
        dzv = dgv * dgv_dz
        dz_ref[:, 1024:1536] = dzu.astype(BF16)
        dz_ref[:, 1536:2048] = dzv.astype(BF16)
        dbin_ref[:, 1024:1536] += _rowsum(dzu)
        dbin_ref[:, 1536:2048] += _rowsum(dzv)

        dhn = jnp.zeros((t, D_MODEL), F32)
        for j in range(4):
            dhn = dhn + _dot_nt(dz_ref[:, j * 512:(j + 1) * 512], wi_ref[j])
        xv = x_ref[...]
        dv, dg = _rms_bwd(dhn, xv, _rms_stats(xv), gm_ref[...])
        gx_ref[...] = dh1_ref[...] + dv
        dgm_ref[...] += dg

    rev = lambda w: pl.BlockSpec((t, w), lambda i: (nt - 1 - i, 0))
    const = lambda *shape: pl.BlockSpec(shape, lambda i: (0,) * len(shape))
    halo = pl.BlockSpec((CONV_HALO, D_MODEL), lambda i: (jnp.maximum((nt - 1 - i) * halo_blocks - 1, 0), 0))
    f32 = lambda *shape: jax.ShapeDtypeStruct(shape, F32)
    return _tied_call(
        body, after, name="seqmix_bwd", grid=(nt,),
        in_specs=[rev(D_MODEL), rev(D_MODEL), rev(2048), halo, rev(CONV_WIDTH),
                  const(D_MODEL, D_MODEL), const(4, D_MODEL, 512), const(1, D_MODEL),
                  const(CONV_HALO, CONV_WIDTH), const(1, 512), const(1, 512), const(1, 512), const(1, 512),
                  const(4, CHUNK, 2 * CHUNK), const(4, CHUNK, 2 * CHUNK), const(CHUNK, GM_WIDTH)],
        out_specs=[rev(D_MODEL), rev(2048),
                   const(CONV_HALO, CONV_WIDTH), const(1, 512), const(1, 512), const(1, 512), const(1, 512),
                   const(1, 512), const(GM_HEADS, CHUNK, CHUNK), const(CHUNK, GM_WIDTH), const(1, 2048),
                   const(1, D_MODEL)],
        out_shape=[f32(s, D_MODEL), jax.ShapeDtypeStruct((s, 2048), BF16),
                   f32(CONV_HALO, CONV_WIDTH), f32(1, 512), f32(1, 512), f32(1, 512), f32(1, 512),
                   f32(1, 512), f32(GM_HEADS, CHUNK, CHUNK), f32(CHUNK, GM_WIDTH), f32(1, 2048),
                   f32(1, D_MODEL)],
        scratch_shapes=[pltpu.VMEM((t + CONV_HALO, CONV_WIDTH), F32), pltpu.VMEM((t + CONV_HALO, CONV_WIDTH), F32),
                        pltpu.VMEM((SUBLANES - 1, t + CONV_HALO - SUBLANES, CONV_WIDTH), F32),
                        pltpu.VMEM((SUBLANES - 1, t + CONV_HALO - SUBLANES, CONV_WIDTH), F32),
                        pltpu.VMEM((t, GM_WIDTH), F32), pltpu.VMEM((t, GM_WIDTH), F32)],
        compiler_params=_params(("arbitrary",)),
    )(dh1, x, z, z, c1, w_out, w_in, g_mix, cw, lng, lnb, gg, gb, wpair, wpair_t, bias)


def _head_bias_grad(dbs):
    def body(d_ref, o_ref):
        dv = d_ref[...]
        lane = lax.broadcasted_iota(jnp.int32, (CHUNK, LANES), 1)
        acc = jnp.zeros((CHUNK, LANES), F32)
        for h in range(GM_HEADS):
            sh = jnp.sum(dv[:, h * GM_HEAD_DIM:(h + 1) * GM_HEAD_DIM], axis=-1, keepdims=True)
            acc = acc + jnp.where(lane == h, sh, 0.0)
        o_ref[...] = acc

    return pl.pallas_call(body, name="head_bias_grad",
                          out_shape=jax.ShapeDtypeStruct((CHUNK, LANES), F32))(dbs)


def kernel(x, mem, norm_mix_g, w_in, b_in, conv_w, conv_b, conv_ln_g, conv_ln_b, gm_ln_g, gm_ln_b, gm_w_s, gm_b_s, w_out, norm_xa_g, mem_norm_g, xa_wq, xa_wkv, xa_wo, norm_ffn_g, ffn_w_gate_up, ffn_w_down, final_norm_g, loss_target, m_norm_mix_g, m_w_in, m_b_in, m_conv_w, m_conv_b, m_conv_ln_g, m_conv_ln_b, m_gm_ln_g, m_gm_ln_b, m_gm_w_s, m_gm_b_s, m_w_out, m_norm_xa_g, m_mem_norm_g, m_xa_wq, m_xa_wkv, m_xa_wo, m_norm_ffn_g, m_ffn_w_gate_up, m_ffn_w_down, m_final_norm_g, v_norm_mix_g, v_w_in, v_b_in, v_conv_w, v_conv_b, v_conv_ln_g, v_conv_ln_b, v_gm_ln_g, v_gm_ln_b, v_gm_w_s, v_gm_b_s, v_w_out, v_norm_xa_g, v_mem_norm_g, v_xa_wq, v_xa_wkv, v_xa_wo, v_norm_ffn_g, v_ffn_w_gate_up, v_ffn_w_down, v_final_norm_g):
    weights = dict(norm_mix_g=norm_mix_g, w_in=w_in, b_in=b_in, conv_w=conv_w, conv_b=conv_b, conv_ln_g=conv_ln_g,
                   conv_ln_b=conv_ln_b, gm_ln_g=gm_ln_g, gm_ln_b=gm_ln_b, gm_w_s=gm_w_s, gm_b_s=gm_b_s, w_out=w_out,
                   norm_xa_g=norm_xa_g, mem_norm_g=mem_norm_g, xa_wq=xa_wq, xa_wkv=xa_wkv, xa_wo=xa_wo,
                   norm_ffn_g=norm_ffn_g, ffn_w_gate_up=ffn_w_gate_up, ffn_w_down=ffn_w_down,
                   final_norm_g=final_norm_g)
    m_in = dict(norm_mix_g=m_norm_mix_g, w_in=m_w_in, b_in=m_b_in, conv_w=m_conv_w, conv_b=m_conv_b,
                conv_ln_g=m_conv_ln_g, conv_ln_b=m_conv_ln_b, gm_ln_g=m_gm_ln_g, gm_ln_b=m_gm_ln_b, gm_w_s=m_gm_w_s,
                gm_b_s=m_gm_b_s, w_out=m_w_out, norm_xa_g=m_norm_xa_g, mem_norm_g=m_mem_norm_g, xa_wq=m_xa_wq,
                xa_wkv=m_xa_wkv, xa_wo=m_xa_wo, norm_ffn_g=m_norm_ffn_g, ffn_w_gate_up=m_ffn_w_gate_up,
                ffn_w_down=m_ffn_w_down, final_norm_g=m_final_norm_g)
    v_in = dict(norm_mix_g=v_norm_mix_g, w_in=v_w_in, b_in=v_b_in, conv_w=v_conv_w, conv_b=v_conv_b,
                conv_ln_g=v_conv_ln_g, conv_ln_b=v_conv_ln_b, gm_ln_g=v_gm_ln_g, gm_ln_b=v_gm_ln_b, gm_w_s=v_gm_w_s,
                gm_b_s=v_gm_b_s, w_out=v_w_out, norm_xa_g=v_norm_xa_g, mem_norm_g=v_mem_norm_g, xa_wq=v_xa_wq,
                xa_wkv=v_xa_wkv, xa_wo=v_xa_wo, norm_ffn_g=v_norm_ffn_g, ffn_w_gate_up=v_ffn_w_gate_up,
                ffn_w_down=v_ffn_w_down, final_norm_g=v_final_norm_g)
    grads, delta, new_m, new_v = {}, {}, {}, {}

    s = x.shape[1]
    ts = _row_tile(s)
    tb = max(CHUNK, ts // 2)
    tw = 2 * ts if s % (2 * ts) == 0 and ts >= 512 else ts
    cx, cy, cc = _mesh_pos()
    chip = 2 * cx + cy
    pos = jnp.stack([chip, cc]).astype(jnp.int32)
    row = lambda a: a.reshape(1, -1)
    x2, mem2, tgt2 = x[0], mem[0], loss_target[0]

    big = dict(w_in=w_in, xa_wkv=xa_wkv, w_out=w_out, xa_wq=xa_wq, xa_wo=xa_wo,
               ffn_w_gate_up=ffn_w_gate_up, ffn_w_down=ffn_w_down)
    big_names = list(big)
    halves = lambda a: a.reshape(2, a.shape[0] // 2, a.shape[1])
    conv_w_pad = jnp.pad(conv_w, ((0, CONV_HALO - CONV_KERNEL), (0, 0)))
    first_names = ["w_in", "conv_w"]
    later_names = [nm for nm in big_names if nm != "w_in"]
    cast = dict(zip(first_names, _cast_into_slots([halves(w_in), halves(conv_w_pad)], pos, [BF16, F32], "cast_w_in")))
    cast.update(zip(later_names, _cast_into_slots([halves(big[nm]) for nm in later_names], pos,
                                                  [BF16] * len(later_names), "cast_" + later_names[0])))

    def start_gather(names, after):
        return _gather_start([cast[nm] for nm in names], "gather_start_" + names[0], after)

    def finish_gather(names, started, after):
        send_sems, recv_sems, bufs, _ = started
        landed = _gather_wait(send_sems, recv_sems, bufs, after, "gather_wait_" + names[0])
        return dict(zip(names, (b.reshape(N_CHIPS, -1, b.shape[-1])
                                for b in _pass_to_sibling(landed, "pass_" + names[0]))))

    attn_names = ["w_out", "xa_wq", "xa_wkv", "xa_wo"]
    gather_first = start_gather(first_names, ())
    gw = finish_gather(first_names, gather_first, [cast[nm] for nm in later_names])
    w_in_g = gw["w_in"]
    cw_g = jnp.concatenate([gw["conv_w"][k] for k in range(N_CHIPS)], axis=1)
    gather_attn = start_gather(attn_names, w_in_g)

    tril = jnp.tril(jnp.ones((CHUNK, CHUNK), dtype=bool))
    ws = jnp.where(tril[None], gm_w_s, 0.0)
    wpair = jnp.concatenate([ws[0::2], ws[1::2]], axis=2).astype(BF16)
    ws_t = jnp.swapaxes(ws, 1, 2)
    wpair_t = jnp.concatenate([ws_t[0::2], ws_t[1::2]], axis=2).astype(BF16)
    bias = jnp.repeat(gm_b_s.T, GM_HEAD_DIM, axis=1)

    z, hn1 = _mix_in(x2, row(norm_mix_g), w_in_g, row(b_in), tw, after=gather_attn[3])
    mix, c1 = _seqmix_fwd(z, cw_g, row(conv_b), row(conv_ln_g), row(conv_ln_b), row(gm_ln_g), row(gm_ln_b),
                          wpair, bias, ts)
    gw = finish_gather(attn_names, gather_attn, mix)
    w_out_g = gw["w_out"].reshape(D_MODEL, D_MODEL)
    wq_g = gw["xa_wq"].reshape(D_MODEL, D_MODEL)
    wkv_g = gw["xa_wkv"]
    wo_g = gw["xa_wo"].reshape(D_MODEL, D_MODEL)
    gather_gu = start_gather(["ffn_w_gate_up"], w_out_g)
    h1, hn2, q = _out_proj_q(x2, mix, w_out_g, row(norm_xa_g), wq_g, tw, after=gather_gu[3])
    mn, kv = _mem_kv(mem2, row(mem_norm_g), wkv_g)
    o, h2, hn3 = _attn_fwd(q, kv, h1, wo_g, row(norm_ffn_g), ts)
    wgu_g = finish_gather(["ffn_w_gate_up"], gather_gu, hn3)["ffn_w_gate_up"]
    gather_down = start_gather(["ffn_w_down"], wgu_g)
    gu, act = _ffn_up(hn3, wgu_g.reshape(2, 2, D_MODEL, FFN_HALF), tw, after=gather_down[3])
    wd_g = finish_gather(["ffn_w_down"], gather_down, act)["ffn_w_down"].reshape(FFN_HIDDEN, D_MODEL)
    dh3, dh3_b, sq, d_final_g = _ffn_down_loss(act, wd_g, h2, row(final_norm_g), tgt2, ts)
    loss_here = jnp.broadcast_to(0.5 * jnp.sum(sq) / D_MODEL, (1, 2, SUBLANES, LANES))

    def split(g, nm):
        r, c = big[nm].shape
        return g.reshape(N_CHIPS, 2, r // 2, c)

    def chip_sums(group, arrays, got):
        sums, parts = [None] * len(group), [None] * len(group)
        for blocks in (N_CHIPS, 1):
            idx = [i for i, a in enumerate(arrays) if a.shape[0] == blocks]
            if idx:
                out = _add_halves([arrays[i] for i in idx], [got[i] for i in idx], pos, "chip_sum_" + group[idx[0]],
                                  [F32 if group[i] == "loss" else BF16 for i in idx])
                for k, i in enumerate(idx):
                    sums[i], parts[i] = out[0][k], out[1][k]
        return sums, parts

    def start_swap(group, grads):
        return _swap_start([split(g, nm) for g, nm in zip(grads, group)], "swap_start_" + group[0])

    def start_exchange(group, swapping, after):
        sems, arrays, lands, _ = swapping
        arrays, got = _swap_wait(sems, arrays, lands, after, "swap_wait_" + group[0])
        sums, parts = chip_sums(group, arrays, got)
        return _exchange_start(sums, parts, "exchange_start_" + group[0])

    def finish_exchange(group, started, after):
        sems, sums, parts, _ = started
        parts = _exchange_wait(sems, sums, parts, after, "exchange_wait_" + group[0])
        return _sum_chips(parts, pos, "total_" + group[0])

    def join_and_update(group, after):
        joined = _join_halves([halves_of[nm] for nm in group], "join_halves_" + group[0], after)
        for nm, j in zip(group, joined):
            grads[nm] = j.reshape(big[nm].shape)
        outs = _adamw([(weights[nm], grads[nm], m_in[nm], v_in[nm]) for nm in group], "adamw_" + group[0])
        for nm, (d, nm_, nv_) in zip(group, outs):
            delta[nm], new_m[nm], new_v[nm] = d, nm_, nv_
        return [new_v[nm] for nm in group]

    as3 = lambda a: a.reshape((1,) + a.shape)
    halves_of = {}

    dgu = _ffn_bwd_act(dh3_b, wd_g.reshape(2, FFN_HALF, D_MODEL), gu, tw)
    g_down = _grad_w(act, as3(dh3_b), FFN_HALF, D_MODEL, "grad_ffn_w_down")
    group_a = ["ffn_w_down"]
    swap_a = start_swap(group_a, [g_down])
    dh2, dh2_b, d_ffn_g = _ffn_bwd_in(dgu, wgu_g, dh3, h2, row(norm_ffn_g), ts, after=swap_a[3])
    exch_a = start_exchange(group_a, swap_a, dh2)
    g_gu = _grad_w(hn3, dgu, D_MODEL, FFN_HALF, "grad_ffn_w_gate_up", after=exch_a[3])
    halves_of.update(zip(group_a, finish_exchange(group_a, exch_a, g_gu)))

    group_b = ["ffn_w_gate_up"]
    swap_b = start_swap(group_b, [g_gu])
    dh1, dh1_b, dq, dkv, d_xa_g = _attn_bwd(dh2, wo_g, q, kv, wq_g, h1, row(norm_xa_g), ts, after=swap_b[3])
    exch_b = start_exchange(group_b, swap_b, dh1)
    g_wkv, d_mem_g = _mem_kv_bwd(dkv, mn, wkv_g, mem2, row(mem_norm_g), after=exch_b[3])
    g_wo = _grad_w(o, as3(dh2_b), D_MODEL, D_MODEL, "grad_xa_wo", after=exch_b[3])
    g_wq = _grad_w(hn2, as3(dq), D_MODEL, D_MODEL, "grad_xa_wq", after=exch_b[3])
    g_wout = _grad_w(mix, as3(dh1_b), D_MODEL, D_MODEL, "grad_w_out", after=exch_b[3])
    halves_of.update(zip(group_b, finish_exchange(group_b, exch_b, (g_wkv, g_wo, g_wq, g_wout))))

    group_c = ["xa_wo", "xa_wq", "xa_wkv", "w_out"]
    swap_c = start_swap(group_c, [g_wo, g_wq, g_wkv, g_wout])
    (gx, dz, d_cw, d_cb, d_lng, d_lnb, d_gg, d_gb, d_ws, d_bs_sum, d_bin, d_mix_g) = _seqmix_bwd(
        dh1, x2, z, c1, w_out_g, w_in_g, row(norm_mix_g), cw_g, row(conv_ln_g), row(conv_ln_b),
        row(gm_ln_g), row(gm_ln_b), wpair, wpair_t, bias, tb, after=swap_c[3])
    d_bs = _head_bias_grad(d_bs_sum)[:, :GM_HEADS].T
    exch_c = start_exchange(group_c, swap_c, dz)
    g_win = _grad_w(hn1, as3(dz), D_MODEL, 512, "grad_w_in", after=exch_c[3])
    done_ab = join_and_update(group_a + group_b, g_win)
    halves_of.update(zip(group_c, finish_exchange(group_c, exch_c, (g_win, *done_ab))))

    small_names = ["norm_mix_g", "b_in", "conv_w", "conv_b", "conv_ln_g", "conv_ln_b", "gm_ln_g", "gm_ln_b",
                   "gm_w_s", "gm_b_s", "norm_xa_g", "mem_norm_g", "norm_ffn_g", "final_norm_g"]
    d_cw_by_chip = jnp.swapaxes(d_cw.reshape(CONV_HALO, N_CHIPS, LANES), 0, 1).reshape(-1, LANES)
    small_grads = dict(norm_mix_g=d_mix_g, b_in=d_bin, conv_w=d_cw_by_chip, conv_b=d_cb, conv_ln_g=d_lng,
                       conv_ln_b=d_lnb, gm_ln_g=d_gg, gm_ln_b=d_gb, gm_w_s=d_ws, gm_b_s=d_bs, norm_xa_g=d_xa_g,
                       mem_norm_g=d_mem_g, norm_ffn_g=d_ffn_g, final_norm_g=d_final_g)

    def rows_form(a):
        a = a.reshape(-1, LANES)
        return jnp.pad(a, ((0, -a.shape[0] % SUBLANES), (0, 0)))

    pieces = [rows_form(small_grads[nm]) for nm in small_names]
    offsets, total = [], 0
    for p in pieces:
        offsets.append(total)
        total += p.shape[0]
    pack_rows = -(-total // 32) * 32
    small_pack = jnp.pad(jnp.concatenate(pieces, axis=0), ((0, pack_rows - total), (0, 0)))

    group_d = ["w_in", "small", "loss"]
    arrays_d = [split(g_win, "w_in"), small_pack.reshape(1, 2, pack_rows // 2, LANES), loss_here]
    sums_d, parts_d = chip_sums(group_d, arrays_d, _swap_halves(arrays_d, "swap_halves_w_in"))
    exch_d = _exchange_start(sums_d, parts_d, "exchange_start_w_in")
    done_c = join_and_update(group_c, exch_d[3])
    halves_of.update(zip(group_d, finish_exchange(group_d, exch_d, done_c)))
    joined_d = _join_halves([halves_of[nm] for nm in group_d], "join_halves_w_in")
    grads["w_in"] = joined_d[0].reshape(w_in.shape)
    loss = joined_d[2][0, 0, 0]
    delta["w_in"], new_m["w_in"], new_v["w_in"] = _adamw([(w_in, grads["w_in"], m_w_in, v_w_in)], "adamw_w_in")[0]

    local_rows = lambda a, nm: a if nm == "conv_w" else a.reshape(-1, LANES)
    params = [tuple(local_rows(src[nm], nm) for src in (weights, m_in, v_in)) for nm in small_names]
    outs = _adamw_small(joined_d[1].reshape(pack_rows, LANES), pos, params, offsets, small_names.index("conv_w"))
    for k, nm in enumerate(small_names):
        for dst, a in zip((grads, delta, new_m, new_v), outs[4 * k:4 * k + 4]):
            dst[nm] = a

    order = ["norm_mix_g", "w_in", "b_in", "conv_w", "conv_b", "conv_ln_g", "conv_ln_b", "gm_ln_g", "gm_ln_b",
             "gm_w_s", "gm_b_s", "w_out", "norm_xa_g", "mem_norm_g", "xa_wq", "xa_wkv", "xa_wo", "norm_ffn_g",
             "ffn_w_gate_up", "ffn_w_down", "final_norm_g"]
    fit = lambda a, nm: a.reshape(weights[nm].shape)
    return (loss, gx.reshape(x.shape),
            *[fit(grads[nm], nm) for nm in order], *[fit(delta[nm], nm) for nm in order],
            *[fit(new_m[nm], nm) for nm in order], *[fit(new_v[nm], nm) for nm in order])
```

```python
import functools

import jax
import jax.numpy as jnp
from jax import lax
from jax.experimental import pallas as pl
from jax.experimental.pallas import tpu as pltpu

F32 = jnp.float32
BF16 = jnp.bfloat16

D_MODEL = 1024
CONV_WIDTH = 512
GM_WIDTH = 512
CONV_KERNEL = 31
CONV_HALO = 32
GRAD_ROWS = 2048
CHUNK = 128
GM_HEADS = 8
GM_HEAD_DIM = 64
XA_HEADS = 4
XA_HEAD_DIM = 256
FFN_HIDDEN = 2816
FFN_HALF = FFN_HIDDEN // 2
RMS_EPS = 1e-6
LN_EPS = 1e-5
N_CHIPS = 4
LANES = 128
SUBLANES = 8

ADAM_LR = 0.001
ADAM_B1 = 0.9
ADAM_B2 = 0.999
ADAM_EPS = 1e-08
ADAM_WD = 0.01
ADAM_STEP = 10

VMEM_LIMIT_BYTES = 56 * 1024 * 1024
MESH = pl.DeviceIdType.MESH
ANY = pl.BlockSpec(memory_space=pl.ANY)
HBM_SPEC = pl.BlockSpec(memory_space=pltpu.HBM)
SEM_SPEC = pl.BlockSpec(memory_space=pltpu.SEMAPHORE)

_NT = (((1,), (1,)), ((), ()))
_TN = (((0,), (0,)), ((), ()))
_GELU_C = 0.7978845608028654
_GELU_A = 0.044715


def _dot(a, b):
    return jnp.dot(a, b, preferred_element_type=F32)


def _dot_nt(a, b):
    return lax.dot_general(a, b, _NT, preferred_element_type=F32)


def _dot_tn(a, b):
    return lax.dot_general(a, b, _TN, preferred_element_type=F32)


def _mean(v):
    return jnp.mean(v, axis=-1, keepdims=True)


def _rowsum(v):
    return jnp.sum(v, axis=0, keepdims=True)


def _sigmoid(v):
    return 1.0 / (1.0 + jnp.exp(-v))


def _gelu_parts(v):
    v2 = v * v
    t = jnp.tanh(_GELU_C * (v + _GELU_A * v * v2))
    g = 0.5 * v * (1.0 + t)
    dg = 0.5 * (1.0 + t) + 0.5 * v * (1.0 - t * t) * (_GELU_C * (1.0 + 3.0 * _GELU_A * v2))
    return g, dg


def _rms_stats(v):
    return lax.rsqrt(_mean(v * v) + RMS_EPS)


def _rms_bwd(dy, v, r, g):
    n = v * r
    dn = dy * g
    dv = r * (dn - n * _mean(dn * n))
    return dv, _rowsum(dy * n)


def _ln_stats(v):
    mu = _mean(v)
    xc = v - mu
    rs = lax.rsqrt(_mean(xc * xc) + LN_EPS)
    return xc * rs, rs


def _ln_bwd(dy, xh, rs, g):
    dxh = dy * g
    dv = rs * (dxh - _mean(dxh) - xh * _mean(dxh * xh))
    return dv, _rowsum(dy * xh), _rowsum(dy)


def _params(sem):
    return pltpu.CompilerParams(dimension_semantics=sem, vmem_limit_bytes=VMEM_LIMIT_BYTES)


def _row_tile(s):
    return 512 if s % 512 == 0 and s >= 2048 else 128


def _mesh_pos():
    return lax.axis_index("x"), lax.axis_index("y"), lax.axis_index("c")


def _cast_into_slots(ws, pos, dtypes, name):
    n = len(ws)

    def body(pos_ref, *refs):
        for a in range(n):
            refs[n + a][0] = refs[a][...].astype(dtypes[a])

    return pl.pallas_call(
        body, name=name,
        grid_spec=pltpu.PrefetchScalarGridSpec(
            num_scalar_prefetch=1, grid=(2,),
            in_specs=[pl.BlockSpec((1,) + w.shape[1:], lambda i, p: (i, 0, 0)) for w in ws],
            out_specs=[pl.BlockSpec((1, 1) + w.shape[1:], lambda i, p: (p[0], i, 0, 0)) for w in ws]),
        out_shape=[jax.ShapeDtypeStruct((N_CHIPS,) + w.shape, dt) for w, dt in zip(ws, dtypes)],
        compiler_params=_params(("parallel",)),
    )(pos, *ws)


def _adam_update(w, g, m, v):
    nm = ADAM_B1 * m + (1.0 - ADAM_B1) * g
    nv = ADAM_B2 * v + (1.0 - ADAM_B2) * (g * g)
    m_hat = nm / (1.0 - ADAM_B1 ** ADAM_STEP)
    v_hat = nv / (1.0 - ADAM_B2 ** ADAM_STEP)
    return -ADAM_LR * (m_hat / (jnp.sqrt(v_hat) + ADAM_EPS) + ADAM_WD * w), nm, nv


ADAM_STEPS = 4


def _adamw(quads, name, after=()):
    n = len(quads)

    def body(*refs):
        ins, outs = refs[:4 * n], refs[4 * n:]
        for a in range(n):
            w, g, m, v = (r[...] for r in ins[4 * a:4 * a + 4])
            outs[3 * a][...], outs[3 * a + 1][...], outs[3 * a + 2][...] = _adam_update(w, g, m, v)

    specs = [pl.BlockSpec((q[0].shape[0] // ADAM_STEPS, q[0].shape[1]), lambda i: (i, 0)) for q in quads]
    out = _tied_call(
        body, after, name=name, grid=(ADAM_STEPS,),
        in_specs=[sp for sp in specs for _ in range(4)], out_specs=[sp for sp in specs for _ in range(3)],
        out_shape=[jax.ShapeDtypeStruct(q[0].shape, F32) for q in quads for _ in range(3)],
        compiler_params=_params(("parallel",)),
    )(*[a for q in quads for a in q])
    return [tuple(out[3 * a:3 * a + 3]) for a in range(n)]


def _adamw_small(gpack, pos, params, offsets, conv_at):
    n = len(params)

    def body(pos_ref, g_ref, *refs):
        ins, outs = refs[:3 * n], refs[3 * n:]
        for k in range(n):
            rows = params[k][0].shape[0]
            start = offsets[k]
            if k == conv_at:
                start = pl.multiple_of(start + pos_ref[0] * CONV_HALO, SUBLANES)
            g = g_ref[pl.ds(start, rows), :]
            outs[4 * k][...] = g
            outs[4 * k + 1][...], outs[4 * k + 2][...], outs[4 * k + 3][...] = _adam_update(
                ins[3 * k][...], g, ins[3 * k + 1][...], ins[3 * k + 2][...])

    flat = [a for p in params for a in p]
    vmem = pl.BlockSpec(memory_space=pltpu.VMEM)
    return pl.pallas_call(
        body, name="adamw_small",
        in_specs=[pl.BlockSpec(memory_space=pltpu.SMEM), vmem] + [vmem] * len(flat),
        out_specs=[vmem] * (4 * n),
        out_shape=[jax.ShapeDtypeStruct(p[0].shape, F32) for p in params for _ in range(4)],
    )(pos, gpack, *flat)


def _as_tuple(after):
    return tuple(after) if isinstance(after, (tuple, list)) else (after,)


def _tied_call(body, after, *, in_specs, **kwargs):
    after = _as_tuple(after)
    n_in, n_after = len(in_specs), len(after)

    def tied(*refs):
        body(*refs[:n_in], *refs[n_in + n_after:])

    call = pl.pallas_call(tied, in_specs=list(in_specs) + [ANY] * n_after, **kwargs)
    return lambda *operands: call(*operands, *after)


def _other_chips(x, y):
    return [(1 - x, y), (x, 1 - y), (1 - x, 1 - y)]


def _gather_descriptors(bufs, send_of, recv_of):
    x, y, c = _mesh_pos()
    me = 2 * x + y
    chips = _other_chips(x, y)
    sends, arrivals = [], []
    for a in range(len(bufs)):
        for k in range(3):
            ck = 2 * chips[k][0] + chips[k][1]

            def copy(slot, a=a, k=k):
                return pltpu.make_async_remote_copy(
                    src_ref=bufs[a].at[slot, c], dst_ref=bufs[a].at[slot, c],
                    send_sem=send_of(a, k), recv_sem=recv_of(a, k),
                    device_id=(*chips[k], c), device_id_type=MESH)

            sends.append(functools.partial(copy, me))
            arrivals.append(functools.partial(copy, ck))
    return sends, arrivals


def _gather_start(bufs, name, after=()):
    n = len(bufs)
    ns = 3 * n

    def body(*refs):
        sems = refs[n:n + 2 * ns]
        thru = refs[n + 2 * ns:2 * n + 2 * ns]
        token = refs[2 * n + 2 * ns]
        sends, _ = _gather_descriptors(thru, lambda a, k: sems[3 * a + k], lambda a, k: sems[ns + 3 * a + k])
        for cp in sends:
            cp().start()
        token[...] = jnp.zeros_like(token)

    held = [pltpu.with_memory_space_constraint(b, pltpu.HBM) for b in bufs]
    out = _tied_call(
        body, after, name=name,
        out_shape=(*[pltpu.SemaphoreType.DMA(())] * (2 * ns), *[pltpu.HBM(b.shape, b.dtype) for b in held],
                   jax.ShapeDtypeStruct((8, LANES), F32)),
        in_specs=[HBM_SPEC] * n,
        out_specs=(*[SEM_SPEC] * (2 * ns), *[HBM_SPEC] * n, pl.BlockSpec(memory_space=pltpu.VMEM)),
        input_output_aliases={i: 2 * ns + i for i in range(n)},
        compiler_params=pltpu.CompilerParams(has_side_effects=pltpu.SideEffectType.DATAFLOW_SIDE_EFFECTING),
    )(*held)
    return list(out[:ns]), list(out[ns:2 * ns]), list(out[2 * ns:2 * ns + n]), out[2 * ns + n]


def _gather_wait(send_sems, recv_sems, bufs, after, name):
    n = len(bufs)
    ns = 3 * n

    def body(*refs):
        buf_ref = refs[:n]
        sem_ref = refs[n:n + 2 * ns]
        sends, arrivals = _gather_descriptors(buf_ref, lambda a, k: sem_ref[3 * a + k],
                                              lambda a, k: sem_ref[ns + 3 * a + k])
        for cp in sends:
            cp().wait_send()
        for cp in arrivals:
            cp().wait_recv()

    out = pl.pallas_call(
        body, name=name,
        out_shape=tuple(pltpu.HBM(b.shape, b.dtype) for b in bufs),
        in_specs=[HBM_SPEC] * n + [SEM_SPEC] * (2 * ns) + [ANY] * len(_as_tuple(after)),
        out_specs=tuple([HBM_SPEC] * n),
        input_output_aliases={i: i for i in range(n)},
        compiler_params=pltpu.CompilerParams(has_side_effects=pltpu.SideEffectType.DATAFLOW_SIDE_EFFECTING),
    )(*bufs, *send_sems, *recv_sems, *_as_tuple(after))
    return list(out)


def _pass_to_sibling(bufs, name):
    n = len(bufs)

    def body(*refs):
        outs = refs[n:2 * n]
        send_sem, recv_sem = refs[2 * n:]
        x, y, c = _mesh_pos()
        chips = _other_chips(x, y)

        def half(a, k, which):
            ck = 2 * chips[k][0] + chips[k][1]
            return pltpu.make_async_remote_copy(
                src_ref=outs[a].at[ck, which], dst_ref=outs[a].at[ck, which],
                send_sem=send_sem.at[a, k], recv_sem=recv_sem.at[a, k],
                device_id=(x, y, 1 - c), device_id_type=MESH)

        sends = [half(a, k, c) for a in range(n) for k in range(3)]
        for cp in sends:
            cp.start()
        for a in range(n):
            for k in range(3):
                half(a, k, 1 - c).wait_recv()
        for cp in sends:
            cp.wait_send()

    return pl.pallas_call(
        body, name=name,
        in_specs=[ANY] * n, out_specs=[ANY] * n,
        out_shape=[jax.ShapeDtypeStruct(b.shape, b.dtype) for b in bufs],
        input_output_aliases={a: a for a in range(n)},
        scratch_shapes=[pltpu.SemaphoreType.DMA((n, 3))] * 2,
    )(*bufs)


def _swap_halves(grads, name):
    n = len(grads)

    def body(*refs):
        ins, outs = refs[:n], refs[n:2 * n]
        send_sem, recv_sem = refs[2 * n:]
        x, y, c = _mesh_pos()
        cps = [pltpu.make_async_remote_copy(
            src_ref=ins[a].at[:, pl.ds(1 - c, 1)], dst_ref=outs[a],
            send_sem=send_sem.at[a], recv_sem=recv_sem.at[a],
            device_id=(x, y, 1 - c), device_id_type=MESH) for a in range(n)]
        for cp in cps:
            cp.start()
        for cp in cps:
            cp.wait()

    out_shape = [jax.ShapeDtypeStruct((g.shape[0], 1) + g.shape[2:], g.dtype) for g in grads]
    return pl.pallas_call(
        body, name=name,
        in_specs=[ANY] * n, out_specs=[ANY] * n, out_shape=out_shape,
        scratch_shapes=[pltpu.SemaphoreType.DMA((n,))] * 2,
    )(*grads)


def _swap_descriptors(grads, lands, send_of, recv_of):
    x, y, c = _mesh_pos()
    return [functools.partial(
        pltpu.make_async_remote_copy,
        src_ref=grads[a].at[:, pl.ds(1 - c, 1)], dst_ref=lands[a],
        send_sem=send_of(a), recv_sem=recv_of(a),
        device_id=(x, y, 1 - c), device_id_type=MESH) for a in range(len(grads))]


def _swap_start(grads, name):
    n = len(grads)

    def body(*refs):
        sems = refs[2 * n:4 * n]
        g_thru, l_thru = refs[4 * n:5 * n], refs[5 * n:6 * n]
        token = refs[6 * n]
        for cp in _swap_descriptors(g_thru, l_thru, lambda a: sems[a], lambda a: sems[n + a]):
            cp().start()
        token[...] = jnp.zeros_like(token)

    lands = [lax.empty((g.shape[0], 1) + g.shape[2:], g.dtype) for g in grads]
    held = [pltpu.with_memory_space_constraint(a, pltpu.HBM) for a in (*grads, *lands)]
    out = pl.pallas_call(
        body, name=name,
        out_shape=(*[pltpu.SemaphoreType.DMA(())] * (2 * n), *[pltpu.HBM(a.shape, a.dtype) for a in held],
                   jax.ShapeDtypeStruct((8, LANES), F32)),
        in_specs=[HBM_SPEC] * (2 * n),
        out_specs=(*[SEM_SPEC] * (2 * n), *[HBM_SPEC] * (2 * n), pl.BlockSpec(memory_space=pltpu.VMEM)),
        input_output_aliases={i: 2 * n + i for i in range(2 * n)},
        compiler_params=pltpu.CompilerParams(has_side_effects=pltpu.SideEffectType.DATAFLOW_SIDE_EFFECTING),
    )(*held)
    return list(out[:2 * n]), list(out[2 * n:3 * n]), list(out[3 * n:4 * n]), out[4 * n]


def _swap_wait(sems, grads, lands, after, name):
    n = len(grads)

    def body(*refs):
        g_ref, l_ref = refs[:n], refs[n:2 * n]
        sem_ref = refs[2 * n:4 * n]
        for cp in _swap_descriptors(g_ref, l_ref, lambda a: sem_ref[a], lambda a: sem_ref[n + a]):
            cp().wait()

    out = pl.pallas_call(
        body, name=name,
        out_shape=tuple(pltpu.HBM(a.shape, a.dtype) for a in (*grads, *lands)),
        in_specs=[HBM_SPEC] * (2 * n) + [SEM_SPEC] * (2 * n) + [ANY] * len(_as_tuple(after)),
        out_specs=tuple([HBM_SPEC] * (2 * n)),
        input_output_aliases={i: i for i in range(2 * n)},
        compiler_params=pltpu.CompilerParams(has_side_effects=pltpu.SideEffectType.DATAFLOW_SIDE_EFFECTING),
    )(*grads, *lands, *sems, *_as_tuple(after))
    return list(out[:n]), list(out[n:])


def _add_halves(gs, gots, pos, name, dtypes):
    n = len(gs)
    j = gs[0].shape[0]

    def body(pos_ref, *refs):
        g_refs, r_refs = refs[:n], refs[n:2 * n]
        o_refs, p_refs = refs[2 * n:3 * n], refs[3 * n:]
        vals = [(g_refs[a][0, 0] + r_refs[a][0, 0]).astype(dtypes[a]) for a in range(n)]
        for a in range(n):
            o_refs[a][0] = vals[a]
        if j == 1:
            for a in range(n):
                p_refs[a][0] = vals[a]
        else:
            @pl.when(pl.program_id(0) == pos_ref[0])
            def _():
                for a in range(n):
                    p_refs[a][0] = vals[a]

    blk = lambda g: (1,) + g.shape[2:]
    out = pl.pallas_call(
        body, name=name,
        grid_spec=pltpu.PrefetchScalarGridSpec(
            num_scalar_prefetch=1, grid=(j,),
            in_specs=[pl.BlockSpec((1,) + blk(g), lambda i, p: (i, p[1], 0, 0)) for g in gs]
            + [pl.BlockSpec((1,) + blk(g), lambda i, p: (i, 0, 0, 0)) for g in gs],
            out_specs=[pl.BlockSpec(blk(g), lambda i, p: (i, 0, 0)) for g in gs]
            + [pl.BlockSpec(blk(g), lambda i, p: (p[0], 0, 0)) for g in gs]),
        out_shape=[jax.ShapeDtypeStruct((j,) + g.shape[2:], dt) for g, dt in zip(gs, dtypes)]
        + [jax.ShapeDtypeStruct((N_CHIPS,) + g.shape[2:], dt) for g, dt in zip(gs, dtypes)],
        compiler_params=_params(("arbitrary",)),
    )(pos, *gs, *gots)
    return list(out[:n]), list(out[n:])


def _exchange_descriptors(sums, parts, send_of, recv_of):
    x, y, c = _mesh_pos()
    me = 2 * x + y
    chips = _other_chips(x, y)
    sends, arrivals = [], []
    for a in range(len(sums)):
        for k in range(3):
            ck = 2 * chips[k][0] + chips[k][1]
            mine = sums[a].at[ck] if sums[a].shape[0] == N_CHIPS else sums[a].at[0]

            def copy(dst_slot, a=a, k=k, mine=mine):
                return pltpu.make_async_remote_copy(
                    src_ref=mine, dst_ref=parts[a].at[dst_slot],
                    send_sem=send_of(a, k), recv_sem=recv_of(a, k),
                    device_id=(*chips[k], c), device_id_type=MESH)

            sends.append(functools.partial(copy, me))
            arrivals.append(functools.partial(copy, ck))
    return sends, arrivals


def _exchange_start(sums, parts, name):
    n = len(sums)
    ns = 3 * n

    def body(*refs):
        sems = refs[2 * n:2 * n + 2 * ns]
        sums_thru = refs[2 * n + 2 * ns:3 * n + 2 * ns]
        parts_thru = refs[3 * n + 2 * ns:4 * n + 2 * ns]
        token = refs[4 * n + 2 * ns]
        sends, _ = _exchange_descriptors(sums_thru, parts_thru, lambda a, k: sems[3 * a + k],
                                         lambda a, k: sems[ns + 3 * a + k])
        for cp in sends:
            cp().start()
        token[...] = jnp.zeros_like(token)

    hbm = lambda a: pltpu.HBM(a.shape, a.dtype)
    held = [pltpu.with_memory_space_constraint(a, pltpu.HBM) for a in (*sums, *parts)]
    out = pl.pallas_call(
        body, name=name,
        out_shape=(*[pltpu.SemaphoreType.DMA(())] * (2 * ns), *[hbm(a) for a in held],
                   jax.ShapeDtypeStruct((8, LANES), F32)),
        in_specs=[HBM_SPEC] * (2 * n),
        out_specs=(*[SEM_SPEC] * (2 * ns), *[HBM_SPEC] * (2 * n), pl.BlockSpec(memory_space=pltpu.VMEM)),
        input_output_aliases={i: 2 * ns + i for i in range(2 * n)},
        compiler_params=pltpu.CompilerParams(has_side_effects=pltpu.SideEffectType.DATAFLOW_SIDE_EFFECTING),
    )(*held)
    return (list(out[:2 * ns]), list(out[2 * ns:2 * ns + n]), list(out[2 * ns + n:2 * ns + 2 * n]),
            out[2 * ns + 2 * n])


def _exchange_wait(sems, sums, parts, after, name):
    n = len(sums)
    ns = 3 * n

    def body(*refs):
        sums_ref, parts_ref = refs[:n], refs[n:2 * n]
        sem_ref = refs[2 * n:2 * n + 2 * ns]
        sends, arrivals = _exchange_descriptors(sums_ref, parts_ref, lambda a, k: sem_ref[3 * a + k],
                                                lambda a, k: sem_ref[ns + 3 * a + k])
        for cp in sends:
            cp().wait_send()
        for cp in arrivals:
            cp().wait_recv()

    hbm = lambda a: pltpu.HBM(a.shape, a.dtype)
    out = pl.pallas_call(
        body, name=name,
        out_shape=tuple(hbm(a) for a in (*sums, *parts)),
        in_specs=[HBM_SPEC] * (2 * n) + [SEM_SPEC] * (2 * ns) + [ANY] * len(_as_tuple(after)),
        out_specs=tuple([HBM_SPEC] * (2 * n)),
        input_output_aliases={i: i for i in range(2 * n)},
        compiler_params=pltpu.CompilerParams(has_side_effects=pltpu.SideEffectType.DATAFLOW_SIDE_EFFECTING),
    )(*sums, *parts, *sems, *_as_tuple(after))
    return list(out[n:])


def _sum_chips(parts, pos, name):
    n = len(parts)

    def body(pos_ref, *refs):
        for a in range(n):
            p_ref = refs[a]
            refs[n + a][0] = (((p_ref[0].astype(F32) + p_ref[1].astype(F32)) + p_ref[2].astype(F32))
                              + p_ref[3].astype(F32))

    out = pl.pallas_call(
        body, name=name,
        grid_spec=pltpu.PrefetchScalarGridSpec(
            num_scalar_prefetch=1, grid=(1,),
            in_specs=[pl.BlockSpec(p.shape, lambda i, q: (0, 0, 0)) for p in parts],
            out_specs=[pl.BlockSpec((1,) + p.shape[1:], lambda i, q: (q[1], 0, 0)) for p in parts]),
        out_shape=[jax.ShapeDtypeStruct((2,) + p.shape[1:], F32) for p in parts],
        compiler_params=_params(("arbitrary",)),
    )(pos, *parts)
    return list(out)


def _join_halves(fulls, name, after=()):
    n = len(fulls)

    def body(*refs):
        outs = refs[n:2 * n]
        send_sem, recv_sem = refs[2 * n:]
        x, y, c = _mesh_pos()

        def half(a, which):
            return pltpu.make_async_remote_copy(
                src_ref=outs[a].at[which], dst_ref=outs[a].at[which],
                send_sem=send_sem.at[a], recv_sem=recv_sem.at[a],
                device_id=(x, y, 1 - c), device_id_type=MESH)

        sends = [half(a, c) for a in range(n)]
        for cp in sends:
            cp.start()
        for a in range(n):
            half(a, 1 - c).wait_recv()
        for cp in sends:
            cp.wait_send()

    out_shape = [jax.ShapeDtypeStruct(f.shape, f.dtype) for f in fulls]
    return _tied_call(
        body, after, name=name,
        in_specs=[ANY] * n, out_specs=[ANY] * n, out_shape=out_shape,
        input_output_aliases={a: a for a in range(n)},
        scratch_shapes=[pltpu.SemaphoreType.DMA((n,))] * 2,
    )(*fulls)


def _mix_in(x, g, w_in, b_in, ts, after=()):
    s = x.shape[0]

    def body(x_ref, g_ref, w_ref, b_ref, z_ref, hn_ref):
        xv = x_ref[...]
        hn = (xv * _rms_stats(xv) * g_ref[...]).astype(BF16)
        hn_ref[...] = hn
        for j in range(4):
            cols = slice(j * 512, (j + 1) * 512)
            z_ref[:, cols] = _dot(hn, w_ref[j]) + b_ref[:, cols]

    return _tied_call(
        body, after, name="mix_in", grid=(s // ts,),
        in_specs=[pl.BlockSpec((ts, D_MODEL), lambda i: (i, 0)),
                  pl.BlockSpec((1, D_MODEL), lambda i: (0, 0)),
                  pl.BlockSpec((4, D_MODEL, 512), lambda i: (0, 0, 0)),
                  pl.BlockSpec((1, 2048), lambda i: (0, 0))],
        out_specs=[pl.BlockSpec((ts, 2048), lambda i: (i, 0)),
                   pl.BlockSpec((ts, D_MODEL), lambda i: (i, 0))],
        out_shape=[jax.ShapeDtypeStruct((s, 2048), F32), jax.ShapeDtypeStruct((s, D_MODEL), BF16)],
        compiler_params=_params(("parallel",)),
    )(x, g, w_in, b_in)


def _shift_rows(buf, shifted, t):
    rows = t + CONV_HALO - SUBLANES
    for r in range(1, SUBLANES):
        shifted[r - 1, 0:rows, :] = buf[pl.ds(r, rows), :]


def _window(buf, shifted, offset, t):
    r = offset % SUBLANES
    if r == 0:
        return buf[pl.ds(offset, t), :]
    return shifted[r - 1, pl.ds(offset - r, t), :]


def _lane_is_low_head():
    lane = lax.broadcasted_iota(jnp.int32, (1, GM_WIDTH), 1)
    return (lane & GM_HEAD_DIM) == 0


def _gm_mix(v_lo, v_hi, wpair_ref, bias_ref, mixed_ref, t):
    for n in range(t // CHUNK):
        rows = slice(n * CHUNK, (n + 1) * CHUNK)
        for j in range(GM_HEADS // 2):
            cols = slice(j * LANES, (j + 1) * LANES)
            rhs = jnp.concatenate([v_lo[rows, cols], v_hi[rows, cols]], axis=0)
            mixed_ref[rows, cols] = _dot(wpair_ref[j], rhs) + bias_ref[:, cols]


def _seqmix_fwd(z, cw, cb, lng, lnb, gg, gb, wpair, bias, t):
    s = z.shape[0]

    def body(z_ref, cw_ref, cb_ref, lng_ref, lnb_ref, gg_ref, gb_ref, wpair_ref, bias_ref,
             mix_ref, c1_ref, abuf, ash, mixed_ref):
        i = pl.program_id(0)

        @pl.when(i == 0)
        def _():
            abuf[0:CONV_HALO, :] = jnp.zeros((CONV_HALO, CONV_WIDTH), F32)

        @pl.when(i > 0)
        def _():
            abuf[0:CONV_HALO, :] = abuf[t:t + CONV_HALO, :]

        abuf[CONV_HALO:, :] = z_ref[:, 0:512] * _sigmoid(z_ref[:, 512:1024])
        _shift_rows(abuf, ash, t)
        acc = jnp.zeros((t, CONV_WIDTH), F32)
        for k in range(CONV_KERNEL):
            acc = acc + cw_ref[k:k + 1, :] * _window(abuf, ash, CONV_HALO - (CONV_KERNEL - 1) + k, t)
        c1 = acc + cb_ref[...]
        c1_ref[...] = c1
        xh, _ = _ln_stats(c1)
        ln = xh * lng_ref[...] + lnb_ref[...]
        mix_ref[:, 0:512] = (ln * _sigmoid(ln)).astype(BF16)

        u, _ = _gelu_parts(z_ref[:, 1024:1536])
        gv, _ = _gelu_parts(z_ref[:, 1536:2048])
        vxh, _ = _ln_stats(gv)
        v = vxh * gg_ref[...] + gb_ref[...]
        low = _lane_is_low_head()
        v_lo = jnp.where(low, v, 0.0).astype(BF16)
        v_hi = jnp.where(low, 0.0, v).astype(BF16)
        _gm_mix(v_lo, v_hi, wpair_ref, bias_ref, mixed_ref, t)
        mix_ref[:, 512:1024] = (u * mixed_ref[...]).astype(BF16)

    vec = lambda n: pl.BlockSpec((1, n), lambda i: (0, 0))
    return pl.pallas_call(
        body, name="seqmix_fwd", grid=(s // t,),
        in_specs=[pl.BlockSpec((t, 2048), lambda i: (i, 0)),
                  pl.BlockSpec((CONV_HALO, CONV_WIDTH), lambda i: (0, 0)),
                  vec(512), vec(512), vec(512), vec(512), vec(512),
                  pl.BlockSpec((4, CHUNK, 2 * CHUNK), lambda i: (0, 0, 0)),
                  pl.BlockSpec((CHUNK, GM_WIDTH), lambda i: (0, 0))],
        out_specs=[pl.BlockSpec((t, D_MODEL), lambda i: (i, 0)),
                   pl.BlockSpec((t, CONV_WIDTH), lambda i: (i, 0))],
        out_shape=[jax.ShapeDtypeStruct((s, D_MODEL), BF16), jax.ShapeDtypeStruct((s, CONV_WIDTH), F32)],
        scratch_shapes=[pltpu.VMEM((t + CONV_HALO, CONV_WIDTH), F32),
                        pltpu.VMEM((SUBLANES - 1, t + CONV_HALO - SUBLANES, CONV_WIDTH), F32),
                        pltpu.VMEM((t, GM_WIDTH), F32)],
        compiler_params=_params(("arbitrary",)),
    )(z, cw, cb, lng, lnb, gg, gb, wpair, bias)


def _out_proj_q(x, mix, w_out, g, wq, ts, after=()):
    s = x.shape[0]

    def body(x_ref, mix_ref, wo_ref, g_ref, wq_ref, h1_ref, hn_ref, q_ref):
        h1 = x_ref[...] + _dot(mix_ref[...], wo_ref[...])
        h1_ref[...] = h1
        hn = (h1 * _rms_stats(h1) * g_ref[...]).astype(BF16)
        hn_ref[...] = hn
        q_ref[...] = _dot(hn, wq_ref[...]).astype(BF16)

    row = lambda dt: pl.BlockSpec((ts, D_MODEL), lambda i: (i, 0))
    full = pl.BlockSpec((D_MODEL, D_MODEL), lambda i: (0, 0))
    return _tied_call(
        body, after, name="out_proj_q", grid=(s // ts,),
        in_specs=[row(F32), row(BF16), full, pl.BlockSpec((1, D_MODEL), lambda i: (0, 0)), full],
        out_specs=[row(F32), row(BF16), row(BF16)],
        out_shape=[jax.ShapeDtypeStruct((s, D_MODEL), F32), jax.ShapeDtypeStruct((s, D_MODEL), BF16),
                   jax.ShapeDtypeStruct((s, D_MODEL), BF16)],
        compiler_params=_params(("parallel",)),
    )(x, mix, w_out, g, wq)


def _mem_kv(mem, g, wkv):
    m = mem.shape[0]

    def body(mem_ref, g_ref, w_ref, mn_ref, kv_ref):
        mv = mem_ref[...]
        mn = (mv * _rms_stats(mv) * g_ref[...]).astype(BF16)
        mn_ref[...] = mn
        for j in range(4):
            kv_ref[:, j * 512:(j + 1) * 512] = _dot(mn, w_ref[j]).astype(BF16)

    return pl.pallas_call(
        body, name="mem_kv",
        out_shape=[jax.ShapeDtypeStruct((m, D_MODEL), BF16), jax.ShapeDtypeStruct((m, 2 * D_MODEL), BF16)],
        compiler_params=pltpu.CompilerParams(vmem_limit_bytes=VMEM_LIMIT_BYTES),
    )(mem, g, wkv)


def _softmax_rows(sc):
    e = jnp.exp(sc - jnp.max(sc, axis=-1, keepdims=True))
    return e / jnp.sum(e, axis=-1, keepdims=True)


def _attn_fwd(q, kv, h1, wo, g, ts):
    s, m = q.shape[0], kv.shape[0]
    scale = XA_HEAD_DIM ** -0.5

    def body(q_ref, kv_ref, h1_ref, wo_ref, g_ref, o_ref, h2_ref, hn_ref):
        for h in range(XA_HEADS):
            cols = slice(h * XA_HEAD_DIM, (h + 1) * XA_HEAD_DIM)
            vcols = slice(D_MODEL + h * XA_HEAD_DIM, D_MODEL + (h + 1) * XA_HEAD_DIM)
            p = _softmax_rows(_dot_nt(q_ref[:, cols], kv_ref[:, cols]) * scale)
            o_ref[:, cols] = _dot(p.astype(BF16), kv_ref[:, vcols]).astype(BF16)
        h2 = h1_ref[...] + _dot(o_ref[...], wo_ref[...])
        h2_ref[...] = h2
        hn_ref[...] = (h2 * _rms_stats(h2) * g_ref[...]).astype(BF16)

    row = pl.BlockSpec((ts, D_MODEL), lambda i: (i, 0))
    return pl.pallas_call(
        body, name="attn_fwd", grid=(s // ts,),
        in_specs=[row, pl.BlockSpec((m, 2 * D_MODEL), lambda i: (0, 0)), row,
                  pl.BlockSpec((D_MODEL, D_MODEL), lambda i: (0, 0)),
                  pl.BlockSpec((1, D_MODEL), lambda i: (0, 0))],
        out_specs=[row, row, row],
        out_shape=[jax.ShapeDtypeStruct((s, D_MODEL), BF16), jax.ShapeDtypeStruct((s, D_MODEL), F32),
                   jax.ShapeDtypeStruct((s, D_MODEL), BF16)],
        compiler_params=_params(("parallel",)),
    )(q, kv, h1, wo, g)


_FFN_CHUNKS_FWD = (slice(0, 6 * LANES), slice(6 * LANES, FFN_HALF))
_FFN_CHUNKS_BWD = (slice(0, 4 * LANES), slice(4 * LANES, 8 * LANES), slice(8 * LANES, FFN_HALF))


def _ffn_up(hn, wgu, ts, after=()):
    s = hn.shape[0]

    def body(hn_ref, w_ref, gu_ref, act_ref):
        hv = hn_ref[...]
        for cols in _FFN_CHUNKS_FWD:
            gate = _dot(hv, w_ref[0, 0, :, cols])
            up = _dot(hv, w_ref[1, 0, :, cols])
            gu_ref[0, :, cols] = gate.astype(BF16)
            gu_ref[1, :, cols] = up.astype(BF16)
            act_ref[:, cols] = (gate * _sigmoid(gate) * up).astype(BF16)

    return _tied_call(
        body, after, name="ffn_up", grid=(2, s // ts),
        in_specs=[pl.BlockSpec((ts, D_MODEL), lambda j, i: (i, 0)),
                  pl.BlockSpec((2, 1, D_MODEL, FFN_HALF), lambda j, i: (0, j, 0, 0))],
        out_specs=[pl.BlockSpec((2, ts, FFN_HALF), lambda j, i: (0, i, j)),
                   pl.BlockSpec((ts, FFN_HALF), lambda j, i: (i, j))],
        out_shape=[jax.ShapeDtypeStruct((2, s, FFN_HIDDEN), BF16), jax.ShapeDtypeStruct((s, FFN_HIDDEN), BF16)],
        compiler_params=_params(("parallel", "parallel")),
    )(hn, wgu)


def _ffn_down_loss(act, wd, h2, g, target, ts):
    s = act.shape[0]

    def body(act_ref, wd_ref, h2_ref, g_ref, t_ref, dh_ref, dhb_ref, sq_ref, dg_ref):
        @pl.when(pl.program_id(0) == 0)
        def _():
            sq_ref[...] = jnp.zeros_like(sq_ref)
            dg_ref[...] = jnp.zeros_like(dg_ref)

        h3 = h2_ref[...] + _dot(act_ref[...], wd_ref[...])
        r = _rms_stats(h3)
        gv = g_ref[...]
        diff = h3 * r * gv - t_ref[...]
        sq_ref[...] += _rowsum(diff * diff)
        dh, dg = _rms_bwd(diff / D_MODEL, h3, r, gv)
        dh_ref[...] = dh
        dhb_ref[...] = dh.astype(BF16)
        dg_ref[...] += dg

    row = pl.BlockSpec((ts, D_MODEL), lambda i: (i, 0))
    vec = pl.BlockSpec((1, D_MODEL), lambda i: (0, 0))
    return pl.pallas_call(
        body, name="ffn_down_loss", grid=(s // ts,),
        in_specs=[pl.BlockSpec((ts, FFN_HIDDEN), lambda i: (i, 0)),
                  pl.BlockSpec((FFN_HIDDEN, D_MODEL), lambda i: (0, 0)), row, vec, row],
        out_specs=[row, row, vec, vec],
        out_shape=[jax.ShapeDtypeStruct((s, D_MODEL), F32), jax.ShapeDtypeStruct((s, D_MODEL), BF16),
                   jax.ShapeDtypeStruct((1, D_MODEL), F32), jax.ShapeDtypeStruct((1, D_MODEL), F32)],
        compiler_params=_params(("arbitrary",)),
    )(act, wd, h2, g, target)


def _grad_w(a, b, tk, tn, name, after=()):
    s, k = a.shape
    gb, _, n = b.shape
    nblk = n // tn
    tsr = GRAD_ROWS if s % GRAD_ROWS == 0 else s

    def body(a_ref, b_ref, o_ref):
        part = _dot_tn(a_ref[...], b_ref[0])

        @pl.when(pl.program_id(2) == 0)
        def _():
            o_ref[0] = part

        @pl.when(pl.program_id(2) > 0)
        def _():
            o_ref[0] += part

    return _tied_call(
        body, after, name=name, grid=(gb * nblk, k // tk, s // tsr),
        in_specs=[pl.BlockSpec((tsr, tk), lambda ni, ki, si: (si, ki)),
                  pl.BlockSpec((1, tsr, tn), lambda ni, ki, si: (ni // nblk, si, ni % nblk))],
        out_specs=pl.BlockSpec((1, tk, tn), lambda ni, ki, si: (ni, ki, 0)),
        out_shape=jax.ShapeDtypeStruct((gb * nblk, k, tn), F32),
        compiler_params=_params(("parallel", "parallel", "arbitrary")),
    )(a, b)


def _ffn_bwd(dh3, wd, gu, wgu, h2, g, t, after=()):
    s = dh3.shape[0]

    def body(dh3_ref, wd_ref, gu_ref, w_ref, h2_ref, g_ref, dgu_ref, dh2_ref, dh2b_ref, dg_ref):
        @pl.when(pl.program_id(0) == 0)
        def _():
            dg_ref[...] = jnp.zeros_like(dg_ref)

        dh3v = dh3_ref[...]
        dhb = dh3v.astype(BF16)
        dhn = jnp.zeros((t, D_MODEL), F32)
        for j in range(2):
            for cols in _FFN_CHUNKS_BWD:
                whole = slice(j * FFN_HALF + cols.start, j * FFN_HALF + cols.stop)
                dact = _dot_nt(dhb, wd_ref[j, cols, :])
                gate, up = gu_ref[0, :, whole].astype(F32), gu_ref[1, :, whole].astype(F32)
                sg = _sigmoid(gate)
                dgate = (dact * up * (sg * (1.0 + gate * (1.0 - sg)))).astype(BF16)
                dup = (dact * (gate * sg)).astype(BF16)
                dgu_ref[0, :, whole] = dgate
                dgu_ref[1, :, whole] = dup
                dhn = dhn + _dot_nt(dgate, w_ref[j, :, cols]) + _dot_nt(dup, w_ref[2 + j, :, cols])
        h2 = h2_ref[...]
        dv, dg = _rms_bwd(dhn, h2, _rms_stats(h2), g_ref[...])
        dh2 = dh3v + dv
        dh2_ref[...] = dh2
        dh2b_ref[...] = dh2.astype(BF16)
        dg_ref[...] += dg

    row = pl.BlockSpec((t, D_MODEL), lambda i: (i, 0))
    wide = pl.BlockSpec((2, t, FFN_HIDDEN), lambda i: (0, i, 0))
    vec = pl.BlockSpec((1, D_MODEL), lambda i: (0, 0))
    return _tied_call(
        body, after, name="ffn_bwd", grid=(s // t,),
        in_specs=[row, pl.BlockSpec((2, FFN_HALF, D_MODEL), lambda i: (0, 0, 0)), wide,
                  pl.BlockSpec((4, D_MODEL, FFN_HALF), lambda i: (0, 0, 0)), row, vec],
        out_specs=[wide, row, row, vec],
        out_shape=[jax.ShapeDtypeStruct((2, s, FFN_HIDDEN), BF16), jax.ShapeDtypeStruct((s, D_MODEL), F32),
                   jax.ShapeDtypeStruct((s, D_MODEL), BF16), jax.ShapeDtypeStruct((1, D_MODEL), F32)],
        compiler_params=_params(("arbitrary",)),
    )(dh3, wd, gu, wgu, h2, g)


def _attn_bwd(dh2, wo, q, kv, wq, h1, g, ts, after=()):
    s, m = q.shape[0], kv.shape[0]
    scale = XA_HEAD_DIM ** -0.5

    def body(dh2_ref, wo_ref, q_ref, kv_ref, wq_ref, h1_ref, g_ref, dh1_ref, dh1b_ref, dq_ref, dkv_ref, dg_ref):
        @pl.when(pl.program_id(0) == 0)
        def _():
            dkv_ref[...] = jnp.zeros_like(dkv_ref)
            dg_ref[...] = jnp.zeros_like(dg_ref)

        do = _dot_nt(dh2_ref[...].astype(BF16), wo_ref[...]).astype(BF16)
        for h in range(XA_HEADS):
            cols = slice(h * XA_HEAD_DIM, (h + 1) * XA_HEAD_DIM)
            vcols = slice(D_MODEL + h * XA_HEAD_DIM, D_MODEL + (h + 1) * XA_HEAD_DIM)
            qh, kh, vh, doh = q_ref[:, cols], kv_ref[:, cols], kv_ref[:, vcols], do[:, cols]
            p = _softmax_rows(_dot_nt(qh, kh) * scale)
            dp = _dot_nt(doh, vh)
            ds = (p * (dp - jnp.sum(dp * p, axis=-1, keepdims=True)) * scale).astype(BF16)
            dq_ref[:, cols] = _dot(ds, kh).astype(BF16)
            dkv_ref[:, cols] += _dot_tn(ds, qh)
            dkv_ref[:, vcols] += _dot_tn(p.astype(BF16), doh)
        dhn = _dot_nt(dq_ref[...], wq_ref[...])
        h1 = h1_ref[...]
        dv, dg = _rms_bwd(dhn, h1, _rms_stats(h1), g_ref[...])
        dh1 = dh2_ref[...] + dv
        dh1_ref[...] = dh1
        dh1b_ref[...] = dh1.astype(BF16)
        dg_ref[...] += dg

    row = pl.BlockSpec((ts, D_MODEL), lambda i: (i, 0))
    full = pl.BlockSpec((D_MODEL, D_MODEL), lambda i: (0, 0))
    kvs = pl.BlockSpec((m, 2 * D_MODEL), lambda i: (0, 0))
    vec = pl.BlockSpec((1, D_MODEL), lambda i: (0, 0))
    return _tied_call(
        body, after, name="attn_bwd", grid=(s // ts,),
        in_specs=[row, full, row, kvs, full, row, vec],
        out_specs=[row, row, row, kvs, vec],
        out_shape=[jax.ShapeDtypeStruct((s, D_MODEL), F32), jax.ShapeDtypeStruct((s, D_MODEL), BF16),
                   jax.ShapeDtypeStruct((s, D_MODEL), BF16),
                   jax.ShapeDtypeStruct((m, 2 * D_MODEL), F32), jax.ShapeDtypeStruct((1, D_MODEL), F32)],
        compiler_params=_params(("arbitrary",)),
    )(dh2, wo, q, kv, wq, h1, g)


def _mem_kv_bwd(dkv, mn, wkv, mem, g, after=()):
    m = mem.shape[0]

    def body(dkv_ref, mn_ref, w_ref, mem_ref, g_ref, dw_ref, dg_ref):
        dmn = jnp.zeros((m, D_MODEL), F32)
        mn = mn_ref[...]
        for j in range(4):
            dj = dkv_ref[:, j * 512:(j + 1) * 512].astype(BF16)
            dw_ref[j] = _dot_tn(mn, dj)
            dmn = dmn + _dot_nt(dj, w_ref[j])
        mv = mem_ref[...]
        dg_ref[...] = _rowsum(dmn * (mv * _rms_stats(mv)))

    return _tied_call(
        body, after, name="mem_kv_bwd", in_specs=[pl.BlockSpec(memory_space=pltpu.VMEM)] * 5,
        out_shape=[jax.ShapeDtypeStruct((4, D_MODEL, 512), F32), jax.ShapeDtypeStruct((1, D_MODEL), F32)],
        compiler_params=pltpu.CompilerParams(vmem_limit_bytes=VMEM_LIMIT_BYTES),
    )(dkv, mn, wkv, mem, g)


def _seqmix_bwd(dh1, x, z, c1, w_out, w_in, g_mix, cw, lng, lnb, gg, gb, wpair, wpair_t, bias, t, after=()):
    s = x.shape[0]
    nt = s // t

    def body(dh1_ref, x_ref, z_ref, c1_ref, wo_ref, wi_ref, gm_ref, cw_ref, lng_ref, lnb_ref,
             gg_ref, gb_ref, wpair_ref, wpt_ref, bias_ref,
             gx_ref, dz_ref, dcw_ref, dcb_ref, dlng_ref, dlnb_ref, dgg_ref, dgb_ref, dws_ref, dbs_ref,
             dbin_ref, dgm_ref, dbuf, dsh, mixed_ref, dv_ref):
        i = pl.program_id(0)
        accs = (dcw_ref, dcb_ref, dlng_ref, dlnb_ref, dgg_ref, dgb_ref, dws_ref, dbs_ref, dbin_ref, dgm_ref)

        @pl.when(i == 0)
        def _():
            for r in accs:
                r[...] = jnp.zeros_like(r)
            dbuf[t:t + CONV_HALO, :] = jnp.zeros((CONV_HALO, CONV_WIDTH), F32)

        @pl.when(i > 0)
        def _():
            dbuf[t:t + CONV_HALO, :] = dbuf[0:CONV_HALO, :]

        dmix = _dot_nt(dh1_ref[...].astype(BF16), wo_ref[...])

        xh, rs = _ln_stats(c1_ref[...])
        lng = lng_ref[...]
        ln = xh * lng + lnb_ref[...]
        sl = _sigmoid(ln)
        dln = dmix[:, 0:512] * (sl * (1.0 + ln * (1.0 - sl)))
        dc1, dg_ln, db_ln = _ln_bwd(dln, xh, rs, lng)
        dlng_ref[...] += dg_ln
        dlnb_ref[...] += db_ln
        dcb_ref[...] += _rowsum(dc1)
        dbuf[0:t, :] = dc1

        za = z_ref[:, 0:512]
        sg = _sigmoid(z_ref[:, 512:1024])
        a = za * sg
        _shift_rows(dbuf, dsh, t)

        da = jnp.zeros((t, CONV_WIDTH), F32)
        for k in range(CONV_KERNEL):
            later = _window(dbuf, dsh, CONV_KERNEL - 1 - k, t)
            da = da + cw_ref[k:k + 1, :] * later
            dcw_ref[k:k + 1, :] += _rowsum(a * later)
        dza = da * sg
        dzg = da * za * (sg * (1.0 - sg))
        dz_ref[:, 0:512] = dza.astype(BF16)
        dz_ref[:, 512:1024] = dzg.astype(BF16)
        dbin_ref[:, 0:512] += _rowsum(dza)
        dbin_ref[:, 512:1024] += _rowsum(dzg)

        dgm = dmix[:, 512:1024]
        u, du_dz = _gelu_parts(z_ref[:, 1024:1536])
        gv, dgv_dz = _gelu_parts(z_ref[:, 1536:2048])
        vxh, vrs = _ln_stats(gv)
        ggv = gg_ref[...]
        v = vxh * ggv + gb_ref[...]
        low = _lane_is_low_head()
        v_lo = jnp.where(low, v, 0.0).astype(BF16)
        v_hi = jnp.where(low, 0.0, v).astype(BF16)
        _gm_mix(v_lo, v_hi, wpair_ref, bias_ref, mixed_ref, t)
        dzu = dgm * mixed_ref[...] * du_dz
        dm = dgm * u
        dm_lo = jnp.where(low, dm, 0.0).astype(BF16)
        dm_hi = jnp.where(low, 0.0, dm).astype(BF16)
        vb = v.astype(BF16)
        tril = (lax.broadcasted_iota(jnp.int32, (CHUNK, CHUNK), 1)
                <= lax.broadcasted_iota(jnp.int32, (CHUNK, CHUNK), 0))
        for n in range(t // CHUNK):
            rows = slice(n * CHUNK, (n + 1) * CHUNK)
            dbs_ref[...] += dm[rows, :]
            for j in range(GM_HEADS // 2):
                cols = slice(j * LANES, (j + 1) * LANES)
                stack = jnp.concatenate([dm_lo[rows, cols], dm_hi[rows, cols]], axis=0)
                dws = _dot_nt(stack, vb[rows, cols])
                dws_ref[2 * j] += jnp.where(tril, dws[0:CHUNK], 0.0)
                dws_ref[2 * j + 1] += jnp.where(tril, dws[CHUNK:2 * CHUNK], 0.0)
                dv_ref[rows, cols] = _dot(wpt_ref[j], stack)
        dgv, dg_gm, db_gm = _ln_bwd(dv_ref[...], vxh, vrs, ggv)
        dgg_ref[...] += dg_gm
        dgb_ref[...] += db_gm
        dzv = dgv * dgv_dz
        dz_ref[:, 1024:1536] = dzu.astype(BF16)
        dz_ref[:, 1536:2048] = dzv.astype(BF16)
        dbin_ref[:, 1024:1536] += _rowsum(dzu)
        dbin_ref[:, 1536:2048] += _rowsum(dzv)

        dhn = jnp.zeros((t, D_MODEL), F32)
        for j in range(4):
            dhn = dhn + _dot_nt(dz_ref[:, j * 512:(j + 1) * 512], wi_ref[j])
        xv = x_ref[...]
        dv, dg = _rms_bwd(dhn, xv, _rms_stats(xv), gm_ref[...])
        gx_ref[...] = dh1_ref[...] + dv
        dgm_ref[...] += dg

    rev = lambda w: pl.BlockSpec((t, w), lambda i: (nt - 1 - i, 0))
    const = lambda *shape: pl.BlockSpec(shape, lambda i: (0,) * len(shape))
    f32 = lambda *shape: jax.ShapeDtypeStruct(shape, F32)
    return _tied_call(
        body, after, name="seqmix_bwd", grid=(nt,),
        in_specs=[rev(D_MODEL), rev(D_MODEL), rev(2048), rev(CONV_WIDTH),
                  const(D_MODEL, D_MODEL), const(4, D_MODEL, 512), const(1, D_MODEL),
                  const(CONV_HALO, CONV_WIDTH), const(1, 512), const(1, 512), const(1, 512), const(1, 512),
                  const(4, CHUNK, 2 * CHUNK), const(4, CHUNK, 2 * CHUNK), const(CHUNK, GM_WIDTH)],
        out_specs=[rev(D_MODEL), rev(2048),
                   const(CONV_HALO, CONV_WIDTH), const(1, 512), const(1, 512), const(1, 512), const(1, 512),
                   const(1, 512), const(GM_HEADS, CHUNK, CHUNK), const(CHUNK, GM_WIDTH), const(1, 2048),
                   const(1, D_MODEL)],
        out_shape=[f32(s, D_MODEL), jax.ShapeDtypeStruct((s, 2048), BF16),
                   f32(CONV_HALO, CONV_WIDTH), f32(1, 512), f32(1, 512), f32(1, 512), f32(1, 512),
                   f32(1, 512), f32(GM_HEADS, CHUNK, CHUNK), f32(CHUNK, GM_WIDTH), f32(1, 2048),
                   f32(1, D_MODEL)],
        scratch_shapes=[pltpu.VMEM((t + CONV_HALO, CONV_WIDTH), F32),
                        pltpu.VMEM((SUBLANES - 1, t + CONV_HALO - SUBLANES, CONV_WIDTH), F32),
                        pltpu.VMEM((t, GM_WIDTH), F32), pltpu.VMEM((t, GM_WIDTH), F32)],
        compiler_params=_params(("arbitrary",)),
    )(dh1, x, z, c1, w_out, w_in, g_mix, cw, lng, lnb, gg, gb, wpair, wpair_t, bias)


def _head_bias_grad(dbs):
    def body(d_ref, o_ref):
        dv = d_ref[...]
        lane = lax.broadcasted_iota(jnp.int32, (CHUNK, LANES), 1)
        acc = jnp.zeros((CHUNK, LANES), F32)
        for h in range(GM_HEADS):
            sh = jnp.sum(dv[:, h * GM_HEAD_DIM:(h + 1) * GM_HEAD_DIM], axis=-1, keepdims=True)
            acc = acc + jnp.where(lane == h, sh, 0.0)
        o_ref[...] = acc

    return pl.pallas_call(body, name="head_bias_grad",
                          out_shape=jax.ShapeDtypeStruct((CHUNK, LANES), F32))(dbs)


def kernel(x, mem, norm_mix_g, w_in, b_in, conv_w, conv_b, conv_ln_g, conv_ln_b, gm_ln_g, gm_ln_b, gm_w_s, gm_b_s, w_out, norm_xa_g, mem_norm_g, xa_wq, xa_wkv, xa_wo, norm_ffn_g, ffn_w_gate_up, ffn_w_down, final_norm_g, loss_target, m_norm_mix_g, m_w_in, m_b_in, m_conv_w, m_conv_b, m_conv_ln_g, m_conv_ln_b, m_gm_ln_g, m_gm_ln_b, m_gm_w_s, m_gm_b_s, m_w_out, m_norm_xa_g, m_mem_norm_g, m_xa_wq, m_xa_wkv, m_xa_wo, m_norm_ffn_g, m_ffn_w_gate_up, m_ffn_w_down, m_final_norm_g, v_norm_mix_g, v_w_in, v_b_in, v_conv_w, v_conv_b, v_conv_ln_g, v_conv_ln_b, v_gm_ln_g, v_gm_ln_b, v_gm_w_s, v_gm_b_s, v_w_out, v_norm_xa_g, v_mem_norm_g, v_xa_wq, v_xa_wkv, v_xa_wo, v_norm_ffn_g, v_ffn_w_gate_up, v_ffn_w_down, v_final_norm_g):
    weights = dict(norm_mix_g=norm_mix_g, w_in=w_in, b_in=b_in, conv_w=conv_w, conv_b=conv_b, conv_ln_g=conv_ln_g,
                   conv_ln_b=conv_ln_b, gm_ln_g=gm_ln_g, gm_ln_b=gm_ln_b, gm_w_s=gm_w_s, gm_b_s=gm_b_s, w_out=w_out,
                   norm_xa_g=norm_xa_g, mem_norm_g=mem_norm_g, xa_wq=xa_wq, xa_wkv=xa_wkv, xa_wo=xa_wo,
                   norm_ffn_g=norm_ffn_g, ffn_w_gate_up=ffn_w_gate_up, ffn_w_down=ffn_w_down,
                   final_norm_g=final_norm_g)
    m_in = dict(norm_mix_g=m_norm_mix_g, w_in=m_w_in, b_in=m_b_in, conv_w=m_conv_w, conv_b=m_conv_b,
                conv_ln_g=m_conv_ln_g, conv_ln_b=m_conv_ln_b, gm_ln_g=m_gm_ln_g, gm_ln_b=m_gm_ln_b, gm_w_s=m_gm_w_s,
                gm_b_s=m_gm_b_s, w_out=m_w_out, norm_xa_g=m_norm_xa_g, mem_norm_g=m_mem_norm_g, xa_wq=m_xa_wq,
                xa_wkv=m_xa_wkv, xa_wo=m_xa_wo, norm_ffn_g=m_norm_ffn_g, ffn_w_gate_up=m_ffn_w_gate_up,
                ffn_w_down=m_ffn_w_down, final_norm_g=m_final_norm_g)
    v_in = dict(norm_mix_g=v_norm_mix_g, w_in=v_w_in, b_in=v_b_in, conv_w=v_conv_w, conv_b=v_conv_b,
                conv_ln_g=v_conv_ln_g, conv_ln_b=v_conv_ln_b, gm_ln_g=v_gm_ln_g, gm_ln_b=v_gm_ln_b, gm_w_s=v_gm_w_s,
                gm_b_s=v_gm_b_s, w_out=v_w_out, norm_xa_g=v_norm_xa_g, mem_norm_g=v_mem_norm_g, xa_wq=v_xa_wq,
                xa_wkv=v_xa_wkv, xa_wo=v_xa_wo, norm_ffn_g=v_norm_ffn_g, ffn_w_gate_up=v_ffn_w_gate_up,
                ffn_w_down=v_ffn_w_down, final_norm_g=v_final_norm_g)
    grads, delta, new_m, new_v = {}, {}, {}, {}

    s = x.shape[1]
    ts = _row_tile(s)
    tb = max(CHUNK, ts // 2)
    tw = 2 * ts if s % (2 * ts) == 0 and ts >= 512 else ts
    cx, cy, cc = _mesh_pos()
    chip = 2 * cx + cy
    pos = jnp.stack([chip, cc]).astype(jnp.int32)
    row = lambda a: a.reshape(1, -1)
    x2, mem2, tgt2 = x[0], mem[0], loss_target[0]

    big = dict(w_in=w_in, xa_wkv=xa_wkv, w_out=w_out, xa_wq=xa_wq, xa_wo=xa_wo,
               ffn_w_gate_up=ffn_w_gate_up, ffn_w_down=ffn_w_down)
    big_names = list(big)
    halves = lambda a: a.reshape(2, a.shape[0] // 2, a.shape[1])
    conv_w_pad = jnp.pad(conv_w, ((0, CONV_HALO - CONV_KERNEL), (0, 0)))
    first_names = ["w_in", "conv_w"]
    later_names = [nm for nm in big_names if nm != "w_in"]
    cast = dict(zip(first_names, _cast_into_slots([halves(w_in), halves(conv_w_pad)], pos, [BF16, F32], "cast_w_in")))
    cast.update(zip(later_names, _cast_into_slots([halves(big[nm]) for nm in later_names], pos,
                                                  [BF16] * len(later_names), "cast_" + later_names[0])))

    def start_gather(names, after):
        return _gather_start([cast[nm] for nm in names], "gather_start_" + names[0], after)

    def finish_gather(names, started, after):
        send_sems, recv_sems, bufs, _ = started
        landed = _gather_wait(send_sems, recv_sems, bufs, after, "gather_wait_" + names[0])
        return dict(zip(names, (b.reshape(N_CHIPS, -1, b.shape[-1])
                                for b in _pass_to_sibling(landed, "pass_" + names[0]))))

    attn_names = ["w_out", "xa_wq", "xa_wkv", "xa_wo"]
    gather_first = start_gather(first_names, ())
    gw = finish_gather(first_names, gather_first, [cast[nm] for nm in later_names])
    w_in_g = gw["w_in"]
    cw_g = jnp.concatenate([gw["conv_w"][k] for k in range(N_CHIPS)], axis=1)
    gather_attn = start_gather(attn_names, w_in_g)

    tril = jnp.tril(jnp.ones((CHUNK, CHUNK), dtype=bool))
    ws = jnp.where(tril[None], gm_w_s, 0.0)
    wpair = jnp.concatenate([ws[0::2], ws[1::2]], axis=2).astype(BF16)
    ws_t = jnp.swapaxes(ws, 1, 2)
    wpair_t = jnp.concatenate([ws_t[0::2], ws_t[1::2]], axis=2).astype(BF16)
    bias = jnp.repeat(gm_b_s.T, GM_HEAD_DIM, axis=1)

    z, hn1 = _mix_in(x2, row(norm_mix_g), w_in_g, row(b_in), tw, after=gather_attn[3])
    mix, c1 = _seqmix_fwd(z, cw_g, row(conv_b), row(conv_ln_g), row(conv_ln_b), row(gm_ln_g), row(gm_ln_b),
                          wpair, bias, ts)
    gw = finish_gather(attn_names, gather_attn, mix)
    w_out_g = gw["w_out"].reshape(D_MODEL, D_MODEL)
    wq_g = gw["xa_wq"].reshape(D_MODEL, D_MODEL)
    wkv_g = gw["xa_wkv"]
    wo_g = gw["xa_wo"].reshape(D_MODEL, D_MODEL)
    gather_gu = start_gather(["ffn_w_gate_up"], w_out_g)
    h1, hn2, q = _out_proj_q(x2, mix, w_out_g, row(norm_xa_g), wq_g, tw, after=gather_gu[3])
    mn, kv = _mem_kv(mem2, row(mem_norm_g), wkv_g)
    o, h2, hn3 = _attn_fwd(q, kv, h1, wo_g, row(norm_ffn_g), ts)
    wgu_g = finish_gather(["ffn_w_gate_up"], gather_gu, hn3)["ffn_w_gate_up"]
    gather_down = start_gather(["ffn_w_down"], wgu_g)
    gu, act = _ffn_up(hn3, wgu_g.reshape(2, 2, D_MODEL, FFN_HALF), tw, after=gather_down[3])
    wd_g = finish_gather(["ffn_w_down"], gather_down, act)["ffn_w_down"].reshape(FFN_HIDDEN, D_MODEL)
    dh3, dh3_b, sq, d_final_g = _ffn_down_loss(act, wd_g, h2, row(final_norm_g), tgt2, ts)
    loss_here = jnp.broadcast_to(0.5 * jnp.sum(sq) / D_MODEL, (1, 2, SUBLANES, LANES))

    def split(g, nm):
        r, c = big[nm].shape
        return g.reshape(N_CHIPS, 2, r // 2, c)

    def chip_sums(group, arrays, got):
        sums, parts = [None] * len(group), [None] * len(group)
        for blocks in (N_CHIPS, 1):
            idx = [i for i, a in enumerate(arrays) if a.shape[0] == blocks]
            if idx:
                out = _add_halves([arrays[i] for i in idx], [got[i] for i in idx], pos, "chip_sum_" + group[idx[0]],
                                  [F32 if group[i] == "loss" else BF16 for i in idx])
                for k, i in enumerate(idx):
                    sums[i], parts[i] = out[0][k], out[1][k]
        return sums, parts

    def start_swap(group, grads):
        return _swap_start([split(g, nm) for g, nm in zip(grads, group)], "swap_start_" + group[0])

    def start_exchange(group, swapping, after):
        sems, arrays, lands, _ = swapping
        arrays, got = _swap_wait(sems, arrays, lands, after, "swap_wait_" + group[0])
        sums, parts = chip_sums(group, arrays, got)
        return _exchange_start(sums, parts, "exchange_start_" + group[0])

    def finish_exchange(group, started, after):
        sems, sums, parts, _ = started
        parts = _exchange_wait(sems, sums, parts, after, "exchange_wait_" + group[0])
        return _sum_chips(parts, pos, "total_" + group[0])

    def join_and_update(group, after):
        joined = _join_halves([halves_of[nm] for nm in group], "join_halves_" + group[0], after)
        for nm, j in zip(group, joined):
            grads[nm] = j.reshape(big[nm].shape)
        outs = _adamw([(weights[nm], grads[nm], m_in[nm], v_in[nm]) for nm in group], "adamw_" + group[0])
        for nm, (d, nm_, nv_) in zip(group, outs):
            delta[nm], new_m[nm], new_v[nm] = d, nm_, nv_
        return [new_v[nm] for nm in group]

    as3 = lambda a: a.reshape((1,) + a.shape)
    halves_of = {}

    g_down = _grad_w(act, as3(dh3_b), FFN_HALF, D_MODEL, "grad_ffn_w_down")
    group_a = ["ffn_w_down"]
    swap_a = start_swap(group_a, [g_down])
    dgu, dh2, dh2_b, d_ffn_g = _ffn_bwd(dh3, wd_g.reshape(2, FFN_HALF, D_MODEL), gu, wgu_g, h2, row(norm_ffn_g), tb,
                                        after=swap_a[3])
    exch_a = start_exchange(group_a, swap_a, dh2)
    g_gu = _grad_w(hn3, dgu, D_MODEL, FFN_HALF, "grad_ffn_w_gate_up", after=exch_a[3])
    halves_of.update(zip(group_a, finish_exchange(group_a, exch_a, g_gu)))

    group_b = ["ffn_w_gate_up"]
    swap_b = start_swap(group_b, [g_gu])
    dh1, dh1_b, dq, dkv, d_xa_g = _attn_bwd(dh2, wo_g, q, kv, wq_g, h1, row(norm_xa_g), ts, after=swap_b[3])
    exch_b = start_exchange(group_b, swap_b, dh1)
    g_wkv, d_mem_g = _mem_kv_bwd(dkv, mn, wkv_g, mem2, row(mem_norm_g), after=exch_b[3])
    g_wo = _grad_w(o, as3(dh2_b), D_MODEL, D_MODEL, "grad_xa_wo", after=exch_b[3])
    g_wq = _grad_w(hn2, as3(dq), D_MODEL, D_MODEL, "grad_xa_wq", after=exch_b[3])
    g_wout = _grad_w(mix, as3(dh1_b), D_MODEL, D_MODEL, "grad_w_out", after=exch_b[3])
    halves_of.update(zip(group_b, finish_exchange(group_b, exch_b, (g_wkv, g_wo, g_wq, g_wout))))

    group_c = ["xa_wo", "xa_wq", "xa_wkv", "w_out"]
    swap_c = start_swap(group_c, [g_wo, g_wq, g_wkv, g_wout])
    (gx, dz, d_cw, d_cb, d_lng, d_lnb, d_gg, d_gb, d_ws, d_bs_sum, d_bin, d_mix_g) = _seqmix_bwd(
        dh1, x2, z, c1, w_out_g, w_in_g, row(norm_mix_g), cw_g, row(conv_ln_g), row(conv_ln_b),
        row(gm_ln_g), row(gm_ln_b), wpair, wpair_t, bias, tb, after=swap_c[3])
    d_bs = _head_bias_grad(d_bs_sum)[:, :GM_HEADS].T
    exch_c = start_exchange(group_c, swap_c, dz)
    g_win = _grad_w(hn1, as3(dz), D_MODEL, 512, "grad_w_in", after=exch_c[3])
    done_ab = join_and_update(group_a + group_b, g_win)
    halves_of.update(zip(group_c, finish_exchange(group_c, exch_c, (g_win, *done_ab))))

    small_names = ["norm_mix_g", "b_in", "conv_w", "conv_b", "conv_ln_g", "conv_ln_b", "gm_ln_g", "gm_ln_b",
                   "gm_w_s", "gm_b_s", "norm_xa_g", "mem_norm_g", "norm_ffn_g", "final_norm_g"]
    d_cw_by_chip = jnp.swapaxes(d_cw.reshape(CONV_HALO, N_CHIPS, LANES), 0, 1).reshape(-1, LANES)
    small_grads = dict(norm_mix_g=d_mix_g, b_in=d_bin, conv_w=d_cw_by_chip, conv_b=d_cb, conv_ln_g=d_lng,
                       conv_ln_b=d_lnb, gm_ln_g=d_gg, gm_ln_b=d_gb, gm_w_s=d_ws, gm_b_s=d_bs, norm_xa_g=d_xa_g,
                       mem_norm_g=d_mem_g, norm_ffn_g=d_ffn_g, final_norm_g=d_final_g)

    def rows_form(a):
        a = a.reshape(-1, LANES)
        return jnp.pad(a, ((0, -a.shape[0] % SUBLANES), (0, 0)))

    pieces = [rows_form(small_grads[nm]) for nm in small_names]
    offsets, total = [], 0
    for p in pieces:
        offsets.append(total)
        total += p.shape[0]
    pack_rows = -(-total // 32) * 32
    small_pack = jnp.pad(jnp.concatenate(pieces, axis=0), ((0, pack_rows - total), (0, 0)))

    group_d = ["w_in", "small", "loss"]
    arrays_d = [split(g_win, "w_in"), small_pack.reshape(1, 2, pack_rows // 2, LANES), loss_here]
    sums_d, parts_d = chip_sums(group_d, arrays_d, _swap_halves(arrays_d, "swap_halves_w_in"))
    exch_d = _exchange_start(sums_d, parts_d, "exchange_start_w_in")
    done_c = join_and_update(group_c, exch_d[3])
    halves_of.update(zip(group_d, finish_exchange(group_d, exch_d, done_c)))
    joined_d = _join_halves([halves_of[nm] for nm in group_d], "join_halves_w_in")
    grads["w_in"] = joined_d[0].reshape(w_in.shape)
    loss = joined_d[2][0, 0, 0]
    delta["w_in"], new_m["w_in"], new_v["w_in"] = _adamw([(w_in, grads["w_in"], m_w_in, v_w_in)], "adamw_w_in")[0]

    local_rows = lambda a, nm: a if nm == "conv_w" else a.reshape(-1, LANES)
    params = [tuple(local_rows(src[nm], nm) for src in (weights, m_in, v_in)) for nm in small_names]
    outs = _adamw_small(joined_d[1].reshape(pack_rows, LANES), pos, params, offsets, small_names.index("conv_w"))
    for k, nm in enumerate(small_names):
        for dst, a in zip((grads, delta, new_m, new_v), outs[4 * k:4 * k + 4]):
            dst[nm] = a

    order = ["norm_mix_g", "w_in", "b_in", "conv_w", "conv_b", "conv_ln_g", "conv_ln_b", "gm_ln_g", "gm_ln_b",
             "gm_w_s", "gm_b_s", "w_out", "norm_xa_g", "mem_norm_g", "xa_wq", "xa_wkv", "xa_wo", "norm_ffn_g",
             "ffn_w_gate_up", "ffn_w_down", "final_norm_g"]
    fit = lambda a, nm: a.reshape(weights[nm].shape)
    return (loss, gx.reshape(x.shape),
            *[fit(grads[nm], nm) for nm in order], *[fit(delta[nm], nm) for nm in order],
            *[fit(new_m[nm], nm) for nm in order], *[fit(new_v[nm], nm) for nm in order])
```

```python
import functools

import jax
import jax.numpy as jnp
from jax import lax
from jax.experimental import pallas as pl
from jax.experimental.pallas import tpu as pltpu

F32 = jnp.float32
BF16 = jnp.bfloat16

D_MODEL = 1024
CONV_WIDTH = 512
GM_WIDTH = 512
CONV_KERNEL = 31
CONV_HALO = 32
GRAD_ROWS = 2048
CHUNK = 128
GM_HEADS = 8
GM_HEAD_DIM = 64
XA_HEADS = 4
XA_HEAD_DIM = 256
FFN_HIDDEN = 2816
FFN_HALF = FFN_HIDDEN // 2
RMS_EPS = 1e-6
LN_EPS = 1e-5
N_CHIPS = 4
LANES = 128
SUBLANES = 8

ADAM_LR = 0.001
ADAM_B1 = 0.9
ADAM_B2 = 0.999
ADAM_EPS = 1e-08
ADAM_WD = 0.01
ADAM_STEP = 10

VMEM_LIMIT_BYTES = 56 * 1024 * 1024
MESH = pl.DeviceIdType.MESH
ANY = pl.BlockSpec(memory_space=pl.ANY)
HBM_SPEC = pl.BlockSpec(memory_space=pltpu.HBM)
SEM_SPEC = pl.BlockSpec(memory_space=pltpu.SEMAPHORE)

_NT = (((1,), (1,)), ((), ()))
_TN = (((0,), (0,)), ((), ()))
_GELU_C = 0.7978845608028654
_GELU_A = 0.044715


def _dot(a, b):
    return jnp.dot(a, b, preferred_element_type=F32)


def _dot_nt(a, b):
    return lax.dot_general(a, b, _NT, preferred_element_type=F32)


def _dot_tn(a, b):
    return lax.dot_general(a, b, _TN, preferred_element_type=F32)


def _mean(v):
    return jnp.mean(v, axis=-1, keepdims=True)


def _rowsum(v):
    return jnp.sum(v, axis=0, keepdims=True)


def _sigmoid(v):
    return 1.0 / (1.0 + jnp.exp(-v))


def _gelu_parts(v):
    v2 = v * v
    t = jnp.tanh(_GELU_C * (v + _GELU_A * v * v2))
    g = 0.5 * v * (1.0 + t)
    dg = 0.5 * (1.0 + t) + 0.5 * v * (1.0 - t * t) * (_GELU_C * (1.0 + 3.0 * _GELU_A * v2))
    return g, dg


def _rms_stats(v):
    return lax.rsqrt(_mean(v * v) + RMS_EPS)


def _rms_bwd(dy, v, r, g):
    n = v * r
    dn = dy * g
    dv = r * (dn - n * _mean(dn * n))
    return dv, _rowsum(dy * n)


def _ln_stats(v):
    mu = _mean(v)
    xc = v - mu
    rs = lax.rsqrt(_mean(xc * xc) + LN_EPS)
    return xc * rs, rs


def _ln_bwd(dy, xh, rs, g):
    dxh = dy * g
    dv = rs * (dxh - _mean(dxh) - xh * _mean(dxh * xh))
    return dv, _rowsum(dy * xh), _rowsum(dy)


def _params(sem):
    return pltpu.CompilerParams(dimension_semantics=sem, vmem_limit_bytes=VMEM_LIMIT_BYTES)


def _row_tile(s):
    return 512 if s % 512 == 0 and s >= 2048 else 128


def _mesh_pos():
    return lax.axis_index("x"), lax.axis_index("y"), lax.axis_index("c")


def _cast_into_slots(ws, pos, dtypes, name):
    n = len(ws)

    def body(pos_ref, *refs):
        for a in range(n):
            refs[n + a][0] = refs[a][...].astype(dtypes[a])

    return pl.pallas_call(
        body, name=name,
        grid_spec=pltpu.PrefetchScalarGridSpec(
            num_scalar_prefetch=1, grid=(2,),
            in_specs=[pl.BlockSpec((1,) + w.shape[1:], lambda i, p: (i, 0, 0)) for w in ws],
            out_specs=[pl.BlockSpec((1, 1) + w.shape[1:], lambda i, p: (p[0], i, 0, 0)) for w in ws]),
        out_shape=[jax.ShapeDtypeStruct((N_CHIPS,) + w.shape, dt) for w, dt in zip(ws, dtypes)],
        compiler_params=_params(("parallel",)),
    )(pos, *ws)


def _adam_update(w, g, m, v):
    nm = ADAM_B1 * m + (1.0 - ADAM_B1) * g
    nv = ADAM_B2 * v + (1.0 - ADAM_B2) * (g * g)
    m_hat = nm / (1.0 - ADAM_B1 ** ADAM_STEP)
    v_hat = nv / (1.0 - ADAM_B2 ** ADAM_STEP)
    return -ADAM_LR * (m_hat / (jnp.sqrt(v_hat) + ADAM_EPS) + ADAM_WD * w), nm, nv


ADAM_STEPS = 4


def _adamw(quads, name, after=()):
    n = len(quads)

    def body(*refs):
        ins, outs = refs[:4 * n], refs[4 * n:]
        for a in range(n):
            w, g, m, v = (r[...] for r in ins[4 * a:4 * a + 4])
            outs[4 * a][...] = g
            outs[4 * a + 1][...], outs[4 * a + 2][...], outs[4 * a + 3][...] = _adam_update(w, g, m, v)

    specs = [pl.BlockSpec((q[0].shape[0] // ADAM_STEPS, q[0].shape[1]), lambda i: (i, 0)) for q in quads]
    out = _tied_call(
        body, after, name=name, grid=(ADAM_STEPS,),
        in_specs=[sp for sp in specs for _ in range(4)], out_specs=[sp for sp in specs for _ in range(4)],
        out_shape=[jax.ShapeDtypeStruct(q[0].shape, F32) for q in quads for _ in range(4)],
        compiler_params=_params(("parallel",)),
    )(*[a for q in quads for a in q])
    return [tuple(out[4 * a:4 * a + 4]) for a in range(n)]


def _adamw_small(gpack, pos, params, offsets, conv_at):
    n = len(params)

    def body(pos_ref, g_ref, *refs):
        ins, outs = refs[:3 * n], refs[3 * n:]
        for k in range(n):
            rows = params[k][0].shape[0]
            start = offsets[k]
            if k == conv_at:
                start = pl.multiple_of(start + pos_ref[0] * CONV_HALO, SUBLANES)
            g = g_ref[pl.ds(start, rows), :]
            outs[4 * k][...] = g
            outs[4 * k + 1][...], outs[4 * k + 2][...], outs[4 * k + 3][...] = _adam_update(
                ins[3 * k][...], g, ins[3 * k + 1][...], ins[3 * k + 2][...])

    flat = [a for p in params for a in p]
    vmem = pl.BlockSpec(memory_space=pltpu.VMEM)
    return pl.pallas_call(
        body, name="adamw_small",
        in_specs=[pl.BlockSpec(memory_space=pltpu.SMEM), vmem] + [vmem] * len(flat),
        out_specs=[vmem] * (4 * n),
        out_shape=[jax.ShapeDtypeStruct(p[0].shape, F32) for p in params for _ in range(4)],
    )(pos, gpack, *flat)


def _as_tuple(after):
    return tuple(after) if isinstance(after, (tuple, list)) else (after,)


def _tied_call(body, after, *, in_specs, **kwargs):
    after = _as_tuple(after)
    n_in, n_after = len(in_specs), len(after)

    def tied(*refs):
        body(*refs[:n_in], *refs[n_in + n_after:])

    call = pl.pallas_call(tied, in_specs=list(in_specs) + [ANY] * n_after, **kwargs)
    return lambda *operands: call(*operands, *after)


def _other_chips(x, y):
    return [(1 - x, y), (x, 1 - y), (1 - x, 1 - y)]


def _gather_descriptors(bufs, send_of, recv_of):
    x, y, c = _mesh_pos()
    me = 2 * x + y
    chips = _other_chips(x, y)
    sends, arrivals = [], []
    for a in range(len(bufs)):
        for k in range(3):
            ck = 2 * chips[k][0] + chips[k][1]

            def copy(slot, a=a, k=k):
                return pltpu.make_async_remote_copy(
                    src_ref=bufs[a].at[slot, c], dst_ref=bufs[a].at[slot, c],
                    send_sem=send_of(a, k), recv_sem=recv_of(a, k),
                    device_id=(*chips[k], c), device_id_type=MESH)

            sends.append(functools.partial(copy, me))
            arrivals.append(functools.partial(copy, ck))
    return sends, arrivals


def _gather_start(bufs, name, after=()):
    n = len(bufs)
    ns = 3 * n

    def body(*refs):
        sems = refs[n:n + 2 * ns]
        thru = refs[n + 2 * ns:2 * n + 2 * ns]
        token = refs[2 * n + 2 * ns]
        sends, _ = _gather_descriptors(thru, lambda a, k: sems[3 * a + k], lambda a, k: sems[ns + 3 * a + k])
        for cp in sends:
            cp().start()
        token[...] = jnp.zeros_like(token)

    held = [pltpu.with_memory_space_constraint(b, pltpu.HBM) for b in bufs]
    out = _tied_call(
        body, after, name=name,
        out_shape=(*[pltpu.SemaphoreType.DMA(())] * (2 * ns), *[pltpu.HBM(b.shape, b.dtype) for b in held],
                   jax.ShapeDtypeStruct((8, LANES), F32)),
        in_specs=[HBM_SPEC] * n,
        out_specs=(*[SEM_SPEC] * (2 * ns), *[HBM_SPEC] * n, pl.BlockSpec(memory_space=pltpu.VMEM)),
        input_output_aliases={i: 2 * ns + i for i in range(n)},
        compiler_params=pltpu.CompilerParams(has_side_effects=pltpu.SideEffectType.DATAFLOW_SIDE_EFFECTING),
    )(*held)
    return list(out[:ns]), list(out[ns:2 * ns]), list(out[2 * ns:2 * ns + n]), out[2 * ns + n]


def _gather_wait(send_sems, recv_sems, bufs, after, name):
    n = len(bufs)
    ns = 3 * n

    def body(*refs):
        buf_ref = refs[:n]
        sem_ref = refs[n:n + 2 * ns]
        sends, arrivals = _gather_descriptors(buf_ref, lambda a, k: sem_ref[3 * a + k],
                                              lambda a, k: sem_ref[ns + 3 * a + k])
        for cp in sends:
            cp().wait_send()
        for cp in arrivals:
            cp().wait_recv()

    out = pl.pallas_call(
        body, name=name,
        out_shape=tuple(pltpu.HBM(b.shape, b.dtype) for b in bufs),
        in_specs=[HBM_SPEC] * n + [SEM_SPEC] * (2 * ns) + [ANY] * len(_as_tuple(after)),
        out_specs=tuple([HBM_SPEC] * n),
        input_output_aliases={i: i for i in range(n)},
        compiler_params=pltpu.CompilerParams(has_side_effects=pltpu.SideEffectType.DATAFLOW_SIDE_EFFECTING),
    )(*bufs, *send_sems, *recv_sems, *_as_tuple(after))
    return list(out)


SIBLING_COLLECTIVE_ID = 0


def _sibling_handshake():
    x, y, c = _mesh_pos()
    barrier = pltpu.get_barrier_semaphore()
    pl.semaphore_signal(barrier, inc=1, device_id=(x, y, 1 - c), device_id_type=MESH)
    pl.semaphore_wait(barrier, 1)


def _pass_to_sibling(bufs, name):
    n = len(bufs)

    def body(*refs):
        outs = refs[n:2 * n]
        send_sem, recv_sem = refs[2 * n:]
        x, y, c = _mesh_pos()
        chips = _other_chips(x, y)
        _sibling_handshake()

        def half(a, k, which):
            ck = 2 * chips[k][0] + chips[k][1]
            return pltpu.make_async_remote_copy(
                src_ref=outs[a].at[ck, which], dst_ref=outs[a].at[ck, which],
                send_sem=send_sem.at[a, k], recv_sem=recv_sem.at[a, k],
                device_id=(x, y, 1 - c), device_id_type=MESH)

        sends = [half(a, k, c) for a in range(n) for k in range(3)]
        for cp in sends:
            cp.start()
        for a in range(n):
            for k in range(3):
                half(a, k, 1 - c).wait_recv()
        for cp in sends:
            cp.wait_send()

    return pl.pallas_call(
        body, name=name,
        in_specs=[ANY] * n, out_specs=[ANY] * n,
        out_shape=[jax.ShapeDtypeStruct(b.shape, b.dtype) for b in bufs],
        input_output_aliases={a: a for a in range(n)},
        scratch_shapes=[pltpu.SemaphoreType.DMA((n, 3))] * 2,
        compiler_params=pltpu.CompilerParams(collective_id=SIBLING_COLLECTIVE_ID),
    )(*bufs)


def _swap_halves(grads, name):
    n = len(grads)

    def body(*refs):
        ins, outs = refs[:n], refs[n:2 * n]
        send_sem, recv_sem = refs[2 * n:]
        x, y, c = _mesh_pos()
        _sibling_handshake()
        cps = [pltpu.make_async_remote_copy(
            src_ref=ins[a].at[:, pl.ds(1 - c, 1)], dst_ref=outs[a],
            send_sem=send_sem.at[a], recv_sem=recv_sem.at[a],
            device_id=(x, y, 1 - c), device_id_type=MESH) for a in range(n)]
        for cp in cps:
            cp.start()
        for cp in cps:
            cp.wait()

    out_shape = [jax.ShapeDtypeStruct((g.shape[0], 1) + g.shape[2:], g.dtype) for g in grads]
    return pl.pallas_call(
        body, name=name,
        in_specs=[ANY] * n, out_specs=[ANY] * n, out_shape=out_shape,
        scratch_shapes=[pltpu.SemaphoreType.DMA((n,))] * 2,
        compiler_params=pltpu.CompilerParams(collective_id=SIBLING_COLLECTIVE_ID),
    )(*grads)


def _swap_descriptors(grads, lands, send_of, recv_of):
    x, y, c = _mesh_pos()
    return [functools.partial(
        pltpu.make_async_remote_copy,
        src_ref=grads[a].at[:, pl.ds(1 - c, 1)], dst_ref=lands[a],
        send_sem=send_of(a), recv_sem=recv_of(a),
        device_id=(x, y, 1 - c), device_id_type=MESH) for a in range(len(grads))]


def _swap_start(grads, name):
    n = len(grads)

    def body(*refs):
        sems = refs[2 * n:4 * n]
        g_thru, l_thru = refs[4 * n:5 * n], refs[5 * n:6 * n]
        token = refs[6 * n]
        for cp in _swap_descriptors(g_thru, l_thru, lambda a: sems[a], lambda a: sems[n + a]):
            cp().start()
        token[...] = jnp.zeros_like(token)

    lands = [lax.empty((g.shape[0], 1) + g.shape[2:], g.dtype) for g in grads]
    held = [pltpu.with_memory_space_constraint(a, pltpu.HBM) for a in (*grads, *lands)]
    out = pl.pallas_call(
        body, name=name,
        out_shape=(*[pltpu.SemaphoreType.DMA(())] * (2 * n), *[pltpu.HBM(a.shape, a.dtype) for a in held],
                   jax.ShapeDtypeStruct((8, LANES), F32)),
        in_specs=[HBM_SPEC] * (2 * n),
        out_specs=(*[SEM_SPEC] * (2 * n), *[HBM_SPEC] * (2 * n), pl.BlockSpec(memory_space=pltpu.VMEM)),
        input_output_aliases={i: 2 * n + i for i in range(2 * n)},
        compiler_params=pltpu.CompilerParams(has_side_effects=pltpu.SideEffectType.DATAFLOW_SIDE_EFFECTING),
    )(*held)
    return list(out[:2 * n]), list(out[2 * n:3 * n]), list(out[3 * n:4 * n]), out[4 * n]


def _swap_wait(sems, grads, lands, after, name):
    n = len(grads)

    def body(*refs):
        g_ref, l_ref = refs[:n], refs[n:2 * n]
        sem_ref = refs[2 * n:4 * n]
        for cp in _swap_descriptors(g_ref, l_ref, lambda a: sem_ref[a], lambda a: sem_ref[n + a]):
            cp().wait()

    out = pl.pallas_call(
        body, name=name,
        out_shape=tuple(pltpu.HBM(a.shape, a.dtype) for a in (*grads, *lands)),
        in_specs=[HBM_SPEC] * (2 * n) + [SEM_SPEC] * (2 * n) + [ANY] * len(_as_tuple(after)),
        out_specs=tuple([HBM_SPEC] * (2 * n)),
        input_output_aliases={i: i for i in range(2 * n)},
        compiler_params=pltpu.CompilerParams(has_side_effects=pltpu.SideEffectType.DATAFLOW_SIDE_EFFECTING),
    )(*grads, *lands, *sems, *_as_tuple(after))
    return list(out[:n]), list(out[n:])


def _add_halves(gs, gots, pos, name, dtypes):
    n = len(gs)
    j = gs[0].shape[0]

    def body(pos_ref, *refs):
        g_refs, r_refs = refs[:n], refs[n:2 * n]
        o_refs, p_refs = refs[2 * n:3 * n], refs[3 * n:]
        vals = [(g_refs[a][0, 0] + r_refs[a][0, 0]).astype(dtypes[a]) for a in range(n)]
        for a in range(n):
            o_refs[a][0] = vals[a]
        if j == 1:
            for a in range(n):
                p_refs[a][0] = vals[a]
        else:
            @pl.when(pl.program_id(0) == pos_ref[0])
            def _():
                for a in range(n):
                    p_refs[a][0] = vals[a]

    blk = lambda g: (1,) + g.shape[2:]
    out = pl.pallas_call(
        body, name=name,
        grid_spec=pltpu.PrefetchScalarGridSpec(
            num_scalar_prefetch=1, grid=(j,),
            in_specs=[pl.BlockSpec((1,) + blk(g), lambda i, p: (i, p[1], 0, 0)) for g in gs]
            + [pl.BlockSpec((1,) + blk(g), lambda i, p: (i, 0, 0, 0)) for g in gs],
            out_specs=[pl.BlockSpec(blk(g), lambda i, p: (i, 0, 0)) for g in gs]
            + [pl.BlockSpec(blk(g), lambda i, p: (p[0], 0, 0)) for g in gs]),
        out_shape=[jax.ShapeDtypeStruct((j,) + g.shape[2:], dt) for g, dt in zip(gs, dtypes)]
        + [jax.ShapeDtypeStruct((N_CHIPS,) + g.shape[2:], dt) for g, dt in zip(gs, dtypes)],
        compiler_params=_params(("arbitrary",)),
    )(pos, *gs, *gots)
    return list(out[:n]), list(out[n:])


def _exchange_descriptors(sums, parts, send_of, recv_of):
    x, y, c = _mesh_pos()
    me = 2 * x + y
    chips = _other_chips(x, y)
    sends, arrivals = [], []
    for a in range(len(sums)):
        for k in range(3):
            ck = 2 * chips[k][0] + chips[k][1]
            mine = sums[a].at[ck] if sums[a].shape[0] == N_CHIPS else sums[a].at[0]

            def copy(dst_slot, a=a, k=k, mine=mine):
                return pltpu.make_async_remote_copy(
                    src_ref=mine, dst_ref=parts[a].at[dst_slot],
                    send_sem=send_of(a, k), recv_sem=recv_of(a, k),
                    device_id=(*chips[k], c), device_id_type=MESH)

            sends.append(functools.partial(copy, me))
            arrivals.append(functools.partial(copy, ck))
    return sends, arrivals


def _exchange_start(sums, parts, name):
    n = len(sums)
    ns = 3 * n

    def body(*refs):
        sems = refs[2 * n:2 * n + 2 * ns]
        sums_thru = refs[2 * n + 2 * ns:3 * n + 2 * ns]
        parts_thru = refs[3 * n + 2 * ns:4 * n + 2 * ns]
        token = refs[4 * n + 2 * ns]
        sends, _ = _exchange_descriptors(sums_thru, parts_thru, lambda a, k: sems[3 * a + k],
                                         lambda a, k: sems[ns + 3 * a + k])
        for cp in sends:
            cp().start()
        token[...] = jnp.zeros_like(token)

    hbm = lambda a: pltpu.HBM(a.shape, a.dtype)
    held = [pltpu.with_memory_space_constraint(a, pltpu.HBM) for a in (*sums, *parts)]
    out = pl.pallas_call(
        body, name=name,
        out_shape=(*[pltpu.SemaphoreType.DMA(())] * (2 * ns), *[hbm(a) for a in held],
                   jax.ShapeDtypeStruct((8, LANES), F32)),
        in_specs=[HBM_SPEC] * (2 * n),
        out_specs=(*[SEM_SPEC] * (2 * ns), *[HBM_SPEC] * (2 * n), pl.BlockSpec(memory_space=pltpu.VMEM)),
        input_output_aliases={i: 2 * ns + i for i in range(2 * n)},
        compiler_params=pltpu.CompilerParams(has_side_effects=pltpu.SideEffectType.DATAFLOW_SIDE_EFFECTING),
    )(*held)
    return (list(out[:2 * ns]), list(out[2 * ns:2 * ns + n]), list(out[2 * ns + n:2 * ns + 2 * n]),
            out[2 * ns + 2 * n])


def _exchange_wait(sems, sums, parts, after, name):
    n = len(sums)
    ns = 3 * n

    def body(*refs):
        sums_ref, parts_ref = refs[:n], refs[n:2 * n]
        sem_ref = refs[2 * n:2 * n + 2 * ns]
        sends, arrivals = _exchange_descriptors(sums_ref, parts_ref, lambda a, k: sem_ref[3 * a + k],
                                                lambda a, k: sem_ref[ns + 3 * a + k])
        for cp in sends:
            cp().wait_send()
        for cp in arrivals:
            cp().wait_recv()

    hbm = lambda a: pltpu.HBM(a.shape, a.dtype)
    out = pl.pallas_call(
        body, name=name,
        out_shape=tuple(hbm(a) for a in (*sums, *parts)),
        in_specs=[HBM_SPEC] * (2 * n) + [SEM_SPEC] * (2 * ns) + [ANY] * len(_as_tuple(after)),
        out_specs=tuple([HBM_SPEC] * (2 * n)),
        input_output_aliases={i: i for i in range(2 * n)},
        compiler_params=pltpu.CompilerParams(has_side_effects=pltpu.SideEffectType.DATAFLOW_SIDE_EFFECTING),
    )(*sums, *parts, *sems, *_as_tuple(after))
    return list(out[n:])


def _sum_chips(parts, pos, name):
    n = len(parts)

    def body(pos_ref, *refs):
        for a in range(n):
            p_ref = refs[a]
            refs[n + a][0] = (((p_ref[0].astype(F32) + p_ref[1].astype(F32)) + p_ref[2].astype(F32))
                              + p_ref[3].astype(F32))

    out = pl.pallas_call(
        body, name=name,
        grid_spec=pltpu.PrefetchScalarGridSpec(
            num_scalar_prefetch=1, grid=(1,),
            in_specs=[pl.BlockSpec(p.shape, lambda i, q: (0, 0, 0)) for p in parts],
            out_specs=[pl.BlockSpec((1,) + p.shape[1:], lambda i, q: (q[1], 0, 0)) for p in parts]),
        out_shape=[jax.ShapeDtypeStruct((2,) + p.shape[1:], F32) for p in parts],
        compiler_params=_params(("arbitrary",)),
    )(pos, *parts)
    return list(out)


def _join_halves(fulls, name, after=()):
    n = len(fulls)

    def body(*refs):
        outs = refs[n:2 * n]
        send_sem, recv_sem = refs[2 * n:]
        x, y, c = _mesh_pos()
        _sibling_handshake()

        def half(a, which):
            return pltpu.make_async_remote_copy(
                src_ref=outs[a].at[which], dst_ref=outs[a].at[which],
                send_sem=send_sem.at[a], recv_sem=recv_sem.at[a],
                device_id=(x, y, 1 - c), device_id_type=MESH)

        sends = [half(a, c) for a in range(n)]
        for cp in sends:
            cp.start()
        for a in range(n):
            half(a, 1 - c).wait_recv()
        for cp in sends:
            cp.wait_send()

    out_shape = [jax.ShapeDtypeStruct(f.shape, f.dtype) for f in fulls]
    return _tied_call(
        body, after, name=name,
        in_specs=[ANY] * n, out_specs=[ANY] * n, out_shape=out_shape,
        input_output_aliases={a: a for a in range(n)},
        scratch_shapes=[pltpu.SemaphoreType.DMA((n,))] * 2,
        compiler_params=pltpu.CompilerParams(collective_id=SIBLING_COLLECTIVE_ID),
    )(*fulls)


def _norm_in(x, g, ts, after=()):
    s = x.shape[0]

    def body(x_ref, g_ref, hn_ref):
        xv = x_ref[...]
        hn_ref[...] = (xv * _rms_stats(xv) * g_ref[...]).astype(BF16)

    row = pl.BlockSpec((ts, D_MODEL), lambda i: (i, 0))
    return _tied_call(
        body, after, name="norm_in", grid=(s // ts,),
        in_specs=[row, pl.BlockSpec((1, D_MODEL), lambda i: (0, 0))], out_specs=row,
        out_shape=jax.ShapeDtypeStruct((s, D_MODEL), BF16),
        compiler_params=_params(("parallel",)),
    )(x, g)


def _mix_in(hn, w_in, b_in, ts, after=()):
    s = hn.shape[0]

    def body(hn_ref, w_ref, b_ref, z_ref):
        hv = hn_ref[...]
        for j in range(4):
            cols = slice(j * 512, (j + 1) * 512)
            z_ref[:, cols] = _dot(hv, w_ref[j]) + b_ref[:, cols]

    return _tied_call(
        body, after, name="mix_in", grid=(s // ts,),
        in_specs=[pl.BlockSpec((ts, D_MODEL), lambda i: (i, 0)),
                  pl.BlockSpec((4, D_MODEL, 512), lambda i: (0, 0, 0)),
                  pl.BlockSpec((1, 2048), lambda i: (0, 0))],
        out_specs=pl.BlockSpec((ts, 2048), lambda i: (i, 0)),
        out_shape=jax.ShapeDtypeStruct((s, 2048), F32),
        compiler_params=_params(("parallel",)),
    )(hn, w_in, b_in)


def _shift_rows(buf, shifted, t):
    rows = t + CONV_HALO - SUBLANES
    for r in range(1, SUBLANES):
        shifted[r - 1, 0:rows, :] = buf[pl.ds(r, rows), :]


def _window(buf, shifted, offset, t):
    r = offset % SUBLANES
    if r == 0:
        return buf[pl.ds(offset, t), :]
    return shifted[r - 1, pl.ds(offset - r, t), :]


def _lane_is_low_head():
    lane = lax.broadcasted_iota(jnp.int32, (1, GM_WIDTH), 1)
    return (lane & GM_HEAD_DIM) == 0


def _gm_mix(v_lo, v_hi, wpair_ref, bias_ref, mixed_ref, t):
    for n in range(t // CHUNK):
        rows = slice(n * CHUNK, (n + 1) * CHUNK)
        for j in range(GM_HEADS // 2):
            cols = slice(j * LANES, (j + 1) * LANES)
            rhs = jnp.concatenate([v_lo[rows, cols], v_hi[rows, cols]], axis=0)
            mixed_ref[rows, cols] = _dot(wpair_ref[j], rhs) + bias_ref[:, cols]


def _seqmix_fwd(z, cw, cb, lng, lnb, gg, gb, wpair, bias, t):
    s = z.shape[0]

    def body(z_ref, cw_ref, cb_ref, lng_ref, lnb_ref, gg_ref, gb_ref, wpair_ref, bias_ref,
             mix_ref, c1_ref, abuf, ash, mixed_ref):
        i = pl.program_id(0)

        @pl.when(i == 0)
        def _():
            abuf[0:CONV_HALO, :] = jnp.zeros((CONV_HALO, CONV_WIDTH), F32)

        @pl.when(i > 0)
        def _():
            abuf[0:CONV_HALO, :] = abuf[t:t + CONV_HALO, :]

        abuf[CONV_HALO:, :] = z_ref[:, 0:512] * _sigmoid(z_ref[:, 512:1024])
        _shift_rows(abuf, ash, t)
        acc = jnp.zeros((t, CONV_WIDTH), F32)
        for k in range(CONV_KERNEL):
            acc = acc + cw_ref[k:k + 1, :] * _window(abuf, ash, CONV_HALO - (CONV_KERNEL - 1) + k, t)
        c1 = acc + cb_ref[...]
        c1_ref[...] = c1
        xh, _ = _ln_stats(c1)
        ln = xh * lng_ref[...] + lnb_ref[...]
        mix_ref[:, 0:512] = (ln * _sigmoid(ln)).astype(BF16)

        u, _ = _gelu_parts(z_ref[:, 1024:1536])
        gv, _ = _gelu_parts(z_ref[:, 1536:2048])
        vxh, _ = _ln_stats(gv)
        v = vxh * gg_ref[...] + gb_ref[...]
        low = _lane_is_low_head()
        v_lo = jnp.where(low, v, 0.0).astype(BF16)
        v_hi = jnp.where(low, 0.0, v).astype(BF16)
        _gm_mix(v_lo, v_hi, wpair_ref, bias_ref, mixed_ref, t)
        mix_ref[:, 512:1024] = (u * mixed_ref[...]).astype(BF16)

    vec = lambda n: pl.BlockSpec((1, n), lambda i: (0, 0))
    return pl.pallas_call(
        body, name="seqmix_fwd", grid=(s // t,),
        in_specs=[pl.BlockSpec((t, 2048), lambda i: (i, 0)),
                  pl.BlockSpec((CONV_HALO, CONV_WIDTH), lambda i: (0, 0)),
                  vec(512), vec(512), vec(512), vec(512), vec(512),
                  pl.BlockSpec((4, CHUNK, 2 * CHUNK), lambda i: (0, 0, 0)),
                  pl.BlockSpec((CHUNK, GM_WIDTH), lambda i: (0, 0))],
        out_specs=[pl.BlockSpec((t, D_MODEL), lambda i: (i, 0)),
                   pl.BlockSpec((t, CONV_WIDTH), lambda i: (i, 0))],
        out_shape=[jax.ShapeDtypeStruct((s, D_MODEL), BF16), jax.ShapeDtypeStruct((s, CONV_WIDTH), F32)],
        scratch_shapes=[pltpu.VMEM((t + CONV_HALO, CONV_WIDTH), F32),
                        pltpu.VMEM((SUBLANES - 1, t + CONV_HALO - SUBLANES, CONV_WIDTH), F32),
                        pltpu.VMEM((t, GM_WIDTH), F32)],
        compiler_params=_params(("arbitrary",)),
    )(z, cw, cb, lng, lnb, gg, gb, wpair, bias)


def _out_proj_q(x, mix, w_out, g, wq, ts, after=()):
    s = x.shape[0]

    def body(x_ref, mix_ref, wo_ref, g_ref, wq_ref, h1_ref, hn_ref, q_ref):
        h1 = x_ref[...] + _dot(mix_ref[...], wo_ref[...])
        h1_ref[...] = h1
        hn = (h1 * _rms_stats(h1) * g_ref[...]).astype(BF16)
        hn_ref[...] = hn
        q_ref[...] = _dot(hn, wq_ref[...]).astype(BF16)

    row = lambda dt: pl.BlockSpec((ts, D_MODEL), lambda i: (i, 0))
    full = pl.BlockSpec((D_MODEL, D_MODEL), lambda i: (0, 0))
    return _tied_call(
        body, after, name="out_proj_q", grid=(s // ts,),
        in_specs=[row(F32), row(BF16), full, pl.BlockSpec((1, D_MODEL), lambda i: (0, 0)), full],
        out_specs=[row(F32), row(BF16), row(BF16)],
        out_shape=[jax.ShapeDtypeStruct((s, D_MODEL), F32), jax.ShapeDtypeStruct((s, D_MODEL), BF16),
                   jax.ShapeDtypeStruct((s, D_MODEL), BF16)],
        compiler_params=_params(("parallel",)),
    )(x, mix, w_out, g, wq)


def _mem_kv(mem, g, wkv):
    m = mem.shape[0]

    def body(mem_ref, g_ref, w_ref, mn_ref, kv_ref):
        mv = mem_ref[...]
        mn = (mv * _rms_stats(mv) * g_ref[...]).astype(BF16)
        mn_ref[...] = mn
        for j in range(4):
            kv_ref[:, j * 512:(j + 1) * 512] = _dot(mn, w_ref[j]).astype(BF16)

    return pl.pallas_call(
        body, name="mem_kv",
        out_shape=[jax.ShapeDtypeStruct((m, D_MODEL), BF16), jax.ShapeDtypeStruct((m, 2 * D_MODEL), BF16)],
        compiler_params=pltpu.CompilerParams(vmem_limit_bytes=VMEM_LIMIT_BYTES),
    )(mem, g, wkv)


def _softmax_rows(sc):
    e = jnp.exp(sc - jnp.max(sc, axis=-1, keepdims=True))
    return e / jnp.sum(e, axis=-1, keepdims=True)


def _attn_fwd(q, kv, h1, wo, g, ts):
    s, m = q.shape[0], kv.shape[0]
    scale = XA_HEAD_DIM ** -0.5

    def body(q_ref, kv_ref, h1_ref, wo_ref, g_ref, o_ref, h2_ref, hn_ref):
        for h in range(XA_HEADS):
            cols = slice(h * XA_HEAD_DIM, (h + 1) * XA_HEAD_DIM)
            vcols = slice(D_MODEL + h * XA_HEAD_DIM, D_MODEL + (h + 1) * XA_HEAD_DIM)
            p = _softmax_rows(_dot_nt(q_ref[:, cols], kv_ref[:, cols]) * scale)
            o_ref[:, cols] = _dot(p.astype(BF16), kv_ref[:, vcols]).astype(BF16)
        h2 = h1_ref[...] + _dot(o_ref[...], wo_ref[...])
        h2_ref[...] = h2
        hn_ref[...] = (h2 * _rms_stats(h2) * g_ref[...]).astype(BF16)

    row = pl.BlockSpec((ts, D_MODEL), lambda i: (i, 0))
    return pl.pallas_call(
        body, name="attn_fwd", grid=(s // ts,),
        in_specs=[row, pl.BlockSpec((m, 2 * D_MODEL), lambda i: (0, 0)), row,
                  pl.BlockSpec((D_MODEL, D_MODEL), lambda i: (0, 0)),
                  pl.BlockSpec((1, D_MODEL), lambda i: (0, 0))],
        out_specs=[row, row, row],
        out_shape=[jax.ShapeDtypeStruct((s, D_MODEL), BF16), jax.ShapeDtypeStruct((s, D_MODEL), F32),
                   jax.ShapeDtypeStruct((s, D_MODEL), BF16)],
        compiler_params=_params(("parallel",)),
    )(q, kv, h1, wo, g)


_FFN_CHUNKS_FWD = (slice(0, 6 * LANES), slice(6 * LANES, FFN_HALF))
_FFN_CHUNKS_BWD = (slice(0, 4 * LANES), slice(4 * LANES, 8 * LANES), slice(8 * LANES, FFN_HALF))


def _ffn_up(hn, wgu, ts, after=()):
    s = hn.shape[0]

    def body(hn_ref, w_ref, gu_ref, act_ref):
        hv = hn_ref[...]
        for cols in _FFN_CHUNKS_FWD:
            gate = _dot(hv, w_ref[0, 0, :, cols])
            up = _dot(hv, w_ref[1, 0, :, cols])
            gu_ref[0, :, cols] = gate.astype(BF16)
            gu_ref[1, :, cols] = up.astype(BF16)
            act_ref[:, cols] = (gate * _sigmoid(gate) * up).astype(BF16)

    return _tied_call(
        body, after, name="ffn_up", grid=(2, s // ts),
        in_specs=[pl.BlockSpec((ts, D_MODEL), lambda j, i: (i, 0)),
                  pl.BlockSpec((2, 1, D_MODEL, FFN_HALF), lambda j, i: (0, j, 0, 0))],
        out_specs=[pl.BlockSpec((2, ts, FFN_HALF), lambda j, i: (0, i, j)),
                   pl.BlockSpec((ts, FFN_HALF), lambda j, i: (i, j))],
        out_shape=[jax.ShapeDtypeStruct((2, s, FFN_HIDDEN), BF16), jax.ShapeDtypeStruct((s, FFN_HIDDEN), BF16)],
        compiler_params=_params(("parallel", "parallel")),
    )(hn, wgu)


def _ffn_down_loss(act, wd, h2, g, target, ts):
    s = act.shape[0]

    def body(act_ref, wd_ref, h2_ref, g_ref, t_ref, dh_ref, dhb_ref, sq_ref, dg_ref):
        @pl.when(pl.program_id(0) == 0)
        def _():
            sq_ref[...] = jnp.zeros_like(sq_ref)
            dg_ref[...] = jnp.zeros_like(dg_ref)

        h3 = h2_ref[...] + _dot(act_ref[...], wd_ref[...])
        r = _rms_stats(h3)
        gv = g_ref[...]
        diff = h3 * r * gv - t_ref[...]
        sq_ref[...] += _rowsum(diff * diff)
        dh, dg = _rms_bwd(diff / D_MODEL, h3, r, gv)
        dh_ref[...] = dh
        dhb_ref[...] = dh.astype(BF16)
        dg_ref[...] += dg

    row = pl.BlockSpec((ts, D_MODEL), lambda i: (i, 0))
    vec = pl.BlockSpec((1, D_MODEL), lambda i: (0, 0))
    return pl.pallas_call(
        body, name="ffn_down_loss", grid=(s // ts,),
        in_specs=[pl.BlockSpec((ts, FFN_HIDDEN), lambda i: (i, 0)),
                  pl.BlockSpec((FFN_HIDDEN, D_MODEL), lambda i: (0, 0)), row, vec, row],
        out_specs=[row, row, vec, vec],
        out_shape=[jax.ShapeDtypeStruct((s, D_MODEL), F32), jax.ShapeDtypeStruct((s, D_MODEL), BF16),
                   jax.ShapeDtypeStruct((1, D_MODEL), F32), jax.ShapeDtypeStruct((1, D_MODEL), F32)],
        compiler_params=_params(("arbitrary",)),
    )(act, wd, h2, g, target)


def _grad_w(a, b, tk, tn, name, after=()):
    s, k = a.shape
    gb, _, n = b.shape
    nblk = n // tn
    tsr = GRAD_ROWS if s % GRAD_ROWS == 0 else s

    def body(a_ref, b_ref, o_ref):
        part = _dot_tn(a_ref[...], b_ref[0])

        @pl.when(pl.program_id(2) == 0)
        def _():
            o_ref[0] = part

        @pl.when(pl.program_id(2) > 0)
        def _():
            o_ref[0] += part

    return _tied_call(
        body, after, name=name, grid=(gb * nblk, k // tk, s // tsr),
        in_specs=[pl.BlockSpec((tsr, tk), lambda ni, ki, si: (si, ki)),
                  pl.BlockSpec((1, tsr, tn), lambda ni, ki, si: (ni // nblk, si, ni % nblk))],
        out_specs=pl.BlockSpec((1, tk, tn), lambda ni, ki, si: (ni, ki, 0)),
        out_shape=jax.ShapeDtypeStruct((gb * nblk, k, tn), F32),
        compiler_params=_params(("parallel", "parallel", "arbitrary")),
    )(a, b)


def _ffn_bwd(dh3, wd, gu, wgu, h2, g, t, after=()):
    s = dh3.shape[0]

    def body(dh3_ref, wd_ref, gu_ref, w_ref, h2_ref, g_ref, dgu_ref, dh2_ref, dh2b_ref, dg_ref):
        @pl.when(pl.program_id(0) == 0)
        def _():
            dg_ref[...] = jnp.zeros_like(dg_ref)

        dh3v = dh3_ref[...]
        dhb = dh3v.astype(BF16)
        dhn = jnp.zeros((t, D_MODEL), F32)
        for j in range(2):
            for cols in _FFN_CHUNKS_BWD:
                whole = slice(j * FFN_HALF + cols.start, j * FFN_HALF + cols.stop)
                dact = _dot_nt(dhb, wd_ref[j, cols, :])
                gate, up = gu_ref[0, :, whole].astype(F32), gu_ref[1, :, whole].astype(F32)
                sg = _sigmoid(gate)
                dgate = (dact * up * (sg * (1.0 + gate * (1.0 - sg)))).astype(BF16)
                dup = (dact * (gate * sg)).astype(BF16)
                dgu_ref[0, :, whole] = dgate
                dgu_ref[1, :, whole] = dup
                dhn = dhn + _dot_nt(dgate, w_ref[j, :, cols]) + _dot_nt(dup, w_ref[2 + j, :, cols])
        h2 = h2_ref[...]
        dv, dg = _rms_bwd(dhn, h2, _rms_stats(h2), g_ref[...])
        dh2 = dh3v + dv
        dh2_ref[...] = dh2
        dh2b_ref[...] = dh2.astype(BF16)
        dg_ref[...] += dg

    row = pl.BlockSpec((t, D_MODEL), lambda i: (i, 0))
    wide = pl.BlockSpec((2, t, FFN_HIDDEN), lambda i: (0, i, 0))
    vec = pl.BlockSpec((1, D_MODEL), lambda i: (0, 0))
    return _tied_call(
        body, after, name="ffn_bwd", grid=(s // t,),
        in_specs=[row, pl.BlockSpec((2, FFN_HALF, D_MODEL), lambda i: (0, 0, 0)), wide,
                  pl.BlockSpec((4, D_MODEL, FFN_HALF), lambda i: (0, 0, 0)), row, vec],
        out_specs=[wide, row, row, vec],
        out_shape=[jax.ShapeDtypeStruct((2, s, FFN_HIDDEN), BF16), jax.ShapeDtypeStruct((s, D_MODEL), F32),
                   jax.ShapeDtypeStruct((s, D_MODEL), BF16), jax.ShapeDtypeStruct((1, D_MODEL), F32)],
        compiler_params=_params(("arbitrary",)),
    )(dh3, wd, gu, wgu, h2, g)


def _attn_bwd(dh2, wo, q, kv, wq, h1, g, ts, after=()):
    s, m = q.shape[0], kv.shape[0]
    scale = XA_HEAD_DIM ** -0.5

    def body(dh2_ref, wo_ref, q_ref, kv_ref, wq_ref, h1_ref, g_ref, dh1_ref, dh1b_ref, dq_ref, dkv_ref, dg_ref):
        @pl.when(pl.program_id(0) == 0)
        def _():
            dkv_ref[...] = jnp.zeros_like(dkv_ref)
            dg_ref[...] = jnp.zeros_like(dg_ref)

        do = _dot_nt(dh2_ref[...].astype(BF16), wo_ref[...]).astype(BF16)
        for h in range(XA_HEADS):
            cols = slice(h * XA_HEAD_DIM, (h + 1) * XA_HEAD_DIM)
            vcols = slice(D_MODEL + h * XA_HEAD_DIM, D_MODEL + (h + 1) * XA_HEAD_DIM)
            qh, kh, vh, doh = q_ref[:, cols], kv_ref[:, cols], kv_ref[:, vcols], do[:, cols]
            p = _softmax_rows(_dot_nt(qh, kh) * scale)
            dp = _dot_nt(doh, vh)
            ds = (p * (dp - jnp.sum(dp * p, axis=-1, keepdims=True)) * scale).astype(BF16)
            dq_ref[:, cols] = _dot(ds, kh).astype(BF16)
            dkv_ref[:, cols] += _dot_tn(ds, qh)
            dkv_ref[:, vcols] += _dot_tn(p.astype(BF16), doh)
        dhn = _dot_nt(dq_ref[...], wq_ref[...])
        h1 = h1_ref[...]
        dv, dg = _rms_bwd(dhn, h1, _rms_stats(h1), g_ref[...])
        dh1 = dh2_ref[...] + dv
        dh1_ref[...] = dh1
        dh1b_ref[...] = dh1.astype(BF16)
        dg_ref[...] += dg

    row = pl.BlockSpec((ts, D_MODEL), lambda i: (i, 0))
    full = pl.BlockSpec((D_MODEL, D_MODEL), lambda i: (0, 0))
    kvs = pl.BlockSpec((m, 2 * D_MODEL), lambda i: (0, 0))
    vec = pl.BlockSpec((1, D_MODEL), lambda i: (0, 0))
    return _tied_call(
        body, after, name="attn_bwd", grid=(s // ts,),
        in_specs=[row, full, row, kvs, full, row, vec],
        out_specs=[row, row, row, kvs, vec],
        out_shape=[jax.ShapeDtypeStruct((s, D_MODEL), F32), jax.ShapeDtypeStruct((s, D_MODEL), BF16),
                   jax.ShapeDtypeStruct((s, D_MODEL), BF16),
                   jax.ShapeDtypeStruct((m, 2 * D_MODEL), F32), jax.ShapeDtypeStruct((1, D_MODEL), F32)],
        compiler_params=_params(("arbitrary",)),
    )(dh2, wo, q, kv, wq, h1, g)


def _mem_kv_bwd(dkv, mn, wkv, mem, g, after=()):
    m = mem.shape[0]

    def body(dkv_ref, mn_ref, w_ref, mem_ref, g_ref, dw_ref, dg_ref):
        dmn = jnp.zeros((m, D_MODEL), F32)
        mn = mn_ref[...]
        for j in range(4):
            dj = dkv_ref[:, j * 512:(j + 1) * 512].astype(BF16)
            dw_ref[j] = _dot_tn(mn, dj)
            dmn = dmn + _dot_nt(dj, w_ref[j])
        mv = mem_ref[...]
        dg_ref[...] = _rowsum(dmn * (mv * _rms_stats(mv)))

    return _tied_call(
        body, after, name="mem_kv_bwd", in_specs=[pl.BlockSpec(memory_space=pltpu.VMEM)] * 5,
        out_shape=[jax.ShapeDtypeStruct((4, D_MODEL, 512), F32), jax.ShapeDtypeStruct((1, D_MODEL), F32)],
        compiler_params=pltpu.CompilerParams(vmem_limit_bytes=VMEM_LIMIT_BYTES),
    )(dkv, mn, wkv, mem, g)


def _seqmix_bwd(dh1, x, z, c1, w_out, w_in, g_mix, cw, lng, lnb, gg, gb, wpair, wpair_t, bias, t, after=()):
    s = x.shape[0]
    nt = s // t

    def body(dh1_ref, x_ref, z_ref, c1_ref, wo_ref, wi_ref, gm_ref, cw_ref, lng_ref, lnb_ref,
             gg_ref, gb_ref, wpair_ref, wpt_ref, bias_ref,
             gx_ref, dz_ref, dcw_ref, dcb_ref, dlng_ref, dlnb_ref, dgg_ref, dgb_ref, dws_ref, dbs_ref,
             dbin_ref, dgm_ref, dbuf, dsh, mixed_ref, dv_ref):
        i = pl.program_id(0)
        accs = (dcw_ref, dcb_ref, dlng_ref, dlnb_ref, dgg_ref, dgb_ref, dws_ref, dbs_ref, dbin_ref, dgm_ref)

        @pl.when(i == 0)
        def _():
            for r in accs:
                r[...] = jnp.zeros_like(r)
            dbuf[t:t + CONV_HALO, :] = jnp.zeros((CONV_HALO, CONV_WIDTH), F32)

        @pl.when(i > 0)
        def _():
            dbuf[t:t + CONV_HALO, :] = dbuf[0:CONV_HALO, :]

        dmix = _dot_nt(dh1_ref[...].astype(BF16), wo_ref[...])

        xh, rs = _ln_stats(c1_ref[...])
        lng = lng_ref[...]
        ln = xh * lng + lnb_ref[...]
        sl = _sigmoid(ln)
        dln = dmix[:, 0:512] * (sl * (1.0 + ln * (1.0 - sl)))
        dc1, dg_ln, db_ln = _ln_bwd(dln, xh, rs, lng)
        dlng_ref[...] += dg_ln
        dlnb_ref[...] += db_ln
        dcb_ref[...] += _rowsum(dc1)
        dbuf[0:t, :] = dc1

        za = z_ref[:, 0:512]
        sg = _sigmoid(z_ref[:, 512:1024])
        a = za * sg
        _shift_rows(dbuf, dsh, t)

        da = jnp.zeros((t, CONV_WIDTH), F32)
        for k in range(CONV_KERNEL):
            later = _window(dbuf, dsh, CONV_KERNEL - 1 - k, t)
            da = da + cw_ref[k:k + 1, :] * later
            dcw_ref[k:k + 1, :] += _rowsum(a * later)
        dza = da * sg
        dzg = da * za * (sg * (1.0 - sg))
        dz_ref[:, 0:512] = dza.astype(BF16)
        dz_ref[:, 512:1024] = dzg.astype(BF16)
        dbin_ref[:, 0:512] += _rowsum(dza)
        dbin_ref[:, 512:1024] += _rowsum(dzg)

        dgm = dmix[:, 512:1024]
        u, du_dz = _gelu_parts(z_ref[:, 1024:1536])
        gv, dgv_dz = _gelu_parts(z_ref[:, 1536:2048])
        vxh, vrs = _ln_stats(gv)
        ggv = gg_ref[...]
        v = vxh * ggv + gb_ref[...]
        low = _lane_is_low_head()
        v_lo = jnp.where(low, v, 0.0).astype(BF16)
        v_hi = jnp.where(low, 0.0, v).astype(BF16)
        _gm_mix(v_lo, v_hi, wpair_ref, bias_ref, mixed_ref, t)
        dzu = dgm * mixed_ref[...] * du_dz
        dm = dgm * u
        dm_lo = jnp.where(low, dm, 0.0).astype(BF16)
        dm_hi = jnp.where(low, 0.0, dm).astype(BF16)
        vb = v.astype(BF16)
        tril = (lax.broadcasted_iota(jnp.int32, (CHUNK, CHUNK), 1)
                <= lax.broadcasted_iota(jnp.int32, (CHUNK, CHUNK), 0))
        for n in range(t // CHUNK):
            rows = slice(n * CHUNK, (n + 1) * CHUNK)
            dbs_ref[...] += dm[rows, :]
            for j in range(GM_HEADS // 2):
                cols = slice(j * LANES, (j + 1) * LANES)
                stack = jnp.concatenate([dm_lo[rows, cols], dm_hi[rows, cols]], axis=0)
                dws = _dot_nt(stack, vb[rows, cols])
                dws_ref[2 * j] += jnp.where(tril, dws[0:CHUNK], 0.0)
                dws_ref[2 * j + 1] += jnp.where(tril, dws[CHUNK:2 * CHUNK], 0.0)
                dv_ref[rows, cols] = _dot(wpt_ref[j], stack)
        dgv, dg_gm, db_gm = _ln_bwd(dv_ref[...], vxh, vrs, ggv)
        dgg_ref[...] += dg_gm
        dgb_ref[...] += db_gm
        dzv = dgv * dgv_dz
        dz_ref[:, 1024:1536] = dzu.astype(BF16)
        dz_ref[:, 1536:2048] = dzv.astype(BF16)
        dbin_ref[:, 1024:1536] += _rowsum(dzu)
        dbin_ref[:, 1536:2048] += _rowsum(dzv)

        dhn = jnp.zeros((t, D_MODEL), F32)
        for j in range(4):
            dhn = dhn + _dot_nt(dz_ref[:, j * 512:(j + 1) * 512], wi_ref[j])
        xv = x_ref[...]
        dv, dg = _rms_bwd(dhn, xv, _rms_stats(xv), gm_ref[...])
        gx_ref[...] = dh1_ref[...] + dv
        dgm_ref[...] += dg

    rev = lambda w: pl.BlockSpec((t, w), lambda i: (nt - 1 - i, 0))
    const = lambda *shape: pl.BlockSpec(shape, lambda i: (0,) * len(shape))
    f32 = lambda *shape: jax.ShapeDtypeStruct(shape, F32)
    return _tied_call(
        body, after, name="seqmix_bwd", grid=(nt,),
        in_specs=[rev(D_MODEL), rev(D_MODEL), rev(2048), rev(CONV_WIDTH),
                  const(D_MODEL, D_MODEL), const(4, D_MODEL, 512), const(1, D_MODEL),
                  const(CONV_HALO, CONV_WIDTH), const(1, 512), const(1, 512), const(1, 512), const(1, 512),
                  const(4, CHUNK, 2 * CHUNK), const(4, CHUNK, 2 * CHUNK), const(CHUNK, GM_WIDTH)],
        out_specs=[rev(D_MODEL), rev(2048),
                   const(CONV_HALO, CONV_WIDTH), const(1, 512), const(1, 512), const(1, 512), const(1, 512),
                   const(1, 512), const(GM_HEADS, CHUNK, CHUNK), const(CHUNK, GM_WIDTH), const(1, 2048),
                   const(1, D_MODEL)],
        out_shape=[f32(s, D_MODEL), jax.ShapeDtypeStruct((s, 2048), BF16),
                   f32(CONV_HALO, CONV_WIDTH), f32(1, 512), f32(1, 512), f32(1, 512), f32(1, 512),
                   f32(1, 512), f32(GM_HEADS, CHUNK, CHUNK), f32(CHUNK, GM_WIDTH), f32(1, 2048),
                   f32(1, D_MODEL)],
        scratch_shapes=[pltpu.VMEM((t + CONV_HALO, CONV_WIDTH), F32),
                        pltpu.VMEM((SUBLANES - 1, t + CONV_HALO - SUBLANES, CONV_WIDTH), F32),
                        pltpu.VMEM((t, GM_WIDTH), F32), pltpu.VMEM((t, GM_WIDTH), F32)],
        compiler_params=_params(("arbitrary",)),
    )(dh1, x, z, c1, w_out, w_in, g_mix, cw, lng, lnb, gg, gb, wpair, wpair_t, bias)


def _head_bias_grad(dbs):
    def body(d_ref, o_ref):
        dv = d_ref[...]
        lane = lax.broadcasted_iota(jnp.int32, (CHUNK, LANES), 1)
        acc = jnp.zeros((CHUNK, LANES), F32)
        for h in range(GM_HEADS):
            sh = jnp.sum(dv[:, h * GM_HEAD_DIM:(h + 1) * GM_HEAD_DIM], axis=-1, keepdims=True)
            acc = acc + jnp.where(lane == h, sh, 0.0)
        o_ref[...] = acc

    return pl.pallas_call(body, name="head_bias_grad",
                          out_shape=jax.ShapeDtypeStruct((CHUNK, LANES), F32))(dbs)


def kernel(x, mem, norm_mix_g, w_in, b_in, conv_w, conv_b, conv_ln_g, conv_ln_b, gm_ln_g, gm_ln_b, gm_w_s, gm_b_s, w_out, norm_xa_g, mem_norm_g, xa_wq, xa_wkv, xa_wo, norm_ffn_g, ffn_w_gate_up, ffn_w_down, final_norm_g, loss_target, m_norm_mix_g, m_w_in, m_b_in, m_conv_w, m_conv_b, m_conv_ln_g, m_conv_ln_b, m_gm_ln_g, m_gm_ln_b, m_gm_w_s, m_gm_b_s, m_w_out, m_norm_xa_g, m_mem_norm_g, m_xa_wq, m_xa_wkv, m_xa_wo, m_norm_ffn_g, m_ffn_w_gate_up, m_ffn_w_down, m_final_norm_g, v_norm_mix_g, v_w_in, v_b_in, v_conv_w, v_conv_b, v_conv_ln_g, v_conv_ln_b, v_gm_ln_g, v_gm_ln_b, v_gm_w_s, v_gm_b_s, v_w_out, v_norm_xa_g, v_mem_norm_g, v_xa_wq, v_xa_wkv, v_xa_wo, v_norm_ffn_g, v_ffn_w_gate_up, v_ffn_w_down, v_final_norm_g):
    weights = dict(norm_mix_g=norm_mix_g, w_in=w_in, b_in=b_in, conv_w=conv_w, conv_b=conv_b, conv_ln_g=conv_ln_g,
                   conv_ln_b=conv_ln_b, gm_ln_g=gm_ln_g, gm_ln_b=gm_ln_b, gm_w_s=gm_w_s, gm_b_s=gm_b_s, w_out=w_out,
                   norm_xa_g=norm_xa_g, mem_norm_g=mem_norm_g, xa_wq=xa_wq, xa_wkv=xa_wkv, xa_wo=xa_wo,
                   norm_ffn_g=norm_ffn_g, ffn_w_gate_up=ffn_w_gate_up, ffn_w_down=ffn_w_down,
                   final_norm_g=final_norm_g)
    m_in = dict(norm_mix_g=m_norm_mix_g, w_in=m_w_in, b_in=m_b_in, conv_w=m_conv_w, conv_b=m_conv_b,
                conv_ln_g=m_conv_ln_g, conv_ln_b=m_conv_ln_b, gm_ln_g=m_gm_ln_g, gm_ln_b=m_gm_ln_b, gm_w_s=m_gm_w_s,
                gm_b_s=m_gm_b_s, w_out=m_w_out, norm_xa_g=m_norm_xa_g, mem_norm_g=m_mem_norm_g, xa_wq=m_xa_wq,
                xa_wkv=m_xa_wkv, xa_wo=m_xa_wo, norm_ffn_g=m_norm_ffn_g, ffn_w_gate_up=m_ffn_w_gate_up,
                ffn_w_down=m_ffn_w_down, final_norm_g=m_final_norm_g)
    v_in = dict(norm_mix_g=v_norm_mix_g, w_in=v_w_in, b_in=v_b_in, conv_w=v_conv_w, conv_b=v_conv_b,
                conv_ln_g=v_conv_ln_g, conv_ln_b=v_conv_ln_b, gm_ln_g=v_gm_ln_g, gm_ln_b=v_gm_ln_b, gm_w_s=v_gm_w_s,
                gm_b_s=v_gm_b_s, w_out=v_w_out, norm_xa_g=v_norm_xa_g, mem_norm_g=v_mem_norm_g, xa_wq=v_xa_wq,
                xa_wkv=v_xa_wkv, xa_wo=v_xa_wo, norm_ffn_g=v_norm_ffn_g, ffn_w_gate_up=v_ffn_w_gate_up,
                ffn_w_down=v_ffn_w_down, final_norm_g=v_final_norm_g)
    grads, delta, new_m, new_v = {}, {}, {}, {}

    s = x.shape[1]
    ts = _row_tile(s)
    tb = max(CHUNK, ts // 2)
    tw = 2 * ts if s % (2 * ts) == 0 and ts >= 512 else ts
    cx, cy, cc = _mesh_pos()
    chip = 2 * cx + cy
    pos = jnp.stack([chip, cc]).astype(jnp.int32)
    row = lambda a: a.reshape(1, -1)
    x2, mem2, tgt2 = x[0], mem[0], loss_target[0]

    big = dict(w_in=w_in, xa_wkv=xa_wkv, w_out=w_out, xa_wq=xa_wq, xa_wo=xa_wo,
               ffn_w_gate_up=ffn_w_gate_up, ffn_w_down=ffn_w_down)
    big_names = list(big)
    halves = lambda a: a.reshape(2, a.shape[0] // 2, a.shape[1])
    conv_w_pad = jnp.pad(conv_w, ((0, CONV_HALO - CONV_KERNEL), (0, 0)))
    first_names = ["w_in", "conv_w"]
    later_names = [nm for nm in big_names if nm != "w_in"]
    cast = dict(zip(first_names, _cast_into_slots([halves(w_in), halves(conv_w_pad)], pos, [BF16, F32], "cast_w_in")))
    cast.update(zip(later_names, _cast_into_slots([halves(big[nm]) for nm in later_names], pos,
                                                  [BF16] * len(later_names), "cast_" + later_names[0])))

    def start_gather(names, after):
        return _gather_start([cast[nm] for nm in names], "gather_start_" + names[0], after)

    def finish_gather(names, started, after):
        send_sems, recv_sems, bufs, _ = started
        landed = _gather_wait(send_sems, recv_sems, bufs, after, "gather_wait_" + names[0])
        return dict(zip(names, (b.reshape(N_CHIPS, -1, b.shape[-1])
                                for b in _pass_to_sibling(landed, "pass_" + names[0]))))

    attn_names = ["w_out", "xa_wq", "xa_wkv", "xa_wo"]
    gather_first = start_gather(first_names, ())
    hn1 = _norm_in(x2, row(norm_mix_g), tw, after=gather_first[3])
    gw = finish_gather(first_names, gather_first, [cast[nm] for nm in later_names] + [hn1])
    w_in_g = gw["w_in"]
    cw_g = jnp.concatenate([gw["conv_w"][k] for k in range(N_CHIPS)], axis=1)
    gather_attn = start_gather(attn_names, w_in_g)

    tril = jnp.tril(jnp.ones((CHUNK, CHUNK), dtype=bool))
    ws = jnp.where(tril[None], gm_w_s, 0.0)
    wpair = jnp.concatenate([ws[0::2], ws[1::2]], axis=2).astype(BF16)
    ws_t = jnp.swapaxes(ws, 1, 2)
    wpair_t = jnp.concatenate([ws_t[0::2], ws_t[1::2]], axis=2).astype(BF16)
    bias = jnp.repeat(gm_b_s.T, GM_HEAD_DIM, axis=1)

    z = _mix_in(hn1, w_in_g, row(b_in), tw, after=gather_attn[3])
    mix, c1 = _seqmix_fwd(z, cw_g, row(conv_b), row(conv_ln_g), row(conv_ln_b), row(gm_ln_g), row(gm_ln_b),
                          wpair, bias, ts)
    gw = finish_gather(attn_names, gather_attn, mix)
    w_out_g = gw["w_out"].reshape(D_MODEL, D_MODEL)
    wq_g = gw["xa_wq"].reshape(D_MODEL, D_MODEL)
    wkv_g = gw["xa_wkv"]
    wo_g = gw["xa_wo"].reshape(D_MODEL, D_MODEL)
    gather_gu = start_gather(["ffn_w_gate_up"], w_out_g)
    h1, hn2, q = _out_proj_q(x2, mix, w_out_g, row(norm_xa_g), wq_g, tw, after=gather_gu[3])
    mn, kv = _mem_kv(mem2, row(mem_norm_g), wkv_g)
    o, h2, hn3 = _attn_fwd(q, kv, h1, wo_g, row(norm_ffn_g), ts)
    wgu_g = finish_gather(["ffn_w_gate_up"], gather_gu, hn3)["ffn_w_gate_up"]
    gather_down = start_gather(["ffn_w_down"], wgu_g)
    gu, act = _ffn_up(hn3, wgu_g.reshape(2, 2, D_MODEL, FFN_HALF), tw, after=gather_down[3])
    wd_g = finish_gather(["ffn_w_down"], gather_down, act)["ffn_w_down"].reshape(FFN_HIDDEN, D_MODEL)
    dh3, dh3_b, sq, d_final_g = _ffn_down_loss(act, wd_g, h2, row(final_norm_g), tgt2, ts)
    loss_here = jnp.broadcast_to(0.5 * jnp.sum(sq) / D_MODEL, (1, 2, SUBLANES, LANES))

    def split(g, nm):
        r, c = big[nm].shape
        return g.reshape(N_CHIPS, 2, r // 2, c)

    def chip_sums(group, arrays, got):
        sums, parts = [None] * len(group), [None] * len(group)
        for blocks in (N_CHIPS, 1):
            idx = [i for i, a in enumerate(arrays) if a.shape[0] == blocks]
            if idx:
                out = _add_halves([arrays[i] for i in idx], [got[i] for i in idx], pos, "chip_sum_" + group[idx[0]],
                                  [F32 if group[i] == "loss" else BF16 for i in idx])
                for k, i in enumerate(idx):
                    sums[i], parts[i] = out[0][k], out[1][k]
        return sums, parts

    def start_swap(group, grads):
        return _swap_start([split(g, nm) for g, nm in zip(grads, group)], "swap_start_" + group[0])

    def start_exchange(group, swapping, after):
        sems, arrays, lands, _ = swapping
        arrays, got = _swap_wait(sems, arrays, lands, after, "swap_wait_" + group[0])
        sums, parts = chip_sums(group, arrays, got)
        return _exchange_start(sums, parts, "exchange_start_" + group[0])

    def finish_exchange(group, started, after):
        sems, sums, parts, _ = started
        parts = _exchange_wait(sems, sums, parts, after, "exchange_wait_" + group[0])
        return _sum_chips(parts, pos, "total_" + group[0])

    def join_and_update(group, after):
        joined = _join_halves([halves_of[nm] for nm in group], "join_halves_" + group[0], after)
        outs = _adamw([(weights[nm], j.reshape(big[nm].shape), m_in[nm], v_in[nm]) for nm, j in zip(group, joined)],
                      "adamw_" + group[0])
        for nm, out in zip(group, outs):
            grads[nm], delta[nm], new_m[nm], new_v[nm] = out
        return [new_v[nm] for nm in group]

    as3 = lambda a: a.reshape((1,) + a.shape)
    halves_of = {}

    g_down = _grad_w(act, as3(dh3_b), FFN_HALF, D_MODEL, "grad_ffn_w_down")
    group_a = ["ffn_w_down"]
    swap_a = start_swap(group_a, [g_down])
    dgu, dh2, dh2_b, d_ffn_g = _ffn_bwd(dh3, wd_g.reshape(2, FFN_HALF, D_MODEL), gu, wgu_g, h2, row(norm_ffn_g), tb,
                                        after=swap_a[3])
    exch_a = start_exchange(group_a, swap_a, dh2)
    g_gu = _grad_w(hn3, dgu, D_MODEL, FFN_HALF, "grad_ffn_w_gate_up", after=exch_a[3])
    halves_of.update(zip(group_a, finish_exchange(group_a, exch_a, g_gu)))

    group_b = ["ffn_w_gate_up"]
    swap_b = start_swap(group_b, [g_gu])
    dh1, dh1_b, dq, dkv, d_xa_g = _attn_bwd(dh2, wo_g, q, kv, wq_g, h1, row(norm_xa_g), ts, after=swap_b[3])
    exch_b = start_exchange(group_b, swap_b, dh1)
    g_wkv, d_mem_g = _mem_kv_bwd(dkv, mn, wkv_g, mem2, row(mem_norm_g), after=exch_b[3])
    g_wo = _grad_w(o, as3(dh2_b), D_MODEL, D_MODEL, "grad_xa_wo", after=exch_b[3])
    g_wq = _grad_w(hn2, as3(dq), D_MODEL, D_MODEL, "grad_xa_wq", after=exch_b[3])
    g_wout = _grad_w(mix, as3(dh1_b), D_MODEL, D_MODEL, "grad_w_out", after=exch_b[3])
    halves_of.update(zip(group_b, finish_exchange(group_b, exch_b, (g_wkv, g_wo, g_wq, g_wout))))

    group_c = ["xa_wo", "xa_wq", "xa_wkv", "w_out"]
    swap_c = start_swap(group_c, [g_wo, g_wq, g_wkv, g_wout])
    (gx, dz, d_cw, d_cb, d_lng, d_lnb, d_gg, d_gb, d_ws, d_bs_sum, d_bin, d_mix_g) = _seqmix_bwd(
        dh1, x2, z, c1, w_out_g, w_in_g, row(norm_mix_g), cw_g, row(conv_ln_g), row(conv_ln_b),
        row(gm_ln_g), row(gm_ln_b), wpair, wpair_t, bias, tb, after=swap_c[3])
    d_bs = _head_bias_grad(d_bs_sum)[:, :GM_HEADS].T
    exch_c = start_exchange(group_c, swap_c, dz)
    g_win = _grad_w(hn1, as3(dz), D_MODEL, 512, "grad_w_in", after=exch_c[3])
    done_ab = join_and_update(group_a + group_b, g_win)
    halves_of.update(zip(group_c, finish_exchange(group_c, exch_c, (g_win, *done_ab))))

    small_names = ["norm_mix_g", "b_in", "conv_w", "conv_b", "conv_ln_g", "conv_ln_b", "gm_ln_g", "gm_ln_b",
                   "gm_w_s", "gm_b_s", "norm_xa_g", "mem_norm_g", "norm_ffn_g", "final_norm_g"]
    d_cw_by_chip = jnp.swapaxes(d_cw.reshape(CONV_HALO, N_CHIPS, LANES), 0, 1).reshape(-1, LANES)
    small_grads = dict(norm_mix_g=d_mix_g, b_in=d_bin, conv_w=d_cw_by_chip, conv_b=d_cb, conv_ln_g=d_lng,
                       conv_ln_b=d_lnb, gm_ln_g=d_gg, gm_ln_b=d_gb, gm_w_s=d_ws, gm_b_s=d_bs, norm_xa_g=d_xa_g,
                       mem_norm_g=d_mem_g, norm_ffn_g=d_ffn_g, final_norm_g=d_final_g)

    def rows_form(a):
        a = a.reshape(-1, LANES)
        return jnp.pad(a, ((0, -a.shape[0] % SUBLANES), (0, 0)))

    pieces = [rows_form(small_grads[nm]) for nm in small_names]
    offsets, total = [], 0
    for p in pieces:
        offsets.append(total)
        total += p.shape[0]
    pack_rows = -(-total // 32) * 32
    small_pack = jnp.pad(jnp.concatenate(pieces, axis=0), ((0, pack_rows - total), (0, 0)))

    group_d = ["w_in", "small", "loss"]
    arrays_d = [split(g_win, "w_in"), small_pack.reshape(1, 2, pack_rows // 2, LANES), loss_here]
    sums_d, parts_d = chip_sums(group_d, arrays_d, _swap_halves(arrays_d, "swap_halves_w_in"))
    exch_d = _exchange_start(sums_d, parts_d, "exchange_start_w_in")
    done_c = join_and_update(group_c, exch_d[3])
    halves_of.update(zip(group_d, finish_exchange(group_d, exch_d, done_c)))
    joined_d = _join_halves([halves_of[nm] for nm in group_d], "join_halves_w_in")
    loss = joined_d[2][0, 0, 0]
    grads["w_in"], delta["w_in"], new_m["w_in"], new_v["w_in"] = _adamw(
        [(w_in, joined_d[0].reshape(w_in.shape), m_w_in, v_w_in)], "adamw_w_in")[0]

    local_rows = lambda a, nm: a if nm == "conv_w" else a.reshape(-1, LANES)
    params = [tuple(local_rows(src[nm], nm) for src in (weights, m_in, v_in)) for nm in small_names]
    outs = _adamw_small(joined_d[1].reshape(pack_rows, LANES), pos, params, offsets, small_names.index("conv_w"))
    for k, nm in enumerate(small_names):
        for dst, a in zip((grads, delta, new_m, new_v), outs[4 * k:4 * k + 4]):
            dst[nm] = a

    order = ["norm_mix_g", "w_in", "b_in", "conv_w", "conv_b", "conv_ln_g", "conv_ln_b", "gm_ln_g", "gm_ln_b",
             "gm_w_s", "gm_b_s", "w_out", "norm_xa_g", "mem_norm_g", "xa_wq", "xa_wkv", "xa_wo", "norm_ffn_g",
             "ffn_w_gate_up", "ffn_w_down", "final_norm_g"]
    fit = lambda a, nm: a.reshape(weights[nm].shape)
    return (loss, gx.reshape(x.shape),
            *[fit(grads[nm], nm) for nm in order], *[fit(delta[nm], nm) for nm in order],
            *[fit(new_m[nm], nm) for nm in order], *[fit(new_v[nm], nm) for nm in order])
```

```python
import functools

import jax
import jax.numpy as jnp
from jax import lax
from jax.experimental import pallas as pl
from jax.experimental.pallas import tpu as pltpu

F32 = jnp.float32
BF16 = jnp.bfloat16

D_MODEL = 1024
CONV_WIDTH = 512
GM_WIDTH = 512
CONV_KERNEL = 31
CONV_HALO = 32
GRAD_ROWS = 2048
CHUNK = 128
GM_HEADS = 8
GM_HEAD_DIM = 64
XA_HEADS = 4
XA_HEAD_DIM = 256
FFN_HIDDEN = 2816
FFN_HALF = FFN_HIDDEN // 2
RMS_EPS = 1e-6
LN_EPS = 1e-5
N_CHIPS = 4
LANES = 128
SUBLANES = 8

ADAM_LR = 0.001
ADAM_B1 = 0.9
ADAM_B2 = 0.999
ADAM_EPS = 1e-08
ADAM_WD = 0.01
ADAM_STEP = 10

VMEM_LIMIT_BYTES = 56 * 1024 * 1024
MESH = pl.DeviceIdType.MESH
ANY = pl.BlockSpec(memory_space=pl.ANY)
HBM_SPEC = pl.BlockSpec(memory_space=pltpu.HBM)
SEM_SPEC = pl.BlockSpec(memory_space=pltpu.SEMAPHORE)

_NT = (((1,), (1,)), ((), ()))
_TN = (((0,), (0,)), ((), ()))
_GELU_C = 0.7978845608028654
_GELU_A = 0.044715


def _dot(a, b):
    return jnp.dot(a, b, preferred_element_type=F32)


def _dot_nt(a, b):
    return lax.dot_general(a, b, _NT, preferred_element_type=F32)


def _dot_tn(a, b):
    return lax.dot_general(a, b, _TN, preferred_element_type=F32)


def _mean(v):
    return jnp.mean(v, axis=-1, keepdims=True)


def _rowsum(v):
    return jnp.sum(v, axis=0, keepdims=True)


def _sigmoid(v):
    return 1.0 / (1.0 + jnp.exp(-v))


def _gelu_parts(v):
    v2 = v * v
    t = jnp.tanh(_GELU_C * (v + _GELU_A * v * v2))
    g = 0.5 * v * (1.0 + t)
    dg = 0.5 * (1.0 + t) + 0.5 * v * (1.0 - t * t) * (_GELU_C * (1.0 + 3.0 * _GELU_A * v2))
    return g, dg


def _rms_stats(v):
    return lax.rsqrt(_mean(v * v) + RMS_EPS)


def _rms_bwd(dy, v, r, g):
    n = v * r
    dn = dy * g
    dv = r * (dn - n * _mean(dn * n))
    return dv, _rowsum(dy * n)


def _ln_stats(v):
    mu = _mean(v)
    xc = v - mu
    rs = lax.rsqrt(_mean(xc * xc) + LN_EPS)
    return xc * rs, rs


def _ln_bwd(dy, xh, rs, g):
    dxh = dy * g
    dv = rs * (dxh - _mean(dxh) - xh * _mean(dxh * xh))
    return dv, _rowsum(dy * xh), _rowsum(dy)


def _params(sem):
    return pltpu.CompilerParams(dimension_semantics=sem, vmem_limit_bytes=VMEM_LIMIT_BYTES)


def _row_tile(s):
    return 512 if s % 512 == 0 and s >= 2048 else 128


def _mesh_pos():
    return lax.axis_index("x"), lax.axis_index("y"), lax.axis_index("c")


def _cast_into_slots(ws, pos, dtypes, name):
    n = len(ws)

    def body(pos_ref, *refs):
        for a in range(n):
            refs[n + a][0] = refs[a][...].astype(dtypes[a])

    return pl.pallas_call(
        body, name=name,
        grid_spec=pltpu.PrefetchScalarGridSpec(
            num_scalar_prefetch=1, grid=(2,),
            in_specs=[pl.BlockSpec((1,) + w.shape[1:], lambda i, p: (i, 0, 0)) for w in ws],
            out_specs=[pl.BlockSpec((1, 1) + w.shape[1:], lambda i, p: (p[0], i, 0, 0)) for w in ws]),
        out_shape=[jax.ShapeDtypeStruct((N_CHIPS,) + w.shape, dt) for w, dt in zip(ws, dtypes)],
        compiler_params=_params(("parallel",)),
    )(pos, *ws)


def _adam_update(w, g, m, v):
    nm = ADAM_B1 * m + (1.0 - ADAM_B1) * g
    nv = ADAM_B2 * v + (1.0 - ADAM_B2) * (g * g)
    m_hat = nm / (1.0 - ADAM_B1 ** ADAM_STEP)
    v_hat = nv / (1.0 - ADAM_B2 ** ADAM_STEP)
    return -ADAM_LR * (m_hat / (jnp.sqrt(v_hat) + ADAM_EPS) + ADAM_WD * w), nm, nv


ADAM_STEPS = 4


def _adamw(quads, name, after=()):
    n = len(quads)

    def body(*refs):
        ins, outs = refs[:4 * n], refs[4 * n:]
        for a in range(n):
            w, g, m, v = (r[...] for r in ins[4 * a:4 * a + 4])
            outs[4 * a][...] = g
            outs[4 * a + 1][...], outs[4 * a + 2][...], outs[4 * a + 3][...] = _adam_update(w, g, m, v)

    specs = [pl.BlockSpec((q[0].shape[0] // ADAM_STEPS, q[0].shape[1]), lambda i: (i, 0)) for q in quads]
    out = _tied_call(
        body, after, name=name, grid=(ADAM_STEPS,),
        in_specs=[sp for sp in specs for _ in range(4)], out_specs=[sp for sp in specs for _ in range(4)],
        out_shape=[jax.ShapeDtypeStruct(q[0].shape, F32) for q in quads for _ in range(4)],
        compiler_params=_params(("parallel",)),
    )(*[a for q in quads for a in q])
    return [tuple(out[4 * a:4 * a + 4]) for a in range(n)]


def _adamw_small(gpack, pos, params, offsets, conv_at):
    n = len(params)

    def body(pos_ref, g_ref, *refs):
        ins, outs = refs[:3 * n], refs[3 * n:]
        for k in range(n):
            rows = params[k][0].shape[0]
            start = offsets[k]
            if k == conv_at:
                start = pl.multiple_of(start + pos_ref[0] * CONV_HALO, SUBLANES)
            g = g_ref[pl.ds(start, rows), :]
            outs[4 * k][...] = g
            outs[4 * k + 1][...], outs[4 * k + 2][...], outs[4 * k + 3][...] = _adam_update(
                ins[3 * k][...], g, ins[3 * k + 1][...], ins[3 * k + 2][...])

    flat = [a for p in params for a in p]
    vmem = pl.BlockSpec(memory_space=pltpu.VMEM)
    return pl.pallas_call(
        body, name="adamw_small",
        in_specs=[pl.BlockSpec(memory_space=pltpu.SMEM), vmem] + [vmem] * len(flat),
        out_specs=[vmem] * (4 * n),
        out_shape=[jax.ShapeDtypeStruct(p[0].shape, F32) for p in params for _ in range(4)],
    )(pos, gpack, *flat)


def _as_tuple(after):
    return tuple(after) if isinstance(after, (tuple, list)) else (after,)


def _tied_call(body, after, *, in_specs, **kwargs):
    after = _as_tuple(after)
    n_in, n_after = len(in_specs), len(after)

    def tied(*refs):
        body(*refs[:n_in], *refs[n_in + n_after:])

    call = pl.pallas_call(tied, in_specs=list(in_specs) + [ANY] * n_after, **kwargs)
    return lambda *operands: call(*operands, *after)


def _other_chips(x, y):
    return [(1 - x, y), (x, 1 - y), (1 - x, 1 - y)]


def _gather_descriptors(bufs, send_of, recv_of):
    x, y, c = _mesh_pos()
    me = 2 * x + y
    chips = _other_chips(x, y)
    sends, arrivals = [], []
    for a in range(len(bufs)):
        for k in range(3):
            ck = 2 * chips[k][0] + chips[k][1]

            def copy(slot, a=a, k=k):
                return pltpu.make_async_remote_copy(
                    src_ref=bufs[a].at[slot, c], dst_ref=bufs[a].at[slot, c],
                    send_sem=send_of(a, k), recv_sem=recv_of(a, k),
                    device_id=(*chips[k], c), device_id_type=MESH)

            sends.append(functools.partial(copy, me))
            arrivals.append(functools.partial(copy, ck))
    return sends, arrivals


def _gather_start(bufs, name, after=()):
    n = len(bufs)
    ns = 3 * n

    def body(*refs):
        sems = refs[n:n + 2 * ns]
        thru = refs[n + 2 * ns:2 * n + 2 * ns]
        token = refs[2 * n + 2 * ns]
        _chips_handshake()
        sends, _ = _gather_descriptors(thru, lambda a, k: sems[3 * a + k], lambda a, k: sems[ns + 3 * a + k])
        for cp in sends:
            cp().start()
        token[...] = jnp.zeros_like(token)

    held = [pltpu.with_memory_space_constraint(b, pltpu.HBM) for b in bufs]
    out = _tied_call(
        body, after, name=name,
        out_shape=(*[pltpu.SemaphoreType.DMA(())] * (2 * ns), *[pltpu.HBM(b.shape, b.dtype) for b in held],
                   jax.ShapeDtypeStruct((8, LANES), F32)),
        in_specs=[HBM_SPEC] * n,
        out_specs=(*[SEM_SPEC] * (2 * ns), *[HBM_SPEC] * n, pl.BlockSpec(memory_space=pltpu.VMEM)),
        input_output_aliases={i: 2 * ns + i for i in range(n)},
        compiler_params=pltpu.CompilerParams(has_side_effects=pltpu.SideEffectType.DATAFLOW_SIDE_EFFECTING,
                                             collective_id=CHIPS_COLLECTIVE_ID),
    )(*held)
    return list(out[:ns]), list(out[ns:2 * ns]), list(out[2 * ns:2 * ns + n]), out[2 * ns + n]


def _gather_wait(send_sems, recv_sems, bufs, after, name):
    n = len(bufs)
    ns = 3 * n

    def body(*refs):
        buf_ref = refs[:n]
        sem_ref = refs[n:n + 2 * ns]
        sends, arrivals = _gather_descriptors(buf_ref, lambda a, k: sem_ref[3 * a + k],
                                              lambda a, k: sem_ref[ns + 3 * a + k])
        for cp in sends:
            cp().wait_send()
        for cp in arrivals:
            cp().wait_recv()

    out = pl.pallas_call(
        body, name=name,
        out_shape=tuple(pltpu.HBM(b.shape, b.dtype) for b in bufs),
        in_specs=[HBM_SPEC] * n + [SEM_SPEC] * (2 * ns) + [ANY] * len(_as_tuple(after)),
        out_specs=tuple([HBM_SPEC] * n),
        input_output_aliases={i: i for i in range(n)},
        compiler_params=pltpu.CompilerParams(has_side_effects=pltpu.SideEffectType.DATAFLOW_SIDE_EFFECTING),
    )(*bufs, *send_sems, *recv_sems, *_as_tuple(after))
    return list(out)


SIBLING_COLLECTIVE_ID = 0


def _sibling_handshake():
    x, y, c = _mesh_pos()
    barrier = pltpu.get_barrier_semaphore()
    pl.semaphore_signal(barrier, inc=1, device_id=(x, y, 1 - c), device_id_type=MESH)
    pl.semaphore_wait(barrier, 1)


CHIPS_COLLECTIVE_ID = 1


def _chips_handshake():
    x, y, c = _mesh_pos()
    barrier = pltpu.get_barrier_semaphore()
    for chip in _other_chips(x, y):
        pl.semaphore_signal(barrier, inc=1, device_id=(*chip, c), device_id_type=MESH)
    pl.semaphore_wait(barrier, 3)


def _pass_to_sibling(bufs, name):
    n = len(bufs)

    def body(*refs):
        outs = refs[n:2 * n]
        send_sem, recv_sem = refs[2 * n:]
        x, y, c = _mesh_pos()
        chips = _other_chips(x, y)
        _sibling_handshake()

        def half(a, k, which):
            ck = 2 * chips[k][0] + chips[k][1]
            return pltpu.make_async_remote_copy(
                src_ref=outs[a].at[ck, which], dst_ref=outs[a].at[ck, which],
                send_sem=send_sem.at[a, k], recv_sem=recv_sem.at[a, k],
                device_id=(x, y, 1 - c), device_id_type=MESH)

        sends = [half(a, k, c) for a in range(n) for k in range(3)]
        for cp in sends:
            cp.start()
        for a in range(n):
            for k in range(3):
                half(a, k, 1 - c).wait_recv()
        for cp in sends:
            cp.wait_send()

    return pl.pallas_call(
        body, name=name,
        in_specs=[ANY] * n, out_specs=[ANY] * n,
        out_shape=[jax.ShapeDtypeStruct(b.shape, b.dtype) for b in bufs],
        input_output_aliases={a: a for a in range(n)},
        scratch_shapes=[pltpu.SemaphoreType.DMA((n, 3))] * 2,
        compiler_params=pltpu.CompilerParams(collective_id=SIBLING_COLLECTIVE_ID),
    )(*bufs)


def _swap_halves(grads, name):
    n = len(grads)

    def body(*refs):
        ins, outs = refs[:n], refs[n:2 * n]
        send_sem, recv_sem = refs[2 * n:]
        x, y, c = _mesh_pos()
        _sibling_handshake()
        cps = [pltpu.make_async_remote_copy(
            src_ref=ins[a].at[:, pl.ds(1 - c, 1)], dst_ref=outs[a],
            send_sem=send_sem.at[a], recv_sem=recv_sem.at[a],
            device_id=(x, y, 1 - c), device_id_type=MESH) for a in range(n)]
        for cp in cps:
            cp.start()
        for cp in cps:
            cp.wait()

    out_shape = [jax.ShapeDtypeStruct((g.shape[0], 1) + g.shape[2:], g.dtype) for g in grads]
    return pl.pallas_call(
        body, name=name,
        in_specs=[ANY] * n, out_specs=[ANY] * n, out_shape=out_shape,
        scratch_shapes=[pltpu.SemaphoreType.DMA((n,))] * 2,
        compiler_params=pltpu.CompilerParams(collective_id=SIBLING_COLLECTIVE_ID),
    )(*grads)


def _swap_descriptors(grads, lands, send_of, recv_of):
    x, y, c = _mesh_pos()
    return [functools.partial(
        pltpu.make_async_remote_copy,
        src_ref=grads[a].at[:, pl.ds(1 - c, 1)], dst_ref=lands[a],
        send_sem=send_of(a), recv_sem=recv_of(a),
        device_id=(x, y, 1 - c), device_id_type=MESH) for a in range(len(grads))]


def _swap_start(grads, name):
    n = len(grads)

    def body(*refs):
        sems = refs[2 * n:4 * n]
        g_thru, l_thru = refs[4 * n:5 * n], refs[5 * n:6 * n]
        token = refs[6 * n]
        _sibling_handshake()
        for cp in _swap_descriptors(g_thru, l_thru, lambda a: sems[a], lambda a: sems[n + a]):
            cp().start()
        token[...] = jnp.zeros_like(token)

    lands = [lax.empty((g.shape[0], 1) + g.shape[2:], g.dtype) for g in grads]
    held = [pltpu.with_memory_space_constraint(a, pltpu.HBM) for a in (*grads, *lands)]
    out = pl.pallas_call(
        body, name=name,
        out_shape=(*[pltpu.SemaphoreType.DMA(())] * (2 * n), *[pltpu.HBM(a.shape, a.dtype) for a in held],
                   jax.ShapeDtypeStruct((8, LANES), F32)),
        in_specs=[HBM_SPEC] * (2 * n),
        out_specs=(*[SEM_SPEC] * (2 * n), *[HBM_SPEC] * (2 * n), pl.BlockSpec(memory_space=pltpu.VMEM)),
        input_output_aliases={i: 2 * n + i for i in range(2 * n)},
        compiler_params=pltpu.CompilerParams(has_side_effects=pltpu.SideEffectType.DATAFLOW_SIDE_EFFECTING,
                                             collective_id=SIBLING_COLLECTIVE_ID),
    )(*held)
    return list(out[:2 * n]), list(out[2 * n:3 * n]), list(out[3 * n:4 * n]), out[4 * n]


def _swap_wait(sems, grads, lands, after, name):
    n = len(grads)

    def body(*refs):
        g_ref, l_ref = refs[:n], refs[n:2 * n]
        sem_ref = refs[2 * n:4 * n]
        for cp in _swap_descriptors(g_ref, l_ref, lambda a: sem_ref[a], lambda a: sem_ref[n + a]):
            cp().wait()

    out = pl.pallas_call(
        body, name=name,
        out_shape=tuple(pltpu.HBM(a.shape, a.dtype) for a in (*grads, *lands)),
        in_specs=[HBM_SPEC] * (2 * n) + [SEM_SPEC] * (2 * n) + [ANY] * len(_as_tuple(after)),
        out_specs=tuple([HBM_SPEC] * (2 * n)),
        input_output_aliases={i: i for i in range(2 * n)},
        compiler_params=pltpu.CompilerParams(has_side_effects=pltpu.SideEffectType.DATAFLOW_SIDE_EFFECTING),
    )(*grads, *lands, *sems, *_as_tuple(after))
    return list(out[:n]), list(out[n:])


def _add_halves(gs, gots, pos, name, dtypes):
    n = len(gs)
    j = gs[0].shape[0]

    def body(pos_ref, *refs):
        g_refs, r_refs = refs[:n], refs[n:2 * n]
        o_refs, p_refs = refs[2 * n:3 * n], refs[3 * n:]
        vals = [(g_refs[a][0, 0] + r_refs[a][0, 0]).astype(dtypes[a]) for a in range(n)]
        for a in range(n):
            o_refs[a][0] = vals[a]
        if j == 1:
            for a in range(n):
                p_refs[a][0] = vals[a]
        else:
            @pl.when(pl.program_id(0) == pos_ref[0])
            def _():
                for a in range(n):
                    p_refs[a][0] = vals[a]

    blk = lambda g: (1,) + g.shape[2:]
    out = pl.pallas_call(
        body, name=name,
        grid_spec=pltpu.PrefetchScalarGridSpec(
            num_scalar_prefetch=1, grid=(j,),
            in_specs=[pl.BlockSpec((1,) + blk(g), lambda i, p: (i, p[1], 0, 0)) for g in gs]
            + [pl.BlockSpec((1,) + blk(g), lambda i, p: (i, 0, 0, 0)) for g in gs],
            out_specs=[pl.BlockSpec(blk(g), lambda i, p: (i, 0, 0)) for g in gs]
            + [pl.BlockSpec(blk(g), lambda i, p: (p[0], 0, 0)) for g in gs]),
        out_shape=[jax.ShapeDtypeStruct((j,) + g.shape[2:], dt) for g, dt in zip(gs, dtypes)]
        + [jax.ShapeDtypeStruct((N_CHIPS,) + g.shape[2:], dt) for g, dt in zip(gs, dtypes)],
        compiler_params=_params(("arbitrary",)),
    )(pos, *gs, *gots)
    return list(out[:n]), list(out[n:])


def _exchange_descriptors(sums, parts, send_of, recv_of):
    x, y, c = _mesh_pos()
    me = 2 * x + y
    chips = _other_chips(x, y)
    sends, arrivals = [], []
    for a in range(len(sums)):
        for k in range(3):
            ck = 2 * chips[k][0] + chips[k][1]
            mine = sums[a].at[ck] if sums[a].shape[0] == N_CHIPS else sums[a].at[0]

            def copy(dst_slot, a=a, k=k, mine=mine):
                return pltpu.make_async_remote_copy(
                    src_ref=mine, dst_ref=parts[a].at[dst_slot],
                    send_sem=send_of(a, k), recv_sem=recv_of(a, k),
                    device_id=(*chips[k], c), device_id_type=MESH)

            sends.append(functools.partial(copy, me))
            arrivals.append(functools.partial(copy, ck))
    return sends, arrivals


def _exchange_start(sums, parts, name):
    n = len(sums)
    ns = 3 * n

    def body(*refs):
        sems = refs[2 * n:2 * n + 2 * ns]
        sums_thru = refs[2 * n + 2 * ns:3 * n + 2 * ns]
        parts_thru = refs[3 * n + 2 * ns:4 * n + 2 * ns]
        token = refs[4 * n + 2 * ns]
        _chips_handshake()
        sends, _ = _exchange_descriptors(sums_thru, parts_thru, lambda a, k: sems[3 * a + k],
                                         lambda a, k: sems[ns + 3 * a + k])
        for cp in sends:
            cp().start()
        token[...] = jnp.zeros_like(token)

    hbm = lambda a: pltpu.HBM(a.shape, a.dtype)
    held = [pltpu.with_memory_space_constraint(a, pltpu.HBM) for a in (*sums, *parts)]
    out = pl.pallas_call(
        body, name=name,
        out_shape=(*[pltpu.SemaphoreType.DMA(())] * (2 * ns), *[hbm(a) for a in held],
                   jax.ShapeDtypeStruct((8, LANES), F32)),
        in_specs=[HBM_SPEC] * (2 * n),
        out_specs=(*[SEM_SPEC] * (2 * ns), *[HBM_SPEC] * (2 * n), pl.BlockSpec(memory_space=pltpu.VMEM)),
        input_output_aliases={i: 2 * ns + i for i in range(2 * n)},
        compiler_params=pltpu.CompilerParams(has_side_effects=pltpu.SideEffectType.DATAFLOW_SIDE_EFFECTING,
                                             collective_id=CHIPS_COLLECTIVE_ID),
    )(*held)
    return (list(out[:2 * ns]), list(out[2 * ns:2 * ns + n]), list(out[2 * ns + n:2 * ns + 2 * n]),
            out[2 * ns + 2 * n])


def _exchange_wait(sems, sums, parts, after, name):
    n = len(sums)
    ns = 3 * n

    def body(*refs):
        sums_ref, parts_ref = refs[:n], refs[n:2 * n]
        sem_ref = refs[2 * n:2 * n + 2 * ns]
        sends, arrivals = _exchange_descriptors(sums_ref, parts_ref, lambda a, k: sem_ref[3 * a + k],
                                                lambda a, k: sem_ref[ns + 3 * a + k])
        for cp in sends:
            cp().wait_send()
        for cp in arrivals:
            cp().wait_recv()

    hbm = lambda a: pltpu.HBM(a.shape, a.dtype)
    out = pl.pallas_call(
        body, name=name,
        out_shape=tuple(hbm(a) for a in (*sums, *parts)),
        in_specs=[HBM_SPEC] * (2 * n) + [SEM_SPEC] * (2 * ns) + [ANY] * len(_as_tuple(after)),
        out_specs=tuple([HBM_SPEC] * (2 * n)),
        input_output_aliases={i: i for i in range(2 * n)},
        compiler_params=pltpu.CompilerParams(has_side_effects=pltpu.SideEffectType.DATAFLOW_SIDE_EFFECTING),
    )(*sums, *parts, *sems, *_as_tuple(after))
    return list(out[n:])


def _sum_chips(parts, pos, name):
    n = len(parts)

    def body(pos_ref, *refs):
        for a in range(n):
            p_ref = refs[a]
            refs[n + a][0] = (((p_ref[0].astype(F32) + p_ref[1].astype(F32)) + p_ref[2].astype(F32))
                              + p_ref[3].astype(F32))

    out = pl.pallas_call(
        body, name=name,
        grid_spec=pltpu.PrefetchScalarGridSpec(
            num_scalar_prefetch=1, grid=(1,),
            in_specs=[pl.BlockSpec(p.shape, lambda i, q: (0, 0, 0)) for p in parts],
            out_specs=[pl.BlockSpec((1,) + p.shape[1:], lambda i, q: (q[1], 0, 0)) for p in parts]),
        out_shape=[jax.ShapeDtypeStruct((2,) + p.shape[1:], F32) for p in parts],
        compiler_params=_params(("arbitrary",)),
    )(pos, *parts)
    return list(out)


def _join_halves(fulls, name, after=()):
    n = len(fulls)

    def body(*refs):
        outs = refs[n:2 * n]
        send_sem, recv_sem = refs[2 * n:]
        x, y, c = _mesh_pos()
        _sibling_handshake()

        def half(a, which):
            return pltpu.make_async_remote_copy(
                src_ref=outs[a].at[which], dst_ref=outs[a].at[which],
                send_sem=send_sem.at[a], recv_sem=recv_sem.at[a],
                device_id=(x, y, 1 - c), device_id_type=MESH)

        sends = [half(a, c) for a in range(n)]
        for cp in sends:
            cp.start()
        for a in range(n):
            half(a, 1 - c).wait_recv()
        for cp in sends:
            cp.wait_send()

    out_shape = [jax.ShapeDtypeStruct(f.shape, f.dtype) for f in fulls]
    return _tied_call(
        body, after, name=name,
        in_specs=[ANY] * n, out_specs=[ANY] * n, out_shape=out_shape,
        input_output_aliases={a: a for a in range(n)},
        scratch_shapes=[pltpu.SemaphoreType.DMA((n,))] * 2,
        compiler_params=pltpu.CompilerParams(collective_id=SIBLING_COLLECTIVE_ID),
    )(*fulls)


def _norm_in(x, g, ts, after=()):
    s = x.shape[0]

    def body(x_ref, g_ref, hn_ref):
        xv = x_ref[...]
        hn_ref[...] = (xv * _rms_stats(xv) * g_ref[...]).astype(BF16)

    row = pl.BlockSpec((ts, D_MODEL), lambda i: (i, 0))
    return _tied_call(
        body, after, name="norm_in", grid=(s // ts,),
        in_specs=[row, pl.BlockSpec((1, D_MODEL), lambda i: (0, 0))], out_specs=row,
        out_shape=jax.ShapeDtypeStruct((s, D_MODEL), BF16),
        compiler_params=_params(("parallel",)),
    )(x, g)


def _mix_in(hn, w_in, b_in, ts, after=()):
    s = hn.shape[0]

    def body(hn_ref, w_ref, b_ref, z_ref):
        hv = hn_ref[...]
        for j in range(4):
            cols = slice(j * 512, (j + 1) * 512)
            z_ref[:, cols] = _dot(hv, w_ref[j]) + b_ref[:, cols]

    return _tied_call(
        body, after, name="mix_in", grid=(s // ts,),
        in_specs=[pl.BlockSpec((ts, D_MODEL), lambda i: (i, 0)),
                  pl.BlockSpec((4, D_MODEL, 512), lambda i: (0, 0, 0)),
                  pl.BlockSpec((1, 2048), lambda i: (0, 0))],
        out_specs=pl.BlockSpec((ts, 2048), lambda i: (i, 0)),
        out_shape=jax.ShapeDtypeStruct((s, 2048), F32),
        compiler_params=_params(("parallel",)),
    )(hn, w_in, b_in)


def _shift_rows(buf, shifted, t):
    rows = t + CONV_HALO - SUBLANES
    for r in range(1, SUBLANES):
        shifted[r - 1, 0:rows, :] = buf[pl.ds(r, rows), :]


def _window(buf, shifted, offset, t):
    r = offset % SUBLANES
    if r == 0:
        return buf[pl.ds(offset, t), :]
    return shifted[r - 1, pl.ds(offset - r, t), :]


def _lane_is_low_head():
    lane = lax.broadcasted_iota(jnp.int32, (1, GM_WIDTH), 1)
    return (lane & GM_HEAD_DIM) == 0


def _gm_mix(v_lo, v_hi, wpair_ref, bias_ref, mixed_ref, t):
    for n in range(t // CHUNK):
        rows = slice(n * CHUNK, (n + 1) * CHUNK)
        for j in range(GM_HEADS // 2):
            cols = slice(j * LANES, (j + 1) * LANES)
            rhs = jnp.concatenate([v_lo[rows, cols], v_hi[rows, cols]], axis=0)
            mixed_ref[rows, cols] = _dot(wpair_ref[j], rhs) + bias_ref[:, cols]


def _seqmix_fwd(z, cw, cb, lng, lnb, gg, gb, wpair, bias, t):
    s = z.shape[0]

    def body(z_ref, cw_ref, cb_ref, lng_ref, lnb_ref, gg_ref, gb_ref, wpair_ref, bias_ref,
             mix_ref, c1_ref, abuf, ash, mixed_ref):
        i = pl.program_id(0)

        @pl.when(i == 0)
        def _():
            abuf[0:CONV_HALO, :] = jnp.zeros((CONV_HALO, CONV_WIDTH), F32)

        @pl.when(i > 0)
        def _():
            abuf[0:CONV_HALO, :] = abuf[t:t + CONV_HALO, :]

        abuf[CONV_HALO:, :] = z_ref[:, 0:512] * _sigmoid(z_ref[:, 512:1024])
        _shift_rows(abuf, ash, t)
        acc = jnp.zeros((t, CONV_WIDTH), F32)
        for k in range(CONV_KERNEL):
            acc = acc + cw_ref[k:k + 1, :] * _window(abuf, ash, CONV_HALO - (CONV_KERNEL - 1) + k, t)
        c1 = acc + cb_ref[...]
        c1_ref[...] = c1
        xh, _ = _ln_stats(c1)
        ln = xh * lng_ref[...] + lnb_ref[...]
        mix_ref[:, 0:512] = (ln * _sigmoid(ln)).astype(BF16)

        u, _ = _gelu_parts(z_ref[:, 1024:1536])
        gv, _ = _gelu_parts(z_ref[:, 1536:2048])
        vxh, _ = _ln_stats(gv)
        v = vxh * gg_ref[...] + gb_ref[...]
        low = _lane_is_low_head()
        v_lo = jnp.where(low, v, 0.0).astype(BF16)
        v_hi = jnp.where(low, 0.0, v).astype(BF16)
        _gm_mix(v_lo, v_hi, wpair_ref, bias_ref, mixed_ref, t)
        mix_ref[:, 512:1024] = (u * mixed_ref[...]).astype(BF16)

    vec = lambda n: pl.BlockSpec((1, n), lambda i: (0, 0))
    return pl.pallas_call(
        body, name="seqmix_fwd", grid=(s // t,),
        in_specs=[pl.BlockSpec((t, 2048), lambda i: (i, 0)),
                  pl.BlockSpec((CONV_HALO, CONV_WIDTH), lambda i: (0, 0)),
                  vec(512), vec(512), vec(512), vec(512), vec(512),
                  pl.BlockSpec((4, CHUNK, 2 * CHUNK), lambda i: (0, 0, 0)),
                  pl.BlockSpec((CHUNK, GM_WIDTH), lambda i: (0, 0))],
        out_specs=[pl.BlockSpec((t, D_MODEL), lambda i: (i, 0)),
                   pl.BlockSpec((t, CONV_WIDTH), lambda i: (i, 0))],
        out_shape=[jax.ShapeDtypeStruct((s, D_MODEL), BF16), jax.ShapeDtypeStruct((s, CONV_WIDTH), F32)],
        scratch_shapes=[pltpu.VMEM((t + CONV_HALO, CONV_WIDTH), F32),
                        pltpu.VMEM((SUBLANES - 1, t + CONV_HALO - SUBLANES, CONV_WIDTH), F32),
                        pltpu.VMEM((t, GM_WIDTH), F32)],
        compiler_params=_params(("arbitrary",)),
    )(z, cw, cb, lng, lnb, gg, gb, wpair, bias)


def _out_proj_q(x, mix, w_out, g, wq, ts, after=()):
    s = x.shape[0]

    def body(x_ref, mix_ref, wo_ref, g_ref, wq_ref, h1_ref, hn_ref, q_ref):
        h1 = x_ref[...] + _dot(mix_ref[...], wo_ref[...])
        h1_ref[...] = h1
        hn = (h1 * _rms_stats(h1) * g_ref[...]).astype(BF16)
        hn_ref[...] = hn
        q_ref[...] = _dot(hn, wq_ref[...]).astype(BF16)

    row = lambda dt: pl.BlockSpec((ts, D_MODEL), lambda i: (i, 0))
    full = pl.BlockSpec((D_MODEL, D_MODEL), lambda i: (0, 0))
    return _tied_call(
        body, after, name="out_proj_q", grid=(s // ts,),
        in_specs=[row(F32), row(BF16), full, pl.BlockSpec((1, D_MODEL), lambda i: (0, 0)), full],
        out_specs=[row(F32), row(BF16), row(BF16)],
        out_shape=[jax.ShapeDtypeStruct((s, D_MODEL), F32), jax.ShapeDtypeStruct((s, D_MODEL), BF16),
                   jax.ShapeDtypeStruct((s, D_MODEL), BF16)],
        compiler_params=_params(("parallel",)),
    )(x, mix, w_out, g, wq)


def _mem_kv(mem, g, wkv):
    m = mem.shape[0]

    def body(mem_ref, g_ref, w_ref, mn_ref, kv_ref):
        mv = mem_ref[...]
        mn = (mv * _rms_stats(mv) * g_ref[...]).astype(BF16)
        mn_ref[...] = mn
        for j in range(4):
            kv_ref[:, j * 512:(j + 1) * 512] = _dot(mn, w_ref[j]).astype(BF16)

    return pl.pallas_call(
        body, name="mem_kv",
        out_shape=[jax.ShapeDtypeStruct((m, D_MODEL), BF16), jax.ShapeDtypeStruct((m, 2 * D_MODEL), BF16)],
        compiler_params=pltpu.CompilerParams(vmem_limit_bytes=VMEM_LIMIT_BYTES),
    )(mem, g, wkv)


def _softmax_rows(sc):
    e = jnp.exp(sc - jnp.max(sc, axis=-1, keepdims=True))
    return e / jnp.sum(e, axis=-1, keepdims=True)


def _attn_fwd(q, kv, h1, wo, g, ts):
    s, m = q.shape[0], kv.shape[0]
    scale = XA_HEAD_DIM ** -0.5

    def body(q_ref, kv_ref, h1_ref, wo_ref, g_ref, o_ref, h2_ref, hn_ref):
        for h in range(XA_HEADS):
            cols = slice(h * XA_HEAD_DIM, (h + 1) * XA_HEAD_DIM)
            vcols = slice(D_MODEL + h * XA_HEAD_DIM, D_MODEL + (h + 1) * XA_HEAD_DIM)
            p = _softmax_rows(_dot_nt(q_ref[:, cols], kv_ref[:, cols]) * scale)
            o_ref[:, cols] = _dot(p.astype(BF16), kv_ref[:, vcols]).astype(BF16)
        h2 = h1_ref[...] + _dot(o_ref[...], wo_ref[...])
        h2_ref[...] = h2
        hn_ref[...] = (h2 * _rms_stats(h2) * g_ref[...]).astype(BF16)

    row = pl.BlockSpec((ts, D_MODEL), lambda i: (i, 0))
    return pl.pallas_call(
        body, name="attn_fwd", grid=(s // ts,),
        in_specs=[row, pl.BlockSpec((m, 2 * D_MODEL), lambda i: (0, 0)), row,
                  pl.BlockSpec((D_MODEL, D_MODEL), lambda i: (0, 0)),
                  pl.BlockSpec((1, D_MODEL), lambda i: (0, 0))],
        out_specs=[row, row, row],
        out_shape=[jax.ShapeDtypeStruct((s, D_MODEL), BF16), jax.ShapeDtypeStruct((s, D_MODEL), F32),
                   jax.ShapeDtypeStruct((s, D_MODEL), BF16)],
        compiler_params=_params(("parallel",)),
    )(q, kv, h1, wo, g)


_FFN_CHUNKS_FWD = (slice(0, 6 * LANES), slice(6 * LANES, FFN_HALF))
_FFN_CHUNKS_BWD = (slice(0, 4 * LANES), slice(4 * LANES, 8 * LANES), slice(8 * LANES, FFN_HALF))


def _ffn_up(hn, wgu, ts, after=()):
    s = hn.shape[0]

    def body(hn_ref, w_ref, gu_ref, act_ref):
        hv = hn_ref[...]
        for cols in _FFN_CHUNKS_FWD:
            gate = _dot(hv, w_ref[0, 0, :, cols])
            up = _dot(hv, w_ref[1, 0, :, cols])
            gu_ref[0, :, cols] = gate.astype(BF16)
            gu_ref[1, :, cols] = up.astype(BF16)
            act_ref[:, cols] = (gate * _sigmoid(gate) * up).astype(BF16)

    return _tied_call(
        body, after, name="ffn_up", grid=(2, s // ts),
        in_specs=[pl.BlockSpec((ts, D_MODEL), lambda j, i: (i, 0)),
                  pl.BlockSpec((2, 1, D_MODEL, FFN_HALF), lambda j, i: (0, j, 0, 0))],
        out_specs=[pl.BlockSpec((2, ts, FFN_HALF), lambda j, i: (0, i, j)),
                   pl.BlockSpec((ts, FFN_HALF), lambda j, i: (i, j))],
        out_shape=[jax.ShapeDtypeStruct((2, s, FFN_HIDDEN), BF16), jax.ShapeDtypeStruct((s, FFN_HIDDEN), BF16)],
        compiler_params=_params(("parallel", "parallel")),
    )(hn, wgu)


def _ffn_down_loss(act, wd, h2, g, target, ts):
    s = act.shape[0]

    def body(act_ref, wd_ref, h2_ref, g_ref, t_ref, dh_ref, dhb_ref, sq_ref, dg_ref):
        @pl.when(pl.program_id(0) == 0)
        def _():
            sq_ref[...] = jnp.zeros_like(sq_ref)
            dg_ref[...] = jnp.zeros_like(dg_ref)

        h3 = h2_ref[...] + _dot(act_ref[...], wd_ref[...])
        r = _rms_stats(h3)
        gv = g_ref[...]
        diff = h3 * r * gv - t_ref[...]
        sq_ref[...] += _rowsum(diff * diff)
        dh, dg = _rms_bwd(diff / D_MODEL, h3, r, gv)
        dh_ref[...] = dh
        dhb_ref[...] = dh.astype(BF16)
        dg_ref[...] += dg

    row = pl.BlockSpec((ts, D_MODEL), lambda i: (i, 0))
    vec = pl.BlockSpec((1, D_MODEL), lambda i: (0, 0))
    return pl.pallas_call(
        body, name="ffn_down_loss", grid=(s // ts,),
        in_specs=[pl.BlockSpec((ts, FFN_HIDDEN), lambda i: (i, 0)),
                  pl.BlockSpec((FFN_HIDDEN, D_MODEL), lambda i: (0, 0)), row, vec, row],
        out_specs=[row, row, vec, vec],
        out_shape=[jax.ShapeDtypeStruct((s, D_MODEL), F32), jax.ShapeDtypeStruct((s, D_MODEL), BF16),
                   jax.ShapeDtypeStruct((1, D_MODEL), F32), jax.ShapeDtypeStruct((1, D_MODEL), F32)],
        compiler_params=_params(("arbitrary",)),
    )(act, wd, h2, g, target)


def _grad_w(a, b, tk, tn, name, after=()):
    s, k = a.shape
    gb, _, n = b.shape
    nblk = n // tn
    tsr = GRAD_ROWS if s % GRAD_ROWS == 0 else s

    def body(a_ref, b_ref, o_ref):
        part = _dot_tn(a_ref[...], b_ref[0])

        @pl.when(pl.program_id(2) == 0)
        def _():
            o_ref[0] = part

        @pl.when(pl.program_id(2) > 0)
        def _():
            o_ref[0] += part

    return _tied_call(
        body, after, name=name, grid=(gb * nblk, k // tk, s // tsr),
        in_specs=[pl.BlockSpec((tsr, tk), lambda ni, ki, si: (si, ki)),
                  pl.BlockSpec((1, tsr, tn), lambda ni, ki, si: (ni // nblk, si, ni % nblk))],
        out_specs=pl.BlockSpec((1, tk, tn), lambda ni, ki, si: (ni, ki, 0)),
        out_shape=jax.ShapeDtypeStruct((gb * nblk, k, tn), F32),
        compiler_params=_params(("parallel", "parallel", "arbitrary")),
    )(a, b)


def _ffn_bwd(dh3, wd, gu, wgu, h2, g, t, after=()):
    s = dh3.shape[0]

    def body(dh3_ref, wd_ref, gu_ref, w_ref, h2_ref, g_ref, dgu_ref, dh2_ref, dh2b_ref, dg_ref):
        @pl.when(pl.program_id(0) == 0)
        def _():
            dg_ref[...] = jnp.zeros_like(dg_ref)

        dh3v = dh3_ref[...]
        dhb = dh3v.astype(BF16)
        dhn = jnp.zeros((t, D_MODEL), F32)
        for j in range(2):
            for cols in _FFN_CHUNKS_BWD:
                whole = slice(j * FFN_HALF + cols.start, j * FFN_HALF + cols.stop)
                dact = _dot_nt(dhb, wd_ref[j, cols, :])
                gate, up = gu_ref[0, :, whole].astype(F32), gu_ref[1, :, whole].astype(F32)
                sg = _sigmoid(gate)
                dgate = (dact * up * (sg * (1.0 + gate * (1.0 - sg)))).astype(BF16)
                dup = (dact * (gate * sg)).astype(BF16)
                dgu_ref[0, :, whole] = dgate
                dgu_ref[1, :, whole] = dup
                dhn = dhn + _dot_nt(dgate, w_ref[j, :, cols]) + _dot_nt(dup, w_ref[2 + j, :, cols])
        h2 = h2_ref[...]
        dv, dg = _rms_bwd(dhn, h2, _rms_stats(h2), g_ref[...])
        dh2 = dh3v + dv
        dh2_ref[...] = dh2
        dh2b_ref[...] = dh2.astype(BF16)
        dg_ref[...] += dg

    row = pl.BlockSpec((t, D_MODEL), lambda i: (i, 0))
    wide = pl.BlockSpec((2, t, FFN_HIDDEN), lambda i: (0, i, 0))
    vec = pl.BlockSpec((1, D_MODEL), lambda i: (0, 0))
    return _tied_call(
        body, after, name="ffn_bwd", grid=(s // t,),
        in_specs=[row, pl.BlockSpec((2, FFN_HALF, D_MODEL), lambda i: (0, 0, 0)), wide,
                  pl.BlockSpec((4, D_MODEL, FFN_HALF), lambda i: (0, 0, 0)), row, vec],
        out_specs=[wide, row, row, vec],
        out_shape=[jax.ShapeDtypeStruct((2, s, FFN_HIDDEN), BF16), jax.ShapeDtypeStruct((s, D_MODEL), F32),
                   jax.ShapeDtypeStruct((s, D_MODEL), BF16), jax.ShapeDtypeStruct((1, D_MODEL), F32)],
        compiler_params=_params(("arbitrary",)),
    )(dh3, wd, gu, wgu, h2, g)


def _attn_bwd(dh2, wo, q, kv, wq, h1, g, ts, after=()):
    s, m = q.shape[0], kv.shape[0]
    scale = XA_HEAD_DIM ** -0.5

    def body(dh2_ref, wo_ref, q_ref, kv_ref, wq_ref, h1_ref, g_ref, dh1_ref, dh1b_ref, dq_ref, dkv_ref, dg_ref):
        @pl.when(pl.program_id(0) == 0)
        def _():
            dkv_ref[...] = jnp.zeros_like(dkv_ref)
            dg_ref[...] = jnp.zeros_like(dg_ref)

        do = _dot_nt(dh2_ref[...].astype(BF16), wo_ref[...]).astype(BF16)
        for h in range(XA_HEADS):
            cols = slice(h * XA_HEAD_DIM, (h + 1) * XA_HEAD_DIM)
            vcols = slice(D_MODEL + h * XA_HEAD_DIM, D_MODEL + (h + 1) * XA_HEAD_DIM)
            qh, kh, vh, doh = q_ref[:, cols], kv_ref[:, cols], kv_ref[:, vcols], do[:, cols]
            p = _softmax_rows(_dot_nt(qh, kh) * scale)
            dp = _dot_nt(doh, vh)
            ds = (p * (dp - jnp.sum(dp * p, axis=-1, keepdims=True)) * scale).astype(BF16)
            dq_ref[:, cols] = _dot(ds, kh).astype(BF16)
            dkv_ref[:, cols] += _dot_tn(ds, qh)
            dkv_ref[:, vcols] += _dot_tn(p.astype(BF16), doh)
        dhn = _dot_nt(dq_ref[...], wq_ref[...])
        h1 = h1_ref[...]
        dv, dg = _rms_bwd(dhn, h1, _rms_stats(h1), g_ref[...])
        dh1 = dh2_ref[...] + dv
        dh1_ref[...] = dh1
        dh1b_ref[...] = dh1.astype(BF16)
        dg_ref[...] += dg

    row = pl.BlockSpec((ts, D_MODEL), lambda i: (i, 0))
    full = pl.BlockSpec((D_MODEL, D_MODEL), lambda i: (0, 0))
    kvs = pl.BlockSpec((m, 2 * D_MODEL), lambda i: (0, 0))
    vec = pl.BlockSpec((1, D_MODEL), lambda i: (0, 0))
    return _tied_call(
        body, after, name="attn_bwd", grid=(s // ts,),
        in_specs=[row, full, row, kvs, full, row, vec],
        out_specs=[row, row, row, kvs, vec],
        out_shape=[jax.ShapeDtypeStruct((s, D_MODEL), F32), jax.ShapeDtypeStruct((s, D_MODEL), BF16),
                   jax.ShapeDtypeStruct((s, D_MODEL), BF16),
                   jax.ShapeDtypeStruct((m, 2 * D_MODEL), F32), jax.ShapeDtypeStruct((1, D_MODEL), F32)],
        compiler_params=_params(("arbitrary",)),
    )(dh2, wo, q, kv, wq, h1, g)


def _mem_kv_bwd(dkv, mn, wkv, mem, g, after=()):
    m = mem.shape[0]

    def body(dkv_ref, mn_ref, w_ref, mem_ref, g_ref, dw_ref, dg_ref):
        dmn = jnp.zeros((m, D_MODEL), F32)
        mn = mn_ref[...]
        for j in range(4):
            dj = dkv_ref[:, j * 512:(j + 1) * 512].astype(BF16)
            dw_ref[j] = _dot_tn(mn, dj)
            dmn = dmn + _dot_nt(dj, w_ref[j])
        mv = mem_ref[...]
        dg_ref[...] = _rowsum(dmn * (mv * _rms_stats(mv)))

    return _tied_call(
        body, after, name="mem_kv_bwd", in_specs=[pl.BlockSpec(memory_space=pltpu.VMEM)] * 5,
        out_shape=[jax.ShapeDtypeStruct((4, D_MODEL, 512), F32), jax.ShapeDtypeStruct((1, D_MODEL), F32)],
        compiler_params=pltpu.CompilerParams(vmem_limit_bytes=VMEM_LIMIT_BYTES),
    )(dkv, mn, wkv, mem, g)


def _seqmix_bwd(dh1, x, z, c1, w_out, w_in, g_mix, cw, lng, lnb, gg, gb, wpair, wpair_t, bias, t, after=()):
    s = x.shape[0]
    nt = s // t

    def body(dh1_ref, x_ref, z_ref, c1_ref, wo_ref, wi_ref, gm_ref, cw_ref, lng_ref, lnb_ref,
             gg_ref, gb_ref, wpair_ref, wpt_ref, bias_ref,
             gx_ref, dz_ref, dcw_ref, dcb_ref, dlng_ref, dlnb_ref, dgg_ref, dgb_ref, dws_ref, dbs_ref,
             dbin_ref, dgm_ref, dbuf, dsh, mixed_ref, dv_ref):
        i = pl.program_id(0)
        accs = (dcw_ref, dcb_ref, dlng_ref, dlnb_ref, dgg_ref, dgb_ref, dws_ref, dbs_ref, dbin_ref, dgm_ref)

        @pl.when(i == 0)
        def _():
            for r in accs:
                r[...] = jnp.zeros_like(r)
            dbuf[t:t + CONV_HALO, :] = jnp.zeros((CONV_HALO, CONV_WIDTH), F32)

        @pl.when(i > 0)
        def _():
            dbuf[t:t + CONV_HALO, :] = dbuf[0:CONV_HALO, :]

        dmix = _dot_nt(dh1_ref[...].astype(BF16), wo_ref[...])

        xh, rs = _ln_stats(c1_ref[...])
        lng = lng_ref[...]
        ln = xh * lng + lnb_ref[...]
        sl = _sigmoid(ln)
        dln = dmix[:, 0:512] * (sl * (1.0 + ln * (1.0 - sl)))
        dc1, dg_ln, db_ln = _ln_bwd(dln, xh, rs, lng)
        dlng_ref[...] += dg_ln
        dlnb_ref[...] += db_ln
        dcb_ref[...] += _rowsum(dc1)
        dbuf[0:t, :] = dc1

        za = z_ref[:, 0:512]
        sg = _sigmoid(z_ref[:, 512:1024])
        a = za * sg
        _shift_rows(dbuf, dsh, t)

        da = jnp.zeros((t, CONV_WIDTH), F32)
        for k in range(CONV_KERNEL):
            later = _window(dbuf, dsh, CONV_KERNEL - 1 - k, t)
            da = da + cw_ref[k:k + 1, :] * later
            dcw_ref[k:k + 1, :] += _rowsum(a * later)
        dza = da * sg
        dzg = da * za * (sg * (1.0 - sg))
        dz_ref[:, 0:512] = dza.astype(BF16)
        dz_ref[:, 512:1024] = dzg.astype(BF16)
        dbin_ref[:, 0:512] += _rowsum(dza)
        dbin_ref[:, 512:1024] += _rowsum(dzg)

        dgm = dmix[:, 512:1024]
        u, du_dz = _gelu_parts(z_ref[:, 1024:1536])
        gv, dgv_dz = _gelu_parts(z_ref[:, 1536:2048])
        vxh, vrs = _ln_stats(gv)
        ggv = gg_ref[...]
        v = vxh * ggv + gb_ref[...]
        low = _lane_is_low_head()
        v_lo = jnp.where(low, v, 0.0).astype(BF16)
        v_hi = jnp.where(low, 0.0, v).astype(BF16)
        _gm_mix(v_lo, v_hi, wpair_ref, bias_ref, mixed_ref, t)
        dzu = dgm * mixed_ref[...] * du_dz
        dm = dgm * u
        dm_lo = jnp.where(low, dm, 0.0).astype(BF16)
        dm_hi = jnp.where(low, 0.0, dm).astype(BF16)
        vb = v.astype(BF16)
        tril = (lax.broadcasted_iota(jnp.int32, (CHUNK, CHUNK), 1)
                <= lax.broadcasted_iota(jnp.int32, (CHUNK, CHUNK), 0))
        for n in range(t // CHUNK):
            rows = slice(n * CHUNK, (n + 1) * CHUNK)
            dbs_ref[...] += dm[rows, :]
            for j in range(GM_HEADS // 2):
                cols = slice(j * LANES, (j + 1) * LANES)
                stack = jnp.concatenate([dm_lo[rows, cols], dm_hi[rows, cols]], axis=0)
                dws = _dot_nt(stack, vb[rows, cols])
                dws_ref[2 * j] += jnp.where(tril, dws[0:CHUNK], 0.0)
                dws_ref[2 * j + 1] += jnp.where(tril, dws[CHUNK:2 * CHUNK], 0.0)
                dv_ref[rows, cols] = _dot(wpt_ref[j], stack)
        dgv, dg_gm, db_gm = _ln_bwd(dv_ref[...], vxh, vrs, ggv)
        dgg_ref[...] += dg_gm
        dgb_ref[...] += db_gm
        dzv = dgv * dgv_dz
        dz_ref[:, 1024:1536] = dzu.astype(BF16)
        dz_ref[:, 1536:2048] = dzv.astype(BF16)
        dbin_ref[:, 1024:1536] += _rowsum(dzu)
        dbin_ref[:, 1536:2048] += _rowsum(dzv)

        dhn = jnp.zeros((t, D_MODEL), F32)
        for j in range(4):
            dhn = dhn + _dot_nt(dz_ref[:, j * 512:(j + 1) * 512], wi_ref[j])
        xv = x_ref[...]
        dv, dg = _rms_bwd(dhn, xv, _rms_stats(xv), gm_ref[...])
        gx_ref[...] = dh1_ref[...] + dv
        dgm_ref[...] += dg

    rev = lambda w: pl.BlockSpec((t, w), lambda i: (nt - 1 - i, 0))
    const = lambda *shape: pl.BlockSpec(shape, lambda i: (0,) * len(shape))
    f32 = lambda *shape: jax.ShapeDtypeStruct(shape, F32)
    return _tied_call(
        body, after, name="seqmix_bwd", grid=(nt,),
        in_specs=[rev(D_MODEL), rev(D_MODEL), rev(2048), rev(CONV_WIDTH),
                  const(D_MODEL, D_MODEL), const(4, D_MODEL, 512), const(1, D_MODEL),
                  const(CONV_HALO, CONV_WIDTH), const(1, 512), const(1, 512), const(1, 512), const(1, 512),
                  const(4, CHUNK, 2 * CHUNK), const(4, CHUNK, 2 * CHUNK), const(CHUNK, GM_WIDTH)],
        out_specs=[rev(D_MODEL), rev(2048),
                   const(CONV_HALO, CONV_WIDTH), const(1, 512), const(1, 512), const(1, 512), const(1, 512),
                   const(1, 512), const(GM_HEADS, CHUNK, CHUNK), const(CHUNK, GM_WIDTH), const(1, 2048),
                   const(1, D_MODEL)],
        out_shape=[f32(s, D_MODEL), jax.ShapeDtypeStruct((s, 2048), BF16),
                   f32(CONV_HALO, CONV_WIDTH), f32(1, 512), f32(1, 512), f32(1, 512), f32(1, 512),
                   f32(1, 512), f32(GM_HEADS, CHUNK, CHUNK), f32(CHUNK, GM_WIDTH), f32(1, 2048),
                   f32(1, D_MODEL)],
        scratch_shapes=[pltpu.VMEM((t + CONV_HALO, CONV_WIDTH), F32),
                        pltpu.VMEM((SUBLANES - 1, t + CONV_HALO - SUBLANES, CONV_WIDTH), F32),
                        pltpu.VMEM((t, GM_WIDTH), F32), pltpu.VMEM((t, GM_WIDTH), F32)],
        compiler_params=_params(("arbitrary",)),
    )(dh1, x, z, c1, w_out, w_in, g_mix, cw, lng, lnb, gg, gb, wpair, wpair_t, bias)


def _head_bias_grad(dbs):
    def body(d_ref, o_ref):
        dv = d_ref[...]
        lane = lax.broadcasted_iota(jnp.int32, (CHUNK, LANES), 1)
        acc = jnp.zeros((CHUNK, LANES), F32)
        for h in range(GM_HEADS):
            sh = jnp.sum(dv[:, h * GM_HEAD_DIM:(h + 1) * GM_HEAD_DIM], axis=-1, keepdims=True)
            acc = acc + jnp.where(lane == h, sh, 0.0)
        o_ref[...] = acc

    return pl.pallas_call(body, name="head_bias_grad",
                          out_shape=jax.ShapeDtypeStruct((CHUNK, LANES), F32))(dbs)


def kernel(x, mem, norm_mix_g, w_in, b_in, conv_w, conv_b, conv_ln_g, conv_ln_b, gm_ln_g, gm_ln_b, gm_w_s, gm_b_s, w_out, norm_xa_g, mem_norm_g, xa_wq, xa_wkv, xa_wo, norm_ffn_g, ffn_w_gate_up, ffn_w_down, final_norm_g, loss_target, m_norm_mix_g, m_w_in, m_b_in, m_conv_w, m_conv_b, m_conv_ln_g, m_conv_ln_b, m_gm_ln_g, m_gm_ln_b, m_gm_w_s, m_gm_b_s, m_w_out, m_norm_xa_g, m_mem_norm_g, m_xa_wq, m_xa_wkv, m_xa_wo, m_norm_ffn_g, m_ffn_w_gate_up, m_ffn_w_down, m_final_norm_g, v_norm_mix_g, v_w_in, v_b_in, v_conv_w, v_conv_b, v_conv_ln_g, v_conv_ln_b, v_gm_ln_g, v_gm_ln_b, v_gm_w_s, v_gm_b_s, v_w_out, v_norm_xa_g, v_mem_norm_g, v_xa_wq, v_xa_wkv, v_xa_wo, v_norm_ffn_g, v_ffn_w_gate_up, v_ffn_w_down, v_final_norm_g):
    weights = dict(norm_mix_g=norm_mix_g, w_in=w_in, b_in=b_in, conv_w=conv_w, conv_b=conv_b, conv_ln_g=conv_ln_g,
                   conv_ln_b=conv_ln_b, gm_ln_g=gm_ln_g, gm_ln_b=gm_ln_b, gm_w_s=gm_w_s, gm_b_s=gm_b_s, w_out=w_out,
                   norm_xa_g=norm_xa_g, mem_norm_g=mem_norm_g, xa_wq=xa_wq, xa_wkv=xa_wkv, xa_wo=xa_wo,
                   norm_ffn_g=norm_ffn_g, ffn_w_gate_up=ffn_w_gate_up, ffn_w_down=ffn_w_down,
                   final_norm_g=final_norm_g)
    m_in = dict(norm_mix_g=m_norm_mix_g, w_in=m_w_in, b_in=m_b_in, conv_w=m_conv_w, conv_b=m_conv_b,
                conv_ln_g=m_conv_ln_g, conv_ln_b=m_conv_ln_b, gm_ln_g=m_gm_ln_g, gm_ln_b=m_gm_ln_b, gm_w_s=m_gm_w_s,
                gm_b_s=m_gm_b_s, w_out=m_w_out, norm_xa_g=m_norm_xa_g, mem_norm_g=m_mem_norm_g, xa_wq=m_xa_wq,
                xa_wkv=m_xa_wkv, xa_wo=m_xa_wo, norm_ffn_g=m_norm_ffn_g, ffn_w_gate_up=m_ffn_w_gate_up,
                ffn_w_down=m_ffn_w_down, final_norm_g=m_final_norm_g)
    v_in = dict(norm_mix_g=v_norm_mix_g, w_in=v_w_in, b_in=v_b_in, conv_w=v_conv_w, conv_b=v_conv_b,
                conv_ln_g=v_conv_ln_g, conv_ln_b=v_conv_ln_b, gm_ln_g=v_gm_ln_g, gm_ln_b=v_gm_ln_b, gm_w_s=v_gm_w_s,
                gm_b_s=v_gm_b_s, w_out=v_w_out, norm_xa_g=v_norm_xa_g, mem_norm_g=v_mem_norm_g, xa_wq=v_xa_wq,
                xa_wkv=v_xa_wkv, xa_wo=v_xa_wo, norm_ffn_g=v_norm_ffn_g, ffn_w_gate_up=v_ffn_w_gate_up,
                ffn_w_down=v_ffn_w_down, final_norm_g=v_final_norm_g)
    grads, delta, new_m, new_v = {}, {}, {}, {}

    s = x.shape[1]
    ts = _row_tile(s)
    tb = max(CHUNK, ts // 2)
    tw = 2 * ts if s % (2 * ts) == 0 and ts >= 512 else ts
    cx, cy, cc = _mesh_pos()
    chip = 2 * cx + cy
    pos = jnp.stack([chip, cc]).astype(jnp.int32)
    row = lambda a: a.reshape(1, -1)
    x2, mem2, tgt2 = x[0], mem[0], loss_target[0]

    big = dict(w_in=w_in, xa_wkv=xa_wkv, w_out=w_out, xa_wq=xa_wq, xa_wo=xa_wo,
               ffn_w_gate_up=ffn_w_gate_up, ffn_w_down=ffn_w_down)
    big_names = list(big)
    halves = lambda a: a.reshape(2, a.shape[0] // 2, a.shape[1])
    conv_w_pad = jnp.pad(conv_w, ((0, CONV_HALO - CONV_KERNEL), (0, 0)))
    first_names = ["w_in", "conv_w"]
    later_names = [nm for nm in big_names if nm != "w_in"]
    cast = dict(zip(first_names, _cast_into_slots([halves(w_in), halves(conv_w_pad)], pos, [BF16, F32], "cast_w_in")))
    cast.update(zip(later_names, _cast_into_slots([halves(big[nm]) for nm in later_names], pos,
                                                  [BF16] * len(later_names), "cast_" + later_names[0])))

    def start_gather(names, after):
        return _gather_start([cast[nm] for nm in names], "gather_start_" + names[0], after)

    def finish_gather(names, started, after):
        send_sems, recv_sems, bufs, _ = started
        landed = _gather_wait(send_sems, recv_sems, bufs, after, "gather_wait_" + names[0])
        return dict(zip(names, (b.reshape(N_CHIPS, -1, b.shape[-1])
                                for b in _pass_to_sibling(landed, "pass_" + names[0]))))

    attn_names = ["w_out", "xa_wq", "xa_wkv", "xa_wo"]
    gather_first = start_gather(first_names, ())
    hn1 = _norm_in(x2, row(norm_mix_g), tw, after=gather_first[3])
    gw = finish_gather(first_names, gather_first, [cast[nm] for nm in later_names] + [hn1])
    w_in_g = gw["w_in"]
    cw_g = jnp.concatenate([gw["conv_w"][k] for k in range(N_CHIPS)], axis=1)
    gather_attn = start_gather(attn_names, w_in_g)

    tril = jnp.tril(jnp.ones((CHUNK, CHUNK), dtype=bool))
    ws = jnp.where(tril[None], gm_w_s, 0.0)
    wpair = jnp.concatenate([ws[0::2], ws[1::2]], axis=2).astype(BF16)
    ws_t = jnp.swapaxes(ws, 1, 2)
    wpair_t = jnp.concatenate([ws_t[0::2], ws_t[1::2]], axis=2).astype(BF16)
    bias = jnp.repeat(gm_b_s.T, GM_HEAD_DIM, axis=1)

    z = _mix_in(hn1, w_in_g, row(b_in), tw, after=gather_attn[3])
    mix, c1 = _seqmix_fwd(z, cw_g, row(conv_b), row(conv_ln_g), row(conv_ln_b), row(gm_ln_g), row(gm_ln_b),
                          wpair, bias, ts)
    gw = finish_gather(attn_names, gather_attn, mix)
    w_out_g = gw["w_out"].reshape(D_MODEL, D_MODEL)
    wq_g = gw["xa_wq"].reshape(D_MODEL, D_MODEL)
    wkv_g = gw["xa_wkv"]
    wo_g = gw["xa_wo"].reshape(D_MODEL, D_MODEL)
    gather_gu = start_gather(["ffn_w_gate_up"], w_out_g)
    h1, hn2, q = _out_proj_q(x2, mix, w_out_g, row(norm_xa_g), wq_g, tw, after=gather_gu[3])
    mn, kv = _mem_kv(mem2, row(mem_norm_g), wkv_g)
    o, h2, hn3 = _attn_fwd(q, kv, h1, wo_g, row(norm_ffn_g), ts)
    wgu_g = finish_gather(["ffn_w_gate_up"], gather_gu, hn3)["ffn_w_gate_up"]
    gather_down = start_gather(["ffn_w_down"], wgu_g)
    gu, act = _ffn_up(hn3, wgu_g.reshape(2, 2, D_MODEL, FFN_HALF), tw, after=gather_down[3])
    wd_g = finish_gather(["ffn_w_down"], gather_down, act)["ffn_w_down"].reshape(FFN_HIDDEN, D_MODEL)
    dh3, dh3_b, sq, d_final_g = _ffn_down_loss(act, wd_g, h2, row(final_norm_g), tgt2, ts)
    loss_here = jnp.broadcast_to(0.5 * jnp.sum(sq) / D_MODEL, (1, 2, SUBLANES, LANES))

    def split(g, nm):
        r, c = big[nm].shape
        return g.reshape(N_CHIPS, 2, r // 2, c)

    def chip_sums(group, arrays, got):
        sums, parts = [None] * len(group), [None] * len(group)
        for blocks in (N_CHIPS, 1):
            idx = [i for i, a in enumerate(arrays) if a.shape[0] == blocks]
            if idx:
                out = _add_halves([arrays[i] for i in idx], [got[i] for i in idx], pos, "chip_sum_" + group[idx[0]],
                                  [F32 if group[i] == "loss" else BF16 for i in idx])
                for k, i in enumerate(idx):
                    sums[i], parts[i] = out[0][k], out[1][k]
        return sums, parts

    def start_swap(group, grads):
        return _swap_start([split(g, nm) for g, nm in zip(grads, group)], "swap_start_" + group[0])

    def start_exchange(group, swapping, after):
        sems, arrays, lands, _ = swapping
        arrays, got = _swap_wait(sems, arrays, lands, after, "swap_wait_" + group[0])
        sums, parts = chip_sums(group, arrays, got)
        return _exchange_start(sums, parts, "exchange_start_" + group[0])

    def finish_exchange(group, started, after):
        sems, sums, parts, _ = started
        parts = _exchange_wait(sems, sums, parts, after, "exchange_wait_" + group[0])
        return _sum_chips(parts, pos, "total_" + group[0])

    def join_and_update(group, after):
        joined = _join_halves([halves_of[nm] for nm in group], "join_halves_" + group[0], after)
        outs = _adamw([(weights[nm], j.reshape(big[nm].shape), m_in[nm], v_in[nm]) for nm, j in zip(group, joined)],
                      "adamw_" + group[0])
        for nm, out in zip(group, outs):
            grads[nm], delta[nm], new_m[nm], new_v[nm] = out
        return [new_v[nm] for nm in group]

    as3 = lambda a: a.reshape((1,) + a.shape)
    halves_of = {}

    g_down = _grad_w(act, as3(dh3_b), FFN_HALF, D_MODEL, "grad_ffn_w_down")
    group_a = ["ffn_w_down"]
    swap_a = start_swap(group_a, [g_down])
    dgu, dh2, dh2_b, d_ffn_g = _ffn_bwd(dh3, wd_g.reshape(2, FFN_HALF, D_MODEL), gu, wgu_g, h2, row(norm_ffn_g), tb,
                                        after=swap_a[3])
    exch_a = start_exchange(group_a, swap_a, dh2)
    g_gu = _grad_w(hn3, dgu, D_MODEL, FFN_HALF, "grad_ffn_w_gate_up", after=exch_a[3])
    halves_of.update(zip(group_a, finish_exchange(group_a, exch_a, g_gu)))

    group_b = ["ffn_w_gate_up"]
    swap_b = start_swap(group_b, [g_gu])
    dh1, dh1_b, dq, dkv, d_xa_g = _attn_bwd(dh2, wo_g, q, kv, wq_g, h1, row(norm_xa_g), ts, after=swap_b[3])
    exch_b = start_exchange(group_b, swap_b, dh1)
    g_wkv, d_mem_g = _mem_kv_bwd(dkv, mn, wkv_g, mem2, row(mem_norm_g), after=exch_b[3])
    g_wo = _grad_w(o, as3(dh2_b), D_MODEL, D_MODEL, "grad_xa_wo", after=exch_b[3])
    g_wq = _grad_w(hn2, as3(dq), D_MODEL, D_MODEL, "grad_xa_wq", after=exch_b[3])
    g_wout = _grad_w(mix, as3(dh1_b), D_MODEL, D_MODEL, "grad_w_out", after=exch_b[3])
    halves_of.update(zip(group_b, finish_exchange(group_b, exch_b, (g_wkv, g_wo, g_wq, g_wout))))

    group_c = ["xa_wo", "xa_wq", "xa_wkv", "w_out"]
    swap_c = start_swap(group_c, [g_wo, g_wq, g_wkv, g_wout])
    (gx, dz, d_cw, d_cb, d_lng, d_lnb, d_gg, d_gb, d_ws, d_bs_sum, d_bin, d_mix_g) = _seqmix_bwd(
        dh1, x2, z, c1, w_out_g, w_in_g, row(norm_mix_g), cw_g, row(conv_ln_g), row(conv_ln_b),
        row(gm_ln_g), row(gm_ln_b), wpair, wpair_t, bias, tb, after=swap_c[3])
    d_bs = _head_bias_grad(d_bs_sum)[:, :GM_HEADS].T
    exch_c = start_exchange(group_c, swap_c, dz)
    g_win = _grad_w(hn1, as3(dz), D_MODEL, 512, "grad_w_in", after=exch_c[3])
    done_ab = join_and_update(group_a + group_b, g_win)
    halves_of.update(zip(group_c, finish_exchange(group_c, exch_c, (g_win, *done_ab))))

    small_names = ["norm_mix_g", "b_in", "conv_w", "conv_b", "conv_ln_g", "conv_ln_b", "gm_ln_g", "gm_ln_b",
                   "gm_w_s", "gm_b_s", "norm_xa_g", "mem_norm_g", "norm_ffn_g", "final_norm_g"]
    d_cw_by_chip = jnp.swapaxes(d_cw.reshape(CONV_HALO, N_CHIPS, LANES), 0, 1).reshape(-1, LANES)
    small_grads = dict(norm_mix_g=d_mix_g, b_in=d_bin, conv_w=d_cw_by_chip, conv_b=d_cb, conv_ln_g=d_lng,
                       conv_ln_b=d_lnb, gm_ln_g=d_gg, gm_ln_b=d_gb, gm_w_s=d_ws, gm_b_s=d_bs, norm_xa_g=d_xa_g,
                       mem_norm_g=d_mem_g, norm_ffn_g=d_ffn_g, final_norm_g=d_final_g)

    def rows_form(a):
        a = a.reshape(-1, LANES)
        return jnp.pad(a, ((0, -a.shape[0] % SUBLANES), (0, 0)))

    pieces = [rows_form(small_grads[nm]) for nm in small_names]
    offsets, total = [], 0
    for p in pieces:
        offsets.append(total)
        total += p.shape[0]
    pack_rows = -(-total // 32) * 32
    small_pack = jnp.pad(jnp.concatenate(pieces, axis=0), ((0, pack_rows - total), (0, 0)))

    group_d = ["w_in", "small", "loss"]
    arrays_d = [split(g_win, "w_in"), small_pack.reshape(1, 2, pack_rows // 2, LANES), loss_here]
    sums_d, parts_d = chip_sums(group_d, arrays_d, _swap_halves(arrays_d, "swap_halves_w_in"))
    exch_d = _exchange_start(sums_d, parts_d, "exchange_start_w_in")
    done_c = join_and_update(group_c, exch_d[3])
    halves_of.update(zip(group_d, finish_exchange(group_d, exch_d, done_c)))
    joined_d = _join_halves([halves_of[nm] for nm in group_d], "join_halves_w_in")
    loss = joined_d[2][0, 0, 0]
    grads["w_in"], delta["w_in"], new_m["w_in"], new_v["w_in"] = _adamw(
        [(w_in, joined_d[0].reshape(w_in.shape), m_w_in, v_w_in)], "adamw_w_in")[0]

    local_rows = lambda a, nm: a if nm == "conv_w" else a.reshape(-1, LANES)
    params = [tuple(local_rows(src[nm], nm) for src in (weights, m_in, v_in)) for nm in small_names]
    outs = _adamw_small(joined_d[1].reshape(pack_rows, LANES), pos, params, offsets, small_names.index("conv_w"))
    for k, nm in enumerate(small_names):
        for dst, a in zip((grads, delta, new_m, new_v), outs[4 * k:4 * k + 4]):
            dst[nm] = a

    order = ["norm_mix_g", "w_in", "b_in", "conv_w", "conv_b", "conv_ln_g", "conv_ln_b", "gm_ln_g", "gm_ln_b",
             "gm_w_s", "gm_b_s", "w_out", "norm_xa_g", "mem_norm_g", "xa_wq", "xa_wkv", "xa_wo", "norm_ffn_g",
             "ffn_w_gate_up", "ffn_w_down", "final_norm_g"]
    fit = lambda a, nm: a.reshape(weights[nm].shape)
    return (loss, gx.reshape(x.shape),
            *[fit(grads[nm], nm) for nm in order], *[fit(delta[nm], nm) for nm in order],
            *[fit(new_m[nm], nm) for nm in order], *[fit(new_v[nm], nm) for nm in order])
```

```python
import functools

import jax
import jax.numpy as jnp
from jax import lax
from jax.experimental import pallas as pl
from jax.experimental.pallas import tpu as pltpu

F32 = jnp.float32
BF16 = jnp.bfloat16

D_MODEL = 1024
CONV_WIDTH = 512
GM_WIDTH = 512
CONV_KERNEL = 31
CONV_HALO = 32
GRAD_ROWS = 2048
CHUNK = 128
GM_HEADS = 8
GM_HEAD_DIM = 64
XA_HEADS = 4
XA_HEAD_DIM = 256
FFN_HIDDEN = 2816
FFN_HALF = FFN_HIDDEN // 2
RMS_EPS = 1e-6
LN_EPS = 1e-5
N_CHIPS = 4
LANES = 128
SUBLANES = 8

ADAM_LR = 0.001
ADAM_B1 = 0.9
ADAM_B2 = 0.999
ADAM_EPS = 1e-08
ADAM_WD = 0.01
ADAM_STEP = 10

VMEM_LIMIT_BYTES = 56 * 1024 * 1024
MESH = pl.DeviceIdType.MESH
ANY = pl.BlockSpec(memory_space=pl.ANY)
HBM_SPEC = pl.BlockSpec(memory_space=pltpu.HBM)
SEM_SPEC = pl.BlockSpec(memory_space=pltpu.SEMAPHORE)

_NT = (((1,), (1,)), ((), ()))
_TN = (((0,), (0,)), ((), ()))
_GELU_C = 0.7978845608028654
_GELU_A = 0.044715


def _dot(a, b):
    return jnp.dot(a, b, preferred_element_type=F32)


def _dot_nt(a, b):
    return lax.dot_general(a, b, _NT, preferred_element_type=F32)


def _dot_tn(a, b):
    return lax.dot_general(a, b, _TN, preferred_element_type=F32)


def _mean(v):
    return jnp.mean(v, axis=-1, keepdims=True)


def _rowsum(v):
    return jnp.sum(v, axis=0, keepdims=True)


def _sigmoid(v):
    return 1.0 / (1.0 + jnp.exp(-v))


def _gelu_parts(v):
    v2 = v * v
    t = jnp.tanh(_GELU_C * (v + _GELU_A * v * v2))
    g = 0.5 * v * (1.0 + t)
    dg = 0.5 * (1.0 + t) + 0.5 * v * (1.0 - t * t) * (_GELU_C * (1.0 + 3.0 * _GELU_A * v2))
    return g, dg


def _rms_stats(v):
    return lax.rsqrt(_mean(v * v) + RMS_EPS)


def _rms_bwd(dy, v, r, g):
    n = v * r
    dn = dy * g
    dv = r * (dn - n * _mean(dn * n))
    return dv, _rowsum(dy * n)


def _ln_stats(v):
    mu = _mean(v)
    xc = v - mu
    rs = lax.rsqrt(_mean(xc * xc) + LN_EPS)
    return xc * rs, rs


def _ln_bwd(dy, xh, rs, g):
    dxh = dy * g
    dv = rs * (dxh - _mean(dxh) - xh * _mean(dxh * xh))
    return dv, _rowsum(dy * xh), _rowsum(dy)


def _params(sem):
    return pltpu.CompilerParams(dimension_semantics=sem, vmem_limit_bytes=VMEM_LIMIT_BYTES)


def _row_tile(s):
    return 512 if s % 512 == 0 and s >= 2048 else 128


def _mesh_pos():
    return lax.axis_index("x"), lax.axis_index("y"), lax.axis_index("c")


def _cast_into_slots(ws, pos, dtypes, name):
    n = len(ws)

    def body(pos_ref, *refs):
        for a in range(n):
            refs[n + a][0] = refs[a][...].astype(dtypes[a])

    return pl.pallas_call(
        body, name=name,
        grid_spec=pltpu.PrefetchScalarGridSpec(
            num_scalar_prefetch=1, grid=(2,),
            in_specs=[pl.BlockSpec((1,) + w.shape[1:], lambda i, p: (i, 0, 0)) for w in ws],
            out_specs=[pl.BlockSpec((1, 1) + w.shape[1:], lambda i, p: (p[0], i, 0, 0)) for w in ws]),
        out_shape=[jax.ShapeDtypeStruct((N_CHIPS,) + w.shape, dt) for w, dt in zip(ws, dtypes)],
        compiler_params=_params(("parallel",)),
    )(pos, *ws)


def _adam_update(w, g, m, v):
    nm = ADAM_B1 * m + (1.0 - ADAM_B1) * g
    nv = ADAM_B2 * v + (1.0 - ADAM_B2) * (g * g)
    m_hat = nm / (1.0 - ADAM_B1 ** ADAM_STEP)
    v_hat = nv / (1.0 - ADAM_B2 ** ADAM_STEP)
    return -ADAM_LR * (m_hat / (jnp.sqrt(v_hat) + ADAM_EPS) + ADAM_WD * w), nm, nv


ADAM_STEPS = 4


def _adamw(quads, name, after=()):
    n = len(quads)

    def body(*refs):
        ins, outs = refs[:4 * n], refs[4 * n:]
        for a in range(n):
            w, g, m, v = (r[...] for r in ins[4 * a:4 * a + 4])
            outs[4 * a][...] = g
            outs[4 * a + 1][...], outs[4 * a + 2][...], outs[4 * a + 3][...] = _adam_update(w, g, m, v)

    specs = [pl.BlockSpec((q[0].shape[0] // ADAM_STEPS, q[0].shape[1]), lambda i: (i, 0)) for q in quads]
    out = _tied_call(
        body, after, name=name, grid=(ADAM_STEPS,),
        in_specs=[sp for sp in specs for _ in range(4)], out_specs=[sp for sp in specs for _ in range(4)],
        out_shape=[jax.ShapeDtypeStruct(q[0].shape, F32) for q in quads for _ in range(4)],
        compiler_params=_params(("parallel",)),
    )(*[a for q in quads for a in q])
    return [tuple(out[4 * a:4 * a + 4]) for a in range(n)]


def _adamw_small(gpack, pos, params, offsets, conv_at):
    n = len(params)

    def body(pos_ref, g_ref, *refs):
        ins, outs = refs[:3 * n], refs[3 * n:]
        for k in range(n):
            rows = params[k][0].shape[0]
            start = offsets[k]
            if k == conv_at:
                start = pl.multiple_of(start + pos_ref[0] * CONV_HALO, SUBLANES)
            g = g_ref[pl.ds(start, rows), :]
            outs[4 * k][...] = g
            outs[4 * k + 1][...], outs[4 * k + 2][...], outs[4 * k + 3][...] = _adam_update(
                ins[3 * k][...], g, ins[3 * k + 1][...], ins[3 * k + 2][...])

    flat = [a for p in params for a in p]
    vmem = pl.BlockSpec(memory_space=pltpu.VMEM)
    return pl.pallas_call(
        body, name="adamw_small",
        in_specs=[pl.BlockSpec(memory_space=pltpu.SMEM), vmem] + [vmem] * len(flat),
        out_specs=[vmem] * (4 * n),
        out_shape=[jax.ShapeDtypeStruct(p[0].shape, F32) for p in params for _ in range(4)],
    )(pos, gpack, *flat)


def _as_tuple(after):
    return tuple(after) if isinstance(after, (tuple, list)) else (after,)


def _tied_call(body, after, *, in_specs, **kwargs):
    after = _as_tuple(after)
    n_in, n_after = len(in_specs), len(after)

    def tied(*refs):
        body(*refs[:n_in], *refs[n_in + n_after:])

    call = pl.pallas_call(tied, in_specs=list(in_specs) + [ANY] * n_after, **kwargs)
    return lambda *operands: call(*operands, *after)


def _other_chips(x, y):
    return [(1 - x, y), (x, 1 - y), (1 - x, 1 - y)]


def _gather_descriptors(bufs, send_of, recv_of):
    x, y, c = _mesh_pos()
    me = 2 * x + y
    chips = _other_chips(x, y)
    sends, arrivals = [], []
    for a in range(len(bufs)):
        for k in range(3):
            ck = 2 * chips[k][0] + chips[k][1]

            def copy(slot, a=a, k=k):
                return pltpu.make_async_remote_copy(
                    src_ref=bufs[a].at[slot, c], dst_ref=bufs[a].at[slot, c],
                    send_sem=send_of(a, k), recv_sem=recv_of(a, k),
                    device_id=(*chips[k], c), device_id_type=MESH)

            sends.append(functools.partial(copy, me))
            arrivals.append(functools.partial(copy, ck))
    return sends, arrivals


def _gather_start(bufs, name, after=()):
    n = len(bufs)
    ns = 3 * n

    def body(*refs):
        sems = refs[n:n + 2 * ns]
        thru = refs[n + 2 * ns:2 * n + 2 * ns]
        token = refs[2 * n + 2 * ns]
        _chips_handshake()
        sends, _ = _gather_descriptors(thru, lambda a, k: sems[3 * a + k], lambda a, k: sems[ns + 3 * a + k])
        for cp in sends:
            cp().start()
        token[...] = jnp.zeros_like(token)

    held = [pltpu.with_memory_space_constraint(b, pltpu.HBM) for b in bufs]
    out = _tied_call(
        body, after, name=name,
        out_shape=(*[pltpu.SemaphoreType.DMA(())] * (2 * ns), *[pltpu.HBM(b.shape, b.dtype) for b in held],
                   jax.ShapeDtypeStruct((8, LANES), F32)),
        in_specs=[HBM_SPEC] * n,
        out_specs=(*[SEM_SPEC] * (2 * ns), *[HBM_SPEC] * n, pl.BlockSpec(memory_space=pltpu.VMEM)),
        input_output_aliases={i: 2 * ns + i for i in range(n)},
        compiler_params=pltpu.CompilerParams(has_side_effects=pltpu.SideEffectType.DATAFLOW_SIDE_EFFECTING,
                                             collective_id=CHIPS_COLLECTIVE_ID),
    )(*held)
    return list(out[:ns]), list(out[ns:2 * ns]), list(out[2 * ns:2 * ns + n]), out[2 * ns + n]


def _gather_wait(send_sems, recv_sems, bufs, after, name):
    n = len(bufs)
    ns = 3 * n

    def body(*refs):
        buf_ref = refs[:n]
        sem_ref = refs[n:n + 2 * ns]
        sends, arrivals = _gather_descriptors(buf_ref, lambda a, k: sem_ref[3 * a + k],
                                              lambda a, k: sem_ref[ns + 3 * a + k])
        for cp in sends:
            cp().wait_send()
        for cp in arrivals:
            cp().wait_recv()

    out = pl.pallas_call(
        body, name=name,
        out_shape=tuple(pltpu.HBM(b.shape, b.dtype) for b in bufs),
        in_specs=[HBM_SPEC] * n + [SEM_SPEC] * (2 * ns) + [ANY] * len(_as_tuple(after)),
        out_specs=tuple([HBM_SPEC] * n),
        input_output_aliases={i: i for i in range(n)},
        compiler_params=pltpu.CompilerParams(has_side_effects=pltpu.SideEffectType.DATAFLOW_SIDE_EFFECTING),
    )(*bufs, *send_sems, *recv_sems, *_as_tuple(after))
    return list(out)


SIBLING_COLLECTIVE_ID = 0


def _sibling_handshake():
    x, y, c = _mesh_pos()
    barrier = pltpu.get_barrier_semaphore()
    pl.semaphore_signal(barrier, inc=1, device_id=(x, y, 1 - c), device_id_type=MESH)
    pl.semaphore_wait(barrier, 1)


CHIPS_COLLECTIVE_ID = 1


def _chips_handshake():
    x, y, c = _mesh_pos()
    barrier = pltpu.get_barrier_semaphore()
    for chip in _other_chips(x, y):
        pl.semaphore_signal(barrier, inc=1, device_id=(*chip, c), device_id_type=MESH)
    pl.semaphore_wait(barrier, 3)


def _pass_to_sibling(bufs, name):
    n = len(bufs)

    def body(*refs):
        outs = refs[n:2 * n]
        send_sem, recv_sem = refs[2 * n:]
        x, y, c = _mesh_pos()
        chips = _other_chips(x, y)
        _sibling_handshake()

        def half(a, k, which):
            ck = 2 * chips[k][0] + chips[k][1]
            return pltpu.make_async_remote_copy(
                src_ref=outs[a].at[ck, which], dst_ref=outs[a].at[ck, which],
                send_sem=send_sem.at[a, k], recv_sem=recv_sem.at[a, k],
                device_id=(x, y, 1 - c), device_id_type=MESH)

        sends = [half(a, k, c) for a in range(n) for k in range(3)]
        for cp in sends:
            cp.start()
        for a in range(n):
            for k in range(3):
                half(a, k, 1 - c).wait_recv()
        for cp in sends:
            cp.wait_send()

    return pl.pallas_call(
        body, name=name,
        in_specs=[ANY] * n, out_specs=[ANY] * n,
        out_shape=[jax.ShapeDtypeStruct(b.shape, b.dtype) for b in bufs],
        input_output_aliases={a: a for a in range(n)},
        scratch_shapes=[pltpu.SemaphoreType.DMA((n, 3))] * 2,
        compiler_params=pltpu.CompilerParams(collective_id=SIBLING_COLLECTIVE_ID),
    )(*bufs)


def _swap_halves(grads, name):
    n = len(grads)

    def body(*refs):
        ins, outs = refs[:n], refs[n:2 * n]
        send_sem, recv_sem = refs[2 * n:]
        x, y, c = _mesh_pos()
        _sibling_handshake()
        cps = [pltpu.make_async_remote_copy(
            src_ref=ins[a].at[:, pl.ds(1 - c, 1)], dst_ref=outs[a],
            send_sem=send_sem.at[a], recv_sem=recv_sem.at[a],
            device_id=(x, y, 1 - c), device_id_type=MESH) for a in range(n)]
        for cp in cps:
            cp.start()
        for cp in cps:
            cp.wait()

    out_shape = [jax.ShapeDtypeStruct((g.shape[0], 1) + g.shape[2:], g.dtype) for g in grads]
    return pl.pallas_call(
        body, name=name,
        in_specs=[ANY] * n, out_specs=[ANY] * n, out_shape=out_shape,
        scratch_shapes=[pltpu.SemaphoreType.DMA((n,))] * 2,
        compiler_params=pltpu.CompilerParams(collective_id=SIBLING_COLLECTIVE_ID),
    )(*grads)


def _swap_descriptors(grads, lands, send_of, recv_of):
    x, y, c = _mesh_pos()
    return [functools.partial(
        pltpu.make_async_remote_copy,
        src_ref=grads[a].at[:, pl.ds(1 - c, 1)], dst_ref=lands[a],
        send_sem=send_of(a), recv_sem=recv_of(a),
        device_id=(x, y, 1 - c), device_id_type=MESH) for a in range(len(grads))]


def _swap_start(grads, name):
    n = len(grads)

    def body(*refs):
        sems = refs[2 * n:4 * n]
        g_thru, l_thru = refs[4 * n:5 * n], refs[5 * n:6 * n]
        token = refs[6 * n]
        _sibling_handshake()
        for cp in _swap_descriptors(g_thru, l_thru, lambda a: sems[a], lambda a: sems[n + a]):
            cp().start()
        token[...] = jnp.zeros_like(token)

    lands = [lax.empty((g.shape[0], 1) + g.shape[2:], g.dtype) for g in grads]
    held = [pltpu.with_memory_space_constraint(a, pltpu.HBM) for a in (*grads, *lands)]
    out = pl.pallas_call(
        body, name=name,
        out_shape=(*[pltpu.SemaphoreType.DMA(())] * (2 * n), *[pltpu.HBM(a.shape, a.dtype) for a in held],
                   jax.ShapeDtypeStruct((8, LANES), F32)),
        in_specs=[HBM_SPEC] * (2 * n),
        out_specs=(*[SEM_SPEC] * (2 * n), *[HBM_SPEC] * (2 * n), pl.BlockSpec(memory_space=pltpu.VMEM)),
        input_output_aliases={i: 2 * n + i for i in range(2 * n)},
        compiler_params=pltpu.CompilerParams(has_side_effects=pltpu.SideEffectType.DATAFLOW_SIDE_EFFECTING,
                                             collective_id=SIBLING_COLLECTIVE_ID),
    )(*held)
    return list(out[:2 * n]), list(out[2 * n:3 * n]), list(out[3 * n:4 * n]), out[4 * n]


def _swap_wait(sems, grads, lands, after, name):
    n = len(grads)

    def body(*refs):
        g_ref, l_ref = refs[:n], refs[n:2 * n]
        sem_ref = refs[2 * n:4 * n]
        for cp in _swap_descriptors(g_ref, l_ref, lambda a: sem_ref[a], lambda a: sem_ref[n + a]):
            cp().wait()

    out = pl.pallas_call(
        body, name=name,
        out_shape=tuple(pltpu.HBM(a.shape, a.dtype) for a in (*grads, *lands)),
        in_specs=[HBM_SPEC] * (2 * n) + [SEM_SPEC] * (2 * n) + [ANY] * len(_as_tuple(after)),
        out_specs=tuple([HBM_SPEC] * (2 * n)),
        input_output_aliases={i: i for i in range(2 * n)},
        compiler_params=pltpu.CompilerParams(has_side_effects=pltpu.SideEffectType.DATAFLOW_SIDE_EFFECTING),
    )(*grads, *lands, *sems, *_as_tuple(after))
    return list(out[:n]), list(out[n:])


def _add_halves(gs, gots, pos, name, dtypes):
    n = len(gs)
    j = gs[0].shape[0]

    def body(pos_ref, *refs):
        g_refs, r_refs = refs[:n], refs[n:2 * n]
        o_refs, p_refs = refs[2 * n:3 * n], refs[3 * n:]
        vals = [(g_refs[a][0, 0] + r_refs[a][0, 0]).astype(dtypes[a]) for a in range(n)]
        for a in range(n):
            o_refs[a][0] = vals[a]
        if j == 1:
            for a in range(n):
                p_refs[a][0] = vals[a]
        else:
            @pl.when(pl.program_id(0) == pos_ref[0])
            def _():
                for a in range(n):
                    p_refs[a][0] = vals[a]

    blk = lambda g: (1,) + g.shape[2:]
    out = pl.pallas_call(
        body, name=name,
        grid_spec=pltpu.PrefetchScalarGridSpec(
            num_scalar_prefetch=1, grid=(j,),
            in_specs=[pl.BlockSpec((1,) + blk(g), lambda i, p: (i, p[1], 0, 0)) for g in gs]
            + [pl.BlockSpec((1,) + blk(g), lambda i, p: (i, 0, 0, 0)) for g in gs],
            out_specs=[pl.BlockSpec(blk(g), lambda i, p: (i, 0, 0)) for g in gs]
            + [pl.BlockSpec(blk(g), lambda i, p: (p[0], 0, 0)) for g in gs]),
        out_shape=[jax.ShapeDtypeStruct((j,) + g.shape[2:], dt) for g, dt in zip(gs, dtypes)]
        + [jax.ShapeDtypeStruct((N_CHIPS,) + g.shape[2:], dt) for g, dt in zip(gs, dtypes)],
        compiler_params=_params(("arbitrary",)),
    )(pos, *gs, *gots)
    return list(out[:n]), list(out[n:])


def _exchange_descriptors(sums, parts, send_of, recv_of):
    x, y, c = _mesh_pos()
    me = 2 * x + y
    chips = _other_chips(x, y)
    sends, arrivals = [], []
    for a in range(len(sums)):
        for k in range(3):
            ck = 2 * chips[k][0] + chips[k][1]
            mine = sums[a].at[ck] if sums[a].shape[0] == N_CHIPS else sums[a].at[0]

            def copy(dst_slot, a=a, k=k, mine=mine):
                return pltpu.make_async_remote_copy(
                    src_ref=mine, dst_ref=parts[a].at[dst_slot],
                    send_sem=send_of(a, k), recv_sem=recv_of(a, k),
                    device_id=(*chips[k], c), device_id_type=MESH)

            sends.append(functools.partial(copy, me))
            arrivals.append(functools.partial(copy, ck))
    return sends, arrivals


def _exchange_start(sums, parts, name, after=()):
    n = len(sums)
    ns = 3 * n

    def body(*refs):
        sems = refs[2 * n:2 * n + 2 * ns]
        sums_thru = refs[2 * n + 2 * ns:3 * n + 2 * ns]
        parts_thru = refs[3 * n + 2 * ns:4 * n + 2 * ns]
        token = refs[4 * n + 2 * ns]
        _chips_handshake()
        sends, _ = _exchange_descriptors(sums_thru, parts_thru, lambda a, k: sems[3 * a + k],
                                         lambda a, k: sems[ns + 3 * a + k])
        for cp in sends:
            cp().start()
        token[...] = jnp.zeros_like(token)

    hbm = lambda a: pltpu.HBM(a.shape, a.dtype)
    held = [pltpu.with_memory_space_constraint(a, pltpu.HBM) for a in (*sums, *parts)]
    out = _tied_call(
        body, after, name=name,
        out_shape=(*[pltpu.SemaphoreType.DMA(())] * (2 * ns), *[hbm(a) for a in held],
                   jax.ShapeDtypeStruct((8, LANES), F32)),
        in_specs=[HBM_SPEC] * (2 * n),
        out_specs=(*[SEM_SPEC] * (2 * ns), *[HBM_SPEC] * (2 * n), pl.BlockSpec(memory_space=pltpu.VMEM)),
        input_output_aliases={i: 2 * ns + i for i in range(2 * n)},
        compiler_params=pltpu.CompilerParams(has_side_effects=pltpu.SideEffectType.DATAFLOW_SIDE_EFFECTING,
                                             collective_id=CHIPS_COLLECTIVE_ID),
    )(*held)
    return (list(out[:2 * ns]), list(out[2 * ns:2 * ns + n]), list(out[2 * ns + n:2 * ns + 2 * n]),
            out[2 * ns + 2 * n])


def _exchange_wait(sems, sums, parts, after, name):
    n = len(sums)
    ns = 3 * n

    def body(*refs):
        sums_ref, parts_ref = refs[:n], refs[n:2 * n]
        sem_ref = refs[2 * n:2 * n + 2 * ns]
        sends, arrivals = _exchange_descriptors(sums_ref, parts_ref, lambda a, k: sem_ref[3 * a + k],
                                                lambda a, k: sem_ref[ns + 3 * a + k])
        for cp in sends:
            cp().wait_send()
        for cp in arrivals:
            cp().wait_recv()

    hbm = lambda a: pltpu.HBM(a.shape, a.dtype)
    out = pl.pallas_call(
        body, name=name,
        out_shape=tuple(hbm(a) for a in (*sums, *parts)),
        in_specs=[HBM_SPEC] * (2 * n) + [SEM_SPEC] * (2 * ns) + [ANY] * len(_as_tuple(after)),
        out_specs=tuple([HBM_SPEC] * (2 * n)),
        input_output_aliases={i: i for i in range(2 * n)},
        compiler_params=pltpu.CompilerParams(has_side_effects=pltpu.SideEffectType.DATAFLOW_SIDE_EFFECTING),
    )(*sums, *parts, *sems, *_as_tuple(after))
    return list(out[n:])


def _sum_chips(parts, pos, name, after=()):
    n = len(parts)
    after = _as_tuple(after)

    def body(pos_ref, *refs):
        outs = refs[n + len(after):]
        for a in range(n):
            p_ref = refs[a]
            outs[a][0] = (((p_ref[0].astype(F32) + p_ref[1].astype(F32)) + p_ref[2].astype(F32))
                          + p_ref[3].astype(F32))

    out = pl.pallas_call(
        body, name=name,
        grid_spec=pltpu.PrefetchScalarGridSpec(
            num_scalar_prefetch=1, grid=(1,),
            in_specs=[pl.BlockSpec(p.shape, lambda i, q: (0, 0, 0)) for p in parts] + [ANY] * len(after),
            out_specs=[pl.BlockSpec((1,) + p.shape[1:], lambda i, q: (q[1], 0, 0)) for p in parts]),
        out_shape=[jax.ShapeDtypeStruct((2,) + p.shape[1:], F32) for p in parts],
        compiler_params=_params(("arbitrary",)),
    )(pos, *parts, *after)
    return list(out)


def _join_halves(fulls, name, after=()):
    n = len(fulls)

    def body(*refs):
        outs = refs[n:2 * n]
        send_sem, recv_sem = refs[2 * n:]
        x, y, c = _mesh_pos()
        _sibling_handshake()

        def half(a, which):
            return pltpu.make_async_remote_copy(
                src_ref=outs[a].at[which], dst_ref=outs[a].at[which],
                send_sem=send_sem.at[a], recv_sem=recv_sem.at[a],
                device_id=(x, y, 1 - c), device_id_type=MESH)

        sends = [half(a, c) for a in range(n)]
        for cp in sends:
            cp.start()
        for a in range(n):
            half(a, 1 - c).wait_recv()
        for cp in sends:
            cp.wait_send()

    out_shape = [jax.ShapeDtypeStruct(f.shape, f.dtype) for f in fulls]
    return _tied_call(
        body, after, name=name,
        in_specs=[ANY] * n, out_specs=[ANY] * n, out_shape=out_shape,
        input_output_aliases={a: a for a in range(n)},
        scratch_shapes=[pltpu.SemaphoreType.DMA((n,))] * 2,
        compiler_params=pltpu.CompilerParams(collective_id=SIBLING_COLLECTIVE_ID),
    )(*fulls)


def _norm_in(x, g, ts, after=()):
    s = x.shape[0]

    def body(x_ref, g_ref, hn_ref):
        xv = x_ref[...]
        hn_ref[...] = (xv * _rms_stats(xv) * g_ref[...]).astype(BF16)

    row = pl.BlockSpec((ts, D_MODEL), lambda i: (i, 0))
    return _tied_call(
        body, after, name="norm_in", grid=(s // ts,),
        in_specs=[row, pl.BlockSpec((1, D_MODEL), lambda i: (0, 0))], out_specs=row,
        out_shape=jax.ShapeDtypeStruct((s, D_MODEL), BF16),
        compiler_params=_params(("parallel",)),
    )(x, g)


def _shift_rows(buf, shifted, t):
    rows = t + CONV_HALO - SUBLANES
    for r in range(1, SUBLANES):
        shifted[r - 1, 0:rows, :] = buf[pl.ds(r, rows), :]


def _window(buf, shifted, offset, t):
    r = offset % SUBLANES
    if r == 0:
        return buf[pl.ds(offset, t), :]
    return shifted[r - 1, pl.ds(offset - r, t), :]


def _lane_is_low_head():
    lane = lax.broadcasted_iota(jnp.int32, (1, GM_WIDTH), 1)
    return (lane & GM_HEAD_DIM) == 0


def _gm_mix(v_lo, v_hi, wpair_ref, bias_ref, mixed_ref, t):
    for n in range(t // CHUNK):
        rows = slice(n * CHUNK, (n + 1) * CHUNK)
        for j in range(GM_HEADS // 2):
            cols = slice(j * LANES, (j + 1) * LANES)
            rhs = jnp.concatenate([v_lo[rows, cols], v_hi[rows, cols]], axis=0)
            mixed_ref[rows, cols] = _dot(wpair_ref[j], rhs) + bias_ref[:, cols]


def _seqmix_fwd(hn, w_in, b_in, cw, cb, lng, lnb, gg, gb, wpair, bias, t, after=()):
    s = hn.shape[0]

    def body(hn_ref, w_ref, b_ref, cw_ref, cb_ref, lng_ref, lnb_ref, gg_ref, gb_ref, wpair_ref, bias_ref,
             z_ref, mix_ref, c1_ref, abuf, ash, mixed_ref):
        i = pl.program_id(0)

        @pl.when(i == 0)
        def _():
            abuf[0:CONV_HALO, :] = jnp.zeros((CONV_HALO, CONV_WIDTH), F32)

        @pl.when(i > 0)
        def _():
            abuf[0:CONV_HALO, :] = abuf[t:t + CONV_HALO, :]

        hv = hn_ref[...]
        for j in range(4):
            cols = slice(j * 512, (j + 1) * 512)
            z_ref[:, cols] = _dot(hv, w_ref[j]) + b_ref[:, cols]

        abuf[CONV_HALO:, :] = z_ref[:, 0:512] * _sigmoid(z_ref[:, 512:1024])
        _shift_rows(abuf, ash, t)
        acc = jnp.zeros((t, CONV_WIDTH), F32)
        for k in range(CONV_KERNEL):
            acc = acc + cw_ref[k:k + 1, :] * _window(abuf, ash, CONV_HALO - (CONV_KERNEL - 1) + k, t)
        c1 = acc + cb_ref[...]
        c1_ref[...] = c1
        xh, _ = _ln_stats(c1)
        ln = xh * lng_ref[...] + lnb_ref[...]
        mix_ref[:, 0:512] = (ln * _sigmoid(ln)).astype(BF16)

        u, _ = _gelu_parts(z_ref[:, 1024:1536])
        gv, _ = _gelu_parts(z_ref[:, 1536:2048])
        vxh, _ = _ln_stats(gv)
        v = vxh * gg_ref[...] + gb_ref[...]
        low = _lane_is_low_head()
        v_lo = jnp.where(low, v, 0.0).astype(BF16)
        v_hi = jnp.where(low, 0.0, v).astype(BF16)
        _gm_mix(v_lo, v_hi, wpair_ref, bias_ref, mixed_ref, t)
        mix_ref[:, 512:1024] = (u * mixed_ref[...]).astype(BF16)

    vec = lambda n: pl.BlockSpec((1, n), lambda i: (0, 0))
    return _tied_call(
        body, after, name="seqmix_fwd", grid=(s // t,),
        in_specs=[pl.BlockSpec((t, D_MODEL), lambda i: (i, 0)),
                  pl.BlockSpec((4, D_MODEL, 512), lambda i: (0, 0, 0)), vec(2048),
                  pl.BlockSpec((CONV_HALO, CONV_WIDTH), lambda i: (0, 0)),
                  vec(512), vec(512), vec(512), vec(512), vec(512),
                  pl.BlockSpec((4, CHUNK, 2 * CHUNK), lambda i: (0, 0, 0)),
                  pl.BlockSpec((CHUNK, GM_WIDTH), lambda i: (0, 0))],
        out_specs=[pl.BlockSpec((t, 2048), lambda i: (i, 0)),
                   pl.BlockSpec((t, D_MODEL), lambda i: (i, 0)),
                   pl.BlockSpec((t, CONV_WIDTH), lambda i: (i, 0))],
        out_shape=[jax.ShapeDtypeStruct((s, 2048), F32), jax.ShapeDtypeStruct((s, D_MODEL), BF16),
                   jax.ShapeDtypeStruct((s, CONV_WIDTH), F32)],
        scratch_shapes=[pltpu.VMEM((t + CONV_HALO, CONV_WIDTH), F32),
                        pltpu.VMEM((SUBLANES - 1, t + CONV_HALO - SUBLANES, CONV_WIDTH), F32),
                        pltpu.VMEM((t, GM_WIDTH), F32)],
        compiler_params=_params(("arbitrary",)),
    )(hn, w_in, b_in, cw, cb, lng, lnb, gg, gb, wpair, bias)


def _mem_kv(mem, g, wkv):
    m = mem.shape[0]

    def body(mem_ref, g_ref, w_ref, mn_ref, kv_ref):
        mv = mem_ref[...]
        mn = (mv * _rms_stats(mv) * g_ref[...]).astype(BF16)
        mn_ref[...] = mn
        for j in range(4):
            kv_ref[:, j * 512:(j + 1) * 512] = _dot(mn, w_ref[j]).astype(BF16)

    return pl.pallas_call(
        body, name="mem_kv",
        out_shape=[jax.ShapeDtypeStruct((m, D_MODEL), BF16), jax.ShapeDtypeStruct((m, 2 * D_MODEL), BF16)],
        compiler_params=pltpu.CompilerParams(vmem_limit_bytes=VMEM_LIMIT_BYTES),
    )(mem, g, wkv)


def _softmax_rows(sc):
    e = jnp.exp(sc - jnp.max(sc, axis=-1, keepdims=True))
    return e / jnp.sum(e, axis=-1, keepdims=True)


def _attn_block_fwd(x, mix, w_out, g_xa, wq, kv, wo, g_ffn, ts, after=()):
    s, m = x.shape[0], kv.shape[0]
    scale = XA_HEAD_DIM ** -0.5

    def body(x_ref, mix_ref, wout_ref, gxa_ref, wq_ref, kv_ref, wo_ref, gffn_ref,
             h1_ref, hn2_ref, q_ref, o_ref, h2_ref, hn3_ref):
        h1 = x_ref[...] + _dot(mix_ref[...], wout_ref[...])
        h1_ref[...] = h1
        hn2 = (h1 * _rms_stats(h1) * gxa_ref[...]).astype(BF16)
        hn2_ref[...] = hn2
        q_ref[...] = _dot(hn2, wq_ref[...]).astype(BF16)
        for h in range(XA_HEADS):
            cols = slice(h * XA_HEAD_DIM, (h + 1) * XA_HEAD_DIM)
            vcols = slice(D_MODEL + h * XA_HEAD_DIM, D_MODEL + (h + 1) * XA_HEAD_DIM)
            p = _softmax_rows(_dot_nt(q_ref[:, cols], kv_ref[:, cols]) * scale)
            o_ref[:, cols] = _dot(p.astype(BF16), kv_ref[:, vcols]).astype(BF16)
        h2 = h1 + _dot(o_ref[...], wo_ref[...])
        h2_ref[...] = h2
        hn3_ref[...] = (h2 * _rms_stats(h2) * gffn_ref[...]).astype(BF16)

    row = pl.BlockSpec((ts, D_MODEL), lambda i: (i, 0))
    full = pl.BlockSpec((D_MODEL, D_MODEL), lambda i: (0, 0))
    vec = pl.BlockSpec((1, D_MODEL), lambda i: (0, 0))
    f32 = jax.ShapeDtypeStruct((s, D_MODEL), F32)
    bf16 = jax.ShapeDtypeStruct((s, D_MODEL), BF16)
    return _tied_call(
        body, after, name="attn_block_fwd", grid=(s // ts,),
        in_specs=[row, row, full, vec, full, pl.BlockSpec((m, 2 * D_MODEL), lambda i: (0, 0)), full, vec],
        out_specs=[row] * 6,
        out_shape=[f32, bf16, bf16, bf16, f32, bf16],
        compiler_params=_params(("parallel",)),
    )(x, mix, w_out, g_xa, wq, kv, wo, g_ffn)


_FFN_CHUNKS_FWD = (slice(0, 6 * LANES), slice(6 * LANES, FFN_HALF))
_FFN_CHUNKS_BWD = (slice(0, 4 * LANES), slice(4 * LANES, 8 * LANES), slice(8 * LANES, FFN_HALF))


def _ffn_up(hn, wgu, ts, after=()):
    s = hn.shape[0]

    def body(hn_ref, w_ref, gu_ref, act_ref):
        hv = hn_ref[...]
        for cols in _FFN_CHUNKS_FWD:
            gate = _dot(hv, w_ref[0, 0, :, cols])
            up = _dot(hv, w_ref[1, 0, :, cols])
            gu_ref[0, :, cols] = gate.astype(BF16)
            gu_ref[1, :, cols] = up.astype(BF16)
            act_ref[:, cols] = (gate * _sigmoid(gate) * up).astype(BF16)

    return _tied_call(
        body, after, name="ffn_up", grid=(2, s // ts),
        in_specs=[pl.BlockSpec((ts, D_MODEL), lambda j, i: (i, 0)),
                  pl.BlockSpec((2, 1, D_MODEL, FFN_HALF), lambda j, i: (0, j, 0, 0))],
        out_specs=[pl.BlockSpec((2, ts, FFN_HALF), lambda j, i: (0, i, j)),
                   pl.BlockSpec((ts, FFN_HALF), lambda j, i: (i, j))],
        out_shape=[jax.ShapeDtypeStruct((2, s, FFN_HIDDEN), BF16), jax.ShapeDtypeStruct((s, FFN_HIDDEN), BF16)],
        compiler_params=_params(("parallel", "parallel")),
    )(hn, wgu)


def _ffn_down_loss(act, wd, h2, g, target, ts):
    s = act.shape[0]

    def body(act_ref, wd_ref, h2_ref, g_ref, t_ref, dh_ref, dhb_ref, sq_ref, dg_ref):
        @pl.when(pl.program_id(0) == 0)
        def _():
            sq_ref[...] = jnp.zeros_like(sq_ref)
            dg_ref[...] = jnp.zeros_like(dg_ref)

        h3 = h2_ref[...] + _dot(act_ref[...], wd_ref[...])
        r = _rms_stats(h3)
        gv = g_ref[...]
        diff = h3 * r * gv - t_ref[...]
        sq_ref[...] += _rowsum(diff * diff)
        dh, dg = _rms_bwd(diff / D_MODEL, h3, r, gv)
        dh_ref[...] = dh
        dhb_ref[...] = dh.astype(BF16)
        dg_ref[...] += dg

    row = pl.BlockSpec((ts, D_MODEL), lambda i: (i, 0))
    vec = pl.BlockSpec((1, D_MODEL), lambda i: (0, 0))
    return pl.pallas_call(
        body, name="ffn_down_loss", grid=(s // ts,),
        in_specs=[pl.BlockSpec((ts, FFN_HIDDEN), lambda i: (i, 0)),
                  pl.BlockSpec((FFN_HIDDEN, D_MODEL), lambda i: (0, 0)), row, vec, row],
        out_specs=[row, row, vec, vec],
        out_shape=[jax.ShapeDtypeStruct((s, D_MODEL), F32), jax.ShapeDtypeStruct((s, D_MODEL), BF16),
                   jax.ShapeDtypeStruct((1, D_MODEL), F32), jax.ShapeDtypeStruct((1, D_MODEL), F32)],
        compiler_params=_params(("arbitrary",)),
    )(act, wd, h2, g, target)


def _grad_w(a, b, tk, tn, name, after=()):
    s, k = a.shape
    gb, _, n = b.shape
    nblk = n // tn
    tsr = GRAD_ROWS if s % GRAD_ROWS == 0 else s

    def body(a_ref, b_ref, o_ref):
        part = _dot_tn(a_ref[...], b_ref[0])

        @pl.when(pl.program_id(2) == 0)
        def _():
            o_ref[0] = part

        @pl.when(pl.program_id(2) > 0)
        def _():
            o_ref[0] += part

    return _tied_call(
        body, after, name=name, grid=(gb * nblk, k // tk, s // tsr),
        in_specs=[pl.BlockSpec((tsr, tk), lambda ni, ki, si: (si, ki)),
                  pl.BlockSpec((1, tsr, tn), lambda ni, ki, si: (ni // nblk, si, ni % nblk))],
        out_specs=pl.BlockSpec((1, tk, tn), lambda ni, ki, si: (ni, ki, 0)),
        out_shape=jax.ShapeDtypeStruct((gb * nblk, k, tn), F32),
        compiler_params=_params(("parallel", "parallel", "arbitrary")),
    )(a, b)


def _ffn_bwd(dh3, wd, gu, wgu, h2, g, t, after=()):
    s = dh3.shape[0]

    def body(dh3_ref, wd_ref, gu_ref, w_ref, h2_ref, g_ref, dgu_ref, dh2_ref, dh2b_ref, dg_ref):
        @pl.when(pl.program_id(0) == 0)
        def _():
            dg_ref[...] = jnp.zeros_like(dg_ref)

        dh3v = dh3_ref[...]
        dhb = dh3v.astype(BF16)
        dhn = jnp.zeros((t, D_MODEL), F32)
        for j in range(2):
            for cols in _FFN_CHUNKS_BWD:
                whole = slice(j * FFN_HALF + cols.start, j * FFN_HALF + cols.stop)
                dact = _dot_nt(dhb, wd_ref[j, cols, :])
                gate, up = gu_ref[0, :, whole].astype(F32), gu_ref[1, :, whole].astype(F32)
                sg = _sigmoid(gate)
                dgate = (dact * up * (sg * (1.0 + gate * (1.0 - sg)))).astype(BF16)
                dup = (dact * (gate * sg)).astype(BF16)
                dgu_ref[0, :, whole] = dgate
                dgu_ref[1, :, whole] = dup
                dhn = dhn + _dot_nt(dgate, w_ref[j, :, cols]) + _dot_nt(dup, w_ref[2 + j, :, cols])
        h2 = h2_ref[...]
        dv, dg = _rms_bwd(dhn, h2, _rms_stats(h2), g_ref[...])
        dh2 = dh3v + dv
        dh2_ref[...] = dh2
        dh2b_ref[...] = dh2.astype(BF16)
        dg_ref[...] += dg

    row = pl.BlockSpec((t, D_MODEL), lambda i: (i, 0))
    wide = pl.BlockSpec((2, t, FFN_HIDDEN), lambda i: (0, i, 0))
    vec = pl.BlockSpec((1, D_MODEL), lambda i: (0, 0))
    return _tied_call(
        body, after, name="ffn_bwd", grid=(s // t,),
        in_specs=[row, pl.BlockSpec((2, FFN_HALF, D_MODEL), lambda i: (0, 0, 0)), wide,
                  pl.BlockSpec((4, D_MODEL, FFN_HALF), lambda i: (0, 0, 0)), row, vec],
        out_specs=[wide, row, row, vec],
        out_shape=[jax.ShapeDtypeStruct((2, s, FFN_HIDDEN), BF16), jax.ShapeDtypeStruct((s, D_MODEL), F32),
                   jax.ShapeDtypeStruct((s, D_MODEL), BF16), jax.ShapeDtypeStruct((1, D_MODEL), F32)],
        compiler_params=_params(("arbitrary",)),
    )(dh3, wd, gu, wgu, h2, g)


def _attn_bwd(dh2, wo, q, kv, wq, h1, g, ts, after=()):
    s, m = q.shape[0], kv.shape[0]
    scale = XA_HEAD_DIM ** -0.5

    def body(dh2_ref, wo_ref, q_ref, kv_ref, wq_ref, h1_ref, g_ref, dh1_ref, dh1b_ref, dq_ref, dkv_ref, dg_ref):
        @pl.when(pl.program_id(0) == 0)
        def _():
            dkv_ref[...] = jnp.zeros_like(dkv_ref)
            dg_ref[...] = jnp.zeros_like(dg_ref)

        do = _dot_nt(dh2_ref[...].astype(BF16), wo_ref[...]).astype(BF16)
        for h in range(XA_HEADS):
            cols = slice(h * XA_HEAD_DIM, (h + 1) * XA_HEAD_DIM)
            vcols = slice(D_MODEL + h * XA_HEAD_DIM, D_MODEL + (h + 1) * XA_HEAD_DIM)
            qh, kh, vh, doh = q_ref[:, cols], kv_ref[:, cols], kv_ref[:, vcols], do[:, cols]
            p = _softmax_rows(_dot_nt(qh, kh) * scale)
            dp = _dot_nt(doh, vh)
            ds = (p * (dp - jnp.sum(dp * p, axis=-1, keepdims=True)) * scale).astype(BF16)
            dq_ref[:, cols] = _dot(ds, kh).astype(BF16)
            dkv_ref[:, cols] += _dot_tn(ds, qh)
            dkv_ref[:, vcols] += _dot_tn(p.astype(BF16), doh)
        dhn = _dot_nt(dq_ref[...], wq_ref[...])
        h1 = h1_ref[...]
        dv, dg = _rms_bwd(dhn, h1, _rms_stats(h1), g_ref[...])
        dh1 = dh2_ref[...] + dv
        dh1_ref[...] = dh1
        dh1b_ref[...] = dh1.astype(BF16)
        dg_ref[...] += dg

    row = pl.BlockSpec((ts, D_MODEL), lambda i: (i, 0))
    full = pl.BlockSpec((D_MODEL, D_MODEL), lambda i: (0, 0))
    kvs = pl.BlockSpec((m, 2 * D_MODEL), lambda i: (0, 0))
    vec = pl.BlockSpec((1, D_MODEL), lambda i: (0, 0))
    return _tied_call(
        body, after, name="attn_bwd", grid=(s // ts,),
        in_specs=[row, full, row, kvs, full, row, vec],
        out_specs=[row, row, row, kvs, vec],
        out_shape=[jax.ShapeDtypeStruct((s, D_MODEL), F32), jax.ShapeDtypeStruct((s, D_MODEL), BF16),
                   jax.ShapeDtypeStruct((s, D_MODEL), BF16),
                   jax.ShapeDtypeStruct((m, 2 * D_MODEL), F32), jax.ShapeDtypeStruct((1, D_MODEL), F32)],
        compiler_params=_params(("arbitrary",)),
    )(dh2, wo, q, kv, wq, h1, g)


def _mem_kv_bwd(dkv, mn, wkv, mem, g, after=()):
    m = mem.shape[0]

    def body(dkv_ref, mn_ref, w_ref, mem_ref, g_ref, dw_ref, dg_ref):
        dmn = jnp.zeros((m, D_MODEL), F32)
        mn = mn_ref[...]
        for j in range(4):
            dj = dkv_ref[:, j * 512:(j + 1) * 512].astype(BF16)
            dw_ref[j] = _dot_tn(mn, dj)
            dmn = dmn + _dot_nt(dj, w_ref[j])
        mv = mem_ref[...]
        dg_ref[...] = _rowsum(dmn * (mv * _rms_stats(mv)))

    return _tied_call(
        body, after, name="mem_kv_bwd", in_specs=[pl.BlockSpec(memory_space=pltpu.VMEM)] * 5,
        out_shape=[jax.ShapeDtypeStruct((4, D_MODEL, 512), F32), jax.ShapeDtypeStruct((1, D_MODEL), F32)],
        compiler_params=pltpu.CompilerParams(vmem_limit_bytes=VMEM_LIMIT_BYTES),
    )(dkv, mn, wkv, mem, g)


def _seqmix_bwd(dh1, x, z, c1, w_out, w_in, g_mix, cw, lng, lnb, gg, gb, wpair, wpair_t, bias, t, after=()):
    s = x.shape[0]
    nt = s // t

    def body(dh1_ref, x_ref, z_ref, c1_ref, wo_ref, wi_ref, gm_ref, cw_ref, lng_ref, lnb_ref,
             gg_ref, gb_ref, wpair_ref, wpt_ref, bias_ref,
             gx_ref, dz_ref, dcw_ref, dcb_ref, dlng_ref, dlnb_ref, dgg_ref, dgb_ref, dws_ref, dbs_ref,
             dbin_ref, dgm_ref, dbuf, dsh, mixed_ref, dv_ref):
        i = pl.program_id(0)
        accs = (dcw_ref, dcb_ref, dlng_ref, dlnb_ref, dgg_ref, dgb_ref, dws_ref, dbs_ref, dbin_ref, dgm_ref)

        @pl.when(i == 0)
        def _():
            for r in accs:
                r[...] = jnp.zeros_like(r)
            dbuf[t:t + CONV_HALO, :] = jnp.zeros((CONV_HALO, CONV_WIDTH), F32)

        @pl.when(i > 0)
        def _():
            dbuf[t:t + CONV_HALO, :] = dbuf[0:CONV_HALO, :]

        dmix = _dot_nt(dh1_ref[...].astype(BF16), wo_ref[...])

        xh, rs = _ln_stats(c1_ref[...])
        lng = lng_ref[...]
        ln = xh * lng + lnb_ref[...]
        sl = _sigmoid(ln)
        dln = dmix[:, 0:512] * (sl * (1.0 + ln * (1.0 - sl)))
        dc1, dg_ln, db_ln = _ln_bwd(dln, xh, rs, lng)
        dlng_ref[...] += dg_ln
        dlnb_ref[...] += db_ln
        dcb_ref[...] += _rowsum(dc1)
        dbuf[0:t, :] = dc1

        za = z_ref[:, 0:512]
        sg = _sigmoid(z_ref[:, 512:1024])
        a = za * sg
        _shift_rows(dbuf, dsh, t)

        da = jnp.zeros((t, CONV_WIDTH), F32)
        for k in range(CONV_KERNEL):
            later = _window(dbuf, dsh, CONV_KERNEL - 1 - k, t)
            da = da + cw_ref[k:k + 1, :] * later
            dcw_ref[k:k + 1, :] += _rowsum(a * later)
        dza = da * sg
        dzg = da * za * (sg * (1.0 - sg))
        dz_ref[:, 0:512] = dza.astype(BF16)
        dz_ref[:, 512:1024] = dzg.astype(BF16)
        dbin_ref[:, 0:512] += _rowsum(dza)
        dbin_ref[:, 512:1024] += _rowsum(dzg)

        dgm = dmix[:, 512:1024]
        u, du_dz = _gelu_parts(z_ref[:, 1024:1536])
        gv, dgv_dz = _gelu_parts(z_ref[:, 1536:2048])
        vxh, vrs = _ln_stats(gv)
        ggv = gg_ref[...]
        v = vxh * ggv + gb_ref[...]
        low = _lane_is_low_head()
        v_lo = jnp.where(low, v, 0.0).astype(BF16)
        v_hi = jnp.where(low, 0.0, v).astype(BF16)
        _gm_mix(v_lo, v_hi, wpair_ref, bias_ref, mixed_ref, t)
        dzu = dgm * mixed_ref[...] * du_dz
        dm = dgm * u
        dm_lo = jnp.where(low, dm, 0.0).astype(BF16)
        dm_hi = jnp.where(low, 0.0, dm).astype(BF16)
        vb = v.astype(BF16)
        tril = (lax.broadcasted_iota(jnp.int32, (CHUNK, CHUNK), 1)
                <= lax.broadcasted_iota(jnp.int32, (CHUNK, CHUNK), 0))
        for n in range(t // CHUNK):
            rows = slice(n * CHUNK, (n + 1) * CHUNK)
            dbs_ref[...] += dm[rows, :]
            for j in range(GM_HEADS // 2):
                cols = slice(j * LANES, (j + 1) * LANES)
                stack = jnp.concatenate([dm_lo[rows, cols], dm_hi[rows, cols]], axis=0)
                dws = _dot_nt(stack, vb[rows, cols])
                dws_ref[2 * j] += jnp.where(tril, dws[0:CHUNK], 0.0)
                dws_ref[2 * j + 1] += jnp.where(tril, dws[CHUNK:2 * CHUNK], 0.0)
                dv_ref[rows, cols] = _dot(wpt_ref[j], stack)
        dgv, dg_gm, db_gm = _ln_bwd(dv_ref[...], vxh, vrs, ggv)
        dgg_ref[...] += dg_gm
        dgb_ref[...] += db_gm
        dzv = dgv * dgv_dz
        dz_ref[:, 1024:1536] = dzu.astype(BF16)
        dz_ref[:, 1536:2048] = dzv.astype(BF16)
        dbin_ref[:, 1024:1536] += _rowsum(dzu)
        dbin_ref[:, 1536:2048] += _rowsum(dzv)

        dhn = jnp.zeros((t, D_MODEL), F32)
        for j in range(4):
            dhn = dhn + _dot_nt(dz_ref[:, j * 512:(j + 1) * 512], wi_ref[j])
        xv = x_ref[...]
        dv, dg = _rms_bwd(dhn, xv, _rms_stats(xv), gm_ref[...])
        gx_ref[...] = dh1_ref[...] + dv
        dgm_ref[...] += dg

    rev = lambda w: pl.BlockSpec((t, w), lambda i: (nt - 1 - i, 0))
    const = lambda *shape: pl.BlockSpec(shape, lambda i: (0,) * len(shape))
    f32 = lambda *shape: jax.ShapeDtypeStruct(shape, F32)
    return _tied_call(
        body, after, name="seqmix_bwd", grid=(nt,),
        in_specs=[rev(D_MODEL), rev(D_MODEL), rev(2048), rev(CONV_WIDTH),
                  const(D_MODEL, D_MODEL), const(4, D_MODEL, 512), const(1, D_MODEL),
                  const(CONV_HALO, CONV_WIDTH), const(1, 512), const(1, 512), const(1, 512), const(1, 512),
                  const(4, CHUNK, 2 * CHUNK), const(4, CHUNK, 2 * CHUNK), const(CHUNK, GM_WIDTH)],
        out_specs=[rev(D_MODEL), rev(2048),
                   const(CONV_HALO, CONV_WIDTH), const(1, 512), const(1, 512), const(1, 512), const(1, 512),
                   const(1, 512), const(GM_HEADS, CHUNK, CHUNK), const(CHUNK, GM_WIDTH), const(1, 2048),
                   const(1, D_MODEL)],
        out_shape=[f32(s, D_MODEL), jax.ShapeDtypeStruct((s, 2048), BF16),
                   f32(CONV_HALO, CONV_WIDTH), f32(1, 512), f32(1, 512), f32(1, 512), f32(1, 512),
                   f32(1, 512), f32(GM_HEADS, CHUNK, CHUNK), f32(CHUNK, GM_WIDTH), f32(1, 2048),
                   f32(1, D_MODEL)],
        scratch_shapes=[pltpu.VMEM((t + CONV_HALO, CONV_WIDTH), F32),
                        pltpu.VMEM((SUBLANES - 1, t + CONV_HALO - SUBLANES, CONV_WIDTH), F32),
                        pltpu.VMEM((t, GM_WIDTH), F32), pltpu.VMEM((t, GM_WIDTH), F32)],
        compiler_params=_params(("arbitrary",)),
    )(dh1, x, z, c1, w_out, w_in, g_mix, cw, lng, lnb, gg, gb, wpair, wpair_t, bias)


def _head_bias_grad(dbs):
    def body(d_ref, o_ref):
        dv = d_ref[...]
        lane = lax.broadcasted_iota(jnp.int32, (CHUNK, LANES), 1)
        acc = jnp.zeros((CHUNK, LANES), F32)
        for h in range(GM_HEADS):
            sh = jnp.sum(dv[:, h * GM_HEAD_DIM:(h + 1) * GM_HEAD_DIM], axis=-1, keepdims=True)
            acc = acc + jnp.where(lane == h, sh, 0.0)
        o_ref[...] = acc

    return pl.pallas_call(body, name="head_bias_grad",
                          out_shape=jax.ShapeDtypeStruct((CHUNK, LANES), F32))(dbs)


def kernel(x, mem, norm_mix_g, w_in, b_in, conv_w, conv_b, conv_ln_g, conv_ln_b, gm_ln_g, gm_ln_b, gm_w_s, gm_b_s, w_out, norm_xa_g, mem_norm_g, xa_wq, xa_wkv, xa_wo, norm_ffn_g, ffn_w_gate_up, ffn_w_down, final_norm_g, loss_target, m_norm_mix_g, m_w_in, m_b_in, m_conv_w, m_conv_b, m_conv_ln_g, m_conv_ln_b, m_gm_ln_g, m_gm_ln_b, m_gm_w_s, m_gm_b_s, m_w_out, m_norm_xa_g, m_mem_norm_g, m_xa_wq, m_xa_wkv, m_xa_wo, m_norm_ffn_g, m_ffn_w_gate_up, m_ffn_w_down, m_final_norm_g, v_norm_mix_g, v_w_in, v_b_in, v_conv_w, v_conv_b, v_conv_ln_g, v_conv_ln_b, v_gm_ln_g, v_gm_ln_b, v_gm_w_s, v_gm_b_s, v_w_out, v_norm_xa_g, v_mem_norm_g, v_xa_wq, v_xa_wkv, v_xa_wo, v_norm_ffn_g, v_ffn_w_gate_up, v_ffn_w_down, v_final_norm_g):
    weights = dict(norm_mix_g=norm_mix_g, w_in=w_in, b_in=b_in, conv_w=conv_w, conv_b=conv_b, conv_ln_g=conv_ln_g,
                   conv_ln_b=conv_ln_b, gm_ln_g=gm_ln_g, gm_ln_b=gm_ln_b, gm_w_s=gm_w_s, gm_b_s=gm_b_s, w_out=w_out,
                   norm_xa_g=norm_xa_g, mem_norm_g=mem_norm_g, xa_wq=xa_wq, xa_wkv=xa_wkv, xa_wo=xa_wo,
                   norm_ffn_g=norm_ffn_g, ffn_w_gate_up=ffn_w_gate_up, ffn_w_down=ffn_w_down,
                   final_norm_g=final_norm_g)
    m_in = dict(norm_mix_g=m_norm_mix_g, w_in=m_w_in, b_in=m_b_in, conv_w=m_conv_w, conv_b=m_conv_b,
                conv_ln_g=m_conv_ln_g, conv_ln_b=m_conv_ln_b, gm_ln_g=m_gm_ln_g, gm_ln_b=m_gm_ln_b, gm_w_s=m_gm_w_s,
                gm_b_s=m_gm_b_s, w_out=m_w_out, norm_xa_g=m_norm_xa_g, mem_norm_g=m_mem_norm_g, xa_wq=m_xa_wq,
                xa_wkv=m_xa_wkv, xa_wo=m_xa_wo, norm_ffn_g=m_norm_ffn_g, ffn_w_gate_up=m_ffn_w_gate_up,
                ffn_w_down=m_ffn_w_down, final_norm_g=m_final_norm_g)
    v_in = dict(norm_mix_g=v_norm_mix_g, w_in=v_w_in, b_in=v_b_in, conv_w=v_conv_w, conv_b=v_conv_b,
                conv_ln_g=v_conv_ln_g, conv_ln_b=v_conv_ln_b, gm_ln_g=v_gm_ln_g, gm_ln_b=v_gm_ln_b, gm_w_s=v_gm_w_s,
                gm_b_s=v_gm_b_s, w_out=v_w_out, norm_xa_g=v_norm_xa_g, mem_norm_g=v_mem_norm_g, xa_wq=v_xa_wq,
                xa_wkv=v_xa_wkv, xa_wo=v_xa_wo, norm_ffn_g=v_norm_ffn_g, ffn_w_gate_up=v_ffn_w_gate_up,
                ffn_w_down=v_ffn_w_down, final_norm_g=v_final_norm_g)
    grads, delta, new_m, new_v = {}, {}, {}, {}

    s = x.shape[1]
    ts = _row_tile(s)
    tb = max(CHUNK, ts // 2)
    tw = 2 * ts if s % (2 * ts) == 0 and ts >= 512 else ts
    cx, cy, cc = _mesh_pos()
    chip = 2 * cx + cy
    pos = jnp.stack([chip, cc]).astype(jnp.int32)
    row = lambda a: a.reshape(1, -1)
    x2, mem2, tgt2 = x[0], mem[0], loss_target[0]

    big = dict(w_in=w_in, xa_wkv=xa_wkv, w_out=w_out, xa_wq=xa_wq, xa_wo=xa_wo,
               ffn_w_gate_up=ffn_w_gate_up, ffn_w_down=ffn_w_down)
    big_names = list(big)
    halves = lambda a: a.reshape(2, a.shape[0] // 2, a.shape[1])
    conv_w_pad = jnp.pad(conv_w, ((0, CONV_HALO - CONV_KERNEL), (0, 0)))
    first_names = ["w_in", "conv_w"]
    later_names = [nm for nm in big_names if nm != "w_in"]
    cast = dict(zip(first_names, _cast_into_slots([halves(w_in), halves(conv_w_pad)], pos, [BF16, F32], "cast_w_in")))
    cast.update(zip(later_names, _cast_into_slots([halves(big[nm]) for nm in later_names], pos,
                                                  [BF16] * len(later_names), "cast_" + later_names[0])))

    def start_gather(names, after):
        return _gather_start([cast[nm] for nm in names], "gather_start_" + names[0], after)

    def finish_gather(names, started, after):
        send_sems, recv_sems, bufs, _ = started
        landed = _gather_wait(send_sems, recv_sems, bufs, after, "gather_wait_" + names[0])
        return dict(zip(names, (b.reshape(N_CHIPS, -1, b.shape[-1])
                                for b in _pass_to_sibling(landed, "pass_" + names[0]))))

    attn_names = ["w_out", "xa_wq", "xa_wkv", "xa_wo"]
    gather_first = start_gather(first_names, ())
    hn1 = _norm_in(x2, row(norm_mix_g), tw, after=gather_first[3])
    gw = finish_gather(first_names, gather_first, [cast[nm] for nm in later_names] + [hn1])
    w_in_g = gw["w_in"]
    cw_g = jnp.concatenate([gw["conv_w"][k] for k in range(N_CHIPS)], axis=1)
    gather_attn = start_gather(attn_names, w_in_g)

    tril = jnp.tril(jnp.ones((CHUNK, CHUNK), dtype=bool))
    ws = jnp.where(tril[None], gm_w_s, 0.0)
    wpair = jnp.concatenate([ws[0::2], ws[1::2]], axis=2).astype(BF16)
    ws_t = jnp.swapaxes(ws, 1, 2)
    wpair_t = jnp.concatenate([ws_t[0::2], ws_t[1::2]], axis=2).astype(BF16)
    bias = jnp.repeat(gm_b_s.T, GM_HEAD_DIM, axis=1)

    z, mix, c1 = _seqmix_fwd(hn1, w_in_g, row(b_in), cw_g, row(conv_b), row(conv_ln_g), row(conv_ln_b),
                             row(gm_ln_g), row(gm_ln_b), wpair, bias, ts, after=gather_attn[3])
    gw = finish_gather(attn_names, gather_attn, mix)
    w_out_g = gw["w_out"].reshape(D_MODEL, D_MODEL)
    wq_g = gw["xa_wq"].reshape(D_MODEL, D_MODEL)
    wkv_g = gw["xa_wkv"]
    wo_g = gw["xa_wo"].reshape(D_MODEL, D_MODEL)
    gather_gu = start_gather(["ffn_w_gate_up"], w_out_g)
    mn, kv = _mem_kv(mem2, row(mem_norm_g), wkv_g)
    h1, hn2, q, o, h2, hn3 = _attn_block_fwd(x2, mix, w_out_g, row(norm_xa_g), wq_g, kv, wo_g, row(norm_ffn_g), ts,
                                             after=gather_gu[3])
    wgu_g = finish_gather(["ffn_w_gate_up"], gather_gu, hn3)["ffn_w_gate_up"]
    gather_down = start_gather(["ffn_w_down"], wgu_g)
    gu, act = _ffn_up(hn3, wgu_g.reshape(2, 2, D_MODEL, FFN_HALF), tw, after=gather_down[3])
    wd_g = finish_gather(["ffn_w_down"], gather_down, act)["ffn_w_down"].reshape(FFN_HIDDEN, D_MODEL)
    dh3, dh3_b, sq, d_final_g = _ffn_down_loss(act, wd_g, h2, row(final_norm_g), tgt2, ts)
    loss_here = jnp.broadcast_to(0.5 * jnp.sum(sq) / D_MODEL, (1, 2, SUBLANES, LANES))

    def split(g, nm):
        r, c = big[nm].shape
        return g.reshape(N_CHIPS, 2, r // 2, c)

    def chip_sums(group, arrays, got):
        sums, parts = [None] * len(group), [None] * len(group)
        for blocks in (N_CHIPS, 1):
            idx = [i for i, a in enumerate(arrays) if a.shape[0] == blocks]
            if idx:
                out = _add_halves([arrays[i] for i in idx], [got[i] for i in idx], pos, "chip_sum_" + group[idx[0]],
                                  [F32 if group[i] == "loss" else BF16 for i in idx])
                for k, i in enumerate(idx):
                    sums[i], parts[i] = out[0][k], out[1][k]
        return sums, parts

    def start_swap(group, grads):
        return _swap_start([split(g, nm) for g, nm in zip(grads, group)], "swap_start_" + group[0])

    def start_exchange(group, swapping, after, landed):
        sems, arrays, lands, _ = swapping
        arrays, got = _swap_wait(sems, arrays, lands, after, "swap_wait_" + group[0])
        sums, parts = chip_sums(group, arrays, got)
        return _exchange_start(sums, parts, "exchange_start_" + group[0], landed)

    def wait_exchange(group, started, after):
        sems, sums, parts, _ = started
        return _exchange_wait(sems, sums, parts, after, "exchange_wait_" + group[0])

    def finish_exchange(group, started, after):
        return _sum_chips(wait_exchange(group, started, after), pos, "total_" + group[0])

    def join_and_update(group, after):
        joined = _join_halves([halves_of[nm] for nm in group], "join_halves_" + group[0], after)
        outs = _adamw([(weights[nm], j.reshape(big[nm].shape), m_in[nm], v_in[nm]) for nm, j in zip(group, joined)],
                      "adamw_" + group[0])
        for nm, out in zip(group, outs):
            grads[nm], delta[nm], new_m[nm], new_v[nm] = out
        return [new_v[nm] for nm in group]

    as3 = lambda a: a.reshape((1,) + a.shape)
    halves_of = {}

    g_down = _grad_w(act, as3(dh3_b), FFN_HALF, D_MODEL, "grad_ffn_w_down")
    group_a = ["ffn_w_down"]
    swap_a = start_swap(group_a, [g_down])
    dgu, dh2, dh2_b, d_ffn_g = _ffn_bwd(dh3, wd_g.reshape(2, FFN_HALF, D_MODEL), gu, wgu_g, h2, row(norm_ffn_g), tb,
                                        after=swap_a[3])
    exch_a = start_exchange(group_a, swap_a, dh2, wd_g)
    g_gu = _grad_w(hn3, dgu, D_MODEL, FFN_HALF, "grad_ffn_w_gate_up", after=exch_a[3])
    halves_of.update(zip(group_a, finish_exchange(group_a, exch_a, g_gu)))

    group_b = ["ffn_w_gate_up"]
    swap_b = start_swap(group_b, [g_gu])
    dh1, dh1_b, dq, dkv, d_xa_g = _attn_bwd(dh2, wo_g, q, kv, wq_g, h1, row(norm_xa_g), ts, after=swap_b[3])
    exch_b = start_exchange(group_b, swap_b, dh1, [halves_of[nm] for nm in group_a])
    g_wkv, d_mem_g = _mem_kv_bwd(dkv, mn, wkv_g, mem2, row(mem_norm_g), after=exch_b[3])
    g_wo = _grad_w(o, as3(dh2_b), D_MODEL, D_MODEL, "grad_xa_wo", after=exch_b[3])
    g_wq = _grad_w(hn2, as3(dq), D_MODEL, D_MODEL, "grad_xa_wq", after=exch_b[3])
    g_wout = _grad_w(mix, as3(dh1_b), D_MODEL, D_MODEL, "grad_w_out", after=exch_b[3])
    done_a = join_and_update(group_a, (g_wkv, g_wo, g_wq, g_wout))
    halves_of.update(zip(group_b, finish_exchange(group_b, exch_b, done_a)))

    group_c = ["xa_wo", "xa_wq", "xa_wkv", "w_out"]
    swap_c = start_swap(group_c, [g_wo, g_wq, g_wkv, g_wout])
    (gx, dz, d_cw, d_cb, d_lng, d_lnb, d_gg, d_gb, d_ws, d_bs_sum, d_bin, d_mix_g) = _seqmix_bwd(
        dh1, x2, z, c1, w_out_g, w_in_g, row(norm_mix_g), cw_g, row(conv_ln_g), row(conv_ln_b),
        row(gm_ln_g), row(gm_ln_b), wpair, wpair_t, bias, tb, after=swap_c[3])
    d_bs = _head_bias_grad(d_bs_sum)[:, :GM_HEADS].T
    exch_c = start_exchange(group_c, swap_c, dz, [halves_of[nm] for nm in group_b])
    g_win = _grad_w(hn1, as3(dz), D_MODEL, 512, "grad_w_in", after=exch_c[3])
    done_b = join_and_update(group_b, g_win)
    parts_c = wait_exchange(group_c, exch_c, (g_win, *done_b))

    small_names = ["norm_mix_g", "b_in", "conv_w", "conv_b", "conv_ln_g", "conv_ln_b", "gm_ln_g", "gm_ln_b",
                   "gm_w_s", "gm_b_s", "norm_xa_g", "mem_norm_g", "norm_ffn_g", "final_norm_g"]
    d_cw_by_chip = jnp.swapaxes(d_cw.reshape(CONV_HALO, N_CHIPS, LANES), 0, 1).reshape(-1, LANES)
    small_grads = dict(norm_mix_g=d_mix_g, b_in=d_bin, conv_w=d_cw_by_chip, conv_b=d_cb, conv_ln_g=d_lng,
                       conv_ln_b=d_lnb, gm_ln_g=d_gg, gm_ln_b=d_gb, gm_w_s=d_ws, gm_b_s=d_bs, norm_xa_g=d_xa_g,
                       mem_norm_g=d_mem_g, norm_ffn_g=d_ffn_g, final_norm_g=d_final_g)

    def rows_form(a):
        a = a.reshape(-1, LANES)
        return jnp.pad(a, ((0, -a.shape[0] % SUBLANES), (0, 0)))

    pieces = [rows_form(small_grads[nm]) for nm in small_names]
    offsets, total = [], 0
    for p in pieces:
        offsets.append(total)
        total += p.shape[0]
    pack_rows = -(-total // 32) * 32
    small_pack = jnp.pad(jnp.concatenate(pieces, axis=0), ((0, pack_rows - total), (0, 0)))

    group_d = ["w_in", "small", "loss"]
    arrays_d = [split(g_win, "w_in"), small_pack.reshape(1, 2, pack_rows // 2, LANES), loss_here]
    sums_d, parts_d = chip_sums(group_d, arrays_d, _swap_halves(arrays_d, "swap_halves_w_in"))
    exch_d = _exchange_start(sums_d, parts_d, "exchange_start_w_in", parts_c)
    halves_of.update(zip(group_c, _sum_chips(parts_c, pos, "total_xa_wo", exch_d[3])))
    done_c = join_and_update(group_c, exch_d[3])
    halves_of.update(zip(group_d, finish_exchange(group_d, exch_d, done_c)))
    joined_d = _join_halves([halves_of[nm] for nm in group_d], "join_halves_w_in")
    loss = joined_d[2][0, 0, 0]
    grads["w_in"], delta["w_in"], new_m["w_in"], new_v["w_in"] = _adamw(
        [(w_in, joined_d[0].reshape(w_in.shape), m_w_in, v_w_in)], "adamw_w_in")[0]

    local_rows = lambda a, nm: a if nm == "conv_w" else a.reshape(-1, LANES)
    params = [tuple(local_rows(src[nm], nm) for src in (weights, m_in, v_in)) for nm in small_names]
    outs = _adamw_small(joined_d[1].reshape(pack_rows, LANES), pos, params, offsets, small_names.index("conv_w"))
    for k, nm in enumerate(small_names):
        for dst, a in zip((grads, delta, new_m, new_v), outs[4 * k:4 * k + 4]):
            dst[nm] = a

    order = ["norm_mix_g", "w_in", "b_in", "conv_w", "conv_b", "conv_ln_g", "conv_ln_b", "gm_ln_g", "gm_ln_b",
             "gm_w_s", "gm_b_s", "w_out", "norm_xa_g", "mem_norm_g", "xa_wq", "xa_wkv", "xa_wo", "norm_ffn_g",
             "ffn_w_gate_up", "ffn_w_down", "final_norm_g"]
    fit = lambda a, nm: a.reshape(weights[nm].shape)
    return (loss, gx.reshape(x.shape),
            *[fit(grads[nm], nm) for nm in order], *[fit(delta[nm], nm) for nm in order],
            *[fit(new_m[nm], nm) for nm in order], *[fit(new_v[nm], nm) for nm in order])
```

```python
import functools

import jax
import jax.numpy as jnp
from jax import lax
from jax.experimental import pallas as pl
from jax.experimental.pallas import tpu as pltpu

F32 = jnp.float32
BF16 = jnp.bfloat16

D_MODEL = 1024
CONV_WIDTH = 512
GM_WIDTH = 512
CONV_KERNEL = 31
CONV_HALO = 32
GRAD_ROWS = 2048
CHUNK = 128
GM_HEADS = 8
GM_HEAD_DIM = 64
XA_HEADS = 4
XA_HEAD_DIM = 256
FFN_HIDDEN = 2816
FFN_HALF = FFN_HIDDEN // 2
RMS_EPS = 1e-6
LN_EPS = 1e-5
N_CHIPS = 4
LANES = 128
SUBLANES = 8

ADAM_LR = 0.001
ADAM_B1 = 0.9
ADAM_B2 = 0.999
ADAM_EPS = 1e-08
ADAM_WD = 0.01
ADAM_STEP = 10

VMEM_LIMIT_BYTES = 56 * 1024 * 1024
MESH = pl.DeviceIdType.MESH
ANY = pl.BlockSpec(memory_space=pl.ANY)
HBM_SPEC = pl.BlockSpec(memory_space=pltpu.HBM)
SEM_SPEC = pl.BlockSpec(memory_space=pltpu.SEMAPHORE)

_NT = (((1,), (1,)), ((), ()))
_TN = (((0,), (0,)), ((), ()))
_GELU_C = 0.7978845608028654
_GELU_A = 0.044715


def _dot(a, b):
    return jnp.dot(a, b, preferred_element_type=F32)


def _dot_nt(a, b):
    return lax.dot_general(a, b, _NT, preferred_element_type=F32)


def _dot_tn(a, b):
    return lax.dot_general(a, b, _TN, preferred_element_type=F32)


def _mean(v):
    return jnp.mean(v, axis=-1, keepdims=True)


def _rowsum(v):
    return jnp.sum(v, axis=0, keepdims=True)


def _sigmoid(v):
    return 1.0 / (1.0 + jnp.exp(-v))


def _gelu_parts(v):
    v2 = v * v
    t = jnp.tanh(_GELU_C * (v + _GELU_A * v * v2))
    g = 0.5 * v * (1.0 + t)
    dg = 0.5 * (1.0 + t) + 0.5 * v * (1.0 - t * t) * (_GELU_C * (1.0 + 3.0 * _GELU_A * v2))
    return g, dg


def _rms_stats(v):
    return lax.rsqrt(_mean(v * v) + RMS_EPS)


def _rms_bwd(dy, v, r, g):
    n = v * r
    dn = dy * g
    dv = r * (dn - n * _mean(dn * n))
    return dv, _rowsum(dy * n)


def _ln_stats(v):
    mu = _mean(v)
    xc = v - mu
    rs = lax.rsqrt(_mean(xc * xc) + LN_EPS)
    return xc * rs, rs


def _ln_bwd(dy, xh, rs, g):
    dxh = dy * g
    dv = rs * (dxh - _mean(dxh) - xh * _mean(dxh * xh))
    return dv, _rowsum(dy * xh), _rowsum(dy)


def _params(sem):
    return pltpu.CompilerParams(dimension_semantics=sem, vmem_limit_bytes=VMEM_LIMIT_BYTES)


def _row_tile(s):
    return 512 if s % 512 == 0 and s >= 2048 else 128


def _mesh_pos():
    return lax.axis_index("x"), lax.axis_index("y"), lax.axis_index("c")


def _cast_into_slots(ws, pos, dtypes, name):
    n = len(ws)

    def body(pos_ref, *refs):
        for a in range(n):
            refs[n + a][0] = refs[a][...].astype(dtypes[a])

    return pl.pallas_call(
        body, name=name,
        grid_spec=pltpu.PrefetchScalarGridSpec(
            num_scalar_prefetch=1, grid=(2,),
            in_specs=[pl.BlockSpec((1,) + w.shape[1:], lambda i, p: (i, 0, 0)) for w in ws],
            out_specs=[pl.BlockSpec((1, 1) + w.shape[1:], lambda i, p: (p[0], i, 0, 0)) for w in ws]),
        out_shape=[jax.ShapeDtypeStruct((N_CHIPS,) + w.shape, dt) for w, dt in zip(ws, dtypes)],
        compiler_params=_params(("parallel",)),
    )(pos, *ws)


def _adam_update(w, g, m, v):
    nm = ADAM_B1 * m + (1.0 - ADAM_B1) * g
    nv = ADAM_B2 * v + (1.0 - ADAM_B2) * (g * g)
    m_hat = nm / (1.0 - ADAM_B1 ** ADAM_STEP)
    v_hat = nv / (1.0 - ADAM_B2 ** ADAM_STEP)
    return -ADAM_LR * (m_hat / (jnp.sqrt(v_hat) + ADAM_EPS) + ADAM_WD * w), nm, nv


ADAM_STEPS = 4


def _adamw(quads, name, after=()):
    n = len(quads)

    def body(*refs):
        ins, outs = refs[:4 * n], refs[4 * n:]
        for a in range(n):
            w, g, m, v = (r[...] for r in ins[4 * a:4 * a + 4])
            outs[4 * a][...] = g
            outs[4 * a + 1][...], outs[4 * a + 2][...], outs[4 * a + 3][...] = _adam_update(w, g, m, v)

    specs = [pl.BlockSpec((q[0].shape[0] // ADAM_STEPS, q[0].shape[1]), lambda i: (i, 0)) for q in quads]
    out = _tied_call(
        body, after, name=name, grid=(ADAM_STEPS,),
        in_specs=[sp for sp in specs for _ in range(4)], out_specs=[sp for sp in specs for _ in range(4)],
        out_shape=[jax.ShapeDtypeStruct(q[0].shape, F32) for q in quads for _ in range(4)],
        compiler_params=_params(("parallel",)),
    )(*[a for q in quads for a in q])
    return [tuple(out[4 * a:4 * a + 4]) for a in range(n)]


def _adamw_small(gpack, pos, params, offsets, conv_at):
    n = len(params)

    def body(pos_ref, g_ref, *refs):
        ins, outs = refs[:3 * n], refs[3 * n:]
        for k in range(n):
            rows = params[k][0].shape[0]
            start = offsets[k]
            if k == conv_at:
                start = pl.multiple_of(start + pos_ref[0] * CONV_HALO, SUBLANES)
            g = g_ref[pl.ds(start, rows), :]
            outs[4 * k][...] = g
            outs[4 * k + 1][...], outs[4 * k + 2][...], outs[4 * k + 3][...] = _adam_update(
                ins[3 * k][...], g, ins[3 * k + 1][...], ins[3 * k + 2][...])

    flat = [a for p in params for a in p]
    vmem = pl.BlockSpec(memory_space=pltpu.VMEM)
    return pl.pallas_call(
        body, name="adamw_small",
        in_specs=[pl.BlockSpec(memory_space=pltpu.SMEM), vmem] + [vmem] * len(flat),
        out_specs=[vmem] * (4 * n),
        out_shape=[jax.ShapeDtypeStruct(p[0].shape, F32) for p in params for _ in range(4)],
    )(pos, gpack, *flat)


def _as_tuple(after):
    return tuple(after) if isinstance(after, (tuple, list)) else (after,)


def _tied_call(body, after, *, in_specs, **kwargs):
    after = _as_tuple(after)
    n_in, n_after = len(in_specs), len(after)

    def tied(*refs):
        body(*refs[:n_in], *refs[n_in + n_after:])

    call = pl.pallas_call(tied, in_specs=list(in_specs) + [ANY] * n_after, **kwargs)
    return lambda *operands: call(*operands, *after)


def _other_chips(x, y):
    return [(1 - x, y), (x, 1 - y), (1 - x, 1 - y)]


def _gather_descriptors(bufs, send_of, recv_of):
    x, y, c = _mesh_pos()
    me = 2 * x + y
    chips = _other_chips(x, y)
    sends, arrivals = [], []
    for a in range(len(bufs)):
        for k in range(3):
            ck = 2 * chips[k][0] + chips[k][1]

            def copy(slot, a=a, k=k):
                return pltpu.make_async_remote_copy(
                    src_ref=bufs[a].at[slot, c], dst_ref=bufs[a].at[slot, c],
                    send_sem=send_of(a, k), recv_sem=recv_of(a, k),
                    device_id=(*chips[k], c), device_id_type=MESH)

            sends.append(functools.partial(copy, me))
            arrivals.append(functools.partial(copy, ck))
    return sends, arrivals


def _gather_start(bufs, name, after=()):
    n = len(bufs)
    ns = 3 * n

    def body(*refs):
        sems = refs[n:n + 2 * ns]
        thru = refs[n + 2 * ns:2 * n + 2 * ns]
        token = refs[2 * n + 2 * ns]
        _chips_handshake()
        sends, _ = _gather_descriptors(thru, lambda a, k: sems[3 * a + k], lambda a, k: sems[ns + 3 * a + k])
        for cp in sends:
            cp().start()
        token[...] = jnp.zeros_like(token)

    held = [pltpu.with_memory_space_constraint(b, pltpu.HBM) for b in bufs]
    out = _tied_call(
        body, after, name=name,
        out_shape=(*[pltpu.SemaphoreType.DMA(())] * (2 * ns), *[pltpu.HBM(b.shape, b.dtype) for b in held],
                   jax.ShapeDtypeStruct((8, LANES), F32)),
        in_specs=[HBM_SPEC] * n,
        out_specs=(*[SEM_SPEC] * (2 * ns), *[HBM_SPEC] * n, pl.BlockSpec(memory_space=pltpu.VMEM)),
        input_output_aliases={i: 2 * ns + i for i in range(n)},
        compiler_params=pltpu.CompilerParams(has_side_effects=pltpu.SideEffectType.DATAFLOW_SIDE_EFFECTING,
                                             collective_id=CHIPS_COLLECTIVE_ID),
    )(*held)
    return list(out[:ns]), list(out[ns:2 * ns]), list(out[2 * ns:2 * ns + n]), out[2 * ns + n]


def _gather_wait(send_sems, recv_sems, bufs, after, name):
    n = len(bufs)
    ns = 3 * n

    def body(*refs):
        buf_ref = refs[:n]
        sem_ref = refs[n:n + 2 * ns]
        sends, arrivals = _gather_descriptors(buf_ref, lambda a, k: sem_ref[3 * a + k],
                                              lambda a, k: sem_ref[ns + 3 * a + k])
        for cp in sends:
            cp().wait_send()
        for cp in arrivals:
            cp().wait_recv()

    out = pl.pallas_call(
        body, name=name,
        out_shape=tuple(pltpu.HBM(b.shape, b.dtype) for b in bufs),
        in_specs=[HBM_SPEC] * n + [SEM_SPEC] * (2 * ns) + [ANY] * len(_as_tuple(after)),
        out_specs=tuple([HBM_SPEC] * n),
        input_output_aliases={i: i for i in range(n)},
        compiler_params=pltpu.CompilerParams(has_side_effects=pltpu.SideEffectType.DATAFLOW_SIDE_EFFECTING),
    )(*bufs, *send_sems, *recv_sems, *_as_tuple(after))
    return list(out)


SIBLING_COLLECTIVE_ID = 0


def _sibling_handshake():
    x, y, c = _mesh_pos()
    barrier = pltpu.get_barrier_semaphore()
    pl.semaphore_signal(barrier, inc=1, device_id=(x, y, 1 - c), device_id_type=MESH)
    pl.semaphore_wait(barrier, 1)


CHIPS_COLLECTIVE_ID = 1


def _chips_handshake():
    x, y, c = _mesh_pos()
    barrier = pltpu.get_barrier_semaphore()
    for chip in _other_chips(x, y):
        pl.semaphore_signal(barrier, inc=1, device_id=(*chip, c), device_id_type=MESH)
    pl.semaphore_wait(barrier, 3)


def _pass_to_sibling(bufs, name, after=()):
    n = len(bufs)

    def body(*refs):
        outs = refs[n:2 * n]
        send_sem, recv_sem = refs[2 * n:]
        x, y, c = _mesh_pos()
        chips = _other_chips(x, y)
        _sibling_handshake()

        def half(a, k, which):
            ck = 2 * chips[k][0] + chips[k][1]
            return pltpu.make_async_remote_copy(
                src_ref=outs[a].at[ck, which], dst_ref=outs[a].at[ck, which],
                send_sem=send_sem.at[a, k], recv_sem=recv_sem.at[a, k],
                device_id=(x, y, 1 - c), device_id_type=MESH)

        sends = [half(a, k, c) for a in range(n) for k in range(3)]
        for cp in sends:
            cp.start()
        for a in range(n):
            for k in range(3):
                half(a, k, 1 - c).wait_recv()
        for cp in sends:
            cp.wait_send()

    return _tied_call(
        body, after, name=name,
        in_specs=[ANY] * n, out_specs=[ANY] * n,
        out_shape=[jax.ShapeDtypeStruct(b.shape, b.dtype) for b in bufs],
        input_output_aliases={a: a for a in range(n)},
        scratch_shapes=[pltpu.SemaphoreType.DMA((n, 3))] * 2,
        compiler_params=pltpu.CompilerParams(collective_id=SIBLING_COLLECTIVE_ID),
    )(*bufs)


def _swap_halves(grads, name):
    n = len(grads)

    def body(*refs):
        ins, outs = refs[:n], refs[n:2 * n]
        send_sem, recv_sem = refs[2 * n:]
        x, y, c = _mesh_pos()
        _sibling_handshake()
        cps = [pltpu.make_async_remote_copy(
            src_ref=ins[a].at[:, pl.ds(1 - c, 1)], dst_ref=outs[a],
            send_sem=send_sem.at[a], recv_sem=recv_sem.at[a],
            device_id=(x, y, 1 - c), device_id_type=MESH) for a in range(n)]
        for cp in cps:
            cp.start()
        for cp in cps:
            cp.wait()

    out_shape = [jax.ShapeDtypeStruct((g.shape[0], 1) + g.shape[2:], g.dtype) for g in grads]
    return pl.pallas_call(
        body, name=name,
        in_specs=[ANY] * n, out_specs=[ANY] * n, out_shape=out_shape,
        scratch_shapes=[pltpu.SemaphoreType.DMA((n,))] * 2,
        compiler_params=pltpu.CompilerParams(collective_id=SIBLING_COLLECTIVE_ID),
    )(*grads)


def _swap_descriptors(grads, lands, send_of, recv_of):
    x, y, c = _mesh_pos()
    return [functools.partial(
        pltpu.make_async_remote_copy,
        src_ref=grads[a].at[:, pl.ds(1 - c, 1)], dst_ref=lands[a],
        send_sem=send_of(a), recv_sem=recv_of(a),
        device_id=(x, y, 1 - c), device_id_type=MESH) for a in range(len(grads))]


def _swap_start(grads, name):
    n = len(grads)

    def body(*refs):
        sems = refs[2 * n:4 * n]
        g_thru, l_thru = refs[4 * n:5 * n], refs[5 * n:6 * n]
        token = refs[6 * n]
        _sibling_handshake()
        for cp in _swap_descriptors(g_thru, l_thru, lambda a: sems[a], lambda a: sems[n + a]):
            cp().start()
        token[...] = jnp.zeros_like(token)

    lands = [lax.empty((g.shape[0], 1) + g.shape[2:], g.dtype) for g in grads]
    held = [pltpu.with_memory_space_constraint(a, pltpu.HBM) for a in (*grads, *lands)]
    out = pl.pallas_call(
        body, name=name,
        out_shape=(*[pltpu.SemaphoreType.DMA(())] * (2 * n), *[pltpu.HBM(a.shape, a.dtype) for a in held],
                   jax.ShapeDtypeStruct((8, LANES), F32)),
        in_specs=[HBM_SPEC] * (2 * n),
        out_specs=(*[SEM_SPEC] * (2 * n), *[HBM_SPEC] * (2 * n), pl.BlockSpec(memory_space=pltpu.VMEM)),
        input_output_aliases={i: 2 * n + i for i in range(2 * n)},
        compiler_params=pltpu.CompilerParams(has_side_effects=pltpu.SideEffectType.DATAFLOW_SIDE_EFFECTING,
                                             collective_id=SIBLING_COLLECTIVE_ID),
    )(*held)
    return list(out[:2 * n]), list(out[2 * n:3 * n]), list(out[3 * n:4 * n]), out[4 * n]


def _swap_wait(sems, grads, lands, after, name):
    n = len(grads)

    def body(*refs):
        g_ref, l_ref = refs[:n], refs[n:2 * n]
        sem_ref = refs[2 * n:4 * n]
        for cp in _swap_descriptors(g_ref, l_ref, lambda a: sem_ref[a], lambda a: sem_ref[n + a]):
            cp().wait()

    out = pl.pallas_call(
        body, name=name,
        out_shape=tuple(pltpu.HBM(a.shape, a.dtype) for a in (*grads, *lands)),
        in_specs=[HBM_SPEC] * (2 * n) + [SEM_SPEC] * (2 * n) + [ANY] * len(_as_tuple(after)),
        out_specs=tuple([HBM_SPEC] * (2 * n)),
        input_output_aliases={i: i for i in range(2 * n)},
        compiler_params=pltpu.CompilerParams(has_side_effects=pltpu.SideEffectType.DATAFLOW_SIDE_EFFECTING),
    )(*grads, *lands, *sems, *_as_tuple(after))
    return list(out[:n]), list(out[n:])


def _add_halves(gs, gots, pos, name, dtypes):
    n = len(gs)
    j = gs[0].shape[0]

    def body(pos_ref, *refs):
        g_refs, r_refs = refs[:n], refs[n:2 * n]
        o_refs, p_refs = refs[2 * n:3 * n], refs[3 * n:]
        vals = [(g_refs[a][0, 0] + r_refs[a][0, 0]).astype(dtypes[a]) for a in range(n)]
        for a in range(n):
            o_refs[a][0] = vals[a]
        if j == 1:
            for a in range(n):
                p_refs[a][0] = vals[a]
        else:
            @pl.when(pl.program_id(0) == pos_ref[0])
            def _():
                for a in range(n):
                    p_refs[a][0] = vals[a]

    blk = lambda g: (1,) + g.shape[2:]
    out = pl.pallas_call(
        body, name=name,
        grid_spec=pltpu.PrefetchScalarGridSpec(
            num_scalar_prefetch=1, grid=(j,),
            in_specs=[pl.BlockSpec((1,) + blk(g), lambda i, p: (i, p[1], 0, 0)) for g in gs]
            + [pl.BlockSpec((1,) + blk(g), lambda i, p: (i, 0, 0, 0)) for g in gs],
            out_specs=[pl.BlockSpec(blk(g), lambda i, p: (i, 0, 0)) for g in gs]
            + [pl.BlockSpec(blk(g), lambda i, p: (p[0], 0, 0)) for g in gs]),
        out_shape=[jax.ShapeDtypeStruct((j,) + g.shape[2:], dt) for g, dt in zip(gs, dtypes)]
        + [jax.ShapeDtypeStruct((N_CHIPS,) + g.shape[2:], dt) for g, dt in zip(gs, dtypes)],
        compiler_params=_params(("arbitrary",)),
    )(pos, *gs, *gots)
    return list(out[:n]), list(out[n:])


def _exchange_descriptors(sums, parts, send_of, recv_of):
    x, y, c = _mesh_pos()
    me = 2 * x + y
    chips = _other_chips(x, y)
    sends, arrivals = [], []
    for a in range(len(sums)):
        for k in range(3):
            ck = 2 * chips[k][0] + chips[k][1]
            mine = sums[a].at[ck] if sums[a].shape[0] == N_CHIPS else sums[a].at[0]

            def copy(dst_slot, a=a, k=k, mine=mine):
                return pltpu.make_async_remote_copy(
                    src_ref=mine, dst_ref=parts[a].at[dst_slot],
                    send_sem=send_of(a, k), recv_sem=recv_of(a, k),
                    device_id=(*chips[k], c), device_id_type=MESH)

            sends.append(functools.partial(copy, me))
            arrivals.append(functools.partial(copy, ck))
    return sends, arrivals


def _exchange_start(sums, parts, name, after=()):
    n = len(sums)
    ns = 3 * n

    def body(*refs):
        sems = refs[2 * n:2 * n + 2 * ns]
        sums_thru = refs[2 * n + 2 * ns:3 * n + 2 * ns]
        parts_thru = refs[3 * n + 2 * ns:4 * n + 2 * ns]
        token = refs[4 * n + 2 * ns]
        _chips_handshake()
        sends, _ = _exchange_descriptors(sums_thru, parts_thru, lambda a, k: sems[3 * a + k],
                                         lambda a, k: sems[ns + 3 * a + k])
        for cp in sends:
            cp().start()
        token[...] = jnp.zeros_like(token)

    hbm = lambda a: pltpu.HBM(a.shape, a.dtype)
    held = [pltpu.with_memory_space_constraint(a, pltpu.HBM) for a in (*sums, *parts)]
    out = _tied_call(
        body, after, name=name,
        out_shape=(*[pltpu.SemaphoreType.DMA(())] * (2 * ns), *[hbm(a) for a in held],
                   jax.ShapeDtypeStruct((8, LANES), F32)),
        in_specs=[HBM_SPEC] * (2 * n),
        out_specs=(*[SEM_SPEC] * (2 * ns), *[HBM_SPEC] * (2 * n), pl.BlockSpec(memory_space=pltpu.VMEM)),
        input_output_aliases={i: 2 * ns + i for i in range(2 * n)},
        compiler_params=pltpu.CompilerParams(has_side_effects=pltpu.SideEffectType.DATAFLOW_SIDE_EFFECTING,
                                             collective_id=CHIPS_COLLECTIVE_ID),
    )(*held)
    return (list(out[:2 * ns]), list(out[2 * ns:2 * ns + n]), list(out[2 * ns + n:2 * ns + 2 * n]),
            out[2 * ns + 2 * n])


def _exchange_wait(sems, sums, parts, after, name):
    n = len(sums)
    ns = 3 * n

    def body(*refs):
        sums_ref, parts_ref = refs[:n], refs[n:2 * n]
        sem_ref = refs[2 * n:2 * n + 2 * ns]
        sends, arrivals = _exchange_descriptors(sums_ref, parts_ref, lambda a, k: sem_ref[3 * a + k],
                                                lambda a, k: sem_ref[ns + 3 * a + k])
        for cp in sends:
            cp().wait_send()
        for cp in arrivals:
            cp().wait_recv()

    hbm = lambda a: pltpu.HBM(a.shape, a.dtype)
    out = pl.pallas_call(
        body, name=name,
        out_shape=tuple(hbm(a) for a in (*sums, *parts)),
        in_specs=[HBM_SPEC] * (2 * n) + [SEM_SPEC] * (2 * ns) + [ANY] * len(_as_tuple(after)),
        out_specs=tuple([HBM_SPEC] * (2 * n)),
        input_output_aliases={i: i for i in range(2 * n)},
        compiler_params=pltpu.CompilerParams(has_side_effects=pltpu.SideEffectType.DATAFLOW_SIDE_EFFECTING),
    )(*sums, *parts, *sems, *_as_tuple(after))
    return list(out[n:])


def _sum_chips(parts, pos, name, after=()):
    n = len(parts)
    after = _as_tuple(after)

    def body(pos_ref, *refs):
        outs = refs[n + len(after):]
        for a in range(n):
            p_ref = refs[a]
            outs[a][0] = (((p_ref[0].astype(F32) + p_ref[1].astype(F32)) + p_ref[2].astype(F32))
                          + p_ref[3].astype(F32))

    out = pl.pallas_call(
        body, name=name,
        grid_spec=pltpu.PrefetchScalarGridSpec(
            num_scalar_prefetch=1, grid=(1,),
            in_specs=[pl.BlockSpec(p.shape, lambda i, q: (0, 0, 0)) for p in parts] + [ANY] * len(after),
            out_specs=[pl.BlockSpec((1,) + p.shape[1:], lambda i, q: (q[1], 0, 0)) for p in parts]),
        out_shape=[jax.ShapeDtypeStruct((2,) + p.shape[1:], F32) for p in parts],
        compiler_params=_params(("arbitrary",)),
    )(pos, *parts, *after)
    return list(out)


def _join_halves(fulls, name, after=()):
    n = len(fulls)

    def body(*refs):
        outs = refs[n:2 * n]
        send_sem, recv_sem = refs[2 * n:]
        x, y, c = _mesh_pos()
        _sibling_handshake()

        def half(a, which):
            return pltpu.make_async_remote_copy(
                src_ref=outs[a].at[which], dst_ref=outs[a].at[which],
                send_sem=send_sem.at[a], recv_sem=recv_sem.at[a],
                device_id=(x, y, 1 - c), device_id_type=MESH)

        sends = [half(a, c) for a in range(n)]
        for cp in sends:
            cp.start()
        for a in range(n):
            half(a, 1 - c).wait_recv()
        for cp in sends:
            cp.wait_send()

    out_shape = [jax.ShapeDtypeStruct(f.shape, f.dtype) for f in fulls]
    return _tied_call(
        body, after, name=name,
        in_specs=[ANY] * n, out_specs=[ANY] * n, out_shape=out_shape,
        input_output_aliases={a: a for a in range(n)},
        scratch_shapes=[pltpu.SemaphoreType.DMA((n,))] * 2,
        compiler_params=pltpu.CompilerParams(collective_id=SIBLING_COLLECTIVE_ID),
    )(*fulls)


def _norm_in(x, g, ts, after=()):
    s = x.shape[0]

    def body(x_ref, g_ref, hn_ref):
        xv = x_ref[...]
        hn_ref[...] = (xv * _rms_stats(xv) * g_ref[...]).astype(BF16)

    row = pl.BlockSpec((ts, D_MODEL), lambda i: (i, 0))
    return _tied_call(
        body, after, name="norm_in", grid=(s // ts,),
        in_specs=[row, pl.BlockSpec((1, D_MODEL), lambda i: (0, 0))], out_specs=row,
        out_shape=jax.ShapeDtypeStruct((s, D_MODEL), BF16),
        compiler_params=_params(("parallel",)),
    )(x, g)


def _shift_rows(buf, shifted, t):
    rows = t + CONV_HALO - SUBLANES
    for r in range(1, SUBLANES):
        shifted[r - 1, 0:rows, :] = buf[pl.ds(r, rows), :]


def _window(buf, shifted, offset, t):
    r = offset % SUBLANES
    if r == 0:
        return buf[pl.ds(offset, t), :]
    return shifted[r - 1, pl.ds(offset - r, t), :]


def _lane_is_low_head():
    lane = lax.broadcasted_iota(jnp.int32, (1, GM_WIDTH), 1)
    return (lane & GM_HEAD_DIM) == 0


def _gm_mix(v_lo, v_hi, wpair_ref, bias_ref, mixed_ref, t):
    for n in range(t // CHUNK):
        rows = slice(n * CHUNK, (n + 1) * CHUNK)
        for j in range(GM_HEADS // 2):
            cols = slice(j * LANES, (j + 1) * LANES)
            rhs = jnp.concatenate([v_lo[rows, cols], v_hi[rows, cols]], axis=0)
            mixed_ref[rows, cols] = _dot(wpair_ref[j], rhs) + bias_ref[:, cols]


def _seqmix_fwd(hn, w_in, b_in, cw, cb, lng, lnb, gg, gb, wpair, bias, t, after=()):
    s = hn.shape[0]

    def body(hn_ref, w_ref, b_ref, cw_ref, cb_ref, lng_ref, lnb_ref, gg_ref, gb_ref, wpair_ref, bias_ref,
             z_ref, mix_ref, c1_ref, abuf, ash, mixed_ref):
        i = pl.program_id(0)

        @pl.when(i == 0)
        def _():
            abuf[0:CONV_HALO, :] = jnp.zeros((CONV_HALO, CONV_WIDTH), F32)

        @pl.when(i > 0)
        def _():
            abuf[0:CONV_HALO, :] = abuf[t:t + CONV_HALO, :]

        hv = hn_ref[...]
        for j in range(4):
            cols = slice(j * 512, (j + 1) * 512)
            z_ref[:, cols] = _dot(hv, w_ref[j]) + b_ref[:, cols]

        abuf[CONV_HALO:, :] = z_ref[:, 0:512] * _sigmoid(z_ref[:, 512:1024])
        _shift_rows(abuf, ash, t)
        acc = jnp.zeros((t, CONV_WIDTH), F32)
        for k in range(CONV_KERNEL):
            acc = acc + cw_ref[k:k + 1, :] * _window(abuf, ash, CONV_HALO - (CONV_KERNEL - 1) + k, t)
        c1 = acc + cb_ref[...]
        c1_ref[...] = c1
        xh, _ = _ln_stats(c1)
        ln = xh * lng_ref[...] + lnb_ref[...]
        mix_ref[:, 0:512] = (ln * _sigmoid(ln)).astype(BF16)

        u, _ = _gelu_parts(z_ref[:, 1024:1536])
        gv, _ = _gelu_parts(z_ref[:, 1536:2048])
        vxh, _ = _ln_stats(gv)
        v = vxh * gg_ref[...] + gb_ref[...]
        low = _lane_is_low_head()
        v_lo = jnp.where(low, v, 0.0).astype(BF16)
        v_hi = jnp.where(low, 0.0, v).astype(BF16)
        _gm_mix(v_lo, v_hi, wpair_ref, bias_ref, mixed_ref, t)
        mix_ref[:, 512:1024] = (u * mixed_ref[...]).astype(BF16)

    vec = lambda n: pl.BlockSpec((1, n), lambda i: (0, 0))
    return _tied_call(
        body, after, name="seqmix_fwd", grid=(s // t,),
        in_specs=[pl.BlockSpec((t, D_MODEL), lambda i: (i, 0)),
                  pl.BlockSpec((4, D_MODEL, 512), lambda i: (0, 0, 0)), vec(2048),
                  pl.BlockSpec((CONV_HALO, CONV_WIDTH), lambda i: (0, 0)),
                  vec(512), vec(512), vec(512), vec(512), vec(512),
                  pl.BlockSpec((4, CHUNK, 2 * CHUNK), lambda i: (0, 0, 0)),
                  pl.BlockSpec((CHUNK, GM_WIDTH), lambda i: (0, 0))],
        out_specs=[pl.BlockSpec((t, 2048), lambda i: (i, 0)),
                   pl.BlockSpec((t, D_MODEL), lambda i: (i, 0)),
                   pl.BlockSpec((t, CONV_WIDTH), lambda i: (i, 0))],
        out_shape=[jax.ShapeDtypeStruct((s, 2048), F32), jax.ShapeDtypeStruct((s, D_MODEL), BF16),
                   jax.ShapeDtypeStruct((s, CONV_WIDTH), F32)],
        scratch_shapes=[pltpu.VMEM((t + CONV_HALO, CONV_WIDTH), F32),
                        pltpu.VMEM((SUBLANES - 1, t + CONV_HALO - SUBLANES, CONV_WIDTH), F32),
                        pltpu.VMEM((t, GM_WIDTH), F32)],
        compiler_params=_params(("arbitrary",)),
    )(hn, w_in, b_in, cw, cb, lng, lnb, gg, gb, wpair, bias)


def _mem_kv(mem, g, wkv):
    m = mem.shape[0]

    def body(mem_ref, g_ref, w_ref, mn_ref, kv_ref):
        mv = mem_ref[...]
        mn = (mv * _rms_stats(mv) * g_ref[...]).astype(BF16)
        mn_ref[...] = mn
        for j in range(4):
            kv_ref[:, j * 512:(j + 1) * 512] = _dot(mn, w_ref[j]).astype(BF16)

    return pl.pallas_call(
        body, name="mem_kv",
        out_shape=[jax.ShapeDtypeStruct((m, D_MODEL), BF16), jax.ShapeDtypeStruct((m, 2 * D_MODEL), BF16)],
        compiler_params=pltpu.CompilerParams(vmem_limit_bytes=VMEM_LIMIT_BYTES),
    )(mem, g, wkv)


def _softmax_rows(sc):
    e = jnp.exp(sc - jnp.max(sc, axis=-1, keepdims=True))
    return e / jnp.sum(e, axis=-1, keepdims=True)


def _attn_block_fwd(x, mix, w_out, g_xa, wq, kv, wo, g_ffn, ts, after=()):
    s, m = x.shape[0], kv.shape[0]
    scale = XA_HEAD_DIM ** -0.5

    def body(x_ref, mix_ref, wout_ref, gxa_ref, wq_ref, kv_ref, wo_ref, gffn_ref,
             h1_ref, hn2_ref, q_ref, o_ref, h2_ref, hn3_ref):
        h1 = x_ref[...] + _dot(mix_ref[...], wout_ref[...])
        h1_ref[...] = h1
        hn2 = (h1 * _rms_stats(h1) * gxa_ref[...]).astype(BF16)
        hn2_ref[...] = hn2
        q_ref[...] = _dot(hn2, wq_ref[...]).astype(BF16)
        for h in range(XA_HEADS):
            cols = slice(h * XA_HEAD_DIM, (h + 1) * XA_HEAD_DIM)
            vcols = slice(D_MODEL + h * XA_HEAD_DIM, D_MODEL + (h + 1) * XA_HEAD_DIM)
            p = _softmax_rows(_dot_nt(q_ref[:, cols], kv_ref[:, cols]) * scale)
            o_ref[:, cols] = _dot(p.astype(BF16), kv_ref[:, vcols]).astype(BF16)
        h2 = h1 + _dot(o_ref[...], wo_ref[...])
        h2_ref[...] = h2
        hn3_ref[...] = (h2 * _rms_stats(h2) * gffn_ref[...]).astype(BF16)

    row = pl.BlockSpec((ts, D_MODEL), lambda i: (i, 0))
    full = pl.BlockSpec((D_MODEL, D_MODEL), lambda i: (0, 0))
    vec = pl.BlockSpec((1, D_MODEL), lambda i: (0, 0))
    f32 = jax.ShapeDtypeStruct((s, D_MODEL), F32)
    bf16 = jax.ShapeDtypeStruct((s, D_MODEL), BF16)
    return _tied_call(
        body, after, name="attn_block_fwd", grid=(s // ts,),
        in_specs=[row, row, full, vec, full, pl.BlockSpec((m, 2 * D_MODEL), lambda i: (0, 0)), full, vec],
        out_specs=[row] * 6,
        out_shape=[f32, bf16, bf16, bf16, f32, bf16],
        compiler_params=_params(("parallel",)),
    )(x, mix, w_out, g_xa, wq, kv, wo, g_ffn)


_FFN_CHUNKS_FWD = (slice(0, 6 * LANES), slice(6 * LANES, FFN_HALF))
_FFN_CHUNKS_BWD = (slice(0, 4 * LANES), slice(4 * LANES, 8 * LANES), slice(8 * LANES, FFN_HALF))


def _ffn_up(hn, wgu, ts, after=()):
    s = hn.shape[0]

    def body(hn_ref, w_ref, gu_ref, act_ref):
        hv = hn_ref[...]
        for cols in _FFN_CHUNKS_FWD:
            gate = _dot(hv, w_ref[0, 0, :, cols])
            up = _dot(hv, w_ref[1, 0, :, cols])
            gu_ref[0, :, cols] = gate.astype(BF16)
            gu_ref[1, :, cols] = up.astype(BF16)
            act_ref[:, cols] = (gate * _sigmoid(gate) * up).astype(BF16)

    return _tied_call(
        body, after, name="ffn_up", grid=(2, s // ts),
        in_specs=[pl.BlockSpec((ts, D_MODEL), lambda j, i: (i, 0)),
                  pl.BlockSpec((2, 1, D_MODEL, FFN_HALF), lambda j, i: (0, j, 0, 0))],
        out_specs=[pl.BlockSpec((2, ts, FFN_HALF), lambda j, i: (0, i, j)),
                   pl.BlockSpec((ts, FFN_HALF), lambda j, i: (i, j))],
        out_shape=[jax.ShapeDtypeStruct((2, s, FFN_HIDDEN), BF16), jax.ShapeDtypeStruct((s, FFN_HIDDEN), BF16)],
        compiler_params=_params(("parallel", "parallel")),
    )(hn, wgu)


def _ffn_down_loss(act, wd, h2, g, target, ts):
    s = act.shape[0]

    def body(act_ref, wd_ref, h2_ref, g_ref, t_ref, dh_ref, dhb_ref, sq_ref, dg_ref):
        @pl.when(pl.program_id(0) == 0)
        def _():
            sq_ref[...] = jnp.zeros_like(sq_ref)
            dg_ref[...] = jnp.zeros_like(dg_ref)

        h3 = h2_ref[...] + _dot(act_ref[...], wd_ref[...])
        r = _rms_stats(h3)
        gv = g_ref[...]
        diff = h3 * r * gv - t_ref[...]
        sq_ref[...] += _rowsum(diff * diff)
        dh, dg = _rms_bwd(diff / D_MODEL, h3, r, gv)
        dh_ref[...] = dh
        dhb_ref[...] = dh.astype(BF16)
        dg_ref[...] += dg

    row = pl.BlockSpec((ts, D_MODEL), lambda i: (i, 0))
    vec = pl.BlockSpec((1, D_MODEL), lambda i: (0, 0))
    return pl.pallas_call(
        body, name="ffn_down_loss", grid=(s // ts,),
        in_specs=[pl.BlockSpec((ts, FFN_HIDDEN), lambda i: (i, 0)),
                  pl.BlockSpec((FFN_HIDDEN, D_MODEL), lambda i: (0, 0)), row, vec, row],
        out_specs=[row, row, vec, vec],
        out_shape=[jax.ShapeDtypeStruct((s, D_MODEL), F32), jax.ShapeDtypeStruct((s, D_MODEL), BF16),
                   jax.ShapeDtypeStruct((1, D_MODEL), F32), jax.ShapeDtypeStruct((1, D_MODEL), F32)],
        compiler_params=_params(("arbitrary",)),
    )(act, wd, h2, g, target)


def _grad_w(a, b, tk, tn, name, after=()):
    s, k = a.shape
    gb, _, n = b.shape
    nblk = n // tn
    tsr = GRAD_ROWS if s % GRAD_ROWS == 0 else s

    def body(a_ref, b_ref, o_ref):
        part = _dot_tn(a_ref[...], b_ref[0])

        @pl.when(pl.program_id(2) == 0)
        def _():
            o_ref[0] = part

        @pl.when(pl.program_id(2) > 0)
        def _():
            o_ref[0] += part

    return _tied_call(
        body, after, name=name, grid=(gb * nblk, k // tk, s // tsr),
        in_specs=[pl.BlockSpec((tsr, tk), lambda ni, ki, si: (si, ki)),
                  pl.BlockSpec((1, tsr, tn), lambda ni, ki, si: (ni // nblk, si, ni % nblk))],
        out_specs=pl.BlockSpec((1, tk, tn), lambda ni, ki, si: (ni, ki, 0)),
        out_shape=jax.ShapeDtypeStruct((gb * nblk, k, tn), F32),
        compiler_params=_params(("parallel", "parallel", "arbitrary")),
    )(a, b)


def _ffn_bwd(dh3, wd, gu, wgu, h2, g, t, after=()):
    s = dh3.shape[0]

    def body(dh3_ref, wd_ref, gu_ref, w_ref, h2_ref, g_ref, dgu_ref, dh2_ref, dh2b_ref, dg_ref):
        @pl.when(pl.program_id(0) == 0)
        def _():
            dg_ref[...] = jnp.zeros_like(dg_ref)

        dh3v = dh3_ref[...]
        dhb = dh3v.astype(BF16)
        dhn = jnp.zeros((t, D_MODEL), F32)
        for j in range(2):
            for cols in _FFN_CHUNKS_BWD:
                whole = slice(j * FFN_HALF + cols.start, j * FFN_HALF + cols.stop)
                dact = _dot_nt(dhb, wd_ref[j, cols, :])
                gate, up = gu_ref[0, :, whole].astype(F32), gu_ref[1, :, whole].astype(F32)
                sg = _sigmoid(gate)
                dgate = (dact * up * (sg * (1.0 + gate * (1.0 - sg)))).astype(BF16)
                dup = (dact * (gate * sg)).astype(BF16)
                dgu_ref[0, :, whole] = dgate
                dgu_ref[1, :, whole] = dup
                dhn = dhn + _dot_nt(dgate, w_ref[j, :, cols]) + _dot_nt(dup, w_ref[2 + j, :, cols])
        h2 = h2_ref[...]
        dv, dg = _rms_bwd(dhn, h2, _rms_stats(h2), g_ref[...])
        dh2 = dh3v + dv
        dh2_ref[...] = dh2
        dh2b_ref[...] = dh2.astype(BF16)
        dg_ref[...] += dg

    row = pl.BlockSpec((t, D_MODEL), lambda i: (i, 0))
    wide = pl.BlockSpec((2, t, FFN_HIDDEN), lambda i: (0, i, 0))
    vec = pl.BlockSpec((1, D_MODEL), lambda i: (0, 0))
    return _tied_call(
        body, after, name="ffn_bwd", grid=(s // t,),
        in_specs=[row, pl.BlockSpec((2, FFN_HALF, D_MODEL), lambda i: (0, 0, 0)), wide,
                  pl.BlockSpec((4, D_MODEL, FFN_HALF), lambda i: (0, 0, 0)), row, vec],
        out_specs=[wide, row, row, vec],
        out_shape=[jax.ShapeDtypeStruct((2, s, FFN_HIDDEN), BF16), jax.ShapeDtypeStruct((s, D_MODEL), F32),
                   jax.ShapeDtypeStruct((s, D_MODEL), BF16), jax.ShapeDtypeStruct((1, D_MODEL), F32)],
        compiler_params=_params(("arbitrary",)),
    )(dh3, wd, gu, wgu, h2, g)


def _attn_bwd(dh2, wo, q, kv, wq, h1, g, ts, after=()):
    s, m = q.shape[0], kv.shape[0]
    scale = XA_HEAD_DIM ** -0.5

    def body(dh2_ref, wo_ref, q_ref, kv_ref, wq_ref, h1_ref, g_ref, dh1_ref, dh1b_ref, dq_ref, dkv_ref, dg_ref):
        @pl.when(pl.program_id(0) == 0)
        def _():
            dkv_ref[...] = jnp.zeros_like(dkv_ref)
            dg_ref[...] = jnp.zeros_like(dg_ref)

        do = _dot_nt(dh2_ref[...].astype(BF16), wo_ref[...]).astype(BF16)
        for h in range(XA_HEADS):
            cols = slice(h * XA_HEAD_DIM, (h + 1) * XA_HEAD_DIM)
            vcols = slice(D_MODEL + h * XA_HEAD_DIM, D_MODEL + (h + 1) * XA_HEAD_DIM)
            qh, kh, vh, doh = q_ref[:, cols], kv_ref[:, cols], kv_ref[:, vcols], do[:, cols]
            p = _softmax_rows(_dot_nt(qh, kh) * scale)
            dp = _dot_nt(doh, vh)
            ds = (p * (dp - jnp.sum(dp * p, axis=-1, keepdims=True)) * scale).astype(BF16)
            dq_ref[:, cols] = _dot(ds, kh).astype(BF16)
            dkv_ref[:, cols] += _dot_tn(ds, qh)
            dkv_ref[:, vcols] += _dot_tn(p.astype(BF16), doh)
        dhn = _dot_nt(dq_ref[...], wq_ref[...])
        h1 = h1_ref[...]
        dv, dg = _rms_bwd(dhn, h1, _rms_stats(h1), g_ref[...])
        dh1 = dh2_ref[...] + dv
        dh1_ref[...] = dh1
        dh1b_ref[...] = dh1.astype(BF16)
        dg_ref[...] += dg

    row = pl.BlockSpec((ts, D_MODEL), lambda i: (i, 0))
    full = pl.BlockSpec((D_MODEL, D_MODEL), lambda i: (0, 0))
    kvs = pl.BlockSpec((m, 2 * D_MODEL), lambda i: (0, 0))
    vec = pl.BlockSpec((1, D_MODEL), lambda i: (0, 0))
    return _tied_call(
        body, after, name="attn_bwd", grid=(s // ts,),
        in_specs=[row, full, row, kvs, full, row, vec],
        out_specs=[row, row, row, kvs, vec],
        out_shape=[jax.ShapeDtypeStruct((s, D_MODEL), F32), jax.ShapeDtypeStruct((s, D_MODEL), BF16),
                   jax.ShapeDtypeStruct((s, D_MODEL), BF16),
                   jax.ShapeDtypeStruct((m, 2 * D_MODEL), F32), jax.ShapeDtypeStruct((1, D_MODEL), F32)],
        compiler_params=_params(("arbitrary",)),
    )(dh2, wo, q, kv, wq, h1, g)


def _mem_kv_bwd(dkv, mn, wkv, mem, g, after=()):
    m = mem.shape[0]

    def body(dkv_ref, mn_ref, w_ref, mem_ref, g_ref, dw_ref, dg_ref):
        dmn = jnp.zeros((m, D_MODEL), F32)
        mn = mn_ref[...]
        for j in range(4):
            dj = dkv_ref[:, j * 512:(j + 1) * 512].astype(BF16)
            dw_ref[j] = _dot_tn(mn, dj)
            dmn = dmn + _dot_nt(dj, w_ref[j])
        mv = mem_ref[...]
        dg_ref[...] = _rowsum(dmn * (mv * _rms_stats(mv)))

    return _tied_call(
        body, after, name="mem_kv_bwd", in_specs=[pl.BlockSpec(memory_space=pltpu.VMEM)] * 5,
        out_shape=[jax.ShapeDtypeStruct((4, D_MODEL, 512), F32), jax.ShapeDtypeStruct((1, D_MODEL), F32)],
        compiler_params=pltpu.CompilerParams(vmem_limit_bytes=VMEM_LIMIT_BYTES),
    )(dkv, mn, wkv, mem, g)


def _seqmix_bwd(dh1, x, z, c1, w_out, w_in, g_mix, cw, lng, lnb, gg, gb, wpair, wpair_t, bias, t, after=()):
    s = x.shape[0]
    nt = s // t

    def body(dh1_ref, x_ref, z_ref, c1_ref, wo_ref, wi_ref, gm_ref, cw_ref, lng_ref, lnb_ref,
             gg_ref, gb_ref, wpair_ref, wpt_ref, bias_ref,
             gx_ref, dz_ref, dcw_ref, dcb_ref, dlng_ref, dlnb_ref, dgg_ref, dgb_ref, dws_ref, dbs_ref,
             dbin_ref, dgm_ref, dbuf, dsh, mixed_ref, dv_ref):
        i = pl.program_id(0)
        accs = (dcw_ref, dcb_ref, dlng_ref, dlnb_ref, dgg_ref, dgb_ref, dws_ref, dbs_ref, dbin_ref, dgm_ref)

        @pl.when(i == 0)
        def _():
            for r in accs:
                r[...] = jnp.zeros_like(r)
            dbuf[t:t + CONV_HALO, :] = jnp.zeros((CONV_HALO, CONV_WIDTH), F32)

        @pl.when(i > 0)
        def _():
            dbuf[t:t + CONV_HALO, :] = dbuf[0:CONV_HALO, :]

        dmix = _dot_nt(dh1_ref[...].astype(BF16), wo_ref[...])

        xh, rs = _ln_stats(c1_ref[...])
        lng = lng_ref[...]
        ln = xh * lng + lnb_ref[...]
        sl = _sigmoid(ln)
        dln = dmix[:, 0:512] * (sl * (1.0 + ln * (1.0 - sl)))
        dc1, dg_ln, db_ln = _ln_bwd(dln, xh, rs, lng)
        dlng_ref[...] += dg_ln
        dlnb_ref[...] += db_ln
        dcb_ref[...] += _rowsum(dc1)
        dbuf[0:t, :] = dc1

        za = z_ref[:, 0:512]
        sg = _sigmoid(z_ref[:, 512:1024])
        a = za * sg
        _shift_rows(dbuf, dsh, t)

        da = jnp.zeros((t, CONV_WIDTH), F32)
        for k in range(CONV_KERNEL):
            later = _window(dbuf, dsh, CONV_KERNEL - 1 - k, t)
            da = da + cw_ref[k:k + 1, :] * later
            dcw_ref[k:k + 1, :] += _rowsum(a * later)
        dza = da * sg
        dzg = da * za * (sg * (1.0 - sg))
        dz_ref[:, 0:512] = dza.astype(BF16)
        dz_ref[:, 512:1024] = dzg.astype(BF16)
        dbin_ref[:, 0:512] += _rowsum(dza)
        dbin_ref[:, 512:1024] += _rowsum(dzg)

        dgm = dmix[:, 512:1024]
        u, du_dz = _gelu_parts(z_ref[:, 1024:1536])
        gv, dgv_dz = _gelu_parts(z_ref[:, 1536:2048])
        vxh, vrs = _ln_stats(gv)
        ggv = gg_ref[...]
        v = vxh * ggv + gb_ref[...]
        low = _lane_is_low_head()
        v_lo = jnp.where(low, v, 0.0).astype(BF16)
        v_hi = jnp.where(low, 0.0, v).astype(BF16)
        _gm_mix(v_lo, v_hi, wpair_ref, bias_ref, mixed_ref, t)
        dzu = dgm * mixed_ref[...] * du_dz
        dm = dgm * u
        dm_lo = jnp.where(low, dm, 0.0).astype(BF16)
        dm_hi = jnp.where(low, 0.0, dm).astype(BF16)
        vb = v.astype(BF16)
        tril = (lax.broadcasted_iota(jnp.int32, (CHUNK, CHUNK), 1)
                <= lax.broadcasted_iota(jnp.int32, (CHUNK, CHUNK), 0))
        for n in range(t // CHUNK):
            rows = slice(n * CHUNK, (n + 1) * CHUNK)
            dbs_ref[...] += dm[rows, :]
            for j in range(GM_HEADS // 2):
                cols = slice(j * LANES, (j + 1) * LANES)
                stack = jnp.concatenate([dm_lo[rows, cols], dm_hi[rows, cols]], axis=0)
                dws = _dot_nt(stack, vb[rows, cols])
                dws_ref[2 * j] += jnp.where(tril, dws[0:CHUNK], 0.0)
                dws_ref[2 * j + 1] += jnp.where(tril, dws[CHUNK:2 * CHUNK], 0.0)
                dv_ref[rows, cols] = _dot(wpt_ref[j], stack)
        dgv, dg_gm, db_gm = _ln_bwd(dv_ref[...], vxh, vrs, ggv)
        dgg_ref[...] += dg_gm
        dgb_ref[...] += db_gm
        dzv = dgv * dgv_dz
        dz_ref[:, 1024:1536] = dzu.astype(BF16)
        dz_ref[:, 1536:2048] = dzv.astype(BF16)
        dbin_ref[:, 1024:1536] += _rowsum(dzu)
        dbin_ref[:, 1536:2048] += _rowsum(dzv)

        dhn = jnp.zeros((t, D_MODEL), F32)
        for j in range(4):
            dhn = dhn + _dot_nt(dz_ref[:, j * 512:(j + 1) * 512], wi_ref[j])
        xv = x_ref[...]
        dv, dg = _rms_bwd(dhn, xv, _rms_stats(xv), gm_ref[...])
        gx_ref[...] = dh1_ref[...] + dv
        dgm_ref[...] += dg

    rev = lambda w: pl.BlockSpec((t, w), lambda i: (nt - 1 - i, 0))
    const = lambda *shape: pl.BlockSpec(shape, lambda i: (0,) * len(shape))
    f32 = lambda *shape: jax.ShapeDtypeStruct(shape, F32)
    return _tied_call(
        body, after, name="seqmix_bwd", grid=(nt,),
        in_specs=[rev(D_MODEL), rev(D_MODEL), rev(2048), rev(CONV_WIDTH),
                  const(D_MODEL, D_MODEL), const(4, D_MODEL, 512), const(1, D_MODEL),
                  const(CONV_HALO, CONV_WIDTH), const(1, 512), const(1, 512), const(1, 512), const(1, 512),
                  const(4, CHUNK, 2 * CHUNK), const(4, CHUNK, 2 * CHUNK), const(CHUNK, GM_WIDTH)],
        out_specs=[rev(D_MODEL), rev(2048),
                   const(CONV_HALO, CONV_WIDTH), const(1, 512), const(1, 512), const(1, 512), const(1, 512),
                   const(1, 512), const(GM_HEADS, CHUNK, CHUNK), const(CHUNK, GM_WIDTH), const(1, 2048),
                   const(1, D_MODEL)],
        out_shape=[f32(s, D_MODEL), jax.ShapeDtypeStruct((s, 2048), BF16),
                   f32(CONV_HALO, CONV_WIDTH), f32(1, 512), f32(1, 512), f32(1, 512), f32(1, 512),
                   f32(1, 512), f32(GM_HEADS, CHUNK, CHUNK), f32(CHUNK, GM_WIDTH), f32(1, 2048),
                   f32(1, D_MODEL)],
        scratch_shapes=[pltpu.VMEM((t + CONV_HALO, CONV_WIDTH), F32),
                        pltpu.VMEM((SUBLANES - 1, t + CONV_HALO - SUBLANES, CONV_WIDTH), F32),
                        pltpu.VMEM((t, GM_WIDTH), F32), pltpu.VMEM((t, GM_WIDTH), F32)],
        compiler_params=_params(("arbitrary",)),
    )(dh1, x, z, c1, w_out, w_in, g_mix, cw, lng, lnb, gg, gb, wpair, wpair_t, bias)


def _head_bias_grad(dbs):
    def body(d_ref, o_ref):
        dv = d_ref[...]
        lane = lax.broadcasted_iota(jnp.int32, (CHUNK, LANES), 1)
        acc = jnp.zeros((CHUNK, LANES), F32)
        for h in range(GM_HEADS):
            sh = jnp.sum(dv[:, h * GM_HEAD_DIM:(h + 1) * GM_HEAD_DIM], axis=-1, keepdims=True)
            acc = acc + jnp.where(lane == h, sh, 0.0)
        o_ref[...] = acc

    return pl.pallas_call(body, name="head_bias_grad",
                          out_shape=jax.ShapeDtypeStruct((CHUNK, LANES), F32))(dbs)


def kernel(x, mem, norm_mix_g, w_in, b_in, conv_w, conv_b, conv_ln_g, conv_ln_b, gm_ln_g, gm_ln_b, gm_w_s, gm_b_s, w_out, norm_xa_g, mem_norm_g, xa_wq, xa_wkv, xa_wo, norm_ffn_g, ffn_w_gate_up, ffn_w_down, final_norm_g, loss_target, m_norm_mix_g, m_w_in, m_b_in, m_conv_w, m_conv_b, m_conv_ln_g, m_conv_ln_b, m_gm_ln_g, m_gm_ln_b, m_gm_w_s, m_gm_b_s, m_w_out, m_norm_xa_g, m_mem_norm_g, m_xa_wq, m_xa_wkv, m_xa_wo, m_norm_ffn_g, m_ffn_w_gate_up, m_ffn_w_down, m_final_norm_g, v_norm_mix_g, v_w_in, v_b_in, v_conv_w, v_conv_b, v_conv_ln_g, v_conv_ln_b, v_gm_ln_g, v_gm_ln_b, v_gm_w_s, v_gm_b_s, v_w_out, v_norm_xa_g, v_mem_norm_g, v_xa_wq, v_xa_wkv, v_xa_wo, v_norm_ffn_g, v_ffn_w_gate_up, v_ffn_w_down, v_final_norm_g):
    weights = dict(norm_mix_g=norm_mix_g, w_in=w_in, b_in=b_in, conv_w=conv_w, conv_b=conv_b, conv_ln_g=conv_ln_g,
                   conv_ln_b=conv_ln_b, gm_ln_g=gm_ln_g, gm_ln_b=gm_ln_b, gm_w_s=gm_w_s, gm_b_s=gm_b_s, w_out=w_out,
                   norm_xa_g=norm_xa_g, mem_norm_g=mem_norm_g, xa_wq=xa_wq, xa_wkv=xa_wkv, xa_wo=xa_wo,
                   norm_ffn_g=norm_ffn_g, ffn_w_gate_up=ffn_w_gate_up, ffn_w_down=ffn_w_down,
                   final_norm_g=final_norm_g)
    m_in = dict(norm_mix_g=m_norm_mix_g, w_in=m_w_in, b_in=m_b_in, conv_w=m_conv_w, conv_b=m_conv_b,
                conv_ln_g=m_conv_ln_g, conv_ln_b=m_conv_ln_b, gm_ln_g=m_gm_ln_g, gm_ln_b=m_gm_ln_b, gm_w_s=m_gm_w_s,
                gm_b_s=m_gm_b_s, w_out=m_w_out, norm_xa_g=m_norm_xa_g, mem_norm_g=m_mem_norm_g, xa_wq=m_xa_wq,
                xa_wkv=m_xa_wkv, xa_wo=m_xa_wo, norm_ffn_g=m_norm_ffn_g, ffn_w_gate_up=m_ffn_w_gate_up,
                ffn_w_down=m_ffn_w_down, final_norm_g=m_final_norm_g)
    v_in = dict(norm_mix_g=v_norm_mix_g, w_in=v_w_in, b_in=v_b_in, conv_w=v_conv_w, conv_b=v_conv_b,
                conv_ln_g=v_conv_ln_g, conv_ln_b=v_conv_ln_b, gm_ln_g=v_gm_ln_g, gm_ln_b=v_gm_ln_b, gm_w_s=v_gm_w_s,
                gm_b_s=v_gm_b_s, w_out=v_w_out, norm_xa_g=v_norm_xa_g, mem_norm_g=v_mem_norm_g, xa_wq=v_xa_wq,
                xa_wkv=v_xa_wkv, xa_wo=v_xa_wo, norm_ffn_g=v_norm_ffn_g, ffn_w_gate_up=v_ffn_w_gate_up,
                ffn_w_down=v_ffn_w_down, final_norm_g=v_final_norm_g)
    grads, delta, new_m, new_v = {}, {}, {}, {}

    s = x.shape[1]
    ts = _row_tile(s)
    tb = max(CHUNK, ts // 2)
    tw = 2 * ts if s % (2 * ts) == 0 and ts >= 512 else ts
    cx, cy, cc = _mesh_pos()
    chip = 2 * cx + cy
    pos = jnp.stack([chip, cc]).astype(jnp.int32)
    row = lambda a: a.reshape(1, -1)
    x2, mem2, tgt2 = x[0], mem[0], loss_target[0]

    big = dict(w_in=w_in, xa_wkv=xa_wkv, w_out=w_out, xa_wq=xa_wq, xa_wo=xa_wo,
               ffn_w_gate_up=ffn_w_gate_up, ffn_w_down=ffn_w_down)
    big_names = list(big)
    halves = lambda a: a.reshape(2, a.shape[0] // 2, a.shape[1])
    conv_w_pad = jnp.pad(conv_w, ((0, CONV_HALO - CONV_KERNEL), (0, 0)))
    first_names = ["w_in", "conv_w"]
    later_names = [nm for nm in big_names if nm != "w_in"]
    cast = dict(zip(first_names, _cast_into_slots([halves(w_in), halves(conv_w_pad)], pos, [BF16, F32], "cast_w_in")))
    cast.update(zip(later_names, _cast_into_slots([halves(big[nm]) for nm in later_names], pos,
                                                  [BF16] * len(later_names), "cast_" + later_names[0])))

    def start_gather(names, after):
        return _gather_start([cast[nm] for nm in names], "gather_start_" + names[0], after)

    def land_gather(names, started, after):
        send_sems, recv_sems, bufs, _ = started
        return _gather_wait(send_sems, recv_sems, bufs, after, "gather_wait_" + names[0])

    def share_gather(names, landed, after=()):
        return dict(zip(names, (b.reshape(N_CHIPS, -1, b.shape[-1])
                                for b in _pass_to_sibling(landed, "pass_" + names[0], after))))

    attn_names = ["w_out", "xa_wq", "xa_wkv", "xa_wo"]
    gather_first = start_gather(first_names, ())
    hn1 = _norm_in(x2, row(norm_mix_g), tw, after=gather_first[3])
    landed = land_gather(first_names, gather_first, [cast[nm] for nm in later_names] + [hn1])
    gather_attn = start_gather(attn_names, landed)
    gw = share_gather(first_names, landed, gather_attn[3])
    w_in_g = gw["w_in"]
    cw_g = jnp.concatenate([gw["conv_w"][k] for k in range(N_CHIPS)], axis=1)

    tril = jnp.tril(jnp.ones((CHUNK, CHUNK), dtype=bool))
    ws = jnp.where(tril[None], gm_w_s, 0.0)
    wpair = jnp.concatenate([ws[0::2], ws[1::2]], axis=2).astype(BF16)
    ws_t = jnp.swapaxes(ws, 1, 2)
    wpair_t = jnp.concatenate([ws_t[0::2], ws_t[1::2]], axis=2).astype(BF16)
    bias = jnp.repeat(gm_b_s.T, GM_HEAD_DIM, axis=1)

    z, mix, c1 = _seqmix_fwd(hn1, w_in_g, row(b_in), cw_g, row(conv_b), row(conv_ln_g), row(conv_ln_b),
                             row(gm_ln_g), row(gm_ln_b), wpair, bias, ts)
    landed = land_gather(attn_names, gather_attn, mix)
    gather_gu = start_gather(["ffn_w_gate_up"], landed)
    gw = share_gather(attn_names, landed, gather_gu[3])
    w_out_g = gw["w_out"].reshape(D_MODEL, D_MODEL)
    wq_g = gw["xa_wq"].reshape(D_MODEL, D_MODEL)
    wkv_g = gw["xa_wkv"]
    wo_g = gw["xa_wo"].reshape(D_MODEL, D_MODEL)
    mn, kv = _mem_kv(mem2, row(mem_norm_g), wkv_g)
    h1, hn2, q, o, h2, hn3 = _attn_block_fwd(x2, mix, w_out_g, row(norm_xa_g), wq_g, kv, wo_g, row(norm_ffn_g), ts)
    landed = land_gather(["ffn_w_gate_up"], gather_gu, hn3)
    gather_down = start_gather(["ffn_w_down"], landed)
    wgu_g = share_gather(["ffn_w_gate_up"], landed, gather_down[3])["ffn_w_gate_up"]
    gu, act = _ffn_up(hn3, wgu_g.reshape(2, 2, D_MODEL, FFN_HALF), tw)
    landed = land_gather(["ffn_w_down"], gather_down, act)
    wd_g = share_gather(["ffn_w_down"], landed)["ffn_w_down"].reshape(FFN_HIDDEN, D_MODEL)
    dh3, dh3_b, sq, d_final_g = _ffn_down_loss(act, wd_g, h2, row(final_norm_g), tgt2, ts)
    loss_here = jnp.broadcast_to(0.5 * jnp.sum(sq) / D_MODEL, (1, 2, SUBLANES, LANES))

    def split(g, nm):
        r, c = big[nm].shape
        return g.reshape(N_CHIPS, 2, r // 2, c)

    def chip_sums(group, arrays, got):
        sums, parts = [None] * len(group), [None] * len(group)
        for blocks in (N_CHIPS, 1):
            idx = [i for i, a in enumerate(arrays) if a.shape[0] == blocks]
            if idx:
                out = _add_halves([arrays[i] for i in idx], [got[i] for i in idx], pos, "chip_sum_" + group[idx[0]],
                                  [F32 if group[i] == "loss" else BF16 for i in idx])
                for k, i in enumerate(idx):
                    sums[i], parts[i] = out[0][k], out[1][k]
        return sums, parts

    def start_swap(group, grads):
        return _swap_start([split(g, nm) for g, nm in zip(grads, group)], "swap_start_" + group[0])

    def start_exchange(group, swapping, after, landed):
        sems, arrays, lands, _ = swapping
        arrays, got = _swap_wait(sems, arrays, lands, after, "swap_wait_" + group[0])
        sums, parts = chip_sums(group, arrays, got)
        return _exchange_start(sums, parts, "exchange_start_" + group[0], landed)

    def wait_exchange(group, started, after):
        sems, sums, parts, _ = started
        return _exchange_wait(sems, sums, parts, after, "exchange_wait_" + group[0])

    def finish_exchange(group, started, after):
        return _sum_chips(wait_exchange(group, started, after), pos, "total_" + group[0])

    def join_and_update(group, after):
        joined = _join_halves([halves_of[nm] for nm in group], "join_halves_" + group[0], after)
        outs = _adamw([(weights[nm], j.reshape(big[nm].shape), m_in[nm], v_in[nm]) for nm, j in zip(group, joined)],
                      "adamw_" + group[0])
        for nm, out in zip(group, outs):
            grads[nm], delta[nm], new_m[nm], new_v[nm] = out
        return [new_v[nm] for nm in group]

    as3 = lambda a: a.reshape((1,) + a.shape)
    halves_of = {}

    g_down = _grad_w(act, as3(dh3_b), FFN_HALF, D_MODEL, "grad_ffn_w_down")
    group_a = ["ffn_w_down"]
    swap_a = start_swap(group_a, [g_down])
    dgu, dh2, dh2_b, d_ffn_g = _ffn_bwd(dh3, wd_g.reshape(2, FFN_HALF, D_MODEL), gu, wgu_g, h2, row(norm_ffn_g), tb,
                                        after=swap_a[3])
    exch_a = start_exchange(group_a, swap_a, dh2, wd_g)
    g_gu = _grad_w(hn3, dgu, D_MODEL, FFN_HALF, "grad_ffn_w_gate_up", after=exch_a[3])
    halves_of.update(zip(group_a, finish_exchange(group_a, exch_a, g_gu)))

    group_b = ["ffn_w_gate_up"]
    swap_b = start_swap(group_b, [g_gu])
    dh1, dh1_b, dq, dkv, d_xa_g = _attn_bwd(dh2, wo_g, q, kv, wq_g, h1, row(norm_xa_g), ts, after=swap_b[3])
    exch_b = start_exchange(group_b, swap_b, dh1, [halves_of[nm] for nm in group_a])
    g_wkv, d_mem_g = _mem_kv_bwd(dkv, mn, wkv_g, mem2, row(mem_norm_g), after=exch_b[3])
    g_wo = _grad_w(o, as3(dh2_b), D_MODEL, D_MODEL, "grad_xa_wo", after=exch_b[3])
    g_wq = _grad_w(hn2, as3(dq), D_MODEL, D_MODEL, "grad_xa_wq", after=exch_b[3])
    g_wout = _grad_w(mix, as3(dh1_b), D_MODEL, D_MODEL, "grad_w_out", after=exch_b[3])
    done_a = join_and_update(group_a, (g_wkv, g_wo, g_wq, g_wout))
    halves_of.update(zip(group_b, finish_exchange(group_b, exch_b, done_a)))

    group_c = ["xa_wo", "xa_wq", "xa_wkv", "w_out"]
    swap_c = start_swap(group_c, [g_wo, g_wq, g_wkv, g_wout])
    (gx, dz, d_cw, d_cb, d_lng, d_lnb, d_gg, d_gb, d_ws, d_bs_sum, d_bin, d_mix_g) = _seqmix_bwd(
        dh1, x2, z, c1, w_out_g, w_in_g, row(norm_mix_g), cw_g, row(conv_ln_g), row(conv_ln_b),
        row(gm_ln_g), row(gm_ln_b), wpair, wpair_t, bias, tb, after=swap_c[3])
    d_bs = _head_bias_grad(d_bs_sum)[:, :GM_HEADS].T
    exch_c = start_exchange(group_c, swap_c, dz, [halves_of[nm] for nm in group_b])
    g_win = _grad_w(hn1, as3(dz), D_MODEL, 512, "grad_w_in", after=exch_c[3])
    done_b = join_and_update(group_b, g_win)
    parts_c = wait_exchange(group_c, exch_c, (g_win, *done_b))

    small_names = ["norm_mix_g", "b_in", "conv_w", "conv_b", "conv_ln_g", "conv_ln_b", "gm_ln_g", "gm_ln_b",
                   "gm_w_s", "gm_b_s", "norm_xa_g", "mem_norm_g", "norm_ffn_g", "final_norm_g"]
    d_cw_by_chip = jnp.swapaxes(d_cw.reshape(CONV_HALO, N_CHIPS, LANES), 0, 1).reshape(-1, LANES)
    small_grads = dict(norm_mix_g=d_mix_g, b_in=d_bin, conv_w=d_cw_by_chip, conv_b=d_cb, conv_ln_g=d_lng,
                       conv_ln_b=d_lnb, gm_ln_g=d_gg, gm_ln_b=d_gb, gm_w_s=d_ws, gm_b_s=d_bs, norm_xa_g=d_xa_g,
                       mem_norm_g=d_mem_g, norm_ffn_g=d_ffn_g, final_norm_g=d_final_g)

    def rows_form(a):
        a = a.reshape(-1, LANES)
        return jnp.pad(a, ((0, -a.shape[0] % SUBLANES), (0, 0)))

    pieces = [rows_form(small_grads[nm]) for nm in small_names]
    offsets, total = [], 0
    for p in pieces:
        offsets.append(total)
        total += p.shape[0]
    pack_rows = -(-total // 32) * 32
    small_pack = jnp.pad(jnp.concatenate(pieces, axis=0), ((0, pack_rows - total), (0, 0)))

    group_d = ["w_in", "small", "loss"]
    arrays_d = [split(g_win, "w_in"), small_pack.reshape(1, 2, pack_rows // 2, LANES), loss_here]
    sums_d, parts_d = chip_sums(group_d, arrays_d, _swap_halves(arrays_d, "swap_halves_w_in"))
    exch_d = _exchange_start(sums_d, parts_d, "exchange_start_w_in", parts_c)
    halves_of.update(zip(group_c, _sum_chips(parts_c, pos, "total_xa_wo", exch_d[3])))
    done_c = join_and_update(group_c, exch_d[3])
    halves_of.update(zip(group_d, finish_exchange(group_d, exch_d, done_c)))
    joined_d = _join_halves([halves_of[nm] for nm in group_d], "join_halves_w_in")
    loss = joined_d[2][0, 0, 0]
    grads["w_in"], delta["w_in"], new_m["w_in"], new_v["w_in"] = _adamw(
        [(w_in, joined_d[0].reshape(w_in.shape), m_w_in, v_w_in)], "adamw_w_in")[0]

    local_rows = lambda a, nm: a if nm == "conv_w" else a.reshape(-1, LANES)
    params = [tuple(local_rows(src[nm], nm) for src in (weights, m_in, v_in)) for nm in small_names]
    outs = _adamw_small(joined_d[1].reshape(pack_rows, LANES), pos, params, offsets, small_names.index("conv_w"))
    for k, nm in enumerate(small_names):
        for dst, a in zip((grads, delta, new_m, new_v), outs[4 * k:4 * k + 4]):
            dst[nm] = a

    order = ["norm_mix_g", "w_in", "b_in", "conv_w", "conv_b", "conv_ln_g", "conv_ln_b", "gm_ln_g", "gm_ln_b",
             "gm_w_s", "gm_b_s", "w_out", "norm_xa_g", "mem_norm_g", "xa_wq", "xa_wkv", "xa_wo", "norm_ffn_g",
             "ffn_w_gate_up", "ffn_w_down", "final_norm_g"]
    fit = lambda a, nm: a.reshape(weights[nm].shape)
    return (loss, gx.reshape(x.shape),
            *[fit(grads[nm], nm) for nm in order], *[fit(delta[nm], nm) for nm in order],
            *[fit(new_m[nm], nm) for nm in order], *[fit(new_v[nm], nm) for nm in order])
```

```python
import functools

import jax
import jax.numpy as jnp
from jax import lax
from jax.experimental import pallas as pl
from jax.experimental.pallas import tpu as pltpu

F32 = jnp.float32
BF16 = jnp.bfloat16

D_MODEL = 1024
CONV_WIDTH = 512
GM_WIDTH = 512
CONV_KERNEL = 31
CONV_HALO = 32
GRAD_ROWS = 2048
CHUNK = 128
GM_HEADS = 8
GM_HEAD_DIM = 64
XA_HEADS = 4
XA_HEAD_DIM = 256
FFN_HIDDEN = 2816
FFN_HALF = FFN_HIDDEN // 2
RMS_EPS = 1e-6
LN_EPS = 1e-5
N_CHIPS = 4
LANES = 128
SUBLANES = 8

ADAM_LR = 0.001
ADAM_B1 = 0.9
ADAM_B2 = 0.999
ADAM_EPS = 1e-08
ADAM_WD = 0.01
ADAM_STEP = 10

VMEM_LIMIT_BYTES = 56 * 1024 * 1024
MESH = pl.DeviceIdType.MESH
ANY = pl.BlockSpec(memory_space=pl.ANY)
HBM_SPEC = pl.BlockSpec(memory_space=pltpu.HBM)
SEM_SPEC = pl.BlockSpec(memory_space=pltpu.SEMAPHORE)

_NT = (((1,), (1,)), ((), ()))
_TN = (((0,), (0,)), ((), ()))
_GELU_C = 0.7978845608028654
_GELU_A = 0.044715


def _dot(a, b):
    return jnp.dot(a, b, preferred_element_type=F32)


def _dot_nt(a, b):
    return lax.dot_general(a, b, _NT, preferred_element_type=F32)


def _dot_tn(a, b):
    return lax.dot_general(a, b, _TN, preferred_element_type=F32)


def _mean(v):
    return jnp.mean(v, axis=-1, keepdims=True)


def _rowsum(v):
    return jnp.sum(v, axis=0, keepdims=True)


def _sigmoid(v):
    return 1.0 / (1.0 + jnp.exp(-v))


def _gelu_parts(v):
    v2 = v * v
    t = jnp.tanh(_GELU_C * (v + _GELU_A * v * v2))
    g = 0.5 * v * (1.0 + t)
    dg = 0.5 * (1.0 + t) + 0.5 * v * (1.0 - t * t) * (_GELU_C * (1.0 + 3.0 * _GELU_A * v2))
    return g, dg


def _rms_stats(v):
    return lax.rsqrt(_mean(v * v) + RMS_EPS)


def _rms_bwd(dy, v, r, g):
    n = v * r
    dn = dy * g
    dv = r * (dn - n * _mean(dn * n))
    return dv, _rowsum(dy * n)


def _ln_stats(v):
    mu = _mean(v)
    xc = v - mu
    rs = lax.rsqrt(_mean(xc * xc) + LN_EPS)
    return xc * rs, rs


def _ln_bwd(dy, xh, rs, g):
    dxh = dy * g
    dv = rs * (dxh - _mean(dxh) - xh * _mean(dxh * xh))
    return dv, _rowsum(dy * xh), _rowsum(dy)


def _params(sem):
    return pltpu.CompilerParams(dimension_semantics=sem, vmem_limit_bytes=VMEM_LIMIT_BYTES)


def _row_tile(s):
    return 512 if s % 512 == 0 and s >= 2048 else 128


def _mesh_pos():
    return lax.axis_index("x"), lax.axis_index("y"), lax.axis_index("c")


def _cast_into_slots(ws, pos, dtypes, name):
    n = len(ws)

    def body(pos_ref, *refs):
        for a in range(n):
            refs[n + a][0] = refs[a][...].astype(dtypes[a])

    return pl.pallas_call(
        body, name=name,
        grid_spec=pltpu.PrefetchScalarGridSpec(
            num_scalar_prefetch=1, grid=(2,),
            in_specs=[pl.BlockSpec((1,) + w.shape[1:], lambda i, p: (i, 0, 0)) for w in ws],
            out_specs=[pl.BlockSpec((1, 1) + w.shape[1:], lambda i, p: (p[0], i, 0, 0)) for w in ws]),
        out_shape=[jax.ShapeDtypeStruct((N_CHIPS,) + w.shape, dt) for w, dt in zip(ws, dtypes)],
        compiler_params=_params(("parallel",)),
    )(pos, *ws)


def _adam_update(w, g, m, v):
    nm = ADAM_B1 * m + (1.0 - ADAM_B1) * g
    nv = ADAM_B2 * v + (1.0 - ADAM_B2) * (g * g)
    m_hat = nm / (1.0 - ADAM_B1 ** ADAM_STEP)
    v_hat = nv / (1.0 - ADAM_B2 ** ADAM_STEP)
    return -ADAM_LR * (m_hat / (jnp.sqrt(v_hat) + ADAM_EPS) + ADAM_WD * w), nm, nv


ADAM_STEPS = 4


def _adamw(quads, name, after=()):
    n = len(quads)

    def body(*refs):
        ins, outs = refs[:4 * n], refs[4 * n:]
        for a in range(n):
            w, g, m, v = (r[...] for r in ins[4 * a:4 * a + 4])
            outs[4 * a][...] = g
            outs[4 * a + 1][...], outs[4 * a + 2][...], outs[4 * a + 3][...] = _adam_update(w, g, m, v)

    specs = [pl.BlockSpec((q[0].shape[0] // ADAM_STEPS, q[0].shape[1]), lambda i: (i, 0)) for q in quads]
    out = _tied_call(
        body, after, name=name, grid=(ADAM_STEPS,),
        in_specs=[sp for sp in specs for _ in range(4)], out_specs=[sp for sp in specs for _ in range(4)],
        out_shape=[jax.ShapeDtypeStruct(q[0].shape, F32) for q in quads for _ in range(4)],
        compiler_params=_params(("parallel",)),
    )(*[a for q in quads for a in q])
    return [tuple(out[4 * a:4 * a + 4]) for a in range(n)]


def _adamw_small(gpack, pos, params, offsets, conv_at):
    n = len(params)

    def body(pos_ref, g_ref, *refs):
        ins, outs = refs[:3 * n], refs[3 * n:]
        for k in range(n):
            rows = params[k][0].shape[0]
            start = offsets[k]
            if k == conv_at:
                start = pl.multiple_of(start + pos_ref[0] * CONV_HALO, SUBLANES)
            g = g_ref[pl.ds(start, rows), :]
            outs[4 * k][...] = g
            outs[4 * k + 1][...], outs[4 * k + 2][...], outs[4 * k + 3][...] = _adam_update(
                ins[3 * k][...], g, ins[3 * k + 1][...], ins[3 * k + 2][...])

    flat = [a for p in params for a in p]
    vmem = pl.BlockSpec(memory_space=pltpu.VMEM)
    return pl.pallas_call(
        body, name="adamw_small",
        in_specs=[pl.BlockSpec(memory_space=pltpu.SMEM), vmem] + [vmem] * len(flat),
        out_specs=[vmem] * (4 * n),
        out_shape=[jax.ShapeDtypeStruct(p[0].shape, F32) for p in params for _ in range(4)],
    )(pos, gpack, *flat)


def _as_tuple(after):
    return tuple(after) if isinstance(after, (tuple, list)) else (after,)


def _tied_call(body, after, *, in_specs, **kwargs):
    after = _as_tuple(after)
    n_in, n_after = len(in_specs), len(after)

    def tied(*refs):
        body(*refs[:n_in], *refs[n_in + n_after:])

    call = pl.pallas_call(tied, in_specs=list(in_specs) + [ANY] * n_after, **kwargs)
    return lambda *operands: call(*operands, *after)


def _other_chips(x, y):
    return [(1 - x, y), (x, 1 - y), (1 - x, 1 - y)]


def _gather_descriptors(bufs, send_of, recv_of):
    x, y, c = _mesh_pos()
    me = 2 * x + y
    chips = _other_chips(x, y)
    sends, arrivals = [], []
    for a in range(len(bufs)):
        for k in range(3):
            ck = 2 * chips[k][0] + chips[k][1]

            def copy(slot, a=a, k=k):
                return pltpu.make_async_remote_copy(
                    src_ref=bufs[a].at[slot, c], dst_ref=bufs[a].at[slot, c],
                    send_sem=send_of(a, k), recv_sem=recv_of(a, k),
                    device_id=(*chips[k], c), device_id_type=MESH)

            sends.append(functools.partial(copy, me))
            arrivals.append(functools.partial(copy, ck))
    return sends, arrivals


def _gather_start(bufs, name, after=()):
    n = len(bufs)
    ns = 3 * n

    def body(*refs):
        sems = refs[n:n + 2 * ns]
        thru = refs[n + 2 * ns:2 * n + 2 * ns]
        token = refs[2 * n + 2 * ns]
        _chips_handshake()
        sends, _ = _gather_descriptors(thru, lambda a, k: sems[3 * a + k], lambda a, k: sems[ns + 3 * a + k])
        for cp in sends:
            cp().start()
        token[...] = jnp.zeros_like(token)

    held = [pltpu.with_memory_space_constraint(b, pltpu.HBM) for b in bufs]
    out = _tied_call(
        body, after, name=name,
        out_shape=(*[pltpu.SemaphoreType.DMA(())] * (2 * ns), *[pltpu.HBM(b.shape, b.dtype) for b in held],
                   jax.ShapeDtypeStruct((8, LANES), F32)),
        in_specs=[HBM_SPEC] * n,
        out_specs=(*[SEM_SPEC] * (2 * ns), *[HBM_SPEC] * n, pl.BlockSpec(memory_space=pltpu.VMEM)),
        input_output_aliases={i: 2 * ns + i for i in range(n)},
        compiler_params=pltpu.CompilerParams(has_side_effects=pltpu.SideEffectType.DATAFLOW_SIDE_EFFECTING,
                                             collective_id=CHIPS_COLLECTIVE_ID),
    )(*held)
    return list(out[:ns]), list(out[ns:2 * ns]), list(out[2 * ns:2 * ns + n]), out[2 * ns + n]


def _gather_wait(send_sems, recv_sems, bufs, after, name):
    n = len(bufs)
    ns = 3 * n

    def body(*refs):
        buf_ref = refs[:n]
        sem_ref = refs[n:n + 2 * ns]
        sends, arrivals = _gather_descriptors(buf_ref, lambda a, k: sem_ref[3 * a + k],
                                              lambda a, k: sem_ref[ns + 3 * a + k])
        for cp in sends:
            cp().wait_send()
        for cp in arrivals:
            cp().wait_recv()

    out = pl.pallas_call(
        body, name=name,
        out_shape=tuple(pltpu.HBM(b.shape, b.dtype) for b in bufs),
        in_specs=[HBM_SPEC] * n + [SEM_SPEC] * (2 * ns) + [ANY] * len(_as_tuple(after)),
        out_specs=tuple([HBM_SPEC] * n),
        input_output_aliases={i: i for i in range(n)},
        compiler_params=pltpu.CompilerParams(has_side_effects=pltpu.SideEffectType.DATAFLOW_SIDE_EFFECTING),
    )(*bufs, *send_sems, *recv_sems, *_as_tuple(after))
    return list(out)


SIBLING_COLLECTIVE_ID = 0


def _sibling_handshake():
    x, y, c = _mesh_pos()
    barrier = pltpu.get_barrier_semaphore()
    pl.semaphore_signal(barrier, inc=1, device_id=(x, y, 1 - c), device_id_type=MESH)
    pl.semaphore_wait(barrier, 1)


CHIPS_COLLECTIVE_ID = 1


def _chips_handshake():
    x, y, c = _mesh_pos()
    barrier = pltpu.get_barrier_semaphore()
    for chip in _other_chips(x, y):
        pl.semaphore_signal(barrier, inc=1, device_id=(*chip, c), device_id_type=MESH)
    pl.semaphore_wait(barrier, 3)


def _pass_to_sibling(bufs, name, after=()):
    n = len(bufs)

    def body(*refs):
        outs = refs[n:2 * n]
        send_sem, recv_sem = refs[2 * n:]
        x, y, c = _mesh_pos()
        chips = _other_chips(x, y)
        _sibling_handshake()

        def half(a, k, which):
            ck = 2 * chips[k][0] + chips[k][1]
            return pltpu.make_async_remote_copy(
                src_ref=outs[a].at[ck, which], dst_ref=outs[a].at[ck, which],
                send_sem=send_sem.at[a, k], recv_sem=recv_sem.at[a, k],
                device_id=(x, y, 1 - c), device_id_type=MESH)

        sends = [half(a, k, c) for a in range(n) for k in range(3)]
        for cp in sends:
            cp.start()
        for a in range(n):
            for k in range(3):
                half(a, k, 1 - c).wait_recv()
        for cp in sends:
            cp.wait_send()

    return _tied_call(
        body, after, name=name,
        in_specs=[ANY] * n, out_specs=[ANY] * n,
        out_shape=[jax.ShapeDtypeStruct(b.shape, b.dtype) for b in bufs],
        input_output_aliases={a: a for a in range(n)},
        scratch_shapes=[pltpu.SemaphoreType.DMA((n, 3))] * 2,
        compiler_params=pltpu.CompilerParams(collective_id=SIBLING_COLLECTIVE_ID),
    )(*bufs)


def _swap_halves(grads, name):
    n = len(grads)

    def body(*refs):
        ins, outs = refs[:n], refs[n:2 * n]
        send_sem, recv_sem = refs[2 * n:]
        x, y, c = _mesh_pos()
        _sibling_handshake()
        cps = [pltpu.make_async_remote_copy(
            src_ref=ins[a].at[:, pl.ds(1 - c, 1)], dst_ref=outs[a],
            send_sem=send_sem.at[a], recv_sem=recv_sem.at[a],
            device_id=(x, y, 1 - c), device_id_type=MESH) for a in range(n)]
        for cp in cps:
            cp.start()
        for cp in cps:
            cp.wait()

    out_shape = [jax.ShapeDtypeStruct((g.shape[0], 1) + g.shape[2:], g.dtype) for g in grads]
    return pl.pallas_call(
        body, name=name,
        in_specs=[ANY] * n, out_specs=[ANY] * n, out_shape=out_shape,
        scratch_shapes=[pltpu.SemaphoreType.DMA((n,))] * 2,
        compiler_params=pltpu.CompilerParams(collective_id=SIBLING_COLLECTIVE_ID),
    )(*grads)


def _swap_descriptors(grads, lands, send_of, recv_of):
    x, y, c = _mesh_pos()
    return [functools.partial(
        pltpu.make_async_remote_copy,
        src_ref=grads[a].at[:, pl.ds(1 - c, 1)], dst_ref=lands[a],
        send_sem=send_of(a), recv_sem=recv_of(a),
        device_id=(x, y, 1 - c), device_id_type=MESH) for a in range(len(grads))]


def _swap_start(grads, name):
    n = len(grads)

    def body(*refs):
        sems = refs[2 * n:4 * n]
        g_thru, l_thru = refs[4 * n:5 * n], refs[5 * n:6 * n]
        token = refs[6 * n]
        _sibling_handshake()
        for cp in _swap_descriptors(g_thru, l_thru, lambda a: sems[a], lambda a: sems[n + a]):
            cp().start()
        token[...] = jnp.zeros_like(token)

    lands = [lax.empty((g.shape[0], 1) + g.shape[2:], g.dtype) for g in grads]
    held = [pltpu.with_memory_space_constraint(a, pltpu.HBM) for a in (*grads, *lands)]
    out = pl.pallas_call(
        body, name=name,
        out_shape=(*[pltpu.SemaphoreType.DMA(())] * (2 * n), *[pltpu.HBM(a.shape, a.dtype) for a in held],
                   jax.ShapeDtypeStruct((8, LANES), F32)),
        in_specs=[HBM_SPEC] * (2 * n),
        out_specs=(*[SEM_SPEC] * (2 * n), *[HBM_SPEC] * (2 * n), pl.BlockSpec(memory_space=pltpu.VMEM)),
        input_output_aliases={i: 2 * n + i for i in range(2 * n)},
        compiler_params=pltpu.CompilerParams(has_side_effects=pltpu.SideEffectType.DATAFLOW_SIDE_EFFECTING,
                                             collective_id=SIBLING_COLLECTIVE_ID),
    )(*held)
    return list(out[:2 * n]), list(out[2 * n:3 * n]), list(out[3 * n:4 * n]), out[4 * n]


def _swap_wait(sems, grads, lands, after, name):
    n = len(grads)

    def body(*refs):
        g_ref, l_ref = refs[:n], refs[n:2 * n]
        sem_ref = refs[2 * n:4 * n]
        for cp in _swap_descriptors(g_ref, l_ref, lambda a: sem_ref[a], lambda a: sem_ref[n + a]):
            cp().wait()

    out = pl.pallas_call(
        body, name=name,
        out_shape=tuple(pltpu.HBM(a.shape, a.dtype) for a in (*grads, *lands)),
        in_specs=[HBM_SPEC] * (2 * n) + [SEM_SPEC] * (2 * n) + [ANY] * len(_as_tuple(after)),
        out_specs=tuple([HBM_SPEC] * (2 * n)),
        input_output_aliases={i: i for i in range(2 * n)},
        compiler_params=pltpu.CompilerParams(has_side_effects=pltpu.SideEffectType.DATAFLOW_SIDE_EFFECTING),
    )(*grads, *lands, *sems, *_as_tuple(after))
    return list(out[:n]), list(out[n:])


def _add_halves(gs, gots, pos, name, dtypes):
    n = len(gs)
    j = gs[0].shape[0]

    def body(pos_ref, *refs):
        g_refs, r_refs = refs[:n], refs[n:2 * n]
        o_refs, p_refs = refs[2 * n:3 * n], refs[3 * n:]
        vals = [(g_refs[a][0, 0] + r_refs[a][0, 0]).astype(dtypes[a]) for a in range(n)]
        for a in range(n):
            o_refs[a][0] = vals[a]
        if j == 1:
            for a in range(n):
                p_refs[a][0] = vals[a]
        else:
            @pl.when(pl.program_id(0) == pos_ref[0])
            def _():
                for a in range(n):
                    p_refs[a][0] = vals[a]

    blk = lambda g: (1,) + g.shape[2:]
    out = pl.pallas_call(
        body, name=name,
        grid_spec=pltpu.PrefetchScalarGridSpec(
            num_scalar_prefetch=1, grid=(j,),
            in_specs=[pl.BlockSpec((1,) + blk(g), lambda i, p: (i, p[1], 0, 0)) for g in gs]
            + [pl.BlockSpec((1,) + blk(g), lambda i, p: (i, 0, 0, 0)) for g in gs],
            out_specs=[pl.BlockSpec(blk(g), lambda i, p: (i, 0, 0)) for g in gs]
            + [pl.BlockSpec(blk(g), lambda i, p: (p[0], 0, 0)) for g in gs]),
        out_shape=[jax.ShapeDtypeStruct((j,) + g.shape[2:], dt) for g, dt in zip(gs, dtypes)]
        + [jax.ShapeDtypeStruct((N_CHIPS,) + g.shape[2:], dt) for g, dt in zip(gs, dtypes)],
        compiler_params=_params(("arbitrary",)),
    )(pos, *gs, *gots)
    return list(out[:n]), list(out[n:])


def _exchange_descriptors(sums, parts, send_of, recv_of):
    x, y, c = _mesh_pos()
    me = 2 * x + y
    chips = _other_chips(x, y)
    sends, arrivals = [], []
    for a in range(len(sums)):
        for k in range(3):
            ck = 2 * chips[k][0] + chips[k][1]
            mine = sums[a].at[ck] if sums[a].shape[0] == N_CHIPS else sums[a].at[0]

            def copy(dst_slot, a=a, k=k, mine=mine):
                return pltpu.make_async_remote_copy(
                    src_ref=mine, dst_ref=parts[a].at[dst_slot],
                    send_sem=send_of(a, k), recv_sem=recv_of(a, k),
                    device_id=(*chips[k], c), device_id_type=MESH)

            sends.append(functools.partial(copy, me))
            arrivals.append(functools.partial(copy, ck))
    return sends, arrivals


def _exchange_start(sums, parts, name, after=()):
    n = len(sums)
    ns = 3 * n

    def body(*refs):
        sems = refs[2 * n:2 * n + 2 * ns]
        sums_thru = refs[2 * n + 2 * ns:3 * n + 2 * ns]
        parts_thru = refs[3 * n + 2 * ns:4 * n + 2 * ns]
        token = refs[4 * n + 2 * ns]
        _chips_handshake()
        sends, _ = _exchange_descriptors(sums_thru, parts_thru, lambda a, k: sems[3 * a + k],
                                         lambda a, k: sems[ns + 3 * a + k])
        for cp in sends:
            cp().start()
        token[...] = jnp.zeros_like(token)

    hbm = lambda a: pltpu.HBM(a.shape, a.dtype)
    held = [pltpu.with_memory_space_constraint(a, pltpu.HBM) for a in (*sums, *parts)]
    out = _tied_call(
        body, after, name=name,
        out_shape=(*[pltpu.SemaphoreType.DMA(())] * (2 * ns), *[hbm(a) for a in held],
                   jax.ShapeDtypeStruct((8, LANES), F32)),
        in_specs=[HBM_SPEC] * (2 * n),
        out_specs=(*[SEM_SPEC] * (2 * ns), *[HBM_SPEC] * (2 * n), pl.BlockSpec(memory_space=pltpu.VMEM)),
        input_output_aliases={i: 2 * ns + i for i in range(2 * n)},
        compiler_params=pltpu.CompilerParams(has_side_effects=pltpu.SideEffectType.DATAFLOW_SIDE_EFFECTING,
                                             collective_id=CHIPS_COLLECTIVE_ID),
    )(*held)
    return (list(out[:2 * ns]), list(out[2 * ns:2 * ns + n]), list(out[2 * ns + n:2 * ns + 2 * n]),
            out[2 * ns + 2 * n])


def _exchange_wait(sems, sums, parts, after, name):
    n = len(sums)
    ns = 3 * n

    def body(*refs):
        sums_ref, parts_ref = refs[:n], refs[n:2 * n]
        sem_ref = refs[2 * n:2 * n + 2 * ns]
        sends, arrivals = _exchange_descriptors(sums_ref, parts_ref, lambda a, k: sem_ref[3 * a + k],
                                                lambda a, k: sem_ref[ns + 3 * a + k])
        for cp in sends:
            cp().wait_send()
        for cp in arrivals:
            cp().wait_recv()

    hbm = lambda a: pltpu.HBM(a.shape, a.dtype)
    out = pl.pallas_call(
        body, name=name,
        out_shape=tuple(hbm(a) for a in (*sums, *parts)),
        in_specs=[HBM_SPEC] * (2 * n) + [SEM_SPEC] * (2 * ns) + [ANY] * len(_as_tuple(after)),
        out_specs=tuple([HBM_SPEC] * (2 * n)),
        input_output_aliases={i: i for i in range(2 * n)},
        compiler_params=pltpu.CompilerParams(has_side_effects=pltpu.SideEffectType.DATAFLOW_SIDE_EFFECTING),
    )(*sums, *parts, *sems, *_as_tuple(after))
    return list(out[n:])


def _sum_chips(parts, pos, name, after=()):
    n = len(parts)
    after = _as_tuple(after)

    def body(pos_ref, *refs):
        outs = refs[n + len(after):]
        for a in range(n):
            p_ref = refs[a]
            outs[a][0] = (((p_ref[0].astype(F32) + p_ref[1].astype(F32)) + p_ref[2].astype(F32))
                          + p_ref[3].astype(F32))

    out = pl.pallas_call(
        body, name=name,
        grid_spec=pltpu.PrefetchScalarGridSpec(
            num_scalar_prefetch=1, grid=(1,),
            in_specs=[pl.BlockSpec(p.shape, lambda i, q: (0, 0, 0)) for p in parts] + [ANY] * len(after),
            out_specs=[pl.BlockSpec((1,) + p.shape[1:], lambda i, q: (q[1], 0, 0)) for p in parts]),
        out_shape=[jax.ShapeDtypeStruct((2,) + p.shape[1:], F32) for p in parts],
        compiler_params=_params(("arbitrary",)),
    )(pos, *parts, *after)
    return list(out)


def _join_halves(fulls, name, after=()):
    n = len(fulls)

    def body(*refs):
        outs = refs[n:2 * n]
        send_sem, recv_sem = refs[2 * n:]
        x, y, c = _mesh_pos()
        _sibling_handshake()

        def half(a, which):
            return pltpu.make_async_remote_copy(
                src_ref=outs[a].at[which], dst_ref=outs[a].at[which],
                send_sem=send_sem.at[a], recv_sem=recv_sem.at[a],
                device_id=(x, y, 1 - c), device_id_type=MESH)

        sends = [half(a, c) for a in range(n)]
        for cp in sends:
            cp.start()
        for a in range(n):
            half(a, 1 - c).wait_recv()
        for cp in sends:
            cp.wait_send()

    out_shape = [jax.ShapeDtypeStruct(f.shape, f.dtype) for f in fulls]
    return _tied_call(
        body, after, name=name,
        in_specs=[ANY] * n, out_specs=[ANY] * n, out_shape=out_shape,
        input_output_aliases={a: a for a in range(n)},
        scratch_shapes=[pltpu.SemaphoreType.DMA((n,))] * 2,
        compiler_params=pltpu.CompilerParams(collective_id=SIBLING_COLLECTIVE_ID),
    )(*fulls)


def _norm_in(x, g, ts, after=()):
    s = x.shape[0]

    def body(x_ref, g_ref, hn_ref):
        xv = x_ref[...]
        hn_ref[...] = (xv * _rms_stats(xv) * g_ref[...]).astype(BF16)

    row = pl.BlockSpec((ts, D_MODEL), lambda i: (i, 0))
    return _tied_call(
        body, after, name="norm_in", grid=(s // ts,),
        in_specs=[row, pl.BlockSpec((1, D_MODEL), lambda i: (0, 0))], out_specs=row,
        out_shape=jax.ShapeDtypeStruct((s, D_MODEL), BF16),
        compiler_params=_params(("parallel",)),
    )(x, g)


def _shift_rows(buf, shifted, t):
    rows = t + CONV_HALO - SUBLANES
    for r in range(1, SUBLANES):
        shifted[r - 1, 0:rows, :] = buf[pl.ds(r, rows), :]


def _window(buf, shifted, offset, t):
    r = offset % SUBLANES
    if r == 0:
        return buf[pl.ds(offset, t), :]
    return shifted[r - 1, pl.ds(offset - r, t), :]


def _lane_is_low_head():
    lane = lax.broadcasted_iota(jnp.int32, (1, GM_WIDTH), 1)
    return (lane & GM_HEAD_DIM) == 0


def _gm_mix(v_lo, v_hi, wpair_ref, bias_ref, mixed_ref, t):
    for n in range(t // CHUNK):
        rows = slice(n * CHUNK, (n + 1) * CHUNK)
        for j in range(GM_HEADS // 2):
            cols = slice(j * LANES, (j + 1) * LANES)
            rhs = jnp.concatenate([v_lo[rows, cols], v_hi[rows, cols]], axis=0)
            mixed_ref[rows, cols] = _dot(wpair_ref[j], rhs) + bias_ref[:, cols]


def _seqmix_fwd(hn, w_in, b_in, cw, cb, lng, lnb, gg, gb, wpair, bias, t, after=()):
    s = hn.shape[0]

    def body(hn_ref, w_ref, b_ref, cw_ref, cb_ref, lng_ref, lnb_ref, gg_ref, gb_ref, wpair_ref, bias_ref,
             z_ref, mix_ref, c1_ref, abuf, ash, mixed_ref):
        i = pl.program_id(0)

        @pl.when(i == 0)
        def _():
            abuf[0:CONV_HALO, :] = jnp.zeros((CONV_HALO, CONV_WIDTH), F32)

        @pl.when(i > 0)
        def _():
            abuf[0:CONV_HALO, :] = abuf[t:t + CONV_HALO, :]

        hv = hn_ref[...]
        for j in range(4):
            cols = slice(j * 512, (j + 1) * 512)
            z_ref[:, cols] = _dot(hv, w_ref[j]) + b_ref[:, cols]

        abuf[CONV_HALO:, :] = z_ref[:, 0:512] * _sigmoid(z_ref[:, 512:1024])
        _shift_rows(abuf, ash, t)
        acc = jnp.zeros((t, CONV_WIDTH), F32)
        for k in range(CONV_KERNEL):
            acc = acc + cw_ref[k:k + 1, :] * _window(abuf, ash, CONV_HALO - (CONV_KERNEL - 1) + k, t)
        c1 = acc + cb_ref[...]
        c1_ref[...] = c1
        xh, _ = _ln_stats(c1)
        ln = xh * lng_ref[...] + lnb_ref[...]
        mix_ref[:, 0:512] = (ln * _sigmoid(ln)).astype(BF16)

        u, _ = _gelu_parts(z_ref[:, 1024:1536])
        gv, _ = _gelu_parts(z_ref[:, 1536:2048])
        vxh, _ = _ln_stats(gv)
        v = vxh * gg_ref[...] + gb_ref[...]
        low = _lane_is_low_head()
        v_lo = jnp.where(low, v, 0.0).astype(BF16)
        v_hi = jnp.where(low, 0.0, v).astype(BF16)
        _gm_mix(v_lo, v_hi, wpair_ref, bias_ref, mixed_ref, t)
        mix_ref[:, 512:1024] = (u * mixed_ref[...]).astype(BF16)

    vec = lambda n: pl.BlockSpec((1, n), lambda i: (0, 0))
    return _tied_call(
        body, after, name="seqmix_fwd", grid=(s // t,),
        in_specs=[pl.BlockSpec((t, D_MODEL), lambda i: (i, 0)),
                  pl.BlockSpec((4, D_MODEL, 512), lambda i: (0, 0, 0)), vec(2048),
                  pl.BlockSpec((CONV_HALO, CONV_WIDTH), lambda i: (0, 0)),
                  vec(512), vec(512), vec(512), vec(512), vec(512),
                  pl.BlockSpec((4, CHUNK, 2 * CHUNK), lambda i: (0, 0, 0)),
                  pl.BlockSpec((CHUNK, GM_WIDTH), lambda i: (0, 0))],
        out_specs=[pl.BlockSpec((t, 2048), lambda i: (i, 0)),
                   pl.BlockSpec((t, D_MODEL), lambda i: (i, 0)),
                   pl.BlockSpec((t, CONV_WIDTH), lambda i: (i, 0))],
        out_shape=[jax.ShapeDtypeStruct((s, 2048), F32), jax.ShapeDtypeStruct((s, D_MODEL), BF16),
                   jax.ShapeDtypeStruct((s, CONV_WIDTH), F32)],
        scratch_shapes=[pltpu.VMEM((t + CONV_HALO, CONV_WIDTH), F32),
                        pltpu.VMEM((SUBLANES - 1, t + CONV_HALO - SUBLANES, CONV_WIDTH), F32),
                        pltpu.VMEM((t, GM_WIDTH), F32)],
        compiler_params=_params(("arbitrary",)),
    )(hn, w_in, b_in, cw, cb, lng, lnb, gg, gb, wpair, bias)


def _mem_kv(mem, g, wkv):
    m = mem.shape[0]

    def body(mem_ref, g_ref, w_ref, mn_ref, kv_ref):
        mv = mem_ref[...]
        mn = (mv * _rms_stats(mv) * g_ref[...]).astype(BF16)
        mn_ref[...] = mn
        for j in range(4):
            kv_ref[:, j * 512:(j + 1) * 512] = _dot(mn, w_ref[j]).astype(BF16)

    return pl.pallas_call(
        body, name="mem_kv",
        out_shape=[jax.ShapeDtypeStruct((m, D_MODEL), BF16), jax.ShapeDtypeStruct((m, 2 * D_MODEL), BF16)],
        compiler_params=pltpu.CompilerParams(vmem_limit_bytes=VMEM_LIMIT_BYTES),
    )(mem, g, wkv)


def _softmax_rows(sc):
    e = jnp.exp(sc - jnp.max(sc, axis=-1, keepdims=True))
    return e / jnp.sum(e, axis=-1, keepdims=True)


def _attn_block_fwd(x, mix, w_out, g_xa, wq, kv, wo, g_ffn, ts, after=()):
    s, m = x.shape[0], kv.shape[0]
    scale = XA_HEAD_DIM ** -0.5

    def body(x_ref, mix_ref, wout_ref, gxa_ref, wq_ref, kv_ref, wo_ref, gffn_ref,
             h1_ref, hn2_ref, q_ref, o_ref, h2_ref, hn3_ref):
        h1 = x_ref[...] + _dot(mix_ref[...], wout_ref[...])
        h1_ref[...] = h1
        hn2 = (h1 * _rms_stats(h1) * gxa_ref[...]).astype(BF16)
        hn2_ref[...] = hn2
        q_ref[...] = _dot(hn2, wq_ref[...]).astype(BF16)
        for h in range(XA_HEADS):
            cols = slice(h * XA_HEAD_DIM, (h + 1) * XA_HEAD_DIM)
            vcols = slice(D_MODEL + h * XA_HEAD_DIM, D_MODEL + (h + 1) * XA_HEAD_DIM)
            p = _softmax_rows(_dot_nt(q_ref[:, cols], kv_ref[:, cols]) * scale)
            o_ref[:, cols] = _dot(p.astype(BF16), kv_ref[:, vcols]).astype(BF16)
        h2 = h1 + _dot(o_ref[...], wo_ref[...])
        h2_ref[...] = h2
        hn3_ref[...] = (h2 * _rms_stats(h2) * gffn_ref[...]).astype(BF16)

    row = pl.BlockSpec((ts, D_MODEL), lambda i: (i, 0))
    full = pl.BlockSpec((D_MODEL, D_MODEL), lambda i: (0, 0))
    vec = pl.BlockSpec((1, D_MODEL), lambda i: (0, 0))
    f32 = jax.ShapeDtypeStruct((s, D_MODEL), F32)
    bf16 = jax.ShapeDtypeStruct((s, D_MODEL), BF16)
    return _tied_call(
        body, after, name="attn_block_fwd", grid=(s // ts,),
        in_specs=[row, row, full, vec, full, pl.BlockSpec((m, 2 * D_MODEL), lambda i: (0, 0)), full, vec],
        out_specs=[row] * 6,
        out_shape=[f32, bf16, bf16, bf16, f32, bf16],
        compiler_params=_params(("parallel",)),
    )(x, mix, w_out, g_xa, wq, kv, wo, g_ffn)


_FFN_CHUNKS_FWD = (slice(0, 6 * LANES), slice(6 * LANES, FFN_HALF))
_FFN_CHUNKS_BWD = (slice(0, 4 * LANES), slice(4 * LANES, 8 * LANES), slice(8 * LANES, FFN_HALF))


def _ffn_up(hn, wgu, ts, after=()):
    s = hn.shape[0]

    def body(hn_ref, w_ref, gu_ref, act_ref):
        hv = hn_ref[...]
        for cols in _FFN_CHUNKS_FWD:
            gate = _dot(hv, w_ref[0, 0, :, cols])
            up = _dot(hv, w_ref[1, 0, :, cols])
            gu_ref[0, :, cols] = gate.astype(BF16)
            gu_ref[1, :, cols] = up.astype(BF16)
            act_ref[:, cols] = (gate * _sigmoid(gate) * up).astype(BF16)

    return _tied_call(
        body, after, name="ffn_up", grid=(2, s // ts),
        in_specs=[pl.BlockSpec((ts, D_MODEL), lambda j, i: (i, 0)),
                  pl.BlockSpec((2, 1, D_MODEL, FFN_HALF), lambda j, i: (0, j, 0, 0))],
        out_specs=[pl.BlockSpec((2, ts, FFN_HALF), lambda j, i: (0, i, j)),
                   pl.BlockSpec((ts, FFN_HALF), lambda j, i: (i, j))],
        out_shape=[jax.ShapeDtypeStruct((2, s, FFN_HIDDEN), BF16), jax.ShapeDtypeStruct((s, FFN_HIDDEN), BF16)],
        compiler_params=_params(("parallel", "parallel")),
    )(hn, wgu)


def _ffn_down_loss(act, wd, h2, g, target, ts):
    s = act.shape[0]

    def body(act_ref, wd_ref, h2_ref, g_ref, t_ref, dh_ref, dhb_ref, sq_ref, dg_ref):
        @pl.when(pl.program_id(0) == 0)
        def _():
            sq_ref[...] = jnp.zeros_like(sq_ref)
            dg_ref[...] = jnp.zeros_like(dg_ref)

        h3 = h2_ref[...] + _dot(act_ref[...], wd_ref[...])
        r = _rms_stats(h3)
        gv = g_ref[...]
        diff = h3 * r * gv - t_ref[...]
        sq_ref[...] += _rowsum(diff * diff)
        dh, dg = _rms_bwd(diff / D_MODEL, h3, r, gv)
        dh_ref[...] = dh
        dhb_ref[...] = dh.astype(BF16)
        dg_ref[...] += dg

    row = pl.BlockSpec((ts, D_MODEL), lambda i: (i, 0))
    vec = pl.BlockSpec((1, D_MODEL), lambda i: (0, 0))
    return pl.pallas_call(
        body, name="ffn_down_loss", grid=(s // ts,),
        in_specs=[pl.BlockSpec((ts, FFN_HIDDEN), lambda i: (i, 0)),
                  pl.BlockSpec((FFN_HIDDEN, D_MODEL), lambda i: (0, 0)), row, vec, row],
        out_specs=[row, row, vec, vec],
        out_shape=[jax.ShapeDtypeStruct((s, D_MODEL), F32), jax.ShapeDtypeStruct((s, D_MODEL), BF16),
                   jax.ShapeDtypeStruct((1, D_MODEL), F32), jax.ShapeDtypeStruct((1, D_MODEL), F32)],
        compiler_params=_params(("arbitrary",)),
    )(act, wd, h2, g, target)


def _grad_w(a, b, tk, tn, name, after=()):
    s, k = a.shape
    gb, _, n = b.shape
    nblk = n // tn
    tsr = GRAD_ROWS if s % GRAD_ROWS == 0 else s

    def body(a_ref, b_ref, o_ref):
        part = _dot_tn(a_ref[...], b_ref[0])

        @pl.when(pl.program_id(2) == 0)
        def _():
            o_ref[0] = part

        @pl.when(pl.program_id(2) > 0)
        def _():
            o_ref[0] += part

    return _tied_call(
        body, after, name=name, grid=(gb * nblk, k // tk, s // tsr),
        in_specs=[pl.BlockSpec((tsr, tk), lambda ni, ki, si: (si, ki)),
                  pl.BlockSpec((1, tsr, tn), lambda ni, ki, si: (ni // nblk, si, ni % nblk))],
        out_specs=pl.BlockSpec((1, tk, tn), lambda ni, ki, si: (ni, ki, 0)),
        out_shape=jax.ShapeDtypeStruct((gb * nblk, k, tn), F32),
        compiler_params=_params(("parallel", "parallel", "arbitrary")),
    )(a, b)


def _grad_w_square(pairs, name, after=()):
    n = len(pairs)
    s = pairs[0][0].shape[0]
    tsr = GRAD_ROWS // 2 if s % (GRAD_ROWS // 2) == 0 else s

    def body(*refs):
        ins, outs = refs[:2 * n], refs[2 * n:]
        parts = [_dot_tn(ins[2 * a][...], ins[2 * a + 1][...]) for a in range(n)]

        @pl.when(pl.program_id(0) == 0)
        def _():
            for a in range(n):
                outs[a][...] = parts[a]

        @pl.when(pl.program_id(0) > 0)
        def _():
            for a in range(n):
                outs[a][...] += parts[a]

    row = pl.BlockSpec((tsr, D_MODEL), lambda i: (i, 0))
    return _tied_call(
        body, after, name=name, grid=(s // tsr,),
        in_specs=[row] * (2 * n), out_specs=[pl.BlockSpec((D_MODEL, D_MODEL), lambda i: (0, 0))] * n,
        out_shape=[jax.ShapeDtypeStruct((D_MODEL, D_MODEL), F32)] * n,
        compiler_params=_params(("arbitrary",)),
    )(*[x for p in pairs for x in p])


def _ffn_bwd(dh3, wd, gu, wgu, h2, g, t, after=()):
    s = dh3.shape[0]

    def body(dh3_ref, wd_ref, gu_ref, w_ref, h2_ref, g_ref, dgu_ref, dh2_ref, dh2b_ref, dg_ref):
        @pl.when(pl.program_id(0) == 0)
        def _():
            dg_ref[...] = jnp.zeros_like(dg_ref)

        dh3v = dh3_ref[...]
        dhb = dh3v.astype(BF16)
        dhn = jnp.zeros((t, D_MODEL), F32)
        for j in range(2):
            for cols in _FFN_CHUNKS_BWD:
                whole = slice(j * FFN_HALF + cols.start, j * FFN_HALF + cols.stop)
                dact = _dot_nt(dhb, wd_ref[j, cols, :])
                gate, up = gu_ref[0, :, whole].astype(F32), gu_ref[1, :, whole].astype(F32)
                sg = _sigmoid(gate)
                dgate = (dact * up * (sg * (1.0 + gate * (1.0 - sg)))).astype(BF16)
                dup = (dact * (gate * sg)).astype(BF16)
                dgu_ref[0, :, whole] = dgate
                dgu_ref[1, :, whole] = dup
                dhn = dhn + _dot_nt(dgate, w_ref[j, :, cols]) + _dot_nt(dup, w_ref[2 + j, :, cols])
        h2 = h2_ref[...]
        dv, dg = _rms_bwd(dhn, h2, _rms_stats(h2), g_ref[...])
        dh2 = dh3v + dv
        dh2_ref[...] = dh2
        dh2b_ref[...] = dh2.astype(BF16)
        dg_ref[...] += dg

    row = pl.BlockSpec((t, D_MODEL), lambda i: (i, 0))
    wide = pl.BlockSpec((2, t, FFN_HIDDEN), lambda i: (0, i, 0))
    vec = pl.BlockSpec((1, D_MODEL), lambda i: (0, 0))
    return _tied_call(
        body, after, name="ffn_bwd", grid=(s // t,),
        in_specs=[row, pl.BlockSpec((2, FFN_HALF, D_MODEL), lambda i: (0, 0, 0)), wide,
                  pl.BlockSpec((4, D_MODEL, FFN_HALF), lambda i: (0, 0, 0)), row, vec],
        out_specs=[wide, row, row, vec],
        out_shape=[jax.ShapeDtypeStruct((2, s, FFN_HIDDEN), BF16), jax.ShapeDtypeStruct((s, D_MODEL), F32),
                   jax.ShapeDtypeStruct((s, D_MODEL), BF16), jax.ShapeDtypeStruct((1, D_MODEL), F32)],
        compiler_params=_params(("arbitrary",)),
    )(dh3, wd, gu, wgu, h2, g)


def _attn_bwd(dh2, wo, q, kv, wq, h1, g, ts, after=()):
    s, m = q.shape[0], kv.shape[0]
    scale = XA_HEAD_DIM ** -0.5

    def body(dh2_ref, wo_ref, q_ref, kv_ref, wq_ref, h1_ref, g_ref, dh1_ref, dh1b_ref, dq_ref, dkv_ref, dg_ref):
        @pl.when(pl.program_id(0) == 0)
        def _():
            dkv_ref[...] = jnp.zeros_like(dkv_ref)
            dg_ref[...] = jnp.zeros_like(dg_ref)

        do = _dot_nt(dh2_ref[...].astype(BF16), wo_ref[...]).astype(BF16)
        for h in range(XA_HEADS):
            cols = slice(h * XA_HEAD_DIM, (h + 1) * XA_HEAD_DIM)
            vcols = slice(D_MODEL + h * XA_HEAD_DIM, D_MODEL + (h + 1) * XA_HEAD_DIM)
            qh, kh, vh, doh = q_ref[:, cols], kv_ref[:, cols], kv_ref[:, vcols], do[:, cols]
            p = _softmax_rows(_dot_nt(qh, kh) * scale)
            dp = _dot_nt(doh, vh)
            ds = (p * (dp - jnp.sum(dp * p, axis=-1, keepdims=True)) * scale).astype(BF16)
            dq_ref[:, cols] = _dot(ds, kh).astype(BF16)
            dkv_ref[:, cols] += _dot_tn(ds, qh)
            dkv_ref[:, vcols] += _dot_tn(p.astype(BF16), doh)
        dhn = _dot_nt(dq_ref[...], wq_ref[...])
        h1 = h1_ref[...]
        dv, dg = _rms_bwd(dhn, h1, _rms_stats(h1), g_ref[...])
        dh1 = dh2_ref[...] + dv
        dh1_ref[...] = dh1
        dh1b_ref[...] = dh1.astype(BF16)
        dg_ref[...] += dg

    row = pl.BlockSpec((ts, D_MODEL), lambda i: (i, 0))
    full = pl.BlockSpec((D_MODEL, D_MODEL), lambda i: (0, 0))
    kvs = pl.BlockSpec((m, 2 * D_MODEL), lambda i: (0, 0))
    vec = pl.BlockSpec((1, D_MODEL), lambda i: (0, 0))
    return _tied_call(
        body, after, name="attn_bwd", grid=(s // ts,),
        in_specs=[row, full, row, kvs, full, row, vec],
        out_specs=[row, row, row, kvs, vec],
        out_shape=[jax.ShapeDtypeStruct((s, D_MODEL), F32), jax.ShapeDtypeStruct((s, D_MODEL), BF16),
                   jax.ShapeDtypeStruct((s, D_MODEL), BF16),
                   jax.ShapeDtypeStruct((m, 2 * D_MODEL), F32), jax.ShapeDtypeStruct((1, D_MODEL), F32)],
        compiler_params=_params(("arbitrary",)),
    )(dh2, wo, q, kv, wq, h1, g)


def _mem_kv_bwd(dkv, mn, wkv, mem, g, after=()):
    m = mem.shape[0]

    def body(dkv_ref, mn_ref, w_ref, mem_ref, g_ref, dw_ref, dg_ref):
        dmn = jnp.zeros((m, D_MODEL), F32)
        mn = mn_ref[...]
        for j in range(4):
            dj = dkv_ref[:, j * 512:(j + 1) * 512].astype(BF16)
            dw_ref[j] = _dot_tn(mn, dj)
            dmn = dmn + _dot_nt(dj, w_ref[j])
        mv = mem_ref[...]
        dg_ref[...] = _rowsum(dmn * (mv * _rms_stats(mv)))

    return _tied_call(
        body, after, name="mem_kv_bwd", in_specs=[pl.BlockSpec(memory_space=pltpu.VMEM)] * 5,
        out_shape=[jax.ShapeDtypeStruct((4, D_MODEL, 512), F32), jax.ShapeDtypeStruct((1, D_MODEL), F32)],
        compiler_params=pltpu.CompilerParams(vmem_limit_bytes=VMEM_LIMIT_BYTES),
    )(dkv, mn, wkv, mem, g)


def _seqmix_bwd(dh1, x, z, c1, w_out, w_in, g_mix, cw, lng, lnb, gg, gb, wpair, wpair_t, bias, t, after=()):
    s = x.shape[0]
    nt = s // t

    def body(dh1_ref, x_ref, z_ref, c1_ref, wo_ref, wi_ref, gm_ref, cw_ref, lng_ref, lnb_ref,
             gg_ref, gb_ref, wpair_ref, wpt_ref, bias_ref,
             gx_ref, dz_ref, dcw_ref, dcb_ref, dlng_ref, dlnb_ref, dgg_ref, dgb_ref, dws_ref, dbs_ref,
             dbin_ref, dgm_ref, dbuf, dsh, mixed_ref, dv_ref):
        i = pl.program_id(0)
        accs = (dcw_ref, dcb_ref, dlng_ref, dlnb_ref, dgg_ref, dgb_ref, dws_ref, dbs_ref, dbin_ref, dgm_ref)

        @pl.when(i == 0)
        def _():
            for r in accs:
                r[...] = jnp.zeros_like(r)
            dbuf[t:t + CONV_HALO, :] = jnp.zeros((CONV_HALO, CONV_WIDTH), F32)

        @pl.when(i > 0)
        def _():
            dbuf[t:t + CONV_HALO, :] = dbuf[0:CONV_HALO, :]

        dmix = _dot_nt(dh1_ref[...].astype(BF16), wo_ref[...])

        xh, rs = _ln_stats(c1_ref[...])
        lng = lng_ref[...]
        ln = xh * lng + lnb_ref[...]
        sl = _sigmoid(ln)
        dln = dmix[:, 0:512] * (sl * (1.0 + ln * (1.0 - sl)))
        dc1, dg_ln, db_ln = _ln_bwd(dln, xh, rs, lng)
        dlng_ref[...] += dg_ln
        dlnb_ref[...] += db_ln
        dcb_ref[...] += _rowsum(dc1)
        dbuf[0:t, :] = dc1

        za = z_ref[:, 0:512]
        sg = _sigmoid(z_ref[:, 512:1024])
        a = za * sg
        _shift_rows(dbuf, dsh, t)

        da = jnp.zeros((t, CONV_WIDTH), F32)
        for k in range(CONV_KERNEL):
            later = _window(dbuf, dsh, CONV_KERNEL - 1 - k, t)
            da = da + cw_ref[k:k + 1, :] * later
            dcw_ref[k:k + 1, :] += _rowsum(a * later)
        dza = da * sg
        dzg = da * za * (sg * (1.0 - sg))
        dz_ref[:, 0:512] = dza.astype(BF16)
        dz_ref[:, 512:1024] = dzg.astype(BF16)
        dbin_ref[:, 0:512] += _rowsum(dza)
        dbin_ref[:, 512:1024] += _rowsum(dzg)

        dgm = dmix[:, 512:1024]
        u, du_dz = _gelu_parts(z_ref[:, 1024:1536])
        gv, dgv_dz = _gelu_parts(z_ref[:, 1536:2048])
        vxh, vrs = _ln_stats(gv)
        ggv = gg_ref[...]
        v = vxh * ggv + gb_ref[...]
        low = _lane_is_low_head()
        v_lo = jnp.where(low, v, 0.0).astype(BF16)
        v_hi = jnp.where(low, 0.0, v).astype(BF16)
        _gm_mix(v_lo, v_hi, wpair_ref, bias_ref, mixed_ref, t)
        dzu = dgm * mixed_ref[...] * du_dz
        dm = dgm * u
        dm_lo = jnp.where(low, dm, 0.0).astype(BF16)
        dm_hi = jnp.where(low, 0.0, dm).astype(BF16)
        vb = v.astype(BF16)
        tril = (lax.broadcasted_iota(jnp.int32, (CHUNK, CHUNK), 1)
                <= lax.broadcasted_iota(jnp.int32, (CHUNK, CHUNK), 0))
        for n in range(t // CHUNK):
            rows = slice(n * CHUNK, (n + 1) * CHUNK)
            dbs_ref[...] += dm[rows, :]
            for j in range(GM_HEADS // 2):
                cols = slice(j * LANES, (j + 1) * LANES)
                stack = jnp.concatenate([dm_lo[rows, cols], dm_hi[rows, cols]], axis=0)
                dws = _dot_nt(stack, vb[rows, cols])
                dws_ref[2 * j] += jnp.where(tril, dws[0:CHUNK], 0.0)
                dws_ref[2 * j + 1] += jnp.where(tril, dws[CHUNK:2 * CHUNK], 0.0)
                dv_ref[rows, cols] = _dot(wpt_ref[j], stack)
        dgv, dg_gm, db_gm = _ln_bwd(dv_ref[...], vxh, vrs, ggv)
        dgg_ref[...] += dg_gm
        dgb_ref[...] += db_gm
        dzv = dgv * dgv_dz
        dz_ref[:, 1024:1536] = dzu.astype(BF16)
        dz_ref[:, 1536:2048] = dzv.astype(BF16)
        dbin_ref[:, 1024:1536] += _rowsum(dzu)
        dbin_ref[:, 1536:2048] += _rowsum(dzv)

        dhn = jnp.zeros((t, D_MODEL), F32)
        for j in range(4):
            dhn = dhn + _dot_nt(dz_ref[:, j * 512:(j + 1) * 512], wi_ref[j])
        xv = x_ref[...]
        dv, dg = _rms_bwd(dhn, xv, _rms_stats(xv), gm_ref[...])
        gx_ref[...] = dh1_ref[...] + dv
        dgm_ref[...] += dg

    rev = lambda w: pl.BlockSpec((t, w), lambda i: (nt - 1 - i, 0))
    const = lambda *shape: pl.BlockSpec(shape, lambda i: (0,) * len(shape))
    f32 = lambda *shape: jax.ShapeDtypeStruct(shape, F32)
    return _tied_call(
        body, after, name="seqmix_bwd", grid=(nt,),
        in_specs=[rev(D_MODEL), rev(D_MODEL), rev(2048), rev(CONV_WIDTH),
                  const(D_MODEL, D_MODEL), const(4, D_MODEL, 512), const(1, D_MODEL),
                  const(CONV_HALO, CONV_WIDTH), const(1, 512), const(1, 512), const(1, 512), const(1, 512),
                  const(4, CHUNK, 2 * CHUNK), const(4, CHUNK, 2 * CHUNK), const(CHUNK, GM_WIDTH)],
        out_specs=[rev(D_MODEL), rev(2048),
                   const(CONV_HALO, CONV_WIDTH), const(1, 512), const(1, 512), const(1, 512), const(1, 512),
                   const(1, 512), const(GM_HEADS, CHUNK, CHUNK), const(CHUNK, GM_WIDTH), const(1, 2048),
                   const(1, D_MODEL)],
        out_shape=[f32(s, D_MODEL), jax.ShapeDtypeStruct((s, 2048), BF16),
                   f32(CONV_HALO, CONV_WIDTH), f32(1, 512), f32(1, 512), f32(1, 512), f32(1, 512),
                   f32(1, 512), f32(GM_HEADS, CHUNK, CHUNK), f32(CHUNK, GM_WIDTH), f32(1, 2048),
                   f32(1, D_MODEL)],
        scratch_shapes=[pltpu.VMEM((t + CONV_HALO, CONV_WIDTH), F32),
                        pltpu.VMEM((SUBLANES - 1, t + CONV_HALO - SUBLANES, CONV_WIDTH), F32),
                        pltpu.VMEM((t, GM_WIDTH), F32), pltpu.VMEM((t, GM_WIDTH), F32)],
        compiler_params=_params(("arbitrary",)),
    )(dh1, x, z, c1, w_out, w_in, g_mix, cw, lng, lnb, gg, gb, wpair, wpair_t, bias)


def _head_bias_grad(dbs):
    def body(d_ref, o_ref):
        dv = d_ref[...]
        lane = lax.broadcasted_iota(jnp.int32, (CHUNK, LANES), 1)
        acc = jnp.zeros((CHUNK, LANES), F32)
        for h in range(GM_HEADS):
            sh = jnp.sum(dv[:, h * GM_HEAD_DIM:(h + 1) * GM_HEAD_DIM], axis=-1, keepdims=True)
            acc = acc + jnp.where(lane == h, sh, 0.0)
        o_ref[...] = acc

    return pl.pallas_call(body, name="head_bias_grad",
                          out_shape=jax.ShapeDtypeStruct((CHUNK, LANES), F32))(dbs)


def kernel(x, mem, norm_mix_g, w_in, b_in, conv_w, conv_b, conv_ln_g, conv_ln_b, gm_ln_g, gm_ln_b, gm_w_s, gm_b_s, w_out, norm_xa_g, mem_norm_g, xa_wq, xa_wkv, xa_wo, norm_ffn_g, ffn_w_gate_up, ffn_w_down, final_norm_g, loss_target, m_norm_mix_g, m_w_in, m_b_in, m_conv_w, m_conv_b, m_conv_ln_g, m_conv_ln_b, m_gm_ln_g, m_gm_ln_b, m_gm_w_s, m_gm_b_s, m_w_out, m_norm_xa_g, m_mem_norm_g, m_xa_wq, m_xa_wkv, m_xa_wo, m_norm_ffn_g, m_ffn_w_gate_up, m_ffn_w_down, m_final_norm_g, v_norm_mix_g, v_w_in, v_b_in, v_conv_w, v_conv_b, v_conv_ln_g, v_conv_ln_b, v_gm_ln_g, v_gm_ln_b, v_gm_w_s, v_gm_b_s, v_w_out, v_norm_xa_g, v_mem_norm_g, v_xa_wq, v_xa_wkv, v_xa_wo, v_norm_ffn_g, v_ffn_w_gate_up, v_ffn_w_down, v_final_norm_g):
    weights = dict(norm_mix_g=norm_mix_g, w_in=w_in, b_in=b_in, conv_w=conv_w, conv_b=conv_b, conv_ln_g=conv_ln_g,
                   conv_ln_b=conv_ln_b, gm_ln_g=gm_ln_g, gm_ln_b=gm_ln_b, gm_w_s=gm_w_s, gm_b_s=gm_b_s, w_out=w_out,
                   norm_xa_g=norm_xa_g, mem_norm_g=mem_norm_g, xa_wq=xa_wq, xa_wkv=xa_wkv, xa_wo=xa_wo,
                   norm_ffn_g=norm_ffn_g, ffn_w_gate_up=ffn_w_gate_up, ffn_w_down=ffn_w_down,
                   final_norm_g=final_norm_g)
    m_in = dict(norm_mix_g=m_norm_mix_g, w_in=m_w_in, b_in=m_b_in, conv_w=m_conv_w, conv_b=m_conv_b,
                conv_ln_g=m_conv_ln_g, conv_ln_b=m_conv_ln_b, gm_ln_g=m_gm_ln_g, gm_ln_b=m_gm_ln_b, gm_w_s=m_gm_w_s,
                gm_b_s=m_gm_b_s, w_out=m_w_out, norm_xa_g=m_norm_xa_g, mem_norm_g=m_mem_norm_g, xa_wq=m_xa_wq,
                xa_wkv=m_xa_wkv, xa_wo=m_xa_wo, norm_ffn_g=m_norm_ffn_g, ffn_w_gate_up=m_ffn_w_gate_up,
                ffn_w_down=m_ffn_w_down, final_norm_g=m_final_norm_g)
    v_in = dict(norm_mix_g=v_norm_mix_g, w_in=v_w_in, b_in=v_b_in, conv_w=v_conv_w, conv_b=v_conv_b,
                conv_ln_g=v_conv_ln_g, conv_ln_b=v_conv_ln_b, gm_ln_g=v_gm_ln_g, gm_ln_b=v_gm_ln_b, gm_w_s=v_gm_w_s,
                gm_b_s=v_gm_b_s, w_out=v_w_out, norm_xa_g=v_norm_xa_g, mem_norm_g=v_mem_norm_g, xa_wq=v_xa_wq,
                xa_wkv=v_xa_wkv, xa_wo=v_xa_wo, norm_ffn_g=v_norm_ffn_g, ffn_w_gate_up=v_ffn_w_gate_up,
                ffn_w_down=v_ffn_w_down, final_norm_g=v_final_norm_g)
    grads, delta, new_m, new_v = {}, {}, {}, {}

    s = x.shape[1]
    ts = _row_tile(s)
    tb = max(CHUNK, ts // 2)
    tw = 2 * ts if s % (2 * ts) == 0 and ts >= 512 else ts
    cx, cy, cc = _mesh_pos()
    chip = 2 * cx + cy
    pos = jnp.stack([chip, cc]).astype(jnp.int32)
    row = lambda a: a.reshape(1, -1)
    x2, mem2, tgt2 = x[0], mem[0], loss_target[0]

    big = dict(w_in=w_in, xa_wkv=xa_wkv, w_out=w_out, xa_wq=xa_wq, xa_wo=xa_wo,
               ffn_w_gate_up=ffn_w_gate_up, ffn_w_down=ffn_w_down)
    big_names = list(big)
    halves = lambda a: a.reshape(2, a.shape[0] // 2, a.shape[1])
    conv_w_pad = jnp.pad(conv_w, ((0, CONV_HALO - CONV_KERNEL), (0, 0)))
    first_names = ["w_in", "conv_w"]
    later_names = [nm for nm in big_names if nm != "w_in"]
    cast = dict(zip(first_names, _cast_into_slots([halves(w_in), halves(conv_w_pad)], pos, [BF16, F32], "cast_w_in")))
    cast.update(zip(later_names, _cast_into_slots([halves(big[nm]) for nm in later_names], pos,
                                                  [BF16] * len(later_names), "cast_" + later_names[0])))

    def start_gather(names, after):
        return _gather_start([cast[nm] for nm in names], "gather_start_" + names[0], after)

    def land_gather(names, started, after):
        send_sems, recv_sems, bufs, _ = started
        return _gather_wait(send_sems, recv_sems, bufs, after, "gather_wait_" + names[0])

    def share_gather(names, landed, after=()):
        return dict(zip(names, (b.reshape(N_CHIPS, -1, b.shape[-1])
                                for b in _pass_to_sibling(landed, "pass_" + names[0], after))))

    tril = jnp.tril(jnp.ones((CHUNK, CHUNK), dtype=bool))
    ws = jnp.where(tril[None], gm_w_s, 0.0)
    wpair = jnp.concatenate([ws[0::2], ws[1::2]], axis=2).astype(BF16)
    ws_t = jnp.swapaxes(ws, 1, 2)
    wpair_t = jnp.concatenate([ws_t[0::2], ws_t[1::2]], axis=2).astype(BF16)
    bias = jnp.repeat(gm_b_s.T, GM_HEAD_DIM, axis=1)

    attn_names = ["w_out", "xa_wq", "xa_wkv", "xa_wo"]
    gather_first = start_gather(first_names, ())
    hn1 = _norm_in(x2, row(norm_mix_g), tw, after=(gather_first[3], wpair, wpair_t, bias))
    landed = land_gather(first_names, gather_first, [cast[nm] for nm in later_names] + [hn1])
    gather_attn = start_gather(attn_names, landed)
    gw = share_gather(first_names, landed, gather_attn[3])
    w_in_g = gw["w_in"]
    cw_g = jnp.concatenate([gw["conv_w"][k] for k in range(N_CHIPS)], axis=1)

    z, mix, c1 = _seqmix_fwd(hn1, w_in_g, row(b_in), cw_g, row(conv_b), row(conv_ln_g), row(conv_ln_b),
                             row(gm_ln_g), row(gm_ln_b), wpair, bias, ts)
    landed = land_gather(attn_names, gather_attn, mix)
    gather_gu = start_gather(["ffn_w_gate_up"], landed)
    gw = share_gather(attn_names, landed, gather_gu[3])
    w_out_g = gw["w_out"].reshape(D_MODEL, D_MODEL)
    wq_g = gw["xa_wq"].reshape(D_MODEL, D_MODEL)
    wkv_g = gw["xa_wkv"]
    wo_g = gw["xa_wo"].reshape(D_MODEL, D_MODEL)
    mn, kv = _mem_kv(mem2, row(mem_norm_g), wkv_g)
    h1, hn2, q, o, h2, hn3 = _attn_block_fwd(x2, mix, w_out_g, row(norm_xa_g), wq_g, kv, wo_g, row(norm_ffn_g), ts)
    landed = land_gather(["ffn_w_gate_up"], gather_gu, hn3)
    gather_down = start_gather(["ffn_w_down"], landed)
    wgu_g = share_gather(["ffn_w_gate_up"], landed, gather_down[3])["ffn_w_gate_up"]
    gu, act = _ffn_up(hn3, wgu_g.reshape(2, 2, D_MODEL, FFN_HALF), tw)
    landed = land_gather(["ffn_w_down"], gather_down, act)
    wd_g = share_gather(["ffn_w_down"], landed)["ffn_w_down"].reshape(FFN_HIDDEN, D_MODEL)
    dh3, dh3_b, sq, d_final_g = _ffn_down_loss(act, wd_g, h2, row(final_norm_g), tgt2, ts)
    loss_here = jnp.broadcast_to(0.5 * jnp.sum(sq) / D_MODEL, (1, 2, SUBLANES, LANES))

    def split(g, nm):
        r, c = big[nm].shape
        return g.reshape(N_CHIPS, 2, r // 2, c)

    def chip_sums(group, arrays, got):
        sums, parts = [None] * len(group), [None] * len(group)
        for blocks in (N_CHIPS, 1):
            idx = [i for i, a in enumerate(arrays) if a.shape[0] == blocks]
            if idx:
                out = _add_halves([arrays[i] for i in idx], [got[i] for i in idx], pos, "chip_sum_" + group[idx[0]],
                                  [F32 if group[i] == "loss" else BF16 for i in idx])
                for k, i in enumerate(idx):
                    sums[i], parts[i] = out[0][k], out[1][k]
        return sums, parts

    def start_swap(group, grads):
        return _swap_start([split(g, nm) for g, nm in zip(grads, group)], "swap_start_" + group[0])

    def start_exchange(group, swapping, after, landed):
        sems, arrays, lands, _ = swapping
        arrays, got = _swap_wait(sems, arrays, lands, after, "swap_wait_" + group[0])
        sums, parts = chip_sums(group, arrays, got)
        return _exchange_start(sums, parts, "exchange_start_" + group[0], landed)

    def wait_exchange(group, started, after):
        sems, sums, parts, _ = started
        return _exchange_wait(sems, sums, parts, after, "exchange_wait_" + group[0])

    def finish_exchange(group, started, after):
        return _sum_chips(wait_exchange(group, started, after), pos, "total_" + group[0])

    def join_and_update(group, after):
        joined = _join_halves([halves_of[nm] for nm in group], "join_halves_" + group[0], after)
        outs = _adamw([(weights[nm], j.reshape(big[nm].shape), m_in[nm], v_in[nm]) for nm, j in zip(group, joined)],
                      "adamw_" + group[0])
        for nm, out in zip(group, outs):
            grads[nm], delta[nm], new_m[nm], new_v[nm] = out
        return [new_v[nm] for nm in group]

    as3 = lambda a: a.reshape((1,) + a.shape)
    halves_of = {}

    g_down = _grad_w(act, as3(dh3_b), FFN_HALF, D_MODEL, "grad_ffn_w_down")
    group_a = ["ffn_w_down"]
    swap_a = start_swap(group_a, [g_down])
    dgu, dh2, dh2_b, d_ffn_g = _ffn_bwd(dh3, wd_g.reshape(2, FFN_HALF, D_MODEL), gu, wgu_g, h2, row(norm_ffn_g), tb,
                                        after=swap_a[3])
    exch_a = start_exchange(group_a, swap_a, dh2, wd_g)
    g_gu = _grad_w(hn3, dgu, D_MODEL, FFN_HALF, "grad_ffn_w_gate_up", after=exch_a[3])
    halves_of.update(zip(group_a, finish_exchange(group_a, exch_a, g_gu)))

    group_b = ["ffn_w_gate_up"]
    swap_b = start_swap(group_b, [g_gu])
    dh1, dh1_b, dq, dkv, d_xa_g = _attn_bwd(dh2, wo_g, q, kv, wq_g, h1, row(norm_xa_g), ts, after=swap_b[3])
    exch_b = start_exchange(group_b, swap_b, dh1, [halves_of[nm] for nm in group_a])
    g_wkv, d_mem_g = _mem_kv_bwd(dkv, mn, wkv_g, mem2, row(mem_norm_g), after=exch_b[3])
    g_wo, g_wq, g_wout = _grad_w_square([(o, dh2_b), (hn2, dq), (mix, dh1_b)], "grad_xa_wo", after=exch_b[3])
    done_a = join_and_update(group_a, (g_wkv, g_wo, g_wq, g_wout))
    halves_of.update(zip(group_b, finish_exchange(group_b, exch_b, done_a)))

    group_c = ["xa_wo", "xa_wq", "xa_wkv", "w_out"]
    swap_c = start_swap(group_c, [g_wo, g_wq, g_wkv, g_wout])
    (gx, dz, d_cw, d_cb, d_lng, d_lnb, d_gg, d_gb, d_ws, d_bs_sum, d_bin, d_mix_g) = _seqmix_bwd(
        dh1, x2, z, c1, w_out_g, w_in_g, row(norm_mix_g), cw_g, row(conv_ln_g), row(conv_ln_b),
        row(gm_ln_g), row(gm_ln_b), wpair, wpair_t, bias, tb, after=swap_c[3])
    d_bs = _head_bias_grad(d_bs_sum)[:, :GM_HEADS].T
    exch_c = start_exchange(group_c, swap_c, dz, [halves_of[nm] for nm in group_b])
    g_win = _grad_w(hn1, as3(dz), D_MODEL, 512, "grad_w_in", after=exch_c[3])
    done_b = join_and_update(group_b, g_win)
    parts_c = wait_exchange(group_c, exch_c, (g_win, *done_b))

    small_names = ["norm_mix_g", "b_in", "conv_w", "conv_b", "conv_ln_g", "conv_ln_b", "gm_ln_g", "gm_ln_b",
                   "gm_w_s", "gm_b_s", "norm_xa_g", "mem_norm_g", "norm_ffn_g", "final_norm_g"]
    d_cw_by_chip = jnp.swapaxes(d_cw.reshape(CONV_HALO, N_CHIPS, LANES), 0, 1).reshape(-1, LANES)
    small_grads = dict(norm_mix_g=d_mix_g, b_in=d_bin, conv_w=d_cw_by_chip, conv_b=d_cb, conv_ln_g=d_lng,
                       conv_ln_b=d_lnb, gm_ln_g=d_gg, gm_ln_b=d_gb, gm_w_s=d_ws, gm_b_s=d_bs, norm_xa_g=d_xa_g,
                       mem_norm_g=d_mem_g, norm_ffn_g=d_ffn_g, final_norm_g=d_final_g)

    def rows_form(a):
        a = a.reshape(-1, LANES)
        return jnp.pad(a, ((0, -a.shape[0] % SUBLANES), (0, 0)))

    pieces = [rows_form(small_grads[nm]) for nm in small_names]
    offsets, total = [], 0
    for p in pieces:
        offsets.append(total)
        total += p.shape[0]
    pack_rows = -(-total // 32) * 32
    small_pack = jnp.pad(jnp.concatenate(pieces, axis=0), ((0, pack_rows - total), (0, 0)))

    group_d = ["w_in", "small", "loss"]
    arrays_d = [split(g_win, "w_in"), small_pack.reshape(1, 2, pack_rows // 2, LANES), loss_here]
    sums_d, parts_d = chip_sums(group_d, arrays_d, _swap_halves(arrays_d, "swap_halves_w_in"))
    exch_d = _exchange_start(sums_d, parts_d, "exchange_start_w_in", parts_c)
    halves_of.update(zip(group_c, _sum_chips(parts_c, pos, "total_xa_wo", exch_d[3])))
    done_c = join_and_update(group_c, exch_d[3])
    halves_of.update(zip(group_d, finish_exchange(group_d, exch_d, done_c)))
    joined_d = _join_halves([halves_of[nm] for nm in group_d], "join_halves_w_in")
    loss = joined_d[2][0, 0, 0]
    grads["w_in"], delta["w_in"], new_m["w_in"], new_v["w_in"] = _adamw(
        [(w_in, joined_d[0].reshape(w_in.shape), m_w_in, v_w_in)], "adamw_w_in")[0]

    local_rows = lambda a, nm: a if nm == "conv_w" else a.reshape(-1, LANES)
    params = [tuple(local_rows(src[nm], nm) for src in (weights, m_in, v_in)) for nm in small_names]
    outs = _adamw_small(joined_d[1].reshape(pack_rows, LANES), pos, params, offsets, small_names.index("conv_w"))
    for k, nm in enumerate(small_names):
        for dst, a in zip((grads, delta, new_m, new_v), outs[4 * k:4 * k + 4]):
            dst[nm] = a

    order = ["norm_mix_g", "w_in", "b_in", "conv_w", "conv_b", "conv_ln_g", "conv_ln_b", "gm_ln_g", "gm_ln_b",
             "gm_w_s", "gm_b_s", "w_out", "norm_xa_g", "mem_norm_g", "xa_wq", "xa_wkv", "xa_wo", "norm_ffn_g",
             "ffn_w_gate_up", "ffn_w_down", "final_norm_g"]
    fit = lambda a, nm: a.reshape(weights[nm].shape)
    return (loss, gx.reshape(x.shape),
            *[fit(grads[nm], nm) for nm in order], *[fit(delta[nm], nm) for nm in order],
            *[fit(new_m[nm], nm) for nm in order], *[fit(new_v[nm], nm) for nm in order])
```

```python
import functools

import jax
import jax.numpy as jnp
from jax import lax
from jax.experimental import pallas as pl
from jax.experimental.pallas import tpu as pltpu

F32 = jnp.float32
BF16 = jnp.bfloat16

D_MODEL = 1024
CONV_WIDTH = 512
GM_WIDTH = 512
CONV_KERNEL = 31
CONV_HALO = 32
GRAD_ROWS = 2048
CHUNK = 128
GM_HEADS = 8
GM_HEAD_DIM = 64
XA_HEADS = 4
XA_HEAD_DIM = 256
FFN_HIDDEN = 2816
FFN_HALF = FFN_HIDDEN // 2
RMS_EPS = 1e-6
LN_EPS = 1e-5
N_CHIPS = 4
LANES = 128
SUBLANES = 8

ADAM_LR = 0.001
ADAM_B1 = 0.9
ADAM_B2 = 0.999
ADAM_EPS = 1e-08
ADAM_WD = 0.01
ADAM_STEP = 10

VMEM_LIMIT_BYTES = 56 * 1024 * 1024
MESH = pl.DeviceIdType.MESH
ANY = pl.BlockSpec(memory_space=pl.ANY)
HBM_SPEC = pl.BlockSpec(memory_space=pltpu.HBM)
SEM_SPEC = pl.BlockSpec(memory_space=pltpu.SEMAPHORE)

_NT = (((1,), (1,)), ((), ()))
_TN = (((0,), (0,)), ((), ()))
_GELU_C = 0.7978845608028654
_GELU_A = 0.044715


def _dot(a, b):
    return jnp.dot(a, b, preferred_element_type=F32)


def _dot_nt(a, b):
    return lax.dot_general(a, b, _NT, preferred_element_type=F32)


def _dot_tn(a, b):
    return lax.dot_general(a, b, _TN, preferred_element_type=F32)


def _mean(v):
    return jnp.mean(v, axis=-1, keepdims=True)


def _rowsum(v):
    return jnp.sum(v, axis=0, keepdims=True)


def _sigmoid(v):
    return 1.0 / (1.0 + jnp.exp(-v))


def _gelu_parts(v):
    v2 = v * v
    t = jnp.tanh(_GELU_C * (v + _GELU_A * v * v2))
    g = 0.5 * v * (1.0 + t)
    dg = 0.5 * (1.0 + t) + 0.5 * v * (1.0 - t * t) * (_GELU_C * (1.0 + 3.0 * _GELU_A * v2))
    return g, dg


def _rms_stats(v):
    return lax.rsqrt(_mean(v * v) + RMS_EPS)


def _rms_bwd(dy, v, r, g):
    n = v * r
    dn = dy * g
    dv = r * (dn - n * _mean(dn * n))
    return dv, _rowsum(dy * n)


def _ln_stats(v):
    mu = _mean(v)
    xc = v - mu
    rs = lax.rsqrt(_mean(xc * xc) + LN_EPS)
    return xc * rs, rs


def _ln_bwd(dy, xh, rs, g):
    dxh = dy * g
    dv = rs * (dxh - _mean(dxh) - xh * _mean(dxh * xh))
    return dv, _rowsum(dy * xh), _rowsum(dy)


def _params(sem):
    return pltpu.CompilerParams(dimension_semantics=sem, vmem_limit_bytes=VMEM_LIMIT_BYTES)


def _row_tile(s):
    return 512 if s % 512 == 0 and s >= 2048 else 128


def _mesh_pos():
    return lax.axis_index("x"), lax.axis_index("y"), lax.axis_index("c")


def _cast_into_slots(ws, pos, dtypes, name):
    n = len(ws)

    def body(pos_ref, *refs):
        for a in range(n):
            refs[n + a][0] = refs[a][...].astype(dtypes[a])

    return pl.pallas_call(
        body, name=name,
        grid_spec=pltpu.PrefetchScalarGridSpec(
            num_scalar_prefetch=1, grid=(2,),
            in_specs=[pl.BlockSpec((1,) + w.shape[1:], lambda i, p: (i, 0, 0)) for w in ws],
            out_specs=[pl.BlockSpec((1, 1) + w.shape[1:], lambda i, p: (p[0], i, 0, 0)) for w in ws]),
        out_shape=[jax.ShapeDtypeStruct((N_CHIPS,) + w.shape, dt) for w, dt in zip(ws, dtypes)],
        compiler_params=_params(("parallel",)),
    )(pos, *ws)


def _adam_update(w, g, m, v):
    nm = ADAM_B1 * m + (1.0 - ADAM_B1) * g
    nv = ADAM_B2 * v + (1.0 - ADAM_B2) * (g * g)
    m_hat = nm / (1.0 - ADAM_B1 ** ADAM_STEP)
    v_hat = nv / (1.0 - ADAM_B2 ** ADAM_STEP)
    return -ADAM_LR * (m_hat / (jnp.sqrt(v_hat) + ADAM_EPS) + ADAM_WD * w), nm, nv


ADAM_STEPS = 4


def _adamw(quads, name, after=()):
    n = len(quads)

    def body(*refs):
        ins, outs = refs[:4 * n], refs[4 * n:]
        for a in range(n):
            w, g, m, v = (r[...] for r in ins[4 * a:4 * a + 4])
            outs[4 * a][...] = g
            outs[4 * a + 1][...], outs[4 * a + 2][...], outs[4 * a + 3][...] = _adam_update(w, g, m, v)

    specs = [pl.BlockSpec((q[0].shape[0] // ADAM_STEPS, q[0].shape[1]), lambda i: (i, 0)) for q in quads]
    out = _tied_call(
        body, after, name=name, grid=(ADAM_STEPS,),
        in_specs=[sp for sp in specs for _ in range(4)], out_specs=[sp for sp in specs for _ in range(4)],
        out_shape=[jax.ShapeDtypeStruct(q[0].shape, F32) for q in quads for _ in range(4)],
        compiler_params=_params(("parallel",)),
    )(*[a for q in quads for a in q])
    return [tuple(out[4 * a:4 * a + 4]) for a in range(n)]


def _adamw_small(gpack, pos, params, offsets, conv_at):
    n = len(params)

    def body(pos_ref, g_ref, *refs):
        ins, outs = refs[:3 * n], refs[3 * n:]
        for k in range(n):
            rows = params[k][0].shape[0]
            start = offsets[k]
            if k == conv_at:
                start = pl.multiple_of(start + pos_ref[0] * CONV_HALO, SUBLANES)
            g = g_ref[pl.ds(start, rows), :]
            outs[4 * k][...] = g
            outs[4 * k + 1][...], outs[4 * k + 2][...], outs[4 * k + 3][...] = _adam_update(
                ins[3 * k][...], g, ins[3 * k + 1][...], ins[3 * k + 2][...])

    flat = [a for p in params for a in p]
    vmem = pl.BlockSpec(memory_space=pltpu.VMEM)
    return pl.pallas_call(
        body, name="adamw_small",
        in_specs=[pl.BlockSpec(memory_space=pltpu.SMEM), vmem] + [vmem] * len(flat),
        out_specs=[vmem] * (4 * n),
        out_shape=[jax.ShapeDtypeStruct(p[0].shape, F32) for p in params for _ in range(4)],
    )(pos, gpack, *flat)


def _as_tuple(after):
    return tuple(after) if isinstance(after, (tuple, list)) else (after,)


def _tied_call(body, after, *, in_specs, **kwargs):
    after = _as_tuple(after)
    n_in, n_after = len(in_specs), len(after)

    def tied(*refs):
        body(*refs[:n_in], *refs[n_in + n_after:])

    call = pl.pallas_call(tied, in_specs=list(in_specs) + [ANY] * n_after, **kwargs)
    return lambda *operands: call(*operands, *after)


def _other_chips(x, y):
    return [(1 - x, y), (x, 1 - y), (1 - x, 1 - y)]


def _gather_descriptors(bufs, send_of, recv_of):
    x, y, c = _mesh_pos()
    me = 2 * x + y
    chips = _other_chips(x, y)
    sends, arrivals = [], []
    for a in range(len(bufs)):
        for k in range(3):
            ck = 2 * chips[k][0] + chips[k][1]

            def copy(slot, a=a, k=k):
                return pltpu.make_async_remote_copy(
                    src_ref=bufs[a].at[slot, c], dst_ref=bufs[a].at[slot, c],
                    send_sem=send_of(a, k), recv_sem=recv_of(a, k),
                    device_id=(*chips[k], c), device_id_type=MESH)

            sends.append(functools.partial(copy, me))
            arrivals.append(functools.partial(copy, ck))
    return sends, arrivals


def _gather_start(bufs, name, after=()):
    n = len(bufs)
    ns = 3 * n

    def body(*refs):
        sems = refs[n:n + 2 * ns]
        thru = refs[n + 2 * ns:2 * n + 2 * ns]
        token = refs[2 * n + 2 * ns]
        _chips_handshake()
        sends, _ = _gather_descriptors(thru, lambda a, k: sems[3 * a + k], lambda a, k: sems[ns + 3 * a + k])
        for cp in sends:
            cp().start()
        token[...] = jnp.zeros_like(token)

    held = [pltpu.with_memory_space_constraint(b, pltpu.HBM) for b in bufs]
    out = _tied_call(
        body, after, name=name,
        out_shape=(*[pltpu.SemaphoreType.DMA(())] * (2 * ns), *[pltpu.HBM(b.shape, b.dtype) for b in held],
                   jax.ShapeDtypeStruct((8, LANES), F32)),
        in_specs=[HBM_SPEC] * n,
        out_specs=(*[SEM_SPEC] * (2 * ns), *[HBM_SPEC] * n, pl.BlockSpec(memory_space=pltpu.VMEM)),
        input_output_aliases={i: 2 * ns + i for i in range(n)},
        compiler_params=pltpu.CompilerParams(has_side_effects=pltpu.SideEffectType.DATAFLOW_SIDE_EFFECTING,
                                             collective_id=CHIPS_COLLECTIVE_ID),
    )(*held)
    return list(out[:ns]), list(out[ns:2 * ns]), list(out[2 * ns:2 * ns + n]), out[2 * ns + n]


def _gather_wait(send_sems, recv_sems, bufs, after, name):
    n = len(bufs)
    ns = 3 * n

    def body(*refs):
        buf_ref = refs[:n]
        sem_ref = refs[n:n + 2 * ns]
        sends, arrivals = _gather_descriptors(buf_ref, lambda a, k: sem_ref[3 * a + k],
                                              lambda a, k: sem_ref[ns + 3 * a + k])
        for cp in sends:
            cp().wait_send()
        for cp in arrivals:
            cp().wait_recv()

    out = pl.pallas_call(
        body, name=name,
        out_shape=tuple(pltpu.HBM(b.shape, b.dtype) for b in bufs),
        in_specs=[HBM_SPEC] * n + [SEM_SPEC] * (2 * ns) + [ANY] * len(_as_tuple(after)),
        out_specs=tuple([HBM_SPEC] * n),
        input_output_aliases={i: i for i in range(n)},
        compiler_params=pltpu.CompilerParams(has_side_effects=pltpu.SideEffectType.DATAFLOW_SIDE_EFFECTING),
    )(*bufs, *send_sems, *recv_sems, *_as_tuple(after))
    return list(out)


SIBLING_COLLECTIVE_ID = 0


def _sibling_handshake():
    x, y, c = _mesh_pos()
    barrier = pltpu.get_barrier_semaphore()
    pl.semaphore_signal(barrier, inc=1, device_id=(x, y, 1 - c), device_id_type=MESH)
    pl.semaphore_wait(barrier, 1)


CHIPS_COLLECTIVE_ID = 1


def _chips_handshake():
    x, y, c = _mesh_pos()
    barrier = pltpu.get_barrier_semaphore()
    for chip in _other_chips(x, y):
        pl.semaphore_signal(barrier, inc=1, device_id=(*chip, c), device_id_type=MESH)
    pl.semaphore_wait(barrier, 3)


def _pass_to_sibling(bufs, name, after=()):
    n = len(bufs)

    def body(*refs):
        outs = refs[n:2 * n]
        send_sem, recv_sem = refs[2 * n:]
        x, y, c = _mesh_pos()
        chips = _other_chips(x, y)
        _sibling_handshake()

        def half(a, k, which):
            ck = 2 * chips[k][0] + chips[k][1]
            return pltpu.make_async_remote_copy(
                src_ref=outs[a].at[ck, which], dst_ref=outs[a].at[ck, which],
                send_sem=send_sem.at[a, k], recv_sem=recv_sem.at[a, k],
                device_id=(x, y, 1 - c), device_id_type=MESH)

        sends = [half(a, k, c) for a in range(n) for k in range(3)]
        for cp in sends:
            cp.start()
        for a in range(n):
            for k in range(3):
                half(a, k, 1 - c).wait_recv()
        for cp in sends:
            cp.wait_send()

    return _tied_call(
        body, after, name=name,
        in_specs=[ANY] * n, out_specs=[ANY] * n,
        out_shape=[jax.ShapeDtypeStruct(b.shape, b.dtype) for b in bufs],
        input_output_aliases={a: a for a in range(n)},
        scratch_shapes=[pltpu.SemaphoreType.DMA((n, 3))] * 2,
        compiler_params=pltpu.CompilerParams(collective_id=SIBLING_COLLECTIVE_ID),
    )(*bufs)


def _swap_halves(grads, name):
    n = len(grads)

    def body(*refs):
        ins, outs = refs[:n], refs[n:2 * n]
        send_sem, recv_sem = refs[2 * n:]
        x, y, c = _mesh_pos()
        _sibling_handshake()
        cps = [pltpu.make_async_remote_copy(
            src_ref=ins[a].at[:, pl.ds(1 - c, 1)], dst_ref=outs[a],
            send_sem=send_sem.at[a], recv_sem=recv_sem.at[a],
            device_id=(x, y, 1 - c), device_id_type=MESH) for a in range(n)]
        for cp in cps:
            cp.start()
        for cp in cps:
            cp.wait()

    out_shape = [jax.ShapeDtypeStruct((g.shape[0], 1) + g.shape[2:], g.dtype) for g in grads]
    return pl.pallas_call(
        body, name=name,
        in_specs=[ANY] * n, out_specs=[ANY] * n, out_shape=out_shape,
        scratch_shapes=[pltpu.SemaphoreType.DMA((n,))] * 2,
        compiler_params=pltpu.CompilerParams(collective_id=SIBLING_COLLECTIVE_ID),
    )(*grads)


def _swap_descriptors(grads, lands, send_of, recv_of):
    x, y, c = _mesh_pos()
    return [functools.partial(
        pltpu.make_async_remote_copy,
        src_ref=grads[a].at[:, pl.ds(1 - c, 1)], dst_ref=lands[a],
        send_sem=send_of(a), recv_sem=recv_of(a),
        device_id=(x, y, 1 - c), device_id_type=MESH) for a in range(len(grads))]


def _swap_start(grads, name):
    n = len(grads)

    def body(*refs):
        sems = refs[2 * n:4 * n]
        g_thru, l_thru = refs[4 * n:5 * n], refs[5 * n:6 * n]
        token = refs[6 * n]
        _sibling_handshake()
        for cp in _swap_descriptors(g_thru, l_thru, lambda a: sems[a], lambda a: sems[n + a]):
            cp().start()
        token[...] = jnp.zeros_like(token)

    lands = [lax.empty((g.shape[0], 1) + g.shape[2:], g.dtype) for g in grads]
    held = [pltpu.with_memory_space_constraint(a, pltpu.HBM) for a in (*grads, *lands)]
    out = pl.pallas_call(
        body, name=name,
        out_shape=(*[pltpu.SemaphoreType.DMA(())] * (2 * n), *[pltpu.HBM(a.shape, a.dtype) for a in held],
                   jax.ShapeDtypeStruct((8, LANES), F32)),
        in_specs=[HBM_SPEC] * (2 * n),
        out_specs=(*[SEM_SPEC] * (2 * n), *[HBM_SPEC] * (2 * n), pl.BlockSpec(memory_space=pltpu.VMEM)),
        input_output_aliases={i: 2 * n + i for i in range(2 * n)},
        compiler_params=pltpu.CompilerParams(has_side_effects=pltpu.SideEffectType.DATAFLOW_SIDE_EFFECTING,
                                             collective_id=SIBLING_COLLECTIVE_ID),
    )(*held)
    return list(out[:2 * n]), list(out[2 * n:3 * n]), list(out[3 * n:4 * n]), out[4 * n]


def _swap_wait(sems, grads, lands, after, name):
    n = len(grads)

    def body(*refs):
        g_ref, l_ref = refs[:n], refs[n:2 * n]
        sem_ref = refs[2 * n:4 * n]
        for cp in _swap_descriptors(g_ref, l_ref, lambda a: sem_ref[a], lambda a: sem_ref[n + a]):
            cp().wait()

    out = pl.pallas_call(
        body, name=name,
        out_shape=tuple(pltpu.HBM(a.shape, a.dtype) for a in (*grads, *lands)),
        in_specs=[HBM_SPEC] * (2 * n) + [SEM_SPEC] * (2 * n) + [ANY] * len(_as_tuple(after)),
        out_specs=tuple([HBM_SPEC] * (2 * n)),
        input_output_aliases={i: i for i in range(2 * n)},
        compiler_params=pltpu.CompilerParams(has_side_effects=pltpu.SideEffectType.DATAFLOW_SIDE_EFFECTING),
    )(*grads, *lands, *sems, *_as_tuple(after))
    return list(out[:n]), list(out[n:])


def _add_halves(gs, gots, pos, name, dtypes):
    n = len(gs)
    j = gs[0].shape[0]

    def body(pos_ref, *refs):
        g_refs, r_refs = refs[:n], refs[n:2 * n]
        o_refs, p_refs = refs[2 * n:3 * n], refs[3 * n:]
        vals = [(g_refs[a][0, 0] + r_refs[a][0, 0]).astype(dtypes[a]) for a in range(n)]
        for a in range(n):
            o_refs[a][0] = vals[a]
        if j == 1:
            for a in range(n):
                p_refs[a][0] = vals[a]
        else:
            @pl.when(pl.program_id(0) == pos_ref[0])
            def _():
                for a in range(n):
                    p_refs[a][0] = vals[a]

    blk = lambda g: (1,) + g.shape[2:]
    out = pl.pallas_call(
        body, name=name,
        grid_spec=pltpu.PrefetchScalarGridSpec(
            num_scalar_prefetch=1, grid=(j,),
            in_specs=[pl.BlockSpec((1,) + blk(g), lambda i, p: (i, p[1], 0, 0)) for g in gs]
            + [pl.BlockSpec((1,) + blk(g), lambda i, p: (i, 0, 0, 0)) for g in gs],
            out_specs=[pl.BlockSpec(blk(g), lambda i, p: (i, 0, 0)) for g in gs]
            + [pl.BlockSpec(blk(g), lambda i, p: (p[0], 0, 0)) for g in gs]),
        out_shape=[jax.ShapeDtypeStruct((j,) + g.shape[2:], dt) for g, dt in zip(gs, dtypes)]
        + [jax.ShapeDtypeStruct((N_CHIPS,) + g.shape[2:], dt) for g, dt in zip(gs, dtypes)],
        compiler_params=_params(("arbitrary",)),
    )(pos, *gs, *gots)
    return list(out[:n]), list(out[n:])


def _exchange_descriptors(sums, parts, send_of, recv_of):
    x, y, c = _mesh_pos()
    me = 2 * x + y
    chips = _other_chips(x, y)
    sends, arrivals = [], []
    for a in range(len(sums)):
        for k in range(3):
            ck = 2 * chips[k][0] + chips[k][1]
            mine = sums[a].at[ck] if sums[a].shape[0] == N_CHIPS else sums[a].at[0]

            def copy(dst_slot, a=a, k=k, mine=mine):
                return pltpu.make_async_remote_copy(
                    src_ref=mine, dst_ref=parts[a].at[dst_slot],
                    send_sem=send_of(a, k), recv_sem=recv_of(a, k),
                    device_id=(*chips[k], c), device_id_type=MESH)

            sends.append(functools.partial(copy, me))
            arrivals.append(functools.partial(copy, ck))
    return sends, arrivals


def _exchange_start(sums, parts, name, after=()):
    n = len(sums)
    ns = 3 * n

    def body(*refs):
        sems = refs[2 * n:2 * n + 2 * ns]
        sums_thru = refs[2 * n + 2 * ns:3 * n + 2 * ns]
        parts_thru = refs[3 * n + 2 * ns:4 * n + 2 * ns]
        token = refs[4 * n + 2 * ns]
        _chips_handshake()
        sends, _ = _exchange_descriptors(sums_thru, parts_thru, lambda a, k: sems[3 * a + k],
                                         lambda a, k: sems[ns + 3 * a + k])
        for cp in sends:
            cp().start()
        token[...] = jnp.zeros_like(token)

    hbm = lambda a: pltpu.HBM(a.shape, a.dtype)
    held = [pltpu.with_memory_space_constraint(a, pltpu.HBM) for a in (*sums, *parts)]
    out = _tied_call(
        body, after, name=name,
        out_shape=(*[pltpu.SemaphoreType.DMA(())] * (2 * ns), *[hbm(a) for a in held],
                   jax.ShapeDtypeStruct((8, LANES), F32)),
        in_specs=[HBM_SPEC] * (2 * n),
        out_specs=(*[SEM_SPEC] * (2 * ns), *[HBM_SPEC] * (2 * n), pl.BlockSpec(memory_space=pltpu.VMEM)),
        input_output_aliases={i: 2 * ns + i for i in range(2 * n)},
        compiler_params=pltpu.CompilerParams(has_side_effects=pltpu.SideEffectType.DATAFLOW_SIDE_EFFECTING,
                                             collective_id=CHIPS_COLLECTIVE_ID),
    )(*held)
    return (list(out[:2 * ns]), list(out[2 * ns:2 * ns + n]), list(out[2 * ns + n:2 * ns + 2 * n]),
            out[2 * ns + 2 * n])


def _exchange_wait(sems, sums, parts, after, name):
    n = len(sums)
    ns = 3 * n

    def body(*refs):
        sums_ref, parts_ref = refs[:n], refs[n:2 * n]
        sem_ref = refs[2 * n:2 * n + 2 * ns]
        sends, arrivals = _exchange_descriptors(sums_ref, parts_ref, lambda a, k: sem_ref[3 * a + k],
                                                lambda a, k: sem_ref[ns + 3 * a + k])
        for cp in sends:
            cp().wait_send()
        for cp in arrivals:
            cp().wait_recv()

    hbm = lambda a: pltpu.HBM(a.shape, a.dtype)
    out = pl.pallas_call(
        body, name=name,
        out_shape=tuple(hbm(a) for a in (*sums, *parts)),
        in_specs=[HBM_SPEC] * (2 * n) + [SEM_SPEC] * (2 * ns) + [ANY] * len(_as_tuple(after)),
        out_specs=tuple([HBM_SPEC] * (2 * n)),
        input_output_aliases={i: i for i in range(2 * n)},
        compiler_params=pltpu.CompilerParams(has_side_effects=pltpu.SideEffectType.DATAFLOW_SIDE_EFFECTING),
    )(*sums, *parts, *sems, *_as_tuple(after))
    return list(out[n:])


def _sum_chips(parts, pos, name, after=()):
    n = len(parts)
    after = _as_tuple(after)

    def body(pos_ref, *refs):
        outs = refs[n + len(after):]
        for a in range(n):
            p_ref = refs[a]
            outs[a][0] = (((p_ref[0].astype(F32) + p_ref[1].astype(F32)) + p_ref[2].astype(F32))
                          + p_ref[3].astype(F32))

    out = pl.pallas_call(
        body, name=name,
        grid_spec=pltpu.PrefetchScalarGridSpec(
            num_scalar_prefetch=1, grid=(1,),
            in_specs=[pl.BlockSpec(p.shape, lambda i, q: (0, 0, 0)) for p in parts] + [ANY] * len(after),
            out_specs=[pl.BlockSpec((1,) + p.shape[1:], lambda i, q: (q[1], 0, 0)) for p in parts]),
        out_shape=[jax.ShapeDtypeStruct((2,) + p.shape[1:], F32) for p in parts],
        compiler_params=_params(("arbitrary",)),
    )(pos, *parts, *after)
    return list(out)


def _join_halves(fulls, name, after=()):
    n = len(fulls)

    def body(*refs):
        outs = refs[n:2 * n]
        send_sem, recv_sem = refs[2 * n:]
        x, y, c = _mesh_pos()
        _sibling_handshake()

        def half(a, which):
            return pltpu.make_async_remote_copy(
                src_ref=outs[a].at[which], dst_ref=outs[a].at[which],
                send_sem=send_sem.at[a], recv_sem=recv_sem.at[a],
                device_id=(x, y, 1 - c), device_id_type=MESH)

        sends = [half(a, c) for a in range(n)]
        for cp in sends:
            cp.start()
        for a in range(n):
            half(a, 1 - c).wait_recv()
        for cp in sends:
            cp.wait_send()

    out_shape = [jax.ShapeDtypeStruct(f.shape, f.dtype) for f in fulls]
    return _tied_call(
        body, after, name=name,
        in_specs=[ANY] * n, out_specs=[ANY] * n, out_shape=out_shape,
        input_output_aliases={a: a for a in range(n)},
        scratch_shapes=[pltpu.SemaphoreType.DMA((n,))] * 2,
        compiler_params=pltpu.CompilerParams(collective_id=SIBLING_COLLECTIVE_ID),
    )(*fulls)


def _norm_in(x, g, ts, after=()):
    s = x.shape[0]

    def body(x_ref, g_ref, hn_ref):
        xv = x_ref[...]
        hn_ref[...] = (xv * _rms_stats(xv) * g_ref[...]).astype(BF16)

    row = pl.BlockSpec((ts, D_MODEL), lambda i: (i, 0))
    return _tied_call(
        body, after, name="norm_in", grid=(s // ts,),
        in_specs=[row, pl.BlockSpec((1, D_MODEL), lambda i: (0, 0))], out_specs=row,
        out_shape=jax.ShapeDtypeStruct((s, D_MODEL), BF16),
        compiler_params=_params(("parallel",)),
    )(x, g)


def _shift_rows(buf, shifted, t):
    rows = t + CONV_HALO - SUBLANES
    for r in range(1, SUBLANES):
        shifted[r - 1, 0:rows, :] = buf[pl.ds(r, rows), :]


def _window(buf, shifted, offset, t):
    r = offset % SUBLANES
    if r == 0:
        return buf[pl.ds(offset, t), :]
    return shifted[r - 1, pl.ds(offset - r, t), :]


def _lane_is_low_head():
    lane = lax.broadcasted_iota(jnp.int32, (1, GM_WIDTH), 1)
    return (lane & GM_HEAD_DIM) == 0


def _gm_mix(v_lo, v_hi, wpair_ref, bias_ref, mixed_ref, t):
    for n in range(t // CHUNK):
        rows = slice(n * CHUNK, (n + 1) * CHUNK)
        for j in range(GM_HEADS // 2):
            cols = slice(j * LANES, (j + 1) * LANES)
            rhs = jnp.concatenate([v_lo[rows, cols], v_hi[rows, cols]], axis=0)
            mixed_ref[rows, cols] = _dot(wpair_ref[j], rhs) + bias_ref[:, cols]


def _seqmix_fwd(hn, w_in, b_in, cw, cb, lng, lnb, gg, gb, wpair, bias, t, after=()):
    s = hn.shape[0]

    def body(hn_ref, w_ref, b_ref, cw_ref, cb_ref, lng_ref, lnb_ref, gg_ref, gb_ref, wpair_ref, bias_ref,
             z_ref, mix_ref, c1_ref, abuf, ash, mixed_ref):
        i = pl.program_id(0)

        @pl.when(i == 0)
        def _():
            abuf[0:CONV_HALO, :] = jnp.zeros((CONV_HALO, CONV_WIDTH), F32)

        @pl.when(i > 0)
        def _():
            abuf[0:CONV_HALO, :] = abuf[t:t + CONV_HALO, :]

        hv = hn_ref[...]
        for j in range(4):
            cols = slice(j * 512, (j + 1) * 512)
            z_ref[:, cols] = _dot(hv, w_ref[j]) + b_ref[:, cols]

        abuf[CONV_HALO:, :] = z_ref[:, 0:512] * _sigmoid(z_ref[:, 512:1024])
        _shift_rows(abuf, ash, t)
        acc = jnp.zeros((t, CONV_WIDTH), F32)
        for k in range(CONV_KERNEL):
            acc = acc + cw_ref[k:k + 1, :] * _window(abuf, ash, CONV_HALO - (CONV_KERNEL - 1) + k, t)
        c1 = acc + cb_ref[...]
        c1_ref[...] = c1
        xh, _ = _ln_stats(c1)
        ln = xh * lng_ref[...] + lnb_ref[...]
        mix_ref[:, 0:512] = (ln * _sigmoid(ln)).astype(BF16)

        u, _ = _gelu_parts(z_ref[:, 1024:1536])
        gv, _ = _gelu_parts(z_ref[:, 1536:2048])
        vxh, _ = _ln_stats(gv)
        v = vxh * gg_ref[...] + gb_ref[...]
        low = _lane_is_low_head()
        v_lo = jnp.where(low, v, 0.0).astype(BF16)
        v_hi = jnp.where(low, 0.0, v).astype(BF16)
        _gm_mix(v_lo, v_hi, wpair_ref, bias_ref, mixed_ref, t)
        mix_ref[:, 512:1024] = (u * mixed_ref[...]).astype(BF16)

    vec = lambda n: pl.BlockSpec((1, n), lambda i: (0, 0))
    return _tied_call(
        body, after, name="seqmix_fwd", grid=(s // t,),
        in_specs=[pl.BlockSpec((t, D_MODEL), lambda i: (i, 0)),
                  pl.BlockSpec((4, D_MODEL, 512), lambda i: (0, 0, 0)), vec(2048),
                  pl.BlockSpec((CONV_HALO, CONV_WIDTH), lambda i: (0, 0)),
                  vec(512), vec(512), vec(512), vec(512), vec(512),
                  pl.BlockSpec((4, CHUNK, 2 * CHUNK), lambda i: (0, 0, 0)),
                  pl.BlockSpec((CHUNK, GM_WIDTH), lambda i: (0, 0))],
        out_specs=[pl.BlockSpec((t, 2048), lambda i: (i, 0)),
                   pl.BlockSpec((t, D_MODEL), lambda i: (i, 0)),
                   pl.BlockSpec((t, CONV_WIDTH), lambda i: (i, 0))],
        out_shape=[jax.ShapeDtypeStruct((s, 2048), F32), jax.ShapeDtypeStruct((s, D_MODEL), BF16),
                   jax.ShapeDtypeStruct((s, CONV_WIDTH), F32)],
        scratch_shapes=[pltpu.VMEM((t + CONV_HALO, CONV_WIDTH), F32),
                        pltpu.VMEM((SUBLANES - 1, t + CONV_HALO - SUBLANES, CONV_WIDTH), F32),
                        pltpu.VMEM((t, GM_WIDTH), F32)],
        compiler_params=_params(("arbitrary",)),
    )(hn, w_in, b_in, cw, cb, lng, lnb, gg, gb, wpair, bias)


def _mem_kv(mem, g, wkv):
    m = mem.shape[0]

    def body(mem_ref, g_ref, w_ref, mn_ref, kv_ref):
        mv = mem_ref[...]
        mn = (mv * _rms_stats(mv) * g_ref[...]).astype(BF16)
        mn_ref[...] = mn
        for j in range(4):
            kv_ref[:, j * 512:(j + 1) * 512] = _dot(mn, w_ref[j]).astype(BF16)

    return pl.pallas_call(
        body, name="mem_kv",
        out_shape=[jax.ShapeDtypeStruct((m, D_MODEL), BF16), jax.ShapeDtypeStruct((m, 2 * D_MODEL), BF16)],
        compiler_params=pltpu.CompilerParams(vmem_limit_bytes=VMEM_LIMIT_BYTES),
    )(mem, g, wkv)


def _softmax_rows(sc):
    e = jnp.exp(sc - jnp.max(sc, axis=-1, keepdims=True))
    return e / jnp.sum(e, axis=-1, keepdims=True)


def _attn_block_fwd(x, mix, w_out, g_xa, wq, kv, wo, g_ffn, ts, after=()):
    s, m = x.shape[0], kv.shape[0]
    scale = XA_HEAD_DIM ** -0.5

    def body(x_ref, mix_ref, wout_ref, gxa_ref, wq_ref, kv_ref, wo_ref, gffn_ref,
             h1_ref, hn2_ref, q_ref, o_ref, h2_ref, hn3_ref):
        h1 = x_ref[...] + _dot(mix_ref[...], wout_ref[...])
        h1_ref[...] = h1
        hn2 = (h1 * _rms_stats(h1) * gxa_ref[...]).astype(BF16)
        hn2_ref[...] = hn2
        q_ref[...] = _dot(hn2, wq_ref[...]).astype(BF16)
        for h in range(XA_HEADS):
            cols = slice(h * XA_HEAD_DIM, (h + 1) * XA_HEAD_DIM)
            vcols = slice(D_MODEL + h * XA_HEAD_DIM, D_MODEL + (h + 1) * XA_HEAD_DIM)
            p = _softmax_rows(_dot_nt(q_ref[:, cols], kv_ref[:, cols]) * scale)
            o_ref[:, cols] = _dot(p.astype(BF16), kv_ref[:, vcols]).astype(BF16)
        h2 = h1 + _dot(o_ref[...], wo_ref[...])
        h2_ref[...] = h2
        hn3_ref[...] = (h2 * _rms_stats(h2) * gffn_ref[...]).astype(BF16)

    row = pl.BlockSpec((ts, D_MODEL), lambda i: (i, 0))
    full = pl.BlockSpec((D_MODEL, D_MODEL), lambda i: (0, 0))
    vec = pl.BlockSpec((1, D_MODEL), lambda i: (0, 0))
    f32 = jax.ShapeDtypeStruct((s, D_MODEL), F32)
    bf16 = jax.ShapeDtypeStruct((s, D_MODEL), BF16)
    return _tied_call(
        body, after, name="attn_block_fwd", grid=(s // ts,),
        in_specs=[row, row, full, vec, full, pl.BlockSpec((m, 2 * D_MODEL), lambda i: (0, 0)), full, vec],
        out_specs=[row] * 6,
        out_shape=[f32, bf16, bf16, bf16, f32, bf16],
        compiler_params=_params(("parallel",)),
    )(x, mix, w_out, g_xa, wq, kv, wo, g_ffn)


_FFN_CHUNKS_FWD = (slice(0, 6 * LANES), slice(6 * LANES, FFN_HALF))
_FFN_CHUNKS_BWD = (slice(0, 4 * LANES), slice(4 * LANES, 8 * LANES), slice(8 * LANES, FFN_HALF))


def _ffn_up(hn, wgu, ts, after=()):
    s = hn.shape[0]

    def body(hn_ref, w_ref, gu_ref, act_ref):
        hv = hn_ref[...]
        for cols in _FFN_CHUNKS_FWD:
            gate = _dot(hv, w_ref[0, 0, :, cols])
            up = _dot(hv, w_ref[1, 0, :, cols])
            gu_ref[0, :, cols] = gate.astype(BF16)
            gu_ref[1, :, cols] = up.astype(BF16)
            act_ref[:, cols] = (gate * _sigmoid(gate) * up).astype(BF16)

    return _tied_call(
        body, after, name="ffn_up", grid=(2, s // ts),
        in_specs=[pl.BlockSpec((ts, D_MODEL), lambda j, i: (i, 0)),
                  pl.BlockSpec((2, 1, D_MODEL, FFN_HALF), lambda j, i: (0, j, 0, 0))],
        out_specs=[pl.BlockSpec((2, ts, FFN_HALF), lambda j, i: (0, i, j)),
                   pl.BlockSpec((ts, FFN_HALF), lambda j, i: (i, j))],
        out_shape=[jax.ShapeDtypeStruct((2, s, FFN_HIDDEN), BF16), jax.ShapeDtypeStruct((s, FFN_HIDDEN), BF16)],
        compiler_params=_params(("parallel", "parallel")),
    )(hn, wgu)


def _ffn_down_loss(act, wd, h2, g, target, ts):
    s = act.shape[0]

    def body(act_ref, wd_ref, h2_ref, g_ref, t_ref, dh_ref, dhb_ref, sq_ref, dg_ref):
        @pl.when(pl.program_id(0) == 0)
        def _():
            sq_ref[...] = jnp.zeros_like(sq_ref)
            dg_ref[...] = jnp.zeros_like(dg_ref)

        h3 = h2_ref[...] + _dot(act_ref[...], wd_ref[...])
        r = _rms_stats(h3)
        gv = g_ref[...]
        diff = h3 * r * gv - t_ref[...]
        sq_ref[...] += _rowsum(diff * diff)
        dh, dg = _rms_bwd(diff / D_MODEL, h3, r, gv)
        dh_ref[...] = dh
        dhb_ref[...] = dh.astype(BF16)
        dg_ref[...] += dg

    row = pl.BlockSpec((ts, D_MODEL), lambda i: (i, 0))
    vec = pl.BlockSpec((1, D_MODEL), lambda i: (0, 0))
    return pl.pallas_call(
        body, name="ffn_down_loss", grid=(s // ts,),
        in_specs=[pl.BlockSpec((ts, FFN_HIDDEN), lambda i: (i, 0)),
                  pl.BlockSpec((FFN_HIDDEN, D_MODEL), lambda i: (0, 0)), row, vec, row],
        out_specs=[row, row, vec, vec],
        out_shape=[jax.ShapeDtypeStruct((s, D_MODEL), F32), jax.ShapeDtypeStruct((s, D_MODEL), BF16),
                   jax.ShapeDtypeStruct((1, D_MODEL), F32), jax.ShapeDtypeStruct((1, D_MODEL), F32)],
        compiler_params=_params(("arbitrary",)),
    )(act, wd, h2, g, target)


def _grad_w(a, b, tk, tn, name, after=(), shards=1):
    s, k = a.shape
    gb, _, n = b.shape
    nblk = n // tn
    ws = tn // shards
    tsr = GRAD_ROWS if s % GRAD_ROWS == 0 else s

    def body(a_ref, b_ref, o_ref):
        part = _dot_tn(a_ref[...], b_ref[0])

        @pl.when(pl.program_id(2) == 0)
        def _():
            for j in range(shards):
                o_ref[j] = part[:, j * ws:(j + 1) * ws]

        @pl.when(pl.program_id(2) > 0)
        def _():
            for j in range(shards):
                o_ref[j] += part[:, j * ws:(j + 1) * ws]

    return _tied_call(
        body, after, name=name, grid=(gb * nblk, k // tk, s // tsr),
        in_specs=[pl.BlockSpec((tsr, tk), lambda ni, ki, si: (si, ki)),
                  pl.BlockSpec((1, tsr, tn), lambda ni, ki, si: (ni // nblk, si, ni % nblk))],
        out_specs=pl.BlockSpec((shards, tk, ws), lambda ni, ki, si: (ni, ki, 0)),
        out_shape=jax.ShapeDtypeStruct((gb * nblk * shards, k, ws), F32),
        compiler_params=_params(("parallel", "parallel", "arbitrary")),
    )(a, b)


def _grad_w_square(pairs, name, after=()):
    n = len(pairs)
    s = pairs[0][0].shape[0]
    tsr = GRAD_ROWS // 2 if s % (GRAD_ROWS // 2) == 0 else s

    def body(*refs):
        ins, outs = refs[:2 * n], refs[2 * n:]
        parts = [_dot_tn(ins[2 * a][...], ins[2 * a + 1][...]) for a in range(n)]

        @pl.when(pl.program_id(0) == 0)
        def _():
            for a in range(n):
                outs[a][...] = parts[a]

        @pl.when(pl.program_id(0) > 0)
        def _():
            for a in range(n):
                outs[a][...] += parts[a]

    row = pl.BlockSpec((tsr, D_MODEL), lambda i: (i, 0))
    return _tied_call(
        body, after, name=name, grid=(s // tsr,),
        in_specs=[row] * (2 * n), out_specs=[pl.BlockSpec((D_MODEL, D_MODEL), lambda i: (0, 0))] * n,
        out_shape=[jax.ShapeDtypeStruct((D_MODEL, D_MODEL), F32)] * n,
        compiler_params=_params(("arbitrary",)),
    )(*[x for p in pairs for x in p])


def _ffn_bwd(dh3, wd, gu, wgu, h2, g, t, after=()):
    s = dh3.shape[0]

    def body(dh3_ref, wd_ref, gu_ref, w_ref, h2_ref, g_ref, dgu_ref, dh2_ref, dh2b_ref, dg_ref):
        @pl.when(pl.program_id(0) == 0)
        def _():
            dg_ref[...] = jnp.zeros_like(dg_ref)

        dh3v = dh3_ref[...]
        dhb = dh3v.astype(BF16)
        dhn = jnp.zeros((t, D_MODEL), F32)
        for j in range(2):
            for cols in _FFN_CHUNKS_BWD:
                whole = slice(j * FFN_HALF + cols.start, j * FFN_HALF + cols.stop)
                dact = _dot_nt(dhb, wd_ref[j, cols, :])
                gate, up = gu_ref[0, :, whole].astype(F32), gu_ref[1, :, whole].astype(F32)
                sg = _sigmoid(gate)
                dgate = (dact * up * (sg * (1.0 + gate * (1.0 - sg)))).astype(BF16)
                dup = (dact * (gate * sg)).astype(BF16)
                dgu_ref[0, :, whole] = dgate
                dgu_ref[1, :, whole] = dup
                dhn = dhn + _dot_nt(dgate, w_ref[j, :, cols]) + _dot_nt(dup, w_ref[2 + j, :, cols])
        h2 = h2_ref[...]
        dv, dg = _rms_bwd(dhn, h2, _rms_stats(h2), g_ref[...])
        dh2 = dh3v + dv
        dh2_ref[...] = dh2
        dh2b_ref[...] = dh2.astype(BF16)
        dg_ref[...] += dg

    row = pl.BlockSpec((t, D_MODEL), lambda i: (i, 0))
    wide = pl.BlockSpec((2, t, FFN_HIDDEN), lambda i: (0, i, 0))
    vec = pl.BlockSpec((1, D_MODEL), lambda i: (0, 0))
    return _tied_call(
        body, after, name="ffn_bwd", grid=(s // t,),
        in_specs=[row, pl.BlockSpec((2, FFN_HALF, D_MODEL), lambda i: (0, 0, 0)), wide,
                  pl.BlockSpec((4, D_MODEL, FFN_HALF), lambda i: (0, 0, 0)), row, vec],
        out_specs=[wide, row, row, vec],
        out_shape=[jax.ShapeDtypeStruct((2, s, FFN_HIDDEN), BF16), jax.ShapeDtypeStruct((s, D_MODEL), F32),
                   jax.ShapeDtypeStruct((s, D_MODEL), BF16), jax.ShapeDtypeStruct((1, D_MODEL), F32)],
        compiler_params=_params(("arbitrary",)),
    )(dh3, wd, gu, wgu, h2, g)


def _attn_bwd(dh2, wo, q, kv, wq, h1, g, ts, after=()):
    s, m = q.shape[0], kv.shape[0]
    scale = XA_HEAD_DIM ** -0.5

    def body(dh2_ref, wo_ref, q_ref, kv_ref, wq_ref, h1_ref, g_ref, dh1_ref, dh1b_ref, dq_ref, dkv_ref, dg_ref):
        @pl.when(pl.program_id(0) == 0)
        def _():
            dkv_ref[...] = jnp.zeros_like(dkv_ref)
            dg_ref[...] = jnp.zeros_like(dg_ref)

        do = _dot_nt(dh2_ref[...].astype(BF16), wo_ref[...]).astype(BF16)
        for h in range(XA_HEADS):
            cols = slice(h * XA_HEAD_DIM, (h + 1) * XA_HEAD_DIM)
            vcols = slice(D_MODEL + h * XA_HEAD_DIM, D_MODEL + (h + 1) * XA_HEAD_DIM)
            qh, kh, vh, doh = q_ref[:, cols], kv_ref[:, cols], kv_ref[:, vcols], do[:, cols]
            p = _softmax_rows(_dot_nt(qh, kh) * scale)
            dp = _dot_nt(doh, vh)
            ds = (p * (dp - jnp.sum(dp * p, axis=-1, keepdims=True)) * scale).astype(BF16)
            dq_ref[:, cols] = _dot(ds, kh).astype(BF16)
            dkv_ref[:, cols] += _dot_tn(ds, qh)
            dkv_ref[:, vcols] += _dot_tn(p.astype(BF16), doh)
        dhn = _dot_nt(dq_ref[...], wq_ref[...])
        h1 = h1_ref[...]
        dv, dg = _rms_bwd(dhn, h1, _rms_stats(h1), g_ref[...])
        dh1 = dh2_ref[...] + dv
        dh1_ref[...] = dh1
        dh1b_ref[...] = dh1.astype(BF16)
        dg_ref[...] += dg

    row = pl.BlockSpec((ts, D_MODEL), lambda i: (i, 0))
    full = pl.BlockSpec((D_MODEL, D_MODEL), lambda i: (0, 0))
    kvs = pl.BlockSpec((m, 2 * D_MODEL), lambda i: (0, 0))
    vec = pl.BlockSpec((1, D_MODEL), lambda i: (0, 0))
    return _tied_call(
        body, after, name="attn_bwd", grid=(s // ts,),
        in_specs=[row, full, row, kvs, full, row, vec],
        out_specs=[row, row, row, kvs, vec],
        out_shape=[jax.ShapeDtypeStruct((s, D_MODEL), F32), jax.ShapeDtypeStruct((s, D_MODEL), BF16),
                   jax.ShapeDtypeStruct((s, D_MODEL), BF16),
                   jax.ShapeDtypeStruct((m, 2 * D_MODEL), F32), jax.ShapeDtypeStruct((1, D_MODEL), F32)],
        compiler_params=_params(("arbitrary",)),
    )(dh2, wo, q, kv, wq, h1, g)


def _mem_kv_bwd(dkv, mn, wkv, mem, g, after=()):
    m = mem.shape[0]

    def body(dkv_ref, mn_ref, w_ref, mem_ref, g_ref, dw_ref, dg_ref):
        dmn = jnp.zeros((m, D_MODEL), F32)
        mn = mn_ref[...]
        for j in range(4):
            dj = dkv_ref[:, j * 512:(j + 1) * 512].astype(BF16)
            dw_ref[j] = _dot_tn(mn, dj)
            dmn = dmn + _dot_nt(dj, w_ref[j])
        mv = mem_ref[...]
        dg_ref[...] = _rowsum(dmn * (mv * _rms_stats(mv)))

    return _tied_call(
        body, after, name="mem_kv_bwd", in_specs=[pl.BlockSpec(memory_space=pltpu.VMEM)] * 5,
        out_shape=[jax.ShapeDtypeStruct((4, D_MODEL, 512), F32), jax.ShapeDtypeStruct((1, D_MODEL), F32)],
        compiler_params=pltpu.CompilerParams(vmem_limit_bytes=VMEM_LIMIT_BYTES),
    )(dkv, mn, wkv, mem, g)


def _seqmix_bwd(dh1, x, z, c1, w_out, w_in, g_mix, cw, lng, lnb, gg, gb, wpair, wpair_t, bias, t, after=()):
    s = x.shape[0]
    nt = s // t

    def body(dh1_ref, x_ref, z_ref, c1_ref, wo_ref, wi_ref, gm_ref, cw_ref, lng_ref, lnb_ref,
             gg_ref, gb_ref, wpair_ref, wpt_ref, bias_ref,
             gx_ref, dz_ref, dcw_ref, dcb_ref, dlng_ref, dlnb_ref, dgg_ref, dgb_ref, dws_ref, dbs_ref,
             dbin_ref, dgm_ref, dbuf, dsh, mixed_ref, dv_ref):
        i = pl.program_id(0)
        accs = (dcw_ref, dcb_ref, dlng_ref, dlnb_ref, dgg_ref, dgb_ref, dws_ref, dbs_ref, dbin_ref, dgm_ref)

        @pl.when(i == 0)
        def _():
            for r in accs:
                r[...] = jnp.zeros_like(r)
            dbuf[t:t + CONV_HALO, :] = jnp.zeros((CONV_HALO, CONV_WIDTH), F32)

        @pl.when(i > 0)
        def _():
            dbuf[t:t + CONV_HALO, :] = dbuf[0:CONV_HALO, :]

        dmix = _dot_nt(dh1_ref[...].astype(BF16), wo_ref[...])

        xh, rs = _ln_stats(c1_ref[...])
        lng = lng_ref[...]
        ln = xh * lng + lnb_ref[...]
        sl = _sigmoid(ln)
        dln = dmix[:, 0:512] * (sl * (1.0 + ln * (1.0 - sl)))
        dc1, dg_ln, db_ln = _ln_bwd(dln, xh, rs, lng)
        dlng_ref[...] += dg_ln
        dlnb_ref[...] += db_ln
        dcb_ref[...] += _rowsum(dc1)
        dbuf[0:t, :] = dc1

        za = z_ref[:, 0:512]
        sg = _sigmoid(z_ref[:, 512:1024])
        a = za * sg
        _shift_rows(dbuf, dsh, t)

        da = jnp.zeros((t, CONV_WIDTH), F32)
        for k in range(CONV_KERNEL):
            later = _window(dbuf, dsh, CONV_KERNEL - 1 - k, t)
            da = da + cw_ref[k:k + 1, :] * later
            dcw_ref[k:k + 1, :] += _rowsum(a * later)
        dza = da * sg
        dzg = da * za * (sg * (1.0 - sg))
        dz_ref[:, 0:512] = dza.astype(BF16)
        dz_ref[:, 512:1024] = dzg.astype(BF16)
        dbin_ref[:, 0:512] += _rowsum(dza)
        dbin_ref[:, 512:1024] += _rowsum(dzg)

        dgm = dmix[:, 512:1024]
        u, du_dz = _gelu_parts(z_ref[:, 1024:1536])
        gv, dgv_dz = _gelu_parts(z_ref[:, 1536:2048])
        vxh, vrs = _ln_stats(gv)
        ggv = gg_ref[...]
        v = vxh * ggv + gb_ref[...]
        low = _lane_is_low_head()
        v_lo = jnp.where(low, v, 0.0).astype(BF16)
        v_hi = jnp.where(low, 0.0, v).astype(BF16)
        _gm_mix(v_lo, v_hi, wpair_ref, bias_ref, mixed_ref, t)
        dzu = dgm * mixed_ref[...] * du_dz
        dm = dgm * u
        dm_lo = jnp.where(low, dm, 0.0).astype(BF16)
        dm_hi = jnp.where(low, 0.0, dm).astype(BF16)
        vb = v.astype(BF16)
        tril = (lax.broadcasted_iota(jnp.int32, (CHUNK, CHUNK), 1)
                <= lax.broadcasted_iota(jnp.int32, (CHUNK, CHUNK), 0))
        for n in range(t // CHUNK):
            rows = slice(n * CHUNK, (n + 1) * CHUNK)
            dbs_ref[...] += dm[rows, :]
            for j in range(GM_HEADS // 2):
                cols = slice(j * LANES, (j + 1) * LANES)
                stack = jnp.concatenate([dm_lo[rows, cols], dm_hi[rows, cols]], axis=0)
                dws = _dot_nt(stack, vb[rows, cols])
                dws_ref[2 * j] += jnp.where(tril, dws[0:CHUNK], 0.0)
                dws_ref[2 * j + 1] += jnp.where(tril, dws[CHUNK:2 * CHUNK], 0.0)
                dv_ref[rows, cols] = _dot(wpt_ref[j], stack)
        dgv, dg_gm, db_gm = _ln_bwd(dv_ref[...], vxh, vrs, ggv)
        dgg_ref[...] += dg_gm
        dgb_ref[...] += db_gm
        dzv = dgv * dgv_dz
        dz_ref[:, 1024:1536] = dzu.astype(BF16)
        dz_ref[:, 1536:2048] = dzv.astype(BF16)
        dbin_ref[:, 1024:1536] += _rowsum(dzu)
        dbin_ref[:, 1536:2048] += _rowsum(dzv)

        dhn = jnp.zeros((t, D_MODEL), F32)
        for j in range(4):
            dhn = dhn + _dot_nt(dz_ref[:, j * 512:(j + 1) * 512], wi_ref[j])
        xv = x_ref[...]
        dv, dg = _rms_bwd(dhn, xv, _rms_stats(xv), gm_ref[...])
        gx_ref[...] = dh1_ref[...] + dv
        dgm_ref[...] += dg

    rev = lambda w: pl.BlockSpec((t, w), lambda i: (nt - 1 - i, 0))
    const = lambda *shape: pl.BlockSpec(shape, lambda i: (0,) * len(shape))
    f32 = lambda *shape: jax.ShapeDtypeStruct(shape, F32)
    return _tied_call(
        body, after, name="seqmix_bwd", grid=(nt,),
        in_specs=[rev(D_MODEL), rev(D_MODEL), rev(2048), rev(CONV_WIDTH),
                  const(D_MODEL, D_MODEL), const(4, D_MODEL, 512), const(1, D_MODEL),
                  const(CONV_HALO, CONV_WIDTH), const(1, 512), const(1, 512), const(1, 512), const(1, 512),
                  const(4, CHUNK, 2 * CHUNK), const(4, CHUNK, 2 * CHUNK), const(CHUNK, GM_WIDTH)],
        out_specs=[rev(D_MODEL), rev(2048),
                   const(CONV_HALO, CONV_WIDTH), const(1, 512), const(1, 512), const(1, 512), const(1, 512),
                   const(1, 512), const(GM_HEADS, CHUNK, CHUNK), const(CHUNK, GM_WIDTH), const(1, 2048),
                   const(1, D_MODEL)],
        out_shape=[f32(s, D_MODEL), jax.ShapeDtypeStruct((s, 2048), BF16),
                   f32(CONV_HALO, CONV_WIDTH), f32(1, 512), f32(1, 512), f32(1, 512), f32(1, 512),
                   f32(1, 512), f32(GM_HEADS, CHUNK, CHUNK), f32(CHUNK, GM_WIDTH), f32(1, 2048),
                   f32(1, D_MODEL)],
        scratch_shapes=[pltpu.VMEM((t + CONV_HALO, CONV_WIDTH), F32),
                        pltpu.VMEM((SUBLANES - 1, t + CONV_HALO - SUBLANES, CONV_WIDTH), F32),
                        pltpu.VMEM((t, GM_WIDTH), F32), pltpu.VMEM((t, GM_WIDTH), F32)],
        compiler_params=_params(("arbitrary",)),
    )(dh1, x, z, c1, w_out, w_in, g_mix, cw, lng, lnb, gg, gb, wpair, wpair_t, bias)


def _head_bias_grad(dbs):
    def body(d_ref, o_ref):
        dv = d_ref[...]
        lane = lax.broadcasted_iota(jnp.int32, (CHUNK, LANES), 1)
        acc = jnp.zeros((CHUNK, LANES), F32)
        for h in range(GM_HEADS):
            sh = jnp.sum(dv[:, h * GM_HEAD_DIM:(h + 1) * GM_HEAD_DIM], axis=-1, keepdims=True)
            acc = acc + jnp.where(lane == h, sh, 0.0)
        o_ref[...] = acc

    return pl.pallas_call(body, name="head_bias_grad",
                          out_shape=jax.ShapeDtypeStruct((CHUNK, LANES), F32))(dbs)


def kernel(x, mem, norm_mix_g, w_in, b_in, conv_w, conv_b, conv_ln_g, conv_ln_b, gm_ln_g, gm_ln_b, gm_w_s, gm_b_s, w_out, norm_xa_g, mem_norm_g, xa_wq, xa_wkv, xa_wo, norm_ffn_g, ffn_w_gate_up, ffn_w_down, final_norm_g, loss_target, m_norm_mix_g, m_w_in, m_b_in, m_conv_w, m_conv_b, m_conv_ln_g, m_conv_ln_b, m_gm_ln_g, m_gm_ln_b, m_gm_w_s, m_gm_b_s, m_w_out, m_norm_xa_g, m_mem_norm_g, m_xa_wq, m_xa_wkv, m_xa_wo, m_norm_ffn_g, m_ffn_w_gate_up, m_ffn_w_down, m_final_norm_g, v_norm_mix_g, v_w_in, v_b_in, v_conv_w, v_conv_b, v_conv_ln_g, v_conv_ln_b, v_gm_ln_g, v_gm_ln_b, v_gm_w_s, v_gm_b_s, v_w_out, v_norm_xa_g, v_mem_norm_g, v_xa_wq, v_xa_wkv, v_xa_wo, v_norm_ffn_g, v_ffn_w_gate_up, v_ffn_w_down, v_final_norm_g):
    weights = dict(norm_mix_g=norm_mix_g, w_in=w_in, b_in=b_in, conv_w=conv_w, conv_b=conv_b, conv_ln_g=conv_ln_g,
                   conv_ln_b=conv_ln_b, gm_ln_g=gm_ln_g, gm_ln_b=gm_ln_b, gm_w_s=gm_w_s, gm_b_s=gm_b_s, w_out=w_out,
                   norm_xa_g=norm_xa_g, mem_norm_g=mem_norm_g, xa_wq=xa_wq, xa_wkv=xa_wkv, xa_wo=xa_wo,
                   norm_ffn_g=norm_ffn_g, ffn_w_gate_up=ffn_w_gate_up, ffn_w_down=ffn_w_down,
                   final_norm_g=final_norm_g)
    m_in = dict(norm_mix_g=m_norm_mix_g, w_in=m_w_in, b_in=m_b_in, conv_w=m_conv_w, conv_b=m_conv_b,
                conv_ln_g=m_conv_ln_g, conv_ln_b=m_conv_ln_b, gm_ln_g=m_gm_ln_g, gm_ln_b=m_gm_ln_b, gm_w_s=m_gm_w_s,
                gm_b_s=m_gm_b_s, w_out=m_w_out, norm_xa_g=m_norm_xa_g, mem_norm_g=m_mem_norm_g, xa_wq=m_xa_wq,
                xa_wkv=m_xa_wkv, xa_wo=m_xa_wo, norm_ffn_g=m_norm_ffn_g, ffn_w_gate_up=m_ffn_w_gate_up,
                ffn_w_down=m_ffn_w_down, final_norm_g=m_final_norm_g)
    v_in = dict(norm_mix_g=v_norm_mix_g, w_in=v_w_in, b_in=v_b_in, conv_w=v_conv_w, conv_b=v_conv_b,
                conv_ln_g=v_conv_ln_g, conv_ln_b=v_conv_ln_b, gm_ln_g=v_gm_ln_g, gm_ln_b=v_gm_ln_b, gm_w_s=v_gm_w_s,
                gm_b_s=v_gm_b_s, w_out=v_w_out, norm_xa_g=v_norm_xa_g, mem_norm_g=v_mem_norm_g, xa_wq=v_xa_wq,
                xa_wkv=v_xa_wkv, xa_wo=v_xa_wo, norm_ffn_g=v_norm_ffn_g, ffn_w_gate_up=v_ffn_w_gate_up,
                ffn_w_down=v_ffn_w_down, final_norm_g=v_final_norm_g)
    grads, delta, new_m, new_v = {}, {}, {}, {}

    s = x.shape[1]
    ts = _row_tile(s)
    tb = max(CHUNK, ts // 2)
    tw = 2 * ts if s % (2 * ts) == 0 and ts >= 512 else ts
    cx, cy, cc = _mesh_pos()
    chip = 2 * cx + cy
    pos = jnp.stack([chip, cc]).astype(jnp.int32)
    row = lambda a: a.reshape(1, -1)
    x2, mem2, tgt2 = x[0], mem[0], loss_target[0]

    big = dict(w_in=w_in, xa_wkv=xa_wkv, w_out=w_out, xa_wq=xa_wq, xa_wo=xa_wo,
               ffn_w_gate_up=ffn_w_gate_up, ffn_w_down=ffn_w_down)
    big_names = list(big)
    halves = lambda a: a.reshape(2, a.shape[0] // 2, a.shape[1])
    conv_w_pad = jnp.pad(conv_w, ((0, CONV_HALO - CONV_KERNEL), (0, 0)))
    first_names = ["w_in", "conv_w"]
    later_names = [nm for nm in big_names if nm != "w_in"]
    cast = dict(zip(first_names, _cast_into_slots([halves(w_in), halves(conv_w_pad)], pos, [BF16, F32], "cast_w_in")))
    cast.update(zip(later_names, _cast_into_slots([halves(big[nm]) for nm in later_names], pos,
                                                  [BF16] * len(later_names), "cast_" + later_names[0])))

    def start_gather(names, after):
        return _gather_start([cast[nm] for nm in names], "gather_start_" + names[0], after)

    def land_gather(names, started, after):
        send_sems, recv_sems, bufs, _ = started
        return _gather_wait(send_sems, recv_sems, bufs, after, "gather_wait_" + names[0])

    def share_gather(names, landed, after=()):
        return dict(zip(names, (b.reshape(N_CHIPS, -1, b.shape[-1])
                                for b in _pass_to_sibling(landed, "pass_" + names[0], after))))

    tril = jnp.tril(jnp.ones((CHUNK, CHUNK), dtype=bool))
    ws = jnp.where(tril[None], gm_w_s, 0.0)
    wpair = jnp.concatenate([ws[0::2], ws[1::2]], axis=2).astype(BF16)
    ws_t = jnp.swapaxes(ws, 1, 2)
    wpair_t = jnp.concatenate([ws_t[0::2], ws_t[1::2]], axis=2).astype(BF16)
    bias = jnp.repeat(gm_b_s.T, GM_HEAD_DIM, axis=1)

    attn_names = ["w_out", "xa_wq", "xa_wkv", "xa_wo"]
    gather_first = start_gather(first_names, ())
    hn1 = _norm_in(x2, row(norm_mix_g), tw, after=(gather_first[3], wpair, wpair_t, bias))
    landed = land_gather(first_names, gather_first, [cast[nm] for nm in later_names] + [hn1])
    gather_attn = start_gather(attn_names, landed)
    gw = share_gather(first_names, landed, gather_attn[3])
    w_in_g = gw["w_in"]
    cw_g = jnp.concatenate([gw["conv_w"][k] for k in range(N_CHIPS)], axis=1)

    z, mix, c1 = _seqmix_fwd(hn1, w_in_g, row(b_in), cw_g, row(conv_b), row(conv_ln_g), row(conv_ln_b),
                             row(gm_ln_g), row(gm_ln_b), wpair, bias, ts)
    landed = land_gather(attn_names, gather_attn, mix)
    gather_gu = start_gather(["ffn_w_gate_up"], landed)
    gw = share_gather(attn_names, landed, gather_gu[3])
    w_out_g = gw["w_out"].reshape(D_MODEL, D_MODEL)
    wq_g = gw["xa_wq"].reshape(D_MODEL, D_MODEL)
    wkv_g = gw["xa_wkv"]
    wo_g = gw["xa_wo"].reshape(D_MODEL, D_MODEL)
    mn, kv = _mem_kv(mem2, row(mem_norm_g), wkv_g)
    h1, hn2, q, o, h2, hn3 = _attn_block_fwd(x2, mix, w_out_g, row(norm_xa_g), wq_g, kv, wo_g, row(norm_ffn_g), ts)
    landed = land_gather(["ffn_w_gate_up"], gather_gu, hn3)
    gather_down = start_gather(["ffn_w_down"], landed)
    wgu_g = share_gather(["ffn_w_gate_up"], landed, gather_down[3])["ffn_w_gate_up"]
    gu, act = _ffn_up(hn3, wgu_g.reshape(2, 2, D_MODEL, FFN_HALF), tw)
    landed = land_gather(["ffn_w_down"], gather_down, act)
    wd_g = share_gather(["ffn_w_down"], landed)["ffn_w_down"].reshape(FFN_HIDDEN, D_MODEL)
    dh3, dh3_b, sq, d_final_g = _ffn_down_loss(act, wd_g, h2, row(final_norm_g), tgt2, ts)
    loss_here = jnp.broadcast_to(0.5 * jnp.sum(sq) / D_MODEL, (1, 2, SUBLANES, LANES))

    def split(g, nm):
        r, c = big[nm].shape
        return g.reshape(N_CHIPS, 2, r // 2, c)

    def chip_sums(group, arrays, got):
        sums, parts = [None] * len(group), [None] * len(group)
        for blocks in (N_CHIPS, 1):
            idx = [i for i, a in enumerate(arrays) if a.shape[0] == blocks]
            if idx:
                out = _add_halves([arrays[i] for i in idx], [got[i] for i in idx], pos, "chip_sum_" + group[idx[0]],
                                  [F32 if group[i] == "loss" else BF16 for i in idx])
                for k, i in enumerate(idx):
                    sums[i], parts[i] = out[0][k], out[1][k]
        return sums, parts

    def start_swap(group, grads):
        return _swap_start([split(g, nm) for g, nm in zip(grads, group)], "swap_start_" + group[0])

    def start_exchange(group, swapping, after, landed):
        sems, arrays, lands, _ = swapping
        arrays, got = _swap_wait(sems, arrays, lands, after, "swap_wait_" + group[0])
        sums, parts = chip_sums(group, arrays, got)
        return _exchange_start(sums, parts, "exchange_start_" + group[0], landed)

    def wait_exchange(group, started, after):
        sems, sums, parts, _ = started
        return _exchange_wait(sems, sums, parts, after, "exchange_wait_" + group[0])

    def finish_exchange(group, started, after):
        return _sum_chips(wait_exchange(group, started, after), pos, "total_" + group[0])

    def join_and_update(group, after):
        joined = _join_halves([halves_of[nm] for nm in group], "join_halves_" + group[0], after)
        outs = _adamw([(weights[nm], j.reshape(big[nm].shape), m_in[nm], v_in[nm]) for nm, j in zip(group, joined)],
                      "adamw_" + group[0])
        for nm, out in zip(group, outs):
            grads[nm], delta[nm], new_m[nm], new_v[nm] = out
        return [new_v[nm] for nm in group]

    as3 = lambda a: a.reshape((1,) + a.shape)
    halves_of = {}

    g_down = _grad_w(act, as3(dh3_b), FFN_HALF, D_MODEL, "grad_ffn_w_down")
    group_a = ["ffn_w_down"]
    swap_a = start_swap(group_a, [g_down])
    dgu, dh2, dh2_b, d_ffn_g = _ffn_bwd(dh3, wd_g.reshape(2, FFN_HALF, D_MODEL), gu, wgu_g, h2, row(norm_ffn_g), tb,
                                        after=swap_a[3])
    exch_a = start_exchange(group_a, swap_a, dh2, wd_g)
    g_gu = _grad_w(hn3, dgu, D_MODEL, FFN_HALF, "grad_ffn_w_gate_up", after=exch_a[3])
    halves_of.update(zip(group_a, finish_exchange(group_a, exch_a, g_gu)))

    group_b = ["ffn_w_gate_up"]
    swap_b = start_swap(group_b, [g_gu])
    dh1, dh1_b, dq, dkv, d_xa_g = _attn_bwd(dh2, wo_g, q, kv, wq_g, h1, row(norm_xa_g), ts, after=swap_b[3])
    exch_b = start_exchange(group_b, swap_b, dh1, [halves_of[nm] for nm in group_a])
    g_wkv, d_mem_g = _mem_kv_bwd(dkv, mn, wkv_g, mem2, row(mem_norm_g), after=exch_b[3])
    g_wo, g_wq, g_wout = _grad_w_square([(o, dh2_b), (hn2, dq), (mix, dh1_b)], "grad_xa_wo", after=exch_b[3])
    done_a = join_and_update(group_a, (g_wkv, g_wo, g_wq, g_wout))
    halves_of.update(zip(group_b, finish_exchange(group_b, exch_b, done_a)))

    group_c = ["xa_wo", "xa_wq", "xa_wkv", "w_out"]
    swap_c = start_swap(group_c, [g_wo, g_wq, g_wkv, g_wout])
    (gx, dz, d_cw, d_cb, d_lng, d_lnb, d_gg, d_gb, d_ws, d_bs_sum, d_bin, d_mix_g) = _seqmix_bwd(
        dh1, x2, z, c1, w_out_g, w_in_g, row(norm_mix_g), cw_g, row(conv_ln_g), row(conv_ln_b),
        row(gm_ln_g), row(gm_ln_b), wpair, wpair_t, bias, tb, after=swap_c[3])
    d_bs = _head_bias_grad(d_bs_sum)[:, :GM_HEADS].T
    exch_c = start_exchange(group_c, swap_c, dz, [halves_of[nm] for nm in group_b])
    g_win = _grad_w(hn1, as3(dz), D_MODEL, 1024, "grad_w_in", after=exch_c[3], shards=2)
    done_b = join_and_update(group_b, g_win)
    parts_c = wait_exchange(group_c, exch_c, (g_win, *done_b))

    small_names = ["norm_mix_g", "b_in", "conv_w", "conv_b", "conv_ln_g", "conv_ln_b", "gm_ln_g", "gm_ln_b",
                   "gm_w_s", "gm_b_s", "norm_xa_g", "mem_norm_g", "norm_ffn_g", "final_norm_g"]
    d_cw_by_chip = jnp.swapaxes(d_cw.reshape(CONV_HALO, N_CHIPS, LANES), 0, 1).reshape(-1, LANES)
    small_grads = dict(norm_mix_g=d_mix_g, b_in=d_bin, conv_w=d_cw_by_chip, conv_b=d_cb, conv_ln_g=d_lng,
                       conv_ln_b=d_lnb, gm_ln_g=d_gg, gm_ln_b=d_gb, gm_w_s=d_ws, gm_b_s=d_bs, norm_xa_g=d_xa_g,
                       mem_norm_g=d_mem_g, norm_ffn_g=d_ffn_g, final_norm_g=d_final_g)

    def rows_form(a):
        a = a.reshape(-1, LANES)
        return jnp.pad(a, ((0, -a.shape[0] % SUBLANES), (0, 0)))

    pieces = [rows_form(small_grads[nm]) for nm in small_names]
    offsets, total = [], 0
    for p in pieces:
        offsets.append(total)
        total += p.shape[0]
    pack_rows = -(-total // 32) * 32
    small_pack = jnp.pad(jnp.concatenate(pieces, axis=0), ((0, pack_rows - total), (0, 0)))

    group_d = ["w_in", "small", "loss"]
    arrays_d = [split(g_win, "w_in"), small_pack.reshape(1, 2, pack_rows // 2, LANES), loss_here]
    sums_d, parts_d = chip_sums(group_d, arrays_d, _swap_halves(arrays_d, "swap_halves_w_in"))
    exch_d = _exchange_start(sums_d, parts_d, "exchange_start_w_in", parts_c)
    halves_of.update(zip(group_c, _sum_chips(parts_c, pos, "total_xa_wo", exch_d[3])))
    done_c = join_and_update(group_c, exch_d[3])
    halves_of.update(zip(group_d, finish_exchange(group_d, exch_d, done_c)))
    joined_d = _join_halves([halves_of[nm] for nm in group_d], "join_halves_w_in")
    loss = joined_d[2][0, 0, 0]
    grads["w_in"], delta["w_in"], new_m["w_in"], new_v["w_in"] = _adamw(
        [(w_in, joined_d[0].reshape(w_in.shape), m_w_in, v_w_in)], "adamw_w_in")[0]

    local_rows = lambda a, nm: a if nm == "conv_w" else a.reshape(-1, LANES)
    params = [tuple(local_rows(src[nm], nm) for src in (weights, m_in, v_in)) for nm in small_names]
    outs = _adamw_small(joined_d[1].reshape(pack_rows, LANES), pos, params, offsets, small_names.index("conv_w"))
    for k, nm in enumerate(small_names):
        for dst, a in zip((grads, delta, new_m, new_v), outs[4 * k:4 * k + 4]):
            dst[nm] = a

    order = ["norm_mix_g", "w_in", "b_in", "conv_w", "conv_b", "conv_ln_g", "conv_ln_b", "gm_ln_g", "gm_ln_b",
             "gm_w_s", "gm_b_s", "w_out", "norm_xa_g", "mem_norm_g", "xa_wq", "xa_wkv", "xa_wo", "norm_ffn_g",
             "ffn_w_gate_up", "ffn_w_down", "final_norm_g"]
    fit = lambda a, nm: a.reshape(weights[nm].shape)
    return (loss, gx.reshape(x.shape),
            *[fit(grads[nm], nm) for nm in order], *[fit(delta[nm], nm) for nm in order],
            *[fit(new_m[nm], nm) for nm in order], *[fit(new_v[nm], nm) for nm in order])
```

```python
import functools

import jax
import jax.numpy as jnp
from jax import lax
from jax.experimental import pallas as pl
from jax.experimental.pallas import tpu as pltpu

F32 = jnp.float32
BF16 = jnp.bfloat16

D_MODEL = 1024
CONV_WIDTH = 512
GM_WIDTH = 512
CONV_KERNEL = 31
CONV_HALO = 32
GRAD_ROWS = 2048
CHUNK = 128
GM_HEADS = 8
GM_HEAD_DIM = 64
XA_HEADS = 4
XA_HEAD_DIM = 256
FFN_HIDDEN = 2816
FFN_HALF = FFN_HIDDEN // 2
RMS_EPS = 1e-6
LN_EPS = 1e-5
N_CHIPS = 4
LANES = 128
SUBLANES = 8

ADAM_LR = 0.001
ADAM_B1 = 0.9
ADAM_B2 = 0.999
ADAM_EPS = 1e-08
ADAM_WD = 0.01
ADAM_STEP = 10

VMEM_LIMIT_BYTES = 56 * 1024 * 1024
MESH = pl.DeviceIdType.MESH
ANY = pl.BlockSpec(memory_space=pl.ANY)
HBM_SPEC = pl.BlockSpec(memory_space=pltpu.HBM)
SEM_SPEC = pl.BlockSpec(memory_space=pltpu.SEMAPHORE)

_NT = (((1,), (1,)), ((), ()))
_TN = (((0,), (0,)), ((), ()))
_GELU_C = 0.7978845608028654
_GELU_A = 0.044715


def _dot(a, b):
    return jnp.dot(a, b, preferred_element_type=F32)


def _dot_nt(a, b):
    return lax.dot_general(a, b, _NT, preferred_element_type=F32)


def _dot_tn(a, b):
    return lax.dot_general(a, b, _TN, preferred_element_type=F32)


def _mean(v):
    return jnp.mean(v, axis=-1, keepdims=True)


def _rowsum(v):
    return jnp.sum(v, axis=0, keepdims=True)


def _sigmoid(v):
    return 1.0 / (1.0 + jnp.exp(-v))


def _gelu_parts(v):
    v2 = v * v
    t = jnp.tanh(_GELU_C * (v + _GELU_A * v * v2))
    g = 0.5 * v * (1.0 + t)
    dg = 0.5 * (1.0 + t) + 0.5 * v * (1.0 - t * t) * (_GELU_C * (1.0 + 3.0 * _GELU_A * v2))
    return g, dg


def _rms_stats(v):
    return lax.rsqrt(_mean(v * v) + RMS_EPS)


def _rms_bwd(dy, v, r, g):
    n = v * r
    dn = dy * g
    dv = r * (dn - n * _mean(dn * n))
    return dv, _rowsum(dy * n)


def _ln_stats(v):
    mu = _mean(v)
    xc = v - mu
    rs = lax.rsqrt(_mean(xc * xc) + LN_EPS)
    return xc * rs, rs


def _ln_bwd(dy, xh, rs, g):
    dxh = dy * g
    dv = rs * (dxh - _mean(dxh) - xh * _mean(dxh * xh))
    return dv, _rowsum(dy * xh), _rowsum(dy)


def _params(sem):
    return pltpu.CompilerParams(dimension_semantics=sem, vmem_limit_bytes=VMEM_LIMIT_BYTES)


def _row_tile(s):
    return 512 if s % 512 == 0 and s >= 2048 else 128


def _mesh_pos():
    return lax.axis_index("x"), lax.axis_index("y"), lax.axis_index("c")


def _cast_into_slots(ws, pos, dtypes, name):
    n = len(ws)

    def body(pos_ref, *refs):
        for a in range(n):
            refs[n + a][0] = refs[a][...].astype(dtypes[a])

    return pl.pallas_call(
        body, name=name,
        grid_spec=pltpu.PrefetchScalarGridSpec(
            num_scalar_prefetch=1, grid=(2,),
            in_specs=[pl.BlockSpec((1,) + w.shape[1:], lambda i, p: (i, 0, 0)) for w in ws],
            out_specs=[pl.BlockSpec((1, 1) + w.shape[1:], lambda i, p: (p[0], i, 0, 0)) for w in ws]),
        out_shape=[jax.ShapeDtypeStruct((N_CHIPS,) + w.shape, dt) for w, dt in zip(ws, dtypes)],
        compiler_params=_params(("parallel",)),
    )(pos, *ws)


def _adam_update(w, g, m, v):
    nm = ADAM_B1 * m + (1.0 - ADAM_B1) * g
    nv = ADAM_B2 * v + (1.0 - ADAM_B2) * (g * g)
    m_hat = nm / (1.0 - ADAM_B1 ** ADAM_STEP)
    v_hat = nv / (1.0 - ADAM_B2 ** ADAM_STEP)
    return -ADAM_LR * (m_hat / (jnp.sqrt(v_hat) + ADAM_EPS) + ADAM_WD * w), nm, nv


ADAM_STEPS = 4


def _adamw(quads, name, after=()):
    n = len(quads)

    def body(*refs):
        ins, outs = refs[:4 * n], refs[4 * n:]
        for a in range(n):
            w, g, m, v = (r[...] for r in ins[4 * a:4 * a + 4])
            outs[4 * a][...] = g
            outs[4 * a + 1][...], outs[4 * a + 2][...], outs[4 * a + 3][...] = _adam_update(w, g, m, v)

    specs = [pl.BlockSpec((q[0].shape[0] // ADAM_STEPS, q[0].shape[1]), lambda i: (i, 0)) for q in quads]
    out = _tied_call(
        body, after, name=name, grid=(ADAM_STEPS,),
        in_specs=[sp for sp in specs for _ in range(4)], out_specs=[sp for sp in specs for _ in range(4)],
        out_shape=[jax.ShapeDtypeStruct(q[0].shape, F32) for q in quads for _ in range(4)],
        compiler_params=_params(("parallel",)),
    )(*[a for q in quads for a in q])
    return [tuple(out[4 * a:4 * a + 4]) for a in range(n)]


def _adamw_small(gpack, pos, params, offsets, conv_at):
    n = len(params)

    def body(pos_ref, g_ref, *refs):
        ins, outs = refs[:3 * n], refs[3 * n:]
        for k in range(n):
            rows = params[k][0].shape[0]
            start = offsets[k]
            if k == conv_at:
                start = pl.multiple_of(start + pos_ref[0] * CONV_HALO, SUBLANES)
            g = g_ref[pl.ds(start, rows), :]
            outs[4 * k][...] = g
            outs[4 * k + 1][...], outs[4 * k + 2][...], outs[4 * k + 3][...] = _adam_update(
                ins[3 * k][...], g, ins[3 * k + 1][...], ins[3 * k + 2][...])

    flat = [a for p in params for a in p]
    vmem = pl.BlockSpec(memory_space=pltpu.VMEM)
    return pl.pallas_call(
        body, name="adamw_small",
        in_specs=[pl.BlockSpec(memory_space=pltpu.SMEM), vmem] + [vmem] * len(flat),
        out_specs=[vmem] * (4 * n),
        out_shape=[jax.ShapeDtypeStruct(p[0].shape, F32) for p in params for _ in range(4)],
    )(pos, gpack, *flat)


def _as_tuple(after):
    return tuple(after) if isinstance(after, (tuple, list)) else (after,)


def _tied_call(body, after, *, in_specs, **kwargs):
    after = _as_tuple(after)
    n_in, n_after = len(in_specs), len(after)

    def tied(*refs):
        body(*refs[:n_in], *refs[n_in + n_after:])

    call = pl.pallas_call(tied, in_specs=list(in_specs) + [ANY] * n_after, **kwargs)
    return lambda *operands: call(*operands, *after)


def _other_chips(x, y):
    return [(1 - x, y), (x, 1 - y), (1 - x, 1 - y)]


def _gather_descriptors(bufs, send_of, recv_of):
    x, y, c = _mesh_pos()
    me = 2 * x + y
    chips = _other_chips(x, y)
    sends, arrivals = [], []
    for a in range(len(bufs)):
        for k in range(3):
            ck = 2 * chips[k][0] + chips[k][1]

            def copy(slot, a=a, k=k):
                return pltpu.make_async_remote_copy(
                    src_ref=bufs[a].at[slot, c], dst_ref=bufs[a].at[slot, c],
                    send_sem=send_of(a, k), recv_sem=recv_of(a, k),
                    device_id=(*chips[k], c), device_id_type=MESH)

            sends.append(functools.partial(copy, me))
            arrivals.append(functools.partial(copy, ck))
    return sends, arrivals


def _gather_start(bufs, name, after=()):
    n = len(bufs)
    ns = 3 * n

    def body(*refs):
        sems = refs[n:n + 2 * ns]
        thru = refs[n + 2 * ns:2 * n + 2 * ns]
        token = refs[2 * n + 2 * ns]
        _chips_handshake()
        sends, _ = _gather_descriptors(thru, lambda a, k: sems[3 * a + k], lambda a, k: sems[ns + 3 * a + k])
        for cp in sends:
            cp().start()
        token[...] = jnp.zeros_like(token)

    held = [pltpu.with_memory_space_constraint(b, pltpu.HBM) for b in bufs]
    out = _tied_call(
        body, after, name=name,
        out_shape=(*[pltpu.SemaphoreType.DMA(())] * (2 * ns), *[pltpu.HBM(b.shape, b.dtype) for b in held],
                   jax.ShapeDtypeStruct((8, LANES), F32)),
        in_specs=[HBM_SPEC] * n,
        out_specs=(*[SEM_SPEC] * (2 * ns), *[HBM_SPEC] * n, pl.BlockSpec(memory_space=pltpu.VMEM)),
        input_output_aliases={i: 2 * ns + i for i in range(n)},
        compiler_params=pltpu.CompilerParams(has_side_effects=pltpu.SideEffectType.DATAFLOW_SIDE_EFFECTING,
                                             collective_id=CHIPS_COLLECTIVE_ID),
    )(*held)
    return list(out[:ns]), list(out[ns:2 * ns]), list(out[2 * ns:2 * ns + n]), out[2 * ns + n]


def _gather_wait(send_sems, recv_sems, bufs, after, name):
    n = len(bufs)
    ns = 3 * n

    def body(*refs):
        buf_ref = refs[:n]
        sem_ref = refs[n:n + 2 * ns]
        sends, arrivals = _gather_descriptors(buf_ref, lambda a, k: sem_ref[3 * a + k],
                                              lambda a, k: sem_ref[ns + 3 * a + k])
        for cp in sends:
            cp().wait_send()
        for cp in arrivals:
            cp().wait_recv()

    out = pl.pallas_call(
        body, name=name,
        out_shape=tuple(pltpu.HBM(b.shape, b.dtype) for b in bufs),
        in_specs=[HBM_SPEC] * n + [SEM_SPEC] * (2 * ns) + [ANY] * len(_as_tuple(after)),
        out_specs=tuple([HBM_SPEC] * n),
        input_output_aliases={i: i for i in range(n)},
        compiler_params=pltpu.CompilerParams(has_side_effects=pltpu.SideEffectType.DATAFLOW_SIDE_EFFECTING),
    )(*bufs, *send_sems, *recv_sems, *_as_tuple(after))
    return list(out)


SIBLING_COLLECTIVE_ID = 0


def _sibling_handshake():
    x, y, c = _mesh_pos()
    barrier = pltpu.get_barrier_semaphore()
    pl.semaphore_signal(barrier, inc=1, device_id=(x, y, 1 - c), device_id_type=MESH)
    pl.semaphore_wait(barrier, 1)


CHIPS_COLLECTIVE_ID = 1


def _chips_handshake():
    x, y, c = _mesh_pos()
    barrier = pltpu.get_barrier_semaphore()
    for chip in _other_chips(x, y):
        pl.semaphore_signal(barrier, inc=1, device_id=(*chip, c), device_id_type=MESH)
    pl.semaphore_wait(barrier, 3)


def _pass_to_sibling(bufs, name, after=()):
    n = len(bufs)

    def body(*refs):
        outs = refs[n:2 * n]
        send_sem, recv_sem = refs[2 * n:]
        x, y, c = _mesh_pos()
        chips = _other_chips(x, y)
        _sibling_handshake()

        def half(a, k, which):
            ck = 2 * chips[k][0] + chips[k][1]
            return pltpu.make_async_remote_copy(
                src_ref=outs[a].at[ck, which], dst_ref=outs[a].at[ck, which],
                send_sem=send_sem.at[a, k], recv_sem=recv_sem.at[a, k],
                device_id=(x, y, 1 - c), device_id_type=MESH)

        sends = [half(a, k, c) for a in range(n) for k in range(3)]
        for cp in sends:
            cp.start()
        for a in range(n):
            for k in range(3):
                half(a, k, 1 - c).wait_recv()
        for cp in sends:
            cp.wait_send()

    return _tied_call(
        body, after, name=name,
        in_specs=[ANY] * n, out_specs=[ANY] * n,
        out_shape=[jax.ShapeDtypeStruct(b.shape, b.dtype) for b in bufs],
        input_output_aliases={a: a for a in range(n)},
        scratch_shapes=[pltpu.SemaphoreType.DMA((n, 3))] * 2,
        compiler_params=pltpu.CompilerParams(collective_id=SIBLING_COLLECTIVE_ID),
    )(*bufs)


def _swap_halves(grads, name):
    n = len(grads)

    def body(*refs):
        ins, outs = refs[:n], refs[n:2 * n]
        send_sem, recv_sem = refs[2 * n:]
        x, y, c = _mesh_pos()
        _sibling_handshake()
        cps = [pltpu.make_async_remote_copy(
            src_ref=ins[a].at[:, pl.ds(1 - c, 1)], dst_ref=outs[a],
            send_sem=send_sem.at[a], recv_sem=recv_sem.at[a],
            device_id=(x, y, 1 - c), device_id_type=MESH) for a in range(n)]
        for cp in cps:
            cp.start()
        for cp in cps:
            cp.wait()

    out_shape = [jax.ShapeDtypeStruct((g.shape[0], 1) + g.shape[2:], g.dtype) for g in grads]
    return pl.pallas_call(
        body, name=name,
        in_specs=[ANY] * n, out_specs=[ANY] * n, out_shape=out_shape,
        scratch_shapes=[pltpu.SemaphoreType.DMA((n,))] * 2,
        compiler_params=pltpu.CompilerParams(collective_id=SIBLING_COLLECTIVE_ID),
    )(*grads)


def _swap_descriptors(grads, lands, send_of, recv_of):
    x, y, c = _mesh_pos()
    return [functools.partial(
        pltpu.make_async_remote_copy,
        src_ref=grads[a].at[:, pl.ds(1 - c, 1)], dst_ref=lands[a],
        send_sem=send_of(a), recv_sem=recv_of(a),
        device_id=(x, y, 1 - c), device_id_type=MESH) for a in range(len(grads))]


def _swap_start(grads, name):
    n = len(grads)

    def body(*refs):
        sems = refs[2 * n:4 * n]
        g_thru, l_thru = refs[4 * n:5 * n], refs[5 * n:6 * n]
        token = refs[6 * n]
        _sibling_handshake()
        for cp in _swap_descriptors(g_thru, l_thru, lambda a: sems[a], lambda a: sems[n + a]):
            cp().start()
        token[...] = jnp.zeros_like(token)

    lands = [lax.empty((g.shape[0], 1) + g.shape[2:], g.dtype) for g in grads]
    held = [pltpu.with_memory_space_constraint(a, pltpu.HBM) for a in (*grads, *lands)]
    out = pl.pallas_call(
        body, name=name,
        out_shape=(*[pltpu.SemaphoreType.DMA(())] * (2 * n), *[pltpu.HBM(a.shape, a.dtype) for a in held],
                   jax.ShapeDtypeStruct((8, LANES), F32)),
        in_specs=[HBM_SPEC] * (2 * n),
        out_specs=(*[SEM_SPEC] * (2 * n), *[HBM_SPEC] * (2 * n), pl.BlockSpec(memory_space=pltpu.VMEM)),
        input_output_aliases={i: 2 * n + i for i in range(2 * n)},
        compiler_params=pltpu.CompilerParams(has_side_effects=pltpu.SideEffectType.DATAFLOW_SIDE_EFFECTING,
                                             collective_id=SIBLING_COLLECTIVE_ID),
    )(*held)
    return list(out[:2 * n]), list(out[2 * n:3 * n]), list(out[3 * n:4 * n]), out[4 * n]


def _swap_wait(sems, grads, lands, after, name):
    n = len(grads)

    def body(*refs):
        g_ref, l_ref = refs[:n], refs[n:2 * n]
        sem_ref = refs[2 * n:4 * n]
        for cp in _swap_descriptors(g_ref, l_ref, lambda a: sem_ref[a], lambda a: sem_ref[n + a]):
            cp().wait()

    out = pl.pallas_call(
        body, name=name,
        out_shape=tuple(pltpu.HBM(a.shape, a.dtype) for a in (*grads, *lands)),
        in_specs=[HBM_SPEC] * (2 * n) + [SEM_SPEC] * (2 * n) + [ANY] * len(_as_tuple(after)),
        out_specs=tuple([HBM_SPEC] * (2 * n)),
        input_output_aliases={i: i for i in range(2 * n)},
        compiler_params=pltpu.CompilerParams(has_side_effects=pltpu.SideEffectType.DATAFLOW_SIDE_EFFECTING),
    )(*grads, *lands, *sems, *_as_tuple(after))
    return list(out[:n]), list(out[n:])


def _add_halves(gs, gots, pos, name, dtypes):
    n = len(gs)
    j = gs[0].shape[0]

    def body(pos_ref, *refs):
        g_refs, r_refs = refs[:n], refs[n:2 * n]
        o_refs, p_refs = refs[2 * n:3 * n], refs[3 * n:]
        vals = [(g_refs[a][0, 0] + r_refs[a][0, 0]).astype(dtypes[a]) for a in range(n)]
        for a in range(n):
            o_refs[a][0] = vals[a]
        if j == 1:
            for a in range(n):
                p_refs[a][0] = vals[a]
        else:
            @pl.when(pl.program_id(0) == pos_ref[0])
            def _():
                for a in range(n):
                    p_refs[a][0] = vals[a]

    blk = lambda g: (1,) + g.shape[2:]
    out = pl.pallas_call(
        body, name=name,
        grid_spec=pltpu.PrefetchScalarGridSpec(
            num_scalar_prefetch=1, grid=(j,),
            in_specs=[pl.BlockSpec((1,) + blk(g), lambda i, p: (i, p[1], 0, 0)) for g in gs]
            + [pl.BlockSpec((1,) + blk(g), lambda i, p: (i, 0, 0, 0)) for g in gs],
            out_specs=[pl.BlockSpec(blk(g), lambda i, p: (i, 0, 0)) for g in gs]
            + [pl.BlockSpec(blk(g), lambda i, p: (p[0], 0, 0)) for g in gs]),
        out_shape=[jax.ShapeDtypeStruct((j,) + g.shape[2:], dt) for g, dt in zip(gs, dtypes)]
        + [jax.ShapeDtypeStruct((N_CHIPS,) + g.shape[2:], dt) for g, dt in zip(gs, dtypes)],
        compiler_params=_params(("arbitrary",)),
    )(pos, *gs, *gots)
    return list(out[:n]), list(out[n:])


def _exchange_descriptors(sums, parts, send_of, recv_of):
    x, y, c = _mesh_pos()
    me = 2 * x + y
    chips = _other_chips(x, y)
    sends, arrivals = [], []
    for a in range(len(sums)):
        for k in range(3):
            ck = 2 * chips[k][0] + chips[k][1]
            mine = sums[a].at[ck] if sums[a].shape[0] == N_CHIPS else sums[a].at[0]

            def copy(dst_slot, a=a, k=k, mine=mine):
                return pltpu.make_async_remote_copy(
                    src_ref=mine, dst_ref=parts[a].at[dst_slot],
                    send_sem=send_of(a, k), recv_sem=recv_of(a, k),
                    device_id=(*chips[k], c), device_id_type=MESH)

            sends.append(functools.partial(copy, me))
            arrivals.append(functools.partial(copy, ck))
    return sends, arrivals


def _exchange_start(sums, parts, name, after=()):
    n = len(sums)
    ns = 3 * n

    def body(*refs):
        sems = refs[2 * n:2 * n + 2 * ns]
        sums_thru = refs[2 * n + 2 * ns:3 * n + 2 * ns]
        parts_thru = refs[3 * n + 2 * ns:4 * n + 2 * ns]
        token = refs[4 * n + 2 * ns]
        _chips_handshake()
        sends, _ = _exchange_descriptors(sums_thru, parts_thru, lambda a, k: sems[3 * a + k],
                                         lambda a, k: sems[ns + 3 * a + k])
        for cp in sends:
            cp().start()
        token[...] = jnp.zeros_like(token)

    hbm = lambda a: pltpu.HBM(a.shape, a.dtype)
    held = [pltpu.with_memory_space_constraint(a, pltpu.HBM) for a in (*sums, *parts)]
    out = _tied_call(
        body, after, name=name,
        out_shape=(*[pltpu.SemaphoreType.DMA(())] * (2 * ns), *[hbm(a) for a in held],
                   jax.ShapeDtypeStruct((8, LANES), F32)),
        in_specs=[HBM_SPEC] * (2 * n),
        out_specs=(*[SEM_SPEC] * (2 * ns), *[HBM_SPEC] * (2 * n), pl.BlockSpec(memory_space=pltpu.VMEM)),
        input_output_aliases={i: 2 * ns + i for i in range(2 * n)},
        compiler_params=pltpu.CompilerParams(has_side_effects=pltpu.SideEffectType.DATAFLOW_SIDE_EFFECTING,
                                             collective_id=CHIPS_COLLECTIVE_ID),
    )(*held)
    return (list(out[:2 * ns]), list(out[2 * ns:2 * ns + n]), list(out[2 * ns + n:2 * ns + 2 * n]),
            out[2 * ns + 2 * n])


def _exchange_wait(sems, sums, parts, after, name):
    n = len(sums)
    ns = 3 * n

    def body(*refs):
        sums_ref, parts_ref = refs[:n], refs[n:2 * n]
        sem_ref = refs[2 * n:2 * n + 2 * ns]
        sends, arrivals = _exchange_descriptors(sums_ref, parts_ref, lambda a, k: sem_ref[3 * a + k],
                                                lambda a, k: sem_ref[ns + 3 * a + k])
        for cp in sends:
            cp().wait_send()
        for cp in arrivals:
            cp().wait_recv()

    hbm = lambda a: pltpu.HBM(a.shape, a.dtype)
    out = pl.pallas_call(
        body, name=name,
        out_shape=tuple(hbm(a) for a in (*sums, *parts)),
        in_specs=[HBM_SPEC] * (2 * n) + [SEM_SPEC] * (2 * ns) + [ANY] * len(_as_tuple(after)),
        out_specs=tuple([HBM_SPEC] * (2 * n)),
        input_output_aliases={i: i for i in range(2 * n)},
        compiler_params=pltpu.CompilerParams(has_side_effects=pltpu.SideEffectType.DATAFLOW_SIDE_EFFECTING),
    )(*sums, *parts, *sems, *_as_tuple(after))
    return list(out[n:])


def _sum_chips(parts, pos, name, after=()):
    n = len(parts)
    after = _as_tuple(after)

    def body(pos_ref, *refs):
        outs = refs[n + len(after):]
        for a in range(n):
            p_ref = refs[a]
            outs[a][0] = (((p_ref[0].astype(F32) + p_ref[1].astype(F32)) + p_ref[2].astype(F32))
                          + p_ref[3].astype(F32))

    out = pl.pallas_call(
        body, name=name,
        grid_spec=pltpu.PrefetchScalarGridSpec(
            num_scalar_prefetch=1, grid=(1,),
            in_specs=[pl.BlockSpec(p.shape, lambda i, q: (0, 0, 0)) for p in parts] + [ANY] * len(after),
            out_specs=[pl.BlockSpec((1,) + p.shape[1:], lambda i, q: (q[1], 0, 0)) for p in parts]),
        out_shape=[jax.ShapeDtypeStruct((2,) + p.shape[1:], F32) for p in parts],
        compiler_params=_params(("arbitrary",)),
    )(pos, *parts, *after)
    return list(out)


def _join_halves(fulls, name, after=()):
    n = len(fulls)

    def body(*refs):
        outs = refs[n:2 * n]
        send_sem, recv_sem = refs[2 * n:]
        x, y, c = _mesh_pos()
        _sibling_handshake()

        def half(a, which):
            return pltpu.make_async_remote_copy(
                src_ref=outs[a].at[which], dst_ref=outs[a].at[which],
                send_sem=send_sem.at[a], recv_sem=recv_sem.at[a],
                device_id=(x, y, 1 - c), device_id_type=MESH)

        sends = [half(a, c) for a in range(n)]
        for cp in sends:
            cp.start()
        for a in range(n):
            half(a, 1 - c).wait_recv()
        for cp in sends:
            cp.wait_send()

    out_shape = [jax.ShapeDtypeStruct(f.shape, f.dtype) for f in fulls]
    return _tied_call(
        body, after, name=name,
        in_specs=[ANY] * n, out_specs=[ANY] * n, out_shape=out_shape,
        input_output_aliases={a: a for a in range(n)},
        scratch_shapes=[pltpu.SemaphoreType.DMA((n,))] * 2,
        compiler_params=pltpu.CompilerParams(collective_id=SIBLING_COLLECTIVE_ID),
    )(*fulls)


def _norm_in(x, g, ts, after=()):
    s = x.shape[0]

    def body(x_ref, g_ref, hn_ref):
        xv = x_ref[...]
        hn_ref[...] = (xv * _rms_stats(xv) * g_ref[...]).astype(BF16)

    row = pl.BlockSpec((ts, D_MODEL), lambda i: (i, 0))
    return _tied_call(
        body, after, name="norm_in", grid=(s // ts,),
        in_specs=[row, pl.BlockSpec((1, D_MODEL), lambda i: (0, 0))], out_specs=row,
        out_shape=jax.ShapeDtypeStruct((s, D_MODEL), BF16),
        compiler_params=_params(("parallel",)),
    )(x, g)


def _shift_rows(buf, shifted, t):
    rows = t + CONV_HALO - SUBLANES
    for r in range(1, SUBLANES):
        shifted[r - 1, 0:rows, :] = buf[pl.ds(r, rows), :]


def _window(buf, shifted, offset, t):
    r = offset % SUBLANES
    if r == 0:
        return buf[pl.ds(offset, t), :]
    return shifted[r - 1, pl.ds(offset - r, t), :]


def _lane_is_low_head():
    lane = lax.broadcasted_iota(jnp.int32, (1, GM_WIDTH), 1)
    return (lane & GM_HEAD_DIM) == 0


def _gm_mix(v_lo, v_hi, wpair_ref, bias_ref, mixed_ref, t):
    for n in range(t // CHUNK):
        rows = slice(n * CHUNK, (n + 1) * CHUNK)
        for j in range(GM_HEADS // 2):
            cols = slice(j * LANES, (j + 1) * LANES)
            rhs = jnp.concatenate([v_lo[rows, cols], v_hi[rows, cols]], axis=0)
            mixed_ref[rows, cols] = _dot(wpair_ref[j], rhs) + bias_ref[:, cols]


def _seqmix_fwd(hn, w_in, b_in, cw, cb, lng, lnb, gg, gb, wpair, bias, t, after=()):
    s = hn.shape[0]

    def body(hn_ref, w_ref, b_ref, cw_ref, cb_ref, lng_ref, lnb_ref, gg_ref, gb_ref, wpair_ref, bias_ref,
             z_ref, mix_ref, c1_ref, abuf, ash, mixed_ref):
        i = pl.program_id(0)

        @pl.when(i == 0)
        def _():
            abuf[0:CONV_HALO, :] = jnp.zeros((CONV_HALO, CONV_WIDTH), F32)

        @pl.when(i > 0)
        def _():
            abuf[0:CONV_HALO, :] = abuf[t:t + CONV_HALO, :]

        hv = hn_ref[...]
        for j in range(4):
            cols = slice(j * 512, (j + 1) * 512)
            z_ref[:, cols] = _dot(hv, w_ref[j]) + b_ref[:, cols]

        abuf[CONV_HALO:, :] = z_ref[:, 0:512] * _sigmoid(z_ref[:, 512:1024])
        _shift_rows(abuf, ash, t)
        acc = jnp.zeros((t, CONV_WIDTH), F32)
        for k in range(CONV_KERNEL):
            acc = acc + cw_ref[k:k + 1, :] * _window(abuf, ash, CONV_HALO - (CONV_KERNEL - 1) + k, t)
        c1 = acc + cb_ref[...]
        c1_ref[...] = c1
        xh, _ = _ln_stats(c1)
        ln = xh * lng_ref[...] + lnb_ref[...]
        mix_ref[:, 0:512] = (ln * _sigmoid(ln)).astype(BF16)

        u, _ = _gelu_parts(z_ref[:, 1024:1536])
        gv, _ = _gelu_parts(z_ref[:, 1536:2048])
        vxh, _ = _ln_stats(gv)
        v = vxh * gg_ref[...] + gb_ref[...]
        low = _lane_is_low_head()
        v_lo = jnp.where(low, v, 0.0).astype(BF16)
        v_hi = jnp.where(low, 0.0, v).astype(BF16)
        _gm_mix(v_lo, v_hi, wpair_ref, bias_ref, mixed_ref, t)
        mix_ref[:, 512:1024] = (u * mixed_ref[...]).astype(BF16)

    vec = lambda n: pl.BlockSpec((1, n), lambda i: (0, 0))
    return _tied_call(
        body, after, name="seqmix_fwd", grid=(s // t,),
        in_specs=[pl.BlockSpec((t, D_MODEL), lambda i: (i, 0)),
                  pl.BlockSpec((4, D_MODEL, 512), lambda i: (0, 0, 0)), vec(2048),
                  pl.BlockSpec((CONV_HALO, CONV_WIDTH), lambda i: (0, 0)),
                  vec(512), vec(512), vec(512), vec(512), vec(512),
                  pl.BlockSpec((4, CHUNK, 2 * CHUNK), lambda i: (0, 0, 0)),
                  pl.BlockSpec((CHUNK, GM_WIDTH), lambda i: (0, 0))],
        out_specs=[pl.BlockSpec((t, 2048), lambda i: (i, 0)),
                   pl.BlockSpec((t, D_MODEL), lambda i: (i, 0)),
                   pl.BlockSpec((t, CONV_WIDTH), lambda i: (i, 0))],
        out_shape=[jax.ShapeDtypeStruct((s, 2048), F32), jax.ShapeDtypeStruct((s, D_MODEL), BF16),
                   jax.ShapeDtypeStruct((s, CONV_WIDTH), F32)],
        scratch_shapes=[pltpu.VMEM((t + CONV_HALO, CONV_WIDTH), F32),
                        pltpu.VMEM((SUBLANES - 1, t + CONV_HALO - SUBLANES, CONV_WIDTH), F32),
                        pltpu.VMEM((t, GM_WIDTH), F32)],
        compiler_params=_params(("arbitrary",)),
    )(hn, w_in, b_in, cw, cb, lng, lnb, gg, gb, wpair, bias)


def _mem_kv(mem, g, wkv):
    m = mem.shape[0]

    def body(mem_ref, g_ref, w_ref, mn_ref, kv_ref):
        mv = mem_ref[...]
        mn = (mv * _rms_stats(mv) * g_ref[...]).astype(BF16)
        mn_ref[...] = mn
        for j in range(4):
            kv_ref[:, j * 512:(j + 1) * 512] = _dot(mn, w_ref[j]).astype(BF16)

    return pl.pallas_call(
        body, name="mem_kv",
        out_shape=[jax.ShapeDtypeStruct((m, D_MODEL), BF16), jax.ShapeDtypeStruct((m, 2 * D_MODEL), BF16)],
        compiler_params=pltpu.CompilerParams(vmem_limit_bytes=VMEM_LIMIT_BYTES),
    )(mem, g, wkv)


def _softmax_rows(sc):
    e = jnp.exp(sc - jnp.max(sc, axis=-1, keepdims=True))
    return e / jnp.sum(e, axis=-1, keepdims=True)


def _attn_block_fwd(x, mix, w_out, g_xa, wq, kv, wo, g_ffn, ts, after=()):
    s, m = x.shape[0], kv.shape[0]
    scale = XA_HEAD_DIM ** -0.5

    def body(x_ref, mix_ref, wout_ref, gxa_ref, wq_ref, kv_ref, wo_ref, gffn_ref,
             h1_ref, hn2_ref, q_ref, o_ref, h2_ref, hn3_ref):
        h1 = x_ref[...] + _dot(mix_ref[...], wout_ref[...])
        h1_ref[...] = h1
        hn2 = (h1 * _rms_stats(h1) * gxa_ref[...]).astype(BF16)
        hn2_ref[...] = hn2
        q_ref[...] = _dot(hn2, wq_ref[...]).astype(BF16)
        for h in range(XA_HEADS):
            cols = slice(h * XA_HEAD_DIM, (h + 1) * XA_HEAD_DIM)
            vcols = slice(D_MODEL + h * XA_HEAD_DIM, D_MODEL + (h + 1) * XA_HEAD_DIM)
            p = _softmax_rows(_dot_nt(q_ref[:, cols], kv_ref[:, cols]) * scale)
            o_ref[:, cols] = _dot(p.astype(BF16), kv_ref[:, vcols]).astype(BF16)
        h2 = h1 + _dot(o_ref[...], wo_ref[...])
        h2_ref[...] = h2
        hn3_ref[...] = (h2 * _rms_stats(h2) * gffn_ref[...]).astype(BF16)

    row = pl.BlockSpec((ts, D_MODEL), lambda i: (i, 0))
    full = pl.BlockSpec((D_MODEL, D_MODEL), lambda i: (0, 0))
    vec = pl.BlockSpec((1, D_MODEL), lambda i: (0, 0))
    f32 = jax.ShapeDtypeStruct((s, D_MODEL), F32)
    bf16 = jax.ShapeDtypeStruct((s, D_MODEL), BF16)
    return _tied_call(
        body, after, name="attn_block_fwd", grid=(s // ts,),
        in_specs=[row, row, full, vec, full, pl.BlockSpec((m, 2 * D_MODEL), lambda i: (0, 0)), full, vec],
        out_specs=[row] * 6,
        out_shape=[f32, bf16, bf16, bf16, f32, bf16],
        compiler_params=_params(("parallel",)),
    )(x, mix, w_out, g_xa, wq, kv, wo, g_ffn)


_FFN_CHUNKS = (slice(0, 6 * LANES), slice(6 * LANES, FFN_HALF))


def _ffn_up(hn, wgu, ts, after=()):
    s = hn.shape[0]

    def body(hn_ref, w_ref, gu_ref, act_ref):
        hv = hn_ref[...]
        for cols in _FFN_CHUNKS:
            gate = _dot(hv, w_ref[0, 0, :, cols])
            up = _dot(hv, w_ref[1, 0, :, cols])
            gu_ref[0, :, cols] = gate.astype(BF16)
            gu_ref[1, :, cols] = up.astype(BF16)
            act_ref[:, cols] = (gate * _sigmoid(gate) * up).astype(BF16)

    return _tied_call(
        body, after, name="ffn_up", grid=(2, s // ts),
        in_specs=[pl.BlockSpec((ts, D_MODEL), lambda j, i: (i, 0)),
                  pl.BlockSpec((2, 1, D_MODEL, FFN_HALF), lambda j, i: (0, j, 0, 0))],
        out_specs=[pl.BlockSpec((2, ts, FFN_HALF), lambda j, i: (0, i, j)),
                   pl.BlockSpec((ts, FFN_HALF), lambda j, i: (i, j))],
        out_shape=[jax.ShapeDtypeStruct((2, s, FFN_HIDDEN), BF16), jax.ShapeDtypeStruct((s, FFN_HIDDEN), BF16)],
        compiler_params=_params(("parallel", "parallel")),
    )(hn, wgu)


def _ffn_down_loss(act, wd, h2, g, target, ts):
    s = act.shape[0]

    def body(act_ref, wd_ref, h2_ref, g_ref, t_ref, dh_ref, dhb_ref, sq_ref, dg_ref):
        @pl.when(pl.program_id(0) == 0)
        def _():
            sq_ref[...] = jnp.zeros_like(sq_ref)
            dg_ref[...] = jnp.zeros_like(dg_ref)

        h3 = h2_ref[...] + _dot(act_ref[...], wd_ref[...])
        r = _rms_stats(h3)
        gv = g_ref[...]
        diff = h3 * r * gv - t_ref[...]
        sq_ref[...] += _rowsum(diff * diff)
        dh, dg = _rms_bwd(diff / D_MODEL, h3, r, gv)
        dh_ref[...] = dh
        dhb_ref[...] = dh.astype(BF16)
        dg_ref[...] += dg

    row = pl.BlockSpec((ts, D_MODEL), lambda i: (i, 0))
    vec = pl.BlockSpec((1, D_MODEL), lambda i: (0, 0))
    return pl.pallas_call(
        body, name="ffn_down_loss", grid=(s // ts,),
        in_specs=[pl.BlockSpec((ts, FFN_HIDDEN), lambda i: (i, 0)),
                  pl.BlockSpec((FFN_HIDDEN, D_MODEL), lambda i: (0, 0)), row, vec, row],
        out_specs=[row, row, vec, vec],
        out_shape=[jax.ShapeDtypeStruct((s, D_MODEL), F32), jax.ShapeDtypeStruct((s, D_MODEL), BF16),
                   jax.ShapeDtypeStruct((1, D_MODEL), F32), jax.ShapeDtypeStruct((1, D_MODEL), F32)],
        compiler_params=_params(("arbitrary",)),
    )(act, wd, h2, g, target)


def _grad_w(a, b, tk, tn, name, after=(), shards=1):
    s, k = a.shape
    gb, _, n = b.shape
    nblk = n // tn
    ws = tn // shards
    tsr = GRAD_ROWS if s % GRAD_ROWS == 0 else s

    def body(a_ref, b_ref, o_ref):
        part = _dot_tn(a_ref[...], b_ref[0])

        @pl.when(pl.program_id(2) == 0)
        def _():
            for j in range(shards):
                o_ref[j] = part[:, j * ws:(j + 1) * ws]

        @pl.when(pl.program_id(2) > 0)
        def _():
            for j in range(shards):
                o_ref[j] += part[:, j * ws:(j + 1) * ws]

    return _tied_call(
        body, after, name=name, grid=(gb * nblk, k // tk, s // tsr),
        in_specs=[pl.BlockSpec((tsr, tk), lambda ni, ki, si: (si, ki)),
                  pl.BlockSpec((1, tsr, tn), lambda ni, ki, si: (ni // nblk, si, ni % nblk))],
        out_specs=pl.BlockSpec((shards, tk, ws), lambda ni, ki, si: (ni, ki, 0)),
        out_shape=jax.ShapeDtypeStruct((gb * nblk * shards, k, ws), F32),
        compiler_params=_params(("parallel", "parallel", "arbitrary")),
    )(a, b)


def _grad_w_square(pairs, name, after=()):
    n = len(pairs)
    s = pairs[0][0].shape[0]
    tsr = GRAD_ROWS // 2 if s % (GRAD_ROWS // 2) == 0 else s

    def body(*refs):
        ins, outs = refs[:2 * n], refs[2 * n:]
        parts = [_dot_tn(ins[2 * a][...], ins[2 * a + 1][...]) for a in range(n)]

        @pl.when(pl.program_id(0) == 0)
        def _():
            for a in range(n):
                outs[a][...] = parts[a]

        @pl.when(pl.program_id(0) > 0)
        def _():
            for a in range(n):
                outs[a][...] += parts[a]

    row = pl.BlockSpec((tsr, D_MODEL), lambda i: (i, 0))
    return _tied_call(
        body, after, name=name, grid=(s // tsr,),
        in_specs=[row] * (2 * n), out_specs=[pl.BlockSpec((D_MODEL, D_MODEL), lambda i: (0, 0))] * n,
        out_shape=[jax.ShapeDtypeStruct((D_MODEL, D_MODEL), F32)] * n,
        compiler_params=_params(("arbitrary",)),
    )(*[x for p in pairs for x in p])


def _ffn_bwd(dh3, wd, gu, wgu, h2, g, t, after=()):
    s = dh3.shape[0]

    def body(dh3_ref, wd_ref, gu_ref, w_ref, h2_ref, g_ref, dgu_ref, dh2_ref, dh2b_ref, dg_ref):
        @pl.when(pl.program_id(0) == 0)
        def _():
            dg_ref[...] = jnp.zeros_like(dg_ref)

        dh3v = dh3_ref[...]
        dhb = dh3v.astype(BF16)
        dhn = jnp.zeros((t, D_MODEL), F32)
        for j in range(2):
            for cols in _FFN_CHUNKS:
                whole = slice(j * FFN_HALF + cols.start, j * FFN_HALF + cols.stop)
                dact = _dot_nt(dhb, wd_ref[j, cols, :])
                gate, up = gu_ref[0, :, whole].astype(F32), gu_ref[1, :, whole].astype(F32)
                sg = _sigmoid(gate)
                dgate = (dact * up * (sg * (1.0 + gate * (1.0 - sg)))).astype(BF16)
                dup = (dact * (gate * sg)).astype(BF16)
                dgu_ref[0, :, whole] = dgate
                dgu_ref[1, :, whole] = dup
            half = slice(j * FFN_HALF, (j + 1) * FFN_HALF)
            dhn = dhn + _dot_nt(dgu_ref[0, :, half], w_ref[j]) + _dot_nt(dgu_ref[1, :, half], w_ref[2 + j])
        h2 = h2_ref[...]
        dv, dg = _rms_bwd(dhn, h2, _rms_stats(h2), g_ref[...])
        dh2 = dh3v + dv
        dh2_ref[...] = dh2
        dh2b_ref[...] = dh2.astype(BF16)
        dg_ref[...] += dg

    row = pl.BlockSpec((t, D_MODEL), lambda i: (i, 0))
    wide = pl.BlockSpec((2, t, FFN_HIDDEN), lambda i: (0, i, 0))
    vec = pl.BlockSpec((1, D_MODEL), lambda i: (0, 0))
    return _tied_call(
        body, after, name="ffn_bwd", grid=(s // t,),
        in_specs=[row, pl.BlockSpec((2, FFN_HALF, D_MODEL), lambda i: (0, 0, 0)), wide,
                  pl.BlockSpec((4, D_MODEL, FFN_HALF), lambda i: (0, 0, 0)), row, vec],
        out_specs=[wide, row, row, vec],
        out_shape=[jax.ShapeDtypeStruct((2, s, FFN_HIDDEN), BF16), jax.ShapeDtypeStruct((s, D_MODEL), F32),
                   jax.ShapeDtypeStruct((s, D_MODEL), BF16), jax.ShapeDtypeStruct((1, D_MODEL), F32)],
        compiler_params=_params(("arbitrary",)),
    )(dh3, wd, gu, wgu, h2, g)


def _attn_bwd(dh2, wo, q, kv, wq, h1, g, ts, after=()):
    s, m = q.shape[0], kv.shape[0]
    scale = XA_HEAD_DIM ** -0.5

    def body(dh2_ref, wo_ref, q_ref, kv_ref, wq_ref, h1_ref, g_ref, dh1_ref, dh1b_ref, dq_ref, dkv_ref, dg_ref):
        @pl.when(pl.program_id(0) == 0)
        def _():
            dkv_ref[...] = jnp.zeros_like(dkv_ref)
            dg_ref[...] = jnp.zeros_like(dg_ref)

        do = _dot_nt(dh2_ref[...].astype(BF16), wo_ref[...]).astype(BF16)
        for h in range(XA_HEADS):
            cols = slice(h * XA_HEAD_DIM, (h + 1) * XA_HEAD_DIM)
            vcols = slice(D_MODEL + h * XA_HEAD_DIM, D_MODEL + (h + 1) * XA_HEAD_DIM)
            qh, kh, vh, doh = q_ref[:, cols], kv_ref[:, cols], kv_ref[:, vcols], do[:, cols]
            p = _softmax_rows(_dot_nt(qh, kh) * scale)
            dp = _dot_nt(doh, vh)
            ds = (p * (dp - jnp.sum(dp * p, axis=-1, keepdims=True)) * scale).astype(BF16)
            dq_ref[:, cols] = _dot(ds, kh).astype(BF16)
            dkv_ref[:, cols] += _dot_tn(ds, qh)
            dkv_ref[:, vcols] += _dot_tn(p.astype(BF16), doh)
        dhn = _dot_nt(dq_ref[...], wq_ref[...])
        h1 = h1_ref[...]
        dv, dg = _rms_bwd(dhn, h1, _rms_stats(h1), g_ref[...])
        dh1 = dh2_ref[...] + dv
        dh1_ref[...] = dh1
        dh1b_ref[...] = dh1.astype(BF16)
        dg_ref[...] += dg

    row = pl.BlockSpec((ts, D_MODEL), lambda i: (i, 0))
    full = pl.BlockSpec((D_MODEL, D_MODEL), lambda i: (0, 0))
    kvs = pl.BlockSpec((m, 2 * D_MODEL), lambda i: (0, 0))
    vec = pl.BlockSpec((1, D_MODEL), lambda i: (0, 0))
    return _tied_call(
        body, after, name="attn_bwd", grid=(s // ts,),
        in_specs=[row, full, row, kvs, full, row, vec],
        out_specs=[row, row, row, kvs, vec],
        out_shape=[jax.ShapeDtypeStruct((s, D_MODEL), F32), jax.ShapeDtypeStruct((s, D_MODEL), BF16),
                   jax.ShapeDtypeStruct((s, D_MODEL), BF16),
                   jax.ShapeDtypeStruct((m, 2 * D_MODEL), F32), jax.ShapeDtypeStruct((1, D_MODEL), F32)],
        compiler_params=_params(("arbitrary",)),
    )(dh2, wo, q, kv, wq, h1, g)


def _mem_kv_bwd(dkv, mn, wkv, mem, g, after=()):
    m = mem.shape[0]

    def body(dkv_ref, mn_ref, w_ref, mem_ref, g_ref, dw_ref, dg_ref):
        dmn = jnp.zeros((m, D_MODEL), F32)
        mn = mn_ref[...]
        for j in range(4):
            dj = dkv_ref[:, j * 512:(j + 1) * 512].astype(BF16)
            dw_ref[j] = _dot_tn(mn, dj)
            dmn = dmn + _dot_nt(dj, w_ref[j])
        mv = mem_ref[...]
        dg_ref[...] = _rowsum(dmn * (mv * _rms_stats(mv)))

    return _tied_call(
        body, after, name="mem_kv_bwd", in_specs=[pl.BlockSpec(memory_space=pltpu.VMEM)] * 5,
        out_shape=[jax.ShapeDtypeStruct((4, D_MODEL, 512), F32), jax.ShapeDtypeStruct((1, D_MODEL), F32)],
        compiler_params=pltpu.CompilerParams(vmem_limit_bytes=VMEM_LIMIT_BYTES),
    )(dkv, mn, wkv, mem, g)


def _seqmix_bwd(dh1, x, z, c1, w_out, w_in, g_mix, cw, lng, lnb, gg, gb, wpair, wpair_t, bias, t, after=()):
    s = x.shape[0]
    nt = s // t

    def body(dh1_ref, x_ref, z_ref, c1_ref, wo_ref, wi_ref, gm_ref, cw_ref, lng_ref, lnb_ref,
             gg_ref, gb_ref, wpair_ref, wpt_ref, bias_ref,
             gx_ref, dz_ref, dcw_ref, dcb_ref, dlng_ref, dlnb_ref, dgg_ref, dgb_ref, dws_ref, dbs_ref,
             dbin_ref, dgm_ref, dbuf, dsh, mixed_ref, dv_ref):
        i = pl.program_id(0)
        accs = (dcw_ref, dcb_ref, dlng_ref, dlnb_ref, dgg_ref, dgb_ref, dws_ref, dbs_ref, dbin_ref, dgm_ref)

        @pl.when(i == 0)
        def _():
            for r in accs:
                r[...] = jnp.zeros_like(r)
            dbuf[t:t + CONV_HALO, :] = jnp.zeros((CONV_HALO, CONV_WIDTH), F32)

        @pl.when(i > 0)
        def _():
            dbuf[t:t + CONV_HALO, :] = dbuf[0:CONV_HALO, :]

        dmix = _dot_nt(dh1_ref[...].astype(BF16), wo_ref[...])

        xh, rs = _ln_stats(c1_ref[...])
        lng = lng_ref[...]
        ln = xh * lng + lnb_ref[...]
        sl = _sigmoid(ln)
        dln = dmix[:, 0:512] * (sl * (1.0 + ln * (1.0 - sl)))
        dc1, dg_ln, db_ln = _ln_bwd(dln, xh, rs, lng)
        dlng_ref[...] += dg_ln
        dlnb_ref[...] += db_ln
        dcb_ref[...] += _rowsum(dc1)
        dbuf[0:t, :] = dc1

        za = z_ref[:, 0:512]
        sg = _sigmoid(z_ref[:, 512:1024])
        a = za * sg
        _shift_rows(dbuf, dsh, t)

        da = jnp.zeros((t, CONV_WIDTH), F32)
        for k in range(CONV_KERNEL):
            later = _window(dbuf, dsh, CONV_KERNEL - 1 - k, t)
            da = da + cw_ref[k:k + 1, :] * later
            dcw_ref[k:k + 1, :] += _rowsum(a * later)
        dza = da * sg
        dzg = da * za * (sg * (1.0 - sg))
        dz_ref[:, 0:512] = dza.astype(BF16)
        dz_ref[:, 512:1024] = dzg.astype(BF16)
        dbin_ref[:, 0:512] += _rowsum(dza)
        dbin_ref[:, 512:1024] += _rowsum(dzg)

        dgm = dmix[:, 512:1024]
        u, du_dz = _gelu_parts(z_ref[:, 1024:1536])
        gv, dgv_dz = _gelu_parts(z_ref[:, 1536:2048])
        vxh, vrs = _ln_stats(gv)
        ggv = gg_ref[...]
        v = vxh * ggv + gb_ref[...]
        low = _lane_is_low_head()
        v_lo = jnp.where(low, v, 0.0).astype(BF16)
        v_hi = jnp.where(low, 0.0, v).astype(BF16)
        _gm_mix(v_lo, v_hi, wpair_ref, bias_ref, mixed_ref, t)
        dzu = dgm * mixed_ref[...] * du_dz
        dm = dgm * u
        dm_lo = jnp.where(low, dm, 0.0).astype(BF16)
        dm_hi = jnp.where(low, 0.0, dm).astype(BF16)
        vb = v.astype(BF16)
        tril = (lax.broadcasted_iota(jnp.int32, (CHUNK, CHUNK), 1)
                <= lax.broadcasted_iota(jnp.int32, (CHUNK, CHUNK), 0))
        for n in range(t // CHUNK):
            rows = slice(n * CHUNK, (n + 1) * CHUNK)
            dbs_ref[...] += dm[rows, :]
            for j in range(GM_HEADS // 2):
                cols = slice(j * LANES, (j + 1) * LANES)
                stack = jnp.concatenate([dm_lo[rows, cols], dm_hi[rows, cols]], axis=0)
                dws = _dot_nt(stack, vb[rows, cols])
                dws_ref[2 * j] += jnp.where(tril, dws[0:CHUNK], 0.0)
                dws_ref[2 * j + 1] += jnp.where(tril, dws[CHUNK:2 * CHUNK], 0.0)
                dv_ref[rows, cols] = _dot(wpt_ref[j], stack)
        dgv, dg_gm, db_gm = _ln_bwd(dv_ref[...], vxh, vrs, ggv)
        dgg_ref[...] += dg_gm
        dgb_ref[...] += db_gm
        dzv = dgv * dgv_dz
        dz_ref[:, 1024:1536] = dzu.astype(BF16)
        dz_ref[:, 1536:2048] = dzv.astype(BF16)
        dbin_ref[:, 1024:1536] += _rowsum(dzu)
        dbin_ref[:, 1536:2048] += _rowsum(dzv)

        dhn = jnp.zeros((t, D_MODEL), F32)
        for j in range(4):
            dhn = dhn + _dot_nt(dz_ref[:, j * 512:(j + 1) * 512], wi_ref[j])
        xv = x_ref[...]
        dv, dg = _rms_bwd(dhn, xv, _rms_stats(xv), gm_ref[...])
        gx_ref[...] = dh1_ref[...] + dv
        dgm_ref[...] += dg

    rev = lambda w: pl.BlockSpec((t, w), lambda i: (nt - 1 - i, 0))
    const = lambda *shape: pl.BlockSpec(shape, lambda i: (0,) * len(shape))
    f32 = lambda *shape: jax.ShapeDtypeStruct(shape, F32)
    return _tied_call(
        body, after, name="seqmix_bwd", grid=(nt,),
        in_specs=[rev(D_MODEL), rev(D_MODEL), rev(2048), rev(CONV_WIDTH),
                  const(D_MODEL, D_MODEL), const(4, D_MODEL, 512), const(1, D_MODEL),
                  const(CONV_HALO, CONV_WIDTH), const(1, 512), const(1, 512), const(1, 512), const(1, 512),
                  const(4, CHUNK, 2 * CHUNK), const(4, CHUNK, 2 * CHUNK), const(CHUNK, GM_WIDTH)],
        out_specs=[rev(D_MODEL), rev(2048),
                   const(CONV_HALO, CONV_WIDTH), const(1, 512), const(1, 512), const(1, 512), const(1, 512),
                   const(1, 512), const(GM_HEADS, CHUNK, CHUNK), const(CHUNK, GM_WIDTH), const(1, 2048),
                   const(1, D_MODEL)],
        out_shape=[f32(s, D_MODEL), jax.ShapeDtypeStruct((s, 2048), BF16),
                   f32(CONV_HALO, CONV_WIDTH), f32(1, 512), f32(1, 512), f32(1, 512), f32(1, 512),
                   f32(1, 512), f32(GM_HEADS, CHUNK, CHUNK), f32(CHUNK, GM_WIDTH), f32(1, 2048),
                   f32(1, D_MODEL)],
        scratch_shapes=[pltpu.VMEM((t + CONV_HALO, CONV_WIDTH), F32),
                        pltpu.VMEM((SUBLANES - 1, t + CONV_HALO - SUBLANES, CONV_WIDTH), F32),
                        pltpu.VMEM((t, GM_WIDTH), F32), pltpu.VMEM((t, GM_WIDTH), F32)],
        compiler_params=_params(("arbitrary",)),
    )(dh1, x, z, c1, w_out, w_in, g_mix, cw, lng, lnb, gg, gb, wpair, wpair_t, bias)


def _head_bias_grad(dbs):
    def body(d_ref, o_ref):
        dv = d_ref[...]
        lane = lax.broadcasted_iota(jnp.int32, (CHUNK, LANES), 1)
        acc = jnp.zeros((CHUNK, LANES), F32)
        for h in range(GM_HEADS):
            sh = jnp.sum(dv[:, h * GM_HEAD_DIM:(h + 1) * GM_HEAD_DIM], axis=-1, keepdims=True)
            acc = acc + jnp.where(lane == h, sh, 0.0)
        o_ref[...] = acc

    return pl.pallas_call(body, name="head_bias_grad",
                          out_shape=jax.ShapeDtypeStruct((CHUNK, LANES), F32))(dbs)


def kernel(x, mem, norm_mix_g, w_in, b_in, conv_w, conv_b, conv_ln_g, conv_ln_b, gm_ln_g, gm_ln_b, gm_w_s, gm_b_s, w_out, norm_xa_g, mem_norm_g, xa_wq, xa_wkv, xa_wo, norm_ffn_g, ffn_w_gate_up, ffn_w_down, final_norm_g, loss_target, m_norm_mix_g, m_w_in, m_b_in, m_conv_w, m_conv_b, m_conv_ln_g, m_conv_ln_b, m_gm_ln_g, m_gm_ln_b, m_gm_w_s, m_gm_b_s, m_w_out, m_norm_xa_g, m_mem_norm_g, m_xa_wq, m_xa_wkv, m_xa_wo, m_norm_ffn_g, m_ffn_w_gate_up, m_ffn_w_down, m_final_norm_g, v_norm_mix_g, v_w_in, v_b_in, v_conv_w, v_conv_b, v_conv_ln_g, v_conv_ln_b, v_gm_ln_g, v_gm_ln_b, v_gm_w_s, v_gm_b_s, v_w_out, v_norm_xa_g, v_mem_norm_g, v_xa_wq, v_xa_wkv, v_xa_wo, v_norm_ffn_g, v_ffn_w_gate_up, v_ffn_w_down, v_final_norm_g):
    weights = dict(norm_mix_g=norm_mix_g, w_in=w_in, b_in=b_in, conv_w=conv_w, conv_b=conv_b, conv_ln_g=conv_ln_g,
                   conv_ln_b=conv_ln_b, gm_ln_g=gm_ln_g, gm_ln_b=gm_ln_b, gm_w_s=gm_w_s, gm_b_s=gm_b_s, w_out=w_out,
                   norm_xa_g=norm_xa_g, mem_norm_g=mem_norm_g, xa_wq=xa_wq, xa_wkv=xa_wkv, xa_wo=xa_wo,
                   norm_ffn_g=norm_ffn_g, ffn_w_gate_up=ffn_w_gate_up, ffn_w_down=ffn_w_down,
                   final_norm_g=final_norm_g)
    m_in = dict(norm_mix_g=m_norm_mix_g, w_in=m_w_in, b_in=m_b_in, conv_w=m_conv_w, conv_b=m_conv_b,
                conv_ln_g=m_conv_ln_g, conv_ln_b=m_conv_ln_b, gm_ln_g=m_gm_ln_g, gm_ln_b=m_gm_ln_b, gm_w_s=m_gm_w_s,
                gm_b_s=m_gm_b_s, w_out=m_w_out, norm_xa_g=m_norm_xa_g, mem_norm_g=m_mem_norm_g, xa_wq=m_xa_wq,
                xa_wkv=m_xa_wkv, xa_wo=m_xa_wo, norm_ffn_g=m_norm_ffn_g, ffn_w_gate_up=m_ffn_w_gate_up,
                ffn_w_down=m_ffn_w_down, final_norm_g=m_final_norm_g)
    v_in = dict(norm_mix_g=v_norm_mix_g, w_in=v_w_in, b_in=v_b_in, conv_w=v_conv_w, conv_b=v_conv_b,
                conv_ln_g=v_conv_ln_g, conv_ln_b=v_conv_ln_b, gm_ln_g=v_gm_ln_g, gm_ln_b=v_gm_ln_b, gm_w_s=v_gm_w_s,
                gm_b_s=v_gm_b_s, w_out=v_w_out, norm_xa_g=v_norm_xa_g, mem_norm_g=v_mem_norm_g, xa_wq=v_xa_wq,
                xa_wkv=v_xa_wkv, xa_wo=v_xa_wo, norm_ffn_g=v_norm_ffn_g, ffn_w_gate_up=v_ffn_w_gate_up,
                ffn_w_down=v_ffn_w_down, final_norm_g=v_final_norm_g)
    grads, delta, new_m, new_v = {}, {}, {}, {}

    s = x.shape[1]
    ts = _row_tile(s)
    tb = max(CHUNK, ts // 2)
    tw = 2 * ts if s % (2 * ts) == 0 and ts >= 512 else ts
    cx, cy, cc = _mesh_pos()
    chip = 2 * cx + cy
    pos = jnp.stack([chip, cc]).astype(jnp.int32)
    row = lambda a: a.reshape(1, -1)
    x2, mem2, tgt2 = x[0], mem[0], loss_target[0]

    big = dict(w_in=w_in, xa_wkv=xa_wkv, w_out=w_out, xa_wq=xa_wq, xa_wo=xa_wo,
               ffn_w_gate_up=ffn_w_gate_up, ffn_w_down=ffn_w_down)
    big_names = list(big)
    halves = lambda a: a.reshape(2, a.shape[0] // 2, a.shape[1])
    conv_w_pad = jnp.pad(conv_w, ((0, CONV_HALO - CONV_KERNEL), (0, 0)))
    first_names = ["w_in", "conv_w"]
    later_names = [nm for nm in big_names if nm != "w_in"]
    cast = dict(zip(first_names, _cast_into_slots([halves(w_in), halves(conv_w_pad)], pos, [BF16, F32], "cast_w_in")))
    cast.update(zip(later_names, _cast_into_slots([halves(big[nm]) for nm in later_names], pos,
                                                  [BF16] * len(later_names), "cast_" + later_names[0])))

    def start_gather(names, after):
        return _gather_start([cast[nm] for nm in names], "gather_start_" + names[0], after)

    def land_gather(names, started, after):
        send_sems, recv_sems, bufs, _ = started
        return _gather_wait(send_sems, recv_sems, bufs, after, "gather_wait_" + names[0])

    def share_gather(names, landed, after=()):
        return dict(zip(names, (b.reshape(N_CHIPS, -1, b.shape[-1])
                                for b in _pass_to_sibling(landed, "pass_" + names[0], after))))

    tril = jnp.tril(jnp.ones((CHUNK, CHUNK), dtype=bool))
    ws = jnp.where(tril[None], gm_w_s, 0.0)
    wpair = jnp.concatenate([ws[0::2], ws[1::2]], axis=2).astype(BF16)
    ws_t = jnp.swapaxes(ws, 1, 2)
    wpair_t = jnp.concatenate([ws_t[0::2], ws_t[1::2]], axis=2).astype(BF16)
    bias = jnp.repeat(gm_b_s.T, GM_HEAD_DIM, axis=1)

    attn_names = ["w_out", "xa_wq", "xa_wkv", "xa_wo"]
    gather_first = start_gather(first_names, ())
    hn1 = _norm_in(x2, row(norm_mix_g), tw, after=(gather_first[3], wpair, wpair_t, bias))
    landed = land_gather(first_names, gather_first, [cast[nm] for nm in later_names] + [hn1])
    gather_attn = start_gather(attn_names, landed)
    gw = share_gather(first_names, landed, gather_attn[3])
    w_in_g = gw["w_in"]
    cw_g = jnp.concatenate([gw["conv_w"][k] for k in range(N_CHIPS)], axis=1)

    z, mix, c1 = _seqmix_fwd(hn1, w_in_g, row(b_in), cw_g, row(conv_b), row(conv_ln_g), row(conv_ln_b),
                             row(gm_ln_g), row(gm_ln_b), wpair, bias, ts)
    landed = land_gather(attn_names, gather_attn, mix)
    gather_gu = start_gather(["ffn_w_gate_up"], landed)
    gw = share_gather(attn_names, landed, gather_gu[3])
    w_out_g = gw["w_out"].reshape(D_MODEL, D_MODEL)
    wq_g = gw["xa_wq"].reshape(D_MODEL, D_MODEL)
    wkv_g = gw["xa_wkv"]
    wo_g = gw["xa_wo"].reshape(D_MODEL, D_MODEL)
    mn, kv = _mem_kv(mem2, row(mem_norm_g), wkv_g)
    h1, hn2, q, o, h2, hn3 = _attn_block_fwd(x2, mix, w_out_g, row(norm_xa_g), wq_g, kv, wo_g, row(norm_ffn_g), ts)
    landed = land_gather(["ffn_w_gate_up"], gather_gu, hn3)
    gather_down = start_gather(["ffn_w_down"], landed)
    wgu_g = share_gather(["ffn_w_gate_up"], landed, gather_down[3])["ffn_w_gate_up"]
    gu, act = _ffn_up(hn3, wgu_g.reshape(2, 2, D_MODEL, FFN_HALF), tw)
    landed = land_gather(["ffn_w_down"], gather_down, act)
    wd_g = share_gather(["ffn_w_down"], landed)["ffn_w_down"].reshape(FFN_HIDDEN, D_MODEL)
    dh3, dh3_b, sq, d_final_g = _ffn_down_loss(act, wd_g, h2, row(final_norm_g), tgt2, ts)
    loss_here = jnp.broadcast_to(0.5 * jnp.sum(sq) / D_MODEL, (1, 2, SUBLANES, LANES))

    def split(g, nm):
        r, c = big[nm].shape
        return g.reshape(N_CHIPS, 2, r // 2, c)

    def chip_sums(group, arrays, got):
        sums, parts = [None] * len(group), [None] * len(group)
        for blocks in (N_CHIPS, 1):
            idx = [i for i, a in enumerate(arrays) if a.shape[0] == blocks]
            if idx:
                out = _add_halves([arrays[i] for i in idx], [got[i] for i in idx], pos, "chip_sum_" + group[idx[0]],
                                  [F32 if group[i] == "loss" else BF16 for i in idx])
                for k, i in enumerate(idx):
                    sums[i], parts[i] = out[0][k], out[1][k]
        return sums, parts

    def start_swap(group, grads):
        return _swap_start([split(g, nm) for g, nm in zip(grads, group)], "swap_start_" + group[0])

    def start_exchange(group, swapping, after, landed):
        sems, arrays, lands, _ = swapping
        arrays, got = _swap_wait(sems, arrays, lands, after, "swap_wait_" + group[0])
        sums, parts = chip_sums(group, arrays, got)
        return _exchange_start(sums, parts, "exchange_start_" + group[0], landed)

    def wait_exchange(group, started, after):
        sems, sums, parts, _ = started
        return _exchange_wait(sems, sums, parts, after, "exchange_wait_" + group[0])

    def finish_exchange(group, started, after):
        return _sum_chips(wait_exchange(group, started, after), pos, "total_" + group[0])

    def join_and_update(group, after):
        joined = _join_halves([halves_of[nm] for nm in group], "join_halves_" + group[0], after)
        outs = _adamw([(weights[nm], j.reshape(big[nm].shape), m_in[nm], v_in[nm]) for nm, j in zip(group, joined)],
                      "adamw_" + group[0])
        for nm, out in zip(group, outs):
            grads[nm], delta[nm], new_m[nm], new_v[nm] = out
        return [new_v[nm] for nm in group]

    as3 = lambda a: a.reshape((1,) + a.shape)
    halves_of = {}

    g_down = _grad_w(act, as3(dh3_b), FFN_HALF, D_MODEL, "grad_ffn_w_down")
    group_a = ["ffn_w_down"]
    swap_a = start_swap(group_a, [g_down])
    dgu, dh2, dh2_b, d_ffn_g = _ffn_bwd(dh3, wd_g.reshape(2, FFN_HALF, D_MODEL), gu, wgu_g, h2, row(norm_ffn_g), tb,
                                        after=swap_a[3])
    exch_a = start_exchange(group_a, swap_a, dh2, wd_g)
    g_gu = _grad_w(hn3, dgu, D_MODEL, FFN_HALF, "grad_ffn_w_gate_up", after=exch_a[3])
    halves_of.update(zip(group_a, finish_exchange(group_a, exch_a, g_gu)))

    group_b = ["ffn_w_gate_up"]
    swap_b = start_swap(group_b, [g_gu])
    dh1, dh1_b, dq, dkv, d_xa_g = _attn_bwd(dh2, wo_g, q, kv, wq_g, h1, row(norm_xa_g), ts, after=swap_b[3])
    exch_b = start_exchange(group_b, swap_b, dh1, [halves_of[nm] for nm in group_a])
    g_wkv, d_mem_g = _mem_kv_bwd(dkv, mn, wkv_g, mem2, row(mem_norm_g), after=exch_b[3])
    g_wo, g_wq, g_wout = _grad_w_square([(o, dh2_b), (hn2, dq), (mix, dh1_b)], "grad_xa_wo", after=exch_b[3])
    done_a = join_and_update(group_a, (g_wkv, g_wo, g_wq, g_wout))
    halves_of.update(zip(group_b, finish_exchange(group_b, exch_b, done_a)))

    group_c = ["xa_wo", "xa_wq", "xa_wkv", "w_out"]
    swap_c = start_swap(group_c, [g_wo, g_wq, g_wkv, g_wout])
    (gx, dz, d_cw, d_cb, d_lng, d_lnb, d_gg, d_gb, d_ws, d_bs_sum, d_bin, d_mix_g) = _seqmix_bwd(
        dh1, x2, z, c1, w_out_g, w_in_g, row(norm_mix_g), cw_g, row(conv_ln_g), row(conv_ln_b),
        row(gm_ln_g), row(gm_ln_b), wpair, wpair_t, bias, tb, after=swap_c[3])
    d_bs = _head_bias_grad(d_bs_sum)[:, :GM_HEADS].T
    exch_c = start_exchange(group_c, swap_c, dz, [halves_of[nm] for nm in group_b])
    g_win = _grad_w(hn1, as3(dz), D_MODEL, 1024, "grad_w_in", after=exch_c[3], shards=2)
    done_b = join_and_update(group_b, g_win)
    parts_c = wait_exchange(group_c, exch_c, (g_win, *done_b))

    small_names = ["norm_mix_g", "b_in", "conv_w", "conv_b", "conv_ln_g", "conv_ln_b", "gm_ln_g", "gm_ln_b",
                   "gm_w_s", "gm_b_s", "norm_xa_g", "mem_norm_g", "norm_ffn_g", "final_norm_g"]
    d_cw_by_chip = jnp.swapaxes(d_cw.reshape(CONV_HALO, N_CHIPS, LANES), 0, 1).reshape(-1, LANES)
    small_grads = dict(norm_mix_g=d_mix_g, b_in=d_bin, conv_w=d_cw_by_chip, conv_b=d_cb, conv_ln_g=d_lng,
                       conv_ln_b=d_lnb, gm_ln_g=d_gg, gm_ln_b=d_gb, gm_w_s=d_ws, gm_b_s=d_bs, norm_xa_g=d_xa_g,
                       mem_norm_g=d_mem_g, norm_ffn_g=d_ffn_g, final_norm_g=d_final_g)

    def rows_form(a):
        a = a.reshape(-1, LANES)
        return jnp.pad(a, ((0, -a.shape[0] % SUBLANES), (0, 0)))

    pieces = [rows_form(small_grads[nm]) for nm in small_names]
    offsets, total = [], 0
    for p in pieces:
        offsets.append(total)
        total += p.shape[0]
    pack_rows = -(-total // 32) * 32
    small_pack = jnp.pad(jnp.concatenate(pieces, axis=0), ((0, pack_rows - total), (0, 0)))

    group_d = ["w_in", "small", "loss"]
    arrays_d = [split(g_win, "w_in"), small_pack.reshape(1, 2, pack_rows // 2, LANES), loss_here]
    sums_d, parts_d = chip_sums(group_d, arrays_d, _swap_halves(arrays_d, "swap_halves_w_in"))
    exch_d = _exchange_start(sums_d, parts_d, "exchange_start_w_in", parts_c)
    halves_of.update(zip(group_c, _sum_chips(parts_c, pos, "total_xa_wo", exch_d[3])))
    done_c = join_and_update(group_c, exch_d[3])
    halves_of.update(zip(group_d, finish_exchange(group_d, exch_d, done_c)))
    joined_d = _join_halves([halves_of[nm] for nm in group_d], "join_halves_w_in")
    loss = joined_d[2][0, 0, 0]
    grads["w_in"], delta["w_in"], new_m["w_in"], new_v["w_in"] = _adamw(
        [(w_in, joined_d[0].reshape(w_in.shape), m_w_in, v_w_in)], "adamw_w_in")[0]

    local_rows = lambda a, nm: a if nm == "conv_w" else a.reshape(-1, LANES)
    params = [tuple(local_rows(src[nm], nm) for src in (weights, m_in, v_in)) for nm in small_names]
    outs = _adamw_small(joined_d[1].reshape(pack_rows, LANES), pos, params, offsets, small_names.index("conv_w"))
    for k, nm in enumerate(small_names):
        for dst, a in zip((grads, delta, new_m, new_v), outs[4 * k:4 * k + 4]):
            dst[nm] = a

    order = ["norm_mix_g", "w_in", "b_in", "conv_w", "conv_b", "conv_ln_g", "conv_ln_b", "gm_ln_g", "gm_ln_b",
             "gm_w_s", "gm_b_s", "w_out", "norm_xa_g", "mem_norm_g", "xa_wq", "xa_wkv", "xa_wo", "norm_ffn_g",
             "ffn_w_gate_up", "ffn_w_down", "final_norm_g"]
    fit = lambda a, nm: a.reshape(weights[nm].shape)
    return (loss, gx.reshape(x.shape),
            *[fit(grads[nm], nm) for nm in order], *[fit(delta[nm], nm) for nm in order],
            *[fit(new_m[nm], nm) for nm in order], *[fit(new_v[nm], nm) for nm in order])
```

```python
import functools

import jax
import jax.numpy as jnp
from jax import lax
from jax.experimental import pallas as pl
from jax.experimental.pallas import tpu as pltpu

F32 = jnp.float32
BF16 = jnp.bfloat16

D_MODEL = 1024
CONV_WIDTH = 512
GM_WIDTH = 512
CONV_KERNEL = 31
CONV_HALO = 32
GRAD_ROWS = 2048
CHUNK = 128
GM_HEADS = 8
GM_HEAD_DIM = 64
XA_HEADS = 4
XA_HEAD_DIM = 256
FFN_HIDDEN = 2816
FFN_HALF = FFN_HIDDEN // 2
RMS_EPS = 1e-6
LN_EPS = 1e-5
N_CHIPS = 4
LANES = 128
SUBLANES = 8

ADAM_LR = 0.001
ADAM_B1 = 0.9
ADAM_B2 = 0.999
ADAM_EPS = 1e-08
ADAM_WD = 0.01
ADAM_STEP = 10

VMEM_LIMIT_BYTES = 56 * 1024 * 1024
MESH = pl.DeviceIdType.MESH
ANY = pl.BlockSpec(memory_space=pl.ANY)
HBM_SPEC = pl.BlockSpec(memory_space=pltpu.HBM)
SEM_SPEC = pl.BlockSpec(memory_space=pltpu.SEMAPHORE)

_NT = (((1,), (1,)), ((), ()))
_TN = (((0,), (0,)), ((), ()))
_GELU_C = 0.7978845608028654
_GELU_A = 0.044715


def _dot(a, b):
    return jnp.dot(a, b, preferred_element_type=F32)


def _dot_nt(a, b):
    return lax.dot_general(a, b, _NT, preferred_element_type=F32)


def _dot_tn(a, b):
    return lax.dot_general(a, b, _TN, preferred_element_type=F32)


def _mean(v):
    return jnp.mean(v, axis=-1, keepdims=True)


def _rowsum(v):
    return jnp.sum(v, axis=0, keepdims=True)


def _sigmoid(v):
    return 1.0 / (1.0 + jnp.exp(-v))


def _gelu_parts(v):
    v2 = v * v
    t = jnp.tanh(_GELU_C * (v + _GELU_A * v * v2))
    g = 0.5 * v * (1.0 + t)
    dg = 0.5 * (1.0 + t) + 0.5 * v * (1.0 - t * t) * (_GELU_C * (1.0 + 3.0 * _GELU_A * v2))
    return g, dg


def _rms_stats(v):
    return lax.rsqrt(_mean(v * v) + RMS_EPS)


def _rms_bwd(dy, v, r, g):
    n = v * r
    dn = dy * g
    dv = r * (dn - n * _mean(dn * n))
    return dv, _rowsum(dy * n)


def _ln_stats(v):
    mu = _mean(v)
    xc = v - mu
    rs = lax.rsqrt(_mean(xc * xc) + LN_EPS)
    return xc * rs, rs


def _ln_bwd(dy, xh, rs, g):
    dxh = dy * g
    dv = rs * (dxh - _mean(dxh) - xh * _mean(dxh * xh))
    return dv, _rowsum(dy * xh), _rowsum(dy)


def _params(sem):
    return pltpu.CompilerParams(dimension_semantics=sem, vmem_limit_bytes=VMEM_LIMIT_BYTES)


def _row_tile(s):
    return 512 if s % 512 == 0 and s >= 2048 else 128


def _mesh_pos():
    return lax.axis_index("x"), lax.axis_index("y"), lax.axis_index("c")


def _slot(buf, chip_idx, half):
    if buf.shape[0] == N_CHIPS:
        return buf.at[chip_idx, half]
    width = buf.shape[-1] // 2
    return buf.at[chip_idx // 2, half, :, pl.ds(pl.multiple_of((chip_idx % 2) * width, LANES), width)]


def _cast_into_slots(ws, pos, dtypes, name, side_by_side=()):
    n = len(ws)

    def body(pos_ref, *refs):
        for a in range(n):
            refs[n + a][0] = refs[a][...].astype(dtypes[a])

    def out_spec(a, w):
        if a in side_by_side:
            return pl.BlockSpec((1, 1) + w.shape[1:], lambda i, p: (p[0] // 2, i, 0, p[0] % 2))
        return pl.BlockSpec((1, 1) + w.shape[1:], lambda i, p: (p[0], i, 0, 0))

    def out_shape(a, w):
        if a in side_by_side:
            return (2, 2, w.shape[1], 2 * w.shape[2])
        return (N_CHIPS,) + w.shape

    return pl.pallas_call(
        body, name=name,
        grid_spec=pltpu.PrefetchScalarGridSpec(
            num_scalar_prefetch=1, grid=(2,),
            in_specs=[pl.BlockSpec((1,) + w.shape[1:], lambda i, p: (i, 0, 0)) for w in ws],
            out_specs=[out_spec(a, w) for a, w in enumerate(ws)]),
        out_shape=[jax.ShapeDtypeStruct(out_shape(a, w), dt) for a, (w, dt) in enumerate(zip(ws, dtypes))],
        compiler_params=_params(("parallel",)),
    )(pos, *ws)


def _adam_update(w, g, m, v):
    nm = ADAM_B1 * m + (1.0 - ADAM_B1) * g
    nv = ADAM_B2 * v + (1.0 - ADAM_B2) * (g * g)
    m_hat = nm / (1.0 - ADAM_B1 ** ADAM_STEP)
    v_hat = nv / (1.0 - ADAM_B2 ** ADAM_STEP)
    return -ADAM_LR * (m_hat / (jnp.sqrt(v_hat) + ADAM_EPS) + ADAM_WD * w), nm, nv


ADAM_STEPS = 4


def _adamw(quads, name, after=()):
    n = len(quads)

    def body(*refs):
        ins, outs = refs[:4 * n], refs[4 * n:]
        for a in range(n):
            w, g, m, v = (r[...] for r in ins[4 * a:4 * a + 4])
            outs[4 * a][...] = g
            outs[4 * a + 1][...], outs[4 * a + 2][...], outs[4 * a + 3][...] = _adam_update(w, g, m, v)

    specs = [pl.BlockSpec((q[0].shape[0] // ADAM_STEPS, q[0].shape[1]), lambda i: (i, 0)) for q in quads]
    out = _tied_call(
        body, after, name=name, grid=(ADAM_STEPS,),
        in_specs=[sp for sp in specs for _ in range(4)], out_specs=[sp for sp in specs for _ in range(4)],
        out_shape=[jax.ShapeDtypeStruct(q[0].shape, F32) for q in quads for _ in range(4)],
        compiler_params=_params(("parallel",)),
    )(*[a for q in quads for a in q])
    return [tuple(out[4 * a:4 * a + 4]) for a in range(n)]


def _adamw_small(gpack, pos, params, offsets, conv_at):
    n = len(params)

    def body(pos_ref, g_ref, *refs):
        ins, outs = refs[:3 * n], refs[3 * n:]
        for k in range(n):
            rows = params[k][0].shape[0]
            start = offsets[k]
            if k == conv_at:
                start = pl.multiple_of(start + pos_ref[0] * CONV_HALO, SUBLANES)
            g = g_ref[pl.ds(start, rows), :]
            outs[4 * k][...] = g
            outs[4 * k + 1][...], outs[4 * k + 2][...], outs[4 * k + 3][...] = _adam_update(
                ins[3 * k][...], g, ins[3 * k + 1][...], ins[3 * k + 2][...])

    flat = [a for p in params for a in p]
    vmem = pl.BlockSpec(memory_space=pltpu.VMEM)
    return pl.pallas_call(
        body, name="adamw_small",
        in_specs=[pl.BlockSpec(memory_space=pltpu.SMEM), vmem] + [vmem] * len(flat),
        out_specs=[vmem] * (4 * n),
        out_shape=[jax.ShapeDtypeStruct(p[0].shape, F32) for p in params for _ in range(4)],
    )(pos, gpack, *flat)


def _as_tuple(after):
    return tuple(after) if isinstance(after, (tuple, list)) else (after,)


def _tied_call(body, after, *, in_specs, **kwargs):
    after = _as_tuple(after)
    n_in, n_after = len(in_specs), len(after)

    def tied(*refs):
        body(*refs[:n_in], *refs[n_in + n_after:])

    call = pl.pallas_call(tied, in_specs=list(in_specs) + [ANY] * n_after, **kwargs)
    return lambda *operands: call(*operands, *after)


def _other_chips(x, y):
    return [(1 - x, y), (x, 1 - y), (1 - x, 1 - y)]


def _gather_descriptors(bufs, send_of, recv_of):
    x, y, c = _mesh_pos()
    me = 2 * x + y
    chips = _other_chips(x, y)
    sends, arrivals = [], []
    for a in range(len(bufs)):
        for k in range(3):
            ck = 2 * chips[k][0] + chips[k][1]

            def copy(slot, a=a, k=k):
                return pltpu.make_async_remote_copy(
                    src_ref=_slot(bufs[a], slot, c), dst_ref=_slot(bufs[a], slot, c),
                    send_sem=send_of(a, k), recv_sem=recv_of(a, k),
                    device_id=(*chips[k], c), device_id_type=MESH)

            sends.append(functools.partial(copy, me))
            arrivals.append(functools.partial(copy, ck))
    return sends, arrivals


def _gather_start(bufs, name, after=()):
    n = len(bufs)
    ns = 3 * n

    def body(*refs):
        sems = refs[n:n + 2 * ns]
        thru = refs[n + 2 * ns:2 * n + 2 * ns]
        token = refs[2 * n + 2 * ns]
        _chips_handshake()
        sends, _ = _gather_descriptors(thru, lambda a, k: sems[3 * a + k], lambda a, k: sems[ns + 3 * a + k])
        for cp in sends:
            cp().start()
        token[...] = jnp.zeros_like(token)

    held = [pltpu.with_memory_space_constraint(b, pltpu.HBM) for b in bufs]
    out = _tied_call(
        body, after, name=name,
        out_shape=(*[pltpu.SemaphoreType.DMA(())] * (2 * ns), *[pltpu.HBM(b.shape, b.dtype) for b in held],
                   jax.ShapeDtypeStruct((8, LANES), F32)),
        in_specs=[HBM_SPEC] * n,
        out_specs=(*[SEM_SPEC] * (2 * ns), *[HBM_SPEC] * n, pl.BlockSpec(memory_space=pltpu.VMEM)),
        input_output_aliases={i: 2 * ns + i for i in range(n)},
        compiler_params=pltpu.CompilerParams(has_side_effects=pltpu.SideEffectType.DATAFLOW_SIDE_EFFECTING,
                                             collective_id=CHIPS_COLLECTIVE_ID),
    )(*held)
    return list(out[:ns]), list(out[ns:2 * ns]), list(out[2 * ns:2 * ns + n]), out[2 * ns + n]


def _gather_wait(send_sems, recv_sems, bufs, after, name):
    n = len(bufs)
    ns = 3 * n

    def body(*refs):
        buf_ref = refs[:n]
        sem_ref = refs[n:n + 2 * ns]
        sends, arrivals = _gather_descriptors(buf_ref, lambda a, k: sem_ref[3 * a + k],
                                              lambda a, k: sem_ref[ns + 3 * a + k])
        for cp in sends:
            cp().wait_send()
        for cp in arrivals:
            cp().wait_recv()

    out = pl.pallas_call(
        body, name=name,
        out_shape=tuple(pltpu.HBM(b.shape, b.dtype) for b in bufs),
        in_specs=[HBM_SPEC] * n + [SEM_SPEC] * (2 * ns) + [ANY] * len(_as_tuple(after)),
        out_specs=tuple([HBM_SPEC] * n),
        input_output_aliases={i: i for i in range(n)},
        compiler_params=pltpu.CompilerParams(has_side_effects=pltpu.SideEffectType.DATAFLOW_SIDE_EFFECTING),
    )(*bufs, *send_sems, *recv_sems, *_as_tuple(after))
    return list(out)


SIBLING_COLLECTIVE_ID = 0


def _sibling_handshake():
    x, y, c = _mesh_pos()
    barrier = pltpu.get_barrier_semaphore()
    pl.semaphore_signal(barrier, inc=1, device_id=(x, y, 1 - c), device_id_type=MESH)
    pl.semaphore_wait(barrier, 1)


CHIPS_COLLECTIVE_ID = 1


def _chips_handshake():
    x, y, c = _mesh_pos()
    barrier = pltpu.get_barrier_semaphore()
    for chip in _other_chips(x, y):
        pl.semaphore_signal(barrier, inc=1, device_id=(*chip, c), device_id_type=MESH)
    pl.semaphore_wait(barrier, 3)


def _pass_to_sibling(bufs, name, after=()):
    n = len(bufs)

    def body(*refs):
        outs = refs[n:2 * n]
        send_sem, recv_sem = refs[2 * n:]
        x, y, c = _mesh_pos()
        chips = _other_chips(x, y)
        _sibling_handshake()

        def half(a, k, which):
            ck = 2 * chips[k][0] + chips[k][1]
            return pltpu.make_async_remote_copy(
                src_ref=_slot(outs[a], ck, which), dst_ref=_slot(outs[a], ck, which),
                send_sem=send_sem.at[a, k], recv_sem=recv_sem.at[a, k],
                device_id=(x, y, 1 - c), device_id_type=MESH)

        sends = [half(a, k, c) for a in range(n) for k in range(3)]
        for cp in sends:
            cp.start()
        for a in range(n):
            for k in range(3):
                half(a, k, 1 - c).wait_recv()
        for cp in sends:
            cp.wait_send()

    return _tied_call(
        body, after, name=name,
        in_specs=[ANY] * n, out_specs=[ANY] * n,
        out_shape=[jax.ShapeDtypeStruct(b.shape, b.dtype) for b in bufs],
        input_output_aliases={a: a for a in range(n)},
        scratch_shapes=[pltpu.SemaphoreType.DMA((n, 3))] * 2,
        compiler_params=pltpu.CompilerParams(collective_id=SIBLING_COLLECTIVE_ID),
    )(*bufs)


def _swap_halves(grads, name):
    n = len(grads)

    def body(*refs):
        ins, outs = refs[:n], refs[n:2 * n]
        send_sem, recv_sem = refs[2 * n:]
        x, y, c = _mesh_pos()
        _sibling_handshake()
        cps = [pltpu.make_async_remote_copy(
            src_ref=ins[a].at[:, pl.ds(1 - c, 1)], dst_ref=outs[a],
            send_sem=send_sem.at[a], recv_sem=recv_sem.at[a],
            device_id=(x, y, 1 - c), device_id_type=MESH) for a in range(n)]
        for cp in cps:
            cp.start()
        for cp in cps:
            cp.wait()

    out_shape = [jax.ShapeDtypeStruct((g.shape[0], 1) + g.shape[2:], g.dtype) for g in grads]
    return pl.pallas_call(
        body, name=name,
        in_specs=[ANY] * n, out_specs=[ANY] * n, out_shape=out_shape,
        scratch_shapes=[pltpu.SemaphoreType.DMA((n,))] * 2,
        compiler_params=pltpu.CompilerParams(collective_id=SIBLING_COLLECTIVE_ID),
    )(*grads)


def _swap_descriptors(grads, lands, send_of, recv_of):
    x, y, c = _mesh_pos()
    return [functools.partial(
        pltpu.make_async_remote_copy,
        src_ref=grads[a].at[:, pl.ds(1 - c, 1)], dst_ref=lands[a],
        send_sem=send_of(a), recv_sem=recv_of(a),
        device_id=(x, y, 1 - c), device_id_type=MESH) for a in range(len(grads))]


def _swap_start(grads, name):
    n = len(grads)

    def body(*refs):
        sems = refs[2 * n:4 * n]
        g_thru, l_thru = refs[4 * n:5 * n], refs[5 * n:6 * n]
        token = refs[6 * n]
        _sibling_handshake()
        for cp in _swap_descriptors(g_thru, l_thru, lambda a: sems[a], lambda a: sems[n + a]):
            cp().start()
        token[...] = jnp.zeros_like(token)

    lands = [lax.empty((g.shape[0], 1) + g.shape[2:], g.dtype) for g in grads]
    held = [pltpu.with_memory_space_constraint(a, pltpu.HBM) for a in (*grads, *lands)]
    out = pl.pallas_call(
        body, name=name,
        out_shape=(*[pltpu.SemaphoreType.DMA(())] * (2 * n), *[pltpu.HBM(a.shape, a.dtype) for a in held],
                   jax.ShapeDtypeStruct((8, LANES), F32)),
        in_specs=[HBM_SPEC] * (2 * n),
        out_specs=(*[SEM_SPEC] * (2 * n), *[HBM_SPEC] * (2 * n), pl.BlockSpec(memory_space=pltpu.VMEM)),
        input_output_aliases={i: 2 * n + i for i in range(2 * n)},
        compiler_params=pltpu.CompilerParams(has_side_effects=pltpu.SideEffectType.DATAFLOW_SIDE_EFFECTING,
                                             collective_id=SIBLING_COLLECTIVE_ID),
    )(*held)
    return list(out[:2 * n]), list(out[2 * n:3 * n]), list(out[3 * n:4 * n]), out[4 * n]


def _swap_wait(sems, grads, lands, after, name):
    n = len(grads)

    def body(*refs):
        g_ref, l_ref = refs[:n], refs[n:2 * n]
        sem_ref = refs[2 * n:4 * n]
        for cp in _swap_descriptors(g_ref, l_ref, lambda a: sem_ref[a], lambda a: sem_ref[n + a]):
            cp().wait()

    out = pl.pallas_call(
        body, name=name,
        out_shape=tuple(pltpu.HBM(a.shape, a.dtype) for a in (*grads, *lands)),
        in_specs=[HBM_SPEC] * (2 * n) + [SEM_SPEC] * (2 * n) + [ANY] * len(_as_tuple(after)),
        out_specs=tuple([HBM_SPEC] * (2 * n)),
        input_output_aliases={i: i for i in range(2 * n)},
        compiler_params=pltpu.CompilerParams(has_side_effects=pltpu.SideEffectType.DATAFLOW_SIDE_EFFECTING),
    )(*grads, *lands, *sems, *_as_tuple(after))
    return list(out[:n]), list(out[n:])


def _add_halves(gs, gots, pos, name, dtypes):
    n = len(gs)
    j = gs[0].shape[0]

    def body(pos_ref, *refs):
        g_refs, r_refs = refs[:n], refs[n:2 * n]
        o_refs, p_refs = refs[2 * n:3 * n], refs[3 * n:]
        vals = [(g_refs[a][0, 0] + r_refs[a][0, 0]).astype(dtypes[a]) for a in range(n)]
        for a in range(n):
            o_refs[a][0] = vals[a]
        if j == 1:
            for a in range(n):
                p_refs[a][0] = vals[a]
        else:
            @pl.when(pl.program_id(0) == pos_ref[0])
            def _():
                for a in range(n):
                    p_refs[a][0] = vals[a]

    blk = lambda g: (1,) + g.shape[2:]
    out = pl.pallas_call(
        body, name=name,
        grid_spec=pltpu.PrefetchScalarGridSpec(
            num_scalar_prefetch=1, grid=(j,),
            in_specs=[pl.BlockSpec((1,) + blk(g), lambda i, p: (i, p[1], 0, 0)) for g in gs]
            + [pl.BlockSpec((1,) + blk(g), lambda i, p: (i, 0, 0, 0)) for g in gs],
            out_specs=[pl.BlockSpec(blk(g), lambda i, p: (i, 0, 0)) for g in gs]
            + [pl.BlockSpec(blk(g), lambda i, p: (p[0], 0, 0)) for g in gs]),
        out_shape=[jax.ShapeDtypeStruct((j,) + g.shape[2:], dt) for g, dt in zip(gs, dtypes)]
        + [jax.ShapeDtypeStruct((N_CHIPS,) + g.shape[2:], dt) for g, dt in zip(gs, dtypes)],
        compiler_params=_params(("arbitrary",)),
    )(pos, *gs, *gots)
    return list(out[:n]), list(out[n:])


def _exchange_descriptors(sums, parts, send_of, recv_of):
    x, y, c = _mesh_pos()
    me = 2 * x + y
    chips = _other_chips(x, y)
    sends, arrivals = [], []
    for a in range(len(sums)):
        for k in range(3):
            ck = 2 * chips[k][0] + chips[k][1]
            mine = sums[a].at[ck] if sums[a].shape[0] == N_CHIPS else sums[a].at[0]

            def copy(dst_slot, a=a, k=k, mine=mine):
                return pltpu.make_async_remote_copy(
                    src_ref=mine, dst_ref=parts[a].at[dst_slot],
                    send_sem=send_of(a, k), recv_sem=recv_of(a, k),
                    device_id=(*chips[k], c), device_id_type=MESH)

            sends.append(functools.partial(copy, me))
            arrivals.append(functools.partial(copy, ck))
    return sends, arrivals


def _exchange_start(sums, parts, name, after=()):
    n = len(sums)
    ns = 3 * n

    def body(*refs):
        sems = refs[2 * n:2 * n + 2 * ns]
        sums_thru = refs[2 * n + 2 * ns:3 * n + 2 * ns]
        parts_thru = refs[3 * n + 2 * ns:4 * n + 2 * ns]
        token = refs[4 * n + 2 * ns]
        _chips_handshake()
        sends, _ = _exchange_descriptors(sums_thru, parts_thru, lambda a, k: sems[3 * a + k],
                                         lambda a, k: sems[ns + 3 * a + k])
        for cp in sends:
            cp().start()
        token[...] = jnp.zeros_like(token)

    hbm = lambda a: pltpu.HBM(a.shape, a.dtype)
    held = [pltpu.with_memory_space_constraint(a, pltpu.HBM) for a in (*sums, *parts)]
    out = _tied_call(
        body, after, name=name,
        out_shape=(*[pltpu.SemaphoreType.DMA(())] * (2 * ns), *[hbm(a) for a in held],
                   jax.ShapeDtypeStruct((8, LANES), F32)),
        in_specs=[HBM_SPEC] * (2 * n),
        out_specs=(*[SEM_SPEC] * (2 * ns), *[HBM_SPEC] * (2 * n), pl.BlockSpec(memory_space=pltpu.VMEM)),
        input_output_aliases={i: 2 * ns + i for i in range(2 * n)},
        compiler_params=pltpu.CompilerParams(has_side_effects=pltpu.SideEffectType.DATAFLOW_SIDE_EFFECTING,
                                             collective_id=CHIPS_COLLECTIVE_ID),
    )(*held)
    return (list(out[:2 * ns]), list(out[2 * ns:2 * ns + n]), list(out[2 * ns + n:2 * ns + 2 * n]),
            out[2 * ns + 2 * n])


def _exchange_wait(sems, sums, parts, after, name):
    n = len(sums)
    ns = 3 * n

    def body(*refs):
        sums_ref, parts_ref = refs[:n], refs[n:2 * n]
        sem_ref = refs[2 * n:2 * n + 2 * ns]
        sends, arrivals = _exchange_descriptors(sums_ref, parts_ref, lambda a, k: sem_ref[3 * a + k],
                                                lambda a, k: sem_ref[ns + 3 * a + k])
        for cp in sends:
            cp().wait_send()
        for cp in arrivals:
            cp().wait_recv()

    hbm = lambda a: pltpu.HBM(a.shape, a.dtype)
    out = pl.pallas_call(
        body, name=name,
        out_shape=tuple(hbm(a) for a in (*sums, *parts)),
        in_specs=[HBM_SPEC] * (2 * n) + [SEM_SPEC] * (2 * ns) + [ANY] * len(_as_tuple(after)),
        out_specs=tuple([HBM_SPEC] * (2 * n)),
        input_output_aliases={i: i for i in range(2 * n)},
        compiler_params=pltpu.CompilerParams(has_side_effects=pltpu.SideEffectType.DATAFLOW_SIDE_EFFECTING),
    )(*sums, *parts, *sems, *_as_tuple(after))
    return list(out[n:])


def _sum_chips(parts, pos, name, after=()):
    n = len(parts)
    after = _as_tuple(after)

    def body(pos_ref, *refs):
        outs = refs[n + len(after):]
        for a in range(n):
            p_ref = refs[a]
            outs[a][0] = (((p_ref[0].astype(F32) + p_ref[1].astype(F32)) + p_ref[2].astype(F32))
                          + p_ref[3].astype(F32))

    out = pl.pallas_call(
        body, name=name,
        grid_spec=pltpu.PrefetchScalarGridSpec(
            num_scalar_prefetch=1, grid=(1,),
            in_specs=[pl.BlockSpec(p.shape, lambda i, q: (0, 0, 0)) for p in parts] + [ANY] * len(after),
            out_specs=[pl.BlockSpec((1,) + p.shape[1:], lambda i, q: (q[1], 0, 0)) for p in parts]),
        out_shape=[jax.ShapeDtypeStruct((2,) + p.shape[1:], F32) for p in parts],
        compiler_params=_params(("arbitrary",)),
    )(pos, *parts, *after)
    return list(out)


def _join_halves(fulls, name, after=()):
    n = len(fulls)

    def body(*refs):
        outs = refs[n:2 * n]
        send_sem, recv_sem = refs[2 * n:]
        x, y, c = _mesh_pos()
        _sibling_handshake()

        def half(a, which):
            return pltpu.make_async_remote_copy(
                src_ref=outs[a].at[which], dst_ref=outs[a].at[which],
                send_sem=send_sem.at[a], recv_sem=recv_sem.at[a],
                device_id=(x, y, 1 - c), device_id_type=MESH)

        sends = [half(a, c) for a in range(n)]
        for cp in sends:
            cp.start()
        for a in range(n):
            half(a, 1 - c).wait_recv()
        for cp in sends:
            cp.wait_send()

    out_shape = [jax.ShapeDtypeStruct(f.shape, f.dtype) for f in fulls]
    return _tied_call(
        body, after, name=name,
        in_specs=[ANY] * n, out_specs=[ANY] * n, out_shape=out_shape,
        input_output_aliases={a: a for a in range(n)},
        scratch_shapes=[pltpu.SemaphoreType.DMA((n,))] * 2,
        compiler_params=pltpu.CompilerParams(collective_id=SIBLING_COLLECTIVE_ID),
    )(*fulls)


def _norm_in(x, g, ts, after=()):
    s = x.shape[0]

    def body(x_ref, g_ref, hn_ref):
        xv = x_ref[...]
        hn_ref[...] = (xv * _rms_stats(xv) * g_ref[...]).astype(BF16)

    row = pl.BlockSpec((ts, D_MODEL), lambda i: (i, 0))
    return _tied_call(
        body, after, name="norm_in", grid=(s // ts,),
        in_specs=[row, pl.BlockSpec((1, D_MODEL), lambda i: (0, 0))], out_specs=row,
        out_shape=jax.ShapeDtypeStruct((s, D_MODEL), BF16),
        compiler_params=_params(("parallel",)),
    )(x, g)


def _shift_rows(buf, shifted, t):
    rows = t + CONV_HALO - SUBLANES
    for r in range(1, SUBLANES):
        shifted[r - 1, 0:rows, :] = buf[pl.ds(r, rows), :]


def _window(buf, shifted, offset, t):
    r = offset % SUBLANES
    if r == 0:
        return buf[pl.ds(offset, t), :]
    return shifted[r - 1, pl.ds(offset - r, t), :]


def _lane_is_low_head():
    lane = lax.broadcasted_iota(jnp.int32, (1, GM_WIDTH), 1)
    return (lane & GM_HEAD_DIM) == 0


def _gm_mix(v_lo, v_hi, wpair_ref, bias_ref, mixed_ref, t):
    for n in range(t // CHUNK):
        rows = slice(n * CHUNK, (n + 1) * CHUNK)
        for j in range(GM_HEADS // 2):
            cols = slice(j * LANES, (j + 1) * LANES)
            rhs = jnp.concatenate([v_lo[rows, cols], v_hi[rows, cols]], axis=0)
            mixed_ref[rows, cols] = _dot(wpair_ref[j], rhs) + bias_ref[:, cols]


def _seqmix_fwd(hn, w_in, b_in, cw, cb, lng, lnb, gg, gb, wpair, bias, t, after=()):
    s = hn.shape[0]

    def body(hn_ref, w_ref, b_ref, cw_ref, cb_ref, lng_ref, lnb_ref, gg_ref, gb_ref, wpair_ref, bias_ref,
             z_ref, mix_ref, c1_ref, abuf, ash, mixed_ref):
        i = pl.program_id(0)

        @pl.when(i == 0)
        def _():
            abuf[0:CONV_HALO, :] = jnp.zeros((CONV_HALO, CONV_WIDTH), F32)

        @pl.when(i > 0)
        def _():
            abuf[0:CONV_HALO, :] = abuf[t:t + CONV_HALO, :]

        hv = hn_ref[...]
        for j in range(4):
            cols = slice(j * 512, (j + 1) * 512)
            z_ref[:, cols] = _dot(hv, w_ref[j]) + b_ref[:, cols]

        abuf[CONV_HALO:, :] = z_ref[:, 0:512] * _sigmoid(z_ref[:, 512:1024])
        _shift_rows(abuf, ash, t)
        acc = jnp.zeros((t, CONV_WIDTH), F32)
        for k in range(CONV_KERNEL):
            acc = acc + cw_ref[k:k + 1, :] * _window(abuf, ash, CONV_HALO - (CONV_KERNEL - 1) + k, t)
        c1 = acc + cb_ref[...]
        c1_ref[...] = c1
        xh, _ = _ln_stats(c1)
        ln = xh * lng_ref[...] + lnb_ref[...]
        mix_ref[:, 0:512] = (ln * _sigmoid(ln)).astype(BF16)

        u, _ = _gelu_parts(z_ref[:, 1024:1536])
        gv, _ = _gelu_parts(z_ref[:, 1536:2048])
        vxh, _ = _ln_stats(gv)
        v = vxh * gg_ref[...] + gb_ref[...]
        low = _lane_is_low_head()
        v_lo = jnp.where(low, v, 0.0).astype(BF16)
        v_hi = jnp.where(low, 0.0, v).astype(BF16)
        _gm_mix(v_lo, v_hi, wpair_ref, bias_ref, mixed_ref, t)
        mix_ref[:, 512:1024] = (u * mixed_ref[...]).astype(BF16)

    vec = lambda n: pl.BlockSpec((1, n), lambda i: (0, 0))
    return _tied_call(
        body, after, name="seqmix_fwd", grid=(s // t,),
        in_specs=[pl.BlockSpec((t, D_MODEL), lambda i: (i, 0)),
                  pl.BlockSpec((4, D_MODEL, 512), lambda i: (0, 0, 0)), vec(2048),
                  pl.BlockSpec((CONV_HALO, CONV_WIDTH), lambda i: (0, 0)),
                  vec(512), vec(512), vec(512), vec(512), vec(512),
                  pl.BlockSpec((4, CHUNK, 2 * CHUNK), lambda i: (0, 0, 0)),
                  pl.BlockSpec((CHUNK, GM_WIDTH), lambda i: (0, 0))],
        out_specs=[pl.BlockSpec((t, 2048), lambda i: (i, 0)),
                   pl.BlockSpec((t, D_MODEL), lambda i: (i, 0)),
                   pl.BlockSpec((t, CONV_WIDTH), lambda i: (i, 0))],
        out_shape=[jax.ShapeDtypeStruct((s, 2048), F32), jax.ShapeDtypeStruct((s, D_MODEL), BF16),
                   jax.ShapeDtypeStruct((s, CONV_WIDTH), F32)],
        scratch_shapes=[pltpu.VMEM((t + CONV_HALO, CONV_WIDTH), F32),
                        pltpu.VMEM((SUBLANES - 1, t + CONV_HALO - SUBLANES, CONV_WIDTH), F32),
                        pltpu.VMEM((t, GM_WIDTH), F32)],
        compiler_params=_params(("arbitrary",)),
    )(hn, w_in, b_in, cw, cb, lng, lnb, gg, gb, wpair, bias)


def _mem_kv(mem, g, wkv):
    m = mem.shape[0]

    def body(mem_ref, g_ref, w_ref, mn_ref, kv_ref):
        mv = mem_ref[...]
        mn = (mv * _rms_stats(mv) * g_ref[...]).astype(BF16)
        mn_ref[...] = mn
        for j in range(4):
            kv_ref[:, j * 512:(j + 1) * 512] = _dot(mn, w_ref[j]).astype(BF16)

    return pl.pallas_call(
        body, name="mem_kv",
        out_shape=[jax.ShapeDtypeStruct((m, D_MODEL), BF16), jax.ShapeDtypeStruct((m, 2 * D_MODEL), BF16)],
        compiler_params=pltpu.CompilerParams(vmem_limit_bytes=VMEM_LIMIT_BYTES),
    )(mem, g, wkv)


def _softmax_rows(sc):
    e = jnp.exp(sc - jnp.max(sc, axis=-1, keepdims=True))
    return e / jnp.sum(e, axis=-1, keepdims=True)


def _attn_block_fwd(x, mix, w_out, g_xa, wq, kv, wo, g_ffn, ts, after=()):
    s, m = x.shape[0], kv.shape[0]
    scale = XA_HEAD_DIM ** -0.5

    def body(x_ref, mix_ref, wout_ref, gxa_ref, wq_ref, kv_ref, wo_ref, gffn_ref,
             h1_ref, hn2_ref, q_ref, o_ref, h2_ref, hn3_ref):
        h1 = x_ref[...] + _dot(mix_ref[...], wout_ref[...])
        h1_ref[...] = h1
        hn2 = (h1 * _rms_stats(h1) * gxa_ref[...]).astype(BF16)
        hn2_ref[...] = hn2
        q_ref[...] = _dot(hn2, wq_ref[...]).astype(BF16)
        for h in range(XA_HEADS):
            cols = slice(h * XA_HEAD_DIM, (h + 1) * XA_HEAD_DIM)
            vcols = slice(D_MODEL + h * XA_HEAD_DIM, D_MODEL + (h + 1) * XA_HEAD_DIM)
            p = _softmax_rows(_dot_nt(q_ref[:, cols], kv_ref[:, cols]) * scale)
            o_ref[:, cols] = _dot(p.astype(BF16), kv_ref[:, vcols]).astype(BF16)
        h2 = h1 + _dot(o_ref[...], wo_ref[...])
        h2_ref[...] = h2
        hn3_ref[...] = (h2 * _rms_stats(h2) * gffn_ref[...]).astype(BF16)

    row = pl.BlockSpec((ts, D_MODEL), lambda i: (i, 0))
    full = pl.BlockSpec((D_MODEL, D_MODEL), lambda i: (0, 0))
    vec = pl.BlockSpec((1, D_MODEL), lambda i: (0, 0))
    f32 = jax.ShapeDtypeStruct((s, D_MODEL), F32)
    bf16 = jax.ShapeDtypeStruct((s, D_MODEL), BF16)
    return _tied_call(
        body, after, name="attn_block_fwd", grid=(s // ts,),
        in_specs=[row, row, full, vec, full, pl.BlockSpec((m, 2 * D_MODEL), lambda i: (0, 0)), full, vec],
        out_specs=[row] * 6,
        out_shape=[f32, bf16, bf16, bf16, f32, bf16],
        compiler_params=_params(("parallel",)),
    )(x, mix, w_out, g_xa, wq, kv, wo, g_ffn)


_FFN_CHUNKS = (slice(0, 8 * LANES), slice(8 * LANES, 16 * LANES), slice(16 * LANES, FFN_HIDDEN))


def _ffn_up(hn, wgu, ts, after=()):
    s = hn.shape[0]

    def body(hn_ref, w_ref, gu_ref, act_ref):
        hv = hn_ref[...]
        for cols in _FFN_CHUNKS:
            gate = _dot(hv, w_ref[0, :, cols])
            up = _dot(hv, w_ref[1, :, cols])
            gu_ref[0, :, cols] = gate.astype(BF16)
            gu_ref[1, :, cols] = up.astype(BF16)
            act_ref[:, cols] = (gate * _sigmoid(gate) * up).astype(BF16)

    return _tied_call(
        body, after, name="ffn_up", grid=(s // ts,),
        in_specs=[pl.BlockSpec((ts, D_MODEL), lambda i: (i, 0)),
                  pl.BlockSpec((2, D_MODEL, FFN_HIDDEN), lambda i: (0, 0, 0))],
        out_specs=[pl.BlockSpec((2, ts, FFN_HIDDEN), lambda i: (0, i, 0)),
                   pl.BlockSpec((ts, FFN_HIDDEN), lambda i: (i, 0))],
        out_shape=[jax.ShapeDtypeStruct((2, s, FFN_HIDDEN), BF16), jax.ShapeDtypeStruct((s, FFN_HIDDEN), BF16)],
        compiler_params=_params(("parallel",)),
    )(hn, wgu)


def _ffn_down_loss(act, wd, h2, g, target, ts):
    s = act.shape[0]

    def body(act_ref, wd_ref, h2_ref, g_ref, t_ref, dh_ref, dhb_ref, sq_ref, dg_ref):
        @pl.when(pl.program_id(0) == 0)
        def _():
            sq_ref[...] = jnp.zeros_like(sq_ref)
            dg_ref[...] = jnp.zeros_like(dg_ref)

        h3 = h2_ref[...] + _dot(act_ref[...], wd_ref[...])
        r = _rms_stats(h3)
        gv = g_ref[...]
        diff = h3 * r * gv - t_ref[...]
        sq_ref[...] += _rowsum(diff * diff)
        dh, dg = _rms_bwd(diff / D_MODEL, h3, r, gv)
        dh_ref[...] = dh
        dhb_ref[...] = dh.astype(BF16)
        dg_ref[...] += dg

    row = pl.BlockSpec((ts, D_MODEL), lambda i: (i, 0))
    vec = pl.BlockSpec((1, D_MODEL), lambda i: (0, 0))
    return pl.pallas_call(
        body, name="ffn_down_loss", grid=(s // ts,),
        in_specs=[pl.BlockSpec((ts, FFN_HIDDEN), lambda i: (i, 0)),
                  pl.BlockSpec((FFN_HIDDEN, D_MODEL), lambda i: (0, 0)), row, vec, row],
        out_specs=[row, row, vec, vec],
        out_shape=[jax.ShapeDtypeStruct((s, D_MODEL), F32), jax.ShapeDtypeStruct((s, D_MODEL), BF16),
                   jax.ShapeDtypeStruct((1, D_MODEL), F32), jax.ShapeDtypeStruct((1, D_MODEL), F32)],
        compiler_params=_params(("arbitrary",)),
    )(act, wd, h2, g, target)


def _grad_w(a, b, tk, tn, name, after=(), shards=1):
    s, k = a.shape
    gb, _, n = b.shape
    nblk = n // tn
    ws = tn // shards
    tsr = GRAD_ROWS if s % GRAD_ROWS == 0 else s

    def body(a_ref, b_ref, o_ref):
        part = _dot_tn(a_ref[...], b_ref[0])

        @pl.when(pl.program_id(2) == 0)
        def _():
            for j in range(shards):
                o_ref[j] = part[:, j * ws:(j + 1) * ws]

        @pl.when(pl.program_id(2) > 0)
        def _():
            for j in range(shards):
                o_ref[j] += part[:, j * ws:(j + 1) * ws]

    return _tied_call(
        body, after, name=name, grid=(gb * nblk, k // tk, s // tsr),
        in_specs=[pl.BlockSpec((tsr, tk), lambda ni, ki, si: (si, ki)),
                  pl.BlockSpec((1, tsr, tn), lambda ni, ki, si: (ni // nblk, si, ni % nblk))],
        out_specs=pl.BlockSpec((shards, tk, ws), lambda ni, ki, si: (ni, ki, 0)),
        out_shape=jax.ShapeDtypeStruct((gb * nblk * shards, k, ws), F32),
        compiler_params=_params(("parallel", "parallel", "arbitrary")),
    )(a, b)


def _grad_w_square(pairs, name, after=()):
    n = len(pairs)
    s = pairs[0][0].shape[0]
    tsr = GRAD_ROWS // 2 if s % (GRAD_ROWS // 2) == 0 else s

    def body(*refs):
        ins, outs = refs[:2 * n], refs[2 * n:]
        parts = [_dot_tn(ins[2 * a][...], ins[2 * a + 1][...]) for a in range(n)]

        @pl.when(pl.program_id(0) == 0)
        def _():
            for a in range(n):
                outs[a][...] = parts[a]

        @pl.when(pl.program_id(0) > 0)
        def _():
            for a in range(n):
                outs[a][...] += parts[a]

    row = pl.BlockSpec((tsr, D_MODEL), lambda i: (i, 0))
    return _tied_call(
        body, after, name=name, grid=(s // tsr,),
        in_specs=[row] * (2 * n), out_specs=[pl.BlockSpec((D_MODEL, D_MODEL), lambda i: (0, 0))] * n,
        out_shape=[jax.ShapeDtypeStruct((D_MODEL, D_MODEL), F32)] * n,
        compiler_params=_params(("arbitrary",)),
    )(*[x for p in pairs for x in p])


def _ffn_bwd(dh3, wd, gu, wgu, h2, g, t, after=()):
    s = dh3.shape[0]

    def body(dh3_ref, wd_ref, gu_ref, w_ref, h2_ref, g_ref, dgu_ref, dh2_ref, dh2b_ref, dg_ref):
        @pl.when(pl.program_id(0) == 0)
        def _():
            dg_ref[...] = jnp.zeros_like(dg_ref)

        dh3v = dh3_ref[...]
        dhb = dh3v.astype(BF16)
        for cols in _FFN_CHUNKS:
            dact = _dot_nt(dhb, wd_ref[cols, :])
            gate, up = gu_ref[0, :, cols].astype(F32), gu_ref[1, :, cols].astype(F32)
            sg = _sigmoid(gate)
            dgu_ref[0, :, cols] = (dact * up * (sg * (1.0 + gate * (1.0 - sg)))).astype(BF16)
            dgu_ref[1, :, cols] = (dact * (gate * sg)).astype(BF16)
        dhn = _dot_nt(dgu_ref[0], w_ref[0]) + _dot_nt(dgu_ref[1], w_ref[1])
        h2 = h2_ref[...]
        dv, dg = _rms_bwd(dhn, h2, _rms_stats(h2), g_ref[...])
        dh2 = dh3v + dv
        dh2_ref[...] = dh2
        dh2b_ref[...] = dh2.astype(BF16)
        dg_ref[...] += dg

    row = pl.BlockSpec((t, D_MODEL), lambda i: (i, 0))
    wide = pl.BlockSpec((2, t, FFN_HIDDEN), lambda i: (0, i, 0))
    vec = pl.BlockSpec((1, D_MODEL), lambda i: (0, 0))
    return _tied_call(
        body, after, name="ffn_bwd", grid=(s // t,),
        in_specs=[row, pl.BlockSpec((FFN_HIDDEN, D_MODEL), lambda i: (0, 0)), wide,
                  pl.BlockSpec((2, D_MODEL, FFN_HIDDEN), lambda i: (0, 0, 0)), row, vec],
        out_specs=[wide, row, row, vec],
        out_shape=[jax.ShapeDtypeStruct((2, s, FFN_HIDDEN), BF16), jax.ShapeDtypeStruct((s, D_MODEL), F32),
                   jax.ShapeDtypeStruct((s, D_MODEL), BF16), jax.ShapeDtypeStruct((1, D_MODEL), F32)],
        compiler_params=_params(("arbitrary",)),
    )(dh3, wd, gu, wgu, h2, g)


def _attn_bwd(dh2, wo, q, kv, wq, h1, g, ts, after=()):
    s, m = q.shape[0], kv.shape[0]
    scale = XA_HEAD_DIM ** -0.5

    def body(dh2_ref, wo_ref, q_ref, kv_ref, wq_ref, h1_ref, g_ref, dh1_ref, dh1b_ref, dq_ref, dkv_ref, dg_ref):
        @pl.when(pl.program_id(0) == 0)
        def _():
            dkv_ref[...] = jnp.zeros_like(dkv_ref)
            dg_ref[...] = jnp.zeros_like(dg_ref)

        do = _dot_nt(dh2_ref[...].astype(BF16), wo_ref[...]).astype(BF16)
        for h in range(XA_HEADS):
            cols = slice(h * XA_HEAD_DIM, (h + 1) * XA_HEAD_DIM)
            vcols = slice(D_MODEL + h * XA_HEAD_DIM, D_MODEL + (h + 1) * XA_HEAD_DIM)
            qh, kh, vh, doh = q_ref[:, cols], kv_ref[:, cols], kv_ref[:, vcols], do[:, cols]
            p = _softmax_rows(_dot_nt(qh, kh) * scale)
            dp = _dot_nt(doh, vh)
            ds = (p * (dp - jnp.sum(dp * p, axis=-1, keepdims=True)) * scale).astype(BF16)
            dq_ref[:, cols] = _dot(ds, kh).astype(BF16)
            dkv_ref[:, cols] += _dot_tn(ds, qh)
            dkv_ref[:, vcols] += _dot_tn(p.astype(BF16), doh)
        dhn = _dot_nt(dq_ref[...], wq_ref[...])
        h1 = h1_ref[...]
        dv, dg = _rms_bwd(dhn, h1, _rms_stats(h1), g_ref[...])
        dh1 = dh2_ref[...] + dv
        dh1_ref[...] = dh1
        dh1b_ref[...] = dh1.astype(BF16)
        dg_ref[...] += dg

    row = pl.BlockSpec((ts, D_MODEL), lambda i: (i, 0))
    full = pl.BlockSpec((D_MODEL, D_MODEL), lambda i: (0, 0))
    kvs = pl.BlockSpec((m, 2 * D_MODEL), lambda i: (0, 0))
    vec = pl.BlockSpec((1, D_MODEL), lambda i: (0, 0))
    return _tied_call(
        body, after, name="attn_bwd", grid=(s // ts,),
        in_specs=[row, full, row, kvs, full, row, vec],
        out_specs=[row, row, row, kvs, vec],
        out_shape=[jax.ShapeDtypeStruct((s, D_MODEL), F32), jax.ShapeDtypeStruct((s, D_MODEL), BF16),
                   jax.ShapeDtypeStruct((s, D_MODEL), BF16),
                   jax.ShapeDtypeStruct((m, 2 * D_MODEL), F32), jax.ShapeDtypeStruct((1, D_MODEL), F32)],
        compiler_params=_params(("arbitrary",)),
    )(dh2, wo, q, kv, wq, h1, g)


def _mem_kv_bwd(dkv, mn, wkv, mem, g, after=()):
    m = mem.shape[0]

    def body(dkv_ref, mn_ref, w_ref, mem_ref, g_ref, dw_ref, dg_ref):
        dmn = jnp.zeros((m, D_MODEL), F32)
        mn = mn_ref[...]
        for j in range(4):
            dj = dkv_ref[:, j * 512:(j + 1) * 512].astype(BF16)
            dw_ref[j] = _dot_tn(mn, dj)
            dmn = dmn + _dot_nt(dj, w_ref[j])
        mv = mem_ref[...]
        dg_ref[...] = _rowsum(dmn * (mv * _rms_stats(mv)))

    return _tied_call(
        body, after, name="mem_kv_bwd", in_specs=[pl.BlockSpec(memory_space=pltpu.VMEM)] * 5,
        out_shape=[jax.ShapeDtypeStruct((4, D_MODEL, 512), F32), jax.ShapeDtypeStruct((1, D_MODEL), F32)],
        compiler_params=pltpu.CompilerParams(vmem_limit_bytes=VMEM_LIMIT_BYTES),
    )(dkv, mn, wkv, mem, g)


def _seqmix_bwd(dh1, x, z, c1, w_out, w_in, g_mix, cw, lng, lnb, gg, gb, wpair, wpair_t, bias, t, after=()):
    s = x.shape[0]
    nt = s // t

    def body(dh1_ref, x_ref, z_ref, c1_ref, wo_ref, wi_ref, gm_ref, cw_ref, lng_ref, lnb_ref,
             gg_ref, gb_ref, wpair_ref, wpt_ref, bias_ref,
             gx_ref, dz_ref, dcw_ref, dcb_ref, dlng_ref, dlnb_ref, dgg_ref, dgb_ref, dws_ref, dbs_ref,
             dbin_ref, dgm_ref, dbuf, dsh, mixed_ref, dv_ref):
        i = pl.program_id(0)
        accs = (dcw_ref, dcb_ref, dlng_ref, dlnb_ref, dgg_ref, dgb_ref, dws_ref, dbs_ref, dbin_ref, dgm_ref)

        @pl.when(i == 0)
        def _():
            for r in accs:
                r[...] = jnp.zeros_like(r)
            dbuf[t:t + CONV_HALO, :] = jnp.zeros((CONV_HALO, CONV_WIDTH), F32)

        @pl.when(i > 0)
        def _():
            dbuf[t:t + CONV_HALO, :] = dbuf[0:CONV_HALO, :]

        dmix = _dot_nt(dh1_ref[...].astype(BF16), wo_ref[...])

        xh, rs = _ln_stats(c1_ref[...])
        lng = lng_ref[...]
        ln = xh * lng + lnb_ref[...]
        sl = _sigmoid(ln)
        dln = dmix[:, 0:512] * (sl * (1.0 + ln * (1.0 - sl)))
        dc1, dg_ln, db_ln = _ln_bwd(dln, xh, rs, lng)
        dlng_ref[...] += dg_ln
        dlnb_ref[...] += db_ln
        dcb_ref[...] += _rowsum(dc1)
        dbuf[0:t, :] = dc1

        za = z_ref[:, 0:512]
        sg = _sigmoid(z_ref[:, 512:1024])
        a = za * sg
        _shift_rows(dbuf, dsh, t)

        da = jnp.zeros((t, CONV_WIDTH), F32)
        for k in range(CONV_KERNEL):
            later = _window(dbuf, dsh, CONV_KERNEL - 1 - k, t)
            da = da + cw_ref[k:k + 1, :] * later
            dcw_ref[k:k + 1, :] += _rowsum(a * later)
        dza = da * sg
        dzg = da * za * (sg * (1.0 - sg))
        dz_ref[:, 0:512] = dza.astype(BF16)
        dz_ref[:, 512:1024] = dzg.astype(BF16)
        dbin_ref[:, 0:512] += _rowsum(dza)
        dbin_ref[:, 512:1024] += _rowsum(dzg)

        dgm = dmix[:, 512:1024]
        u, du_dz = _gelu_parts(z_ref[:, 1024:1536])
        gv, dgv_dz = _gelu_parts(z_ref[:, 1536:2048])
        vxh, vrs = _ln_stats(gv)
        ggv = gg_ref[...]
        v = vxh * ggv + gb_ref[...]
        low = _lane_is_low_head()
        v_lo = jnp.where(low, v, 0.0).astype(BF16)
        v_hi = jnp.where(low, 0.0, v).astype(BF16)
        _gm_mix(v_lo, v_hi, wpair_ref, bias_ref, mixed_ref, t)
        dzu = dgm * mixed_ref[...] * du_dz
        dm = dgm * u
        dm_lo = jnp.where(low, dm, 0.0).astype(BF16)
        dm_hi = jnp.where(low, 0.0, dm).astype(BF16)
        vb = v.astype(BF16)
        tril = (lax.broadcasted_iota(jnp.int32, (CHUNK, CHUNK), 1)
                <= lax.broadcasted_iota(jnp.int32, (CHUNK, CHUNK), 0))
        for n in range(t // CHUNK):
            rows = slice(n * CHUNK, (n + 1) * CHUNK)
            dbs_ref[...] += dm[rows, :]
            for j in range(GM_HEADS // 2):
                cols = slice(j * LANES, (j + 1) * LANES)
                stack = jnp.concatenate([dm_lo[rows, cols], dm_hi[rows, cols]], axis=0)
                dws = _dot_nt(stack, vb[rows, cols])
                dws_ref[2 * j] += jnp.where(tril, dws[0:CHUNK], 0.0)
                dws_ref[2 * j + 1] += jnp.where(tril, dws[CHUNK:2 * CHUNK], 0.0)
                dv_ref[rows, cols] = _dot(wpt_ref[j], stack)
        dgv, dg_gm, db_gm = _ln_bwd(dv_ref[...], vxh, vrs, ggv)
        dgg_ref[...] += dg_gm
        dgb_ref[...] += db_gm
        dzv = dgv * dgv_dz
        dz_ref[:, 1024:1536] = dzu.astype(BF16)
        dz_ref[:, 1536:2048] = dzv.astype(BF16)
        dbin_ref[:, 1024:1536] += _rowsum(dzu)
        dbin_ref[:, 1536:2048] += _rowsum(dzv)

        dhn = jnp.zeros((t, D_MODEL), F32)
        for j in range(4):
            dhn = dhn + _dot_nt(dz_ref[:, j * 512:(j + 1) * 512], wi_ref[j])
        xv = x_ref[...]
        dv, dg = _rms_bwd(dhn, xv, _rms_stats(xv), gm_ref[...])
        gx_ref[...] = dh1_ref[...] + dv
        dgm_ref[...] += dg

    rev = lambda w: pl.BlockSpec((t, w), lambda i: (nt - 1 - i, 0))
    const = lambda *shape: pl.BlockSpec(shape, lambda i: (0,) * len(shape))
    f32 = lambda *shape: jax.ShapeDtypeStruct(shape, F32)
    return _tied_call(
        body, after, name="seqmix_bwd", grid=(nt,),
        in_specs=[rev(D_MODEL), rev(D_MODEL), rev(2048), rev(CONV_WIDTH),
                  const(D_MODEL, D_MODEL), const(4, D_MODEL, 512), const(1, D_MODEL),
                  const(CONV_HALO, CONV_WIDTH), const(1, 512), const(1, 512), const(1, 512), const(1, 512),
                  const(4, CHUNK, 2 * CHUNK), const(4, CHUNK, 2 * CHUNK), const(CHUNK, GM_WIDTH)],
        out_specs=[rev(D_MODEL), rev(2048),
                   const(CONV_HALO, CONV_WIDTH), const(1, 512), const(1, 512), const(1, 512), const(1, 512),
                   const(1, 512), const(GM_HEADS, CHUNK, CHUNK), const(CHUNK, GM_WIDTH), const(1, 2048),
                   const(1, D_MODEL)],
        out_shape=[f32(s, D_MODEL), jax.ShapeDtypeStruct((s, 2048), BF16),
                   f32(CONV_HALO, CONV_WIDTH), f32(1, 512), f32(1, 512), f32(1, 512), f32(1, 512),
                   f32(1, 512), f32(GM_HEADS, CHUNK, CHUNK), f32(CHUNK, GM_WIDTH), f32(1, 2048),
                   f32(1, D_MODEL)],
        scratch_shapes=[pltpu.VMEM((t + CONV_HALO, CONV_WIDTH), F32),
                        pltpu.VMEM((SUBLANES - 1, t + CONV_HALO - SUBLANES, CONV_WIDTH), F32),
                        pltpu.VMEM((t, GM_WIDTH), F32), pltpu.VMEM((t, GM_WIDTH), F32)],
        compiler_params=_params(("arbitrary",)),
    )(dh1, x, z, c1, w_out, w_in, g_mix, cw, lng, lnb, gg, gb, wpair, wpair_t, bias)


def _head_bias_grad(dbs):
    def body(d_ref, o_ref):
        dv = d_ref[...]
        lane = lax.broadcasted_iota(jnp.int32, (CHUNK, LANES), 1)
        acc = jnp.zeros((CHUNK, LANES), F32)
        for h in range(GM_HEADS):
            sh = jnp.sum(dv[:, h * GM_HEAD_DIM:(h + 1) * GM_HEAD_DIM], axis=-1, keepdims=True)
            acc = acc + jnp.where(lane == h, sh, 0.0)
        o_ref[...] = acc

    return pl.pallas_call(body, name="head_bias_grad",
                          out_shape=jax.ShapeDtypeStruct((CHUNK, LANES), F32))(dbs)


def kernel(x, mem, norm_mix_g, w_in, b_in, conv_w, conv_b, conv_ln_g, conv_ln_b, gm_ln_g, gm_ln_b, gm_w_s, gm_b_s, w_out, norm_xa_g, mem_norm_g, xa_wq, xa_wkv, xa_wo, norm_ffn_g, ffn_w_gate_up, ffn_w_down, final_norm_g, loss_target, m_norm_mix_g, m_w_in, m_b_in, m_conv_w, m_conv_b, m_conv_ln_g, m_conv_ln_b, m_gm_ln_g, m_gm_ln_b, m_gm_w_s, m_gm_b_s, m_w_out, m_norm_xa_g, m_mem_norm_g, m_xa_wq, m_xa_wkv, m_xa_wo, m_norm_ffn_g, m_ffn_w_gate_up, m_ffn_w_down, m_final_norm_g, v_norm_mix_g, v_w_in, v_b_in, v_conv_w, v_conv_b, v_conv_ln_g, v_conv_ln_b, v_gm_ln_g, v_gm_ln_b, v_gm_w_s, v_gm_b_s, v_w_out, v_norm_xa_g, v_mem_norm_g, v_xa_wq, v_xa_wkv, v_xa_wo, v_norm_ffn_g, v_ffn_w_gate_up, v_ffn_w_down, v_final_norm_g):
    weights = dict(norm_mix_g=norm_mix_g, w_in=w_in, b_in=b_in, conv_w=conv_w, conv_b=conv_b, conv_ln_g=conv_ln_g,
                   conv_ln_b=conv_ln_b, gm_ln_g=gm_ln_g, gm_ln_b=gm_ln_b, gm_w_s=gm_w_s, gm_b_s=gm_b_s, w_out=w_out,
                   norm_xa_g=norm_xa_g, mem_norm_g=mem_norm_g, xa_wq=xa_wq, xa_wkv=xa_wkv, xa_wo=xa_wo,
                   norm_ffn_g=norm_ffn_g, ffn_w_gate_up=ffn_w_gate_up, ffn_w_down=ffn_w_down,
                   final_norm_g=final_norm_g)
    m_in = dict(norm_mix_g=m_norm_mix_g, w_in=m_w_in, b_in=m_b_in, conv_w=m_conv_w, conv_b=m_conv_b,
                conv_ln_g=m_conv_ln_g, conv_ln_b=m_conv_ln_b, gm_ln_g=m_gm_ln_g, gm_ln_b=m_gm_ln_b, gm_w_s=m_gm_w_s,
                gm_b_s=m_gm_b_s, w_out=m_w_out, norm_xa_g=m_norm_xa_g, mem_norm_g=m_mem_norm_g, xa_wq=m_xa_wq,
                xa_wkv=m_xa_wkv, xa_wo=m_xa_wo, norm_ffn_g=m_norm_ffn_g, ffn_w_gate_up=m_ffn_w_gate_up,
                ffn_w_down=m_ffn_w_down, final_norm_g=m_final_norm_g)
    v_in = dict(norm_mix_g=v_norm_mix_g, w_in=v_w_in, b_in=v_b_in, conv_w=v_conv_w, conv_b=v_conv_b,
                conv_ln_g=v_conv_ln_g, conv_ln_b=v_conv_ln_b, gm_ln_g=v_gm_ln_g, gm_ln_b=v_gm_ln_b, gm_w_s=v_gm_w_s,
                gm_b_s=v_gm_b_s, w_out=v_w_out, norm_xa_g=v_norm_xa_g, mem_norm_g=v_mem_norm_g, xa_wq=v_xa_wq,
                xa_wkv=v_xa_wkv, xa_wo=v_xa_wo, norm_ffn_g=v_norm_ffn_g, ffn_w_gate_up=v_ffn_w_gate_up,
                ffn_w_down=v_ffn_w_down, final_norm_g=v_final_norm_g)
    grads, delta, new_m, new_v = {}, {}, {}, {}

    s = x.shape[1]
    ts = _row_tile(s)
    tb = max(CHUNK, ts // 2)
    tw = 2 * ts if s % (2 * ts) == 0 and ts >= 512 else ts
    cx, cy, cc = _mesh_pos()
    chip = 2 * cx + cy
    pos = jnp.stack([chip, cc]).astype(jnp.int32)
    row = lambda a: a.reshape(1, -1)
    x2, mem2, tgt2 = x[0], mem[0], loss_target[0]

    big = dict(w_in=w_in, xa_wkv=xa_wkv, w_out=w_out, xa_wq=xa_wq, xa_wo=xa_wo,
               ffn_w_gate_up=ffn_w_gate_up, ffn_w_down=ffn_w_down)
    big_names = list(big)
    halves = lambda a: a.reshape(2, a.shape[0] // 2, a.shape[1])
    conv_w_pad = jnp.pad(conv_w, ((0, CONV_HALO - CONV_KERNEL), (0, 0)))
    first_names = ["w_in", "conv_w"]
    later_names = [nm for nm in big_names if nm != "w_in"]
    cast = dict(zip(first_names, _cast_into_slots([halves(w_in), halves(conv_w_pad)], pos, [BF16, F32], "cast_w_in")))
    cast.update(zip(later_names, _cast_into_slots([halves(big[nm]) for nm in later_names], pos,
                                                  [BF16] * len(later_names), "cast_" + later_names[0],
                                                  side_by_side=(later_names.index("ffn_w_gate_up"),))))

    def start_gather(names, after):
        return _gather_start([cast[nm] for nm in names], "gather_start_" + names[0], after)

    def land_gather(names, started, after):
        send_sems, recv_sems, bufs, _ = started
        return _gather_wait(send_sems, recv_sems, bufs, after, "gather_wait_" + names[0])

    def share_gather(names, landed, after=()):
        return dict(zip(names, (b.reshape(b.shape[0], -1, b.shape[-1])
                                for b in _pass_to_sibling(landed, "pass_" + names[0], after))))

    tril = jnp.tril(jnp.ones((CHUNK, CHUNK), dtype=bool))
    ws = jnp.where(tril[None], gm_w_s, 0.0)
    wpair = jnp.concatenate([ws[0::2], ws[1::2]], axis=2).astype(BF16)
    ws_t = jnp.swapaxes(ws, 1, 2)
    wpair_t = jnp.concatenate([ws_t[0::2], ws_t[1::2]], axis=2).astype(BF16)
    bias = jnp.repeat(gm_b_s.T, GM_HEAD_DIM, axis=1)

    attn_names = ["w_out", "xa_wq", "xa_wkv", "xa_wo"]
    gather_first = start_gather(first_names, ())
    hn1 = _norm_in(x2, row(norm_mix_g), tw, after=(gather_first[3], wpair, wpair_t, bias))
    landed = land_gather(first_names, gather_first, [cast[nm] for nm in later_names] + [hn1])
    gather_attn = start_gather(attn_names, landed)
    gw = share_gather(first_names, landed, gather_attn[3])
    w_in_g = gw["w_in"]
    cw_g = jnp.concatenate([gw["conv_w"][k] for k in range(N_CHIPS)], axis=1)

    z, mix, c1 = _seqmix_fwd(hn1, w_in_g, row(b_in), cw_g, row(conv_b), row(conv_ln_g), row(conv_ln_b),
                             row(gm_ln_g), row(gm_ln_b), wpair, bias, ts)
    landed = land_gather(attn_names, gather_attn, mix)
    gather_gu = start_gather(["ffn_w_gate_up"], landed)
    gw = share_gather(attn_names, landed, gather_gu[3])
    w_out_g = gw["w_out"].reshape(D_MODEL, D_MODEL)
    wq_g = gw["xa_wq"].reshape(D_MODEL, D_MODEL)
    wkv_g = gw["xa_wkv"]
    wo_g = gw["xa_wo"].reshape(D_MODEL, D_MODEL)
    mn, kv = _mem_kv(mem2, row(mem_norm_g), wkv_g)
    h1, hn2, q, o, h2, hn3 = _attn_block_fwd(x2, mix, w_out_g, row(norm_xa_g), wq_g, kv, wo_g, row(norm_ffn_g), ts)
    landed = land_gather(["ffn_w_gate_up"], gather_gu, hn3)
    gather_down = start_gather(["ffn_w_down"], landed)
    wgu_g = share_gather(["ffn_w_gate_up"], landed, gather_down[3])["ffn_w_gate_up"]
    gu, act = _ffn_up(hn3, wgu_g, ts)
    landed = land_gather(["ffn_w_down"], gather_down, act)
    wd_g = share_gather(["ffn_w_down"], landed)["ffn_w_down"].reshape(FFN_HIDDEN, D_MODEL)
    dh3, dh3_b, sq, d_final_g = _ffn_down_loss(act, wd_g, h2, row(final_norm_g), tgt2, ts)
    loss_here = jnp.broadcast_to(0.5 * jnp.sum(sq) / D_MODEL, (1, 2, SUBLANES, LANES))

    def split(g, nm):
        r, c = big[nm].shape
        return g.reshape(N_CHIPS, 2, r // 2, c)

    def chip_sums(group, arrays, got):
        sums, parts = [None] * len(group), [None] * len(group)
        for blocks in (N_CHIPS, 1):
            idx = [i for i, a in enumerate(arrays) if a.shape[0] == blocks]
            if idx:
                out = _add_halves([arrays[i] for i in idx], [got[i] for i in idx], pos, "chip_sum_" + group[idx[0]],
                                  [F32 if group[i] == "loss" else BF16 for i in idx])
                for k, i in enumerate(idx):
                    sums[i], parts[i] = out[0][k], out[1][k]
        return sums, parts

    def start_swap(group, grads):
        return _swap_start([split(g, nm) for g, nm in zip(grads, group)], "swap_start_" + group[0])

    def start_exchange(group, swapping, after, landed):
        sems, arrays, lands, _ = swapping
        arrays, got = _swap_wait(sems, arrays, lands, after, "swap_wait_" + group[0])
        sums, parts = chip_sums(group, arrays, got)
        return _exchange_start(sums, parts, "exchange_start_" + group[0], landed)

    def wait_exchange(group, started, after):
        sems, sums, parts, _ = started
        return _exchange_wait(sems, sums, parts, after, "exchange_wait_" + group[0])

    def finish_exchange(group, started, after):
        return _sum_chips(wait_exchange(group, started, after), pos, "total_" + group[0])

    def join_and_update(group, after):
        joined = _join_halves([halves_of[nm] for nm in group], "join_halves_" + group[0], after)
        outs = _adamw([(weights[nm], j.reshape(big[nm].shape), m_in[nm], v_in[nm]) for nm, j in zip(group, joined)],
                      "adamw_" + group[0])
        for nm, out in zip(group, outs):
            grads[nm], delta[nm], new_m[nm], new_v[nm] = out
        return [new_v[nm] for nm in group]

    as3 = lambda a: a.reshape((1,) + a.shape)
    halves_of = {}

    g_down = _grad_w(act, as3(dh3_b), FFN_HALF, D_MODEL, "grad_ffn_w_down")
    group_a = ["ffn_w_down"]
    swap_a = start_swap(group_a, [g_down])
    dgu, dh2, dh2_b, d_ffn_g = _ffn_bwd(dh3, wd_g, gu, wgu_g, h2, row(norm_ffn_g), tb,
                                        after=swap_a[3])
    exch_a = start_exchange(group_a, swap_a, dh2, wd_g)
    g_gu = _grad_w(hn3, dgu, D_MODEL, FFN_HALF, "grad_ffn_w_gate_up", after=exch_a[3])
    halves_of.update(zip(group_a, finish_exchange(group_a, exch_a, g_gu)))

    group_b = ["ffn_w_gate_up"]
    swap_b = start_swap(group_b, [g_gu])
    dh1, dh1_b, dq, dkv, d_xa_g = _attn_bwd(dh2, wo_g, q, kv, wq_g, h1, row(norm_xa_g), ts, after=swap_b[3])
    exch_b = start_exchange(group_b, swap_b, dh1, [halves_of[nm] for nm in group_a])
    g_wkv, d_mem_g = _mem_kv_bwd(dkv, mn, wkv_g, mem2, row(mem_norm_g), after=exch_b[3])
    g_wo, g_wq, g_wout = _grad_w_square([(o, dh2_b), (hn2, dq), (mix, dh1_b)], "grad_xa_wo", after=exch_b[3])
    done_a = join_and_update(group_a, (g_wkv, g_wo, g_wq, g_wout))
    halves_of.update(zip(group_b, finish_exchange(group_b, exch_b, done_a)))

    group_c = ["xa_wo", "xa_wq", "xa_wkv", "w_out"]
    swap_c = start_swap(group_c, [g_wo, g_wq, g_wkv, g_wout])
    (gx, dz, d_cw, d_cb, d_lng, d_lnb, d_gg, d_gb, d_ws, d_bs_sum, d_bin, d_mix_g) = _seqmix_bwd(
        dh1, x2, z, c1, w_out_g, w_in_g, row(norm_mix_g), cw_g, row(conv_ln_g), row(conv_ln_b),
        row(gm_ln_g), row(gm_ln_b), wpair, wpair_t, bias, tb, after=swap_c[3])
    d_bs = _head_bias_grad(d_bs_sum)[:, :GM_HEADS].T
    exch_c = start_exchange(group_c, swap_c, dz, [halves_of[nm] for nm in group_b])
    g_win = _grad_w(hn1, as3(dz), D_MODEL, 1024, "grad_w_in", after=exch_c[3], shards=2)
    done_b = join_and_update(group_b, g_win)
    parts_c = wait_exchange(group_c, exch_c, (g_win, *done_b))

    small_names = ["norm_mix_g", "b_in", "conv_w", "conv_b", "conv_ln_g", "conv_ln_b", "gm_ln_g", "gm_ln_b",
                   "gm_w_s", "gm_b_s", "norm_xa_g", "mem_norm_g", "norm_ffn_g", "final_norm_g"]
    d_cw_by_chip = jnp.swapaxes(d_cw.reshape(CONV_HALO, N_CHIPS, LANES), 0, 1).reshape(-1, LANES)
    small_grads = dict(norm_mix_g=d_mix_g, b_in=d_bin, conv_w=d_cw_by_chip, conv_b=d_cb, conv_ln_g=d_lng,
                       conv_ln_b=d_lnb, gm_ln_g=d_gg, gm_ln_b=d_gb, gm_w_s=d_ws, gm_b_s=d_bs, norm_xa_g=d_xa_g,
                       mem_norm_g=d_mem_g, norm_ffn_g=d_ffn_g, final_norm_g=d_final_g)

    def rows_form(a):
        a = a.reshape(-1, LANES)
        return jnp.pad(a, ((0, -a.shape[0] % SUBLANES), (0, 0)))

    pieces = [rows_form(small_grads[nm]) for nm in small_names]
    offsets, total = [], 0
    for p in pieces:
        offsets.append(total)
        total += p.shape[0]
    pack_rows = -(-total // 32) * 32
    small_pack = jnp.pad(jnp.concatenate(pieces, axis=0), ((0, pack_rows - total), (0, 0)))

    group_d = ["w_in", "small", "loss"]
    arrays_d = [split(g_win, "w_in"), small_pack.reshape(1, 2, pack_rows // 2, LANES), loss_here]
    sums_d, parts_d = chip_sums(group_d, arrays_d, _swap_halves(arrays_d, "swap_halves_w_in"))
    exch_d = _exchange_start(sums_d, parts_d, "exchange_start_w_in", parts_c)
    halves_of.update(zip(group_c, _sum_chips(parts_c, pos, "total_xa_wo", exch_d[3])))
    done_c = join_and_update(group_c, exch_d[3])
    halves_of.update(zip(group_d, finish_exchange(group_d, exch_d, done_c)))
    joined_d = _join_halves([halves_of[nm] for nm in group_d], "join_halves_w_in")
    loss = joined_d[2][0, 0, 0]
    grads["w_in"], delta["w_in"], new_m["w_in"], new_v["w_in"] = _adamw(
        [(w_in, joined_d[0].reshape(w_in.shape), m_w_in, v_w_in)], "adamw_w_in")[0]

    local_rows = lambda a, nm: a if nm == "conv_w" else a.reshape(-1, LANES)
    params = [tuple(local_rows(src[nm], nm) for src in (weights, m_in, v_in)) for nm in small_names]
    outs = _adamw_small(joined_d[1].reshape(pack_rows, LANES), pos, params, offsets, small_names.index("conv_w"))
    for k, nm in enumerate(small_names):
        for dst, a in zip((grads, delta, new_m, new_v), outs[4 * k:4 * k + 4]):
            dst[nm] = a

    order = ["norm_mix_g", "w_in", "b_in", "conv_w", "conv_b", "conv_ln_g", "conv_ln_b", "gm_ln_g", "gm_ln_b",
             "gm_w_s", "gm_b_s", "w_out", "norm_xa_g", "mem_norm_g", "xa_wq", "xa_wkv", "xa_wo", "norm_ffn_g",
             "ffn_w_gate_up", "ffn_w_down", "final_norm_g"]
    fit = lambda a, nm: a.reshape(weights[nm].shape)
    return (loss, gx.reshape(x.shape),
            *[fit(grads[nm], nm) for nm in order], *[fit(delta[nm], nm) for nm in order],
            *[fit(new_m[nm], nm) for nm in order], *[fit(new_v[nm], nm) for nm in order])
```

```python
import functools

import jax
import jax.numpy as jnp
from jax import lax
from jax.experimental import pallas as pl
from jax.experimental.pallas import tpu as pltpu

F32 = jnp.float32
BF16 = jnp.bfloat16

D_MODEL = 1024
CONV_WIDTH = 512
GM_WIDTH = 512
CONV_KERNEL = 31
CONV_HALO = 32
GRAD_ROWS = 2048
CHUNK = 128
GM_HEADS = 8
GM_HEAD_DIM = 64
XA_HEADS = 4
XA_HEAD_DIM = 256
FFN_HIDDEN = 2816
FFN_HALF = FFN_HIDDEN // 2
RMS_EPS = 1e-6
LN_EPS = 1e-5
N_CHIPS = 4
LANES = 128
SUBLANES = 8

ADAM_LR = 0.001
ADAM_B1 = 0.9
ADAM_B2 = 0.999
ADAM_EPS = 1e-08
ADAM_WD = 0.01
ADAM_STEP = 10

VMEM_LIMIT_BYTES = 56 * 1024 * 1024
MESH = pl.DeviceIdType.MESH
ANY = pl.BlockSpec(memory_space=pl.ANY)
HBM_SPEC = pl.BlockSpec(memory_space=pltpu.HBM)
SEM_SPEC = pl.BlockSpec(memory_space=pltpu.SEMAPHORE)

_NT = (((1,), (1,)), ((), ()))
_TN = (((0,), (0,)), ((), ()))
_GELU_C = 0.7978845608028654
_GELU_A = 0.044715


def _dot(a, b):
    return jnp.dot(a, b, preferred_element_type=F32)


def _dot_nt(a, b):
    return lax.dot_general(a, b, _NT, preferred_element_type=F32)


def _dot_tn(a, b):
    return lax.dot_general(a, b, _TN, preferred_element_type=F32)


def _mean(v):
    return jnp.mean(v, axis=-1, keepdims=True)


def _rowsum(v):
    return jnp.sum(v, axis=0, keepdims=True)


def _sigmoid(v):
    return 1.0 / (1.0 + jnp.exp(-v))


def _gelu_parts(v):
    v2 = v * v
    t = jnp.tanh(_GELU_C * (v + _GELU_A * v * v2))
    g = 0.5 * v * (1.0 + t)
    dg = 0.5 * (1.0 + t) + 0.5 * v * (1.0 - t * t) * (_GELU_C * (1.0 + 3.0 * _GELU_A * v2))
    return g, dg


def _rms_stats(v):
    return lax.rsqrt(_mean(v * v) + RMS_EPS)


def _rms_bwd(dy, v, r, g):
    n = v * r
    dn = dy * g
    dv = r * (dn - n * _mean(dn * n))
    return dv, _rowsum(dy * n)


def _ln_stats(v):
    mu = _mean(v)
    xc = v - mu
    rs = lax.rsqrt(_mean(xc * xc) + LN_EPS)
    return xc * rs, rs


def _ln_bwd(dy, xh, rs, g):
    dxh = dy * g
    dv = rs * (dxh - _mean(dxh) - xh * _mean(dxh * xh))
    return dv, _rowsum(dy * xh), _rowsum(dy)


def _params(sem):
    return pltpu.CompilerParams(dimension_semantics=sem, vmem_limit_bytes=VMEM_LIMIT_BYTES)


def _row_tile(s):
    return 512 if s % 512 == 0 and s >= 2048 else 128


def _mesh_pos():
    return lax.axis_index("x"), lax.axis_index("y"), lax.axis_index("c")


def _slot(buf, chip_idx, half):
    if buf.shape[0] == N_CHIPS:
        return buf.at[chip_idx, half]
    width = buf.shape[-1] // 2
    return buf.at[chip_idx // 2, half, :, pl.ds(pl.multiple_of((chip_idx % 2) * width, LANES), width)]


def _cast_into_slots(ws, pos, dtypes, name, side_by_side=()):
    n = len(ws)

    def body(pos_ref, *refs):
        for a in range(n):
            refs[n + a][0] = refs[a][...].astype(dtypes[a])

    def out_spec(a, w):
        if a in side_by_side:
            return pl.BlockSpec((1, 1) + w.shape[1:], lambda i, p: (p[0] // 2, i, 0, p[0] % 2))
        return pl.BlockSpec((1, 1) + w.shape[1:], lambda i, p: (p[0], i, 0, 0))

    def out_shape(a, w):
        if a in side_by_side:
            return (2, 2, w.shape[1], 2 * w.shape[2])
        return (N_CHIPS,) + w.shape

    return pl.pallas_call(
        body, name=name,
        grid_spec=pltpu.PrefetchScalarGridSpec(
            num_scalar_prefetch=1, grid=(2,),
            in_specs=[pl.BlockSpec((1,) + w.shape[1:], lambda i, p: (i, 0, 0)) for w in ws],
            out_specs=[out_spec(a, w) for a, w in enumerate(ws)]),
        out_shape=[jax.ShapeDtypeStruct(out_shape(a, w), dt) for a, (w, dt) in enumerate(zip(ws, dtypes))],
        compiler_params=_params(("parallel",)),
    )(pos, *ws)


def _adam_update(w, g, m, v):
    nm = ADAM_B1 * m + (1.0 - ADAM_B1) * g
    nv = ADAM_B2 * v + (1.0 - ADAM_B2) * (g * g)
    m_hat = nm / (1.0 - ADAM_B1 ** ADAM_STEP)
    v_hat = nv / (1.0 - ADAM_B2 ** ADAM_STEP)
    return -ADAM_LR * (m_hat / (jnp.sqrt(v_hat) + ADAM_EPS) + ADAM_WD * w), nm, nv


ADAM_STEPS = 4


def _adamw(quads, name, after=()):
    n = len(quads)

    def body(*refs):
        ins, outs = refs[:4 * n], refs[4 * n:]
        for a in range(n):
            w, g, m, v = (r[...] for r in ins[4 * a:4 * a + 4])
            outs[4 * a][...] = g
            outs[4 * a + 1][...], outs[4 * a + 2][...], outs[4 * a + 3][...] = _adam_update(w, g, m, v)

    specs = [pl.BlockSpec((q[0].shape[0] // ADAM_STEPS, q[0].shape[1]), lambda i: (i, 0)) for q in quads]
    out = _tied_call(
        body, after, name=name, grid=(ADAM_STEPS,),
        in_specs=[sp for sp in specs for _ in range(4)], out_specs=[sp for sp in specs for _ in range(4)],
        out_shape=[jax.ShapeDtypeStruct(q[0].shape, F32) for q in quads for _ in range(4)],
        compiler_params=_params(("parallel",)),
    )(*[a for q in quads for a in q])
    return [tuple(out[4 * a:4 * a + 4]) for a in range(n)]


def _adamw_small(gpack, pos, params, offsets, conv_at):
    n = len(params)

    def body(pos_ref, g_ref, *refs):
        ins, outs = refs[:3 * n], refs[3 * n:]
        for k in range(n):
            rows = params[k][0].shape[0]
            start = offsets[k]
            if k == conv_at:
                start = pl.multiple_of(start + pos_ref[0] * CONV_HALO, SUBLANES)
            g = g_ref[pl.ds(start, rows), :]
            outs[4 * k][...] = g
            outs[4 * k + 1][...], outs[4 * k + 2][...], outs[4 * k + 3][...] = _adam_update(
                ins[3 * k][...], g, ins[3 * k + 1][...], ins[3 * k + 2][...])

    flat = [a for p in params for a in p]
    vmem = pl.BlockSpec(memory_space=pltpu.VMEM)
    return pl.pallas_call(
        body, name="adamw_small",
        in_specs=[pl.BlockSpec(memory_space=pltpu.SMEM), vmem] + [vmem] * len(flat),
        out_specs=[vmem] * (4 * n),
        out_shape=[jax.ShapeDtypeStruct(p[0].shape, F32) for p in params for _ in range(4)],
    )(pos, gpack, *flat)


def _as_tuple(after):
    return tuple(after) if isinstance(after, (tuple, list)) else (after,)


def _tied_call(body, after, *, in_specs, **kwargs):
    after = _as_tuple(after)
    n_in, n_after = len(in_specs), len(after)

    def tied(*refs):
        body(*refs[:n_in], *refs[n_in + n_after:])

    call = pl.pallas_call(tied, in_specs=list(in_specs) + [ANY] * n_after, **kwargs)
    return lambda *operands: call(*operands, *after)


def _other_chips(x, y):
    return [(1 - x, y), (x, 1 - y), (1 - x, 1 - y)]


def _gather_descriptors(bufs, send_of, recv_of):
    x, y, c = _mesh_pos()
    me = 2 * x + y
    chips = _other_chips(x, y)
    sends, arrivals = [], []
    for a in range(len(bufs)):
        for k in range(3):
            ck = 2 * chips[k][0] + chips[k][1]

            def copy(slot, a=a, k=k):
                return pltpu.make_async_remote_copy(
                    src_ref=_slot(bufs[a], slot, c), dst_ref=_slot(bufs[a], slot, c),
                    send_sem=send_of(a, k), recv_sem=recv_of(a, k),
                    device_id=(*chips[k], c), device_id_type=MESH)

            sends.append(functools.partial(copy, me))
            arrivals.append(functools.partial(copy, ck))
    return sends, arrivals


def _gather_start(bufs, name, after=()):
    n = len(bufs)
    ns = 3 * n

    def body(*refs):
        sems = refs[n:n + 2 * ns]
        thru = refs[n + 2 * ns:2 * n + 2 * ns]
        token = refs[2 * n + 2 * ns]
        _chips_handshake()
        sends, _ = _gather_descriptors(thru, lambda a, k: sems[3 * a + k], lambda a, k: sems[ns + 3 * a + k])
        for cp in sends:
            cp().start()
        token[...] = jnp.zeros_like(token)

    held = [pltpu.with_memory_space_constraint(b, pltpu.HBM) for b in bufs]
    out = _tied_call(
        body, after, name=name,
        out_shape=(*[pltpu.SemaphoreType.DMA(())] * (2 * ns), *[pltpu.HBM(b.shape, b.dtype) for b in held],
                   jax.ShapeDtypeStruct((8, LANES), F32)),
        in_specs=[HBM_SPEC] * n,
        out_specs=(*[SEM_SPEC] * (2 * ns), *[HBM_SPEC] * n, pl.BlockSpec(memory_space=pltpu.VMEM)),
        input_output_aliases={i: 2 * ns + i for i in range(n)},
        compiler_params=pltpu.CompilerParams(has_side_effects=pltpu.SideEffectType.DATAFLOW_SIDE_EFFECTING,
                                             collective_id=CHIPS_COLLECTIVE_ID),
    )(*held)
    return list(out[:ns]), list(out[ns:2 * ns]), list(out[2 * ns:2 * ns + n]), out[2 * ns + n]


def _gather_wait(send_sems, recv_sems, bufs, after, name):
    n = len(bufs)
    ns = 3 * n

    def body(*refs):
        buf_ref = refs[:n]
        sem_ref = refs[n:n + 2 * ns]
        sends, arrivals = _gather_descriptors(buf_ref, lambda a, k: sem_ref[3 * a + k],
                                              lambda a, k: sem_ref[ns + 3 * a + k])
        for cp in sends:
            cp().wait_send()
        for cp in arrivals:
            cp().wait_recv()

    out = pl.pallas_call(
        body, name=name,
        out_shape=tuple(pltpu.HBM(b.shape, b.dtype) for b in bufs),
        in_specs=[HBM_SPEC] * n + [SEM_SPEC] * (2 * ns) + [ANY] * len(_as_tuple(after)),
        out_specs=tuple([HBM_SPEC] * n),
        input_output_aliases={i: i for i in range(n)},
        compiler_params=pltpu.CompilerParams(has_side_effects=pltpu.SideEffectType.DATAFLOW_SIDE_EFFECTING),
    )(*bufs, *send_sems, *recv_sems, *_as_tuple(after))
    return list(out)


SIBLING_COLLECTIVE_ID = 0


def _sibling_handshake():
    x, y, c = _mesh_pos()
    barrier = pltpu.get_barrier_semaphore()
    pl.semaphore_signal(barrier, inc=1, device_id=(x, y, 1 - c), device_id_type=MESH)
    pl.semaphore_wait(barrier, 1)


CHIPS_COLLECTIVE_ID = 1


def _chips_handshake():
    x, y, c = _mesh_pos()
    barrier = pltpu.get_barrier_semaphore()
    for chip in _other_chips(x, y):
        pl.semaphore_signal(barrier, inc=1, device_id=(*chip, c), device_id_type=MESH)
    pl.semaphore_wait(barrier, 3)


def _pass_to_sibling(bufs, name, after=()):
    n = len(bufs)

    def body(*refs):
        outs = refs[n:2 * n]
        send_sem, recv_sem = refs[2 * n:]
        x, y, c = _mesh_pos()
        chips = _other_chips(x, y)
        _sibling_handshake()

        def half(a, k, which):
            ck = 2 * chips[k][0] + chips[k][1]
            return pltpu.make_async_remote_copy(
                src_ref=_slot(outs[a], ck, which), dst_ref=_slot(outs[a], ck, which),
                send_sem=send_sem.at[a, k], recv_sem=recv_sem.at[a, k],
                device_id=(x, y, 1 - c), device_id_type=MESH)

        sends = [half(a, k, c) for a in range(n) for k in range(3)]
        for cp in sends:
            cp.start()
        for a in range(n):
            for k in range(3):
                half(a, k, 1 - c).wait_recv()
        for cp in sends:
            cp.wait_send()

    return _tied_call(
        body, after, name=name,
        in_specs=[ANY] * n, out_specs=[ANY] * n,
        out_shape=[jax.ShapeDtypeStruct(b.shape, b.dtype) for b in bufs],
        input_output_aliases={a: a for a in range(n)},
        scratch_shapes=[pltpu.SemaphoreType.DMA((n, 3))] * 2,
        compiler_params=pltpu.CompilerParams(collective_id=SIBLING_COLLECTIVE_ID),
    )(*bufs)


def _swap_halves(grads, name):
    n = len(grads)

    def body(*refs):
        ins, outs = refs[:n], refs[n:2 * n]
        send_sem, recv_sem = refs[2 * n:]
        x, y, c = _mesh_pos()
        _sibling_handshake()
        cps = [pltpu.make_async_remote_copy(
            src_ref=ins[a].at[:, pl.ds(1 - c, 1)], dst_ref=outs[a],
            send_sem=send_sem.at[a], recv_sem=recv_sem.at[a],
            device_id=(x, y, 1 - c), device_id_type=MESH) for a in range(n)]
        for cp in cps:
            cp.start()
        for cp in cps:
            cp.wait()

    out_shape = [jax.ShapeDtypeStruct((g.shape[0], 1) + g.shape[2:], g.dtype) for g in grads]
    return pl.pallas_call(
        body, name=name,
        in_specs=[ANY] * n, out_specs=[ANY] * n, out_shape=out_shape,
        scratch_shapes=[pltpu.SemaphoreType.DMA((n,))] * 2,
        compiler_params=pltpu.CompilerParams(collective_id=SIBLING_COLLECTIVE_ID),
    )(*grads)


def _swap_descriptors(grads, lands, send_of, recv_of):
    x, y, c = _mesh_pos()
    return [functools.partial(
        pltpu.make_async_remote_copy,
        src_ref=grads[a].at[:, pl.ds(1 - c, 1)], dst_ref=lands[a],
        send_sem=send_of(a), recv_sem=recv_of(a),
        device_id=(x, y, 1 - c), device_id_type=MESH) for a in range(len(grads))]


def _swap_start(grads, name):
    n = len(grads)

    def body(*refs):
        sems = refs[2 * n:4 * n]
        g_thru, l_thru = refs[4 * n:5 * n], refs[5 * n:6 * n]
        token = refs[6 * n]
        _sibling_handshake()
        for cp in _swap_descriptors(g_thru, l_thru, lambda a: sems[a], lambda a: sems[n + a]):
            cp().start()
        token[...] = jnp.zeros_like(token)

    lands = [lax.empty((g.shape[0], 1) + g.shape[2:], g.dtype) for g in grads]
    held = [pltpu.with_memory_space_constraint(a, pltpu.HBM) for a in (*grads, *lands)]
    out = pl.pallas_call(
        body, name=name,
        out_shape=(*[pltpu.SemaphoreType.DMA(())] * (2 * n), *[pltpu.HBM(a.shape, a.dtype) for a in held],
                   jax.ShapeDtypeStruct((8, LANES), F32)),
        in_specs=[HBM_SPEC] * (2 * n),
        out_specs=(*[SEM_SPEC] * (2 * n), *[HBM_SPEC] * (2 * n), pl.BlockSpec(memory_space=pltpu.VMEM)),
        input_output_aliases={i: 2 * n + i for i in range(2 * n)},
        compiler_params=pltpu.CompilerParams(has_side_effects=pltpu.SideEffectType.DATAFLOW_SIDE_EFFECTING,
                                             collective_id=SIBLING_COLLECTIVE_ID),
    )(*held)
    return list(out[:2 * n]), list(out[2 * n:3 * n]), list(out[3 * n:4 * n]), out[4 * n]


def _swap_wait(sems, grads, lands, after, name):
    n = len(grads)

    def body(*refs):
        g_ref, l_ref = refs[:n], refs[n:2 * n]
        sem_ref = refs[2 * n:4 * n]
        for cp in _swap_descriptors(g_ref, l_ref, lambda a: sem_ref[a], lambda a: sem_ref[n + a]):
            cp().wait()

    out = pl.pallas_call(
        body, name=name,
        out_shape=tuple(pltpu.HBM(a.shape, a.dtype) for a in (*grads, *lands)),
        in_specs=[HBM_SPEC] * (2 * n) + [SEM_SPEC] * (2 * n) + [ANY] * len(_as_tuple(after)),
        out_specs=tuple([HBM_SPEC] * (2 * n)),
        input_output_aliases={i: i for i in range(2 * n)},
        compiler_params=pltpu.CompilerParams(has_side_effects=pltpu.SideEffectType.DATAFLOW_SIDE_EFFECTING),
    )(*grads, *lands, *sems, *_as_tuple(after))
    return list(out[:n]), list(out[n:])


def _add_halves(gs, gots, pos, name, dtypes):
    n = len(gs)
    j = gs[0].shape[0]

    def body(pos_ref, *refs):
        g_refs, r_refs = refs[:n], refs[n:2 * n]
        o_refs, p_refs = refs[2 * n:3 * n], refs[3 * n:]
        vals = [(g_refs[a][0, 0] + r_refs[a][0, 0]).astype(dtypes[a]) for a in range(n)]
        for a in range(n):
            o_refs[a][0] = vals[a]
        if j == 1:
            for a in range(n):
                p_refs[a][0] = vals[a]
        else:
            @pl.when(pl.program_id(0) == pos_ref[0])
            def _():
                for a in range(n):
                    p_refs[a][0] = vals[a]

    blk = lambda g: (1,) + g.shape[2:]
    out = pl.pallas_call(
        body, name=name,
        grid_spec=pltpu.PrefetchScalarGridSpec(
            num_scalar_prefetch=1, grid=(j,),
            in_specs=[pl.BlockSpec((1,) + blk(g), lambda i, p: (i, p[1], 0, 0)) for g in gs]
            + [pl.BlockSpec((1,) + blk(g), lambda i, p: (i, 0, 0, 0)) for g in gs],
            out_specs=[pl.BlockSpec(blk(g), lambda i, p: (i, 0, 0)) for g in gs]
            + [pl.BlockSpec(blk(g), lambda i, p: (p[0], 0, 0)) for g in gs]),
        out_shape=[jax.ShapeDtypeStruct((j,) + g.shape[2:], dt) for g, dt in zip(gs, dtypes)]
        + [jax.ShapeDtypeStruct((N_CHIPS,) + g.shape[2:], dt) for g, dt in zip(gs, dtypes)],
        compiler_params=_params(("arbitrary",)),
    )(pos, *gs, *gots)
    return list(out[:n]), list(out[n:])


def _exchange_descriptors(sums, parts, send_of, recv_of):
    x, y, c = _mesh_pos()
    me = 2 * x + y
    chips = _other_chips(x, y)
    sends, arrivals = [], []
    for a in range(len(sums)):
        for k in range(3):
            ck = 2 * chips[k][0] + chips[k][1]
            mine = sums[a].at[ck] if sums[a].shape[0] == N_CHIPS else sums[a].at[0]

            def copy(dst_slot, a=a, k=k, mine=mine):
                return pltpu.make_async_remote_copy(
                    src_ref=mine, dst_ref=parts[a].at[dst_slot],
                    send_sem=send_of(a, k), recv_sem=recv_of(a, k),
                    device_id=(*chips[k], c), device_id_type=MESH)

            sends.append(functools.partial(copy, me))
            arrivals.append(functools.partial(copy, ck))
    return sends, arrivals


def _exchange_start(sums, parts, name, after=()):
    n = len(sums)
    ns = 3 * n

    def body(*refs):
        sems = refs[2 * n:2 * n + 2 * ns]
        sums_thru = refs[2 * n + 2 * ns:3 * n + 2 * ns]
        parts_thru = refs[3 * n + 2 * ns:4 * n + 2 * ns]
        token = refs[4 * n + 2 * ns]
        _chips_handshake()
        sends, _ = _exchange_descriptors(sums_thru, parts_thru, lambda a, k: sems[3 * a + k],
                                         lambda a, k: sems[ns + 3 * a + k])
        for cp in sends:
            cp().start()
        token[...] = jnp.zeros_like(token)

    hbm = lambda a: pltpu.HBM(a.shape, a.dtype)
    held = [pltpu.with_memory_space_constraint(a, pltpu.HBM) for a in (*sums, *parts)]
    out = _tied_call(
        body, after, name=name,
        out_shape=(*[pltpu.SemaphoreType.DMA(())] * (2 * ns), *[hbm(a) for a in held],
                   jax.ShapeDtypeStruct((8, LANES), F32)),
        in_specs=[HBM_SPEC] * (2 * n),
        out_specs=(*[SEM_SPEC] * (2 * ns), *[HBM_SPEC] * (2 * n), pl.BlockSpec(memory_space=pltpu.VMEM)),
        input_output_aliases={i: 2 * ns + i for i in range(2 * n)},
        compiler_params=pltpu.CompilerParams(has_side_effects=pltpu.SideEffectType.DATAFLOW_SIDE_EFFECTING,
                                             collective_id=CHIPS_COLLECTIVE_ID),
    )(*held)
    return (list(out[:2 * ns]), list(out[2 * ns:2 * ns + n]), list(out[2 * ns + n:2 * ns + 2 * n]),
            out[2 * ns + 2 * n])


def _exchange_wait(sems, sums, parts, after, name):
    n = len(sums)
    ns = 3 * n

    def body(*refs):
        sums_ref, parts_ref = refs[:n], refs[n:2 * n]
        sem_ref = refs[2 * n:2 * n + 2 * ns]
        sends, arrivals = _exchange_descriptors(sums_ref, parts_ref, lambda a, k: sem_ref[3 * a + k],
                                                lambda a, k: sem_ref[ns + 3 * a + k])
        for cp in sends:
            cp().wait_send()
        for cp in arrivals:
            cp().wait_recv()

    hbm = lambda a: pltpu.HBM(a.shape, a.dtype)
    out = pl.pallas_call(
        body, name=name,
        out_shape=tuple(hbm(a) for a in (*sums, *parts)),
        in_specs=[HBM_SPEC] * (2 * n) + [SEM_SPEC] * (2 * ns) + [ANY] * len(_as_tuple(after)),
        out_specs=tuple([HBM_SPEC] * (2 * n)),
        input_output_aliases={i: i for i in range(2 * n)},
        compiler_params=pltpu.CompilerParams(has_side_effects=pltpu.SideEffectType.DATAFLOW_SIDE_EFFECTING),
    )(*sums, *parts, *sems, *_as_tuple(after))
    return list(out[n:])


def _sum_chips(parts, pos, name, after=()):
    n = len(parts)
    after = _as_tuple(after)

    def body(pos_ref, *refs):
        outs = refs[n + len(after):]
        for a in range(n):
            p_ref = refs[a]
            outs[a][0] = (((p_ref[0].astype(F32) + p_ref[1].astype(F32)) + p_ref[2].astype(F32))
                          + p_ref[3].astype(F32))

    out = pl.pallas_call(
        body, name=name,
        grid_spec=pltpu.PrefetchScalarGridSpec(
            num_scalar_prefetch=1, grid=(1,),
            in_specs=[pl.BlockSpec(p.shape, lambda i, q: (0, 0, 0)) for p in parts] + [ANY] * len(after),
            out_specs=[pl.BlockSpec((1,) + p.shape[1:], lambda i, q: (q[1], 0, 0)) for p in parts]),
        out_shape=[jax.ShapeDtypeStruct((2,) + p.shape[1:], F32) for p in parts],
        compiler_params=_params(("arbitrary",)),
    )(pos, *parts, *after)
    return list(out)


def _join_halves(fulls, name, after=()):
    n = len(fulls)

    def body(*refs):
        outs = refs[n:2 * n]
        send_sem, recv_sem = refs[2 * n:]
        x, y, c = _mesh_pos()
        _sibling_handshake()

        def half(a, which):
            return pltpu.make_async_remote_copy(
                src_ref=outs[a].at[which], dst_ref=outs[a].at[which],
                send_sem=send_sem.at[a], recv_sem=recv_sem.at[a],
                device_id=(x, y, 1 - c), device_id_type=MESH)

        sends = [half(a, c) for a in range(n)]
        for cp in sends:
            cp.start()
        for a in range(n):
            half(a, 1 - c).wait_recv()
        for cp in sends:
            cp.wait_send()

    out_shape = [jax.ShapeDtypeStruct(f.shape, f.dtype) for f in fulls]
    return _tied_call(
        body, after, name=name,
        in_specs=[ANY] * n, out_specs=[ANY] * n, out_shape=out_shape,
        input_output_aliases={a: a for a in range(n)},
        scratch_shapes=[pltpu.SemaphoreType.DMA((n,))] * 2,
        compiler_params=pltpu.CompilerParams(collective_id=SIBLING_COLLECTIVE_ID),
    )(*fulls)


def _norm_in(x, g, ts, after=()):
    s = x.shape[0]

    def body(x_ref, g_ref, hn_ref):
        xv = x_ref[...]
        hn_ref[...] = (xv * _rms_stats(xv) * g_ref[...]).astype(BF16)

    row = pl.BlockSpec((ts, D_MODEL), lambda i: (i, 0))
    return _tied_call(
        body, after, name="norm_in", grid=(s // ts,),
        in_specs=[row, pl.BlockSpec((1, D_MODEL), lambda i: (0, 0))], out_specs=row,
        out_shape=jax.ShapeDtypeStruct((s, D_MODEL), BF16),
        compiler_params=_params(("parallel",)),
    )(x, g)


def _shift_rows(buf, shifted, t):
    rows = t + CONV_HALO - SUBLANES
    for r in range(1, SUBLANES):
        shifted[r - 1, 0:rows, :] = buf[pl.ds(r, rows), :]


def _window(buf, shifted, offset, t):
    r = offset % SUBLANES
    if r == 0:
        return buf[pl.ds(offset, t), :]
    return shifted[r - 1, pl.ds(offset - r, t), :]


def _lane_is_low_head():
    lane = lax.broadcasted_iota(jnp.int32, (1, GM_WIDTH), 1)
    return (lane & GM_HEAD_DIM) == 0


def _gm_mix(v_lo, v_hi, wpair_ref, bias_ref, mixed_ref, t):
    for n in range(t // CHUNK):
        rows = slice(n * CHUNK, (n + 1) * CHUNK)
        for j in range(GM_HEADS // 2):
            cols = slice(j * LANES, (j + 1) * LANES)
            rhs = jnp.concatenate([v_lo[rows, cols], v_hi[rows, cols]], axis=0)
            mixed_ref[rows, cols] = _dot(wpair_ref[j], rhs) + bias_ref[:, cols]


def _seqmix_fwd(hn, w_in, b_in, cw, cb, lng, lnb, gg, gb, wpair, bias, t, after=()):
    s = hn.shape[0]

    def body(hn_ref, w_ref, b_ref, cw_ref, cb_ref, lng_ref, lnb_ref, gg_ref, gb_ref, wpair_ref, bias_ref,
             z_ref, mix_ref, c1_ref, abuf, ash, mixed_ref):
        i = pl.program_id(0)

        @pl.when(i == 0)
        def _():
            abuf[0:CONV_HALO, :] = jnp.zeros((CONV_HALO, CONV_WIDTH), F32)

        @pl.when(i > 0)
        def _():
            abuf[0:CONV_HALO, :] = abuf[t:t + CONV_HALO, :]

        hv = hn_ref[...]
        for j in range(4):
            cols = slice(j * 512, (j + 1) * 512)
            z_ref[:, cols] = _dot(hv, w_ref[j]) + b_ref[:, cols]

        abuf[CONV_HALO:, :] = z_ref[:, 0:512] * _sigmoid(z_ref[:, 512:1024])
        _shift_rows(abuf, ash, t)
        acc = jnp.zeros((t, CONV_WIDTH), F32)
        for k in range(CONV_KERNEL):
            acc = acc + cw_ref[k:k + 1, :] * _window(abuf, ash, CONV_HALO - (CONV_KERNEL - 1) + k, t)
        c1 = acc + cb_ref[...]
        c1_ref[...] = c1
        xh, _ = _ln_stats(c1)
        ln = xh * lng_ref[...] + lnb_ref[...]
        mix_ref[:, 0:512] = (ln * _sigmoid(ln)).astype(BF16)

        u, _ = _gelu_parts(z_ref[:, 1024:1536])
        gv, _ = _gelu_parts(z_ref[:, 1536:2048])
        vxh, _ = _ln_stats(gv)
        v = vxh * gg_ref[...] + gb_ref[...]
        low = _lane_is_low_head()
        v_lo = jnp.where(low, v, 0.0).astype(BF16)
        v_hi = jnp.where(low, 0.0, v).astype(BF16)
        _gm_mix(v_lo, v_hi, wpair_ref, bias_ref, mixed_ref, t)
        mix_ref[:, 512:1024] = (u * mixed_ref[...]).astype(BF16)

    vec = lambda n: pl.BlockSpec((1, n), lambda i: (0, 0))
    return _tied_call(
        body, after, name="seqmix_fwd", grid=(s // t,),
        in_specs=[pl.BlockSpec((t, D_MODEL), lambda i: (i, 0)),
                  pl.BlockSpec((4, D_MODEL, 512), lambda i: (0, 0, 0)), vec(2048),
                  pl.BlockSpec((CONV_HALO, CONV_WIDTH), lambda i: (0, 0)),
                  vec(512), vec(512), vec(512), vec(512), vec(512),
                  pl.BlockSpec((4, CHUNK, 2 * CHUNK), lambda i: (0, 0, 0)),
                  pl.BlockSpec((CHUNK, GM_WIDTH), lambda i: (0, 0))],
        out_specs=[pl.BlockSpec((t, 2048), lambda i: (i, 0)),
                   pl.BlockSpec((t, D_MODEL), lambda i: (i, 0)),
                   pl.BlockSpec((t, CONV_WIDTH), lambda i: (i, 0))],
        out_shape=[jax.ShapeDtypeStruct((s, 2048), F32), jax.ShapeDtypeStruct((s, D_MODEL), BF16),
                   jax.ShapeDtypeStruct((s, CONV_WIDTH), F32)],
        scratch_shapes=[pltpu.VMEM((t + CONV_HALO, CONV_WIDTH), F32),
                        pltpu.VMEM((SUBLANES - 1, t + CONV_HALO - SUBLANES, CONV_WIDTH), F32),
                        pltpu.VMEM((t, GM_WIDTH), F32)],
        compiler_params=_params(("arbitrary",)),
    )(hn, w_in, b_in, cw, cb, lng, lnb, gg, gb, wpair, bias)


def _mem_kv(mem, g, wkv):
    m = mem.shape[0]

    def body(mem_ref, g_ref, w_ref, mn_ref, kv_ref):
        mv = mem_ref[...]
        mn = (mv * _rms_stats(mv) * g_ref[...]).astype(BF16)
        mn_ref[...] = mn
        for j in range(4):
            kv_ref[:, j * 512:(j + 1) * 512] = _dot(mn, w_ref[j]).astype(BF16)

    return pl.pallas_call(
        body, name="mem_kv",
        out_shape=[jax.ShapeDtypeStruct((m, D_MODEL), BF16), jax.ShapeDtypeStruct((m, 2 * D_MODEL), BF16)],
        compiler_params=pltpu.CompilerParams(vmem_limit_bytes=VMEM_LIMIT_BYTES),
    )(mem, g, wkv)


def _softmax_rows(sc):
    e = jnp.exp(sc - jnp.max(sc, axis=-1, keepdims=True))
    return e / jnp.sum(e, axis=-1, keepdims=True)


def _attn_block_fwd(x, mix, w_out, g_xa, wq, kv, wo, g_ffn, ts, after=()):
    s, m = x.shape[0], kv.shape[0]
    scale = XA_HEAD_DIM ** -0.5

    def body(x_ref, mix_ref, wout_ref, gxa_ref, wq_ref, kv_ref, wo_ref, gffn_ref,
             h1_ref, hn2_ref, q_ref, o_ref, h2_ref, hn3_ref):
        h1 = x_ref[...] + _dot(mix_ref[...], wout_ref[...])
        h1_ref[...] = h1
        hn2 = (h1 * _rms_stats(h1) * gxa_ref[...]).astype(BF16)
        hn2_ref[...] = hn2
        q_ref[...] = _dot(hn2, wq_ref[...]).astype(BF16)
        for h in range(XA_HEADS):
            cols = slice(h * XA_HEAD_DIM, (h + 1) * XA_HEAD_DIM)
            vcols = slice(D_MODEL + h * XA_HEAD_DIM, D_MODEL + (h + 1) * XA_HEAD_DIM)
            p = _softmax_rows(_dot_nt(q_ref[:, cols], kv_ref[:, cols]) * scale)
            o_ref[:, cols] = _dot(p.astype(BF16), kv_ref[:, vcols]).astype(BF16)
        h2 = h1 + _dot(o_ref[...], wo_ref[...])
        h2_ref[...] = h2
        hn3_ref[...] = (h2 * _rms_stats(h2) * gffn_ref[...]).astype(BF16)

    row = pl.BlockSpec((ts, D_MODEL), lambda i: (i, 0))
    full = pl.BlockSpec((D_MODEL, D_MODEL), lambda i: (0, 0))
    vec = pl.BlockSpec((1, D_MODEL), lambda i: (0, 0))
    f32 = jax.ShapeDtypeStruct((s, D_MODEL), F32)
    bf16 = jax.ShapeDtypeStruct((s, D_MODEL), BF16)
    return _tied_call(
        body, after, name="attn_block_fwd", grid=(s // ts,),
        in_specs=[row, row, full, vec, full, pl.BlockSpec((m, 2 * D_MODEL), lambda i: (0, 0)), full, vec],
        out_specs=[row] * 6,
        out_shape=[f32, bf16, bf16, bf16, f32, bf16],
        compiler_params=_params(("parallel",)),
    )(x, mix, w_out, g_xa, wq, kv, wo, g_ffn)


_FFN_CHUNKS = (slice(0, 8 * LANES), slice(8 * LANES, 16 * LANES), slice(16 * LANES, FFN_HIDDEN))


def _ffn_up(hn, wgu, ts, after=()):
    s = hn.shape[0]

    def body(hn_ref, w_ref, gu_ref, act_ref):
        hv = hn_ref[...]
        for cols in _FFN_CHUNKS:
            gate = _dot(hv, w_ref[0, :, cols])
            up = _dot(hv, w_ref[1, :, cols])
            gu_ref[0, :, cols] = gate.astype(BF16)
            gu_ref[1, :, cols] = up.astype(BF16)
            act_ref[:, cols] = (gate * _sigmoid(gate) * up).astype(BF16)

    return _tied_call(
        body, after, name="ffn_up", grid=(s // ts,),
        in_specs=[pl.BlockSpec((ts, D_MODEL), lambda i: (i, 0)),
                  pl.BlockSpec((2, D_MODEL, FFN_HIDDEN), lambda i: (0, 0, 0))],
        out_specs=[pl.BlockSpec((2, ts, FFN_HIDDEN), lambda i: (0, i, 0)),
                   pl.BlockSpec((ts, FFN_HIDDEN), lambda i: (i, 0))],
        out_shape=[jax.ShapeDtypeStruct((2, s, FFN_HIDDEN), BF16), jax.ShapeDtypeStruct((s, FFN_HIDDEN), BF16)],
        compiler_params=_params(("parallel",)),
    )(hn, wgu)


def _ffn_down_loss(act, wd, h2, g, target, ts):
    s = act.shape[0]

    def body(act_ref, wd_ref, h2_ref, g_ref, t_ref, dh_ref, dhb_ref, sq_ref, dg_ref):
        @pl.when(pl.program_id(0) == 0)
        def _():
            sq_ref[...] = jnp.zeros_like(sq_ref)
            dg_ref[...] = jnp.zeros_like(dg_ref)

        h3 = h2_ref[...] + _dot(act_ref[...], wd_ref[...])
        r = _rms_stats(h3)
        gv = g_ref[...]
        diff = h3 * r * gv - t_ref[...]
        sq_ref[...] += _rowsum(diff * diff)
        dh, dg = _rms_bwd(diff / D_MODEL, h3, r, gv)
        dh_ref[...] = dh
        dhb_ref[...] = dh.astype(BF16)
        dg_ref[...] += dg

    row = pl.BlockSpec((ts, D_MODEL), lambda i: (i, 0))
    vec = pl.BlockSpec((1, D_MODEL), lambda i: (0, 0))
    return pl.pallas_call(
        body, name="ffn_down_loss", grid=(s // ts,),
        in_specs=[pl.BlockSpec((ts, FFN_HIDDEN), lambda i: (i, 0)),
                  pl.BlockSpec((FFN_HIDDEN, D_MODEL), lambda i: (0, 0)), row, vec, row],
        out_specs=[row, row, vec, vec],
        out_shape=[jax.ShapeDtypeStruct((s, D_MODEL), F32), jax.ShapeDtypeStruct((s, D_MODEL), BF16),
                   jax.ShapeDtypeStruct((1, D_MODEL), F32), jax.ShapeDtypeStruct((1, D_MODEL), F32)],
        compiler_params=_params(("arbitrary",)),
    )(act, wd, h2, g, target)


def _grad_w(a, b, tk, tn, name, after=(), shards=1):
    s, k = a.shape
    gb, _, n = b.shape
    nblk = n // tn
    ws = tn // shards
    tsr = GRAD_ROWS if s % GRAD_ROWS == 0 else s

    def body(a_ref, b_ref, o_ref):
        part = _dot_tn(a_ref[...], b_ref[0])

        @pl.when(pl.program_id(2) == 0)
        def _():
            for j in range(shards):
                o_ref[j] = part[:, j * ws:(j + 1) * ws]

        @pl.when(pl.program_id(2) > 0)
        def _():
            for j in range(shards):
                o_ref[j] += part[:, j * ws:(j + 1) * ws]

    return _tied_call(
        body, after, name=name, grid=(gb * nblk, k // tk, s // tsr),
        in_specs=[pl.BlockSpec((tsr, tk), lambda ni, ki, si: (si, ki)),
                  pl.BlockSpec((1, tsr, tn), lambda ni, ki, si: (ni // nblk, si, ni % nblk))],
        out_specs=pl.BlockSpec((shards, tk, ws), lambda ni, ki, si: (ni, ki, 0)),
        out_shape=jax.ShapeDtypeStruct((gb * nblk * shards, k, ws), F32),
        compiler_params=_params(("parallel", "parallel", "arbitrary")),
    )(a, b)


def _grad_w_square(pairs, name, after=()):
    n = len(pairs)
    s = pairs[0][0].shape[0]
    tsr = GRAD_ROWS // 2 if s % (GRAD_ROWS // 2) == 0 else s

    def body(*refs):
        ins, outs = refs[:2 * n], refs[2 * n:]
        parts = [_dot_tn(ins[2 * a][...], ins[2 * a + 1][...]) for a in range(n)]

        @pl.when(pl.program_id(0) == 0)
        def _():
            for a in range(n):
                outs[a][...] = parts[a]

        @pl.when(pl.program_id(0) > 0)
        def _():
            for a in range(n):
                outs[a][...] += parts[a]

    row = pl.BlockSpec((tsr, D_MODEL), lambda i: (i, 0))
    return _tied_call(
        body, after, name=name, grid=(s // tsr,),
        in_specs=[row] * (2 * n), out_specs=[pl.BlockSpec((D_MODEL, D_MODEL), lambda i: (0, 0))] * n,
        out_shape=[jax.ShapeDtypeStruct((D_MODEL, D_MODEL), F32)] * n,
        compiler_params=_params(("arbitrary",)),
    )(*[x for p in pairs for x in p])


def _ffn_bwd(dh3, wd, gu, wgu, h2, g, t, after=()):
    s = dh3.shape[0]

    def body(dh3_ref, wd_ref, gu_ref, w_ref, h2_ref, g_ref, dgu_ref, dh2_ref, dh2b_ref, dg_ref):
        @pl.when(pl.program_id(0) == 0)
        def _():
            dg_ref[...] = jnp.zeros_like(dg_ref)

        dh3v = dh3_ref[...]
        dhb = dh3v.astype(BF16)
        for cols in _FFN_CHUNKS:
            dact = _dot_nt(dhb, wd_ref[cols, :])
            gate, up = gu_ref[0, :, cols].astype(F32), gu_ref[1, :, cols].astype(F32)
            sg = _sigmoid(gate)
            dgu_ref[0, :, cols] = (dact * up * (sg * (1.0 + gate * (1.0 - sg)))).astype(BF16)
            dgu_ref[1, :, cols] = (dact * (gate * sg)).astype(BF16)
        dhn = _dot_nt(dgu_ref[0], w_ref[0]) + _dot_nt(dgu_ref[1], w_ref[1])
        h2 = h2_ref[...]
        dv, dg = _rms_bwd(dhn, h2, _rms_stats(h2), g_ref[...])
        dh2 = dh3v + dv
        dh2_ref[...] = dh2
        dh2b_ref[...] = dh2.astype(BF16)
        dg_ref[...] += dg

    row = pl.BlockSpec((t, D_MODEL), lambda i: (i, 0))
    wide = pl.BlockSpec((2, t, FFN_HIDDEN), lambda i: (0, i, 0))
    vec = pl.BlockSpec((1, D_MODEL), lambda i: (0, 0))
    return _tied_call(
        body, after, name="ffn_bwd", grid=(s // t,),
        in_specs=[row, pl.BlockSpec((FFN_HIDDEN, D_MODEL), lambda i: (0, 0)), wide,
                  pl.BlockSpec((2, D_MODEL, FFN_HIDDEN), lambda i: (0, 0, 0)), row, vec],
        out_specs=[wide, row, row, vec],
        out_shape=[jax.ShapeDtypeStruct((2, s, FFN_HIDDEN), BF16), jax.ShapeDtypeStruct((s, D_MODEL), F32),
                   jax.ShapeDtypeStruct((s, D_MODEL), BF16), jax.ShapeDtypeStruct((1, D_MODEL), F32)],
        compiler_params=_params(("arbitrary",)),
    )(dh3, wd, gu, wgu, h2, g)


def _attn_bwd(dh2, wo, q, kv, wq, h1, g, ts, after=()):
    s, m = q.shape[0], kv.shape[0]
    scale = XA_HEAD_DIM ** -0.5

    def body(dh2_ref, wo_ref, q_ref, kv_ref, wq_ref, h1_ref, g_ref, dh1_ref, dh1b_ref, dq_ref, dkv_ref, dg_ref):
        @pl.when(pl.program_id(0) == 0)
        def _():
            dkv_ref[...] = jnp.zeros_like(dkv_ref)
            dg_ref[...] = jnp.zeros_like(dg_ref)

        do = _dot_nt(dh2_ref[...].astype(BF16), wo_ref[...]).astype(BF16)
        for h in range(XA_HEADS):
            cols = slice(h * XA_HEAD_DIM, (h + 1) * XA_HEAD_DIM)
            vcols = slice(D_MODEL + h * XA_HEAD_DIM, D_MODEL + (h + 1) * XA_HEAD_DIM)
            qh, kh, vh, doh = q_ref[:, cols], kv_ref[:, cols], kv_ref[:, vcols], do[:, cols]
            p = _softmax_rows(_dot_nt(qh, kh) * scale)
            dp = _dot_nt(doh, vh)
            ds = (p * (dp - jnp.sum(dp * p, axis=-1, keepdims=True)) * scale).astype(BF16)
            dq_ref[:, cols] = _dot(ds, kh).astype(BF16)
            dkv_ref[:, cols] += _dot_tn(ds, qh)
            dkv_ref[:, vcols] += _dot_tn(p.astype(BF16), doh)
        dhn = _dot_nt(dq_ref[...], wq_ref[...])
        h1 = h1_ref[...]
        dv, dg = _rms_bwd(dhn, h1, _rms_stats(h1), g_ref[...])
        dh1 = dh2_ref[...] + dv
        dh1_ref[...] = dh1
        dh1b_ref[...] = dh1.astype(BF16)
        dg_ref[...] += dg

    row = pl.BlockSpec((ts, D_MODEL), lambda i: (i, 0))
    full = pl.BlockSpec((D_MODEL, D_MODEL), lambda i: (0, 0))
    kvs = pl.BlockSpec((m, 2 * D_MODEL), lambda i: (0, 0))
    vec = pl.BlockSpec((1, D_MODEL), lambda i: (0, 0))
    return _tied_call(
        body, after, name="attn_bwd", grid=(s // ts,),
        in_specs=[row, full, row, kvs, full, row, vec],
        out_specs=[row, row, row, kvs, vec],
        out_shape=[jax.ShapeDtypeStruct((s, D_MODEL), F32), jax.ShapeDtypeStruct((s, D_MODEL), BF16),
                   jax.ShapeDtypeStruct((s, D_MODEL), BF16),
                   jax.ShapeDtypeStruct((m, 2 * D_MODEL), F32), jax.ShapeDtypeStruct((1, D_MODEL), F32)],
        compiler_params=_params(("arbitrary",)),
    )(dh2, wo, q, kv, wq, h1, g)


def _mem_kv_bwd(dkv, mn, wkv, mem, g, after=()):
    m = mem.shape[0]

    def body(dkv_ref, mn_ref, w_ref, mem_ref, g_ref, dw_ref, dg_ref):
        dmn = jnp.zeros((m, D_MODEL), F32)
        mn = mn_ref[...]
        for j in range(4):
            dj = dkv_ref[:, j * 512:(j + 1) * 512].astype(BF16)
            dw_ref[j] = _dot_tn(mn, dj)
            dmn = dmn + _dot_nt(dj, w_ref[j])
        mv = mem_ref[...]
        dg_ref[...] = _rowsum(dmn * (mv * _rms_stats(mv)))

    return _tied_call(
        body, after, name="mem_kv_bwd", in_specs=[pl.BlockSpec(memory_space=pltpu.VMEM)] * 5,
        out_shape=[jax.ShapeDtypeStruct((4, D_MODEL, 512), F32), jax.ShapeDtypeStruct((1, D_MODEL), F32)],
        compiler_params=pltpu.CompilerParams(vmem_limit_bytes=VMEM_LIMIT_BYTES),
    )(dkv, mn, wkv, mem, g)


def _seqmix_bwd(dh1, x, z, c1, w_out, w_in, g_mix, cw, lng, lnb, gg, gb, wpair, wpair_t, bias, t, after=()):
    s = x.shape[0]
    nt = s // t

    def body(dh1_ref, x_ref, z_ref, c1_ref, wo_ref, wi_ref, gm_ref, cw_ref, lng_ref, lnb_ref,
             gg_ref, gb_ref, wpair_ref, wpt_ref, bias_ref,
             gx_ref, dz_ref, dcw_ref, dcb_ref, dlng_ref, dlnb_ref, dgg_ref, dgb_ref, dws_ref, dbs_ref,
             dbin_ref, dgm_ref, dbuf, dsh, mixed_ref, dv_ref):
        i = pl.program_id(0)
        accs = (dcw_ref, dcb_ref, dlng_ref, dlnb_ref, dgg_ref, dgb_ref, dws_ref, dbs_ref, dbin_ref, dgm_ref)

        @pl.when(i == 0)
        def _():
            for r in accs:
                r[...] = jnp.zeros_like(r)
            dbuf[t:t + CONV_HALO, :] = jnp.zeros((CONV_HALO, CONV_WIDTH), F32)

        @pl.when(i > 0)
        def _():
            dbuf[t:t + CONV_HALO, :] = dbuf[0:CONV_HALO, :]

        dmix = _dot_nt(dh1_ref[...].astype(BF16), wo_ref[...])

        xh, rs = _ln_stats(c1_ref[...])
        lng = lng_ref[...]
        ln = xh * lng + lnb_ref[...]
        sl = _sigmoid(ln)
        dln = dmix[:, 0:512] * (sl * (1.0 + ln * (1.0 - sl)))
        dc1, dg_ln, db_ln = _ln_bwd(dln, xh, rs, lng)
        dlng_ref[...] += dg_ln
        dlnb_ref[...] += db_ln
        dcb_ref[...] += _rowsum(dc1)
        dbuf[0:t, :] = dc1

        za = z_ref[:, 0:512]
        sg = _sigmoid(z_ref[:, 512:1024])
        a = za * sg
        _shift_rows(dbuf, dsh, t)

        da = jnp.zeros((t, CONV_WIDTH), F32)
        for k in range(CONV_KERNEL):
            later = _window(dbuf, dsh, CONV_KERNEL - 1 - k, t)
            da = da + cw_ref[k:k + 1, :] * later
            dcw_ref[k:k + 1, :] += _rowsum(a * later)
        dza = da * sg
        dzg = da * za * (sg * (1.0 - sg))
        dz_ref[:, 0:512] = dza.astype(BF16)
        dz_ref[:, 512:1024] = dzg.astype(BF16)
        dbin_ref[:, 0:512] += _rowsum(dza)
        dbin_ref[:, 512:1024] += _rowsum(dzg)

        dgm = dmix[:, 512:1024]
        u, du_dz = _gelu_parts(z_ref[:, 1024:1536])
        gv, dgv_dz = _gelu_parts(z_ref[:, 1536:2048])
        vxh, vrs = _ln_stats(gv)
        ggv = gg_ref[...]
        v = vxh * ggv + gb_ref[...]
        low = _lane_is_low_head()
        v_lo = jnp.where(low, v, 0.0).astype(BF16)
        v_hi = jnp.where(low, 0.0, v).astype(BF16)
        _gm_mix(v_lo, v_hi, wpair_ref, bias_ref, mixed_ref, t)
        dzu = dgm * mixed_ref[...] * du_dz
        dm = dgm * u
        dm_lo = jnp.where(low, dm, 0.0).astype(BF16)
        dm_hi = jnp.where(low, 0.0, dm).astype(BF16)
        vb = v.astype(BF16)
        tril = (lax.broadcasted_iota(jnp.int32, (CHUNK, CHUNK), 1)
                <= lax.broadcasted_iota(jnp.int32, (CHUNK, CHUNK), 0))
        for n in range(t // CHUNK):
            rows = slice(n * CHUNK, (n + 1) * CHUNK)
            dbs_ref[...] += dm[rows, :]
            for j in range(GM_HEADS // 2):
                cols = slice(j * LANES, (j + 1) * LANES)
                stack = jnp.concatenate([dm_lo[rows, cols], dm_hi[rows, cols]], axis=0)
                dws = _dot_nt(stack, vb[rows, cols])
                dws_ref[2 * j] += jnp.where(tril, dws[0:CHUNK], 0.0)
                dws_ref[2 * j + 1] += jnp.where(tril, dws[CHUNK:2 * CHUNK], 0.0)
                dv_ref[rows, cols] = _dot(wpt_ref[j], stack)
        dgv, dg_gm, db_gm = _ln_bwd(dv_ref[...], vxh, vrs, ggv)
        dgg_ref[...] += dg_gm
        dgb_ref[...] += db_gm
        dzv = dgv * dgv_dz
        dz_ref[:, 1024:1536] = dzu.astype(BF16)
        dz_ref[:, 1536:2048] = dzv.astype(BF16)
        dbin_ref[:, 1024:1536] += _rowsum(dzu)
        dbin_ref[:, 1536:2048] += _rowsum(dzv)

        dhn = jnp.zeros((t, D_MODEL), F32)
        for j in range(4):
            dhn = dhn + _dot_nt(dz_ref[:, j * 512:(j + 1) * 512], wi_ref[j])
        xv = x_ref[...]
        dv, dg = _rms_bwd(dhn, xv, _rms_stats(xv), gm_ref[...])
        gx_ref[...] = dh1_ref[...] + dv
        dgm_ref[...] += dg

    rev = lambda w: pl.BlockSpec((t, w), lambda i: (nt - 1 - i, 0))
    const = lambda *shape: pl.BlockSpec(shape, lambda i: (0,) * len(shape))
    f32 = lambda *shape: jax.ShapeDtypeStruct(shape, F32)
    return _tied_call(
        body, after, name="seqmix_bwd", grid=(nt,),
        in_specs=[rev(D_MODEL), rev(D_MODEL), rev(2048), rev(CONV_WIDTH),
                  const(D_MODEL, D_MODEL), const(4, D_MODEL, 512), const(1, D_MODEL),
                  const(CONV_HALO, CONV_WIDTH), const(1, 512), const(1, 512), const(1, 512), const(1, 512),
                  const(4, CHUNK, 2 * CHUNK), const(4, CHUNK, 2 * CHUNK), const(CHUNK, GM_WIDTH)],
        out_specs=[rev(D_MODEL), rev(2048),
                   const(CONV_HALO, CONV_WIDTH), const(1, 512), const(1, 512), const(1, 512), const(1, 512),
                   const(1, 512), const(GM_HEADS, CHUNK, CHUNK), const(CHUNK, GM_WIDTH), const(1, 2048),
                   const(1, D_MODEL)],
        out_shape=[f32(s, D_MODEL), jax.ShapeDtypeStruct((s, 2048), BF16),
                   f32(CONV_HALO, CONV_WIDTH), f32(1, 512), f32(1, 512), f32(1, 512), f32(1, 512),
                   f32(1, 512), f32(GM_HEADS, CHUNK, CHUNK), f32(CHUNK, GM_WIDTH), f32(1, 2048),
                   f32(1, D_MODEL)],
        scratch_shapes=[pltpu.VMEM((t + CONV_HALO, CONV_WIDTH), F32),
                        pltpu.VMEM((SUBLANES - 1, t + CONV_HALO - SUBLANES, CONV_WIDTH), F32),
                        pltpu.VMEM((t, GM_WIDTH), F32), pltpu.VMEM((t, GM_WIDTH), F32)],
        compiler_params=_params(("arbitrary",)),
    )(dh1, x, z, c1, w_out, w_in, g_mix, cw, lng, lnb, gg, gb, wpair, wpair_t, bias)


def _head_bias_grad(dbs):
    def body(d_ref, o_ref):
        dv = d_ref[...]
        lane = lax.broadcasted_iota(jnp.int32, (CHUNK, LANES), 1)
        acc = jnp.zeros((CHUNK, LANES), F32)
        for h in range(GM_HEADS):
            sh = jnp.sum(dv[:, h * GM_HEAD_DIM:(h + 1) * GM_HEAD_DIM], axis=-1, keepdims=True)
            acc = acc + jnp.where(lane == h, sh, 0.0)
        o_ref[...] = acc

    return pl.pallas_call(body, name="head_bias_grad",
                          out_shape=jax.ShapeDtypeStruct((CHUNK, LANES), F32))(dbs)


def kernel(x, mem, norm_mix_g, w_in, b_in, conv_w, conv_b, conv_ln_g, conv_ln_b, gm_ln_g, gm_ln_b, gm_w_s, gm_b_s, w_out, norm_xa_g, mem_norm_g, xa_wq, xa_wkv, xa_wo, norm_ffn_g, ffn_w_gate_up, ffn_w_down, final_norm_g, loss_target, m_norm_mix_g, m_w_in, m_b_in, m_conv_w, m_conv_b, m_conv_ln_g, m_conv_ln_b, m_gm_ln_g, m_gm_ln_b, m_gm_w_s, m_gm_b_s, m_w_out, m_norm_xa_g, m_mem_norm_g, m_xa_wq, m_xa_wkv, m_xa_wo, m_norm_ffn_g, m_ffn_w_gate_up, m_ffn_w_down, m_final_norm_g, v_norm_mix_g, v_w_in, v_b_in, v_conv_w, v_conv_b, v_conv_ln_g, v_conv_ln_b, v_gm_ln_g, v_gm_ln_b, v_gm_w_s, v_gm_b_s, v_w_out, v_norm_xa_g, v_mem_norm_g, v_xa_wq, v_xa_wkv, v_xa_wo, v_norm_ffn_g, v_ffn_w_gate_up, v_ffn_w_down, v_final_norm_g):
    weights = dict(norm_mix_g=norm_mix_g, w_in=w_in, b_in=b_in, conv_w=conv_w, conv_b=conv_b, conv_ln_g=conv_ln_g,
                   conv_ln_b=conv_ln_b, gm_ln_g=gm_ln_g, gm_ln_b=gm_ln_b, gm_w_s=gm_w_s, gm_b_s=gm_b_s, w_out=w_out,
                   norm_xa_g=norm_xa_g, mem_norm_g=mem_norm_g, xa_wq=xa_wq, xa_wkv=xa_wkv, xa_wo=xa_wo,
                   norm_ffn_g=norm_ffn_g, ffn_w_gate_up=ffn_w_gate_up, ffn_w_down=ffn_w_down,
                   final_norm_g=final_norm_g)
    m_in = dict(norm_mix_g=m_norm_mix_g, w_in=m_w_in, b_in=m_b_in, conv_w=m_conv_w, conv_b=m_conv_b,
                conv_ln_g=m_conv_ln_g, conv_ln_b=m_conv_ln_b, gm_ln_g=m_gm_ln_g, gm_ln_b=m_gm_ln_b, gm_w_s=m_gm_w_s,
                gm_b_s=m_gm_b_s, w_out=m_w_out, norm_xa_g=m_norm_xa_g, mem_norm_g=m_mem_norm_g, xa_wq=m_xa_wq,
                xa_wkv=m_xa_wkv, xa_wo=m_xa_wo, norm_ffn_g=m_norm_ffn_g, ffn_w_gate_up=m_ffn_w_gate_up,
                ffn_w_down=m_ffn_w_down, final_norm_g=m_final_norm_g)
    v_in = dict(norm_mix_g=v_norm_mix_g, w_in=v_w_in, b_in=v_b_in, conv_w=v_conv_w, conv_b=v_conv_b,
                conv_ln_g=v_conv_ln_g, conv_ln_b=v_conv_ln_b, gm_ln_g=v_gm_ln_g, gm_ln_b=v_gm_ln_b, gm_w_s=v_gm_w_s,
                gm_b_s=v_gm_b_s, w_out=v_w_out, norm_xa_g=v_norm_xa_g, mem_norm_g=v_mem_norm_g, xa_wq=v_xa_wq,
                xa_wkv=v_xa_wkv, xa_wo=v_xa_wo, norm_ffn_g=v_norm_ffn_g, ffn_w_gate_up=v_ffn_w_gate_up,
                ffn_w_down=v_ffn_w_down, final_norm_g=v_final_norm_g)
    grads, delta, new_m, new_v = {}, {}, {}, {}

    s = x.shape[1]
    ts = _row_tile(s)
    tb = max(CHUNK, ts // 2)
    tw = 2 * ts if s % (2 * ts) == 0 and ts >= 512 else ts
    cx, cy, cc = _mesh_pos()
    chip = 2 * cx + cy
    pos = jnp.stack([chip, cc]).astype(jnp.int32)
    row = lambda a: a.reshape(1, -1)
    x2, mem2, tgt2 = x[0], mem[0], loss_target[0]

    big = dict(w_in=w_in, xa_wkv=xa_wkv, w_out=w_out, xa_wq=xa_wq, xa_wo=xa_wo,
               ffn_w_gate_up=ffn_w_gate_up, ffn_w_down=ffn_w_down)
    big_names = list(big)
    halves = lambda a: a.reshape(2, a.shape[0] // 2, a.shape[1])
    conv_w_pad = jnp.pad(conv_w, ((0, CONV_HALO - CONV_KERNEL), (0, 0)))
    first_names = ["w_in", "conv_w"]
    later_names = [nm for nm in big_names if nm != "w_in"]
    cast = dict(zip(first_names, _cast_into_slots([halves(w_in), halves(conv_w_pad)], pos, [BF16, F32], "cast_w_in")))
    cast.update(zip(later_names, _cast_into_slots([halves(big[nm]) for nm in later_names], pos,
                                                  [BF16] * len(later_names), "cast_" + later_names[0],
                                                  side_by_side=(later_names.index("ffn_w_gate_up"),))))

    def start_gather(names, after):
        return _gather_start([cast[nm] for nm in names], "gather_start_" + names[0], after)

    def land_gather(names, started, after):
        send_sems, recv_sems, bufs, _ = started
        return _gather_wait(send_sems, recv_sems, bufs, after, "gather_wait_" + names[0])

    def share_gather(names, landed, after=()):
        return dict(zip(names, (b.reshape(b.shape[0], -1, b.shape[-1])
                                for b in _pass_to_sibling(landed, "pass_" + names[0], after))))

    tril = jnp.tril(jnp.ones((CHUNK, CHUNK), dtype=bool))
    ws = jnp.where(tril[None], gm_w_s, 0.0)
    wpair = jnp.concatenate([ws[0::2], ws[1::2]], axis=2).astype(BF16)
    ws_t = jnp.swapaxes(ws, 1, 2)
    wpair_t = jnp.concatenate([ws_t[0::2], ws_t[1::2]], axis=2).astype(BF16)
    bias = jnp.repeat(gm_b_s.T, GM_HEAD_DIM, axis=1)

    attn_names = ["w_out", "xa_wq", "xa_wkv", "xa_wo"]
    gather_first = start_gather(first_names, ())
    hn1 = _norm_in(x2, row(norm_mix_g), tw, after=(gather_first[3], wpair, wpair_t, bias))
    landed = land_gather(first_names, gather_first, [cast[nm] for nm in later_names] + [hn1])
    gather_attn = start_gather(attn_names, landed)
    gw = share_gather(first_names, landed, gather_attn[3])
    w_in_g = gw["w_in"]
    cw_g = jnp.concatenate([gw["conv_w"][k] for k in range(N_CHIPS)], axis=1)

    z, mix, c1 = _seqmix_fwd(hn1, w_in_g, row(b_in), cw_g, row(conv_b), row(conv_ln_g), row(conv_ln_b),
                             row(gm_ln_g), row(gm_ln_b), wpair, bias, ts)
    landed = land_gather(attn_names, gather_attn, mix)
    gather_gu = start_gather(["ffn_w_gate_up"], landed)
    gw = share_gather(attn_names, landed, gather_gu[3])
    w_out_g = gw["w_out"].reshape(D_MODEL, D_MODEL)
    wq_g = gw["xa_wq"].reshape(D_MODEL, D_MODEL)
    wkv_g = gw["xa_wkv"]
    wo_g = gw["xa_wo"].reshape(D_MODEL, D_MODEL)
    mn, kv = _mem_kv(mem2, row(mem_norm_g), wkv_g)
    h1, hn2, q, o, h2, hn3 = _attn_block_fwd(x2, mix, w_out_g, row(norm_xa_g), wq_g, kv, wo_g, row(norm_ffn_g), ts)
    landed = land_gather(["ffn_w_gate_up"], gather_gu, hn3)
    gather_down = start_gather(["ffn_w_down"], landed)
    wgu_g = share_gather(["ffn_w_gate_up"], landed, gather_down[3])["ffn_w_gate_up"]
    gu, act = _ffn_up(hn3, wgu_g, ts)
    landed = land_gather(["ffn_w_down"], gather_down, act)
    wd_g = share_gather(["ffn_w_down"], landed)["ffn_w_down"].reshape(FFN_HIDDEN, D_MODEL)
    dh3, dh3_b, sq, d_final_g = _ffn_down_loss(act, wd_g, h2, row(final_norm_g), tgt2, ts)
    loss_here = jnp.broadcast_to(0.5 * jnp.sum(sq) / D_MODEL, (1, 2, SUBLANES, LANES))

    def split(g, nm):
        r, c = big[nm].shape
        return g.reshape(N_CHIPS, 2, r // 2, c)

    def chip_sums(group, arrays, got):
        sums, parts = [None] * len(group), [None] * len(group)
        for blocks in (N_CHIPS, 1):
            idx = [i for i, a in enumerate(arrays) if a.shape[0] == blocks]
            if idx:
                out = _add_halves([arrays[i] for i in idx], [got[i] for i in idx], pos, "chip_sum_" + group[idx[0]],
                                  [F32 if group[i] == "loss" else BF16 for i in idx])
                for k, i in enumerate(idx):
                    sums[i], parts[i] = out[0][k], out[1][k]
        return sums, parts

    def start_swap(group, grads):
        return _swap_start([split(g, nm) for g, nm in zip(grads, group)], "swap_start_" + group[0])

    def start_exchange(group, swapping, after, landed):
        sems, arrays, lands, _ = swapping
        arrays, got = _swap_wait(sems, arrays, lands, after, "swap_wait_" + group[0])
        sums, parts = chip_sums(group, arrays, got)
        return _exchange_start(sums, parts, "exchange_start_" + group[0], landed)

    def wait_exchange(group, started, after):
        sems, sums, parts, _ = started
        return _exchange_wait(sems, sums, parts, after, "exchange_wait_" + group[0])

    def finish_exchange(group, started, after):
        return _sum_chips(wait_exchange(group, started, after), pos, "total_" + group[0])

    def join_and_update(group, after):
        joined = _join_halves([halves_of[nm] for nm in group], "join_halves_" + group[0], after)
        outs = _adamw([(weights[nm], j.reshape(big[nm].shape), m_in[nm], v_in[nm]) for nm, j in zip(group, joined)],
                      "adamw_" + group[0])
        for nm, out in zip(group, outs):
            grads[nm], delta[nm], new_m[nm], new_v[nm] = out
        return [new_v[nm] for nm in group]

    as3 = lambda a: a.reshape((1,) + a.shape)
    halves_of = {}

    g_down = _grad_w(act, as3(dh3_b), FFN_HALF, D_MODEL, "grad_ffn_w_down")
    group_a = ["ffn_w_down"]
    swap_a = start_swap(group_a, [g_down])
    dgu, dh2, dh2_b, d_ffn_g = _ffn_bwd(dh3, wd_g, gu, wgu_g, h2, row(norm_ffn_g), tb,
                                        after=swap_a[3])
    exch_a = start_exchange(group_a, swap_a, dh2, wd_g)
    g_gu = _grad_w(hn3, dgu, D_MODEL, FFN_HALF, "grad_ffn_w_gate_up", after=exch_a[3])
    halves_of.update(zip(group_a, finish_exchange(group_a, exch_a, g_gu)))

    group_b = ["ffn_w_gate_up"]
    swap_b = start_swap(group_b, [g_gu])
    dh1, dh1_b, dq, dkv, d_xa_g = _attn_bwd(dh2, wo_g, q, kv, wq_g, h1, row(norm_xa_g), ts, after=swap_b[3])
    exch_b = start_exchange(group_b, swap_b, dh1, [halves_of[nm] for nm in group_a])
    g_wkv, d_mem_g = _mem_kv_bwd(dkv, mn, wkv_g, mem2, row(mem_norm_g), after=exch_b[3])
    g_wo, g_wq, g_wout = _grad_w_square([(o, dh2_b), (hn2, dq), (mix, dh1_b)], "grad_xa_wo", after=exch_b[3])
    done_a = join_and_update(group_a, (g_wkv, g_wo, g_wq, g_wout))
    halves_of.update(zip(group_b, finish_exchange(group_b, exch_b, done_a)))

    group_c = ["xa_wo", "xa_wq", "xa_wkv", "w_out"]
    swap_c = start_swap(group_c, [g_wo, g_wq, g_wkv, g_wout])
    (gx, dz, d_cw, d_cb, d_lng, d_lnb, d_gg, d_gb, d_ws, d_bs_sum, d_bin, d_mix_g) = _seqmix_bwd(
        dh1, x2, z, c1, w_out_g, w_in_g, row(norm_mix_g), cw_g, row(conv_ln_g), row(conv_ln_b),
        row(gm_ln_g), row(gm_ln_b), wpair, wpair_t, bias, tb, after=swap_c[3])
    d_bs = _head_bias_grad(d_bs_sum)[:, :GM_HEADS].T
    exch_c = start_exchange(group_c, swap_c, dz, [halves_of[nm] for nm in group_b])
    g_win = _grad_w(hn1, as3(dz), D_MODEL, 1024, "grad_w_in", after=exch_c[3], shards=2)
    done_b = join_and_update(group_b, g_win)

    small_names = ["norm_mix_g", "b_in", "conv_w", "conv_b", "conv_ln_g", "conv_ln_b", "gm_ln_g", "gm_ln_b",
                   "gm_w_s", "gm_b_s", "norm_xa_g", "mem_norm_g", "norm_ffn_g", "final_norm_g"]
    d_cw_by_chip = jnp.swapaxes(d_cw.reshape(CONV_HALO, N_CHIPS, LANES), 0, 1).reshape(-1, LANES)
    small_grads = dict(norm_mix_g=d_mix_g, b_in=d_bin, conv_w=d_cw_by_chip, conv_b=d_cb, conv_ln_g=d_lng,
                       conv_ln_b=d_lnb, gm_ln_g=d_gg, gm_ln_b=d_gb, gm_w_s=d_ws, gm_b_s=d_bs, norm_xa_g=d_xa_g,
                       mem_norm_g=d_mem_g, norm_ffn_g=d_ffn_g, final_norm_g=d_final_g)

    def rows_form(a):
        a = a.reshape(-1, LANES)
        return jnp.pad(a, ((0, -a.shape[0] % SUBLANES), (0, 0)))

    pieces = [rows_form(small_grads[nm]) for nm in small_names]
    offsets, total = [], 0
    for p in pieces:
        offsets.append(total)
        total += p.shape[0]
    pack_rows = -(-total // 32) * 32
    small_pack = jnp.pad(jnp.concatenate(pieces, axis=0), ((0, pack_rows - total), (0, 0)))

    group_d = ["w_in", "small", "loss"]
    arrays_d = [split(g_win, "w_in"), small_pack.reshape(1, 2, pack_rows // 2, LANES), loss_here]
    sums_d, parts_d = chip_sums(group_d, arrays_d, _swap_halves(arrays_d, "swap_halves_w_in"))
    parts_c = wait_exchange(group_c, exch_c, (g_win, *done_b, *sums_d))
    exch_d = _exchange_start(sums_d, parts_d, "exchange_start_w_in", parts_c)
    halves_of.update(zip(group_c, _sum_chips(parts_c, pos, "total_xa_wo", exch_d[3])))
    done_c = join_and_update(group_c, exch_d[3])
    halves_of.update(zip(group_d, finish_exchange(group_d, exch_d, done_c)))
    joined_d = _join_halves([halves_of[nm] for nm in group_d], "join_halves_w_in")
    loss = joined_d[2][0, 0, 0]
    grads["w_in"], delta["w_in"], new_m["w_in"], new_v["w_in"] = _adamw(
        [(w_in, joined_d[0].reshape(w_in.shape), m_w_in, v_w_in)], "adamw_w_in")[0]

    local_rows = lambda a, nm: a if nm == "conv_w" else a.reshape(-1, LANES)
    params = [tuple(local_rows(src[nm], nm) for src in (weights, m_in, v_in)) for nm in small_names]
    outs = _adamw_small(joined_d[1].reshape(pack_rows, LANES), pos, params, offsets, small_names.index("conv_w"))
    for k, nm in enumerate(small_names):
        for dst, a in zip((grads, delta, new_m, new_v), outs[4 * k:4 * k + 4]):
            dst[nm] = a

    order = ["norm_mix_g", "w_in", "b_in", "conv_w", "conv_b", "conv_ln_g", "conv_ln_b", "gm_ln_g", "gm_ln_b",
             "gm_w_s", "gm_b_s", "w_out", "norm_xa_g", "mem_norm_g", "xa_wq", "xa_wkv", "xa_wo", "norm_ffn_g",
             "ffn_w_gate_up", "ffn_w_down", "final_norm_g"]
    fit = lambda a, nm: a.reshape(weights[nm].shape)
    return (loss, gx.reshape(x.shape),
            *[fit(grads[nm], nm) for nm in order], *[fit(delta[nm], nm) for nm in order],
            *[fit(new_m[nm], nm) for nm in order], *[fit(new_v[nm], nm) for nm in order])
```

```python
import functools

import jax
import jax.numpy as jnp
from jax import lax
from jax.experimental import pallas as pl
from jax.experimental.pallas import tpu as pltpu

F32 = jnp.float32
BF16 = jnp.bfloat16

D_MODEL = 1024
CONV_WIDTH = 512
GM_WIDTH = 512
CONV_KERNEL = 31
CONV_HALO = 32
GRAD_ROWS = 2048
CHUNK = 128
GM_HEADS = 8
GM_HEAD_DIM = 64
XA_HEADS = 4
XA_HEAD_DIM = 256
FFN_HIDDEN = 2816
FFN_HALF = FFN_HIDDEN // 2
RMS_EPS = 1e-6
LN_EPS = 1e-5
N_CHIPS = 4
LANES = 128
SUBLANES = 8

ADAM_LR = 0.001
ADAM_B1 = 0.9
ADAM_B2 = 0.999
ADAM_EPS = 1e-08
ADAM_WD = 0.01
ADAM_STEP = 10

VMEM_LIMIT_BYTES = 56 * 1024 * 1024
MESH = pl.DeviceIdType.MESH
ANY = pl.BlockSpec(memory_space=pl.ANY)
HBM_SPEC = pl.BlockSpec(memory_space=pltpu.HBM)
SEM_SPEC = pl.BlockSpec(memory_space=pltpu.SEMAPHORE)

_NT = (((1,), (1,)), ((), ()))
_TN = (((0,), (0,)), ((), ()))
_GELU_C = 0.7978845608028654
_GELU_A = 0.044715


def _dot(a, b):
    return jnp.dot(a, b, preferred_element_type=F32)


def _dot_nt(a, b):
    return lax.dot_general(a, b, _NT, preferred_element_type=F32)


def _dot_tn(a, b):
    return lax.dot_general(a, b, _TN, preferred_element_type=F32)


def _mean(v):
    return jnp.mean(v, axis=-1, keepdims=True)


def _rowsum(v):
    return jnp.sum(v, axis=0, keepdims=True)


def _sigmoid(v):
    return 1.0 / (1.0 + jnp.exp(-v))


def _gelu_parts(v):
    v2 = v * v
    t = jnp.tanh(_GELU_C * (v + _GELU_A * v * v2))
    g = 0.5 * v * (1.0 + t)
    dg = 0.5 * (1.0 + t) + 0.5 * v * (1.0 - t * t) * (_GELU_C * (1.0 + 3.0 * _GELU_A * v2))
    return g, dg


def _rms_stats(v):
    return lax.rsqrt(_mean(v * v) + RMS_EPS)


def _rms_bwd(dy, v, r, g):
    n = v * r
    dn = dy * g
    dv = r * (dn - n * _mean(dn * n))
    return dv, _rowsum(dy * n)


def _ln_stats(v):
    mu = _mean(v)
    xc = v - mu
    rs = lax.rsqrt(_mean(xc * xc) + LN_EPS)
    return xc * rs, rs


def _ln_bwd(dy, xh, rs, g):
    dxh = dy * g
    dv = rs * (dxh - _mean(dxh) - xh * _mean(dxh * xh))
    return dv, _rowsum(dy * xh), _rowsum(dy)


def _params(sem):
    return pltpu.CompilerParams(dimension_semantics=sem, vmem_limit_bytes=VMEM_LIMIT_BYTES)


def _row_tile(s):
    return 512 if s % 512 == 0 and s >= 2048 else 128


def _mesh_pos():
    return lax.axis_index("x"), lax.axis_index("y"), lax.axis_index("c")


def _slot(buf, chip_idx, half):
    if buf.shape[0] == N_CHIPS:
        return buf.at[chip_idx, half]
    width = buf.shape[-1] // 2
    return buf.at[chip_idx // 2, half, :, pl.ds(pl.multiple_of((chip_idx % 2) * width, LANES), width)]


def _cast_into_slots(ws, pos, dtypes, name, side_by_side=()):
    n = len(ws)

    def body(pos_ref, *refs):
        for a in range(n):
            refs[n + a][0] = refs[a][...].astype(dtypes[a])

    def out_spec(a, w):
        if a in side_by_side:
            return pl.BlockSpec((1, 1) + w.shape[1:], lambda i, p: (p[0] // 2, i, 0, p[0] % 2))
        return pl.BlockSpec((1, 1) + w.shape[1:], lambda i, p: (p[0], i, 0, 0))

    def out_shape(a, w):
        if a in side_by_side:
            return (2, 2, w.shape[1], 2 * w.shape[2])
        return (N_CHIPS,) + w.shape

    return pl.pallas_call(
        body, name=name,
        grid_spec=pltpu.PrefetchScalarGridSpec(
            num_scalar_prefetch=1, grid=(2,),
            in_specs=[pl.BlockSpec((1,) + w.shape[1:], lambda i, p: (i, 0, 0)) for w in ws],
            out_specs=[out_spec(a, w) for a, w in enumerate(ws)]),
        out_shape=[jax.ShapeDtypeStruct(out_shape(a, w), dt) for a, (w, dt) in enumerate(zip(ws, dtypes))],
        compiler_params=_params(("parallel",)),
    )(pos, *ws)


def _adam_update(w, g, m, v):
    nm = ADAM_B1 * m + (1.0 - ADAM_B1) * g
    nv = ADAM_B2 * v + (1.0 - ADAM_B2) * (g * g)
    m_hat = nm / (1.0 - ADAM_B1 ** ADAM_STEP)
    v_hat = nv / (1.0 - ADAM_B2 ** ADAM_STEP)
    return -ADAM_LR * (m_hat / (jnp.sqrt(v_hat) + ADAM_EPS) + ADAM_WD * w), nm, nv


ADAM_STEPS = 4


def _adamw(quads, name, after=()):
    n = len(quads)

    def body(*refs):
        ins, outs = refs[:4 * n], refs[4 * n:]
        for a in range(n):
            w, g, m, v = (r[...] for r in ins[4 * a:4 * a + 4])
            outs[4 * a][...] = g
            outs[4 * a + 1][...], outs[4 * a + 2][...], outs[4 * a + 3][...] = _adam_update(w, g, m, v)

    specs = [pl.BlockSpec((q[0].shape[0] // ADAM_STEPS, q[0].shape[1]), lambda i: (i, 0)) for q in quads]
    out = _tied_call(
        body, after, name=name, grid=(ADAM_STEPS,),
        in_specs=[sp for sp in specs for _ in range(4)], out_specs=[sp for sp in specs for _ in range(4)],
        out_shape=[jax.ShapeDtypeStruct(q[0].shape, F32) for q in quads for _ in range(4)],
        compiler_params=_params(("parallel",)),
    )(*[a for q in quads for a in q])
    return [tuple(out[4 * a:4 * a + 4]) for a in range(n)]


def _adamw_small(gpack, pos, params, offsets, conv_at):
    n = len(params)

    def body(pos_ref, g_ref, *refs):
        ins, outs = refs[:3 * n], refs[3 * n:]
        for k in range(n):
            rows = params[k][0].shape[0]
            start = offsets[k]
            if k == conv_at:
                start = pl.multiple_of(start + pos_ref[0] * CONV_HALO, SUBLANES)
            g = g_ref[pl.ds(start, rows), :]
            outs[4 * k][...] = g
            outs[4 * k + 1][...], outs[4 * k + 2][...], outs[4 * k + 3][...] = _adam_update(
                ins[3 * k][...], g, ins[3 * k + 1][...], ins[3 * k + 2][...])

    flat = [a for p in params for a in p]
    vmem = pl.BlockSpec(memory_space=pltpu.VMEM)
    return pl.pallas_call(
        body, name="adamw_small",
        in_specs=[pl.BlockSpec(memory_space=pltpu.SMEM), vmem] + [vmem] * len(flat),
        out_specs=[vmem] * (4 * n),
        out_shape=[jax.ShapeDtypeStruct(p[0].shape, F32) for p in params for _ in range(4)],
    )(pos, gpack, *flat)


def _as_tuple(after):
    return tuple(after) if isinstance(after, (tuple, list)) else (after,)


def _tied_call(body, after, *, in_specs, **kwargs):
    after = _as_tuple(after)
    n_in, n_after = len(in_specs), len(after)

    def tied(*refs):
        body(*refs[:n_in], *refs[n_in + n_after:])

    call = pl.pallas_call(tied, in_specs=list(in_specs) + [ANY] * n_after, **kwargs)
    return lambda *operands: call(*operands, *after)


def _other_chips(x, y):
    return [(1 - x, y), (x, 1 - y), (1 - x, 1 - y)]


def _gather_descriptors(bufs, send_of, recv_of):
    x, y, c = _mesh_pos()
    me = 2 * x + y
    chips = _other_chips(x, y)
    sends, arrivals = [], []
    for a in range(len(bufs)):
        for k in range(3):
            ck = 2 * chips[k][0] + chips[k][1]

            def copy(slot, a=a, k=k):
                return pltpu.make_async_remote_copy(
                    src_ref=_slot(bufs[a], slot, c), dst_ref=_slot(bufs[a], slot, c),
                    send_sem=send_of(a, k), recv_sem=recv_of(a, k),
                    device_id=(*chips[k], c), device_id_type=MESH)

            sends.append(functools.partial(copy, me))
            arrivals.append(functools.partial(copy, ck))
    return sends, arrivals


def _gather_start(bufs, name, after=()):
    n = len(bufs)
    ns = 3 * n

    def body(*refs):
        sems = refs[n:n + 2 * ns]
        thru = refs[n + 2 * ns:2 * n + 2 * ns]
        token = refs[2 * n + 2 * ns]
        _chips_handshake()
        sends, _ = _gather_descriptors(thru, lambda a, k: sems[3 * a + k], lambda a, k: sems[ns + 3 * a + k])
        for cp in sends:
            cp().start()
        token[...] = jnp.zeros_like(token)

    held = [pltpu.with_memory_space_constraint(b, pltpu.HBM) for b in bufs]
    out = _tied_call(
        body, after, name=name,
        out_shape=(*[pltpu.SemaphoreType.DMA(())] * (2 * ns), *[pltpu.HBM(b.shape, b.dtype) for b in held],
                   jax.ShapeDtypeStruct((8, LANES), F32)),
        in_specs=[HBM_SPEC] * n,
        out_specs=(*[SEM_SPEC] * (2 * ns), *[HBM_SPEC] * n, pl.BlockSpec(memory_space=pltpu.VMEM)),
        input_output_aliases={i: 2 * ns + i for i in range(n)},
        compiler_params=pltpu.CompilerParams(has_side_effects=pltpu.SideEffectType.DATAFLOW_SIDE_EFFECTING,
                                             collective_id=CHIPS_COLLECTIVE_ID),
    )(*held)
    return list(out[:ns]), list(out[ns:2 * ns]), list(out[2 * ns:2 * ns + n]), out[2 * ns + n]


def _gather_wait(send_sems, recv_sems, bufs, after, name):
    n = len(bufs)
    ns = 3 * n

    def body(*refs):
        buf_ref = refs[:n]
        sem_ref = refs[n:n + 2 * ns]
        sends, arrivals = _gather_descriptors(buf_ref, lambda a, k: sem_ref[3 * a + k],
                                              lambda a, k: sem_ref[ns + 3 * a + k])
        for cp in sends:
            cp().wait_send()
        for cp in arrivals:
            cp().wait_recv()

    out = pl.pallas_call(
        body, name=name,
        out_shape=tuple(pltpu.HBM(b.shape, b.dtype) for b in bufs),
        in_specs=[HBM_SPEC] * n + [SEM_SPEC] * (2 * ns) + [ANY] * len(_as_tuple(after)),
        out_specs=tuple([HBM_SPEC] * n),
        input_output_aliases={i: i for i in range(n)},
        compiler_params=pltpu.CompilerParams(has_side_effects=pltpu.SideEffectType.DATAFLOW_SIDE_EFFECTING),
    )(*bufs, *send_sems, *recv_sems, *_as_tuple(after))
    return list(out)


SIBLING_COLLECTIVE_ID = 0


def _sibling_handshake():
    x, y, c = _mesh_pos()
    barrier = pltpu.get_barrier_semaphore()
    pl.semaphore_signal(barrier, inc=1, device_id=(x, y, 1 - c), device_id_type=MESH)
    pl.semaphore_wait(barrier, 1)


CHIPS_COLLECTIVE_ID = 1


def _chips_handshake():
    x, y, c = _mesh_pos()
    barrier = pltpu.get_barrier_semaphore()
    for chip in _other_chips(x, y):
        pl.semaphore_signal(barrier, inc=1, device_id=(*chip, c), device_id_type=MESH)
    pl.semaphore_wait(barrier, 3)


def _pass_to_sibling(bufs, name, after=()):
    n = len(bufs)

    def body(*refs):
        outs = refs[n:2 * n]
        send_sem, recv_sem = refs[2 * n:]
        x, y, c = _mesh_pos()
        chips = _other_chips(x, y)
        _sibling_handshake()

        def half(a, k, which):
            ck = 2 * chips[k][0] + chips[k][1]
            return pltpu.make_async_remote_copy(
                src_ref=_slot(outs[a], ck, which), dst_ref=_slot(outs[a], ck, which),
                send_sem=send_sem.at[a, k], recv_sem=recv_sem.at[a, k],
                device_id=(x, y, 1 - c), device_id_type=MESH)

        sends = [half(a, k, c) for a in range(n) for k in range(3)]
        for cp in sends:
            cp.start()
        for a in range(n):
            for k in range(3):
                half(a, k, 1 - c).wait_recv()
        for cp in sends:
            cp.wait_send()

    return _tied_call(
        body, after, name=name,
        in_specs=[ANY] * n, out_specs=[ANY] * n,
        out_shape=[jax.ShapeDtypeStruct(b.shape, b.dtype) for b in bufs],
        input_output_aliases={a: a for a in range(n)},
        scratch_shapes=[pltpu.SemaphoreType.DMA((n, 3))] * 2,
        compiler_params=pltpu.CompilerParams(collective_id=SIBLING_COLLECTIVE_ID),
    )(*bufs)


def _swap_descriptors(grads, lands, send_of, recv_of):
    x, y, c = _mesh_pos()
    return [functools.partial(
        pltpu.make_async_remote_copy,
        src_ref=grads[a].at[:, pl.ds(1 - c, 1)], dst_ref=lands[a],
        send_sem=send_of(a), recv_sem=recv_of(a),
        device_id=(x, y, 1 - c), device_id_type=MESH) for a in range(len(grads))]


def _swap_halves(grads, name, after=()):
    n = len(grads)

    def body(*refs):
        ins, outs = refs[:n], refs[n:2 * n]
        send_sem, recv_sem = refs[2 * n:]
        _sibling_handshake()
        cps = [cp() for cp in _swap_descriptors(ins, outs, lambda a: send_sem.at[a], lambda a: recv_sem.at[a])]
        for cp in cps:
            cp.start()
        for cp in cps:
            cp.wait()

    out_shape = [jax.ShapeDtypeStruct((g.shape[0], 1) + g.shape[2:], g.dtype) for g in grads]
    return _tied_call(
        body, after, name=name,
        in_specs=[ANY] * n, out_specs=[ANY] * n, out_shape=out_shape,
        scratch_shapes=[pltpu.SemaphoreType.DMA((n,))] * 2,
        compiler_params=pltpu.CompilerParams(collective_id=SIBLING_COLLECTIVE_ID),
    )(*grads)


def _swap_start(grads, name, after=()):
    n, after = len(grads), _as_tuple(after)

    def body(*refs):
        outs = refs[2 * n + len(after):]
        sems, g_thru, l_thru, token = outs[:2 * n], outs[2 * n:3 * n], outs[3 * n:4 * n], outs[4 * n]
        _sibling_handshake()
        for cp in _swap_descriptors(g_thru, l_thru, lambda a: sems[a], lambda a: sems[n + a]):
            cp().start()
        token[...] = jnp.zeros_like(token)

    lands = [lax.empty((g.shape[0], 1) + g.shape[2:], g.dtype) for g in grads]
    held = [pltpu.with_memory_space_constraint(a, pltpu.HBM) for a in (*grads, *lands)]
    out = pl.pallas_call(
        body, name=name,
        out_shape=(*[pltpu.SemaphoreType.DMA(())] * (2 * n), *[pltpu.HBM(a.shape, a.dtype) for a in held],
                   jax.ShapeDtypeStruct((8, LANES), F32)),
        in_specs=[HBM_SPEC] * (2 * n) + [ANY] * len(after),
        out_specs=(*[SEM_SPEC] * (2 * n), *[HBM_SPEC] * (2 * n), pl.BlockSpec(memory_space=pltpu.VMEM)),
        input_output_aliases={i: 2 * n + i for i in range(2 * n)},
        compiler_params=pltpu.CompilerParams(has_side_effects=pltpu.SideEffectType.DATAFLOW_SIDE_EFFECTING,
                                             collective_id=SIBLING_COLLECTIVE_ID),
    )(*held, *after)
    return list(out[:2 * n]), list(out[2 * n:3 * n]), list(out[3 * n:4 * n]), out[4 * n]


def _swap_wait(sems, grads, lands, after, name):
    n = len(grads)

    def body(*refs):
        g_ref, l_ref = refs[:n], refs[n:2 * n]
        sem_ref = refs[2 * n:4 * n]
        for cp in _swap_descriptors(g_ref, l_ref, lambda a: sem_ref[a], lambda a: sem_ref[n + a]):
            cp().wait()

    out = pl.pallas_call(
        body, name=name,
        out_shape=tuple(pltpu.HBM(a.shape, a.dtype) for a in (*grads, *lands)),
        in_specs=[HBM_SPEC] * (2 * n) + [SEM_SPEC] * (2 * n) + [ANY] * len(_as_tuple(after)),
        out_specs=tuple([HBM_SPEC] * (2 * n)),
        input_output_aliases={i: i for i in range(2 * n)},
        compiler_params=pltpu.CompilerParams(has_side_effects=pltpu.SideEffectType.DATAFLOW_SIDE_EFFECTING),
    )(*grads, *lands, *sems, *_as_tuple(after))
    return list(out[:n]), list(out[n:])


def _add_halves(gs, gots, pos, name, dtypes):
    n = len(gs)
    j = gs[0].shape[0]

    def body(pos_ref, *refs):
        g_refs, r_refs = refs[:n], refs[n:2 * n]
        o_refs, p_refs = refs[2 * n:3 * n], refs[3 * n:]
        vals = [(g_refs[a][0, 0] + r_refs[a][0, 0]).astype(dtypes[a]) for a in range(n)]
        for a in range(n):
            o_refs[a][0] = vals[a]
        if j == 1:
            for a in range(n):
                p_refs[a][0] = vals[a]
        else:
            @pl.when(pl.program_id(0) == pos_ref[0])
            def _():
                for a in range(n):
                    p_refs[a][0] = vals[a]

    blk = lambda g: (1,) + g.shape[2:]
    out = pl.pallas_call(
        body, name=name,
        grid_spec=pltpu.PrefetchScalarGridSpec(
            num_scalar_prefetch=1, grid=(j,),
            in_specs=[pl.BlockSpec((1,) + blk(g), lambda i, p: (i, p[1], 0, 0)) for g in gs]
            + [pl.BlockSpec((1,) + blk(g), lambda i, p: (i, 0, 0, 0)) for g in gs],
            out_specs=[pl.BlockSpec(blk(g), lambda i, p: (i, 0, 0)) for g in gs]
            + [pl.BlockSpec(blk(g), lambda i, p: (p[0], 0, 0)) for g in gs]),
        out_shape=[jax.ShapeDtypeStruct((j,) + g.shape[2:], dt) for g, dt in zip(gs, dtypes)]
        + [jax.ShapeDtypeStruct((N_CHIPS,) + g.shape[2:], dt) for g, dt in zip(gs, dtypes)],
        compiler_params=_params(("arbitrary",)),
    )(pos, *gs, *gots)
    return list(out[:n]), list(out[n:])


def _exchange_descriptors(sums, parts, send_of, recv_of):
    x, y, c = _mesh_pos()
    me = 2 * x + y
    chips = _other_chips(x, y)
    sends, arrivals = [], []
    for a in range(len(sums)):
        for k in range(3):
            ck = 2 * chips[k][0] + chips[k][1]
            mine = sums[a].at[ck] if sums[a].shape[0] == N_CHIPS else sums[a].at[0]

            def copy(dst_slot, a=a, k=k, mine=mine):
                return pltpu.make_async_remote_copy(
                    src_ref=mine, dst_ref=parts[a].at[dst_slot],
                    send_sem=send_of(a, k), recv_sem=recv_of(a, k),
                    device_id=(*chips[k], c), device_id_type=MESH)

            sends.append(functools.partial(copy, me))
            arrivals.append(functools.partial(copy, ck))
    return sends, arrivals


def _exchange_start(sums, parts, name, after=()):
    n = len(sums)
    ns = 3 * n

    def body(*refs):
        sems = refs[2 * n:2 * n + 2 * ns]
        sums_thru = refs[2 * n + 2 * ns:3 * n + 2 * ns]
        parts_thru = refs[3 * n + 2 * ns:4 * n + 2 * ns]
        token = refs[4 * n + 2 * ns]
        _chips_handshake()
        sends, _ = _exchange_descriptors(sums_thru, parts_thru, lambda a, k: sems[3 * a + k],
                                         lambda a, k: sems[ns + 3 * a + k])
        for cp in sends:
            cp().start()
        token[...] = jnp.zeros_like(token)

    hbm = lambda a: pltpu.HBM(a.shape, a.dtype)
    held = [pltpu.with_memory_space_constraint(a, pltpu.HBM) for a in (*sums, *parts)]
    out = _tied_call(
        body, after, name=name,
        out_shape=(*[pltpu.SemaphoreType.DMA(())] * (2 * ns), *[hbm(a) for a in held],
                   jax.ShapeDtypeStruct((8, LANES), F32)),
        in_specs=[HBM_SPEC] * (2 * n),
        out_specs=(*[SEM_SPEC] * (2 * ns), *[HBM_SPEC] * (2 * n), pl.BlockSpec(memory_space=pltpu.VMEM)),
        input_output_aliases={i: 2 * ns + i for i in range(2 * n)},
        compiler_params=pltpu.CompilerParams(has_side_effects=pltpu.SideEffectType.DATAFLOW_SIDE_EFFECTING,
                                             collective_id=CHIPS_COLLECTIVE_ID),
    )(*held)
    return (list(out[:2 * ns]), list(out[2 * ns:2 * ns + n]), list(out[2 * ns + n:2 * ns + 2 * n]),
            out[2 * ns + 2 * n])


def _exchange_wait(sems, sums, parts, after, name):
    n = len(sums)
    ns = 3 * n

    def body(*refs):
        sums_ref, parts_ref = refs[:n], refs[n:2 * n]
        sem_ref = refs[2 * n:2 * n + 2 * ns]
        sends, arrivals = _exchange_descriptors(sums_ref, parts_ref, lambda a, k: sem_ref[3 * a + k],
                                                lambda a, k: sem_ref[ns + 3 * a + k])
        for cp in sends:
            cp().wait_send()
        for cp in arrivals:
            cp().wait_recv()

    hbm = lambda a: pltpu.HBM(a.shape, a.dtype)
    out = pl.pallas_call(
        body, name=name,
        out_shape=tuple(hbm(a) for a in (*sums, *parts)),
        in_specs=[HBM_SPEC] * (2 * n) + [SEM_SPEC] * (2 * ns) + [ANY] * len(_as_tuple(after)),
        out_specs=tuple([HBM_SPEC] * (2 * n)),
        input_output_aliases={i: i for i in range(2 * n)},
        compiler_params=pltpu.CompilerParams(has_side_effects=pltpu.SideEffectType.DATAFLOW_SIDE_EFFECTING),
    )(*sums, *parts, *sems, *_as_tuple(after))
    return list(out[n:])


def _sum_chips(parts, pos, name, after=()):
    n = len(parts)
    after = _as_tuple(after)

    def body(pos_ref, *refs):
        outs = refs[n + len(after):]
        for a in range(n):
            p_ref = refs[a]
            outs[a][0] = (((p_ref[0].astype(F32) + p_ref[1].astype(F32)) + p_ref[2].astype(F32))
                          + p_ref[3].astype(F32))

    out = pl.pallas_call(
        body, name=name,
        grid_spec=pltpu.PrefetchScalarGridSpec(
            num_scalar_prefetch=1, grid=(1,),
            in_specs=[pl.BlockSpec(p.shape, lambda i, q: (0, 0, 0)) for p in parts] + [ANY] * len(after),
            out_specs=[pl.BlockSpec((1,) + p.shape[1:], lambda i, q: (q[1], 0, 0)) for p in parts]),
        out_shape=[jax.ShapeDtypeStruct((2,) + p.shape[1:], F32) for p in parts],
        compiler_params=_params(("arbitrary",)),
    )(pos, *parts, *after)
    return list(out)


def _join_halves(fulls, name, after=()):
    n = len(fulls)

    def body(*refs):
        outs = refs[n:2 * n]
        send_sem, recv_sem = refs[2 * n:]
        x, y, c = _mesh_pos()
        _sibling_handshake()

        def half(a, which):
            return pltpu.make_async_remote_copy(
                src_ref=outs[a].at[which], dst_ref=outs[a].at[which],
                send_sem=send_sem.at[a], recv_sem=recv_sem.at[a],
                device_id=(x, y, 1 - c), device_id_type=MESH)

        sends = [half(a, c) for a in range(n)]
        for cp in sends:
            cp.start()
        for a in range(n):
            half(a, 1 - c).wait_recv()
        for cp in sends:
            cp.wait_send()

    out_shape = [jax.ShapeDtypeStruct(f.shape, f.dtype) for f in fulls]
    return _tied_call(
        body, after, name=name,
        in_specs=[ANY] * n, out_specs=[ANY] * n, out_shape=out_shape,
        input_output_aliases={a: a for a in range(n)},
        scratch_shapes=[pltpu.SemaphoreType.DMA((n,))] * 2,
        compiler_params=pltpu.CompilerParams(collective_id=SIBLING_COLLECTIVE_ID),
    )(*fulls)


def _norm_in(x, g, ts, after=()):
    s = x.shape[0]

    def body(x_ref, g_ref, hn_ref):
        xv = x_ref[...]
        hn_ref[...] = (xv * _rms_stats(xv) * g_ref[...]).astype(BF16)

    row = pl.BlockSpec((ts, D_MODEL), lambda i: (i, 0))
    return _tied_call(
        body, after, name="norm_in", grid=(s // ts,),
        in_specs=[row, pl.BlockSpec((1, D_MODEL), lambda i: (0, 0))], out_specs=row,
        out_shape=jax.ShapeDtypeStruct((s, D_MODEL), BF16),
        compiler_params=_params(("parallel",)),
    )(x, g)


def _shift_rows(buf, shifted, t):
    rows = t + CONV_HALO - SUBLANES
    for r in range(1, SUBLANES):
        shifted[r - 1, 0:rows, :] = buf[pl.ds(r, rows), :]


def _window(buf, shifted, offset, t):
    r = offset % SUBLANES
    if r == 0:
        return buf[pl.ds(offset, t), :]
    return shifted[r - 1, pl.ds(offset - r, t), :]


def _lane_is_low_head():
    lane = lax.broadcasted_iota(jnp.int32, (1, GM_WIDTH), 1)
    return (lane & GM_HEAD_DIM) == 0


def _gm_mix(v_lo, v_hi, wpair_ref, bias_ref, mixed_ref, t):
    for n in range(t // CHUNK):
        rows = slice(n * CHUNK, (n + 1) * CHUNK)
        for j in range(GM_HEADS // 2):
            cols = slice(j * LANES, (j + 1) * LANES)
            rhs = jnp.concatenate([v_lo[rows, cols], v_hi[rows, cols]], axis=0)
            mixed_ref[rows, cols] = _dot(wpair_ref[j], rhs) + bias_ref[:, cols]


def _seqmix_fwd(hn, w_in, b_in, cw, cb, lng, lnb, gg, gb, wpair, bias, t, after=()):
    s = hn.shape[0]

    def body(hn_ref, w_ref, b_ref, cw_ref, cb_ref, lng_ref, lnb_ref, gg_ref, gb_ref, wpair_ref, bias_ref,
             z_ref, mix_ref, c1_ref, abuf, ash, mixed_ref):
        i = pl.program_id(0)

        @pl.when(i == 0)
        def _():
            abuf[0:CONV_HALO, :] = jnp.zeros((CONV_HALO, CONV_WIDTH), F32)

        @pl.when(i > 0)
        def _():
            abuf[0:CONV_HALO, :] = abuf[t:t + CONV_HALO, :]

        hv = hn_ref[...]
        for j in range(4):
            cols = slice(j * 512, (j + 1) * 512)
            z_ref[:, cols] = _dot(hv, w_ref[j]) + b_ref[:, cols]

        abuf[CONV_HALO:, :] = z_ref[:, 0:512] * _sigmoid(z_ref[:, 512:1024])
        _shift_rows(abuf, ash, t)
        acc = jnp.zeros((t, CONV_WIDTH), F32)
        for k in range(CONV_KERNEL):
            acc = acc + cw_ref[k:k + 1, :] * _window(abuf, ash, CONV_HALO - (CONV_KERNEL - 1) + k, t)
        c1 = acc + cb_ref[...]
        c1_ref[...] = c1
        xh, _ = _ln_stats(c1)
        ln = xh * lng_ref[...] + lnb_ref[...]
        mix_ref[:, 0:512] = (ln * _sigmoid(ln)).astype(BF16)

        u, _ = _gelu_parts(z_ref[:, 1024:1536])
        gv, _ = _gelu_parts(z_ref[:, 1536:2048])
        vxh, _ = _ln_stats(gv)
        v = vxh * gg_ref[...] + gb_ref[...]
        low = _lane_is_low_head()
        v_lo = jnp.where(low, v, 0.0).astype(BF16)
        v_hi = jnp.where(low, 0.0, v).astype(BF16)
        _gm_mix(v_lo, v_hi, wpair_ref, bias_ref, mixed_ref, t)
        mix_ref[:, 512:1024] = (u * mixed_ref[...]).astype(BF16)

    vec = lambda n: pl.BlockSpec((1, n), lambda i: (0, 0))
    return _tied_call(
        body, after, name="seqmix_fwd", grid=(s // t,),
        in_specs=[pl.BlockSpec((t, D_MODEL), lambda i: (i, 0)),
                  pl.BlockSpec((4, D_MODEL, 512), lambda i: (0, 0, 0)), vec(2048),
                  pl.BlockSpec((CONV_HALO, CONV_WIDTH), lambda i: (0, 0)),
                  vec(512), vec(512), vec(512), vec(512), vec(512),
                  pl.BlockSpec((4, CHUNK, 2 * CHUNK), lambda i: (0, 0, 0)),
                  pl.BlockSpec((CHUNK, GM_WIDTH), lambda i: (0, 0))],
        out_specs=[pl.BlockSpec((t, 2048), lambda i: (i, 0)),
                   pl.BlockSpec((t, D_MODEL), lambda i: (i, 0)),
                   pl.BlockSpec((t, CONV_WIDTH), lambda i: (i, 0))],
        out_shape=[jax.ShapeDtypeStruct((s, 2048), F32), jax.ShapeDtypeStruct((s, D_MODEL), BF16),
                   jax.ShapeDtypeStruct((s, CONV_WIDTH), F32)],
        scratch_shapes=[pltpu.VMEM((t + CONV_HALO, CONV_WIDTH), F32),
                        pltpu.VMEM((SUBLANES - 1, t + CONV_HALO - SUBLANES, CONV_WIDTH), F32),
                        pltpu.VMEM((t, GM_WIDTH), F32)],
        compiler_params=_params(("arbitrary",)),
    )(hn, w_in, b_in, cw, cb, lng, lnb, gg, gb, wpair, bias)


def _mem_kv(mem, g, wkv):
    m = mem.shape[0]

    def body(mem_ref, g_ref, w_ref, mn_ref, kv_ref):
        mv = mem_ref[...]
        mn = (mv * _rms_stats(mv) * g_ref[...]).astype(BF16)
        mn_ref[...] = mn
        for j in range(4):
            kv_ref[:, j * 512:(j + 1) * 512] = _dot(mn, w_ref[j]).astype(BF16)

    return pl.pallas_call(
        body, name="mem_kv",
        out_shape=[jax.ShapeDtypeStruct((m, D_MODEL), BF16), jax.ShapeDtypeStruct((m, 2 * D_MODEL), BF16)],
        compiler_params=pltpu.CompilerParams(vmem_limit_bytes=VMEM_LIMIT_BYTES),
    )(mem, g, wkv)


def _softmax_rows(sc):
    e = jnp.exp(sc - jnp.max(sc, axis=-1, keepdims=True))
    return e / jnp.sum(e, axis=-1, keepdims=True)


def _attn_block_fwd(x, mix, w_out, g_xa, wq, kv, wo, g_ffn, ts, after=()):
    s, m = x.shape[0], kv.shape[0]
    scale = XA_HEAD_DIM ** -0.5

    def body(x_ref, mix_ref, wout_ref, gxa_ref, wq_ref, kv_ref, wo_ref, gffn_ref,
             h1_ref, hn2_ref, q_ref, o_ref, h2_ref, hn3_ref):
        h1 = x_ref[...] + _dot(mix_ref[...], wout_ref[...])
        h1_ref[...] = h1
        hn2 = (h1 * _rms_stats(h1) * gxa_ref[...]).astype(BF16)
        hn2_ref[...] = hn2
        q_ref[...] = _dot(hn2, wq_ref[...]).astype(BF16)
        for h in range(XA_HEADS):
            cols = slice(h * XA_HEAD_DIM, (h + 1) * XA_HEAD_DIM)
            vcols = slice(D_MODEL + h * XA_HEAD_DIM, D_MODEL + (h + 1) * XA_HEAD_DIM)
            p = _softmax_rows(_dot_nt(q_ref[:, cols], kv_ref[:, cols]) * scale)
            o_ref[:, cols] = _dot(p.astype(BF16), kv_ref[:, vcols]).astype(BF16)
        h2 = h1 + _dot(o_ref[...], wo_ref[...])
        h2_ref[...] = h2
        hn3_ref[...] = (h2 * _rms_stats(h2) * gffn_ref[...]).astype(BF16)

    row = pl.BlockSpec((ts, D_MODEL), lambda i: (i, 0))
    full = pl.BlockSpec((D_MODEL, D_MODEL), lambda i: (0, 0))
    vec = pl.BlockSpec((1, D_MODEL), lambda i: (0, 0))
    f32 = jax.ShapeDtypeStruct((s, D_MODEL), F32)
    bf16 = jax.ShapeDtypeStruct((s, D_MODEL), BF16)
    return _tied_call(
        body, after, name="attn_block_fwd", grid=(s // ts,),
        in_specs=[row, row, full, vec, full, pl.BlockSpec((m, 2 * D_MODEL), lambda i: (0, 0)), full, vec],
        out_specs=[row] * 6,
        out_shape=[f32, bf16, bf16, bf16, f32, bf16],
        compiler_params=_params(("parallel",)),
    )(x, mix, w_out, g_xa, wq, kv, wo, g_ffn)


_FFN_CHUNKS = (slice(0, 8 * LANES), slice(8 * LANES, 16 * LANES), slice(16 * LANES, FFN_HIDDEN))


def _ffn_up(hn, wgu, ts, after=()):
    s = hn.shape[0]

    def body(hn_ref, w_ref, gu_ref, act_ref):
        hv = hn_ref[...]
        for cols in _FFN_CHUNKS:
            gate = _dot(hv, w_ref[0, :, cols])
            up = _dot(hv, w_ref[1, :, cols])
            gu_ref[0, :, cols] = gate.astype(BF16)
            gu_ref[1, :, cols] = up.astype(BF16)
            act_ref[:, cols] = (gate * _sigmoid(gate) * up).astype(BF16)

    return _tied_call(
        body, after, name="ffn_up", grid=(s // ts,),
        in_specs=[pl.BlockSpec((ts, D_MODEL), lambda i: (i, 0)),
                  pl.BlockSpec((2, D_MODEL, FFN_HIDDEN), lambda i: (0, 0, 0))],
        out_specs=[pl.BlockSpec((2, ts, FFN_HIDDEN), lambda i: (0, i, 0)),
                   pl.BlockSpec((ts, FFN_HIDDEN), lambda i: (i, 0))],
        out_shape=[jax.ShapeDtypeStruct((2, s, FFN_HIDDEN), BF16), jax.ShapeDtypeStruct((s, FFN_HIDDEN), BF16)],
        compiler_params=_params(("parallel",)),
    )(hn, wgu)


def _ffn_down_loss(act, wd, h2, g, target, ts):
    s = act.shape[0]

    def body(act_ref, wd_ref, h2_ref, g_ref, t_ref, dh_ref, dhb_ref, sq_ref, dg_ref):
        @pl.when(pl.program_id(0) == 0)
        def _():
            sq_ref[...] = jnp.zeros_like(sq_ref)
            dg_ref[...] = jnp.zeros_like(dg_ref)

        h3 = h2_ref[...] + _dot(act_ref[...], wd_ref[...])
        r = _rms_stats(h3)
        gv = g_ref[...]
        diff = h3 * r * gv - t_ref[...]
        sq_ref[...] += _rowsum(diff * diff)
        dh, dg = _rms_bwd(diff / D_MODEL, h3, r, gv)
        dh_ref[...] = dh
        dhb_ref[...] = dh.astype(BF16)
        dg_ref[...] += dg

    row = pl.BlockSpec((ts, D_MODEL), lambda i: (i, 0))
    vec = pl.BlockSpec((1, D_MODEL), lambda i: (0, 0))
    return pl.pallas_call(
        body, name="ffn_down_loss", grid=(s // ts,),
        in_specs=[pl.BlockSpec((ts, FFN_HIDDEN), lambda i: (i, 0)),
                  pl.BlockSpec((FFN_HIDDEN, D_MODEL), lambda i: (0, 0)), row, vec, row],
        out_specs=[row, row, vec, vec],
        out_shape=[jax.ShapeDtypeStruct((s, D_MODEL), F32), jax.ShapeDtypeStruct((s, D_MODEL), BF16),
                   jax.ShapeDtypeStruct((1, D_MODEL), F32), jax.ShapeDtypeStruct((1, D_MODEL), F32)],
        compiler_params=_params(("arbitrary",)),
    )(act, wd, h2, g, target)


def _grad_w(a, b, tk, tn, name, after=(), shards=1):
    s, k = a.shape
    gb, _, n = b.shape
    nblk = n // tn
    ws = tn // shards
    tsr = GRAD_ROWS if s % GRAD_ROWS == 0 else s

    def body(a_ref, b_ref, o_ref):
        part = _dot_tn(a_ref[...], b_ref[0])

        @pl.when(pl.program_id(2) == 0)
        def _():
            for j in range(shards):
                o_ref[j] = part[:, j * ws:(j + 1) * ws]

        @pl.when(pl.program_id(2) > 0)
        def _():
            for j in range(shards):
                o_ref[j] += part[:, j * ws:(j + 1) * ws]

    return _tied_call(
        body, after, name=name, grid=(gb * nblk, k // tk, s // tsr),
        in_specs=[pl.BlockSpec((tsr, tk), lambda ni, ki, si: (si, ki)),
                  pl.BlockSpec((1, tsr, tn), lambda ni, ki, si: (ni // nblk, si, ni % nblk))],
        out_specs=pl.BlockSpec((shards, tk, ws), lambda ni, ki, si: (ni, ki, 0)),
        out_shape=jax.ShapeDtypeStruct((gb * nblk * shards, k, ws), F32),
        compiler_params=_params(("parallel", "parallel", "arbitrary")),
    )(a, b)


def _grad_w_square(pairs, name, after=()):
    n = len(pairs)
    s = pairs[0][0].shape[0]
    tsr = GRAD_ROWS // 2 if s % (GRAD_ROWS // 2) == 0 else s

    def body(*refs):
        ins, outs = refs[:2 * n], refs[2 * n:]
        parts = [_dot_tn(ins[2 * a][...], ins[2 * a + 1][...]) for a in range(n)]

        @pl.when(pl.program_id(0) == 0)
        def _():
            for a in range(n):
                outs[a][...] = parts[a]

        @pl.when(pl.program_id(0) > 0)
        def _():
            for a in range(n):
                outs[a][...] += parts[a]

    row = pl.BlockSpec((tsr, D_MODEL), lambda i: (i, 0))
    return _tied_call(
        body, after, name=name, grid=(s // tsr,),
        in_specs=[row] * (2 * n), out_specs=[pl.BlockSpec((D_MODEL, D_MODEL), lambda i: (0, 0))] * n,
        out_shape=[jax.ShapeDtypeStruct((D_MODEL, D_MODEL), F32)] * n,
        compiler_params=_params(("arbitrary",)),
    )(*[x for p in pairs for x in p])


def _ffn_bwd(dh3, wd, gu, wgu, h2, g, t, after=()):
    s = dh3.shape[0]

    def body(dh3_ref, wd_ref, gu_ref, w_ref, h2_ref, g_ref, dgu_ref, dh2_ref, dh2b_ref, dg_ref):
        @pl.when(pl.program_id(0) == 0)
        def _():
            dg_ref[...] = jnp.zeros_like(dg_ref)

        dh3v = dh3_ref[...]
        dhb = dh3v.astype(BF16)
        for cols in _FFN_CHUNKS:
            dact = _dot_nt(dhb, wd_ref[cols, :])
            gate, up = gu_ref[0, :, cols].astype(F32), gu_ref[1, :, cols].astype(F32)
            sg = _sigmoid(gate)
            dgu_ref[0, :, cols] = (dact * up * (sg * (1.0 + gate * (1.0 - sg)))).astype(BF16)
            dgu_ref[1, :, cols] = (dact * (gate * sg)).astype(BF16)
        dhn = _dot_nt(dgu_ref[0], w_ref[0]) + _dot_nt(dgu_ref[1], w_ref[1])
        h2 = h2_ref[...]
        dv, dg = _rms_bwd(dhn, h2, _rms_stats(h2), g_ref[...])
        dh2 = dh3v + dv
        dh2_ref[...] = dh2
        dh2b_ref[...] = dh2.astype(BF16)
        dg_ref[...] += dg

    row = pl.BlockSpec((t, D_MODEL), lambda i: (i, 0))
    wide = pl.BlockSpec((2, t, FFN_HIDDEN), lambda i: (0, i, 0))
    vec = pl.BlockSpec((1, D_MODEL), lambda i: (0, 0))
    return _tied_call(
        body, after, name="ffn_bwd", grid=(s // t,),
        in_specs=[row, pl.BlockSpec((FFN_HIDDEN, D_MODEL), lambda i: (0, 0)), wide,
                  pl.BlockSpec((2, D_MODEL, FFN_HIDDEN), lambda i: (0, 0, 0)), row, vec],
        out_specs=[wide, row, row, vec],
        out_shape=[jax.ShapeDtypeStruct((2, s, FFN_HIDDEN), BF16), jax.ShapeDtypeStruct((s, D_MODEL), F32),
                   jax.ShapeDtypeStruct((s, D_MODEL), BF16), jax.ShapeDtypeStruct((1, D_MODEL), F32)],
        compiler_params=_params(("arbitrary",)),
    )(dh3, wd, gu, wgu, h2, g)


def _attn_bwd(dh2, wo, q, kv, wq, h1, g, ts, after=()):
    s, m = q.shape[0], kv.shape[0]
    scale = XA_HEAD_DIM ** -0.5

    def body(dh2_ref, wo_ref, q_ref, kv_ref, wq_ref, h1_ref, g_ref, dh1_ref, dh1b_ref, dq_ref, dkv_ref, dg_ref):
        @pl.when(pl.program_id(0) == 0)
        def _():
            dkv_ref[...] = jnp.zeros_like(dkv_ref)
            dg_ref[...] = jnp.zeros_like(dg_ref)

        do = _dot_nt(dh2_ref[...].astype(BF16), wo_ref[...]).astype(BF16)
        for h in range(XA_HEADS):
            cols = slice(h * XA_HEAD_DIM, (h + 1) * XA_HEAD_DIM)
            vcols = slice(D_MODEL + h * XA_HEAD_DIM, D_MODEL + (h + 1) * XA_HEAD_DIM)
            qh, kh, vh, doh = q_ref[:, cols], kv_ref[:, cols], kv_ref[:, vcols], do[:, cols]
            p = _softmax_rows(_dot_nt(qh, kh) * scale)
            dp = _dot_nt(doh, vh)
            ds = (p * (dp - jnp.sum(dp * p, axis=-1, keepdims=True)) * scale).astype(BF16)
            dq_ref[:, cols] = _dot(ds, kh).astype(BF16)
            dkv_ref[:, cols] += _dot_tn(ds, qh)
            dkv_ref[:, vcols] += _dot_tn(p.astype(BF16), doh)
        dhn = _dot_nt(dq_ref[...], wq_ref[...])
        h1 = h1_ref[...]
        dv, dg = _rms_bwd(dhn, h1, _rms_stats(h1), g_ref[...])
        dh1 = dh2_ref[...] + dv
        dh1_ref[...] = dh1
        dh1b_ref[...] = dh1.astype(BF16)
        dg_ref[...] += dg

    row = pl.BlockSpec((ts, D_MODEL), lambda i: (i, 0))
    full = pl.BlockSpec((D_MODEL, D_MODEL), lambda i: (0, 0))
    kvs = pl.BlockSpec((m, 2 * D_MODEL), lambda i: (0, 0))
    vec = pl.BlockSpec((1, D_MODEL), lambda i: (0, 0))
    return _tied_call(
        body, after, name="attn_bwd", grid=(s // ts,),
        in_specs=[row, full, row, kvs, full, row, vec],
        out_specs=[row, row, row, kvs, vec],
        out_shape=[jax.ShapeDtypeStruct((s, D_MODEL), F32), jax.ShapeDtypeStruct((s, D_MODEL), BF16),
                   jax.ShapeDtypeStruct((s, D_MODEL), BF16),
                   jax.ShapeDtypeStruct((m, 2 * D_MODEL), F32), jax.ShapeDtypeStruct((1, D_MODEL), F32)],
        compiler_params=_params(("arbitrary",)),
    )(dh2, wo, q, kv, wq, h1, g)


def _mem_kv_bwd(dkv, mn, wkv, mem, g, after=()):
    m = mem.shape[0]

    def body(dkv_ref, mn_ref, w_ref, mem_ref, g_ref, dw_ref, dg_ref):
        dmn = jnp.zeros((m, D_MODEL), F32)
        mn = mn_ref[...]
        for j in range(4):
            dj = dkv_ref[:, j * 512:(j + 1) * 512].astype(BF16)
            dw_ref[j] = _dot_tn(mn, dj)
            dmn = dmn + _dot_nt(dj, w_ref[j])
        mv = mem_ref[...]
        dg_ref[...] = _rowsum(dmn * (mv * _rms_stats(mv)))

    return _tied_call(
        body, after, name="mem_kv_bwd", in_specs=[pl.BlockSpec(memory_space=pltpu.VMEM)] * 5,
        out_shape=[jax.ShapeDtypeStruct((4, D_MODEL, 512), F32), jax.ShapeDtypeStruct((1, D_MODEL), F32)],
        compiler_params=pltpu.CompilerParams(vmem_limit_bytes=VMEM_LIMIT_BYTES),
    )(dkv, mn, wkv, mem, g)


def _seqmix_bwd(dh1, x, z, c1, w_out, w_in, g_mix, cw, lng, lnb, gg, gb, wpair, wpair_t, bias, t, after=()):
    s = x.shape[0]
    nt = s // t

    def body(dh1_ref, x_ref, z_ref, c1_ref, wo_ref, wi_ref, gm_ref, cw_ref, lng_ref, lnb_ref,
             gg_ref, gb_ref, wpair_ref, wpt_ref, bias_ref,
             gx_ref, dz_ref, dcw_ref, dcb_ref, dlng_ref, dlnb_ref, dgg_ref, dgb_ref, dws_ref, dbs_ref,
             dbin_ref, dgm_ref, dbuf, dsh, mixed_ref, dv_ref):
        i = pl.program_id(0)
        accs = (dcw_ref, dcb_ref, dlng_ref, dlnb_ref, dgg_ref, dgb_ref, dws_ref, dbs_ref, dbin_ref, dgm_ref)

        @pl.when(i == 0)
        def _():
            for r in accs:
                r[...] = jnp.zeros_like(r)
            dbuf[t:t + CONV_HALO, :] = jnp.zeros((CONV_HALO, CONV_WIDTH), F32)

        @pl.when(i > 0)
        def _():
            dbuf[t:t + CONV_HALO, :] = dbuf[0:CONV_HALO, :]

        dmix = _dot_nt(dh1_ref[...].astype(BF16), wo_ref[...])

        xh, rs = _ln_stats(c1_ref[...])
        lng = lng_ref[...]
        ln = xh * lng + lnb_ref[...]
        sl = _sigmoid(ln)
        dln = dmix[:, 0:512] * (sl * (1.0 + ln * (1.0 - sl)))
        dc1, dg_ln, db_ln = _ln_bwd(dln, xh, rs, lng)
        dlng_ref[...] += dg_ln
        dlnb_ref[...] += db_ln
        dcb_ref[...] += _rowsum(dc1)
        dbuf[0:t, :] = dc1

        za = z_ref[:, 0:512]
        sg = _sigmoid(z_ref[:, 512:1024])
        a = za * sg
        _shift_rows(dbuf, dsh, t)

        da = jnp.zeros((t, CONV_WIDTH), F32)
        for k in range(CONV_KERNEL):
            later = _window(dbuf, dsh, CONV_KERNEL - 1 - k, t)
            da = da + cw_ref[k:k + 1, :] * later
            dcw_ref[k:k + 1, :] += _rowsum(a * later)
        dza = da * sg
        dzg = da * za * (sg * (1.0 - sg))
        dz_ref[:, 0:512] = dza.astype(BF16)
        dz_ref[:, 512:1024] = dzg.astype(BF16)
        dbin_ref[:, 0:512] += _rowsum(dza)
        dbin_ref[:, 512:1024] += _rowsum(dzg)

        dgm = dmix[:, 512:1024]
        u, du_dz = _gelu_parts(z_ref[:, 1024:1536])
        gv, dgv_dz = _gelu_parts(z_ref[:, 1536:2048])
        vxh, vrs = _ln_stats(gv)
        ggv = gg_ref[...]
        v = vxh * ggv + gb_ref[...]
        low = _lane_is_low_head()
        v_lo = jnp.where(low, v, 0.0).astype(BF16)
        v_hi = jnp.where(low, 0.0, v).astype(BF16)
        _gm_mix(v_lo, v_hi, wpair_ref, bias_ref, mixed_ref, t)
        dzu = dgm * mixed_ref[...] * du_dz
        dm = dgm * u
        dm_lo = jnp.where(low, dm, 0.0).astype(BF16)
        dm_hi = jnp.where(low, 0.0, dm).astype(BF16)
        vb = v.astype(BF16)
        tril = (lax.broadcasted_iota(jnp.int32, (CHUNK, CHUNK), 1)
                <= lax.broadcasted_iota(jnp.int32, (CHUNK, CHUNK), 0))
        for n in range(t // CHUNK):
            rows = slice(n * CHUNK, (n + 1) * CHUNK)
            dbs_ref[...] += dm[rows, :]
            for j in range(GM_HEADS // 2):
                cols = slice(j * LANES, (j + 1) * LANES)
                stack = jnp.concatenate([dm_lo[rows, cols], dm_hi[rows, cols]], axis=0)
                dws = _dot_nt(stack, vb[rows, cols])
                dws_ref[2 * j] += jnp.where(tril, dws[0:CHUNK], 0.0)
                dws_ref[2 * j + 1] += jnp.where(tril, dws[CHUNK:2 * CHUNK], 0.0)
                dv_ref[rows, cols] = _dot(wpt_ref[j], stack)
        dgv, dg_gm, db_gm = _ln_bwd(dv_ref[...], vxh, vrs, ggv)
        dgg_ref[...] += dg_gm
        dgb_ref[...] += db_gm
        dzv = dgv * dgv_dz
        dz_ref[:, 1024:1536] = dzu.astype(BF16)
        dz_ref[:, 1536:2048] = dzv.astype(BF16)
        dbin_ref[:, 1024:1536] += _rowsum(dzu)
        dbin_ref[:, 1536:2048] += _rowsum(dzv)

        dhn = jnp.zeros((t, D_MODEL), F32)
        for j in range(4):
            dhn = dhn + _dot_nt(dz_ref[:, j * 512:(j + 1) * 512], wi_ref[j])
        xv = x_ref[...]
        dv, dg = _rms_bwd(dhn, xv, _rms_stats(xv), gm_ref[...])
        gx_ref[...] = dh1_ref[...] + dv
        dgm_ref[...] += dg

    rev = lambda w: pl.BlockSpec((t, w), lambda i: (nt - 1 - i, 0))
    const = lambda *shape: pl.BlockSpec(shape, lambda i: (0,) * len(shape))
    f32 = lambda *shape: jax.ShapeDtypeStruct(shape, F32)
    return _tied_call(
        body, after, name="seqmix_bwd", grid=(nt,),
        in_specs=[rev(D_MODEL), rev(D_MODEL), rev(2048), rev(CONV_WIDTH),
                  const(D_MODEL, D_MODEL), const(4, D_MODEL, 512), const(1, D_MODEL),
                  const(CONV_HALO, CONV_WIDTH), const(1, 512), const(1, 512), const(1, 512), const(1, 512),
                  const(4, CHUNK, 2 * CHUNK), const(4, CHUNK, 2 * CHUNK), const(CHUNK, GM_WIDTH)],
        out_specs=[rev(D_MODEL), rev(2048),
                   const(CONV_HALO, CONV_WIDTH), const(1, 512), const(1, 512), const(1, 512), const(1, 512),
                   const(1, 512), const(GM_HEADS, CHUNK, CHUNK), const(CHUNK, GM_WIDTH), const(1, 2048),
                   const(1, D_MODEL)],
        out_shape=[f32(s, D_MODEL), jax.ShapeDtypeStruct((s, 2048), BF16),
                   f32(CONV_HALO, CONV_WIDTH), f32(1, 512), f32(1, 512), f32(1, 512), f32(1, 512),
                   f32(1, 512), f32(GM_HEADS, CHUNK, CHUNK), f32(CHUNK, GM_WIDTH), f32(1, 2048),
                   f32(1, D_MODEL)],
        scratch_shapes=[pltpu.VMEM((t + CONV_HALO, CONV_WIDTH), F32),
                        pltpu.VMEM((SUBLANES - 1, t + CONV_HALO - SUBLANES, CONV_WIDTH), F32),
                        pltpu.VMEM((t, GM_WIDTH), F32), pltpu.VMEM((t, GM_WIDTH), F32)],
        compiler_params=_params(("arbitrary",)),
    )(dh1, x, z, c1, w_out, w_in, g_mix, cw, lng, lnb, gg, gb, wpair, wpair_t, bias)


def _head_bias_grad(dbs):
    def body(d_ref, o_ref):
        dv = d_ref[...]
        lane = lax.broadcasted_iota(jnp.int32, (CHUNK, LANES), 1)
        acc = jnp.zeros((CHUNK, LANES), F32)
        for h in range(GM_HEADS):
            sh = jnp.sum(dv[:, h * GM_HEAD_DIM:(h + 1) * GM_HEAD_DIM], axis=-1, keepdims=True)
            acc = acc + jnp.where(lane == h, sh, 0.0)
        o_ref[...] = acc

    return pl.pallas_call(body, name="head_bias_grad",
                          out_shape=jax.ShapeDtypeStruct((CHUNK, LANES), F32))(dbs)


def kernel(x, mem, norm_mix_g, w_in, b_in, conv_w, conv_b, conv_ln_g, conv_ln_b, gm_ln_g, gm_ln_b, gm_w_s, gm_b_s, w_out, norm_xa_g, mem_norm_g, xa_wq, xa_wkv, xa_wo, norm_ffn_g, ffn_w_gate_up, ffn_w_down, final_norm_g, loss_target, m_norm_mix_g, m_w_in, m_b_in, m_conv_w, m_conv_b, m_conv_ln_g, m_conv_ln_b, m_gm_ln_g, m_gm_ln_b, m_gm_w_s, m_gm_b_s, m_w_out, m_norm_xa_g, m_mem_norm_g, m_xa_wq, m_xa_wkv, m_xa_wo, m_norm_ffn_g, m_ffn_w_gate_up, m_ffn_w_down, m_final_norm_g, v_norm_mix_g, v_w_in, v_b_in, v_conv_w, v_conv_b, v_conv_ln_g, v_conv_ln_b, v_gm_ln_g, v_gm_ln_b, v_gm_w_s, v_gm_b_s, v_w_out, v_norm_xa_g, v_mem_norm_g, v_xa_wq, v_xa_wkv, v_xa_wo, v_norm_ffn_g, v_ffn_w_gate_up, v_ffn_w_down, v_final_norm_g):
    weights = dict(norm_mix_g=norm_mix_g, w_in=w_in, b_in=b_in, conv_w=conv_w, conv_b=conv_b, conv_ln_g=conv_ln_g,
                   conv_ln_b=conv_ln_b, gm_ln_g=gm_ln_g, gm_ln_b=gm_ln_b, gm_w_s=gm_w_s, gm_b_s=gm_b_s, w_out=w_out,
                   norm_xa_g=norm_xa_g, mem_norm_g=mem_norm_g, xa_wq=xa_wq, xa_wkv=xa_wkv, xa_wo=xa_wo,
                   norm_ffn_g=norm_ffn_g, ffn_w_gate_up=ffn_w_gate_up, ffn_w_down=ffn_w_down,
                   final_norm_g=final_norm_g)
    m_in = dict(norm_mix_g=m_norm_mix_g, w_in=m_w_in, b_in=m_b_in, conv_w=m_conv_w, conv_b=m_conv_b,
                conv_ln_g=m_conv_ln_g, conv_ln_b=m_conv_ln_b, gm_ln_g=m_gm_ln_g, gm_ln_b=m_gm_ln_b, gm_w_s=m_gm_w_s,
                gm_b_s=m_gm_b_s, w_out=m_w_out, norm_xa_g=m_norm_xa_g, mem_norm_g=m_mem_norm_g, xa_wq=m_xa_wq,
                xa_wkv=m_xa_wkv, xa_wo=m_xa_wo, norm_ffn_g=m_norm_ffn_g, ffn_w_gate_up=m_ffn_w_gate_up,
                ffn_w_down=m_ffn_w_down, final_norm_g=m_final_norm_g)
    v_in = dict(norm_mix_g=v_norm_mix_g, w_in=v_w_in, b_in=v_b_in, conv_w=v_conv_w, conv_b=v_conv_b,
                conv_ln_g=v_conv_ln_g, conv_ln_b=v_conv_ln_b, gm_ln_g=v_gm_ln_g, gm_ln_b=v_gm_ln_b, gm_w_s=v_gm_w_s,
                gm_b_s=v_gm_b_s, w_out=v_w_out, norm_xa_g=v_norm_xa_g, mem_norm_g=v_mem_norm_g, xa_wq=v_xa_wq,
                xa_wkv=v_xa_wkv, xa_wo=v_xa_wo, norm_ffn_g=v_norm_ffn_g, ffn_w_gate_up=v_ffn_w_gate_up,
                ffn_w_down=v_ffn_w_down, final_norm_g=v_final_norm_g)
    grads, delta, new_m, new_v = {}, {}, {}, {}

    s = x.shape[1]
    ts = _row_tile(s)
    tb = max(CHUNK, ts // 2)
    tw = 2 * ts if s % (2 * ts) == 0 and ts >= 512 else ts
    cx, cy, cc = _mesh_pos()
    chip = 2 * cx + cy
    pos = jnp.stack([chip, cc]).astype(jnp.int32)
    row = lambda a: a.reshape(1, -1)
    x2, mem2, tgt2 = x[0], mem[0], loss_target[0]

    big = dict(w_in=w_in, xa_wkv=xa_wkv, w_out=w_out, xa_wq=xa_wq, xa_wo=xa_wo,
               ffn_w_gate_up=ffn_w_gate_up, ffn_w_down=ffn_w_down)
    big_names = list(big)
    halves = lambda a: a.reshape(2, a.shape[0] // 2, a.shape[1])
    conv_w_pad = jnp.pad(conv_w, ((0, CONV_HALO - CONV_KERNEL), (0, 0)))
    first_names = ["w_in", "conv_w"]
    later_names = [nm for nm in big_names if nm != "w_in"]
    cast = dict(zip(first_names, _cast_into_slots([halves(w_in), halves(conv_w_pad)], pos, [BF16, F32], "cast_w_in")))
    cast.update(zip(later_names, _cast_into_slots([halves(big[nm]) for nm in later_names], pos,
                                                  [BF16] * len(later_names), "cast_" + later_names[0],
                                                  side_by_side=(later_names.index("ffn_w_gate_up"),))))

    def start_gather(names, after):
        return _gather_start([cast[nm] for nm in names], "gather_start_" + names[0], after)

    def land_gather(names, started, after):
        send_sems, recv_sems, bufs, _ = started
        return _gather_wait(send_sems, recv_sems, bufs, after, "gather_wait_" + names[0])

    def share_gather(names, landed, after=()):
        return dict(zip(names, (b.reshape(b.shape[0], -1, b.shape[-1])
                                for b in _pass_to_sibling(landed, "pass_" + names[0], after))))

    tril = jnp.tril(jnp.ones((CHUNK, CHUNK), dtype=bool))
    ws = jnp.where(tril[None], gm_w_s, 0.0)
    wpair = jnp.concatenate([ws[0::2], ws[1::2]], axis=2).astype(BF16)
    ws_t = jnp.swapaxes(ws, 1, 2)
    wpair_t = jnp.concatenate([ws_t[0::2], ws_t[1::2]], axis=2).astype(BF16)
    bias = jnp.repeat(gm_b_s.T, GM_HEAD_DIM, axis=1)

    attn_names = ["w_out", "xa_wq", "xa_wkv", "xa_wo"]
    gather_first = start_gather(first_names, ())
    hn1 = _norm_in(x2, row(norm_mix_g), tw, after=(gather_first[3], wpair, wpair_t, bias))
    landed = land_gather(first_names, gather_first, [cast[nm] for nm in later_names] + [hn1])
    gather_attn = start_gather(attn_names, landed)
    gw = share_gather(first_names, landed, gather_attn[3])
    w_in_g = gw["w_in"]
    cw_g = jnp.concatenate([gw["conv_w"][k] for k in range(N_CHIPS)], axis=1)

    z, mix, c1 = _seqmix_fwd(hn1, w_in_g, row(b_in), cw_g, row(conv_b), row(conv_ln_g), row(conv_ln_b),
                             row(gm_ln_g), row(gm_ln_b), wpair, bias, ts)
    landed = land_gather(attn_names, gather_attn, mix)
    gather_gu = start_gather(["ffn_w_gate_up"], landed)
    gw = share_gather(attn_names, landed, gather_gu[3])
    w_out_g = gw["w_out"].reshape(D_MODEL, D_MODEL)
    wq_g = gw["xa_wq"].reshape(D_MODEL, D_MODEL)
    wkv_g = gw["xa_wkv"]
    wo_g = gw["xa_wo"].reshape(D_MODEL, D_MODEL)
    mn, kv = _mem_kv(mem2, row(mem_norm_g), wkv_g)
    h1, hn2, q, o, h2, hn3 = _attn_block_fwd(x2, mix, w_out_g, row(norm_xa_g), wq_g, kv, wo_g, row(norm_ffn_g), ts)
    landed = land_gather(["ffn_w_gate_up"], gather_gu, hn3)
    gather_down = start_gather(["ffn_w_down"], landed)
    wgu_g = share_gather(["ffn_w_gate_up"], landed, gather_down[3])["ffn_w_gate_up"]
    gu, act = _ffn_up(hn3, wgu_g, ts)
    landed = land_gather(["ffn_w_down"], gather_down, act)
    wd_g = share_gather(["ffn_w_down"], landed)["ffn_w_down"].reshape(FFN_HIDDEN, D_MODEL)
    dh3, dh3_b, sq, d_final_g = _ffn_down_loss(act, wd_g, h2, row(final_norm_g), tgt2, ts)
    loss_here = jnp.broadcast_to(0.5 * jnp.sum(sq) / D_MODEL, (1, 2, SUBLANES, LANES))

    def split(g, nm):
        r, c = big[nm].shape
        return g.reshape(N_CHIPS, 2, r // 2, c)

    def chip_sums(group, arrays, got):
        sums, parts = [None] * len(group), [None] * len(group)
        for blocks in (N_CHIPS, 1):
            idx = [i for i, a in enumerate(arrays) if a.shape[0] == blocks]
            if idx:
                out = _add_halves([arrays[i] for i in idx], [got[i] for i in idx], pos, "chip_sum_" + group[idx[0]],
                                  [F32 if group[i] == "loss" else BF16 for i in idx])
                for k, i in enumerate(idx):
                    sums[i], parts[i] = out[0][k], out[1][k]
        return sums, parts

    def start_swap(group, grads):
        return _swap_start([split(g, nm) for g, nm in zip(grads, group)], "swap_start_" + group[0])

    def start_exchange(group, swapping, after, landed):
        arrays, got = _swap_wait(*swapping[:3], after, "swap_wait_" + group[0])
        sums, parts = chip_sums(group, arrays, got)
        return _exchange_start(sums, parts, "exchange_start_" + group[0], landed)

    def wait_exchange(group, started, after):
        sems, sums, parts, _ = started
        return _exchange_wait(sems, sums, parts, after, "exchange_wait_" + group[0])

    def finish_exchange(group, started, after):
        return _sum_chips(wait_exchange(group, started, after), pos, "total_" + group[0])

    def join(group, after):
        return _join_halves([halves_of[nm] for nm in group], "join_halves_" + group[0], after)

    def update(group, joined, after=()):
        outs = _adamw([(weights[nm], j.reshape(big[nm].shape), m_in[nm], v_in[nm]) for nm, j in zip(group, joined)],
                      "adamw_" + group[0], after)
        for nm, out in zip(group, outs):
            grads[nm], delta[nm], new_m[nm], new_v[nm] = out
        return [new_v[nm] for nm in group]

    def join_and_update(group, after):
        return update(group, join(group, after))

    as3 = lambda a: a.reshape((1,) + a.shape)
    halves_of = {}

    g_down = _grad_w(act, as3(dh3_b), FFN_HALF, D_MODEL, "grad_ffn_w_down")
    group_a = ["ffn_w_down"]
    swap_a = start_swap(group_a, [g_down])
    dgu, dh2, dh2_b, d_ffn_g = _ffn_bwd(dh3, wd_g, gu, wgu_g, h2, row(norm_ffn_g), tb,
                                        after=swap_a[3])
    exch_a = start_exchange(group_a, swap_a, dh2, wd_g)
    g_gu = _grad_w(hn3, dgu, D_MODEL, FFN_HALF, "grad_ffn_w_gate_up", after=exch_a[3])
    halves_of.update(zip(group_a, finish_exchange(group_a, exch_a, g_gu)))

    group_b = ["ffn_w_gate_up"]
    swap_b = start_swap(group_b, [g_gu])
    dh1, dh1_b, dq, dkv, d_xa_g = _attn_bwd(dh2, wo_g, q, kv, wq_g, h1, row(norm_xa_g), ts, after=swap_b[3])
    exch_b = start_exchange(group_b, swap_b, dh1, [halves_of[nm] for nm in group_a])
    g_wkv, d_mem_g = _mem_kv_bwd(dkv, mn, wkv_g, mem2, row(mem_norm_g), after=exch_b[3])
    g_wo, g_wq, g_wout = _grad_w_square([(o, dh2_b), (hn2, dq), (mix, dh1_b)], "grad_xa_wo", after=exch_b[3])
    done_a = join_and_update(group_a, (g_wkv, g_wo, g_wq, g_wout))
    halves_of.update(zip(group_b, finish_exchange(group_b, exch_b, done_a)))

    group_c = ["xa_wo", "xa_wq", "xa_wkv", "w_out"]
    swap_c = start_swap(group_c, [g_wo, g_wq, g_wkv, g_wout])
    (gx, dz, d_cw, d_cb, d_lng, d_lnb, d_gg, d_gb, d_ws, d_bs_sum, d_bin, d_mix_g) = _seqmix_bwd(
        dh1, x2, z, c1, w_out_g, w_in_g, row(norm_mix_g), cw_g, row(conv_ln_g), row(conv_ln_b),
        row(gm_ln_g), row(gm_ln_b), wpair, wpair_t, bias, tb, after=swap_c[3])
    d_bs = _head_bias_grad(d_bs_sum)[:, :GM_HEADS].T
    exch_c = start_exchange(group_c, swap_c, dz, [halves_of[nm] for nm in group_b])
    g_win = _grad_w(hn1, as3(dz), D_MODEL, 1024, "grad_w_in", after=exch_c[3], shards=2)

    small_names = ["norm_mix_g", "b_in", "conv_w", "conv_b", "conv_ln_g", "conv_ln_b", "gm_ln_g", "gm_ln_b",
                   "gm_w_s", "gm_b_s", "norm_xa_g", "mem_norm_g", "norm_ffn_g", "final_norm_g"]
    d_cw_by_chip = jnp.swapaxes(d_cw.reshape(CONV_HALO, N_CHIPS, LANES), 0, 1).reshape(-1, LANES)
    small_grads = dict(norm_mix_g=d_mix_g, b_in=d_bin, conv_w=d_cw_by_chip, conv_b=d_cb, conv_ln_g=d_lng,
                       conv_ln_b=d_lnb, gm_ln_g=d_gg, gm_ln_b=d_gb, gm_w_s=d_ws, gm_b_s=d_bs, norm_xa_g=d_xa_g,
                       mem_norm_g=d_mem_g, norm_ffn_g=d_ffn_g, final_norm_g=d_final_g)

    def rows_form(a):
        a = a.reshape(-1, LANES)
        return jnp.pad(a, ((0, -a.shape[0] % SUBLANES), (0, 0)))

    pieces = [rows_form(small_grads[nm]) for nm in small_names]
    offsets, total = [], 0
    for p in pieces:
        offsets.append(total)
        total += p.shape[0]
    pack_rows = -(-total // 32) * 32
    small_pack = jnp.pad(jnp.concatenate(pieces, axis=0), ((0, pack_rows - total), (0, 0)))

    group_d = ["w_in", "small", "loss"]
    joined_b = join(group_b, g_win)
    swap_d = _swap_start([split(g_win, "w_in")], "swap_start_w_in", joined_b)
    done_b = update(group_b, joined_b, swap_d[3])
    small_d = [small_pack.reshape(1, 2, pack_rows // 2, LANES), loss_here]
    got_small = _swap_halves(small_d, "swap_halves_small", done_b)
    arrays_d, got_d = _swap_wait(*swap_d[:3], got_small, "swap_wait_w_in")
    sums_d, parts_d = chip_sums(group_d, arrays_d + small_d, got_d + list(got_small))
    parts_c = wait_exchange(group_c, exch_c, sums_d)
    exch_d = _exchange_start(sums_d, parts_d, "exchange_start_w_in", parts_c)
    halves_of.update(zip(group_c, _sum_chips(parts_c, pos, "total_xa_wo", exch_d[3])))
    done_c = join_and_update(group_c, exch_d[3])
    halves_of.update(zip(group_d, finish_exchange(group_d, exch_d, done_c)))
    joined_d = _join_halves([halves_of[nm] for nm in group_d], "join_halves_w_in")
    loss = joined_d[2][0, 0, 0]
    grads["w_in"], delta["w_in"], new_m["w_in"], new_v["w_in"] = _adamw(
        [(w_in, joined_d[0].reshape(w_in.shape), m_w_in, v_w_in)], "adamw_w_in")[0]

    local_rows = lambda a, nm: a if nm == "conv_w" else a.reshape(-1, LANES)
    params = [tuple(local_rows(src[nm], nm) for src in (weights, m_in, v_in)) for nm in small_names]
    outs = _adamw_small(joined_d[1].reshape(pack_rows, LANES), pos, params, offsets, small_names.index("conv_w"))
    for k, nm in enumerate(small_names):
        for dst, a in zip((grads, delta, new_m, new_v), outs[4 * k:4 * k + 4]):
            dst[nm] = a

    order = ["norm_mix_g", "w_in", "b_in", "conv_w", "conv_b", "conv_ln_g", "conv_ln_b", "gm_ln_g", "gm_ln_b",
             "gm_w_s", "gm_b_s", "w_out", "norm_xa_g", "mem_norm_g", "xa_wq", "xa_wkv", "xa_wo", "norm_ffn_g",
             "ffn_w_gate_up", "ffn_w_down", "final_norm_g"]
    fit = lambda a, nm: a.reshape(weights[nm].shape)
    return (loss, gx.reshape(x.shape),
            *[fit(grads[nm], nm) for nm in order], *[fit(delta[nm], nm) for nm in order],
            *[fit(new_m[nm], nm) for nm in order], *[fit(new_v[nm], nm) for nm in order])
```

```python
import functools

import jax
import jax.numpy as jnp
from jax import lax
from jax.experimental import pallas as pl
from jax.experimental.pallas import tpu as pltpu

F32 = jnp.float32
BF16 = jnp.bfloat16

D_MODEL = 1024
CONV_WIDTH = 512
GM_WIDTH = 512
CONV_KERNEL = 31
CONV_HALO = 32
GRAD_ROWS = 2048
CHUNK = 128
GM_HEADS = 8
GM_HEAD_DIM = 64
XA_HEADS = 4
XA_HEAD_DIM = 256
FFN_HIDDEN = 2816
FFN_HALF = FFN_HIDDEN // 2
RMS_EPS = 1e-6
LN_EPS = 1e-5
N_CHIPS = 4
LANES = 128
SUBLANES = 8

ADAM_LR = 0.001
ADAM_B1 = 0.9
ADAM_B2 = 0.999
ADAM_EPS = 1e-08
ADAM_WD = 0.01
ADAM_STEP = 10

VMEM_LIMIT_BYTES = 56 * 1024 * 1024
MESH = pl.DeviceIdType.MESH
ANY = pl.BlockSpec(memory_space=pl.ANY)
HBM_SPEC = pl.BlockSpec(memory_space=pltpu.HBM)
SEM_SPEC = pl.BlockSpec(memory_space=pltpu.SEMAPHORE)

_NT = (((1,), (1,)), ((), ()))
_TN = (((0,), (0,)), ((), ()))
_GELU_C = 0.7978845608028654
_GELU_A = 0.044715


def _dot(a, b):
    return jnp.dot(a, b, preferred_element_type=F32)


def _dot_nt(a, b):
    return lax.dot_general(a, b, _NT, preferred_element_type=F32)


def _dot_tn(a, b):
    return lax.dot_general(a, b, _TN, preferred_element_type=F32)


def _mean(v):
    return jnp.mean(v, axis=-1, keepdims=True)


def _rowsum(v):
    return jnp.sum(v, axis=0, keepdims=True)


def _sigmoid(v):
    return 1.0 / (1.0 + jnp.exp(-v))


def _gelu_parts(v):
    v2 = v * v
    t = jnp.tanh(_GELU_C * (v + _GELU_A * v * v2))
    g = 0.5 * v * (1.0 + t)
    dg = 0.5 * (1.0 + t) + 0.5 * v * (1.0 - t * t) * (_GELU_C * (1.0 + 3.0 * _GELU_A * v2))
    return g, dg


def _rms_stats(v):
    return lax.rsqrt(_mean(v * v) + RMS_EPS)


def _rms_bwd(dy, v, r, g):
    n = v * r
    dn = dy * g
    dv = r * (dn - n * _mean(dn * n))
    return dv, _rowsum(dy * n)


def _ln_stats(v):
    mu = _mean(v)
    xc = v - mu
    rs = lax.rsqrt(_mean(xc * xc) + LN_EPS)
    return xc * rs, rs


def _ln_bwd(dy, xh, rs, g):
    dxh = dy * g
    dv = rs * (dxh - _mean(dxh) - xh * _mean(dxh * xh))
    return dv, _rowsum(dy * xh), _rowsum(dy)


def _params(sem):
    return pltpu.CompilerParams(dimension_semantics=sem, vmem_limit_bytes=VMEM_LIMIT_BYTES)


def _row_tile(s):
    return 512 if s % 512 == 0 and s >= 2048 else 128


def _mesh_pos():
    return lax.axis_index("x"), lax.axis_index("y"), lax.axis_index("c")


def _slot(buf, chip_idx, half):
    if buf.shape[0] == N_CHIPS:
        return buf.at[chip_idx, half]
    width = buf.shape[-1] // 2
    return buf.at[chip_idx // 2, half, :, pl.ds(pl.multiple_of((chip_idx % 2) * width, LANES), width)]


def _cast_into_slots(ws, pos, dtypes, name, side_by_side=()):
    n = len(ws)

    def body(pos_ref, *refs):
        for a in range(n):
            refs[n + a][0] = refs[a][...].astype(dtypes[a])

    def out_spec(a, w):
        if a in side_by_side:
            return pl.BlockSpec((1, 1) + w.shape[1:], lambda i, p: (p[0] // 2, i, 0, p[0] % 2))
        return pl.BlockSpec((1, 1) + w.shape[1:], lambda i, p: (p[0], i, 0, 0))

    def out_shape(a, w):
        if a in side_by_side:
            return (2, 2, w.shape[1], 2 * w.shape[2])
        return (N_CHIPS,) + w.shape

    return pl.pallas_call(
        body, name=name,
        grid_spec=pltpu.PrefetchScalarGridSpec(
            num_scalar_prefetch=1, grid=(2,),
            in_specs=[pl.BlockSpec((1,) + w.shape[1:], lambda i, p: (i, 0, 0)) for w in ws],
            out_specs=[out_spec(a, w) for a, w in enumerate(ws)]),
        out_shape=[jax.ShapeDtypeStruct(out_shape(a, w), dt) for a, (w, dt) in enumerate(zip(ws, dtypes))],
        compiler_params=_params(("parallel",)),
    )(pos, *ws)


def _adam_update(w, g, m, v):
    nm = ADAM_B1 * m + (1.0 - ADAM_B1) * g
    nv = ADAM_B2 * v + (1.0 - ADAM_B2) * (g * g)
    m_hat = nm / (1.0 - ADAM_B1 ** ADAM_STEP)
    v_hat = nv / (1.0 - ADAM_B2 ** ADAM_STEP)
    return -ADAM_LR * (m_hat / (jnp.sqrt(v_hat) + ADAM_EPS) + ADAM_WD * w), nm, nv


ADAM_STEPS = 4


def _adamw(quads, name, after=()):
    n = len(quads)

    def body(*refs):
        ins, outs = refs[:4 * n], refs[4 * n:]
        for a in range(n):
            w, g, m, v = (r[...] for r in ins[4 * a:4 * a + 4])
            outs[4 * a][...] = g
            outs[4 * a + 1][...], outs[4 * a + 2][...], outs[4 * a + 3][...] = _adam_update(w, g, m, v)

    specs = [pl.BlockSpec((q[0].shape[0] // ADAM_STEPS, q[0].shape[1]), lambda i: (i, 0)) for q in quads]
    out = _tied_call(
        body, after, name=name, grid=(ADAM_STEPS,),
        in_specs=[sp for sp in specs for _ in range(4)], out_specs=[sp for sp in specs for _ in range(4)],
        out_shape=[jax.ShapeDtypeStruct(q[0].shape, F32) for q in quads for _ in range(4)],
        compiler_params=_params(("parallel",)),
    )(*[a for q in quads for a in q])
    return [tuple(out[4 * a:4 * a + 4]) for a in range(n)]


def _adamw_small(gpack, pos, params, offsets, conv_at):
    n = len(params)

    def body(pos_ref, g_ref, *refs):
        ins, outs = refs[:3 * n], refs[3 * n:]
        for k in range(n):
            rows = params[k][0].shape[0]
            start = offsets[k]
            if k == conv_at:
                start = pl.multiple_of(start + pos_ref[0] * CONV_HALO, SUBLANES)
            g = g_ref[pl.ds(start, rows), :]
            outs[4 * k][...] = g
            outs[4 * k + 1][...], outs[4 * k + 2][...], outs[4 * k + 3][...] = _adam_update(
                ins[3 * k][...], g, ins[3 * k + 1][...], ins[3 * k + 2][...])

    flat = [a for p in params for a in p]
    vmem = pl.BlockSpec(memory_space=pltpu.VMEM)
    return pl.pallas_call(
        body, name="adamw_small",
        in_specs=[pl.BlockSpec(memory_space=pltpu.SMEM), vmem] + [vmem] * len(flat),
        out_specs=[vmem] * (4 * n),
        out_shape=[jax.ShapeDtypeStruct(p[0].shape, F32) for p in params for _ in range(4)],
    )(pos, gpack, *flat)


def _as_tuple(after):
    return tuple(after) if isinstance(after, (tuple, list)) else (after,)


def _tied_call(body, after, *, in_specs, **kwargs):
    after = _as_tuple(after)
    n_in, n_after = len(in_specs), len(after)

    def tied(*refs):
        body(*refs[:n_in], *refs[n_in + n_after:])

    call = pl.pallas_call(tied, in_specs=list(in_specs) + [ANY] * n_after, **kwargs)
    return lambda *operands: call(*operands, *after)


def _other_chips(x, y):
    return [(1 - x, y), (x, 1 - y), (1 - x, 1 - y)]


def _gather_descriptors(bufs, send_of, recv_of):
    x, y, c = _mesh_pos()
    me = 2 * x + y
    chips = _other_chips(x, y)
    sends, arrivals = [], []
    for a in range(len(bufs)):
        for k in range(3):
            ck = 2 * chips[k][0] + chips[k][1]

            def copy(slot, a=a, k=k):
                return pltpu.make_async_remote_copy(
                    src_ref=_slot(bufs[a], slot, c), dst_ref=_slot(bufs[a], slot, c),
                    send_sem=send_of(a, k), recv_sem=recv_of(a, k),
                    device_id=(*chips[k], c), device_id_type=MESH)

            sends.append(functools.partial(copy, me))
            arrivals.append(functools.partial(copy, ck))
    return sends, arrivals


def _gather_start(bufs, name, after=()):
    n = len(bufs)
    ns = 3 * n

    def body(*refs):
        sems = refs[n:n + 2 * ns]
        thru = refs[n + 2 * ns:2 * n + 2 * ns]
        token = refs[2 * n + 2 * ns]
        _chips_handshake()
        sends, _ = _gather_descriptors(thru, lambda a, k: sems[3 * a + k], lambda a, k: sems[ns + 3 * a + k])
        for cp in sends:
            cp().start()
        token[...] = jnp.zeros_like(token)

    held = [pltpu.with_memory_space_constraint(b, pltpu.HBM) for b in bufs]
    out = _tied_call(
        body, after, name=name,
        out_shape=(*[pltpu.SemaphoreType.DMA(())] * (2 * ns), *[pltpu.HBM(b.shape, b.dtype) for b in held],
                   jax.ShapeDtypeStruct((8, LANES), F32)),
        in_specs=[HBM_SPEC] * n,
        out_specs=(*[SEM_SPEC] * (2 * ns), *[HBM_SPEC] * n, pl.BlockSpec(memory_space=pltpu.VMEM)),
        input_output_aliases={i: 2 * ns + i for i in range(n)},
        compiler_params=pltpu.CompilerParams(has_side_effects=pltpu.SideEffectType.DATAFLOW_SIDE_EFFECTING,
                                             collective_id=CHIPS_COLLECTIVE_ID),
    )(*held)
    return list(out[:ns]), list(out[ns:2 * ns]), list(out[2 * ns:2 * ns + n]), out[2 * ns + n]


def _gather_wait(send_sems, recv_sems, bufs, after, name, descriptors=_gather_descriptors):
    n = len(bufs)
    ns = 3 * n

    def body(*refs):
        buf_ref = refs[:n]
        sem_ref = refs[n:n + 2 * ns]
        sends, arrivals = descriptors(buf_ref, lambda a, k: sem_ref[3 * a + k], lambda a, k: sem_ref[ns + 3 * a + k])
        for cp in sends:
            cp().wait_send()
        for cp in arrivals:
            cp().wait_recv()

    out = pl.pallas_call(
        body, name=name,
        out_shape=tuple(pltpu.HBM(b.shape, b.dtype) for b in bufs),
        in_specs=[HBM_SPEC] * n + [SEM_SPEC] * (2 * ns) + [ANY] * len(_as_tuple(after)),
        out_specs=tuple([HBM_SPEC] * n),
        input_output_aliases={i: i for i in range(n)},
        compiler_params=pltpu.CompilerParams(has_side_effects=pltpu.SideEffectType.DATAFLOW_SIDE_EFFECTING),
    )(*bufs, *send_sems, *recv_sems, *_as_tuple(after))
    return list(out)


SIBLING_COLLECTIVE_ID = 0


def _sibling_handshake():
    x, y, c = _mesh_pos()
    barrier = pltpu.get_barrier_semaphore()
    pl.semaphore_signal(barrier, inc=1, device_id=(x, y, 1 - c), device_id_type=MESH)
    pl.semaphore_wait(barrier, 1)


CHIPS_COLLECTIVE_ID = 1


def _chips_handshake():
    x, y, c = _mesh_pos()
    barrier = pltpu.get_barrier_semaphore()
    for chip in _other_chips(x, y):
        pl.semaphore_signal(barrier, inc=1, device_id=(*chip, c), device_id_type=MESH)
    pl.semaphore_wait(barrier, 3)


def _pass_descriptors(bufs, send_of, recv_of):
    x, y, c = _mesh_pos()
    chips = _other_chips(x, y)

    def half(a, k, which):
        ck = 2 * chips[k][0] + chips[k][1]
        return functools.partial(
            pltpu.make_async_remote_copy,
            src_ref=_slot(bufs[a], ck, which), dst_ref=_slot(bufs[a], ck, which),
            send_sem=send_of(a, k), recv_sem=recv_of(a, k),
            device_id=(x, y, 1 - c), device_id_type=MESH)

    pairs = [(a, k) for a in range(len(bufs)) for k in range(3)]
    return [half(a, k, c) for a, k in pairs], [half(a, k, 1 - c) for a, k in pairs]


def _pass_start(bufs, name, after=()):
    n = len(bufs)
    ns = 3 * n

    def body(*refs):
        sems = refs[n:n + 2 * ns]
        thru = refs[n + 2 * ns:2 * n + 2 * ns]
        token = refs[2 * n + 2 * ns]
        _sibling_handshake()
        sends, _ = _pass_descriptors(thru, lambda a, k: sems[3 * a + k], lambda a, k: sems[ns + 3 * a + k])
        for cp in sends:
            cp().start()
        token[...] = jnp.zeros_like(token)

    held = [pltpu.with_memory_space_constraint(b, pltpu.HBM) for b in bufs]
    out = _tied_call(
        body, after, name=name,
        out_shape=(*[pltpu.SemaphoreType.DMA(())] * (2 * ns), *[pltpu.HBM(b.shape, b.dtype) for b in held],
                   jax.ShapeDtypeStruct((8, LANES), F32)),
        in_specs=[HBM_SPEC] * n,
        out_specs=(*[SEM_SPEC] * (2 * ns), *[HBM_SPEC] * n, pl.BlockSpec(memory_space=pltpu.VMEM)),
        input_output_aliases={i: 2 * ns + i for i in range(n)},
        compiler_params=pltpu.CompilerParams(has_side_effects=pltpu.SideEffectType.DATAFLOW_SIDE_EFFECTING,
                                             collective_id=SIBLING_COLLECTIVE_ID),
    )(*held)
    return list(out[:ns]), list(out[ns:2 * ns]), list(out[2 * ns:2 * ns + n]), out[2 * ns + n]


def _pass_to_sibling(bufs, name, after=()):
    n = len(bufs)

    def body(*refs):
        outs = refs[n:2 * n]
        send_sem, recv_sem = refs[2 * n:]
        _sibling_handshake()
        sends, arrivals = _pass_descriptors(outs, lambda a, k: send_sem.at[a, k], lambda a, k: recv_sem.at[a, k])
        sends = [cp() for cp in sends]
        for cp in sends:
            cp.start()
        for cp in arrivals:
            cp().wait_recv()
        for cp in sends:
            cp.wait_send()

    return _tied_call(
        body, after, name=name,
        in_specs=[ANY] * n, out_specs=[ANY] * n,
        out_shape=[jax.ShapeDtypeStruct(b.shape, b.dtype) for b in bufs],
        input_output_aliases={a: a for a in range(n)},
        scratch_shapes=[pltpu.SemaphoreType.DMA((n, 3))] * 2,
        compiler_params=pltpu.CompilerParams(collective_id=SIBLING_COLLECTIVE_ID),
    )(*bufs)


def _swap_descriptors(grads, lands, send_of, recv_of):
    x, y, c = _mesh_pos()
    return [functools.partial(
        pltpu.make_async_remote_copy,
        src_ref=grads[a].at[:, pl.ds(1 - c, 1)], dst_ref=lands[a],
        send_sem=send_of(a), recv_sem=recv_of(a),
        device_id=(x, y, 1 - c), device_id_type=MESH) for a in range(len(grads))]


def _swap_halves(grads, name, after=()):
    n = len(grads)

    def body(*refs):
        ins, outs = refs[:n], refs[n:2 * n]
        send_sem, recv_sem = refs[2 * n:]
        _sibling_handshake()
        cps = [cp() for cp in _swap_descriptors(ins, outs, lambda a: send_sem.at[a], lambda a: recv_sem.at[a])]
        for cp in cps:
            cp.start()
        for cp in cps:
            cp.wait()

    out_shape = [jax.ShapeDtypeStruct((g.shape[0], 1) + g.shape[2:], g.dtype) for g in grads]
    return _tied_call(
        body, after, name=name,
        in_specs=[ANY] * n, out_specs=[ANY] * n, out_shape=out_shape,
        scratch_shapes=[pltpu.SemaphoreType.DMA((n,))] * 2,
        compiler_params=pltpu.CompilerParams(collective_id=SIBLING_COLLECTIVE_ID),
    )(*grads)


def _swap_start(grads, name, after=()):
    n, after = len(grads), _as_tuple(after)

    def body(*refs):
        outs = refs[2 * n + len(after):]
        sems, g_thru, l_thru, token = outs[:2 * n], outs[2 * n:3 * n], outs[3 * n:4 * n], outs[4 * n]
        _sibling_handshake()
        for cp in _swap_descriptors(g_thru, l_thru, lambda a: sems[a], lambda a: sems[n + a]):
            cp().start()
        token[...] = jnp.zeros_like(token)

    lands = [lax.empty((g.shape[0], 1) + g.shape[2:], g.dtype) for g in grads]
    held = [pltpu.with_memory_space_constraint(a, pltpu.HBM) for a in (*grads, *lands)]
    out = pl.pallas_call(
        body, name=name,
        out_shape=(*[pltpu.SemaphoreType.DMA(())] * (2 * n), *[pltpu.HBM(a.shape, a.dtype) for a in held],
                   jax.ShapeDtypeStruct((8, LANES), F32)),
        in_specs=[HBM_SPEC] * (2 * n) + [ANY] * len(after),
        out_specs=(*[SEM_SPEC] * (2 * n), *[HBM_SPEC] * (2 * n), pl.BlockSpec(memory_space=pltpu.VMEM)),
        input_output_aliases={i: 2 * n + i for i in range(2 * n)},
        compiler_params=pltpu.CompilerParams(has_side_effects=pltpu.SideEffectType.DATAFLOW_SIDE_EFFECTING,
                                             collective_id=SIBLING_COLLECTIVE_ID),
    )(*held, *after)
    return list(out[:2 * n]), list(out[2 * n:3 * n]), list(out[3 * n:4 * n]), out[4 * n]


def _swap_wait(sems, grads, lands, after, name):
    n = len(grads)

    def body(*refs):
        g_ref, l_ref = refs[:n], refs[n:2 * n]
        sem_ref = refs[2 * n:4 * n]
        for cp in _swap_descriptors(g_ref, l_ref, lambda a: sem_ref[a], lambda a: sem_ref[n + a]):
            cp().wait()

    out = pl.pallas_call(
        body, name=name,
        out_shape=tuple(pltpu.HBM(a.shape, a.dtype) for a in (*grads, *lands)),
        in_specs=[HBM_SPEC] * (2 * n) + [SEM_SPEC] * (2 * n) + [ANY] * len(_as_tuple(after)),
        out_specs=tuple([HBM_SPEC] * (2 * n)),
        input_output_aliases={i: i for i in range(2 * n)},
        compiler_params=pltpu.CompilerParams(has_side_effects=pltpu.SideEffectType.DATAFLOW_SIDE_EFFECTING),
    )(*grads, *lands, *sems, *_as_tuple(after))
    return list(out[:n]), list(out[n:])


def _add_halves(gs, gots, pos, name, dtypes):
    n = len(gs)
    j = gs[0].shape[0]

    def body(pos_ref, *refs):
        g_refs, r_refs = refs[:n], refs[n:2 * n]
        o_refs, p_refs = refs[2 * n:3 * n], refs[3 * n:]
        vals = [(g_refs[a][0, 0] + r_refs[a][0, 0]).astype(dtypes[a]) for a in range(n)]
        for a in range(n):
            o_refs[a][0] = vals[a]
        if j == 1:
            for a in range(n):
                p_refs[a][0] = vals[a]
        else:
            @pl.when(pl.program_id(0) == pos_ref[0])
            def _():
                for a in range(n):
                    p_refs[a][0] = vals[a]

    blk = lambda g: (1,) + g.shape[2:]
    out = pl.pallas_call(
        body, name=name,
        grid_spec=pltpu.PrefetchScalarGridSpec(
            num_scalar_prefetch=1, grid=(j,),
            in_specs=[pl.BlockSpec((1,) + blk(g), lambda i, p: (i, p[1], 0, 0)) for g in gs]
            + [pl.BlockSpec((1,) + blk(g), lambda i, p: (i, 0, 0, 0)) for g in gs],
            out_specs=[pl.BlockSpec(blk(g), lambda i, p: (i, 0, 0)) for g in gs]
            + [pl.BlockSpec(blk(g), lambda i, p: (p[0], 0, 0)) for g in gs]),
        out_shape=[jax.ShapeDtypeStruct((j,) + g.shape[2:], dt) for g, dt in zip(gs, dtypes)]
        + [jax.ShapeDtypeStruct((N_CHIPS,) + g.shape[2:], dt) for g, dt in zip(gs, dtypes)],
        compiler_params=_params(("arbitrary",)),
    )(pos, *gs, *gots)
    return list(out[:n]), list(out[n:])


def _exchange_descriptors(sums, parts, send_of, recv_of):
    x, y, c = _mesh_pos()
    me = 2 * x + y
    chips = _other_chips(x, y)
    sends, arrivals = [], []
    for a in range(len(sums)):
        for k in range(3):
            ck = 2 * chips[k][0] + chips[k][1]
            mine = sums[a].at[ck] if sums[a].shape[0] == N_CHIPS else sums[a].at[0]

            def copy(dst_slot, a=a, k=k, mine=mine):
                return pltpu.make_async_remote_copy(
                    src_ref=mine, dst_ref=parts[a].at[dst_slot],
                    send_sem=send_of(a, k), recv_sem=recv_of(a, k),
                    device_id=(*chips[k], c), device_id_type=MESH)

            sends.append(functools.partial(copy, me))
            arrivals.append(functools.partial(copy, ck))
    return sends, arrivals


def _exchange_start(sums, parts, name, after=()):
    n = len(sums)
    ns = 3 * n

    def body(*refs):
        sems = refs[2 * n:2 * n + 2 * ns]
        sums_thru = refs[2 * n + 2 * ns:3 * n + 2 * ns]
        parts_thru = refs[3 * n + 2 * ns:4 * n + 2 * ns]
        token = refs[4 * n + 2 * ns]
        _chips_handshake()
        sends, _ = _exchange_descriptors(sums_thru, parts_thru, lambda a, k: sems[3 * a + k],
                                         lambda a, k: sems[ns + 3 * a + k])
        for cp in sends:
            cp().start()
        token[...] = jnp.zeros_like(token)

    hbm = lambda a: pltpu.HBM(a.shape, a.dtype)
    held = [pltpu.with_memory_space_constraint(a, pltpu.HBM) for a in (*sums, *parts)]
    out = _tied_call(
        body, after, name=name,
        out_shape=(*[pltpu.SemaphoreType.DMA(())] * (2 * ns), *[hbm(a) for a in held],
                   jax.ShapeDtypeStruct((8, LANES), F32)),
        in_specs=[HBM_SPEC] * (2 * n),
        out_specs=(*[SEM_SPEC] * (2 * ns), *[HBM_SPEC] * (2 * n), pl.BlockSpec(memory_space=pltpu.VMEM)),
        input_output_aliases={i: 2 * ns + i for i in range(2 * n)},
        compiler_params=pltpu.CompilerParams(has_side_effects=pltpu.SideEffectType.DATAFLOW_SIDE_EFFECTING,
                                             collective_id=CHIPS_COLLECTIVE_ID),
    )(*held)
    return (list(out[:2 * ns]), list(out[2 * ns:2 * ns + n]), list(out[2 * ns + n:2 * ns + 2 * n]),
            out[2 * ns + 2 * n])


def _exchange_wait(sems, sums, parts, after, name):
    n = len(sums)
    ns = 3 * n

    def body(*refs):
        sums_ref, parts_ref = refs[:n], refs[n:2 * n]
        sem_ref = refs[2 * n:2 * n + 2 * ns]
        sends, arrivals = _exchange_descriptors(sums_ref, parts_ref, lambda a, k: sem_ref[3 * a + k],
                                                lambda a, k: sem_ref[ns + 3 * a + k])
        for cp in sends:
            cp().wait_send()
        for cp in arrivals:
            cp().wait_recv()

    hbm = lambda a: pltpu.HBM(a.shape, a.dtype)
    out = pl.pallas_call(
        body, name=name,
        out_shape=tuple(hbm(a) for a in (*sums, *parts)),
        in_specs=[HBM_SPEC] * (2 * n) + [SEM_SPEC] * (2 * ns) + [ANY] * len(_as_tuple(after)),
        out_specs=tuple([HBM_SPEC] * (2 * n)),
        input_output_aliases={i: i for i in range(2 * n)},
        compiler_params=pltpu.CompilerParams(has_side_effects=pltpu.SideEffectType.DATAFLOW_SIDE_EFFECTING),
    )(*sums, *parts, *sems, *_as_tuple(after))
    return list(out[n:])


def _sum_chips(parts, pos, name, after=()):
    n = len(parts)
    after = _as_tuple(after)

    def body(pos_ref, *refs):
        outs = refs[n + len(after):]
        for a in range(n):
            p_ref = refs[a]
            outs[a][0] = (((p_ref[0].astype(F32) + p_ref[1].astype(F32)) + p_ref[2].astype(F32))
                          + p_ref[3].astype(F32))

    out = pl.pallas_call(
        body, name=name,
        grid_spec=pltpu.PrefetchScalarGridSpec(
            num_scalar_prefetch=1, grid=(1,),
            in_specs=[pl.BlockSpec(p.shape, lambda i, q: (0, 0, 0)) for p in parts] + [ANY] * len(after),
            out_specs=[pl.BlockSpec((1,) + p.shape[1:], lambda i, q: (q[1], 0, 0)) for p in parts]),
        out_shape=[jax.ShapeDtypeStruct((2,) + p.shape[1:], F32) for p in parts],
        compiler_params=_params(("arbitrary",)),
    )(pos, *parts, *after)
    return list(out)


def _join_halves(fulls, name, after=()):
    n = len(fulls)

    def body(*refs):
        outs = refs[n:2 * n]
        send_sem, recv_sem = refs[2 * n:]
        x, y, c = _mesh_pos()
        _sibling_handshake()

        def half(a, which):
            return pltpu.make_async_remote_copy(
                src_ref=outs[a].at[which], dst_ref=outs[a].at[which],
                send_sem=send_sem.at[a], recv_sem=recv_sem.at[a],
                device_id=(x, y, 1 - c), device_id_type=MESH)

        sends = [half(a, c) for a in range(n)]
        for cp in sends:
            cp.start()
        for a in range(n):
            half(a, 1 - c).wait_recv()
        for cp in sends:
            cp.wait_send()

    out_shape = [jax.ShapeDtypeStruct(f.shape, f.dtype) for f in fulls]
    return _tied_call(
        body, after, name=name,
        in_specs=[ANY] * n, out_specs=[ANY] * n, out_shape=out_shape,
        input_output_aliases={a: a for a in range(n)},
        scratch_shapes=[pltpu.SemaphoreType.DMA((n,))] * 2,
        compiler_params=pltpu.CompilerParams(collective_id=SIBLING_COLLECTIVE_ID),
    )(*fulls)


def _norm_in(x, g, ts, after=()):
    s = x.shape[0]

    def body(x_ref, g_ref, hn_ref):
        xv = x_ref[...]
        hn_ref[...] = (xv * _rms_stats(xv) * g_ref[...]).astype(BF16)

    row = pl.BlockSpec((ts, D_MODEL), lambda i: (i, 0))
    return _tied_call(
        body, after, name="norm_in", grid=(s // ts,),
        in_specs=[row, pl.BlockSpec((1, D_MODEL), lambda i: (0, 0))], out_specs=row,
        out_shape=jax.ShapeDtypeStruct((s, D_MODEL), BF16),
        compiler_params=_params(("parallel",)),
    )(x, g)


def _shift_rows(buf, shifted, t):
    rows = t + CONV_HALO - SUBLANES
    for r in range(1, SUBLANES):
        shifted[r - 1, 0:rows, :] = buf[pl.ds(r, rows), :]


def _window(buf, shifted, offset, t):
    r = offset % SUBLANES
    if r == 0:
        return buf[pl.ds(offset, t), :]
    return shifted[r - 1, pl.ds(offset - r, t), :]


def _lane_is_low_head():
    lane = lax.broadcasted_iota(jnp.int32, (1, GM_WIDTH), 1)
    return (lane & GM_HEAD_DIM) == 0


def _gm_mix(v_lo, v_hi, wpair_ref, bias_ref, mixed_ref, t):
    for n in range(t // CHUNK):
        rows = slice(n * CHUNK, (n + 1) * CHUNK)
        for j in range(GM_HEADS // 2):
            cols = slice(j * LANES, (j + 1) * LANES)
            rhs = jnp.concatenate([v_lo[rows, cols], v_hi[rows, cols]], axis=0)
            mixed_ref[rows, cols] = _dot(wpair_ref[j], rhs) + bias_ref[:, cols]


def _seqmix_fwd(hn, w_in, b_in, cw, cb, lng, lnb, gg, gb, wpair, bias, t, after=()):
    s = hn.shape[0]

    def body(hn_ref, w_ref, b_ref, cw_ref, cb_ref, lng_ref, lnb_ref, gg_ref, gb_ref, wpair_ref, bias_ref,
             z_ref, mix_ref, c1_ref, abuf, ash, mixed_ref):
        i = pl.program_id(0)

        @pl.when(i == 0)
        def _():
            abuf[0:CONV_HALO, :] = jnp.zeros((CONV_HALO, CONV_WIDTH), F32)

        @pl.when(i > 0)
        def _():
            abuf[0:CONV_HALO, :] = abuf[t:t + CONV_HALO, :]

        hv = hn_ref[...]
        for j in range(4):
            cols = slice(j * 512, (j + 1) * 512)
            z_ref[:, cols] = _dot(hv, w_ref[j]) + b_ref[:, cols]

        abuf[CONV_HALO:, :] = z_ref[:, 0:512] * _sigmoid(z_ref[:, 512:1024])
        _shift_rows(abuf, ash, t)
        acc = jnp.zeros((t, CONV_WIDTH), F32)
        for k in range(CONV_KERNEL):
            acc = acc + cw_ref[k:k + 1, :] * _window(abuf, ash, CONV_HALO - (CONV_KERNEL - 1) + k, t)
        c1 = acc + cb_ref[...]
        c1_ref[...] = c1
        xh, _ = _ln_stats(c1)
        ln = xh * lng_ref[...] + lnb_ref[...]
        mix_ref[:, 0:512] = (ln * _sigmoid(ln)).astype(BF16)

        u, _ = _gelu_parts(z_ref[:, 1024:1536])
        gv, _ = _gelu_parts(z_ref[:, 1536:2048])
        vxh, _ = _ln_stats(gv)
        v = vxh * gg_ref[...] + gb_ref[...]
        low = _lane_is_low_head()
        v_lo = jnp.where(low, v, 0.0).astype(BF16)
        v_hi = jnp.where(low, 0.0, v).astype(BF16)
        _gm_mix(v_lo, v_hi, wpair_ref, bias_ref, mixed_ref, t)
        mix_ref[:, 512:1024] = (u * mixed_ref[...]).astype(BF16)

    vec = lambda n: pl.BlockSpec((1, n), lambda i: (0, 0))
    return _tied_call(
        body, after, name="seqmix_fwd", grid=(s // t,),
        in_specs=[pl.BlockSpec((t, D_MODEL), lambda i: (i, 0)),
                  pl.BlockSpec((4, D_MODEL, 512), lambda i: (0, 0, 0)), vec(2048),
                  pl.BlockSpec((CONV_HALO, CONV_WIDTH), lambda i: (0, 0)),
                  vec(512), vec(512), vec(512), vec(512), vec(512),
                  pl.BlockSpec((4, CHUNK, 2 * CHUNK), lambda i: (0, 0, 0)),
                  pl.BlockSpec((CHUNK, GM_WIDTH), lambda i: (0, 0))],
        out_specs=[pl.BlockSpec((t, 2048), lambda i: (i, 0)),
                   pl.BlockSpec((t, D_MODEL), lambda i: (i, 0)),
                   pl.BlockSpec((t, CONV_WIDTH), lambda i: (i, 0))],
        out_shape=[jax.ShapeDtypeStruct((s, 2048), F32), jax.ShapeDtypeStruct((s, D_MODEL), BF16),
                   jax.ShapeDtypeStruct((s, CONV_WIDTH), F32)],
        scratch_shapes=[pltpu.VMEM((t + CONV_HALO, CONV_WIDTH), F32),
                        pltpu.VMEM((SUBLANES - 1, t + CONV_HALO - SUBLANES, CONV_WIDTH), F32),
                        pltpu.VMEM((t, GM_WIDTH), F32)],
        compiler_params=_params(("arbitrary",)),
    )(hn, w_in, b_in, cw, cb, lng, lnb, gg, gb, wpair, bias)


def _mem_kv(mem, g, wkv):
    m = mem.shape[0]

    def body(mem_ref, g_ref, w_ref, mn_ref, kv_ref):
        mv = mem_ref[...]
        mn = (mv * _rms_stats(mv) * g_ref[...]).astype(BF16)
        mn_ref[...] = mn
        for j in range(4):
            kv_ref[:, j * 512:(j + 1) * 512] = _dot(mn, w_ref[j]).astype(BF16)

    return pl.pallas_call(
        body, name="mem_kv",
        out_shape=[jax.ShapeDtypeStruct((m, D_MODEL), BF16), jax.ShapeDtypeStruct((m, 2 * D_MODEL), BF16)],
        compiler_params=pltpu.CompilerParams(vmem_limit_bytes=VMEM_LIMIT_BYTES),
    )(mem, g, wkv)


def _softmax_rows(sc):
    e = jnp.exp(sc - jnp.max(sc, axis=-1, keepdims=True))
    return e / jnp.sum(e, axis=-1, keepdims=True)


def _attn_block_fwd(x, mix, w_out, g_xa, wq, kv, wo, g_ffn, ts, after=()):
    s, m = x.shape[0], kv.shape[0]
    scale = XA_HEAD_DIM ** -0.5

    def body(x_ref, mix_ref, wout_ref, gxa_ref, wq_ref, kv_ref, wo_ref, gffn_ref,
             h1_ref, hn2_ref, q_ref, o_ref, h2_ref, hn3_ref):
        h1 = x_ref[...] + _dot(mix_ref[...], wout_ref[...])
        h1_ref[...] = h1
        hn2 = (h1 * _rms_stats(h1) * gxa_ref[...]).astype(BF16)
        hn2_ref[...] = hn2
        q_ref[...] = _dot(hn2, wq_ref[...]).astype(BF16)
        for h in range(XA_HEADS):
            cols = slice(h * XA_HEAD_DIM, (h + 1) * XA_HEAD_DIM)
            vcols = slice(D_MODEL + h * XA_HEAD_DIM, D_MODEL + (h + 1) * XA_HEAD_DIM)
            p = _softmax_rows(_dot_nt(q_ref[:, cols], kv_ref[:, cols]) * scale)
            o_ref[:, cols] = _dot(p.astype(BF16), kv_ref[:, vcols]).astype(BF16)
        h2 = h1 + _dot(o_ref[...], wo_ref[...])
        h2_ref[...] = h2
        hn3_ref[...] = (h2 * _rms_stats(h2) * gffn_ref[...]).astype(BF16)

    row = pl.BlockSpec((ts, D_MODEL), lambda i: (i, 0))
    full = pl.BlockSpec((D_MODEL, D_MODEL), lambda i: (0, 0))
    vec = pl.BlockSpec((1, D_MODEL), lambda i: (0, 0))
    f32 = jax.ShapeDtypeStruct((s, D_MODEL), F32)
    bf16 = jax.ShapeDtypeStruct((s, D_MODEL), BF16)
    return _tied_call(
        body, after, name="attn_block_fwd", grid=(s // ts,),
        in_specs=[row, row, full, vec, full, pl.BlockSpec((m, 2 * D_MODEL), lambda i: (0, 0)), full, vec],
        out_specs=[row] * 6,
        out_shape=[f32, bf16, bf16, bf16, f32, bf16],
        compiler_params=_params(("parallel",)),
    )(x, mix, w_out, g_xa, wq, kv, wo, g_ffn)


_FFN_CHUNKS = (slice(0, 8 * LANES), slice(8 * LANES, 16 * LANES), slice(16 * LANES, FFN_HIDDEN))


def _ffn_up(hn, wgu, ts, after=()):
    s = hn.shape[0]

    def body(hn_ref, w_ref, gu_ref, act_ref):
        hv = hn_ref[...]
        for cols in _FFN_CHUNKS:
            gate = _dot(hv, w_ref[0, :, cols])
            up = _dot(hv, w_ref[1, :, cols])
            gu_ref[0, :, cols] = gate.astype(BF16)
            gu_ref[1, :, cols] = up.astype(BF16)
            act_ref[:, cols] = (gate * _sigmoid(gate) * up).astype(BF16)

    return _tied_call(
        body, after, name="ffn_up", grid=(s // ts,),
        in_specs=[pl.BlockSpec((ts, D_MODEL), lambda i: (i, 0)),
                  pl.BlockSpec((2, D_MODEL, FFN_HIDDEN), lambda i: (0, 0, 0))],
        out_specs=[pl.BlockSpec((2, ts, FFN_HIDDEN), lambda i: (0, i, 0)),
                   pl.BlockSpec((ts, FFN_HIDDEN), lambda i: (i, 0))],
        out_shape=[jax.ShapeDtypeStruct((2, s, FFN_HIDDEN), BF16), jax.ShapeDtypeStruct((s, FFN_HIDDEN), BF16)],
        compiler_params=_params(("parallel",)),
    )(hn, wgu)


def _ffn_down_loss(act, wd, h2, g, target, ts):
    s = act.shape[0]

    def body(act_ref, wd_ref, h2_ref, g_ref, t_ref, dh_ref, dhb_ref, sq_ref, dg_ref):
        @pl.when(pl.program_id(0) == 0)
        def _():
            sq_ref[...] = jnp.zeros_like(sq_ref)
            dg_ref[...] = jnp.zeros_like(dg_ref)

        h3 = h2_ref[...] + _dot(act_ref[...], wd_ref[...])
        r = _rms_stats(h3)
        gv = g_ref[...]
        diff = h3 * r * gv - t_ref[...]
        sq_ref[...] += _rowsum(diff * diff)
        dh, dg = _rms_bwd(diff / D_MODEL, h3, r, gv)
        dh_ref[...] = dh
        dhb_ref[...] = dh.astype(BF16)
        dg_ref[...] += dg

    row = pl.BlockSpec((ts, D_MODEL), lambda i: (i, 0))
    vec = pl.BlockSpec((1, D_MODEL), lambda i: (0, 0))
    return pl.pallas_call(
        body, name="ffn_down_loss", grid=(s // ts,),
        in_specs=[pl.BlockSpec((ts, FFN_HIDDEN), lambda i: (i, 0)),
                  pl.BlockSpec((FFN_HIDDEN, D_MODEL), lambda i: (0, 0)), row, vec, row],
        out_specs=[row, row, vec, vec],
        out_shape=[jax.ShapeDtypeStruct((s, D_MODEL), F32), jax.ShapeDtypeStruct((s, D_MODEL), BF16),
                   jax.ShapeDtypeStruct((1, D_MODEL), F32), jax.ShapeDtypeStruct((1, D_MODEL), F32)],
        compiler_params=_params(("arbitrary",)),
    )(act, wd, h2, g, target)


def _grad_w(a, b, tk, tn, name, after=(), shards=1):
    s, k = a.shape
    gb, _, n = b.shape
    nblk = n // tn
    ws = tn // shards
    tsr = GRAD_ROWS if s % GRAD_ROWS == 0 else s

    def body(a_ref, b_ref, o_ref):
        part = _dot_tn(a_ref[...], b_ref[0])

        @pl.when(pl.program_id(2) == 0)
        def _():
            for j in range(shards):
                o_ref[j] = part[:, j * ws:(j + 1) * ws]

        @pl.when(pl.program_id(2) > 0)
        def _():
            for j in range(shards):
                o_ref[j] += part[:, j * ws:(j + 1) * ws]

    return _tied_call(
        body, after, name=name, grid=(gb * nblk, k // tk, s // tsr),
        in_specs=[pl.BlockSpec((tsr, tk), lambda ni, ki, si: (si, ki)),
                  pl.BlockSpec((1, tsr, tn), lambda ni, ki, si: (ni // nblk, si, ni % nblk))],
        out_specs=pl.BlockSpec((shards, tk, ws), lambda ni, ki, si: (ni, ki, 0)),
        out_shape=jax.ShapeDtypeStruct((gb * nblk * shards, k, ws), F32),
        compiler_params=_params(("parallel", "parallel", "arbitrary")),
    )(a, b)


def _grad_w_square(pairs, name, after=()):
    n = len(pairs)
    s = pairs[0][0].shape[0]
    tsr = GRAD_ROWS // 2 if s % (GRAD_ROWS // 2) == 0 else s

    def body(*refs):
        ins, outs = refs[:2 * n], refs[2 * n:]
        parts = [_dot_tn(ins[2 * a][...], ins[2 * a + 1][...]) for a in range(n)]

        @pl.when(pl.program_id(0) == 0)
        def _():
            for a in range(n):
                outs[a][...] = parts[a]

        @pl.when(pl.program_id(0) > 0)
        def _():
            for a in range(n):
                outs[a][...] += parts[a]

    row = pl.BlockSpec((tsr, D_MODEL), lambda i: (i, 0))
    return _tied_call(
        body, after, name=name, grid=(s // tsr,),
        in_specs=[row] * (2 * n), out_specs=[pl.BlockSpec((D_MODEL, D_MODEL), lambda i: (0, 0))] * n,
        out_shape=[jax.ShapeDtypeStruct((D_MODEL, D_MODEL), F32)] * n,
        compiler_params=_params(("arbitrary",)),
    )(*[x for p in pairs for x in p])


def _ffn_bwd(dh3, wd, gu, wgu, h2, g, t, after=()):
    s = dh3.shape[0]

    def body(dh3_ref, wd_ref, gu_ref, w_ref, h2_ref, g_ref, dgu_ref, dh2_ref, dh2b_ref, dg_ref):
        @pl.when(pl.program_id(0) == 0)
        def _():
            dg_ref[...] = jnp.zeros_like(dg_ref)

        dh3v = dh3_ref[...]
        dhb = dh3v.astype(BF16)
        for cols in _FFN_CHUNKS:
            dact = _dot_nt(dhb, wd_ref[cols, :])
            gate, up = gu_ref[0, :, cols].astype(F32), gu_ref[1, :, cols].astype(F32)
            sg = _sigmoid(gate)
            dgu_ref[0, :, cols] = (dact * up * (sg * (1.0 + gate * (1.0 - sg)))).astype(BF16)
            dgu_ref[1, :, cols] = (dact * (gate * sg)).astype(BF16)
        dhn = _dot_nt(dgu_ref[0], w_ref[0]) + _dot_nt(dgu_ref[1], w_ref[1])
        h2 = h2_ref[...]
        dv, dg = _rms_bwd(dhn, h2, _rms_stats(h2), g_ref[...])
        dh2 = dh3v + dv
        dh2_ref[...] = dh2
        dh2b_ref[...] = dh2.astype(BF16)
        dg_ref[...] += dg

    row = pl.BlockSpec((t, D_MODEL), lambda i: (i, 0))
    wide = pl.BlockSpec((2, t, FFN_HIDDEN), lambda i: (0, i, 0))
    vec = pl.BlockSpec((1, D_MODEL), lambda i: (0, 0))
    return _tied_call(
        body, after, name="ffn_bwd", grid=(s // t,),
        in_specs=[row, pl.BlockSpec((FFN_HIDDEN, D_MODEL), lambda i: (0, 0)), wide,
                  pl.BlockSpec((2, D_MODEL, FFN_HIDDEN), lambda i: (0, 0, 0)), row, vec],
        out_specs=[wide, row, row, vec],
        out_shape=[jax.ShapeDtypeStruct((2, s, FFN_HIDDEN), BF16), jax.ShapeDtypeStruct((s, D_MODEL), F32),
                   jax.ShapeDtypeStruct((s, D_MODEL), BF16), jax.ShapeDtypeStruct((1, D_MODEL), F32)],
        compiler_params=_params(("arbitrary",)),
    )(dh3, wd, gu, wgu, h2, g)


def _attn_bwd(dh2, wo, q, kv, wq, h1, g, ts, after=()):
    s, m = q.shape[0], kv.shape[0]
    scale = XA_HEAD_DIM ** -0.5

    def body(dh2_ref, wo_ref, q_ref, kv_ref, wq_ref, h1_ref, g_ref, dh1_ref, dh1b_ref, dq_ref, dkv_ref, dg_ref):
        @pl.when(pl.program_id(0) == 0)
        def _():
            dkv_ref[...] = jnp.zeros_like(dkv_ref)
            dg_ref[...] = jnp.zeros_like(dg_ref)

        do = _dot_nt(dh2_ref[...].astype(BF16), wo_ref[...]).astype(BF16)
        for h in range(XA_HEADS):
            cols = slice(h * XA_HEAD_DIM, (h + 1) * XA_HEAD_DIM)
            vcols = slice(D_MODEL + h * XA_HEAD_DIM, D_MODEL + (h + 1) * XA_HEAD_DIM)
            qh, kh, vh, doh = q_ref[:, cols], kv_ref[:, cols], kv_ref[:, vcols], do[:, cols]
            p = _softmax_rows(_dot_nt(qh, kh) * scale)
            dp = _dot_nt(doh, vh)
            ds = (p * (dp - jnp.sum(dp * p, axis=-1, keepdims=True)) * scale).astype(BF16)
            dq_ref[:, cols] = _dot(ds, kh).astype(BF16)
            dkv_ref[:, cols] += _dot_tn(ds, qh)
            dkv_ref[:, vcols] += _dot_tn(p.astype(BF16), doh)
        dhn = _dot_nt(dq_ref[...], wq_ref[...])
        h1 = h1_ref[...]
        dv, dg = _rms_bwd(dhn, h1, _rms_stats(h1), g_ref[...])
        dh1 = dh2_ref[...] + dv
        dh1_ref[...] = dh1
        dh1b_ref[...] = dh1.astype(BF16)
        dg_ref[...] += dg

    row = pl.BlockSpec((ts, D_MODEL), lambda i: (i, 0))
    full = pl.BlockSpec((D_MODEL, D_MODEL), lambda i: (0, 0))
    kvs = pl.BlockSpec((m, 2 * D_MODEL), lambda i: (0, 0))
    vec = pl.BlockSpec((1, D_MODEL), lambda i: (0, 0))
    return _tied_call(
        body, after, name="attn_bwd", grid=(s // ts,),
        in_specs=[row, full, row, kvs, full, row, vec],
        out_specs=[row, row, row, kvs, vec],
        out_shape=[jax.ShapeDtypeStruct((s, D_MODEL), F32), jax.ShapeDtypeStruct((s, D_MODEL), BF16),
                   jax.ShapeDtypeStruct((s, D_MODEL), BF16),
                   jax.ShapeDtypeStruct((m, 2 * D_MODEL), F32), jax.ShapeDtypeStruct((1, D_MODEL), F32)],
        compiler_params=_params(("arbitrary",)),
    )(dh2, wo, q, kv, wq, h1, g)


def _mem_kv_bwd(dkv, mn, wkv, mem, g, after=()):
    m = mem.shape[0]

    def body(dkv_ref, mn_ref, w_ref, mem_ref, g_ref, dw_ref, dg_ref):
        dmn = jnp.zeros((m, D_MODEL), F32)
        mn = mn_ref[...]
        for j in range(4):
            dj = dkv_ref[:, j * 512:(j + 1) * 512].astype(BF16)
            dw_ref[j] = _dot_tn(mn, dj)
            dmn = dmn + _dot_nt(dj, w_ref[j])
        mv = mem_ref[...]
        dg_ref[...] = _rowsum(dmn * (mv * _rms_stats(mv)))

    return _tied_call(
        body, after, name="mem_kv_bwd", in_specs=[pl.BlockSpec(memory_space=pltpu.VMEM)] * 5,
        out_shape=[jax.ShapeDtypeStruct((4, D_MODEL, 512), F32), jax.ShapeDtypeStruct((1, D_MODEL), F32)],
        compiler_params=pltpu.CompilerParams(vmem_limit_bytes=VMEM_LIMIT_BYTES),
    )(dkv, mn, wkv, mem, g)


def _seqmix_bwd(dh1, x, z, c1, w_out, w_in, g_mix, cw, lng, lnb, gg, gb, wpair, wpair_t, bias, t, after=()):
    s = x.shape[0]
    nt = s // t

    def body(dh1_ref, x_ref, z_ref, c1_ref, wo_ref, wi_ref, gm_ref, cw_ref, lng_ref, lnb_ref,
             gg_ref, gb_ref, wpair_ref, wpt_ref, bias_ref,
             gx_ref, dz_ref, dcw_ref, dcb_ref, dlng_ref, dlnb_ref, dgg_ref, dgb_ref, dws_ref, dbs_ref,
             dbin_ref, dgm_ref, dbuf, dsh, mixed_ref, dv_ref):
        i = pl.program_id(0)
        accs = (dcw_ref, dcb_ref, dlng_ref, dlnb_ref, dgg_ref, dgb_ref, dws_ref, dbs_ref, dbin_ref, dgm_ref)

        @pl.when(i == 0)
        def _():
            for r in accs:
                r[...] = jnp.zeros_like(r)
            dbuf[t:t + CONV_HALO, :] = jnp.zeros((CONV_HALO, CONV_WIDTH), F32)

        @pl.when(i > 0)
        def _():
            dbuf[t:t + CONV_HALO, :] = dbuf[0:CONV_HALO, :]

        dmix = _dot_nt(dh1_ref[...].astype(BF16), wo_ref[...])

        xh, rs = _ln_stats(c1_ref[...])
        lng = lng_ref[...]
        ln = xh * lng + lnb_ref[...]
        sl = _sigmoid(ln)
        dln = dmix[:, 0:512] * (sl * (1.0 + ln * (1.0 - sl)))
        dc1, dg_ln, db_ln = _ln_bwd(dln, xh, rs, lng)
        dlng_ref[...] += dg_ln
        dlnb_ref[...] += db_ln
        dcb_ref[...] += _rowsum(dc1)
        dbuf[0:t, :] = dc1

        za = z_ref[:, 0:512]
        sg = _sigmoid(z_ref[:, 512:1024])
        a = za * sg
        _shift_rows(dbuf, dsh, t)

        da = jnp.zeros((t, CONV_WIDTH), F32)
        for k in range(CONV_KERNEL):
            later = _window(dbuf, dsh, CONV_KERNEL - 1 - k, t)
            da = da + cw_ref[k:k + 1, :] * later
            dcw_ref[k:k + 1, :] += _rowsum(a * later)
        dza = da * sg
        dzg = da * za * (sg * (1.0 - sg))
        dz_ref[:, 0:512] = dza.astype(BF16)
        dz_ref[:, 512:1024] = dzg.astype(BF16)
        dbin_ref[:, 0:512] += _rowsum(dza)
        dbin_ref[:, 512:1024] += _rowsum(dzg)

        dgm = dmix[:, 512:1024]
        u, du_dz = _gelu_parts(z_ref[:, 1024:1536])
        gv, dgv_dz = _gelu_parts(z_ref[:, 1536:2048])
        vxh, vrs = _ln_stats(gv)
        ggv = gg_ref[...]
        v = vxh * ggv + gb_ref[...]
        low = _lane_is_low_head()
        v_lo = jnp.where(low, v, 0.0).astype(BF16)
        v_hi = jnp.where(low, 0.0, v).astype(BF16)
        _gm_mix(v_lo, v_hi, wpair_ref, bias_ref, mixed_ref, t)
        dzu = dgm * mixed_ref[...] * du_dz
        dm = dgm * u
        dm_lo = jnp.where(low, dm, 0.0).astype(BF16)
        dm_hi = jnp.where(low, 0.0, dm).astype(BF16)
        vb = v.astype(BF16)
        tril = (lax.broadcasted_iota(jnp.int32, (CHUNK, CHUNK), 1)
                <= lax.broadcasted_iota(jnp.int32, (CHUNK, CHUNK), 0))
        for n in range(t // CHUNK):
            rows = slice(n * CHUNK, (n + 1) * CHUNK)
            dbs_ref[...] += dm[rows, :]
            for j in range(GM_HEADS // 2):
                cols = slice(j * LANES, (j + 1) * LANES)
                stack = jnp.concatenate([dm_lo[rows, cols], dm_hi[rows, cols]], axis=0)
                dws = _dot_nt(stack, vb[rows, cols])
                dws_ref[2 * j] += jnp.where(tril, dws[0:CHUNK], 0.0)
                dws_ref[2 * j + 1] += jnp.where(tril, dws[CHUNK:2 * CHUNK], 0.0)
                dv_ref[rows, cols] = _dot(wpt_ref[j], stack)
        dgv, dg_gm, db_gm = _ln_bwd(dv_ref[...], vxh, vrs, ggv)
        dgg_ref[...] += dg_gm
        dgb_ref[...] += db_gm
        dzv = dgv * dgv_dz
        dz_ref[:, 1024:1536] = dzu.astype(BF16)
        dz_ref[:, 1536:2048] = dzv.astype(BF16)
        dbin_ref[:, 1024:1536] += _rowsum(dzu)
        dbin_ref[:, 1536:2048] += _rowsum(dzv)

        dhn = jnp.zeros((t, D_MODEL), F32)
        for j in range(4):
            dhn = dhn + _dot_nt(dz_ref[:, j * 512:(j + 1) * 512], wi_ref[j])
        xv = x_ref[...]
        dv, dg = _rms_bwd(dhn, xv, _rms_stats(xv), gm_ref[...])
        gx_ref[...] = dh1_ref[...] + dv
        dgm_ref[...] += dg

    rev = lambda w: pl.BlockSpec((t, w), lambda i: (nt - 1 - i, 0))
    const = lambda *shape: pl.BlockSpec(shape, lambda i: (0,) * len(shape))
    f32 = lambda *shape: jax.ShapeDtypeStruct(shape, F32)
    return _tied_call(
        body, after, name="seqmix_bwd", grid=(nt,),
        in_specs=[rev(D_MODEL), rev(D_MODEL), rev(2048), rev(CONV_WIDTH),
                  const(D_MODEL, D_MODEL), const(4, D_MODEL, 512), const(1, D_MODEL),
                  const(CONV_HALO, CONV_WIDTH), const(1, 512), const(1, 512), const(1, 512), const(1, 512),
                  const(4, CHUNK, 2 * CHUNK), const(4, CHUNK, 2 * CHUNK), const(CHUNK, GM_WIDTH)],
        out_specs=[rev(D_MODEL), rev(2048),
                   const(CONV_HALO, CONV_WIDTH), const(1, 512), const(1, 512), const(1, 512), const(1, 512),
                   const(1, 512), const(GM_HEADS, CHUNK, CHUNK), const(CHUNK, GM_WIDTH), const(1, 2048),
                   const(1, D_MODEL)],
        out_shape=[f32(s, D_MODEL), jax.ShapeDtypeStruct((s, 2048), BF16),
                   f32(CONV_HALO, CONV_WIDTH), f32(1, 512), f32(1, 512), f32(1, 512), f32(1, 512),
                   f32(1, 512), f32(GM_HEADS, CHUNK, CHUNK), f32(CHUNK, GM_WIDTH), f32(1, 2048),
                   f32(1, D_MODEL)],
        scratch_shapes=[pltpu.VMEM((t + CONV_HALO, CONV_WIDTH), F32),
                        pltpu.VMEM((SUBLANES - 1, t + CONV_HALO - SUBLANES, CONV_WIDTH), F32),
                        pltpu.VMEM((t, GM_WIDTH), F32), pltpu.VMEM((t, GM_WIDTH), F32)],
        compiler_params=_params(("arbitrary",)),
    )(dh1, x, z, c1, w_out, w_in, g_mix, cw, lng, lnb, gg, gb, wpair, wpair_t, bias)


def _head_bias_grad(dbs):
    def body(d_ref, o_ref):
        dv = d_ref[...]
        lane = lax.broadcasted_iota(jnp.int32, (CHUNK, LANES), 1)
        acc = jnp.zeros((CHUNK, LANES), F32)
        for h in range(GM_HEADS):
            sh = jnp.sum(dv[:, h * GM_HEAD_DIM:(h + 1) * GM_HEAD_DIM], axis=-1, keepdims=True)
            acc = acc + jnp.where(lane == h, sh, 0.0)
        o_ref[...] = acc

    return pl.pallas_call(body, name="head_bias_grad",
                          out_shape=jax.ShapeDtypeStruct((CHUNK, LANES), F32))(dbs)


def kernel(x, mem, norm_mix_g, w_in, b_in, conv_w, conv_b, conv_ln_g, conv_ln_b, gm_ln_g, gm_ln_b, gm_w_s, gm_b_s, w_out, norm_xa_g, mem_norm_g, xa_wq, xa_wkv, xa_wo, norm_ffn_g, ffn_w_gate_up, ffn_w_down, final_norm_g, loss_target, m_norm_mix_g, m_w_in, m_b_in, m_conv_w, m_conv_b, m_conv_ln_g, m_conv_ln_b, m_gm_ln_g, m_gm_ln_b, m_gm_w_s, m_gm_b_s, m_w_out, m_norm_xa_g, m_mem_norm_g, m_xa_wq, m_xa_wkv, m_xa_wo, m_norm_ffn_g, m_ffn_w_gate_up, m_ffn_w_down, m_final_norm_g, v_norm_mix_g, v_w_in, v_b_in, v_conv_w, v_conv_b, v_conv_ln_g, v_conv_ln_b, v_gm_ln_g, v_gm_ln_b, v_gm_w_s, v_gm_b_s, v_w_out, v_norm_xa_g, v_mem_norm_g, v_xa_wq, v_xa_wkv, v_xa_wo, v_norm_ffn_g, v_ffn_w_gate_up, v_ffn_w_down, v_final_norm_g):
    weights = dict(norm_mix_g=norm_mix_g, w_in=w_in, b_in=b_in, conv_w=conv_w, conv_b=conv_b, conv_ln_g=conv_ln_g,
                   conv_ln_b=conv_ln_b, gm_ln_g=gm_ln_g, gm_ln_b=gm_ln_b, gm_w_s=gm_w_s, gm_b_s=gm_b_s, w_out=w_out,
                   norm_xa_g=norm_xa_g, mem_norm_g=mem_norm_g, xa_wq=xa_wq, xa_wkv=xa_wkv, xa_wo=xa_wo,
                   norm_ffn_g=norm_ffn_g, ffn_w_gate_up=ffn_w_gate_up, ffn_w_down=ffn_w_down,
                   final_norm_g=final_norm_g)
    m_in = dict(norm_mix_g=m_norm_mix_g, w_in=m_w_in, b_in=m_b_in, conv_w=m_conv_w, conv_b=m_conv_b,
                conv_ln_g=m_conv_ln_g, conv_ln_b=m_conv_ln_b, gm_ln_g=m_gm_ln_g, gm_ln_b=m_gm_ln_b, gm_w_s=m_gm_w_s,
                gm_b_s=m_gm_b_s, w_out=m_w_out, norm_xa_g=m_norm_xa_g, mem_norm_g=m_mem_norm_g, xa_wq=m_xa_wq,
                xa_wkv=m_xa_wkv, xa_wo=m_xa_wo, norm_ffn_g=m_norm_ffn_g, ffn_w_gate_up=m_ffn_w_gate_up,
                ffn_w_down=m_ffn_w_down, final_norm_g=m_final_norm_g)
    v_in = dict(norm_mix_g=v_norm_mix_g, w_in=v_w_in, b_in=v_b_in, conv_w=v_conv_w, conv_b=v_conv_b,
                conv_ln_g=v_conv_ln_g, conv_ln_b=v_conv_ln_b, gm_ln_g=v_gm_ln_g, gm_ln_b=v_gm_ln_b, gm_w_s=v_gm_w_s,
                gm_b_s=v_gm_b_s, w_out=v_w_out, norm_xa_g=v_norm_xa_g, mem_norm_g=v_mem_norm_g, xa_wq=v_xa_wq,
                xa_wkv=v_xa_wkv, xa_wo=v_xa_wo, norm_ffn_g=v_norm_ffn_g, ffn_w_gate_up=v_ffn_w_gate_up,
                ffn_w_down=v_ffn_w_down, final_norm_g=v_final_norm_g)
    grads, delta, new_m, new_v = {}, {}, {}, {}

    s = x.shape[1]
    ts = _row_tile(s)
    tb = max(CHUNK, ts // 2)
    tw = 2 * ts if s % (2 * ts) == 0 and ts >= 512 else ts
    cx, cy, cc = _mesh_pos()
    chip = 2 * cx + cy
    pos = jnp.stack([chip, cc]).astype(jnp.int32)
    row = lambda a: a.reshape(1, -1)
    x2, mem2, tgt2 = x[0], mem[0], loss_target[0]

    big = dict(w_in=w_in, xa_wkv=xa_wkv, w_out=w_out, xa_wq=xa_wq, xa_wo=xa_wo,
               ffn_w_gate_up=ffn_w_gate_up, ffn_w_down=ffn_w_down)
    big_names = list(big)
    halves = lambda a: a.reshape(2, a.shape[0] // 2, a.shape[1])
    conv_w_pad = jnp.pad(conv_w, ((0, CONV_HALO - CONV_KERNEL), (0, 0)))
    first_names = ["w_in", "conv_w"]
    later_names = [nm for nm in big_names if nm != "w_in"]
    cast = dict(zip(first_names, _cast_into_slots([halves(w_in), halves(conv_w_pad)], pos, [BF16, F32], "cast_w_in")))
    cast.update(zip(later_names, _cast_into_slots([halves(big[nm]) for nm in later_names], pos,
                                                  [BF16] * len(later_names), "cast_" + later_names[0],
                                                  side_by_side=(later_names.index("ffn_w_gate_up"),))))

    def start_gather(names, after):
        return _gather_start([cast[nm] for nm in names], "gather_start_" + names[0], after)

    def land_gather(names, started, after):
        send_sems, recv_sems, bufs, _ = started
        return _gather_wait(send_sems, recv_sems, bufs, after, "gather_wait_" + names[0])

    as_weights = lambda names, bufs: dict(zip(names, (b.reshape(b.shape[0], -1, b.shape[-1]) for b in bufs)))

    def start_share(names, landed):
        return _pass_start(landed, "pass_start_" + names[0])

    def share_gather(names, passing, which, after):
        send_sems, recv_sems, bufs, _ = passing
        sems = lambda s: [s[3 * a + k] for a in which for k in range(3)]
        picked = [names[a] for a in which]
        return as_weights(picked, _gather_wait(sems(send_sems), sems(recv_sems), [bufs[a] for a in which], after,
                                               "pass_wait_" + picked[0], _pass_descriptors))

    tril = jnp.tril(jnp.ones((CHUNK, CHUNK), dtype=bool))
    ws = jnp.where(tril[None], gm_w_s, 0.0)
    wpair = jnp.concatenate([ws[0::2], ws[1::2]], axis=2).astype(BF16)
    ws_t = jnp.swapaxes(ws, 1, 2)
    wpair_t = jnp.concatenate([ws_t[0::2], ws_t[1::2]], axis=2).astype(BF16)
    bias = jnp.repeat(gm_b_s.T, GM_HEAD_DIM, axis=1)

    attn_names = ["xa_wkv", "w_out", "xa_wq", "xa_wo"]
    gather_first = start_gather(first_names, ())
    hn1 = _norm_in(x2, row(norm_mix_g), tw, after=(gather_first[3], wpair, wpair_t, bias))
    landed = land_gather(first_names, gather_first, [cast[nm] for nm in later_names] + [hn1])
    passing = start_share(first_names, landed)
    gather_attn = start_gather(attn_names, passing[3])
    gw = share_gather(first_names, passing, (0, 1), gather_attn[3])
    w_in_g = gw["w_in"]
    cw_g = jnp.concatenate([gw["conv_w"][k] for k in range(N_CHIPS)], axis=1)

    z, mix, c1 = _seqmix_fwd(hn1, w_in_g, row(b_in), cw_g, row(conv_b), row(conv_ln_g), row(conv_ln_b),
                             row(gm_ln_g), row(gm_ln_b), wpair, bias, ts)
    landed = land_gather(attn_names, gather_attn, mix)
    passing = start_share(attn_names, landed)
    gather_gu = start_gather(["ffn_w_gate_up"], passing[3])
    wkv_g = share_gather(attn_names, passing, (0,), gather_gu[3])["xa_wkv"]
    mn, kv = _mem_kv(mem2, row(mem_norm_g), wkv_g)
    gw = share_gather(attn_names, passing, (1, 2, 3), kv)
    w_out_g = gw["w_out"].reshape(D_MODEL, D_MODEL)
    wq_g = gw["xa_wq"].reshape(D_MODEL, D_MODEL)
    wo_g = gw["xa_wo"].reshape(D_MODEL, D_MODEL)
    h1, hn2, q, o, h2, hn3 = _attn_block_fwd(x2, mix, w_out_g, row(norm_xa_g), wq_g, kv, wo_g, row(norm_ffn_g), ts)
    landed = land_gather(["ffn_w_gate_up"], gather_gu, hn3)
    passing = start_share(["ffn_w_gate_up"], landed)
    gather_down = start_gather(["ffn_w_down"], passing[3])
    wgu_g = share_gather(["ffn_w_gate_up"], passing, (0,), gather_down[3])["ffn_w_gate_up"]
    gu, act = _ffn_up(hn3, wgu_g, ts)
    landed = land_gather(["ffn_w_down"], gather_down, act)
    wd_g = as_weights(["ffn_w_down"], _pass_to_sibling(landed, "pass_ffn_w_down"))["ffn_w_down"].reshape(
        FFN_HIDDEN, D_MODEL)
    dh3, dh3_b, sq, d_final_g = _ffn_down_loss(act, wd_g, h2, row(final_norm_g), tgt2, ts)
    loss_here = jnp.broadcast_to(0.5 * jnp.sum(sq) / D_MODEL, (1, 2, SUBLANES, LANES))

    def split(g, nm):
        r, c = big[nm].shape
        return g.reshape(N_CHIPS, 2, r // 2, c)

    def chip_sums(group, arrays, got):
        sums, parts = [None] * len(group), [None] * len(group)
        for blocks in (N_CHIPS, 1):
            idx = [i for i, a in enumerate(arrays) if a.shape[0] == blocks]
            if idx:
                out = _add_halves([arrays[i] for i in idx], [got[i] for i in idx], pos, "chip_sum_" + group[idx[0]],
                                  [F32 if group[i] == "loss" else BF16 for i in idx])
                for k, i in enumerate(idx):
                    sums[i], parts[i] = out[0][k], out[1][k]
        return sums, parts

    def start_swap(group, grads):
        return _swap_start([split(g, nm) for g, nm in zip(grads, group)], "swap_start_" + group[0])

    def start_exchange(group, swapping, after, landed):
        arrays, got = _swap_wait(*swapping[:3], after, "swap_wait_" + group[0])
        sums, parts = chip_sums(group, arrays, got)
        return _exchange_start(sums, parts, "exchange_start_" + group[0], landed)

    def wait_exchange(group, started, after):
        sems, sums, parts, _ = started
        return _exchange_wait(sems, sums, parts, after, "exchange_wait_" + group[0])

    def finish_exchange(group, started, after):
        return _sum_chips(wait_exchange(group, started, after), pos, "total_" + group[0])

    def join(group, after):
        return _join_halves([halves_of[nm] for nm in group], "join_halves_" + group[0], after)

    def update(group, joined, after=()):
        outs = _adamw([(weights[nm], j.reshape(big[nm].shape), m_in[nm], v_in[nm]) for nm, j in zip(group, joined)],
                      "adamw_" + group[0], after)
        for nm, out in zip(group, outs):
            grads[nm], delta[nm], new_m[nm], new_v[nm] = out
        return [new_v[nm] for nm in group]

    def join_and_update(group, after):
        return update(group, join(group, after))

    as3 = lambda a: a.reshape((1,) + a.shape)
    halves_of = {}

    g_down = _grad_w(act, as3(dh3_b), FFN_HALF, D_MODEL, "grad_ffn_w_down")
    group_a = ["ffn_w_down"]
    swap_a = start_swap(group_a, [g_down])
    dgu, dh2, dh2_b, d_ffn_g = _ffn_bwd(dh3, wd_g, gu, wgu_g, h2, row(norm_ffn_g), tb,
                                        after=swap_a[3])
    exch_a = start_exchange(group_a, swap_a, dh2, wd_g)
    g_gu = _grad_w(hn3, dgu, D_MODEL, FFN_HALF, "grad_ffn_w_gate_up", after=exch_a[3])
    halves_of.update(zip(group_a, finish_exchange(group_a, exch_a, g_gu)))

    group_b = ["ffn_w_gate_up"]
    swap_b = start_swap(group_b, [g_gu])
    dh1, dh1_b, dq, dkv, d_xa_g = _attn_bwd(dh2, wo_g, q, kv, wq_g, h1, row(norm_xa_g), ts, after=swap_b[3])
    exch_b = start_exchange(group_b, swap_b, dh1, [halves_of[nm] for nm in group_a])
    g_wkv, d_mem_g = _mem_kv_bwd(dkv, mn, wkv_g, mem2, row(mem_norm_g), after=exch_b[3])
    g_wo, g_wq, g_wout = _grad_w_square([(o, dh2_b), (hn2, dq), (mix, dh1_b)], "grad_xa_wo", after=exch_b[3])
    done_a = join_and_update(group_a, (g_wkv, g_wo, g_wq, g_wout))
    halves_of.update(zip(group_b, finish_exchange(group_b, exch_b, done_a)))

    group_c = ["xa_wo", "xa_wq", "xa_wkv", "w_out"]
    swap_c = start_swap(group_c, [g_wo, g_wq, g_wkv, g_wout])
    (gx, dz, d_cw, d_cb, d_lng, d_lnb, d_gg, d_gb, d_ws, d_bs_sum, d_bin, d_mix_g) = _seqmix_bwd(
        dh1, x2, z, c1, w_out_g, w_in_g, row(norm_mix_g), cw_g, row(conv_ln_g), row(conv_ln_b),
        row(gm_ln_g), row(gm_ln_b), wpair, wpair_t, bias, tb, after=swap_c[3])
    d_bs = _head_bias_grad(d_bs_sum)[:, :GM_HEADS].T
    exch_c = start_exchange(group_c, swap_c, dz, [halves_of[nm] for nm in group_b])
    g_win = _grad_w(hn1, as3(dz), D_MODEL, 1024, "grad_w_in", after=exch_c[3], shards=2)

    small_names = ["norm_mix_g", "b_in", "conv_w", "conv_b", "conv_ln_g", "conv_ln_b", "gm_ln_g", "gm_ln_b",
                   "gm_w_s", "gm_b_s", "norm_xa_g", "mem_norm_g", "norm_ffn_g", "final_norm_g"]
    d_cw_by_chip = jnp.swapaxes(d_cw.reshape(CONV_HALO, N_CHIPS, LANES), 0, 1).reshape(-1, LANES)
    small_grads = dict(norm_mix_g=d_mix_g, b_in=d_bin, conv_w=d_cw_by_chip, conv_b=d_cb, conv_ln_g=d_lng,
                       conv_ln_b=d_lnb, gm_ln_g=d_gg, gm_ln_b=d_gb, gm_w_s=d_ws, gm_b_s=d_bs, norm_xa_g=d_xa_g,
                       mem_norm_g=d_mem_g, norm_ffn_g=d_ffn_g, final_norm_g=d_final_g)

    def rows_form(a):
        a = a.reshape(-1, LANES)
        return jnp.pad(a, ((0, -a.shape[0] % SUBLANES), (0, 0)))

    pieces = [rows_form(small_grads[nm]) for nm in small_names]
    offsets, total = [], 0
    for p in pieces:
        offsets.append(total)
        total += p.shape[0]
    pack_rows = -(-total // 32) * 32
    small_pack = jnp.pad(jnp.concatenate(pieces, axis=0), ((0, pack_rows - total), (0, 0)))

    group_d = ["w_in", "small", "loss"]
    joined_b = join(group_b, g_win)
    swap_d = _swap_start([split(g_win, "w_in")], "swap_start_w_in", joined_b)
    done_b = update(group_b, joined_b, swap_d[3])
    small_d = [small_pack.reshape(1, 2, pack_rows // 2, LANES), loss_here]
    got_small = _swap_halves(small_d, "swap_halves_small", done_b)
    arrays_d, got_d = _swap_wait(*swap_d[:3], got_small, "swap_wait_w_in")
    sums_d, parts_d = chip_sums(group_d, arrays_d + small_d, got_d + list(got_small))
    parts_c = wait_exchange(group_c, exch_c, sums_d)
    exch_d = _exchange_start(sums_d, parts_d, "exchange_start_w_in", parts_c)
    halves_of.update(zip(group_c, _sum_chips(parts_c, pos, "total_xa_wo", exch_d[3])))
    done_c = join_and_update(group_c, exch_d[3])
    halves_of.update(zip(group_d, finish_exchange(group_d, exch_d, done_c)))
    joined_d = _join_halves([halves_of[nm] for nm in group_d], "join_halves_w_in")
    loss = joined_d[2][0, 0, 0]
    grads["w_in"], delta["w_in"], new_m["w_in"], new_v["w_in"] = _adamw(
        [(w_in, joined_d[0].reshape(w_in.shape), m_w_in, v_w_in)], "adamw_w_in")[0]

    local_rows = lambda a, nm: a if nm == "conv_w" else a.reshape(-1, LANES)
    params = [tuple(local_rows(src[nm], nm) for src in (weights, m_in, v_in)) for nm in small_names]
    outs = _adamw_small(joined_d[1].reshape(pack_rows, LANES), pos, params, offsets, small_names.index("conv_w"))
    for k, nm in enumerate(small_names):
        for dst, a in zip((grads, delta, new_m, new_v), outs[4 * k:4 * k + 4]):
            dst[nm] = a

    order = ["norm_mix_g", "w_in", "b_in", "conv_w", "conv_b", "conv_ln_g", "conv_ln_b", "gm_ln_g", "gm_ln_b",
             "gm_w_s", "gm_b_s", "w_out", "norm_xa_g", "mem_norm_g", "xa_wq", "xa_wkv", "xa_wo", "norm_ffn_g",
             "ffn_w_gate_up", "ffn_w_down", "final_norm_g"]
    fit = lambda a, nm: a.reshape(weights[nm].shape)
    return (loss, gx.reshape(x.shape),
            *[fit(grads[nm], nm) for nm in order], *[fit(delta[nm], nm) for nm in order],
            *[fit(new_m[nm], nm) for nm in order], *[fit(new_v[nm], nm) for nm in order])
```

```python
import functools

import jax
import jax.numpy as jnp
from jax import lax
from jax.experimental import pallas as pl
from jax.experimental.pallas import tpu as pltpu

F32 = jnp.float32
BF16 = jnp.bfloat16

D_MODEL = 1024
CONV_WIDTH = 512
GM_WIDTH = 512
CONV_KERNEL = 31
CONV_HALO = 32
GRAD_ROWS = 2048
CHUNK = 128
GM_HEADS = 8
GM_HEAD_DIM = 64
XA_HEADS = 4
XA_HEAD_DIM = 256
FFN_HIDDEN = 2816
FFN_HALF = FFN_HIDDEN // 2
RMS_EPS = 1e-6
LN_EPS = 1e-5
N_CHIPS = 4
LANES = 128
SUBLANES = 8

ADAM_LR = 0.001
ADAM_B1 = 0.9
ADAM_B2 = 0.999
ADAM_EPS = 1e-08
ADAM_WD = 0.01
ADAM_STEP = 10

VMEM_LIMIT_BYTES = 56 * 1024 * 1024
MESH = pl.DeviceIdType.MESH
ANY = pl.BlockSpec(memory_space=pl.ANY)
HBM_SPEC = pl.BlockSpec(memory_space=pltpu.HBM)
SEM_SPEC = pl.BlockSpec(memory_space=pltpu.SEMAPHORE)

_NT = (((1,), (1,)), ((), ()))
_TN = (((0,), (0,)), ((), ()))
_GELU_C = 0.7978845608028654
_GELU_A = 0.044715


def _dot(a, b):
    return jnp.dot(a, b, preferred_element_type=F32)


def _dot_nt(a, b):
    return lax.dot_general(a, b, _NT, preferred_element_type=F32)


def _dot_tn(a, b):
    return lax.dot_general(a, b, _TN, preferred_element_type=F32)


def _mean(v):
    return jnp.mean(v, axis=-1, keepdims=True)


def _rowsum(v):
    return jnp.sum(v, axis=0, keepdims=True)


def _sigmoid(v):
    return 1.0 / (1.0 + jnp.exp(-v))


def _gelu_parts(v):
    v2 = v * v
    t = jnp.tanh(_GELU_C * (v + _GELU_A * v * v2))
    g = 0.5 * v * (1.0 + t)
    dg = 0.5 * (1.0 + t) + 0.5 * v * (1.0 - t * t) * (_GELU_C * (1.0 + 3.0 * _GELU_A * v2))
    return g, dg


def _rms_stats(v):
    return lax.rsqrt(_mean(v * v) + RMS_EPS)


def _rms_bwd(dy, v, r, g):
    n = v * r
    dn = dy * g
    dv = r * (dn - n * _mean(dn * n))
    return dv, _rowsum(dy * n)


def _ln_stats(v):
    mu = _mean(v)
    xc = v - mu
    rs = lax.rsqrt(_mean(xc * xc) + LN_EPS)
    return xc * rs, rs


def _ln_bwd(dy, xh, rs, g):
    dxh = dy * g
    dv = rs * (dxh - _mean(dxh) - xh * _mean(dxh * xh))
    return dv, _rowsum(dy * xh), _rowsum(dy)


def _params(sem):
    return pltpu.CompilerParams(dimension_semantics=sem, vmem_limit_bytes=VMEM_LIMIT_BYTES)


def _row_tile(s):
    return 512 if s % 512 == 0 and s >= 2048 else 128


def _mesh_pos():
    return lax.axis_index("x"), lax.axis_index("y"), lax.axis_index("c")


def _slot(buf, chip_idx, half):
    if buf.shape[0] == N_CHIPS:
        return buf.at[chip_idx, half]
    width = buf.shape[-1] // 2
    return buf.at[chip_idx // 2, half, :, pl.ds(pl.multiple_of((chip_idx % 2) * width, LANES), width)]


def _cast_into_slots(ws, pos, dtypes, name, side_by_side=()):
    n = len(ws)

    def body(pos_ref, *refs):
        for a in range(n):
            refs[n + a][0] = refs[a][...].astype(dtypes[a])

    def out_spec(a, w):
        if a in side_by_side:
            return pl.BlockSpec((1, 1) + w.shape[1:], lambda i, p: (p[0] // 2, i, 0, p[0] % 2))
        return pl.BlockSpec((1, 1) + w.shape[1:], lambda i, p: (p[0], i, 0, 0))

    def out_shape(a, w):
        if a in side_by_side:
            return (2, 2, w.shape[1], 2 * w.shape[2])
        return (N_CHIPS,) + w.shape

    return pl.pallas_call(
        body, name=name,
        grid_spec=pltpu.PrefetchScalarGridSpec(
            num_scalar_prefetch=1, grid=(2,),
            in_specs=[pl.BlockSpec((1,) + w.shape[1:], lambda i, p: (i, 0, 0)) for w in ws],
            out_specs=[out_spec(a, w) for a, w in enumerate(ws)]),
        out_shape=[jax.ShapeDtypeStruct(out_shape(a, w), dt) for a, (w, dt) in enumerate(zip(ws, dtypes))],
        compiler_params=_params(("parallel",)),
    )(pos, *ws)


def _adam_update(w, g, m, v):
    nm = ADAM_B1 * m + (1.0 - ADAM_B1) * g
    nv = ADAM_B2 * v + (1.0 - ADAM_B2) * (g * g)
    m_hat = nm / (1.0 - ADAM_B1 ** ADAM_STEP)
    v_hat = nv / (1.0 - ADAM_B2 ** ADAM_STEP)
    return -ADAM_LR * (m_hat / (jnp.sqrt(v_hat) + ADAM_EPS) + ADAM_WD * w), nm, nv


ADAM_STEPS = 4


def _adamw(quads, name, after=()):
    n = len(quads)

    def body(*refs):
        ins, outs = refs[:4 * n], refs[4 * n:]
        for a in range(n):
            w, g, m, v = (r[...] for r in ins[4 * a:4 * a + 4])
            outs[4 * a][...] = g
            outs[4 * a + 1][...], outs[4 * a + 2][...], outs[4 * a + 3][...] = _adam_update(w, g, m, v)

    specs = [pl.BlockSpec((q[0].shape[0] // ADAM_STEPS, q[0].shape[1]), lambda i: (i, 0)) for q in quads]
    out = _tied_call(
        body, after, name=name, grid=(ADAM_STEPS,),
        in_specs=[sp for sp in specs for _ in range(4)], out_specs=[sp for sp in specs for _ in range(4)],
        out_shape=[jax.ShapeDtypeStruct(q[0].shape, F32) for q in quads for _ in range(4)],
        compiler_params=_params(("parallel",)),
    )(*[a for q in quads for a in q])
    return [tuple(out[4 * a:4 * a + 4]) for a in range(n)]


def _adamw_small(gpack, pos, params, offsets, conv_at):
    n = len(params)

    def body(pos_ref, g_ref, *refs):
        ins, outs = refs[:3 * n], refs[3 * n:]
        for k in range(n):
            rows = params[k][0].shape[0]
            start = offsets[k]
            if k == conv_at:
                start = pl.multiple_of(start + pos_ref[0] * CONV_HALO, SUBLANES)
            g = g_ref[pl.ds(start, rows), :]
            outs[4 * k][...] = g
            outs[4 * k + 1][...], outs[4 * k + 2][...], outs[4 * k + 3][...] = _adam_update(
                ins[3 * k][...], g, ins[3 * k + 1][...], ins[3 * k + 2][...])

    flat = [a for p in params for a in p]
    vmem = pl.BlockSpec(memory_space=pltpu.VMEM)
    return pl.pallas_call(
        body, name="adamw_small",
        in_specs=[pl.BlockSpec(memory_space=pltpu.SMEM), vmem] + [vmem] * len(flat),
        out_specs=[vmem] * (4 * n),
        out_shape=[jax.ShapeDtypeStruct(p[0].shape, F32) for p in params for _ in range(4)],
    )(pos, gpack, *flat)


def _as_tuple(after):
    return tuple(after) if isinstance(after, (tuple, list)) else (after,)


def _tied_call(body, after, *, in_specs, **kwargs):
    after = _as_tuple(after)
    n_in, n_after = len(in_specs), len(after)

    def tied(*refs):
        body(*refs[:n_in], *refs[n_in + n_after:])

    call = pl.pallas_call(tied, in_specs=list(in_specs) + [ANY] * n_after, **kwargs)
    return lambda *operands: call(*operands, *after)


def _other_chips(x, y):
    return [(1 - x, y), (x, 1 - y), (1 - x, 1 - y)]


def _gather_descriptors(bufs, send_of, recv_of):
    x, y, c = _mesh_pos()
    me = 2 * x + y
    chips = _other_chips(x, y)
    sends, arrivals = [], []
    for a in range(len(bufs)):
        for k in range(3):
            ck = 2 * chips[k][0] + chips[k][1]

            def copy(slot, a=a, k=k):
                return pltpu.make_async_remote_copy(
                    src_ref=_slot(bufs[a], slot, c), dst_ref=_slot(bufs[a], slot, c),
                    send_sem=send_of(a, k), recv_sem=recv_of(a, k),
                    device_id=(*chips[k], c), device_id_type=MESH)

            sends.append(functools.partial(copy, me))
            arrivals.append(functools.partial(copy, ck))
    return sends, arrivals


def _gather_start(bufs, name, after=()):
    n = len(bufs)
    ns = 3 * n

    def body(*refs):
        sems = refs[n:n + 2 * ns]
        thru = refs[n + 2 * ns:2 * n + 2 * ns]
        token = refs[2 * n + 2 * ns]
        _chips_handshake()
        sends, _ = _gather_descriptors(thru, lambda a, k: sems[3 * a + k], lambda a, k: sems[ns + 3 * a + k])
        for cp in sends:
            cp().start()
        token[...] = jnp.zeros_like(token)

    held = [pltpu.with_memory_space_constraint(b, pltpu.HBM) for b in bufs]
    out = _tied_call(
        body, after, name=name,
        out_shape=(*[pltpu.SemaphoreType.DMA(())] * (2 * ns), *[pltpu.HBM(b.shape, b.dtype) for b in held],
                   jax.ShapeDtypeStruct((8, LANES), F32)),
        in_specs=[HBM_SPEC] * n,
        out_specs=(*[SEM_SPEC] * (2 * ns), *[HBM_SPEC] * n, pl.BlockSpec(memory_space=pltpu.VMEM)),
        input_output_aliases={i: 2 * ns + i for i in range(n)},
        compiler_params=pltpu.CompilerParams(has_side_effects=pltpu.SideEffectType.DATAFLOW_SIDE_EFFECTING,
                                             collective_id=CHIPS_COLLECTIVE_ID),
    )(*held)
    return list(out[:ns]), list(out[ns:2 * ns]), list(out[2 * ns:2 * ns + n]), out[2 * ns + n]


def _gather_wait(send_sems, recv_sems, bufs, after, name, descriptors=_gather_descriptors):
    n = len(bufs)
    ns = 3 * n

    def body(*refs):
        buf_ref = refs[:n]
        sem_ref = refs[n:n + 2 * ns]
        sends, arrivals = descriptors(buf_ref, lambda a, k: sem_ref[3 * a + k], lambda a, k: sem_ref[ns + 3 * a + k])
        for cp in sends:
            cp().wait_send()
        for cp in arrivals:
            cp().wait_recv()

    out = pl.pallas_call(
        body, name=name,
        out_shape=tuple(pltpu.HBM(b.shape, b.dtype) for b in bufs),
        in_specs=[HBM_SPEC] * n + [SEM_SPEC] * (2 * ns) + [ANY] * len(_as_tuple(after)),
        out_specs=tuple([HBM_SPEC] * n),
        input_output_aliases={i: i for i in range(n)},
        compiler_params=pltpu.CompilerParams(has_side_effects=pltpu.SideEffectType.DATAFLOW_SIDE_EFFECTING),
    )(*bufs, *send_sems, *recv_sems, *_as_tuple(after))
    return list(out)


SIBLING_COLLECTIVE_ID = 0


def _sibling_handshake():
    x, y, c = _mesh_pos()
    barrier = pltpu.get_barrier_semaphore()
    pl.semaphore_signal(barrier, inc=1, device_id=(x, y, 1 - c), device_id_type=MESH)
    pl.semaphore_wait(barrier, 1)


CHIPS_COLLECTIVE_ID = 1


def _chips_handshake():
    x, y, c = _mesh_pos()
    barrier = pltpu.get_barrier_semaphore()
    for chip in _other_chips(x, y):
        pl.semaphore_signal(barrier, inc=1, device_id=(*chip, c), device_id_type=MESH)
    pl.semaphore_wait(barrier, 3)


def _pass_descriptors(bufs, send_of, recv_of):
    x, y, c = _mesh_pos()
    chips = _other_chips(x, y)

    def half(a, k, which):
        ck = 2 * chips[k][0] + chips[k][1]
        return functools.partial(
            pltpu.make_async_remote_copy,
            src_ref=_slot(bufs[a], ck, which), dst_ref=_slot(bufs[a], ck, which),
            send_sem=send_of(a, k), recv_sem=recv_of(a, k),
            device_id=(x, y, 1 - c), device_id_type=MESH)

    pairs = [(a, k) for a in range(len(bufs)) for k in range(3)]
    return [half(a, k, c) for a, k in pairs], [half(a, k, 1 - c) for a, k in pairs]


def _pass_start(bufs, name, after=()):
    n = len(bufs)
    ns = 3 * n

    def body(*refs):
        sems = refs[n:n + 2 * ns]
        thru = refs[n + 2 * ns:2 * n + 2 * ns]
        token = refs[2 * n + 2 * ns]
        _sibling_handshake()
        sends, _ = _pass_descriptors(thru, lambda a, k: sems[3 * a + k], lambda a, k: sems[ns + 3 * a + k])
        for cp in sends:
            cp().start()
        token[...] = jnp.zeros_like(token)

    held = [pltpu.with_memory_space_constraint(b, pltpu.HBM) for b in bufs]
    out = _tied_call(
        body, after, name=name,
        out_shape=(*[pltpu.SemaphoreType.DMA(())] * (2 * ns), *[pltpu.HBM(b.shape, b.dtype) for b in held],
                   jax.ShapeDtypeStruct((8, LANES), F32)),
        in_specs=[HBM_SPEC] * n,
        out_specs=(*[SEM_SPEC] * (2 * ns), *[HBM_SPEC] * n, pl.BlockSpec(memory_space=pltpu.VMEM)),
        input_output_aliases={i: 2 * ns + i for i in range(n)},
        compiler_params=pltpu.CompilerParams(has_side_effects=pltpu.SideEffectType.DATAFLOW_SIDE_EFFECTING,
                                             collective_id=SIBLING_COLLECTIVE_ID),
    )(*held)
    return list(out[:ns]), list(out[ns:2 * ns]), list(out[2 * ns:2 * ns + n]), out[2 * ns + n]


def _pass_to_sibling(bufs, name, after=()):
    n = len(bufs)

    def body(*refs):
        outs = refs[n:2 * n]
        send_sem, recv_sem = refs[2 * n:]
        _sibling_handshake()
        sends, arrivals = _pass_descriptors(outs, lambda a, k: send_sem.at[a, k], lambda a, k: recv_sem.at[a, k])
        sends = [cp() for cp in sends]
        for cp in sends:
            cp.start()
        for cp in arrivals:
            cp().wait_recv()
        for cp in sends:
            cp.wait_send()

    return _tied_call(
        body, after, name=name,
        in_specs=[ANY] * n, out_specs=[ANY] * n,
        out_shape=[jax.ShapeDtypeStruct(b.shape, b.dtype) for b in bufs],
        input_output_aliases={a: a for a in range(n)},
        scratch_shapes=[pltpu.SemaphoreType.DMA((n, 3))] * 2,
        compiler_params=pltpu.CompilerParams(collective_id=SIBLING_COLLECTIVE_ID),
    )(*bufs)


def _swap_descriptors(grads, lands, send_of, recv_of):
    x, y, c = _mesh_pos()
    return [functools.partial(
        pltpu.make_async_remote_copy,
        src_ref=grads[a].at[:, pl.ds(1 - c, 1)], dst_ref=lands[a],
        send_sem=send_of(a), recv_sem=recv_of(a),
        device_id=(x, y, 1 - c), device_id_type=MESH) for a in range(len(grads))]


def _swap_halves(grads, name, after=()):
    n = len(grads)

    def body(*refs):
        ins, outs = refs[:n], refs[n:2 * n]
        send_sem, recv_sem = refs[2 * n:]
        _sibling_handshake()
        cps = [cp() for cp in _swap_descriptors(ins, outs, lambda a: send_sem.at[a], lambda a: recv_sem.at[a])]
        for cp in cps:
            cp.start()
        for cp in cps:
            cp.wait()

    out_shape = [jax.ShapeDtypeStruct((g.shape[0], 1) + g.shape[2:], g.dtype) for g in grads]
    return _tied_call(
        body, after, name=name,
        in_specs=[ANY] * n, out_specs=[ANY] * n, out_shape=out_shape,
        scratch_shapes=[pltpu.SemaphoreType.DMA((n,))] * 2,
        compiler_params=pltpu.CompilerParams(collective_id=SIBLING_COLLECTIVE_ID),
    )(*grads)


def _swap_start(grads, name, after=()):
    n, after = len(grads), _as_tuple(after)

    def body(*refs):
        outs = refs[2 * n + len(after):]
        sems, g_thru, l_thru, token = outs[:2 * n], outs[2 * n:3 * n], outs[3 * n:4 * n], outs[4 * n]
        _sibling_handshake()
        for cp in _swap_descriptors(g_thru, l_thru, lambda a: sems[a], lambda a: sems[n + a]):
            cp().start()
        token[...] = jnp.zeros_like(token)

    lands = [lax.empty((g.shape[0], 1) + g.shape[2:], g.dtype) for g in grads]
    held = [pltpu.with_memory_space_constraint(a, pltpu.HBM) for a in (*grads, *lands)]
    out = pl.pallas_call(
        body, name=name,
        out_shape=(*[pltpu.SemaphoreType.DMA(())] * (2 * n), *[pltpu.HBM(a.shape, a.dtype) for a in held],
                   jax.ShapeDtypeStruct((8, LANES), F32)),
        in_specs=[HBM_SPEC] * (2 * n) + [ANY] * len(after),
        out_specs=(*[SEM_SPEC] * (2 * n), *[HBM_SPEC] * (2 * n), pl.BlockSpec(memory_space=pltpu.VMEM)),
        input_output_aliases={i: 2 * n + i for i in range(2 * n)},
        compiler_params=pltpu.CompilerParams(has_side_effects=pltpu.SideEffectType.DATAFLOW_SIDE_EFFECTING,
                                             collective_id=SIBLING_COLLECTIVE_ID),
    )(*held, *after)
    return list(out[:2 * n]), list(out[2 * n:3 * n]), list(out[3 * n:4 * n]), out[4 * n]


def _swap_wait(sems, grads, lands, after, name):
    n = len(grads)

    def body(*refs):
        g_ref, l_ref = refs[:n], refs[n:2 * n]
        sem_ref = refs[2 * n:4 * n]
        for cp in _swap_descriptors(g_ref, l_ref, lambda a: sem_ref[a], lambda a: sem_ref[n + a]):
            cp().wait()

    out = pl.pallas_call(
        body, name=name,
        out_shape=tuple(pltpu.HBM(a.shape, a.dtype) for a in (*grads, *lands)),
        in_specs=[HBM_SPEC] * (2 * n) + [SEM_SPEC] * (2 * n) + [ANY] * len(_as_tuple(after)),
        out_specs=tuple([HBM_SPEC] * (2 * n)),
        input_output_aliases={i: i for i in range(2 * n)},
        compiler_params=pltpu.CompilerParams(has_side_effects=pltpu.SideEffectType.DATAFLOW_SIDE_EFFECTING),
    )(*grads, *lands, *sems, *_as_tuple(after))
    return list(out[:n]), list(out[n:])


def _add_halves(gs, gots, pos, name, dtypes):
    n = len(gs)
    j = gs[0].shape[0]

    def body(pos_ref, *refs):
        g_refs, r_refs = refs[:n], refs[n:2 * n]
        o_refs, p_refs = refs[2 * n:3 * n], refs[3 * n:]
        vals = [(g_refs[a][0, 0] + r_refs[a][0, 0]).astype(dtypes[a]) for a in range(n)]
        for a in range(n):
            o_refs[a][0] = vals[a]
        if j == 1:
            for a in range(n):
                p_refs[a][0] = vals[a]
        else:
            @pl.when(pl.program_id(0) == pos_ref[0])
            def _():
                for a in range(n):
                    p_refs[a][0] = vals[a]

    blk = lambda g: (1,) + g.shape[2:]
    out = pl.pallas_call(
        body, name=name,
        grid_spec=pltpu.PrefetchScalarGridSpec(
            num_scalar_prefetch=1, grid=(j,),
            in_specs=[pl.BlockSpec((1,) + blk(g), lambda i, p: (i, p[1], 0, 0)) for g in gs]
            + [pl.BlockSpec((1,) + blk(g), lambda i, p: (i, 0, 0, 0)) for g in gs],
            out_specs=[pl.BlockSpec(blk(g), lambda i, p: (i, 0, 0)) for g in gs]
            + [pl.BlockSpec(blk(g), lambda i, p: (p[0], 0, 0)) for g in gs]),
        out_shape=[jax.ShapeDtypeStruct((j,) + g.shape[2:], dt) for g, dt in zip(gs, dtypes)]
        + [jax.ShapeDtypeStruct((N_CHIPS,) + g.shape[2:], dt) for g, dt in zip(gs, dtypes)],
        compiler_params=_params(("arbitrary",)),
    )(pos, *gs, *gots)
    return list(out[:n]), list(out[n:])


def _exchange_descriptors(sums, parts, send_of, recv_of):
    x, y, c = _mesh_pos()
    me = 2 * x + y
    chips = _other_chips(x, y)
    sends, arrivals = [], []
    for a in range(len(sums)):
        for k in range(3):
            ck = 2 * chips[k][0] + chips[k][1]
            mine = sums[a].at[ck] if sums[a].shape[0] == N_CHIPS else sums[a].at[0]

            def copy(dst_slot, a=a, k=k, mine=mine):
                return pltpu.make_async_remote_copy(
                    src_ref=mine, dst_ref=parts[a].at[dst_slot],
                    send_sem=send_of(a, k), recv_sem=recv_of(a, k),
                    device_id=(*chips[k], c), device_id_type=MESH)

            sends.append(functools.partial(copy, me))
            arrivals.append(functools.partial(copy, ck))
    return sends, arrivals


def _exchange_start(sums, parts, name, after=()):
    n = len(sums)
    ns = 3 * n

    def body(*refs):
        sems = refs[2 * n:2 * n + 2 * ns]
        sums_thru = refs[2 * n + 2 * ns:3 * n + 2 * ns]
        parts_thru = refs[3 * n + 2 * ns:4 * n + 2 * ns]
        token = refs[4 * n + 2 * ns]
        _chips_handshake()
        sends, _ = _exchange_descriptors(sums_thru, parts_thru, lambda a, k: sems[3 * a + k],
                                         lambda a, k: sems[ns + 3 * a + k])
        for cp in sends:
            cp().start()
        token[...] = jnp.zeros_like(token)

    hbm = lambda a: pltpu.HBM(a.shape, a.dtype)
    held = [pltpu.with_memory_space_constraint(a, pltpu.HBM) for a in (*sums, *parts)]
    out = _tied_call(
        body, after, name=name,
        out_shape=(*[pltpu.SemaphoreType.DMA(())] * (2 * ns), *[hbm(a) for a in held],
                   jax.ShapeDtypeStruct((8, LANES), F32)),
        in_specs=[HBM_SPEC] * (2 * n),
        out_specs=(*[SEM_SPEC] * (2 * ns), *[HBM_SPEC] * (2 * n), pl.BlockSpec(memory_space=pltpu.VMEM)),
        input_output_aliases={i: 2 * ns + i for i in range(2 * n)},
        compiler_params=pltpu.CompilerParams(has_side_effects=pltpu.SideEffectType.DATAFLOW_SIDE_EFFECTING,
                                             collective_id=CHIPS_COLLECTIVE_ID),
    )(*held)
    return (list(out[:2 * ns]), list(out[2 * ns:2 * ns + n]), list(out[2 * ns + n:2 * ns + 2 * n]),
            out[2 * ns + 2 * n])


def _exchange_wait(sems, sums, parts, after, name):
    n = len(sums)
    ns = 3 * n

    def body(*refs):
        sums_ref, parts_ref = refs[:n], refs[n:2 * n]
        sem_ref = refs[2 * n:2 * n + 2 * ns]
        sends, arrivals = _exchange_descriptors(sums_ref, parts_ref, lambda a, k: sem_ref[3 * a + k],
                                                lambda a, k: sem_ref[ns + 3 * a + k])
        for cp in sends:
            cp().wait_send()
        for cp in arrivals:
            cp().wait_recv()

    hbm = lambda a: pltpu.HBM(a.shape, a.dtype)
    out = pl.pallas_call(
        body, name=name,
        out_shape=tuple(hbm(a) for a in (*sums, *parts)),
        in_specs=[HBM_SPEC] * (2 * n) + [SEM_SPEC] * (2 * ns) + [ANY] * len(_as_tuple(after)),
        out_specs=tuple([HBM_SPEC] * (2 * n)),
        input_output_aliases={i: i for i in range(2 * n)},
        compiler_params=pltpu.CompilerParams(has_side_effects=pltpu.SideEffectType.DATAFLOW_SIDE_EFFECTING),
    )(*sums, *parts, *sems, *_as_tuple(after))
    return list(out[n:])


def _sum_chips(parts, pos, name, after=()):
    n = len(parts)
    after = _as_tuple(after)

    def body(pos_ref, *refs):
        outs = refs[n + len(after):]
        for a in range(n):
            p_ref = refs[a]
            outs[a][0] = (((p_ref[0].astype(F32) + p_ref[1].astype(F32)) + p_ref[2].astype(F32))
                          + p_ref[3].astype(F32))

    out = pl.pallas_call(
        body, name=name,
        grid_spec=pltpu.PrefetchScalarGridSpec(
            num_scalar_prefetch=1, grid=(1,),
            in_specs=[pl.BlockSpec(p.shape, lambda i, q: (0, 0, 0)) for p in parts] + [ANY] * len(after),
            out_specs=[pl.BlockSpec((1,) + p.shape[1:], lambda i, q: (q[1], 0, 0)) for p in parts]),
        out_shape=[jax.ShapeDtypeStruct((2,) + p.shape[1:], F32) for p in parts],
        compiler_params=_params(("arbitrary",)),
    )(pos, *parts, *after)
    return list(out)


def _join_descriptors(fulls, send_of, recv_of):
    x, y, c = _mesh_pos()

    def half(a, which):
        return functools.partial(
            pltpu.make_async_remote_copy,
            src_ref=fulls[a].at[which], dst_ref=fulls[a].at[which],
            send_sem=send_of(a), recv_sem=recv_of(a),
            device_id=(x, y, 1 - c), device_id_type=MESH)

    return [half(a, c) for a in range(len(fulls))], [half(a, 1 - c) for a in range(len(fulls))]


def _join_start(fulls, name, after=()):
    n = len(fulls)

    def body(*refs):
        sems, thru, token = refs[n:3 * n], refs[3 * n:4 * n], refs[4 * n]
        _sibling_handshake()
        sends, _ = _join_descriptors(thru, lambda a: sems[a], lambda a: sems[n + a])
        for cp in sends:
            cp().start()
        token[...] = jnp.zeros_like(token)

    held = [pltpu.with_memory_space_constraint(f, pltpu.HBM) for f in fulls]
    out = _tied_call(
        body, after, name=name,
        out_shape=(*[pltpu.SemaphoreType.DMA(())] * (2 * n), *[pltpu.HBM(f.shape, f.dtype) for f in held],
                   jax.ShapeDtypeStruct((8, LANES), F32)),
        in_specs=[HBM_SPEC] * n,
        out_specs=(*[SEM_SPEC] * (2 * n), *[HBM_SPEC] * n, pl.BlockSpec(memory_space=pltpu.VMEM)),
        input_output_aliases={i: 2 * n + i for i in range(n)},
        compiler_params=pltpu.CompilerParams(has_side_effects=pltpu.SideEffectType.DATAFLOW_SIDE_EFFECTING,
                                             collective_id=SIBLING_COLLECTIVE_ID),
    )(*held)
    return list(out[:2 * n]), list(out[2 * n:3 * n]), out[3 * n]


def _join_wait(sems, fulls, after, name):
    n = len(fulls)

    def body(*refs):
        sem_ref = refs[n:3 * n]
        sends, arrivals = _join_descriptors(refs[:n], lambda a: sem_ref[a], lambda a: sem_ref[n + a])
        for cp in sends:
            cp().wait_send()
        for cp in arrivals:
            cp().wait_recv()

    out = pl.pallas_call(
        body, name=name,
        out_shape=tuple(pltpu.HBM(f.shape, f.dtype) for f in fulls),
        in_specs=[HBM_SPEC] * n + [SEM_SPEC] * (2 * n) + [ANY] * len(_as_tuple(after)),
        out_specs=tuple([HBM_SPEC] * n),
        input_output_aliases={i: i for i in range(n)},
        compiler_params=pltpu.CompilerParams(has_side_effects=pltpu.SideEffectType.DATAFLOW_SIDE_EFFECTING),
    )(*fulls, *sems, *_as_tuple(after))
    return list(out)


def _join_halves(fulls, name, after=()):
    n = len(fulls)

    def body(*refs):
        send_sem, recv_sem = refs[2 * n:]
        _sibling_handshake()
        sends, arrivals = _join_descriptors(refs[n:2 * n], lambda a: send_sem.at[a], lambda a: recv_sem.at[a])
        sends = [cp() for cp in sends]
        for cp in sends:
            cp.start()
        for cp in arrivals:
            cp().wait_recv()
        for cp in sends:
            cp.wait_send()

    out_shape = [jax.ShapeDtypeStruct(f.shape, f.dtype) for f in fulls]
    return _tied_call(
        body, after, name=name,
        in_specs=[ANY] * n, out_specs=[ANY] * n, out_shape=out_shape,
        input_output_aliases={a: a for a in range(n)},
        scratch_shapes=[pltpu.SemaphoreType.DMA((n,))] * 2,
        compiler_params=pltpu.CompilerParams(collective_id=SIBLING_COLLECTIVE_ID),
    )(*fulls)


def _norm_in(x, g, ts, after=()):
    s = x.shape[0]

    def body(x_ref, g_ref, hn_ref):
        xv = x_ref[...]
        hn_ref[...] = (xv * _rms_stats(xv) * g_ref[...]).astype(BF16)

    row = pl.BlockSpec((ts, D_MODEL), lambda i: (i, 0))
    return _tied_call(
        body, after, name="norm_in", grid=(s // ts,),
        in_specs=[row, pl.BlockSpec((1, D_MODEL), lambda i: (0, 0))], out_specs=row,
        out_shape=jax.ShapeDtypeStruct((s, D_MODEL), BF16),
        compiler_params=_params(("parallel",)),
    )(x, g)


def _shift_rows(buf, shifted, t):
    rows = t + CONV_HALO - SUBLANES
    for r in range(1, SUBLANES):
        shifted[r - 1, 0:rows, :] = buf[pl.ds(r, rows), :]


def _window(buf, shifted, offset, t):
    r = offset % SUBLANES
    if r == 0:
        return buf[pl.ds(offset, t), :]
    return shifted[r - 1, pl.ds(offset - r, t), :]


def _lane_is_low_head():
    lane = lax.broadcasted_iota(jnp.int32, (1, GM_WIDTH), 1)
    return (lane & GM_HEAD_DIM) == 0


def _gm_mix(v_lo, v_hi, wpair_ref, bias_ref, mixed_ref, t):
    for n in range(t // CHUNK):
        rows = slice(n * CHUNK, (n + 1) * CHUNK)
        for j in range(GM_HEADS // 2):
            cols = slice(j * LANES, (j + 1) * LANES)
            rhs = jnp.concatenate([v_lo[rows, cols], v_hi[rows, cols]], axis=0)
            mixed_ref[rows, cols] = _dot(wpair_ref[j], rhs) + bias_ref[:, cols]


def _seqmix_fwd(hn, w_in, b_in, cw, cb, lng, lnb, gg, gb, wpair, bias, t, after=()):
    s = hn.shape[0]

    def body(hn_ref, w_ref, b_ref, cw_ref, cb_ref, lng_ref, lnb_ref, gg_ref, gb_ref, wpair_ref, bias_ref,
             z_ref, mix_ref, c1_ref, abuf, ash, mixed_ref):
        i = pl.program_id(0)

        @pl.when(i == 0)
        def _():
            abuf[0:CONV_HALO, :] = jnp.zeros((CONV_HALO, CONV_WIDTH), F32)

        @pl.when(i > 0)
        def _():
            abuf[0:CONV_HALO, :] = abuf[t:t + CONV_HALO, :]

        hv = hn_ref[...]
        for j in range(4):
            cols = slice(j * 512, (j + 1) * 512)
            z_ref[:, cols] = _dot(hv, w_ref[j]) + b_ref[:, cols]

        abuf[CONV_HALO:, :] = z_ref[:, 0:512] * _sigmoid(z_ref[:, 512:1024])
        _shift_rows(abuf, ash, t)
        acc = jnp.zeros((t, CONV_WIDTH), F32)
        for k in range(CONV_KERNEL):
            acc = acc + cw_ref[k:k + 1, :] * _window(abuf, ash, CONV_HALO - (CONV_KERNEL - 1) + k, t)
        c1 = acc + cb_ref[...]
        c1_ref[...] = c1
        xh, _ = _ln_stats(c1)
        ln = xh * lng_ref[...] + lnb_ref[...]
        mix_ref[:, 0:512] = (ln * _sigmoid(ln)).astype(BF16)

        u, _ = _gelu_parts(z_ref[:, 1024:1536])
        gv, _ = _gelu_parts(z_ref[:, 1536:2048])
        vxh, _ = _ln_stats(gv)
        v = vxh * gg_ref[...] + gb_ref[...]
        low = _lane_is_low_head()
        v_lo = jnp.where(low, v, 0.0).astype(BF16)
        v_hi = jnp.where(low, 0.0, v).astype(BF16)
        _gm_mix(v_lo, v_hi, wpair_ref, bias_ref, mixed_ref, t)
        mix_ref[:, 512:1024] = (u * mixed_ref[...]).astype(BF16)

    vec = lambda n: pl.BlockSpec((1, n), lambda i: (0, 0))
    return _tied_call(
        body, after, name="seqmix_fwd", grid=(s // t,),
        in_specs=[pl.BlockSpec((t, D_MODEL), lambda i: (i, 0)),
                  pl.BlockSpec((4, D_MODEL, 512), lambda i: (0, 0, 0)), vec(2048),
                  pl.BlockSpec((CONV_HALO, CONV_WIDTH), lambda i: (0, 0)),
                  vec(512), vec(512), vec(512), vec(512), vec(512),
                  pl.BlockSpec((4, CHUNK, 2 * CHUNK), lambda i: (0, 0, 0)),
                  pl.BlockSpec((CHUNK, GM_WIDTH), lambda i: (0, 0))],
        out_specs=[pl.BlockSpec((t, 2048), lambda i: (i, 0)),
                   pl.BlockSpec((t, D_MODEL), lambda i: (i, 0)),
                   pl.BlockSpec((t, CONV_WIDTH), lambda i: (i, 0))],
        out_shape=[jax.ShapeDtypeStruct((s, 2048), F32), jax.ShapeDtypeStruct((s, D_MODEL), BF16),
                   jax.ShapeDtypeStruct((s, CONV_WIDTH), F32)],
        scratch_shapes=[pltpu.VMEM((t + CONV_HALO, CONV_WIDTH), F32),
                        pltpu.VMEM((SUBLANES - 1, t + CONV_HALO - SUBLANES, CONV_WIDTH), F32),
                        pltpu.VMEM((t, GM_WIDTH), F32)],
        compiler_params=_params(("arbitrary",)),
    )(hn, w_in, b_in, cw, cb, lng, lnb, gg, gb, wpair, bias)


def _mem_kv(mem, g, wkv):
    m = mem.shape[0]

    def body(mem_ref, g_ref, w_ref, mn_ref, kv_ref):
        mv = mem_ref[...]
        mn = (mv * _rms_stats(mv) * g_ref[...]).astype(BF16)
        mn_ref[...] = mn
        for j in range(4):
            kv_ref[:, j * 512:(j + 1) * 512] = _dot(mn, w_ref[j]).astype(BF16)

    return pl.pallas_call(
        body, name="mem_kv",
        out_shape=[jax.ShapeDtypeStruct((m, D_MODEL), BF16), jax.ShapeDtypeStruct((m, 2 * D_MODEL), BF16)],
        compiler_params=pltpu.CompilerParams(vmem_limit_bytes=VMEM_LIMIT_BYTES),
    )(mem, g, wkv)


def _softmax_rows(sc):
    e = jnp.exp(sc - jnp.max(sc, axis=-1, keepdims=True))
    return e / jnp.sum(e, axis=-1, keepdims=True)


def _attn_block_fwd(x, mix, w_out, g_xa, wq, kv, wo, g_ffn, ts, after=()):
    s, m = x.shape[0], kv.shape[0]
    scale = XA_HEAD_DIM ** -0.5

    def body(x_ref, mix_ref, wout_ref, gxa_ref, wq_ref, kv_ref, wo_ref, gffn_ref,
             h1_ref, hn2_ref, q_ref, o_ref, h2_ref, hn3_ref):
        h1 = x_ref[...] + _dot(mix_ref[...], wout_ref[...])
        h1_ref[...] = h1
        hn2 = (h1 * _rms_stats(h1) * gxa_ref[...]).astype(BF16)
        hn2_ref[...] = hn2
        q_ref[...] = _dot(hn2, wq_ref[...]).astype(BF16)
        for h in range(XA_HEADS):
            cols = slice(h * XA_HEAD_DIM, (h + 1) * XA_HEAD_DIM)
            vcols = slice(D_MODEL + h * XA_HEAD_DIM, D_MODEL + (h + 1) * XA_HEAD_DIM)
            p = _softmax_rows(_dot_nt(q_ref[:, cols], kv_ref[:, cols]) * scale)
            o_ref[:, cols] = _dot(p.astype(BF16), kv_ref[:, vcols]).astype(BF16)
        h2 = h1 + _dot(o_ref[...], wo_ref[...])
        h2_ref[...] = h2
        hn3_ref[...] = (h2 * _rms_stats(h2) * gffn_ref[...]).astype(BF16)

    row = pl.BlockSpec((ts, D_MODEL), lambda i: (i, 0))
    full = pl.BlockSpec((D_MODEL, D_MODEL), lambda i: (0, 0))
    vec = pl.BlockSpec((1, D_MODEL), lambda i: (0, 0))
    f32 = jax.ShapeDtypeStruct((s, D_MODEL), F32)
    bf16 = jax.ShapeDtypeStruct((s, D_MODEL), BF16)
    return _tied_call(
        body, after, name="attn_block_fwd", grid=(s // ts,),
        in_specs=[row, row, full, vec, full, pl.BlockSpec((m, 2 * D_MODEL), lambda i: (0, 0)), full, vec],
        out_specs=[row] * 6,
        out_shape=[f32, bf16, bf16, bf16, f32, bf16],
        compiler_params=_params(("parallel",)),
    )(x, mix, w_out, g_xa, wq, kv, wo, g_ffn)


_FFN_CHUNKS = (slice(0, 8 * LANES), slice(8 * LANES, 16 * LANES), slice(16 * LANES, FFN_HIDDEN))


def _ffn_up(hn, wgu, ts, after=()):
    s = hn.shape[0]

    def body(hn_ref, w_ref, gu_ref, act_ref):
        hv = hn_ref[...]
        for cols in _FFN_CHUNKS:
            gate = _dot(hv, w_ref[0, :, cols])
            up = _dot(hv, w_ref[1, :, cols])
            gu_ref[0, :, cols] = gate.astype(BF16)
            gu_ref[1, :, cols] = up.astype(BF16)
            act_ref[:, cols] = (gate * _sigmoid(gate) * up).astype(BF16)

    return _tied_call(
        body, after, name="ffn_up", grid=(s // ts,),
        in_specs=[pl.BlockSpec((ts, D_MODEL), lambda i: (i, 0)),
                  pl.BlockSpec((2, D_MODEL, FFN_HIDDEN), lambda i: (0, 0, 0))],
        out_specs=[pl.BlockSpec((2, ts, FFN_HIDDEN), lambda i: (0, i, 0)),
                   pl.BlockSpec((ts, FFN_HIDDEN), lambda i: (i, 0))],
        out_shape=[jax.ShapeDtypeStruct((2, s, FFN_HIDDEN), BF16), jax.ShapeDtypeStruct((s, FFN_HIDDEN), BF16)],
        compiler_params=_params(("parallel",)),
    )(hn, wgu)


def _ffn_down_loss(act, wd, h2, g, target, ts):
    s = act.shape[0]

    def body(act_ref, wd_ref, h2_ref, g_ref, t_ref, dh_ref, dhb_ref, sq_ref, dg_ref):
        @pl.when(pl.program_id(0) == 0)
        def _():
            sq_ref[...] = jnp.zeros_like(sq_ref)
            dg_ref[...] = jnp.zeros_like(dg_ref)

        h3 = h2_ref[...] + _dot(act_ref[...], wd_ref[...])
        r = _rms_stats(h3)
        gv = g_ref[...]
        diff = h3 * r * gv - t_ref[...]
        sq_ref[...] += _rowsum(diff * diff)
        dh, dg = _rms_bwd(diff / D_MODEL, h3, r, gv)
        dh_ref[...] = dh
        dhb_ref[...] = dh.astype(BF16)
        dg_ref[...] += dg

    row = pl.BlockSpec((ts, D_MODEL), lambda i: (i, 0))
    vec = pl.BlockSpec((1, D_MODEL), lambda i: (0, 0))
    return pl.pallas_call(
        body, name="ffn_down_loss", grid=(s // ts,),
        in_specs=[pl.BlockSpec((ts, FFN_HIDDEN), lambda i: (i, 0)),
                  pl.BlockSpec((FFN_HIDDEN, D_MODEL), lambda i: (0, 0)), row, vec, row],
        out_specs=[row, row, vec, vec],
        out_shape=[jax.ShapeDtypeStruct((s, D_MODEL), F32), jax.ShapeDtypeStruct((s, D_MODEL), BF16),
                   jax.ShapeDtypeStruct((1, D_MODEL), F32), jax.ShapeDtypeStruct((1, D_MODEL), F32)],
        compiler_params=_params(("arbitrary",)),
    )(act, wd, h2, g, target)


def _grad_w(a, b, tk, tn, name, after=(), shards=1):
    s, k = a.shape
    gb, _, n = b.shape
    nblk = n // tn
    ws = tn // shards
    tsr = GRAD_ROWS if s % GRAD_ROWS == 0 else s

    def body(a_ref, b_ref, o_ref):
        part = _dot_tn(a_ref[...], b_ref[0])

        @pl.when(pl.program_id(2) == 0)
        def _():
            for j in range(shards):
                o_ref[j] = part[:, j * ws:(j + 1) * ws]

        @pl.when(pl.program_id(2) > 0)
        def _():
            for j in range(shards):
                o_ref[j] += part[:, j * ws:(j + 1) * ws]

    return _tied_call(
        body, after, name=name, grid=(gb * nblk, k // tk, s // tsr),
        in_specs=[pl.BlockSpec((tsr, tk), lambda ni, ki, si: (si, ki)),
                  pl.BlockSpec((1, tsr, tn), lambda ni, ki, si: (ni // nblk, si, ni % nblk))],
        out_specs=pl.BlockSpec((shards, tk, ws), lambda ni, ki, si: (ni, ki, 0)),
        out_shape=jax.ShapeDtypeStruct((gb * nblk * shards, k, ws), F32),
        compiler_params=_params(("parallel", "parallel", "arbitrary")),
    )(a, b)


def _grad_w_square(pairs, name, after=()):
    n = len(pairs)
    s = pairs[0][0].shape[0]
    tsr = GRAD_ROWS // 2 if s % (GRAD_ROWS // 2) == 0 else s

    def body(*refs):
        ins, outs = refs[:2 * n], refs[2 * n:]
        parts = [_dot_tn(ins[2 * a][...], ins[2 * a + 1][...]) for a in range(n)]

        @pl.when(pl.program_id(0) == 0)
        def _():
            for a in range(n):
                outs[a][...] = parts[a]

        @pl.when(pl.program_id(0) > 0)
        def _():
            for a in range(n):
                outs[a][...] += parts[a]

    row = pl.BlockSpec((tsr, D_MODEL), lambda i: (i, 0))
    return _tied_call(
        body, after, name=name, grid=(s // tsr,),
        in_specs=[row] * (2 * n), out_specs=[pl.BlockSpec((D_MODEL, D_MODEL), lambda i: (0, 0))] * n,
        out_shape=[jax.ShapeDtypeStruct((D_MODEL, D_MODEL), F32)] * n,
        compiler_params=_params(("arbitrary",)),
    )(*[x for p in pairs for x in p])


def _ffn_bwd(dh3, wd, gu, wgu, h2, g, t, after=()):
    s = dh3.shape[0]

    def body(dh3_ref, wd_ref, gu_ref, w_ref, h2_ref, g_ref, dgu_ref, dh2_ref, dh2b_ref, dg_ref):
        @pl.when(pl.program_id(0) == 0)
        def _():
            dg_ref[...] = jnp.zeros_like(dg_ref)

        dh3v = dh3_ref[...]
        dhb = dh3v.astype(BF16)
        for cols in _FFN_CHUNKS:
            dact = _dot_nt(dhb, wd_ref[cols, :])
            gate, up = gu_ref[0, :, cols].astype(F32), gu_ref[1, :, cols].astype(F32)
            sg = _sigmoid(gate)
            dgu_ref[0, :, cols] = (dact * up * (sg * (1.0 + gate * (1.0 - sg)))).astype(BF16)
            dgu_ref[1, :, cols] = (dact * (gate * sg)).astype(BF16)
        dhn = _dot_nt(dgu_ref[0], w_ref[0]) + _dot_nt(dgu_ref[1], w_ref[1])
        h2 = h2_ref[...]
        dv, dg = _rms_bwd(dhn, h2, _rms_stats(h2), g_ref[...])
        dh2 = dh3v + dv
        dh2_ref[...] = dh2
        dh2b_ref[...] = dh2.astype(BF16)
        dg_ref[...] += dg

    row = pl.BlockSpec((t, D_MODEL), lambda i: (i, 0))
    wide = pl.BlockSpec((2, t, FFN_HIDDEN), lambda i: (0, i, 0))
    vec = pl.BlockSpec((1, D_MODEL), lambda i: (0, 0))
    return _tied_call(
        body, after, name="ffn_bwd", grid=(s // t,),
        in_specs=[row, pl.BlockSpec((FFN_HIDDEN, D_MODEL), lambda i: (0, 0)), wide,
                  pl.BlockSpec((2, D_MODEL, FFN_HIDDEN), lambda i: (0, 0, 0)), row, vec],
        out_specs=[wide, row, row, vec],
        out_shape=[jax.ShapeDtypeStruct((2, s, FFN_HIDDEN), BF16), jax.ShapeDtypeStruct((s, D_MODEL), F32),
                   jax.ShapeDtypeStruct((s, D_MODEL), BF16), jax.ShapeDtypeStruct((1, D_MODEL), F32)],
        compiler_params=_params(("arbitrary",)),
    )(dh3, wd, gu, wgu, h2, g)


def _attn_bwd(dh2, wo, q, kv, wq, h1, g, ts, after=()):
    s, m = q.shape[0], kv.shape[0]
    scale = XA_HEAD_DIM ** -0.5

    def body(dh2_ref, wo_ref, q_ref, kv_ref, wq_ref, h1_ref, g_ref, dh1_ref, dh1b_ref, dq_ref, dkv_ref, dg_ref):
        @pl.when(pl.program_id(0) == 0)
        def _():
            dkv_ref[...] = jnp.zeros_like(dkv_ref)
            dg_ref[...] = jnp.zeros_like(dg_ref)

        do = _dot_nt(dh2_ref[...].astype(BF16), wo_ref[...]).astype(BF16)
        for h in range(XA_HEADS):
            cols = slice(h * XA_HEAD_DIM, (h + 1) * XA_HEAD_DIM)
            vcols = slice(D_MODEL + h * XA_HEAD_DIM, D_MODEL + (h + 1) * XA_HEAD_DIM)
            qh, kh, vh, doh = q_ref[:, cols], kv_ref[:, cols], kv_ref[:, vcols], do[:, cols]
            p = _softmax_rows(_dot_nt(qh, kh) * scale)
            dp = _dot_nt(doh, vh)
            ds = (p * (dp - jnp.sum(dp * p, axis=-1, keepdims=True)) * scale).astype(BF16)
            dq_ref[:, cols] = _dot(ds, kh).astype(BF16)
            dkv_ref[:, cols] += _dot_tn(ds, qh)
            dkv_ref[:, vcols] += _dot_tn(p.astype(BF16), doh)
        dhn = _dot_nt(dq_ref[...], wq_ref[...])
        h1 = h1_ref[...]
        dv, dg = _rms_bwd(dhn, h1, _rms_stats(h1), g_ref[...])
        dh1 = dh2_ref[...] + dv
        dh1_ref[...] = dh1
        dh1b_ref[...] = dh1.astype(BF16)
        dg_ref[...] += dg

    row = pl.BlockSpec((ts, D_MODEL), lambda i: (i, 0))
    full = pl.BlockSpec((D_MODEL, D_MODEL), lambda i: (0, 0))
    kvs = pl.BlockSpec((m, 2 * D_MODEL), lambda i: (0, 0))
    vec = pl.BlockSpec((1, D_MODEL), lambda i: (0, 0))
    return _tied_call(
        body, after, name="attn_bwd", grid=(s // ts,),
        in_specs=[row, full, row, kvs, full, row, vec],
        out_specs=[row, row, row, kvs, vec],
        out_shape=[jax.ShapeDtypeStruct((s, D_MODEL), F32), jax.ShapeDtypeStruct((s, D_MODEL), BF16),
                   jax.ShapeDtypeStruct((s, D_MODEL), BF16),
                   jax.ShapeDtypeStruct((m, 2 * D_MODEL), F32), jax.ShapeDtypeStruct((1, D_MODEL), F32)],
        compiler_params=_params(("arbitrary",)),
    )(dh2, wo, q, kv, wq, h1, g)


def _mem_kv_bwd(dkv, mn, wkv, mem, g, after=()):
    m = mem.shape[0]

    def body(dkv_ref, mn_ref, w_ref, mem_ref, g_ref, dw_ref, dg_ref):
        dmn = jnp.zeros((m, D_MODEL), F32)
        mn = mn_ref[...]
        for j in range(4):
            dj = dkv_ref[:, j * 512:(j + 1) * 512].astype(BF16)
            dw_ref[j] = _dot_tn(mn, dj)
            dmn = dmn + _dot_nt(dj, w_ref[j])
        mv = mem_ref[...]
        dg_ref[...] = _rowsum(dmn * (mv * _rms_stats(mv)))

    return _tied_call(
        body, after, name="mem_kv_bwd", in_specs=[pl.BlockSpec(memory_space=pltpu.VMEM)] * 5,
        out_shape=[jax.ShapeDtypeStruct((4, D_MODEL, 512), F32), jax.ShapeDtypeStruct((1, D_MODEL), F32)],
        compiler_params=pltpu.CompilerParams(vmem_limit_bytes=VMEM_LIMIT_BYTES),
    )(dkv, mn, wkv, mem, g)


def _seqmix_bwd(dh1, x, z, c1, w_out, w_in, g_mix, cw, lng, lnb, gg, gb, wpair, wpair_t, bias, t, after=()):
    s = x.shape[0]
    nt = s // t

    def body(dh1_ref, x_ref, z_ref, c1_ref, wo_ref, wi_ref, gm_ref, cw_ref, lng_ref, lnb_ref,
             gg_ref, gb_ref, wpair_ref, wpt_ref, bias_ref,
             gx_ref, dz_ref, dcw_ref, dcb_ref, dlng_ref, dlnb_ref, dgg_ref, dgb_ref, dws_ref, dbs_ref,
             dbin_ref, dgm_ref, dbuf, dsh, mixed_ref, dv_ref):
        i = pl.program_id(0)
        accs = (dcw_ref, dcb_ref, dlng_ref, dlnb_ref, dgg_ref, dgb_ref, dws_ref, dbs_ref, dbin_ref, dgm_ref)

        @pl.when(i == 0)
        def _():
            for r in accs:
                r[...] = jnp.zeros_like(r)
            dbuf[t:t + CONV_HALO, :] = jnp.zeros((CONV_HALO, CONV_WIDTH), F32)

        @pl.when(i > 0)
        def _():
            dbuf[t:t + CONV_HALO, :] = dbuf[0:CONV_HALO, :]

        dmix = _dot_nt(dh1_ref[...].astype(BF16), wo_ref[...])

        xh, rs = _ln_stats(c1_ref[...])
        lng = lng_ref[...]
        ln = xh * lng + lnb_ref[...]
        sl = _sigmoid(ln)
        dln = dmix[:, 0:512] * (sl * (1.0 + ln * (1.0 - sl)))
        dc1, dg_ln, db_ln = _ln_bwd(dln, xh, rs, lng)
        dlng_ref[...] += dg_ln
        dlnb_ref[...] += db_ln
        dcb_ref[...] += _rowsum(dc1)
        dbuf[0:t, :] = dc1

        za = z_ref[:, 0:512]
        sg = _sigmoid(z_ref[:, 512:1024])
        a = za * sg
        _shift_rows(dbuf, dsh, t)

        da = jnp.zeros((t, CONV_WIDTH), F32)
        for k in range(CONV_KERNEL):
            later = _window(dbuf, dsh, CONV_KERNEL - 1 - k, t)
            da = da + cw_ref[k:k + 1, :] * later
            dcw_ref[k:k + 1, :] += _rowsum(a * later)
        dza = da * sg
        dzg = da * za * (sg * (1.0 - sg))
        dz_ref[:, 0:512] = dza.astype(BF16)
        dz_ref[:, 512:1024] = dzg.astype(BF16)
        dbin_ref[:, 0:512] += _rowsum(dza)
        dbin_ref[:, 512:1024] += _rowsum(dzg)

        dgm = dmix[:, 512:1024]
        u, du_dz = _gelu_parts(z_ref[:, 1024:1536])
        gv, dgv_dz = _gelu_parts(z_ref[:, 1536:2048])
        vxh, vrs = _ln_stats(gv)
        ggv = gg_ref[...]
        v = vxh * ggv + gb_ref[...]
        low = _lane_is_low_head()
        v_lo = jnp.where(low, v, 0.0).astype(BF16)
        v_hi = jnp.where(low, 0.0, v).astype(BF16)
        _gm_mix(v_lo, v_hi, wpair_ref, bias_ref, mixed_ref, t)
        dzu = dgm * mixed_ref[...] * du_dz
        dm = dgm * u
        dm_lo = jnp.where(low, dm, 0.0).astype(BF16)
        dm_hi = jnp.where(low, 0.0, dm).astype(BF16)
        vb = v.astype(BF16)
        tril = (lax.broadcasted_iota(jnp.int32, (CHUNK, CHUNK), 1)
                <= lax.broadcasted_iota(jnp.int32, (CHUNK, CHUNK), 0))
        for n in range(t // CHUNK):
            rows = slice(n * CHUNK, (n + 1) * CHUNK)
            dbs_ref[...] += dm[rows, :]
            for j in range(GM_HEADS // 2):
                cols = slice(j * LANES, (j + 1) * LANES)
                stack = jnp.concatenate([dm_lo[rows, cols], dm_hi[rows, cols]], axis=0)
                dws = _dot_nt(stack, vb[rows, cols])
                dws_ref[2 * j] += jnp.where(tril, dws[0:CHUNK], 0.0)
                dws_ref[2 * j + 1] += jnp.where(tril, dws[CHUNK:2 * CHUNK], 0.0)
                dv_ref[rows, cols] = _dot(wpt_ref[j], stack)
        dgv, dg_gm, db_gm = _ln_bwd(dv_ref[...], vxh, vrs, ggv)
        dgg_ref[...] += dg_gm
        dgb_ref[...] += db_gm
        dzv = dgv * dgv_dz
        dz_ref[:, 1024:1536] = dzu.astype(BF16)
        dz_ref[:, 1536:2048] = dzv.astype(BF16)
        dbin_ref[:, 1024:1536] += _rowsum(dzu)
        dbin_ref[:, 1536:2048] += _rowsum(dzv)

        dhn = jnp.zeros((t, D_MODEL), F32)
        for j in range(4):
            dhn = dhn + _dot_nt(dz_ref[:, j * 512:(j + 1) * 512], wi_ref[j])
        xv = x_ref[...]
        dv, dg = _rms_bwd(dhn, xv, _rms_stats(xv), gm_ref[...])
        gx_ref[...] = dh1_ref[...] + dv
        dgm_ref[...] += dg

    rev = lambda w: pl.BlockSpec((t, w), lambda i: (nt - 1 - i, 0))
    const = lambda *shape: pl.BlockSpec(shape, lambda i: (0,) * len(shape))
    f32 = lambda *shape: jax.ShapeDtypeStruct(shape, F32)
    return _tied_call(
        body, after, name="seqmix_bwd", grid=(nt,),
        in_specs=[rev(D_MODEL), rev(D_MODEL), rev(2048), rev(CONV_WIDTH),
                  const(D_MODEL, D_MODEL), const(4, D_MODEL, 512), const(1, D_MODEL),
                  const(CONV_HALO, CONV_WIDTH), const(1, 512), const(1, 512), const(1, 512), const(1, 512),
                  const(4, CHUNK, 2 * CHUNK), const(4, CHUNK, 2 * CHUNK), const(CHUNK, GM_WIDTH)],
        out_specs=[rev(D_MODEL), rev(2048),
                   const(CONV_HALO, CONV_WIDTH), const(1, 512), const(1, 512), const(1, 512), const(1, 512),
                   const(1, 512), const(GM_HEADS, CHUNK, CHUNK), const(CHUNK, GM_WIDTH), const(1, 2048),
                   const(1, D_MODEL)],
        out_shape=[f32(s, D_MODEL), jax.ShapeDtypeStruct((s, 2048), BF16),
                   f32(CONV_HALO, CONV_WIDTH), f32(1, 512), f32(1, 512), f32(1, 512), f32(1, 512),
                   f32(1, 512), f32(GM_HEADS, CHUNK, CHUNK), f32(CHUNK, GM_WIDTH), f32(1, 2048),
                   f32(1, D_MODEL)],
        scratch_shapes=[pltpu.VMEM((t + CONV_HALO, CONV_WIDTH), F32),
                        pltpu.VMEM((SUBLANES - 1, t + CONV_HALO - SUBLANES, CONV_WIDTH), F32),
                        pltpu.VMEM((t, GM_WIDTH), F32), pltpu.VMEM((t, GM_WIDTH), F32)],
        compiler_params=_params(("arbitrary",)),
    )(dh1, x, z, c1, w_out, w_in, g_mix, cw, lng, lnb, gg, gb, wpair, wpair_t, bias)


def _head_bias_grad(dbs):
    def body(d_ref, o_ref):
        dv = d_ref[...]
        lane = lax.broadcasted_iota(jnp.int32, (CHUNK, LANES), 1)
        acc = jnp.zeros((CHUNK, LANES), F32)
        for h in range(GM_HEADS):
            sh = jnp.sum(dv[:, h * GM_HEAD_DIM:(h + 1) * GM_HEAD_DIM], axis=-1, keepdims=True)
            acc = acc + jnp.where(lane == h, sh, 0.0)
        o_ref[...] = acc

    return pl.pallas_call(body, name="head_bias_grad",
                          out_shape=jax.ShapeDtypeStruct((CHUNK, LANES), F32))(dbs)


def kernel(x, mem, norm_mix_g, w_in, b_in, conv_w, conv_b, conv_ln_g, conv_ln_b, gm_ln_g, gm_ln_b, gm_w_s, gm_b_s, w_out, norm_xa_g, mem_norm_g, xa_wq, xa_wkv, xa_wo, norm_ffn_g, ffn_w_gate_up, ffn_w_down, final_norm_g, loss_target, m_norm_mix_g, m_w_in, m_b_in, m_conv_w, m_conv_b, m_conv_ln_g, m_conv_ln_b, m_gm_ln_g, m_gm_ln_b, m_gm_w_s, m_gm_b_s, m_w_out, m_norm_xa_g, m_mem_norm_g, m_xa_wq, m_xa_wkv, m_xa_wo, m_norm_ffn_g, m_ffn_w_gate_up, m_ffn_w_down, m_final_norm_g, v_norm_mix_g, v_w_in, v_b_in, v_conv_w, v_conv_b, v_conv_ln_g, v_conv_ln_b, v_gm_ln_g, v_gm_ln_b, v_gm_w_s, v_gm_b_s, v_w_out, v_norm_xa_g, v_mem_norm_g, v_xa_wq, v_xa_wkv, v_xa_wo, v_norm_ffn_g, v_ffn_w_gate_up, v_ffn_w_down, v_final_norm_g):
    weights = dict(norm_mix_g=norm_mix_g, w_in=w_in, b_in=b_in, conv_w=conv_w, conv_b=conv_b, conv_ln_g=conv_ln_g,
                   conv_ln_b=conv_ln_b, gm_ln_g=gm_ln_g, gm_ln_b=gm_ln_b, gm_w_s=gm_w_s, gm_b_s=gm_b_s, w_out=w_out,
                   norm_xa_g=norm_xa_g, mem_norm_g=mem_norm_g, xa_wq=xa_wq, xa_wkv=xa_wkv, xa_wo=xa_wo,
                   norm_ffn_g=norm_ffn_g, ffn_w_gate_up=ffn_w_gate_up, ffn_w_down=ffn_w_down,
                   final_norm_g=final_norm_g)
    m_in = dict(norm_mix_g=m_norm_mix_g, w_in=m_w_in, b_in=m_b_in, conv_w=m_conv_w, conv_b=m_conv_b,
                conv_ln_g=m_conv_ln_g, conv_ln_b=m_conv_ln_b, gm_ln_g=m_gm_ln_g, gm_ln_b=m_gm_ln_b, gm_w_s=m_gm_w_s,
                gm_b_s=m_gm_b_s, w_out=m_w_out, norm_xa_g=m_norm_xa_g, mem_norm_g=m_mem_norm_g, xa_wq=m_xa_wq,
                xa_wkv=m_xa_wkv, xa_wo=m_xa_wo, norm_ffn_g=m_norm_ffn_g, ffn_w_gate_up=m_ffn_w_gate_up,
                ffn_w_down=m_ffn_w_down, final_norm_g=m_final_norm_g)
    v_in = dict(norm_mix_g=v_norm_mix_g, w_in=v_w_in, b_in=v_b_in, conv_w=v_conv_w, conv_b=v_conv_b,
                conv_ln_g=v_conv_ln_g, conv_ln_b=v_conv_ln_b, gm_ln_g=v_gm_ln_g, gm_ln_b=v_gm_ln_b, gm_w_s=v_gm_w_s,
                gm_b_s=v_gm_b_s, w_out=v_w_out, norm_xa_g=v_norm_xa_g, mem_norm_g=v_mem_norm_g, xa_wq=v_xa_wq,
                xa_wkv=v_xa_wkv, xa_wo=v_xa_wo, norm_ffn_g=v_norm_ffn_g, ffn_w_gate_up=v_ffn_w_gate_up,
                ffn_w_down=v_ffn_w_down, final_norm_g=v_final_norm_g)
    grads, delta, new_m, new_v = {}, {}, {}, {}

    s = x.shape[1]
    ts = _row_tile(s)
    tb = max(CHUNK, ts // 2)
    tw = 2 * ts if s % (2 * ts) == 0 and ts >= 512 else ts
    cx, cy, cc = _mesh_pos()
    chip = 2 * cx + cy
    pos = jnp.stack([chip, cc]).astype(jnp.int32)
    row = lambda a: a.reshape(1, -1)
    x2, mem2, tgt2 = x[0], mem[0], loss_target[0]

    big = dict(w_in=w_in, xa_wkv=xa_wkv, w_out=w_out, xa_wq=xa_wq, xa_wo=xa_wo,
               ffn_w_gate_up=ffn_w_gate_up, ffn_w_down=ffn_w_down)
    big_names = list(big)
    halves = lambda a: a.reshape(2, a.shape[0] // 2, a.shape[1])
    conv_w_pad = jnp.pad(conv_w, ((0, CONV_HALO - CONV_KERNEL), (0, 0)))
    first_names = ["w_in", "conv_w"]
    later_names = [nm for nm in big_names if nm != "w_in"]
    cast = dict(zip(first_names, _cast_into_slots([halves(w_in), halves(conv_w_pad)], pos, [BF16, F32], "cast_w_in")))
    cast.update(zip(later_names, _cast_into_slots([halves(big[nm]) for nm in later_names], pos,
                                                  [BF16] * len(later_names), "cast_" + later_names[0],
                                                  side_by_side=(later_names.index("ffn_w_gate_up"),))))

    def start_gather(names, after):
        return _gather_start([cast[nm] for nm in names], "gather_start_" + names[0], after)

    def land_gather(names, started, after):
        send_sems, recv_sems, bufs, _ = started
        return _gather_wait(send_sems, recv_sems, bufs, after, "gather_wait_" + names[0])

    as_weights = lambda names, bufs: dict(zip(names, (b.reshape(b.shape[0], -1, b.shape[-1]) for b in bufs)))

    def start_share(names, landed):
        return _pass_start(landed, "pass_start_" + names[0])

    def share_gather(names, passing, which, after):
        send_sems, recv_sems, bufs, _ = passing
        sems = lambda s: [s[3 * a + k] for a in which for k in range(3)]
        picked = [names[a] for a in which]
        return as_weights(picked, _gather_wait(sems(send_sems), sems(recv_sems), [bufs[a] for a in which], after,
                                               "pass_wait_" + picked[0], _pass_descriptors))

    tril = jnp.tril(jnp.ones((CHUNK, CHUNK), dtype=bool))
    ws = jnp.where(tril[None], gm_w_s, 0.0)
    wpair = jnp.concatenate([ws[0::2], ws[1::2]], axis=2).astype(BF16)
    ws_t = jnp.swapaxes(ws, 1, 2)
    wpair_t = jnp.concatenate([ws_t[0::2], ws_t[1::2]], axis=2).astype(BF16)
    bias = jnp.repeat(gm_b_s.T, GM_HEAD_DIM, axis=1)

    attn_names = ["xa_wkv", "w_out", "xa_wq", "xa_wo"]
    gather_first = start_gather(first_names, ())
    hn1 = _norm_in(x2, row(norm_mix_g), tw, after=(gather_first[3], wpair, wpair_t, bias))
    landed = land_gather(first_names, gather_first, [cast[nm] for nm in later_names] + [hn1])
    passing = start_share(first_names, landed)
    gather_attn = start_gather(attn_names, passing[3])
    gw = share_gather(first_names, passing, (0, 1), gather_attn[3])
    w_in_g = gw["w_in"]
    cw_g = jnp.concatenate([gw["conv_w"][k] for k in range(N_CHIPS)], axis=1)

    z, mix, c1 = _seqmix_fwd(hn1, w_in_g, row(b_in), cw_g, row(conv_b), row(conv_ln_g), row(conv_ln_b),
                             row(gm_ln_g), row(gm_ln_b), wpair, bias, ts)
    landed = land_gather(attn_names, gather_attn, mix)
    passing = start_share(attn_names, landed)
    gather_gu = start_gather(["ffn_w_gate_up"], passing[3])
    wkv_g = share_gather(attn_names, passing, (0,), gather_gu[3])["xa_wkv"]
    mn, kv = _mem_kv(mem2, row(mem_norm_g), wkv_g)
    gw = share_gather(attn_names, passing, (1, 2, 3), kv)
    w_out_g = gw["w_out"].reshape(D_MODEL, D_MODEL)
    wq_g = gw["xa_wq"].reshape(D_MODEL, D_MODEL)
    wo_g = gw["xa_wo"].reshape(D_MODEL, D_MODEL)
    h1, hn2, q, o, h2, hn3 = _attn_block_fwd(x2, mix, w_out_g, row(norm_xa_g), wq_g, kv, wo_g, row(norm_ffn_g), ts)
    landed = land_gather(["ffn_w_gate_up"], gather_gu, hn3)
    passing = start_share(["ffn_w_gate_up"], landed)
    gather_down = start_gather(["ffn_w_down"], passing[3])
    wgu_g = share_gather(["ffn_w_gate_up"], passing, (0,), gather_down[3])["ffn_w_gate_up"]
    gu, act = _ffn_up(hn3, wgu_g, ts)
    landed = land_gather(["ffn_w_down"], gather_down, act)
    wd_g = as_weights(["ffn_w_down"], _pass_to_sibling(landed, "pass_ffn_w_down"))["ffn_w_down"].reshape(
        FFN_HIDDEN, D_MODEL)
    dh3, dh3_b, sq, d_final_g = _ffn_down_loss(act, wd_g, h2, row(final_norm_g), tgt2, ts)
    loss_here = jnp.broadcast_to(0.5 * jnp.sum(sq) / D_MODEL, (1, 2, SUBLANES, LANES))

    def split(g, nm):
        r, c = big[nm].shape
        return g.reshape(N_CHIPS, 2, r // 2, c)

    def chip_sums(group, arrays, got):
        sums, parts = [None] * len(group), [None] * len(group)
        for blocks in (N_CHIPS, 1):
            idx = [i for i, a in enumerate(arrays) if a.shape[0] == blocks]
            if idx:
                out = _add_halves([arrays[i] for i in idx], [got[i] for i in idx], pos, "chip_sum_" + group[idx[0]],
                                  [F32 if group[i] == "loss" else BF16 for i in idx])
                for k, i in enumerate(idx):
                    sums[i], parts[i] = out[0][k], out[1][k]
        return sums, parts

    def start_swap(group, grads, after=()):
        return _swap_start([split(g, nm) for g, nm in zip(grads, group)], "swap_start_" + group[0], after)

    def start_exchange(group, swapping, after, landed):
        arrays, got = _swap_wait(*swapping[:3], after, "swap_wait_" + group[0])
        sums, parts = chip_sums(group, arrays, got)
        return _exchange_start(sums, parts, "exchange_start_" + group[0], landed)

    def wait_exchange(group, started, after):
        sems, sums, parts, _ = started
        return _exchange_wait(sems, sums, parts, after, "exchange_wait_" + group[0])

    def finish_exchange(group, started, after):
        return _sum_chips(wait_exchange(group, started, after), pos, "total_" + group[0])

    def join(group, after):
        return _join_halves([halves_of[nm] for nm in group], "join_halves_" + group[0], after)

    def start_join(group, after):
        return _join_start([halves_of[nm] for nm in group], "join_start_" + group[0], after)

    def end_join(group, joining, after):
        return _join_wait(*joining[:2], after, "join_wait_" + group[0])

    def update(group, joined, after=()):
        outs = _adamw([(weights[nm], j.reshape(big[nm].shape), m_in[nm], v_in[nm]) for nm, j in zip(group, joined)],
                      "adamw_" + group[0], after)
        for nm, out in zip(group, outs):
            grads[nm], delta[nm], new_m[nm], new_v[nm] = out
        return [new_v[nm] for nm in group]

    def join_and_update(group, after):
        return update(group, join(group, after))

    as3 = lambda a: a.reshape((1,) + a.shape)
    halves_of = {}

    g_down = _grad_w(act, as3(dh3_b), FFN_HALF, D_MODEL, "grad_ffn_w_down")
    group_a = ["ffn_w_down"]
    swap_a = start_swap(group_a, [g_down])
    dgu, dh2, dh2_b, d_ffn_g = _ffn_bwd(dh3, wd_g, gu, wgu_g, h2, row(norm_ffn_g), tb,
                                        after=swap_a[3])
    exch_a = start_exchange(group_a, swap_a, dh2, wd_g)
    g_gu = _grad_w(hn3, dgu, D_MODEL, FFN_HALF, "grad_ffn_w_gate_up", after=exch_a[3])
    halves_of.update(zip(group_a, finish_exchange(group_a, exch_a, g_gu)))

    group_b = ["ffn_w_gate_up"]
    swap_b = start_swap(group_b, [g_gu])
    dh1, dh1_b, dq, dkv, d_xa_g = _attn_bwd(dh2, wo_g, q, kv, wq_g, h1, row(norm_xa_g), ts, after=swap_b[3])
    exch_b = start_exchange(group_b, swap_b, dh1, [halves_of[nm] for nm in group_a])
    joining_a = start_join(group_a, exch_b[3])
    g_wkv, d_mem_g = _mem_kv_bwd(dkv, mn, wkv_g, mem2, row(mem_norm_g), after=joining_a[2])
    g_wo, g_wq, g_wout = _grad_w_square([(o, dh2_b), (hn2, dq), (mix, dh1_b)], "grad_xa_wo", after=joining_a[2])
    done_a = update(group_a, end_join(group_a, joining_a, (g_wkv, g_wo, g_wq, g_wout)))
    halves_of.update(zip(group_b, finish_exchange(group_b, exch_b, done_a)))
    joining_b = start_join(group_b, done_a)
    group_c = ["xa_wo", "xa_wq", "xa_wkv", "w_out"]
    swap_c = start_swap(group_c, [g_wo, g_wq, g_wkv, g_wout], joining_b[2])
    (gx, dz, d_cw, d_cb, d_lng, d_lnb, d_gg, d_gb, d_ws, d_bs_sum, d_bin, d_mix_g) = _seqmix_bwd(
        dh1, x2, z, c1, w_out_g, w_in_g, row(norm_mix_g), cw_g, row(conv_ln_g), row(conv_ln_b),
        row(gm_ln_g), row(gm_ln_b), wpair, wpair_t, bias, tb, after=swap_c[3])
    d_bs = _head_bias_grad(d_bs_sum)[:, :GM_HEADS].T
    exch_c = start_exchange(group_c, swap_c, dz, joining_b[2])
    g_win = _grad_w(hn1, as3(dz), D_MODEL, 1024, "grad_w_in", after=exch_c[3], shards=2)

    small_names = ["norm_mix_g", "b_in", "conv_w", "conv_b", "conv_ln_g", "conv_ln_b", "gm_ln_g", "gm_ln_b",
                   "gm_w_s", "gm_b_s", "norm_xa_g", "mem_norm_g", "norm_ffn_g", "final_norm_g"]
    d_cw_by_chip = jnp.swapaxes(d_cw.reshape(CONV_HALO, N_CHIPS, LANES), 0, 1).reshape(-1, LANES)
    small_grads = dict(norm_mix_g=d_mix_g, b_in=d_bin, conv_w=d_cw_by_chip, conv_b=d_cb, conv_ln_g=d_lng,
                       conv_ln_b=d_lnb, gm_ln_g=d_gg, gm_ln_b=d_gb, gm_w_s=d_ws, gm_b_s=d_bs, norm_xa_g=d_xa_g,
                       mem_norm_g=d_mem_g, norm_ffn_g=d_ffn_g, final_norm_g=d_final_g)

    def rows_form(a):
        a = a.reshape(-1, LANES)
        return jnp.pad(a, ((0, -a.shape[0] % SUBLANES), (0, 0)))

    pieces = [rows_form(small_grads[nm]) for nm in small_names]
    offsets, total = [], 0
    for p in pieces:
        offsets.append(total)
        total += p.shape[0]
    pack_rows = -(-total // 32) * 32
    small_pack = jnp.pad(jnp.concatenate(pieces, axis=0), ((0, pack_rows - total), (0, 0)))

    group_d = ["w_in", "small", "loss"]
    joined_b = end_join(group_b, joining_b, g_win)
    swap_d = _swap_start([split(g_win, "w_in")], "swap_start_w_in", joined_b)
    done_b = update(group_b, joined_b, swap_d[3])
    small_d = [small_pack.reshape(1, 2, pack_rows // 2, LANES), loss_here]
    got_small = _swap_halves(small_d, "swap_halves_small", done_b)
    arrays_d, got_d = _swap_wait(*swap_d[:3], got_small, "swap_wait_w_in")
    sums_d, parts_d = chip_sums(group_d, arrays_d + small_d, got_d + list(got_small))
    parts_c = wait_exchange(group_c, exch_c, sums_d)
    exch_d = _exchange_start(sums_d, parts_d, "exchange_start_w_in", parts_c)
    halves_of.update(zip(group_c, _sum_chips(parts_c, pos, "total_xa_wo", exch_d[3])))
    done_c = join_and_update(group_c, exch_d[3])
    halves_of.update(zip(group_d, finish_exchange(group_d, exch_d, done_c)))
    joined_d = _join_halves([halves_of[nm] for nm in group_d], "join_halves_w_in")
    loss = joined_d[2][0, 0, 0]
    grads["w_in"], delta["w_in"], new_m["w_in"], new_v["w_in"] = _adamw(
        [(w_in, joined_d[0].reshape(w_in.shape), m_w_in, v_w_in)], "adamw_w_in")[0]

    local_rows = lambda a, nm: a if nm == "conv_w" else a.reshape(-1, LANES)
    params = [tuple(local_rows(src[nm], nm) for src in (weights, m_in, v_in)) for nm in small_names]
    outs = _adamw_small(joined_d[1].reshape(pack_rows, LANES), pos, params, offsets, small_names.index("conv_w"))
    for k, nm in enumerate(small_names):
        for dst, a in zip((grads, delta, new_m, new_v), outs[4 * k:4 * k + 4]):
            dst[nm] = a

    order = ["norm_mix_g", "w_in", "b_in", "conv_w", "conv_b", "conv_ln_g", "conv_ln_b", "gm_ln_g", "gm_ln_b",
             "gm_w_s", "gm_b_s", "w_out", "norm_xa_g", "mem_norm_g", "xa_wq", "xa_wkv", "xa_wo", "norm_ffn_g",
             "ffn_w_gate_up", "ffn_w_down", "final_norm_g"]
    fit = lambda a, nm: a.reshape(weights[nm].shape)
    return (loss, gx.reshape(x.shape),
            *[fit(grads[nm], nm) for nm in order], *[fit(delta[nm], nm) for nm in order],
            *[fit(new_m[nm], nm) for nm in order], *[fit(new_v[nm], nm) for nm in order])
```

```python
import functools

import jax
import jax.numpy as jnp
from jax import lax
from jax.experimental import pallas as pl
from jax.experimental.pallas import tpu as pltpu

F32 = jnp.float32
BF16 = jnp.bfloat16

D_MODEL = 1024
CONV_WIDTH = 512
GM_WIDTH = 512
CONV_KERNEL = 31
CONV_HALO = 32
GRAD_ROWS = 2048
CHUNK = 128
GM_HEADS = 8
GM_HEAD_DIM = 64
XA_HEADS = 4
XA_HEAD_DIM = 256
FFN_HIDDEN = 2816
FFN_HALF = FFN_HIDDEN // 2
RMS_EPS = 1e-6
LN_EPS = 1e-5
N_CHIPS = 4
LANES = 128
SUBLANES = 8

ADAM_LR = 0.001
ADAM_B1 = 0.9
ADAM_B2 = 0.999
ADAM_EPS = 1e-08
ADAM_WD = 0.01
ADAM_STEP = 10

VMEM_LIMIT_BYTES = 56 * 1024 * 1024
MESH = pl.DeviceIdType.MESH
ANY = pl.BlockSpec(memory_space=pl.ANY)
HBM_SPEC = pl.BlockSpec(memory_space=pltpu.HBM)
SEM_SPEC = pl.BlockSpec(memory_space=pltpu.SEMAPHORE)

_NT = (((1,), (1,)), ((), ()))
_TN = (((0,), (0,)), ((), ()))
_GELU_C = 0.7978845608028654
_GELU_A = 0.044715


def _dot(a, b):
    return jnp.dot(a, b, preferred_element_type=F32)


def _dot_nt(a, b):
    return lax.dot_general(a, b, _NT, preferred_element_type=F32)


def _dot_tn(a, b):
    return lax.dot_general(a, b, _TN, preferred_element_type=F32)


def _mean(v):
    return jnp.mean(v, axis=-1, keepdims=True)


def _rowsum(v):
    return jnp.sum(v, axis=0, keepdims=True)


def _sigmoid(v):
    return 1.0 / (1.0 + jnp.exp(-v))


def _gelu_parts(v):
    v2 = v * v
    t = jnp.tanh(_GELU_C * (v + _GELU_A * v * v2))
    g = 0.5 * v * (1.0 + t)
    dg = 0.5 * (1.0 + t) + 0.5 * v * (1.0 - t * t) * (_GELU_C * (1.0 + 3.0 * _GELU_A * v2))
    return g, dg


def _rms_stats(v):
    return lax.rsqrt(_mean(v * v) + RMS_EPS)


def _rms_bwd(dy, v, r, g):
    n = v * r
    dn = dy * g
    dv = r * (dn - n * _mean(dn * n))
    return dv, _rowsum(dy * n)


def _ln_stats(v):
    mu = _mean(v)
    xc = v - mu
    rs = lax.rsqrt(_mean(xc * xc) + LN_EPS)
    return xc * rs, rs


def _ln_bwd(dy, xh, rs, g):
    dxh = dy * g
    dv = rs * (dxh - _mean(dxh) - xh * _mean(dxh * xh))
    return dv, _rowsum(dy * xh), _rowsum(dy)


def _params(sem):
    return pltpu.CompilerParams(dimension_semantics=sem, vmem_limit_bytes=VMEM_LIMIT_BYTES)


def _row_tile(s):
    return 512 if s % 512 == 0 and s >= 2048 else 128


def _mesh_pos():
    return lax.axis_index("x"), lax.axis_index("y"), lax.axis_index("c")


def _slot(buf, chip_idx, half):
    if buf.shape[0] == N_CHIPS:
        return buf.at[chip_idx, half]
    width = buf.shape[-1] // 2
    return buf.at[chip_idx // 2, half, :, pl.ds(pl.multiple_of((chip_idx % 2) * width, LANES), width)]


def _cast_into_slots(ws, pos, dtypes, name, side_by_side=()):
    n = len(ws)

    def body(pos_ref, *refs):
        for a in range(n):
            refs[n + a][0] = refs[a][...].astype(dtypes[a])

    def out_spec(a, w):
        if a in side_by_side:
            return pl.BlockSpec((1, 1) + w.shape[1:], lambda i, p: (p[0] // 2, i, 0, p[0] % 2))
        return pl.BlockSpec((1, 1) + w.shape[1:], lambda i, p: (p[0], i, 0, 0))

    def out_shape(a, w):
        if a in side_by_side:
            return (2, 2, w.shape[1], 2 * w.shape[2])
        return (N_CHIPS,) + w.shape

    return pl.pallas_call(
        body, name=name,
        grid_spec=pltpu.PrefetchScalarGridSpec(
            num_scalar_prefetch=1, grid=(2,),
            in_specs=[pl.BlockSpec((1,) + w.shape[1:], lambda i, p: (i, 0, 0)) for w in ws],
            out_specs=[out_spec(a, w) for a, w in enumerate(ws)]),
        out_shape=[jax.ShapeDtypeStruct(out_shape(a, w), dt) for a, (w, dt) in enumerate(zip(ws, dtypes))],
        compiler_params=_params(("parallel",)),
    )(pos, *ws)


def _adam_update(w, g, m, v):
    nm = ADAM_B1 * m + (1.0 - ADAM_B1) * g
    nv = ADAM_B2 * v + (1.0 - ADAM_B2) * (g * g)
    m_hat = nm / (1.0 - ADAM_B1 ** ADAM_STEP)
    v_hat = nv / (1.0 - ADAM_B2 ** ADAM_STEP)
    return -ADAM_LR * (m_hat / (jnp.sqrt(v_hat) + ADAM_EPS) + ADAM_WD * w), nm, nv


ADAM_STEPS = 4


def _adamw(quads, name, after=()):
    n = len(quads)

    def body(*refs):
        ins, outs = refs[:4 * n], refs[4 * n:]
        for a in range(n):
            w, g, m, v = (r[...] for r in ins[4 * a:4 * a + 4])
            outs[4 * a][...] = g
            outs[4 * a + 1][...], outs[4 * a + 2][...], outs[4 * a + 3][...] = _adam_update(w, g, m, v)

    specs = [pl.BlockSpec((q[0].shape[0] // ADAM_STEPS, q[0].shape[1]), lambda i: (i, 0)) for q in quads]
    out = _tied_call(
        body, after, name=name, grid=(ADAM_STEPS,),
        in_specs=[sp for sp in specs for _ in range(4)], out_specs=[sp for sp in specs for _ in range(4)],
        out_shape=[jax.ShapeDtypeStruct(q[0].shape, F32) for q in quads for _ in range(4)],
        compiler_params=_params(("parallel",)),
    )(*[a for q in quads for a in q])
    return [tuple(out[4 * a:4 * a + 4]) for a in range(n)]


def _adamw_small(gpack, pos, params, offsets, conv_at):
    n = len(params)

    def body(pos_ref, g_ref, *refs):
        ins, outs = refs[:3 * n], refs[3 * n:]
        for k in range(n):
            rows = params[k][0].shape[0]
            start = offsets[k]
            if k == conv_at:
                start = pl.multiple_of(start + pos_ref[0] * CONV_HALO, SUBLANES)
            g = g_ref[pl.ds(start, rows), :]
            outs[4 * k][...] = g
            outs[4 * k + 1][...], outs[4 * k + 2][...], outs[4 * k + 3][...] = _adam_update(
                ins[3 * k][...], g, ins[3 * k + 1][...], ins[3 * k + 2][...])

    flat = [a for p in params for a in p]
    vmem = pl.BlockSpec(memory_space=pltpu.VMEM)
    return pl.pallas_call(
        body, name="adamw_small",
        in_specs=[pl.BlockSpec(memory_space=pltpu.SMEM), vmem] + [vmem] * len(flat),
        out_specs=[vmem] * (4 * n),
        out_shape=[jax.ShapeDtypeStruct(p[0].shape, F32) for p in params for _ in range(4)],
    )(pos, gpack, *flat)


def _as_tuple(after):
    return tuple(after) if isinstance(after, (tuple, list)) else (after,)


def _tied_call(body, after, *, in_specs, **kwargs):
    after = _as_tuple(after)
    n_in, n_after = len(in_specs), len(after)

    def tied(*refs):
        body(*refs[:n_in], *refs[n_in + n_after:])

    call = pl.pallas_call(tied, in_specs=list(in_specs) + [ANY] * n_after, **kwargs)
    return lambda *operands: call(*operands, *after)


def _other_chips(x, y):
    return [(1 - x, y), (x, 1 - y), (1 - x, 1 - y)]


def _gather_descriptors(bufs, send_of, recv_of):
    x, y, c = _mesh_pos()
    me = 2 * x + y
    chips = _other_chips(x, y)
    sends, arrivals = [], []
    for a in range(len(bufs)):
        for k in range(3):
            ck = 2 * chips[k][0] + chips[k][1]

            def copy(slot, a=a, k=k):
                return pltpu.make_async_remote_copy(
                    src_ref=_slot(bufs[a], slot, c), dst_ref=_slot(bufs[a], slot, c),
                    send_sem=send_of(a, k), recv_sem=recv_of(a, k),
                    device_id=(*chips[k], c), device_id_type=MESH)

            sends.append(functools.partial(copy, me))
            arrivals.append(functools.partial(copy, ck))
    return sends, arrivals


def _gather_start(bufs, name, after=()):
    n = len(bufs)
    ns = 3 * n

    def body(*refs):
        sems = refs[n:n + 2 * ns]
        thru = refs[n + 2 * ns:2 * n + 2 * ns]
        token = refs[2 * n + 2 * ns]
        _chips_handshake()
        sends, _ = _gather_descriptors(thru, lambda a, k: sems[3 * a + k], lambda a, k: sems[ns + 3 * a + k])
        for cp in sends:
            cp().start()
        token[...] = jnp.zeros_like(token)

    held = [pltpu.with_memory_space_constraint(b, pltpu.HBM) for b in bufs]
    out = _tied_call(
        body, after, name=name,
        out_shape=(*[pltpu.SemaphoreType.DMA(())] * (2 * ns), *[pltpu.HBM(b.shape, b.dtype) for b in held],
                   jax.ShapeDtypeStruct((8, LANES), F32)),
        in_specs=[HBM_SPEC] * n,
        out_specs=(*[SEM_SPEC] * (2 * ns), *[HBM_SPEC] * n, pl.BlockSpec(memory_space=pltpu.VMEM)),
        input_output_aliases={i: 2 * ns + i for i in range(n)},
        compiler_params=pltpu.CompilerParams(has_side_effects=pltpu.SideEffectType.DATAFLOW_SIDE_EFFECTING,
                                             collective_id=CHIPS_COLLECTIVE_ID),
    )(*held)
    return list(out[:ns]), list(out[ns:2 * ns]), list(out[2 * ns:2 * ns + n]), out[2 * ns + n]


def _gather_wait(send_sems, recv_sems, bufs, after, name, descriptors=_gather_descriptors):
    n = len(bufs)
    ns = 3 * n

    def body(*refs):
        buf_ref = refs[:n]
        sem_ref = refs[n:n + 2 * ns]
        sends, arrivals = descriptors(buf_ref, lambda a, k: sem_ref[3 * a + k], lambda a, k: sem_ref[ns + 3 * a + k])
        for cp in sends:
            cp().wait_send()
        for cp in arrivals:
            cp().wait_recv()

    out = pl.pallas_call(
        body, name=name,
        out_shape=tuple(pltpu.HBM(b.shape, b.dtype) for b in bufs),
        in_specs=[HBM_SPEC] * n + [SEM_SPEC] * (2 * ns) + [ANY] * len(_as_tuple(after)),
        out_specs=tuple([HBM_SPEC] * n),
        input_output_aliases={i: i for i in range(n)},
        compiler_params=pltpu.CompilerParams(has_side_effects=pltpu.SideEffectType.DATAFLOW_SIDE_EFFECTING),
    )(*bufs, *send_sems, *recv_sems, *_as_tuple(after))
    return list(out)


SIBLING_COLLECTIVE_ID = 0


def _sibling_handshake():
    x, y, c = _mesh_pos()
    barrier = pltpu.get_barrier_semaphore()
    pl.semaphore_signal(barrier, inc=1, device_id=(x, y, 1 - c), device_id_type=MESH)
    pl.semaphore_wait(barrier, 1)


CHIPS_COLLECTIVE_ID = 1


def _chips_handshake():
    x, y, c = _mesh_pos()
    barrier = pltpu.get_barrier_semaphore()
    for chip in _other_chips(x, y):
        pl.semaphore_signal(barrier, inc=1, device_id=(*chip, c), device_id_type=MESH)
    pl.semaphore_wait(barrier, 3)


def _pass_descriptors(bufs, send_of, recv_of):
    x, y, c = _mesh_pos()
    chips = _other_chips(x, y)

    def half(a, k, which):
        ck = 2 * chips[k][0] + chips[k][1]
        return functools.partial(
            pltpu.make_async_remote_copy,
            src_ref=_slot(bufs[a], ck, which), dst_ref=_slot(bufs[a], ck, which),
            send_sem=send_of(a, k), recv_sem=recv_of(a, k),
            device_id=(x, y, 1 - c), device_id_type=MESH)

    pairs = [(a, k) for a in range(len(bufs)) for k in range(3)]
    return [half(a, k, c) for a, k in pairs], [half(a, k, 1 - c) for a, k in pairs]


def _pass_start(bufs, name, after=()):
    n = len(bufs)
    ns = 3 * n

    def body(*refs):
        sems = refs[n:n + 2 * ns]
        thru = refs[n + 2 * ns:2 * n + 2 * ns]
        token = refs[2 * n + 2 * ns]
        _sibling_handshake()
        sends, _ = _pass_descriptors(thru, lambda a, k: sems[3 * a + k], lambda a, k: sems[ns + 3 * a + k])
        for cp in sends:
            cp().start()
        token[...] = jnp.zeros_like(token)

    held = [pltpu.with_memory_space_constraint(b, pltpu.HBM) for b in bufs]
    out = _tied_call(
        body, after, name=name,
        out_shape=(*[pltpu.SemaphoreType.DMA(())] * (2 * ns), *[pltpu.HBM(b.shape, b.dtype) for b in held],
                   jax.ShapeDtypeStruct((8, LANES), F32)),
        in_specs=[HBM_SPEC] * n,
        out_specs=(*[SEM_SPEC] * (2 * ns), *[HBM_SPEC] * n, pl.BlockSpec(memory_space=pltpu.VMEM)),
        input_output_aliases={i: 2 * ns + i for i in range(n)},
        compiler_params=pltpu.CompilerParams(has_side_effects=pltpu.SideEffectType.DATAFLOW_SIDE_EFFECTING,
                                             collective_id=SIBLING_COLLECTIVE_ID),
    )(*held)
    return list(out[:ns]), list(out[ns:2 * ns]), list(out[2 * ns:2 * ns + n]), out[2 * ns + n]


def _pass_to_sibling(bufs, name, after=()):
    n = len(bufs)

    def body(*refs):
        outs = refs[n:2 * n]
        send_sem, recv_sem = refs[2 * n:]
        _sibling_handshake()
        sends, arrivals = _pass_descriptors(outs, lambda a, k: send_sem.at[a, k], lambda a, k: recv_sem.at[a, k])
        sends = [cp() for cp in sends]
        for cp in sends:
            cp.start()
        for cp in arrivals:
            cp().wait_recv()
        for cp in sends:
            cp.wait_send()

    return _tied_call(
        body, after, name=name,
        in_specs=[ANY] * n, out_specs=[ANY] * n,
        out_shape=[jax.ShapeDtypeStruct(b.shape, b.dtype) for b in bufs],
        input_output_aliases={a: a for a in range(n)},
        scratch_shapes=[pltpu.SemaphoreType.DMA((n, 3))] * 2,
        compiler_params=pltpu.CompilerParams(collective_id=SIBLING_COLLECTIVE_ID),
    )(*bufs)


def _swap_descriptors(grads, lands, send_of, recv_of):
    x, y, c = _mesh_pos()
    return [functools.partial(
        pltpu.make_async_remote_copy,
        src_ref=grads[a].at[:, pl.ds(1 - c, 1)], dst_ref=lands[a],
        send_sem=send_of(a), recv_sem=recv_of(a),
        device_id=(x, y, 1 - c), device_id_type=MESH) for a in range(len(grads))]


def _swap_halves(grads, name, after=()):
    n = len(grads)

    def body(*refs):
        ins, outs = refs[:n], refs[n:2 * n]
        send_sem, recv_sem = refs[2 * n:]
        _sibling_handshake()
        cps = [cp() for cp in _swap_descriptors(ins, outs, lambda a: send_sem.at[a], lambda a: recv_sem.at[a])]
        for cp in cps:
            cp.start()
        for cp in cps:
            cp.wait()

    out_shape = [jax.ShapeDtypeStruct((g.shape[0], 1) + g.shape[2:], g.dtype) for g in grads]
    return _tied_call(
        body, after, name=name,
        in_specs=[ANY] * n, out_specs=[ANY] * n, out_shape=out_shape,
        scratch_shapes=[pltpu.SemaphoreType.DMA((n,))] * 2,
        compiler_params=pltpu.CompilerParams(collective_id=SIBLING_COLLECTIVE_ID),
    )(*grads)


def _swap_start(grads, name, after=()):
    n, after = len(grads), _as_tuple(after)

    def body(*refs):
        outs = refs[2 * n + len(after):]
        sems, g_thru, l_thru, token = outs[:2 * n], outs[2 * n:3 * n], outs[3 * n:4 * n], outs[4 * n]
        _sibling_handshake()
        for cp in _swap_descriptors(g_thru, l_thru, lambda a: sems[a], lambda a: sems[n + a]):
            cp().start()
        token[...] = jnp.zeros_like(token)

    lands = [lax.empty((g.shape[0], 1) + g.shape[2:], g.dtype) for g in grads]
    held = [pltpu.with_memory_space_constraint(a, pltpu.HBM) for a in (*grads, *lands)]
    out = pl.pallas_call(
        body, name=name,
        out_shape=(*[pltpu.SemaphoreType.DMA(())] * (2 * n), *[pltpu.HBM(a.shape, a.dtype) for a in held],
                   jax.ShapeDtypeStruct((8, LANES), F32)),
        in_specs=[HBM_SPEC] * (2 * n) + [ANY] * len(after),
        out_specs=(*[SEM_SPEC] * (2 * n), *[HBM_SPEC] * (2 * n), pl.BlockSpec(memory_space=pltpu.VMEM)),
        input_output_aliases={i: 2 * n + i for i in range(2 * n)},
        compiler_params=pltpu.CompilerParams(has_side_effects=pltpu.SideEffectType.DATAFLOW_SIDE_EFFECTING,
                                             collective_id=SIBLING_COLLECTIVE_ID),
    )(*held, *after)
    return list(out[:2 * n]), list(out[2 * n:3 * n]), list(out[3 * n:4 * n]), out[4 * n]


def _swap_wait(sems, grads, lands, after, name):
    n = len(grads)

    def body(*refs):
        g_ref, l_ref = refs[:n], refs[n:2 * n]
        sem_ref = refs[2 * n:4 * n]
        for cp in _swap_descriptors(g_ref, l_ref, lambda a: sem_ref[a], lambda a: sem_ref[n + a]):
            cp().wait()

    out = pl.pallas_call(
        body, name=name,
        out_shape=tuple(pltpu.HBM(a.shape, a.dtype) for a in (*grads, *lands)),
        in_specs=[HBM_SPEC] * (2 * n) + [SEM_SPEC] * (2 * n) + [ANY] * len(_as_tuple(after)),
        out_specs=tuple([HBM_SPEC] * (2 * n)),
        input_output_aliases={i: i for i in range(2 * n)},
        compiler_params=pltpu.CompilerParams(has_side_effects=pltpu.SideEffectType.DATAFLOW_SIDE_EFFECTING),
    )(*grads, *lands, *sems, *_as_tuple(after))
    return list(out[:n]), list(out[n:])


def _add_halves(gs, gots, pos, name, dtypes):
    n = len(gs)
    j = gs[0].shape[0]

    def body(pos_ref, *refs):
        g_refs, r_refs = refs[:n], refs[n:2 * n]
        o_refs, p_refs = refs[2 * n:3 * n], refs[3 * n:]
        vals = [(g_refs[a][0, 0] + r_refs[a][0, 0]).astype(dtypes[a]) for a in range(n)]
        for a in range(n):
            o_refs[a][0] = vals[a]
        if j == 1:
            for a in range(n):
                p_refs[a][0] = vals[a]
        else:
            @pl.when(pl.program_id(0) == pos_ref[0])
            def _():
                for a in range(n):
                    p_refs[a][0] = vals[a]

    blk = lambda g: (1,) + g.shape[2:]
    out = pl.pallas_call(
        body, name=name,
        grid_spec=pltpu.PrefetchScalarGridSpec(
            num_scalar_prefetch=1, grid=(j,),
            in_specs=[pl.BlockSpec((1,) + blk(g), lambda i, p: (i, p[1], 0, 0)) for g in gs]
            + [pl.BlockSpec((1,) + blk(g), lambda i, p: (i, 0, 0, 0)) for g in gs],
            out_specs=[pl.BlockSpec(blk(g), lambda i, p: (i, 0, 0)) for g in gs]
            + [pl.BlockSpec(blk(g), lambda i, p: (p[0], 0, 0)) for g in gs]),
        out_shape=[jax.ShapeDtypeStruct((j,) + g.shape[2:], dt) for g, dt in zip(gs, dtypes)]
        + [jax.ShapeDtypeStruct((N_CHIPS,) + g.shape[2:], dt) for g, dt in zip(gs, dtypes)],
        compiler_params=_params(("arbitrary",)),
    )(pos, *gs, *gots)
    return list(out[:n]), list(out[n:])


def _exchange_descriptors(sums, parts, send_of, recv_of):
    x, y, c = _mesh_pos()
    me = 2 * x + y
    chips = _other_chips(x, y)
    sends, arrivals = [], []
    for a in range(len(sums)):
        for k in range(3):
            ck = 2 * chips[k][0] + chips[k][1]
            mine = sums[a].at[ck] if sums[a].shape[0] == N_CHIPS else sums[a].at[0]

            def copy(dst_slot, a=a, k=k, mine=mine):
                return pltpu.make_async_remote_copy(
                    src_ref=mine, dst_ref=parts[a].at[dst_slot],
                    send_sem=send_of(a, k), recv_sem=recv_of(a, k),
                    device_id=(*chips[k], c), device_id_type=MESH)

            sends.append(functools.partial(copy, me))
            arrivals.append(functools.partial(copy, ck))
    return sends, arrivals


def _exchange_start(sums, parts, name, after=()):
    n = len(sums)
    ns = 3 * n

    def body(*refs):
        sems = refs[2 * n:2 * n + 2 * ns]
        sums_thru = refs[2 * n + 2 * ns:3 * n + 2 * ns]
        parts_thru = refs[3 * n + 2 * ns:4 * n + 2 * ns]
        token = refs[4 * n + 2 * ns]
        _chips_handshake()
        sends, _ = _exchange_descriptors(sums_thru, parts_thru, lambda a, k: sems[3 * a + k],
                                         lambda a, k: sems[ns + 3 * a + k])
        for cp in sends:
            cp().start()
        token[...] = jnp.zeros_like(token)

    hbm = lambda a: pltpu.HBM(a.shape, a.dtype)
    held = [pltpu.with_memory_space_constraint(a, pltpu.HBM) for a in (*sums, *parts)]
    out = _tied_call(
        body, after, name=name,
        out_shape=(*[pltpu.SemaphoreType.DMA(())] * (2 * ns), *[hbm(a) for a in held],
                   jax.ShapeDtypeStruct((8, LANES), F32)),
        in_specs=[HBM_SPEC] * (2 * n),
        out_specs=(*[SEM_SPEC] * (2 * ns), *[HBM_SPEC] * (2 * n), pl.BlockSpec(memory_space=pltpu.VMEM)),
        input_output_aliases={i: 2 * ns + i for i in range(2 * n)},
        compiler_params=pltpu.CompilerParams(has_side_effects=pltpu.SideEffectType.DATAFLOW_SIDE_EFFECTING,
                                             collective_id=CHIPS_COLLECTIVE_ID),
    )(*held)
    return (list(out[:2 * ns]), list(out[2 * ns:2 * ns + n]), list(out[2 * ns + n:2 * ns + 2 * n]),
            out[2 * ns + 2 * n])


def _exchange_wait(sems, sums, parts, after, name):
    n = len(sums)
    ns = 3 * n

    def body(*refs):
        sums_ref, parts_ref = refs[:n], refs[n:2 * n]
        sem_ref = refs[2 * n:2 * n + 2 * ns]
        sends, arrivals = _exchange_descriptors(sums_ref, parts_ref, lambda a, k: sem_ref[3 * a + k],
                                                lambda a, k: sem_ref[ns + 3 * a + k])
        for cp in sends:
            cp().wait_send()
        for cp in arrivals:
            cp().wait_recv()

    hbm = lambda a: pltpu.HBM(a.shape, a.dtype)
    out = pl.pallas_call(
        body, name=name,
        out_shape=tuple(hbm(a) for a in (*sums, *parts)),
        in_specs=[HBM_SPEC] * (2 * n) + [SEM_SPEC] * (2 * ns) + [ANY] * len(_as_tuple(after)),
        out_specs=tuple([HBM_SPEC] * (2 * n)),
        input_output_aliases={i: i for i in range(2 * n)},
        compiler_params=pltpu.CompilerParams(has_side_effects=pltpu.SideEffectType.DATAFLOW_SIDE_EFFECTING),
    )(*sums, *parts, *sems, *_as_tuple(after))
    return list(out[n:])


def _sum_chips(parts, pos, name, after=()):
    n = len(parts)
    after = _as_tuple(after)

    def body(pos_ref, *refs):
        outs = refs[n + len(after):]
        for a in range(n):
            p_ref = refs[a]
            outs[a][0] = (((p_ref[0].astype(F32) + p_ref[1].astype(F32)) + p_ref[2].astype(F32))
                          + p_ref[3].astype(F32))

    out = pl.pallas_call(
        body, name=name,
        grid_spec=pltpu.PrefetchScalarGridSpec(
            num_scalar_prefetch=1, grid=(1,),
            in_specs=[pl.BlockSpec(p.shape, lambda i, q: (0, 0, 0)) for p in parts] + [ANY] * len(after),
            out_specs=[pl.BlockSpec((1,) + p.shape[1:], lambda i, q: (q[1], 0, 0)) for p in parts]),
        out_shape=[jax.ShapeDtypeStruct((2,) + p.shape[1:], F32) for p in parts],
        compiler_params=_params(("arbitrary",)),
    )(pos, *parts, *after)
    return list(out)


def _join_descriptors(fulls, send_of, recv_of):
    x, y, c = _mesh_pos()

    def half(a, which):
        return functools.partial(
            pltpu.make_async_remote_copy,
            src_ref=fulls[a].at[which], dst_ref=fulls[a].at[which],
            send_sem=send_of(a), recv_sem=recv_of(a),
            device_id=(x, y, 1 - c), device_id_type=MESH)

    return [half(a, c) for a in range(len(fulls))], [half(a, 1 - c) for a in range(len(fulls))]


def _join_start(fulls, name, after=()):
    n = len(fulls)

    def body(*refs):
        sems, thru, token = refs[n:3 * n], refs[3 * n:4 * n], refs[4 * n]
        _sibling_handshake()
        sends, _ = _join_descriptors(thru, lambda a: sems[a], lambda a: sems[n + a])
        for cp in sends:
            cp().start()
        token[...] = jnp.zeros_like(token)

    held = [pltpu.with_memory_space_constraint(f, pltpu.HBM) for f in fulls]
    out = _tied_call(
        body, after, name=name,
        out_shape=(*[pltpu.SemaphoreType.DMA(())] * (2 * n), *[pltpu.HBM(f.shape, f.dtype) for f in held],
                   jax.ShapeDtypeStruct((8, LANES), F32)),
        in_specs=[HBM_SPEC] * n,
        out_specs=(*[SEM_SPEC] * (2 * n), *[HBM_SPEC] * n, pl.BlockSpec(memory_space=pltpu.VMEM)),
        input_output_aliases={i: 2 * n + i for i in range(n)},
        compiler_params=pltpu.CompilerParams(has_side_effects=pltpu.SideEffectType.DATAFLOW_SIDE_EFFECTING,
                                             collective_id=SIBLING_COLLECTIVE_ID),
    )(*held)
    return list(out[:2 * n]), list(out[2 * n:3 * n]), out[3 * n]


def _join_wait(sems, fulls, after, name):
    n = len(fulls)

    def body(*refs):
        sem_ref = refs[n:3 * n]
        sends, arrivals = _join_descriptors(refs[:n], lambda a: sem_ref[a], lambda a: sem_ref[n + a])
        for cp in sends:
            cp().wait_send()
        for cp in arrivals:
            cp().wait_recv()

    out = pl.pallas_call(
        body, name=name,
        out_shape=tuple(pltpu.HBM(f.shape, f.dtype) for f in fulls),
        in_specs=[HBM_SPEC] * n + [SEM_SPEC] * (2 * n) + [ANY] * len(_as_tuple(after)),
        out_specs=tuple([HBM_SPEC] * n),
        input_output_aliases={i: i for i in range(n)},
        compiler_params=pltpu.CompilerParams(has_side_effects=pltpu.SideEffectType.DATAFLOW_SIDE_EFFECTING),
    )(*fulls, *sems, *_as_tuple(after))
    return list(out)


def _join_halves(fulls, name, after=()):
    n = len(fulls)

    def body(*refs):
        send_sem, recv_sem = refs[2 * n:]
        _sibling_handshake()
        sends, arrivals = _join_descriptors(refs[n:2 * n], lambda a: send_sem.at[a], lambda a: recv_sem.at[a])
        sends = [cp() for cp in sends]
        for cp in sends:
            cp.start()
        for cp in arrivals:
            cp().wait_recv()
        for cp in sends:
            cp.wait_send()

    out_shape = [jax.ShapeDtypeStruct(f.shape, f.dtype) for f in fulls]
    return _tied_call(
        body, after, name=name,
        in_specs=[ANY] * n, out_specs=[ANY] * n, out_shape=out_shape,
        input_output_aliases={a: a for a in range(n)},
        scratch_shapes=[pltpu.SemaphoreType.DMA((n,))] * 2,
        compiler_params=pltpu.CompilerParams(collective_id=SIBLING_COLLECTIVE_ID),
    )(*fulls)


def _norm_in(x, g, ts, after=()):
    s = x.shape[0]

    def body(x_ref, g_ref, hn_ref):
        xv = x_ref[...]
        hn_ref[...] = (xv * _rms_stats(xv) * g_ref[...]).astype(BF16)

    row = pl.BlockSpec((ts, D_MODEL), lambda i: (i, 0))
    return _tied_call(
        body, after, name="norm_in", grid=(s // ts,),
        in_specs=[row, pl.BlockSpec((1, D_MODEL), lambda i: (0, 0))], out_specs=row,
        out_shape=jax.ShapeDtypeStruct((s, D_MODEL), BF16),
        compiler_params=_params(("parallel",)),
    )(x, g)


def _shift_rows(buf, shifted, t):
    rows = t + CONV_HALO - SUBLANES
    for r in range(1, SUBLANES):
        shifted[r - 1, 0:rows, :] = buf[pl.ds(r, rows), :]


def _window(buf, shifted, offset, t):
    r = offset % SUBLANES
    if r == 0:
        return buf[pl.ds(offset, t), :]
    return shifted[r - 1, pl.ds(offset - r, t), :]


def _lane_is_low_head():
    lane = lax.broadcasted_iota(jnp.int32, (1, GM_WIDTH), 1)
    return (lane & GM_HEAD_DIM) == 0


def _gm_mix(v_lo, v_hi, wpair_ref, bias_ref, mixed_ref, t):
    for n in range(t // CHUNK):
        rows = slice(n * CHUNK, (n + 1) * CHUNK)
        for j in range(GM_HEADS // 2):
            cols = slice(j * LANES, (j + 1) * LANES)
            rhs = jnp.concatenate([v_lo[rows, cols], v_hi[rows, cols]], axis=0)
            mixed_ref[rows, cols] = _dot(wpair_ref[j], rhs) + bias_ref[:, cols]


def _seqmix_fwd(hn, w_in, b_in, cw, cb, lng, lnb, gg, gb, wpair, bias, t, tiles, carry=None, after=()):
    s = hn.shape[0]
    first, count = tiles[0], tiles[1] - tiles[0]

    def body(hn_ref, w_ref, b_ref, cw_ref, cb_ref, lng_ref, lnb_ref, gg_ref, gb_ref, wpair_ref, bias_ref, *rest):
        z_ref, mix_ref, c1_ref, halo_ref, abuf, ash, mixed_ref = rest[-7:]
        i = pl.program_id(0)

        @pl.when(i == 0)
        def _():
            abuf[0:CONV_HALO, :] = jnp.zeros((CONV_HALO, CONV_WIDTH), F32) if carry is None else rest[0][...]

        @pl.when(i > 0)
        def _():
            abuf[0:CONV_HALO, :] = abuf[t:t + CONV_HALO, :]

        hv = hn_ref[...]
        for j in range(4):
            cols = slice(j * 512, (j + 1) * 512)
            z_ref[:, cols] = _dot(hv, w_ref[j]) + b_ref[:, cols]

        abuf[CONV_HALO:, :] = z_ref[:, 0:512] * _sigmoid(z_ref[:, 512:1024])
        halo_ref[...] = abuf[t:t + CONV_HALO, :]
        _shift_rows(abuf, ash, t)
        acc = jnp.zeros((t, CONV_WIDTH), F32)
        for k in range(CONV_KERNEL):
            acc = acc + cw_ref[k:k + 1, :] * _window(abuf, ash, CONV_HALO - (CONV_KERNEL - 1) + k, t)
        c1 = acc + cb_ref[...]
        c1_ref[...] = c1
        xh, _ = _ln_stats(c1)
        ln = xh * lng_ref[...] + lnb_ref[...]
        mix_ref[:, 0:512] = (ln * _sigmoid(ln)).astype(BF16)

        u, _ = _gelu_parts(z_ref[:, 1024:1536])
        gv, _ = _gelu_parts(z_ref[:, 1536:2048])
        vxh, _ = _ln_stats(gv)
        v = vxh * gg_ref[...] + gb_ref[...]
        low = _lane_is_low_head()
        v_lo = jnp.where(low, v, 0.0).astype(BF16)
        v_hi = jnp.where(low, 0.0, v).astype(BF16)
        _gm_mix(v_lo, v_hi, wpair_ref, bias_ref, mixed_ref, t)
        mix_ref[:, 512:1024] = (u * mixed_ref[...]).astype(BF16)

    vec = lambda n: pl.BlockSpec((1, n), lambda i: (0, 0))
    tile = lambda n: pl.BlockSpec((t, n), lambda i: (i + first, 0))
    halo = pl.BlockSpec((CONV_HALO, CONV_WIDTH), lambda i: (0, 0))
    in_specs = [tile(D_MODEL), pl.BlockSpec((4, D_MODEL, 512), lambda i: (0, 0, 0)), vec(2048), halo,
                vec(512), vec(512), vec(512), vec(512), vec(512),
                pl.BlockSpec((4, CHUNK, 2 * CHUNK), lambda i: (0, 0, 0)),
                pl.BlockSpec((CHUNK, GM_WIDTH), lambda i: (0, 0))]
    operands = [hn, w_in, b_in, cw, cb, lng, lnb, gg, gb, wpair, bias]
    aliases = {}
    if carry is not None:
        aliases = {len(operands) + 1 + k: k for k in range(3)}
        in_specs += [halo, ANY, ANY, ANY]
        operands += [carry[3], *carry[:3]]
    return _tied_call(
        body, after, name="seqmix_fwd_%d" % first, grid=(count,),
        in_specs=in_specs, out_specs=[tile(2048), tile(D_MODEL), tile(CONV_WIDTH), halo],
        out_shape=[jax.ShapeDtypeStruct((s, 2048), F32), jax.ShapeDtypeStruct((s, D_MODEL), BF16),
                   jax.ShapeDtypeStruct((s, CONV_WIDTH), F32), jax.ShapeDtypeStruct((CONV_HALO, CONV_WIDTH), F32)],
        input_output_aliases=aliases,
        scratch_shapes=[pltpu.VMEM((t + CONV_HALO, CONV_WIDTH), F32),
                        pltpu.VMEM((SUBLANES - 1, t + CONV_HALO - SUBLANES, CONV_WIDTH), F32),
                        pltpu.VMEM((t, GM_WIDTH), F32)],
        compiler_params=_params(("arbitrary",)),
    )(*operands)


def _mem_kv(mem, g, wkv):
    m = mem.shape[0]

    def body(mem_ref, g_ref, w_ref, mn_ref, kv_ref):
        mv = mem_ref[...]
        mn = (mv * _rms_stats(mv) * g_ref[...]).astype(BF16)
        mn_ref[...] = mn
        for j in range(4):
            kv_ref[:, j * 512:(j + 1) * 512] = _dot(mn, w_ref[j]).astype(BF16)

    return pl.pallas_call(
        body, name="mem_kv",
        out_shape=[jax.ShapeDtypeStruct((m, D_MODEL), BF16), jax.ShapeDtypeStruct((m, 2 * D_MODEL), BF16)],
        compiler_params=pltpu.CompilerParams(vmem_limit_bytes=VMEM_LIMIT_BYTES),
    )(mem, g, wkv)


def _softmax_rows(sc):
    e = jnp.exp(sc - jnp.max(sc, axis=-1, keepdims=True))
    return e / jnp.sum(e, axis=-1, keepdims=True)


def _attn_block_fwd(x, mix, w_out, g_xa, wq, kv, wo, g_ffn, ts, after=()):
    s, m = x.shape[0], kv.shape[0]
    scale = XA_HEAD_DIM ** -0.5

    def body(x_ref, mix_ref, wout_ref, gxa_ref, wq_ref, kv_ref, wo_ref, gffn_ref,
             h1_ref, hn2_ref, q_ref, o_ref, h2_ref, hn3_ref):
        h1 = x_ref[...] + _dot(mix_ref[...], wout_ref[...])
        h1_ref[...] = h1
        hn2 = (h1 * _rms_stats(h1) * gxa_ref[...]).astype(BF16)
        hn2_ref[...] = hn2
        q_ref[...] = _dot(hn2, wq_ref[...]).astype(BF16)
        for h in range(XA_HEADS):
            cols = slice(h * XA_HEAD_DIM, (h + 1) * XA_HEAD_DIM)
            vcols = slice(D_MODEL + h * XA_HEAD_DIM, D_MODEL + (h + 1) * XA_HEAD_DIM)
            p = _softmax_rows(_dot_nt(q_ref[:, cols], kv_ref[:, cols]) * scale)
            o_ref[:, cols] = _dot(p.astype(BF16), kv_ref[:, vcols]).astype(BF16)
        h2 = h1 + _dot(o_ref[...], wo_ref[...])
        h2_ref[...] = h2
        hn3_ref[...] = (h2 * _rms_stats(h2) * gffn_ref[...]).astype(BF16)

    row = pl.BlockSpec((ts, D_MODEL), lambda i: (i, 0))
    full = pl.BlockSpec((D_MODEL, D_MODEL), lambda i: (0, 0))
    vec = pl.BlockSpec((1, D_MODEL), lambda i: (0, 0))
    f32 = jax.ShapeDtypeStruct((s, D_MODEL), F32)
    bf16 = jax.ShapeDtypeStruct((s, D_MODEL), BF16)
    return _tied_call(
        body, after, name="attn_block_fwd", grid=(s // ts,),
        in_specs=[row, row, full, vec, full, pl.BlockSpec((m, 2 * D_MODEL), lambda i: (0, 0)), full, vec],
        out_specs=[row] * 6,
        out_shape=[f32, bf16, bf16, bf16, f32, bf16],
        compiler_params=_params(("parallel",)),
    )(x, mix, w_out, g_xa, wq, kv, wo, g_ffn)


_FFN_CHUNKS = (slice(0, 8 * LANES), slice(8 * LANES, 16 * LANES), slice(16 * LANES, FFN_HIDDEN))


def _ffn_up(hn, wgu, ts, after=()):
    s = hn.shape[0]

    def body(hn_ref, w_ref, gu_ref, act_ref):
        hv = hn_ref[...]
        for cols in _FFN_CHUNKS:
            gate = _dot(hv, w_ref[0, :, cols])
            up = _dot(hv, w_ref[1, :, cols])
            gu_ref[0, :, cols] = gate.astype(BF16)
            gu_ref[1, :, cols] = up.astype(BF16)
            act_ref[:, cols] = (gate * _sigmoid(gate) * up).astype(BF16)

    return _tied_call(
        body, after, name="ffn_up", grid=(s // ts,),
        in_specs=[pl.BlockSpec((ts, D_MODEL), lambda i: (i, 0)),
                  pl.BlockSpec((2, D_MODEL, FFN_HIDDEN), lambda i: (0, 0, 0))],
        out_specs=[pl.BlockSpec((2, ts, FFN_HIDDEN), lambda i: (0, i, 0)),
                   pl.BlockSpec((ts, FFN_HIDDEN), lambda i: (i, 0))],
        out_shape=[jax.ShapeDtypeStruct((2, s, FFN_HIDDEN), BF16), jax.ShapeDtypeStruct((s, FFN_HIDDEN), BF16)],
        compiler_params=_params(("parallel",)),
    )(hn, wgu)


def _ffn_down_loss(act, wd, h2, g, target, ts):
    s = act.shape[0]

    def body(act_ref, wd_ref, h2_ref, g_ref, t_ref, dh_ref, dhb_ref, sq_ref, dg_ref):
        @pl.when(pl.program_id(0) == 0)
        def _():
            sq_ref[...] = jnp.zeros_like(sq_ref)
            dg_ref[...] = jnp.zeros_like(dg_ref)

        h3 = h2_ref[...] + _dot(act_ref[...], wd_ref[...])
        r = _rms_stats(h3)
        gv = g_ref[...]
        diff = h3 * r * gv - t_ref[...]
        sq_ref[...] += _rowsum(diff * diff)
        dh, dg = _rms_bwd(diff / D_MODEL, h3, r, gv)
        dh_ref[...] = dh
        dhb_ref[...] = dh.astype(BF16)
        dg_ref[...] += dg

    row = pl.BlockSpec((ts, D_MODEL), lambda i: (i, 0))
    vec = pl.BlockSpec((1, D_MODEL), lambda i: (0, 0))
    return pl.pallas_call(
        body, name="ffn_down_loss", grid=(s // ts,),
        in_specs=[pl.BlockSpec((ts, FFN_HIDDEN), lambda i: (i, 0)),
                  pl.BlockSpec((FFN_HIDDEN, D_MODEL), lambda i: (0, 0)), row, vec, row],
        out_specs=[row, row, vec, vec],
        out_shape=[jax.ShapeDtypeStruct((s, D_MODEL), F32), jax.ShapeDtypeStruct((s, D_MODEL), BF16),
                   jax.ShapeDtypeStruct((1, D_MODEL), F32), jax.ShapeDtypeStruct((1, D_MODEL), F32)],
        compiler_params=_params(("arbitrary",)),
    )(act, wd, h2, g, target)


def _grad_w(a, b, tk, tn, name, after=(), shards=1):
    s, k = a.shape
    gb, _, n = b.shape
    nblk = n // tn
    ws = tn // shards
    tsr = GRAD_ROWS if s % GRAD_ROWS == 0 else s

    def body(a_ref, b_ref, o_ref):
        part = _dot_tn(a_ref[...], b_ref[0])

        @pl.when(pl.program_id(2) == 0)
        def _():
            for j in range(shards):
                o_ref[j] = part[:, j * ws:(j + 1) * ws]

        @pl.when(pl.program_id(2) > 0)
        def _():
            for j in range(shards):
                o_ref[j] += part[:, j * ws:(j + 1) * ws]

    return _tied_call(
        body, after, name=name, grid=(gb * nblk, k // tk, s // tsr),
        in_specs=[pl.BlockSpec((tsr, tk), lambda ni, ki, si: (si, ki)),
                  pl.BlockSpec((1, tsr, tn), lambda ni, ki, si: (ni // nblk, si, ni % nblk))],
        out_specs=pl.BlockSpec((shards, tk, ws), lambda ni, ki, si: (ni, ki, 0)),
        out_shape=jax.ShapeDtypeStruct((gb * nblk * shards, k, ws), F32),
        compiler_params=_params(("parallel", "parallel", "arbitrary")),
    )(a, b)


def _grad_w_square(pairs, name, after=()):
    n = len(pairs)
    s = pairs[0][0].shape[0]
    tsr = GRAD_ROWS // 2 if s % (GRAD_ROWS // 2) == 0 else s

    def body(*refs):
        ins, outs = refs[:2 * n], refs[2 * n:]
        parts = [_dot_tn(ins[2 * a][...], ins[2 * a + 1][...]) for a in range(n)]

        @pl.when(pl.program_id(0) == 0)
        def _():
            for a in range(n):
                outs[a][...] = parts[a]

        @pl.when(pl.program_id(0) > 0)
        def _():
            for a in range(n):
                outs[a][...] += parts[a]

    row = pl.BlockSpec((tsr, D_MODEL), lambda i: (i, 0))
    return _tied_call(
        body, after, name=name, grid=(s // tsr,),
        in_specs=[row] * (2 * n), out_specs=[pl.BlockSpec((D_MODEL, D_MODEL), lambda i: (0, 0))] * n,
        out_shape=[jax.ShapeDtypeStruct((D_MODEL, D_MODEL), F32)] * n,
        compiler_params=_params(("arbitrary",)),
    )(*[x for p in pairs for x in p])


def _ffn_bwd(dh3, wd, gu, wgu, h2, g, t, after=()):
    s = dh3.shape[0]

    def body(dh3_ref, wd_ref, gu_ref, w_ref, h2_ref, g_ref, dgu_ref, dh2_ref, dh2b_ref, dg_ref):
        @pl.when(pl.program_id(0) == 0)
        def _():
            dg_ref[...] = jnp.zeros_like(dg_ref)

        dh3v = dh3_ref[...]
        dhb = dh3v.astype(BF16)
        for cols in _FFN_CHUNKS:
            dact = _dot_nt(dhb, wd_ref[cols, :])
            gate, up = gu_ref[0, :, cols].astype(F32), gu_ref[1, :, cols].astype(F32)
            sg = _sigmoid(gate)
            dgu_ref[0, :, cols] = (dact * up * (sg * (1.0 + gate * (1.0 - sg)))).astype(BF16)
            dgu_ref[1, :, cols] = (dact * (gate * sg)).astype(BF16)
        dhn = _dot_nt(dgu_ref[0], w_ref[0]) + _dot_nt(dgu_ref[1], w_ref[1])
        h2 = h2_ref[...]
        dv, dg = _rms_bwd(dhn, h2, _rms_stats(h2), g_ref[...])
        dh2 = dh3v + dv
        dh2_ref[...] = dh2
        dh2b_ref[...] = dh2.astype(BF16)
        dg_ref[...] += dg

    row = pl.BlockSpec((t, D_MODEL), lambda i: (i, 0))
    wide = pl.BlockSpec((2, t, FFN_HIDDEN), lambda i: (0, i, 0))
    vec = pl.BlockSpec((1, D_MODEL), lambda i: (0, 0))
    return _tied_call(
        body, after, name="ffn_bwd", grid=(s // t,),
        in_specs=[row, pl.BlockSpec((FFN_HIDDEN, D_MODEL), lambda i: (0, 0)), wide,
                  pl.BlockSpec((2, D_MODEL, FFN_HIDDEN), lambda i: (0, 0, 0)), row, vec],
        out_specs=[wide, row, row, vec],
        out_shape=[jax.ShapeDtypeStruct((2, s, FFN_HIDDEN), BF16), jax.ShapeDtypeStruct((s, D_MODEL), F32),
                   jax.ShapeDtypeStruct((s, D_MODEL), BF16), jax.ShapeDtypeStruct((1, D_MODEL), F32)],
        compiler_params=_params(("arbitrary",)),
    )(dh3, wd, gu, wgu, h2, g)


def _attn_bwd(dh2, wo, q, kv, wq, h1, g, ts, after=()):
    s, m = q.shape[0], kv.shape[0]
    scale = XA_HEAD_DIM ** -0.5

    def body(dh2_ref, wo_ref, q_ref, kv_ref, wq_ref, h1_ref, g_ref, dh1_ref, dh1b_ref, dq_ref, dkv_ref, dg_ref):
        @pl.when(pl.program_id(0) == 0)
        def _():
            dkv_ref[...] = jnp.zeros_like(dkv_ref)
            dg_ref[...] = jnp.zeros_like(dg_ref)

        do = _dot_nt(dh2_ref[...].astype(BF16), wo_ref[...]).astype(BF16)
        for h in range(XA_HEADS):
            cols = slice(h * XA_HEAD_DIM, (h + 1) * XA_HEAD_DIM)
            vcols = slice(D_MODEL + h * XA_HEAD_DIM, D_MODEL + (h + 1) * XA_HEAD_DIM)
            qh, kh, vh, doh = q_ref[:, cols], kv_ref[:, cols], kv_ref[:, vcols], do[:, cols]
            p = _softmax_rows(_dot_nt(qh, kh) * scale)
            dp = _dot_nt(doh, vh)
            ds = (p * (dp - jnp.sum(dp * p, axis=-1, keepdims=True)) * scale).astype(BF16)
            dq_ref[:, cols] = _dot(ds, kh).astype(BF16)
            dkv_ref[:, cols] += _dot_tn(ds, qh)
            dkv_ref[:, vcols] += _dot_tn(p.astype(BF16), doh)
        dhn = _dot_nt(dq_ref[...], wq_ref[...])
        h1 = h1_ref[...]
        dv, dg = _rms_bwd(dhn, h1, _rms_stats(h1), g_ref[...])
        dh1 = dh2_ref[...] + dv
        dh1_ref[...] = dh1
        dh1b_ref[...] = dh1.astype(BF16)
        dg_ref[...] += dg

    row = pl.BlockSpec((ts, D_MODEL), lambda i: (i, 0))
    full = pl.BlockSpec((D_MODEL, D_MODEL), lambda i: (0, 0))
    kvs = pl.BlockSpec((m, 2 * D_MODEL), lambda i: (0, 0))
    vec = pl.BlockSpec((1, D_MODEL), lambda i: (0, 0))
    return _tied_call(
        body, after, name="attn_bwd", grid=(s // ts,),
        in_specs=[row, full, row, kvs, full, row, vec],
        out_specs=[row, row, row, kvs, vec],
        out_shape=[jax.ShapeDtypeStruct((s, D_MODEL), F32), jax.ShapeDtypeStruct((s, D_MODEL), BF16),
                   jax.ShapeDtypeStruct((s, D_MODEL), BF16),
                   jax.ShapeDtypeStruct((m, 2 * D_MODEL), F32), jax.ShapeDtypeStruct((1, D_MODEL), F32)],
        compiler_params=_params(("arbitrary",)),
    )(dh2, wo, q, kv, wq, h1, g)


def _mem_kv_bwd(dkv, mn, wkv, mem, g, after=()):
    m = mem.shape[0]

    def body(dkv_ref, mn_ref, w_ref, mem_ref, g_ref, dw_ref, dg_ref):
        dmn = jnp.zeros((m, D_MODEL), F32)
        mn = mn_ref[...]
        for j in range(4):
            dj = dkv_ref[:, j * 512:(j + 1) * 512].astype(BF16)
            dw_ref[j] = _dot_tn(mn, dj)
            dmn = dmn + _dot_nt(dj, w_ref[j])
        mv = mem_ref[...]
        dg_ref[...] = _rowsum(dmn * (mv * _rms_stats(mv)))

    return _tied_call(
        body, after, name="mem_kv_bwd", in_specs=[pl.BlockSpec(memory_space=pltpu.VMEM)] * 5,
        out_shape=[jax.ShapeDtypeStruct((4, D_MODEL, 512), F32), jax.ShapeDtypeStruct((1, D_MODEL), F32)],
        compiler_params=pltpu.CompilerParams(vmem_limit_bytes=VMEM_LIMIT_BYTES),
    )(dkv, mn, wkv, mem, g)


def _seqmix_bwd(dh1, x, z, c1, w_out, w_in, g_mix, cw, lng, lnb, gg, gb, wpair, wpair_t, bias, t, after=()):
    s = x.shape[0]
    nt = s // t

    def body(dh1_ref, x_ref, z_ref, c1_ref, wo_ref, wi_ref, gm_ref, cw_ref, lng_ref, lnb_ref,
             gg_ref, gb_ref, wpair_ref, wpt_ref, bias_ref,
             gx_ref, dz_ref, dcw_ref, dcb_ref, dlng_ref, dlnb_ref, dgg_ref, dgb_ref, dws_ref, dbs_ref,
             dbin_ref, dgm_ref, dbuf, dsh, mixed_ref, dv_ref):
        i = pl.program_id(0)
        accs = (dcw_ref, dcb_ref, dlng_ref, dlnb_ref, dgg_ref, dgb_ref, dws_ref, dbs_ref, dbin_ref, dgm_ref)

        @pl.when(i == 0)
        def _():
            for r in accs:
                r[...] = jnp.zeros_like(r)
            dbuf[t:t + CONV_HALO, :] = jnp.zeros((CONV_HALO, CONV_WIDTH), F32)

        @pl.when(i > 0)
        def _():
            dbuf[t:t + CONV_HALO, :] = dbuf[0:CONV_HALO, :]

        dmix = _dot_nt(dh1_ref[...].astype(BF16), wo_ref[...])

        xh, rs = _ln_stats(c1_ref[...])
        lng = lng_ref[...]
        ln = xh * lng + lnb_ref[...]
        sl = _sigmoid(ln)
        dln = dmix[:, 0:512] * (sl * (1.0 + ln * (1.0 - sl)))
        dc1, dg_ln, db_ln = _ln_bwd(dln, xh, rs, lng)
        dlng_ref[...] += dg_ln
        dlnb_ref[...] += db_ln
        dcb_ref[...] += _rowsum(dc1)
        dbuf[0:t, :] = dc1

        za = z_ref[:, 0:512]
        sg = _sigmoid(z_ref[:, 512:1024])
        a = za * sg
        _shift_rows(dbuf, dsh, t)

        da = jnp.zeros((t, CONV_WIDTH), F32)
        for k in range(CONV_KERNEL):
            later = _window(dbuf, dsh, CONV_KERNEL - 1 - k, t)
            da = da + cw_ref[k:k + 1, :] * later
            dcw_ref[k:k + 1, :] += _rowsum(a * later)
        dza = da * sg
        dzg = da * za * (sg * (1.0 - sg))
        dz_ref[:, 0:512] = dza.astype(BF16)
        dz_ref[:, 512:1024] = dzg.astype(BF16)
        dbin_ref[:, 0:512] += _rowsum(dza)
        dbin_ref[:, 512:1024] += _rowsum(dzg)

        dgm = dmix[:, 512:1024]
        u, du_dz = _gelu_parts(z_ref[:, 1024:1536])
        gv, dgv_dz = _gelu_parts(z_ref[:, 1536:2048])
        vxh, vrs = _ln_stats(gv)
        ggv = gg_ref[...]
        v = vxh * ggv + gb_ref[...]
        low = _lane_is_low_head()
        v_lo = jnp.where(low, v, 0.0).astype(BF16)
        v_hi = jnp.where(low, 0.0, v).astype(BF16)
        _gm_mix(v_lo, v_hi, wpair_ref, bias_ref, mixed_ref, t)
        dzu = dgm * mixed_ref[...] * du_dz
        dm = dgm * u
        dm_lo = jnp.where(low, dm, 0.0).astype(BF16)
        dm_hi = jnp.where(low, 0.0, dm).astype(BF16)
        vb = v.astype(BF16)
        tril = (lax.broadcasted_iota(jnp.int32, (CHUNK, CHUNK), 1)
                <= lax.broadcasted_iota(jnp.int32, (CHUNK, CHUNK), 0))
        for n in range(t // CHUNK):
            rows = slice(n * CHUNK, (n + 1) * CHUNK)
            dbs_ref[...] += dm[rows, :]
            for j in range(GM_HEADS // 2):
                cols = slice(j * LANES, (j + 1) * LANES)
                stack = jnp.concatenate([dm_lo[rows, cols], dm_hi[rows, cols]], axis=0)
                dws = _dot_nt(stack, vb[rows, cols])
                dws_ref[2 * j] += jnp.where(tril, dws[0:CHUNK], 0.0)
                dws_ref[2 * j + 1] += jnp.where(tril, dws[CHUNK:2 * CHUNK], 0.0)
                dv_ref[rows, cols] = _dot(wpt_ref[j], stack)
        dgv, dg_gm, db_gm = _ln_bwd(dv_ref[...], vxh, vrs, ggv)
        dgg_ref[...] += dg_gm
        dgb_ref[...] += db_gm
        dzv = dgv * dgv_dz
        dz_ref[:, 1024:1536] = dzu.astype(BF16)
        dz_ref[:, 1536:2048] = dzv.astype(BF16)
        dbin_ref[:, 1024:1536] += _rowsum(dzu)
        dbin_ref[:, 1536:2048] += _rowsum(dzv)

        dhn = jnp.zeros((t, D_MODEL), F32)
        for j in range(4):
            dhn = dhn + _dot_nt(dz_ref[:, j * 512:(j + 1) * 512], wi_ref[j])
        xv = x_ref[...]
        dv, dg = _rms_bwd(dhn, xv, _rms_stats(xv), gm_ref[...])
        gx_ref[...] = dh1_ref[...] + dv
        dgm_ref[...] += dg

    rev = lambda w: pl.BlockSpec((t, w), lambda i: (nt - 1 - i, 0))
    const = lambda *shape: pl.BlockSpec(shape, lambda i: (0,) * len(shape))
    f32 = lambda *shape: jax.ShapeDtypeStruct(shape, F32)
    return _tied_call(
        body, after, name="seqmix_bwd", grid=(nt,),
        in_specs=[rev(D_MODEL), rev(D_MODEL), rev(2048), rev(CONV_WIDTH),
                  const(D_MODEL, D_MODEL), const(4, D_MODEL, 512), const(1, D_MODEL),
                  const(CONV_HALO, CONV_WIDTH), const(1, 512), const(1, 512), const(1, 512), const(1, 512),
                  const(4, CHUNK, 2 * CHUNK), const(4, CHUNK, 2 * CHUNK), const(CHUNK, GM_WIDTH)],
        out_specs=[rev(D_MODEL), rev(2048),
                   const(CONV_HALO, CONV_WIDTH), const(1, 512), const(1, 512), const(1, 512), const(1, 512),
                   const(1, 512), const(GM_HEADS, CHUNK, CHUNK), const(CHUNK, GM_WIDTH), const(1, 2048),
                   const(1, D_MODEL)],
        out_shape=[f32(s, D_MODEL), jax.ShapeDtypeStruct((s, 2048), BF16),
                   f32(CONV_HALO, CONV_WIDTH), f32(1, 512), f32(1, 512), f32(1, 512), f32(1, 512),
                   f32(1, 512), f32(GM_HEADS, CHUNK, CHUNK), f32(CHUNK, GM_WIDTH), f32(1, 2048),
                   f32(1, D_MODEL)],
        scratch_shapes=[pltpu.VMEM((t + CONV_HALO, CONV_WIDTH), F32),
                        pltpu.VMEM((SUBLANES - 1, t + CONV_HALO - SUBLANES, CONV_WIDTH), F32),
                        pltpu.VMEM((t, GM_WIDTH), F32), pltpu.VMEM((t, GM_WIDTH), F32)],
        compiler_params=_params(("arbitrary",)),
    )(dh1, x, z, c1, w_out, w_in, g_mix, cw, lng, lnb, gg, gb, wpair, wpair_t, bias)


def _head_bias_grad(dbs):
    def body(d_ref, o_ref):
        dv = d_ref[...]
        lane = lax.broadcasted_iota(jnp.int32, (CHUNK, LANES), 1)
        acc = jnp.zeros((CHUNK, LANES), F32)
        for h in range(GM_HEADS):
            sh = jnp.sum(dv[:, h * GM_HEAD_DIM:(h + 1) * GM_HEAD_DIM], axis=-1, keepdims=True)
            acc = acc + jnp.where(lane == h, sh, 0.0)
        o_ref[...] = acc

    return pl.pallas_call(body, name="head_bias_grad",
                          out_shape=jax.ShapeDtypeStruct((CHUNK, LANES), F32))(dbs)


def kernel(x, mem, norm_mix_g, w_in, b_in, conv_w, conv_b, conv_ln_g, conv_ln_b, gm_ln_g, gm_ln_b, gm_w_s, gm_b_s, w_out, norm_xa_g, mem_norm_g, xa_wq, xa_wkv, xa_wo, norm_ffn_g, ffn_w_gate_up, ffn_w_down, final_norm_g, loss_target, m_norm_mix_g, m_w_in, m_b_in, m_conv_w, m_conv_b, m_conv_ln_g, m_conv_ln_b, m_gm_ln_g, m_gm_ln_b, m_gm_w_s, m_gm_b_s, m_w_out, m_norm_xa_g, m_mem_norm_g, m_xa_wq, m_xa_wkv, m_xa_wo, m_norm_ffn_g, m_ffn_w_gate_up, m_ffn_w_down, m_final_norm_g, v_norm_mix_g, v_w_in, v_b_in, v_conv_w, v_conv_b, v_conv_ln_g, v_conv_ln_b, v_gm_ln_g, v_gm_ln_b, v_gm_w_s, v_gm_b_s, v_w_out, v_norm_xa_g, v_mem_norm_g, v_xa_wq, v_xa_wkv, v_xa_wo, v_norm_ffn_g, v_ffn_w_gate_up, v_ffn_w_down, v_final_norm_g):
    weights = dict(norm_mix_g=norm_mix_g, w_in=w_in, b_in=b_in, conv_w=conv_w, conv_b=conv_b, conv_ln_g=conv_ln_g,
                   conv_ln_b=conv_ln_b, gm_ln_g=gm_ln_g, gm_ln_b=gm_ln_b, gm_w_s=gm_w_s, gm_b_s=gm_b_s, w_out=w_out,
                   norm_xa_g=norm_xa_g, mem_norm_g=mem_norm_g, xa_wq=xa_wq, xa_wkv=xa_wkv, xa_wo=xa_wo,
                   norm_ffn_g=norm_ffn_g, ffn_w_gate_up=ffn_w_gate_up, ffn_w_down=ffn_w_down,
                   final_norm_g=final_norm_g)
    m_in = dict(norm_mix_g=m_norm_mix_g, w_in=m_w_in, b_in=m_b_in, conv_w=m_conv_w, conv_b=m_conv_b,
                conv_ln_g=m_conv_ln_g, conv_ln_b=m_conv_ln_b, gm_ln_g=m_gm_ln_g, gm_ln_b=m_gm_ln_b, gm_w_s=m_gm_w_s,
                gm_b_s=m_gm_b_s, w_out=m_w_out, norm_xa_g=m_norm_xa_g, mem_norm_g=m_mem_norm_g, xa_wq=m_xa_wq,
                xa_wkv=m_xa_wkv, xa_wo=m_xa_wo, norm_ffn_g=m_norm_ffn_g, ffn_w_gate_up=m_ffn_w_gate_up,
                ffn_w_down=m_ffn_w_down, final_norm_g=m_final_norm_g)
    v_in = dict(norm_mix_g=v_norm_mix_g, w_in=v_w_in, b_in=v_b_in, conv_w=v_conv_w, conv_b=v_conv_b,
                conv_ln_g=v_conv_ln_g, conv_ln_b=v_conv_ln_b, gm_ln_g=v_gm_ln_g, gm_ln_b=v_gm_ln_b, gm_w_s=v_gm_w_s,
                gm_b_s=v_gm_b_s, w_out=v_w_out, norm_xa_g=v_norm_xa_g, mem_norm_g=v_mem_norm_g, xa_wq=v_xa_wq,
                xa_wkv=v_xa_wkv, xa_wo=v_xa_wo, norm_ffn_g=v_norm_ffn_g, ffn_w_gate_up=v_ffn_w_gate_up,
                ffn_w_down=v_ffn_w_down, final_norm_g=v_final_norm_g)
    grads, delta, new_m, new_v = {}, {}, {}, {}

    s = x.shape[1]
    ts = _row_tile(s)
    tb = max(CHUNK, ts // 2)
    tw = 2 * ts if s % (2 * ts) == 0 and ts >= 512 else ts
    cx, cy, cc = _mesh_pos()
    chip = 2 * cx + cy
    pos = jnp.stack([chip, cc]).astype(jnp.int32)
    row = lambda a: a.reshape(1, -1)
    x2, mem2, tgt2 = x[0], mem[0], loss_target[0]

    big = dict(w_in=w_in, xa_wkv=xa_wkv, w_out=w_out, xa_wq=xa_wq, xa_wo=xa_wo,
               ffn_w_gate_up=ffn_w_gate_up, ffn_w_down=ffn_w_down)
    big_names = list(big)
    halves = lambda a: a.reshape(2, a.shape[0] // 2, a.shape[1])
    conv_w_pad = jnp.pad(conv_w, ((0, CONV_HALO - CONV_KERNEL), (0, 0)))
    first_names = ["w_in", "conv_w"]
    later_names = [nm for nm in big_names if nm != "w_in"]
    cast = dict(zip(first_names, _cast_into_slots([halves(w_in), halves(conv_w_pad)], pos, [BF16, F32], "cast_w_in")))
    cast.update(zip(later_names, _cast_into_slots([halves(big[nm]) for nm in later_names], pos,
                                                  [BF16] * len(later_names), "cast_" + later_names[0],
                                                  side_by_side=(later_names.index("ffn_w_gate_up"),))))

    def start_gather(names, after):
        return _gather_start([cast[nm] for nm in names], "gather_start_" + names[0], after)

    def land_gather(names, started, after):
        send_sems, recv_sems, bufs, _ = started
        return _gather_wait(send_sems, recv_sems, bufs, after, "gather_wait_" + names[0])

    as_weights = lambda names, bufs: dict(zip(names, (b.reshape(b.shape[0], -1, b.shape[-1]) for b in bufs)))

    def start_share(names, landed):
        return _pass_start(landed, "pass_start_" + names[0])

    def share_gather(names, passing, which, after):
        send_sems, recv_sems, bufs, _ = passing
        sems = lambda s: [s[3 * a + k] for a in which for k in range(3)]
        picked = [names[a] for a in which]
        return as_weights(picked, _gather_wait(sems(send_sems), sems(recv_sems), [bufs[a] for a in which], after,
                                               "pass_wait_" + picked[0], _pass_descriptors))

    tril = jnp.tril(jnp.ones((CHUNK, CHUNK), dtype=bool))
    ws = jnp.where(tril[None], gm_w_s, 0.0)
    wpair = jnp.concatenate([ws[0::2], ws[1::2]], axis=2).astype(BF16)
    ws_t = jnp.swapaxes(ws, 1, 2)
    wpair_t = jnp.concatenate([ws_t[0::2], ws_t[1::2]], axis=2).astype(BF16)
    bias = jnp.repeat(gm_b_s.T, GM_HEAD_DIM, axis=1)

    attn_names = ["xa_wkv", "w_out", "xa_wq", "xa_wo"]
    gather_first = start_gather(first_names, ())
    hn1 = _norm_in(x2, row(norm_mix_g), tw, after=(gather_first[3], wpair, wpair_t, bias))
    landed = land_gather(first_names, gather_first, [cast[nm] for nm in later_names] + [hn1])
    passing = start_share(first_names, landed)
    gather_attn = start_gather(attn_names, passing[3])
    gw = share_gather(first_names, passing, (0, 1), gather_attn[3])
    w_in_g = gw["w_in"]
    cw_g = jnp.concatenate([gw["conv_w"][k] for k in range(N_CHIPS)], axis=1)

    seqmix = lambda tiles, **kw: _seqmix_fwd(hn1, w_in_g, row(b_in), cw_g, row(conv_b), row(conv_ln_g),
                                             row(conv_ln_b), row(gm_ln_g), row(gm_ln_b), wpair, bias, ts, tiles, **kw)
    n_tiles = x2.shape[0] // ts
    cut = n_tiles - n_tiles // 4
    begun = seqmix((0, cut))
    landed = land_gather(attn_names, gather_attn, begun[3])
    passing = start_share(attn_names, landed)
    gather_gu = start_gather(["ffn_w_gate_up"], passing[3])
    z, mix, c1, _ = seqmix((cut, n_tiles), carry=begun, after=gather_gu[3])
    wkv_g = share_gather(attn_names, passing, (0,), mix)["xa_wkv"]
    mn, kv = _mem_kv(mem2, row(mem_norm_g), wkv_g)
    gw = share_gather(attn_names, passing, (1, 2, 3), kv)
    w_out_g = gw["w_out"].reshape(D_MODEL, D_MODEL)
    wq_g = gw["xa_wq"].reshape(D_MODEL, D_MODEL)
    wo_g = gw["xa_wo"].reshape(D_MODEL, D_MODEL)
    h1, hn2, q, o, h2, hn3 = _attn_block_fwd(x2, mix, w_out_g, row(norm_xa_g), wq_g, kv, wo_g, row(norm_ffn_g), ts)
    landed = land_gather(["ffn_w_gate_up"], gather_gu, hn3)
    passing = start_share(["ffn_w_gate_up"], landed)
    gather_down = start_gather(["ffn_w_down"], passing[3])
    wgu_g = share_gather(["ffn_w_gate_up"], passing, (0,), gather_down[3])["ffn_w_gate_up"]
    gu, act = _ffn_up(hn3, wgu_g, ts)
    landed = land_gather(["ffn_w_down"], gather_down, act)
    wd_g = as_weights(["ffn_w_down"], _pass_to_sibling(landed, "pass_ffn_w_down"))["ffn_w_down"].reshape(
        FFN_HIDDEN, D_MODEL)
    dh3, dh3_b, sq, d_final_g = _ffn_down_loss(act, wd_g, h2, row(final_norm_g), tgt2, ts)
    loss_here = jnp.broadcast_to(0.5 * jnp.sum(sq) / D_MODEL, (1, 2, SUBLANES, LANES))

    def split(g, nm):
        r, c = big[nm].shape
        return g.reshape(N_CHIPS, 2, r // 2, c)

    def chip_sums(group, arrays, got):
        sums, parts = [None] * len(group), [None] * len(group)
        for blocks in (N_CHIPS, 1):
            idx = [i for i, a in enumerate(arrays) if a.shape[0] == blocks]
            if idx:
                out = _add_halves([arrays[i] for i in idx], [got[i] for i in idx], pos, "chip_sum_" + group[idx[0]],
                                  [F32 if group[i] == "loss" else BF16 for i in idx])
                for k, i in enumerate(idx):
                    sums[i], parts[i] = out[0][k], out[1][k]
        return sums, parts

    def start_swap(group, grads, after=()):
        return _swap_start([split(g, nm) for g, nm in zip(grads, group)], "swap_start_" + group[0], after)

    def start_exchange(group, swapping, after, landed):
        arrays, got = _swap_wait(*swapping[:3], after, "swap_wait_" + group[0])
        sums, parts = chip_sums(group, arrays, got)
        return _exchange_start(sums, parts, "exchange_start_" + group[0], landed)

    def wait_exchange(group, started, after):
        sems, sums, parts, _ = started
        return _exchange_wait(sems, sums, parts, after, "exchange_wait_" + group[0])

    def finish_exchange(group, started, after):
        return _sum_chips(wait_exchange(group, started, after), pos, "total_" + group[0])

    def join(group, after):
        return _join_halves([halves_of[nm] for nm in group], "join_halves_" + group[0], after)

    def start_join(group, after):
        return _join_start([halves_of[nm] for nm in group], "join_start_" + group[0], after)

    def end_join(group, joining, after):
        return _join_wait(*joining[:2], after, "join_wait_" + group[0])

    def update(group, joined, after=()):
        outs = _adamw([(weights[nm], j.reshape(big[nm].shape), m_in[nm], v_in[nm]) for nm, j in zip(group, joined)],
                      "adamw_" + group[0], after)
        for nm, out in zip(group, outs):
            grads[nm], delta[nm], new_m[nm], new_v[nm] = out
        return [new_v[nm] for nm in group]

    def join_and_update(group, after):
        return update(group, join(group, after))

    as3 = lambda a: a.reshape((1,) + a.shape)
    halves_of = {}

    g_down = _grad_w(act, as3(dh3_b), FFN_HALF, D_MODEL, "grad_ffn_w_down")
    group_a = ["ffn_w_down"]
    swap_a = start_swap(group_a, [g_down])
    dgu, dh2, dh2_b, d_ffn_g = _ffn_bwd(dh3, wd_g, gu, wgu_g, h2, row(norm_ffn_g), tb,
                                        after=swap_a[3])
    exch_a = start_exchange(group_a, swap_a, dh2, wd_g)
    g_gu = _grad_w(hn3, dgu, D_MODEL, FFN_HALF, "grad_ffn_w_gate_up", after=exch_a[3])
    halves_of.update(zip(group_a, finish_exchange(group_a, exch_a, g_gu)))

    group_b = ["ffn_w_gate_up"]
    swap_b = start_swap(group_b, [g_gu])
    dh1, dh1_b, dq, dkv, d_xa_g = _attn_bwd(dh2, wo_g, q, kv, wq_g, h1, row(norm_xa_g), ts, after=swap_b[3])
    exch_b = start_exchange(group_b, swap_b, dh1, [halves_of[nm] for nm in group_a])
    joining_a = start_join(group_a, exch_b[3])
    g_wkv, d_mem_g = _mem_kv_bwd(dkv, mn, wkv_g, mem2, row(mem_norm_g), after=joining_a[2])
    g_wo, g_wq, g_wout = _grad_w_square([(o, dh2_b), (hn2, dq), (mix, dh1_b)], "grad_xa_wo", after=joining_a[2])
    done_a = update(group_a, end_join(group_a, joining_a, (g_wkv, g_wo, g_wq, g_wout)))
    halves_of.update(zip(group_b, finish_exchange(group_b, exch_b, done_a)))
    joining_b = start_join(group_b, done_a)
    group_c = ["xa_wo", "xa_wq", "xa_wkv", "w_out"]
    swap_c = start_swap(group_c, [g_wo, g_wq, g_wkv, g_wout], joining_b[2])
    (gx, dz, d_cw, d_cb, d_lng, d_lnb, d_gg, d_gb, d_ws, d_bs_sum, d_bin, d_mix_g) = _seqmix_bwd(
        dh1, x2, z, c1, w_out_g, w_in_g, row(norm_mix_g), cw_g, row(conv_ln_g), row(conv_ln_b),
        row(gm_ln_g), row(gm_ln_b), wpair, wpair_t, bias, tb, after=swap_c[3])
    d_bs = _head_bias_grad(d_bs_sum)[:, :GM_HEADS].T
    exch_c = start_exchange(group_c, swap_c, dz, joining_b[2])
    g_win = _grad_w(hn1, as3(dz), D_MODEL, 1024, "grad_w_in", after=exch_c[3], shards=2)

    small_names = ["norm_mix_g", "b_in", "conv_w", "conv_b", "conv_ln_g", "conv_ln_b", "gm_ln_g", "gm_ln_b",
                   "gm_w_s", "gm_b_s", "norm_xa_g", "mem_norm_g", "norm_ffn_g", "final_norm_g"]
    d_cw_by_chip = jnp.swapaxes(d_cw.reshape(CONV_HALO, N_CHIPS, LANES), 0, 1).reshape(-1, LANES)
    small_grads = dict(norm_mix_g=d_mix_g, b_in=d_bin, conv_w=d_cw_by_chip, conv_b=d_cb, conv_ln_g=d_lng,
                       conv_ln_b=d_lnb, gm_ln_g=d_gg, gm_ln_b=d_gb, gm_w_s=d_ws, gm_b_s=d_bs, norm_xa_g=d_xa_g,
                       mem_norm_g=d_mem_g, norm_ffn_g=d_ffn_g, final_norm_g=d_final_g)

    def rows_form(a):
        a = a.reshape(-1, LANES)
        return jnp.pad(a, ((0, -a.shape[0] % SUBLANES), (0, 0)))

    pieces = [rows_form(small_grads[nm]) for nm in small_names]
    offsets, total = [], 0
    for p in pieces:
        offsets.append(total)
        total += p.shape[0]
    pack_rows = -(-total // 32) * 32
    small_pack = jnp.pad(jnp.concatenate(pieces, axis=0), ((0, pack_rows - total), (0, 0)))

    group_d = ["w_in", "small", "loss"]
    joined_b = end_join(group_b, joining_b, g_win)
    swap_d = _swap_start([split(g_win, "w_in")], "swap_start_w_in", joined_b)
    done_b = update(group_b, joined_b, swap_d[3])
    small_d = [small_pack.reshape(1, 2, pack_rows // 2, LANES), loss_here]
    got_small = _swap_halves(small_d, "swap_halves_small", done_b)
    arrays_d, got_d = _swap_wait(*swap_d[:3], got_small, "swap_wait_w_in")
    sums_d, parts_d = chip_sums(group_d, arrays_d + small_d, got_d + list(got_small))
    parts_c = wait_exchange(group_c, exch_c, sums_d)
    exch_d = _exchange_start(sums_d, parts_d, "exchange_start_w_in", parts_c)
    halves_of.update(zip(group_c, _sum_chips(parts_c, pos, "total_xa_wo", exch_d[3])))
    done_c = join_and_update(group_c, exch_d[3])
    halves_of.update(zip(group_d, finish_exchange(group_d, exch_d, done_c)))
    joined_d = _join_halves([halves_of[nm] for nm in group_d], "join_halves_w_in")
    loss = joined_d[2][0, 0, 0]
    grads["w_in"], delta["w_in"], new_m["w_in"], new_v["w_in"] = _adamw(
        [(w_in, joined_d[0].reshape(w_in.shape), m_w_in, v_w_in)], "adamw_w_in")[0]

    local_rows = lambda a, nm: a if nm == "conv_w" else a.reshape(-1, LANES)
    params = [tuple(local_rows(src[nm], nm) for src in (weights, m_in, v_in)) for nm in small_names]
    outs = _adamw_small(joined_d[1].reshape(pack_rows, LANES), pos, params, offsets, small_names.index("conv_w"))
    for k, nm in enumerate(small_names):
        for dst, a in zip((grads, delta, new_m, new_v), outs[4 * k:4 * k + 4]):
            dst[nm] = a

    order = ["norm_mix_g", "w_in", "b_in", "conv_w", "conv_b", "conv_ln_g", "conv_ln_b", "gm_ln_g", "gm_ln_b",
             "gm_w_s", "gm_b_s", "w_out", "norm_xa_g", "mem_norm_g", "xa_wq", "xa_wkv", "xa_wo", "norm_ffn_g",
             "ffn_w_gate_up", "ffn_w_down", "final_norm_g"]
    fit = lambda a, nm: a.reshape(weights[nm].shape)
    return (loss, gx.reshape(x.shape),
            *[fit(grads[nm], nm) for nm in order], *[fit(delta[nm], nm) for nm in order],
            *[fit(new_m[nm], nm) for nm in order], *[fit(new_v[nm], nm) for nm in order])
```

```python
import functools

import jax
import jax.numpy as jnp
from jax import lax
from jax.experimental import pallas as pl
from jax.experimental.pallas import tpu as pltpu

F32 = jnp.float32
BF16 = jnp.bfloat16

D_MODEL = 1024
CONV_WIDTH = 512
GM_WIDTH = 512
CONV_KERNEL = 31
CONV_HALO = 32
GRAD_ROWS = 2048
CHUNK = 128
GM_HEADS = 8
GM_HEAD_DIM = 64
XA_HEADS = 4
XA_HEAD_DIM = 256
FFN_HIDDEN = 2816
FFN_HALF = FFN_HIDDEN // 2
RMS_EPS = 1e-6
LN_EPS = 1e-5
N_CHIPS = 4
LANES = 128
SUBLANES = 8

ADAM_LR = 0.001
ADAM_B1 = 0.9
ADAM_B2 = 0.999
ADAM_EPS = 1e-08
ADAM_WD = 0.01
ADAM_STEP = 10

VMEM_LIMIT_BYTES = 56 * 1024 * 1024
MESH = pl.DeviceIdType.MESH
ANY = pl.BlockSpec(memory_space=pl.ANY)
HBM_SPEC = pl.BlockSpec(memory_space=pltpu.HBM)
SEM_SPEC = pl.BlockSpec(memory_space=pltpu.SEMAPHORE)

_NT = (((1,), (1,)), ((), ()))
_TN = (((0,), (0,)), ((), ()))
_GELU_C = 0.7978845608028654
_GELU_A = 0.044715


def _dot(a, b):
    return jnp.dot(a, b, preferred_element_type=F32)


def _dot_nt(a, b):
    return lax.dot_general(a, b, _NT, preferred_element_type=F32)


def _dot_tn(a, b):
    return lax.dot_general(a, b, _TN, preferred_element_type=F32)


def _mean(v):
    return jnp.mean(v, axis=-1, keepdims=True)


def _rowsum(v):
    return jnp.sum(v, axis=0, keepdims=True)


def _sigmoid(v):
    return 1.0 / (1.0 + jnp.exp(-v))


def _gelu_parts(v):
    v2 = v * v
    t = jnp.tanh(_GELU_C * (v + _GELU_A * v * v2))
    g = 0.5 * v * (1.0 + t)
    dg = 0.5 * (1.0 + t) + 0.5 * v * (1.0 - t * t) * (_GELU_C * (1.0 + 3.0 * _GELU_A * v2))
    return g, dg


def _rms_stats(v):
    return lax.rsqrt(_mean(v * v) + RMS_EPS)


def _rms_bwd(dy, v, r, g):
    n = v * r
    dn = dy * g
    dv = r * (dn - n * _mean(dn * n))
    return dv, _rowsum(dy * n)


def _ln_stats(v):
    mu = _mean(v)
    xc = v - mu
    rs = lax.rsqrt(_mean(xc * xc) + LN_EPS)
    return xc * rs, rs


def _ln_bwd(dy, xh, rs, g):
    dxh = dy * g
    dv = rs * (dxh - _mean(dxh) - xh * _mean(dxh * xh))
    return dv, _rowsum(dy * xh), _rowsum(dy)


def _params(sem):
    return pltpu.CompilerParams(dimension_semantics=sem, vmem_limit_bytes=VMEM_LIMIT_BYTES)


def _row_tile(s):
    return 512 if s % 512 == 0 and s >= 2048 else 128


def _mesh_pos():
    return lax.axis_index("x"), lax.axis_index("y"), lax.axis_index("c")


def _slot(buf, chip_idx, half):
    if buf.shape[0] == N_CHIPS:
        return buf.at[chip_idx, half]
    width = buf.shape[-1] // 2
    return buf.at[chip_idx // 2, half, :, pl.ds(pl.multiple_of((chip_idx % 2) * width, LANES), width)]


def _cast_into_slots(ws, pos, dtypes, name, side_by_side=()):
    n = len(ws)

    def body(pos_ref, *refs):
        for a in range(n):
            refs[n + a][0] = refs[a][...].astype(dtypes[a])

    def out_spec(a, w):
        if a in side_by_side:
            return pl.BlockSpec((1, 1) + w.shape[1:], lambda i, p: (p[0] // 2, i, 0, p[0] % 2))
        return pl.BlockSpec((1, 1) + w.shape[1:], lambda i, p: (p[0], i, 0, 0))

    def out_shape(a, w):
        if a in side_by_side:
            return (2, 2, w.shape[1], 2 * w.shape[2])
        return (N_CHIPS,) + w.shape

    return pl.pallas_call(
        body, name=name,
        grid_spec=pltpu.PrefetchScalarGridSpec(
            num_scalar_prefetch=1, grid=(2,),
            in_specs=[pl.BlockSpec((1,) + w.shape[1:], lambda i, p: (i, 0, 0)) for w in ws],
            out_specs=[out_spec(a, w) for a, w in enumerate(ws)]),
        out_shape=[jax.ShapeDtypeStruct(out_shape(a, w), dt) for a, (w, dt) in enumerate(zip(ws, dtypes))],
        compiler_params=_params(("parallel",)),
    )(pos, *ws)


def _adam_update(w, g, m, v):
    nm = ADAM_B1 * m + (1.0 - ADAM_B1) * g
    nv = ADAM_B2 * v + (1.0 - ADAM_B2) * (g * g)
    m_hat = nm / (1.0 - ADAM_B1 ** ADAM_STEP)
    v_hat = nv / (1.0 - ADAM_B2 ** ADAM_STEP)
    return -ADAM_LR * (m_hat / (jnp.sqrt(v_hat) + ADAM_EPS) + ADAM_WD * w), nm, nv


ADAM_STEPS = 8


def _adamw(quads, name, after=()):
    n = len(quads)

    def body(*refs):
        ins, outs = refs[:4 * n], refs[4 * n:]
        for a in range(n):
            w, g, m, v = (r[...] for r in ins[4 * a:4 * a + 4])
            outs[4 * a][...] = g
            outs[4 * a + 1][...], outs[4 * a + 2][...], outs[4 * a + 3][...] = _adam_update(w, g, m, v)

    specs = [pl.BlockSpec((q[0].shape[0] // ADAM_STEPS, q[0].shape[1]), lambda i: (i, 0)) for q in quads]
    out = _tied_call(
        body, after, name=name, grid=(ADAM_STEPS,),
        in_specs=[sp for sp in specs for _ in range(4)], out_specs=[sp for sp in specs for _ in range(4)],
        out_shape=[jax.ShapeDtypeStruct(q[0].shape, F32) for q in quads for _ in range(4)],
        compiler_params=_params(("parallel",)),
    )(*[a for q in quads for a in q])
    return [tuple(out[4 * a:4 * a + 4]) for a in range(n)]


def _adamw_small(gpack, pos, params, offsets, conv_at):
    n = len(params)

    def body(pos_ref, g_ref, *refs):
        ins, outs = refs[:3 * n], refs[3 * n:]
        for k in range(n):
            rows = params[k][0].shape[0]
            start = offsets[k]
            if k == conv_at:
                start = pl.multiple_of(start + pos_ref[0] * CONV_HALO, SUBLANES)
            g = g_ref[pl.ds(start, rows), :]
            outs[4 * k][...] = g
            outs[4 * k + 1][...], outs[4 * k + 2][...], outs[4 * k + 3][...] = _adam_update(
                ins[3 * k][...], g, ins[3 * k + 1][...], ins[3 * k + 2][...])

    flat = [a for p in params for a in p]
    vmem = pl.BlockSpec(memory_space=pltpu.VMEM)
    return pl.pallas_call(
        body, name="adamw_small",
        in_specs=[pl.BlockSpec(memory_space=pltpu.SMEM), vmem] + [vmem] * len(flat),
        out_specs=[vmem] * (4 * n),
        out_shape=[jax.ShapeDtypeStruct(p[0].shape, F32) for p in params for _ in range(4)],
    )(pos, gpack, *flat)


def _as_tuple(after):
    return tuple(after) if isinstance(after, (tuple, list)) else (after,)


def _tied_call(body, after, *, in_specs, **kwargs):
    after = _as_tuple(after)
    n_in, n_after = len(in_specs), len(after)

    def tied(*refs):
        body(*refs[:n_in], *refs[n_in + n_after:])

    call = pl.pallas_call(tied, in_specs=list(in_specs) + [ANY] * n_after, **kwargs)
    return lambda *operands: call(*operands, *after)


def _other_chips(x, y):
    return [(1 - x, y), (x, 1 - y), (1 - x, 1 - y)]


def _gather_descriptors(bufs, send_of, recv_of):
    x, y, c = _mesh_pos()
    me = 2 * x + y
    chips = _other_chips(x, y)
    sends, arrivals = [], []
    for a in range(len(bufs)):
        for k in range(3):
            ck = 2 * chips[k][0] + chips[k][1]

            def copy(slot, a=a, k=k):
                return pltpu.make_async_remote_copy(
                    src_ref=_slot(bufs[a], slot, c), dst_ref=_slot(bufs[a], slot, c),
                    send_sem=send_of(a, k), recv_sem=recv_of(a, k),
                    device_id=(*chips[k], c), device_id_type=MESH)

            sends.append(functools.partial(copy, me))
            arrivals.append(functools.partial(copy, ck))
    return sends, arrivals


def _gather_start(bufs, name, after=()):
    n = len(bufs)
    ns = 3 * n

    def body(*refs):
        sems = refs[n:n + 2 * ns]
        thru = refs[n + 2 * ns:2 * n + 2 * ns]
        token = refs[2 * n + 2 * ns]
        _chips_handshake()
        sends, _ = _gather_descriptors(thru, lambda a, k: sems[3 * a + k], lambda a, k: sems[ns + 3 * a + k])
        for cp in sends:
            cp().start()
        token[...] = jnp.zeros_like(token)

    held = [pltpu.with_memory_space_constraint(b, pltpu.HBM) for b in bufs]
    out = _tied_call(
        body, after, name=name,
        out_shape=(*[pltpu.SemaphoreType.DMA(())] * (2 * ns), *[pltpu.HBM(b.shape, b.dtype) for b in held],
                   jax.ShapeDtypeStruct((8, LANES), F32)),
        in_specs=[HBM_SPEC] * n,
        out_specs=(*[SEM_SPEC] * (2 * ns), *[HBM_SPEC] * n, pl.BlockSpec(memory_space=pltpu.VMEM)),
        input_output_aliases={i: 2 * ns + i for i in range(n)},
        compiler_params=pltpu.CompilerParams(has_side_effects=pltpu.SideEffectType.DATAFLOW_SIDE_EFFECTING,
                                             collective_id=CHIPS_COLLECTIVE_ID),
    )(*held)
    return list(out[:ns]), list(out[ns:2 * ns]), list(out[2 * ns:2 * ns + n]), out[2 * ns + n]


def _gather_wait(send_sems, recv_sems, bufs, after, name, descriptors=_gather_descriptors):
    n = len(bufs)
    ns = 3 * n

    def body(*refs):
        buf_ref = refs[:n]
        sem_ref = refs[n:n + 2 * ns]
        sends, arrivals = descriptors(buf_ref, lambda a, k: sem_ref[3 * a + k], lambda a, k: sem_ref[ns + 3 * a + k])
        for cp in sends:
            cp().wait_send()
        for cp in arrivals:
            cp().wait_recv()

    out = pl.pallas_call(
        body, name=name,
        out_shape=tuple(pltpu.HBM(b.shape, b.dtype) for b in bufs),
        in_specs=[HBM_SPEC] * n + [SEM_SPEC] * (2 * ns) + [ANY] * len(_as_tuple(after)),
        out_specs=tuple([HBM_SPEC] * n),
        input_output_aliases={i: i for i in range(n)},
        compiler_params=pltpu.CompilerParams(has_side_effects=pltpu.SideEffectType.DATAFLOW_SIDE_EFFECTING),
    )(*bufs, *send_sems, *recv_sems, *_as_tuple(after))
    return list(out)


SIBLING_COLLECTIVE_ID = 0


def _sibling_handshake():
    x, y, c = _mesh_pos()
    barrier = pltpu.get_barrier_semaphore()
    pl.semaphore_signal(barrier, inc=1, device_id=(x, y, 1 - c), device_id_type=MESH)
    pl.semaphore_wait(barrier, 1)


CHIPS_COLLECTIVE_ID = 1


def _chips_handshake():
    x, y, c = _mesh_pos()
    barrier = pltpu.get_barrier_semaphore()
    for chip in _other_chips(x, y):
        pl.semaphore_signal(barrier, inc=1, device_id=(*chip, c), device_id_type=MESH)
    pl.semaphore_wait(barrier, 3)


def _pass_descriptors(bufs, send_of, recv_of):
    x, y, c = _mesh_pos()
    chips = _other_chips(x, y)

    def half(a, k, which):
        ck = 2 * chips[k][0] + chips[k][1]
        return functools.partial(
            pltpu.make_async_remote_copy,
            src_ref=_slot(bufs[a], ck, which), dst_ref=_slot(bufs[a], ck, which),
            send_sem=send_of(a, k), recv_sem=recv_of(a, k),
            device_id=(x, y, 1 - c), device_id_type=MESH)

    pairs = [(a, k) for a in range(len(bufs)) for k in range(3)]
    return [half(a, k, c) for a, k in pairs], [half(a, k, 1 - c) for a, k in pairs]


def _pass_start(bufs, name, after=()):
    n = len(bufs)
    ns = 3 * n

    def body(*refs):
        sems = refs[n:n + 2 * ns]
        thru = refs[n + 2 * ns:2 * n + 2 * ns]
        token = refs[2 * n + 2 * ns]
        _sibling_handshake()
        sends, _ = _pass_descriptors(thru, lambda a, k: sems[3 * a + k], lambda a, k: sems[ns + 3 * a + k])
        for cp in sends:
            cp().start()
        token[...] = jnp.zeros_like(token)

    held = [pltpu.with_memory_space_constraint(b, pltpu.HBM) for b in bufs]
    out = _tied_call(
        body, after, name=name,
        out_shape=(*[pltpu.SemaphoreType.DMA(())] * (2 * ns), *[pltpu.HBM(b.shape, b.dtype) for b in held],
                   jax.ShapeDtypeStruct((8, LANES), F32)),
        in_specs=[HBM_SPEC] * n,
        out_specs=(*[SEM_SPEC] * (2 * ns), *[HBM_SPEC] * n, pl.BlockSpec(memory_space=pltpu.VMEM)),
        input_output_aliases={i: 2 * ns + i for i in range(n)},
        compiler_params=pltpu.CompilerParams(has_side_effects=pltpu.SideEffectType.DATAFLOW_SIDE_EFFECTING,
                                             collective_id=SIBLING_COLLECTIVE_ID),
    )(*held)
    return list(out[:ns]), list(out[ns:2 * ns]), list(out[2 * ns:2 * ns + n]), out[2 * ns + n]


def _pass_to_sibling(bufs, name, after=()):
    n = len(bufs)

    def body(*refs):
        outs = refs[n:2 * n]
        send_sem, recv_sem = refs[2 * n:]
        _sibling_handshake()
        sends, arrivals = _pass_descriptors(outs, lambda a, k: send_sem.at[a, k], lambda a, k: recv_sem.at[a, k])
        sends = [cp() for cp in sends]
        for cp in sends:
            cp.start()
        for cp in arrivals:
            cp().wait_recv()
        for cp in sends:
            cp.wait_send()

    return _tied_call(
        body, after, name=name,
        in_specs=[ANY] * n, out_specs=[ANY] * n,
        out_shape=[jax.ShapeDtypeStruct(b.shape, b.dtype) for b in bufs],
        input_output_aliases={a: a for a in range(n)},
        scratch_shapes=[pltpu.SemaphoreType.DMA((n, 3))] * 2,
        compiler_params=pltpu.CompilerParams(collective_id=SIBLING_COLLECTIVE_ID),
    )(*bufs)


def _swap_descriptors(grads, lands, send_of, recv_of):
    x, y, c = _mesh_pos()
    return [functools.partial(
        pltpu.make_async_remote_copy,
        src_ref=grads[a].at[:, pl.ds(1 - c, 1)], dst_ref=lands[a],
        send_sem=send_of(a), recv_sem=recv_of(a),
        device_id=(x, y, 1 - c), device_id_type=MESH) for a in range(len(grads))]


def _swap_halves(grads, name, after=()):
    n = len(grads)

    def body(*refs):
        ins, outs = refs[:n], refs[n:2 * n]
        send_sem, recv_sem = refs[2 * n:]
        _sibling_handshake()
        cps = [cp() for cp in _swap_descriptors(ins, outs, lambda a: send_sem.at[a], lambda a: recv_sem.at[a])]
        for cp in cps:
            cp.start()
        for cp in cps:
            cp.wait()

    out_shape = [jax.ShapeDtypeStruct((g.shape[0], 1) + g.shape[2:], g.dtype) for g in grads]
    return _tied_call(
        body, after, name=name,
        in_specs=[ANY] * n, out_specs=[ANY] * n, out_shape=out_shape,
        scratch_shapes=[pltpu.SemaphoreType.DMA((n,))] * 2,
        compiler_params=pltpu.CompilerParams(collective_id=SIBLING_COLLECTIVE_ID),
    )(*grads)


def _swap_start(grads, name, after=()):
    n, after = len(grads), _as_tuple(after)

    def body(*refs):
        outs = refs[2 * n + len(after):]
        sems, g_thru, l_thru, token = outs[:2 * n], outs[2 * n:3 * n], outs[3 * n:4 * n], outs[4 * n]
        _sibling_handshake()
        for cp in _swap_descriptors(g_thru, l_thru, lambda a: sems[a], lambda a: sems[n + a]):
            cp().start()
        token[...] = jnp.zeros_like(token)

    lands = [lax.empty((g.shape[0], 1) + g.shape[2:], g.dtype) for g in grads]
    held = [pltpu.with_memory_space_constraint(a, pltpu.HBM) for a in (*grads, *lands)]
    out = pl.pallas_call(
        body, name=name,
        out_shape=(*[pltpu.SemaphoreType.DMA(())] * (2 * n), *[pltpu.HBM(a.shape, a.dtype) for a in held],
                   jax.ShapeDtypeStruct((8, LANES), F32)),
        in_specs=[HBM_SPEC] * (2 * n) + [ANY] * len(after),
        out_specs=(*[SEM_SPEC] * (2 * n), *[HBM_SPEC] * (2 * n), pl.BlockSpec(memory_space=pltpu.VMEM)),
        input_output_aliases={i: 2 * n + i for i in range(2 * n)},
        compiler_params=pltpu.CompilerParams(has_side_effects=pltpu.SideEffectType.DATAFLOW_SIDE_EFFECTING,
                                             collective_id=SIBLING_COLLECTIVE_ID),
    )(*held, *after)
    return list(out[:2 * n]), list(out[2 * n:3 * n]), list(out[3 * n:4 * n]), out[4 * n]


def _swap_wait(sems, grads, lands, after, name):
    n = len(grads)

    def body(*refs):
        g_ref, l_ref = refs[:n], refs[n:2 * n]
        sem_ref = refs[2 * n:4 * n]
        for cp in _swap_descriptors(g_ref, l_ref, lambda a: sem_ref[a], lambda a: sem_ref[n + a]):
            cp().wait()

    out = pl.pallas_call(
        body, name=name,
        out_shape=tuple(pltpu.HBM(a.shape, a.dtype) for a in (*grads, *lands)),
        in_specs=[HBM_SPEC] * (2 * n) + [SEM_SPEC] * (2 * n) + [ANY] * len(_as_tuple(after)),
        out_specs=tuple([HBM_SPEC] * (2 * n)),
        input_output_aliases={i: i for i in range(2 * n)},
        compiler_params=pltpu.CompilerParams(has_side_effects=pltpu.SideEffectType.DATAFLOW_SIDE_EFFECTING),
    )(*grads, *lands, *sems, *_as_tuple(after))
    return list(out[:n]), list(out[n:])


def _add_halves(gs, gots, pos, name, dtypes):
    n = len(gs)
    j = gs[0].shape[0]

    def body(pos_ref, *refs):
        g_refs, r_refs = refs[:n], refs[n:2 * n]
        o_refs, p_refs = refs[2 * n:3 * n], refs[3 * n:]
        vals = [(g_refs[a][0, 0] + r_refs[a][0, 0]).astype(dtypes[a]) for a in range(n)]
        for a in range(n):
            o_refs[a][0] = vals[a]
        if j == 1:
            for a in range(n):
                p_refs[a][0] = vals[a]
        else:
            @pl.when(pl.program_id(0) == pos_ref[0])
            def _():
                for a in range(n):
                    p_refs[a][0] = vals[a]

    blk = lambda g: (1,) + g.shape[2:]
    out = pl.pallas_call(
        body, name=name,
        grid_spec=pltpu.PrefetchScalarGridSpec(
            num_scalar_prefetch=1, grid=(j,),
            in_specs=[pl.BlockSpec((1,) + blk(g), lambda i, p: (i, p[1], 0, 0)) for g in gs]
            + [pl.BlockSpec((1,) + blk(g), lambda i, p: (i, 0, 0, 0)) for g in gs],
            out_specs=[pl.BlockSpec(blk(g), lambda i, p: (i, 0, 0)) for g in gs]
            + [pl.BlockSpec(blk(g), lambda i, p: (p[0], 0, 0)) for g in gs]),
        out_shape=[jax.ShapeDtypeStruct((j,) + g.shape[2:], dt) for g, dt in zip(gs, dtypes)]
        + [jax.ShapeDtypeStruct((N_CHIPS,) + g.shape[2:], dt) for g, dt in zip(gs, dtypes)],
        compiler_params=_params(("arbitrary",)),
    )(pos, *gs, *gots)
    return list(out[:n]), list(out[n:])


def _exchange_descriptors(sums, parts, send_of, recv_of):
    x, y, c = _mesh_pos()
    me = 2 * x + y
    chips = _other_chips(x, y)
    sends, arrivals = [], []
    for a in range(len(sums)):
        for k in range(3):
            ck = 2 * chips[k][0] + chips[k][1]
            mine = sums[a].at[ck] if sums[a].shape[0] == N_CHIPS else sums[a].at[0]

            def copy(dst_slot, a=a, k=k, mine=mine):
                return pltpu.make_async_remote_copy(
                    src_ref=mine, dst_ref=parts[a].at[dst_slot],
                    send_sem=send_of(a, k), recv_sem=recv_of(a, k),
                    device_id=(*chips[k], c), device_id_type=MESH)

            sends.append(functools.partial(copy, me))
            arrivals.append(functools.partial(copy, ck))
    return sends, arrivals


def _exchange_start(sums, parts, name, after=()):
    n = len(sums)
    ns = 3 * n

    def body(*refs):
        sems = refs[2 * n:2 * n + 2 * ns]
        sums_thru = refs[2 * n + 2 * ns:3 * n + 2 * ns]
        parts_thru = refs[3 * n + 2 * ns:4 * n + 2 * ns]
        token = refs[4 * n + 2 * ns]
        _chips_handshake()
        sends, _ = _exchange_descriptors(sums_thru, parts_thru, lambda a, k: sems[3 * a + k],
                                         lambda a, k: sems[ns + 3 * a + k])
        for cp in sends:
            cp().start()
        token[...] = jnp.zeros_like(token)

    hbm = lambda a: pltpu.HBM(a.shape, a.dtype)
    held = [pltpu.with_memory_space_constraint(a, pltpu.HBM) for a in (*sums, *parts)]
    out = _tied_call(
        body, after, name=name,
        out_shape=(*[pltpu.SemaphoreType.DMA(())] * (2 * ns), *[hbm(a) for a in held],
                   jax.ShapeDtypeStruct((8, LANES), F32)),
        in_specs=[HBM_SPEC] * (2 * n),
        out_specs=(*[SEM_SPEC] * (2 * ns), *[HBM_SPEC] * (2 * n), pl.BlockSpec(memory_space=pltpu.VMEM)),
        input_output_aliases={i: 2 * ns + i for i in range(2 * n)},
        compiler_params=pltpu.CompilerParams(has_side_effects=pltpu.SideEffectType.DATAFLOW_SIDE_EFFECTING,
                                             collective_id=CHIPS_COLLECTIVE_ID),
    )(*held)
    return (list(out[:2 * ns]), list(out[2 * ns:2 * ns + n]), list(out[2 * ns + n:2 * ns + 2 * n]),
            out[2 * ns + 2 * n])


def _exchange_wait(sems, sums, parts, after, name):
    n = len(sums)
    ns = 3 * n

    def body(*refs):
        sums_ref, parts_ref = refs[:n], refs[n:2 * n]
        sem_ref = refs[2 * n:2 * n + 2 * ns]
        sends, arrivals = _exchange_descriptors(sums_ref, parts_ref, lambda a, k: sem_ref[3 * a + k],
                                                lambda a, k: sem_ref[ns + 3 * a + k])
        for cp in sends:
            cp().wait_send()
        for cp in arrivals:
            cp().wait_recv()

    hbm = lambda a: pltpu.HBM(a.shape, a.dtype)
    out = pl.pallas_call(
        body, name=name,
        out_shape=tuple(hbm(a) for a in (*sums, *parts)),
        in_specs=[HBM_SPEC] * (2 * n) + [SEM_SPEC] * (2 * ns) + [ANY] * len(_as_tuple(after)),
        out_specs=tuple([HBM_SPEC] * (2 * n)),
        input_output_aliases={i: i for i in range(2 * n)},
        compiler_params=pltpu.CompilerParams(has_side_effects=pltpu.SideEffectType.DATAFLOW_SIDE_EFFECTING),
    )(*sums, *parts, *sems, *_as_tuple(after))
    return list(out[n:])


def _sum_chips(parts, pos, name, after=()):
    n = len(parts)
    after = _as_tuple(after)

    def body(pos_ref, *refs):
        outs = refs[n + len(after):]
        for a in range(n):
            p_ref = refs[a]
            outs[a][0] = (((p_ref[0].astype(F32) + p_ref[1].astype(F32)) + p_ref[2].astype(F32))
                          + p_ref[3].astype(F32))

    out = pl.pallas_call(
        body, name=name,
        grid_spec=pltpu.PrefetchScalarGridSpec(
            num_scalar_prefetch=1, grid=(1,),
            in_specs=[pl.BlockSpec(p.shape, lambda i, q: (0, 0, 0)) for p in parts] + [ANY] * len(after),
            out_specs=[pl.BlockSpec((1,) + p.shape[1:], lambda i, q: (q[1], 0, 0)) for p in parts]),
        out_shape=[jax.ShapeDtypeStruct((2,) + p.shape[1:], F32) for p in parts],
        compiler_params=_params(("arbitrary",)),
    )(pos, *parts, *after)
    return list(out)


def _join_descriptors(fulls, send_of, recv_of):
    x, y, c = _mesh_pos()

    def half(a, which):
        return functools.partial(
            pltpu.make_async_remote_copy,
            src_ref=fulls[a].at[which], dst_ref=fulls[a].at[which],
            send_sem=send_of(a), recv_sem=recv_of(a),
            device_id=(x, y, 1 - c), device_id_type=MESH)

    return [half(a, c) for a in range(len(fulls))], [half(a, 1 - c) for a in range(len(fulls))]


def _join_start(fulls, name, after=()):
    n = len(fulls)

    def body(*refs):
        sems, thru, token = refs[n:3 * n], refs[3 * n:4 * n], refs[4 * n]
        _sibling_handshake()
        sends, _ = _join_descriptors(thru, lambda a: sems[a], lambda a: sems[n + a])
        for cp in sends:
            cp().start()
        token[...] = jnp.zeros_like(token)

    held = [pltpu.with_memory_space_constraint(f, pltpu.HBM) for f in fulls]
    out = _tied_call(
        body, after, name=name,
        out_shape=(*[pltpu.SemaphoreType.DMA(())] * (2 * n), *[pltpu.HBM(f.shape, f.dtype) for f in held],
                   jax.ShapeDtypeStruct((8, LANES), F32)),
        in_specs=[HBM_SPEC] * n,
        out_specs=(*[SEM_SPEC] * (2 * n), *[HBM_SPEC] * n, pl.BlockSpec(memory_space=pltpu.VMEM)),
        input_output_aliases={i: 2 * n + i for i in range(n)},
        compiler_params=pltpu.CompilerParams(has_side_effects=pltpu.SideEffectType.DATAFLOW_SIDE_EFFECTING,
                                             collective_id=SIBLING_COLLECTIVE_ID),
    )(*held)
    return list(out[:2 * n]), list(out[2 * n:3 * n]), out[3 * n]


def _join_wait(sems, fulls, after, name):
    n = len(fulls)

    def body(*refs):
        sem_ref = refs[n:3 * n]
        sends, arrivals = _join_descriptors(refs[:n], lambda a: sem_ref[a], lambda a: sem_ref[n + a])
        for cp in sends:
            cp().wait_send()
        for cp in arrivals:
            cp().wait_recv()

    out = pl.pallas_call(
        body, name=name,
        out_shape=tuple(pltpu.HBM(f.shape, f.dtype) for f in fulls),
        in_specs=[HBM_SPEC] * n + [SEM_SPEC] * (2 * n) + [ANY] * len(_as_tuple(after)),
        out_specs=tuple([HBM_SPEC] * n),
        input_output_aliases={i: i for i in range(n)},
        compiler_params=pltpu.CompilerParams(has_side_effects=pltpu.SideEffectType.DATAFLOW_SIDE_EFFECTING),
    )(*fulls, *sems, *_as_tuple(after))
    return list(out)


def _join_halves(fulls, name, after=()):
    n = len(fulls)

    def body(*refs):
        send_sem, recv_sem = refs[2 * n:]
        _sibling_handshake()
        sends, arrivals = _join_descriptors(refs[n:2 * n], lambda a: send_sem.at[a], lambda a: recv_sem.at[a])
        sends = [cp() for cp in sends]
        for cp in sends:
            cp.start()
        for cp in arrivals:
            cp().wait_recv()
        for cp in sends:
            cp.wait_send()

    out_shape = [jax.ShapeDtypeStruct(f.shape, f.dtype) for f in fulls]
    return _tied_call(
        body, after, name=name,
        in_specs=[ANY] * n, out_specs=[ANY] * n, out_shape=out_shape,
        input_output_aliases={a: a for a in range(n)},
        scratch_shapes=[pltpu.SemaphoreType.DMA((n,))] * 2,
        compiler_params=pltpu.CompilerParams(collective_id=SIBLING_COLLECTIVE_ID),
    )(*fulls)


def _norm_in(x, g, ts, after=()):
    s = x.shape[0]

    def body(x_ref, g_ref, hn_ref):
        xv = x_ref[...]
        hn_ref[...] = (xv * _rms_stats(xv) * g_ref[...]).astype(BF16)

    row = pl.BlockSpec((ts, D_MODEL), lambda i: (i, 0))
    return _tied_call(
        body, after, name="norm_in", grid=(s // ts,),
        in_specs=[row, pl.BlockSpec((1, D_MODEL), lambda i: (0, 0))], out_specs=row,
        out_shape=jax.ShapeDtypeStruct((s, D_MODEL), BF16),
        compiler_params=_params(("parallel",)),
    )(x, g)


def _shift_rows(buf, shifted, t):
    rows = t + CONV_HALO - SUBLANES
    for r in range(1, SUBLANES):
        shifted[r - 1, 0:rows, :] = buf[pl.ds(r, rows), :]


def _window(buf, shifted, offset, t):
    r = offset % SUBLANES
    if r == 0:
        return buf[pl.ds(offset, t), :]
    return shifted[r - 1, pl.ds(offset - r, t), :]


def _lane_is_low_head():
    lane = lax.broadcasted_iota(jnp.int32, (1, GM_WIDTH), 1)
    return (lane & GM_HEAD_DIM) == 0


def _gm_mix(v_lo, v_hi, wpair_ref, bias_ref, mixed_ref, t):
    for n in range(t // CHUNK):
        rows = slice(n * CHUNK, (n + 1) * CHUNK)
        for j in range(GM_HEADS // 2):
            cols = slice(j * LANES, (j + 1) * LANES)
            rhs = jnp.concatenate([v_lo[rows, cols], v_hi[rows, cols]], axis=0)
            mixed_ref[rows, cols] = _dot(wpair_ref[j], rhs) + bias_ref[:, cols]


def _seqmix_fwd(hn, w_in, b_in, cw, cb, lng, lnb, gg, gb, wpair, bias, t, after=()):
    s = hn.shape[0]

    def body(hn_ref, w_ref, b_ref, cw_ref, cb_ref, lng_ref, lnb_ref, gg_ref, gb_ref, wpair_ref, bias_ref,
             z_ref, mix_ref, c1_ref, abuf, ash, mixed_ref):
        i = pl.program_id(0)

        @pl.when(i == 0)
        def _():
            abuf[0:CONV_HALO, :] = jnp.zeros((CONV_HALO, CONV_WIDTH), F32)

        @pl.when(i > 0)
        def _():
            abuf[0:CONV_HALO, :] = abuf[t:t + CONV_HALO, :]

        hv = hn_ref[...]
        for j in range(4):
            cols = slice(j * 512, (j + 1) * 512)
            z_ref[:, cols] = _dot(hv, w_ref[j]) + b_ref[:, cols]

        abuf[CONV_HALO:, :] = z_ref[:, 0:512] * _sigmoid(z_ref[:, 512:1024])
        _shift_rows(abuf, ash, t)
        acc = jnp.zeros((t, CONV_WIDTH), F32)
        for k in range(CONV_KERNEL):
            acc = acc + cw_ref[k:k + 1, :] * _window(abuf, ash, CONV_HALO - (CONV_KERNEL - 1) + k, t)
        c1 = acc + cb_ref[...]
        c1_ref[...] = c1
        xh, _ = _ln_stats(c1)
        ln = xh * lng_ref[...] + lnb_ref[...]
        mix_ref[:, 0:512] = (ln * _sigmoid(ln)).astype(BF16)

        u, _ = _gelu_parts(z_ref[:, 1024:1536])
        gv, _ = _gelu_parts(z_ref[:, 1536:2048])
        vxh, _ = _ln_stats(gv)
        v = vxh * gg_ref[...] + gb_ref[...]
        low = _lane_is_low_head()
        v_lo = jnp.where(low, v, 0.0).astype(BF16)
        v_hi = jnp.where(low, 0.0, v).astype(BF16)
        _gm_mix(v_lo, v_hi, wpair_ref, bias_ref, mixed_ref, t)
        mix_ref[:, 512:1024] = (u * mixed_ref[...]).astype(BF16)

    vec = lambda n: pl.BlockSpec((1, n), lambda i: (0, 0))
    return _tied_call(
        body, after, name="seqmix_fwd", grid=(s // t,),
        in_specs=[pl.BlockSpec((t, D_MODEL), lambda i: (i, 0)),
                  pl.BlockSpec((4, D_MODEL, 512), lambda i: (0, 0, 0)), vec(2048),
                  pl.BlockSpec((CONV_HALO, CONV_WIDTH), lambda i: (0, 0)),
                  vec(512), vec(512), vec(512), vec(512), vec(512),
                  pl.BlockSpec((4, CHUNK, 2 * CHUNK), lambda i: (0, 0, 0)),
                  pl.BlockSpec((CHUNK, GM_WIDTH), lambda i: (0, 0))],
        out_specs=[pl.BlockSpec((t, 2048), lambda i: (i, 0)),
                   pl.BlockSpec((t, D_MODEL), lambda i: (i, 0)),
                   pl.BlockSpec((t, CONV_WIDTH), lambda i: (i, 0))],
        out_shape=[jax.ShapeDtypeStruct((s, 2048), F32), jax.ShapeDtypeStruct((s, D_MODEL), BF16),
                   jax.ShapeDtypeStruct((s, CONV_WIDTH), F32)],
        scratch_shapes=[pltpu.VMEM((t + CONV_HALO, CONV_WIDTH), F32),
                        pltpu.VMEM((SUBLANES - 1, t + CONV_HALO - SUBLANES, CONV_WIDTH), F32),
                        pltpu.VMEM((t, GM_WIDTH), F32)],
        compiler_params=_params(("arbitrary",)),
    )(hn, w_in, b_in, cw, cb, lng, lnb, gg, gb, wpair, bias)


def _mem_kv(mem, g, wkv):
    m = mem.shape[0]

    def body(mem_ref, g_ref, w_ref, mn_ref, kv_ref):
        mv = mem_ref[...]
        mn = (mv * _rms_stats(mv) * g_ref[...]).astype(BF16)
        mn_ref[...] = mn
        for j in range(4):
            kv_ref[:, j * 512:(j + 1) * 512] = _dot(mn, w_ref[j]).astype(BF16)

    return pl.pallas_call(
        body, name="mem_kv",
        out_shape=[jax.ShapeDtypeStruct((m, D_MODEL), BF16), jax.ShapeDtypeStruct((m, 2 * D_MODEL), BF16)],
        compiler_params=pltpu.CompilerParams(vmem_limit_bytes=VMEM_LIMIT_BYTES),
    )(mem, g, wkv)


def _softmax_rows(sc):
    e = jnp.exp(sc - jnp.max(sc, axis=-1, keepdims=True))
    return e / jnp.sum(e, axis=-1, keepdims=True)


def _attn_block_fwd(x, mix, w_out, g_xa, wq, kv, wo, g_ffn, ts, after=()):
    s, m = x.shape[0], kv.shape[0]
    scale = XA_HEAD_DIM ** -0.5

    def body(x_ref, mix_ref, wout_ref, gxa_ref, wq_ref, kv_ref, wo_ref, gffn_ref,
             h1_ref, hn2_ref, q_ref, o_ref, h2_ref, hn3_ref):
        h1 = x_ref[...] + _dot(mix_ref[...], wout_ref[...])
        h1_ref[...] = h1
        hn2 = (h1 * _rms_stats(h1) * gxa_ref[...]).astype(BF16)
        hn2_ref[...] = hn2
        q_ref[...] = _dot(hn2, wq_ref[...]).astype(BF16)
        for h in range(XA_HEADS):
            cols = slice(h * XA_HEAD_DIM, (h + 1) * XA_HEAD_DIM)
            vcols = slice(D_MODEL + h * XA_HEAD_DIM, D_MODEL + (h + 1) * XA_HEAD_DIM)
            p = _softmax_rows(_dot_nt(q_ref[:, cols], kv_ref[:, cols]) * scale)
            o_ref[:, cols] = _dot(p.astype(BF16), kv_ref[:, vcols]).astype(BF16)
        h2 = h1 + _dot(o_ref[...], wo_ref[...])
        h2_ref[...] = h2
        hn3_ref[...] = (h2 * _rms_stats(h2) * gffn_ref[...]).astype(BF16)

    row = pl.BlockSpec((ts, D_MODEL), lambda i: (i, 0))
    full = pl.BlockSpec((D_MODEL, D_MODEL), lambda i: (0, 0))
    vec = pl.BlockSpec((1, D_MODEL), lambda i: (0, 0))
    f32 = jax.ShapeDtypeStruct((s, D_MODEL), F32)
    bf16 = jax.ShapeDtypeStruct((s, D_MODEL), BF16)
    return _tied_call(
        body, after, name="attn_block_fwd", grid=(s // ts,),
        in_specs=[row, row, full, vec, full, pl.BlockSpec((m, 2 * D_MODEL), lambda i: (0, 0)), full, vec],
        out_specs=[row] * 6,
        out_shape=[f32, bf16, bf16, bf16, f32, bf16],
        compiler_params=_params(("parallel",)),
    )(x, mix, w_out, g_xa, wq, kv, wo, g_ffn)


_FFN_CHUNKS = (slice(0, 8 * LANES), slice(8 * LANES, 16 * LANES), slice(16 * LANES, FFN_HIDDEN))


def _ffn_up(hn, wgu, ts, after=()):
    s = hn.shape[0]

    def body(hn_ref, w_ref, gu_ref, act_ref):
        hv = hn_ref[...]
        for cols in _FFN_CHUNKS:
            gate = _dot(hv, w_ref[0, :, cols])
            up = _dot(hv, w_ref[1, :, cols])
            gu_ref[0, :, cols] = gate.astype(BF16)
            gu_ref[1, :, cols] = up.astype(BF16)
            act_ref[:, cols] = (gate * _sigmoid(gate) * up).astype(BF16)

    return _tied_call(
        body, after, name="ffn_up", grid=(s // ts,),
        in_specs=[pl.BlockSpec((ts, D_MODEL), lambda i: (i, 0)),
                  pl.BlockSpec((2, D_MODEL, FFN_HIDDEN), lambda i: (0, 0, 0))],
        out_specs=[pl.BlockSpec((2, ts, FFN_HIDDEN), lambda i: (0, i, 0)),
                   pl.BlockSpec((ts, FFN_HIDDEN), lambda i: (i, 0))],
        out_shape=[jax.ShapeDtypeStruct((2, s, FFN_HIDDEN), BF16), jax.ShapeDtypeStruct((s, FFN_HIDDEN), BF16)],
        compiler_params=_params(("parallel",)),
    )(hn, wgu)


def _ffn_down_loss(act, wd, h2, g, target, ts):
    s = act.shape[0]

    def body(act_ref, wd_ref, h2_ref, g_ref, t_ref, dh_ref, dhb_ref, sq_ref, dg_ref):
        @pl.when(pl.program_id(0) == 0)
        def _():
            sq_ref[...] = jnp.zeros_like(sq_ref)
            dg_ref[...] = jnp.zeros_like(dg_ref)

        h3 = h2_ref[...] + _dot(act_ref[...], wd_ref[...])
        r = _rms_stats(h3)
        gv = g_ref[...]
        diff = h3 * r * gv - t_ref[...]
        sq_ref[...] += _rowsum(diff * diff)
        dh, dg = _rms_bwd(diff / D_MODEL, h3, r, gv)
        dh_ref[...] = dh
        dhb_ref[...] = dh.astype(BF16)
        dg_ref[...] += dg

    row = pl.BlockSpec((ts, D_MODEL), lambda i: (i, 0))
    vec = pl.BlockSpec((1, D_MODEL), lambda i: (0, 0))
    return pl.pallas_call(
        body, name="ffn_down_loss", grid=(s // ts,),
        in_specs=[pl.BlockSpec((ts, FFN_HIDDEN), lambda i: (i, 0)),
                  pl.BlockSpec((FFN_HIDDEN, D_MODEL), lambda i: (0, 0)), row, vec, row],
        out_specs=[row, row, vec, vec],
        out_shape=[jax.ShapeDtypeStruct((s, D_MODEL), F32), jax.ShapeDtypeStruct((s, D_MODEL), BF16),
                   jax.ShapeDtypeStruct((1, D_MODEL), F32), jax.ShapeDtypeStruct((1, D_MODEL), F32)],
        compiler_params=_params(("arbitrary",)),
    )(act, wd, h2, g, target)


def _grad_w(a, b, tk, tn, name, after=(), shards=1):
    s, k = a.shape
    gb, _, n = b.shape
    nblk = n // tn
    ws = tn // shards
    tsr = GRAD_ROWS if s % GRAD_ROWS == 0 else s

    def body(a_ref, b_ref, o_ref):
        part = _dot_tn(a_ref[...], b_ref[0])

        @pl.when(pl.program_id(2) == 0)
        def _():
            for j in range(shards):
                o_ref[j] = part[:, j * ws:(j + 1) * ws]

        @pl.when(pl.program_id(2) > 0)
        def _():
            for j in range(shards):
                o_ref[j] += part[:, j * ws:(j + 1) * ws]

    return _tied_call(
        body, after, name=name, grid=(gb * nblk, k // tk, s // tsr),
        in_specs=[pl.BlockSpec((tsr, tk), lambda ni, ki, si: (si, ki)),
                  pl.BlockSpec((1, tsr, tn), lambda ni, ki, si: (ni // nblk, si, ni % nblk))],
        out_specs=pl.BlockSpec((shards, tk, ws), lambda ni, ki, si: (ni, ki, 0)),
        out_shape=jax.ShapeDtypeStruct((gb * nblk * shards, k, ws), F32),
        compiler_params=_params(("parallel", "parallel", "arbitrary")),
    )(a, b)


def _grad_w_square(pairs, name, after=()):
    n = len(pairs)
    s = pairs[0][0].shape[0]
    tsr = GRAD_ROWS // 2 if s % (GRAD_ROWS // 2) == 0 else s

    def body(*refs):
        ins, outs = refs[:2 * n], refs[2 * n:]
        parts = [_dot_tn(ins[2 * a][...], ins[2 * a + 1][...]) for a in range(n)]

        @pl.when(pl.program_id(0) == 0)
        def _():
            for a in range(n):
                outs[a][...] = parts[a]

        @pl.when(pl.program_id(0) > 0)
        def _():
            for a in range(n):
                outs[a][...] += parts[a]

    row = pl.BlockSpec((tsr, D_MODEL), lambda i: (i, 0))
    return _tied_call(
        body, after, name=name, grid=(s // tsr,),
        in_specs=[row] * (2 * n), out_specs=[pl.BlockSpec((D_MODEL, D_MODEL), lambda i: (0, 0))] * n,
        out_shape=[jax.ShapeDtypeStruct((D_MODEL, D_MODEL), F32)] * n,
        compiler_params=_params(("arbitrary",)),
    )(*[x for p in pairs for x in p])


def _ffn_bwd(dh3, wd, gu, wgu, h2, g, t, after=()):
    s = dh3.shape[0]

    def body(dh3_ref, wd_ref, gu_ref, w_ref, h2_ref, g_ref, dgu_ref, dh2_ref, dh2b_ref, dg_ref):
        @pl.when(pl.program_id(0) == 0)
        def _():
            dg_ref[...] = jnp.zeros_like(dg_ref)

        dh3v = dh3_ref[...]
        dhb = dh3v.astype(BF16)
        for cols in _FFN_CHUNKS:
            dact = _dot_nt(dhb, wd_ref[cols, :])
            gate, up = gu_ref[0, :, cols].astype(F32), gu_ref[1, :, cols].astype(F32)
            sg = _sigmoid(gate)
            dgu_ref[0, :, cols] = (dact * up * (sg * (1.0 + gate * (1.0 - sg)))).astype(BF16)
            dgu_ref[1, :, cols] = (dact * (gate * sg)).astype(BF16)
        dhn = _dot_nt(dgu_ref[0], w_ref[0]) + _dot_nt(dgu_ref[1], w_ref[1])
        h2 = h2_ref[...]
        dv, dg = _rms_bwd(dhn, h2, _rms_stats(h2), g_ref[...])
        dh2 = dh3v + dv
        dh2_ref[...] = dh2
        dh2b_ref[...] = dh2.astype(BF16)
        dg_ref[...] += dg

    row = pl.BlockSpec((t, D_MODEL), lambda i: (i, 0))
    wide = pl.BlockSpec((2, t, FFN_HIDDEN), lambda i: (0, i, 0))
    vec = pl.BlockSpec((1, D_MODEL), lambda i: (0, 0))
    return _tied_call(
        body, after, name="ffn_bwd", grid=(s // t,),
        in_specs=[row, pl.BlockSpec((FFN_HIDDEN, D_MODEL), lambda i: (0, 0)), wide,
                  pl.BlockSpec((2, D_MODEL, FFN_HIDDEN), lambda i: (0, 0, 0)), row, vec],
        out_specs=[wide, row, row, vec],
        out_shape=[jax.ShapeDtypeStruct((2, s, FFN_HIDDEN), BF16), jax.ShapeDtypeStruct((s, D_MODEL), F32),
                   jax.ShapeDtypeStruct((s, D_MODEL), BF16), jax.ShapeDtypeStruct((1, D_MODEL), F32)],
        compiler_params=_params(("arbitrary",)),
    )(dh3, wd, gu, wgu, h2, g)


def _attn_bwd(dh2, wo, q, kv, wq, h1, g, ts, after=()):
    s, m = q.shape[0], kv.shape[0]
    scale = XA_HEAD_DIM ** -0.5

    def body(dh2_ref, wo_ref, q_ref, kv_ref, wq_ref, h1_ref, g_ref, dh1_ref, dh1b_ref, dq_ref, dkv_ref, dg_ref):
        @pl.when(pl.program_id(0) == 0)
        def _():
            dkv_ref[...] = jnp.zeros_like(dkv_ref)
            dg_ref[...] = jnp.zeros_like(dg_ref)

        do = _dot_nt(dh2_ref[...].astype(BF16), wo_ref[...]).astype(BF16)
        for h in range(XA_HEADS):
            cols = slice(h * XA_HEAD_DIM, (h + 1) * XA_HEAD_DIM)
            vcols = slice(D_MODEL + h * XA_HEAD_DIM, D_MODEL + (h + 1) * XA_HEAD_DIM)
            qh, kh, vh, doh = q_ref[:, cols], kv_ref[:, cols], kv_ref[:, vcols], do[:, cols]
            p = _softmax_rows(_dot_nt(qh, kh) * scale)
            dp = _dot_nt(doh, vh)
            ds = (p * (dp - jnp.sum(dp * p, axis=-1, keepdims=True)) * scale).astype(BF16)
            dq_ref[:, cols] = _dot(ds, kh).astype(BF16)
            dkv_ref[:, cols] += _dot_tn(ds, qh)
            dkv_ref[:, vcols] += _dot_tn(p.astype(BF16), doh)
        dhn = _dot_nt(dq_ref[...], wq_ref[...])
        h1 = h1_ref[...]
        dv, dg = _rms_bwd(dhn, h1, _rms_stats(h1), g_ref[...])
        dh1 = dh2_ref[...] + dv
        dh1_ref[...] = dh1
        dh1b_ref[...] = dh1.astype(BF16)
        dg_ref[...] += dg

    row = pl.BlockSpec((ts, D_MODEL), lambda i: (i, 0))
    full = pl.BlockSpec((D_MODEL, D_MODEL), lambda i: (0, 0))
    kvs = pl.BlockSpec((m, 2 * D_MODEL), lambda i: (0, 0))
    vec = pl.BlockSpec((1, D_MODEL), lambda i: (0, 0))
    return _tied_call(
        body, after, name="attn_bwd", grid=(s // ts,),
        in_specs=[row, full, row, kvs, full, row, vec],
        out_specs=[row, row, row, kvs, vec],
        out_shape=[jax.ShapeDtypeStruct((s, D_MODEL), F32), jax.ShapeDtypeStruct((s, D_MODEL), BF16),
                   jax.ShapeDtypeStruct((s, D_MODEL), BF16),
                   jax.ShapeDtypeStruct((m, 2 * D_MODEL), F32), jax.ShapeDtypeStruct((1, D_MODEL), F32)],
        compiler_params=_params(("arbitrary",)),
    )(dh2, wo, q, kv, wq, h1, g)


def _mem_kv_bwd(dkv, mn, wkv, mem, g, after=()):
    m = mem.shape[0]

    def body(dkv_ref, mn_ref, w_ref, mem_ref, g_ref, dw_ref, dg_ref):
        dmn = jnp.zeros((m, D_MODEL), F32)
        mn = mn_ref[...]
        for j in range(4):
            dj = dkv_ref[:, j * 512:(j + 1) * 512].astype(BF16)
            dw_ref[j] = _dot_tn(mn, dj)
            dmn = dmn + _dot_nt(dj, w_ref[j])
        mv = mem_ref[...]
        dg_ref[...] = _rowsum(dmn * (mv * _rms_stats(mv)))

    return _tied_call(
        body, after, name="mem_kv_bwd", in_specs=[pl.BlockSpec(memory_space=pltpu.VMEM)] * 5,
        out_shape=[jax.ShapeDtypeStruct((4, D_MODEL, 512), F32), jax.ShapeDtypeStruct((1, D_MODEL), F32)],
        compiler_params=pltpu.CompilerParams(vmem_limit_bytes=VMEM_LIMIT_BYTES),
    )(dkv, mn, wkv, mem, g)


def _seqmix_bwd(dh1, x, z, c1, w_out, w_in, g_mix, cw, lng, lnb, gg, gb, wpair, wpair_t, bias, t, after=()):
    s = x.shape[0]
    nt = s // t

    def body(dh1_ref, x_ref, z_ref, c1_ref, wo_ref, wi_ref, gm_ref, cw_ref, lng_ref, lnb_ref,
             gg_ref, gb_ref, wpair_ref, wpt_ref, bias_ref,
             gx_ref, dz_ref, dcw_ref, dcb_ref, dlng_ref, dlnb_ref, dgg_ref, dgb_ref, dws_ref, dbs_ref,
             dbin_ref, dgm_ref, dbuf, dsh, mixed_ref, dv_ref):
        i = pl.program_id(0)
        accs = (dcw_ref, dcb_ref, dlng_ref, dlnb_ref, dgg_ref, dgb_ref, dws_ref, dbs_ref, dbin_ref, dgm_ref)

        @pl.when(i == 0)
        def _():
            for r in accs:
                r[...] = jnp.zeros_like(r)
            dbuf[t:t + CONV_HALO, :] = jnp.zeros((CONV_HALO, CONV_WIDTH), F32)

        @pl.when(i > 0)
        def _():
            dbuf[t:t + CONV_HALO, :] = dbuf[0:CONV_HALO, :]

        dmix = _dot_nt(dh1_ref[...].astype(BF16), wo_ref[...])

        xh, rs = _ln_stats(c1_ref[...])
        lng = lng_ref[...]
        ln = xh * lng + lnb_ref[...]
        sl = _sigmoid(ln)
        dln = dmix[:, 0:512] * (sl * (1.0 + ln * (1.0 - sl)))
        dc1, dg_ln, db_ln = _ln_bwd(dln, xh, rs, lng)
        dlng_ref[...] += dg_ln
        dlnb_ref[...] += db_ln
        dcb_ref[...] += _rowsum(dc1)
        dbuf[0:t, :] = dc1

        za = z_ref[:, 0:512]
        sg = _sigmoid(z_ref[:, 512:1024])
        a = za * sg
        _shift_rows(dbuf, dsh, t)

        da = jnp.zeros((t, CONV_WIDTH), F32)
        for k in range(CONV_KERNEL):
            later = _window(dbuf, dsh, CONV_KERNEL - 1 - k, t)
            da = da + cw_ref[k:k + 1, :] * later
            dcw_ref[k:k + 1, :] += _rowsum(a * later)
        dza = da * sg
        dzg = da * za * (sg * (1.0 - sg))
        dz_ref[:, 0:512] = dza.astype(BF16)
        dz_ref[:, 512:1024] = dzg.astype(BF16)
        dbin_ref[:, 0:512] += _rowsum(dza)
        dbin_ref[:, 512:1024] += _rowsum(dzg)

        dgm = dmix[:, 512:1024]
        u, du_dz = _gelu_parts(z_ref[:, 1024:1536])
        gv, dgv_dz = _gelu_parts(z_ref[:, 1536:2048])
        vxh, vrs = _ln_stats(gv)
        ggv = gg_ref[...]
        v = vxh * ggv + gb_ref[...]
        low = _lane_is_low_head()
        v_lo = jnp.where(low, v, 0.0).astype(BF16)
        v_hi = jnp.where(low, 0.0, v).astype(BF16)
        _gm_mix(v_lo, v_hi, wpair_ref, bias_ref, mixed_ref, t)
        dzu = dgm * mixed_ref[...] * du_dz
        dm = dgm * u
        dm_lo = jnp.where(low, dm, 0.0).astype(BF16)
        dm_hi = jnp.where(low, 0.0, dm).astype(BF16)
        vb = v.astype(BF16)
        tril = (lax.broadcasted_iota(jnp.int32, (CHUNK, CHUNK), 1)
                <= lax.broadcasted_iota(jnp.int32, (CHUNK, CHUNK), 0))
        for n in range(t // CHUNK):
            rows = slice(n * CHUNK, (n + 1) * CHUNK)
            dbs_ref[...] += dm[rows, :]
            for j in range(GM_HEADS // 2):
                cols = slice(j * LANES, (j + 1) * LANES)
                stack = jnp.concatenate([dm_lo[rows, cols], dm_hi[rows, cols]], axis=0)
                dws = _dot_nt(stack, vb[rows, cols])
                dws_ref[2 * j] += jnp.where(tril, dws[0:CHUNK], 0.0)
                dws_ref[2 * j + 1] += jnp.where(tril, dws[CHUNK:2 * CHUNK], 0.0)
                dv_ref[rows, cols] = _dot(wpt_ref[j], stack)
        dgv, dg_gm, db_gm = _ln_bwd(dv_ref[...], vxh, vrs, ggv)
        dgg_ref[...] += dg_gm
        dgb_ref[...] += db_gm
        dzv = dgv * dgv_dz
        dz_ref[:, 1024:1536] = dzu.astype(BF16)
        dz_ref[:, 1536:2048] = dzv.astype(BF16)
        dbin_ref[:, 1024:1536] += _rowsum(dzu)
        dbin_ref[:, 1536:2048] += _rowsum(dzv)

        dhn = jnp.zeros((t, D_MODEL), F32)
        for j in range(4):
            dhn = dhn + _dot_nt(dz_ref[:, j * 512:(j + 1) * 512], wi_ref[j])
        xv = x_ref[...]
        dv, dg = _rms_bwd(dhn, xv, _rms_stats(xv), gm_ref[...])
        gx_ref[...] = dh1_ref[...] + dv
        dgm_ref[...] += dg

    rev = lambda w: pl.BlockSpec((t, w), lambda i: (nt - 1 - i, 0))
    const = lambda *shape: pl.BlockSpec(shape, lambda i: (0,) * len(shape))
    f32 = lambda *shape: jax.ShapeDtypeStruct(shape, F32)
    return _tied_call(
        body, after, name="seqmix_bwd", grid=(nt,),
        in_specs=[rev(D_MODEL), rev(D_MODEL), rev(2048), rev(CONV_WIDTH),
                  const(D_MODEL, D_MODEL), const(4, D_MODEL, 512), const(1, D_MODEL),
                  const(CONV_HALO, CONV_WIDTH), const(1, 512), const(1, 512), const(1, 512), const(1, 512),
                  const(4, CHUNK, 2 * CHUNK), const(4, CHUNK, 2 * CHUNK), const(CHUNK, GM_WIDTH)],
        out_specs=[rev(D_MODEL), rev(2048),
                   const(CONV_HALO, CONV_WIDTH), const(1, 512), const(1, 512), const(1, 512), const(1, 512),
                   const(1, 512), const(GM_HEADS, CHUNK, CHUNK), const(CHUNK, GM_WIDTH), const(1, 2048),
                   const(1, D_MODEL)],
        out_shape=[f32(s, D_MODEL), jax.ShapeDtypeStruct((s, 2048), BF16),
                   f32(CONV_HALO, CONV_WIDTH), f32(1, 512), f32(1, 512), f32(1, 512), f32(1, 512),
                   f32(1, 512), f32(GM_HEADS, CHUNK, CHUNK), f32(CHUNK, GM_WIDTH), f32(1, 2048),
                   f32(1, D_MODEL)],
        scratch_shapes=[pltpu.VMEM((t + CONV_HALO, CONV_WIDTH), F32),
                        pltpu.VMEM((SUBLANES - 1, t + CONV_HALO - SUBLANES, CONV_WIDTH), F32),
                        pltpu.VMEM((t, GM_WIDTH), F32), pltpu.VMEM((t, GM_WIDTH), F32)],
        compiler_params=_params(("arbitrary",)),
    )(dh1, x, z, c1, w_out, w_in, g_mix, cw, lng, lnb, gg, gb, wpair, wpair_t, bias)


def _head_bias_grad(dbs):
    def body(d_ref, o_ref):
        dv = d_ref[...]
        lane = lax.broadcasted_iota(jnp.int32, (CHUNK, LANES), 1)
        acc = jnp.zeros((CHUNK, LANES), F32)
        for h in range(GM_HEADS):
            sh = jnp.sum(dv[:, h * GM_HEAD_DIM:(h + 1) * GM_HEAD_DIM], axis=-1, keepdims=True)
            acc = acc + jnp.where(lane == h, sh, 0.0)
        o_ref[...] = acc

    return pl.pallas_call(body, name="head_bias_grad",
                          out_shape=jax.ShapeDtypeStruct((CHUNK, LANES), F32))(dbs)


def kernel(x, mem, norm_mix_g, w_in, b_in, conv_w, conv_b, conv_ln_g, conv_ln_b, gm_ln_g, gm_ln_b, gm_w_s, gm_b_s, w_out, norm_xa_g, mem_norm_g, xa_wq, xa_wkv, xa_wo, norm_ffn_g, ffn_w_gate_up, ffn_w_down, final_norm_g, loss_target, m_norm_mix_g, m_w_in, m_b_in, m_conv_w, m_conv_b, m_conv_ln_g, m_conv_ln_b, m_gm_ln_g, m_gm_ln_b, m_gm_w_s, m_gm_b_s, m_w_out, m_norm_xa_g, m_mem_norm_g, m_xa_wq, m_xa_wkv, m_xa_wo, m_norm_ffn_g, m_ffn_w_gate_up, m_ffn_w_down, m_final_norm_g, v_norm_mix_g, v_w_in, v_b_in, v_conv_w, v_conv_b, v_conv_ln_g, v_conv_ln_b, v_gm_ln_g, v_gm_ln_b, v_gm_w_s, v_gm_b_s, v_w_out, v_norm_xa_g, v_mem_norm_g, v_xa_wq, v_xa_wkv, v_xa_wo, v_norm_ffn_g, v_ffn_w_gate_up, v_ffn_w_down, v_final_norm_g):
    weights = dict(norm_mix_g=norm_mix_g, w_in=w_in, b_in=b_in, conv_w=conv_w, conv_b=conv_b, conv_ln_g=conv_ln_g,
                   conv_ln_b=conv_ln_b, gm_ln_g=gm_ln_g, gm_ln_b=gm_ln_b, gm_w_s=gm_w_s, gm_b_s=gm_b_s, w_out=w_out,
                   norm_xa_g=norm_xa_g, mem_norm_g=mem_norm_g, xa_wq=xa_wq, xa_wkv=xa_wkv, xa_wo=xa_wo,
                   norm_ffn_g=norm_ffn_g, ffn_w_gate_up=ffn_w_gate_up, ffn_w_down=ffn_w_down,
                   final_norm_g=final_norm_g)
    m_in = dict(norm_mix_g=m_norm_mix_g, w_in=m_w_in, b_in=m_b_in, conv_w=m_conv_w, conv_b=m_conv_b,
                conv_ln_g=m_conv_ln_g, conv_ln_b=m_conv_ln_b, gm_ln_g=m_gm_ln_g, gm_ln_b=m_gm_ln_b, gm_w_s=m_gm_w_s,
                gm_b_s=m_gm_b_s, w_out=m_w_out, norm_xa_g=m_norm_xa_g, mem_norm_g=m_mem_norm_g, xa_wq=m_xa_wq,
                xa_wkv=m_xa_wkv, xa_wo=m_xa_wo, norm_ffn_g=m_norm_ffn_g, ffn_w_gate_up=m_ffn_w_gate_up,
                ffn_w_down=m_ffn_w_down, final_norm_g=m_final_norm_g)
    v_in = dict(norm_mix_g=v_norm_mix_g, w_in=v_w_in, b_in=v_b_in, conv_w=v_conv_w, conv_b=v_conv_b,
                conv_ln_g=v_conv_ln_g, conv_ln_b=v_conv_ln_b, gm_ln_g=v_gm_ln_g, gm_ln_b=v_gm_ln_b, gm_w_s=v_gm_w_s,
                gm_b_s=v_gm_b_s, w_out=v_w_out, norm_xa_g=v_norm_xa_g, mem_norm_g=v_mem_norm_g, xa_wq=v_xa_wq,
                xa_wkv=v_xa_wkv, xa_wo=v_xa_wo, norm_ffn_g=v_norm_ffn_g, ffn_w_gate_up=v_ffn_w_gate_up,
                ffn_w_down=v_ffn_w_down, final_norm_g=v_final_norm_g)
    grads, delta, new_m, new_v = {}, {}, {}, {}

    s = x.shape[1]
    ts = _row_tile(s)
    tb = max(CHUNK, ts // 2)
    tw = 2 * ts if s % (2 * ts) == 0 and ts >= 512 else ts
    cx, cy, cc = _mesh_pos()
    chip = 2 * cx + cy
    pos = jnp.stack([chip, cc]).astype(jnp.int32)
    row = lambda a: a.reshape(1, -1)
    x2, mem2, tgt2 = x[0], mem[0], loss_target[0]

    big = dict(w_in=w_in, xa_wkv=xa_wkv, w_out=w_out, xa_wq=xa_wq, xa_wo=xa_wo,
               ffn_w_gate_up=ffn_w_gate_up, ffn_w_down=ffn_w_down)
    big_names = list(big)
    halves = lambda a: a.reshape(2, a.shape[0] // 2, a.shape[1])
    conv_w_pad = jnp.pad(conv_w, ((0, CONV_HALO - CONV_KERNEL), (0, 0)))
    first_names = ["w_in", "conv_w"]
    later_names = [nm for nm in big_names if nm != "w_in"]
    cast = dict(zip(first_names, _cast_into_slots([halves(w_in), halves(conv_w_pad)], pos, [BF16, F32], "cast_w_in")))
    cast.update(zip(later_names, _cast_into_slots([halves(big[nm]) for nm in later_names], pos,
                                                  [BF16] * len(later_names), "cast_" + later_names[0],
                                                  side_by_side=(later_names.index("ffn_w_gate_up"),))))

    def start_gather(names, after):
        return _gather_start([cast[nm] for nm in names], "gather_start_" + names[0], after)

    def land_gather(names, started, after):
        send_sems, recv_sems, bufs, _ = started
        return _gather_wait(send_sems, recv_sems, bufs, after, "gather_wait_" + names[0])

    as_weights = lambda names, bufs: dict(zip(names, (b.reshape(b.shape[0], -1, b.shape[-1]) for b in bufs)))

    def start_share(names, landed):
        return _pass_start(landed, "pass_start_" + names[0])

    def share_gather(names, passing, which, after):
        send_sems, recv_sems, bufs, _ = passing
        sems = lambda s: [s[3 * a + k] for a in which for k in range(3)]
        picked = [names[a] for a in which]
        return as_weights(picked, _gather_wait(sems(send_sems), sems(recv_sems), [bufs[a] for a in which], after,
                                               "pass_wait_" + picked[0], _pass_descriptors))

    tril = jnp.tril(jnp.ones((CHUNK, CHUNK), dtype=bool))
    ws = jnp.where(tril[None], gm_w_s, 0.0)
    wpair = jnp.concatenate([ws[0::2], ws[1::2]], axis=2).astype(BF16)
    ws_t = jnp.swapaxes(ws, 1, 2)
    wpair_t = jnp.concatenate([ws_t[0::2], ws_t[1::2]], axis=2).astype(BF16)
    bias = jnp.repeat(gm_b_s.T, GM_HEAD_DIM, axis=1)

    attn_names = ["xa_wkv", "w_out", "xa_wq", "xa_wo"]
    gather_first = start_gather(first_names, ())
    hn1 = _norm_in(x2, row(norm_mix_g), tw, after=(gather_first[3], wpair, wpair_t, bias))
    landed = land_gather(first_names, gather_first, [cast[nm] for nm in later_names] + [hn1])
    passing = start_share(first_names, landed)
    gather_attn = start_gather(attn_names, passing[3])
    gw = share_gather(first_names, passing, (0, 1), gather_attn[3])
    w_in_g = gw["w_in"]
    cw_g = jnp.concatenate([gw["conv_w"][k] for k in range(N_CHIPS)], axis=1)

    z, mix, c1 = _seqmix_fwd(hn1, w_in_g, row(b_in), cw_g, row(conv_b), row(conv_ln_g), row(conv_ln_b),
                             row(gm_ln_g), row(gm_ln_b), wpair, bias, ts)
    landed = land_gather(attn_names, gather_attn, mix)
    passing = start_share(attn_names, landed)
    gather_gu = start_gather(["ffn_w_gate_up"], passing[3])
    wkv_g = share_gather(attn_names, passing, (0,), gather_gu[3])["xa_wkv"]
    mn, kv = _mem_kv(mem2, row(mem_norm_g), wkv_g)
    gw = share_gather(attn_names, passing, (1, 2, 3), kv)
    w_out_g = gw["w_out"].reshape(D_MODEL, D_MODEL)
    wq_g = gw["xa_wq"].reshape(D_MODEL, D_MODEL)
    wo_g = gw["xa_wo"].reshape(D_MODEL, D_MODEL)
    h1, hn2, q, o, h2, hn3 = _attn_block_fwd(x2, mix, w_out_g, row(norm_xa_g), wq_g, kv, wo_g, row(norm_ffn_g), ts)
    landed = land_gather(["ffn_w_gate_up"], gather_gu, hn3)
    passing = start_share(["ffn_w_gate_up"], landed)
    gather_down = start_gather(["ffn_w_down"], passing[3])
    wgu_g = share_gather(["ffn_w_gate_up"], passing, (0,), gather_down[3])["ffn_w_gate_up"]
    gu, act = _ffn_up(hn3, wgu_g, ts)
    landed = land_gather(["ffn_w_down"], gather_down, act)
    wd_g = as_weights(["ffn_w_down"], _pass_to_sibling(landed, "pass_ffn_w_down"))["ffn_w_down"].reshape(
        FFN_HIDDEN, D_MODEL)
    dh3, dh3_b, sq, d_final_g = _ffn_down_loss(act, wd_g, h2, row(final_norm_g), tgt2, ts)
    loss_here = jnp.broadcast_to(0.5 * jnp.sum(sq) / D_MODEL, (1, 2, SUBLANES, LANES))

    def split(g, nm):
        r, c = big[nm].shape
        return g.reshape(N_CHIPS, 2, r // 2, c)

    def chip_sums(group, arrays, got):
        sums, parts = [None] * len(group), [None] * len(group)
        for blocks in (N_CHIPS, 1):
            idx = [i for i, a in enumerate(arrays) if a.shape[0] == blocks]
            if idx:
                out = _add_halves([arrays[i] for i in idx], [got[i] for i in idx], pos, "chip_sum_" + group[idx[0]],
                                  [F32 if group[i] == "loss" else BF16 for i in idx])
                for k, i in enumerate(idx):
                    sums[i], parts[i] = out[0][k], out[1][k]
        return sums, parts

    def start_swap(group, grads, after=()):
        return _swap_start([split(g, nm) for g, nm in zip(grads, group)], "swap_start_" + group[0], after)

    def start_exchange(group, swapping, after, landed):
        arrays, got = _swap_wait(*swapping[:3], after, "swap_wait_" + group[0])
        sums, parts = chip_sums(group, arrays, got)
        return _exchange_start(sums, parts, "exchange_start_" + group[0], landed)

    def wait_exchange(group, started, after):
        sems, sums, parts, _ = started
        return _exchange_wait(sems, sums, parts, after, "exchange_wait_" + group[0])

    def finish_exchange(group, started, after):
        return _sum_chips(wait_exchange(group, started, after), pos, "total_" + group[0])

    def join(group, after):
        return _join_halves([halves_of[nm] for nm in group], "join_halves_" + group[0], after)

    def start_join(group, after):
        return _join_start([halves_of[nm] for nm in group], "join_start_" + group[0], after)

    def end_join(group, joining, after):
        return _join_wait(*joining[:2], after, "join_wait_" + group[0])

    def update(group, joined, after=()):
        outs = _adamw([(weights[nm], j.reshape(big[nm].shape), m_in[nm], v_in[nm]) for nm, j in zip(group, joined)],
                      "adamw_" + group[0], after)
        for nm, out in zip(group, outs):
            grads[nm], delta[nm], new_m[nm], new_v[nm] = out
        return [new_v[nm] for nm in group]

    def join_and_update(group, after):
        return update(group, join(group, after))

    as3 = lambda a: a.reshape((1,) + a.shape)
    halves_of = {}

    g_down = _grad_w(act, as3(dh3_b), FFN_HALF, D_MODEL, "grad_ffn_w_down")
    group_a = ["ffn_w_down"]
    swap_a = start_swap(group_a, [g_down])
    dgu, dh2, dh2_b, d_ffn_g = _ffn_bwd(dh3, wd_g, gu, wgu_g, h2, row(norm_ffn_g), tb,
                                        after=swap_a[3])
    exch_a = start_exchange(group_a, swap_a, dh2, wd_g)
    g_gu = _grad_w(hn3, dgu, D_MODEL, FFN_HALF, "grad_ffn_w_gate_up", after=exch_a[3])
    halves_of.update(zip(group_a, finish_exchange(group_a, exch_a, g_gu)))

    group_b = ["ffn_w_gate_up"]
    swap_b = start_swap(group_b, [g_gu])
    dh1, dh1_b, dq, dkv, d_xa_g = _attn_bwd(dh2, wo_g, q, kv, wq_g, h1, row(norm_xa_g), ts, after=swap_b[3])
    exch_b = start_exchange(group_b, swap_b, dh1, [halves_of[nm] for nm in group_a])
    joining_a = start_join(group_a, exch_b[3])
    g_wkv, d_mem_g = _mem_kv_bwd(dkv, mn, wkv_g, mem2, row(mem_norm_g), after=joining_a[2])
    g_wo, g_wq, g_wout = _grad_w_square([(o, dh2_b), (hn2, dq), (mix, dh1_b)], "grad_xa_wo", after=joining_a[2])
    done_a = update(group_a, end_join(group_a, joining_a, (g_wkv, g_wo, g_wq, g_wout)))
    halves_of.update(zip(group_b, finish_exchange(group_b, exch_b, done_a)))
    joining_b = start_join(group_b, done_a)
    group_c = ["xa_wo", "xa_wq", "xa_wkv", "w_out"]
    swap_c = start_swap(group_c, [g_wo, g_wq, g_wkv, g_wout], joining_b[2])
    (gx, dz, d_cw, d_cb, d_lng, d_lnb, d_gg, d_gb, d_ws, d_bs_sum, d_bin, d_mix_g) = _seqmix_bwd(
        dh1, x2, z, c1, w_out_g, w_in_g, row(norm_mix_g), cw_g, row(conv_ln_g), row(conv_ln_b),
        row(gm_ln_g), row(gm_ln_b), wpair, wpair_t, bias, tb, after=swap_c[3])
    d_bs = _head_bias_grad(d_bs_sum)[:, :GM_HEADS].T
    exch_c = start_exchange(group_c, swap_c, dz, joining_b[2])
    g_win = _grad_w(hn1, as3(dz), D_MODEL, 1024, "grad_w_in", after=exch_c[3], shards=2)

    small_names = ["norm_mix_g", "b_in", "conv_w", "conv_b", "conv_ln_g", "conv_ln_b", "gm_ln_g", "gm_ln_b",
                   "gm_w_s", "gm_b_s", "norm_xa_g", "mem_norm_g", "norm_ffn_g", "final_norm_g"]
    d_cw_by_chip = jnp.swapaxes(d_cw.reshape(CONV_HALO, N_CHIPS, LANES), 0, 1).reshape(-1, LANES)
    small_grads = dict(norm_mix_g=d_mix_g, b_in=d_bin, conv_w=d_cw_by_chip, conv_b=d_cb, conv_ln_g=d_lng,
                       conv_ln_b=d_lnb, gm_ln_g=d_gg, gm_ln_b=d_gb, gm_w_s=d_ws, gm_b_s=d_bs, norm_xa_g=d_xa_g,
                       mem_norm_g=d_mem_g, norm_ffn_g=d_ffn_g, final_norm_g=d_final_g)

    def rows_form(a):
        a = a.reshape(-1, LANES)
        return jnp.pad(a, ((0, -a.shape[0] % SUBLANES), (0, 0)))

    pieces = [rows_form(small_grads[nm]) for nm in small_names]
    offsets, total = [], 0
    for p in pieces:
        offsets.append(total)
        total += p.shape[0]
    pack_rows = -(-total // 32) * 32
    small_pack = jnp.pad(jnp.concatenate(pieces, axis=0), ((0, pack_rows - total), (0, 0)))

    group_d = ["w_in", "small", "loss"]
    joined_b = end_join(group_b, joining_b, g_win)
    swap_d = _swap_start([split(g_win, "w_in")], "swap_start_w_in", joined_b)
    done_b = update(group_b, joined_b, swap_d[3])
    small_d = [small_pack.reshape(1, 2, pack_rows // 2, LANES), loss_here]
    got_small = _swap_halves(small_d, "swap_halves_small", done_b)
    arrays_d, got_d = _swap_wait(*swap_d[:3], got_small, "swap_wait_w_in")
    sums_d, parts_d = chip_sums(group_d, arrays_d + small_d, got_d + list(got_small))
    parts_c = wait_exchange(group_c, exch_c, sums_d)
    exch_d = _exchange_start(sums_d, parts_d, "exchange_start_w_in", parts_c)
    halves_of.update(zip(group_c, _sum_chips(parts_c, pos, "total_xa_wo", exch_d[3])))
    done_c = join_and_update(group_c, exch_d[3])
    halves_of.update(zip(group_d, finish_exchange(group_d, exch_d, done_c)))
    joined_d = _join_halves([halves_of[nm] for nm in group_d], "join_halves_w_in")
    loss = joined_d[2][0, 0, 0]
    grads["w_in"], delta["w_in"], new_m["w_in"], new_v["w_in"] = _adamw(
        [(w_in, joined_d[0].reshape(w_in.shape), m_w_in, v_w_in)], "adamw_w_in")[0]

    local_rows = lambda a, nm: a if nm == "conv_w" else a.reshape(-1, LANES)
    params = [tuple(local_rows(src[nm], nm) for src in (weights, m_in, v_in)) for nm in small_names]
    outs = _adamw_small(joined_d[1].reshape(pack_rows, LANES), pos, params, offsets, small_names.index("conv_w"))
    for k, nm in enumerate(small_names):
        for dst, a in zip((grads, delta, new_m, new_v), outs[4 * k:4 * k + 4]):
            dst[nm] = a

    order = ["norm_mix_g", "w_in", "b_in", "conv_w", "conv_b", "conv_ln_g", "conv_ln_b", "gm_ln_g", "gm_ln_b",
             "gm_w_s", "gm_b_s", "w_out", "norm_xa_g", "mem_norm_g", "xa_wq", "xa_wkv", "xa_wo", "norm_ffn_g",
             "ffn_w_gate_up", "ffn_w_down", "final_norm_g"]
    fit = lambda a, nm: a.reshape(weights[nm].shape)
    return (loss, gx.reshape(x.shape),
            *[fit(grads[nm], nm) for nm in order], *[fit(delta[nm], nm) for nm in order],
            *[fit(new_m[nm], nm) for nm in order], *[fit(new_v[nm], nm) for nm in order])
```

```python
import functools

import jax
import jax.numpy as jnp
from jax import lax
from jax.experimental import pallas as pl
from jax.experimental.pallas import tpu as pltpu

F32 = jnp.float32
BF16 = jnp.bfloat16

D_MODEL = 1024
CONV_WIDTH = 512
GM_WIDTH = 512
CONV_KERNEL = 31
CONV_HALO = 32
GRAD_ROWS = 2048
CHUNK = 128
GM_HEADS = 8
GM_HEAD_DIM = 64
XA_HEADS = 4
XA_HEAD_DIM = 256
FFN_HIDDEN = 2816
FFN_HALF = FFN_HIDDEN // 2
RMS_EPS = 1e-6
LN_EPS = 1e-5
N_CHIPS = 4
LANES = 128
SUBLANES = 8

ADAM_LR = 0.001
ADAM_B1 = 0.9
ADAM_B2 = 0.999
ADAM_EPS = 1e-08
ADAM_WD = 0.01
ADAM_STEP = 10

VMEM_LIMIT_BYTES = 56 * 1024 * 1024
MESH = pl.DeviceIdType.MESH
ANY = pl.BlockSpec(memory_space=pl.ANY)
HBM_SPEC = pl.BlockSpec(memory_space=pltpu.HBM)
SEM_SPEC = pl.BlockSpec(memory_space=pltpu.SEMAPHORE)

_NT = (((1,), (1,)), ((), ()))
_TN = (((0,), (0,)), ((), ()))
_GELU_C = 0.7978845608028654
_GELU_A = 0.044715


def _dot(a, b):
    return jnp.dot(a, b, preferred_element_type=F32)


def _dot_nt(a, b):
    return lax.dot_general(a, b, _NT, preferred_element_type=F32)


def _dot_tn(a, b):
    return lax.dot_general(a, b, _TN, preferred_element_type=F32)


def _mean(v):
    return jnp.mean(v, axis=-1, keepdims=True)


def _rowsum(v):
    return jnp.sum(v, axis=0, keepdims=True)


def _sigmoid(v):
    return 1.0 / (1.0 + jnp.exp(-v))


def _gelu_parts(v):
    v2 = v * v
    t = jnp.tanh(_GELU_C * (v + _GELU_A * v * v2))
    g = 0.5 * v * (1.0 + t)
    dg = 0.5 * (1.0 + t) + 0.5 * v * (1.0 - t * t) * (_GELU_C * (1.0 + 3.0 * _GELU_A * v2))
    return g, dg


def _rms_stats(v):
    return lax.rsqrt(_mean(v * v) + RMS_EPS)


def _rms_bwd(dy, v, r, g):
    n = v * r
    dn = dy * g
    dv = r * (dn - n * _mean(dn * n))
    return dv, _rowsum(dy * n)


def _ln_stats(v):
    mu = _mean(v)
    xc = v - mu
    rs = lax.rsqrt(_mean(xc * xc) + LN_EPS)
    return xc * rs, rs


def _ln_bwd(dy, xh, rs, g):
    dxh = dy * g
    dv = rs * (dxh - _mean(dxh) - xh * _mean(dxh * xh))
    return dv, _rowsum(dy * xh), _rowsum(dy)


def _params(sem):
    return pltpu.CompilerParams(dimension_semantics=sem, vmem_limit_bytes=VMEM_LIMIT_BYTES)


def _row_tile(s):
    return 512 if s % 512 == 0 and s >= 2048 else 128


def _mesh_pos():
    return lax.axis_index("x"), lax.axis_index("y"), lax.axis_index("c")


def _slot(buf, chip_idx, half):
    if buf.shape[0] == N_CHIPS:
        return buf.at[chip_idx, half]
    width = buf.shape[-1] // 2
    return buf.at[chip_idx // 2, half, :, pl.ds(pl.multiple_of((chip_idx % 2) * width, LANES), width)]


def _cast_into_slots(ws, pos, dtypes, name, side_by_side=()):
    n = len(ws)

    def body(pos_ref, *refs):
        for a in range(n):
            refs[n + a][0] = refs[a][...].astype(dtypes[a])

    def out_spec(a, w):
        if a in side_by_side:
            return pl.BlockSpec((1, 1) + w.shape[1:], lambda i, p: (p[0] // 2, i, 0, p[0] % 2))
        return pl.BlockSpec((1, 1) + w.shape[1:], lambda i, p: (p[0], i, 0, 0))

    def out_shape(a, w):
        if a in side_by_side:
            return (2, 2, w.shape[1], 2 * w.shape[2])
        return (N_CHIPS,) + w.shape

    return pl.pallas_call(
        body, name=name,
        grid_spec=pltpu.PrefetchScalarGridSpec(
            num_scalar_prefetch=1, grid=(2,),
            in_specs=[pl.BlockSpec((1,) + w.shape[1:], lambda i, p: (i, 0, 0)) for w in ws],
            out_specs=[out_spec(a, w) for a, w in enumerate(ws)]),
        out_shape=[jax.ShapeDtypeStruct(out_shape(a, w), dt) for a, (w, dt) in enumerate(zip(ws, dtypes))],
        compiler_params=_params(("parallel",)),
    )(pos, *ws)


def _adam_update(w, g, m, v):
    nm = ADAM_B1 * m + (1.0 - ADAM_B1) * g
    nv = ADAM_B2 * v + (1.0 - ADAM_B2) * (g * g)
    m_hat = nm / (1.0 - ADAM_B1 ** ADAM_STEP)
    v_hat = nv / (1.0 - ADAM_B2 ** ADAM_STEP)
    return -ADAM_LR * (m_hat / (jnp.sqrt(v_hat) + ADAM_EPS) + ADAM_WD * w), nm, nv


ADAM_STEPS = 4


def _adamw(quads, name, after=()):
    n = len(quads)

    def body(*refs):
        ins, outs = refs[:4 * n], refs[4 * n:]
        for a in range(n):
            w, g, m, v = (r[...] for r in ins[4 * a:4 * a + 4])
            outs[4 * a][...] = g
            outs[4 * a + 1][...], outs[4 * a + 2][...], outs[4 * a + 3][...] = _adam_update(w, g, m, v)

    specs = [pl.BlockSpec((q[0].shape[0] // ADAM_STEPS, q[0].shape[1]), lambda i: (i, 0)) for q in quads]
    out = _tied_call(
        body, after, name=name, grid=(ADAM_STEPS,),
        in_specs=[sp for sp in specs for _ in range(4)], out_specs=[sp for sp in specs for _ in range(4)],
        out_shape=[jax.ShapeDtypeStruct(q[0].shape, F32) for q in quads for _ in range(4)],
        compiler_params=_params(("parallel",)),
    )(*[a for q in quads for a in q])
    return [tuple(out[4 * a:4 * a + 4]) for a in range(n)]


def _adamw_small(gpack, pos, params, offsets, conv_at):
    n = len(params)

    def body(pos_ref, g_ref, *refs):
        ins, outs = refs[:3 * n], refs[3 * n:]
        for k in range(n):
            rows = params[k][0].shape[0]
            start = offsets[k]
            if k == conv_at:
                start = pl.multiple_of(start + pos_ref[0] * CONV_HALO, SUBLANES)
            g = g_ref[pl.ds(start, rows), :]
            outs[4 * k][...] = g
            outs[4 * k + 1][...], outs[4 * k + 2][...], outs[4 * k + 3][...] = _adam_update(
                ins[3 * k][...], g, ins[3 * k + 1][...], ins[3 * k + 2][...])

    flat = [a for p in params for a in p]
    vmem = pl.BlockSpec(memory_space=pltpu.VMEM)
    return pl.pallas_call(
        body, name="adamw_small",
        in_specs=[pl.BlockSpec(memory_space=pltpu.SMEM), vmem] + [vmem] * len(flat),
        out_specs=[vmem] * (4 * n),
        out_shape=[jax.ShapeDtypeStruct(p[0].shape, F32) for p in params for _ in range(4)],
    )(pos, gpack, *flat)


def _as_tuple(after):
    return tuple(after) if isinstance(after, (tuple, list)) else (after,)


def _tied_call(body, after, *, in_specs, **kwargs):
    after = _as_tuple(after)
    n_in, n_after = len(in_specs), len(after)

    def tied(*refs):
        body(*refs[:n_in], *refs[n_in + n_after:])

    call = pl.pallas_call(tied, in_specs=list(in_specs) + [ANY] * n_after, **kwargs)
    return lambda *operands: call(*operands, *after)


def _other_chips(x, y):
    return [(1 - x, y), (x, 1 - y), (1 - x, 1 - y)]


def _gather_descriptors(bufs, send_of, recv_of):
    x, y, c = _mesh_pos()
    me = 2 * x + y
    chips = _other_chips(x, y)
    sends, arrivals = [], []
    for a in range(len(bufs)):
        for k in range(3):
            ck = 2 * chips[k][0] + chips[k][1]

            def copy(slot, a=a, k=k):
                return pltpu.make_async_remote_copy(
                    src_ref=_slot(bufs[a], slot, c), dst_ref=_slot(bufs[a], slot, c),
                    send_sem=send_of(a, k), recv_sem=recv_of(a, k),
                    device_id=(*chips[k], c), device_id_type=MESH)

            sends.append(functools.partial(copy, me))
            arrivals.append(functools.partial(copy, ck))
    return sends, arrivals


def _gather_start(bufs, name, after=()):
    n = len(bufs)
    ns = 3 * n

    def body(*refs):
        sems = refs[n:n + 2 * ns]
        thru = refs[n + 2 * ns:2 * n + 2 * ns]
        token = refs[2 * n + 2 * ns]
        _chips_handshake()
        sends, _ = _gather_descriptors(thru, lambda a, k: sems[3 * a + k], lambda a, k: sems[ns + 3 * a + k])
        for cp in sends:
            cp().start()
        token[...] = jnp.zeros_like(token)

    held = [pltpu.with_memory_space_constraint(b, pltpu.HBM) for b in bufs]
    out = _tied_call(
        body, after, name=name,
        out_shape=(*[pltpu.SemaphoreType.DMA(())] * (2 * ns), *[pltpu.HBM(b.shape, b.dtype) for b in held],
                   jax.ShapeDtypeStruct((8, LANES), F32)),
        in_specs=[HBM_SPEC] * n,
        out_specs=(*[SEM_SPEC] * (2 * ns), *[HBM_SPEC] * n, pl.BlockSpec(memory_space=pltpu.VMEM)),
        input_output_aliases={i: 2 * ns + i for i in range(n)},
        compiler_params=pltpu.CompilerParams(has_side_effects=pltpu.SideEffectType.DATAFLOW_SIDE_EFFECTING,
                                             collective_id=CHIPS_COLLECTIVE_ID),
    )(*held)
    return list(out[:ns]), list(out[ns:2 * ns]), list(out[2 * ns:2 * ns + n]), out[2 * ns + n]


def _gather_wait(send_sems, recv_sems, bufs, after, name, descriptors=_gather_descriptors):
    n = len(bufs)
    ns = 3 * n

    def body(*refs):
        buf_ref = refs[:n]
        sem_ref = refs[n:n + 2 * ns]
        sends, arrivals = descriptors(buf_ref, lambda a, k: sem_ref[3 * a + k], lambda a, k: sem_ref[ns + 3 * a + k])
        for cp in sends:
            cp().wait_send()
        for cp in arrivals:
            cp().wait_recv()

    out = pl.pallas_call(
        body, name=name,
        out_shape=tuple(pltpu.HBM(b.shape, b.dtype) for b in bufs),
        in_specs=[HBM_SPEC] * n + [SEM_SPEC] * (2 * ns) + [ANY] * len(_as_tuple(after)),
        out_specs=tuple([HBM_SPEC] * n),
        input_output_aliases={i: i for i in range(n)},
        compiler_params=pltpu.CompilerParams(has_side_effects=pltpu.SideEffectType.DATAFLOW_SIDE_EFFECTING),
    )(*bufs, *send_sems, *recv_sems, *_as_tuple(after))
    return list(out)


SIBLING_COLLECTIVE_ID = 0


def _sibling_handshake():
    x, y, c = _mesh_pos()
    barrier = pltpu.get_barrier_semaphore()
    pl.semaphore_signal(barrier, inc=1, device_id=(x, y, 1 - c), device_id_type=MESH)
    pl.semaphore_wait(barrier, 1)


CHIPS_COLLECTIVE_ID = 1


def _chips_handshake():
    x, y, c = _mesh_pos()
    barrier = pltpu.get_barrier_semaphore()
    for chip in _other_chips(x, y):
        pl.semaphore_signal(barrier, inc=1, device_id=(*chip, c), device_id_type=MESH)
    pl.semaphore_wait(barrier, 3)


def _pass_descriptors(bufs, send_of, recv_of):
    x, y, c = _mesh_pos()
    chips = _other_chips(x, y)

    def half(a, k, which):
        ck = 2 * chips[k][0] + chips[k][1]
        return functools.partial(
            pltpu.make_async_remote_copy,
            src_ref=_slot(bufs[a], ck, which), dst_ref=_slot(bufs[a], ck, which),
            send_sem=send_of(a, k), recv_sem=recv_of(a, k),
            device_id=(x, y, 1 - c), device_id_type=MESH)

    pairs = [(a, k) for a in range(len(bufs)) for k in range(3)]
    return [half(a, k, c) for a, k in pairs], [half(a, k, 1 - c) for a, k in pairs]


def _pass_start(bufs, name, after=()):
    n = len(bufs)
    ns = 3 * n

    def body(*refs):
        sems = refs[n:n + 2 * ns]
        thru = refs[n + 2 * ns:2 * n + 2 * ns]
        token = refs[2 * n + 2 * ns]
        _sibling_handshake()
        sends, _ = _pass_descriptors(thru, lambda a, k: sems[3 * a + k], lambda a, k: sems[ns + 3 * a + k])
        for cp in sends:
            cp().start()
        token[...] = jnp.zeros_like(token)

    held = [pltpu.with_memory_space_constraint(b, pltpu.HBM) for b in bufs]
    out = _tied_call(
        body, after, name=name,
        out_shape=(*[pltpu.SemaphoreType.DMA(())] * (2 * ns), *[pltpu.HBM(b.shape, b.dtype) for b in held],
                   jax.ShapeDtypeStruct((8, LANES), F32)),
        in_specs=[HBM_SPEC] * n,
        out_specs=(*[SEM_SPEC] * (2 * ns), *[HBM_SPEC] * n, pl.BlockSpec(memory_space=pltpu.VMEM)),
        input_output_aliases={i: 2 * ns + i for i in range(n)},
        compiler_params=pltpu.CompilerParams(has_side_effects=pltpu.SideEffectType.DATAFLOW_SIDE_EFFECTING,
                                             collective_id=SIBLING_COLLECTIVE_ID),
    )(*held)
    return list(out[:ns]), list(out[ns:2 * ns]), list(out[2 * ns:2 * ns + n]), out[2 * ns + n]


def _pass_to_sibling(bufs, name, after=()):
    n = len(bufs)

    def body(*refs):
        outs = refs[n:2 * n]
        send_sem, recv_sem = refs[2 * n:]
        _sibling_handshake()
        sends, arrivals = _pass_descriptors(outs, lambda a, k: send_sem.at[a, k], lambda a, k: recv_sem.at[a, k])
        sends = [cp() for cp in sends]
        for cp in sends:
            cp.start()
        for cp in arrivals:
            cp().wait_recv()
        for cp in sends:
            cp.wait_send()

    return _tied_call(
        body, after, name=name,
        in_specs=[ANY] * n, out_specs=[ANY] * n,
        out_shape=[jax.ShapeDtypeStruct(b.shape, b.dtype) for b in bufs],
        input_output_aliases={a: a for a in range(n)},
        scratch_shapes=[pltpu.SemaphoreType.DMA((n, 3))] * 2,
        compiler_params=pltpu.CompilerParams(collective_id=SIBLING_COLLECTIVE_ID),
    )(*bufs)


def _swap_descriptors(grads, lands, send_of, recv_of):
    x, y, c = _mesh_pos()
    return [functools.partial(
        pltpu.make_async_remote_copy,
        src_ref=grads[a].at[:, pl.ds(1 - c, 1)], dst_ref=lands[a],
        send_sem=send_of(a), recv_sem=recv_of(a),
        device_id=(x, y, 1 - c), device_id_type=MESH) for a in range(len(grads))]


def _swap_halves(grads, name, after=()):
    n = len(grads)

    def body(*refs):
        ins, outs = refs[:n], refs[n:2 * n]
        send_sem, recv_sem = refs[2 * n:]
        _sibling_handshake()
        cps = [cp() for cp in _swap_descriptors(ins, outs, lambda a: send_sem.at[a], lambda a: recv_sem.at[a])]
        for cp in cps:
            cp.start()
        for cp in cps:
            cp.wait()

    out_shape = [jax.ShapeDtypeStruct((g.shape[0], 1) + g.shape[2:], g.dtype) for g in grads]
    return _tied_call(
        body, after, name=name,
        in_specs=[ANY] * n, out_specs=[ANY] * n, out_shape=out_shape,
        scratch_shapes=[pltpu.SemaphoreType.DMA((n,))] * 2,
        compiler_params=pltpu.CompilerParams(collective_id=SIBLING_COLLECTIVE_ID),
    )(*grads)


def _swap_start(grads, name, after=()):
    n, after = len(grads), _as_tuple(after)

    def body(*refs):
        outs = refs[2 * n + len(after):]
        sems, g_thru, l_thru, token = outs[:2 * n], outs[2 * n:3 * n], outs[3 * n:4 * n], outs[4 * n]
        _sibling_handshake()
        for cp in _swap_descriptors(g_thru, l_thru, lambda a: sems[a], lambda a: sems[n + a]):
            cp().start()
        token[...] = jnp.zeros_like(token)

    lands = [lax.empty((g.shape[0], 1) + g.shape[2:], g.dtype) for g in grads]
    held = [pltpu.with_memory_space_constraint(a, pltpu.HBM) for a in (*grads, *lands)]
    out = pl.pallas_call(
        body, name=name,
        out_shape=(*[pltpu.SemaphoreType.DMA(())] * (2 * n), *[pltpu.HBM(a.shape, a.dtype) for a in held],
                   jax.ShapeDtypeStruct((8, LANES), F32)),
        in_specs=[HBM_SPEC] * (2 * n) + [ANY] * len(after),
        out_specs=(*[SEM_SPEC] * (2 * n), *[HBM_SPEC] * (2 * n), pl.BlockSpec(memory_space=pltpu.VMEM)),
        input_output_aliases={i: 2 * n + i for i in range(2 * n)},
        compiler_params=pltpu.CompilerParams(has_side_effects=pltpu.SideEffectType.DATAFLOW_SIDE_EFFECTING,
                                             collective_id=SIBLING_COLLECTIVE_ID),
    )(*held, *after)
    return list(out[:2 * n]), list(out[2 * n:3 * n]), list(out[3 * n:4 * n]), out[4 * n]


def _swap_wait(sems, grads, lands, after, name):
    n = len(grads)

    def body(*refs):
        g_ref, l_ref = refs[:n], refs[n:2 * n]
        sem_ref = refs[2 * n:4 * n]
        for cp in _swap_descriptors(g_ref, l_ref, lambda a: sem_ref[a], lambda a: sem_ref[n + a]):
            cp().wait()

    out = pl.pallas_call(
        body, name=name,
        out_shape=tuple(pltpu.HBM(a.shape, a.dtype) for a in (*grads, *lands)),
        in_specs=[HBM_SPEC] * (2 * n) + [SEM_SPEC] * (2 * n) + [ANY] * len(_as_tuple(after)),
        out_specs=tuple([HBM_SPEC] * (2 * n)),
        input_output_aliases={i: i for i in range(2 * n)},
        compiler_params=pltpu.CompilerParams(has_side_effects=pltpu.SideEffectType.DATAFLOW_SIDE_EFFECTING),
    )(*grads, *lands, *sems, *_as_tuple(after))
    return list(out[:n]), list(out[n:])


def _add_halves(gs, gots, pos, name, dtypes):
    n = len(gs)
    j = gs[0].shape[0]

    def body(pos_ref, *refs):
        g_refs, r_refs = refs[:n], refs[n:2 * n]
        o_refs, p_refs = refs[2 * n:3 * n], refs[3 * n:]
        vals = [(g_refs[a][0, 0] + r_refs[a][0, 0]).astype(dtypes[a]) for a in range(n)]
        for a in range(n):
            o_refs[a][0] = vals[a]
        if j == 1:
            for a in range(n):
                p_refs[a][0] = vals[a]
        else:
            @pl.when(pl.program_id(0) == pos_ref[0])
            def _():
                for a in range(n):
                    p_refs[a][0] = vals[a]

    blk = lambda g: (1,) + g.shape[2:]
    out = pl.pallas_call(
        body, name=name,
        grid_spec=pltpu.PrefetchScalarGridSpec(
            num_scalar_prefetch=1, grid=(j,),
            in_specs=[pl.BlockSpec((1,) + blk(g), lambda i, p: (i, p[1], 0, 0)) for g in gs]
            + [pl.BlockSpec((1,) + blk(g), lambda i, p: (i, 0, 0, 0)) for g in gs],
            out_specs=[pl.BlockSpec(blk(g), lambda i, p: (i, 0, 0)) for g in gs]
            + [pl.BlockSpec(blk(g), lambda i, p: (p[0], 0, 0)) for g in gs]),
        out_shape=[jax.ShapeDtypeStruct((j,) + g.shape[2:], dt) for g, dt in zip(gs, dtypes)]
        + [jax.ShapeDtypeStruct((N_CHIPS,) + g.shape[2:], dt) for g, dt in zip(gs, dtypes)],
        compiler_params=_params(("arbitrary",)),
    )(pos, *gs, *gots)
    return list(out[:n]), list(out[n:])


def _exchange_descriptors(sums, parts, send_of, recv_of):
    x, y, c = _mesh_pos()
    me = 2 * x + y
    chips = _other_chips(x, y)
    sends, arrivals = [], []
    for a in range(len(sums)):
        for k in range(3):
            ck = 2 * chips[k][0] + chips[k][1]
            mine = sums[a].at[ck] if sums[a].shape[0] == N_CHIPS else sums[a].at[0]

            def copy(dst_slot, a=a, k=k, mine=mine):
                return pltpu.make_async_remote_copy(
                    src_ref=mine, dst_ref=parts[a].at[dst_slot],
                    send_sem=send_of(a, k), recv_sem=recv_of(a, k),
                    device_id=(*chips[k], c), device_id_type=MESH)

            sends.append(functools.partial(copy, me))
            arrivals.append(functools.partial(copy, ck))
    return sends, arrivals


def _exchange_start(sums, parts, name, after=()):
    n = len(sums)
    ns = 3 * n

    def body(*refs):
        sems = refs[2 * n:2 * n + 2 * ns]
        sums_thru = refs[2 * n + 2 * ns:3 * n + 2 * ns]
        parts_thru = refs[3 * n + 2 * ns:4 * n + 2 * ns]
        token = refs[4 * n + 2 * ns]
        _chips_handshake()
        sends, _ = _exchange_descriptors(sums_thru, parts_thru, lambda a, k: sems[3 * a + k],
                                         lambda a, k: sems[ns + 3 * a + k])
        for cp in sends:
            cp().start()
        token[...] = jnp.zeros_like(token)

    hbm = lambda a: pltpu.HBM(a.shape, a.dtype)
    held = [pltpu.with_memory_space_constraint(a, pltpu.HBM) for a in (*sums, *parts)]
    out = _tied_call(
        body, after, name=name,
        out_shape=(*[pltpu.SemaphoreType.DMA(())] * (2 * ns), *[hbm(a) for a in held],
                   jax.ShapeDtypeStruct((8, LANES), F32)),
        in_specs=[HBM_SPEC] * (2 * n),
        out_specs=(*[SEM_SPEC] * (2 * ns), *[HBM_SPEC] * (2 * n), pl.BlockSpec(memory_space=pltpu.VMEM)),
        input_output_aliases={i: 2 * ns + i for i in range(2 * n)},
        compiler_params=pltpu.CompilerParams(has_side_effects=pltpu.SideEffectType.DATAFLOW_SIDE_EFFECTING,
                                             collective_id=CHIPS_COLLECTIVE_ID),
    )(*held)
    return (list(out[:2 * ns]), list(out[2 * ns:2 * ns + n]), list(out[2 * ns + n:2 * ns + 2 * n]),
            out[2 * ns + 2 * n])


def _exchange_wait(sems, sums, parts, after, name):
    n = len(sums)
    ns = 3 * n

    def body(*refs):
        sums_ref, parts_ref = refs[:n], refs[n:2 * n]
        sem_ref = refs[2 * n:2 * n + 2 * ns]
        sends, arrivals = _exchange_descriptors(sums_ref, parts_ref, lambda a, k: sem_ref[3 * a + k],
                                                lambda a, k: sem_ref[ns + 3 * a + k])
        for cp in sends:
            cp().wait_send()
        for cp in arrivals:
            cp().wait_recv()

    hbm = lambda a: pltpu.HBM(a.shape, a.dtype)
    out = pl.pallas_call(
        body, name=name,
        out_shape=tuple(hbm(a) for a in (*sums, *parts)),
        in_specs=[HBM_SPEC] * (2 * n) + [SEM_SPEC] * (2 * ns) + [ANY] * len(_as_tuple(after)),
        out_specs=tuple([HBM_SPEC] * (2 * n)),
        input_output_aliases={i: i for i in range(2 * n)},
        compiler_params=pltpu.CompilerParams(has_side_effects=pltpu.SideEffectType.DATAFLOW_SIDE_EFFECTING),
    )(*sums, *parts, *sems, *_as_tuple(after))
    return list(out[n:])


def _sum_chips(parts, pos, name, after=()):
    n = len(parts)
    after = _as_tuple(after)

    def body(pos_ref, *refs):
        outs = refs[n + len(after):]
        for a in range(n):
            p_ref = refs[a]
            outs[a][0] = (((p_ref[0].astype(F32) + p_ref[1].astype(F32)) + p_ref[2].astype(F32))
                          + p_ref[3].astype(F32))

    out = pl.pallas_call(
        body, name=name,
        grid_spec=pltpu.PrefetchScalarGridSpec(
            num_scalar_prefetch=1, grid=(1,),
            in_specs=[pl.BlockSpec(p.shape, lambda i, q: (0, 0, 0)) for p in parts] + [ANY] * len(after),
            out_specs=[pl.BlockSpec((1,) + p.shape[1:], lambda i, q: (q[1], 0, 0)) for p in parts]),
        out_shape=[jax.ShapeDtypeStruct((2,) + p.shape[1:], F32) for p in parts],
        compiler_params=_params(("arbitrary",)),
    )(pos, *parts, *after)
    return list(out)


def _join_descriptors(fulls, send_of, recv_of):
    x, y, c = _mesh_pos()

    def half(a, which):
        return functools.partial(
            pltpu.make_async_remote_copy,
            src_ref=fulls[a].at[which], dst_ref=fulls[a].at[which],
            send_sem=send_of(a), recv_sem=recv_of(a),
            device_id=(x, y, 1 - c), device_id_type=MESH)

    return [half(a, c) for a in range(len(fulls))], [half(a, 1 - c) for a in range(len(fulls))]


def _join_start(fulls, name, after=()):
    n = len(fulls)

    def body(*refs):
        sems, thru, token = refs[n:3 * n], refs[3 * n:4 * n], refs[4 * n]
        _sibling_handshake()
        sends, _ = _join_descriptors(thru, lambda a: sems[a], lambda a: sems[n + a])
        for cp in sends:
            cp().start()
        token[...] = jnp.zeros_like(token)

    held = [pltpu.with_memory_space_constraint(f, pltpu.HBM) for f in fulls]
    out = _tied_call(
        body, after, name=name,
        out_shape=(*[pltpu.SemaphoreType.DMA(())] * (2 * n), *[pltpu.HBM(f.shape, f.dtype) for f in held],
                   jax.ShapeDtypeStruct((8, LANES), F32)),
        in_specs=[HBM_SPEC] * n,
        out_specs=(*[SEM_SPEC] * (2 * n), *[HBM_SPEC] * n, pl.BlockSpec(memory_space=pltpu.VMEM)),
        input_output_aliases={i: 2 * n + i for i in range(n)},
        compiler_params=pltpu.CompilerParams(has_side_effects=pltpu.SideEffectType.DATAFLOW_SIDE_EFFECTING,
                                             collective_id=SIBLING_COLLECTIVE_ID),
    )(*held)
    return list(out[:2 * n]), list(out[2 * n:3 * n]), out[3 * n]


def _join_wait(sems, fulls, after, name):
    n = len(fulls)

    def body(*refs):
        sem_ref = refs[n:3 * n]
        sends, arrivals = _join_descriptors(refs[:n], lambda a: sem_ref[a], lambda a: sem_ref[n + a])
        for cp in sends:
            cp().wait_send()
        for cp in arrivals:
            cp().wait_recv()

    out = pl.pallas_call(
        body, name=name,
        out_shape=tuple(pltpu.HBM(f.shape, f.dtype) for f in fulls),
        in_specs=[HBM_SPEC] * n + [SEM_SPEC] * (2 * n) + [ANY] * len(_as_tuple(after)),
        out_specs=tuple([HBM_SPEC] * n),
        input_output_aliases={i: i for i in range(n)},
        compiler_params=pltpu.CompilerParams(has_side_effects=pltpu.SideEffectType.DATAFLOW_SIDE_EFFECTING),
    )(*fulls, *sems, *_as_tuple(after))
    return list(out)


def _join_halves(fulls, name, after=()):
    n = len(fulls)

    def body(*refs):
        send_sem, recv_sem = refs[2 * n:]
        _sibling_handshake()
        sends, arrivals = _join_descriptors(refs[n:2 * n], lambda a: send_sem.at[a], lambda a: recv_sem.at[a])
        sends = [cp() for cp in sends]
        for cp in sends:
            cp.start()
        for cp in arrivals:
            cp().wait_recv()
        for cp in sends:
            cp.wait_send()

    out_shape = [jax.ShapeDtypeStruct(f.shape, f.dtype) for f in fulls]
    return _tied_call(
        body, after, name=name,
        in_specs=[ANY] * n, out_specs=[ANY] * n, out_shape=out_shape,
        input_output_aliases={a: a for a in range(n)},
        scratch_shapes=[pltpu.SemaphoreType.DMA((n,))] * 2,
        compiler_params=pltpu.CompilerParams(collective_id=SIBLING_COLLECTIVE_ID),
    )(*fulls)


def _norm_in(x, g, ts, after=()):
    s = x.shape[0]

    def body(x_ref, g_ref, hn_ref):
        xv = x_ref[...]
        hn_ref[...] = (xv * _rms_stats(xv) * g_ref[...]).astype(BF16)

    row = pl.BlockSpec((ts, D_MODEL), lambda i: (i, 0))
    return _tied_call(
        body, after, name="norm_in", grid=(s // ts,),
        in_specs=[row, pl.BlockSpec((1, D_MODEL), lambda i: (0, 0))], out_specs=row,
        out_shape=jax.ShapeDtypeStruct((s, D_MODEL), BF16),
        compiler_params=_params(("parallel",)),
    )(x, g)


def _shift_rows(buf, shifted, t):
    rows = t + CONV_HALO - SUBLANES
    for r in range(1, SUBLANES):
        shifted[r - 1, 0:rows, :] = buf[pl.ds(r, rows), :]


def _window(buf, shifted, offset, t):
    r = offset % SUBLANES
    if r == 0:
        return buf[pl.ds(offset, t), :]
    return shifted[r - 1, pl.ds(offset - r, t), :]


def _lane_is_low_head():
    lane = lax.broadcasted_iota(jnp.int32, (1, GM_WIDTH), 1)
    return (lane & GM_HEAD_DIM) == 0


def _gm_mix(v_lo, v_hi, wpair_ref, bias_ref, mixed_ref, t):
    for n in range(t // CHUNK):
        rows = slice(n * CHUNK, (n + 1) * CHUNK)
        for j in range(GM_HEADS // 2):
            cols = slice(j * LANES, (j + 1) * LANES)
            rhs = jnp.concatenate([v_lo[rows, cols], v_hi[rows, cols]], axis=0)
            mixed_ref[rows, cols] = _dot(wpair_ref[j], rhs) + bias_ref[:, cols]


def _seqmix_fwd(hn, w_in, b_in, cw, cb, lng, lnb, gg, gb, wpair, bias, t, after=()):
    s = hn.shape[0]

    def body(hn_ref, w_ref, b_ref, cw_ref, cb_ref, lng_ref, lnb_ref, gg_ref, gb_ref, wpair_ref, bias_ref,
             z_ref, mix_ref, c1_ref, abuf, ash, mixed_ref):
        i = pl.program_id(0)

        @pl.when(i == 0)
        def _():
            abuf[0:CONV_HALO, :] = jnp.zeros((CONV_HALO, CONV_WIDTH), F32)

        @pl.when(i > 0)
        def _():
            abuf[0:CONV_HALO, :] = abuf[t:t + CONV_HALO, :]

        hv = hn_ref[...]
        for j in range(4):
            cols = slice(j * 512, (j + 1) * 512)
            z_ref[:, cols] = _dot(hv, w_ref[j]) + b_ref[:, cols]

        abuf[CONV_HALO:, :] = z_ref[:, 0:512] * _sigmoid(z_ref[:, 512:1024])
        _shift_rows(abuf, ash, t)
        acc = jnp.zeros((t, CONV_WIDTH), F32)
        for k in range(CONV_KERNEL):
            acc = acc + cw_ref[k:k + 1, :] * _window(abuf, ash, CONV_HALO - (CONV_KERNEL - 1) + k, t)
        c1 = acc + cb_ref[...]
        c1_ref[...] = c1
        xh, _ = _ln_stats(c1)
        ln = xh * lng_ref[...] + lnb_ref[...]
        mix_ref[:, 0:512] = (ln * _sigmoid(ln)).astype(BF16)

        u, _ = _gelu_parts(z_ref[:, 1024:1536])
        gv, _ = _gelu_parts(z_ref[:, 1536:2048])
        vxh, _ = _ln_stats(gv)
        v = vxh * gg_ref[...] + gb_ref[...]
        low = _lane_is_low_head()
        v_lo = jnp.where(low, v, 0.0).astype(BF16)
        v_hi = jnp.where(low, 0.0, v).astype(BF16)
        _gm_mix(v_lo, v_hi, wpair_ref, bias_ref, mixed_ref, t)
        mix_ref[:, 512:1024] = (u * mixed_ref[...]).astype(BF16)

    vec = lambda n: pl.BlockSpec((1, n), lambda i: (0, 0))
    return _tied_call(
        body, after, name="seqmix_fwd", grid=(s // t,),
        in_specs=[pl.BlockSpec((t, D_MODEL), lambda i: (i, 0)),
                  pl.BlockSpec((4, D_MODEL, 512), lambda i: (0, 0, 0)), vec(2048),
                  pl.BlockSpec((CONV_HALO, CONV_WIDTH), lambda i: (0, 0)),
                  vec(512), vec(512), vec(512), vec(512), vec(512),
                  pl.BlockSpec((4, CHUNK, 2 * CHUNK), lambda i: (0, 0, 0)),
                  pl.BlockSpec((CHUNK, GM_WIDTH), lambda i: (0, 0))],
        out_specs=[pl.BlockSpec((t, 2048), lambda i: (i, 0)),
                   pl.BlockSpec((t, D_MODEL), lambda i: (i, 0)),
                   pl.BlockSpec((t, CONV_WIDTH), lambda i: (i, 0))],
        out_shape=[jax.ShapeDtypeStruct((s, 2048), F32), jax.ShapeDtypeStruct((s, D_MODEL), BF16),
                   jax.ShapeDtypeStruct((s, CONV_WIDTH), F32)],
        scratch_shapes=[pltpu.VMEM((t + CONV_HALO, CONV_WIDTH), F32),
                        pltpu.VMEM((SUBLANES - 1, t + CONV_HALO - SUBLANES, CONV_WIDTH), F32),
                        pltpu.VMEM((t, GM_WIDTH), F32)],
        compiler_params=_params(("arbitrary",)),
    )(hn, w_in, b_in, cw, cb, lng, lnb, gg, gb, wpair, bias)


def _mem_kv(mem, g, wkv):
    m = mem.shape[0]

    def body(mem_ref, g_ref, w_ref, mn_ref, kv_ref):
        mv = mem_ref[...]
        mn = (mv * _rms_stats(mv) * g_ref[...]).astype(BF16)
        mn_ref[...] = mn
        for j in range(4):
            kv_ref[:, j * 512:(j + 1) * 512] = _dot(mn, w_ref[j]).astype(BF16)

    return pl.pallas_call(
        body, name="mem_kv",
        out_shape=[jax.ShapeDtypeStruct((m, D_MODEL), BF16), jax.ShapeDtypeStruct((m, 2 * D_MODEL), BF16)],
        compiler_params=pltpu.CompilerParams(vmem_limit_bytes=VMEM_LIMIT_BYTES),
    )(mem, g, wkv)


def _softmax_rows(sc):
    e = jnp.exp(sc - jnp.max(sc, axis=-1, keepdims=True))
    return e / jnp.sum(e, axis=-1, keepdims=True)


def _attn_block_fwd(x, mix, w_out, g_xa, wq, kv, wo, g_ffn, ts, after=()):
    s, m = x.shape[0], kv.shape[0]
    scale = XA_HEAD_DIM ** -0.5

    def body(x_ref, mix_ref, wout_ref, gxa_ref, wq_ref, kv_ref, wo_ref, gffn_ref,
             h1_ref, hn2_ref, q_ref, o_ref, h2_ref, hn3_ref):
        h1 = x_ref[...] + _dot(mix_ref[...], wout_ref[...])
        h1_ref[...] = h1
        hn2 = (h1 * _rms_stats(h1) * gxa_ref[...]).astype(BF16)
        hn2_ref[...] = hn2
        q_ref[...] = _dot(hn2, wq_ref[...]).astype(BF16)
        for h in range(XA_HEADS):
            cols = slice(h * XA_HEAD_DIM, (h + 1) * XA_HEAD_DIM)
            vcols = slice(D_MODEL + h * XA_HEAD_DIM, D_MODEL + (h + 1) * XA_HEAD_DIM)
            p = _softmax_rows(_dot_nt(q_ref[:, cols], kv_ref[:, cols]) * scale)
            o_ref[:, cols] = _dot(p.astype(BF16), kv_ref[:, vcols]).astype(BF16)
        h2 = h1 + _dot(o_ref[...], wo_ref[...])
        h2_ref[...] = h2
        hn3_ref[...] = (h2 * _rms_stats(h2) * gffn_ref[...]).astype(BF16)

    row = pl.BlockSpec((ts, D_MODEL), lambda i: (i, 0))
    full = pl.BlockSpec((D_MODEL, D_MODEL), lambda i: (0, 0))
    vec = pl.BlockSpec((1, D_MODEL), lambda i: (0, 0))
    f32 = jax.ShapeDtypeStruct((s, D_MODEL), F32)
    bf16 = jax.ShapeDtypeStruct((s, D_MODEL), BF16)
    return _tied_call(
        body, after, name="attn_block_fwd", grid=(s // ts,),
        in_specs=[row, row, full, vec, full, pl.BlockSpec((m, 2 * D_MODEL), lambda i: (0, 0)), full, vec],
        out_specs=[row] * 6,
        out_shape=[f32, bf16, bf16, bf16, f32, bf16],
        compiler_params=_params(("parallel",)),
    )(x, mix, w_out, g_xa, wq, kv, wo, g_ffn)


_FFN_CHUNKS = (slice(0, 8 * LANES), slice(8 * LANES, 16 * LANES), slice(16 * LANES, FFN_HIDDEN))


def _ffn_up(hn, wgu, ts, after=()):
    s = hn.shape[0]

    def body(hn_ref, w_ref, gu_ref, act_ref):
        hv = hn_ref[...]
        for cols in _FFN_CHUNKS:
            gate = _dot(hv, w_ref[0, :, cols])
            up = _dot(hv, w_ref[1, :, cols])
            gu_ref[0, :, cols] = gate.astype(BF16)
            gu_ref[1, :, cols] = up.astype(BF16)
            act_ref[:, cols] = (gate * _sigmoid(gate) * up).astype(BF16)

    return _tied_call(
        body, after, name="ffn_up", grid=(s // ts,),
        in_specs=[pl.BlockSpec((ts, D_MODEL), lambda i: (i, 0)),
                  pl.BlockSpec((2, D_MODEL, FFN_HIDDEN), lambda i: (0, 0, 0))],
        out_specs=[pl.BlockSpec((2, ts, FFN_HIDDEN), lambda i: (0, i, 0)),
                   pl.BlockSpec((ts, FFN_HIDDEN), lambda i: (i, 0))],
        out_shape=[jax.ShapeDtypeStruct((2, s, FFN_HIDDEN), BF16), jax.ShapeDtypeStruct((s, FFN_HIDDEN), BF16)],
        compiler_params=_params(("parallel",)),
    )(hn, wgu)


ACT_RING = 3


def _ffn_down_loss(act, wd, h2, g, target, ts):
    s = act.shape[0]
    n = s // ts

    def body(act_hbm, wd_ref, h2_ref, g_ref, t_ref, dh_ref, dhb_ref, sq_ref, dg_ref, ring, sems):
        i = pl.program_id(0)

        def fetch(k):
            slot = k % ACT_RING
            rows = k * ts if isinstance(k, int) else pl.multiple_of(k * ts, ts)
            return pltpu.make_async_copy(act_hbm.at[pl.ds(rows, ts)], ring.at[slot], sems.at[slot])

        @pl.when(i == 0)
        def _():
            sq_ref[...] = jnp.zeros_like(sq_ref)
            dg_ref[...] = jnp.zeros_like(dg_ref)
            for k in range(min(ACT_RING - 1, n)):
                fetch(k).start()

        @pl.when(i + (ACT_RING - 1) < n)
        def _():
            fetch(i + (ACT_RING - 1)).start()

        fetch(i).wait()
        h3 = h2_ref[...] + _dot(ring[i % ACT_RING], wd_ref[...])
        r = _rms_stats(h3)
        gv = g_ref[...]
        diff = h3 * r * gv - t_ref[...]
        sq_ref[...] += _rowsum(diff * diff)
        dh, dg = _rms_bwd(diff / D_MODEL, h3, r, gv)
        dh_ref[...] = dh
        dhb_ref[...] = dh.astype(BF16)
        dg_ref[...] += dg

    row = pl.BlockSpec((ts, D_MODEL), lambda i: (i, 0))
    vec = pl.BlockSpec((1, D_MODEL), lambda i: (0, 0))
    return pl.pallas_call(
        body, name="ffn_down_loss", grid=(n,),
        in_specs=[ANY, pl.BlockSpec((FFN_HIDDEN, D_MODEL), lambda i: (0, 0)), row, vec, row],
        out_specs=[row, row, vec, vec],
        out_shape=[jax.ShapeDtypeStruct((s, D_MODEL), F32), jax.ShapeDtypeStruct((s, D_MODEL), BF16),
                   jax.ShapeDtypeStruct((1, D_MODEL), F32), jax.ShapeDtypeStruct((1, D_MODEL), F32)],
        scratch_shapes=[pltpu.VMEM((ACT_RING, ts, FFN_HIDDEN), BF16), pltpu.SemaphoreType.DMA((ACT_RING,))],
        compiler_params=_params(("arbitrary",)),
    )(act, wd, h2, g, target)


def _grad_w(a, b, tk, tn, name, after=(), shards=1):
    s, k = a.shape
    gb, _, n = b.shape
    nblk = n // tn
    ws = tn // shards
    tsr = GRAD_ROWS if s % GRAD_ROWS == 0 else s

    def body(a_ref, b_ref, o_ref):
        part = _dot_tn(a_ref[...], b_ref[0])

        @pl.when(pl.program_id(2) == 0)
        def _():
            for j in range(shards):
                o_ref[j] = part[:, j * ws:(j + 1) * ws]

        @pl.when(pl.program_id(2) > 0)
        def _():
            for j in range(shards):
                o_ref[j] += part[:, j * ws:(j + 1) * ws]

    return _tied_call(
        body, after, name=name, grid=(gb * nblk, k // tk, s // tsr),
        in_specs=[pl.BlockSpec((tsr, tk), lambda ni, ki, si: (si, ki)),
                  pl.BlockSpec((1, tsr, tn), lambda ni, ki, si: (ni // nblk, si, ni % nblk))],
        out_specs=pl.BlockSpec((shards, tk, ws), lambda ni, ki, si: (ni, ki, 0)),
        out_shape=jax.ShapeDtypeStruct((gb * nblk * shards, k, ws), F32),
        compiler_params=_params(("parallel", "parallel", "arbitrary")),
    )(a, b)


def _grad_w_square(pairs, name, after=()):
    n = len(pairs)
    s = pairs[0][0].shape[0]
    tsr = GRAD_ROWS // 2 if s % (GRAD_ROWS // 2) == 0 else s

    def body(*refs):
        ins, outs = refs[:2 * n], refs[2 * n:]
        parts = [_dot_tn(ins[2 * a][...], ins[2 * a + 1][...]) for a in range(n)]

        @pl.when(pl.program_id(0) == 0)
        def _():
            for a in range(n):
                outs[a][...] = parts[a]

        @pl.when(pl.program_id(0) > 0)
        def _():
            for a in range(n):
                outs[a][...] += parts[a]

    row = pl.BlockSpec((tsr, D_MODEL), lambda i: (i, 0))
    return _tied_call(
        body, after, name=name, grid=(s // tsr,),
        in_specs=[row] * (2 * n), out_specs=[pl.BlockSpec((D_MODEL, D_MODEL), lambda i: (0, 0))] * n,
        out_shape=[jax.ShapeDtypeStruct((D_MODEL, D_MODEL), F32)] * n,
        compiler_params=_params(("arbitrary",)),
    )(*[x for p in pairs for x in p])


def _ffn_bwd(dh3, wd, gu, wgu, h2, g, t, after=()):
    s = dh3.shape[0]

    def body(dh3_ref, wd_ref, gu_ref, w_ref, h2_ref, g_ref, dgu_ref, dh2_ref, dh2b_ref, dg_ref):
        @pl.when(pl.program_id(0) == 0)
        def _():
            dg_ref[...] = jnp.zeros_like(dg_ref)

        dh3v = dh3_ref[...]
        dhb = dh3v.astype(BF16)
        for cols in _FFN_CHUNKS:
            dact = _dot_nt(dhb, wd_ref[cols, :])
            gate, up = gu_ref[0, :, cols].astype(F32), gu_ref[1, :, cols].astype(F32)
            sg = _sigmoid(gate)
            dgu_ref[0, :, cols] = (dact * up * (sg * (1.0 + gate * (1.0 - sg)))).astype(BF16)
            dgu_ref[1, :, cols] = (dact * (gate * sg)).astype(BF16)
        dhn = _dot_nt(dgu_ref[0], w_ref[0]) + _dot_nt(dgu_ref[1], w_ref[1])
        h2 = h2_ref[...]
        dv, dg = _rms_bwd(dhn, h2, _rms_stats(h2), g_ref[...])
        dh2 = dh3v + dv
        dh2_ref[...] = dh2
        dh2b_ref[...] = dh2.astype(BF16)
        dg_ref[...] += dg

    row = pl.BlockSpec((t, D_MODEL), lambda i: (i, 0))
    wide = pl.BlockSpec((2, t, FFN_HIDDEN), lambda i: (0, i, 0))
    vec = pl.BlockSpec((1, D_MODEL), lambda i: (0, 0))
    return _tied_call(
        body, after, name="ffn_bwd", grid=(s // t,),
        in_specs=[row, pl.BlockSpec((FFN_HIDDEN, D_MODEL), lambda i: (0, 0)), wide,
                  pl.BlockSpec((2, D_MODEL, FFN_HIDDEN), lambda i: (0, 0, 0)), row, vec],
        out_specs=[wide, row, row, vec],
        out_shape=[jax.ShapeDtypeStruct((2, s, FFN_HIDDEN), BF16), jax.ShapeDtypeStruct((s, D_MODEL), F32),
                   jax.ShapeDtypeStruct((s, D_MODEL), BF16), jax.ShapeDtypeStruct((1, D_MODEL), F32)],
        compiler_params=_params(("arbitrary",)),
    )(dh3, wd, gu, wgu, h2, g)


def _attn_bwd(dh2, wo, q, kv, wq, h1, g, ts, after=()):
    s, m = q.shape[0], kv.shape[0]
    scale = XA_HEAD_DIM ** -0.5

    def body(dh2_ref, wo_ref, q_ref, kv_ref, wq_ref, h1_ref, g_ref, dh1_ref, dh1b_ref, dq_ref, dkv_ref, dg_ref):
        @pl.when(pl.program_id(0) == 0)
        def _():
            dkv_ref[...] = jnp.zeros_like(dkv_ref)
            dg_ref[...] = jnp.zeros_like(dg_ref)

        do = _dot_nt(dh2_ref[...].astype(BF16), wo_ref[...]).astype(BF16)
        for h in range(XA_HEADS):
            cols = slice(h * XA_HEAD_DIM, (h + 1) * XA_HEAD_DIM)
            vcols = slice(D_MODEL + h * XA_HEAD_DIM, D_MODEL + (h + 1) * XA_HEAD_DIM)
            qh, kh, vh, doh = q_ref[:, cols], kv_ref[:, cols], kv_ref[:, vcols], do[:, cols]
            p = _softmax_rows(_dot_nt(qh, kh) * scale)
            dp = _dot_nt(doh, vh)
            ds = (p * (dp - jnp.sum(dp * p, axis=-1, keepdims=True)) * scale).astype(BF16)
            dq_ref[:, cols] = _dot(ds, kh).astype(BF16)
            dkv_ref[:, cols] += _dot_tn(ds, qh)
            dkv_ref[:, vcols] += _dot_tn(p.astype(BF16), doh)
        dhn = _dot_nt(dq_ref[...], wq_ref[...])
        h1 = h1_ref[...]
        dv, dg = _rms_bwd(dhn, h1, _rms_stats(h1), g_ref[...])
        dh1 = dh2_ref[...] + dv
        dh1_ref[...] = dh1
        dh1b_ref[...] = dh1.astype(BF16)
        dg_ref[...] += dg

    row = pl.BlockSpec((ts, D_MODEL), lambda i: (i, 0))
    full = pl.BlockSpec((D_MODEL, D_MODEL), lambda i: (0, 0))
    kvs = pl.BlockSpec((m, 2 * D_MODEL), lambda i: (0, 0))
    vec = pl.BlockSpec((1, D_MODEL), lambda i: (0, 0))
    return _tied_call(
        body, after, name="attn_bwd", grid=(s // ts,),
        in_specs=[row, full, row, kvs, full, row, vec],
        out_specs=[row, row, row, kvs, vec],
        out_shape=[jax.ShapeDtypeStruct((s, D_MODEL), F32), jax.ShapeDtypeStruct((s, D_MODEL), BF16),
                   jax.ShapeDtypeStruct((s, D_MODEL), BF16),
                   jax.ShapeDtypeStruct((m, 2 * D_MODEL), F32), jax.ShapeDtypeStruct((1, D_MODEL), F32)],
        compiler_params=_params(("arbitrary",)),
    )(dh2, wo, q, kv, wq, h1, g)


def _mem_kv_bwd(dkv, mn, wkv, mem, g, after=()):
    m = mem.shape[0]

    def body(dkv_ref, mn_ref, w_ref, mem_ref, g_ref, dw_ref, dg_ref):
        dmn = jnp.zeros((m, D_MODEL), F32)
        mn = mn_ref[...]
        for j in range(4):
            dj = dkv_ref[:, j * 512:(j + 1) * 512].astype(BF16)
            dw_ref[j] = _dot_tn(mn, dj)
            dmn = dmn + _dot_nt(dj, w_ref[j])
        mv = mem_ref[...]
        dg_ref[...] = _rowsum(dmn * (mv * _rms_stats(mv)))

    return _tied_call(
        body, after, name="mem_kv_bwd", in_specs=[pl.BlockSpec(memory_space=pltpu.VMEM)] * 5,
        out_shape=[jax.ShapeDtypeStruct((4, D_MODEL, 512), F32), jax.ShapeDtypeStruct((1, D_MODEL), F32)],
        compiler_params=pltpu.CompilerParams(vmem_limit_bytes=VMEM_LIMIT_BYTES),
    )(dkv, mn, wkv, mem, g)


def _seqmix_bwd(dh1, x, z, c1, w_out, w_in, g_mix, cw, lng, lnb, gg, gb, wpair, wpair_t, bias, t, after=()):
    s = x.shape[0]
    nt = s // t

    def body(dh1_ref, x_ref, z_ref, c1_ref, wo_ref, wi_ref, gm_ref, cw_ref, lng_ref, lnb_ref,
             gg_ref, gb_ref, wpair_ref, wpt_ref, bias_ref,
             gx_ref, dz_ref, dcw_ref, dcb_ref, dlng_ref, dlnb_ref, dgg_ref, dgb_ref, dws_ref, dbs_ref,
             dbin_ref, dgm_ref, dbuf, dsh, mixed_ref, dv_ref):
        i = pl.program_id(0)
        accs = (dcw_ref, dcb_ref, dlng_ref, dlnb_ref, dgg_ref, dgb_ref, dws_ref, dbs_ref, dbin_ref, dgm_ref)

        @pl.when(i == 0)
        def _():
            for r in accs:
                r[...] = jnp.zeros_like(r)
            dbuf[t:t + CONV_HALO, :] = jnp.zeros((CONV_HALO, CONV_WIDTH), F32)

        @pl.when(i > 0)
        def _():
            dbuf[t:t + CONV_HALO, :] = dbuf[0:CONV_HALO, :]

        dmix = _dot_nt(dh1_ref[...].astype(BF16), wo_ref[...])

        xh, rs = _ln_stats(c1_ref[...])
        lng = lng_ref[...]
        ln = xh * lng + lnb_ref[...]
        sl = _sigmoid(ln)
        dln = dmix[:, 0:512] * (sl * (1.0 + ln * (1.0 - sl)))
        dc1, dg_ln, db_ln = _ln_bwd(dln, xh, rs, lng)
        dlng_ref[...] += dg_ln
        dlnb_ref[...] += db_ln
        dcb_ref[...] += _rowsum(dc1)
        dbuf[0:t, :] = dc1

        za = z_ref[:, 0:512]
        sg = _sigmoid(z_ref[:, 512:1024])
        a = za * sg
        _shift_rows(dbuf, dsh, t)

        da = jnp.zeros((t, CONV_WIDTH), F32)
        for k in range(CONV_KERNEL):
            later = _window(dbuf, dsh, CONV_KERNEL - 1 - k, t)
            da = da + cw_ref[k:k + 1, :] * later
            dcw_ref[k:k + 1, :] += _rowsum(a * later)
        dza = da * sg
        dzg = da * za * (sg * (1.0 - sg))
        dz_ref[:, 0:512] = dza.astype(BF16)
        dz_ref[:, 512:1024] = dzg.astype(BF16)
        dbin_ref[:, 0:512] += _rowsum(dza)
        dbin_ref[:, 512:1024] += _rowsum(dzg)

        dgm = dmix[:, 512:1024]
        u, du_dz = _gelu_parts(z_ref[:, 1024:1536])
        gv, dgv_dz = _gelu_parts(z_ref[:, 1536:2048])
        vxh, vrs = _ln_stats(gv)
        ggv = gg_ref[...]
        v = vxh * ggv + gb_ref[...]
        low = _lane_is_low_head()
        v_lo = jnp.where(low, v, 0.0).astype(BF16)
        v_hi = jnp.where(low, 0.0, v).astype(BF16)
        _gm_mix(v_lo, v_hi, wpair_ref, bias_ref, mixed_ref, t)
        dzu = dgm * mixed_ref[...] * du_dz
        dm = dgm * u
        dm_lo = jnp.where(low, dm, 0.0).astype(BF16)
        dm_hi = jnp.where(low, 0.0, dm).astype(BF16)
        vb = v.astype(BF16)
        tril = (lax.broadcasted_iota(jnp.int32, (CHUNK, CHUNK), 1)
                <= lax.broadcasted_iota(jnp.int32, (CHUNK, CHUNK), 0))
        for n in range(t // CHUNK):
            rows = slice(n * CHUNK, (n + 1) * CHUNK)
            dbs_ref[...] += dm[rows, :]
            for j in range(GM_HEADS // 2):
                cols = slice(j * LANES, (j + 1) * LANES)
                stack = jnp.concatenate([dm_lo[rows, cols], dm_hi[rows, cols]], axis=0)
                dws = _dot_nt(stack, vb[rows, cols])
                dws_ref[2 * j] += jnp.where(tril, dws[0:CHUNK], 0.0)
                dws_ref[2 * j + 1] += jnp.where(tril, dws[CHUNK:2 * CHUNK], 0.0)
                dv_ref[rows, cols] = _dot(wpt_ref[j], stack)
        dgv, dg_gm, db_gm = _ln_bwd(dv_ref[...], vxh, vrs, ggv)
        dgg_ref[...] += dg_gm
        dgb_ref[...] += db_gm
        dzv = dgv * dgv_dz
        dz_ref[:, 1024:1536] = dzu.astype(BF16)
        dz_ref[:, 1536:2048] = dzv.astype(BF16)
        dbin_ref[:, 1024:1536] += _rowsum(dzu)
        dbin_ref[:, 1536:2048] += _rowsum(dzv)

        dhn = jnp.zeros((t, D_MODEL), F32)
        for j in range(4):
            dhn = dhn + _dot_nt(dz_ref[:, j * 512:(j + 1) * 512], wi_ref[j])
        xv = x_ref[...]
        dv, dg = _rms_bwd(dhn, xv, _rms_stats(xv), gm_ref[...])
        gx_ref[...] = dh1_ref[...] + dv
        dgm_ref[...] += dg

    rev = lambda w: pl.BlockSpec((t, w), lambda i: (nt - 1 - i, 0))
    const = lambda *shape: pl.BlockSpec(shape, lambda i: (0,) * len(shape))
    f32 = lambda *shape: jax.ShapeDtypeStruct(shape, F32)
    return _tied_call(
        body, after, name="seqmix_bwd", grid=(nt,),
        in_specs=[rev(D_MODEL), rev(D_MODEL), rev(2048), rev(CONV_WIDTH),
                  const(D_MODEL, D_MODEL), const(4, D_MODEL, 512), const(1, D_MODEL),
                  const(CONV_HALO, CONV_WIDTH), const(1, 512), const(1, 512), const(1, 512), const(1, 512),
                  const(4, CHUNK, 2 * CHUNK), const(4, CHUNK, 2 * CHUNK), const(CHUNK, GM_WIDTH)],
        out_specs=[rev(D_MODEL), rev(2048),
                   const(CONV_HALO, CONV_WIDTH), const(1, 512), const(1, 512), const(1, 512), const(1, 512),
                   const(1, 512), const(GM_HEADS, CHUNK, CHUNK), const(CHUNK, GM_WIDTH), const(1, 2048),
                   const(1, D_MODEL)],
        out_shape=[f32(s, D_MODEL), jax.ShapeDtypeStruct((s, 2048), BF16),
                   f32(CONV_HALO, CONV_WIDTH), f32(1, 512), f32(1, 512), f32(1, 512), f32(1, 512),
                   f32(1, 512), f32(GM_HEADS, CHUNK, CHUNK), f32(CHUNK, GM_WIDTH), f32(1, 2048),
                   f32(1, D_MODEL)],
        scratch_shapes=[pltpu.VMEM((t + CONV_HALO, CONV_WIDTH), F32),
                        pltpu.VMEM((SUBLANES - 1, t + CONV_HALO - SUBLANES, CONV_WIDTH), F32),
                        pltpu.VMEM((t, GM_WIDTH), F32), pltpu.VMEM((t, GM_WIDTH), F32)],
        compiler_params=_params(("arbitrary",)),
    )(dh1, x, z, c1, w_out, w_in, g_mix, cw, lng, lnb, gg, gb, wpair, wpair_t, bias)


def _head_bias_grad(dbs):
    def body(d_ref, o_ref):
        dv = d_ref[...]
        lane = lax.broadcasted_iota(jnp.int32, (CHUNK, LANES), 1)
        acc = jnp.zeros((CHUNK, LANES), F32)
        for h in range(GM_HEADS):
            sh = jnp.sum(dv[:, h * GM_HEAD_DIM:(h + 1) * GM_HEAD_DIM], axis=-1, keepdims=True)
            acc = acc + jnp.where(lane == h, sh, 0.0)
        o_ref[...] = acc

    return pl.pallas_call(body, name="head_bias_grad",
                          out_shape=jax.ShapeDtypeStruct((CHUNK, LANES), F32))(dbs)


def kernel(x, mem, norm_mix_g, w_in, b_in, conv_w, conv_b, conv_ln_g, conv_ln_b, gm_ln_g, gm_ln_b, gm_w_s, gm_b_s, w_out, norm_xa_g, mem_norm_g, xa_wq, xa_wkv, xa_wo, norm_ffn_g, ffn_w_gate_up, ffn_w_down, final_norm_g, loss_target, m_norm_mix_g, m_w_in, m_b_in, m_conv_w, m_conv_b, m_conv_ln_g, m_conv_ln_b, m_gm_ln_g, m_gm_ln_b, m_gm_w_s, m_gm_b_s, m_w_out, m_norm_xa_g, m_mem_norm_g, m_xa_wq, m_xa_wkv, m_xa_wo, m_norm_ffn_g, m_ffn_w_gate_up, m_ffn_w_down, m_final_norm_g, v_norm_mix_g, v_w_in, v_b_in, v_conv_w, v_conv_b, v_conv_ln_g, v_conv_ln_b, v_gm_ln_g, v_gm_ln_b, v_gm_w_s, v_gm_b_s, v_w_out, v_norm_xa_g, v_mem_norm_g, v_xa_wq, v_xa_wkv, v_xa_wo, v_norm_ffn_g, v_ffn_w_gate_up, v_ffn_w_down, v_final_norm_g):
    weights = dict(norm_mix_g=norm_mix_g, w_in=w_in, b_in=b_in, conv_w=conv_w, conv_b=conv_b, conv_ln_g=conv_ln_g,
                   conv_ln_b=conv_ln_b, gm_ln_g=gm_ln_g, gm_ln_b=gm_ln_b, gm_w_s=gm_w_s, gm_b_s=gm_b_s, w_out=w_out,
                   norm_xa_g=norm_xa_g, mem_norm_g=mem_norm_g, xa_wq=xa_wq, xa_wkv=xa_wkv, xa_wo=xa_wo,
                   norm_ffn_g=norm_ffn_g, ffn_w_gate_up=ffn_w_gate_up, ffn_w_down=ffn_w_down,
                   final_norm_g=final_norm_g)
    m_in = dict(norm_mix_g=m_norm_mix_g, w_in=m_w_in, b_in=m_b_in, conv_w=m_conv_w, conv_b=m_conv_b,
                conv_ln_g=m_conv_ln_g, conv_ln_b=m_conv_ln_b, gm_ln_g=m_gm_ln_g, gm_ln_b=m_gm_ln_b, gm_w_s=m_gm_w_s,
                gm_b_s=m_gm_b_s, w_out=m_w_out, norm_xa_g=m_norm_xa_g, mem_norm_g=m_mem_norm_g, xa_wq=m_xa_wq,
                xa_wkv=m_xa_wkv, xa_wo=m_xa_wo, norm_ffn_g=m_norm_ffn_g, ffn_w_gate_up=m_ffn_w_gate_up,
                ffn_w_down=m_ffn_w_down, final_norm_g=m_final_norm_g)
    v_in = dict(norm_mix_g=v_norm_mix_g, w_in=v_w_in, b_in=v_b_in, conv_w=v_conv_w, conv_b=v_conv_b,
                conv_ln_g=v_conv_ln_g, conv_ln_b=v_conv_ln_b, gm_ln_g=v_gm_ln_g, gm_ln_b=v_gm_ln_b, gm_w_s=v_gm_w_s,
                gm_b_s=v_gm_b_s, w_out=v_w_out, norm_xa_g=v_norm_xa_g, mem_norm_g=v_mem_norm_g, xa_wq=v_xa_wq,
                xa_wkv=v_xa_wkv, xa_wo=v_xa_wo, norm_ffn_g=v_norm_ffn_g, ffn_w_gate_up=v_ffn_w_gate_up,
                ffn_w_down=v_ffn_w_down, final_norm_g=v_final_norm_g)
    grads, delta, new_m, new_v = {}, {}, {}, {}

    s = x.shape[1]
    ts = _row_tile(s)
    tb = max(CHUNK, ts // 2)
    tw = 2 * ts if s % (2 * ts) == 0 and ts >= 512 else ts
    cx, cy, cc = _mesh_pos()
    chip = 2 * cx + cy
    pos = jnp.stack([chip, cc]).astype(jnp.int32)
    row = lambda a: a.reshape(1, -1)
    x2, mem2, tgt2 = x[0], mem[0], loss_target[0]

    big = dict(w_in=w_in, xa_wkv=xa_wkv, w_out=w_out, xa_wq=xa_wq, xa_wo=xa_wo,
               ffn_w_gate_up=ffn_w_gate_up, ffn_w_down=ffn_w_down)
    big_names = list(big)
    halves = lambda a: a.reshape(2, a.shape[0] // 2, a.shape[1])
    conv_w_pad = jnp.pad(conv_w, ((0, CONV_HALO - CONV_KERNEL), (0, 0)))
    first_names = ["w_in", "conv_w"]
    later_names = [nm for nm in big_names if nm != "w_in"]
    cast = dict(zip(first_names, _cast_into_slots([halves(w_in), halves(conv_w_pad)], pos, [BF16, F32], "cast_w_in")))
    cast.update(zip(later_names, _cast_into_slots([halves(big[nm]) for nm in later_names], pos,
                                                  [BF16] * len(later_names), "cast_" + later_names[0],
                                                  side_by_side=(later_names.index("ffn_w_gate_up"),))))

    def start_gather(names, after):
        return _gather_start([cast[nm] for nm in names], "gather_start_" + names[0], after)

    def land_gather(names, started, after):
        send_sems, recv_sems, bufs, _ = started
        return _gather_wait(send_sems, recv_sems, bufs, after, "gather_wait_" + names[0])

    as_weights = lambda names, bufs: dict(zip(names, (b.reshape(b.shape[0], -1, b.shape[-1]) for b in bufs)))

    def start_share(names, landed):
        return _pass_start(landed, "pass_start_" + names[0])

    def share_gather(names, passing, which, after):
        send_sems, recv_sems, bufs, _ = passing
        sems = lambda s: [s[3 * a + k] for a in which for k in range(3)]
        picked = [names[a] for a in which]
        return as_weights(picked, _gather_wait(sems(send_sems), sems(recv_sems), [bufs[a] for a in which], after,
                                               "pass_wait_" + picked[0], _pass_descriptors))

    tril = jnp.tril(jnp.ones((CHUNK, CHUNK), dtype=bool))
    ws = jnp.where(tril[None], gm_w_s, 0.0)
    wpair = jnp.concatenate([ws[0::2], ws[1::2]], axis=2).astype(BF16)
    ws_t = jnp.swapaxes(ws, 1, 2)
    wpair_t = jnp.concatenate([ws_t[0::2], ws_t[1::2]], axis=2).astype(BF16)
    bias = jnp.repeat(gm_b_s.T, GM_HEAD_DIM, axis=1)

    attn_names = ["xa_wkv", "w_out", "xa_wq", "xa_wo"]
    gather_first = start_gather(first_names, ())
    hn1 = _norm_in(x2, row(norm_mix_g), tw, after=(gather_first[3], wpair, wpair_t, bias))
    landed = land_gather(first_names, gather_first, [cast[nm] for nm in later_names] + [hn1])
    passing = start_share(first_names, landed)
    gather_attn = start_gather(attn_names, passing[3])
    gw = share_gather(first_names, passing, (0, 1), gather_attn[3])
    w_in_g = gw["w_in"]
    cw_g = jnp.concatenate([gw["conv_w"][k] for k in range(N_CHIPS)], axis=1)

    z, mix, c1 = _seqmix_fwd(hn1, w_in_g, row(b_in), cw_g, row(conv_b), row(conv_ln_g), row(conv_ln_b),
                             row(gm_ln_g), row(gm_ln_b), wpair, bias, ts)
    landed = land_gather(attn_names, gather_attn, mix)
    passing = start_share(attn_names, landed)
    gather_gu = start_gather(["ffn_w_gate_up"], passing[3])
    wkv_g = share_gather(attn_names, passing, (0,), gather_gu[3])["xa_wkv"]
    mn, kv = _mem_kv(mem2, row(mem_norm_g), wkv_g)
    gw = share_gather(attn_names, passing, (1, 2, 3), kv)
    w_out_g = gw["w_out"].reshape(D_MODEL, D_MODEL)
    wq_g = gw["xa_wq"].reshape(D_MODEL, D_MODEL)
    wo_g = gw["xa_wo"].reshape(D_MODEL, D_MODEL)
    h1, hn2, q, o, h2, hn3 = _attn_block_fwd(x2, mix, w_out_g, row(norm_xa_g), wq_g, kv, wo_g, row(norm_ffn_g), ts)
    landed = land_gather(["ffn_w_gate_up"], gather_gu, hn3)
    passing = start_share(["ffn_w_gate_up"], landed)
    gather_down = start_gather(["ffn_w_down"], passing[3])
    wgu_g = share_gather(["ffn_w_gate_up"], passing, (0,), gather_down[3])["ffn_w_gate_up"]
    gu, act = _ffn_up(hn3, wgu_g, ts)
    landed = land_gather(["ffn_w_down"], gather_down, act)
    wd_g = as_weights(["ffn_w_down"], _pass_to_sibling(landed, "pass_ffn_w_down"))["ffn_w_down"].reshape(
        FFN_HIDDEN, D_MODEL)
    dh3, dh3_b, sq, d_final_g = _ffn_down_loss(act, wd_g, h2, row(final_norm_g), tgt2, ts)
    loss_here = jnp.broadcast_to(0.5 * jnp.sum(sq) / D_MODEL, (1, 2, SUBLANES, LANES))

    def split(g, nm):
        r, c = big[nm].shape
        return g.reshape(N_CHIPS, 2, r // 2, c)

    def chip_sums(group, arrays, got):
        sums, parts = [None] * len(group), [None] * len(group)
        for blocks in (N_CHIPS, 1):
            idx = [i for i, a in enumerate(arrays) if a.shape[0] == blocks]
            if idx:
                out = _add_halves([arrays[i] for i in idx], [got[i] for i in idx], pos, "chip_sum_" + group[idx[0]],
                                  [F32 if group[i] == "loss" else BF16 for i in idx])
                for k, i in enumerate(idx):
                    sums[i], parts[i] = out[0][k], out[1][k]
        return sums, parts

    def start_swap(group, grads, after=()):
        return _swap_start([split(g, nm) for g, nm in zip(grads, group)], "swap_start_" + group[0], after)

    def start_exchange(group, swapping, after, landed):
        arrays, got = _swap_wait(*swapping[:3], after, "swap_wait_" + group[0])
        sums, parts = chip_sums(group, arrays, got)
        return _exchange_start(sums, parts, "exchange_start_" + group[0], landed)

    def wait_exchange(group, started, after):
        sems, sums, parts, _ = started
        return _exchange_wait(sems, sums, parts, after, "exchange_wait_" + group[0])

    def finish_exchange(group, started, after):
        return _sum_chips(wait_exchange(group, started, after), pos, "total_" + group[0])

    def join(group, after):
        return _join_halves([halves_of[nm] for nm in group], "join_halves_" + group[0], after)

    def start_join(group, after):
        return _join_start([halves_of[nm] for nm in group], "join_start_" + group[0], after)

    def end_join(group, joining, after):
        return _join_wait(*joining[:2], after, "join_wait_" + group[0])

    def update(group, joined, after=()):
        outs = _adamw([(weights[nm], j.reshape(big[nm].shape), m_in[nm], v_in[nm]) for nm, j in zip(group, joined)],
                      "adamw_" + group[0], after)
        for nm, out in zip(group, outs):
            grads[nm], delta[nm], new_m[nm], new_v[nm] = out
        return [new_v[nm] for nm in group]

    def join_and_update(group, after):
        return update(group, join(group, after))

    as3 = lambda a: a.reshape((1,) + a.shape)
    halves_of = {}

    g_down = _grad_w(act, as3(dh3_b), FFN_HALF, D_MODEL, "grad_ffn_w_down")
    group_a = ["ffn_w_down"]
    swap_a = start_swap(group_a, [g_down])
    dgu, dh2, dh2_b, d_ffn_g = _ffn_bwd(dh3, wd_g, gu, wgu_g, h2, row(norm_ffn_g), tb,
                                        after=swap_a[3])
    exch_a = start_exchange(group_a, swap_a, dh2, wd_g)
    g_gu = _grad_w(hn3, dgu, D_MODEL, FFN_HALF, "grad_ffn_w_gate_up", after=exch_a[3])
    halves_of.update(zip(group_a, finish_exchange(group_a, exch_a, g_gu)))

    group_b = ["ffn_w_gate_up"]
    swap_b = start_swap(group_b, [g_gu])
    dh1, dh1_b, dq, dkv, d_xa_g = _attn_bwd(dh2, wo_g, q, kv, wq_g, h1, row(norm_xa_g), ts, after=swap_b[3])
    exch_b = start_exchange(group_b, swap_b, dh1, [halves_of[nm] for nm in group_a])
    joining_a = start_join(group_a, exch_b[3])
    g_wkv, d_mem_g = _mem_kv_bwd(dkv, mn, wkv_g, mem2, row(mem_norm_g), after=joining_a[2])
    g_wo, g_wq, g_wout = _grad_w_square([(o, dh2_b), (hn2, dq), (mix, dh1_b)], "grad_xa_wo", after=joining_a[2])
    done_a = update(group_a, end_join(group_a, joining_a, (g_wkv, g_wo, g_wq, g_wout)))
    halves_of.update(zip(group_b, finish_exchange(group_b, exch_b, done_a)))
    joining_b = start_join(group_b, done_a)
    group_c = ["xa_wo", "xa_wq", "xa_wkv", "w_out"]
    swap_c = start_swap(group_c, [g_wo, g_wq, g_wkv, g_wout], joining_b[2])
    (gx, dz, d_cw, d_cb, d_lng, d_lnb, d_gg, d_gb, d_ws, d_bs_sum, d_bin, d_mix_g) = _seqmix_bwd(
        dh1, x2, z, c1, w_out_g, w_in_g, row(norm_mix_g), cw_g, row(conv_ln_g), row(conv_ln_b),
        row(gm_ln_g), row(gm_ln_b), wpair, wpair_t, bias, tb, after=swap_c[3])
    d_bs = _head_bias_grad(d_bs_sum)[:, :GM_HEADS].T
    exch_c = start_exchange(group_c, swap_c, dz, joining_b[2])
    g_win = _grad_w(hn1, as3(dz), D_MODEL, 1024, "grad_w_in", after=exch_c[3], shards=2)

    small_names = ["norm_mix_g", "b_in", "conv_w", "conv_b", "conv_ln_g", "conv_ln_b", "gm_ln_g", "gm_ln_b",
                   "gm_w_s", "gm_b_s", "norm_xa_g", "mem_norm_g", "norm_ffn_g", "final_norm_g"]
    d_cw_by_chip = jnp.swapaxes(d_cw.reshape(CONV_HALO, N_CHIPS, LANES), 0, 1).reshape(-1, LANES)
    small_grads = dict(norm_mix_g=d_mix_g, b_in=d_bin, conv_w=d_cw_by_chip, conv_b=d_cb, conv_ln_g=d_lng,
                       conv_ln_b=d_lnb, gm_ln_g=d_gg, gm_ln_b=d_gb, gm_w_s=d_ws, gm_b_s=d_bs, norm_xa_g=d_xa_g,
                       mem_norm_g=d_mem_g, norm_ffn_g=d_ffn_g, final_norm_g=d_final_g)

    def rows_form(a):
        a = a.reshape(-1, LANES)
        return jnp.pad(a, ((0, -a.shape[0] % SUBLANES), (0, 0)))

    pieces = [rows_form(small_grads[nm]) for nm in small_names]
    offsets, total = [], 0
    for p in pieces:
        offsets.append(total)
        total += p.shape[0]
    pack_rows = -(-total // 32) * 32
    small_pack = jnp.pad(jnp.concatenate(pieces, axis=0), ((0, pack_rows - total), (0, 0)))

    group_d = ["w_in", "small", "loss"]
    joined_b = end_join(group_b, joining_b, g_win)
    swap_d = _swap_start([split(g_win, "w_in")], "swap_start_w_in", joined_b)
    done_b = update(group_b, joined_b, swap_d[3])
    small_d = [small_pack.reshape(1, 2, pack_rows // 2, LANES), loss_here]
    got_small = _swap_halves(small_d, "swap_halves_small", done_b)
    arrays_d, got_d = _swap_wait(*swap_d[:3], got_small, "swap_wait_w_in")
    sums_d, parts_d = chip_sums(group_d, arrays_d + small_d, got_d + list(got_small))
    parts_c = wait_exchange(group_c, exch_c, sums_d)
    exch_d = _exchange_start(sums_d, parts_d, "exchange_start_w_in", parts_c)
    halves_of.update(zip(group_c, _sum_chips(parts_c, pos, "total_xa_wo", exch_d[3])))
    done_c = join_and_update(group_c, exch_d[3])
    halves_of.update(zip(group_d, finish_exchange(group_d, exch_d, done_c)))
    joined_d = _join_halves([halves_of[nm] for nm in group_d], "join_halves_w_in")
    loss = joined_d[2][0, 0, 0]
    grads["w_in"], delta["w_in"], new_m["w_in"], new_v["w_in"] = _adamw(
        [(w_in, joined_d[0].reshape(w_in.shape), m_w_in, v_w_in)], "adamw_w_in")[0]

    local_rows = lambda a, nm: a if nm == "conv_w" else a.reshape(-1, LANES)
    params = [tuple(local_rows(src[nm], nm) for src in (weights, m_in, v_in)) for nm in small_names]
    outs = _adamw_small(joined_d[1].reshape(pack_rows, LANES), pos, params, offsets, small_names.index("conv_w"))
    for k, nm in enumerate(small_names):
        for dst, a in zip((grads, delta, new_m, new_v), outs[4 * k:4 * k + 4]):
            dst[nm] = a

    order = ["norm_mix_g", "w_in", "b_in", "conv_w", "conv_b", "conv_ln_g", "conv_ln_b", "gm_ln_g", "gm_ln_b",
             "gm_w_s", "gm_b_s", "w_out", "norm_xa_g", "mem_norm_g", "xa_wq", "xa_wkv", "xa_wo", "norm_ffn_g",
             "ffn_w_gate_up", "ffn_w_down", "final_norm_g"]
    fit = lambda a, nm: a.reshape(weights[nm].shape)
    return (loss, gx.reshape(x.shape),
            *[fit(grads[nm], nm) for nm in order], *[fit(delta[nm], nm) for nm in order],
            *[fit(new_m[nm], nm) for nm in order], *[fit(new_v[nm], nm) for nm in order])
```

```python
import functools

import jax
import jax.numpy as jnp
from jax import lax
from jax.experimental import pallas as pl
from jax.experimental.pallas import tpu as pltpu

F32 = jnp.float32
BF16 = jnp.bfloat16

D_MODEL = 1024
CONV_WIDTH = 512
GM_WIDTH = 512
CONV_KERNEL = 31
CONV_HALO = 32
GRAD_ROWS = 2048
CHUNK = 128
GM_HEADS = 8
GM_HEAD_DIM = 64
XA_HEADS = 4
XA_HEAD_DIM = 256
FFN_HIDDEN = 2816
FFN_HALF = FFN_HIDDEN // 2
RMS_EPS = 1e-6
LN_EPS = 1e-5
N_CHIPS = 4
LANES = 128
SUBLANES = 8

ADAM_LR = 0.001
ADAM_B1 = 0.9
ADAM_B2 = 0.999
ADAM_EPS = 1e-08
ADAM_WD = 0.01
ADAM_STEP = 10

VMEM_LIMIT_BYTES = 56 * 1024 * 1024
MESH = pl.DeviceIdType.MESH
ANY = pl.BlockSpec(memory_space=pl.ANY)
HBM_SPEC = pl.BlockSpec(memory_space=pltpu.HBM)
SEM_SPEC = pl.BlockSpec(memory_space=pltpu.SEMAPHORE)

_NT = (((1,), (1,)), ((), ()))
_TN = (((0,), (0,)), ((), ()))
_GELU_C = 0.7978845608028654
_GELU_A = 0.044715


def _dot(a, b):
    return jnp.dot(a, b, preferred_element_type=F32)


def _dot_nt(a, b):
    return lax.dot_general(a, b, _NT, preferred_element_type=F32)


def _dot_tn(a, b):
    return lax.dot_general(a, b, _TN, preferred_element_type=F32)


def _mean(v):
    return jnp.mean(v, axis=-1, keepdims=True)


def _rowsum(v):
    return jnp.sum(v, axis=0, keepdims=True)


def _sigmoid(v):
    return 1.0 / (1.0 + jnp.exp(-v))


def _gelu_parts(v):
    v2 = v * v
    t = jnp.tanh(_GELU_C * (v + _GELU_A * v * v2))
    g = 0.5 * v * (1.0 + t)
    dg = 0.5 * (1.0 + t) + 0.5 * v * (1.0 - t * t) * (_GELU_C * (1.0 + 3.0 * _GELU_A * v2))
    return g, dg


def _rms_stats(v):
    return lax.rsqrt(_mean(v * v) + RMS_EPS)


def _rms_bwd(dy, v, r, g):
    n = v * r
    dn = dy * g
    dv = r * (dn - n * _mean(dn * n))
    return dv, _rowsum(dy * n)


def _ln_stats(v):
    mu = _mean(v)
    xc = v - mu
    rs = lax.rsqrt(_mean(xc * xc) + LN_EPS)
    return xc * rs, rs


def _ln_bwd(dy, xh, rs, g):
    dxh = dy * g
    dv = rs * (dxh - _mean(dxh) - xh * _mean(dxh * xh))
    return dv, _rowsum(dy * xh), _rowsum(dy)


def _params(sem):
    return pltpu.CompilerParams(dimension_semantics=sem, vmem_limit_bytes=VMEM_LIMIT_BYTES)


def _row_tile(s):
    return 512 if s % 512 == 0 and s >= 2048 else 128


def _mesh_pos():
    return lax.axis_index("x"), lax.axis_index("y"), lax.axis_index("c")


def _slot(buf, chip_idx, half):
    if buf.shape[0] == N_CHIPS:
        return buf.at[chip_idx, half]
    width = buf.shape[-1] // 2
    return buf.at[chip_idx // 2, half, :, pl.ds(pl.multiple_of((chip_idx % 2) * width, LANES), width)]


def _cast_into_slots(ws, pos, dtypes, name, side_by_side=()):
    n = len(ws)

    def body(pos_ref, *refs):
        for a in range(n):
            refs[n + a][0] = refs[a][...].astype(dtypes[a])

    def out_spec(a, w):
        if a in side_by_side:
            return pl.BlockSpec((1, 1) + w.shape[1:], lambda i, p: (p[0] // 2, i, 0, p[0] % 2))
        return pl.BlockSpec((1, 1) + w.shape[1:], lambda i, p: (p[0], i, 0, 0))

    def out_shape(a, w):
        if a in side_by_side:
            return (2, 2, w.shape[1], 2 * w.shape[2])
        return (N_CHIPS,) + w.shape

    return pl.pallas_call(
        body, name=name,
        grid_spec=pltpu.PrefetchScalarGridSpec(
            num_scalar_prefetch=1, grid=(2,),
            in_specs=[pl.BlockSpec((1,) + w.shape[1:], lambda i, p: (i, 0, 0)) for w in ws],
            out_specs=[out_spec(a, w) for a, w in enumerate(ws)]),
        out_shape=[jax.ShapeDtypeStruct(out_shape(a, w), dt) for a, (w, dt) in enumerate(zip(ws, dtypes))],
        compiler_params=_params(("parallel",)),
    )(pos, *ws)


def _adam_update(w, g, m, v):
    nm = ADAM_B1 * m + (1.0 - ADAM_B1) * g
    nv = ADAM_B2 * v + (1.0 - ADAM_B2) * (g * g)
    m_hat = nm / (1.0 - ADAM_B1 ** ADAM_STEP)
    v_hat = nv / (1.0 - ADAM_B2 ** ADAM_STEP)
    return -ADAM_LR * (m_hat / (jnp.sqrt(v_hat) + ADAM_EPS) + ADAM_WD * w), nm, nv


ADAM_STEPS = 4


def _adamw(quads, name, after=()):
    n = len(quads)

    def body(*refs):
        ins, outs = refs[:4 * n], refs[4 * n:]
        for a in range(n):
            w, g, m, v = (r[...] for r in ins[4 * a:4 * a + 4])
            outs[4 * a][...] = g
            outs[4 * a + 1][...], outs[4 * a + 2][...], outs[4 * a + 3][...] = _adam_update(w, g, m, v)

    specs = [pl.BlockSpec((q[0].shape[0] // ADAM_STEPS, q[0].shape[1]), lambda i: (i, 0)) for q in quads]
    out = _tied_call(
        body, after, name=name, grid=(ADAM_STEPS,),
        in_specs=[sp for sp in specs for _ in range(4)], out_specs=[sp for sp in specs for _ in range(4)],
        out_shape=[jax.ShapeDtypeStruct(q[0].shape, F32) for q in quads for _ in range(4)],
        compiler_params=_params(("parallel",)),
    )(*[a for q in quads for a in q])
    return [tuple(out[4 * a:4 * a + 4]) for a in range(n)]


def _adamw_small(gpack, pos, params, offsets, conv_at):
    n = len(params)

    def body(pos_ref, g_ref, *refs):
        ins, outs = refs[:3 * n], refs[3 * n:]
        for k in range(n):
            rows = params[k][0].shape[0]
            start = offsets[k]
            if k == conv_at:
                start = pl.multiple_of(start + pos_ref[0] * CONV_HALO, SUBLANES)
            g = g_ref[pl.ds(start, rows), :]
            outs[4 * k][...] = g
            outs[4 * k + 1][...], outs[4 * k + 2][...], outs[4 * k + 3][...] = _adam_update(
                ins[3 * k][...], g, ins[3 * k + 1][...], ins[3 * k + 2][...])

    flat = [a for p in params for a in p]
    vmem = pl.BlockSpec(memory_space=pltpu.VMEM)
    return pl.pallas_call(
        body, name="adamw_small",
        in_specs=[pl.BlockSpec(memory_space=pltpu.SMEM), vmem] + [vmem] * len(flat),
        out_specs=[vmem] * (4 * n),
        out_shape=[jax.ShapeDtypeStruct(p[0].shape, F32) for p in params for _ in range(4)],
    )(pos, gpack, *flat)


def _as_tuple(after):
    return tuple(after) if isinstance(after, (tuple, list)) else (after,)


def _tied_call(body, after, *, in_specs, **kwargs):
    after = _as_tuple(after)
    n_in, n_after = len(in_specs), len(after)

    def tied(*refs):
        body(*refs[:n_in], *refs[n_in + n_after:])

    call = pl.pallas_call(tied, in_specs=list(in_specs) + [ANY] * n_after, **kwargs)
    return lambda *operands: call(*operands, *after)


def _other_chips(x, y):
    return [(1 - x, y), (x, 1 - y), (1 - x, 1 - y)]


def _gather_descriptors(bufs, send_of, recv_of):
    x, y, c = _mesh_pos()
    me = 2 * x + y
    chips = _other_chips(x, y)
    sends, arrivals = [], []
    for a in range(len(bufs)):
        for k in range(3):
            ck = 2 * chips[k][0] + chips[k][1]

            def copy(slot, a=a, k=k):
                return pltpu.make_async_remote_copy(
                    src_ref=_slot(bufs[a], slot, c), dst_ref=_slot(bufs[a], slot, c),
                    send_sem=send_of(a, k), recv_sem=recv_of(a, k),
                    device_id=(*chips[k], c), device_id_type=MESH)

            sends.append(functools.partial(copy, me))
            arrivals.append(functools.partial(copy, ck))
    return sends, arrivals


def _gather_start(bufs, name, after=()):
    n = len(bufs)
    ns = 3 * n

    def body(*refs):
        sems = refs[n:n + 2 * ns]
        thru = refs[n + 2 * ns:2 * n + 2 * ns]
        token = refs[2 * n + 2 * ns]
        _chips_handshake()
        sends, _ = _gather_descriptors(thru, lambda a, k: sems[3 * a + k], lambda a, k: sems[ns + 3 * a + k])
        for cp in sends:
            cp().start()
        token[...] = jnp.zeros_like(token)

    held = [pltpu.with_memory_space_constraint(b, pltpu.HBM) for b in bufs]
    out = _tied_call(
        body, after, name=name,
        out_shape=(*[pltpu.SemaphoreType.DMA(())] * (2 * ns), *[pltpu.HBM(b.shape, b.dtype) for b in held],
                   jax.ShapeDtypeStruct((8, LANES), F32)),
        in_specs=[HBM_SPEC] * n,
        out_specs=(*[SEM_SPEC] * (2 * ns), *[HBM_SPEC] * n, pl.BlockSpec(memory_space=pltpu.VMEM)),
        input_output_aliases={i: 2 * ns + i for i in range(n)},
        compiler_params=pltpu.CompilerParams(has_side_effects=pltpu.SideEffectType.DATAFLOW_SIDE_EFFECTING,
                                             collective_id=CHIPS_COLLECTIVE_ID),
    )(*held)
    return list(out[:ns]), list(out[ns:2 * ns]), list(out[2 * ns:2 * ns + n]), out[2 * ns + n]


def _gather_wait(send_sems, recv_sems, bufs, after, name, descriptors=_gather_descriptors):
    n = len(bufs)
    ns = 3 * n

    def body(*refs):
        buf_ref = refs[:n]
        sem_ref = refs[n:n + 2 * ns]
        sends, arrivals = descriptors(buf_ref, lambda a, k: sem_ref[3 * a + k], lambda a, k: sem_ref[ns + 3 * a + k])
        for cp in sends:
            cp().wait_send()
        for cp in arrivals:
            cp().wait_recv()

    out = pl.pallas_call(
        body, name=name,
        out_shape=tuple(pltpu.HBM(b.shape, b.dtype) for b in bufs),
        in_specs=[HBM_SPEC] * n + [SEM_SPEC] * (2 * ns) + [ANY] * len(_as_tuple(after)),
        out_specs=tuple([HBM_SPEC] * n),
        input_output_aliases={i: i for i in range(n)},
        compiler_params=pltpu.CompilerParams(has_side_effects=pltpu.SideEffectType.DATAFLOW_SIDE_EFFECTING),
    )(*bufs, *send_sems, *recv_sems, *_as_tuple(after))
    return list(out)


SIBLING_COLLECTIVE_ID = 0


def _sibling_handshake():
    x, y, c = _mesh_pos()
    barrier = pltpu.get_barrier_semaphore()
    pl.semaphore_signal(barrier, inc=1, device_id=(x, y, 1 - c), device_id_type=MESH)
    pl.semaphore_wait(barrier, 1)


CHIPS_COLLECTIVE_ID = 1


def _chips_handshake():
    x, y, c = _mesh_pos()
    barrier = pltpu.get_barrier_semaphore()
    for chip in _other_chips(x, y):
        pl.semaphore_signal(barrier, inc=1, device_id=(*chip, c), device_id_type=MESH)
    pl.semaphore_wait(barrier, 3)


def _pass_descriptors(bufs, send_of, recv_of):
    x, y, c = _mesh_pos()
    chips = _other_chips(x, y)

    def half(a, k, which):
        ck = 2 * chips[k][0] + chips[k][1]
        return functools.partial(
            pltpu.make_async_remote_copy,
            src_ref=_slot(bufs[a], ck, which), dst_ref=_slot(bufs[a], ck, which),
            send_sem=send_of(a, k), recv_sem=recv_of(a, k),
            device_id=(x, y, 1 - c), device_id_type=MESH)

    pairs = [(a, k) for a in range(len(bufs)) for k in range(3)]
    return [half(a, k, c) for a, k in pairs], [half(a, k, 1 - c) for a, k in pairs]


def _pass_start(bufs, name, after=()):
    n = len(bufs)
    ns = 3 * n

    def body(*refs):
        sems = refs[n:n + 2 * ns]
        thru = refs[n + 2 * ns:2 * n + 2 * ns]
        token = refs[2 * n + 2 * ns]
        _sibling_handshake()
        sends, _ = _pass_descriptors(thru, lambda a, k: sems[3 * a + k], lambda a, k: sems[ns + 3 * a + k])
        for cp in sends:
            cp().start()
        token[...] = jnp.zeros_like(token)

    held = [pltpu.with_memory_space_constraint(b, pltpu.HBM) for b in bufs]
    out = _tied_call(
        body, after, name=name,
        out_shape=(*[pltpu.SemaphoreType.DMA(())] * (2 * ns), *[pltpu.HBM(b.shape, b.dtype) for b in held],
                   jax.ShapeDtypeStruct((8, LANES), F32)),
        in_specs=[HBM_SPEC] * n,
        out_specs=(*[SEM_SPEC] * (2 * ns), *[HBM_SPEC] * n, pl.BlockSpec(memory_space=pltpu.VMEM)),
        input_output_aliases={i: 2 * ns + i for i in range(n)},
        compiler_params=pltpu.CompilerParams(has_side_effects=pltpu.SideEffectType.DATAFLOW_SIDE_EFFECTING,
                                             collective_id=SIBLING_COLLECTIVE_ID),
    )(*held)
    return list(out[:ns]), list(out[ns:2 * ns]), list(out[2 * ns:2 * ns + n]), out[2 * ns + n]


def _pass_to_sibling(bufs, name, after=()):
    n = len(bufs)

    def body(*refs):
        outs = refs[n:2 * n]
        send_sem, recv_sem = refs[2 * n:]
        _sibling_handshake()
        sends, arrivals = _pass_descriptors(outs, lambda a, k: send_sem.at[a, k], lambda a, k: recv_sem.at[a, k])
        sends = [cp() for cp in sends]
        for cp in sends:
            cp.start()
        for cp in arrivals:
            cp().wait_recv()
        for cp in sends:
            cp.wait_send()

    return _tied_call(
        body, after, name=name,
        in_specs=[ANY] * n, out_specs=[ANY] * n,
        out_shape=[jax.ShapeDtypeStruct(b.shape, b.dtype) for b in bufs],
        input_output_aliases={a: a for a in range(n)},
        scratch_shapes=[pltpu.SemaphoreType.DMA((n, 3))] * 2,
        compiler_params=pltpu.CompilerParams(collective_id=SIBLING_COLLECTIVE_ID),
    )(*bufs)


def _swap_descriptors(grads, lands, send_of, recv_of):
    x, y, c = _mesh_pos()
    return [functools.partial(
        pltpu.make_async_remote_copy,
        src_ref=grads[a].at[:, pl.ds(1 - c, 1)], dst_ref=lands[a],
        send_sem=send_of(a), recv_sem=recv_of(a),
        device_id=(x, y, 1 - c), device_id_type=MESH) for a in range(len(grads))]


def _swap_halves(grads, name, after=()):
    n = len(grads)

    def body(*refs):
        ins, outs = refs[:n], refs[n:2 * n]
        send_sem, recv_sem = refs[2 * n:]
        _sibling_handshake()
        cps = [cp() for cp in _swap_descriptors(ins, outs, lambda a: send_sem.at[a], lambda a: recv_sem.at[a])]
        for cp in cps:
            cp.start()
        for cp in cps:
            cp.wait()

    out_shape = [jax.ShapeDtypeStruct((g.shape[0], 1) + g.shape[2:], g.dtype) for g in grads]
    return _tied_call(
        body, after, name=name,
        in_specs=[ANY] * n, out_specs=[ANY] * n, out_shape=out_shape,
        scratch_shapes=[pltpu.SemaphoreType.DMA((n,))] * 2,
        compiler_params=pltpu.CompilerParams(collective_id=SIBLING_COLLECTIVE_ID),
    )(*grads)


def _swap_start(grads, name, after=()):
    n, after = len(grads), _as_tuple(after)

    def body(*refs):
        outs = refs[2 * n + len(after):]
        sems, g_thru, l_thru, token = outs[:2 * n], outs[2 * n:3 * n], outs[3 * n:4 * n], outs[4 * n]
        _sibling_handshake()
        for cp in _swap_descriptors(g_thru, l_thru, lambda a: sems[a], lambda a: sems[n + a]):
            cp().start()
        token[...] = jnp.zeros_like(token)

    lands = [lax.empty((g.shape[0], 1) + g.shape[2:], g.dtype) for g in grads]
    held = [pltpu.with_memory_space_constraint(a, pltpu.HBM) for a in (*grads, *lands)]
    out = pl.pallas_call(
        body, name=name,
        out_shape=(*[pltpu.SemaphoreType.DMA(())] * (2 * n), *[pltpu.HBM(a.shape, a.dtype) for a in held],
                   jax.ShapeDtypeStruct((8, LANES), F32)),
        in_specs=[HBM_SPEC] * (2 * n) + [ANY] * len(after),
        out_specs=(*[SEM_SPEC] * (2 * n), *[HBM_SPEC] * (2 * n), pl.BlockSpec(memory_space=pltpu.VMEM)),
        input_output_aliases={i: 2 * n + i for i in range(2 * n)},
        compiler_params=pltpu.CompilerParams(has_side_effects=pltpu.SideEffectType.DATAFLOW_SIDE_EFFECTING,
                                             collective_id=SIBLING_COLLECTIVE_ID),
    )(*held, *after)
    return list(out[:2 * n]), list(out[2 * n:3 * n]), list(out[3 * n:4 * n]), out[4 * n]


def _swap_wait(sems, grads, lands, after, name):
    n = len(grads)

    def body(*refs):
        g_ref, l_ref = refs[:n], refs[n:2 * n]
        sem_ref = refs[2 * n:4 * n]
        for cp in _swap_descriptors(g_ref, l_ref, lambda a: sem_ref[a], lambda a: sem_ref[n + a]):
            cp().wait()

    out = pl.pallas_call(
        body, name=name,
        out_shape=tuple(pltpu.HBM(a.shape, a.dtype) for a in (*grads, *lands)),
        in_specs=[HBM_SPEC] * (2 * n) + [SEM_SPEC] * (2 * n) + [ANY] * len(_as_tuple(after)),
        out_specs=tuple([HBM_SPEC] * (2 * n)),
        input_output_aliases={i: i for i in range(2 * n)},
        compiler_params=pltpu.CompilerParams(has_side_effects=pltpu.SideEffectType.DATAFLOW_SIDE_EFFECTING),
    )(*grads, *lands, *sems, *_as_tuple(after))
    return list(out[:n]), list(out[n:])


def _add_halves(gs, gots, pos, name, dtypes):
    n = len(gs)
    j = gs[0].shape[0]

    def body(pos_ref, *refs):
        g_refs, r_refs = refs[:n], refs[n:2 * n]
        o_refs, p_refs = refs[2 * n:3 * n], refs[3 * n:]
        vals = [(g_refs[a][0, 0] + r_refs[a][0, 0]).astype(dtypes[a]) for a in range(n)]
        for a in range(n):
            o_refs[a][0] = vals[a]
        if j == 1:
            for a in range(n):
                p_refs[a][0] = vals[a]
        else:
            @pl.when(pl.program_id(0) == pos_ref[0])
            def _():
                for a in range(n):
                    p_refs[a][0] = vals[a]

    blk = lambda g: (1,) + g.shape[2:]
    out = pl.pallas_call(
        body, name=name,
        grid_spec=pltpu.PrefetchScalarGridSpec(
            num_scalar_prefetch=1, grid=(j,),
            in_specs=[pl.BlockSpec((1,) + blk(g), lambda i, p: (i, p[1], 0, 0)) for g in gs]
            + [pl.BlockSpec((1,) + blk(g), lambda i, p: (i, 0, 0, 0)) for g in gs],
            out_specs=[pl.BlockSpec(blk(g), lambda i, p: (i, 0, 0)) for g in gs]
            + [pl.BlockSpec(blk(g), lambda i, p: (p[0], 0, 0)) for g in gs]),
        out_shape=[jax.ShapeDtypeStruct((j,) + g.shape[2:], dt) for g, dt in zip(gs, dtypes)]
        + [jax.ShapeDtypeStruct((N_CHIPS,) + g.shape[2:], dt) for g, dt in zip(gs, dtypes)],
        compiler_params=_params(("arbitrary",)),
    )(pos, *gs, *gots)
    return list(out[:n]), list(out[n:])


def _exchange_descriptors(sums, parts, send_of, recv_of):
    x, y, c = _mesh_pos()
    me = 2 * x + y
    chips = _other_chips(x, y)
    sends, arrivals = [], []
    for a in range(len(sums)):
        for k in range(3):
            ck = 2 * chips[k][0] + chips[k][1]
            mine = sums[a].at[ck] if sums[a].shape[0] == N_CHIPS else sums[a].at[0]

            def copy(dst_slot, a=a, k=k, mine=mine):
                return pltpu.make_async_remote_copy(
                    src_ref=mine, dst_ref=parts[a].at[dst_slot],
                    send_sem=send_of(a, k), recv_sem=recv_of(a, k),
                    device_id=(*chips[k], c), device_id_type=MESH)

            sends.append(functools.partial(copy, me))
            arrivals.append(functools.partial(copy, ck))
    return sends, arrivals


def _exchange_start(sums, parts, name, after=()):
    n = len(sums)
    ns = 3 * n

    def body(*refs):
        sems = refs[2 * n:2 * n + 2 * ns]
        sums_thru = refs[2 * n + 2 * ns:3 * n + 2 * ns]
        parts_thru = refs[3 * n + 2 * ns:4 * n + 2 * ns]
        token = refs[4 * n + 2 * ns]
        _chips_handshake()
        sends, _ = _exchange_descriptors(sums_thru, parts_thru, lambda a, k: sems[3 * a + k],
                                         lambda a, k: sems[ns + 3 * a + k])
        for cp in sends:
            cp().start()
        token[...] = jnp.zeros_like(token)

    hbm = lambda a: pltpu.HBM(a.shape, a.dtype)
    held = [pltpu.with_memory_space_constraint(a, pltpu.HBM) for a in (*sums, *parts)]
    out = _tied_call(
        body, after, name=name,
        out_shape=(*[pltpu.SemaphoreType.DMA(())] * (2 * ns), *[hbm(a) for a in held],
                   jax.ShapeDtypeStruct((8, LANES), F32)),
        in_specs=[HBM_SPEC] * (2 * n),
        out_specs=(*[SEM_SPEC] * (2 * ns), *[HBM_SPEC] * (2 * n), pl.BlockSpec(memory_space=pltpu.VMEM)),
        input_output_aliases={i: 2 * ns + i for i in range(2 * n)},
        compiler_params=pltpu.CompilerParams(has_side_effects=pltpu.SideEffectType.DATAFLOW_SIDE_EFFECTING,
                                             collective_id=CHIPS_COLLECTIVE_ID),
    )(*held)
    return (list(out[:2 * ns]), list(out[2 * ns:2 * ns + n]), list(out[2 * ns + n:2 * ns + 2 * n]),
            out[2 * ns + 2 * n])


def _exchange_wait(sems, sums, parts, after, name):
    n = len(sums)
    ns = 3 * n

    def body(*refs):
        sums_ref, parts_ref = refs[:n], refs[n:2 * n]
        sem_ref = refs[2 * n:2 * n + 2 * ns]
        sends, arrivals = _exchange_descriptors(sums_ref, parts_ref, lambda a, k: sem_ref[3 * a + k],
                                                lambda a, k: sem_ref[ns + 3 * a + k])
        for cp in sends:
            cp().wait_send()
        for cp in arrivals:
            cp().wait_recv()

    hbm = lambda a: pltpu.HBM(a.shape, a.dtype)
    out = pl.pallas_call(
        body, name=name,
        out_shape=tuple(hbm(a) for a in (*sums, *parts)),
        in_specs=[HBM_SPEC] * (2 * n) + [SEM_SPEC] * (2 * ns) + [ANY] * len(_as_tuple(after)),
        out_specs=tuple([HBM_SPEC] * (2 * n)),
        input_output_aliases={i: i for i in range(2 * n)},
        compiler_params=pltpu.CompilerParams(has_side_effects=pltpu.SideEffectType.DATAFLOW_SIDE_EFFECTING),
    )(*sums, *parts, *sems, *_as_tuple(after))
    return list(out[n:])


def _sum_chips(parts, pos, name, after=()):
    n = len(parts)
    after = _as_tuple(after)

    def body(pos_ref, *refs):
        outs = refs[n + len(after):]
        for a in range(n):
            p_ref = refs[a]
            outs[a][0] = (((p_ref[0].astype(F32) + p_ref[1].astype(F32)) + p_ref[2].astype(F32))
                          + p_ref[3].astype(F32))

    out = pl.pallas_call(
        body, name=name,
        grid_spec=pltpu.PrefetchScalarGridSpec(
            num_scalar_prefetch=1, grid=(1,),
            in_specs=[pl.BlockSpec(p.shape, lambda i, q: (0, 0, 0)) for p in parts] + [ANY] * len(after),
            out_specs=[pl.BlockSpec((1,) + p.shape[1:], lambda i, q: (q[1], 0, 0)) for p in parts]),
        out_shape=[jax.ShapeDtypeStruct((2,) + p.shape[1:], F32) for p in parts],
        compiler_params=_params(("arbitrary",)),
    )(pos, *parts, *after)
    return list(out)


def _join_descriptors(fulls, send_of, recv_of):
    x, y, c = _mesh_pos()

    def half(a, which):
        return functools.partial(
            pltpu.make_async_remote_copy,
            src_ref=fulls[a].at[which], dst_ref=fulls[a].at[which],
            send_sem=send_of(a), recv_sem=recv_of(a),
            device_id=(x, y, 1 - c), device_id_type=MESH)

    return [half(a, c) for a in range(len(fulls))], [half(a, 1 - c) for a in range(len(fulls))]


def _join_start(fulls, name, after=()):
    n = len(fulls)

    def body(*refs):
        sems, thru, token = refs[n:3 * n], refs[3 * n:4 * n], refs[4 * n]
        _sibling_handshake()
        sends, _ = _join_descriptors(thru, lambda a: sems[a], lambda a: sems[n + a])
        for cp in sends:
            cp().start()
        token[...] = jnp.zeros_like(token)

    held = [pltpu.with_memory_space_constraint(f, pltpu.HBM) for f in fulls]
    out = _tied_call(
        body, after, name=name,
        out_shape=(*[pltpu.SemaphoreType.DMA(())] * (2 * n), *[pltpu.HBM(f.shape, f.dtype) for f in held],
                   jax.ShapeDtypeStruct((8, LANES), F32)),
        in_specs=[HBM_SPEC] * n,
        out_specs=(*[SEM_SPEC] * (2 * n), *[HBM_SPEC] * n, pl.BlockSpec(memory_space=pltpu.VMEM)),
        input_output_aliases={i: 2 * n + i for i in range(n)},
        compiler_params=pltpu.CompilerParams(has_side_effects=pltpu.SideEffectType.DATAFLOW_SIDE_EFFECTING,
                                             collective_id=SIBLING_COLLECTIVE_ID),
    )(*held)
    return list(out[:2 * n]), list(out[2 * n:3 * n]), out[3 * n]


def _join_wait(sems, fulls, after, name):
    n = len(fulls)

    def body(*refs):
        sem_ref = refs[n:3 * n]
        sends, arrivals = _join_descriptors(refs[:n], lambda a: sem_ref[a], lambda a: sem_ref[n + a])
        for cp in sends:
            cp().wait_send()
        for cp in arrivals:
            cp().wait_recv()

    out = pl.pallas_call(
        body, name=name,
        out_shape=tuple(pltpu.HBM(f.shape, f.dtype) for f in fulls),
        in_specs=[HBM_SPEC] * n + [SEM_SPEC] * (2 * n) + [ANY] * len(_as_tuple(after)),
        out_specs=tuple([HBM_SPEC] * n),
        input_output_aliases={i: i for i in range(n)},
        compiler_params=pltpu.CompilerParams(has_side_effects=pltpu.SideEffectType.DATAFLOW_SIDE_EFFECTING),
    )(*fulls, *sems, *_as_tuple(after))
    return list(out)


def _join_halves(fulls, name, after=()):
    n = len(fulls)

    def body(*refs):
        send_sem, recv_sem = refs[2 * n:]
        _sibling_handshake()
        sends, arrivals = _join_descriptors(refs[n:2 * n], lambda a: send_sem.at[a], lambda a: recv_sem.at[a])
        sends = [cp() for cp in sends]
        for cp in sends:
            cp.start()
        for cp in arrivals:
            cp().wait_recv()
        for cp in sends:
            cp.wait_send()

    out_shape = [jax.ShapeDtypeStruct(f.shape, f.dtype) for f in fulls]
    return _tied_call(
        body, after, name=name,
        in_specs=[ANY] * n, out_specs=[ANY] * n, out_shape=out_shape,
        input_output_aliases={a: a for a in range(n)},
        scratch_shapes=[pltpu.SemaphoreType.DMA((n,))] * 2,
        compiler_params=pltpu.CompilerParams(collective_id=SIBLING_COLLECTIVE_ID),
    )(*fulls)


def _norm_in(x, g, ts, after=()):
    s = x.shape[0]

    def body(x_ref, g_ref, hn_ref):
        xv = x_ref[...]
        hn_ref[...] = (xv * _rms_stats(xv) * g_ref[...]).astype(BF16)

    row = pl.BlockSpec((ts, D_MODEL), lambda i: (i, 0))
    return _tied_call(
        body, after, name="norm_in", grid=(s // ts,),
        in_specs=[row, pl.BlockSpec((1, D_MODEL), lambda i: (0, 0))], out_specs=row,
        out_shape=jax.ShapeDtypeStruct((s, D_MODEL), BF16),
        compiler_params=_params(("parallel",)),
    )(x, g)


def _shift_rows(buf, shifted, t):
    rows = t + CONV_HALO - SUBLANES
    for r in range(1, SUBLANES):
        shifted[r - 1, 0:rows, :] = buf[pl.ds(r, rows), :]


def _window(buf, shifted, offset, t):
    r = offset % SUBLANES
    if r == 0:
        return buf[pl.ds(offset, t), :]
    return shifted[r - 1, pl.ds(offset - r, t), :]


def _lane_is_low_head():
    lane = lax.broadcasted_iota(jnp.int32, (1, GM_WIDTH), 1)
    return (lane & GM_HEAD_DIM) == 0


def _gm_mix(v_lo, v_hi, wpair_ref, bias_ref, mixed_ref, t):
    for n in range(t // CHUNK):
        rows = slice(n * CHUNK, (n + 1) * CHUNK)
        for j in range(GM_HEADS // 2):
            cols = slice(j * LANES, (j + 1) * LANES)
            rhs = jnp.concatenate([v_lo[rows, cols], v_hi[rows, cols]], axis=0)
            mixed_ref[rows, cols] = _dot(wpair_ref[j], rhs) + bias_ref[:, cols]


def _seqmix_fwd(hn, w_in, b_in, cw, cb, lng, lnb, gg, gb, wpair, bias, t, after=()):
    s = hn.shape[0]

    def body(hn_ref, w_ref, b_ref, cw_ref, cb_ref, lng_ref, lnb_ref, gg_ref, gb_ref, wpair_ref, bias_ref,
             z_ref, mix_ref, c1_ref, abuf, ash, mixed_ref):
        i = pl.program_id(0)

        @pl.when(i == 0)
        def _():
            abuf[0:CONV_HALO, :] = jnp.zeros((CONV_HALO, CONV_WIDTH), F32)

        @pl.when(i > 0)
        def _():
            abuf[0:CONV_HALO, :] = abuf[t:t + CONV_HALO, :]

        hv = hn_ref[...]
        for j in range(4):
            cols = slice(j * 512, (j + 1) * 512)
            z_ref[:, cols] = _dot(hv, w_ref[j]) + b_ref[:, cols]

        abuf[CONV_HALO:, :] = z_ref[:, 0:512] * _sigmoid(z_ref[:, 512:1024])
        _shift_rows(abuf, ash, t)
        acc = jnp.zeros((t, CONV_WIDTH), F32)
        for k in range(CONV_KERNEL):
            acc = acc + cw_ref[k:k + 1, :] * _window(abuf, ash, CONV_HALO - (CONV_KERNEL - 1) + k, t)
        c1 = acc + cb_ref[...]
        c1_ref[...] = c1
        xh, _ = _ln_stats(c1)
        ln = xh * lng_ref[...] + lnb_ref[...]
        mix_ref[:, 0:512] = (ln * _sigmoid(ln)).astype(BF16)

        u, _ = _gelu_parts(z_ref[:, 1024:1536])
        gv, _ = _gelu_parts(z_ref[:, 1536:2048])
        vxh, _ = _ln_stats(gv)
        v = vxh * gg_ref[...] + gb_ref[...]
        low = _lane_is_low_head()
        v_lo = jnp.where(low, v, 0.0).astype(BF16)
        v_hi = jnp.where(low, 0.0, v).astype(BF16)
        _gm_mix(v_lo, v_hi, wpair_ref, bias_ref, mixed_ref, t)
        mix_ref[:, 512:1024] = (u * mixed_ref[...]).astype(BF16)

    vec = lambda n: pl.BlockSpec((1, n), lambda i: (0, 0))
    return _tied_call(
        body, after, name="seqmix_fwd", grid=(s // t,),
        in_specs=[pl.BlockSpec((t, D_MODEL), lambda i: (i, 0)),
                  pl.BlockSpec((4, D_MODEL, 512), lambda i: (0, 0, 0)), vec(2048),
                  pl.BlockSpec((CONV_HALO, CONV_WIDTH), lambda i: (0, 0)),
                  vec(512), vec(512), vec(512), vec(512), vec(512),
                  pl.BlockSpec((4, CHUNK, 2 * CHUNK), lambda i: (0, 0, 0)),
                  pl.BlockSpec((CHUNK, GM_WIDTH), lambda i: (0, 0))],
        out_specs=[pl.BlockSpec((t, 2048), lambda i: (i, 0)),
                   pl.BlockSpec((t, D_MODEL), lambda i: (i, 0)),
                   pl.BlockSpec((t, CONV_WIDTH), lambda i: (i, 0))],
        out_shape=[jax.ShapeDtypeStruct((s, 2048), F32), jax.ShapeDtypeStruct((s, D_MODEL), BF16),
                   jax.ShapeDtypeStruct((s, CONV_WIDTH), F32)],
        scratch_shapes=[pltpu.VMEM((t + CONV_HALO, CONV_WIDTH), F32),
                        pltpu.VMEM((SUBLANES - 1, t + CONV_HALO - SUBLANES, CONV_WIDTH), F32),
                        pltpu.VMEM((t, GM_WIDTH), F32)],
        compiler_params=_params(("arbitrary",)),
    )(hn, w_in, b_in, cw, cb, lng, lnb, gg, gb, wpair, bias)


def _mem_kv(mem, g, wkv):
    m = mem.shape[0]

    def body(mem_ref, g_ref, w_ref, mn_ref, kv_ref):
        mv = mem_ref[...]
        mn = (mv * _rms_stats(mv) * g_ref[...]).astype(BF16)
        mn_ref[...] = mn
        for j in range(4):
            kv_ref[:, j * 512:(j + 1) * 512] = _dot(mn, w_ref[j]).astype(BF16)

    return pl.pallas_call(
        body, name="mem_kv",
        out_shape=[jax.ShapeDtypeStruct((m, D_MODEL), BF16), jax.ShapeDtypeStruct((m, 2 * D_MODEL), BF16)],
        compiler_params=pltpu.CompilerParams(vmem_limit_bytes=VMEM_LIMIT_BYTES),
    )(mem, g, wkv)


def _softmax_rows(sc):
    e = jnp.exp(sc - jnp.max(sc, axis=-1, keepdims=True))
    return e / jnp.sum(e, axis=-1, keepdims=True)


def _attn_block_fwd(x, mix, w_out, g_xa, wq, kv, wo, g_ffn, ts, after=()):
    s, m = x.shape[0], kv.shape[0]
    scale = XA_HEAD_DIM ** -0.5

    def body(x_ref, mix_ref, wout_ref, gxa_ref, wq_ref, kv_ref, wo_ref, gffn_ref,
             h1_ref, hn2_ref, q_ref, o_ref, h2_ref, hn3_ref):
        h1 = x_ref[...] + _dot(mix_ref[...], wout_ref[...])
        h1_ref[...] = h1
        hn2 = (h1 * _rms_stats(h1) * gxa_ref[...]).astype(BF16)
        hn2_ref[...] = hn2
        q_ref[...] = _dot(hn2, wq_ref[...]).astype(BF16)
        for h in range(XA_HEADS):
            cols = slice(h * XA_HEAD_DIM, (h + 1) * XA_HEAD_DIM)
            vcols = slice(D_MODEL + h * XA_HEAD_DIM, D_MODEL + (h + 1) * XA_HEAD_DIM)
            p = _softmax_rows(_dot_nt(q_ref[:, cols], kv_ref[:, cols]) * scale)
            o_ref[:, cols] = _dot(p.astype(BF16), kv_ref[:, vcols]).astype(BF16)
        h2 = h1 + _dot(o_ref[...], wo_ref[...])
        h2_ref[...] = h2
        hn3_ref[...] = (h2 * _rms_stats(h2) * gffn_ref[...]).astype(BF16)

    row = pl.BlockSpec((ts, D_MODEL), lambda i: (i, 0))
    full = pl.BlockSpec((D_MODEL, D_MODEL), lambda i: (0, 0))
    vec = pl.BlockSpec((1, D_MODEL), lambda i: (0, 0))
    f32 = jax.ShapeDtypeStruct((s, D_MODEL), F32)
    bf16 = jax.ShapeDtypeStruct((s, D_MODEL), BF16)
    return _tied_call(
        body, after, name="attn_block_fwd", grid=(s // ts,),
        in_specs=[row, row, full, vec, full, pl.BlockSpec((m, 2 * D_MODEL), lambda i: (0, 0)), full, vec],
        out_specs=[row] * 6,
        out_shape=[f32, bf16, bf16, bf16, f32, bf16],
        compiler_params=_params(("parallel",)),
    )(x, mix, w_out, g_xa, wq, kv, wo, g_ffn)


_FFN_CHUNKS = (slice(0, 8 * LANES), slice(8 * LANES, 16 * LANES), slice(16 * LANES, FFN_HIDDEN))


def _ffn_up(hn, wgu, ts, after=()):
    s = hn.shape[0]

    def body(hn_ref, w_ref, gu_ref, act_ref):
        hv = hn_ref[...]
        for cols in _FFN_CHUNKS:
            gate = _dot(hv, w_ref[0, :, cols])
            up = _dot(hv, w_ref[1, :, cols])
            gu_ref[0, :, cols] = gate.astype(BF16)
            gu_ref[1, :, cols] = up.astype(BF16)
            act_ref[:, cols] = (gate * _sigmoid(gate) * up).astype(BF16)

    return _tied_call(
        body, after, name="ffn_up", grid=(s // ts,),
        in_specs=[pl.BlockSpec((ts, D_MODEL), lambda i: (i, 0)),
                  pl.BlockSpec((2, D_MODEL, FFN_HIDDEN), lambda i: (0, 0, 0))],
        out_specs=[pl.BlockSpec((2, ts, FFN_HIDDEN), lambda i: (0, i, 0)),
                   pl.BlockSpec((ts, FFN_HIDDEN), lambda i: (i, 0))],
        out_shape=[jax.ShapeDtypeStruct((2, s, FFN_HIDDEN), BF16), jax.ShapeDtypeStruct((s, FFN_HIDDEN), BF16)],
        compiler_params=_params(("parallel",)),
    )(hn, wgu)


def _ffn_down_loss(act, wd, h2, g, target, ts):
    s = act.shape[0]

    def body(act_ref, wd_ref, h2_ref, g_ref, t_ref, dh_ref, sq_ref, dg_ref):
        @pl.when(pl.program_id(0) == 0)
        def _():
            sq_ref[...] = jnp.zeros_like(sq_ref)
            dg_ref[...] = jnp.zeros_like(dg_ref)

        h3 = h2_ref[...] + _dot(act_ref[...], wd_ref[...])
        r = _rms_stats(h3)
        gv = g_ref[...]
        diff = h3 * r * gv - t_ref[...]
        sq_ref[...] += _rowsum(diff * diff)
        dh, dg = _rms_bwd(diff / D_MODEL, h3, r, gv)
        dh_ref[...] = dh
        dg_ref[...] += dg

    row = pl.BlockSpec((ts, D_MODEL), lambda i: (i, 0))
    vec = pl.BlockSpec((1, D_MODEL), lambda i: (0, 0))
    return pl.pallas_call(
        body, name="ffn_down_loss", grid=(s // ts,),
        in_specs=[pl.BlockSpec((ts, FFN_HIDDEN), lambda i: (i, 0)),
                  pl.BlockSpec((FFN_HIDDEN, D_MODEL), lambda i: (0, 0)), row, vec, row],
        out_specs=[row, vec, vec],
        out_shape=[jax.ShapeDtypeStruct((s, D_MODEL), F32),
                   jax.ShapeDtypeStruct((1, D_MODEL), F32), jax.ShapeDtypeStruct((1, D_MODEL), F32)],
        compiler_params=_params(("arbitrary",)),
    )(act, wd, h2, g, target)


def _grad_w(a, b, tk, tn, name, after=(), shards=1):
    s, k = a.shape
    gb, _, n = b.shape
    nblk = n // tn
    ws = tn // shards
    tsr = GRAD_ROWS if s % GRAD_ROWS == 0 else s

    def body(a_ref, b_ref, o_ref):
        part = _dot_tn(a_ref[...], b_ref[0].astype(BF16))

        @pl.when(pl.program_id(2) == 0)
        def _():
            for j in range(shards):
                o_ref[j] = part[:, j * ws:(j + 1) * ws]

        @pl.when(pl.program_id(2) > 0)
        def _():
            for j in range(shards):
                o_ref[j] += part[:, j * ws:(j + 1) * ws]

    return _tied_call(
        body, after, name=name, grid=(gb * nblk, k // tk, s // tsr),
        in_specs=[pl.BlockSpec((tsr, tk), lambda ni, ki, si: (si, ki)),
                  pl.BlockSpec((1, tsr, tn), lambda ni, ki, si: (ni // nblk, si, ni % nblk))],
        out_specs=pl.BlockSpec((shards, tk, ws), lambda ni, ki, si: (ni, ki, 0)),
        out_shape=jax.ShapeDtypeStruct((gb * nblk * shards, k, ws), F32),
        compiler_params=_params(("parallel", "parallel", "arbitrary")),
    )(a, b)


def _grad_w_square(pairs, name, after=()):
    n = len(pairs)
    s = pairs[0][0].shape[0]
    tsr = GRAD_ROWS // 2 if s % (GRAD_ROWS // 2) == 0 else s

    def body(*refs):
        ins, outs = refs[:2 * n], refs[2 * n:]
        parts = [_dot_tn(ins[2 * a][...], ins[2 * a + 1][...]) for a in range(n)]

        @pl.when(pl.program_id(0) == 0)
        def _():
            for a in range(n):
                outs[a][...] = parts[a]

        @pl.when(pl.program_id(0) > 0)
        def _():
            for a in range(n):
                outs[a][...] += parts[a]

    row = pl.BlockSpec((tsr, D_MODEL), lambda i: (i, 0))
    return _tied_call(
        body, after, name=name, grid=(s // tsr,),
        in_specs=[row] * (2 * n), out_specs=[pl.BlockSpec((D_MODEL, D_MODEL), lambda i: (0, 0))] * n,
        out_shape=[jax.ShapeDtypeStruct((D_MODEL, D_MODEL), F32)] * n,
        compiler_params=_params(("arbitrary",)),
    )(*[x for p in pairs for x in p])


def _ffn_bwd(dh3, wd, gu, wgu, h2, g, t, after=()):
    s = dh3.shape[0]

    def body(dh3_ref, wd_ref, gu_ref, w_ref, h2_ref, g_ref, dgu_ref, dh2_ref, dh2b_ref, dg_ref):
        @pl.when(pl.program_id(0) == 0)
        def _():
            dg_ref[...] = jnp.zeros_like(dg_ref)

        dh3v = dh3_ref[...]
        dhb = dh3v.astype(BF16)
        for cols in _FFN_CHUNKS:
            dact = _dot_nt(dhb, wd_ref[cols, :])
            gate, up = gu_ref[0, :, cols].astype(F32), gu_ref[1, :, cols].astype(F32)
            sg = _sigmoid(gate)
            dgu_ref[0, :, cols] = (dact * up * (sg * (1.0 + gate * (1.0 - sg)))).astype(BF16)
            dgu_ref[1, :, cols] = (dact * (gate * sg)).astype(BF16)
        dhn = _dot_nt(dgu_ref[0], w_ref[0]) + _dot_nt(dgu_ref[1], w_ref[1])
        h2 = h2_ref[...]
        dv, dg = _rms_bwd(dhn, h2, _rms_stats(h2), g_ref[...])
        dh2 = dh3v + dv
        dh2_ref[...] = dh2
        dh2b_ref[...] = dh2.astype(BF16)
        dg_ref[...] += dg

    row = pl.BlockSpec((t, D_MODEL), lambda i: (i, 0))
    wide = pl.BlockSpec((2, t, FFN_HIDDEN), lambda i: (0, i, 0))
    vec = pl.BlockSpec((1, D_MODEL), lambda i: (0, 0))
    return _tied_call(
        body, after, name="ffn_bwd", grid=(s // t,),
        in_specs=[row, pl.BlockSpec((FFN_HIDDEN, D_MODEL), lambda i: (0, 0)), wide,
                  pl.BlockSpec((2, D_MODEL, FFN_HIDDEN), lambda i: (0, 0, 0)), row, vec],
        out_specs=[wide, row, row, vec],
        out_shape=[jax.ShapeDtypeStruct((2, s, FFN_HIDDEN), BF16), jax.ShapeDtypeStruct((s, D_MODEL), F32),
                   jax.ShapeDtypeStruct((s, D_MODEL), BF16), jax.ShapeDtypeStruct((1, D_MODEL), F32)],
        compiler_params=_params(("arbitrary",)),
    )(dh3, wd, gu, wgu, h2, g)


def _attn_bwd(dh2, wo, q, kv, wq, h1, g, ts, after=()):
    s, m = q.shape[0], kv.shape[0]
    scale = XA_HEAD_DIM ** -0.5

    def body(dh2_ref, wo_ref, q_ref, kv_ref, wq_ref, h1_ref, g_ref, dh1_ref, dh1b_ref, dq_ref, dkv_ref, dg_ref):
        @pl.when(pl.program_id(0) == 0)
        def _():
            dkv_ref[...] = jnp.zeros_like(dkv_ref)
            dg_ref[...] = jnp.zeros_like(dg_ref)

        do = _dot_nt(dh2_ref[...].astype(BF16), wo_ref[...]).astype(BF16)
        for h in range(XA_HEADS):
            cols = slice(h * XA_HEAD_DIM, (h + 1) * XA_HEAD_DIM)
            vcols = slice(D_MODEL + h * XA_HEAD_DIM, D_MODEL + (h + 1) * XA_HEAD_DIM)
            qh, kh, vh, doh = q_ref[:, cols], kv_ref[:, cols], kv_ref[:, vcols], do[:, cols]
            p = _softmax_rows(_dot_nt(qh, kh) * scale)
            dp = _dot_nt(doh, vh)
            ds = (p * (dp - jnp.sum(dp * p, axis=-1, keepdims=True)) * scale).astype(BF16)
            dq_ref[:, cols] = _dot(ds, kh).astype(BF16)
            dkv_ref[:, cols] += _dot_tn(ds, qh)
            dkv_ref[:, vcols] += _dot_tn(p.astype(BF16), doh)
        dhn = _dot_nt(dq_ref[...], wq_ref[...])
        h1 = h1_ref[...]
        dv, dg = _rms_bwd(dhn, h1, _rms_stats(h1), g_ref[...])
        dh1 = dh2_ref[...] + dv
        dh1_ref[...] = dh1
        dh1b_ref[...] = dh1.astype(BF16)
        dg_ref[...] += dg

    row = pl.BlockSpec((ts, D_MODEL), lambda i: (i, 0))
    full = pl.BlockSpec((D_MODEL, D_MODEL), lambda i: (0, 0))
    kvs = pl.BlockSpec((m, 2 * D_MODEL), lambda i: (0, 0))
    vec = pl.BlockSpec((1, D_MODEL), lambda i: (0, 0))
    return _tied_call(
        body, after, name="attn_bwd", grid=(s // ts,),
        in_specs=[row, full, row, kvs, full, row, vec],
        out_specs=[row, row, row, kvs, vec],
        out_shape=[jax.ShapeDtypeStruct((s, D_MODEL), F32), jax.ShapeDtypeStruct((s, D_MODEL), BF16),
                   jax.ShapeDtypeStruct((s, D_MODEL), BF16),
                   jax.ShapeDtypeStruct((m, 2 * D_MODEL), F32), jax.ShapeDtypeStruct((1, D_MODEL), F32)],
        compiler_params=_params(("arbitrary",)),
    )(dh2, wo, q, kv, wq, h1, g)


def _mem_kv_bwd(dkv, mn, wkv, mem, g, after=()):
    m = mem.shape[0]

    def body(dkv_ref, mn_ref, w_ref, mem_ref, g_ref, dw_ref, dg_ref):
        dmn = jnp.zeros((m, D_MODEL), F32)
        mn = mn_ref[...]
        for j in range(4):
            dj = dkv_ref[:, j * 512:(j + 1) * 512].astype(BF16)
            dw_ref[j] = _dot_tn(mn, dj)
            dmn = dmn + _dot_nt(dj, w_ref[j])
        mv = mem_ref[...]
        dg_ref[...] = _rowsum(dmn * (mv * _rms_stats(mv)))

    return _tied_call(
        body, after, name="mem_kv_bwd", in_specs=[pl.BlockSpec(memory_space=pltpu.VMEM)] * 5,
        out_shape=[jax.ShapeDtypeStruct((4, D_MODEL, 512), F32), jax.ShapeDtypeStruct((1, D_MODEL), F32)],
        compiler_params=pltpu.CompilerParams(vmem_limit_bytes=VMEM_LIMIT_BYTES),
    )(dkv, mn, wkv, mem, g)


def _seqmix_bwd(dh1, x, z, c1, w_out, w_in, g_mix, cw, lng, lnb, gg, gb, wpair, wpair_t, bias, t, after=()):
    s = x.shape[0]
    nt = s // t

    def body(dh1_ref, x_ref, z_ref, c1_ref, wo_ref, wi_ref, gm_ref, cw_ref, lng_ref, lnb_ref,
             gg_ref, gb_ref, wpair_ref, wpt_ref, bias_ref,
             gx_ref, dz_ref, dcw_ref, dcb_ref, dlng_ref, dlnb_ref, dgg_ref, dgb_ref, dws_ref, dbs_ref,
             dbin_ref, dgm_ref, dbuf, dsh, mixed_ref, dv_ref):
        i = pl.program_id(0)
        accs = (dcw_ref, dcb_ref, dlng_ref, dlnb_ref, dgg_ref, dgb_ref, dws_ref, dbs_ref, dbin_ref, dgm_ref)

        @pl.when(i == 0)
        def _():
            for r in accs:
                r[...] = jnp.zeros_like(r)
            dbuf[t:t + CONV_HALO, :] = jnp.zeros((CONV_HALO, CONV_WIDTH), F32)

        @pl.when(i > 0)
        def _():
            dbuf[t:t + CONV_HALO, :] = dbuf[0:CONV_HALO, :]

        dmix = _dot_nt(dh1_ref[...].astype(BF16), wo_ref[...])

        xh, rs = _ln_stats(c1_ref[...])
        lng = lng_ref[...]
        ln = xh * lng + lnb_ref[...]
        sl = _sigmoid(ln)
        dln = dmix[:, 0:512] * (sl * (1.0 + ln * (1.0 - sl)))
        dc1, dg_ln, db_ln = _ln_bwd(dln, xh, rs, lng)
        dlng_ref[...] += dg_ln
        dlnb_ref[...] += db_ln
        dcb_ref[...] += _rowsum(dc1)
        dbuf[0:t, :] = dc1

        za = z_ref[:, 0:512]
        sg = _sigmoid(z_ref[:, 512:1024])
        a = za * sg
        _shift_rows(dbuf, dsh, t)

        da = jnp.zeros((t, CONV_WIDTH), F32)
        for k in range(CONV_KERNEL):
            later = _window(dbuf, dsh, CONV_KERNEL - 1 - k, t)
            da = da + cw_ref[k:k + 1, :] * later
            dcw_ref[k:k + 1, :] += _rowsum(a * later)
        dza = da * sg
        dzg = da * za * (sg * (1.0 - sg))
        dz_ref[:, 0:512] = dza.astype(BF16)
        dz_ref[:, 512:1024] = dzg.astype(BF16)
        dbin_ref[:, 0:512] += _rowsum(dza)
        dbin_ref[:, 512:1024] += _rowsum(dzg)

        dgm = dmix[:, 512:1024]
        u, du_dz = _gelu_parts(z_ref[:, 1024:1536])
        gv, dgv_dz = _gelu_parts(z_ref[:, 1536:2048])
        vxh, vrs = _ln_stats(gv)
        ggv = gg_ref[...]
        v = vxh * ggv + gb_ref[...]
        low = _lane_is_low_head()
        v_lo = jnp.where(low, v, 0.0).astype(BF16)
        v_hi = jnp.where(low, 0.0, v).astype(BF16)
        _gm_mix(v_lo, v_hi, wpair_ref, bias_ref, mixed_ref, t)
        dzu = dgm * mixed_ref[...] * du_dz
        dm = dgm * u
        dm_lo = jnp.where(low, dm, 0.0).astype(BF16)
        dm_hi = jnp.where(low, 0.0, dm).astype(BF16)
        vb = v.astype(BF16)
        tril = (lax.broadcasted_iota(jnp.int32, (CHUNK, CHUNK), 1)
                <= lax.broadcasted_iota(jnp.int32, (CHUNK, CHUNK), 0))
        for n in range(t // CHUNK):
            rows = slice(n * CHUNK, (n + 1) * CHUNK)
            dbs_ref[...] += dm[rows, :]
            for j in range(GM_HEADS // 2):
                cols = slice(j * LANES, (j + 1) * LANES)
                stack = jnp.concatenate([dm_lo[rows, cols], dm_hi[rows, cols]], axis=0)
                dws = _dot_nt(stack, vb[rows, cols])
                dws_ref[2 * j] += jnp.where(tril, dws[0:CHUNK], 0.0)
                dws_ref[2 * j + 1] += jnp.where(tril, dws[CHUNK:2 * CHUNK], 0.0)
                dv_ref[rows, cols] = _dot(wpt_ref[j], stack)
        dgv, dg_gm, db_gm = _ln_bwd(dv_ref[...], vxh, vrs, ggv)
        dgg_ref[...] += dg_gm
        dgb_ref[...] += db_gm
        dzv = dgv * dgv_dz
        dz_ref[:, 1024:1536] = dzu.astype(BF16)
        dz_ref[:, 1536:2048] = dzv.astype(BF16)
        dbin_ref[:, 1024:1536] += _rowsum(dzu)
        dbin_ref[:, 1536:2048] += _rowsum(dzv)

        dhn = jnp.zeros((t, D_MODEL), F32)
        for j in range(4):
            dhn = dhn + _dot_nt(dz_ref[:, j * 512:(j + 1) * 512], wi_ref[j])
        xv = x_ref[...]
        dv, dg = _rms_bwd(dhn, xv, _rms_stats(xv), gm_ref[...])
        gx_ref[...] = dh1_ref[...] + dv
        dgm_ref[...] += dg

    rev = lambda w: pl.BlockSpec((t, w), lambda i: (nt - 1 - i, 0))
    const = lambda *shape: pl.BlockSpec(shape, lambda i: (0,) * len(shape))
    f32 = lambda *shape: jax.ShapeDtypeStruct(shape, F32)
    return _tied_call(
        body, after, name="seqmix_bwd", grid=(nt,),
        in_specs=[rev(D_MODEL), rev(D_MODEL), rev(2048), rev(CONV_WIDTH),
                  const(D_MODEL, D_MODEL), const(4, D_MODEL, 512), const(1, D_MODEL),
                  const(CONV_HALO, CONV_WIDTH), const(1, 512), const(1, 512), const(1, 512), const(1, 512),
                  const(4, CHUNK, 2 * CHUNK), const(4, CHUNK, 2 * CHUNK), const(CHUNK, GM_WIDTH)],
        out_specs=[rev(D_MODEL), rev(2048),
                   const(CONV_HALO, CONV_WIDTH), const(1, 512), const(1, 512), const(1, 512), const(1, 512),
                   const(1, 512), const(GM_HEADS, CHUNK, CHUNK), const(CHUNK, GM_WIDTH), const(1, 2048),
                   const(1, D_MODEL)],
        out_shape=[f32(s, D_MODEL), jax.ShapeDtypeStruct((s, 2048), BF16),
                   f32(CONV_HALO, CONV_WIDTH), f32(1, 512), f32(1, 512), f32(1, 512), f32(1, 512),
                   f32(1, 512), f32(GM_HEADS, CHUNK, CHUNK), f32(CHUNK, GM_WIDTH), f32(1, 2048),
                   f32(1, D_MODEL)],
        scratch_shapes=[pltpu.VMEM((t + CONV_HALO, CONV_WIDTH), F32),
                        pltpu.VMEM((SUBLANES - 1, t + CONV_HALO - SUBLANES, CONV_WIDTH), F32),
                        pltpu.VMEM((t, GM_WIDTH), F32), pltpu.VMEM((t, GM_WIDTH), F32)],
        compiler_params=_params(("arbitrary",)),
    )(dh1, x, z, c1, w_out, w_in, g_mix, cw, lng, lnb, gg, gb, wpair, wpair_t, bias)


def _head_bias_grad(dbs):
    def body(d_ref, o_ref):
        dv = d_ref[...]
        lane = lax.broadcasted_iota(jnp.int32, (CHUNK, LANES), 1)
        acc = jnp.zeros((CHUNK, LANES), F32)
        for h in range(GM_HEADS):
            sh = jnp.sum(dv[:, h * GM_HEAD_DIM:(h + 1) * GM_HEAD_DIM], axis=-1, keepdims=True)
            acc = acc + jnp.where(lane == h, sh, 0.0)
        o_ref[...] = acc

    return pl.pallas_call(body, name="head_bias_grad",
                          out_shape=jax.ShapeDtypeStruct((CHUNK, LANES), F32))(dbs)


def kernel(x, mem, norm_mix_g, w_in, b_in, conv_w, conv_b, conv_ln_g, conv_ln_b, gm_ln_g, gm_ln_b, gm_w_s, gm_b_s, w_out, norm_xa_g, mem_norm_g, xa_wq, xa_wkv, xa_wo, norm_ffn_g, ffn_w_gate_up, ffn_w_down, final_norm_g, loss_target, m_norm_mix_g, m_w_in, m_b_in, m_conv_w, m_conv_b, m_conv_ln_g, m_conv_ln_b, m_gm_ln_g, m_gm_ln_b, m_gm_w_s, m_gm_b_s, m_w_out, m_norm_xa_g, m_mem_norm_g, m_xa_wq, m_xa_wkv, m_xa_wo, m_norm_ffn_g, m_ffn_w_gate_up, m_ffn_w_down, m_final_norm_g, v_norm_mix_g, v_w_in, v_b_in, v_conv_w, v_conv_b, v_conv_ln_g, v_conv_ln_b, v_gm_ln_g, v_gm_ln_b, v_gm_w_s, v_gm_b_s, v_w_out, v_norm_xa_g, v_mem_norm_g, v_xa_wq, v_xa_wkv, v_xa_wo, v_norm_ffn_g, v_ffn_w_gate_up, v_ffn_w_down, v_final_norm_g):
    weights = dict(norm_mix_g=norm_mix_g, w_in=w_in, b_in=b_in, conv_w=conv_w, conv_b=conv_b, conv_ln_g=conv_ln_g,
                   conv_ln_b=conv_ln_b, gm_ln_g=gm_ln_g, gm_ln_b=gm_ln_b, gm_w_s=gm_w_s, gm_b_s=gm_b_s, w_out=w_out,
                   norm_xa_g=norm_xa_g, mem_norm_g=mem_norm_g, xa_wq=xa_wq, xa_wkv=xa_wkv, xa_wo=xa_wo,
                   norm_ffn_g=norm_ffn_g, ffn_w_gate_up=ffn_w_gate_up, ffn_w_down=ffn_w_down,
                   final_norm_g=final_norm_g)
    m_in = dict(norm_mix_g=m_norm_mix_g, w_in=m_w_in, b_in=m_b_in, conv_w=m_conv_w, conv_b=m_conv_b,
                conv_ln_g=m_conv_ln_g, conv_ln_b=m_conv_ln_b, gm_ln_g=m_gm_ln_g, gm_ln_b=m_gm_ln_b, gm_w_s=m_gm_w_s,
                gm_b_s=m_gm_b_s, w_out=m_w_out, norm_xa_g=m_norm_xa_g, mem_norm_g=m_mem_norm_g, xa_wq=m_xa_wq,
                xa_wkv=m_xa_wkv, xa_wo=m_xa_wo, norm_ffn_g=m_norm_ffn_g, ffn_w_gate_up=m_ffn_w_gate_up,
                ffn_w_down=m_ffn_w_down, final_norm_g=m_final_norm_g)
    v_in = dict(norm_mix_g=v_norm_mix_g, w_in=v_w_in, b_in=v_b_in, conv_w=v_conv_w, conv_b=v_conv_b,
                conv_ln_g=v_conv_ln_g, conv_ln_b=v_conv_ln_b, gm_ln_g=v_gm_ln_g, gm_ln_b=v_gm_ln_b, gm_w_s=v_gm_w_s,
                gm_b_s=v_gm_b_s, w_out=v_w_out, norm_xa_g=v_norm_xa_g, mem_norm_g=v_mem_norm_g, xa_wq=v_xa_wq,
                xa_wkv=v_xa_wkv, xa_wo=v_xa_wo, norm_ffn_g=v_norm_ffn_g, ffn_w_gate_up=v_ffn_w_gate_up,
                ffn_w_down=v_ffn_w_down, final_norm_g=v_final_norm_g)
    grads, delta, new_m, new_v = {}, {}, {}, {}

    s = x.shape[1]
    ts = _row_tile(s)
    tb = max(CHUNK, ts // 2)
    tw = 2 * ts if s % (2 * ts) == 0 and ts >= 512 else ts
    cx, cy, cc = _mesh_pos()
    chip = 2 * cx + cy
    pos = jnp.stack([chip, cc]).astype(jnp.int32)
    row = lambda a: a.reshape(1, -1)
    x2, mem2, tgt2 = x[0], mem[0], loss_target[0]

    big = dict(w_in=w_in, xa_wkv=xa_wkv, w_out=w_out, xa_wq=xa_wq, xa_wo=xa_wo,
               ffn_w_gate_up=ffn_w_gate_up, ffn_w_down=ffn_w_down)
    big_names = list(big)
    halves = lambda a: a.reshape(2, a.shape[0] // 2, a.shape[1])
    conv_w_pad = jnp.pad(conv_w, ((0, CONV_HALO - CONV_KERNEL), (0, 0)))
    first_names = ["w_in", "conv_w"]
    later_names = [nm for nm in big_names if nm != "w_in"]
    cast = dict(zip(first_names, _cast_into_slots([halves(w_in), halves(conv_w_pad)], pos, [BF16, F32], "cast_w_in")))
    cast.update(zip(later_names, _cast_into_slots([halves(big[nm]) for nm in later_names], pos,
                                                  [BF16] * len(later_names), "cast_" + later_names[0],
                                                  side_by_side=(later_names.index("ffn_w_gate_up"),))))

    def start_gather(names, after):
        return _gather_start([cast[nm] for nm in names], "gather_start_" + names[0], after)

    def land_gather(names, started, after):
        send_sems, recv_sems, bufs, _ = started
        return _gather_wait(send_sems, recv_sems, bufs, after, "gather_wait_" + names[0])

    as_weights = lambda names, bufs: dict(zip(names, (b.reshape(b.shape[0], -1, b.shape[-1]) for b in bufs)))

    def start_share(names, landed):
        return _pass_start(landed, "pass_start_" + names[0])

    def share_gather(names, passing, which, after):
        send_sems, recv_sems, bufs, _ = passing
        sems = lambda s: [s[3 * a + k] for a in which for k in range(3)]
        picked = [names[a] for a in which]
        return as_weights(picked, _gather_wait(sems(send_sems), sems(recv_sems), [bufs[a] for a in which], after,
                                               "pass_wait_" + picked[0], _pass_descriptors))

    tril = jnp.tril(jnp.ones((CHUNK, CHUNK), dtype=bool))
    ws = jnp.where(tril[None], gm_w_s, 0.0)
    wpair = jnp.concatenate([ws[0::2], ws[1::2]], axis=2).astype(BF16)
    ws_t = jnp.swapaxes(ws, 1, 2)
    wpair_t = jnp.concatenate([ws_t[0::2], ws_t[1::2]], axis=2).astype(BF16)
    bias = jnp.repeat(gm_b_s.T, GM_HEAD_DIM, axis=1)

    attn_names = ["xa_wkv", "w_out", "xa_wq", "xa_wo"]
    gather_first = start_gather(first_names, ())
    hn1 = _norm_in(x2, row(norm_mix_g), tw, after=(gather_first[3], wpair, wpair_t, bias))
    landed = land_gather(first_names, gather_first, [cast[nm] for nm in later_names] + [hn1])
    passing = start_share(first_names, landed)
    gather_attn = start_gather(attn_names, passing[3])
    gw = share_gather(first_names, passing, (0, 1), gather_attn[3])
    w_in_g = gw["w_in"]
    cw_g = jnp.concatenate([gw["conv_w"][k] for k in range(N_CHIPS)], axis=1)

    z, mix, c1 = _seqmix_fwd(hn1, w_in_g, row(b_in), cw_g, row(conv_b), row(conv_ln_g), row(conv_ln_b),
                             row(gm_ln_g), row(gm_ln_b), wpair, bias, ts)
    landed = land_gather(attn_names, gather_attn, mix)
    passing = start_share(attn_names, landed)
    gather_gu = start_gather(["ffn_w_gate_up"], passing[3])
    wkv_g = share_gather(attn_names, passing, (0,), gather_gu[3])["xa_wkv"]
    mn, kv = _mem_kv(mem2, row(mem_norm_g), wkv_g)
    gw = share_gather(attn_names, passing, (1, 2, 3), kv)
    w_out_g = gw["w_out"].reshape(D_MODEL, D_MODEL)
    wq_g = gw["xa_wq"].reshape(D_MODEL, D_MODEL)
    wo_g = gw["xa_wo"].reshape(D_MODEL, D_MODEL)
    h1, hn2, q, o, h2, hn3 = _attn_block_fwd(x2, mix, w_out_g, row(norm_xa_g), wq_g, kv, wo_g, row(norm_ffn_g), ts)
    landed = land_gather(["ffn_w_gate_up"], gather_gu, hn3)
    passing = start_share(["ffn_w_gate_up"], landed)
    gather_down = start_gather(["ffn_w_down"], passing[3])
    wgu_g = share_gather(["ffn_w_gate_up"], passing, (0,), gather_down[3])["ffn_w_gate_up"]
    gu, act = _ffn_up(hn3, wgu_g, ts)
    landed = land_gather(["ffn_w_down"], gather_down, act)
    wd_g = as_weights(["ffn_w_down"], _pass_to_sibling(landed, "pass_ffn_w_down"))["ffn_w_down"].reshape(
        FFN_HIDDEN, D_MODEL)
    dh3, sq, d_final_g = _ffn_down_loss(act, wd_g, h2, row(final_norm_g), tgt2, ts)
    loss_here = jnp.broadcast_to(0.5 * jnp.sum(sq) / D_MODEL, (1, 2, SUBLANES, LANES))

    def split(g, nm):
        r, c = big[nm].shape
        return g.reshape(N_CHIPS, 2, r // 2, c)

    def chip_sums(group, arrays, got):
        sums, parts = [None] * len(group), [None] * len(group)
        for blocks in (N_CHIPS, 1):
            idx = [i for i, a in enumerate(arrays) if a.shape[0] == blocks]
            if idx:
                out = _add_halves([arrays[i] for i in idx], [got[i] for i in idx], pos, "chip_sum_" + group[idx[0]],
                                  [F32 if group[i] == "loss" else BF16 for i in idx])
                for k, i in enumerate(idx):
                    sums[i], parts[i] = out[0][k], out[1][k]
        return sums, parts

    def start_swap(group, grads, after=()):
        return _swap_start([split(g, nm) for g, nm in zip(grads, group)], "swap_start_" + group[0], after)

    def start_exchange(group, swapping, after, landed):
        arrays, got = _swap_wait(*swapping[:3], after, "swap_wait_" + group[0])
        sums, parts = chip_sums(group, arrays, got)
        return _exchange_start(sums, parts, "exchange_start_" + group[0], landed)

    def wait_exchange(group, started, after):
        sems, sums, parts, _ = started
        return _exchange_wait(sems, sums, parts, after, "exchange_wait_" + group[0])

    def finish_exchange(group, started, after):
        return _sum_chips(wait_exchange(group, started, after), pos, "total_" + group[0])

    def join(group, after):
        return _join_halves([halves_of[nm] for nm in group], "join_halves_" + group[0], after)

    def start_join(group, after):
        return _join_start([halves_of[nm] for nm in group], "join_start_" + group[0], after)

    def end_join(group, joining, after):
        return _join_wait(*joining[:2], after, "join_wait_" + group[0])

    def update(group, joined, after=()):
        outs = _adamw([(weights[nm], j.reshape(big[nm].shape), m_in[nm], v_in[nm]) for nm, j in zip(group, joined)],
                      "adamw_" + group[0], after)
        for nm, out in zip(group, outs):
            grads[nm], delta[nm], new_m[nm], new_v[nm] = out
        return [new_v[nm] for nm in group]

    def join_and_update(group, after):
        return update(group, join(group, after))

    as3 = lambda a: a.reshape((1,) + a.shape)
    halves_of = {}

    g_down = _grad_w(act, as3(dh3), FFN_HALF, D_MODEL, "grad_ffn_w_down")
    group_a = ["ffn_w_down"]
    swap_a = start_swap(group_a, [g_down])
    dgu, dh2, dh2_b, d_ffn_g = _ffn_bwd(dh3, wd_g, gu, wgu_g, h2, row(norm_ffn_g), tb,
                                        after=swap_a[3])
    exch_a = start_exchange(group_a, swap_a, dh2, wd_g)
    g_gu = _grad_w(hn3, dgu, D_MODEL, FFN_HALF, "grad_ffn_w_gate_up", after=exch_a[3])
    halves_of.update(zip(group_a, finish_exchange(group_a, exch_a, g_gu)))

    group_b = ["ffn_w_gate_up"]
    swap_b = start_swap(group_b, [g_gu])
    dh1, dh1_b, dq, dkv, d_xa_g = _attn_bwd(dh2, wo_g, q, kv, wq_g, h1, row(norm_xa_g), ts, after=swap_b[3])
    exch_b = start_exchange(group_b, swap_b, dh1, [halves_of[nm] for nm in group_a])
    joining_a = start_join(group_a, exch_b[3])
    g_wkv, d_mem_g = _mem_kv_bwd(dkv, mn, wkv_g, mem2, row(mem_norm_g), after=joining_a[2])
    g_wo, g_wq, g_wout = _grad_w_square([(o, dh2_b), (hn2, dq), (mix, dh1_b)], "grad_xa_wo", after=joining_a[2])
    done_a = update(group_a, end_join(group_a, joining_a, (g_wkv, g_wo, g_wq, g_wout)))
    halves_of.update(zip(group_b, finish_exchange(group_b, exch_b, done_a)))
    joining_b = start_join(group_b, done_a)
    group_c = ["xa_wo", "xa_wq", "xa_wkv", "w_out"]
    swap_c = start_swap(group_c, [g_wo, g_wq, g_wkv, g_wout], joining_b[2])
    (gx, dz, d_cw, d_cb, d_lng, d_lnb, d_gg, d_gb, d_ws, d_bs_sum, d_bin, d_mix_g) = _seqmix_bwd(
        dh1, x2, z, c1, w_out_g, w_in_g, row(norm_mix_g), cw_g, row(conv_ln_g), row(conv_ln_b),
        row(gm_ln_g), row(gm_ln_b), wpair, wpair_t, bias, tb, after=swap_c[3])
    d_bs = _head_bias_grad(d_bs_sum)[:, :GM_HEADS].T
    exch_c = start_exchange(group_c, swap_c, dz, joining_b[2])
    g_win = _grad_w(hn1, as3(dz), D_MODEL, 1024, "grad_w_in", after=exch_c[3], shards=2)

    small_names = ["norm_mix_g", "b_in", "conv_w", "conv_b", "conv_ln_g", "conv_ln_b", "gm_ln_g", "gm_ln_b",
                   "gm_w_s", "gm_b_s", "norm_xa_g", "mem_norm_g", "norm_ffn_g", "final_norm_g"]
    d_cw_by_chip = jnp.swapaxes(d_cw.reshape(CONV_HALO, N_CHIPS, LANES), 0, 1).reshape(-1, LANES)
    small_grads = dict(norm_mix_g=d_mix_g, b_in=d_bin, conv_w=d_cw_by_chip, conv_b=d_cb, conv_ln_g=d_lng,
                       conv_ln_b=d_lnb, gm_ln_g=d_gg, gm_ln_b=d_gb, gm_w_s=d_ws, gm_b_s=d_bs, norm_xa_g=d_xa_g,
                       mem_norm_g=d_mem_g, norm_ffn_g=d_ffn_g, final_norm_g=d_final_g)

    def rows_form(a):
        a = a.reshape(-1, LANES)
        return jnp.pad(a, ((0, -a.shape[0] % SUBLANES), (0, 0)))

    pieces = [rows_form(small_grads[nm]) for nm in small_names]
    offsets, total = [], 0
    for p in pieces:
        offsets.append(total)
        total += p.shape[0]
    pack_rows = -(-total // 32) * 32
    small_pack = jnp.pad(jnp.concatenate(pieces, axis=0), ((0, pack_rows - total), (0, 0)))

    group_d = ["w_in", "small", "loss"]
    joined_b = end_join(group_b, joining_b, g_win)
    swap_d = _swap_start([split(g_win, "w_in")], "swap_start_w_in", joined_b)
    done_b = update(group_b, joined_b, swap_d[3])
    small_d = [small_pack.reshape(1, 2, pack_rows // 2, LANES), loss_here]
    got_small = _swap_halves(small_d, "swap_halves_small", done_b)
    arrays_d, got_d = _swap_wait(*swap_d[:3], got_small, "swap_wait_w_in")
    sums_d, parts_d = chip_sums(group_d, arrays_d + small_d, got_d + list(got_small))
    parts_c = wait_exchange(group_c, exch_c, sums_d)
    exch_d = _exchange_start(sums_d, parts_d, "exchange_start_w_in", parts_c)
    halves_of.update(zip(group_c, _sum_chips(parts_c, pos, "total_xa_wo", exch_d[3])))
    done_c = join_and_update(group_c, exch_d[3])
    halves_of.update(zip(group_d, finish_exchange(group_d, exch_d, done_c)))
    joined_d = _join_halves([halves_of[nm] for nm in group_d], "join_halves_w_in")
    loss = joined_d[2][0, 0, 0]
    grads["w_in"], delta["w_in"], new_m["w_in"], new_v["w_in"] = _adamw(
        [(w_in, joined_d[0].reshape(w_in.shape), m_w_in, v_w_in)], "adamw_w_in")[0]

    local_rows = lambda a, nm: a if nm == "conv_w" else a.reshape(-1, LANES)
    params = [tuple(local_rows(src[nm], nm) for src in (weights, m_in, v_in)) for nm in small_names]
    outs = _adamw_small(joined_d[1].reshape(pack_rows, LANES), pos, params, offsets, small_names.index("conv_w"))
    for k, nm in enumerate(small_names):
        for dst, a in zip((grads, delta, new_m, new_v), outs[4 * k:4 * k + 4]):
            dst[nm] = a

    order = ["norm_mix_g", "w_in", "b_in", "conv_w", "conv_b", "conv_ln_g", "conv_ln_b", "gm_ln_g", "gm_ln_b",
             "gm_w_s", "gm_b_s", "w_out", "norm_xa_g", "mem_norm_g", "xa_wq", "xa_wkv", "xa_wo", "norm_ffn_g",
             "ffn_w_gate_up", "ffn_w_down", "final_norm_g"]
    fit = lambda a, nm: a.reshape(weights[nm].shape)
    return (loss, gx.reshape(x.shape),
            *[fit(grads[nm], nm) for nm in order], *[fit(delta[nm], nm) for nm in order],
            *[fit(new_m[nm], nm) for nm in order], *[fit(new_v[nm], nm) for nm in order])
```

```python
import functools

import jax
import jax.numpy as jnp
from jax import lax
from jax.experimental import pallas as pl
from jax.experimental.pallas import tpu as pltpu

F32 = jnp.float32
BF16 = jnp.bfloat16

D_MODEL = 1024
CONV_WIDTH = 512
GM_WIDTH = 512
CONV_KERNEL = 31
CONV_HALO = 32
GRAD_ROWS = 2048
CHUNK = 128
GM_HEADS = 8
GM_HEAD_DIM = 64
XA_HEADS = 4
XA_HEAD_DIM = 256
FFN_HIDDEN = 2816
FFN_HALF = FFN_HIDDEN // 2
RMS_EPS = 1e-6
LN_EPS = 1e-5
N_CHIPS = 4
LANES = 128
SUBLANES = 8

ADAM_LR = 0.001
ADAM_B1 = 0.9
ADAM_B2 = 0.999
ADAM_EPS = 1e-08
ADAM_WD = 0.01
ADAM_STEP = 10

VMEM_LIMIT_BYTES = 56 * 1024 * 1024
FFN_BWD_VMEM_LIMIT_BYTES = 60 * 1024 * 1024
MESH = pl.DeviceIdType.MESH
ANY = pl.BlockSpec(memory_space=pl.ANY)
HBM_SPEC = pl.BlockSpec(memory_space=pltpu.HBM)
SEM_SPEC = pl.BlockSpec(memory_space=pltpu.SEMAPHORE)

_NT = (((1,), (1,)), ((), ()))
_TN = (((0,), (0,)), ((), ()))
_GELU_C = 0.7978845608028654
_GELU_A = 0.044715


def _dot(a, b):
    return jnp.dot(a, b, preferred_element_type=F32)


def _dot_nt(a, b):
    return lax.dot_general(a, b, _NT, preferred_element_type=F32)


def _dot_tn(a, b):
    return lax.dot_general(a, b, _TN, preferred_element_type=F32)


def _mean(v):
    return jnp.mean(v, axis=-1, keepdims=True)


def _rowsum(v):
    return jnp.sum(v, axis=0, keepdims=True)


def _sigmoid(v):
    return 1.0 / (1.0 + jnp.exp(-v))


def _gelu_parts(v):
    v2 = v * v
    t = jnp.tanh(_GELU_C * (v + _GELU_A * v * v2))
    g = 0.5 * v * (1.0 + t)
    dg = 0.5 * (1.0 + t) + 0.5 * v * (1.0 - t * t) * (_GELU_C * (1.0 + 3.0 * _GELU_A * v2))
    return g, dg


def _rms_stats(v):
    return lax.rsqrt(_mean(v * v) + RMS_EPS)


def _rms_bwd(dy, v, r, g):
    n = v * r
    dn = dy * g
    dv = r * (dn - n * _mean(dn * n))
    return dv, _rowsum(dy * n)


def _ln_stats(v):
    mu = _mean(v)
    xc = v - mu
    rs = lax.rsqrt(_mean(xc * xc) + LN_EPS)
    return xc * rs, rs


def _ln_bwd(dy, xh, rs, g):
    dxh = dy * g
    dv = rs * (dxh - _mean(dxh) - xh * _mean(dxh * xh))
    return dv, _rowsum(dy * xh), _rowsum(dy)


def _params(sem, vmem_limit_bytes=VMEM_LIMIT_BYTES):
    return pltpu.CompilerParams(dimension_semantics=sem, vmem_limit_bytes=vmem_limit_bytes)


def _row_tile(s):
    return 512 if s % 512 == 0 and s >= 2048 else 128


def _mesh_pos():
    return lax.axis_index("x"), lax.axis_index("y"), lax.axis_index("c")


def _slot(buf, chip_idx, half):
    if buf.shape[0] == N_CHIPS:
        return buf.at[chip_idx, half]
    width = buf.shape[-1] // 2
    return buf.at[chip_idx // 2, half, :, pl.ds(pl.multiple_of((chip_idx % 2) * width, LANES), width)]


def _cast_into_slots(ws, pos, dtypes, name, side_by_side=()):
    n = len(ws)

    def body(pos_ref, *refs):
        for a in range(n):
            refs[n + a][0] = refs[a][...].astype(dtypes[a])

    def out_spec(a, w):
        if a in side_by_side:
            return pl.BlockSpec((1, 1) + w.shape[1:], lambda i, p: (p[0] // 2, i, 0, p[0] % 2))
        return pl.BlockSpec((1, 1) + w.shape[1:], lambda i, p: (p[0], i, 0, 0))

    def out_shape(a, w):
        if a in side_by_side:
            return (2, 2, w.shape[1], 2 * w.shape[2])
        return (N_CHIPS,) + w.shape

    return pl.pallas_call(
        body, name=name,
        grid_spec=pltpu.PrefetchScalarGridSpec(
            num_scalar_prefetch=1, grid=(2,),
            in_specs=[pl.BlockSpec((1,) + w.shape[1:], lambda i, p: (i, 0, 0)) for w in ws],
            out_specs=[out_spec(a, w) for a, w in enumerate(ws)]),
        out_shape=[jax.ShapeDtypeStruct(out_shape(a, w), dt) for a, (w, dt) in enumerate(zip(ws, dtypes))],
        compiler_params=_params(("parallel",)),
    )(pos, *ws)


def _adam_update(w, g, m, v):
    nm = ADAM_B1 * m + (1.0 - ADAM_B1) * g
    nv = ADAM_B2 * v + (1.0 - ADAM_B2) * (g * g)
    m_hat = nm / (1.0 - ADAM_B1 ** ADAM_STEP)
    v_hat = nv / (1.0 - ADAM_B2 ** ADAM_STEP)
    return -ADAM_LR * (m_hat / (jnp.sqrt(v_hat) + ADAM_EPS) + ADAM_WD * w), nm, nv


ADAM_STEPS = 4


def _adamw(quads, name, after=()):
    n = len(quads)

    def body(*refs):
        ins, outs = refs[:4 * n], refs[4 * n:]
        for a in range(n):
            w, g, m, v = (r[...] for r in ins[4 * a:4 * a + 4])
            outs[4 * a][...] = g
            outs[4 * a + 1][...], outs[4 * a + 2][...], outs[4 * a + 3][...] = _adam_update(w, g, m, v)

    specs = [pl.BlockSpec((q[0].shape[0] // ADAM_STEPS, q[0].shape[1]), lambda i: (i, 0)) for q in quads]
    out = _tied_call(
        body, after, name=name, grid=(ADAM_STEPS,),
        in_specs=[sp for sp in specs for _ in range(4)], out_specs=[sp for sp in specs for _ in range(4)],
        out_shape=[jax.ShapeDtypeStruct(q[0].shape, F32) for q in quads for _ in range(4)],
        compiler_params=_params(("parallel",)),
    )(*[a for q in quads for a in q])
    return [tuple(out[4 * a:4 * a + 4]) for a in range(n)]


def _adamw_small(gpack, pos, params, offsets, conv_at):
    n = len(params)

    def body(pos_ref, g_ref, *refs):
        ins, outs = refs[:3 * n], refs[3 * n:]
        for k in range(n):
            rows = params[k][0].shape[0]
            start = offsets[k]
            if k == conv_at:
                start = pl.multiple_of(start + pos_ref[0] * CONV_HALO, SUBLANES)
            g = g_ref[pl.ds(start, rows), :]
            outs[4 * k][...] = g
            outs[4 * k + 1][...], outs[4 * k + 2][...], outs[4 * k + 3][...] = _adam_update(
                ins[3 * k][...], g, ins[3 * k + 1][...], ins[3 * k + 2][...])

    flat = [a for p in params for a in p]
    vmem = pl.BlockSpec(memory_space=pltpu.VMEM)
    return pl.pallas_call(
        body, name="adamw_small",
        in_specs=[pl.BlockSpec(memory_space=pltpu.SMEM), vmem] + [vmem] * len(flat),
        out_specs=[vmem] * (4 * n),
        out_shape=[jax.ShapeDtypeStruct(p[0].shape, F32) for p in params for _ in range(4)],
    )(pos, gpack, *flat)


def _as_tuple(after):
    return tuple(after) if isinstance(after, (tuple, list)) else (after,)


def _tied_call(body, after, *, in_specs, **kwargs):
    after = _as_tuple(after)
    n_in, n_after = len(in_specs), len(after)

    def tied(*refs):
        body(*refs[:n_in], *refs[n_in + n_after:])

    call = pl.pallas_call(tied, in_specs=list(in_specs) + [ANY] * n_after, **kwargs)
    return lambda *operands: call(*operands, *after)


def _other_chips(x, y):
    return [(1 - x, y), (x, 1 - y), (1 - x, 1 - y)]


def _gather_descriptors(bufs, send_of, recv_of):
    x, y, c = _mesh_pos()
    me = 2 * x + y
    chips = _other_chips(x, y)
    sends, arrivals = [], []
    for a in range(len(bufs)):
        for k in range(3):
            ck = 2 * chips[k][0] + chips[k][1]

            def copy(slot, a=a, k=k):
                return pltpu.make_async_remote_copy(
                    src_ref=_slot(bufs[a], slot, c), dst_ref=_slot(bufs[a], slot, c),
                    send_sem=send_of(a, k), recv_sem=recv_of(a, k),
                    device_id=(*chips[k], c), device_id_type=MESH)

            sends.append(functools.partial(copy, me))
            arrivals.append(functools.partial(copy, ck))
    return sends, arrivals


def _gather_start(bufs, name, after=()):
    n = len(bufs)
    ns = 3 * n

    def body(*refs):
        sems = refs[n:n + 2 * ns]
        thru = refs[n + 2 * ns:2 * n + 2 * ns]
        token = refs[2 * n + 2 * ns]
        _chips_handshake()
        sends, _ = _gather_descriptors(thru, lambda a, k: sems[3 * a + k], lambda a, k: sems[ns + 3 * a + k])
        for cp in sends:
            cp().start()
        token[...] = jnp.zeros_like(token)

    held = [pltpu.with_memory_space_constraint(b, pltpu.HBM) for b in bufs]
    out = _tied_call(
        body, after, name=name,
        out_shape=(*[pltpu.SemaphoreType.DMA(())] * (2 * ns), *[pltpu.HBM(b.shape, b.dtype) for b in held],
                   jax.ShapeDtypeStruct((8, LANES), F32)),
        in_specs=[HBM_SPEC] * n,
        out_specs=(*[SEM_SPEC] * (2 * ns), *[HBM_SPEC] * n, pl.BlockSpec(memory_space=pltpu.VMEM)),
        input_output_aliases={i: 2 * ns + i for i in range(n)},
        compiler_params=pltpu.CompilerParams(has_side_effects=pltpu.SideEffectType.DATAFLOW_SIDE_EFFECTING,
                                             collective_id=CHIPS_COLLECTIVE_ID),
    )(*held)
    return list(out[:ns]), list(out[ns:2 * ns]), list(out[2 * ns:2 * ns + n]), out[2 * ns + n]


def _gather_wait(send_sems, recv_sems, bufs, after, name, descriptors=_gather_descriptors):
    n = len(bufs)
    ns = 3 * n

    def body(*refs):
        buf_ref = refs[:n]
        sem_ref = refs[n:n + 2 * ns]
        sends, arrivals = descriptors(buf_ref, lambda a, k: sem_ref[3 * a + k], lambda a, k: sem_ref[ns + 3 * a + k])
        for cp in sends:
            cp().wait_send()
        for cp in arrivals:
            cp().wait_recv()

    out = pl.pallas_call(
        body, name=name,
        out_shape=tuple(pltpu.HBM(b.shape, b.dtype) for b in bufs),
        in_specs=[HBM_SPEC] * n + [SEM_SPEC] * (2 * ns) + [ANY] * len(_as_tuple(after)),
        out_specs=tuple([HBM_SPEC] * n),
        input_output_aliases={i: i for i in range(n)},
        compiler_params=pltpu.CompilerParams(has_side_effects=pltpu.SideEffectType.DATAFLOW_SIDE_EFFECTING),
    )(*bufs, *send_sems, *recv_sems, *_as_tuple(after))
    return list(out)


SIBLING_COLLECTIVE_ID = 0


def _sibling_handshake():
    x, y, c = _mesh_pos()
    barrier = pltpu.get_barrier_semaphore()
    pl.semaphore_signal(barrier, inc=1, device_id=(x, y, 1 - c), device_id_type=MESH)
    pl.semaphore_wait(barrier, 1)


CHIPS_COLLECTIVE_ID = 1


def _chips_handshake():
    x, y, c = _mesh_pos()
    barrier = pltpu.get_barrier_semaphore()
    for chip in _other_chips(x, y):
        pl.semaphore_signal(barrier, inc=1, device_id=(*chip, c), device_id_type=MESH)
    pl.semaphore_wait(barrier, 3)


def _pass_descriptors(bufs, send_of, recv_of):
    x, y, c = _mesh_pos()
    chips = _other_chips(x, y)

    def half(a, k, which):
        ck = 2 * chips[k][0] + chips[k][1]
        return functools.partial(
            pltpu.make_async_remote_copy,
            src_ref=_slot(bufs[a], ck, which), dst_ref=_slot(bufs[a], ck, which),
            send_sem=send_of(a, k), recv_sem=recv_of(a, k),
            device_id=(x, y, 1 - c), device_id_type=MESH)

    pairs = [(a, k) for a in range(len(bufs)) for k in range(3)]
    return [half(a, k, c) for a, k in pairs], [half(a, k, 1 - c) for a, k in pairs]


def _pass_start(bufs, name, after=()):
    n = len(bufs)
    ns = 3 * n

    def body(*refs):
        sems = refs[n:n + 2 * ns]
        thru = refs[n + 2 * ns:2 * n + 2 * ns]
        token = refs[2 * n + 2 * ns]
        _sibling_handshake()
        sends, _ = _pass_descriptors(thru, lambda a, k: sems[3 * a + k], lambda a, k: sems[ns + 3 * a + k])
        for cp in sends:
            cp().start()
        token[...] = jnp.zeros_like(token)

    held = [pltpu.with_memory_space_constraint(b, pltpu.HBM) for b in bufs]
    out = _tied_call(
        body, after, name=name,
        out_shape=(*[pltpu.SemaphoreType.DMA(())] * (2 * ns), *[pltpu.HBM(b.shape, b.dtype) for b in held],
                   jax.ShapeDtypeStruct((8, LANES), F32)),
        in_specs=[HBM_SPEC] * n,
        out_specs=(*[SEM_SPEC] * (2 * ns), *[HBM_SPEC] * n, pl.BlockSpec(memory_space=pltpu.VMEM)),
        input_output_aliases={i: 2 * ns + i for i in range(n)},
        compiler_params=pltpu.CompilerParams(has_side_effects=pltpu.SideEffectType.DATAFLOW_SIDE_EFFECTING,
                                             collective_id=SIBLING_COLLECTIVE_ID),
    )(*held)
    return list(out[:ns]), list(out[ns:2 * ns]), list(out[2 * ns:2 * ns + n]), out[2 * ns + n]


def _pass_to_sibling(bufs, name, after=()):
    n = len(bufs)

    def body(*refs):
        outs = refs[n:2 * n]
        send_sem, recv_sem = refs[2 * n:]
        _sibling_handshake()
        sends, arrivals = _pass_descriptors(outs, lambda a, k: send_sem.at[a, k], lambda a, k: recv_sem.at[a, k])
        sends = [cp() for cp in sends]
        for cp in sends:
            cp.start()
        for cp in arrivals:
            cp().wait_recv()
        for cp in sends:
            cp.wait_send()

    return _tied_call(
        body, after, name=name,
        in_specs=[ANY] * n, out_specs=[ANY] * n,
        out_shape=[jax.ShapeDtypeStruct(b.shape, b.dtype) for b in bufs],
        input_output_aliases={a: a for a in range(n)},
        scratch_shapes=[pltpu.SemaphoreType.DMA((n, 3))] * 2,
        compiler_params=pltpu.CompilerParams(collective_id=SIBLING_COLLECTIVE_ID),
    )(*bufs)


def _swap_descriptors(grads, lands, send_of, recv_of):
    x, y, c = _mesh_pos()
    return [functools.partial(
        pltpu.make_async_remote_copy,
        src_ref=grads[a].at[:, pl.ds(1 - c, 1)], dst_ref=lands[a],
        send_sem=send_of(a), recv_sem=recv_of(a),
        device_id=(x, y, 1 - c), device_id_type=MESH) for a in range(len(grads))]


def _swap_halves(grads, name, after=()):
    n = len(grads)

    def body(*refs):
        ins, outs = refs[:n], refs[n:2 * n]
        send_sem, recv_sem = refs[2 * n:]
        _sibling_handshake()
        cps = [cp() for cp in _swap_descriptors(ins, outs, lambda a: send_sem.at[a], lambda a: recv_sem.at[a])]
        for cp in cps:
            cp.start()
        for cp in cps:
            cp.wait()

    out_shape = [jax.ShapeDtypeStruct((g.shape[0], 1) + g.shape[2:], g.dtype) for g in grads]
    return _tied_call(
        body, after, name=name,
        in_specs=[ANY] * n, out_specs=[ANY] * n, out_shape=out_shape,
        scratch_shapes=[pltpu.SemaphoreType.DMA((n,))] * 2,
        compiler_params=pltpu.CompilerParams(collective_id=SIBLING_COLLECTIVE_ID),
    )(*grads)


def _swap_start(grads, name, after=()):
    n, after = len(grads), _as_tuple(after)

    def body(*refs):
        outs = refs[2 * n + len(after):]
        sems, g_thru, l_thru, token = outs[:2 * n], outs[2 * n:3 * n], outs[3 * n:4 * n], outs[4 * n]
        _sibling_handshake()
        for cp in _swap_descriptors(g_thru, l_thru, lambda a: sems[a], lambda a: sems[n + a]):
            cp().start()
        token[...] = jnp.zeros_like(token)

    lands = [lax.empty((g.shape[0], 1) + g.shape[2:], g.dtype) for g in grads]
    held = [pltpu.with_memory_space_constraint(a, pltpu.HBM) for a in (*grads, *lands)]
    out = pl.pallas_call(
        body, name=name,
        out_shape=(*[pltpu.SemaphoreType.DMA(())] * (2 * n), *[pltpu.HBM(a.shape, a.dtype) for a in held],
                   jax.ShapeDtypeStruct((8, LANES), F32)),
        in_specs=[HBM_SPEC] * (2 * n) + [ANY] * len(after),
        out_specs=(*[SEM_SPEC] * (2 * n), *[HBM_SPEC] * (2 * n), pl.BlockSpec(memory_space=pltpu.VMEM)),
        input_output_aliases={i: 2 * n + i for i in range(2 * n)},
        compiler_params=pltpu.CompilerParams(has_side_effects=pltpu.SideEffectType.DATAFLOW_SIDE_EFFECTING,
                                             collective_id=SIBLING_COLLECTIVE_ID),
    )(*held, *after)
    return list(out[:2 * n]), list(out[2 * n:3 * n]), list(out[3 * n:4 * n]), out[4 * n]


def _swap_wait(sems, grads, lands, after, name):
    n = len(grads)

    def body(*refs):
        g_ref, l_ref = refs[:n], refs[n:2 * n]
        sem_ref = refs[2 * n:4 * n]
        for cp in _swap_descriptors(g_ref, l_ref, lambda a: sem_ref[a], lambda a: sem_ref[n + a]):
            cp().wait()

    out = pl.pallas_call(
        body, name=name,
        out_shape=tuple(pltpu.HBM(a.shape, a.dtype) for a in (*grads, *lands)),
        in_specs=[HBM_SPEC] * (2 * n) + [SEM_SPEC] * (2 * n) + [ANY] * len(_as_tuple(after)),
        out_specs=tuple([HBM_SPEC] * (2 * n)),
        input_output_aliases={i: i for i in range(2 * n)},
        compiler_params=pltpu.CompilerParams(has_side_effects=pltpu.SideEffectType.DATAFLOW_SIDE_EFFECTING),
    )(*grads, *lands, *sems, *_as_tuple(after))
    return list(out[:n]), list(out[n:])


def _add_halves(gs, gots, pos, name, dtypes):
    n = len(gs)
    j = gs[0].shape[0]

    def body(pos_ref, *refs):
        g_refs, r_refs = refs[:n], refs[n:2 * n]
        o_refs, p_refs = refs[2 * n:3 * n], refs[3 * n:]
        vals = [(g_refs[a][0, 0] + r_refs[a][0, 0]).astype(dtypes[a]) for a in range(n)]
        for a in range(n):
            o_refs[a][0] = vals[a]
        if j == 1:
            for a in range(n):
                p_refs[a][0] = vals[a]
        else:
            @pl.when(pl.program_id(0) == pos_ref[0])
            def _():
                for a in range(n):
                    p_refs[a][0] = vals[a]

    blk = lambda g: (1,) + g.shape[2:]
    out = pl.pallas_call(
        body, name=name,
        grid_spec=pltpu.PrefetchScalarGridSpec(
            num_scalar_prefetch=1, grid=(j,),
            in_specs=[pl.BlockSpec((1,) + blk(g), lambda i, p: (i, p[1], 0, 0)) for g in gs]
            + [pl.BlockSpec((1,) + blk(g), lambda i, p: (i, 0, 0, 0)) for g in gs],
            out_specs=[pl.BlockSpec(blk(g), lambda i, p: (i, 0, 0)) for g in gs]
            + [pl.BlockSpec(blk(g), lambda i, p: (p[0], 0, 0)) for g in gs]),
        out_shape=[jax.ShapeDtypeStruct((j,) + g.shape[2:], dt) for g, dt in zip(gs, dtypes)]
        + [jax.ShapeDtypeStruct((N_CHIPS,) + g.shape[2:], dt) for g, dt in zip(gs, dtypes)],
        compiler_params=_params(("arbitrary",)),
    )(pos, *gs, *gots)
    return list(out[:n]), list(out[n:])


def _exchange_descriptors(sums, parts, send_of, recv_of):
    x, y, c = _mesh_pos()
    me = 2 * x + y
    chips = _other_chips(x, y)
    sends, arrivals = [], []
    for a in range(len(sums)):
        for k in range(3):
            ck = 2 * chips[k][0] + chips[k][1]
            mine = sums[a].at[ck] if sums[a].shape[0] == N_CHIPS else sums[a].at[0]

            def copy(dst_slot, a=a, k=k, mine=mine):
                return pltpu.make_async_remote_copy(
                    src_ref=mine, dst_ref=parts[a].at[dst_slot],
                    send_sem=send_of(a, k), recv_sem=recv_of(a, k),
                    device_id=(*chips[k], c), device_id_type=MESH)

            sends.append(functools.partial(copy, me))
            arrivals.append(functools.partial(copy, ck))
    return sends, arrivals


def _exchange_start(sums, parts, name, after=()):
    n = len(sums)
    ns = 3 * n

    def body(*refs):
        sems = refs[2 * n:2 * n + 2 * ns]
        sums_thru = refs[2 * n + 2 * ns:3 * n + 2 * ns]
        parts_thru = refs[3 * n + 2 * ns:4 * n + 2 * ns]
        token = refs[4 * n + 2 * ns]
        _chips_handshake()
        sends, _ = _exchange_descriptors(sums_thru, parts_thru, lambda a, k: sems[3 * a + k],
                                         lambda a, k: sems[ns + 3 * a + k])
        for cp in sends:
            cp().start()
        token[...] = jnp.zeros_like(token)

    hbm = lambda a: pltpu.HBM(a.shape, a.dtype)
    held = [pltpu.with_memory_space_constraint(a, pltpu.HBM) for a in (*sums, *parts)]
    out = _tied_call(
        body, after, name=name,
        out_shape=(*[pltpu.SemaphoreType.DMA(())] * (2 * ns), *[hbm(a) for a in held],
                   jax.ShapeDtypeStruct((8, LANES), F32)),
        in_specs=[HBM_SPEC] * (2 * n),
        out_specs=(*[SEM_SPEC] * (2 * ns), *[HBM_SPEC] * (2 * n), pl.BlockSpec(memory_space=pltpu.VMEM)),
        input_output_aliases={i: 2 * ns + i for i in range(2 * n)},
        compiler_params=pltpu.CompilerParams(has_side_effects=pltpu.SideEffectType.DATAFLOW_SIDE_EFFECTING,
                                             collective_id=CHIPS_COLLECTIVE_ID),
    )(*held)
    return (list(out[:2 * ns]), list(out[2 * ns:2 * ns + n]), list(out[2 * ns + n:2 * ns + 2 * n]),
            out[2 * ns + 2 * n])


def _exchange_wait(sems, sums, parts, after, name):
    n = len(sums)
    ns = 3 * n

    def body(*refs):
        sums_ref, parts_ref = refs[:n], refs[n:2 * n]
        sem_ref = refs[2 * n:2 * n + 2 * ns]
        sends, arrivals = _exchange_descriptors(sums_ref, parts_ref, lambda a, k: sem_ref[3 * a + k],
                                                lambda a, k: sem_ref[ns + 3 * a + k])
        for cp in sends:
            cp().wait_send()
        for cp in arrivals:
            cp().wait_recv()

    hbm = lambda a: pltpu.HBM(a.shape, a.dtype)
    out = pl.pallas_call(
        body, name=name,
        out_shape=tuple(hbm(a) for a in (*sums, *parts)),
        in_specs=[HBM_SPEC] * (2 * n) + [SEM_SPEC] * (2 * ns) + [ANY] * len(_as_tuple(after)),
        out_specs=tuple([HBM_SPEC] * (2 * n)),
        input_output_aliases={i: i for i in range(2 * n)},
        compiler_params=pltpu.CompilerParams(has_side_effects=pltpu.SideEffectType.DATAFLOW_SIDE_EFFECTING),
    )(*sums, *parts, *sems, *_as_tuple(after))
    return list(out[n:])


def _sum_chips(parts, pos, name, after=()):
    n = len(parts)
    after = _as_tuple(after)

    def body(pos_ref, *refs):
        outs = refs[n + len(after):]
        for a in range(n):
            p_ref = refs[a]
            outs[a][0] = (((p_ref[0].astype(F32) + p_ref[1].astype(F32)) + p_ref[2].astype(F32))
                          + p_ref[3].astype(F32))

    out = pl.pallas_call(
        body, name=name,
        grid_spec=pltpu.PrefetchScalarGridSpec(
            num_scalar_prefetch=1, grid=(1,),
            in_specs=[pl.BlockSpec(p.shape, lambda i, q: (0, 0, 0)) for p in parts] + [ANY] * len(after),
            out_specs=[pl.BlockSpec((1,) + p.shape[1:], lambda i, q: (q[1], 0, 0)) for p in parts]),
        out_shape=[jax.ShapeDtypeStruct((2,) + p.shape[1:], F32) for p in parts],
        compiler_params=_params(("arbitrary",)),
    )(pos, *parts, *after)
    return list(out)


def _join_descriptors(fulls, send_of, recv_of):
    x, y, c = _mesh_pos()

    def half(a, which):
        return functools.partial(
            pltpu.make_async_remote_copy,
            src_ref=fulls[a].at[which], dst_ref=fulls[a].at[which],
            send_sem=send_of(a), recv_sem=recv_of(a),
            device_id=(x, y, 1 - c), device_id_type=MESH)

    return [half(a, c) for a in range(len(fulls))], [half(a, 1 - c) for a in range(len(fulls))]


def _join_start(fulls, name, after=()):
    n = len(fulls)

    def body(*refs):
        sems, thru, token = refs[n:3 * n], refs[3 * n:4 * n], refs[4 * n]
        _sibling_handshake()
        sends, _ = _join_descriptors(thru, lambda a: sems[a], lambda a: sems[n + a])
        for cp in sends:
            cp().start()
        token[...] = jnp.zeros_like(token)

    held = [pltpu.with_memory_space_constraint(f, pltpu.HBM) for f in fulls]
    out = _tied_call(
        body, after, name=name,
        out_shape=(*[pltpu.SemaphoreType.DMA(())] * (2 * n), *[pltpu.HBM(f.shape, f.dtype) for f in held],
                   jax.ShapeDtypeStruct((8, LANES), F32)),
        in_specs=[HBM_SPEC] * n,
        out_specs=(*[SEM_SPEC] * (2 * n), *[HBM_SPEC] * n, pl.BlockSpec(memory_space=pltpu.VMEM)),
        input_output_aliases={i: 2 * n + i for i in range(n)},
        compiler_params=pltpu.CompilerParams(has_side_effects=pltpu.SideEffectType.DATAFLOW_SIDE_EFFECTING,
                                             collective_id=SIBLING_COLLECTIVE_ID),
    )(*held)
    return list(out[:2 * n]), list(out[2 * n:3 * n]), out[3 * n]


def _join_wait(sems, fulls, after, name):
    n = len(fulls)

    def body(*refs):
        sem_ref = refs[n:3 * n]
        sends, arrivals = _join_descriptors(refs[:n], lambda a: sem_ref[a], lambda a: sem_ref[n + a])
        for cp in sends:
            cp().wait_send()
        for cp in arrivals:
            cp().wait_recv()

    out = pl.pallas_call(
        body, name=name,
        out_shape=tuple(pltpu.HBM(f.shape, f.dtype) for f in fulls),
        in_specs=[HBM_SPEC] * n + [SEM_SPEC] * (2 * n) + [ANY] * len(_as_tuple(after)),
        out_specs=tuple([HBM_SPEC] * n),
        input_output_aliases={i: i for i in range(n)},
        compiler_params=pltpu.CompilerParams(has_side_effects=pltpu.SideEffectType.DATAFLOW_SIDE_EFFECTING),
    )(*fulls, *sems, *_as_tuple(after))
    return list(out)


def _join_halves(fulls, name, after=()):
    n = len(fulls)

    def body(*refs):
        send_sem, recv_sem = refs[2 * n:]
        _sibling_handshake()
        sends, arrivals = _join_descriptors(refs[n:2 * n], lambda a: send_sem.at[a], lambda a: recv_sem.at[a])
        sends = [cp() for cp in sends]
        for cp in sends:
            cp.start()
        for cp in arrivals:
            cp().wait_recv()
        for cp in sends:
            cp.wait_send()

    out_shape = [jax.ShapeDtypeStruct(f.shape, f.dtype) for f in fulls]
    return _tied_call(
        body, after, name=name,
        in_specs=[ANY] * n, out_specs=[ANY] * n, out_shape=out_shape,
        input_output_aliases={a: a for a in range(n)},
        scratch_shapes=[pltpu.SemaphoreType.DMA((n,))] * 2,
        compiler_params=pltpu.CompilerParams(collective_id=SIBLING_COLLECTIVE_ID),
    )(*fulls)


def _norm_in(x, g, ts, after=()):
    s = x.shape[0]

    def body(x_ref, g_ref, hn_ref):
        xv = x_ref[...]
        hn_ref[...] = (xv * _rms_stats(xv) * g_ref[...]).astype(BF16)

    row = pl.BlockSpec((ts, D_MODEL), lambda i: (i, 0))
    return _tied_call(
        body, after, name="norm_in", grid=(s // ts,),
        in_specs=[row, pl.BlockSpec((1, D_MODEL), lambda i: (0, 0))], out_specs=row,
        out_shape=jax.ShapeDtypeStruct((s, D_MODEL), BF16),
        compiler_params=_params(("parallel",)),
    )(x, g)


def _shift_rows(buf, shifted, t):
    rows = t + CONV_HALO - SUBLANES
    for r in range(1, SUBLANES):
        shifted[r - 1, 0:rows, :] = buf[pl.ds(r, rows), :]


def _window(buf, shifted, offset, t):
    r = offset % SUBLANES
    if r == 0:
        return buf[pl.ds(offset, t), :]
    return shifted[r - 1, pl.ds(offset - r, t), :]


def _lane_is_low_head():
    lane = lax.broadcasted_iota(jnp.int32, (1, GM_WIDTH), 1)
    return (lane & GM_HEAD_DIM) == 0


def _gm_mix(v_lo, v_hi, wpair_ref, bias_ref, mixed_ref, t):
    for n in range(t // CHUNK):
        rows = slice(n * CHUNK, (n + 1) * CHUNK)
        for j in range(GM_HEADS // 2):
            cols = slice(j * LANES, (j + 1) * LANES)
            rhs = jnp.concatenate([v_lo[rows, cols], v_hi[rows, cols]], axis=0)
            mixed_ref[rows, cols] = _dot(wpair_ref[j], rhs) + bias_ref[:, cols]


def _seqmix_fwd(hn, w_in, b_in, cw, cb, lng, lnb, gg, gb, wpair, bias, t, after=()):
    s = hn.shape[0]

    def body(hn_ref, w_ref, b_ref, cw_ref, cb_ref, lng_ref, lnb_ref, gg_ref, gb_ref, wpair_ref, bias_ref,
             z_ref, mix_ref, c1_ref, abuf, ash, mixed_ref):
        i = pl.program_id(0)

        @pl.when(i == 0)
        def _():
            abuf[0:CONV_HALO, :] = jnp.zeros((CONV_HALO, CONV_WIDTH), F32)

        @pl.when(i > 0)
        def _():
            abuf[0:CONV_HALO, :] = abuf[t:t + CONV_HALO, :]

        hv = hn_ref[...]
        for j in range(4):
            cols = slice(j * 512, (j + 1) * 512)
            z_ref[:, cols] = _dot(hv, w_ref[j]) + b_ref[:, cols]

        abuf[CONV_HALO:, :] = z_ref[:, 0:512] * _sigmoid(z_ref[:, 512:1024])
        _shift_rows(abuf, ash, t)
        acc = jnp.zeros((t, CONV_WIDTH), F32)
        for k in range(CONV_KERNEL):
            acc = acc + cw_ref[k:k + 1, :] * _window(abuf, ash, CONV_HALO - (CONV_KERNEL - 1) + k, t)
        c1 = acc + cb_ref[...]
        c1_ref[...] = c1
        xh, _ = _ln_stats(c1)
        ln = xh * lng_ref[...] + lnb_ref[...]
        mix_ref[:, 0:512] = (ln * _sigmoid(ln)).astype(BF16)

        u, _ = _gelu_parts(z_ref[:, 1024:1536])
        gv, _ = _gelu_parts(z_ref[:, 1536:2048])
        vxh, _ = _ln_stats(gv)
        v = vxh * gg_ref[...] + gb_ref[...]
        low = _lane_is_low_head()
        v_lo = jnp.where(low, v, 0.0).astype(BF16)
        v_hi = jnp.where(low, 0.0, v).astype(BF16)
        _gm_mix(v_lo, v_hi, wpair_ref, bias_ref, mixed_ref, t)
        mix_ref[:, 512:1024] = (u * mixed_ref[...]).astype(BF16)

    vec = lambda n: pl.BlockSpec((1, n), lambda i: (0, 0))
    return _tied_call(
        body, after, name="seqmix_fwd", grid=(s // t,),
        in_specs=[pl.BlockSpec((t, D_MODEL), lambda i: (i, 0)),
                  pl.BlockSpec((4, D_MODEL, 512), lambda i: (0, 0, 0)), vec(2048),
                  pl.BlockSpec((CONV_HALO, CONV_WIDTH), lambda i: (0, 0)),
                  vec(512), vec(512), vec(512), vec(512), vec(512),
                  pl.BlockSpec((4, CHUNK, 2 * CHUNK), lambda i: (0, 0, 0)),
                  pl.BlockSpec((CHUNK, GM_WIDTH), lambda i: (0, 0))],
        out_specs=[pl.BlockSpec((t, 2048), lambda i: (i, 0)),
                   pl.BlockSpec((t, D_MODEL), lambda i: (i, 0)),
                   pl.BlockSpec((t, CONV_WIDTH), lambda i: (i, 0))],
        out_shape=[jax.ShapeDtypeStruct((s, 2048), F32), jax.ShapeDtypeStruct((s, D_MODEL), BF16),
                   jax.ShapeDtypeStruct((s, CONV_WIDTH), F32)],
        scratch_shapes=[pltpu.VMEM((t + CONV_HALO, CONV_WIDTH), F32),
                        pltpu.VMEM((SUBLANES - 1, t + CONV_HALO - SUBLANES, CONV_WIDTH), F32),
                        pltpu.VMEM((t, GM_WIDTH), F32)],
        compiler_params=_params(("arbitrary",)),
    )(hn, w_in, b_in, cw, cb, lng, lnb, gg, gb, wpair, bias)


def _mem_kv(mem, g, wkv):
    m = mem.shape[0]

    def body(mem_ref, g_ref, w_ref, mn_ref, kv_ref):
        mv = mem_ref[...]
        mn = (mv * _rms_stats(mv) * g_ref[...]).astype(BF16)
        mn_ref[...] = mn
        for j in range(4):
            kv_ref[:, j * 512:(j + 1) * 512] = _dot(mn, w_ref[j]).astype(BF16)

    return pl.pallas_call(
        body, name="mem_kv",
        out_shape=[jax.ShapeDtypeStruct((m, D_MODEL), BF16), jax.ShapeDtypeStruct((m, 2 * D_MODEL), BF16)],
        compiler_params=pltpu.CompilerParams(vmem_limit_bytes=VMEM_LIMIT_BYTES),
    )(mem, g, wkv)


def _softmax_rows(sc):
    e = jnp.exp(sc - jnp.max(sc, axis=-1, keepdims=True))
    return e / jnp.sum(e, axis=-1, keepdims=True)


def _attn_block_fwd(x, mix, w_out, g_xa, wq, kv, wo, g_ffn, ts, after=()):
    s, m = x.shape[0], kv.shape[0]
    scale = XA_HEAD_DIM ** -0.5

    def body(x_ref, mix_ref, wout_ref, gxa_ref, wq_ref, kv_ref, wo_ref, gffn_ref,
             h1_ref, hn2_ref, q_ref, o_ref, h2_ref, hn3_ref):
        h1 = x_ref[...] + _dot(mix_ref[...], wout_ref[...])
        h1_ref[...] = h1
        hn2 = (h1 * _rms_stats(h1) * gxa_ref[...]).astype(BF16)
        hn2_ref[...] = hn2
        q_ref[...] = _dot(hn2, wq_ref[...]).astype(BF16)
        for h in range(XA_HEADS):
            cols = slice(h * XA_HEAD_DIM, (h + 1) * XA_HEAD_DIM)
            vcols = slice(D_MODEL + h * XA_HEAD_DIM, D_MODEL + (h + 1) * XA_HEAD_DIM)
            p = _softmax_rows(_dot_nt(q_ref[:, cols], kv_ref[:, cols]) * scale)
            o_ref[:, cols] = _dot(p.astype(BF16), kv_ref[:, vcols]).astype(BF16)
        h2 = h1 + _dot(o_ref[...], wo_ref[...])
        h2_ref[...] = h2
        hn3_ref[...] = (h2 * _rms_stats(h2) * gffn_ref[...]).astype(BF16)

    row = pl.BlockSpec((ts, D_MODEL), lambda i: (i, 0))
    full = pl.BlockSpec((D_MODEL, D_MODEL), lambda i: (0, 0))
    vec = pl.BlockSpec((1, D_MODEL), lambda i: (0, 0))
    f32 = jax.ShapeDtypeStruct((s, D_MODEL), F32)
    bf16 = jax.ShapeDtypeStruct((s, D_MODEL), BF16)
    return _tied_call(
        body, after, name="attn_block_fwd", grid=(s // ts,),
        in_specs=[row, row, full, vec, full, pl.BlockSpec((m, 2 * D_MODEL), lambda i: (0, 0)), full, vec],
        out_specs=[row] * 6,
        out_shape=[f32, bf16, bf16, bf16, f32, bf16],
        compiler_params=_params(("parallel",)),
    )(x, mix, w_out, g_xa, wq, kv, wo, g_ffn)


_FFN_CHUNKS = (slice(0, 8 * LANES), slice(8 * LANES, 16 * LANES), slice(16 * LANES, FFN_HIDDEN))


def _ffn_up(hn, wgu, ts, after=()):
    s = hn.shape[0]

    def body(hn_ref, w_ref, gu_ref, act_ref):
        hv = hn_ref[...]
        for cols in _FFN_CHUNKS:
            gate = _dot(hv, w_ref[0, :, cols])
            up = _dot(hv, w_ref[1, :, cols])
            gu_ref[0, :, cols] = gate.astype(BF16)
            gu_ref[1, :, cols] = up.astype(BF16)
            act_ref[:, cols] = (gate * _sigmoid(gate) * up).astype(BF16)

    return _tied_call(
        body, after, name="ffn_up", grid=(s // ts,),
        in_specs=[pl.BlockSpec((ts, D_MODEL), lambda i: (i, 0)),
                  pl.BlockSpec((2, D_MODEL, FFN_HIDDEN), lambda i: (0, 0, 0))],
        out_specs=[pl.BlockSpec((2, ts, FFN_HIDDEN), lambda i: (0, i, 0)),
                   pl.BlockSpec((ts, FFN_HIDDEN), lambda i: (i, 0))],
        out_shape=[jax.ShapeDtypeStruct((2, s, FFN_HIDDEN), BF16), jax.ShapeDtypeStruct((s, FFN_HIDDEN), BF16)],
        compiler_params=_params(("parallel",)),
    )(hn, wgu)


def _ffn_down_loss(act, wd, h2, g, target, ts):
    s = act.shape[0]

    def body(act_ref, wd_ref, h2_ref, g_ref, t_ref, dh_ref, sq_ref, dg_ref):
        @pl.when(pl.program_id(0) == 0)
        def _():
            sq_ref[...] = jnp.zeros_like(sq_ref)
            dg_ref[...] = jnp.zeros_like(dg_ref)

        h3 = h2_ref[...] + _dot(act_ref[...], wd_ref[...])
        r = _rms_stats(h3)
        gv = g_ref[...]
        diff = h3 * r * gv - t_ref[...]
        sq_ref[...] += _rowsum(diff * diff)
        dh, dg = _rms_bwd(diff / D_MODEL, h3, r, gv)
        dh_ref[...] = dh
        dg_ref[...] += dg

    row = pl.BlockSpec((ts, D_MODEL), lambda i: (i, 0))
    vec = pl.BlockSpec((1, D_MODEL), lambda i: (0, 0))
    return pl.pallas_call(
        body, name="ffn_down_loss", grid=(s // ts,),
        in_specs=[pl.BlockSpec((ts, FFN_HIDDEN), lambda i: (i, 0)),
                  pl.BlockSpec((FFN_HIDDEN, D_MODEL), lambda i: (0, 0)), row, vec, row],
        out_specs=[row, vec, vec],
        out_shape=[jax.ShapeDtypeStruct((s, D_MODEL), F32),
                   jax.ShapeDtypeStruct((1, D_MODEL), F32), jax.ShapeDtypeStruct((1, D_MODEL), F32)],
        compiler_params=_params(("arbitrary",)),
    )(act, wd, h2, g, target)


def _grad_w(a, b, tk, tn, name, after=(), shards=1):
    s, k = a.shape
    gb, _, n = b.shape
    nblk = n // tn
    ws = tn // shards
    tsr = GRAD_ROWS if s % GRAD_ROWS == 0 else s

    def body(a_ref, b_ref, o_ref):
        part = _dot_tn(a_ref[...], b_ref[0].astype(BF16))

        @pl.when(pl.program_id(2) == 0)
        def _():
            for j in range(shards):
                o_ref[j] = part[:, j * ws:(j + 1) * ws]

        @pl.when(pl.program_id(2) > 0)
        def _():
            for j in range(shards):
                o_ref[j] += part[:, j * ws:(j + 1) * ws]

    return _tied_call(
        body, after, name=name, grid=(gb * nblk, k // tk, s // tsr),
        in_specs=[pl.BlockSpec((tsr, tk), lambda ni, ki, si: (si, ki)),
                  pl.BlockSpec((1, tsr, tn), lambda ni, ki, si: (ni // nblk, si, ni % nblk))],
        out_specs=pl.BlockSpec((shards, tk, ws), lambda ni, ki, si: (ni, ki, 0)),
        out_shape=jax.ShapeDtypeStruct((gb * nblk * shards, k, ws), F32),
        compiler_params=_params(("parallel", "parallel", "arbitrary")),
    )(a, b)


def _grad_w_square(pairs, name, after=()):
    n = len(pairs)
    s = pairs[0][0].shape[0]
    tsr = GRAD_ROWS // 2 if s % (GRAD_ROWS // 2) == 0 else s

    def body(*refs):
        ins, outs = refs[:2 * n], refs[2 * n:]
        parts = [_dot_tn(ins[2 * a][...], ins[2 * a + 1][...]) for a in range(n)]

        @pl.when(pl.program_id(0) == 0)
        def _():
            for a in range(n):
                outs[a][...] = parts[a]

        @pl.when(pl.program_id(0) > 0)
        def _():
            for a in range(n):
                outs[a][...] += parts[a]

    row = pl.BlockSpec((tsr, D_MODEL), lambda i: (i, 0))
    return _tied_call(
        body, after, name=name, grid=(s // tsr,),
        in_specs=[row] * (2 * n), out_specs=[pl.BlockSpec((D_MODEL, D_MODEL), lambda i: (0, 0))] * n,
        out_shape=[jax.ShapeDtypeStruct((D_MODEL, D_MODEL), F32)] * n,
        compiler_params=_params(("arbitrary",)),
    )(*[x for p in pairs for x in p])


def _ffn_bwd(dh3, wd, gu, wgu, h2, g, t, after=()):
    s = dh3.shape[0]

    def body(dh3_ref, wd_ref, gu_ref, w_ref, h2_ref, g_ref, dgu_ref, dh2_ref, dh2b_ref, dg_ref):
        @pl.when(pl.program_id(0) == 0)
        def _():
            dg_ref[...] = jnp.zeros_like(dg_ref)

        dh3v = dh3_ref[...]
        dhb = dh3v.astype(BF16)
        for cols in _FFN_CHUNKS:
            dact = _dot_nt(dhb, wd_ref[cols, :])
            gate, up = gu_ref[0, :, cols].astype(F32), gu_ref[1, :, cols].astype(F32)
            sg = _sigmoid(gate)
            dgu_ref[0, :, cols] = (dact * up * (sg * (1.0 + gate * (1.0 - sg)))).astype(BF16)
            dgu_ref[1, :, cols] = (dact * (gate * sg)).astype(BF16)
        dhn = _dot_nt(dgu_ref[0], w_ref[0]) + _dot_nt(dgu_ref[1], w_ref[1])
        h2 = h2_ref[...]
        dv, dg = _rms_bwd(dhn, h2, _rms_stats(h2), g_ref[...])
        dh2 = dh3v + dv
        dh2_ref[...] = dh2
        dh2b_ref[...] = dh2.astype(BF16)
        dg_ref[...] += dg

    row = pl.BlockSpec((t, D_MODEL), lambda i: (i, 0))
    wide = pl.BlockSpec((2, t, FFN_HIDDEN), lambda i: (0, i, 0))
    vec = pl.BlockSpec((1, D_MODEL), lambda i: (0, 0))
    return _tied_call(
        body, after, name="ffn_bwd", grid=(s // t,),
        in_specs=[row, pl.BlockSpec((FFN_HIDDEN, D_MODEL), lambda i: (0, 0), pipeline_mode=pl.Buffered(1)), wide,
                  pl.BlockSpec((2, D_MODEL, FFN_HIDDEN), lambda i: (0, 0, 0), pipeline_mode=pl.Buffered(1)), row, vec],
        out_specs=[wide, row, row, vec],
        out_shape=[jax.ShapeDtypeStruct((2, s, FFN_HIDDEN), BF16), jax.ShapeDtypeStruct((s, D_MODEL), F32),
                   jax.ShapeDtypeStruct((s, D_MODEL), BF16), jax.ShapeDtypeStruct((1, D_MODEL), F32)],
        compiler_params=_params(("arbitrary",), FFN_BWD_VMEM_LIMIT_BYTES),
    )(dh3, wd, gu, wgu, h2, g)


def _attn_bwd(dh2, wo, q, kv, wq, h1, g, ts, after=()):
    s, m = q.shape[0], kv.shape[0]
    scale = XA_HEAD_DIM ** -0.5

    def body(dh2_ref, wo_ref, q_ref, kv_ref, wq_ref, h1_ref, g_ref, dh1_ref, dh1b_ref, dq_ref, dkv_ref, dg_ref):
        @pl.when(pl.program_id(0) == 0)
        def _():
            dkv_ref[...] = jnp.zeros_like(dkv_ref)
            dg_ref[...] = jnp.zeros_like(dg_ref)

        do = _dot_nt(dh2_ref[...].astype(BF16), wo_ref[...]).astype(BF16)
        for h in range(XA_HEADS):
            cols = slice(h * XA_HEAD_DIM, (h + 1) * XA_HEAD_DIM)
            vcols = slice(D_MODEL + h * XA_HEAD_DIM, D_MODEL + (h + 1) * XA_HEAD_DIM)
            qh, kh, vh, doh = q_ref[:, cols], kv_ref[:, cols], kv_ref[:, vcols], do[:, cols]
            p = _softmax_rows(_dot_nt(qh, kh) * scale)
            dp = _dot_nt(doh, vh)
            ds = (p * (dp - jnp.sum(dp * p, axis=-1, keepdims=True)) * scale).astype(BF16)
            dq_ref[:, cols] = _dot(ds, kh).astype(BF16)
            dkv_ref[:, cols] += _dot_tn(ds, qh)
            dkv_ref[:, vcols] += _dot_tn(p.astype(BF16), doh)
        dhn = _dot_nt(dq_ref[...], wq_ref[...])
        h1 = h1_ref[...]
        dv, dg = _rms_bwd(dhn, h1, _rms_stats(h1), g_ref[...])
        dh1 = dh2_ref[...] + dv
        dh1_ref[...] = dh1
        dh1b_ref[...] = dh1.astype(BF16)
        dg_ref[...] += dg

    row = pl.BlockSpec((ts, D_MODEL), lambda i: (i, 0))
    full = pl.BlockSpec((D_MODEL, D_MODEL), lambda i: (0, 0))
    kvs = pl.BlockSpec((m, 2 * D_MODEL), lambda i: (0, 0))
    vec = pl.BlockSpec((1, D_MODEL), lambda i: (0, 0))
    return _tied_call(
        body, after, name="attn_bwd", grid=(s // ts,),
        in_specs=[row, full, row, kvs, full, row, vec],
        out_specs=[row, row, row, kvs, vec],
        out_shape=[jax.ShapeDtypeStruct((s, D_MODEL), F32), jax.ShapeDtypeStruct((s, D_MODEL), BF16),
                   jax.ShapeDtypeStruct((s, D_MODEL), BF16),
                   jax.ShapeDtypeStruct((m, 2 * D_MODEL), F32), jax.ShapeDtypeStruct((1, D_MODEL), F32)],
        compiler_params=_params(("arbitrary",)),
    )(dh2, wo, q, kv, wq, h1, g)


def _mem_kv_bwd(dkv, mn, wkv, mem, g, after=()):
    m = mem.shape[0]

    def body(dkv_ref, mn_ref, w_ref, mem_ref, g_ref, dw_ref, dg_ref):
        dmn = jnp.zeros((m, D_MODEL), F32)
        mn = mn_ref[...]
        for j in range(4):
            dj = dkv_ref[:, j * 512:(j + 1) * 512].astype(BF16)
            dw_ref[j] = _dot_tn(mn, dj)
            dmn = dmn + _dot_nt(dj, w_ref[j])
        mv = mem_ref[...]
        dg_ref[...] = _rowsum(dmn * (mv * _rms_stats(mv)))

    return _tied_call(
        body, after, name="mem_kv_bwd", in_specs=[pl.BlockSpec(memory_space=pltpu.VMEM)] * 5,
        out_shape=[jax.ShapeDtypeStruct((4, D_MODEL, 512), F32), jax.ShapeDtypeStruct((1, D_MODEL), F32)],
        compiler_params=pltpu.CompilerParams(vmem_limit_bytes=VMEM_LIMIT_BYTES),
    )(dkv, mn, wkv, mem, g)


def _seqmix_bwd(dh1, x, z, c1, w_out, w_in, g_mix, cw, lng, lnb, gg, gb, wpair, wpair_t, bias, t, after=()):
    s = x.shape[0]
    nt = s // t

    def body(dh1_ref, x_ref, z_ref, c1_ref, wo_ref, wi_ref, gm_ref, cw_ref, lng_ref, lnb_ref,
             gg_ref, gb_ref, wpair_ref, wpt_ref, bias_ref,
             gx_ref, dz_ref, dcw_ref, dcb_ref, dlng_ref, dlnb_ref, dgg_ref, dgb_ref, dws_ref, dbs_ref,
             dbin_ref, dgm_ref, dbuf, dsh, mixed_ref, dv_ref):
        i = pl.program_id(0)
        accs = (dcw_ref, dcb_ref, dlng_ref, dlnb_ref, dgg_ref, dgb_ref, dws_ref, dbs_ref, dbin_ref, dgm_ref)

        @pl.when(i == 0)
        def _():
            for r in accs:
                r[...] = jnp.zeros_like(r)
            dbuf[t:t + CONV_HALO, :] = jnp.zeros((CONV_HALO, CONV_WIDTH), F32)

        @pl.when(i > 0)
        def _():
            dbuf[t:t + CONV_HALO, :] = dbuf[0:CONV_HALO, :]

        dmix = _dot_nt(dh1_ref[...].astype(BF16), wo_ref[...])

        xh, rs = _ln_stats(c1_ref[...])
        lng = lng_ref[...]
        ln = xh * lng + lnb_ref[...]
        sl = _sigmoid(ln)
        dln = dmix[:, 0:512] * (sl * (1.0 + ln * (1.0 - sl)))
        dc1, dg_ln, db_ln = _ln_bwd(dln, xh, rs, lng)
        dlng_ref[...] += dg_ln
        dlnb_ref[...] += db_ln
        dcb_ref[...] += _rowsum(dc1)
        dbuf[0:t, :] = dc1

        za = z_ref[:, 0:512]
        sg = _sigmoid(z_ref[:, 512:1024])
        a = za * sg
        _shift_rows(dbuf, dsh, t)

        da = jnp.zeros((t, CONV_WIDTH), F32)
        for k in range(CONV_KERNEL):
            later = _window(dbuf, dsh, CONV_KERNEL - 1 - k, t)
            da = da + cw_ref[k:k + 1, :] * later
            dcw_ref[k:k + 1, :] += _rowsum(a * later)
        dza = da * sg
        dzg = da * za * (sg * (1.0 - sg))
        dz_ref[:, 0:512] = dza.astype(BF16)
        dz_ref[:, 512:1024] = dzg.astype(BF16)
        dbin_ref[:, 0:512] += _rowsum(dza)
        dbin_ref[:, 512:1024] += _rowsum(dzg)

        dgm = dmix[:, 512:1024]
        u, du_dz = _gelu_parts(z_ref[:, 1024:1536])
        gv, dgv_dz = _gelu_parts(z_ref[:, 1536:2048])
        vxh, vrs = _ln_stats(gv)
        ggv = gg_ref[...]
        v = vxh * ggv + gb_ref[...]
        low = _lane_is_low_head()
        v_lo = jnp.where(low, v, 0.0).astype(BF16)
        v_hi = jnp.where(low, 0.0, v).astype(BF16)
        _gm_mix(v_lo, v_hi, wpair_ref, bias_ref, mixed_ref, t)
        dzu = dgm * mixed_ref[...] * du_dz
        dm = dgm * u
        dm_lo = jnp.where(low, dm, 0.0).astype(BF16)
        dm_hi = jnp.where(low, 0.0, dm).astype(BF16)
        vb = v.astype(BF16)
        tril = (lax.broadcasted_iota(jnp.int32, (CHUNK, CHUNK), 1)
                <= lax.broadcasted_iota(jnp.int32, (CHUNK, CHUNK), 0))
        for n in range(t // CHUNK):
            rows = slice(n * CHUNK, (n + 1) * CHUNK)
            dbs_ref[...] += dm[rows, :]
            for j in range(GM_HEADS // 2):
                cols = slice(j * LANES, (j + 1) * LANES)
                stack = jnp.concatenate([dm_lo[rows, cols], dm_hi[rows, cols]], axis=0)
                dws = _dot_nt(stack, vb[rows, cols])
                dws_ref[2 * j] += jnp.where(tril, dws[0:CHUNK], 0.0)
                dws_ref[2 * j + 1] += jnp.where(tril, dws[CHUNK:2 * CHUNK], 0.0)
                dv_ref[rows, cols] = _dot(wpt_ref[j], stack)
        dgv, dg_gm, db_gm = _ln_bwd(dv_ref[...], vxh, vrs, ggv)
        dgg_ref[...] += dg_gm
        dgb_ref[...] += db_gm
        dzv = dgv * dgv_dz
        dz_ref[:, 1024:1536] = dzu.astype(BF16)
        dz_ref[:, 1536:2048] = dzv.astype(BF16)
        dbin_ref[:, 1024:1536] += _rowsum(dzu)
        dbin_ref[:, 1536:2048] += _rowsum(dzv)

        dhn = jnp.zeros((t, D_MODEL), F32)
        for j in range(4):
            dhn = dhn + _dot_nt(dz_ref[:, j * 512:(j + 1) * 512], wi_ref[j])
        xv = x_ref[...]
        dv, dg = _rms_bwd(dhn, xv, _rms_stats(xv), gm_ref[...])
        gx_ref[...] = dh1_ref[...] + dv
        dgm_ref[...] += dg

    rev = lambda w: pl.BlockSpec((t, w), lambda i: (nt - 1 - i, 0))
    const = lambda *shape: pl.BlockSpec(shape, lambda i: (0,) * len(shape))
    f32 = lambda *shape: jax.ShapeDtypeStruct(shape, F32)
    return _tied_call(
        body, after, name="seqmix_bwd", grid=(nt,),
        in_specs=[rev(D_MODEL), rev(D_MODEL), rev(2048), rev(CONV_WIDTH),
                  const(D_MODEL, D_MODEL), const(4, D_MODEL, 512), const(1, D_MODEL),
                  const(CONV_HALO, CONV_WIDTH), const(1, 512), const(1, 512), const(1, 512), const(1, 512),
                  const(4, CHUNK, 2 * CHUNK), const(4, CHUNK, 2 * CHUNK), const(CHUNK, GM_WIDTH)],
        out_specs=[rev(D_MODEL), rev(2048),
                   const(CONV_HALO, CONV_WIDTH), const(1, 512), const(1, 512), const(1, 512), const(1, 512),
                   const(1, 512), const(GM_HEADS, CHUNK, CHUNK), const(CHUNK, GM_WIDTH), const(1, 2048),
                   const(1, D_MODEL)],
        out_shape=[f32(s, D_MODEL), jax.ShapeDtypeStruct((s, 2048), BF16),
                   f32(CONV_HALO, CONV_WIDTH), f32(1, 512), f32(1, 512), f32(1, 512), f32(1, 512),
                   f32(1, 512), f32(GM_HEADS, CHUNK, CHUNK), f32(CHUNK, GM_WIDTH), f32(1, 2048),
                   f32(1, D_MODEL)],
        scratch_shapes=[pltpu.VMEM((t + CONV_HALO, CONV_WIDTH), F32),
                        pltpu.VMEM((SUBLANES - 1, t + CONV_HALO - SUBLANES, CONV_WIDTH), F32),
                        pltpu.VMEM((t, GM_WIDTH), F32), pltpu.VMEM((t, GM_WIDTH), F32)],
        compiler_params=_params(("arbitrary",)),
    )(dh1, x, z, c1, w_out, w_in, g_mix, cw, lng, lnb, gg, gb, wpair, wpair_t, bias)


def _head_bias_grad(dbs):
    def body(d_ref, o_ref):
        dv = d_ref[...]
        lane = lax.broadcasted_iota(jnp.int32, (CHUNK, LANES), 1)
        acc = jnp.zeros((CHUNK, LANES), F32)
        for h in range(GM_HEADS):
            sh = jnp.sum(dv[:, h * GM_HEAD_DIM:(h + 1) * GM_HEAD_DIM], axis=-1, keepdims=True)
            acc = acc + jnp.where(lane == h, sh, 0.0)
        o_ref[...] = acc

    return pl.pallas_call(body, name="head_bias_grad",
                          out_shape=jax.ShapeDtypeStruct((CHUNK, LANES), F32))(dbs)


def kernel(x, mem, norm_mix_g, w_in, b_in, conv_w, conv_b, conv_ln_g, conv_ln_b, gm_ln_g, gm_ln_b, gm_w_s, gm_b_s, w_out, norm_xa_g, mem_norm_g, xa_wq, xa_wkv, xa_wo, norm_ffn_g, ffn_w_gate_up, ffn_w_down, final_norm_g, loss_target, m_norm_mix_g, m_w_in, m_b_in, m_conv_w, m_conv_b, m_conv_ln_g, m_conv_ln_b, m_gm_ln_g, m_gm_ln_b, m_gm_w_s, m_gm_b_s, m_w_out, m_norm_xa_g, m_mem_norm_g, m_xa_wq, m_xa_wkv, m_xa_wo, m_norm_ffn_g, m_ffn_w_gate_up, m_ffn_w_down, m_final_norm_g, v_norm_mix_g, v_w_in, v_b_in, v_conv_w, v_conv_b, v_conv_ln_g, v_conv_ln_b, v_gm_ln_g, v_gm_ln_b, v_gm_w_s, v_gm_b_s, v_w_out, v_norm_xa_g, v_mem_norm_g, v_xa_wq, v_xa_wkv, v_xa_wo, v_norm_ffn_g, v_ffn_w_gate_up, v_ffn_w_down, v_final_norm_g):
    weights = dict(norm_mix_g=norm_mix_g, w_in=w_in, b_in=b_in, conv_w=conv_w, conv_b=conv_b, conv_ln_g=conv_ln_g,
                   conv_ln_b=conv_ln_b, gm_ln_g=gm_ln_g, gm_ln_b=gm_ln_b, gm_w_s=gm_w_s, gm_b_s=gm_b_s, w_out=w_out,
                   norm_xa_g=norm_xa_g, mem_norm_g=mem_norm_g, xa_wq=xa_wq, xa_wkv=xa_wkv, xa_wo=xa_wo,
                   norm_ffn_g=norm_ffn_g, ffn_w_gate_up=ffn_w_gate_up, ffn_w_down=ffn_w_down,
                   final_norm_g=final_norm_g)
    m_in = dict(norm_mix_g=m_norm_mix_g, w_in=m_w_in, b_in=m_b_in, conv_w=m_conv_w, conv_b=m_conv_b,
                conv_ln_g=m_conv_ln_g, conv_ln_b=m_conv_ln_b, gm_ln_g=m_gm_ln_g, gm_ln_b=m_gm_ln_b, gm_w_s=m_gm_w_s,
                gm_b_s=m_gm_b_s, w_out=m_w_out, norm_xa_g=m_norm_xa_g, mem_norm_g=m_mem_norm_g, xa_wq=m_xa_wq,
                xa_wkv=m_xa_wkv, xa_wo=m_xa_wo, norm_ffn_g=m_norm_ffn_g, ffn_w_gate_up=m_ffn_w_gate_up,
                ffn_w_down=m_ffn_w_down, final_norm_g=m_final_norm_g)
    v_in = dict(norm_mix_g=v_norm_mix_g, w_in=v_w_in, b_in=v_b_in, conv_w=v_conv_w, conv_b=v_conv_b,
                conv_ln_g=v_conv_ln_g, conv_ln_b=v_conv_ln_b, gm_ln_g=v_gm_ln_g, gm_ln_b=v_gm_ln_b, gm_w_s=v_gm_w_s,
                gm_b_s=v_gm_b_s, w_out=v_w_out, norm_xa_g=v_norm_xa_g, mem_norm_g=v_mem_norm_g, xa_wq=v_xa_wq,
                xa_wkv=v_xa_wkv, xa_wo=v_xa_wo, norm_ffn_g=v_norm_ffn_g, ffn_w_gate_up=v_ffn_w_gate_up,
                ffn_w_down=v_ffn_w_down, final_norm_g=v_final_norm_g)
    grads, delta, new_m, new_v = {}, {}, {}, {}

    s = x.shape[1]
    ts = _row_tile(s)
    tb = max(CHUNK, ts // 2)
    tw = 2 * ts if s % (2 * ts) == 0 and ts >= 512 else ts
    cx, cy, cc = _mesh_pos()
    chip = 2 * cx + cy
    pos = jnp.stack([chip, cc]).astype(jnp.int32)
    row = lambda a: a.reshape(1, -1)
    x2, mem2, tgt2 = x[0], mem[0], loss_target[0]

    big = dict(w_in=w_in, xa_wkv=xa_wkv, w_out=w_out, xa_wq=xa_wq, xa_wo=xa_wo,
               ffn_w_gate_up=ffn_w_gate_up, ffn_w_down=ffn_w_down)
    big_names = list(big)
    halves = lambda a: a.reshape(2, a.shape[0] // 2, a.shape[1])
    conv_w_pad = jnp.pad(conv_w, ((0, CONV_HALO - CONV_KERNEL), (0, 0)))
    first_names = ["w_in", "conv_w"]
    later_names = [nm for nm in big_names if nm != "w_in"]
    cast = dict(zip(first_names, _cast_into_slots([halves(w_in), halves(conv_w_pad)], pos, [BF16, F32], "cast_w_in")))
    cast.update(zip(later_names, _cast_into_slots([halves(big[nm]) for nm in later_names], pos,
                                                  [BF16] * len(later_names), "cast_" + later_names[0],
                                                  side_by_side=(later_names.index("ffn_w_gate_up"),))))

    def start_gather(names, after):
        return _gather_start([cast[nm] for nm in names], "gather_start_" + names[0], after)

    def land_gather(names, started, after):
        send_sems, recv_sems, bufs, _ = started
        return _gather_wait(send_sems, recv_sems, bufs, after, "gather_wait_" + names[0])

    as_weights = lambda names, bufs: dict(zip(names, (b.reshape(b.shape[0], -1, b.shape[-1]) for b in bufs)))

    def start_share(names, landed):
        return _pass_start(landed, "pass_start_" + names[0])

    def share_gather(names, passing, which, after):
        send_sems, recv_sems, bufs, _ = passing
        sems = lambda s: [s[3 * a + k] for a in which for k in range(3)]
        picked = [names[a] for a in which]
        return as_weights(picked, _gather_wait(sems(send_sems), sems(recv_sems), [bufs[a] for a in which], after,
                                               "pass_wait_" + picked[0], _pass_descriptors))

    tril = jnp.tril(jnp.ones((CHUNK, CHUNK), dtype=bool))
    ws = jnp.where(tril[None], gm_w_s, 0.0)
    wpair = jnp.concatenate([ws[0::2], ws[1::2]], axis=2).astype(BF16)
    ws_t = jnp.swapaxes(ws, 1, 2)
    wpair_t = jnp.concatenate([ws_t[0::2], ws_t[1::2]], axis=2).astype(BF16)
    bias = jnp.repeat(gm_b_s.T, GM_HEAD_DIM, axis=1)

    attn_names = ["xa_wkv", "w_out", "xa_wq", "xa_wo"]
    gather_first = start_gather(first_names, ())
    hn1 = _norm_in(x2, row(norm_mix_g), tw, after=(gather_first[3], wpair, wpair_t, bias))
    landed = land_gather(first_names, gather_first, [cast[nm] for nm in later_names] + [hn1])
    passing = start_share(first_names, landed)
    gather_attn = start_gather(attn_names, passing[3])
    gw = share_gather(first_names, passing, (0, 1), gather_attn[3])
    w_in_g = gw["w_in"]
    cw_g = jnp.concatenate([gw["conv_w"][k] for k in range(N_CHIPS)], axis=1)

    z, mix, c1 = _seqmix_fwd(hn1, w_in_g, row(b_in), cw_g, row(conv_b), row(conv_ln_g), row(conv_ln_b),
                             row(gm_ln_g), row(gm_ln_b), wpair, bias, ts)
    landed = land_gather(attn_names, gather_attn, mix)
    passing = start_share(attn_names, landed)
    gather_gu = start_gather(["ffn_w_gate_up"], passing[3])
    wkv_g = share_gather(attn_names, passing, (0,), gather_gu[3])["xa_wkv"]
    mn, kv = _mem_kv(mem2, row(mem_norm_g), wkv_g)
    gw = share_gather(attn_names, passing, (1, 2, 3), kv)
    w_out_g = gw["w_out"].reshape(D_MODEL, D_MODEL)
    wq_g = gw["xa_wq"].reshape(D_MODEL, D_MODEL)
    wo_g = gw["xa_wo"].reshape(D_MODEL, D_MODEL)
    h1, hn2, q, o, h2, hn3 = _attn_block_fwd(x2, mix, w_out_g, row(norm_xa_g), wq_g, kv, wo_g, row(norm_ffn_g), ts)
    landed = land_gather(["ffn_w_gate_up"], gather_gu, hn3)
    passing = start_share(["ffn_w_gate_up"], landed)
    gather_down = start_gather(["ffn_w_down"], passing[3])
    wgu_g = share_gather(["ffn_w_gate_up"], passing, (0,), gather_down[3])["ffn_w_gate_up"]
    gu, act = _ffn_up(hn3, wgu_g, ts)
    landed = land_gather(["ffn_w_down"], gather_down, act)
    wd_g = as_weights(["ffn_w_down"], _pass_to_sibling(landed, "pass_ffn_w_down"))["ffn_w_down"].reshape(
        FFN_HIDDEN, D_MODEL)
    dh3, sq, d_final_g = _ffn_down_loss(act, wd_g, h2, row(final_norm_g), tgt2, ts)
    loss_here = jnp.broadcast_to(0.5 * jnp.sum(sq) / D_MODEL, (1, 2, SUBLANES, LANES))

    def split(g, nm):
        r, c = big[nm].shape
        return g.reshape(N_CHIPS, 2, r // 2, c)

    def chip_sums(group, arrays, got):
        sums, parts = [None] * len(group), [None] * len(group)
        for blocks in (N_CHIPS, 1):
            idx = [i for i, a in enumerate(arrays) if a.shape[0] == blocks]
            if idx:
                out = _add_halves([arrays[i] for i in idx], [got[i] for i in idx], pos, "chip_sum_" + group[idx[0]],
                                  [F32 if group[i] == "loss" else BF16 for i in idx])
                for k, i in enumerate(idx):
                    sums[i], parts[i] = out[0][k], out[1][k]
        return sums, parts

    def start_swap(group, grads, after=()):
        return _swap_start([split(g, nm) for g, nm in zip(grads, group)], "swap_start_" + group[0], after)

    def start_exchange(group, swapping, after, landed):
        arrays, got = _swap_wait(*swapping[:3], after, "swap_wait_" + group[0])
        sums, parts = chip_sums(group, arrays, got)
        return _exchange_start(sums, parts, "exchange_start_" + group[0], landed)

    def wait_exchange(group, started, after):
        sems, sums, parts, _ = started
        return _exchange_wait(sems, sums, parts, after, "exchange_wait_" + group[0])

    def finish_exchange(group, started, after):
        return _sum_chips(wait_exchange(group, started, after), pos, "total_" + group[0])

    def join(group, after):
        return _join_halves([halves_of[nm] for nm in group], "join_halves_" + group[0], after)

    def start_join(group, after):
        return _join_start([halves_of[nm] for nm in group], "join_start_" + group[0], after)

    def end_join(group, joining, after):
        return _join_wait(*joining[:2], after, "join_wait_" + group[0])

    def update(group, joined, after=()):
        outs = _adamw([(weights[nm], j.reshape(big[nm].shape), m_in[nm], v_in[nm]) for nm, j in zip(group, joined)],
                      "adamw_" + group[0], after)
        for nm, out in zip(group, outs):
            grads[nm], delta[nm], new_m[nm], new_v[nm] = out
        return [new_v[nm] for nm in group]

    def join_and_update(group, after):
        return update(group, join(group, after))

    as3 = lambda a: a.reshape((1,) + a.shape)
    halves_of = {}

    g_down = _grad_w(act, as3(dh3), FFN_HALF, D_MODEL, "grad_ffn_w_down")
    group_a = ["ffn_w_down"]
    swap_a = start_swap(group_a, [g_down])
    dgu, dh2, dh2_b, d_ffn_g = _ffn_bwd(dh3, wd_g, gu, wgu_g, h2, row(norm_ffn_g), ts,
                                        after=swap_a[3])
    exch_a = start_exchange(group_a, swap_a, dh2, wd_g)
    g_gu = _grad_w(hn3, dgu, D_MODEL, FFN_HALF, "grad_ffn_w_gate_up", after=exch_a[3])
    halves_of.update(zip(group_a, finish_exchange(group_a, exch_a, g_gu)))

    group_b = ["ffn_w_gate_up"]
    swap_b = start_swap(group_b, [g_gu])
    dh1, dh1_b, dq, dkv, d_xa_g = _attn_bwd(dh2, wo_g, q, kv, wq_g, h1, row(norm_xa_g), ts, after=swap_b[3])
    exch_b = start_exchange(group_b, swap_b, dh1, [halves_of[nm] for nm in group_a])
    joining_a = start_join(group_a, exch_b[3])
    g_wkv, d_mem_g = _mem_kv_bwd(dkv, mn, wkv_g, mem2, row(mem_norm_g), after=joining_a[2])
    g_wo, g_wq, g_wout = _grad_w_square([(o, dh2_b), (hn2, dq), (mix, dh1_b)], "grad_xa_wo", after=joining_a[2])
    done_a = update(group_a, end_join(group_a, joining_a, (g_wkv, g_wo, g_wq, g_wout)))
    halves_of.update(zip(group_b, finish_exchange(group_b, exch_b, done_a)))
    joining_b = start_join(group_b, done_a)
    group_c = ["xa_wo", "xa_wq", "xa_wkv", "w_out"]
    swap_c = start_swap(group_c, [g_wo, g_wq, g_wkv, g_wout], joining_b[2])
    (gx, dz, d_cw, d_cb, d_lng, d_lnb, d_gg, d_gb, d_ws, d_bs_sum, d_bin, d_mix_g) = _seqmix_bwd(
        dh1, x2, z, c1, w_out_g, w_in_g, row(norm_mix_g), cw_g, row(conv_ln_g), row(conv_ln_b),
        row(gm_ln_g), row(gm_ln_b), wpair, wpair_t, bias, tb, after=swap_c[3])
    d_bs = _head_bias_grad(d_bs_sum)[:, :GM_HEADS].T
    exch_c = start_exchange(group_c, swap_c, dz, joining_b[2])
    g_win = _grad_w(hn1, as3(dz), D_MODEL, 1024, "grad_w_in", after=exch_c[3], shards=2)

    small_names = ["norm_mix_g", "b_in", "conv_w", "conv_b", "conv_ln_g", "conv_ln_b", "gm_ln_g", "gm_ln_b",
                   "gm_w_s", "gm_b_s", "norm_xa_g", "mem_norm_g", "norm_ffn_g", "final_norm_g"]
    d_cw_by_chip = jnp.swapaxes(d_cw.reshape(CONV_HALO, N_CHIPS, LANES), 0, 1).reshape(-1, LANES)
    small_grads = dict(norm_mix_g=d_mix_g, b_in=d_bin, conv_w=d_cw_by_chip, conv_b=d_cb, conv_ln_g=d_lng,
                       conv_ln_b=d_lnb, gm_ln_g=d_gg, gm_ln_b=d_gb, gm_w_s=d_ws, gm_b_s=d_bs, norm_xa_g=d_xa_g,
                       mem_norm_g=d_mem_g, norm_ffn_g=d_ffn_g, final_norm_g=d_final_g)

    def rows_form(a):
        a = a.reshape(-1, LANES)
        return jnp.pad(a, ((0, -a.shape[0] % SUBLANES), (0, 0)))

    pieces = [rows_form(small_grads[nm]) for nm in small_names]
    offsets, total = [], 0
    for p in pieces:
        offsets.append(total)
        total += p.shape[0]
    pack_rows = -(-total // 32) * 32
    small_pack = jnp.pad(jnp.concatenate(pieces, axis=0), ((0, pack_rows - total), (0, 0)))

    group_d = ["w_in", "small", "loss"]
    joined_b = end_join(group_b, joining_b, g_win)
    swap_d = _swap_start([split(g_win, "w_in")], "swap_start_w_in", joined_b)
    done_b = update(group_b, joined_b, swap_d[3])
    small_d = [small_pack.reshape(1, 2, pack_rows // 2, LANES), loss_here]
    got_small = _swap_halves(small_d, "swap_halves_small", done_b)
    arrays_d, got_d = _swap_wait(*swap_d[:3], got_small, "swap_wait_w_in")
    sums_d, parts_d = chip_sums(group_d, arrays_d + small_d, got_d + list(got_small))
    parts_c = wait_exchange(group_c, exch_c, sums_d)
    exch_d = _exchange_start(sums_d, parts_d, "exchange_start_w_in", parts_c)
    halves_of.update(zip(group_c, _sum_chips(parts_c, pos, "total_xa_wo", exch_d[3])))
    done_c = join_and_update(group_c, exch_d[3])
    halves_of.update(zip(group_d, finish_exchange(group_d, exch_d, done_c)))
    joined_d = _join_halves([halves_of[nm] for nm in group_d], "join_halves_w_in")
    loss = joined_d[2][0, 0, 0]
    grads["w_in"], delta["w_in"], new_m["w_in"], new_v["w_in"] = _adamw(
        [(w_in, joined_d[0].reshape(w_in.shape), m_w_in, v_w_in)], "adamw_w_in")[0]

    local_rows = lambda a, nm: a if nm == "conv_w" else a.reshape(-1, LANES)
    params = [tuple(local_rows(src[nm], nm) for src in (weights, m_in, v_in)) for nm in small_names]
    outs = _adamw_small(joined_d[1].reshape(pack_rows, LANES), pos, params, offsets, small_names.index("conv_w"))
    for k, nm in enumerate(small_names):
        for dst, a in zip((grads, delta, new_m, new_v), outs[4 * k:4 * k + 4]):
            dst[nm] = a

    order = ["norm_mix_g", "w_in", "b_in", "conv_w", "conv_b", "conv_ln_g", "conv_ln_b", "gm_ln_g", "gm_ln_b",
             "gm_w_s", "gm_b_s", "w_out", "norm_xa_g", "mem_norm_g", "xa_wq", "xa_wkv", "xa_wo", "norm_ffn_g",
             "ffn_w_gate_up", "ffn_w_down", "final_norm_g"]
    fit = lambda a, nm: a.reshape(weights[nm].shape)
    return (loss, gx.reshape(x.shape),
            *[fit(grads[nm], nm) for nm in order], *[fit(delta[nm], nm) for nm in order],
            *[fit(new_m[nm], nm) for nm in order], *[fit(new_v[nm], nm) for nm in order])
```

```python
import functools

import jax
import jax.numpy as jnp
from jax import lax
from jax.experimental import pallas as pl
from jax.experimental.pallas import tpu as pltpu

F32 = jnp.float32
BF16 = jnp.bfloat16

D_MODEL = 1024
CONV_WIDTH = 512
GM_WIDTH = 512
CONV_KERNEL = 31
CONV_HALO = 32
GRAD_ROWS = 2048
CHUNK = 128
GM_HEADS = 8
GM_HEAD_DIM = 64
XA_HEADS = 4
XA_HEAD_DIM = 256
FFN_HIDDEN = 2816
FFN_HALF = FFN_HIDDEN // 2
RMS_EPS = 1e-6
LN_EPS = 1e-5
N_CHIPS = 4
LANES = 128
SUBLANES = 8

ADAM_LR = 0.001
ADAM_B1 = 0.9
ADAM_B2 = 0.999
ADAM_EPS = 1e-08
ADAM_WD = 0.01
ADAM_STEP = 10

VMEM_LIMIT_BYTES = 56 * 1024 * 1024
FFN_BWD_VMEM_LIMIT_BYTES = 60 * 1024 * 1024
MESH = pl.DeviceIdType.MESH
ANY = pl.BlockSpec(memory_space=pl.ANY)
HBM_SPEC = pl.BlockSpec(memory_space=pltpu.HBM)
SEM_SPEC = pl.BlockSpec(memory_space=pltpu.SEMAPHORE)

_NT = (((1,), (1,)), ((), ()))
_TN = (((0,), (0,)), ((), ()))
_GELU_C = 0.7978845608028654
_GELU_A = 0.044715


def _dot(a, b):
    return jnp.dot(a, b, preferred_element_type=F32)


def _dot_nt(a, b):
    return lax.dot_general(a, b, _NT, preferred_element_type=F32)


def _dot_tn(a, b):
    return lax.dot_general(a, b, _TN, preferred_element_type=F32)


def _mean(v):
    return jnp.mean(v, axis=-1, keepdims=True)


def _rowsum(v):
    return jnp.sum(v, axis=0, keepdims=True)


def _sigmoid(v):
    return 1.0 / (1.0 + jnp.exp(-v))


def _gelu_parts(v):
    v2 = v * v
    t = jnp.tanh(_GELU_C * (v + _GELU_A * v * v2))
    g = 0.5 * v * (1.0 + t)
    dg = 0.5 * (1.0 + t) + 0.5 * v * (1.0 - t * t) * (_GELU_C * (1.0 + 3.0 * _GELU_A * v2))
    return g, dg


def _rms_stats(v):
    return lax.rsqrt(_mean(v * v) + RMS_EPS)


def _rms_bwd(dy, v, r, g):
    n = v * r
    dn = dy * g
    dv = r * (dn - n * _mean(dn * n))
    return dv, _rowsum(dy * n)


def _ln_stats(v):
    mu = _mean(v)
    xc = v - mu
    rs = lax.rsqrt(_mean(xc * xc) + LN_EPS)
    return xc * rs, rs


def _ln_bwd(dy, xh, rs, g):
    dxh = dy * g
    dv = rs * (dxh - _mean(dxh) - xh * _mean(dxh * xh))
    return dv, _rowsum(dy * xh), _rowsum(dy)


def _params(sem, vmem_limit_bytes=VMEM_LIMIT_BYTES):
    return pltpu.CompilerParams(dimension_semantics=sem, vmem_limit_bytes=vmem_limit_bytes)


def _row_tile(s):
    return 512 if s % 512 == 0 and s >= 2048 else 128


def _mesh_pos():
    return lax.axis_index("x"), lax.axis_index("y"), lax.axis_index("c")


def _slot(buf, chip_idx, half):
    if buf.shape[0] == N_CHIPS:
        return buf.at[chip_idx, half]
    width = buf.shape[-1] // 2
    return buf.at[chip_idx // 2, half, :, pl.ds(pl.multiple_of((chip_idx % 2) * width, LANES), width)]


def _cast_into_slots(ws, pos, dtypes, name, side_by_side=()):
    n = len(ws)

    def body(pos_ref, *refs):
        for a in range(n):
            refs[n + a][0] = refs[a][...].astype(dtypes[a])

    def out_spec(a, w):
        if a in side_by_side:
            return pl.BlockSpec((1, 1) + w.shape[1:], lambda i, p: (p[0] // 2, i, 0, p[0] % 2))
        return pl.BlockSpec((1, 1) + w.shape[1:], lambda i, p: (p[0], i, 0, 0))

    def out_shape(a, w):
        if a in side_by_side:
            return (2, 2, w.shape[1], 2 * w.shape[2])
        return (N_CHIPS,) + w.shape

    return pl.pallas_call(
        body, name=name,
        grid_spec=pltpu.PrefetchScalarGridSpec(
            num_scalar_prefetch=1, grid=(2,),
            in_specs=[pl.BlockSpec((1,) + w.shape[1:], lambda i, p: (i, 0, 0)) for w in ws],
            out_specs=[out_spec(a, w) for a, w in enumerate(ws)]),
        out_shape=[jax.ShapeDtypeStruct(out_shape(a, w), dt) for a, (w, dt) in enumerate(zip(ws, dtypes))],
        compiler_params=_params(("parallel",)),
    )(pos, *ws)


def _adam_update(w, g, m, v):
    nm = ADAM_B1 * m + (1.0 - ADAM_B1) * g
    nv = ADAM_B2 * v + (1.0 - ADAM_B2) * (g * g)
    m_hat = nm / (1.0 - ADAM_B1 ** ADAM_STEP)
    v_hat = nv / (1.0 - ADAM_B2 ** ADAM_STEP)
    return -ADAM_LR * (m_hat / (jnp.sqrt(v_hat) + ADAM_EPS) + ADAM_WD * w), nm, nv


ADAM_STEPS = 4


def _adamw(quads, name, after=()):
    n = len(quads)

    def body(*refs):
        ins, outs = refs[:4 * n], refs[4 * n:]
        for a in range(n):
            w, g, m, v = (r[...] for r in ins[4 * a:4 * a + 4])
            outs[4 * a][...] = g
            outs[4 * a + 1][...], outs[4 * a + 2][...], outs[4 * a + 3][...] = _adam_update(w, g, m, v)

    specs = [pl.BlockSpec((q[0].shape[0] // ADAM_STEPS, q[0].shape[1]), lambda i: (i, 0)) for q in quads]
    out = _tied_call(
        body, after, name=name, grid=(ADAM_STEPS,),
        in_specs=[sp for sp in specs for _ in range(4)], out_specs=[sp for sp in specs for _ in range(4)],
        out_shape=[jax.ShapeDtypeStruct(q[0].shape, F32) for q in quads for _ in range(4)],
        compiler_params=_params(("parallel",)),
    )(*[a for q in quads for a in q])
    return [tuple(out[4 * a:4 * a + 4]) for a in range(n)]


def _adamw_small(gpack, pos, params, offsets, conv_at):
    n = len(params)

    def body(pos_ref, g_ref, *refs):
        ins, outs = refs[:3 * n], refs[3 * n:]
        for k in range(n):
            rows = params[k][0].shape[0]
            start = offsets[k]
            if k == conv_at:
                start = pl.multiple_of(start + pos_ref[0] * CONV_HALO, SUBLANES)
            g = g_ref[pl.ds(start, rows), :]
            outs[4 * k][...] = g
            outs[4 * k + 1][...], outs[4 * k + 2][...], outs[4 * k + 3][...] = _adam_update(
                ins[3 * k][...], g, ins[3 * k + 1][...], ins[3 * k + 2][...])

    flat = [a for p in params for a in p]
    vmem = pl.BlockSpec(memory_space=pltpu.VMEM)
    return pl.pallas_call(
        body, name="adamw_small",
        in_specs=[pl.BlockSpec(memory_space=pltpu.SMEM), vmem] + [vmem] * len(flat),
        out_specs=[vmem] * (4 * n),
        out_shape=[jax.ShapeDtypeStruct(p[0].shape, F32) for p in params for _ in range(4)],
    )(pos, gpack, *flat)


def _as_tuple(after):
    return tuple(after) if isinstance(after, (tuple, list)) else (after,)


def _tied_call(body, after, *, in_specs, **kwargs):
    after = _as_tuple(after)
    n_in, n_after = len(in_specs), len(after)

    def tied(*refs):
        body(*refs[:n_in], *refs[n_in + n_after:])

    call = pl.pallas_call(tied, in_specs=list(in_specs) + [ANY] * n_after, **kwargs)
    return lambda *operands: call(*operands, *after)


def _other_chips(x, y):
    return [(1 - x, y), (x, 1 - y), (1 - x, 1 - y)]


def _gather_descriptors(bufs, send_of, recv_of):
    x, y, c = _mesh_pos()
    me = 2 * x + y
    chips = _other_chips(x, y)
    sends, arrivals = [], []
    for a in range(len(bufs)):
        for k in range(3):
            ck = 2 * chips[k][0] + chips[k][1]

            def copy(slot, a=a, k=k):
                return pltpu.make_async_remote_copy(
                    src_ref=_slot(bufs[a], slot, c), dst_ref=_slot(bufs[a], slot, c),
                    send_sem=send_of(a, k), recv_sem=recv_of(a, k),
                    device_id=(*chips[k], c), device_id_type=MESH)

            sends.append(functools.partial(copy, me))
            arrivals.append(functools.partial(copy, ck))
    return sends, arrivals


def _gather_start(bufs, name, after=()):
    n = len(bufs)
    ns = 3 * n

    def body(*refs):
        sems = refs[n:n + 2 * ns]
        thru = refs[n + 2 * ns:2 * n + 2 * ns]
        token = refs[2 * n + 2 * ns]
        _chips_handshake()
        sends, _ = _gather_descriptors(thru, lambda a, k: sems[3 * a + k], lambda a, k: sems[ns + 3 * a + k])
        for cp in sends:
            cp().start()
        token[...] = jnp.zeros_like(token)

    held = [pltpu.with_memory_space_constraint(b, pltpu.HBM) for b in bufs]
    out = _tied_call(
        body, after, name=name,
        out_shape=(*[pltpu.SemaphoreType.DMA(())] * (2 * ns), *[pltpu.HBM(b.shape, b.dtype) for b in held],
                   jax.ShapeDtypeStruct((8, LANES), F32)),
        in_specs=[HBM_SPEC] * n,
        out_specs=(*[SEM_SPEC] * (2 * ns), *[HBM_SPEC] * n, pl.BlockSpec(memory_space=pltpu.VMEM)),
        input_output_aliases={i: 2 * ns + i for i in range(n)},
        compiler_params=pltpu.CompilerParams(has_side_effects=pltpu.SideEffectType.DATAFLOW_SIDE_EFFECTING,
                                             collective_id=CHIPS_COLLECTIVE_ID),
    )(*held)
    return list(out[:ns]), list(out[ns:2 * ns]), list(out[2 * ns:2 * ns + n]), out[2 * ns + n]


def _gather_wait(send_sems, recv_sems, bufs, after, name, descriptors=_gather_descriptors):
    n = len(bufs)
    ns = 3 * n

    def body(*refs):
        buf_ref = refs[:n]
        sem_ref = refs[n:n + 2 * ns]
        sends, arrivals = descriptors(buf_ref, lambda a, k: sem_ref[3 * a + k], lambda a, k: sem_ref[ns + 3 * a + k])
        for cp in sends:
            cp().wait_send()
        for cp in arrivals:
            cp().wait_recv()

    out = pl.pallas_call(
        body, name=name,
        out_shape=tuple(pltpu.HBM(b.shape, b.dtype) for b in bufs),
        in_specs=[HBM_SPEC] * n + [SEM_SPEC] * (2 * ns) + [ANY] * len(_as_tuple(after)),
        out_specs=tuple([HBM_SPEC] * n),
        input_output_aliases={i: i for i in range(n)},
        compiler_params=pltpu.CompilerParams(has_side_effects=pltpu.SideEffectType.DATAFLOW_SIDE_EFFECTING),
    )(*bufs, *send_sems, *recv_sems, *_as_tuple(after))
    return list(out)


SIBLING_COLLECTIVE_ID = 0


def _sibling_handshake():
    x, y, c = _mesh_pos()
    barrier = pltpu.get_barrier_semaphore()
    pl.semaphore_signal(barrier, inc=1, device_id=(x, y, 1 - c), device_id_type=MESH)
    pl.semaphore_wait(barrier, 1)


CHIPS_COLLECTIVE_ID = 1


def _chips_handshake():
    x, y, c = _mesh_pos()
    barrier = pltpu.get_barrier_semaphore()
    for chip in _other_chips(x, y):
        pl.semaphore_signal(barrier, inc=1, device_id=(*chip, c), device_id_type=MESH)
    pl.semaphore_wait(barrier, 3)


def _pass_descriptors(bufs, send_of, recv_of):
    x, y, c = _mesh_pos()
    chips = _other_chips(x, y)

    def half(a, k, which):
        ck = 2 * chips[k][0] + chips[k][1]
        return functools.partial(
            pltpu.make_async_remote_copy,
            src_ref=_slot(bufs[a], ck, which), dst_ref=_slot(bufs[a], ck, which),
            send_sem=send_of(a, k), recv_sem=recv_of(a, k),
            device_id=(x, y, 1 - c), device_id_type=MESH)

    pairs = [(a, k) for a in range(len(bufs)) for k in range(3)]
    return [half(a, k, c) for a, k in pairs], [half(a, k, 1 - c) for a, k in pairs]


def _pass_start(bufs, name, after=()):
    n = len(bufs)
    ns = 3 * n

    def body(*refs):
        sems = refs[n:n + 2 * ns]
        thru = refs[n + 2 * ns:2 * n + 2 * ns]
        token = refs[2 * n + 2 * ns]
        _sibling_handshake()
        sends, _ = _pass_descriptors(thru, lambda a, k: sems[3 * a + k], lambda a, k: sems[ns + 3 * a + k])
        for cp in sends:
            cp().start()
        token[...] = jnp.zeros_like(token)

    held = [pltpu.with_memory_space_constraint(b, pltpu.HBM) for b in bufs]
    out = _tied_call(
        body, after, name=name,
        out_shape=(*[pltpu.SemaphoreType.DMA(())] * (2 * ns), *[pltpu.HBM(b.shape, b.dtype) for b in held],
                   jax.ShapeDtypeStruct((8, LANES), F32)),
        in_specs=[HBM_SPEC] * n,
        out_specs=(*[SEM_SPEC] * (2 * ns), *[HBM_SPEC] * n, pl.BlockSpec(memory_space=pltpu.VMEM)),
        input_output_aliases={i: 2 * ns + i for i in range(n)},
        compiler_params=pltpu.CompilerParams(has_side_effects=pltpu.SideEffectType.DATAFLOW_SIDE_EFFECTING,
                                             collective_id=SIBLING_COLLECTIVE_ID),
    )(*held)
    return list(out[:ns]), list(out[ns:2 * ns]), list(out[2 * ns:2 * ns + n]), out[2 * ns + n]


def _pass_to_sibling(bufs, name, after=()):
    n = len(bufs)

    def body(*refs):
        outs = refs[n:2 * n]
        send_sem, recv_sem = refs[2 * n:]
        _sibling_handshake()
        sends, arrivals = _pass_descriptors(outs, lambda a, k: send_sem.at[a, k], lambda a, k: recv_sem.at[a, k])
        sends = [cp() for cp in sends]
        for cp in sends:
            cp.start()
        for cp in arrivals:
            cp().wait_recv()
        for cp in sends:
            cp.wait_send()

    return _tied_call(
        body, after, name=name,
        in_specs=[ANY] * n, out_specs=[ANY] * n,
        out_shape=[jax.ShapeDtypeStruct(b.shape, b.dtype) for b in bufs],
        input_output_aliases={a: a for a in range(n)},
        scratch_shapes=[pltpu.SemaphoreType.DMA((n, 3))] * 2,
        compiler_params=pltpu.CompilerParams(collective_id=SIBLING_COLLECTIVE_ID),
    )(*bufs)


def _swap_descriptors(grads, lands, send_of, recv_of):
    x, y, c = _mesh_pos()
    return [functools.partial(
        pltpu.make_async_remote_copy,
        src_ref=grads[a].at[:, pl.ds(1 - c, 1)], dst_ref=lands[a],
        send_sem=send_of(a), recv_sem=recv_of(a),
        device_id=(x, y, 1 - c), device_id_type=MESH) for a in range(len(grads))]


def _swap_halves(grads, name, after=()):
    n = len(grads)

    def body(*refs):
        ins, outs = refs[:n], refs[n:2 * n]
        send_sem, recv_sem = refs[2 * n:]
        _sibling_handshake()
        cps = [cp() for cp in _swap_descriptors(ins, outs, lambda a: send_sem.at[a], lambda a: recv_sem.at[a])]
        for cp in cps:
            cp.start()
        for cp in cps:
            cp.wait()

    out_shape = [jax.ShapeDtypeStruct((g.shape[0], 1) + g.shape[2:], g.dtype) for g in grads]
    return _tied_call(
        body, after, name=name,
        in_specs=[ANY] * n, out_specs=[ANY] * n, out_shape=out_shape,
        scratch_shapes=[pltpu.SemaphoreType.DMA((n,))] * 2,
        compiler_params=pltpu.CompilerParams(collective_id=SIBLING_COLLECTIVE_ID),
    )(*grads)


def _swap_start(grads, name, after=()):
    n, after = len(grads), _as_tuple(after)

    def body(*refs):
        outs = refs[2 * n + len(after):]
        sems, g_thru, l_thru, token = outs[:2 * n], outs[2 * n:3 * n], outs[3 * n:4 * n], outs[4 * n]
        _sibling_handshake()
        for cp in _swap_descriptors(g_thru, l_thru, lambda a: sems[a], lambda a: sems[n + a]):
            cp().start()
        token[...] = jnp.zeros_like(token)

    lands = [lax.empty((g.shape[0], 1) + g.shape[2:], g.dtype) for g in grads]
    held = [pltpu.with_memory_space_constraint(a, pltpu.HBM) for a in (*grads, *lands)]
    out = pl.pallas_call(
        body, name=name,
        out_shape=(*[pltpu.SemaphoreType.DMA(())] * (2 * n), *[pltpu.HBM(a.shape, a.dtype) for a in held],
                   jax.ShapeDtypeStruct((8, LANES), F32)),
        in_specs=[HBM_SPEC] * (2 * n) + [ANY] * len(after),
        out_specs=(*[SEM_SPEC] * (2 * n), *[HBM_SPEC] * (2 * n), pl.BlockSpec(memory_space=pltpu.VMEM)),
        input_output_aliases={i: 2 * n + i for i in range(2 * n)},
        compiler_params=pltpu.CompilerParams(has_side_effects=pltpu.SideEffectType.DATAFLOW_SIDE_EFFECTING,
                                             collective_id=SIBLING_COLLECTIVE_ID),
    )(*held, *after)
    return list(out[:2 * n]), list(out[2 * n:3 * n]), list(out[3 * n:4 * n]), out[4 * n]


def _swap_wait(sems, grads, lands, after, name):
    n = len(grads)

    def body(*refs):
        g_ref, l_ref = refs[:n], refs[n:2 * n]
        sem_ref = refs[2 * n:4 * n]
        for cp in _swap_descriptors(g_ref, l_ref, lambda a: sem_ref[a], lambda a: sem_ref[n + a]):
            cp().wait()

    out = pl.pallas_call(
        body, name=name,
        out_shape=tuple(pltpu.HBM(a.shape, a.dtype) for a in (*grads, *lands)),
        in_specs=[HBM_SPEC] * (2 * n) + [SEM_SPEC] * (2 * n) + [ANY] * len(_as_tuple(after)),
        out_specs=tuple([HBM_SPEC] * (2 * n)),
        input_output_aliases={i: i for i in range(2 * n)},
        compiler_params=pltpu.CompilerParams(has_side_effects=pltpu.SideEffectType.DATAFLOW_SIDE_EFFECTING),
    )(*grads, *lands, *sems, *_as_tuple(after))
    return list(out[:n]), list(out[n:])


def _add_halves(gs, gots, pos, name, dtypes):
    n = len(gs)
    j = gs[0].shape[0]

    def body(pos_ref, *refs):
        g_refs, r_refs = refs[:n], refs[n:2 * n]
        o_refs, p_refs = refs[2 * n:3 * n], refs[3 * n:]
        vals = [(g_refs[a][0, 0] + r_refs[a][0, 0]).astype(dtypes[a]) for a in range(n)]
        for a in range(n):
            o_refs[a][0] = vals[a]
        if j == 1:
            for a in range(n):
                p_refs[a][0] = vals[a]
        else:
            @pl.when(pl.program_id(0) == pos_ref[0])
            def _():
                for a in range(n):
                    p_refs[a][0] = vals[a]

    blk = lambda g: (1,) + g.shape[2:]
    out = pl.pallas_call(
        body, name=name,
        grid_spec=pltpu.PrefetchScalarGridSpec(
            num_scalar_prefetch=1, grid=(j,),
            in_specs=[pl.BlockSpec((1,) + blk(g), lambda i, p: (i, p[1], 0, 0)) for g in gs]
            + [pl.BlockSpec((1,) + blk(g), lambda i, p: (i, 0, 0, 0)) for g in gs],
            out_specs=[pl.BlockSpec(blk(g), lambda i, p: (i, 0, 0)) for g in gs]
            + [pl.BlockSpec(blk(g), lambda i, p: (p[0], 0, 0)) for g in gs]),
        out_shape=[jax.ShapeDtypeStruct((j,) + g.shape[2:], dt) for g, dt in zip(gs, dtypes)]
        + [jax.ShapeDtypeStruct((N_CHIPS,) + g.shape[2:], dt) for g, dt in zip(gs, dtypes)],
        compiler_params=_params(("arbitrary",)),
    )(pos, *gs, *gots)
    return list(out[:n]), list(out[n:])


def _exchange_descriptors(sums, parts, send_of, recv_of):
    x, y, c = _mesh_pos()
    me = 2 * x + y
    chips = _other_chips(x, y)
    sends, arrivals = [], []
    for a in range(len(sums)):
        for k in range(3):
            ck = 2 * chips[k][0] + chips[k][1]
            mine = sums[a].at[ck] if sums[a].shape[0] == N_CHIPS else sums[a].at[0]

            def copy(dst_slot, a=a, k=k, mine=mine):
                return pltpu.make_async_remote_copy(
                    src_ref=mine, dst_ref=parts[a].at[dst_slot],
                    send_sem=send_of(a, k), recv_sem=recv_of(a, k),
                    device_id=(*chips[k], c), device_id_type=MESH)

            sends.append(functools.partial(copy, me))
            arrivals.append(functools.partial(copy, ck))
    return sends, arrivals


def _exchange_start(sums, parts, name, after=()):
    n = len(sums)
    ns = 3 * n

    def body(*refs):
        sems = refs[2 * n:2 * n + 2 * ns]
        sums_thru = refs[2 * n + 2 * ns:3 * n + 2 * ns]
        parts_thru = refs[3 * n + 2 * ns:4 * n + 2 * ns]
        token = refs[4 * n + 2 * ns]
        _chips_handshake()
        sends, _ = _exchange_descriptors(sums_thru, parts_thru, lambda a, k: sems[3 * a + k],
                                         lambda a, k: sems[ns + 3 * a + k])
        for cp in sends:
            cp().start()
        token[...] = jnp.zeros_like(token)

    hbm = lambda a: pltpu.HBM(a.shape, a.dtype)
    held = [pltpu.with_memory_space_constraint(a, pltpu.HBM) for a in (*sums, *parts)]
    out = _tied_call(
        body, after, name=name,
        out_shape=(*[pltpu.SemaphoreType.DMA(())] * (2 * ns), *[hbm(a) for a in held],
                   jax.ShapeDtypeStruct((8, LANES), F32)),
        in_specs=[HBM_SPEC] * (2 * n),
        out_specs=(*[SEM_SPEC] * (2 * ns), *[HBM_SPEC] * (2 * n), pl.BlockSpec(memory_space=pltpu.VMEM)),
        input_output_aliases={i: 2 * ns + i for i in range(2 * n)},
        compiler_params=pltpu.CompilerParams(has_side_effects=pltpu.SideEffectType.DATAFLOW_SIDE_EFFECTING,
                                             collective_id=CHIPS_COLLECTIVE_ID),
    )(*held)
    return (list(out[:2 * ns]), list(out[2 * ns:2 * ns + n]), list(out[2 * ns + n:2 * ns + 2 * n]),
            out[2 * ns + 2 * n])


def _exchange_wait(sems, sums, parts, after, name):
    n = len(sums)
    ns = 3 * n

    def body(*refs):
        sums_ref, parts_ref = refs[:n], refs[n:2 * n]
        sem_ref = refs[2 * n:2 * n + 2 * ns]
        sends, arrivals = _exchange_descriptors(sums_ref, parts_ref, lambda a, k: sem_ref[3 * a + k],
                                                lambda a, k: sem_ref[ns + 3 * a + k])
        for cp in sends:
            cp().wait_send()
        for cp in arrivals:
            cp().wait_recv()

    hbm = lambda a: pltpu.HBM(a.shape, a.dtype)
    out = pl.pallas_call(
        body, name=name,
        out_shape=tuple(hbm(a) for a in (*sums, *parts)),
        in_specs=[HBM_SPEC] * (2 * n) + [SEM_SPEC] * (2 * ns) + [ANY] * len(_as_tuple(after)),
        out_specs=tuple([HBM_SPEC] * (2 * n)),
        input_output_aliases={i: i for i in range(2 * n)},
        compiler_params=pltpu.CompilerParams(has_side_effects=pltpu.SideEffectType.DATAFLOW_SIDE_EFFECTING),
    )(*sums, *parts, *sems, *_as_tuple(after))
    return list(out[n:])


def _sum_chips(parts, pos, name, after=()):
    n = len(parts)
    after = _as_tuple(after)

    def body(pos_ref, *refs):
        outs = refs[n + len(after):]
        for a in range(n):
            p_ref = refs[a]
            outs[a][0] = (((p_ref[0].astype(F32) + p_ref[1].astype(F32)) + p_ref[2].astype(F32))
                          + p_ref[3].astype(F32))

    out = pl.pallas_call(
        body, name=name,
        grid_spec=pltpu.PrefetchScalarGridSpec(
            num_scalar_prefetch=1, grid=(1,),
            in_specs=[pl.BlockSpec(p.shape, lambda i, q: (0, 0, 0)) for p in parts] + [ANY] * len(after),
            out_specs=[pl.BlockSpec((1,) + p.shape[1:], lambda i, q: (q[1], 0, 0)) for p in parts]),
        out_shape=[jax.ShapeDtypeStruct((2,) + p.shape[1:], F32) for p in parts],
        compiler_params=_params(("arbitrary",)),
    )(pos, *parts, *after)
    return list(out)


def _join_descriptors(fulls, send_of, recv_of):
    x, y, c = _mesh_pos()

    def half(a, which):
        return functools.partial(
            pltpu.make_async_remote_copy,
            src_ref=fulls[a].at[which], dst_ref=fulls[a].at[which],
            send_sem=send_of(a), recv_sem=recv_of(a),
            device_id=(x, y, 1 - c), device_id_type=MESH)

    return [half(a, c) for a in range(len(fulls))], [half(a, 1 - c) for a in range(len(fulls))]


def _join_start(fulls, name, after=()):
    n = len(fulls)

    def body(*refs):
        sems, thru, token = refs[n:3 * n], refs[3 * n:4 * n], refs[4 * n]
        _sibling_handshake()
        sends, _ = _join_descriptors(thru, lambda a: sems[a], lambda a: sems[n + a])
        for cp in sends:
            cp().start()
        token[...] = jnp.zeros_like(token)

    held = [pltpu.with_memory_space_constraint(f, pltpu.HBM) for f in fulls]
    out = _tied_call(
        body, after, name=name,
        out_shape=(*[pltpu.SemaphoreType.DMA(())] * (2 * n), *[pltpu.HBM(f.shape, f.dtype) for f in held],
                   jax.ShapeDtypeStruct((8, LANES), F32)),
        in_specs=[HBM_SPEC] * n,
        out_specs=(*[SEM_SPEC] * (2 * n), *[HBM_SPEC] * n, pl.BlockSpec(memory_space=pltpu.VMEM)),
        input_output_aliases={i: 2 * n + i for i in range(n)},
        compiler_params=pltpu.CompilerParams(has_side_effects=pltpu.SideEffectType.DATAFLOW_SIDE_EFFECTING,
                                             collective_id=SIBLING_COLLECTIVE_ID),
    )(*held)
    return list(out[:2 * n]), list(out[2 * n:3 * n]), out[3 * n]


def _join_wait(sems, fulls, after, name):
    n = len(fulls)

    def body(*refs):
        sem_ref = refs[n:3 * n]
        sends, arrivals = _join_descriptors(refs[:n], lambda a: sem_ref[a], lambda a: sem_ref[n + a])
        for cp in sends:
            cp().wait_send()
        for cp in arrivals:
            cp().wait_recv()

    out = pl.pallas_call(
        body, name=name,
        out_shape=tuple(pltpu.HBM(f.shape, f.dtype) for f in fulls),
        in_specs=[HBM_SPEC] * n + [SEM_SPEC] * (2 * n) + [ANY] * len(_as_tuple(after)),
        out_specs=tuple([HBM_SPEC] * n),
        input_output_aliases={i: i for i in range(n)},
        compiler_params=pltpu.CompilerParams(has_side_effects=pltpu.SideEffectType.DATAFLOW_SIDE_EFFECTING),
    )(*fulls, *sems, *_as_tuple(after))
    return list(out)


def _join_halves(fulls, name, after=()):
    n = len(fulls)

    def body(*refs):
        send_sem, recv_sem = refs[2 * n:]
        _sibling_handshake()
        sends, arrivals = _join_descriptors(refs[n:2 * n], lambda a: send_sem.at[a], lambda a: recv_sem.at[a])
        sends = [cp() for cp in sends]
        for cp in sends:
            cp.start()
        for cp in arrivals:
            cp().wait_recv()
        for cp in sends:
            cp.wait_send()

    out_shape = [jax.ShapeDtypeStruct(f.shape, f.dtype) for f in fulls]
    return _tied_call(
        body, after, name=name,
        in_specs=[ANY] * n, out_specs=[ANY] * n, out_shape=out_shape,
        input_output_aliases={a: a for a in range(n)},
        scratch_shapes=[pltpu.SemaphoreType.DMA((n,))] * 2,
        compiler_params=pltpu.CompilerParams(collective_id=SIBLING_COLLECTIVE_ID),
    )(*fulls)


def _norm_in(x, g, ts, after=()):
    s = x.shape[0]

    def body(x_ref, g_ref, hn_ref):
        xv = x_ref[...]
        hn_ref[...] = (xv * _rms_stats(xv) * g_ref[...]).astype(BF16)

    row = pl.BlockSpec((ts, D_MODEL), lambda i: (i, 0))
    return _tied_call(
        body, after, name="norm_in", grid=(s // ts,),
        in_specs=[row, pl.BlockSpec((1, D_MODEL), lambda i: (0, 0))], out_specs=row,
        out_shape=jax.ShapeDtypeStruct((s, D_MODEL), BF16),
        compiler_params=_params(("parallel",)),
    )(x, g)


def _shift_rows(buf, shifted, t):
    rows = t + CONV_HALO - SUBLANES
    for r in range(1, SUBLANES):
        shifted[r - 1, 0:rows, :] = buf[pl.ds(r, rows), :]


def _window(buf, shifted, offset, t):
    r = offset % SUBLANES
    if r == 0:
        return buf[pl.ds(offset, t), :]
    return shifted[r - 1, pl.ds(offset - r, t), :]


def _lane_is_low_head():
    lane = lax.broadcasted_iota(jnp.int32, (1, GM_WIDTH), 1)
    return (lane & GM_HEAD_DIM) == 0


def _gm_mix(v_lo, v_hi, wpair_ref, bias_ref, mixed_ref, t):
    for n in range(t // CHUNK):
        rows = slice(n * CHUNK, (n + 1) * CHUNK)
        for j in range(GM_HEADS // 2):
            cols = slice(j * LANES, (j + 1) * LANES)
            rhs = jnp.concatenate([v_lo[rows, cols], v_hi[rows, cols]], axis=0)
            mixed_ref[rows, cols] = _dot(wpair_ref[j], rhs) + bias_ref[:, cols]


def _seqmix_fwd(hn, w_in, b_in, cw, cb, lng, lnb, gg, gb, wpair, bias, t, after=()):
    s = hn.shape[0]

    def body(hn_ref, w_ref, b_ref, cw_ref, cb_ref, lng_ref, lnb_ref, gg_ref, gb_ref, wpair_ref, bias_ref,
             z_ref, mix_ref, c1_ref, abuf, ash, mixed_ref):
        i = pl.program_id(0)

        @pl.when(i == 0)
        def _():
            abuf[0:CONV_HALO, :] = jnp.zeros((CONV_HALO, CONV_WIDTH), F32)

        @pl.when(i > 0)
        def _():
            abuf[0:CONV_HALO, :] = abuf[t:t + CONV_HALO, :]

        hv = hn_ref[...]
        for j in range(4):
            cols = slice(j * 512, (j + 1) * 512)
            z_ref[:, cols] = _dot(hv, w_ref[j]) + b_ref[:, cols]

        abuf[CONV_HALO:, :] = z_ref[:, 0:512] * _sigmoid(z_ref[:, 512:1024])
        _shift_rows(abuf, ash, t)
        acc = jnp.zeros((t, CONV_WIDTH), F32)
        for k in range(CONV_KERNEL):
            acc = acc + cw_ref[k:k + 1, :] * _window(abuf, ash, CONV_HALO - (CONV_KERNEL - 1) + k, t)
        c1 = acc + cb_ref[...]
        c1_ref[...] = c1
        xh, _ = _ln_stats(c1)
        ln = xh * lng_ref[...] + lnb_ref[...]
        mix_ref[:, 0:512] = (ln * _sigmoid(ln)).astype(BF16)

        u, _ = _gelu_parts(z_ref[:, 1024:1536])
        gv, _ = _gelu_parts(z_ref[:, 1536:2048])
        vxh, _ = _ln_stats(gv)
        v = vxh * gg_ref[...] + gb_ref[...]
        low = _lane_is_low_head()
        v_lo = jnp.where(low, v, 0.0).astype(BF16)
        v_hi = jnp.where(low, 0.0, v).astype(BF16)
        _gm_mix(v_lo, v_hi, wpair_ref, bias_ref, mixed_ref, t)
        mix_ref[:, 512:1024] = (u * mixed_ref[...]).astype(BF16)

    vec = lambda n: pl.BlockSpec((1, n), lambda i: (0, 0))
    return _tied_call(
        body, after, name="seqmix_fwd", grid=(s // t,),
        in_specs=[pl.BlockSpec((t, D_MODEL), lambda i: (i, 0)),
                  pl.BlockSpec((4, D_MODEL, 512), lambda i: (0, 0, 0)), vec(2048),
                  pl.BlockSpec((CONV_HALO, CONV_WIDTH), lambda i: (0, 0)),
                  vec(512), vec(512), vec(512), vec(512), vec(512),
                  pl.BlockSpec((4, CHUNK, 2 * CHUNK), lambda i: (0, 0, 0)),
                  pl.BlockSpec((CHUNK, GM_WIDTH), lambda i: (0, 0))],
        out_specs=[pl.BlockSpec((t, 2048), lambda i: (i, 0)),
                   pl.BlockSpec((t, D_MODEL), lambda i: (i, 0)),
                   pl.BlockSpec((t, CONV_WIDTH), lambda i: (i, 0))],
        out_shape=[jax.ShapeDtypeStruct((s, 2048), F32), jax.ShapeDtypeStruct((s, D_MODEL), BF16),
                   jax.ShapeDtypeStruct((s, CONV_WIDTH), F32)],
        scratch_shapes=[pltpu.VMEM((t + CONV_HALO, CONV_WIDTH), F32),
                        pltpu.VMEM((SUBLANES - 1, t + CONV_HALO - SUBLANES, CONV_WIDTH), F32),
                        pltpu.VMEM((t, GM_WIDTH), F32)],
        compiler_params=_params(("arbitrary",)),
    )(hn, w_in, b_in, cw, cb, lng, lnb, gg, gb, wpair, bias)


def _mem_kv(mem, g, wkv):
    m = mem.shape[0]

    def body(mem_ref, g_ref, w_ref, mn_ref, kv_ref):
        mv = mem_ref[...]
        mn = (mv * _rms_stats(mv) * g_ref[...]).astype(BF16)
        mn_ref[...] = mn
        for j in range(4):
            kv_ref[:, j * 512:(j + 1) * 512] = _dot(mn, w_ref[j]).astype(BF16)

    return pl.pallas_call(
        body, name="mem_kv",
        out_shape=[jax.ShapeDtypeStruct((m, D_MODEL), BF16), jax.ShapeDtypeStruct((m, 2 * D_MODEL), BF16)],
        compiler_params=pltpu.CompilerParams(vmem_limit_bytes=VMEM_LIMIT_BYTES),
    )(mem, g, wkv)


def _softmax_rows(sc):
    e = jnp.exp(sc - jnp.max(sc, axis=-1, keepdims=True))
    return e / jnp.sum(e, axis=-1, keepdims=True)


def _attn_block_fwd(x, mix, w_out, g_xa, wq, kv, wo, g_ffn, ts, after=()):
    s, m = x.shape[0], kv.shape[0]
    scale = XA_HEAD_DIM ** -0.5

    def body(x_ref, mix_ref, wout_ref, gxa_ref, wq_ref, kv_ref, wo_ref, gffn_ref,
             h1_ref, hn2_ref, q_ref, o_ref, h2_ref, hn3_ref):
        h1 = x_ref[...] + _dot(mix_ref[...], wout_ref[...])
        h1_ref[...] = h1
        hn2 = (h1 * _rms_stats(h1) * gxa_ref[...]).astype(BF16)
        hn2_ref[...] = hn2
        q_ref[...] = _dot(hn2, wq_ref[...]).astype(BF16)
        for h in range(XA_HEADS):
            cols = slice(h * XA_HEAD_DIM, (h + 1) * XA_HEAD_DIM)
            vcols = slice(D_MODEL + h * XA_HEAD_DIM, D_MODEL + (h + 1) * XA_HEAD_DIM)
            p = _softmax_rows(_dot_nt(q_ref[:, cols], kv_ref[:, cols]) * scale)
            o_ref[:, cols] = _dot(p.astype(BF16), kv_ref[:, vcols]).astype(BF16)
        h2 = h1 + _dot(o_ref[...], wo_ref[...])
        h2_ref[...] = h2
        hn3_ref[...] = (h2 * _rms_stats(h2) * gffn_ref[...]).astype(BF16)

    row = pl.BlockSpec((ts, D_MODEL), lambda i: (i, 0))
    full = pl.BlockSpec((D_MODEL, D_MODEL), lambda i: (0, 0))
    vec = pl.BlockSpec((1, D_MODEL), lambda i: (0, 0))
    f32 = jax.ShapeDtypeStruct((s, D_MODEL), F32)
    bf16 = jax.ShapeDtypeStruct((s, D_MODEL), BF16)
    return _tied_call(
        body, after, name="attn_block_fwd", grid=(s // ts,),
        in_specs=[row, row, full, vec, full, pl.BlockSpec((m, 2 * D_MODEL), lambda i: (0, 0)), full, vec],
        out_specs=[row] * 6,
        out_shape=[f32, bf16, bf16, bf16, f32, bf16],
        compiler_params=_params(("parallel",)),
    )(x, mix, w_out, g_xa, wq, kv, wo, g_ffn)


_FFN_CHUNKS = (slice(0, 8 * LANES), slice(8 * LANES, 16 * LANES), slice(16 * LANES, FFN_HIDDEN))


def _ffn_up(hn, wgu, ts, after=()):
    s = hn.shape[0]

    def body(hn_ref, w_ref, gu_ref, act_ref):
        hv = hn_ref[...]
        for cols in _FFN_CHUNKS:
            gate = _dot(hv, w_ref[0, :, cols])
            up = _dot(hv, w_ref[1, :, cols])
            gu_ref[0, :, cols] = gate.astype(BF16)
            gu_ref[1, :, cols] = up.astype(BF16)
            act_ref[:, cols] = (gate * _sigmoid(gate) * up).astype(BF16)

    return _tied_call(
        body, after, name="ffn_up", grid=(s // ts,),
        in_specs=[pl.BlockSpec((ts, D_MODEL), lambda i: (i, 0)),
                  pl.BlockSpec((2, D_MODEL, FFN_HIDDEN), lambda i: (0, 0, 0))],
        out_specs=[pl.BlockSpec((2, ts, FFN_HIDDEN), lambda i: (0, i, 0)),
                   pl.BlockSpec((ts, FFN_HIDDEN), lambda i: (i, 0))],
        out_shape=[jax.ShapeDtypeStruct((2, s, FFN_HIDDEN), BF16), jax.ShapeDtypeStruct((s, FFN_HIDDEN), BF16)],
        compiler_params=_params(("parallel",)),
    )(hn, wgu)


def _ffn_down_loss(act, wd, h2, g, target, ts):
    s = act.shape[0]

    def body(act_ref, wd_ref, h2_ref, g_ref, t_ref, dh_ref, sq_ref, dg_ref):
        @pl.when(pl.program_id(0) == 0)
        def _():
            sq_ref[...] = jnp.zeros_like(sq_ref)
            dg_ref[...] = jnp.zeros_like(dg_ref)

        h3 = h2_ref[...] + _dot(act_ref[...], wd_ref[...])
        r = _rms_stats(h3)
        gv = g_ref[...]
        diff = h3 * r * gv - t_ref[...]
        sq_ref[...] += _rowsum(diff * diff)
        dh, dg = _rms_bwd(diff / D_MODEL, h3, r, gv)
        dh_ref[...] = dh
        dg_ref[...] += dg

    row = pl.BlockSpec((ts, D_MODEL), lambda i: (i, 0))
    vec = pl.BlockSpec((1, D_MODEL), lambda i: (0, 0))
    return pl.pallas_call(
        body, name="ffn_down_loss", grid=(s // ts,),
        in_specs=[pl.BlockSpec((ts, FFN_HIDDEN), lambda i: (i, 0)),
                  pl.BlockSpec((FFN_HIDDEN, D_MODEL), lambda i: (0, 0)), row, vec, row],
        out_specs=[row, vec, vec],
        out_shape=[jax.ShapeDtypeStruct((s, D_MODEL), F32),
                   jax.ShapeDtypeStruct((1, D_MODEL), F32), jax.ShapeDtypeStruct((1, D_MODEL), F32)],
        compiler_params=_params(("arbitrary",)),
    )(act, wd, h2, g, target)


def _grad_w(a, b, tk, tn, name, after=(), shards=1):
    s, k = a.shape
    gb, _, n = b.shape
    nblk = n // tn
    ws = tn // shards
    tsr = GRAD_ROWS if s % GRAD_ROWS == 0 else s

    def body(a_ref, b_ref, o_ref):
        part = _dot_tn(a_ref[...], b_ref[0].astype(BF16))

        @pl.when(pl.program_id(2) == 0)
        def _():
            for j in range(shards):
                o_ref[j] = part[:, j * ws:(j + 1) * ws]

        @pl.when(pl.program_id(2) > 0)
        def _():
            for j in range(shards):
                o_ref[j] += part[:, j * ws:(j + 1) * ws]

    return _tied_call(
        body, after, name=name, grid=(gb * nblk, k // tk, s // tsr),
        in_specs=[pl.BlockSpec((tsr, tk), lambda ni, ki, si: (si, ki)),
                  pl.BlockSpec((1, tsr, tn), lambda ni, ki, si: (ni // nblk, si, ni % nblk))],
        out_specs=pl.BlockSpec((shards, tk, ws), lambda ni, ki, si: (ni, ki, 0)),
        out_shape=jax.ShapeDtypeStruct((gb * nblk * shards, k, ws), F32),
        compiler_params=_params(("parallel", "parallel", "arbitrary")),
    )(a, b)


def _grad_w_square(pairs, name, after=()):
    n = len(pairs)
    s = pairs[0][0].shape[0]
    tsr = GRAD_ROWS // 2 if s % (GRAD_ROWS // 2) == 0 else s

    def body(*refs):
        ins, outs = refs[:2 * n], refs[2 * n:]
        parts = [_dot_tn(ins[2 * a][...], ins[2 * a + 1][...]) for a in range(n)]

        @pl.when(pl.program_id(0) == 0)
        def _():
            for a in range(n):
                outs[a][...] = parts[a]

        @pl.when(pl.program_id(0) > 0)
        def _():
            for a in range(n):
                outs[a][...] += parts[a]

    row = pl.BlockSpec((tsr, D_MODEL), lambda i: (i, 0))
    return _tied_call(
        body, after, name=name, grid=(s // tsr,),
        in_specs=[row] * (2 * n), out_specs=[pl.BlockSpec((D_MODEL, D_MODEL), lambda i: (0, 0))] * n,
        out_shape=[jax.ShapeDtypeStruct((D_MODEL, D_MODEL), F32)] * n,
        compiler_params=_params(("arbitrary",)),
    )(*[x for p in pairs for x in p])


def _ffn_bwd(dh3, wd, gu, wgu, h2, g, t, after=()):
    s = dh3.shape[0]

    def body(dh3_ref, wd_ref, gu_ref, w_ref, h2_ref, g_ref, dgu_ref, dh2_ref, dh2b_ref, dg_ref):
        @pl.when(pl.program_id(0) == 0)
        def _():
            dg_ref[...] = jnp.zeros_like(dg_ref)

        dh3v = dh3_ref[...]
        dhb = dh3v.astype(BF16)
        for cols in _FFN_CHUNKS:
            dact = _dot_nt(dhb, wd_ref[cols, :])
            gate, up = gu_ref[0, :, cols].astype(F32), gu_ref[1, :, cols].astype(F32)
            sg = _sigmoid(gate)
            dgu_ref[0, :, cols] = (dact * up * (sg * (1.0 + gate * (1.0 - sg)))).astype(BF16)
            dgu_ref[1, :, cols] = (dact * (gate * sg)).astype(BF16)
        dhn = _dot_nt(dgu_ref[0], w_ref[0]) + _dot_nt(dgu_ref[1], w_ref[1])
        h2 = h2_ref[...]
        dv, dg = _rms_bwd(dhn, h2, _rms_stats(h2), g_ref[...])
        dh2 = dh3v + dv
        dh2_ref[...] = dh2
        dh2b_ref[...] = dh2.astype(BF16)
        dg_ref[...] += dg

    row = pl.BlockSpec((t, D_MODEL), lambda i: (i, 0))
    wide = pl.BlockSpec((2, t, FFN_HIDDEN), lambda i: (0, i, 0))
    vec = pl.BlockSpec((1, D_MODEL), lambda i: (0, 0))
    return _tied_call(
        body, after, name="ffn_bwd", grid=(s // t,),
        in_specs=[row, pl.BlockSpec((FFN_HIDDEN, D_MODEL), lambda i: (0, 0), pipeline_mode=pl.Buffered(1)), wide,
                  pl.BlockSpec((2, D_MODEL, FFN_HIDDEN), lambda i: (0, 0, 0), pipeline_mode=pl.Buffered(1)), row, vec],
        out_specs=[wide, row, row, vec],
        out_shape=[jax.ShapeDtypeStruct((2, s, FFN_HIDDEN), BF16), jax.ShapeDtypeStruct((s, D_MODEL), F32),
                   jax.ShapeDtypeStruct((s, D_MODEL), BF16), jax.ShapeDtypeStruct((1, D_MODEL), F32)],
        compiler_params=_params(("arbitrary",), FFN_BWD_VMEM_LIMIT_BYTES),
    )(dh3, wd, gu, wgu, h2, g)


def _attn_bwd(dh2, wo, q, kv, wq, h1, g, ts, after=()):
    s, m = q.shape[0], kv.shape[0]
    scale = XA_HEAD_DIM ** -0.5

    def body(dh2_ref, wo_ref, q_ref, kv_ref, wq_ref, h1_ref, g_ref, dh1_ref, dh1b_ref, dq_ref, dkv_ref, dg_ref):
        @pl.when(pl.program_id(0) == 0)
        def _():
            dkv_ref[...] = jnp.zeros_like(dkv_ref)
            dg_ref[...] = jnp.zeros_like(dg_ref)

        do = _dot_nt(dh2_ref[...].astype(BF16), wo_ref[...]).astype(BF16)
        for h in range(XA_HEADS):
            cols = slice(h * XA_HEAD_DIM, (h + 1) * XA_HEAD_DIM)
            vcols = slice(D_MODEL + h * XA_HEAD_DIM, D_MODEL + (h + 1) * XA_HEAD_DIM)
            qh, kh, vh, doh = q_ref[:, cols], kv_ref[:, cols], kv_ref[:, vcols], do[:, cols]
            p = _softmax_rows(_dot_nt(qh, kh) * scale)
            dp = _dot_nt(doh, vh)
            ds = (p * (dp - jnp.sum(dp * p, axis=-1, keepdims=True)) * scale).astype(BF16)
            dq_ref[:, cols] = _dot(ds, kh).astype(BF16)
            dkv_ref[:, cols] += _dot_tn(ds, qh)
            dkv_ref[:, vcols] += _dot_tn(p.astype(BF16), doh)
        dhn = _dot_nt(dq_ref[...], wq_ref[...])
        h1 = h1_ref[...]
        dv, dg = _rms_bwd(dhn, h1, _rms_stats(h1), g_ref[...])
        dh1 = dh2_ref[...] + dv
        dh1_ref[...] = dh1
        dh1b_ref[...] = dh1.astype(BF16)
        dg_ref[...] += dg

    row = pl.BlockSpec((ts, D_MODEL), lambda i: (i, 0))
    full = pl.BlockSpec((D_MODEL, D_MODEL), lambda i: (0, 0), pipeline_mode=pl.Buffered(1))
    kvs = pl.BlockSpec((m, 2 * D_MODEL), lambda i: (0, 0))
    vec = pl.BlockSpec((1, D_MODEL), lambda i: (0, 0))
    return _tied_call(
        body, after, name="attn_bwd", grid=(s // ts,),
        in_specs=[row, full, row, kvs, full, row, vec],
        out_specs=[row, row, row, kvs, vec],
        out_shape=[jax.ShapeDtypeStruct((s, D_MODEL), F32), jax.ShapeDtypeStruct((s, D_MODEL), BF16),
                   jax.ShapeDtypeStruct((s, D_MODEL), BF16),
                   jax.ShapeDtypeStruct((m, 2 * D_MODEL), F32), jax.ShapeDtypeStruct((1, D_MODEL), F32)],
        compiler_params=_params(("arbitrary",), FFN_BWD_VMEM_LIMIT_BYTES),
    )(dh2, wo, q, kv, wq, h1, g)


def _mem_kv_bwd(dkv, mn, wkv, mem, g, after=()):
    m = mem.shape[0]

    def body(dkv_ref, mn_ref, w_ref, mem_ref, g_ref, dw_ref, dg_ref):
        dmn = jnp.zeros((m, D_MODEL), F32)
        mn = mn_ref[...]
        for j in range(4):
            dj = dkv_ref[:, j * 512:(j + 1) * 512].astype(BF16)
            dw_ref[j] = _dot_tn(mn, dj)
            dmn = dmn + _dot_nt(dj, w_ref[j])
        mv = mem_ref[...]
        dg_ref[...] = _rowsum(dmn * (mv * _rms_stats(mv)))

    return _tied_call(
        body, after, name="mem_kv_bwd", in_specs=[pl.BlockSpec(memory_space=pltpu.VMEM)] * 5,
        out_shape=[jax.ShapeDtypeStruct((4, D_MODEL, 512), F32), jax.ShapeDtypeStruct((1, D_MODEL), F32)],
        compiler_params=pltpu.CompilerParams(vmem_limit_bytes=VMEM_LIMIT_BYTES),
    )(dkv, mn, wkv, mem, g)


def _seqmix_bwd(dh1, x, z, c1, w_out, w_in, g_mix, cw, lng, lnb, gg, gb, wpair, wpair_t, bias, t, after=()):
    s = x.shape[0]
    nt = s // t

    def body(dh1_ref, x_ref, z_ref, c1_ref, wo_ref, wi_ref, gm_ref, cw_ref, lng_ref, lnb_ref,
             gg_ref, gb_ref, wpair_ref, wpt_ref, bias_ref,
             gx_ref, dz_ref, dcw_ref, dcb_ref, dlng_ref, dlnb_ref, dgg_ref, dgb_ref, dws_ref, dbs_ref,
             dbin_ref, dgm_ref, dbuf, dsh, mixed_ref, dv_ref):
        i = pl.program_id(0)
        accs = (dcw_ref, dcb_ref, dlng_ref, dlnb_ref, dgg_ref, dgb_ref, dws_ref, dbs_ref, dbin_ref, dgm_ref)

        @pl.when(i == 0)
        def _():
            for r in accs:
                r[...] = jnp.zeros_like(r)
            dbuf[t:t + CONV_HALO, :] = jnp.zeros((CONV_HALO, CONV_WIDTH), F32)

        @pl.when(i > 0)
        def _():
            dbuf[t:t + CONV_HALO, :] = dbuf[0:CONV_HALO, :]

        dmix = _dot_nt(dh1_ref[...].astype(BF16), wo_ref[...])

        xh, rs = _ln_stats(c1_ref[...])
        lng = lng_ref[...]
        ln = xh * lng + lnb_ref[...]
        sl = _sigmoid(ln)
        dln = dmix[:, 0:512] * (sl * (1.0 + ln * (1.0 - sl)))
        dc1, dg_ln, db_ln = _ln_bwd(dln, xh, rs, lng)
        dlng_ref[...] += dg_ln
        dlnb_ref[...] += db_ln
        dcb_ref[...] += _rowsum(dc1)
        dbuf[0:t, :] = dc1

        za = z_ref[:, 0:512]
        sg = _sigmoid(z_ref[:, 512:1024])
        a = za * sg
        _shift_rows(dbuf, dsh, t)

        da = jnp.zeros((t, CONV_WIDTH), F32)
        for k in range(CONV_KERNEL):
            later = _window(dbuf, dsh, CONV_KERNEL - 1 - k, t)
            da = da + cw_ref[k:k + 1, :] * later
            dcw_ref[k:k + 1, :] += _rowsum(a * later)
        dza = da * sg
        dzg = da * za * (sg * (1.0 - sg))
        dz_ref[:, 0:512] = dza.astype(BF16)
        dz_ref[:, 512:1024] = dzg.astype(BF16)
        dbin_ref[:, 0:512] += _rowsum(dza)
        dbin_ref[:, 512:1024] += _rowsum(dzg)

        dgm = dmix[:, 512:1024]
        u, du_dz = _gelu_parts(z_ref[:, 1024:1536])
        gv, dgv_dz = _gelu_parts(z_ref[:, 1536:2048])
        vxh, vrs = _ln_stats(gv)
        ggv = gg_ref[...]
        v = vxh * ggv + gb_ref[...]
        low = _lane_is_low_head()
        v_lo = jnp.where(low, v, 0.0).astype(BF16)
        v_hi = jnp.where(low, 0.0, v).astype(BF16)
        _gm_mix(v_lo, v_hi, wpair_ref, bias_ref, mixed_ref, t)
        dzu = dgm * mixed_ref[...] * du_dz
        dm = dgm * u
        dm_lo = jnp.where(low, dm, 0.0).astype(BF16)
        dm_hi = jnp.where(low, 0.0, dm).astype(BF16)
        vb = v.astype(BF16)
        tril = (lax.broadcasted_iota(jnp.int32, (CHUNK, CHUNK), 1)
                <= lax.broadcasted_iota(jnp.int32, (CHUNK, CHUNK), 0))
        for n in range(t // CHUNK):
            rows = slice(n * CHUNK, (n + 1) * CHUNK)
            dbs_ref[...] += dm[rows, :]
            for j in range(GM_HEADS // 2):
                cols = slice(j * LANES, (j + 1) * LANES)
                stack = jnp.concatenate([dm_lo[rows, cols], dm_hi[rows, cols]], axis=0)
                dws = _dot_nt(stack, vb[rows, cols])
                dws_ref[2 * j] += jnp.where(tril, dws[0:CHUNK], 0.0)
                dws_ref[2 * j + 1] += jnp.where(tril, dws[CHUNK:2 * CHUNK], 0.0)
                dv_ref[rows, cols] = _dot(wpt_ref[j], stack)
        dgv, dg_gm, db_gm = _ln_bwd(dv_ref[...], vxh, vrs, ggv)
        dgg_ref[...] += dg_gm
        dgb_ref[...] += db_gm
        dzv = dgv * dgv_dz
        dz_ref[:, 1024:1536] = dzu.astype(BF16)
        dz_ref[:, 1536:2048] = dzv.astype(BF16)
        dbin_ref[:, 1024:1536] += _rowsum(dzu)
        dbin_ref[:, 1536:2048] += _rowsum(dzv)

        dhn = jnp.zeros((t, D_MODEL), F32)
        for j in range(4):
            dhn = dhn + _dot_nt(dz_ref[:, j * 512:(j + 1) * 512], wi_ref[j])
        xv = x_ref[...]
        dv, dg = _rms_bwd(dhn, xv, _rms_stats(xv), gm_ref[...])
        gx_ref[...] = dh1_ref[...] + dv
        dgm_ref[...] += dg

    rev = lambda w: pl.BlockSpec((t, w), lambda i: (nt - 1 - i, 0))
    const = lambda *shape: pl.BlockSpec(shape, lambda i: (0,) * len(shape))
    f32 = lambda *shape: jax.ShapeDtypeStruct(shape, F32)
    return _tied_call(
        body, after, name="seqmix_bwd", grid=(nt,),
        in_specs=[rev(D_MODEL), rev(D_MODEL), rev(2048), rev(CONV_WIDTH),
                  const(D_MODEL, D_MODEL), const(4, D_MODEL, 512), const(1, D_MODEL),
                  const(CONV_HALO, CONV_WIDTH), const(1, 512), const(1, 512), const(1, 512), const(1, 512),
                  const(4, CHUNK, 2 * CHUNK), const(4, CHUNK, 2 * CHUNK), const(CHUNK, GM_WIDTH)],
        out_specs=[rev(D_MODEL), rev(2048),
                   const(CONV_HALO, CONV_WIDTH), const(1, 512), const(1, 512), const(1, 512), const(1, 512),
                   const(1, 512), const(GM_HEADS, CHUNK, CHUNK), const(CHUNK, GM_WIDTH), const(1, 2048),
                   const(1, D_MODEL)],
        out_shape=[f32(s, D_MODEL), jax.ShapeDtypeStruct((s, 2048), BF16),
                   f32(CONV_HALO, CONV_WIDTH), f32(1, 512), f32(1, 512), f32(1, 512), f32(1, 512),
                   f32(1, 512), f32(GM_HEADS, CHUNK, CHUNK), f32(CHUNK, GM_WIDTH), f32(1, 2048),
                   f32(1, D_MODEL)],
        scratch_shapes=[pltpu.VMEM((t + CONV_HALO, CONV_WIDTH), F32),
                        pltpu.VMEM((SUBLANES - 1, t + CONV_HALO - SUBLANES, CONV_WIDTH), F32),
                        pltpu.VMEM((t, GM_WIDTH), F32), pltpu.VMEM((t, GM_WIDTH), F32)],
        compiler_params=_params(("arbitrary",)),
    )(dh1, x, z, c1, w_out, w_in, g_mix, cw, lng, lnb, gg, gb, wpair, wpair_t, bias)


def _head_bias_grad(dbs):
    def body(d_ref, o_ref):
        dv = d_ref[...]
        lane = lax.broadcasted_iota(jnp.int32, (CHUNK, LANES), 1)
        acc = jnp.zeros((CHUNK, LANES), F32)
        for h in range(GM_HEADS):
            sh = jnp.sum(dv[:, h * GM_HEAD_DIM:(h + 1) * GM_HEAD_DIM], axis=-1, keepdims=True)
            acc = acc + jnp.where(lane == h, sh, 0.0)
        o_ref[...] = acc

    return pl.pallas_call(body, name="head_bias_grad",
                          out_shape=jax.ShapeDtypeStruct((CHUNK, LANES), F32))(dbs)


def kernel(x, mem, norm_mix_g, w_in, b_in, conv_w, conv_b, conv_ln_g, conv_ln_b, gm_ln_g, gm_ln_b, gm_w_s, gm_b_s, w_out, norm_xa_g, mem_norm_g, xa_wq, xa_wkv, xa_wo, norm_ffn_g, ffn_w_gate_up, ffn_w_down, final_norm_g, loss_target, m_norm_mix_g, m_w_in, m_b_in, m_conv_w, m_conv_b, m_conv_ln_g, m_conv_ln_b, m_gm_ln_g, m_gm_ln_b, m_gm_w_s, m_gm_b_s, m_w_out, m_norm_xa_g, m_mem_norm_g, m_xa_wq, m_xa_wkv, m_xa_wo, m_norm_ffn_g, m_ffn_w_gate_up, m_ffn_w_down, m_final_norm_g, v_norm_mix_g, v_w_in, v_b_in, v_conv_w, v_conv_b, v_conv_ln_g, v_conv_ln_b, v_gm_ln_g, v_gm_ln_b, v_gm_w_s, v_gm_b_s, v_w_out, v_norm_xa_g, v_mem_norm_g, v_xa_wq, v_xa_wkv, v_xa_wo, v_norm_ffn_g, v_ffn_w_gate_up, v_ffn_w_down, v_final_norm_g):
    weights = dict(norm_mix_g=norm_mix_g, w_in=w_in, b_in=b_in, conv_w=conv_w, conv_b=conv_b, conv_ln_g=conv_ln_g,
                   conv_ln_b=conv_ln_b, gm_ln_g=gm_ln_g, gm_ln_b=gm_ln_b, gm_w_s=gm_w_s, gm_b_s=gm_b_s, w_out=w_out,
                   norm_xa_g=norm_xa_g, mem_norm_g=mem_norm_g, xa_wq=xa_wq, xa_wkv=xa_wkv, xa_wo=xa_wo,
                   norm_ffn_g=norm_ffn_g, ffn_w_gate_up=ffn_w_gate_up, ffn_w_down=ffn_w_down,
                   final_norm_g=final_norm_g)
    m_in = dict(norm_mix_g=m_norm_mix_g, w_in=m_w_in, b_in=m_b_in, conv_w=m_conv_w, conv_b=m_conv_b,
                conv_ln_g=m_conv_ln_g, conv_ln_b=m_conv_ln_b, gm_ln_g=m_gm_ln_g, gm_ln_b=m_gm_ln_b, gm_w_s=m_gm_w_s,
                gm_b_s=m_gm_b_s, w_out=m_w_out, norm_xa_g=m_norm_xa_g, mem_norm_g=m_mem_norm_g, xa_wq=m_xa_wq,
                xa_wkv=m_xa_wkv, xa_wo=m_xa_wo, norm_ffn_g=m_norm_ffn_g, ffn_w_gate_up=m_ffn_w_gate_up,
                ffn_w_down=m_ffn_w_down, final_norm_g=m_final_norm_g)
    v_in = dict(norm_mix_g=v_norm_mix_g, w_in=v_w_in, b_in=v_b_in, conv_w=v_conv_w, conv_b=v_conv_b,
                conv_ln_g=v_conv_ln_g, conv_ln_b=v_conv_ln_b, gm_ln_g=v_gm_ln_g, gm_ln_b=v_gm_ln_b, gm_w_s=v_gm_w_s,
                gm_b_s=v_gm_b_s, w_out=v_w_out, norm_xa_g=v_norm_xa_g, mem_norm_g=v_mem_norm_g, xa_wq=v_xa_wq,
                xa_wkv=v_xa_wkv, xa_wo=v_xa_wo, norm_ffn_g=v_norm_ffn_g, ffn_w_gate_up=v_ffn_w_gate_up,
                ffn_w_down=v_ffn_w_down, final_norm_g=v_final_norm_g)
    grads, delta, new_m, new_v = {}, {}, {}, {}

    s = x.shape[1]
    ts = _row_tile(s)
    tb = max(CHUNK, ts // 2)
    tw = 2 * ts if s % (2 * ts) == 0 and ts >= 512 else ts
    cx, cy, cc = _mesh_pos()
    chip = 2 * cx + cy
    pos = jnp.stack([chip, cc]).astype(jnp.int32)
    row = lambda a: a.reshape(1, -1)
    x2, mem2, tgt2 = x[0], mem[0], loss_target[0]

    big = dict(w_in=w_in, xa_wkv=xa_wkv, w_out=w_out, xa_wq=xa_wq, xa_wo=xa_wo,
               ffn_w_gate_up=ffn_w_gate_up, ffn_w_down=ffn_w_down)
    big_names = list(big)
    halves = lambda a: a.reshape(2, a.shape[0] // 2, a.shape[1])
    conv_w_pad = jnp.pad(conv_w, ((0, CONV_HALO - CONV_KERNEL), (0, 0)))
    first_names = ["w_in", "conv_w"]
    later_names = [nm for nm in big_names if nm != "w_in"]
    cast = dict(zip(first_names, _cast_into_slots([halves(w_in), halves(conv_w_pad)], pos, [BF16, F32], "cast_w_in")))
    cast.update(zip(later_names, _cast_into_slots([halves(big[nm]) for nm in later_names], pos,
                                                  [BF16] * len(later_names), "cast_" + later_names[0],
                                                  side_by_side=(later_names.index("ffn_w_gate_up"),))))

    def start_gather(names, after):
        return _gather_start([cast[nm] for nm in names], "gather_start_" + names[0], after)

    def land_gather(names, started, after):
        send_sems, recv_sems, bufs, _ = started
        return _gather_wait(send_sems, recv_sems, bufs, after, "gather_wait_" + names[0])

    as_weights = lambda names, bufs: dict(zip(names, (b.reshape(b.shape[0], -1, b.shape[-1]) for b in bufs)))

    def start_share(names, landed):
        return _pass_start(landed, "pass_start_" + names[0])

    def share_gather(names, passing, which, after):
        send_sems, recv_sems, bufs, _ = passing
        sems = lambda s: [s[3 * a + k] for a in which for k in range(3)]
        picked = [names[a] for a in which]
        return as_weights(picked, _gather_wait(sems(send_sems), sems(recv_sems), [bufs[a] for a in which], after,
                                               "pass_wait_" + picked[0], _pass_descriptors))

    tril = jnp.tril(jnp.ones((CHUNK, CHUNK), dtype=bool))
    ws = jnp.where(tril[None], gm_w_s, 0.0)
    wpair = jnp.concatenate([ws[0::2], ws[1::2]], axis=2).astype(BF16)
    ws_t = jnp.swapaxes(ws, 1, 2)
    wpair_t = jnp.concatenate([ws_t[0::2], ws_t[1::2]], axis=2).astype(BF16)
    bias = jnp.repeat(gm_b_s.T, GM_HEAD_DIM, axis=1)

    attn_names = ["xa_wkv", "w_out", "xa_wq", "xa_wo"]
    gather_first = start_gather(first_names, ())
    hn1 = _norm_in(x2, row(norm_mix_g), tw, after=(gather_first[3], wpair, wpair_t, bias))
    landed = land_gather(first_names, gather_first, [cast[nm] for nm in later_names] + [hn1])
    passing = start_share(first_names, landed)
    gather_attn = start_gather(attn_names, passing[3])
    gw = share_gather(first_names, passing, (0, 1), gather_attn[3])
    w_in_g = gw["w_in"]
    cw_g = jnp.concatenate([gw["conv_w"][k] for k in range(N_CHIPS)], axis=1)

    z, mix, c1 = _seqmix_fwd(hn1, w_in_g, row(b_in), cw_g, row(conv_b), row(conv_ln_g), row(conv_ln_b),
                             row(gm_ln_g), row(gm_ln_b), wpair, bias, ts)
    landed = land_gather(attn_names, gather_attn, mix)
    passing = start_share(attn_names, landed)
    gather_gu = start_gather(["ffn_w_gate_up"], passing[3])
    wkv_g = share_gather(attn_names, passing, (0,), gather_gu[3])["xa_wkv"]
    mn, kv = _mem_kv(mem2, row(mem_norm_g), wkv_g)
    gw = share_gather(attn_names, passing, (1, 2, 3), kv)
    w_out_g = gw["w_out"].reshape(D_MODEL, D_MODEL)
    wq_g = gw["xa_wq"].reshape(D_MODEL, D_MODEL)
    wo_g = gw["xa_wo"].reshape(D_MODEL, D_MODEL)
    h1, hn2, q, o, h2, hn3 = _attn_block_fwd(x2, mix, w_out_g, row(norm_xa_g), wq_g, kv, wo_g, row(norm_ffn_g), ts)
    landed = land_gather(["ffn_w_gate_up"], gather_gu, hn3)
    passing = start_share(["ffn_w_gate_up"], landed)
    gather_down = start_gather(["ffn_w_down"], passing[3])
    wgu_g = share_gather(["ffn_w_gate_up"], passing, (0,), gather_down[3])["ffn_w_gate_up"]
    gu, act = _ffn_up(hn3, wgu_g, ts)
    landed = land_gather(["ffn_w_down"], gather_down, act)
    wd_g = as_weights(["ffn_w_down"], _pass_to_sibling(landed, "pass_ffn_w_down"))["ffn_w_down"].reshape(
        FFN_HIDDEN, D_MODEL)
    dh3, sq, d_final_g = _ffn_down_loss(act, wd_g, h2, row(final_norm_g), tgt2, ts)
    loss_here = jnp.broadcast_to(0.5 * jnp.sum(sq) / D_MODEL, (1, 2, SUBLANES, LANES))

    def split(g, nm):
        r, c = big[nm].shape
        return g.reshape(N_CHIPS, 2, r // 2, c)

    def chip_sums(group, arrays, got):
        sums, parts = [None] * len(group), [None] * len(group)
        for blocks in (N_CHIPS, 1):
            idx = [i for i, a in enumerate(arrays) if a.shape[0] == blocks]
            if idx:
                out = _add_halves([arrays[i] for i in idx], [got[i] for i in idx], pos, "chip_sum_" + group[idx[0]],
                                  [F32 if group[i] == "loss" else BF16 for i in idx])
                for k, i in enumerate(idx):
                    sums[i], parts[i] = out[0][k], out[1][k]
        return sums, parts

    def start_swap(group, grads, after=()):
        return _swap_start([split(g, nm) for g, nm in zip(grads, group)], "swap_start_" + group[0], after)

    def start_exchange(group, swapping, after, landed):
        arrays, got = _swap_wait(*swapping[:3], after, "swap_wait_" + group[0])
        sums, parts = chip_sums(group, arrays, got)
        return _exchange_start(sums, parts, "exchange_start_" + group[0], landed)

    def wait_exchange(group, started, after):
        sems, sums, parts, _ = started
        return _exchange_wait(sems, sums, parts, after, "exchange_wait_" + group[0])

    def finish_exchange(group, started, after):
        return _sum_chips(wait_exchange(group, started, after), pos, "total_" + group[0])

    def join(group, after):
        return _join_halves([halves_of[nm] for nm in group], "join_halves_" + group[0], after)

    def start_join(group, after):
        return _join_start([halves_of[nm] for nm in group], "join_start_" + group[0], after)

    def end_join(group, joining, after):
        return _join_wait(*joining[:2], after, "join_wait_" + group[0])

    def update(group, joined, after=()):
        outs = _adamw([(weights[nm], j.reshape(big[nm].shape), m_in[nm], v_in[nm]) for nm, j in zip(group, joined)],
                      "adamw_" + group[0], after)
        for nm, out in zip(group, outs):
            grads[nm], delta[nm], new_m[nm], new_v[nm] = out
        return [new_v[nm] for nm in group]

    def join_and_update(group, after):
        return update(group, join(group, after))

    as3 = lambda a: a.reshape((1,) + a.shape)
    halves_of = {}

    g_down = _grad_w(act, as3(dh3), FFN_HALF, D_MODEL, "grad_ffn_w_down")
    group_a = ["ffn_w_down"]
    swap_a = start_swap(group_a, [g_down])
    dgu, dh2, dh2_b, d_ffn_g = _ffn_bwd(dh3, wd_g, gu, wgu_g, h2, row(norm_ffn_g), ts,
                                        after=swap_a[3])
    exch_a = start_exchange(group_a, swap_a, dh2, wd_g)
    g_gu = _grad_w(hn3, dgu, D_MODEL, FFN_HALF, "grad_ffn_w_gate_up", after=exch_a[3])
    halves_of.update(zip(group_a, finish_exchange(group_a, exch_a, g_gu)))

    group_b = ["ffn_w_gate_up"]
    swap_b = start_swap(group_b, [g_gu])
    dh1, dh1_b, dq, dkv, d_xa_g = _attn_bwd(dh2, wo_g, q, kv, wq_g, h1, row(norm_xa_g), tw, after=swap_b[3])
    exch_b = start_exchange(group_b, swap_b, dh1, [halves_of[nm] for nm in group_a])
    joining_a = start_join(group_a, exch_b[3])
    g_wkv, d_mem_g = _mem_kv_bwd(dkv, mn, wkv_g, mem2, row(mem_norm_g), after=joining_a[2])
    g_wo, g_wq, g_wout = _grad_w_square([(o, dh2_b), (hn2, dq), (mix, dh1_b)], "grad_xa_wo", after=joining_a[2])
    done_a = update(group_a, end_join(group_a, joining_a, (g_wkv, g_wo, g_wq, g_wout)))
    halves_of.update(zip(group_b, finish_exchange(group_b, exch_b, done_a)))
    joining_b = start_join(group_b, done_a)
    group_c = ["xa_wo", "xa_wq", "xa_wkv", "w_out"]
    swap_c = start_swap(group_c, [g_wo, g_wq, g_wkv, g_wout], joining_b[2])
    (gx, dz, d_cw, d_cb, d_lng, d_lnb, d_gg, d_gb, d_ws, d_bs_sum, d_bin, d_mix_g) = _seqmix_bwd(
        dh1, x2, z, c1, w_out_g, w_in_g, row(norm_mix_g), cw_g, row(conv_ln_g), row(conv_ln_b),
        row(gm_ln_g), row(gm_ln_b), wpair, wpair_t, bias, tb, after=swap_c[3])
    d_bs = _head_bias_grad(d_bs_sum)[:, :GM_HEADS].T
    exch_c = start_exchange(group_c, swap_c, dz, joining_b[2])
    g_win = _grad_w(hn1, as3(dz), D_MODEL, 1024, "grad_w_in", after=exch_c[3], shards=2)

    small_names = ["norm_mix_g", "b_in", "conv_w", "conv_b", "conv_ln_g", "conv_ln_b", "gm_ln_g", "gm_ln_b",
                   "gm_w_s", "gm_b_s", "norm_xa_g", "mem_norm_g", "norm_ffn_g", "final_norm_g"]
    d_cw_by_chip = jnp.swapaxes(d_cw.reshape(CONV_HALO, N_CHIPS, LANES), 0, 1).reshape(-1, LANES)
    small_grads = dict(norm_mix_g=d_mix_g, b_in=d_bin, conv_w=d_cw_by_chip, conv_b=d_cb, conv_ln_g=d_lng,
                       conv_ln_b=d_lnb, gm_ln_g=d_gg, gm_ln_b=d_gb, gm_w_s=d_ws, gm_b_s=d_bs, norm_xa_g=d_xa_g,
                       mem_norm_g=d_mem_g, norm_ffn_g=d_ffn_g, final_norm_g=d_final_g)

    def rows_form(a):
        a = a.reshape(-1, LANES)
        return jnp.pad(a, ((0, -a.shape[0] % SUBLANES), (0, 0)))

    pieces = [rows_form(small_grads[nm]) for nm in small_names]
    offsets, total = [], 0
    for p in pieces:
        offsets.append(total)
        total += p.shape[0]
    pack_rows = -(-total // 32) * 32
    small_pack = jnp.pad(jnp.concatenate(pieces, axis=0), ((0, pack_rows - total), (0, 0)))

    group_d = ["w_in", "small", "loss"]
    joined_b = end_join(group_b, joining_b, g_win)
    swap_d = _swap_start([split(g_win, "w_in")], "swap_start_w_in", joined_b)
    done_b = update(group_b, joined_b, swap_d[3])
    small_d = [small_pack.reshape(1, 2, pack_rows // 2, LANES), loss_here]
    got_small = _swap_halves(small_d, "swap_halves_small", done_b)
    arrays_d, got_d = _swap_wait(*swap_d[:3], got_small, "swap_wait_w_in")
    sums_d, parts_d = chip_sums(group_d, arrays_d + small_d, got_d + list(got_small))
    parts_c = wait_exchange(group_c, exch_c, sums_d)
    exch_d = _exchange_start(sums_d, parts_d, "exchange_start_w_in", parts_c)
    halves_of.update(zip(group_c, _sum_chips(parts_c, pos, "total_xa_wo", exch_d[3])))
    done_c = join_and_update(group_c, exch_d[3])
    halves_of.update(zip(group_d, finish_exchange(group_d, exch_d, done_c)))
    joined_d = _join_halves([halves_of[nm] for nm in group_d], "join_halves_w_in")
    loss = joined_d[2][0, 0, 0]
    grads["w_in"], delta["w_in"], new_m["w_in"], new_v["w_in"] = _adamw(
        [(w_in, joined_d[0].reshape(w_in.shape), m_w_in, v_w_in)], "adamw_w_in")[0]

    local_rows = lambda a, nm: a if nm == "conv_w" else a.reshape(-1, LANES)
    params = [tuple(local_rows(src[nm], nm) for src in (weights, m_in, v_in)) for nm in small_names]
    outs = _adamw_small(joined_d[1].reshape(pack_rows, LANES), pos, params, offsets, small_names.index("conv_w"))
    for k, nm in enumerate(small_names):
        for dst, a in zip((grads, delta, new_m, new_v), outs[4 * k:4 * k + 4]):
            dst[nm] = a

    order = ["norm_mix_g", "w_in", "b_in", "conv_w", "conv_b", "conv_ln_g", "conv_ln_b", "gm_ln_g", "gm_ln_b",
             "gm_w_s", "gm_b_s", "w_out", "norm_xa_g", "mem_norm_g", "xa_wq", "xa_wkv", "xa_wo", "norm_ffn_g",
             "ffn_w_gate_up", "ffn_w_down", "final_norm_g"]
    fit = lambda a, nm: a.reshape(weights[nm].shape)
    return (loss, gx.reshape(x.shape),
            *[fit(grads[nm], nm) for nm in order], *[fit(delta[nm], nm) for nm in order],
            *[fit(new_m[nm], nm) for nm in order], *[fit(new_v[nm], nm) for nm in order])
```
